```python
import jax, jax.numpy as jnp
from jax import lax
import numpy as np

D_MODEL = 1024
BATCH = 8
SEQ = 4096
DEPTH = 4

N_MIXERS = 3
N_META = 16
NORM_EPS = 1e-6

FOX_HEADS = 16
FOX_HEAD_DIM = D_MODEL // FOX_HEADS
FOX_Q_BLOCK = 128
FOX_IN = 4 * D_MODEL + FOX_HEADS

GLA_HEADS = 4
GLA_DK = D_MODEL // 2 // GLA_HEADS
GLA_DV = D_MODEL // GLA_HEADS
GLA_GATE_RANK = 16
GLA_GATE_NORMALIZER = 16.0
GLA_CHUNK = 64
GLA_QK = GLA_HEADS * GLA_DK
GLA_V = GLA_HEADS * GLA_DV
GLA_IN = 2 * GLA_QK + 2 * GLA_V + GLA_GATE_RANK

GDN_HEADS = 8
GDN_DK = 128
GDN_DV = 128
GDN_CONV = 4
GDN_CHUNK = 64
GDN_CONV_DIM = 2 * GDN_HEADS * GDN_DK + GDN_HEADS * GDN_DV
GDN_IN = GDN_CONV_DIM + GDN_HEADS * GDN_DV + 2 * GDN_HEADS

D_FF = ((-(-8 * D_MODEL // 3) + 255) // 256) * 256

N_FOX = (DEPTH + 2) // 3
N_GLA = (DEPTH + 1) // 3
N_GDN = DEPTH // 3

kernel_name = "fox_gla_gdn_interleaved_hybrid"


def rmsnorm(x, g):
    xf = x.astype(jnp.float32)
    y = xf * lax.rsqrt(jnp.mean(xf * xf, axis=-1, keepdims=True) + NORM_EPS)
    return (y * g.astype(jnp.float32)).astype(x.dtype)


def l2norm(x):
    xf = x.astype(jnp.float32)
    return (xf * lax.rsqrt(jnp.sum(xf * xf, axis=-1, keepdims=True) + NORM_EPS)).astype(x.dtype)


def to_chunks(a, chunk):
    b, t = a.shape[0], a.shape[1]
    return jnp.moveaxis(a.reshape((b, t // chunk, chunk) + a.shape[2:]), 1, 0)


def from_chunks(a):
    a = jnp.moveaxis(a, 0, 1)
    return a.reshape((a.shape[0], a.shape[1] * a.shape[2]) + a.shape[3:])


def chunked_scan(chunk_fn, state0, seqs, chunk):
    state, o_meta = chunk_fn(state0, tuple(a[:, :N_META] for a in seqs))
    real = tuple(to_chunks(a[:, N_META:], chunk) for a in seqs)
    _, o_real = lax.scan(chunk_fn, state, real)
    return jnp.concatenate([o_meta, from_chunks(o_real)], axis=1)


def fox_attend(q, cq, qpos, k, v, ck, kpos):
    s = jnp.einsum('bqhd,bkhd->bhqk', q, k).astype(jnp.float32)
    bias = cq[..., :, None] - ck[..., None, :]
    mask = kpos[None, :] <= qpos[:, None]
    s = jnp.where(mask, s + bias, -jnp.inf)
    p = jax.nn.softmax(s, axis=-1)
    return jnp.einsum('bhqk,bkhd->bqhd', p.astype(v.dtype), v)


def fox_mixer(h, w_in, b_f, q_gain, k_gain, w_out):
    B, L, _ = h.shape
    H, Dh = FOX_HEADS, FOX_HEAD_DIM
    proj = h @ w_in
    q, k, v, gate, f = jnp.split(proj, [D_MODEL, 2 * D_MODEL, 3 * D_MODEL, 4 * D_MODEL], axis=-1)
    q = rmsnorm(q.reshape(B, L, H, Dh), q_gain) * (Dh ** -0.5)
    k = rmsnorm(k.reshape(B, L, H, Dh), k_gain)
    v = v.reshape(B, L, H, Dh)
    log_f = jax.nn.log_sigmoid((f + b_f).astype(jnp.float32))
    c = jnp.cumsum(log_f, axis=1).transpose(0, 2, 1)
    kpos = jnp.arange(L)

    o_meta = fox_attend(q[:, :N_META], c[:, :, :N_META], kpos[:N_META],
                        k[:, :N_META], v[:, :N_META], c[:, :, :N_META], kpos[:N_META])

    n_blk = (L - N_META) // FOX_Q_BLOCK
    q_blocks = q[:, N_META:].reshape(B, n_blk, FOX_Q_BLOCK, H, Dh).swapaxes(0, 1)
    cq_blocks = c[:, :, N_META:].reshape(B, H, n_blk, FOX_Q_BLOCK).transpose(2, 0, 1, 3)
    qpos_blocks = (N_META + jnp.arange(n_blk * FOX_Q_BLOCK)).reshape(n_blk, FOX_Q_BLOCK)

    def block(args):
        qb, cqb, qp = args
        return fox_attend(qb, cqb, qp, k, v, c, kpos)

    o_real = from_chunks(lax.map(block, (q_blocks, cq_blocks, qpos_blocks)))
    o = jnp.concatenate([o_meta, o_real], axis=1).reshape(B, L, H * Dh)
    o = o * jax.nn.sigmoid(gate)
    return (o @ w_out).astype(h.dtype)


def gla_chunk(state, inputs):
    q, k, v, g = inputs
    C = q.shape[1]
    b = jnp.cumsum(g, axis=1)
    causal = jnp.tril(jnp.ones((C, C), dtype=bool))
    diff = b[:, :, None] - b[:, None, :]
    decay = jnp.exp(jnp.where(causal[None, :, :, None, None], diff, -jnp.inf))
    A = jnp.einsum('bthd,bshd,btshd->bhts', q, k, decay)
    o_intra = jnp.einsum('bhts,bshv->bthv', A, v)
    o_inter = jnp.einsum('bthd,bhdv->bthv', q * jnp.exp(b), state)
    b_last = b[:, -1]
    k_dec = k * jnp.exp(b_last[:, None] - b)
    new_state = state * jnp.exp(b_last)[..., None] + jnp.einsum('bshd,bshv->bhdv', k_dec, v)
    return new_state, (o_intra + o_inter).astype(v.dtype)


def gla_mixer(h, w_in, w_alpha2, b_alpha, o_gain, w_out):
    B, L, _ = h.shape
    H = GLA_HEADS
    proj = h @ w_in
    q, k, v, r, a_lr = jnp.split(proj, [GLA_QK, 2 * GLA_QK, 2 * GLA_QK + GLA_V, 2 * GLA_QK + 2 * GLA_V], axis=-1)
    q = q.reshape(B, L, H, GLA_DK) * (GLA_DK ** -0.5)
    k = k.reshape(B, L, H, GLA_DK)
    v = v.reshape(B, L, H, GLA_DV)
    g = jax.nn.log_sigmoid((a_lr @ w_alpha2 + b_alpha).astype(jnp.float32)) / GLA_GATE_NORMALIZER
    g = g.reshape(B, L, H, GLA_DK)
    state0 = jnp.zeros((B, H, GLA_DK, GLA_DV), jnp.float32)
    o = chunked_scan(gla_chunk, state0, (q, k, v, g), GLA_CHUNK)
    o = rmsnorm(o, o_gain) * jax.nn.silu(r.reshape(B, L, H, GLA_DV))
    return (o.reshape(B, L, GLA_V) @ w_out).astype(h.dtype)


def causal_depthwise_conv(x, w):
    return lax.conv_general_dilated(
        x, w.astype(x.dtype), window_strides=(1,), padding=[(GDN_CONV - 1, 0)],
        dimension_numbers=('NWC', 'WIO', 'NWC'), feature_group_count=x.shape[-1])


def gdn_chunk(state, inputs):
    q, k, v, g, beta = inputs
    C = q.shape[1]
    qh = q.transpose(0, 2, 1, 3)
    kh = k.transpose(0, 2, 1, 3)
    vh = v.transpose(0, 2, 1, 3)
    bt = beta.transpose(0, 2, 1).astype(jnp.float32)[..., None]
    b = jnp.cumsum(g, axis=1).transpose(0, 2, 1)
    diff = b[..., :, None] - b[..., None, :]
    incl = jnp.tril(jnp.ones((C, C), dtype=bool))
    strict = jnp.tril(jnp.ones((C, C), dtype=bool), k=-1)
    kb = kh.astype(jnp.float32) * bt
    vb = vh.astype(jnp.float32) * bt
    lower = jnp.einsum('bhtd,bhsd->bhts', kb, kh) * jnp.exp(jnp.where(strict, diff, -jnp.inf))
    t_mat = jnp.eye(C, dtype=jnp.float32) + lower
    rhs = jnp.concatenate([vb, kb * jnp.exp(b)[..., None]], axis=-1)
    sol = lax.linalg.triangular_solve(t_mat, rhs, left_side=True, lower=True, unit_diagonal=True)
    u, w = sol[..., :GDN_DV], sol[..., GDN_DV:]
    v_new = u - jnp.einsum('bhcd,bhdv->bhcv', w, state)
    attn = jnp.einsum('bhtd,bhsd->bhts', qh, kh) * jnp.exp(jnp.where(incl, diff, -jnp.inf))
    o = jnp.einsum('bhts,bhsv->bhtv', attn, v_new) + \
        jnp.einsum('bhtd,bhdv->bhtv', qh * jnp.exp(b)[..., None], state)
    b_last = b[..., -1]
    k_dec = kh * jnp.exp(b_last[..., None] - b)[..., None]
    new_state = state * jnp.exp(b_last)[..., None, None] + jnp.einsum('bhsd,bhsv->bhdv', k_dec, v_new)
    return new_state, o.transpose(0, 2, 1, 3).astype(v.dtype)


def gdn_mixer(h, w_in, conv_w, a_log, dt_bias, o_gain, w_out):
    B, L, _ = h.shape
    H = GDN_HEADS
    proj = h @ w_in
    qkv, gate, a, beta_logit = jnp.split(
        proj, [GDN_CONV_DIM, GDN_CONV_DIM + H * GDN_DV, GDN_CONV_DIM + H * GDN_DV + H], axis=-1)
    qkv = jax.nn.silu(causal_depthwise_conv(qkv, conv_w))
    q, k, v = jnp.split(qkv, [H * GDN_DK, 2 * H * GDN_DK], axis=-1)
    q = l2norm(q.reshape(B, L, H, GDN_DK)) * (GDN_DK ** -0.5)
    k = l2norm(k.reshape(B, L, H, GDN_DK))
    v = v.reshape(B, L, H, GDN_DV)
    beta = jax.nn.sigmoid(beta_logit)
    g = -jnp.exp(a_log.astype(jnp.float32)) * jax.nn.softplus((a + dt_bias).astype(jnp.float32))
    state0 = jnp.zeros((B, H, GDN_DK, GDN_DV), jnp.float32)
    o = chunked_scan(gdn_chunk, state0, (q, k, v, g, beta), GDN_CHUNK)
    o = rmsnorm(o, o_gain) * jax.nn.silu(gate.reshape(B, L, H, GDN_DV))
    return (o.reshape(B, L, H * GDN_DV) @ w_out).astype(h.dtype)


def swiglu(h, w_gate_up, w_down):
    gu = h @ w_gate_up
    gt, up = gu[..., :D_FF], gu[..., D_FF:]
    return ((jax.nn.silu(gt) * up) @ w_down).astype(h.dtype)


def _fwd_setup_inputs(seed: int = 0) -> dict:
    key = jax.random.key(seed)
    ks = jax.random.split(key, 24)

    def nrm(k, shape, fan_in):
        return jax.random.normal(k, shape, jnp.float32) * (fan_in ** -0.5)

    def gain(k, shape):
        return 1.0 + 0.02 * jax.random.normal(k, shape, jnp.float32)

    dt = jnp.exp(jax.random.uniform(ks[20], (N_GDN, GDN_HEADS), jnp.float32, np.log(1e-3), np.log(1e-1)))
    return {
        "x": jax.random.normal(ks[0], (BATCH, SEQ, D_MODEL), jnp.float32),
        "meta_tokens": jax.random.normal(ks[1], (N_META, D_MODEL), jnp.float32),
        "norm_mix": gain(ks[2], (DEPTH, D_MODEL)),
        "norm_ffn": gain(ks[3], (DEPTH, D_MODEL)),
        "w_gate_up": nrm(ks[4], (DEPTH, D_MODEL, 2 * D_FF), D_MODEL),
        "w_down": nrm(ks[5], (DEPTH, D_FF, D_MODEL), D_FF),
        "fox_w_in": nrm(ks[6], (N_FOX, D_MODEL, FOX_IN), D_MODEL),
        "fox_b_f": 2.0 + 0.5 * jax.random.normal(ks[7], (N_FOX, FOX_HEADS), jnp.float32),
        "fox_q_gain": gain(ks[8], (N_FOX, FOX_HEAD_DIM)),
        "fox_k_gain": gain(ks[9], (N_FOX, FOX_HEAD_DIM)),
        "fox_w_out": nrm(ks[10], (N_FOX, D_MODEL, D_MODEL), D_MODEL),
        "gla_w_in": nrm(ks[11], (N_GLA, D_MODEL, GLA_IN), D_MODEL),
        "gla_w_alpha2": nrm(ks[12], (N_GLA, GLA_GATE_RANK, GLA_QK), GLA_GATE_RANK),
        "gla_b_alpha": 0.1 * jax.random.normal(ks[13], (N_GLA, GLA_QK), jnp.float32),
        "gla_o_gain": gain(ks[14], (N_GLA, GLA_DV)),
        "gla_w_out": nrm(ks[15], (N_GLA, GLA_V, D_MODEL), GLA_V),
        "gdn_w_in": nrm(ks[16], (N_GDN, D_MODEL, GDN_IN), D_MODEL),
        "gdn_conv_w": nrm(ks[17], (N_GDN, GDN_CONV, 1, GDN_CONV_DIM), GDN_CONV),
        "gdn_a_log": jnp.log(jax.random.uniform(ks[18], (N_GDN, GDN_HEADS), jnp.float32, 1.0, 16.0)),
        "gdn_dt_bias": dt + jnp.log(-jnp.expm1(-dt)),
        "gdn_o_gain": gain(ks[19], (N_GDN, GDN_DV)),
        "gdn_w_out": nrm(ks[21], (N_GDN, GDN_HEADS * GDN_DV, D_MODEL), GDN_HEADS * GDN_DV),
    }


def _fwd_reference(x, meta_tokens, norm_mix, norm_ffn, w_gate_up, w_down,
              fox_w_in, fox_b_f, fox_q_gain, fox_k_gain, fox_w_out,
              gla_w_in, gla_w_alpha2, gla_b_alpha, gla_o_gain, gla_w_out,
              gdn_w_in, gdn_conv_w, gdn_a_log, gdn_dt_bias, gdn_o_gain, gdn_w_out):
    B = x.shape[0]
    meta = jnp.broadcast_to(meta_tokens[None].astype(x.dtype), (B, N_META, D_MODEL))
    h = jnp.concatenate([meta, x], axis=1)
    for i in range(DEPTH):
        kind, j = i % N_MIXERS, i // N_MIXERS
        y = rmsnorm(h, norm_mix[i])
        if kind == 0:
            mix = fox_mixer(y, fox_w_in[j], fox_b_f[j], fox_q_gain[j], fox_k_gain[j], fox_w_out[j])
        elif kind == 1:
            mix = gla_mixer(y, gla_w_in[j], gla_w_alpha2[j], gla_b_alpha[j], gla_o_gain[j], gla_w_out[j])
        else:
            mix = gdn_mixer(y, gdn_w_in[j], gdn_conv_w[j], gdn_a_log[j], gdn_dt_bias[j], gdn_o_gain[j], gdn_w_out[j])
        h = h + mix
        h = h + swiglu(rmsnorm(h, norm_ffn[i]), w_gate_up[i], w_down[i])
    return h[:, N_META:]


import jax as _jax
import jax.numpy as _jnp

TWIN_FORMAT = 'train_step'
FWD_PARAMS = ['x', 'meta_tokens', 'norm_mix', 'norm_ffn', 'w_gate_up', 'w_down', 'fox_w_in', 'fox_b_f', 'fox_q_gain', 'fox_k_gain', 'fox_w_out', 'gla_w_in', 'gla_w_alpha2', 'gla_b_alpha', 'gla_o_gain', 'gla_w_out', 'gdn_w_in', 'gdn_conv_w', 'gdn_a_log', 'gdn_dt_bias', 'gdn_o_gain', 'gdn_w_out']
TWIN_WEIGHTS = ['meta_tokens', 'norm_mix', 'norm_ffn', 'w_gate_up', 'w_down', 'fox_w_in', 'fox_b_f', 'fox_q_gain', 'fox_k_gain', 'fox_w_out', 'gla_w_in', 'gla_w_alpha2', 'gla_b_alpha', 'gla_o_gain', 'gla_w_out', 'gdn_w_in', 'gdn_conv_w', 'gdn_a_log', 'gdn_dt_bias', 'gdn_o_gain', 'gdn_w_out']
TWIN_DIFF_INPUT = 'x'
TWIN_INPUTS = ['x', 'meta_tokens', 'norm_mix', 'norm_ffn', 'w_gate_up', 'w_down', 'fox_w_in', 'fox_b_f', 'fox_q_gain', 'fox_k_gain', 'fox_w_out', 'gla_w_in', 'gla_w_alpha2', 'gla_b_alpha', 'gla_o_gain', 'gla_w_out', 'gdn_w_in', 'gdn_conv_w', 'gdn_a_log', 'gdn_dt_bias', 'gdn_o_gain', 'gdn_w_out', 'loss_target', 'm_meta_tokens', 'm_norm_mix', 'm_norm_ffn', 'm_w_gate_up', 'm_w_down', 'm_fox_w_in', 'm_fox_b_f', 'm_fox_q_gain', 'm_fox_k_gain', 'm_fox_w_out', 'm_gla_w_in', 'm_gla_w_alpha2', 'm_gla_b_alpha', 'm_gla_o_gain', 'm_gla_w_out', 'm_gdn_w_in', 'm_gdn_conv_w', 'm_gdn_a_log', 'm_gdn_dt_bias', 'm_gdn_o_gain', 'm_gdn_w_out', 'v_meta_tokens', 'v_norm_mix', 'v_norm_ffn', 'v_w_gate_up', 'v_w_down', 'v_fox_w_in', 'v_fox_b_f', 'v_fox_q_gain', 'v_fox_k_gain', 'v_fox_w_out', 'v_gla_w_in', 'v_gla_w_alpha2', 'v_gla_b_alpha', 'v_gla_o_gain', 'v_gla_w_out', 'v_gdn_w_in', 'v_gdn_conv_w', 'v_gdn_a_log', 'v_gdn_dt_bias', 'v_gdn_o_gain', 'v_gdn_w_out']
TWIN_OUTPUTS = ['loss', 'grad_x', 'grad_meta_tokens', 'grad_norm_mix', 'grad_norm_ffn', 'grad_w_gate_up', 'grad_w_down', 'grad_fox_w_in', 'grad_fox_b_f', 'grad_fox_q_gain', 'grad_fox_k_gain', 'grad_fox_w_out', 'grad_gla_w_in', 'grad_gla_w_alpha2', 'grad_gla_b_alpha', 'grad_gla_o_gain', 'grad_gla_w_out', 'grad_gdn_w_in', 'grad_gdn_conv_w', 'grad_gdn_a_log', 'grad_gdn_dt_bias', 'grad_gdn_o_gain', 'grad_gdn_w_out', 'delta_meta_tokens', 'delta_norm_mix', 'delta_norm_ffn', 'delta_w_gate_up', 'delta_w_down', 'delta_fox_w_in', 'delta_fox_b_f', 'delta_fox_q_gain', 'delta_fox_k_gain', 'delta_fox_w_out', 'delta_gla_w_in', 'delta_gla_w_alpha2', 'delta_gla_b_alpha', 'delta_gla_o_gain', 'delta_gla_w_out', 'delta_gdn_w_in', 'delta_gdn_conv_w', 'delta_gdn_a_log', 'delta_gdn_dt_bias', 'delta_gdn_o_gain', 'delta_gdn_w_out', 'new_m_meta_tokens', 'new_m_norm_mix', 'new_m_norm_ffn', 'new_m_w_gate_up', 'new_m_w_down', 'new_m_fox_w_in', 'new_m_fox_b_f', 'new_m_fox_q_gain', 'new_m_fox_k_gain', 'new_m_fox_w_out', 'new_m_gla_w_in', 'new_m_gla_w_alpha2', 'new_m_gla_b_alpha', 'new_m_gla_o_gain', 'new_m_gla_w_out', 'new_m_gdn_w_in', 'new_m_gdn_conv_w', 'new_m_gdn_a_log', 'new_m_gdn_dt_bias', 'new_m_gdn_o_gain', 'new_m_gdn_w_out', 'new_v_meta_tokens', 'new_v_norm_mix', 'new_v_norm_ffn', 'new_v_w_gate_up', 'new_v_w_down', 'new_v_fox_w_in', 'new_v_fox_b_f', 'new_v_fox_q_gain', 'new_v_fox_k_gain', 'new_v_fox_w_out', 'new_v_gla_w_in', 'new_v_gla_w_alpha2', 'new_v_gla_b_alpha', 'new_v_gla_o_gain', 'new_v_gla_w_out', 'new_v_gdn_w_in', 'new_v_gdn_conv_w', 'new_v_gdn_a_log', 'new_v_gdn_dt_bias', 'new_v_gdn_o_gain', 'new_v_gdn_w_out']
TWIN_LEAF_KINDS = {'loss': 'loss', 'grad_x': 'grad_x', 'grad_meta_tokens': 'grad_w', 'grad_norm_mix': 'grad_w', 'grad_norm_ffn': 'grad_w', 'grad_w_gate_up': 'grad_w', 'grad_w_down': 'grad_w', 'grad_fox_w_in': 'grad_w', 'grad_fox_b_f': 'grad_w', 'grad_fox_q_gain': 'grad_w', 'grad_fox_k_gain': 'grad_w', 'grad_fox_w_out': 'grad_w', 'grad_gla_w_in': 'grad_w', 'grad_gla_w_alpha2': 'grad_w', 'grad_gla_b_alpha': 'grad_w', 'grad_gla_o_gain': 'grad_w', 'grad_gla_w_out': 'grad_w', 'grad_gdn_w_in': 'grad_w', 'grad_gdn_conv_w': 'grad_w', 'grad_gdn_a_log': 'grad_w', 'grad_gdn_dt_bias': 'grad_w', 'grad_gdn_o_gain': 'grad_w', 'grad_gdn_w_out': 'grad_w', 'delta_meta_tokens': 'delta_w', 'delta_norm_mix': 'delta_w', 'delta_norm_ffn': 'delta_w', 'delta_w_gate_up': 'delta_w', 'delta_w_down': 'delta_w', 'delta_fox_w_in': 'delta_w', 'delta_fox_b_f': 'delta_w', 'delta_fox_q_gain': 'delta_w', 'delta_fox_k_gain': 'delta_w', 'delta_fox_w_out': 'delta_w', 'delta_gla_w_in': 'delta_w', 'delta_gla_w_alpha2': 'delta_w', 'delta_gla_b_alpha': 'delta_w', 'delta_gla_o_gain': 'delta_w', 'delta_gla_w_out': 'delta_w', 'delta_gdn_w_in': 'delta_w', 'delta_gdn_conv_w': 'delta_w', 'delta_gdn_a_log': 'delta_w', 'delta_gdn_dt_bias': 'delta_w', 'delta_gdn_o_gain': 'delta_w', 'delta_gdn_w_out': 'delta_w', 'new_m_meta_tokens': 'new_m', 'new_m_norm_mix': 'new_m', 'new_m_norm_ffn': 'new_m', 'new_m_w_gate_up': 'new_m', 'new_m_w_down': 'new_m', 'new_m_fox_w_in': 'new_m', 'new_m_fox_b_f': 'new_m', 'new_m_fox_q_gain': 'new_m', 'new_m_fox_k_gain': 'new_m', 'new_m_fox_w_out': 'new_m', 'new_m_gla_w_in': 'new_m', 'new_m_gla_w_alpha2': 'new_m', 'new_m_gla_b_alpha': 'new_m', 'new_m_gla_o_gain': 'new_m', 'new_m_gla_w_out': 'new_m', 'new_m_gdn_w_in': 'new_m', 'new_m_gdn_conv_w': 'new_m', 'new_m_gdn_a_log': 'new_m', 'new_m_gdn_dt_bias': 'new_m', 'new_m_gdn_o_gain': 'new_m', 'new_m_gdn_w_out': 'new_m', 'new_v_meta_tokens': 'new_v', 'new_v_norm_mix': 'new_v', 'new_v_norm_ffn': 'new_v', 'new_v_w_gate_up': 'new_v', 'new_v_w_down': 'new_v', 'new_v_fox_w_in': 'new_v', 'new_v_fox_b_f': 'new_v', 'new_v_fox_q_gain': 'new_v', 'new_v_fox_k_gain': 'new_v', 'new_v_fox_w_out': 'new_v', 'new_v_gla_w_in': 'new_v', 'new_v_gla_w_alpha2': 'new_v', 'new_v_gla_b_alpha': 'new_v', 'new_v_gla_o_gain': 'new_v', 'new_v_gla_w_out': 'new_v', 'new_v_gdn_w_in': 'new_v', 'new_v_gdn_conv_w': 'new_v', 'new_v_gdn_a_log': 'new_v', 'new_v_gdn_dt_bias': 'new_v', 'new_v_gdn_o_gain': 'new_v', 'new_v_gdn_w_out': 'new_v'}


def _forward(args):
    return _fwd_reference(*[args[k] for k in FWD_PARAMS])


def _output_shape():
    out = _jax.eval_shape(lambda: _forward(_fwd_setup_inputs(0)))
    return out.shape, out.dtype

N_MICROBATCH = 1
ADAM_LR = 0.001
ADAM_B1 = 0.9
ADAM_B2 = 0.999
ADAM_EPS = 1e-08
ADAM_WD = 0.01
ADAM_STEP = 10
PER_EXAMPLE_BATCH_AXIS = {'x': 0, 'loss_target': 0}
SHARED_INPUTS = []
_WEIGHT_DTYPES = {'meta_tokens': _jnp.float32, 'norm_mix': _jnp.float32, 'norm_ffn': _jnp.float32, 'w_gate_up': _jnp.float32, 'w_down': _jnp.float32, 'fox_w_in': _jnp.float32, 'fox_b_f': _jnp.float32, 'fox_q_gain': _jnp.float32, 'fox_k_gain': _jnp.float32, 'fox_w_out': _jnp.float32, 'gla_w_in': _jnp.float32, 'gla_w_alpha2': _jnp.float32, 'gla_b_alpha': _jnp.float32, 'gla_o_gain': _jnp.float32, 'gla_w_out': _jnp.float32, 'gdn_w_in': _jnp.float32, 'gdn_conv_w': _jnp.float32, 'gdn_a_log': _jnp.float32, 'gdn_dt_bias': _jnp.float32, 'gdn_o_gain': _jnp.float32, 'gdn_w_out': _jnp.float32}
MOMENT_SCALE = {'meta_tokens': 4.992306e-02, 'norm_mix': 9.112872e+00, 'norm_ffn': 2.444450e+01, 'w_gate_up': 3.082980e-01, 'w_down': 4.982642e-01, 'fox_w_in': 2.170367e-01, 'fox_b_f': 3.368728e+01, 'fox_q_gain': 8.653431e+00, 'fox_k_gain': 8.660228e+00, 'fox_w_out': 2.873296e-01, 'gla_w_in': 6.134576e-01, 'gla_w_alpha2': 8.338273e-02, 'gla_b_alpha': 3.194512e-01, 'gla_o_gain': 4.540937e+01, 'gla_w_out': 5.288811e-01, 'gdn_w_in': 3.819213e-01, 'gdn_conv_w': 4.710399e-01, 'gdn_a_log': 5.520108e+01, 'gdn_dt_bias': 5.277389e+01, 'gdn_o_gain': 8.683390e+01, 'gdn_w_out': 7.386375e-01}


def _to_microbatches(a, axis):
    t = _jnp.moveaxis(a, axis, 0)
    t = t.reshape((N_MICROBATCH, t.shape[0] // N_MICROBATCH) + t.shape[1:])
    return _jnp.moveaxis(t, 1, axis + 1)


def setup_inputs(seed: int = 0) -> dict:
    inp = _fwd_setup_inputs(seed)
    key = _jax.random.fold_in(_jax.random.key(seed), 7919)
    shape, _ = _output_shape()
    out = dict(inp)
    out["loss_target"] = _jax.random.normal(_jax.random.fold_in(key, 0), shape, _jnp.float32)
    for i, name in enumerate(TWIN_WEIGHTS):
        w = inp[name].astype(_jnp.float32)
        if MOMENT_SCALE is None:
            s = _jnp.sqrt(_jnp.mean(_jnp.square(w)) + 1e-30)
        else:
            s = MOMENT_SCALE[name]
        km, kv = _jax.random.split(_jax.random.fold_in(key, i + 1))
        out[name] = w
        out["m_" + name] = s * _jax.random.normal(km, w.shape, _jnp.float32)
        out["v_" + name] = (s * s) * _jax.random.uniform(kv, w.shape, _jnp.float32, 0.5, 1.5)
    if N_MICROBATCH > 1:
        for name, axis in PER_EXAMPLE_BATCH_AXIS.items():
            out[name] = _to_microbatches(out[name], axis)
    return {'x': out['x'], 'meta_tokens': out['meta_tokens'], 'norm_mix': out['norm_mix'], 'norm_ffn': out['norm_ffn'], 'w_gate_up': out['w_gate_up'], 'w_down': out['w_down'], 'fox_w_in': out['fox_w_in'], 'fox_b_f': out['fox_b_f'], 'fox_q_gain': out['fox_q_gain'], 'fox_k_gain': out['fox_k_gain'], 'fox_w_out': out['fox_w_out'], 'gla_w_in': out['gla_w_in'], 'gla_w_alpha2': out['gla_w_alpha2'], 'gla_b_alpha': out['gla_b_alpha'], 'gla_o_gain': out['gla_o_gain'], 'gla_w_out': out['gla_w_out'], 'gdn_w_in': out['gdn_w_in'], 'gdn_conv_w': out['gdn_conv_w'], 'gdn_a_log': out['gdn_a_log'], 'gdn_dt_bias': out['gdn_dt_bias'], 'gdn_o_gain': out['gdn_o_gain'], 'gdn_w_out': out['gdn_w_out'], 'loss_target': out['loss_target'], 'm_meta_tokens': out['m_meta_tokens'], 'm_norm_mix': out['m_norm_mix'], 'm_norm_ffn': out['m_norm_ffn'], 'm_w_gate_up': out['m_w_gate_up'], 'm_w_down': out['m_w_down'], 'm_fox_w_in': out['m_fox_w_in'], 'm_fox_b_f': out['m_fox_b_f'], 'm_fox_q_gain': out['m_fox_q_gain'], 'm_fox_k_gain': out['m_fox_k_gain'], 'm_fox_w_out': out['m_fox_w_out'], 'm_gla_w_in': out['m_gla_w_in'], 'm_gla_w_alpha2': out['m_gla_w_alpha2'], 'm_gla_b_alpha': out['m_gla_b_alpha'], 'm_gla_o_gain': out['m_gla_o_gain'], 'm_gla_w_out': out['m_gla_w_out'], 'm_gdn_w_in': out['m_gdn_w_in'], 'm_gdn_conv_w': out['m_gdn_conv_w'], 'm_gdn_a_log': out['m_gdn_a_log'], 'm_gdn_dt_bias': out['m_gdn_dt_bias'], 'm_gdn_o_gain': out['m_gdn_o_gain'], 'm_gdn_w_out': out['m_gdn_w_out'], 'v_meta_tokens': out['v_meta_tokens'], 'v_norm_mix': out['v_norm_mix'], 'v_norm_ffn': out['v_norm_ffn'], 'v_w_gate_up': out['v_w_gate_up'], 'v_w_down': out['v_w_down'], 'v_fox_w_in': out['v_fox_w_in'], 'v_fox_b_f': out['v_fox_b_f'], 'v_fox_q_gain': out['v_fox_q_gain'], 'v_fox_k_gain': out['v_fox_k_gain'], 'v_fox_w_out': out['v_fox_w_out'], 'v_gla_w_in': out['v_gla_w_in'], 'v_gla_w_alpha2': out['v_gla_w_alpha2'], 'v_gla_b_alpha': out['v_gla_b_alpha'], 'v_gla_o_gain': out['v_gla_o_gain'], 'v_gla_w_out': out['v_gla_w_out'], 'v_gdn_w_in': out['v_gdn_w_in'], 'v_gdn_conv_w': out['v_gdn_conv_w'], 'v_gdn_a_log': out['v_gdn_a_log'], 'v_gdn_dt_bias': out['v_gdn_dt_bias'], 'v_gdn_o_gain': out['v_gdn_o_gain'], 'v_gdn_w_out': out['v_gdn_w_out']}


def _loss(weights, diff, rest, loss_target):
    with _jax.named_scope("forward"):
        args = {**rest, TWIN_DIFF_INPUT: diff, **{k: w.astype(_WEIGHT_DTYPES[k]) for k, w in weights.items()}}
        y = _forward(args)
    with _jax.named_scope("loss_head"):
        err = _jnp.square(y.astype(_jnp.float32) - loss_target)
        return 0.5 * _jnp.sum(_jnp.mean(err, axis=-1)) if err.ndim else 0.5 * err


def _adamw(w, g, m, v):
    m = ADAM_B1 * m + (1.0 - ADAM_B1) * g
    v = ADAM_B2 * v + (1.0 - ADAM_B2) * _jnp.square(g)
    m_hat = m / (1.0 - ADAM_B1 ** ADAM_STEP)
    v_hat = v / (1.0 - ADAM_B2 ** ADAM_STEP)
    delta = -ADAM_LR * (m_hat / (_jnp.sqrt(v_hat) + ADAM_EPS) + ADAM_WD * w)
    return delta, m, v


def reference(x, meta_tokens, norm_mix, norm_ffn, w_gate_up, w_down, fox_w_in, fox_b_f, fox_q_gain, fox_k_gain, fox_w_out, gla_w_in, gla_w_alpha2, gla_b_alpha, gla_o_gain, gla_w_out, gdn_w_in, gdn_conv_w, gdn_a_log, gdn_dt_bias, gdn_o_gain, gdn_w_out, loss_target, m_meta_tokens, m_norm_mix, m_norm_ffn, m_w_gate_up, m_w_down, m_fox_w_in, m_fox_b_f, m_fox_q_gain, m_fox_k_gain, m_fox_w_out, m_gla_w_in, m_gla_w_alpha2, m_gla_b_alpha, m_gla_o_gain, m_gla_w_out, m_gdn_w_in, m_gdn_conv_w, m_gdn_a_log, m_gdn_dt_bias, m_gdn_o_gain, m_gdn_w_out, v_meta_tokens, v_norm_mix, v_norm_ffn, v_w_gate_up, v_w_down, v_fox_w_in, v_fox_b_f, v_fox_q_gain, v_fox_k_gain, v_fox_w_out, v_gla_w_in, v_gla_w_alpha2, v_gla_b_alpha, v_gla_o_gain, v_gla_w_out, v_gdn_w_in, v_gdn_conv_w, v_gdn_a_log, v_gdn_dt_bias, v_gdn_o_gain, v_gdn_w_out):
    given = dict(x=x, meta_tokens=meta_tokens, norm_mix=norm_mix, norm_ffn=norm_ffn, w_gate_up=w_gate_up, w_down=w_down, fox_w_in=fox_w_in, fox_b_f=fox_b_f, fox_q_gain=fox_q_gain, fox_k_gain=fox_k_gain, fox_w_out=fox_w_out, gla_w_in=gla_w_in, gla_w_alpha2=gla_w_alpha2, gla_b_alpha=gla_b_alpha, gla_o_gain=gla_o_gain, gla_w_out=gla_w_out, gdn_w_in=gdn_w_in, gdn_conv_w=gdn_conv_w, gdn_a_log=gdn_a_log, gdn_dt_bias=gdn_dt_bias, gdn_o_gain=gdn_o_gain, gdn_w_out=gdn_w_out, loss_target=loss_target, m_meta_tokens=m_meta_tokens, m_norm_mix=m_norm_mix, m_norm_ffn=m_norm_ffn, m_w_gate_up=m_w_gate_up, m_w_down=m_w_down, m_fox_w_in=m_fox_w_in, m_fox_b_f=m_fox_b_f, m_fox_q_gain=m_fox_q_gain, m_fox_k_gain=m_fox_k_gain, m_fox_w_out=m_fox_w_out, m_gla_w_in=m_gla_w_in, m_gla_w_alpha2=m_gla_w_alpha2, m_gla_b_alpha=m_gla_b_alpha, m_gla_o_gain=m_gla_o_gain, m_gla_w_out=m_gla_w_out, m_gdn_w_in=m_gdn_w_in, m_gdn_conv_w=m_gdn_conv_w, m_gdn_a_log=m_gdn_a_log, m_gdn_dt_bias=m_gdn_dt_bias, m_gdn_o_gain=m_gdn_o_gain, m_gdn_w_out=m_gdn_w_out, v_meta_tokens=v_meta_tokens, v_norm_mix=v_norm_mix, v_norm_ffn=v_norm_ffn, v_w_gate_up=v_w_gate_up, v_w_down=v_w_down, v_fox_w_in=v_fox_w_in, v_fox_b_f=v_fox_b_f, v_fox_q_gain=v_fox_q_gain, v_fox_k_gain=v_fox_k_gain, v_fox_w_out=v_fox_w_out, v_gla_w_in=v_gla_w_in, v_gla_w_alpha2=v_gla_w_alpha2, v_gla_b_alpha=v_gla_b_alpha, v_gla_o_gain=v_gla_o_gain, v_gla_w_out=v_gla_w_out, v_gdn_w_in=v_gdn_w_in, v_gdn_conv_w=v_gdn_conv_w, v_gdn_a_log=v_gdn_a_log, v_gdn_dt_bias=v_gdn_dt_bias, v_gdn_o_gain=v_gdn_o_gain, v_gdn_w_out=v_gdn_w_out)
    weights = {n: given[n] for n in TWIN_WEIGHTS}
    shared = {n: given[n] for n in SHARED_INPUTS}
    per_example = {n: given[n] for n in ['x']}
    grad_fn = _jax.value_and_grad(_loss, argnums=(0, 1))

    def one_microbatch(ex, loss_target):
        ex = dict(ex)
        diff = ex.pop(TWIN_DIFF_INPUT)
        return grad_fn(weights, diff, {**shared, **ex}, loss_target)

    if N_MICROBATCH == 1:
        loss, (grad_w, grad_x) = one_microbatch(per_example, given["loss_target"])
    else:
        def body(carry, xs):
            loss_sum, grad_sum = carry
            l_k, (gw_k, gx_k) = one_microbatch(xs[0], xs[1])
            with _jax.named_scope("update"):
                return (loss_sum + l_k, _jax.tree.map(_jnp.add, grad_sum, gw_k)), gx_k

        init = (_jnp.zeros((), _jnp.float32), _jax.tree.map(_jnp.zeros_like, weights))
        (loss, grad_w), grad_x = _jax.lax.scan(body, init, (per_example, given["loss_target"]))
    with _jax.named_scope("update"):
        delta_w, new_m, new_v = {}, {}, {}
        for n in TWIN_WEIGHTS:
            delta_w[n], new_m[n], new_v[n] = _adamw(weights[n], grad_w[n], given["m_" + n], given["v_" + n])
    return (loss, grad_x, *[grad_w[n] for n in TWIN_WEIGHTS], *[delta_w[n] for n in TWIN_WEIGHTS],
            *[new_m[n] for n in TWIN_WEIGHTS], *[new_v[n] for n in TWIN_WEIGHTS])
```

```python
import functools

import jax
import jax.numpy as jnp
from jax import lax
from jax.experimental import pallas as pl
from jax.experimental.pallas import tpu as pltpu

F32, BF16 = jnp.float32, jnp.bfloat16
D = 1024
N_META = 16
ROW0 = 128
META0 = ROW0 - N_META
EPS = 1e-6
LANES = 128
VMEM_LIMIT = 56 * 1024 * 1024

FOX_H, FOX_DH = 16, 64
FOX_INP = 4224
GLA_H, GLA_DK, GLA_DV, GLA_RANK = 4, 128, 256, 16
GLA_QK, GLA_V = 512, 1024
GLA_INP = 3200
GLA_NORM = 16.0
GDN_H, GDN_DK, GDN_DV = 8, 128, 128
GDN_CONV = 3072
GDN_INP = 4224
CHUNK = 64
D_FF = 2816
DEPTH = 4

ADAM_LR, ADAM_B1, ADAM_B2, ADAM_EPS, ADAM_WD, ADAM_STEP = 0.001, 0.9, 0.999, 1e-08, 0.01, 10

MESH = pl.DeviceIdType.MESH
ANY = pl.BlockSpec(memory_space=pl.ANY)
VM = pl.BlockSpec(memory_space=pltpu.VMEM)


def _params(sem=None, **kw):
    if sem is not None:
        kw["dimension_semantics"] = sem
    return pltpu.CompilerParams(vmem_limit_bytes=VMEM_LIMIT, **kw)


def _tile(n, cap, mult=LANES):
    best = None
    for t in range(mult, min(n, cap) + 1, mult):
        if n % t == 0:
            best = t
    return best if best is not None else n


def nn(a, b, **kw):
    return jnp.dot(a, b, preferred_element_type=F32, **kw)


def nt(a, b, **kw):
    return lax.dot_general(a, b, (((1,), (1,)), ((), ())), preferred_element_type=F32, **kw)


def tn(a, b, **kw):
    return lax.dot_general(a, b, (((0,), (0,)), ((), ())), preferred_element_type=F32, **kw)


def _split3(x):
    hi = x.astype(BF16)
    r = x - hi.astype(F32)
    mid = r.astype(BF16)
    lo = (r - mid.astype(F32)).astype(BF16)
    return hi, mid, lo


def _sel_l(sel, x):
    a, b, c = _split3(x)
    return nn(sel, a) + nn(sel, b) + nn(sel, c)


def _sel_r(x, sel):
    a, b, c = _split3(x)
    return nn(a, sel) + nn(b, sel) + nn(c, sel)


def _iota(shape, dim):
    return lax.broadcasted_iota(jnp.int32, shape, dim)


def _tri(n, upper=False, strict=False):
    i, j = _iota((n, n), 0), _iota((n, n), 1)
    if upper:
        m = (j > i) if strict else (j >= i)
    else:
        m = (j < i) if strict else (j <= i)
    return m


def _sigmoid(x):
    return 1.0 / (1.0 + jnp.exp(-x))


def _log_sigmoid(x):
    return jnp.minimum(x, 0.0) - jnp.log(1.0 + jnp.exp(-jnp.abs(x)))


def _softplus(x):
    return jnp.maximum(x, 0.0) + jnp.log(1.0 + jnp.exp(-jnp.abs(x)))


def _silu(x):
    return x * _sigmoid(x)


def _dsilu(x):
    s = _sigmoid(x)
    return s * (1.0 + x * (1.0 - s))


def _mm(a, b, *, ta=False, tb=False, add=None, out_dtype=F32, name):
    m, k = (a.shape[1], a.shape[0]) if ta else a.shape
    n = b.shape[0] if tb else b.shape[1]
    assert k == (b.shape[1] if tb else b.shape[0])
    tm, tn_, tk = _tile(m, 1024), _tile(n, 768), _tile(k, 1408)
    nk = k // tk

    def body(*refs):
        if add is None:
            a_ref, b_ref, o_ref, acc = refs
        else:
            a_ref, b_ref, r_ref, o_ref, acc = refs
        kk = pl.program_id(2)

        @pl.when(kk == 0)
        def _():
            acc[...] = jnp.zeros_like(acc)

        av, bv = a_ref[...].astype(BF16), b_ref[...].astype(BF16)
        dims = (((0,) if ta else (1,), (1,) if tb else (0,)), ((), ()))
        acc[...] += lax.dot_general(av, bv, dims, preferred_element_type=F32)

        @pl.when(kk == nk - 1)
        def _():
            r = acc[...]
            if add is not None:
                r = r + r_ref[...].astype(F32)
            o_ref[...] = r.astype(out_dtype)

    a_spec = pl.BlockSpec((tk, tm), lambda i, j, q: (q, i)) if ta else pl.BlockSpec((tm, tk), lambda i, j, q: (i, q))
    b_spec = pl.BlockSpec((tn_, tk), lambda i, j, q: (j, q)) if tb else pl.BlockSpec((tk, tn_), lambda i, j, q: (q, j))
    o_spec = pl.BlockSpec((tm, tn_), lambda i, j, q: (i, j))
    ins, specs = [a, b], [a_spec, b_spec]
    if add is not None:
        ins.append(add)
        specs.append(o_spec)
    return pl.pallas_call(
        body, name=name, grid=(m // tm, n // tn_, nk), in_specs=specs, out_specs=o_spec,
        out_shape=jax.ShapeDtypeStruct((m, n), out_dtype),
        scratch_shapes=[pltpu.VMEM((tm, tn_), F32)],
        compiler_params=_params(("parallel", "parallel", "arbitrary")),
    )(*ins)


def _rms_fwd(h, g, *, name):
    lp = h.shape[0]
    tr = _tile(lp, 512)

    def body(h_ref, g_ref, y_ref):
        x = h_ref[...]
        r = lax.rsqrt(jnp.mean(x * x, axis=-1, keepdims=True) + EPS)
        y_ref[...] = (x * r * g_ref[...]).astype(BF16)

    return pl.pallas_call(
        body, name=name, grid=(lp // tr,),
        in_specs=[pl.BlockSpec((tr, D), lambda i: (i, 0)), pl.BlockSpec((1, D), lambda i: (0, 0))],
        out_specs=pl.BlockSpec((tr, D), lambda i: (i, 0)),
        out_shape=jax.ShapeDtypeStruct((lp, D), BF16), compiler_params=_params(("parallel",)),
    )(h, g.reshape(1, D))


def _rms_bwd(h, g, dy, dres, *, name):
    lp = h.shape[0]
    tr = _tile(lp, 512)

    def body(h_ref, g_ref, dy_ref, dr_ref, dh_ref, dg_ref):
        @pl.when(pl.program_id(0) == 0)
        def _():
            dg_ref[...] = jnp.zeros_like(dg_ref)

        x, dyv = h_ref[...], dy_ref[...].astype(F32)
        r = lax.rsqrt(jnp.mean(x * x, axis=-1, keepdims=True) + EPS)
        u = dyv * g_ref[...]
        dx = r * u - x * (r * r * r) * jnp.mean(x * u, axis=-1, keepdims=True)
        dh_ref[...] = dr_ref[...] + dx
        dg_ref[...] += jnp.sum(dyv * x * r, axis=0, keepdims=True)

    return pl.pallas_call(
        body, name=name, grid=(lp // tr,),
        in_specs=[pl.BlockSpec((tr, D), lambda i: (i, 0)), pl.BlockSpec((1, D), lambda i: (0, 0)),
                  pl.BlockSpec((tr, D), lambda i: (i, 0)), pl.BlockSpec((tr, D), lambda i: (i, 0))],
        out_specs=[pl.BlockSpec((tr, D), lambda i: (i, 0)), pl.BlockSpec((1, D), lambda i: (0, 0))],
        out_shape=[jax.ShapeDtypeStruct((lp, D), F32), jax.ShapeDtypeStruct((1, D), F32)],
        compiler_params=_params(("arbitrary",)),
    )(h, g.reshape(1, D), dy, dres)


def _swiglu_fwd(gu, *, name):
    lp = gu.shape[0]
    tr, tc = _tile(lp, 512), _tile(D_FF, 1408)
    nc = D_FF // tc

    def body(g_ref, u_ref, a_ref):
        a_ref[...] = (_silu(g_ref[...]) * u_ref[...]).astype(BF16)

    return pl.pallas_call(
        body, name=name, grid=(lp // tr, nc),
        in_specs=[pl.BlockSpec((tr, tc), lambda i, j: (i, j)), pl.BlockSpec((tr, tc), lambda i, j: (i, j + nc))],
        out_specs=pl.BlockSpec((tr, tc), lambda i, j: (i, j)),
        out_shape=jax.ShapeDtypeStruct((lp, D_FF), BF16), compiler_params=_params(("parallel", "parallel")),
    )(gu, gu)


def _swiglu_bwd(gu, dact, *, name):
    lp = gu.shape[0]
    tr = _tile(lp, 128)

    def body(gu_ref, da_ref, o_ref):
        gv, uv, da = gu_ref[:, 0:D_FF], gu_ref[:, D_FF:2 * D_FF], da_ref[...]
        o_ref[:, 0:D_FF] = (da * uv * _dsilu(gv)).astype(BF16)
        o_ref[:, D_FF:2 * D_FF] = (da * _silu(gv)).astype(BF16)

    return pl.pallas_call(
        body, name=name, grid=(lp // tr,),
        in_specs=[pl.BlockSpec((tr, 2 * D_FF), lambda i: (i, 0)), pl.BlockSpec((tr, D_FF), lambda i: (i, 0))],
        out_specs=pl.BlockSpec((tr, 2 * D_FF), lambda i: (i, 0)),
        out_shape=jax.ShapeDtypeStruct((lp, 2 * D_FF), BF16), compiler_params=_params(("parallel",)),
    )(gu, dact)


def _loss_head(h, target):
    lp = h.shape[0]
    nb = lp // ROW0

    def body(h_ref, t_ref, dh_ref, l_ref):
        i = pl.program_id(0)

        @pl.when(i == 0)
        def _():
            l_ref[...] = jnp.zeros_like(l_ref)
            dh_ref[...] = jnp.zeros_like(dh_ref)

        @pl.when(i > 0)
        def _():
            err = h_ref[...] - t_ref[...]
            dh_ref[...] = err * (1.0 / D)
            l_ref[...] += jnp.sum(err * err) * (0.5 / D)

    return pl.pallas_call(
        body, name="loss_head", grid=(nb,),
        in_specs=[pl.BlockSpec((ROW0, D), lambda i: (i, 0)), pl.BlockSpec((ROW0, D), lambda i: (jnp.maximum(i - 1, 0), 0))],
        out_specs=[pl.BlockSpec((ROW0, D), lambda i: (i, 0)), pl.BlockSpec((8, LANES), lambda i: (0, 0))],
        out_shape=[jax.ShapeDtypeStruct((lp, D), F32), jax.ShapeDtypeStruct((8, LANES), F32)],
        compiler_params=_params(("arbitrary",)),
    )(h, target)


def _adamw(w, g, m, v, *, name):
    shape = w.shape
    c = shape[-1]
    r = w.size // c
    w2, g2, m2, v2 = (t.reshape(r, c) for t in (w, g, m, v))
    tr = _tile(r, max(8, (1 << 19) // c), 8)

    def body(w_ref, g_ref, m_ref, v_ref, d_ref, nm_ref, nv_ref):
        gv = g_ref[...]
        nm = ADAM_B1 * m_ref[...] + (1.0 - ADAM_B1) * gv
        nv = ADAM_B2 * v_ref[...] + (1.0 - ADAM_B2) * (gv * gv)
        m_hat = nm / (1.0 - ADAM_B1 ** ADAM_STEP)
        v_hat = nv / (1.0 - ADAM_B2 ** ADAM_STEP)
        d_ref[...] = -ADAM_LR * (m_hat / (jnp.sqrt(v_hat) + ADAM_EPS) + ADAM_WD * w_ref[...])
        nm_ref[...] = nm
        nv_ref[...] = nv

    spec = pl.BlockSpec((tr, c), lambda i: (i, 0))
    outs = pl.pallas_call(
        body, name=name, grid=(r // tr,), in_specs=[spec] * 4, out_specs=[spec] * 3,
        out_shape=[jax.ShapeDtypeStruct((r, c), F32)] * 3, compiler_params=_params(("parallel",)),
    )(w2, g2, m2, v2)
    return tuple(o.reshape(shape) for o in outs)


def _head_sel(n_heads, width, lanes=LANES):
    r, c = _iota((n_heads * width, lanes), 0), _iota((n_heads * width, lanes), 1)
    down = (r // width == c).astype(BF16)
    r2, c2 = _iota((lanes, n_heads * width), 0), _iota((lanes, n_heads * width), 1)
    up = (c2 // width == r2).astype(BF16)
    return down, up


def _fox_prep(proj, b_f, q_gain, k_gain):
    lp = proj.shape[0]
    nb = lp // LANES

    def body(p_ref, bf_ref, qg_ref, kg_ref, qn_ref, kn_ref, v_ref, cb_ref, ct_ref, carry):
        i = pl.program_id(0)

        @pl.when(i == 0)
        def _():
            carry[...] = jnp.zeros_like(carry)

        down, up = _head_sel(FOX_H, FOX_DH)

        def normed(x, gain):
            ms = _sel_r(x * x, down) * (1.0 / FOX_DH)
            r = _sel_r(lax.rsqrt(ms + EPS), up)
            return x * r * gain

        qn_ref[...] = (normed(p_ref[:, 0:D], qg_ref[...]) * (FOX_DH ** -0.5)).astype(BF16)
        kn_ref[...] = normed(p_ref[:, D:2 * D], kg_ref[...]).astype(BF16)
        v_ref[...] = p_ref[:, 2 * D:3 * D].astype(BF16)
        lane = _iota((LANES, LANES), 1)
        lf = jnp.where(lane < FOX_H, _log_sigmoid(p_ref[:, 4 * D:4 * D + LANES] + bf_ref[...]), 0.0)
        c = _sel_l(_tri(LANES).astype(BF16), lf) + carry[0:1, :]
        carry[...] = jnp.broadcast_to(c[LANES - 1:LANES, :], carry.shape)
        cb_ref[...] = _sel_r(c, up)
        ct = c.T
        for h in range(FOX_H):
            ct_ref[h] = jnp.broadcast_to(ct[h:h + 1, :], (8, LANES))

    blk = lambda w, j: pl.BlockSpec((LANES, w), lambda i, j=j: (i, j))
    row = pl.BlockSpec((1, D), lambda i: (0, 0))
    return pl.pallas_call(
        body, name="fox_prep", grid=(nb,),
        in_specs=[pl.BlockSpec((LANES, FOX_INP), lambda i: (i, 0)), pl.BlockSpec((1, LANES), lambda i: (0, 0)), row, row],
        out_specs=[blk(D, 0)] * 4 + [pl.BlockSpec((FOX_H, 8, LANES), lambda i: (0, 0, i))],
        out_shape=[jax.ShapeDtypeStruct((lp, D), BF16)] * 3 + [jax.ShapeDtypeStruct((lp, D), F32),
                                                               jax.ShapeDtypeStruct((FOX_H, 8, lp), F32)],
        scratch_shapes=[pltpu.VMEM((8, LANES), F32)],
        compiler_params=_params(("arbitrary",)),
    )(proj, jnp.pad(b_f, (0, LANES - FOX_H)).reshape(1, LANES), jnp.tile(q_gain, FOX_H).reshape(1, D),
      jnp.tile(k_gain, FOX_H).reshape(1, D))


def _fox_mask(qi, kj, t):
    qpos = qi * t + _iota((t, t), 0)
    kpos = kj * t + _iota((t, t), 1)
    return (kpos <= qpos) & ((kpos >= META0) | (qpos < META0))


def _fox_attn_fwd(qn, kn, vb, cb, ct, proj):
    lp = qn.shape[0]
    t = LANES
    nq = lp // t
    npair = FOX_H // 2

    def body(q_ref, k_ref, v_ref, cb_ref, ct_ref, gate_ref, o_ref, og_ref, lse_ref):
        i = pl.program_id(1)
        lane = _iota((t, LANES), 1)
        left = lane < FOX_DH
        q = q_ref[...]
        zero = jnp.zeros_like(q)
        qs = (jnp.where(left, q, zero), jnp.where(left, zero, q))
        cq = (cb_ref[:, 0:1], cb_ref[:, FOX_DH:FOX_DH + 1])

        def step(j, carry):
            off = pl.multiple_of(j * t, t)
            k = k_ref[pl.ds(off, t), :]
            v = v_ref[pl.ds(off, t), :]
            mask = _fox_mask(i, j, t)
            out = []
            for hh in range(2):
                m, l, acc = carry[hh]
                ck = ct_ref[hh, 0:1, pl.ds(off, t)]
                s = nt(qs[hh], k) + (cq[hh] - ck)
                s = jnp.where(mask, s, -1e30)
                m2 = jnp.maximum(m, jnp.max(s, axis=-1, keepdims=True))
                p = jnp.exp(s - m2)
                a = jnp.exp(m - m2)
                p_hi = p.astype(BF16)
                p_lo = (p - p_hi.astype(F32)).astype(BF16)
                out.append((m2, a * l + jnp.sum(p, axis=-1, keepdims=True), a * acc + nn(p_hi, v) + nn(p_lo, v)))
            return tuple(out)

        init = tuple((jnp.full((t, 1), -1e30, F32), jnp.zeros((t, 1), F32), jnp.zeros((t, LANES), F32)) for _ in range(2))
        (ma, la, acca), (mb, lb, accb) = lax.fori_loop(0, i + 1, step, init)
        o = jnp.where(left, acca / la, accb / lb)
        o_ref[...] = o
        og_ref[...] = (o * _sigmoid(gate_ref[...])).astype(BF16)
        lse_ref[...] = jnp.where(left, ma + jnp.log(la), mb + jnp.log(lb))

    qspec = pl.BlockSpec((t, LANES), lambda p, i: (i, p))
    kspec = pl.BlockSpec((lp, LANES), lambda p, i: (0, p))
    return pl.pallas_call(
        body, name="fox_attn_fwd", grid=(npair, nq),
        in_specs=[qspec, kspec, kspec, qspec, pl.BlockSpec((2, 8, lp), lambda p, i: (p, 0, 0)),
                  pl.BlockSpec((t, LANES), lambda p, i: (i, 3 * D // LANES + p))],
        out_specs=[qspec] * 3,
        out_shape=[jax.ShapeDtypeStruct((lp, D), F32), jax.ShapeDtypeStruct((lp, D), BF16), jax.ShapeDtypeStruct((lp, D), F32)],
        compiler_params=_params(("parallel", "arbitrary")),
    )(qn, kn, vb, cb, ct, proj)


def _fox_gate_bwd(dog, o, proj):
    lp = o.shape[0]
    tr = _tile(lp, 512)

    def body(d_ref, o_ref, g_ref, do_ref, delta_ref, dgate_ref):
        down, up = _head_sel(FOX_H, FOX_DH)
        sg = _sigmoid(g_ref[...])
        dv, ov = d_ref[...], o_ref[...]
        do = (dv * sg).astype(BF16)
        do_ref[...] = do
        delta_ref[...] = _sel_r(_sel_r(do.astype(F32) * ov, down), up)
        dgate_ref[...] = dv * ov * sg * (1.0 - sg)

    spec = pl.BlockSpec((tr, D), lambda i: (i, 0))
    return pl.pallas_call(
        body, name="fox_gate_bwd", grid=(lp // tr,),
        in_specs=[spec, spec, pl.BlockSpec((tr, D), lambda i: (i, 3))], out_specs=[spec] * 3,
        out_shape=[jax.ShapeDtypeStruct((lp, D), BF16), jax.ShapeDtypeStruct((lp, D), F32), jax.ShapeDtypeStruct((lp, D), F32)],
        compiler_params=_params(("parallel",)),
    )(dog, o, proj)


def _fox_attn_bwd(qn, kn, vb, cb, ct, lse, do, delta):
    lp = qn.shape[0]
    t = LANES
    nq = lp // t
    npair = FOX_H // 2

    def body(q_ref, k_ref, v_ref, cb_ref, ct_ref, lse_ref, do_ref, dl_ref, dq_ref, dk_ref, dv_ref, dc_ref):
        j = pl.program_id(1)

        @pl.when(j == 0)
        def _():
            dq_ref[...] = jnp.zeros_like(dq_ref)

        lane = _iota((t, LANES), 1)
        left = lane < FOX_DH
        koff = pl.multiple_of(j * t, t)
        k = k_ref[...]
        v = v_ref[...]
        zero = jnp.zeros_like(k)
        ks = (jnp.where(left, k, zero), jnp.where(left, zero, k))
        ck = (ct_ref[0, 0:1, pl.ds(koff, t)], ct_ref[1, 0:1, pl.ds(koff, t)])
        one = jnp.ones_like(k)

        def step(i, carry):
            dv_acc, ra, rb = carry
            off = pl.multiple_of(i * t, t)
            q = q_ref[pl.ds(off, t), :]
            dov = do_ref[pl.ds(off, t), :]
            cbv = cb_ref[pl.ds(off, t), :]
            lsev = lse_ref[pl.ds(off, t), :]
            dlv = dl_ref[pl.ds(off, t), :]
            mask = _fox_mask(i, j, t)
            r_out = []
            dq_blk = jnp.zeros((t, LANES), F32)
            for hh in range(2):
                lo = 0 if hh == 0 else FOX_DH
                mine = left if hh == 0 else jnp.logical_not(left)
                qh = jnp.where(mine, q, zero)
                doh = jnp.where(mine, dov, zero)
                s = nt(qh, k) + (cbv[:, lo:lo + 1] - ck[hh])
                p = jnp.where(mask, jnp.exp(s - lsev[:, lo:lo + 1]), 0.0)
                dp = nt(doh, v)
                ds32 = p * (dp - dlv[:, lo:lo + 1])
                ds = ds32.astype(BF16)
                ds_lo = (ds32 - ds.astype(F32)).astype(BF16)
                dv_acc = dv_acc + tn(p.astype(BF16), doh)
                r_out.append(tn(ds, jnp.where(mine, q, one)) + tn(ds_lo, jnp.where(mine, zero, one)))
                dq_blk = dq_blk + nn(ds, ks[hh])
            dq_ref[pl.ds(off, t), :] += dq_blk
            return dv_acc, ra + r_out[0], rb + r_out[1]

        z = jnp.zeros((t, LANES), F32)
        dv_acc, ra, rb = lax.fori_loop(j, nq, step, (z, z, z))
        dv_ref[...] = dv_acc
        dk_ref[...] = jnp.where(left, ra, rb)
        dc_ref[...] = -jnp.where(left, rb, ra)

    full = pl.BlockSpec((lp, LANES), lambda p, j: (0, p))
    kblk = pl.BlockSpec((t, LANES), lambda p, j: (j, p))
    return pl.pallas_call(
        body, name="fox_attn_bwd", grid=(npair, nq),
        in_specs=[full, kblk, kblk, full, pl.BlockSpec((2, 8, lp), lambda p, j: (p, 0, 0)), full, full, full],
        out_specs=[full, kblk, kblk, kblk],
        out_shape=[jax.ShapeDtypeStruct((lp, D), F32)] * 4,
        compiler_params=_params(("parallel", "arbitrary")),
    )(qn, kn, vb, cb, ct, lse, do, delta)


def _fox_prep_bwd(proj, b_f, q_gain, k_gain, dqn, dkn, dv, dgate, dc):
    lp = proj.shape[0]
    nb = lp // LANES

    def body(p_ref, bf_ref, qg_ref, kg_ref, dq_ref, dk_ref, dv_ref, dg_ref, dc_ref,
             dp_ref, dqg_ref, dkg_ref, dbf_ref, carry):
        i = pl.program_id(0)

        @pl.when(i == 0)
        def _():
            carry[...] = jnp.zeros_like(carry)
            dqg_ref[...] = jnp.zeros_like(dqg_ref)
            dkg_ref[...] = jnp.zeros_like(dkg_ref)
            dbf_ref[...] = jnp.zeros_like(dbf_ref)

        down, up = _head_sel(FOX_H, FOX_DH)

        def norm_bwd(x, gain, dy, scale, dgain_ref):
            ms = _sel_r(x * x, down) * (1.0 / FOX_DH)
            r = _sel_r(lax.rsqrt(ms + EPS), up)
            u = dy * gain * scale
            mean_xu = _sel_r(_sel_r(x * u, down) * (1.0 / FOX_DH), up)
            dgain_ref[...] += jnp.sum(dy * scale * x * r, axis=0, keepdims=True)
            return r * u - x * (r * r * r) * mean_xu

        dp_ref[:, 0:D] = norm_bwd(p_ref[:, 0:D], qg_ref[...], dq_ref[...], FOX_DH ** -0.5, dqg_ref).astype(BF16)
        dp_ref[:, D:2 * D] = norm_bwd(p_ref[:, D:2 * D], kg_ref[...], dk_ref[...], 1.0, dkg_ref).astype(BF16)
        dp_ref[:, 2 * D:3 * D] = dv_ref[...].astype(BF16)
        dp_ref[:, 3 * D:4 * D] = dg_ref[...].astype(BF16)
        r_, c_ = _iota((D, LANES), 0), _iota((D, LANES), 1)
        pick = (r_ == (c_ // 2) * LANES + jnp.where(c_ % 2 == 0, FOX_DH, 0)) & (c_ < FOX_H)
        dcc = _sel_r(dc_ref[...], pick.astype(BF16))
        dlf = _sel_l(_tri(LANES, upper=True).astype(BF16), dcc) + carry[0:1, :]
        carry[...] = jnp.broadcast_to(dlf[0:1, :], carry.shape)
        lane = _iota((LANES, LANES), 1)
        z = p_ref[:, 4 * D:4 * D + LANES] + bf_ref[...]
        df = jnp.where(lane < FOX_H, dlf * _sigmoid(-z), 0.0)
        dp_ref[:, 4 * D:4 * D + LANES] = df.astype(BF16)
        dbf_ref[...] += jnp.sum(df, axis=0, keepdims=True)

    rev = lambda i: (nb - 1 - i, 0)
    blk = pl.BlockSpec((LANES, D), rev)
    row = pl.BlockSpec((1, D), lambda i: (0, 0))
    row128 = pl.BlockSpec((1, LANES), lambda i: (0, 0))
    return pl.pallas_call(
        body, name="fox_prep_bwd", grid=(nb,),
        in_specs=[pl.BlockSpec((LANES, FOX_INP), rev), row128, row, row, blk, blk, blk, blk, blk],
        out_specs=[pl.BlockSpec((LANES, FOX_INP), rev), row, row, row128],
        out_shape=[jax.ShapeDtypeStruct((lp, FOX_INP), BF16), jax.ShapeDtypeStruct((1, D), F32),
                   jax.ShapeDtypeStruct((1, D), F32), jax.ShapeDtypeStruct((1, LANES), F32)],
        scratch_shapes=[pltpu.VMEM((8, LANES), F32)],
        compiler_params=_params(("arbitrary",)),
    )(proj, jnp.pad(b_f, (0, LANES - FOX_H)).reshape(1, LANES), jnp.tile(q_gain, FOX_H).reshape(1, D),
      jnp.tile(k_gain, FOX_H).reshape(1, D), dqn, dkn, dv, dgate, dc)


def _gla_gates(p_ref, wa_ref, ba_ref):
    a_lr = p_ref[:, 3072:3072 + LANES]
    z = nn(a_lr.astype(BF16), wa_ref[...].astype(BF16)) + ba_ref[...]
    g = _log_sigmoid(z) * (1.0 / GLA_NORM)
    b = _sel_l(_tri(CHUNK).astype(BF16), g)
    return a_lr, z, b


def _gla_head_fwd(q, k, v, b, st0):
    eb = jnp.exp(b)
    bl = b[CHUNK - 1:CHUNK, :]
    qe, ke, kd = q * eb, k * jnp.exp(-b), k * jnp.exp(bl - b)
    a = jnp.where(_tri(CHUNK), nt(qe, ke), 0.0)
    o = nn(a, v) + nt(qe, st0)
    st1 = st0 * jnp.exp(bl) + tn(v, kd)
    return o, st1, (qe, ke, kd, a, bl)


def _gla_fwd(proj, w_alpha2, b_alpha, o_gain):
    lp = proj.shape[0]
    nc = lp // CHUNK

    def body(p_ref, wa_ref, ba_ref, og_ref, o_ref, y_ref, s_ref, st):
        @pl.when(pl.program_id(0) == 0)
        def _():
            st[...] = jnp.zeros_like(st)

        _, _, b = _gla_gates(p_ref, wa_ref, ba_ref)
        for h in range(GLA_H):
            q = p_ref[:, h * GLA_DK:(h + 1) * GLA_DK] * (GLA_DK ** -0.5)
            k = p_ref[:, GLA_QK + h * GLA_DK:GLA_QK + (h + 1) * GLA_DK]
            v = p_ref[:, 2 * GLA_QK + h * GLA_DV:2 * GLA_QK + (h + 1) * GLA_DV]
            r = p_ref[:, 2 * GLA_QK + GLA_V + h * GLA_DV:2 * GLA_QK + GLA_V + (h + 1) * GLA_DV]
            st0 = st[h]
            s_ref[0, h] = st0
            o, st1, _ = _gla_head_fwd(q, k, v, b[:, h * GLA_DK:(h + 1) * GLA_DK], st0)
            st[h] = st1
            o_ref[:, h * GLA_DV:(h + 1) * GLA_DV] = o
            rs = lax.rsqrt(jnp.mean(o * o, axis=-1, keepdims=True) + EPS)
            y_ref[:, h * GLA_DV:(h + 1) * GLA_DV] = (o * rs * og_ref[...] * _silu(r)).astype(BF16)

    blk = pl.BlockSpec((CHUNK, D), lambda i: (i, 0))
    return pl.pallas_call(
        body, name="gla_fwd", grid=(nc,),
        in_specs=[pl.BlockSpec((CHUNK, GLA_INP), lambda i: (i, 0)), pl.BlockSpec((LANES, GLA_QK), lambda i: (0, 0)),
                  pl.BlockSpec((1, GLA_QK), lambda i: (0, 0)), pl.BlockSpec((1, GLA_DV), lambda i: (0, 0))],
        out_specs=[blk, blk, pl.BlockSpec((1, GLA_H, GLA_DV, GLA_DK), lambda i: (i, 0, 0, 0))],
        out_shape=[jax.ShapeDtypeStruct((lp, D), F32), jax.ShapeDtypeStruct((lp, D), BF16),
                   jax.ShapeDtypeStruct((nc, GLA_H, GLA_DV, GLA_DK), F32)],
        scratch_shapes=[pltpu.VMEM((GLA_H, GLA_DV, GLA_DK), F32)],
        compiler_params=_params(("arbitrary",)),
    )(proj, jnp.pad(w_alpha2, ((0, LANES - GLA_RANK), (0, 0))), b_alpha.reshape(1, GLA_QK), o_gain.reshape(1, GLA_DV))


def _gla_bwd(proj, w_alpha2, b_alpha, o_gain, o, states, dy):
    lp = proj.shape[0]
    nc = lp // CHUNK

    def body(p_ref, wa_ref, ba_ref, og_ref, o_ref, s_ref, dy_ref, dp_ref, dwa_ref, dba_ref, dog_ref, dst):
        @pl.when(pl.program_id(0) == 0)
        def _():
            dst[...] = jnp.zeros_like(dst)
            dwa_ref[...] = jnp.zeros_like(dwa_ref)
            dba_ref[...] = jnp.zeros_like(dba_ref)
            dog_ref[...] = jnp.zeros_like(dog_ref)

        a_lr, z, b_all = _gla_gates(p_ref, wa_ref, ba_ref)
        last_row = _iota((CHUNK, GLA_DK), 0) == CHUNK - 1
        rev = _tri(CHUNK, upper=True).astype(BF16)
        dg_parts = []
        for h in range(GLA_H):
            scale = GLA_DK ** -0.5
            q = p_ref[:, h * GLA_DK:(h + 1) * GLA_DK] * scale
            k = p_ref[:, GLA_QK + h * GLA_DK:GLA_QK + (h + 1) * GLA_DK]
            v = p_ref[:, 2 * GLA_QK + h * GLA_DV:2 * GLA_QK + (h + 1) * GLA_DV]
            r = p_ref[:, 2 * GLA_QK + GLA_V + h * GLA_DV:2 * GLA_QK + GLA_V + (h + 1) * GLA_DV]
            b = b_all[:, h * GLA_DK:(h + 1) * GLA_DK]
            st0 = s_ref[0, h]
            dst1 = dst[h]
            ov = o_ref[:, h * GLA_DV:(h + 1) * GLA_DV]
            dyv = dy_ref[:, h * GLA_DV:(h + 1) * GLA_DV]
            rs = lax.rsqrt(jnp.mean(ov * ov, axis=-1, keepdims=True) + EPS)
            on = ov * rs
            dr = dyv * on * og_ref[...] * _dsilu(r)
            don = dyv * _silu(r)
            dog_ref[...] += jnp.sum(don * on, axis=0, keepdims=True)
            u = don * og_ref[...]
            do = rs * u - ov * (rs * rs * rs) * jnp.mean(ov * u, axis=-1, keepdims=True)
            eb = jnp.exp(b)
            _, _, (qe, ke, kd, a, bl) = _gla_head_fwd(q, k, v, b, st0)
            da = jnp.where(_tri(CHUNK), nt(do, v), 0.0)
            dkd = nn(v, dst1)
            dvv = tn(a, do) + nt(kd, dst1)
            dqe = nn(da, ke) + nn(do, st0)
            dke = tn(da, qe)
            ebl = jnp.exp(bl)
            dst[h] = dst1 * ebl + tn(do, qe)
            db = dqe * qe - dke * ke - dkd * kd
            db_last = jnp.sum(dkd * kd, axis=0, keepdims=True) + jnp.sum(dst1 * st0, axis=0, keepdims=True) * ebl
            db = db + jnp.where(last_row, db_last, 0.0)
            dg_parts.append(_sel_l(rev, db))
            dp_ref[:, h * GLA_DK:(h + 1) * GLA_DK] = (dqe * eb * scale).astype(BF16)
            dp_ref[:, GLA_QK + h * GLA_DK:GLA_QK + (h + 1) * GLA_DK] = (dke * jnp.exp(-b) + dkd * jnp.exp(bl - b)).astype(BF16)
            dp_ref[:, 2 * GLA_QK + h * GLA_DV:2 * GLA_QK + (h + 1) * GLA_DV] = dvv.astype(BF16)
            dp_ref[:, 2 * GLA_QK + GLA_V + h * GLA_DV:2 * GLA_QK + GLA_V + (h + 1) * GLA_DV] = dr.astype(BF16)
        dg = jnp.concatenate(dg_parts, axis=1)
        dz = dg * (1.0 / GLA_NORM) * _sigmoid(-z)
        dzb = dz.astype(BF16)
        dp_ref[:, 3072:3072 + LANES] = nt(dzb, wa_ref[...].astype(BF16)).astype(BF16)
        dwa_ref[...] += tn(a_lr.astype(BF16), dzb)
        dba_ref[...] += jnp.sum(dz, axis=0, keepdims=True)

    rv = lambda i: (nc - 1 - i, 0)
    blk = pl.BlockSpec((CHUNK, D), rv)
    fixed = lambda r, c: pl.BlockSpec((r, c), lambda i: (0, 0))
    return pl.pallas_call(
        body, name="gla_bwd", grid=(nc,),
        in_specs=[pl.BlockSpec((CHUNK, GLA_INP), rv), fixed(LANES, GLA_QK), fixed(1, GLA_QK), fixed(1, GLA_DV), blk,
                  pl.BlockSpec((1, GLA_H, GLA_DV, GLA_DK), lambda i: (nc - 1 - i, 0, 0, 0)), blk],
        out_specs=[pl.BlockSpec((CHUNK, GLA_INP), rv), fixed(LANES, GLA_QK), fixed(1, GLA_QK), fixed(1, GLA_DV)],
        out_shape=[jax.ShapeDtypeStruct((lp, GLA_INP), BF16), jax.ShapeDtypeStruct((LANES, GLA_QK), F32),
                   jax.ShapeDtypeStruct((1, GLA_QK), F32), jax.ShapeDtypeStruct((1, GLA_DV), F32)],
        scratch_shapes=[pltpu.VMEM((GLA_H, GLA_DV, GLA_DK), F32)],
        compiler_params=_params(("arbitrary",)),
    )(proj, jnp.pad(w_alpha2, ((0, LANES - GLA_RANK), (0, 0))), b_alpha.reshape(1, GLA_QK), o_gain.reshape(1, GLA_DV),
      o, states, dy)


HI = lax.Precision.HIGHEST


def _gdn_pre(prev_ref, p_ref, cw_ref, al_ref, dt_ref):
    xc = jnp.concatenate([prev_ref[:, 0:GDN_CONV], p_ref[:, 0:GDN_CONV]], axis=0)
    shifted = [pltpu.roll(xc, 3 - j, 0)[CHUNK:, :] if j < 3 else xc[CHUNK:, :] for j in range(4)]
    conv = sum(shifted[j] * cw_ref[j:j + 1, :] for j in range(4))
    act = _silu(conv)
    slab = p_ref[:, 4096:4096 + LANES]
    lane = _iota((CHUNK, LANES), 1)
    zs = slab + dt_ref[...]
    g = jnp.where(lane < GDN_H, -jnp.exp(al_ref[...]) * _softplus(zs), 0.0)
    bs = _sel_l(_tri(CHUNK).astype(BF16), g)
    beta = _sigmoid(slab)
    return shifted, conv, act, slab, zs, g, bs, beta


def _l2n(x):
    r = lax.rsqrt(jnp.sum(x * x, axis=-1, keepdims=True) + EPS)
    return x * r, r


def _gdn_head_fwd(q, k, v, beta, bcol, brow, s0):
    ii, jj = _iota((CHUNK, CHUNK), 0), _iota((CHUNK, CHUNK), 1)
    diff = bcol - brow
    dm = jnp.where(ii >= jj, jnp.exp(jnp.where(ii >= jj, diff, 0.0)), 0.0)
    dstrict = jnp.where(ii > jj, dm, 0.0)
    eb = jnp.exp(bcol)
    bl = bcol[CHUNK - 1:CHUNK, :]
    kb, vb = k * beta, v * beta
    nmat = nt(kb, k) * dstrict
    eye = (ii == jj).astype(F32)
    x = eye - nmat
    pw = nn(nmat, nmat, precision=HI)
    for it in range(5):
        x = x + nn(x, pw, precision=HI)
        if it < 4:
            pw = nn(pw, pw, precision=HI)
    kbe = kb * eb
    u, w = nn(x, vb, precision=HI), nn(x, kbe, precision=HI)
    vn = u - nn(w, s0)
    pm = nt(q, k) * dm
    qe = q * eb
    o = nn(pm, vn) + nn(qe, s0)
    kd = k * jnp.exp(bl - bcol)
    s1 = s0 * jnp.exp(bl) + tn(kd, vn)
    return o, s1, (dm, dstrict, eb, bl, kb, vb, nmat, x, kbe, u, w, vn, pm, qe, kd)


def _gdn_heads(act, beta_slab, bs, h):
    qa = act[:, h * GDN_DK:(h + 1) * GDN_DK]
    ka = act[:, GDN_H * GDN_DK + h * GDN_DK:GDN_H * GDN_DK + (h + 1) * GDN_DK]
    v = act[:, 2 * GDN_H * GDN_DK + h * GDN_DV:2 * GDN_H * GDN_DK + (h + 1) * GDN_DV]
    return qa, ka, v, beta_slab[:, GDN_H + h:GDN_H + h + 1], bs[:, h:h + 1]


def _gdn_fwd(proj, conv_w, a_log, dt_bias, o_gain):
    lp = proj.shape[0]
    nc = lp // CHUNK

    def body(prev_ref, p_ref, cw_ref, al_ref, dt_ref, og_ref, o_ref, y_ref, s_ref, st):
        @pl.when(pl.program_id(0) == 0)
        def _():
            st[...] = jnp.zeros_like(st)

        _, _, act, _, _, _, bs, beta = _gdn_pre(prev_ref, p_ref, cw_ref, al_ref, dt_ref)
        bst = bs.T
        for h in range(GDN_H):
            qa, ka, v, bet, bcol = _gdn_heads(act, beta, bs, h)
            q = _l2n(qa)[0] * (GDN_DK ** -0.5)
            k = _l2n(ka)[0]
            s0 = st[h]
            s_ref[0, h] = s0
            o, s1, _ = _gdn_head_fwd(q, k, v, bet, bcol, bst[h:h + 1, :], s0)
            st[h] = s1
            o_ref[:, h * GDN_DV:(h + 1) * GDN_DV] = o
            rs = lax.rsqrt(jnp.mean(o * o, axis=-1, keepdims=True) + EPS)
            gate = p_ref[:, GDN_CONV + h * GDN_DV:GDN_CONV + (h + 1) * GDN_DV]
            y_ref[:, h * GDN_DV:(h + 1) * GDN_DV] = (o * rs * og_ref[...] * _silu(gate)).astype(BF16)

    blk = pl.BlockSpec((CHUNK, D), lambda i: (i, 0))
    fixed = lambda r, c: pl.BlockSpec((r, c), lambda i: (0, 0))
    return pl.pallas_call(
        body, name="gdn_fwd", grid=(nc,),
        in_specs=[pl.BlockSpec((CHUNK, GDN_INP), lambda i: (jnp.maximum(i - 1, 0), 0)),
                  pl.BlockSpec((CHUNK, GDN_INP), lambda i: (i, 0)), fixed(8, GDN_CONV), fixed(1, LANES), fixed(1, LANES),
                  fixed(1, GDN_DV)],
        out_specs=[blk, blk, pl.BlockSpec((1, GDN_H, GDN_DK, GDN_DV), lambda i: (i, 0, 0, 0))],
        out_shape=[jax.ShapeDtypeStruct((lp, D), F32), jax.ShapeDtypeStruct((lp, D), BF16),
                   jax.ShapeDtypeStruct((nc, GDN_H, GDN_DK, GDN_DV), F32)],
        scratch_shapes=[pltpu.VMEM((GDN_H, GDN_DK, GDN_DV), F32)],
        compiler_params=_params(("arbitrary",)),
    )(proj, proj, jnp.pad(conv_w.reshape(4, GDN_CONV), ((0, 4), (0, 0))), jnp.pad(a_log, (0, LANES - GDN_H)).reshape(1, LANES),
      jnp.pad(dt_bias, (0, LANES - GDN_H)).reshape(1, LANES), o_gain.reshape(1, GDN_DV))


def _gdn_bwd(proj, conv_w, a_log, dt_bias, o_gain, o, states, dy):
    lp = proj.shape[0]
    nc = lp // CHUNK

    def body(prev_ref, p_ref, cw_ref, al_ref, dt_ref, og_ref, o_ref, s_ref, dy_ref,
             dp_ref, dcw_ref, dal_ref, ddt_ref, dog_ref, dst, dconv_next):
        @pl.when(pl.program_id(0) == 0)
        def _():
            dst[...] = jnp.zeros_like(dst)
            dconv_next[...] = jnp.zeros_like(dconv_next)
            dcw_ref[...] = jnp.zeros_like(dcw_ref)
            dal_ref[...] = jnp.zeros_like(dal_ref)
            ddt_ref[...] = jnp.zeros_like(ddt_ref)
            dog_ref[...] = jnp.zeros_like(dog_ref)

        shifted, conv, act, slab, zs, g, bs, beta = _gdn_pre(prev_ref, p_ref, cw_ref, al_ref, dt_ref)
        bst = bs.T
        lane = _iota((CHUNK, LANES), 1)
        ones = jnp.ones((CHUNK, LANES), F32)
        db_slab = jnp.zeros((CHUNK, LANES), F32)
        dbeta_slab = jnp.zeros((CHUNK, LANES), F32)
        last_row = _iota((CHUNK, 1), 0) == CHUNK - 1
        dact_q, dact_k, dact_v = [], [], []
        for h in range(GDN_H):
            qa, ka, v, bet, bcol = _gdn_heads(act, beta, bs, h)
            qn_, rq = _l2n(qa)
            k, rk = _l2n(ka)
            scale = GDN_DK ** -0.5
            q = qn_ * scale
            s0 = s_ref[0, h]
            ds1 = dst[h]
            ov = o_ref[:, h * GDN_DV:(h + 1) * GDN_DV]
            dyv = dy_ref[:, h * GDN_DV:(h + 1) * GDN_DV]
            gate = p_ref[:, GDN_CONV + h * GDN_DV:GDN_CONV + (h + 1) * GDN_DV]
            rs = lax.rsqrt(jnp.mean(ov * ov, axis=-1, keepdims=True) + EPS)
            on = ov * rs
            dp_ref[:, GDN_CONV + h * GDN_DV:GDN_CONV + (h + 1) * GDN_DV] = (dyv * on * og_ref[...] * _dsilu(gate)).astype(BF16)
            don = dyv * _silu(gate)
            dog_ref[...] += jnp.sum(don * on, axis=0, keepdims=True)
            uu = don * og_ref[...]
            do = rs * uu - ov * (rs * rs * rs) * jnp.mean(ov * uu, axis=-1, keepdims=True)
            _, _, (dm, dstrict, eb, bl, kb, vb, nmat, tinv, kbe, u, w, vn, pm, qe, kd) = _gdn_head_fwd(
                q, k, v, bet, bcol, bst[h:h + 1, :], s0)
            ebl = jnp.exp(bl)
            dvn = tn(pm, do) + nn(kd, ds1)
            dpr = nt(do, vn)
            dqk, gp = dpr * dm, dpr * pm
            dqe = nt(do, s0)
            dkd = nt(vn, ds1)
            dst[h] = ds1 * ebl + tn(qe, do) - tn(w, dvn)
            du_ = tn(tinv, dvn, precision=HI)
            dw_ = tn(tinv, -nt(dvn, s0), precision=HI)
            dn = -(nt(du_, u) + nt(dw_, w))
            dkk, gn = dn * dstrict, dn * nmat
            dkb = nn(dkk, k) + dw_ * eb
            dk = tn(dkk, kb) + tn(dqk, q) + dkd * jnp.exp(bl - bcol) + dkb * bet
            dq = nn(dqk, k) + dqe * eb
            dbeta = jnp.sum(dkb * k, axis=-1, keepdims=True) + jnp.sum(du_ * v, axis=-1, keepdims=True)
            dv_ = du_ * bet
            gsum = gp + gn
            colsum = tn(gsum, ones, precision=HI)[:, 0:1]
            skd = jnp.sum(dkd * kd, axis=-1, keepdims=True)
            db = (jnp.sum(gsum, axis=-1, keepdims=True) - colsum + jnp.sum(dqe * qe, axis=-1, keepdims=True)
                  + jnp.sum(dw_ * kbe, axis=-1, keepdims=True) - skd)
            db_last = jnp.sum(skd, axis=0, keepdims=True) + jnp.sum(ds1 * s0) * ebl
            db = db + jnp.where(last_row, db_last, 0.0)
            db_slab = db_slab + jnp.where(lane == h, db, 0.0)
            dbeta_slab = dbeta_slab + jnp.where(lane == GDN_H + h, dbeta, 0.0)
            dqn = dq * scale
            dact_q.append(rq * dqn - qa * (rq * rq * rq) * jnp.sum(qa * dqn, axis=-1, keepdims=True))
            dact_k.append(rk * dk - ka * (rk * rk * rk) * jnp.sum(ka * dk, axis=-1, keepdims=True))
            dact_v.append(dv_)
        dact = jnp.concatenate(dact_q + dact_k + dact_v, axis=1)
        dconv = dact * _dsilu(conv)
        for j in range(4):
            dcw_ref[j:j + 1, :] += jnp.sum(dconv * shifted[j], axis=0, keepdims=True)
        dcat = jnp.concatenate([dconv, dconv_next[...]], axis=0)
        dx = dconv * cw_ref[3:4, :]
        for j in range(3):
            dx = dx + pltpu.roll(dcat, 2 * CHUNK - (3 - j), 0)[:CHUNK, :] * cw_ref[j:j + 1, :]
        dconv_next[...] = dconv
        dp_ref[:, 0:GDN_CONV] = dx.astype(BF16)
        dg = _sel_l(_tri(CHUNK, upper=True).astype(BF16), db_slab)
        da = dg * (-jnp.exp(al_ref[...])) * _sigmoid(zs)
        da = jnp.where(lane < GDN_H, da, 0.0)
        dal_ref[...] += jnp.sum(dg * g, axis=0, keepdims=True)
        ddt_ref[...] += jnp.sum(da, axis=0, keepdims=True)
        dp_ref[:, 4096:4096 + LANES] = (da + dbeta_slab * beta * (1.0 - beta)).astype(BF16)

    rv = lambda i: (nc - 1 - i, 0)
    blk = pl.BlockSpec((CHUNK, D), rv)
    fixed = lambda r, c: pl.BlockSpec((r, c), lambda i: (0, 0))
    return pl.pallas_call(
        body, name="gdn_bwd", grid=(nc,),
        in_specs=[pl.BlockSpec((CHUNK, GDN_INP), lambda i: (jnp.maximum(nc - 2 - i, 0), 0)),
                  pl.BlockSpec((CHUNK, GDN_INP), rv), fixed(8, GDN_CONV), fixed(1, LANES), fixed(1, LANES), fixed(1, GDN_DV),
                  blk, pl.BlockSpec((1, GDN_H, GDN_DK, GDN_DV), lambda i: (nc - 1 - i, 0, 0, 0)), blk],
        out_specs=[pl.BlockSpec((CHUNK, GDN_INP), rv), fixed(8, GDN_CONV), fixed(1, LANES), fixed(1, LANES), fixed(1, GDN_DV)],
        out_shape=[jax.ShapeDtypeStruct((lp, GDN_INP), BF16), jax.ShapeDtypeStruct((8, GDN_CONV), F32),
                   jax.ShapeDtypeStruct((1, LANES), F32), jax.ShapeDtypeStruct((1, LANES), F32),
                   jax.ShapeDtypeStruct((1, GDN_DV), F32)],
        scratch_shapes=[pltpu.VMEM((GDN_H, GDN_DK, GDN_DV), F32), pltpu.VMEM((CHUNK, GDN_CONV), F32)],
        compiler_params=_params(("arbitrary",)),
    )(proj, proj, jnp.pad(conv_w.reshape(4, GDN_CONV), ((0, 4), (0, 0))), jnp.pad(a_log, (0, LANES - GDN_H)).reshape(1, LANES),
      jnp.pad(dt_bias, (0, LANES - GDN_H)).reshape(1, LANES), o_gain.reshape(1, GDN_DV), o, states, dy)


def _coords():
    return lax.axis_index("x"), lax.axis_index("y"), lax.axis_index("c")


def _other_chips(x, y):
    return [(1 - x, y, 2 * (1 - x) + y), (x, 1 - y, 2 * x + 1 - y), (1 - x, 1 - y, 2 * (1 - x) + 1 - y)]


def _gather8(v, *, reduce, name):
    r, c = v.shape

    def body(v_ref, out_ref, *scratch):
        if reduce:
            buf, send_sems, recv_sems = scratch
        else:
            buf = out_ref
            send_sems, recv_sems = scratch
        x, y, cc = _coords()
        me = 4 * x + 2 * y + cc
        buf[me] = v_ref[...]
        copies = []
        for k in range(1, 8):
            px, py, pc = x ^ (k >> 2), y ^ ((k >> 1) & 1), cc ^ (k & 1)
            copies.append(pltpu.make_async_remote_copy(
                src_ref=v_ref, dst_ref=buf.at[me], send_sem=send_sems.at[k - 1], recv_sem=recv_sems.at[k - 1],
                device_id=(px, py, pc), device_id_type=MESH))
        for cp in copies:
            cp.start()
        for k in range(1, 8):
            peer = (x ^ (k >> 2)) * 4 + (y ^ ((k >> 1) & 1)) * 2 + (cc ^ (k & 1))
            pltpu.make_async_remote_copy(
                src_ref=v_ref, dst_ref=buf.at[peer], send_sem=send_sems.at[k - 1], recv_sem=recv_sems.at[k - 1],
                device_id=(x, y, cc), device_id_type=MESH).wait_recv()
        for cp in copies:
            cp.wait_send()
        if reduce:
            acc = buf[0]
            for d in range(1, 8):
                acc = acc + buf[d]
            out_ref[...] = acc

    scratch = [pltpu.SemaphoreType.DMA((7,)), pltpu.SemaphoreType.DMA((7,))]
    if reduce:
        scratch = [pltpu.VMEM((8, r, c), F32)] + scratch
    return pl.pallas_call(
        body, name=name, in_specs=[VM], out_specs=VM,
        out_shape=jax.ShapeDtypeStruct((r, c) if reduce else (8, r, c), F32),
        scratch_shapes=scratch, compiler_params=_params(),
    )(v)


def _ag_weights(w):
    r, c = w.shape
    half = r // 2

    def body(w_ref, out_ref, send_sems, recv_sems, lsem):
        x, y, cc = _coords()
        p = 2 * x + y
        chips = _other_chips(x, y)

        def rows(chip, hf):
            return out_ref.at[chip, pl.ds(hf * half, half), :]

        mine = pltpu.make_async_copy(w_ref, out_ref.at[p], lsem)
        mine.start()
        first = [pltpu.make_async_remote_copy(
            src_ref=w_ref.at[pl.ds(cc * half, half), :], dst_ref=rows(p, cc), send_sem=send_sems.at[k],
            recv_sem=recv_sems.at[k], device_id=(cx, cy, cc), device_id_type=MESH) for k, (cx, cy, _) in enumerate(chips)]
        for cp in first:
            cp.start()
        passed = []
        for k, (_, _, blk) in enumerate(chips):
            pltpu.make_async_remote_copy(
                src_ref=rows(blk, cc), dst_ref=rows(blk, cc), send_sem=send_sems.at[k], recv_sem=recv_sems.at[k],
                device_id=(x, y, cc), device_id_type=MESH).wait_recv()
            fw = pltpu.make_async_remote_copy(
                src_ref=rows(blk, cc), dst_ref=rows(blk, cc), send_sem=send_sems.at[3 + k], recv_sem=recv_sems.at[3 + k],
                device_id=(x, y, 1 - cc), device_id_type=MESH)
            fw.start()
            passed.append(fw)
        for k, (_, _, blk) in enumerate(chips):
            pltpu.make_async_remote_copy(
                src_ref=rows(blk, 1 - cc), dst_ref=rows(blk, 1 - cc), send_sem=send_sems.at[3 + k], recv_sem=recv_sems.at[3 + k],
                device_id=(x, y, cc), device_id_type=MESH).wait_recv()
        for cp in first + passed:
            cp.wait_send()
        mine.wait()

    return pl.pallas_call(
        body, name="ag_weights", in_specs=[ANY], out_specs=ANY, out_shape=jax.ShapeDtypeStruct((4, r, c), w.dtype),
        scratch_shapes=[pltpu.SemaphoreType.DMA((6,)), pltpu.SemaphoreType.DMA((6,)), pltpu.SemaphoreType.DMA],
        compiler_params=_params(),
    )(w)


def _swap_halves(g, *, name):
    nb, r, c = g.shape
    half = r // 2

    def body(g_ref, out_ref, send_sem, recv_sem):
        x, y, cc = _coords()
        cp = pltpu.make_async_remote_copy(
            src_ref=g_ref.at[:, pl.ds((1 - cc) * half, half), :], dst_ref=out_ref, send_sem=send_sem, recv_sem=recv_sem,
            device_id=(x, y, 1 - cc), device_id_type=MESH)
        cp.start()
        cp.wait()

    return pl.pallas_call(
        body, name=name, in_specs=[ANY], out_specs=ANY, out_shape=jax.ShapeDtypeStruct((nb, half, c), g.dtype),
        scratch_shapes=[pltpu.SemaphoreType.DMA, pltpu.SemaphoreType.DMA], compiler_params=_params(),
    )(g)


def _my_half_index():
    return lax.axis_index("c").astype(jnp.int32).reshape(1)


def _add_halves(g, got):
    nb, r, c = g.shape
    half = r // 2
    tr = _tile(half, 512, 16)
    nt_ = half // tr

    def body(c_ref, a_ref, b_ref, o_ref):
        o_ref[...] = (a_ref[...].astype(F32) + b_ref[...].astype(F32)).astype(BF16)

    return pl.pallas_call(
        body, name="rs_add_sibling",
        grid_spec=pltpu.PrefetchScalarGridSpec(
            num_scalar_prefetch=1, grid=(nb, nt_),
            in_specs=[pl.BlockSpec((1, tr, c), lambda b, i, cr: (b, cr[0] * nt_ + i, 0)),
                      pl.BlockSpec((1, tr, c), lambda b, i, cr: (b, i, 0))],
            out_specs=pl.BlockSpec((1, tr, c), lambda b, i, cr: (b, i, 0))),
        out_shape=jax.ShapeDtypeStruct((nb, half, c), BF16), compiler_params=_params(("parallel", "parallel")),
    )(_my_half_index(), g, got)


def _scatter_chips(s):
    nb, hrows, c = s.shape

    def body(s_ref, out_ref, send_sems, recv_sems):
        x, y, cc = _coords()
        p = 2 * x + y
        chips = _other_chips(x, y)
        sends = [pltpu.make_async_remote_copy(
            src_ref=s_ref.at[blk], dst_ref=out_ref.at[k], send_sem=send_sems.at[k], recv_sem=recv_sems.at[k],
            device_id=(cx, cy, cc), device_id_type=MESH) for k, (cx, cy, blk) in enumerate(chips)]
        for cp in sends:
            cp.start()
        for k in range(3):
            pltpu.make_async_remote_copy(
                src_ref=s_ref.at[p], dst_ref=out_ref.at[k], send_sem=send_sems.at[k], recv_sem=recv_sems.at[k],
                device_id=(x, y, cc), device_id_type=MESH).wait_recv()
        for cp in sends:
            cp.wait_send()

    return pl.pallas_call(
        body, name="rs_scatter", in_specs=[ANY], out_specs=ANY, out_shape=jax.ShapeDtypeStruct((3, hrows, c), s.dtype),
        scratch_shapes=[pltpu.SemaphoreType.DMA((3,)), pltpu.SemaphoreType.DMA((3,))], compiler_params=_params(),
    )(s)


def _sum_chips(s, got):
    nb, hrows, c = s.shape
    tr = _tile(hrows, 512, 16)

    def body(idx_ref, own_ref, got_ref, o_ref):
        p = idx_ref[0]
        own = own_ref[0].astype(F32)
        parts = [got_ref[k].astype(F32) for k in range(3)]
        acc = jnp.zeros_like(own)
        for q in range(4):
            val = own
            for k, rel in enumerate((2, 1, 3)):
                val = jnp.where((p ^ rel) == q, parts[k], val)
            acc = acc + val
        o_ref[...] = acc

    idx = (2 * lax.axis_index("x") + lax.axis_index("y")).astype(jnp.int32).reshape(1)
    return pl.pallas_call(
        body, name="rs_sum_chips",
        grid_spec=pltpu.PrefetchScalarGridSpec(
            num_scalar_prefetch=1, grid=(hrows // tr,),
            in_specs=[pl.BlockSpec((1, tr, c), lambda i, pr: (pr[0], i, 0)), pl.BlockSpec((3, tr, c), lambda i, pr: (0, i, 0))],
            out_specs=pl.BlockSpec((tr, c), lambda i, pr: (i, 0))),
        out_shape=jax.ShapeDtypeStruct((hrows, c), F32), compiler_params=_params(("parallel",)),
    )(idx, s, got)


def _join_halves(t):
    hrows, c = t.shape

    def body(t_ref, out_ref, send_sem, recv_sem, lsem):
        x, y, cc = _coords()
        mine = pltpu.make_async_copy(t_ref, out_ref.at[cc], lsem)
        mine.start()
        cp = pltpu.make_async_remote_copy(
            src_ref=t_ref, dst_ref=out_ref.at[cc], send_sem=send_sem, recv_sem=recv_sem,
            device_id=(x, y, 1 - cc), device_id_type=MESH)
        cp.start()
        pltpu.make_async_remote_copy(
            src_ref=t_ref, dst_ref=out_ref.at[1 - cc], send_sem=send_sem, recv_sem=recv_sem,
            device_id=(x, y, cc), device_id_type=MESH).wait_recv()
        cp.wait_send()
        mine.wait()

    return pl.pallas_call(
        body, name="rs_join", in_specs=[ANY], out_specs=ANY, out_shape=jax.ShapeDtypeStruct((2, hrows, c), t.dtype),
        scratch_shapes=[pltpu.SemaphoreType.DMA, pltpu.SemaphoreType.DMA, pltpu.SemaphoreType.DMA], compiler_params=_params(),
    )(t)


def _reduce_scatter(g):
    got = _swap_halves(g, name="rs_swap")
    s = _add_halves(g, got)
    recv = _scatter_chips(s)
    t = _sum_chips(s, recv)
    return _join_halves(t).reshape(g.shape[1], g.shape[2])


_BIG = (("w_gate_up", 2), ("w_down", 1), ("fox_w_in", 2), ("fox_w_out", 1), ("gla_w_in", 2), ("gla_w_out", 1),
        ("gdn_w_in", 2), ("gdn_w_out", 1))
_SMALL_SHARDED = (("meta_tokens", 1), ("gla_w_alpha2", 2), ("gdn_conv_w", 3))
_REPLICATED = ("norm_mix", "norm_ffn", "fox_b_f", "fox_q_gain", "fox_k_gain", "gla_b_alpha", "gla_o_gain",
               "gdn_a_log", "gdn_dt_bias", "gdn_o_gain")
_WEIGHTS = ("meta_tokens", "norm_mix", "norm_ffn", "w_gate_up", "w_down", "fox_w_in", "fox_b_f", "fox_q_gain",
            "fox_k_gain", "fox_w_out", "gla_w_in", "gla_w_alpha2", "gla_b_alpha", "gla_o_gain", "gla_w_out",
            "gdn_w_in", "gdn_conv_w", "gdn_a_log", "gdn_dt_bias", "gdn_o_gain", "gdn_w_out")
_PACK_ROWS = 512


def _pack(arrays, width, row_mult, dtype):
    flat = jnp.concatenate([a.astype(dtype).reshape(-1) for a in arrays])
    per = width * row_mult
    n = -(-flat.shape[0] // per) * per
    return jnp.pad(flat, (0, n - flat.shape[0])).reshape(n // width, width)


def _unpack(flat, shapes):
    out, off = [], 0
    for s in shapes:
        n = 1
        for d in s:
            n *= d
        out.append(flat[off:off + n].reshape(s))
        off += n
    return out


def _pad_cols(w, n):
    return jnp.pad(w, [(0, 0)] * (w.ndim - 1) + [(0, n - w.shape[-1])])


def kernel(x, meta_tokens, norm_mix, norm_ffn, w_gate_up, w_down, fox_w_in, fox_b_f, fox_q_gain, fox_k_gain, fox_w_out, gla_w_in, gla_w_alpha2, gla_b_alpha, gla_o_gain, gla_w_out, gdn_w_in, gdn_conv_w, gdn_a_log, gdn_dt_bias, gdn_o_gain, gdn_w_out, loss_target, m_meta_tokens, m_norm_mix, m_norm_ffn, m_w_gate_up, m_w_down, m_fox_w_in, m_fox_b_f, m_fox_q_gain, m_fox_k_gain, m_fox_w_out, m_gla_w_in, m_gla_w_alpha2, m_gla_b_alpha, m_gla_o_gain, m_gla_w_out, m_gdn_w_in, m_gdn_conv_w, m_gdn_a_log, m_gdn_dt_bias, m_gdn_o_gain, m_gdn_w_out, v_meta_tokens, v_norm_mix, v_norm_ffn, v_w_gate_up, v_w_down, v_fox_w_in, v_fox_b_f, v_fox_q_gain, v_fox_k_gain, v_fox_w_out, v_gla_w_in, v_gla_w_alpha2, v_gla_b_alpha, v_gla_o_gain, v_gla_w_out, v_gdn_w_in, v_gdn_conv_w, v_gdn_a_log, v_gdn_dt_bias, v_gdn_o_gain, v_gdn_w_out):
    W = dict(meta_tokens=meta_tokens, norm_mix=norm_mix, norm_ffn=norm_ffn, w_gate_up=w_gate_up, w_down=w_down,
             fox_w_in=fox_w_in, fox_b_f=fox_b_f, fox_q_gain=fox_q_gain, fox_k_gain=fox_k_gain, fox_w_out=fox_w_out,
             gla_w_in=gla_w_in, gla_w_alpha2=gla_w_alpha2, gla_b_alpha=gla_b_alpha, gla_o_gain=gla_o_gain,
             gla_w_out=gla_w_out, gdn_w_in=gdn_w_in, gdn_conv_w=gdn_conv_w, gdn_a_log=gdn_a_log,
             gdn_dt_bias=gdn_dt_bias, gdn_o_gain=gdn_o_gain, gdn_w_out=gdn_w_out)
    M = dict(meta_tokens=m_meta_tokens, norm_mix=m_norm_mix, norm_ffn=m_norm_ffn, w_gate_up=m_w_gate_up, w_down=m_w_down,
             fox_w_in=m_fox_w_in, fox_b_f=m_fox_b_f, fox_q_gain=m_fox_q_gain, fox_k_gain=m_fox_k_gain,
             fox_w_out=m_fox_w_out, gla_w_in=m_gla_w_in, gla_w_alpha2=m_gla_w_alpha2, gla_b_alpha=m_gla_b_alpha,
             gla_o_gain=m_gla_o_gain, gla_w_out=m_gla_w_out, gdn_w_in=m_gdn_w_in, gdn_conv_w=m_gdn_conv_w,
             gdn_a_log=m_gdn_a_log, gdn_dt_bias=m_gdn_dt_bias, gdn_o_gain=m_gdn_o_gain, gdn_w_out=m_gdn_w_out)
    V = dict(meta_tokens=v_meta_tokens, norm_mix=v_norm_mix, norm_ffn=v_norm_ffn, w_gate_up=v_w_gate_up, w_down=v_w_down,
             fox_w_in=v_fox_w_in, fox_b_f=v_fox_b_f, fox_q_gain=v_fox_q_gain, fox_k_gain=v_fox_k_gain,
             fox_w_out=v_fox_w_out, gla_w_in=v_gla_w_in, gla_w_alpha2=v_gla_w_alpha2, gla_b_alpha=v_gla_b_alpha,
             gla_o_gain=v_gla_o_gain, gla_w_out=v_gla_w_out, gdn_w_in=v_gdn_w_in, gdn_conv_w=v_gdn_conv_w,
             gdn_a_log=v_gdn_a_log, gdn_dt_bias=v_gdn_dt_bias, gdn_o_gain=v_gdn_o_gain, gdn_w_out=v_gdn_w_out)
    seq = x.shape[1]
    lp = seq + ROW0
    chip = 2 * lax.axis_index("x") + lax.axis_index("y")

    packed = _pack([W[n] for n, _ in _BIG], D, _PACK_ROWS, BF16)
    gathered = _ag_weights(packed).reshape(4, -1)
    full = {}
    for (n, ax), seg in zip(_BIG, _unpack_cols(gathered, [W[n].shape for n, _ in _BIG])):
        full[n] = jnp.concatenate([seg[q] for q in range(4)], axis=ax)
    small = _pack([W[n] for n, _ in _SMALL_SHARDED], LANES, 8, F32)
    small_all = _gather8(small, reduce=False, name="gather_small").reshape(8, -1)
    for (n, ax), seg in zip(_SMALL_SHARDED, _unpack_cols(small_all, [W[n].shape for n, _ in _SMALL_SHARDED])):
        full[n] = jnp.concatenate([seg[2 * q] for q in range(4)], axis=ax)
    fox_in = _pad_cols(full["fox_w_in"], FOX_INP)
    gla_in = _pad_cols(full["gla_w_in"], GLA_INP)
    gdn_in = _pad_cols(full["gdn_w_in"], GDN_INP)
    w_alpha2, conv_w = full["gla_w_alpha2"][0], full["gdn_conv_w"][0]

    h = jnp.concatenate([jnp.zeros((META0, D), F32), full["meta_tokens"], x[0]], axis=0)
    saved = []
    for i in range(DEPTH):
        kind, j = i % 3, i // 3
        y = _rms_fwd(h, norm_mix[i], name=f"norm_mix{i}")
        if kind == 0:
            proj = _mm(y, fox_in[j], name=f"fox_in{j}")
            qn, kn, vb, cb, ct = _fox_prep(proj, fox_b_f[j], fox_q_gain[j], fox_k_gain[j])
            o, og, lse = _fox_attn_fwd(qn, kn, vb, cb, ct, proj)
            w_out, mix = full["fox_w_out"][j], (proj, qn, kn, vb, cb, ct, o, lse)
        elif kind == 1:
            proj = _mm(y, gla_in[j], name=f"gla_in{j}")
            o, og, states = _gla_fwd(proj, w_alpha2, gla_b_alpha[j], gla_o_gain[j])
            w_out, mix = full["gla_w_out"][j], (proj, o, states)
        else:
            proj = _mm(y, gdn_in[j], name=f"gdn_in{j}")
            o, og, states = _gdn_fwd(proj, conv_w, gdn_a_log[j], gdn_dt_bias[j], gdn_o_gain[j])
            w_out, mix = full["gdn_w_out"][j], (proj, o, states)
        hm = _mm(og, w_out, add=h, name=f"mix_out{i}")
        yf = _rms_fwd(hm, norm_ffn[i], name=f"norm_ffn{i}")
        gu = _mm(yf, full["w_gate_up"][i], name=f"ffn_up{i}")
        act = _swiglu_fwd(gu, name=f"swiglu{i}")
        hn = _mm(act, full["w_down"][i], add=hm, name=f"ffn_down{i}")
        saved.append((h, y, mix, og, w_out, hm, yf, gu, act))
        h = hn
    dh, loss_tile = _loss_head(h, loss_target[0])

    G = {n: [None] * W[n].shape[0] for n in _WEIGHTS if n != "meta_tokens"}
    for i in reversed(range(DEPTH)):
        kind, j = i % 3, i // 3
        h_in, y, mix, og, w_out, hm, yf, gu, act = saved[i]
        dact = _mm(dh, full["w_down"][i], tb=True, name=f"d_act{i}")
        G["w_down"][i] = _mm(act, dh, ta=True, out_dtype=BF16, name=f"d_w_down{i}")
        dgu = _swiglu_bwd(gu, dact, name=f"d_swiglu{i}")
        dyf = _mm(dgu, full["w_gate_up"][i], tb=True, name=f"d_yf{i}")
        G["w_gate_up"][i] = _mm(yf, dgu, ta=True, out_dtype=BF16, name=f"d_w_gate_up{i}")
        dhm, dnf = _rms_bwd(hm, norm_ffn[i], dyf, dh, name=f"d_norm_ffn{i}")
        G["norm_ffn"][i] = dnf[0]
        dog = _mm(dhm, w_out, tb=True, name=f"d_og{i}")
        dw_out = _mm(og, dhm, ta=True, out_dtype=BF16, name=f"d_w_out{i}")
        if kind == 0:
            proj, qn, kn, vb, cb, ct, o, lse = mix
            do, delta, dgate = _fox_gate_bwd(dog, o, proj)
            dqn, dkn, dv, dc = _fox_attn_bwd(qn, kn, vb, cb, ct, lse, do, delta)
            dproj, dqg, dkg, dbf = _fox_prep_bwd(proj, fox_b_f[j], fox_q_gain[j], fox_k_gain[j], dqn, dkn, dv, dgate, dc)
            G["fox_w_out"][j] = dw_out
            G["fox_q_gain"][j] = dqg.reshape(FOX_H, FOX_DH).sum(0)
            G["fox_k_gain"][j] = dkg.reshape(FOX_H, FOX_DH).sum(0)
            G["fox_b_f"][j] = dbf[0, :FOX_H]
            w_in, wname, n_in = fox_in[j], "fox_w_in", fox_w_in.shape[2] * 4
        elif kind == 1:
            proj, o, states = mix
            dproj, dwa, dba, dogain = _gla_bwd(proj, w_alpha2, gla_b_alpha[j], gla_o_gain[j], o, states, dog)
            G["gla_w_out"][j] = dw_out
            G["gla_w_alpha2"][j] = dwa[:GLA_RANK]
            G["gla_b_alpha"][j] = dba[0]
            G["gla_o_gain"][j] = dogain[0]
            w_in, wname, n_in = gla_in[j], "gla_w_in", gla_w_in.shape[2] * 4
        else:
            proj, o, states = mix
            dproj, dcw, dal, ddt, dogain = _gdn_bwd(proj, conv_w, gdn_a_log[j], gdn_dt_bias[j], gdn_o_gain[j], o, states, dog)
            G["gdn_w_out"][j] = dw_out
            G["gdn_conv_w"][j] = dcw[:4].reshape(4, 1, GDN_CONV)
            G["gdn_a_log"][j] = dal[0, :GDN_H]
            G["gdn_dt_bias"][j] = ddt[0, :GDN_H]
            G["gdn_o_gain"][j] = dogain[0]
            w_in, wname, n_in = gdn_in[j], "gdn_w_in", gdn_w_in.shape[2] * 4
        dy = _mm(dproj, w_in, tb=True, name=f"d_y{i}")
        G[wname][j] = _mm(y, dproj, ta=True, out_dtype=BF16, name=f"d_w_in{i}")[:, :n_in]
        dh, dnm = _rms_bwd(h_in, norm_mix[i], dy, dhm, name=f"d_norm_mix{i}")
        G["norm_mix"][i] = dnm[0]
    grad_x = dh[ROW0:][None]
    G = {n: jnp.stack(v) for n, v in G.items()}
    G["meta_tokens"] = dh[META0:ROW0]

    blocks = []
    for q in range(4):
        parts = []
        for n, ax in _BIG:
            sz = W[n].shape[ax]
            parts.append(lax.slice_in_dim(G[n], q * sz, (q + 1) * sz, axis=ax))
        blocks.append(_pack(parts, D, _PACK_ROWS, BF16))
    reduced = _reduce_scatter(jnp.stack(blocks)).reshape(-1)
    grads = dict(zip([n for n, _ in _BIG], _unpack(reduced, [W[n].shape for n, _ in _BIG])))
    small_names = [n for n, _ in _SMALL_SHARDED] + list(_REPLICATED)
    small_g = _pack([G[n] for n in small_names] + [loss_tile[0, 0:1]], LANES, 8, F32)
    small_sum = _gather8(small_g, reduce=True, name="allreduce_small").reshape(-1)
    small_shapes = [G[n].shape for n in small_names] + [(1,)]
    small_vals = _unpack(small_sum, small_shapes)
    loss = small_vals[-1][0]
    for n, val in zip(small_names, small_vals[:-1]):
        grads[n] = val
    for n, ax in _SMALL_SHARDED:
        sz = W[n].shape[ax]
        grads[n] = lax.dynamic_slice_in_dim(grads[n], chip * sz, sz, axis=ax)

    delta, new_m, new_v = {}, {}, {}
    for n, _ in _BIG:
        delta[n], new_m[n], new_v[n] = _adamw(W[n], grads[n], M[n], V[n], name=f"adamw_{n}")
    tiny = [n for n in _WEIGHTS if n not in dict(_BIG)]
    packs = [_pack([T[n] for n in tiny], LANES, 8, F32) for T in (W, grads, M, V)]
    outs = _adamw(*packs, name="adamw_small")
    shapes = [W[n].shape for n in tiny]
    for dst, o in zip((delta, new_m, new_v), outs):
        for n, val in zip(tiny, _unpack(o.reshape(-1), shapes)):
            dst[n] = val
    return (loss, grad_x, *[grads[n] for n in _WEIGHTS], *[delta[n] for n in _WEIGHTS],
            *[new_m[n] for n in _WEIGHTS], *[new_v[n] for n in _WEIGHTS])


def _unpack_cols(flat2, shapes):
    out, off = [], 0
    for s in shapes:
        n = 1
        for d in s:
            n *= d
        out.append(flat2[:, off:off + n].reshape((flat2.shape[0],) + tuple(s)))
        off += n
    return out
```

```python
import functools

import jax
import jax.numpy as jnp
from jax import lax
from jax.experimental import pallas as pl
from jax.experimental.pallas import tpu as pltpu

F32, BF16 = jnp.float32, jnp.bfloat16
D = 1024
N_META = 16
ROW0 = 128
META0 = ROW0 - N_META
EPS = 1e-6
LANES = 128
VMEM_LIMIT = 56 * 1024 * 1024

FOX_H, FOX_DH = 16, 64
FOX_INP = 4224
GLA_H, GLA_DK, GLA_DV, GLA_RANK = 4, 128, 256, 16
GLA_QK, GLA_V = 512, 1024
GLA_INP = 3200
GLA_NORM = 16.0
GDN_H, GDN_DK, GDN_DV = 8, 128, 128
GDN_CONV = 3072
GDN_INP = 4224
CHUNK = 64
D_FF = 2816
DEPTH = 4

ADAM_LR, ADAM_B1, ADAM_B2, ADAM_EPS, ADAM_WD, ADAM_STEP = 0.001, 0.9, 0.999, 1e-08, 0.01, 10

MESH = pl.DeviceIdType.MESH
ANY = pl.BlockSpec(memory_space=pl.ANY)
VM = pl.BlockSpec(memory_space=pltpu.VMEM)


def _params(sem=None, **kw):
    if sem is not None:
        kw["dimension_semantics"] = sem
    return pltpu.CompilerParams(vmem_limit_bytes=VMEM_LIMIT, **kw)


def _tile(n, cap, mult=LANES):
    best = None
    for t in range(mult, min(n, cap) + 1, mult):
        if n % t == 0:
            best = t
    return best if best is not None else n


def nn(a, b, **kw):
    return jnp.dot(a, b, preferred_element_type=F32, **kw)


def nt(a, b, **kw):
    return lax.dot_general(a, b, (((1,), (1,)), ((), ())), preferred_element_type=F32, **kw)


def tn(a, b, **kw):
    return lax.dot_general(a, b, (((0,), (0,)), ((), ())), preferred_element_type=F32, **kw)


def _split3(x):
    hi = x.astype(BF16)
    r = x - hi.astype(F32)
    mid = r.astype(BF16)
    lo = (r - mid.astype(F32)).astype(BF16)
    return hi, mid, lo


def _sel_l(sel, x):
    a, b, c = _split3(x)
    return nn(sel, a) + nn(sel, b) + nn(sel, c)


def _sel_r(x, sel):
    a, b, c = _split3(x)
    return nn(a, sel) + nn(b, sel) + nn(c, sel)


def _iota(shape, dim):
    return lax.broadcasted_iota(jnp.int32, shape, dim)


def _tri(n, upper=False, strict=False):
    i, j = _iota((n, n), 0), _iota((n, n), 1)
    if upper:
        m = (j > i) if strict else (j >= i)
    else:
        m = (j < i) if strict else (j <= i)
    return m


def _sigmoid(x):
    return 1.0 / (1.0 + jnp.exp(-x))


def _log_sigmoid(x):
    return jnp.minimum(x, 0.0) - jnp.log(1.0 + jnp.exp(-jnp.abs(x)))


def _softplus(x):
    return jnp.maximum(x, 0.0) + jnp.log(1.0 + jnp.exp(-jnp.abs(x)))


def _silu(x):
    return x * _sigmoid(x)


def _dsilu(x):
    s = _sigmoid(x)
    return s * (1.0 + x * (1.0 - s))


def _mm(a, b, *, ta=False, tb=False, add=None, out_dtype=F32, name):
    m, k = (a.shape[1], a.shape[0]) if ta else a.shape
    n = b.shape[0] if tb else b.shape[1]
    assert k == (b.shape[1] if tb else b.shape[0])
    tm, tn_, tk = _tile(m, 1024), _tile(n, 768), _tile(k, 1408)
    nk = k // tk

    def body(*refs):
        if add is None:
            a_ref, b_ref, o_ref, acc = refs
        else:
            a_ref, b_ref, r_ref, o_ref, acc = refs
        kk = pl.program_id(2)

        @pl.when(kk == 0)
        def _():
            acc[...] = jnp.zeros_like(acc)

        av, bv = a_ref[...].astype(BF16), b_ref[...].astype(BF16)
        dims = (((0,) if ta else (1,), (1,) if tb else (0,)), ((), ()))
        acc[...] += lax.dot_general(av, bv, dims, preferred_element_type=F32)

        @pl.when(kk == nk - 1)
        def _():
            r = acc[...]
            if add is not None:
                r = r + r_ref[...].astype(F32)
            o_ref[...] = r.astype(out_dtype)

    a_spec = pl.BlockSpec((tk, tm), lambda i, j, q: (q, i)) if ta else pl.BlockSpec((tm, tk), lambda i, j, q: (i, q))
    b_spec = pl.BlockSpec((tn_, tk), lambda i, j, q: (j, q)) if tb else pl.BlockSpec((tk, tn_), lambda i, j, q: (q, j))
    o_spec = pl.BlockSpec((tm, tn_), lambda i, j, q: (i, j))
    ins, specs = [a, b], [a_spec, b_spec]
    if add is not None:
        ins.append(add)
        specs.append(o_spec)
    return pl.pallas_call(
        body, name=name, grid=(m // tm, n // tn_, nk), in_specs=specs, out_specs=o_spec,
        out_shape=jax.ShapeDtypeStruct((m, n), out_dtype),
        scratch_shapes=[pltpu.VMEM((tm, tn_), F32)],
        compiler_params=_params(("parallel", "parallel", "arbitrary")),
    )(*ins)


def _rms_fwd(h, g, *, name):
    lp = h.shape[0]
    tr = _tile(lp, 512)

    def body(h_ref, g_ref, y_ref):
        x = h_ref[...]
        r = lax.rsqrt(jnp.mean(x * x, axis=-1, keepdims=True) + EPS)
        y_ref[...] = (x * r * g_ref[...]).astype(BF16)

    return pl.pallas_call(
        body, name=name, grid=(lp // tr,),
        in_specs=[pl.BlockSpec((tr, D), lambda i: (i, 0)), pl.BlockSpec((1, D), lambda i: (0, 0))],
        out_specs=pl.BlockSpec((tr, D), lambda i: (i, 0)),
        out_shape=jax.ShapeDtypeStruct((lp, D), BF16), compiler_params=_params(("parallel",)),
    )(h, g.reshape(1, D))


def _rms_bwd(h, g, dy, dres, *, name):
    lp = h.shape[0]
    tr = _tile(lp, 512)

    def body(h_ref, g_ref, dy_ref, dr_ref, dh_ref, dg_ref):
        @pl.when(pl.program_id(0) == 0)
        def _():
            dg_ref[...] = jnp.zeros_like(dg_ref)

        x, dyv = h_ref[...], dy_ref[...].astype(F32)
        r = lax.rsqrt(jnp.mean(x * x, axis=-1, keepdims=True) + EPS)
        u = dyv * g_ref[...]
        dx = r * u - x * (r * r * r) * jnp.mean(x * u, axis=-1, keepdims=True)
        dh_ref[...] = dr_ref[...] + dx
        dg_ref[...] += jnp.sum(dyv * x * r, axis=0, keepdims=True)

    return pl.pallas_call(
        body, name=name, grid=(lp // tr,),
        in_specs=[pl.BlockSpec((tr, D), lambda i: (i, 0)), pl.BlockSpec((1, D), lambda i: (0, 0)),
                  pl.BlockSpec((tr, D), lambda i: (i, 0)), pl.BlockSpec((tr, D), lambda i: (i, 0))],
        out_specs=[pl.BlockSpec((tr, D), lambda i: (i, 0)), pl.BlockSpec((1, D), lambda i: (0, 0))],
        out_shape=[jax.ShapeDtypeStruct((lp, D), F32), jax.ShapeDtypeStruct((1, D), F32)],
        compiler_params=_params(("arbitrary",)),
    )(h, g.reshape(1, D), dy, dres)


def _swiglu_fwd(gu, *, name):
    lp = gu.shape[0]
    tr, tc = _tile(lp, 512), _tile(D_FF, 1408)
    nc = D_FF // tc

    def body(g_ref, u_ref, a_ref):
        a_ref[...] = (_silu(g_ref[...]) * u_ref[...]).astype(BF16)

    return pl.pallas_call(
        body, name=name, grid=(lp // tr, nc),
        in_specs=[pl.BlockSpec((tr, tc), lambda i, j: (i, j)), pl.BlockSpec((tr, tc), lambda i, j: (i, j + nc))],
        out_specs=pl.BlockSpec((tr, tc), lambda i, j: (i, j)),
        out_shape=jax.ShapeDtypeStruct((lp, D_FF), BF16), compiler_params=_params(("parallel", "parallel")),
    )(gu, gu)


def _swiglu_bwd(gu, dact, *, name):
    lp = gu.shape[0]
    tr = _tile(lp, 128)

    def body(gu_ref, da_ref, o_ref):
        gv, uv, da = gu_ref[:, 0:D_FF], gu_ref[:, D_FF:2 * D_FF], da_ref[...]
        o_ref[:, 0:D_FF] = (da * uv * _dsilu(gv)).astype(BF16)
        o_ref[:, D_FF:2 * D_FF] = (da * _silu(gv)).astype(BF16)

    return pl.pallas_call(
        body, name=name, grid=(lp // tr,),
        in_specs=[pl.BlockSpec((tr, 2 * D_FF), lambda i: (i, 0)), pl.BlockSpec((tr, D_FF), lambda i: (i, 0))],
        out_specs=pl.BlockSpec((tr, 2 * D_FF), lambda i: (i, 0)),
        out_shape=jax.ShapeDtypeStruct((lp, 2 * D_FF), BF16), compiler_params=_params(("parallel",)),
    )(gu, dact)


def _loss_head(h, target):
    lp = h.shape[0]
    nb = lp // ROW0

    def body(h_ref, t_ref, dh_ref, l_ref):
        i = pl.program_id(0)

        @pl.when(i == 0)
        def _():
            l_ref[...] = jnp.zeros_like(l_ref)
            dh_ref[...] = jnp.zeros_like(dh_ref)

        @pl.when(i > 0)
        def _():
            err = h_ref[...] - t_ref[...]
            dh_ref[...] = err * (1.0 / D)
            l_ref[...] += jnp.sum(err * err) * (0.5 / D)

    return pl.pallas_call(
        body, name="loss_head", grid=(nb,),
        in_specs=[pl.BlockSpec((ROW0, D), lambda i: (i, 0)), pl.BlockSpec((ROW0, D), lambda i: (jnp.maximum(i - 1, 0), 0))],
        out_specs=[pl.BlockSpec((ROW0, D), lambda i: (i, 0)), pl.BlockSpec((8, LANES), lambda i: (0, 0))],
        out_shape=[jax.ShapeDtypeStruct((lp, D), F32), jax.ShapeDtypeStruct((8, LANES), F32)],
        compiler_params=_params(("arbitrary",)),
    )(h, target)


def _adamw(w, g, m, v, *, name):
    shape = w.shape
    c = shape[-1]
    r = w.size // c
    w2, g2, m2, v2 = (t.reshape(r, c) for t in (w, g, m, v))
    tr = _tile(r, max(8, (1 << 19) // c), 8)

    def body(w_ref, g_ref, m_ref, v_ref, d_ref, nm_ref, nv_ref):
        gv = g_ref[...]
        nm = ADAM_B1 * m_ref[...] + (1.0 - ADAM_B1) * gv
        nv = ADAM_B2 * v_ref[...] + (1.0 - ADAM_B2) * (gv * gv)
        m_hat = nm / (1.0 - ADAM_B1 ** ADAM_STEP)
        v_hat = nv / (1.0 - ADAM_B2 ** ADAM_STEP)
        d_ref[...] = -ADAM_LR * (m_hat / (jnp.sqrt(v_hat) + ADAM_EPS) + ADAM_WD * w_ref[...])
        nm_ref[...] = nm
        nv_ref[...] = nv

    spec = pl.BlockSpec((tr, c), lambda i: (i, 0))
    outs = pl.pallas_call(
        body, name=name, grid=(r // tr,), in_specs=[spec] * 4, out_specs=[spec] * 3,
        out_shape=[jax.ShapeDtypeStruct((r, c), F32)] * 3, compiler_params=_params(("parallel",)),
    )(w2, g2, m2, v2)
    return tuple(o.reshape(shape) for o in outs)


FOX_AUG = FOX_H * LANES
L_C = 64
L_K = 67
L_LSE = 70
PAD_KEY = -30000.0
FOX_TQ = 384


def _head_sel(n_heads, width, lanes=LANES):
    r, c = _iota((n_heads * width, lanes), 0), _iota((n_heads * width, lanes), 1)
    down = (r // width == c).astype(BF16)
    r2, c2 = _iota((lanes, n_heads * width), 0), _iota((lanes, n_heads * width), 1)
    up = (c2 // width == r2).astype(BF16)
    return down, up


def _place(lane0):
    r, c = _iota((LANES, FOX_AUG), 0), _iota((LANES, FOX_AUG), 1)
    return [((c // LANES == r) & (c % LANES == lane0 + m)).astype(BF16) for m in range(3)]


def _placed(x, lane0):
    pcs = _split3(x)
    mats = _place(lane0)
    return nn(pcs[0], mats[0]) + nn(pcs[1], mats[1]) + nn(pcs[2], mats[2])


def _ones_at(rows, lanes):
    c = _iota((rows, FOX_AUG), 1) % LANES
    m = c == lanes[0]
    for l in lanes[1:]:
        m = m | (c == l)
    return m.astype(F32)


def _spread(x, extras, out_ref):
    rows = x.shape[0]
    left = _iota((rows, LANES), 1) < FOX_DH
    for p in range(FOX_H // 2):
        slab = x[:, p * LANES:(p + 1) * LANES]
        a = jnp.where(left, slab, extras[:, 2 * p * LANES:(2 * p + 1) * LANES])
        b = jnp.where(left, pltpu.roll(slab, FOX_DH, 1), extras[:, (2 * p + 1) * LANES:(2 * p + 2) * LANES])
        out_ref[:, 2 * p * LANES:(2 * p + 1) * LANES] = a.astype(BF16)
        out_ref[:, (2 * p + 1) * LANES:(2 * p + 2) * LANES] = b.astype(BF16)


def _fox_prep(proj, b_f, q_gain, k_gain):
    lp = proj.shape[0]
    nb = lp // LANES

    def body(p_ref, bf_ref, qg_ref, kg_ref, q_ref, k_ref, v_ref, carry):
        i = pl.program_id(0)

        @pl.when(i == 0)
        def _():
            carry[...] = jnp.zeros_like(carry)

        down, up = _head_sel(FOX_H, FOX_DH)

        def normed(x, gain):
            ms = _sel_r(x * x, down) * (1.0 / FOX_DH)
            r = _sel_r(lax.rsqrt(ms + EPS), up)
            return x * r * gain

        lane = _iota((LANES, LANES), 1)
        lf = jnp.where(lane < FOX_H, _log_sigmoid(p_ref[:, 4 * D:4 * D + LANES] + bf_ref[...]), 0.0)
        c = _sel_l(_tri(LANES).astype(BF16), lf) + carry[0:1, :]
        carry[...] = jnp.broadcast_to(c[LANES - 1:LANES, :], carry.shape)
        q_extra = _placed(c, L_C) + _ones_at(LANES, (L_K, L_K + 1, L_K + 2))
        row = i * LANES + _iota((LANES, FOX_AUG), 0)
        lane_a = _iota((LANES, FOX_AUG), 1) % LANES
        k_extra = -_placed(c, L_K) + _ones_at(LANES, (L_C, L_C + 1, L_C + 2, L_LSE, L_LSE + 1, L_LSE + 2))
        pad_val = jnp.where(lane_a == L_K, PAD_KEY, 0.0)
        k_extra = jnp.where((row < META0) & (lane_a >= L_K) & (lane_a < L_K + 3), pad_val, k_extra)
        v_extra = _ones_at(LANES, (L_C, L_C + 1, L_C + 2))
        _spread(normed(p_ref[:, 0:D], qg_ref[...]) * (FOX_DH ** -0.5), q_extra, q_ref)
        _spread(normed(p_ref[:, D:2 * D], kg_ref[...]), k_extra, k_ref)
        _spread(p_ref[:, 2 * D:3 * D], v_extra, v_ref)

    row = pl.BlockSpec((1, D), lambda i: (0, 0))
    aug = pl.BlockSpec((LANES, FOX_AUG), lambda i: (i, 0))
    return pl.pallas_call(
        body, name="fox_prep", grid=(nb,),
        in_specs=[pl.BlockSpec((LANES, FOX_INP), lambda i: (i, 0)), pl.BlockSpec((1, LANES), lambda i: (0, 0)), row, row],
        out_specs=[aug] * 3, out_shape=[jax.ShapeDtypeStruct((lp, FOX_AUG), BF16)] * 3,
        scratch_shapes=[pltpu.VMEM((8, LANES), F32)],
        compiler_params=_params(("arbitrary",)),
    )(proj, jnp.pad(b_f, (0, LANES - FOX_H)).reshape(1, LANES), jnp.tile(q_gain, FOX_H).reshape(1, D),
      jnp.tile(k_gain, FOX_H).reshape(1, D))


def _fox_attn_fwd(qa, ka, va, proj):
    lp = qa.shape[0]
    tq = _tile(lp, FOX_TQ)
    nq = lp // tq

    def body(q_ref, k_ref, v_ref, gate_ref, o_ref, og_ref, lse_ref):
        i = pl.program_id(1)
        causal = _iota((tq, tq), 1) <= _iota((tq, tq), 0)
        qs = [q_ref[:, hh * LANES:(hh + 1) * LANES] for hh in range(2)]

        def block(j, carry, diag):
            off = pl.multiple_of(j * tq, tq)
            out = []
            for hh in range(2):
                m, acc = carry[hh]
                k = k_ref[pl.ds(off, tq), hh * LANES:(hh + 1) * LANES]
                v = v_ref[pl.ds(off, tq), hh * LANES:(hh + 1) * LANES]
                s = nt(qs[hh], k)
                if diag:
                    s = jnp.where(causal, s, -1e30)
                m2 = jnp.maximum(m, jnp.max(s, axis=-1, keepdims=True))
                p = jnp.exp(s - m2)
                p_hi = p.astype(BF16)
                p_lo = (p - p_hi.astype(F32)).astype(BF16)
                out.append((m2, jnp.exp(m - m2) * acc + nn(p_hi, v) + nn(p_lo, v)))
            return tuple(out)

        init = tuple((jnp.full((tq, 1), -1e30, F32), jnp.zeros((tq, LANES), F32)) for _ in range(2))
        carry = lax.fori_loop(0, i, lambda j, c: block(j, c, False), init)
        carry = block(i, carry, True)
        outs, lses = [], []
        for hh in range(2):
            m, acc = carry[hh]
            l = acc[:, L_C:L_C + 1]
            outs.append(acc / l)
            lses.append(jnp.broadcast_to(m + jnp.log(l), (tq, LANES)))
        left = _iota((tq, LANES), 1) < FOX_DH
        o = jnp.where(left, outs[0], pltpu.roll(outs[1], FOX_DH, 1))
        o_ref[...] = o
        og_ref[...] = (o * _sigmoid(gate_ref[...])).astype(BF16)
        lse_ref[...] = jnp.where(left, lses[0], lses[1])

    qspec = pl.BlockSpec((tq, 2 * LANES), lambda p, i: (i, p))
    kspec = pl.BlockSpec((lp, 2 * LANES), lambda p, i: (0, p))
    ospec = pl.BlockSpec((tq, LANES), lambda p, i: (i, p))
    return pl.pallas_call(
        body, name="fox_attn_fwd", grid=(FOX_H // 2, nq),
        in_specs=[qspec, kspec, kspec, pl.BlockSpec((tq, LANES), lambda p, i: (i, 3 * D // LANES + p))],
        out_specs=[ospec] * 3,
        out_shape=[jax.ShapeDtypeStruct((lp, D), F32), jax.ShapeDtypeStruct((lp, D), BF16), jax.ShapeDtypeStruct((lp, D), F32)],
        compiler_params=_params(("parallel", "arbitrary")),
    )(qa, ka, va, proj)


def _fox_gate_bwd(dog, o, proj, lse, qa):
    lp = o.shape[0]
    tr = LANES

    def body(d_ref, o_ref, g_ref, lse_ref, q_ref, do_ref, q2_ref, dgate_ref):
        down, _ = _head_sel(FOX_H, FOX_DH)
        sg = _sigmoid(g_ref[...])
        dv, ov = d_ref[...], o_ref[...]
        do = (dv * sg).astype(BF16).astype(F32)
        dgate_ref[...] = dv * ov * sg * (1.0 - sg)
        delta = _sel_r(do * ov, down)
        _spread(do, -_placed(delta, L_C), do_ref)
        r_, c_ = _iota((D, LANES), 0), _iota((D, LANES), 1)
        lse_c = _sel_r(lse_ref[...], (r_ == c_ * FOX_DH).astype(BF16))
        q2_ref[...] = (q_ref[...].astype(F32) - _placed(lse_c, L_LSE)).astype(BF16)

    spec = pl.BlockSpec((tr, D), lambda i: (i, 0))
    aug = pl.BlockSpec((tr, FOX_AUG), lambda i: (i, 0))
    return pl.pallas_call(
        body, name="fox_gate_bwd", grid=(lp // tr,),
        in_specs=[spec, spec, pl.BlockSpec((tr, D), lambda i: (i, 3)), spec, aug], out_specs=[aug, aug, spec],
        out_shape=[jax.ShapeDtypeStruct((lp, FOX_AUG), BF16), jax.ShapeDtypeStruct((lp, FOX_AUG), BF16),
                   jax.ShapeDtypeStruct((lp, D), F32)],
        compiler_params=_params(("parallel",)),
    )(dog, o, proj, lse, qa)


def _fox_attn_bwd(q2, ka, va, doa):
    lp = q2.shape[0]
    t = _tile(lp, FOX_TQ)
    nb = lp // t

    def body(q_ref, k_ref, v_ref, do_ref, dq_ref, dk_ref, dv_ref, dc_ref, dq_acc, dk_acc, dv_acc, dc_acc):
        j = pl.program_id(1)

        @pl.when(j == 0)
        def _():
            dq_acc[...] = jnp.zeros_like(dq_acc)

        causal = _iota((t, t), 1) <= _iota((t, t), 0)
        for hh in range(2):
            k = k_ref[:, hh * LANES:(hh + 1) * LANES]
            v = v_ref[:, hh * LANES:(hh + 1) * LANES]
            dk_acc[...] = jnp.zeros_like(dk_acc)
            dv_acc[...] = jnp.zeros_like(dv_acc)
            dc_acc[...] = jnp.zeros_like(dc_acc)

            def block(i, diag):
                off = pl.multiple_of(i * t, t)
                q = q_ref[pl.ds(off, t), hh * LANES:(hh + 1) * LANES]
                do = do_ref[pl.ds(off, t), hh * LANES:(hh + 1) * LANES]
                s = nt(q, k)
                if diag:
                    s = jnp.where(causal, s, -1e30)
                p = jnp.exp(s)
                ds = p * nt(do, v)
                dc_acc[...] += jnp.sum(ds, axis=0, keepdims=True)
                dsb = ds.astype(BF16)
                dv_acc[...] += tn(p.astype(BF16), do)
                dk_acc[...] += tn(dsb, q)
                dq_acc[hh, pl.ds(off, t), :] += nn(dsb, k)

            block(j, True)

            def step(i, c):
                block(i, False)
                return c

            lax.fori_loop(j + 1, nb, step, 0)
            left = _iota((t, LANES), 1) < FOX_DH
            if hh == 0:
                dk_ref[...] = dk_acc[...]
                dv_ref[...] = dv_acc[...]
            else:
                dk_ref[...] = jnp.where(left, dk_ref[...], pltpu.roll(dk_acc[...], FOX_DH, 1))
                dv_ref[...] = jnp.where(left, dv_ref[...], pltpu.roll(dv_acc[...], FOX_DH, 1))
            dc_ref[hh] = jnp.broadcast_to(-dc_acc[...], (8, t))

        @pl.when(j == nb - 1)
        def _():
            left = _iota((lp, LANES), 1) < FOX_DH
            dq_ref[...] = jnp.where(left, dq_acc[0], pltpu.roll(dq_acc[1], FOX_DH, 1))

    full = pl.BlockSpec((lp, 2 * LANES), lambda p, j: (0, p))
    kblk = pl.BlockSpec((t, 2 * LANES), lambda p, j: (j, p))
    oblk = pl.BlockSpec((t, LANES), lambda p, j: (j, p))
    return pl.pallas_call(
        body, name="fox_attn_bwd", grid=(FOX_H // 2, nb),
        in_specs=[full, kblk, kblk, full],
        out_specs=[pl.BlockSpec((lp, LANES), lambda p, j: (0, p)), oblk, oblk, pl.BlockSpec((2, 8, t), lambda p, j: (p, 0, j))],
        out_shape=[jax.ShapeDtypeStruct((lp, D), F32)] * 3 + [jax.ShapeDtypeStruct((FOX_H, 8, lp), F32)],
        scratch_shapes=[pltpu.VMEM((2, lp, LANES), F32), pltpu.VMEM((t, LANES), F32), pltpu.VMEM((t, LANES), F32),
                        pltpu.VMEM((1, t), F32)],
        compiler_params=_params(("parallel", "arbitrary")),
    )(q2, ka, va, doa)


def _fox_prep_bwd(proj, b_f, q_gain, k_gain, dqn, dkn, dv, dgate, dct):
    lp = proj.shape[0]
    nb = lp // LANES

    def body(p_ref, bf_ref, qg_ref, kg_ref, dq_ref, dk_ref, dv_ref, dg_ref, dc_ref,
             dp_ref, dqg_ref, dkg_ref, dbf_ref, carry):
        i = pl.program_id(0)

        @pl.when(i == 0)
        def _():
            carry[...] = jnp.zeros_like(carry)
            dqg_ref[...] = jnp.zeros_like(dqg_ref)
            dkg_ref[...] = jnp.zeros_like(dkg_ref)
            dbf_ref[...] = jnp.zeros_like(dbf_ref)

        down, up = _head_sel(FOX_H, FOX_DH)

        def norm_bwd(x, gain, dy, scale, dgain_ref):
            ms = _sel_r(x * x, down) * (1.0 / FOX_DH)
            r = _sel_r(lax.rsqrt(ms + EPS), up)
            u = dy * gain * scale
            mean_xu = _sel_r(_sel_r(x * u, down) * (1.0 / FOX_DH), up)
            dgain_ref[...] += jnp.sum(dy * scale * x * r, axis=0, keepdims=True)
            return r * u - x * (r * r * r) * mean_xu

        dp_ref[:, 0:D] = norm_bwd(p_ref[:, 0:D], qg_ref[...], dq_ref[...], FOX_DH ** -0.5, dqg_ref).astype(BF16)
        dp_ref[:, D:2 * D] = norm_bwd(p_ref[:, D:2 * D], kg_ref[...], dk_ref[...], 1.0, dkg_ref).astype(BF16)
        dp_ref[:, 2 * D:3 * D] = dv_ref[...].astype(BF16)
        dp_ref[:, 3 * D:4 * D] = dg_ref[...].astype(BF16)
        rows = jnp.concatenate([dc_ref[h, 0:1, :] for h in range(FOX_H)] + [jnp.zeros((LANES - FOX_H, LANES), F32)], axis=0)
        dlf = _sel_l(_tri(LANES, upper=True).astype(BF16), rows.T) + carry[0:1, :]
        carry[...] = jnp.broadcast_to(dlf[0:1, :], carry.shape)
        lane = _iota((LANES, LANES), 1)
        z = p_ref[:, 4 * D:4 * D + LANES] + bf_ref[...]
        df = jnp.where(lane < FOX_H, dlf * _sigmoid(-z), 0.0)
        dp_ref[:, 4 * D:4 * D + LANES] = df.astype(BF16)
        dbf_ref[...] += jnp.sum(df, axis=0, keepdims=True)

    rev = lambda i: (nb - 1 - i, 0)
    blk = pl.BlockSpec((LANES, D), rev)
    row = pl.BlockSpec((1, D), lambda i: (0, 0))
    row128 = pl.BlockSpec((1, LANES), lambda i: (0, 0))
    return pl.pallas_call(
        body, name="fox_prep_bwd", grid=(nb,),
        in_specs=[pl.BlockSpec((LANES, FOX_INP), rev), row128, row, row, blk, blk, blk, blk,
                  pl.BlockSpec((FOX_H, 8, LANES), lambda i: (0, 0, nb - 1 - i))],
        out_specs=[pl.BlockSpec((LANES, FOX_INP), rev), row, row, row128],
        out_shape=[jax.ShapeDtypeStruct((lp, FOX_INP), BF16), jax.ShapeDtypeStruct((1, D), F32),
                   jax.ShapeDtypeStruct((1, D), F32), jax.ShapeDtypeStruct((1, LANES), F32)],
        scratch_shapes=[pltpu.VMEM((8, LANES), F32)],
        compiler_params=_params(("arbitrary",)),
    )(proj, jnp.pad(b_f, (0, LANES - FOX_H)).reshape(1, LANES), jnp.tile(q_gain, FOX_H).reshape(1, D),
      jnp.tile(k_gain, FOX_H).reshape(1, D), dqn, dkn, dv, dgate, dct)


def _gla_gates(p_ref, wa_ref, ba_ref):
    a_lr = p_ref[:, 3072:3072 + LANES]
    z = nn(a_lr.astype(BF16), wa_ref[...].astype(BF16)) + ba_ref[...]
    g = _log_sigmoid(z) * (1.0 / GLA_NORM)
    b = _sel_l(_tri(CHUNK).astype(BF16), g)
    return a_lr, z, b


def _gla_head_fwd(q, k, v, b, st0):
    eb = jnp.exp(b)
    bl = b[CHUNK - 1:CHUNK, :]
    qe, ke, kd = q * eb, k * jnp.exp(-b), k * jnp.exp(bl - b)
    a = jnp.where(_tri(CHUNK), nt(qe, ke), 0.0)
    o = nn(a, v) + nt(qe, st0)
    st1 = st0 * jnp.exp(bl) + tn(v, kd)
    return o, st1, (qe, ke, kd, a, bl)


def _gla_fwd(proj, w_alpha2, b_alpha, o_gain):
    lp = proj.shape[0]
    nc = lp // CHUNK

    def body(p_ref, wa_ref, ba_ref, og_ref, o_ref, y_ref, s_ref, st):
        @pl.when(pl.program_id(0) == 0)
        def _():
            st[...] = jnp.zeros_like(st)

        _, _, b = _gla_gates(p_ref, wa_ref, ba_ref)
        for h in range(GLA_H):
            q = p_ref[:, h * GLA_DK:(h + 1) * GLA_DK] * (GLA_DK ** -0.5)
            k = p_ref[:, GLA_QK + h * GLA_DK:GLA_QK + (h + 1) * GLA_DK]
            v = p_ref[:, 2 * GLA_QK + h * GLA_DV:2 * GLA_QK + (h + 1) * GLA_DV]
            r = p_ref[:, 2 * GLA_QK + GLA_V + h * GLA_DV:2 * GLA_QK + GLA_V + (h + 1) * GLA_DV]
            st0 = st[h]
            s_ref[0, h] = st0
            o, st1, _ = _gla_head_fwd(q, k, v, b[:, h * GLA_DK:(h + 1) * GLA_DK], st0)
            st[h] = st1
            o_ref[:, h * GLA_DV:(h + 1) * GLA_DV] = o
            rs = lax.rsqrt(jnp.mean(o * o, axis=-1, keepdims=True) + EPS)
            y_ref[:, h * GLA_DV:(h + 1) * GLA_DV] = (o * rs * og_ref[...] * _silu(r)).astype(BF16)

    blk = pl.BlockSpec((CHUNK, D), lambda i: (i, 0))
    return pl.pallas_call(
        body, name="gla_fwd", grid=(nc,),
        in_specs=[pl.BlockSpec((CHUNK, GLA_INP), lambda i: (i, 0)), pl.BlockSpec((LANES, GLA_QK), lambda i: (0, 0)),
                  pl.BlockSpec((1, GLA_QK), lambda i: (0, 0)), pl.BlockSpec((1, GLA_DV), lambda i: (0, 0))],
        out_specs=[blk, blk, pl.BlockSpec((1, GLA_H, GLA_DV, GLA_DK), lambda i: (i, 0, 0, 0))],
        out_shape=[jax.ShapeDtypeStruct((lp, D), F32), jax.ShapeDtypeStruct((lp, D), BF16),
                   jax.ShapeDtypeStruct((nc, GLA_H, GLA_DV, GLA_DK), F32)],
        scratch_shapes=[pltpu.VMEM((GLA_H, GLA_DV, GLA_DK), F32)],
        compiler_params=_params(("arbitrary",)),
    )(proj, jnp.pad(w_alpha2, ((0, LANES - GLA_RANK), (0, 0))), b_alpha.reshape(1, GLA_QK), o_gain.reshape(1, GLA_DV))


def _gla_bwd(proj, w_alpha2, b_alpha, o_gain, o, states, dy):
    lp = proj.shape[0]
    nc = lp // CHUNK

    def body(p_ref, wa_ref, ba_ref, og_ref, o_ref, s_ref, dy_ref, dp_ref, dwa_ref, dba_ref, dog_ref, dst):
        @pl.when(pl.program_id(0) == 0)
        def _():
            dst[...] = jnp.zeros_like(dst)
            dwa_ref[...] = jnp.zeros_like(dwa_ref)
            dba_ref[...] = jnp.zeros_like(dba_ref)
            dog_ref[...] = jnp.zeros_like(dog_ref)

        a_lr, z, b_all = _gla_gates(p_ref, wa_ref, ba_ref)
        last_row = _iota((CHUNK, GLA_DK), 0) == CHUNK - 1
        rev = _tri(CHUNK, upper=True).astype(BF16)
        dg_parts = []
        for h in range(GLA_H):
            scale = GLA_DK ** -0.5
            q = p_ref[:, h * GLA_DK:(h + 1) * GLA_DK] * scale
            k = p_ref[:, GLA_QK + h * GLA_DK:GLA_QK + (h + 1) * GLA_DK]
            v = p_ref[:, 2 * GLA_QK + h * GLA_DV:2 * GLA_QK + (h + 1) * GLA_DV]
            r = p_ref[:, 2 * GLA_QK + GLA_V + h * GLA_DV:2 * GLA_QK + GLA_V + (h + 1) * GLA_DV]
            b = b_all[:, h * GLA_DK:(h + 1) * GLA_DK]
            st0 = s_ref[0, h]
            dst1 = dst[h]
            ov = o_ref[:, h * GLA_DV:(h + 1) * GLA_DV]
            dyv = dy_ref[:, h * GLA_DV:(h + 1) * GLA_DV]
            rs = lax.rsqrt(jnp.mean(ov * ov, axis=-1, keepdims=True) + EPS)
            on = ov * rs
            dr = dyv * on * og_ref[...] * _dsilu(r)
            don = dyv * _silu(r)
            dog_ref[...] += jnp.sum(don * on, axis=0, keepdims=True)
            u = don * og_ref[...]
            do = rs * u - ov * (rs * rs * rs) * jnp.mean(ov * u, axis=-1, keepdims=True)
            eb = jnp.exp(b)
            _, _, (qe, ke, kd, a, bl) = _gla_head_fwd(q, k, v, b, st0)
            da = jnp.where(_tri(CHUNK), nt(do, v), 0.0)
            dkd = nn(v, dst1)
            dvv = tn(a, do) + nt(kd, dst1)
            dqe = nn(da, ke) + nn(do, st0)
            dke = tn(da, qe)
            ebl = jnp.exp(bl)
            dst[h] = dst1 * ebl + tn(do, qe)
            db = dqe * qe - dke * ke - dkd * kd
            db_last = jnp.sum(dkd * kd, axis=0, keepdims=True) + jnp.sum(dst1 * st0, axis=0, keepdims=True) * ebl
            db = db + jnp.where(last_row, db_last, 0.0)
            dg_parts.append(_sel_l(rev, db))
            dp_ref[:, h * GLA_DK:(h + 1) * GLA_DK] = (dqe * eb * scale).astype(BF16)
            dp_ref[:, GLA_QK + h * GLA_DK:GLA_QK + (h + 1) * GLA_DK] = (dke * jnp.exp(-b) + dkd * jnp.exp(bl - b)).astype(BF16)
            dp_ref[:, 2 * GLA_QK + h * GLA_DV:2 * GLA_QK + (h + 1) * GLA_DV] = dvv.astype(BF16)
            dp_ref[:, 2 * GLA_QK + GLA_V + h * GLA_DV:2 * GLA_QK + GLA_V + (h + 1) * GLA_DV] = dr.astype(BF16)
        dg = jnp.concatenate(dg_parts, axis=1)
        dz = dg * (1.0 / GLA_NORM) * _sigmoid(-z)
        dzb = dz.astype(BF16)
        dp_ref[:, 3072:3072 + LANES] = nt(dzb, wa_ref[...].astype(BF16)).astype(BF16)
        dwa_ref[...] += tn(a_lr.astype(BF16), dzb)
        dba_ref[...] += jnp.sum(dz, axis=0, keepdims=True)

    rv = lambda i: (nc - 1 - i, 0)
    blk = pl.BlockSpec((CHUNK, D), rv)
    fixed = lambda r, c: pl.BlockSpec((r, c), lambda i: (0, 0))
    return pl.pallas_call(
        body, name="gla_bwd", grid=(nc,),
        in_specs=[pl.BlockSpec((CHUNK, GLA_INP), rv), fixed(LANES, GLA_QK), fixed(1, GLA_QK), fixed(1, GLA_DV), blk,
                  pl.BlockSpec((1, GLA_H, GLA_DV, GLA_DK), lambda i: (nc - 1 - i, 0, 0, 0)), blk],
        out_specs=[pl.BlockSpec((CHUNK, GLA_INP), rv), fixed(LANES, GLA_QK), fixed(1, GLA_QK), fixed(1, GLA_DV)],
        out_shape=[jax.ShapeDtypeStruct((lp, GLA_INP), BF16), jax.ShapeDtypeStruct((LANES, GLA_QK), F32),
                   jax.ShapeDtypeStruct((1, GLA_QK), F32), jax.ShapeDtypeStruct((1, GLA_DV), F32)],
        scratch_shapes=[pltpu.VMEM((GLA_H, GLA_DV, GLA_DK), F32)],
        compiler_params=_params(("arbitrary",)),
    )(proj, jnp.pad(w_alpha2, ((0, LANES - GLA_RANK), (0, 0))), b_alpha.reshape(1, GLA_QK), o_gain.reshape(1, GLA_DV),
      o, states, dy)


HI = lax.Precision.HIGHEST


def _gdn_pre(prev_ref, p_ref, cw_ref, al_ref, dt_ref):
    xc = jnp.concatenate([prev_ref[:, 0:GDN_CONV], p_ref[:, 0:GDN_CONV]], axis=0)
    shifted = [pltpu.roll(xc, 3 - j, 0)[CHUNK:, :] if j < 3 else xc[CHUNK:, :] for j in range(4)]
    conv = sum(shifted[j] * cw_ref[j:j + 1, :] for j in range(4))
    act = _silu(conv)
    slab = p_ref[:, 4096:4096 + LANES]
    lane = _iota((CHUNK, LANES), 1)
    zs = slab + dt_ref[...]
    g = jnp.where(lane < GDN_H, -jnp.exp(al_ref[...]) * _softplus(zs), 0.0)
    bs = _sel_l(_tri(CHUNK).astype(BF16), g)
    beta = _sigmoid(slab)
    return shifted, conv, act, slab, zs, g, bs, beta


def _l2n(x):
    r = lax.rsqrt(jnp.sum(x * x, axis=-1, keepdims=True) + EPS)
    return x * r, r


def _gdn_head_fwd(q, k, v, beta, bcol, brow, s0):
    ii, jj = _iota((CHUNK, CHUNK), 0), _iota((CHUNK, CHUNK), 1)
    diff = bcol - brow
    dm = jnp.where(ii >= jj, jnp.exp(jnp.where(ii >= jj, diff, 0.0)), 0.0)
    dstrict = jnp.where(ii > jj, dm, 0.0)
    eb = jnp.exp(bcol)
    bl = bcol[CHUNK - 1:CHUNK, :]
    kb, vb = k * beta, v * beta
    nmat = nt(kb, k) * dstrict
    eye = (ii == jj).astype(F32)
    x = eye - nmat
    pw = nn(nmat, nmat, precision=HI)
    for it in range(5):
        x = x + nn(x, pw, precision=HI)
        if it < 4:
            pw = nn(pw, pw, precision=HI)
    kbe = kb * eb
    u, w = nn(x, vb, precision=HI), nn(x, kbe, precision=HI)
    vn = u - nn(w, s0)
    pm = nt(q, k) * dm
    qe = q * eb
    o = nn(pm, vn) + nn(qe, s0)
    kd = k * jnp.exp(bl - bcol)
    s1 = s0 * jnp.exp(bl) + tn(kd, vn)
    return o, s1, (dm, dstrict, eb, bl, kb, vb, nmat, x, kbe, u, w, vn, pm, qe, kd)


def _gdn_heads(act, beta_slab, bs, h):
    qa = act[:, h * GDN_DK:(h + 1) * GDN_DK]
    ka = act[:, GDN_H * GDN_DK + h * GDN_DK:GDN_H * GDN_DK + (h + 1) * GDN_DK]
    v = act[:, 2 * GDN_H * GDN_DK + h * GDN_DV:2 * GDN_H * GDN_DK + (h + 1) * GDN_DV]
    return qa, ka, v, beta_slab[:, GDN_H + h:GDN_H + h + 1], bs[:, h:h + 1]


def _gdn_fwd(proj, conv_w, a_log, dt_bias, o_gain):
    lp = proj.shape[0]
    nc = lp // CHUNK

    def body(prev_ref, p_ref, cw_ref, al_ref, dt_ref, og_ref, o_ref, y_ref, s_ref, st):
        @pl.when(pl.program_id(0) == 0)
        def _():
            st[...] = jnp.zeros_like(st)

        _, _, act, _, _, _, bs, beta = _gdn_pre(prev_ref, p_ref, cw_ref, al_ref, dt_ref)
        bst = bs.T
        for h in range(GDN_H):
            qa, ka, v, bet, bcol = _gdn_heads(act, beta, bs, h)
            q = _l2n(qa)[0] * (GDN_DK ** -0.5)
            k = _l2n(ka)[0]
            s0 = st[h]
            s_ref[0, h] = s0
            o, s1, _ = _gdn_head_fwd(q, k, v, bet, bcol, bst[h:h + 1, :], s0)
            st[h] = s1
            o_ref[:, h * GDN_DV:(h + 1) * GDN_DV] = o
            rs = lax.rsqrt(jnp.mean(o * o, axis=-1, keepdims=True) + EPS)
            gate = p_ref[:, GDN_CONV + h * GDN_DV:GDN_CONV + (h + 1) * GDN_DV]
            y_ref[:, h * GDN_DV:(h + 1) * GDN_DV] = (o * rs * og_ref[...] * _silu(gate)).astype(BF16)

    blk = pl.BlockSpec((CHUNK, D), lambda i: (i, 0))
    fixed = lambda r, c: pl.BlockSpec((r, c), lambda i: (0, 0))
    return pl.pallas_call(
        body, name="gdn_fwd", grid=(nc,),
        in_specs=[pl.BlockSpec((CHUNK, GDN_INP), lambda i: (jnp.maximum(i - 1, 0), 0)),
                  pl.BlockSpec((CHUNK, GDN_INP), lambda i: (i, 0)), fixed(8, GDN_CONV), fixed(1, LANES), fixed(1, LANES),
                  fixed(1, GDN_DV)],
        out_specs=[blk, blk, pl.BlockSpec((1, GDN_H, GDN_DK, GDN_DV), lambda i: (i, 0, 0, 0))],
        out_shape=[jax.ShapeDtypeStruct((lp, D), F32), jax.ShapeDtypeStruct((lp, D), BF16),
                   jax.ShapeDtypeStruct((nc, GDN_H, GDN_DK, GDN_DV), F32)],
        scratch_shapes=[pltpu.VMEM((GDN_H, GDN_DK, GDN_DV), F32)],
        compiler_params=_params(("arbitrary",)),
    )(proj, proj, jnp.pad(conv_w.reshape(4, GDN_CONV), ((0, 4), (0, 0))), jnp.pad(a_log, (0, LANES - GDN_H)).reshape(1, LANES),
      jnp.pad(dt_bias, (0, LANES - GDN_H)).reshape(1, LANES), o_gain.reshape(1, GDN_DV))


def _gdn_bwd(proj, conv_w, a_log, dt_bias, o_gain, o, states, dy):
    lp = proj.shape[0]
    nc = lp // CHUNK

    def body(prev_ref, p_ref, cw_ref, al_ref, dt_ref, og_ref, o_ref, s_ref, dy_ref,
             dp_ref, dcw_ref, dal_ref, ddt_ref, dog_ref, dst, dconv_next):
        @pl.when(pl.program_id(0) == 0)
        def _():
            dst[...] = jnp.zeros_like(dst)
            dconv_next[...] = jnp.zeros_like(dconv_next)
            dcw_ref[...] = jnp.zeros_like(dcw_ref)
            dal_ref[...] = jnp.zeros_like(dal_ref)
            ddt_ref[...] = jnp.zeros_like(ddt_ref)
            dog_ref[...] = jnp.zeros_like(dog_ref)

        shifted, conv, act, slab, zs, g, bs, beta = _gdn_pre(prev_ref, p_ref, cw_ref, al_ref, dt_ref)
        bst = bs.T
        lane = _iota((CHUNK, LANES), 1)
        ones = jnp.ones((CHUNK, LANES), F32)
        db_slab = jnp.zeros((CHUNK, LANES), F32)
        dbeta_slab = jnp.zeros((CHUNK, LANES), F32)
        last_row = _iota((CHUNK, 1), 0) == CHUNK - 1
        dact_q, dact_k, dact_v = [], [], []
        for h in range(GDN_H):
            qa, ka, v, bet, bcol = _gdn_heads(act, beta, bs, h)
            qn_, rq = _l2n(qa)
            k, rk = _l2n(ka)
            scale = GDN_DK ** -0.5
            q = qn_ * scale
            s0 = s_ref[0, h]
            ds1 = dst[h]
            ov = o_ref[:, h * GDN_DV:(h + 1) * GDN_DV]
            dyv = dy_ref[:, h * GDN_DV:(h + 1) * GDN_DV]
            gate = p_ref[:, GDN_CONV + h * GDN_DV:GDN_CONV + (h + 1) * GDN_DV]
            rs = lax.rsqrt(jnp.mean(ov * ov, axis=-1, keepdims=True) + EPS)
            on = ov * rs
            dp_ref[:, GDN_CONV + h * GDN_DV:GDN_CONV + (h + 1) * GDN_DV] = (dyv * on * og_ref[...] * _dsilu(gate)).astype(BF16)
            don = dyv * _silu(gate)
            dog_ref[...] += jnp.sum(don * on, axis=0, keepdims=True)
            uu = don * og_ref[...]
            do = rs * uu - ov * (rs * rs * rs) * jnp.mean(ov * uu, axis=-1, keepdims=True)
            _, _, (dm, dstrict, eb, bl, kb, vb, nmat, tinv, kbe, u, w, vn, pm, qe, kd) = _gdn_head_fwd(
                q, k, v, bet, bcol, bst[h:h + 1, :], s0)
            ebl = jnp.exp(bl)
            dvn = tn(pm, do) + nn(kd, ds1)
            dpr = nt(do, vn)
            dqk, gp = dpr * dm, dpr * pm
            dqe = nt(do, s0)
            dkd = nt(vn, ds1)
            dst[h] = ds1 * ebl + tn(qe, do) - tn(w, dvn)
            du_ = tn(tinv, dvn, precision=HI)
            dw_ = tn(tinv, -nt(dvn, s0), precision=HI)
            dn = -(nt(du_, u) + nt(dw_, w))
            dkk, gn = dn * dstrict, dn * nmat
            dkb = nn(dkk, k) + dw_ * eb
            dk = tn(dkk, kb) + tn(dqk, q) + dkd * jnp.exp(bl - bcol) + dkb * bet
            dq = nn(dqk, k) + dqe * eb
            dbeta = jnp.sum(dkb * k, axis=-1, keepdims=True) + jnp.sum(du_ * v, axis=-1, keepdims=True)
            dv_ = du_ * bet
            gsum = gp + gn
            colsum = tn(gsum, ones, precision=HI)[:, 0:1]
            skd = jnp.sum(dkd * kd, axis=-1, keepdims=True)
            db = (jnp.sum(gsum, axis=-1, keepdims=True) - colsum + jnp.sum(dqe * qe, axis=-1, keepdims=True)
                  + jnp.sum(dw_ * kbe, axis=-1, keepdims=True) - skd)
            db_last = jnp.sum(skd, axis=0, keepdims=True) + jnp.sum(ds1 * s0) * ebl
            db = db + jnp.where(last_row, db_last, 0.0)
            db_slab = db_slab + jnp.where(lane == h, db, 0.0)
            dbeta_slab = dbeta_slab + jnp.where(lane == GDN_H + h, dbeta, 0.0)
            dqn = dq * scale
            dact_q.append(rq * dqn - qa * (rq * rq * rq) * jnp.sum(qa * dqn, axis=-1, keepdims=True))
            dact_k.append(rk * dk - ka * (rk * rk * rk) * jnp.sum(ka * dk, axis=-1, keepdims=True))
            dact_v.append(dv_)
        dact = jnp.concatenate(dact_q + dact_k + dact_v, axis=1)
        dconv = dact * _dsilu(conv)
        for j in range(4):
            dcw_ref[j:j + 1, :] += jnp.sum(dconv * shifted[j], axis=0, keepdims=True)
        dcat = jnp.concatenate([dconv, dconv_next[...]], axis=0)
        dx = dconv * cw_ref[3:4, :]
        for j in range(3):
            dx = dx + pltpu.roll(dcat, 2 * CHUNK - (3 - j), 0)[:CHUNK, :] * cw_ref[j:j + 1, :]
        dconv_next[...] = dconv
        dp_ref[:, 0:GDN_CONV] = dx.astype(BF16)
        dg = _sel_l(_tri(CHUNK, upper=True).astype(BF16), db_slab)
        da = dg * (-jnp.exp(al_ref[...])) * _sigmoid(zs)
        da = jnp.where(lane < GDN_H, da, 0.0)
        dal_ref[...] += jnp.sum(dg * g, axis=0, keepdims=True)
        ddt_ref[...] += jnp.sum(da, axis=0, keepdims=True)
        dp_ref[:, 4096:4096 + LANES] = (da + dbeta_slab * beta * (1.0 - beta)).astype(BF16)

    rv = lambda i: (nc - 1 - i, 0)
    blk = pl.BlockSpec((CHUNK, D), rv)
    fixed = lambda r, c: pl.BlockSpec((r, c), lambda i: (0, 0))
    return pl.pallas_call(
        body, name="gdn_bwd", grid=(nc,),
        in_specs=[pl.BlockSpec((CHUNK, GDN_INP), lambda i: (jnp.maximum(nc - 2 - i, 0), 0)),
                  pl.BlockSpec((CHUNK, GDN_INP), rv), fixed(8, GDN_CONV), fixed(1, LANES), fixed(1, LANES), fixed(1, GDN_DV),
                  blk, pl.BlockSpec((1, GDN_H, GDN_DK, GDN_DV), lambda i: (nc - 1 - i, 0, 0, 0)), blk],
        out_specs=[pl.BlockSpec((CHUNK, GDN_INP), rv), fixed(8, GDN_CONV), fixed(1, LANES), fixed(1, LANES), fixed(1, GDN_DV)],
        out_shape=[jax.ShapeDtypeStruct((lp, GDN_INP), BF16), jax.ShapeDtypeStruct((8, GDN_CONV), F32),
                   jax.ShapeDtypeStruct((1, LANES), F32), jax.ShapeDtypeStruct((1, LANES), F32),
                   jax.ShapeDtypeStruct((1, GDN_DV), F32)],
        scratch_shapes=[pltpu.VMEM((GDN_H, GDN_DK, GDN_DV), F32), pltpu.VMEM((CHUNK, GDN_CONV), F32)],
        compiler_params=_params(("arbitrary",)),
    )(proj, proj, jnp.pad(conv_w.reshape(4, GDN_CONV), ((0, 4), (0, 0))), jnp.pad(a_log, (0, LANES - GDN_H)).reshape(1, LANES),
      jnp.pad(dt_bias, (0, LANES - GDN_H)).reshape(1, LANES), o_gain.reshape(1, GDN_DV), o, states, dy)


def _coords():
    return lax.axis_index("x"), lax.axis_index("y"), lax.axis_index("c")


def _other_chips(x, y):
    return [(1 - x, y, 2 * (1 - x) + y), (x, 1 - y, 2 * x + 1 - y), (1 - x, 1 - y, 2 * (1 - x) + 1 - y)]


def _gather8(v, *, reduce, name):
    r, c = v.shape

    def body(v_ref, out_ref, *scratch):
        if reduce:
            buf, send_sems, recv_sems = scratch
        else:
            buf = out_ref
            send_sems, recv_sems = scratch
        x, y, cc = _coords()
        me = 4 * x + 2 * y + cc
        buf[me] = v_ref[...]
        copies = []
        for k in range(1, 8):
            px, py, pc = x ^ (k >> 2), y ^ ((k >> 1) & 1), cc ^ (k & 1)
            copies.append(pltpu.make_async_remote_copy(
                src_ref=v_ref, dst_ref=buf.at[me], send_sem=send_sems.at[k - 1], recv_sem=recv_sems.at[k - 1],
                device_id=(px, py, pc), device_id_type=MESH))
        for cp in copies:
            cp.start()
        for k in range(1, 8):
            peer = (x ^ (k >> 2)) * 4 + (y ^ ((k >> 1) & 1)) * 2 + (cc ^ (k & 1))
            pltpu.make_async_remote_copy(
                src_ref=v_ref, dst_ref=buf.at[peer], send_sem=send_sems.at[k - 1], recv_sem=recv_sems.at[k - 1],
                device_id=(x, y, cc), device_id_type=MESH).wait_recv()
        for cp in copies:
            cp.wait_send()
        if reduce:
            acc = buf[0]
            for d in range(1, 8):
                acc = acc + buf[d]
            out_ref[...] = acc

    scratch = [pltpu.SemaphoreType.DMA((7,)), pltpu.SemaphoreType.DMA((7,))]
    if reduce:
        scratch = [pltpu.VMEM((8, r, c), F32)] + scratch
    return pl.pallas_call(
        body, name=name, in_specs=[VM], out_specs=VM,
        out_shape=jax.ShapeDtypeStruct((r, c) if reduce else (8, r, c), F32),
        scratch_shapes=scratch, compiler_params=_params(),
    )(v)


def _ag_weights(w):
    r, c = w.shape
    half = r // 2

    def body(w_ref, out_ref, send_sems, recv_sems, lsem):
        x, y, cc = _coords()
        p = 2 * x + y
        chips = _other_chips(x, y)

        def rows(chip, hf):
            return out_ref.at[chip, pl.ds(hf * half, half), :]

        mine = pltpu.make_async_copy(w_ref, out_ref.at[p], lsem)
        mine.start()
        first = [pltpu.make_async_remote_copy(
            src_ref=w_ref.at[pl.ds(cc * half, half), :], dst_ref=rows(p, cc), send_sem=send_sems.at[k],
            recv_sem=recv_sems.at[k], device_id=(cx, cy, cc), device_id_type=MESH) for k, (cx, cy, _) in enumerate(chips)]
        for cp in first:
            cp.start()
        passed = []
        for k, (_, _, blk) in enumerate(chips):
            pltpu.make_async_remote_copy(
                src_ref=rows(blk, cc), dst_ref=rows(blk, cc), send_sem=send_sems.at[k], recv_sem=recv_sems.at[k],
                device_id=(x, y, cc), device_id_type=MESH).wait_recv()
            fw = pltpu.make_async_remote_copy(
                src_ref=rows(blk, cc), dst_ref=rows(blk, cc), send_sem=send_sems.at[3 + k], recv_sem=recv_sems.at[3 + k],
                device_id=(x, y, 1 - cc), device_id_type=MESH)
            fw.start()
            passed.append(fw)
        for k, (_, _, blk) in enumerate(chips):
            pltpu.make_async_remote_copy(
                src_ref=rows(blk, 1 - cc), dst_ref=rows(blk, 1 - cc), send_sem=send_sems.at[3 + k], recv_sem=recv_sems.at[3 + k],
                device_id=(x, y, cc), device_id_type=MESH).wait_recv()
        for cp in first + passed:
            cp.wait_send()
        mine.wait()

    return pl.pallas_call(
        body, name="ag_weights", in_specs=[ANY], out_specs=ANY, out_shape=jax.ShapeDtypeStruct((4, r, c), w.dtype),
        scratch_shapes=[pltpu.SemaphoreType.DMA((6,)), pltpu.SemaphoreType.DMA((6,)), pltpu.SemaphoreType.DMA],
        compiler_params=_params(),
    )(w)


def _swap_halves(g, *, name):
    nb, r, c = g.shape
    half = r // 2

    def body(g_ref, out_ref, send_sem, recv_sem):
        x, y, cc = _coords()
        cp = pltpu.make_async_remote_copy(
            src_ref=g_ref.at[:, pl.ds((1 - cc) * half, half), :], dst_ref=out_ref, send_sem=send_sem, recv_sem=recv_sem,
            device_id=(x, y, 1 - cc), device_id_type=MESH)
        cp.start()
        cp.wait()

    return pl.pallas_call(
        body, name=name, in_specs=[ANY], out_specs=ANY, out_shape=jax.ShapeDtypeStruct((nb, half, c), g.dtype),
        scratch_shapes=[pltpu.SemaphoreType.DMA, pltpu.SemaphoreType.DMA], compiler_params=_params(),
    )(g)


def _my_half_index():
    return lax.axis_index("c").astype(jnp.int32).reshape(1)


def _add_halves(g, got):
    nb, r, c = g.shape
    half = r // 2
    tr = _tile(half, 512, 16)
    nt_ = half // tr

    def body(c_ref, a_ref, b_ref, o_ref):
        o_ref[...] = (a_ref[...].astype(F32) + b_ref[...].astype(F32)).astype(BF16)

    return pl.pallas_call(
        body, name="rs_add_sibling",
        grid_spec=pltpu.PrefetchScalarGridSpec(
            num_scalar_prefetch=1, grid=(nb, nt_),
            in_specs=[pl.BlockSpec((1, tr, c), lambda b, i, cr: (b, cr[0] * nt_ + i, 0)),
                      pl.BlockSpec((1, tr, c), lambda b, i, cr: (b, i, 0))],
            out_specs=pl.BlockSpec((1, tr, c), lambda b, i, cr: (b, i, 0))),
        out_shape=jax.ShapeDtypeStruct((nb, half, c), BF16), compiler_params=_params(("parallel", "parallel")),
    )(_my_half_index(), g, got)


def _scatter_chips(s):
    nb, hrows, c = s.shape

    def body(s_ref, out_ref, send_sems, recv_sems):
        x, y, cc = _coords()
        p = 2 * x + y
        chips = _other_chips(x, y)
        sends = [pltpu.make_async_remote_copy(
            src_ref=s_ref.at[blk], dst_ref=out_ref.at[k], send_sem=send_sems.at[k], recv_sem=recv_sems.at[k],
            device_id=(cx, cy, cc), device_id_type=MESH) for k, (cx, cy, blk) in enumerate(chips)]
        for cp in sends:
            cp.start()
        for k in range(3):
            pltpu.make_async_remote_copy(
                src_ref=s_ref.at[p], dst_ref=out_ref.at[k], send_sem=send_sems.at[k], recv_sem=recv_sems.at[k],
                device_id=(x, y, cc), device_id_type=MESH).wait_recv()
        for cp in sends:
            cp.wait_send()

    return pl.pallas_call(
        body, name="rs_scatter", in_specs=[ANY], out_specs=ANY, out_shape=jax.ShapeDtypeStruct((3, hrows, c), s.dtype),
        scratch_shapes=[pltpu.SemaphoreType.DMA((3,)), pltpu.SemaphoreType.DMA((3,))], compiler_params=_params(),
    )(s)


def _sum_chips(s, got):
    nb, hrows, c = s.shape
    tr = _tile(hrows, 512, 16)

    def body(idx_ref, own_ref, got_ref, o_ref):
        p = idx_ref[0]
        own = own_ref[0].astype(F32)
        parts = [got_ref[k].astype(F32) for k in range(3)]
        acc = jnp.zeros_like(own)
        for q in range(4):
            val = own
            for k, rel in enumerate((2, 1, 3)):
                val = jnp.where((p ^ rel) == q, parts[k], val)
            acc = acc + val
        o_ref[...] = acc

    idx = (2 * lax.axis_index("x") + lax.axis_index("y")).astype(jnp.int32).reshape(1)
    return pl.pallas_call(
        body, name="rs_sum_chips",
        grid_spec=pltpu.PrefetchScalarGridSpec(
            num_scalar_prefetch=1, grid=(hrows // tr,),
            in_specs=[pl.BlockSpec((1, tr, c), lambda i, pr: (pr[0], i, 0)), pl.BlockSpec((3, tr, c), lambda i, pr: (0, i, 0))],
            out_specs=pl.BlockSpec((tr, c), lambda i, pr: (i, 0))),
        out_shape=jax.ShapeDtypeStruct((hrows, c), F32), compiler_params=_params(("parallel",)),
    )(idx, s, got)


def _join_halves(t):
    hrows, c = t.shape

    def body(t_ref, out_ref, send_sem, recv_sem, lsem):
        x, y, cc = _coords()
        mine = pltpu.make_async_copy(t_ref, out_ref.at[cc], lsem)
        mine.start()
        cp = pltpu.make_async_remote_copy(
            src_ref=t_ref, dst_ref=out_ref.at[cc], send_sem=send_sem, recv_sem=recv_sem,
            device_id=(x, y, 1 - cc), device_id_type=MESH)
        cp.start()
        pltpu.make_async_remote_copy(
            src_ref=t_ref, dst_ref=out_ref.at[1 - cc], send_sem=send_sem, recv_sem=recv_sem,
            device_id=(x, y, cc), device_id_type=MESH).wait_recv()
        cp.wait_send()
        mine.wait()

    return pl.pallas_call(
        body, name="rs_join", in_specs=[ANY], out_specs=ANY, out_shape=jax.ShapeDtypeStruct((2, hrows, c), t.dtype),
        scratch_shapes=[pltpu.SemaphoreType.DMA, pltpu.SemaphoreType.DMA, pltpu.SemaphoreType.DMA], compiler_params=_params(),
    )(t)


def _reduce_scatter(g):
    got = _swap_halves(g, name="rs_swap")
    s = _add_halves(g, got)
    recv = _scatter_chips(s)
    t = _sum_chips(s, recv)
    return _join_halves(t).reshape(g.shape[1], g.shape[2])


_BIG = (("w_gate_up", 2), ("w_down", 1), ("fox_w_in", 2), ("fox_w_out", 1), ("gla_w_in", 2), ("gla_w_out", 1),
        ("gdn_w_in", 2), ("gdn_w_out", 1))
_SMALL_SHARDED = (("meta_tokens", 1), ("gla_w_alpha2", 2), ("gdn_conv_w", 3))
_REPLICATED = ("norm_mix", "norm_ffn", "fox_b_f", "fox_q_gain", "fox_k_gain", "gla_b_alpha", "gla_o_gain",
               "gdn_a_log", "gdn_dt_bias", "gdn_o_gain")
_WEIGHTS = ("meta_tokens", "norm_mix", "norm_ffn", "w_gate_up", "w_down", "fox_w_in", "fox_b_f", "fox_q_gain",
            "fox_k_gain", "fox_w_out", "gla_w_in", "gla_w_alpha2", "gla_b_alpha", "gla_o_gain", "gla_w_out",
            "gdn_w_in", "gdn_conv_w", "gdn_a_log", "gdn_dt_bias", "gdn_o_gain", "gdn_w_out")
_PACK_ROWS = 512


def _pack(arrays, width, row_mult, dtype):
    flat = jnp.concatenate([a.astype(dtype).reshape(-1) for a in arrays])
    per = width * row_mult
    n = -(-flat.shape[0] // per) * per
    return jnp.pad(flat, (0, n - flat.shape[0])).reshape(n // width, width)


def _unpack(flat, shapes):
    out, off = [], 0
    for s in shapes:
        n = 1
        for d in s:
            n *= d
        out.append(flat[off:off + n].reshape(s))
        off += n
    return out


def _unpack_cols(flat2, shapes):
    out, off = [], 0
    for s in shapes:
        n = 1
        for d in s:
            n *= d
        out.append(flat2[:, off:off + n].reshape((flat2.shape[0],) + tuple(s)))
        off += n
    return out


def _pad_cols(w, n):
    return jnp.pad(w, [(0, 0)] * (w.ndim - 1) + [(0, n - w.shape[-1])])


def kernel(x, meta_tokens, norm_mix, norm_ffn, w_gate_up, w_down, fox_w_in, fox_b_f, fox_q_gain, fox_k_gain, fox_w_out, gla_w_in, gla_w_alpha2, gla_b_alpha, gla_o_gain, gla_w_out, gdn_w_in, gdn_conv_w, gdn_a_log, gdn_dt_bias, gdn_o_gain, gdn_w_out, loss_target, m_meta_tokens, m_norm_mix, m_norm_ffn, m_w_gate_up, m_w_down, m_fox_w_in, m_fox_b_f, m_fox_q_gain, m_fox_k_gain, m_fox_w_out, m_gla_w_in, m_gla_w_alpha2, m_gla_b_alpha, m_gla_o_gain, m_gla_w_out, m_gdn_w_in, m_gdn_conv_w, m_gdn_a_log, m_gdn_dt_bias, m_gdn_o_gain, m_gdn_w_out, v_meta_tokens, v_norm_mix, v_norm_ffn, v_w_gate_up, v_w_down, v_fox_w_in, v_fox_b_f, v_fox_q_gain, v_fox_k_gain, v_fox_w_out, v_gla_w_in, v_gla_w_alpha2, v_gla_b_alpha, v_gla_o_gain, v_gla_w_out, v_gdn_w_in, v_gdn_conv_w, v_gdn_a_log, v_gdn_dt_bias, v_gdn_o_gain, v_gdn_w_out):
    W = dict(meta_tokens=meta_tokens, norm_mix=norm_mix, norm_ffn=norm_ffn, w_gate_up=w_gate_up, w_down=w_down,
             fox_w_in=fox_w_in, fox_b_f=fox_b_f, fox_q_gain=fox_q_gain, fox_k_gain=fox_k_gain, fox_w_out=fox_w_out,
             gla_w_in=gla_w_in, gla_w_alpha2=gla_w_alpha2, gla_b_alpha=gla_b_alpha, gla_o_gain=gla_o_gain,
             gla_w_out=gla_w_out, gdn_w_in=gdn_w_in, gdn_conv_w=gdn_conv_w, gdn_a_log=gdn_a_log,
             gdn_dt_bias=gdn_dt_bias, gdn_o_gain=gdn_o_gain, gdn_w_out=gdn_w_out)
    M = dict(meta_tokens=m_meta_tokens, norm_mix=m_norm_mix, norm_ffn=m_norm_ffn, w_gate_up=m_w_gate_up, w_down=m_w_down,
             fox_w_in=m_fox_w_in, fox_b_f=m_fox_b_f, fox_q_gain=m_fox_q_gain, fox_k_gain=m_fox_k_gain,
             fox_w_out=m_fox_w_out, gla_w_in=m_gla_w_in, gla_w_alpha2=m_gla_w_alpha2, gla_b_alpha=m_gla_b_alpha,
             gla_o_gain=m_gla_o_gain, gla_w_out=m_gla_w_out, gdn_w_in=m_gdn_w_in, gdn_conv_w=m_gdn_conv_w,
             gdn_a_log=m_gdn_a_log, gdn_dt_bias=m_gdn_dt_bias, gdn_o_gain=m_gdn_o_gain, gdn_w_out=m_gdn_w_out)
    V = dict(meta_tokens=v_meta_tokens, norm_mix=v_norm_mix, norm_ffn=v_norm_ffn, w_gate_up=v_w_gate_up, w_down=v_w_down,
             fox_w_in=v_fox_w_in, fox_b_f=v_fox_b_f, fox_q_gain=v_fox_q_gain, fox_k_gain=v_fox_k_gain,
             fox_w_out=v_fox_w_out, gla_w_in=v_gla_w_in, gla_w_alpha2=v_gla_w_alpha2, gla_b_alpha=v_gla_b_alpha,
             gla_o_gain=v_gla_o_gain, gla_w_out=v_gla_w_out, gdn_w_in=v_gdn_w_in, gdn_conv_w=v_gdn_conv_w,
             gdn_a_log=v_gdn_a_log, gdn_dt_bias=v_gdn_dt_bias, gdn_o_gain=v_gdn_o_gain, gdn_w_out=v_gdn_w_out)
    chip = 2 * lax.axis_index("x") + lax.axis_index("y")

    packed = _pack([W[n] for n, _ in _BIG], D, _PACK_ROWS, BF16)
    gathered = _ag_weights(packed).reshape(4, -1)
    full = {}
    for (n, ax), seg in zip(_BIG, _unpack_cols(gathered, [W[n].shape for n, _ in _BIG])):
        full[n] = jnp.concatenate([seg[q] for q in range(4)], axis=ax)
    small = _pack([W[n] for n, _ in _SMALL_SHARDED], LANES, 8, F32)
    small_all = _gather8(small, reduce=False, name="gather_small").reshape(8, -1)
    for (n, ax), seg in zip(_SMALL_SHARDED, _unpack_cols(small_all, [W[n].shape for n, _ in _SMALL_SHARDED])):
        full[n] = jnp.concatenate([seg[2 * q] for q in range(4)], axis=ax)
    fox_in = _pad_cols(full["fox_w_in"], FOX_INP)
    gla_in = _pad_cols(full["gla_w_in"], GLA_INP)
    gdn_in = _pad_cols(full["gdn_w_in"], GDN_INP)
    w_alpha2, conv_w = full["gla_w_alpha2"][0], full["gdn_conv_w"][0]

    h = jnp.concatenate([jnp.zeros((META0, D), F32), full["meta_tokens"], x[0]], axis=0)
    saved = []
    for i in range(DEPTH):
        kind, j = i % 3, i // 3
        y = _rms_fwd(h, norm_mix[i], name=f"norm_mix{i}")
        if kind == 0:
            proj = _mm(y, fox_in[j], name=f"fox_in{j}")
            qa, ka, va = _fox_prep(proj, fox_b_f[j], fox_q_gain[j], fox_k_gain[j])
            o, og, lse = _fox_attn_fwd(qa, ka, va, proj)
            w_out, mix = full["fox_w_out"][j], (proj, qa, ka, va, o, lse)
        elif kind == 1:
            proj = _mm(y, gla_in[j], name=f"gla_in{j}")
            o, og, states = _gla_fwd(proj, w_alpha2, gla_b_alpha[j], gla_o_gain[j])
            w_out, mix = full["gla_w_out"][j], (proj, o, states)
        else:
            proj = _mm(y, gdn_in[j], name=f"gdn_in{j}")
            o, og, states = _gdn_fwd(proj, conv_w, gdn_a_log[j], gdn_dt_bias[j], gdn_o_gain[j])
            w_out, mix = full["gdn_w_out"][j], (proj, o, states)
        hm = _mm(og, w_out, add=h, name=f"mix_out{i}")
        yf = _rms_fwd(hm, norm_ffn[i], name=f"norm_ffn{i}")
        gu = _mm(yf, full["w_gate_up"][i], name=f"ffn_up{i}")
        act = _swiglu_fwd(gu, name=f"swiglu{i}")
        hn = _mm(act, full["w_down"][i], add=hm, name=f"ffn_down{i}")
        saved.append((h, y, mix, og, w_out, hm, yf, gu, act))
        h = hn
    dh, loss_tile = _loss_head(h, loss_target[0])

    G = {n: [None] * W[n].shape[0] for n in _WEIGHTS if n != "meta_tokens"}
    for i in reversed(range(DEPTH)):
        kind, j = i % 3, i // 3
        h_in, y, mix, og, w_out, hm, yf, gu, act = saved[i]
        dact = _mm(dh, full["w_down"][i], tb=True, name=f"d_act{i}")
        G["w_down"][i] = _mm(act, dh, ta=True, out_dtype=BF16, name=f"d_w_down{i}")
        dgu = _swiglu_bwd(gu, dact, name=f"d_swiglu{i}")
        dyf = _mm(dgu, full["w_gate_up"][i], tb=True, name=f"d_yf{i}")
        G["w_gate_up"][i] = _mm(yf, dgu, ta=True, out_dtype=BF16, name=f"d_w_gate_up{i}")
        dhm, dnf = _rms_bwd(hm, norm_ffn[i], dyf, dh, name=f"d_norm_ffn{i}")
        G["norm_ffn"][i] = dnf[0]
        dog = _mm(dhm, w_out, tb=True, name=f"d_og{i}")
        dw_out = _mm(og, dhm, ta=True, out_dtype=BF16, name=f"d_w_out{i}")
        if kind == 0:
            proj, qa, ka, va, o, lse = mix
            doa, q2, dgate = _fox_gate_bwd(dog, o, proj, lse, qa)
            dqn, dkn, dv, dct = _fox_attn_bwd(q2, ka, va, doa)
            dproj, dqg, dkg, dbf = _fox_prep_bwd(proj, fox_b_f[j], fox_q_gain[j], fox_k_gain[j], dqn, dkn, dv, dgate, dct)
            G["fox_w_out"][j] = dw_out
            G["fox_q_gain"][j] = dqg.reshape(FOX_H, FOX_DH).sum(0)
            G["fox_k_gain"][j] = dkg.reshape(FOX_H, FOX_DH).sum(0)
            G["fox_b_f"][j] = dbf[0, :FOX_H]
            w_in, wname, n_in = fox_in[j], "fox_w_in", fox_w_in.shape[2] * 4
        elif kind == 1:
            proj, o, states = mix
            dproj, dwa, dba, dogain = _gla_bwd(proj, w_alpha2, gla_b_alpha[j], gla_o_gain[j], o, states, dog)
            G["gla_w_out"][j] = dw_out
            G["gla_w_alpha2"][j] = dwa[:GLA_RANK]
            G["gla_b_alpha"][j] = dba[0]
            G["gla_o_gain"][j] = dogain[0]
            w_in, wname, n_in = gla_in[j], "gla_w_in", gla_w_in.shape[2] * 4
        else:
            proj, o, states = mix
            dproj, dcw, dal, ddt, dogain = _gdn_bwd(proj, conv_w, gdn_a_log[j], gdn_dt_bias[j], gdn_o_gain[j], o, states, dog)
            G["gdn_w_out"][j] = dw_out
            G["gdn_conv_w"][j] = dcw[:4].reshape(4, 1, GDN_CONV)
            G["gdn_a_log"][j] = dal[0, :GDN_H]
            G["gdn_dt_bias"][j] = ddt[0, :GDN_H]
            G["gdn_o_gain"][j] = dogain[0]
            w_in, wname, n_in = gdn_in[j], "gdn_w_in", gdn_w_in.shape[2] * 4
        dy = _mm(dproj, w_in, tb=True, name=f"d_y{i}")
        G[wname][j] = _mm(y, dproj, ta=True, out_dtype=BF16, name=f"d_w_in{i}")[:, :n_in]
        dh, dnm = _rms_bwd(h_in, norm_mix[i], dy, dhm, name=f"d_norm_mix{i}")
        G["norm_mix"][i] = dnm[0]
    grad_x = dh[ROW0:][None]
    G = {n: jnp.stack(v) for n, v in G.items()}
    G["meta_tokens"] = dh[META0:ROW0]

    blocks = []
    for q in range(4):
        parts = []
        for n, ax in _BIG:
            sz = W[n].shape[ax]
            parts.append(lax.slice_in_dim(G[n], q * sz, (q + 1) * sz, axis=ax))
        blocks.append(_pack(parts, D, _PACK_ROWS, BF16))
    reduced = _reduce_scatter(jnp.stack(blocks)).reshape(-1)
    grads = dict(zip([n for n, _ in _BIG], _unpack(reduced, [W[n].shape for n, _ in _BIG])))
    small_names = [n for n, _ in _SMALL_SHARDED] + list(_REPLICATED)
    small_g = _pack([G[n] for n in small_names] + [loss_tile[0, 0:1]], LANES, 8, F32)
    small_sum = _gather8(small_g, reduce=True, name="allreduce_small").reshape(-1)
    small_shapes = [G[n].shape for n in small_names] + [(1,)]
    small_vals = _unpack(small_sum, small_shapes)
    loss = small_vals[-1][0]
    for n, val in zip(small_names, small_vals[:-1]):
        grads[n] = val
    for n, ax in _SMALL_SHARDED:
        sz = W[n].shape[ax]
        grads[n] = lax.dynamic_slice_in_dim(grads[n], chip * sz, sz, axis=ax)

    delta, new_m, new_v = {}, {}, {}
    for n, _ in _BIG:
        delta[n], new_m[n], new_v[n] = _adamw(W[n], grads[n], M[n], V[n], name=f"adamw_{n}")
    tiny = [n for n in _WEIGHTS if n not in dict(_BIG)]
    packs = [_pack([T[n] for n in tiny], LANES, 8, F32) for T in (W, grads, M, V)]
    outs = _adamw(*packs, name="adamw_small")
    shapes = [W[n].shape for n in tiny]
    for dst, o in zip((delta, new_m, new_v), outs):
        for n, val in zip(tiny, _unpack(o.reshape(-1), shapes)):
            dst[n] = val
    return (loss, grad_x, *[grads[n] for n in _WEIGHTS], *[delta[n] for n in _WEIGHTS],
            *[new_m[n] for n in _WEIGHTS], *[new_v[n] for n in _WEIGHTS])
```

```python
import functools

import jax
import jax.numpy as jnp
from jax import lax
from jax.experimental import pallas as pl
from jax.experimental.pallas import tpu as pltpu

F32, BF16 = jnp.float32, jnp.bfloat16
D = 1024
N_META = 16
ROW0 = 128
META0 = ROW0 - N_META
EPS = 1e-6
LANES = 128
VMEM_LIMIT = 56 * 1024 * 1024

FOX_H, FOX_DH = 16, 64
FOX_INP = 4224
GLA_H, GLA_DK, GLA_DV, GLA_RANK = 4, 128, 256, 16
GLA_QK, GLA_V = 512, 1024
GLA_INP = 3200
GLA_NORM = 16.0
GDN_H, GDN_DK, GDN_DV = 8, 128, 128
GDN_CONV = 3072
GDN_INP = 4224
CHUNK = 64
D_FF = 2816
DEPTH = 4

ADAM_LR, ADAM_B1, ADAM_B2, ADAM_EPS, ADAM_WD, ADAM_STEP = 0.001, 0.9, 0.999, 1e-08, 0.01, 10

MESH = pl.DeviceIdType.MESH
ANY = pl.BlockSpec(memory_space=pl.ANY)
VM = pl.BlockSpec(memory_space=pltpu.VMEM)


def _params(sem=None, **kw):
    if sem is not None:
        kw["dimension_semantics"] = sem
    return pltpu.CompilerParams(vmem_limit_bytes=VMEM_LIMIT, **kw)


def _tile(n, cap, mult=LANES):
    best = None
    for t in range(mult, min(n, cap) + 1, mult):
        if n % t == 0:
            best = t
    return best if best is not None else n


def nn(a, b, **kw):
    return jnp.dot(a, b, preferred_element_type=F32, **kw)


def nt(a, b, **kw):
    return lax.dot_general(a, b, (((1,), (1,)), ((), ())), preferred_element_type=F32, **kw)


def tn(a, b, **kw):
    return lax.dot_general(a, b, (((0,), (0,)), ((), ())), preferred_element_type=F32, **kw)


def _split3(x):
    hi = x.astype(BF16)
    r = x - hi.astype(F32)
    mid = r.astype(BF16)
    lo = (r - mid.astype(F32)).astype(BF16)
    return hi, mid, lo


def _sel_l(sel, x):
    a, b, c = _split3(x)
    return nn(sel, a) + nn(sel, b) + nn(sel, c)


def _sel_r(x, sel):
    a, b, c = _split3(x)
    return nn(a, sel) + nn(b, sel) + nn(c, sel)


def _iota(shape, dim):
    return lax.broadcasted_iota(jnp.int32, shape, dim)


def _tri(n, upper=False, strict=False):
    i, j = _iota((n, n), 0), _iota((n, n), 1)
    if upper:
        m = (j > i) if strict else (j >= i)
    else:
        m = (j < i) if strict else (j <= i)
    return m


def _sigmoid(x):
    return 1.0 / (1.0 + jnp.exp(-x))


def _log_sigmoid(x):
    return jnp.minimum(x, 0.0) - jnp.log(1.0 + jnp.exp(-jnp.abs(x)))


def _softplus(x):
    return jnp.maximum(x, 0.0) + jnp.log(1.0 + jnp.exp(-jnp.abs(x)))


def _silu(x):
    return x * _sigmoid(x)


def _dsilu(x):
    s = _sigmoid(x)
    return s * (1.0 + x * (1.0 - s))


def _mm(a, b, *, ta=False, tb=False, add=None, out_dtype=F32, name):
    m, k = (a.shape[1], a.shape[0]) if ta else a.shape
    n = b.shape[0] if tb else b.shape[1]
    assert k == (b.shape[1] if tb else b.shape[0])
    tm, tn_, tk = _tile(m, 1408, 16), _tile(n, 768), _tile(k, 1408)
    nk = k // tk

    def body(*refs):
        if add is None:
            a_ref, b_ref, o_ref, acc = refs
        else:
            a_ref, b_ref, r_ref, o_ref, acc = refs
        kk = pl.program_id(2)

        @pl.when(kk == 0)
        def _():
            acc[...] = jnp.zeros_like(acc)

        av, bv = a_ref[...].astype(BF16), b_ref[...].astype(BF16)
        dims = (((0,) if ta else (1,), (1,) if tb else (0,)), ((), ()))
        acc[...] += lax.dot_general(av, bv, dims, preferred_element_type=F32)

        @pl.when(kk == nk - 1)
        def _():
            r = acc[...]
            if add is not None:
                r = r + r_ref[...].astype(F32)
            o_ref[...] = r.astype(out_dtype)

    a_spec = pl.BlockSpec((tk, tm), lambda i, j, q: (q, i)) if ta else pl.BlockSpec((tm, tk), lambda i, j, q: (i, q))
    b_spec = pl.BlockSpec((tn_, tk), lambda i, j, q: (j, q)) if tb else pl.BlockSpec((tk, tn_), lambda i, j, q: (q, j))
    o_spec = pl.BlockSpec((tm, tn_), lambda i, j, q: (i, j))
    ins, specs = [a, b], [a_spec, b_spec]
    if add is not None:
        ins.append(add)
        specs.append(o_spec)
    return pl.pallas_call(
        body, name=name, grid=(m // tm, n // tn_, nk), in_specs=specs, out_specs=o_spec,
        out_shape=jax.ShapeDtypeStruct((m, n), out_dtype),
        scratch_shapes=[pltpu.VMEM((tm, tn_), F32)],
        compiler_params=_params(("parallel", "parallel", "arbitrary")),
    )(*ins)


def _rms_fwd(h, g, *, name):
    lp = h.shape[0]
    tr = _tile(lp, 512)

    def body(h_ref, g_ref, y_ref):
        x = h_ref[...]
        r = lax.rsqrt(jnp.mean(x * x, axis=-1, keepdims=True) + EPS)
        y_ref[...] = (x * r * g_ref[...]).astype(BF16)

    return pl.pallas_call(
        body, name=name, grid=(lp // tr,),
        in_specs=[pl.BlockSpec((tr, D), lambda i: (i, 0)), pl.BlockSpec((1, D), lambda i: (0, 0))],
        out_specs=pl.BlockSpec((tr, D), lambda i: (i, 0)),
        out_shape=jax.ShapeDtypeStruct((lp, D), BF16), compiler_params=_params(("parallel",)),
    )(h, g.reshape(1, D))


def _rms_bwd(h, g, dy, dres, *, name):
    lp = h.shape[0]
    tr = _tile(lp, 512)

    def body(h_ref, g_ref, dy_ref, dr_ref, dh_ref, dg_ref):
        @pl.when(pl.program_id(0) == 0)
        def _():
            dg_ref[...] = jnp.zeros_like(dg_ref)

        x, dyv = h_ref[...], dy_ref[...].astype(F32)
        r = lax.rsqrt(jnp.mean(x * x, axis=-1, keepdims=True) + EPS)
        u = dyv * g_ref[...]
        dx = r * u - x * (r * r * r) * jnp.mean(x * u, axis=-1, keepdims=True)
        dh_ref[...] = dr_ref[...] + dx
        dg_ref[...] += jnp.sum(dyv * x * r, axis=0, keepdims=True)

    return pl.pallas_call(
        body, name=name, grid=(lp // tr,),
        in_specs=[pl.BlockSpec((tr, D), lambda i: (i, 0)), pl.BlockSpec((1, D), lambda i: (0, 0)),
                  pl.BlockSpec((tr, D), lambda i: (i, 0)), pl.BlockSpec((tr, D), lambda i: (i, 0))],
        out_specs=[pl.BlockSpec((tr, D), lambda i: (i, 0)), pl.BlockSpec((1, D), lambda i: (0, 0))],
        out_shape=[jax.ShapeDtypeStruct((lp, D), F32), jax.ShapeDtypeStruct((1, D), F32)],
        compiler_params=_params(("arbitrary",)),
    )(h, g.reshape(1, D), dy, dres)


GU_ROWS, DOWN_ROWS = 1408, 704
OFF_GU, OFF_DOWN = 0, DEPTH * GU_ROWS
FFN_ROWS = DEPTH * (GU_ROWS + DOWN_ROWS)
FFN_TM = 704


def _gu_spec(fn):
    return pl.BlockSpec((None, GU_ROWS, D), fn)


def _down_spec(fn):
    return pl.BlockSpec((None, DOWN_ROWS, D), fn)


def _down_pair(w0_ref, w1_ref):
    return jnp.concatenate([w0_ref[...], w1_ref[...]], axis=0)


def _ffn_up(yf, wpk, layer):
    lp = yf.shape[0]
    tm = _tile(lp, FFN_TM, 16)

    def body(y_ref, wg_ref, wu_ref, g_ref, u_ref, a_ref):
        y = y_ref[...]
        g, u = nt(y, wg_ref[...]), nt(y, wu_ref[...])
        g_ref[...] = g.astype(BF16)
        u_ref[...] = u.astype(BF16)
        a_ref[...] = (_silu(g) * u).astype(BF16)

    o = pl.BlockSpec((tm, GU_ROWS), lambda i, j: (i, j))
    return pl.pallas_call(
        body, name=f"ffn_up{layer}", grid=(lp // tm, 2),
        in_specs=[pl.BlockSpec((tm, D), lambda i, j: (i, 0)), _gu_spec(lambda i, j: (j, OFF_GU // GU_ROWS + layer, 0)),
                  _gu_spec(lambda i, j: (2 + j, OFF_GU // GU_ROWS + layer, 0))],
        out_specs=[o, o, o], out_shape=[jax.ShapeDtypeStruct((lp, D_FF), BF16)] * 3,
        compiler_params=_params(("parallel", "parallel")),
    )(yf, wpk, wpk)


def _ffn_down(act, wpk, layer, res):
    lp = act.shape[0]
    tm = _tile(lp, FFN_TM, 16)

    def body(a_ref, w0_ref, w1_ref, r_ref, o_ref, acc):
        kk = pl.program_id(1)

        @pl.when(kk == 0)
        def _():
            acc[...] = r_ref[...]

        acc[...] += nn(a_ref[...], _down_pair(w0_ref, w1_ref))

        @pl.when(kk == 1)
        def _():
            o_ref[...] = acc[...]

    o = pl.BlockSpec((tm, D), lambda i, kk: (i, 0))
    blk = OFF_DOWN // DOWN_ROWS + layer
    return pl.pallas_call(
        body, name=f"ffn_down{layer}", grid=(lp // tm, 2),
        in_specs=[pl.BlockSpec((tm, GU_ROWS), lambda i, kk: (i, kk)), _down_spec(lambda i, kk: (2 * kk, blk, 0)),
                  _down_spec(lambda i, kk: (2 * kk + 1, blk, 0)), o],
        out_specs=o, out_shape=jax.ShapeDtypeStruct((lp, D), F32), scratch_shapes=[pltpu.VMEM((tm, D), F32)],
        compiler_params=_params(("parallel", "arbitrary")),
    )(act, wpk, wpk, res)


def _ffn_dact(dh, wpk, layer, gate, up):
    lp = dh.shape[0]
    tm = _tile(lp, FFN_TM, 16)

    def body(d_ref, w0_ref, w1_ref, g_ref, u_ref, dg_ref, du_ref):
        da = nt(d_ref[...].astype(BF16), _down_pair(w0_ref, w1_ref))
        g, u = g_ref[...].astype(F32), u_ref[...].astype(F32)
        dg_ref[...] = (da * u * _dsilu(g)).astype(BF16)
        du_ref[...] = (da * _silu(g)).astype(BF16)

    o = pl.BlockSpec((tm, GU_ROWS), lambda i, j: (i, j))
    blk = OFF_DOWN // DOWN_ROWS + layer
    return pl.pallas_call(
        body, name=f"d_act{layer}", grid=(lp // tm, 2),
        in_specs=[pl.BlockSpec((tm, D), lambda i, j: (i, 0)), _down_spec(lambda i, j: (2 * j, blk, 0)),
                  _down_spec(lambda i, j: (2 * j + 1, blk, 0)), o, o],
        out_specs=[o, o], out_shape=[jax.ShapeDtypeStruct((lp, D_FF), BF16)] * 2,
        compiler_params=_params(("parallel", "parallel")),
    )(dh, wpk, wpk, gate, up)


def _ffn_dyf(dg, du, wpk, layer):
    lp = dg.shape[0]
    tm = _tile(lp, FFN_TM, 16)

    def body(dg_ref, du_ref, w_ref, o_ref, acc):
        kk = pl.program_id(1)

        @pl.when(kk == 0)
        def _():
            acc[...] = jnp.zeros_like(acc)

        @pl.when(kk < 2)
        def _():
            acc[...] += nn(dg_ref[...], w_ref[...])

        @pl.when(kk >= 2)
        def _():
            acc[...] += nn(du_ref[...], w_ref[...])

        @pl.when(kk == 3)
        def _():
            o_ref[...] = acc[...]

    return pl.pallas_call(
        body, name=f"d_yf{layer}", grid=(lp // tm, 4),
        in_specs=[pl.BlockSpec((tm, GU_ROWS), lambda i, kk: (i, jnp.minimum(kk, 1))),
                  pl.BlockSpec((tm, GU_ROWS), lambda i, kk: (i, jnp.maximum(kk - 2, 0))),
                  _gu_spec(lambda i, kk: (kk, OFF_GU // GU_ROWS + layer, 0))],
        out_specs=pl.BlockSpec((tm, D), lambda i, kk: (i, 0)), out_shape=jax.ShapeDtypeStruct((lp, D), F32),
        scratch_shapes=[pltpu.VMEM((tm, D), F32)], compiler_params=_params(("parallel", "arbitrary")),
    )(dg, du, wpk)


def _ffn_dw_down(act, dh, gpk, layer):
    lp = act.shape[0]
    tk = _tile(lp, 1408, 16)
    nk = lp // tk
    row = OFF_DOWN + layer * DOWN_ROWS

    def body(a_ref, d_ref, g_in, g_out, acc, stage, sems):
        jp, kk = pl.program_id(0), pl.program_id(1)

        @pl.when(kk == 0)
        def _():
            acc[...] = jnp.zeros_like(acc)

        acc[...] += tn(a_ref[...], d_ref[...].astype(BF16))

        @pl.when(kk == nk - 1)
        def _():
            stage[...] = acc[...].astype(BF16)
            copies = [pltpu.make_async_copy(stage.at[pl.ds(hf * DOWN_ROWS, DOWN_ROWS), :],
                                            g_out.at[2 * jp + hf, pl.ds(row, DOWN_ROWS), :], sems.at[hf]) for hf in range(2)]
            for cp in copies:
                cp.start()
            for cp in copies:
                cp.wait()

    return pl.pallas_call(
        body, name=f"d_w_down{layer}", grid=(2, nk),
        in_specs=[pl.BlockSpec((tk, GU_ROWS), lambda jp, kk: (kk, jp)), pl.BlockSpec((tk, D), lambda jp, kk: (kk, 0)), ANY],
        out_specs=ANY, out_shape=jax.ShapeDtypeStruct(gpk.shape, gpk.dtype),
        scratch_shapes=[pltpu.VMEM((GU_ROWS, D), F32), pltpu.VMEM((GU_ROWS, D), BF16), pltpu.SemaphoreType.DMA((2,))],
        input_output_aliases={2: 0}, compiler_params=_params(("arbitrary", "arbitrary")),
    )(act, dh, gpk)


def _ffn_dw_gu(dg, du, yf, gpk, layer):
    lp = dg.shape[0]
    tk = _tile(lp, 1408, 16)
    nk = lp // tk

    def body(dg_ref, du_ref, y_ref, g_in, o_ref, acc):
        c, kk = pl.program_id(0), pl.program_id(1)

        @pl.when(kk == 0)
        def _():
            acc[...] = jnp.zeros_like(acc)

        @pl.when(c < 2)
        def _():
            acc[...] += tn(dg_ref[...], y_ref[...])

        @pl.when(c >= 2)
        def _():
            acc[...] += tn(du_ref[...], y_ref[...])

        @pl.when(kk == nk - 1)
        def _():
            o_ref[...] = acc[...].astype(BF16)

    return pl.pallas_call(
        body, name=f"d_w_gate_up{layer}", grid=(4, nk),
        in_specs=[pl.BlockSpec((tk, GU_ROWS), lambda c, kk: (kk, jnp.minimum(c, 1))),
                  pl.BlockSpec((tk, GU_ROWS), lambda c, kk: (kk, jnp.maximum(c - 2, 0))),
                  pl.BlockSpec((tk, D), lambda c, kk: (kk, 0)), ANY],
        out_specs=_gu_spec(lambda c, kk: (c, OFF_GU // GU_ROWS + layer, 0)),
        out_shape=jax.ShapeDtypeStruct(gpk.shape, gpk.dtype),
        scratch_shapes=[pltpu.VMEM((GU_ROWS, D), F32)], input_output_aliases={3: 0},
        compiler_params=_params(("parallel", "arbitrary")),
    )(dg, du, yf, gpk)


def _loss_head(h, target):
    lp = h.shape[0]
    nb = lp // ROW0

    def body(h_ref, t_ref, dh_ref, l_ref):
        i = pl.program_id(0)

        @pl.when(i == 0)
        def _():
            l_ref[...] = jnp.zeros_like(l_ref)
            dh_ref[...] = jnp.zeros_like(dh_ref)

        @pl.when(i > 0)
        def _():
            err = h_ref[...] - t_ref[...]
            dh_ref[...] = err * (1.0 / D)
            l_ref[...] += jnp.sum(err * err) * (0.5 / D)

    return pl.pallas_call(
        body, name="loss_head", grid=(nb,),
        in_specs=[pl.BlockSpec((ROW0, D), lambda i: (i, 0)), pl.BlockSpec((ROW0, D), lambda i: (jnp.maximum(i - 1, 0), 0))],
        out_specs=[pl.BlockSpec((ROW0, D), lambda i: (i, 0)), pl.BlockSpec((8, LANES), lambda i: (0, 0))],
        out_shape=[jax.ShapeDtypeStruct((lp, D), F32), jax.ShapeDtypeStruct((8, LANES), F32)],
        compiler_params=_params(("arbitrary",)),
    )(h, target)


def _adamw(w, g, m, v, *, name):
    shape = w.shape
    c = shape[-1]
    r = w.size // c
    w2, g2, m2, v2 = (t.reshape(r, c) for t in (w, g, m, v))
    tr = _tile(r, max(8, (1 << 19) // c), 8)

    def body(w_ref, g_ref, m_ref, v_ref, d_ref, nm_ref, nv_ref):
        gv = g_ref[...]
        nm = ADAM_B1 * m_ref[...] + (1.0 - ADAM_B1) * gv
        nv = ADAM_B2 * v_ref[...] + (1.0 - ADAM_B2) * (gv * gv)
        m_hat = nm / (1.0 - ADAM_B1 ** ADAM_STEP)
        v_hat = nv / (1.0 - ADAM_B2 ** ADAM_STEP)
        d_ref[...] = -ADAM_LR * (m_hat / (jnp.sqrt(v_hat) + ADAM_EPS) + ADAM_WD * w_ref[...])
        nm_ref[...] = nm
        nv_ref[...] = nv

    spec = pl.BlockSpec((tr, c), lambda i: (i, 0))
    outs = pl.pallas_call(
        body, name=name, grid=(r // tr,), in_specs=[spec] * 4, out_specs=[spec] * 3,
        out_shape=[jax.ShapeDtypeStruct((r, c), F32)] * 3, compiler_params=_params(("parallel",)),
    )(w2, g2, m2, v2)
    return tuple(o.reshape(shape) for o in outs)


def _adam_math(w, g, m, v):
    nm = ADAM_B1 * m + (1.0 - ADAM_B1) * g
    nv = ADAM_B2 * v + (1.0 - ADAM_B2) * (g * g)
    m_hat = nm / (1.0 - ADAM_B1 ** ADAM_STEP)
    v_hat = nv / (1.0 - ADAM_B2 ** ADAM_STEP)
    return -ADAM_LR * (m_hat / (jnp.sqrt(v_hat) + ADAM_EPS) + ADAM_WD * w), nm, nv


def _adamw_packed(w, gred, m, v, *, row_off, transposed, name):
    nl, a, b = w.shape
    if transposed:
        ta = _tile(a, 256)
        wspec = pl.BlockSpec((1, ta, b), lambda l, r: (l, r, 0))
        gspec = pl.BlockSpec((b, ta), lambda l, r: (row_off // b + l, r))
        grid = (nl, a // ta)
    else:
        wspec = pl.BlockSpec((1, a, b), lambda l, r: (l, 0, 0))
        gspec = pl.BlockSpec((a, b), lambda l, r: (row_off // a + l, 0))
        grid = (nl, 1)

    def body(w_ref, g_ref, m_ref, v_ref, go_ref, d_ref, nm_ref, nv_ref):
        g = g_ref[...].T if transposed else g_ref[...]
        d, nm, nv = _adam_math(w_ref[0], g, m_ref[0], v_ref[0])
        go_ref[0], d_ref[0], nm_ref[0], nv_ref[0] = g, d, nm, nv

    return pl.pallas_call(
        body, name=name, grid=grid, in_specs=[wspec, gspec, wspec, wspec], out_specs=[wspec] * 4,
        out_shape=[jax.ShapeDtypeStruct(w.shape, F32)] * 4, compiler_params=_params(("parallel", "parallel")),
    )(w, gred, m, v)


FOX_AUG = FOX_H * LANES
L_C = 64
L_K = 67
L_LSE = 70
PAD_KEY = -30000.0
FOX_TQ = 384


def _head_sel(n_heads, width, lanes=LANES):
    r, c = _iota((n_heads * width, lanes), 0), _iota((n_heads * width, lanes), 1)
    down = (r // width == c).astype(BF16)
    r2, c2 = _iota((lanes, n_heads * width), 0), _iota((lanes, n_heads * width), 1)
    up = (c2 // width == r2).astype(BF16)
    return down, up


def _place(lane0):
    r, c = _iota((LANES, FOX_AUG), 0), _iota((LANES, FOX_AUG), 1)
    return [((c // LANES == r) & (c % LANES == lane0 + m)).astype(BF16) for m in range(3)]


def _placed(x, lane0):
    pcs = _split3(x)
    mats = _place(lane0)
    return nn(pcs[0], mats[0]) + nn(pcs[1], mats[1]) + nn(pcs[2], mats[2])


def _ones_at(rows, lanes):
    c = _iota((rows, FOX_AUG), 1) % LANES
    m = c == lanes[0]
    for l in lanes[1:]:
        m = m | (c == l)
    return m.astype(F32)


def _spread(x, extras, out_ref):
    rows = x.shape[0]
    left = _iota((rows, LANES), 1) < FOX_DH
    for p in range(FOX_H // 2):
        slab = x[:, p * LANES:(p + 1) * LANES]
        a = jnp.where(left, slab, extras[:, 2 * p * LANES:(2 * p + 1) * LANES])
        b = jnp.where(left, pltpu.roll(slab, FOX_DH, 1), extras[:, (2 * p + 1) * LANES:(2 * p + 2) * LANES])
        out_ref[:, 2 * p * LANES:(2 * p + 1) * LANES] = a.astype(BF16)
        out_ref[:, (2 * p + 1) * LANES:(2 * p + 2) * LANES] = b.astype(BF16)


def _fox_prep(proj, b_f, q_gain, k_gain):
    lp = proj.shape[0]
    nb = lp // LANES

    def body(p_ref, bf_ref, qg_ref, kg_ref, q_ref, k_ref, v_ref, carry):
        i = pl.program_id(0)

        @pl.when(i == 0)
        def _():
            carry[...] = jnp.zeros_like(carry)

        down, up = _head_sel(FOX_H, FOX_DH)

        def normed(x, gain):
            ms = _sel_r(x * x, down) * (1.0 / FOX_DH)
            r = _sel_r(lax.rsqrt(ms + EPS), up)
            return x * r * gain

        lane = _iota((LANES, LANES), 1)
        lf = jnp.where(lane < FOX_H, _log_sigmoid(p_ref[:, 4 * D:4 * D + LANES] + bf_ref[...]), 0.0)
        c = _sel_l(_tri(LANES).astype(BF16), lf) + carry[0:1, :]
        carry[...] = jnp.broadcast_to(c[LANES - 1:LANES, :], carry.shape)
        q_extra = _placed(c, L_C) + _ones_at(LANES, (L_K, L_K + 1, L_K + 2))
        row = i * LANES + _iota((LANES, FOX_AUG), 0)
        lane_a = _iota((LANES, FOX_AUG), 1) % LANES
        k_extra = -_placed(c, L_K) + _ones_at(LANES, (L_C, L_C + 1, L_C + 2, L_LSE, L_LSE + 1, L_LSE + 2))
        pad_val = jnp.where(lane_a == L_K, PAD_KEY, 0.0)
        k_extra = jnp.where((row < META0) & (lane_a >= L_K) & (lane_a < L_K + 3), pad_val, k_extra)
        v_extra = _ones_at(LANES, (L_C, L_C + 1, L_C + 2))
        _spread(normed(p_ref[:, 0:D], qg_ref[...]) * (FOX_DH ** -0.5), q_extra, q_ref)
        _spread(normed(p_ref[:, D:2 * D], kg_ref[...]), k_extra, k_ref)
        _spread(p_ref[:, 2 * D:3 * D], v_extra, v_ref)

    row = pl.BlockSpec((1, D), lambda i: (0, 0))
    aug = pl.BlockSpec((LANES, FOX_AUG), lambda i: (i, 0))
    return pl.pallas_call(
        body, name="fox_prep", grid=(nb,),
        in_specs=[pl.BlockSpec((LANES, FOX_INP), lambda i: (i, 0)), pl.BlockSpec((1, LANES), lambda i: (0, 0)), row, row],
        out_specs=[aug] * 3, out_shape=[jax.ShapeDtypeStruct((lp, FOX_AUG), BF16)] * 3,
        scratch_shapes=[pltpu.VMEM((8, LANES), F32)],
        compiler_params=_params(("arbitrary",)),
    )(proj, jnp.pad(b_f, (0, LANES - FOX_H)).reshape(1, LANES), jnp.tile(q_gain, FOX_H).reshape(1, D),
      jnp.tile(k_gain, FOX_H).reshape(1, D))


def _fox_attn_fwd(qa, ka, va, proj):
    lp = qa.shape[0]
    tq = _tile(lp, FOX_TQ)
    nq = lp // tq

    def body(q_ref, k_ref, v_ref, gate_ref, o_ref, og_ref, lse_ref):
        i = pl.program_id(1)
        causal = _iota((tq, tq), 1) <= _iota((tq, tq), 0)
        qs = [q_ref[:, hh * LANES:(hh + 1) * LANES] for hh in range(2)]

        def block(j, carry, diag):
            off = pl.multiple_of(j * tq, tq)
            out = []
            for hh in range(2):
                m, acc = carry[hh]
                k = k_ref[pl.ds(off, tq), hh * LANES:(hh + 1) * LANES]
                v = v_ref[pl.ds(off, tq), hh * LANES:(hh + 1) * LANES]
                s = nt(qs[hh], k)
                if diag:
                    s = jnp.where(causal, s, -1e30)
                m2 = jnp.maximum(m, jnp.max(s, axis=-1, keepdims=True))
                p = jnp.exp(s - m2)
                p_hi = p.astype(BF16)
                p_lo = (p - p_hi.astype(F32)).astype(BF16)
                out.append((m2, jnp.exp(m - m2) * acc + nn(p_hi, v) + nn(p_lo, v)))
            return tuple(out)

        init = tuple((jnp.full((tq, 1), -1e30, F32), jnp.zeros((tq, LANES), F32)) for _ in range(2))
        carry = lax.fori_loop(0, i, lambda j, c: block(j, c, False), init)
        carry = block(i, carry, True)
        outs, lses = [], []
        for hh in range(2):
            m, acc = carry[hh]
            l = acc[:, L_C:L_C + 1]
            outs.append(acc / l)
            lses.append(jnp.broadcast_to(m + jnp.log(l), (tq, LANES)))
        left = _iota((tq, LANES), 1) < FOX_DH
        o = jnp.where(left, outs[0], pltpu.roll(outs[1], FOX_DH, 1))
        o_ref[...] = o
        og_ref[...] = (o * _sigmoid(gate_ref[...])).astype(BF16)
        lse_ref[...] = jnp.where(left, lses[0], lses[1])

    qspec = pl.BlockSpec((tq, 2 * LANES), lambda p, i: (i, p))
    kspec = pl.BlockSpec((lp, 2 * LANES), lambda p, i: (0, p))
    ospec = pl.BlockSpec((tq, LANES), lambda p, i: (i, p))
    return pl.pallas_call(
        body, name="fox_attn_fwd", grid=(FOX_H // 2, nq),
        in_specs=[qspec, kspec, kspec, pl.BlockSpec((tq, LANES), lambda p, i: (i, 3 * D // LANES + p))],
        out_specs=[ospec] * 3,
        out_shape=[jax.ShapeDtypeStruct((lp, D), F32), jax.ShapeDtypeStruct((lp, D), BF16), jax.ShapeDtypeStruct((lp, D), F32)],
        compiler_params=_params(("parallel", "arbitrary")),
    )(qa, ka, va, proj)


def _fox_gate_bwd(dog, o, proj, lse, qa):
    lp = o.shape[0]
    tr = LANES

    def body(d_ref, o_ref, g_ref, lse_ref, q_ref, do_ref, q2_ref, dgate_ref):
        down, _ = _head_sel(FOX_H, FOX_DH)
        sg = _sigmoid(g_ref[...])
        dv, ov = d_ref[...], o_ref[...]
        do = (dv * sg).astype(BF16).astype(F32)
        dgate_ref[...] = dv * ov * sg * (1.0 - sg)
        delta = _sel_r(do * ov, down)
        _spread(do, -_placed(delta, L_C), do_ref)
        r_, c_ = _iota((D, LANES), 0), _iota((D, LANES), 1)
        lse_c = _sel_r(lse_ref[...], (r_ == c_ * FOX_DH).astype(BF16))
        q2_ref[...] = (q_ref[...].astype(F32) - _placed(lse_c, L_LSE)).astype(BF16)

    spec = pl.BlockSpec((tr, D), lambda i: (i, 0))
    aug = pl.BlockSpec((tr, FOX_AUG), lambda i: (i, 0))
    return pl.pallas_call(
        body, name="fox_gate_bwd", grid=(lp // tr,),
        in_specs=[spec, spec, pl.BlockSpec((tr, D), lambda i: (i, 3)), spec, aug], out_specs=[aug, aug, spec],
        out_shape=[jax.ShapeDtypeStruct((lp, FOX_AUG), BF16), jax.ShapeDtypeStruct((lp, FOX_AUG), BF16),
                   jax.ShapeDtypeStruct((lp, D), F32)],
        compiler_params=_params(("parallel",)),
    )(dog, o, proj, lse, qa)


def _fox_attn_bwd(q2, ka, va, doa):
    lp = q2.shape[0]
    t = _tile(lp, FOX_TQ)
    nb = lp // t

    def body(q_ref, k_ref, v_ref, do_ref, dq_ref, dk_ref, dv_ref, dc_ref, dq_acc, dk_acc, dv_acc, dc_acc):
        j = pl.program_id(1)

        @pl.when(j == 0)
        def _():
            dq_acc[...] = jnp.zeros_like(dq_acc)

        causal = _iota((t, t), 1) <= _iota((t, t), 0)
        for hh in range(2):
            k = k_ref[:, hh * LANES:(hh + 1) * LANES]
            v = v_ref[:, hh * LANES:(hh + 1) * LANES]
            dk_acc[...] = jnp.zeros_like(dk_acc)
            dv_acc[...] = jnp.zeros_like(dv_acc)
            dc_acc[...] = jnp.zeros_like(dc_acc)

            def block(i, diag):
                off = pl.multiple_of(i * t, t)
                q = q_ref[pl.ds(off, t), hh * LANES:(hh + 1) * LANES]
                do = do_ref[pl.ds(off, t), hh * LANES:(hh + 1) * LANES]
                s = nt(q, k)
                if diag:
                    s = jnp.where(causal, s, -1e30)
                p = jnp.exp(s)
                ds = p * nt(do, v)
                dc_acc[...] += jnp.sum(ds, axis=0, keepdims=True)
                dsb = ds.astype(BF16)
                dv_acc[...] += tn(p.astype(BF16), do)
                dk_acc[...] += tn(dsb, q)
                dq_acc[hh, pl.ds(off, t), :] += nn(dsb, k)

            block(j, True)

            def step(i, c):
                block(i, False)
                return c

            lax.fori_loop(j + 1, nb, step, 0)
            left = _iota((t, LANES), 1) < FOX_DH
            if hh == 0:
                dk_ref[...] = dk_acc[...]
                dv_ref[...] = dv_acc[...]
            else:
                dk_ref[...] = jnp.where(left, dk_ref[...], pltpu.roll(dk_acc[...], FOX_DH, 1))
                dv_ref[...] = jnp.where(left, dv_ref[...], pltpu.roll(dv_acc[...], FOX_DH, 1))
            dc_ref[hh] = jnp.broadcast_to(-dc_acc[...], (8, t))

        @pl.when(j == nb - 1)
        def _():
            left = _iota((lp, LANES), 1) < FOX_DH
            dq_ref[...] = jnp.where(left, dq_acc[0], pltpu.roll(dq_acc[1], FOX_DH, 1))

    full = pl.BlockSpec((lp, 2 * LANES), lambda p, j: (0, p))
    kblk = pl.BlockSpec((t, 2 * LANES), lambda p, j: (j, p))
    oblk = pl.BlockSpec((t, LANES), lambda p, j: (j, p))
    return pl.pallas_call(
        body, name="fox_attn_bwd", grid=(FOX_H // 2, nb),
        in_specs=[full, kblk, kblk, full],
        out_specs=[pl.BlockSpec((lp, LANES), lambda p, j: (0, p)), oblk, oblk, pl.BlockSpec((2, 8, t), lambda p, j: (p, 0, j))],
        out_shape=[jax.ShapeDtypeStruct((lp, D), F32)] * 3 + [jax.ShapeDtypeStruct((FOX_H, 8, lp), F32)],
        scratch_shapes=[pltpu.VMEM((2, lp, LANES), F32), pltpu.VMEM((t, LANES), F32), pltpu.VMEM((t, LANES), F32),
                        pltpu.VMEM((1, t), F32)],
        compiler_params=_params(("parallel", "arbitrary")),
    )(q2, ka, va, doa)


def _fox_prep_bwd(proj, b_f, q_gain, k_gain, dqn, dkn, dv, dgate, dct):
    lp = proj.shape[0]
    nb = lp // LANES

    def body(p_ref, bf_ref, qg_ref, kg_ref, dq_ref, dk_ref, dv_ref, dg_ref, dc_ref,
             dp_ref, dqg_ref, dkg_ref, dbf_ref, carry):
        i = pl.program_id(0)

        @pl.when(i == 0)
        def _():
            carry[...] = jnp.zeros_like(carry)
            dqg_ref[...] = jnp.zeros_like(dqg_ref)
            dkg_ref[...] = jnp.zeros_like(dkg_ref)
            dbf_ref[...] = jnp.zeros_like(dbf_ref)

        down, up = _head_sel(FOX_H, FOX_DH)

        def norm_bwd(x, gain, dy, scale, dgain_ref):
            ms = _sel_r(x * x, down) * (1.0 / FOX_DH)
            r = _sel_r(lax.rsqrt(ms + EPS), up)
            u = dy * gain * scale
            mean_xu = _sel_r(_sel_r(x * u, down) * (1.0 / FOX_DH), up)
            dgain_ref[...] += jnp.sum(dy * scale * x * r, axis=0, keepdims=True)
            return r * u - x * (r * r * r) * mean_xu

        dp_ref[:, 0:D] = norm_bwd(p_ref[:, 0:D], qg_ref[...], dq_ref[...], FOX_DH ** -0.5, dqg_ref).astype(BF16)
        dp_ref[:, D:2 * D] = norm_bwd(p_ref[:, D:2 * D], kg_ref[...], dk_ref[...], 1.0, dkg_ref).astype(BF16)
        dp_ref[:, 2 * D:3 * D] = dv_ref[...].astype(BF16)
        dp_ref[:, 3 * D:4 * D] = dg_ref[...].astype(BF16)
        rows = jnp.concatenate([dc_ref[h, 0:1, :] for h in range(FOX_H)] + [jnp.zeros((LANES - FOX_H, LANES), F32)], axis=0)
        dlf = _sel_l(_tri(LANES, upper=True).astype(BF16), rows.T) + carry[0:1, :]
        carry[...] = jnp.broadcast_to(dlf[0:1, :], carry.shape)
        lane = _iota((LANES, LANES), 1)
        z = p_ref[:, 4 * D:4 * D + LANES] + bf_ref[...]
        df = jnp.where(lane < FOX_H, dlf * _sigmoid(-z), 0.0)
        dp_ref[:, 4 * D:4 * D + LANES] = df.astype(BF16)
        dbf_ref[...] += jnp.sum(df, axis=0, keepdims=True)

    rev = lambda i: (nb - 1 - i, 0)
    blk = pl.BlockSpec((LANES, D), rev)
    row = pl.BlockSpec((1, D), lambda i: (0, 0))
    row128 = pl.BlockSpec((1, LANES), lambda i: (0, 0))
    return pl.pallas_call(
        body, name="fox_prep_bwd", grid=(nb,),
        in_specs=[pl.BlockSpec((LANES, FOX_INP), rev), row128, row, row, blk, blk, blk, blk,
                  pl.BlockSpec((FOX_H, 8, LANES), lambda i: (0, 0, nb - 1 - i))],
        out_specs=[pl.BlockSpec((LANES, FOX_INP), rev), row, row, row128],
        out_shape=[jax.ShapeDtypeStruct((lp, FOX_INP), BF16), jax.ShapeDtypeStruct((1, D), F32),
                   jax.ShapeDtypeStruct((1, D), F32), jax.ShapeDtypeStruct((1, LANES), F32)],
        scratch_shapes=[pltpu.VMEM((8, LANES), F32)],
        compiler_params=_params(("arbitrary",)),
    )(proj, jnp.pad(b_f, (0, LANES - FOX_H)).reshape(1, LANES), jnp.tile(q_gain, FOX_H).reshape(1, D),
      jnp.tile(k_gain, FOX_H).reshape(1, D), dqn, dkn, dv, dgate, dct)


def _gla_gates(p_ref, wa_ref, ba_ref):
    a_lr = p_ref[:, 3072:3072 + LANES]
    z = nn(a_lr.astype(BF16), wa_ref[...].astype(BF16)) + ba_ref[...]
    g = _log_sigmoid(z) * (1.0 / GLA_NORM)
    b = _sel_l(_tri(CHUNK).astype(BF16), g)
    return a_lr, z, b


def _gla_head_fwd(q, k, v, b, st0):
    eb = jnp.exp(b)
    bl = b[CHUNK - 1:CHUNK, :]
    qe, ke, kd = q * eb, k * jnp.exp(-b), k * jnp.exp(bl - b)
    a = jnp.where(_tri(CHUNK), nt(qe, ke), 0.0)
    o = nn(a, v) + nt(qe, st0)
    st1 = st0 * jnp.exp(bl) + tn(v, kd)
    return o, st1, (qe, ke, kd, a, bl)


def _gla_fwd(proj, w_alpha2, b_alpha, o_gain):
    lp = proj.shape[0]
    nc = lp // CHUNK

    def body(p_ref, wa_ref, ba_ref, og_ref, o_ref, y_ref, s_ref, st):
        @pl.when(pl.program_id(0) == 0)
        def _():
            st[...] = jnp.zeros_like(st)

        _, _, b = _gla_gates(p_ref, wa_ref, ba_ref)
        for h in range(GLA_H):
            q = p_ref[:, h * GLA_DK:(h + 1) * GLA_DK] * (GLA_DK ** -0.5)
            k = p_ref[:, GLA_QK + h * GLA_DK:GLA_QK + (h + 1) * GLA_DK]
            v = p_ref[:, 2 * GLA_QK + h * GLA_DV:2 * GLA_QK + (h + 1) * GLA_DV]
            r = p_ref[:, 2 * GLA_QK + GLA_V + h * GLA_DV:2 * GLA_QK + GLA_V + (h + 1) * GLA_DV]
            st0 = st[h]
            s_ref[0, h] = st0
            o, st1, _ = _gla_head_fwd(q, k, v, b[:, h * GLA_DK:(h + 1) * GLA_DK], st0)
            st[h] = st1
            o_ref[:, h * GLA_DV:(h + 1) * GLA_DV] = o
            rs = lax.rsqrt(jnp.mean(o * o, axis=-1, keepdims=True) + EPS)
            y_ref[:, h * GLA_DV:(h + 1) * GLA_DV] = (o * rs * og_ref[...] * _silu(r)).astype(BF16)

    blk = pl.BlockSpec((CHUNK, D), lambda i: (i, 0))
    return pl.pallas_call(
        body, name="gla_fwd", grid=(nc,),
        in_specs=[pl.BlockSpec((CHUNK, GLA_INP), lambda i: (i, 0)), pl.BlockSpec((LANES, GLA_QK), lambda i: (0, 0)),
                  pl.BlockSpec((1, GLA_QK), lambda i: (0, 0)), pl.BlockSpec((1, GLA_DV), lambda i: (0, 0))],
        out_specs=[blk, blk, pl.BlockSpec((1, GLA_H, GLA_DV, GLA_DK), lambda i: (i, 0, 0, 0))],
        out_shape=[jax.ShapeDtypeStruct((lp, D), F32), jax.ShapeDtypeStruct((lp, D), BF16),
                   jax.ShapeDtypeStruct((nc, GLA_H, GLA_DV, GLA_DK), F32)],
        scratch_shapes=[pltpu.VMEM((GLA_H, GLA_DV, GLA_DK), F32)],
        compiler_params=_params(("arbitrary",)),
    )(proj, jnp.pad(w_alpha2, ((0, LANES - GLA_RANK), (0, 0))), b_alpha.reshape(1, GLA_QK), o_gain.reshape(1, GLA_DV))


def _gla_bwd(proj, w_alpha2, b_alpha, o_gain, o, states, dy):
    lp = proj.shape[0]
    nc = lp // CHUNK

    def body(p_ref, wa_ref, ba_ref, og_ref, o_ref, s_ref, dy_ref, dp_ref, dwa_ref, dba_ref, dog_ref, dst):
        @pl.when(pl.program_id(0) == 0)
        def _():
            dst[...] = jnp.zeros_like(dst)
            dwa_ref[...] = jnp.zeros_like(dwa_ref)
            dba_ref[...] = jnp.zeros_like(dba_ref)
            dog_ref[...] = jnp.zeros_like(dog_ref)

        a_lr, z, b_all = _gla_gates(p_ref, wa_ref, ba_ref)
        last_row = _iota((CHUNK, GLA_DK), 0) == CHUNK - 1
        rev = _tri(CHUNK, upper=True).astype(BF16)
        dg_parts = []
        for h in range(GLA_H):
            scale = GLA_DK ** -0.5
            q = p_ref[:, h * GLA_DK:(h + 1) * GLA_DK] * scale
            k = p_ref[:, GLA_QK + h * GLA_DK:GLA_QK + (h + 1) * GLA_DK]
            v = p_ref[:, 2 * GLA_QK + h * GLA_DV:2 * GLA_QK + (h + 1) * GLA_DV]
            r = p_ref[:, 2 * GLA_QK + GLA_V + h * GLA_DV:2 * GLA_QK + GLA_V + (h + 1) * GLA_DV]
            b = b_all[:, h * GLA_DK:(h + 1) * GLA_DK]
            st0 = s_ref[0, h]
            dst1 = dst[h]
            ov = o_ref[:, h * GLA_DV:(h + 1) * GLA_DV]
            dyv = dy_ref[:, h * GLA_DV:(h + 1) * GLA_DV]
            rs = lax.rsqrt(jnp.mean(ov * ov, axis=-1, keepdims=True) + EPS)
            on = ov * rs
            dr = dyv * on * og_ref[...] * _dsilu(r)
            don = dyv * _silu(r)
            dog_ref[...] += jnp.sum(don * on, axis=0, keepdims=True)
            u = don * og_ref[...]
            do = rs * u - ov * (rs * rs * rs) * jnp.mean(ov * u, axis=-1, keepdims=True)
            eb = jnp.exp(b)
            _, _, (qe, ke, kd, a, bl) = _gla_head_fwd(q, k, v, b, st0)
            da = jnp.where(_tri(CHUNK), nt(do, v), 0.0)
            dkd = nn(v, dst1)
            dvv = tn(a, do) + nt(kd, dst1)
            dqe = nn(da, ke) + nn(do, st0)
            dke = tn(da, qe)
            ebl = jnp.exp(bl)
            dst[h] = dst1 * ebl + tn(do, qe)
            db = dqe * qe - dke * ke - dkd * kd
            db_last = jnp.sum(dkd * kd, axis=0, keepdims=True) + jnp.sum(dst1 * st0, axis=0, keepdims=True) * ebl
            db = db + jnp.where(last_row, db_last, 0.0)
            dg_parts.append(_sel_l(rev, db))
            dp_ref[:, h * GLA_DK:(h + 1) * GLA_DK] = (dqe * eb * scale).astype(BF16)
            dp_ref[:, GLA_QK + h * GLA_DK:GLA_QK + (h + 1) * GLA_DK] = (dke * jnp.exp(-b) + dkd * jnp.exp(bl - b)).astype(BF16)
            dp_ref[:, 2 * GLA_QK + h * GLA_DV:2 * GLA_QK + (h + 1) * GLA_DV] = dvv.astype(BF16)
            dp_ref[:, 2 * GLA_QK + GLA_V + h * GLA_DV:2 * GLA_QK + GLA_V + (h + 1) * GLA_DV] = dr.astype(BF16)
        dg = jnp.concatenate(dg_parts, axis=1)
        dz = dg * (1.0 / GLA_NORM) * _sigmoid(-z)
        dzb = dz.astype(BF16)
        dp_ref[:, 3072:3072 + LANES] = nt(dzb, wa_ref[...].astype(BF16)).astype(BF16)
        dwa_ref[...] += tn(a_lr.astype(BF16), dzb)
        dba_ref[...] += jnp.sum(dz, axis=0, keepdims=True)

    rv = lambda i: (nc - 1 - i, 0)
    blk = pl.BlockSpec((CHUNK, D), rv)
    fixed = lambda r, c: pl.BlockSpec((r, c), lambda i: (0, 0))
    return pl.pallas_call(
        body, name="gla_bwd", grid=(nc,),
        in_specs=[pl.BlockSpec((CHUNK, GLA_INP), rv), fixed(LANES, GLA_QK), fixed(1, GLA_QK), fixed(1, GLA_DV), blk,
                  pl.BlockSpec((1, GLA_H, GLA_DV, GLA_DK), lambda i: (nc - 1 - i, 0, 0, 0)), blk],
        out_specs=[pl.BlockSpec((CHUNK, GLA_INP), rv), fixed(LANES, GLA_QK), fixed(1, GLA_QK), fixed(1, GLA_DV)],
        out_shape=[jax.ShapeDtypeStruct((lp, GLA_INP), BF16), jax.ShapeDtypeStruct((LANES, GLA_QK), F32),
                   jax.ShapeDtypeStruct((1, GLA_QK), F32), jax.ShapeDtypeStruct((1, GLA_DV), F32)],
        scratch_shapes=[pltpu.VMEM((GLA_H, GLA_DV, GLA_DK), F32)],
        compiler_params=_params(("arbitrary",)),
    )(proj, jnp.pad(w_alpha2, ((0, LANES - GLA_RANK), (0, 0))), b_alpha.reshape(1, GLA_QK), o_gain.reshape(1, GLA_DV),
      o, states, dy)


HI = lax.Precision.HIGHEST


def _gdn_pre(prev_ref, p_ref, cw_ref, al_ref, dt_ref):
    xc = jnp.concatenate([prev_ref[:, 0:GDN_CONV], p_ref[:, 0:GDN_CONV]], axis=0)
    shifted = [pltpu.roll(xc, 3 - j, 0)[CHUNK:, :] if j < 3 else xc[CHUNK:, :] for j in range(4)]
    conv = sum(shifted[j] * cw_ref[j:j + 1, :] for j in range(4))
    act = _silu(conv)
    slab = p_ref[:, 4096:4096 + LANES]
    lane = _iota((CHUNK, LANES), 1)
    zs = slab + dt_ref[...]
    g = jnp.where(lane < GDN_H, -jnp.exp(al_ref[...]) * _softplus(zs), 0.0)
    bs = _sel_l(_tri(CHUNK).astype(BF16), g)
    beta = _sigmoid(slab)
    return shifted, conv, act, slab, zs, g, bs, beta


def _l2n(x):
    r = lax.rsqrt(jnp.sum(x * x, axis=-1, keepdims=True) + EPS)
    return x * r, r


def _gdn_head_fwd(q, k, v, beta, bcol, brow, s0):
    ii, jj = _iota((CHUNK, CHUNK), 0), _iota((CHUNK, CHUNK), 1)
    diff = bcol - brow
    dm = jnp.where(ii >= jj, jnp.exp(jnp.where(ii >= jj, diff, 0.0)), 0.0)
    dstrict = jnp.where(ii > jj, dm, 0.0)
    eb = jnp.exp(bcol)
    bl = bcol[CHUNK - 1:CHUNK, :]
    kb, vb = k * beta, v * beta
    nmat = nt(kb, k) * dstrict
    eye = (ii == jj).astype(F32)
    x = eye - nmat
    pw = nn(nmat, nmat, precision=HI)
    for it in range(5):
        x = x + nn(x, pw, precision=HI)
        if it < 4:
            pw = nn(pw, pw, precision=HI)
    kbe = kb * eb
    u, w = nn(x, vb, precision=HI), nn(x, kbe, precision=HI)
    vn = u - nn(w, s0)
    pm = nt(q, k) * dm
    qe = q * eb
    o = nn(pm, vn) + nn(qe, s0)
    kd = k * jnp.exp(bl - bcol)
    s1 = s0 * jnp.exp(bl) + tn(kd, vn)
    return o, s1, (dm, dstrict, eb, bl, kb, vb, nmat, x, kbe, u, w, vn, pm, qe, kd)


def _gdn_heads(act, beta_slab, bs, h):
    qa = act[:, h * GDN_DK:(h + 1) * GDN_DK]
    ka = act[:, GDN_H * GDN_DK + h * GDN_DK:GDN_H * GDN_DK + (h + 1) * GDN_DK]
    v = act[:, 2 * GDN_H * GDN_DK + h * GDN_DV:2 * GDN_H * GDN_DK + (h + 1) * GDN_DV]
    return qa, ka, v, beta_slab[:, GDN_H + h:GDN_H + h + 1], bs[:, h:h + 1]


def _gdn_fwd(proj, conv_w, a_log, dt_bias, o_gain):
    lp = proj.shape[0]
    nc = lp // CHUNK

    def body(prev_ref, p_ref, cw_ref, al_ref, dt_ref, og_ref, o_ref, y_ref, s_ref, st):
        @pl.when(pl.program_id(0) == 0)
        def _():
            st[...] = jnp.zeros_like(st)

        _, _, act, _, _, _, bs, beta = _gdn_pre(prev_ref, p_ref, cw_ref, al_ref, dt_ref)
        bst = bs.T
        for h in range(GDN_H):
            qa, ka, v, bet, bcol = _gdn_heads(act, beta, bs, h)
            q = _l2n(qa)[0] * (GDN_DK ** -0.5)
            k = _l2n(ka)[0]
            s0 = st[h]
            s_ref[0, h] = s0
            o, s1, _ = _gdn_head_fwd(q, k, v, bet, bcol, bst[h:h + 1, :], s0)
            st[h] = s1
            o_ref[:, h * GDN_DV:(h + 1) * GDN_DV] = o
            rs = lax.rsqrt(jnp.mean(o * o, axis=-1, keepdims=True) + EPS)
            gate = p_ref[:, GDN_CONV + h * GDN_DV:GDN_CONV + (h + 1) * GDN_DV]
            y_ref[:, h * GDN_DV:(h + 1) * GDN_DV] = (o * rs * og_ref[...] * _silu(gate)).astype(BF16)

    blk = pl.BlockSpec((CHUNK, D), lambda i: (i, 0))
    fixed = lambda r, c: pl.BlockSpec((r, c), lambda i: (0, 0))
    return pl.pallas_call(
        body, name="gdn_fwd", grid=(nc,),
        in_specs=[pl.BlockSpec((CHUNK, GDN_INP), lambda i: (jnp.maximum(i - 1, 0), 0)),
                  pl.BlockSpec((CHUNK, GDN_INP), lambda i: (i, 0)), fixed(8, GDN_CONV), fixed(1, LANES), fixed(1, LANES),
                  fixed(1, GDN_DV)],
        out_specs=[blk, blk, pl.BlockSpec((1, GDN_H, GDN_DK, GDN_DV), lambda i: (i, 0, 0, 0))],
        out_shape=[jax.ShapeDtypeStruct((lp, D), F32), jax.ShapeDtypeStruct((lp, D), BF16),
                   jax.ShapeDtypeStruct((nc, GDN_H, GDN_DK, GDN_DV), F32)],
        scratch_shapes=[pltpu.VMEM((GDN_H, GDN_DK, GDN_DV), F32)],
        compiler_params=_params(("arbitrary",)),
    )(proj, proj, jnp.pad(conv_w.reshape(4, GDN_CONV), ((0, 4), (0, 0))), jnp.pad(a_log, (0, LANES - GDN_H)).reshape(1, LANES),
      jnp.pad(dt_bias, (0, LANES - GDN_H)).reshape(1, LANES), o_gain.reshape(1, GDN_DV))


def _gdn_bwd(proj, conv_w, a_log, dt_bias, o_gain, o, states, dy):
    lp = proj.shape[0]
    nc = lp // CHUNK

    def body(prev_ref, p_ref, cw_ref, al_ref, dt_ref, og_ref, o_ref, s_ref, dy_ref,
             dp_ref, dcw_ref, dal_ref, ddt_ref, dog_ref, dst, dconv_next):
        @pl.when(pl.program_id(0) == 0)
        def _():
            dst[...] = jnp.zeros_like(dst)
            dconv_next[...] = jnp.zeros_like(dconv_next)
            dcw_ref[...] = jnp.zeros_like(dcw_ref)
            dal_ref[...] = jnp.zeros_like(dal_ref)
            ddt_ref[...] = jnp.zeros_like(ddt_ref)
            dog_ref[...] = jnp.zeros_like(dog_ref)

        shifted, conv, act, slab, zs, g, bs, beta = _gdn_pre(prev_ref, p_ref, cw_ref, al_ref, dt_ref)
        bst = bs.T
        lane = _iota((CHUNK, LANES), 1)
        ones = jnp.ones((CHUNK, LANES), F32)
        db_slab = jnp.zeros((CHUNK, LANES), F32)
        dbeta_slab = jnp.zeros((CHUNK, LANES), F32)
        last_row = _iota((CHUNK, 1), 0) == CHUNK - 1
        dact_q, dact_k, dact_v = [], [], []
        for h in range(GDN_H):
            qa, ka, v, bet, bcol = _gdn_heads(act, beta, bs, h)
            qn_, rq = _l2n(qa)
            k, rk = _l2n(ka)
            scale = GDN_DK ** -0.5
            q = qn_ * scale
            s0 = s_ref[0, h]
            ds1 = dst[h]
            ov = o_ref[:, h * GDN_DV:(h + 1) * GDN_DV]
            dyv = dy_ref[:, h * GDN_DV:(h + 1) * GDN_DV]
            gate = p_ref[:, GDN_CONV + h * GDN_DV:GDN_CONV + (h + 1) * GDN_DV]
            rs = lax.rsqrt(jnp.mean(ov * ov, axis=-1, keepdims=True) + EPS)
            on = ov * rs
            dp_ref[:, GDN_CONV + h * GDN_DV:GDN_CONV + (h + 1) * GDN_DV] = (dyv * on * og_ref[...] * _dsilu(gate)).astype(BF16)
            don = dyv * _silu(gate)
            dog_ref[...] += jnp.sum(don * on, axis=0, keepdims=True)
            uu = don * og_ref[...]
            do = rs * uu - ov * (rs * rs * rs) * jnp.mean(ov * uu, axis=-1, keepdims=True)
            _, _, (dm, dstrict, eb, bl, kb, vb, nmat, tinv, kbe, u, w, vn, pm, qe, kd) = _gdn_head_fwd(
                q, k, v, bet, bcol, bst[h:h + 1, :], s0)
            ebl = jnp.exp(bl)
            dvn = tn(pm, do) + nn(kd, ds1)
            dpr = nt(do, vn)
            dqk, gp = dpr * dm, dpr * pm
            dqe = nt(do, s0)
            dkd = nt(vn, ds1)
            dst[h] = ds1 * ebl + tn(qe, do) - tn(w, dvn)
            du_ = tn(tinv, dvn, precision=HI)
            dw_ = tn(tinv, -nt(dvn, s0), precision=HI)
            dn = -(nt(du_, u) + nt(dw_, w))
            dkk, gn = dn * dstrict, dn * nmat
            dkb = nn(dkk, k) + dw_ * eb
            dk = tn(dkk, kb) + tn(dqk, q) + dkd * jnp.exp(bl - bcol) + dkb * bet
            dq = nn(dqk, k) + dqe * eb
            dbeta = jnp.sum(dkb * k, axis=-1, keepdims=True) + jnp.sum(du_ * v, axis=-1, keepdims=True)
            dv_ = du_ * bet
            gsum = gp + gn
            colsum = tn(gsum, ones, precision=HI)[:, 0:1]
            skd = jnp.sum(dkd * kd, axis=-1, keepdims=True)
            db = (jnp.sum(gsum, axis=-1, keepdims=True) - colsum + jnp.sum(dqe * qe, axis=-1, keepdims=True)
                  + jnp.sum(dw_ * kbe, axis=-1, keepdims=True) - skd)
            db_last = jnp.sum(skd, axis=0, keepdims=True) + jnp.sum(ds1 * s0) * ebl
            db = db + jnp.where(last_row, db_last, 0.0)
            db_slab = db_slab + jnp.where(lane == h, db, 0.0)
            dbeta_slab = dbeta_slab + jnp.where(lane == GDN_H + h, dbeta, 0.0)
            dqn = dq * scale
            dact_q.append(rq * dqn - qa * (rq * rq * rq) * jnp.sum(qa * dqn, axis=-1, keepdims=True))
            dact_k.append(rk * dk - ka * (rk * rk * rk) * jnp.sum(ka * dk, axis=-1, keepdims=True))
            dact_v.append(dv_)
        dact = jnp.concatenate(dact_q + dact_k + dact_v, axis=1)
        dconv = dact * _dsilu(conv)
        for j in range(4):
            dcw_ref[j:j + 1, :] += jnp.sum(dconv * shifted[j], axis=0, keepdims=True)
        dcat = jnp.concatenate([dconv, dconv_next[...]], axis=0)
        dx = dconv * cw_ref[3:4, :]
        for j in range(3):
            dx = dx + pltpu.roll(dcat, 2 * CHUNK - (3 - j), 0)[:CHUNK, :] * cw_ref[j:j + 1, :]
        dconv_next[...] = dconv
        dp_ref[:, 0:GDN_CONV] = dx.astype(BF16)
        dg = _sel_l(_tri(CHUNK, upper=True).astype(BF16), db_slab)
        da = dg * (-jnp.exp(al_ref[...])) * _sigmoid(zs)
        da = jnp.where(lane < GDN_H, da, 0.0)
        dal_ref[...] += jnp.sum(dg * g, axis=0, keepdims=True)
        ddt_ref[...] += jnp.sum(da, axis=0, keepdims=True)
        dp_ref[:, 4096:4096 + LANES] = (da + dbeta_slab * beta * (1.0 - beta)).astype(BF16)

    rv = lambda i: (nc - 1 - i, 0)
    blk = pl.BlockSpec((CHUNK, D), rv)
    fixed = lambda r, c: pl.BlockSpec((r, c), lambda i: (0, 0))
    return pl.pallas_call(
        body, name="gdn_bwd", grid=(nc,),
        in_specs=[pl.BlockSpec((CHUNK, GDN_INP), lambda i: (jnp.maximum(nc - 2 - i, 0), 0)),
                  pl.BlockSpec((CHUNK, GDN_INP), rv), fixed(8, GDN_CONV), fixed(1, LANES), fixed(1, LANES), fixed(1, GDN_DV),
                  blk, pl.BlockSpec((1, GDN_H, GDN_DK, GDN_DV), lambda i: (nc - 1 - i, 0, 0, 0)), blk],
        out_specs=[pl.BlockSpec((CHUNK, GDN_INP), rv), fixed(8, GDN_CONV), fixed(1, LANES), fixed(1, LANES), fixed(1, GDN_DV)],
        out_shape=[jax.ShapeDtypeStruct((lp, GDN_INP), BF16), jax.ShapeDtypeStruct((8, GDN_CONV), F32),
                   jax.ShapeDtypeStruct((1, LANES), F32), jax.ShapeDtypeStruct((1, LANES), F32),
                   jax.ShapeDtypeStruct((1, GDN_DV), F32)],
        scratch_shapes=[pltpu.VMEM((GDN_H, GDN_DK, GDN_DV), F32), pltpu.VMEM((CHUNK, GDN_CONV), F32)],
        compiler_params=_params(("arbitrary",)),
    )(proj, proj, jnp.pad(conv_w.reshape(4, GDN_CONV), ((0, 4), (0, 0))), jnp.pad(a_log, (0, LANES - GDN_H)).reshape(1, LANES),
      jnp.pad(dt_bias, (0, LANES - GDN_H)).reshape(1, LANES), o_gain.reshape(1, GDN_DV), o, states, dy)


def _coords():
    return lax.axis_index("x"), lax.axis_index("y"), lax.axis_index("c")


def _other_chips(x, y):
    return [(1 - x, y, 2 * (1 - x) + y), (x, 1 - y, 2 * x + 1 - y), (1 - x, 1 - y, 2 * (1 - x) + 1 - y)]


def _gather8(v, *, reduce, name):
    r, c = v.shape

    def body(v_ref, out_ref, *scratch):
        if reduce:
            buf, send_sems, recv_sems = scratch
        else:
            buf = out_ref
            send_sems, recv_sems = scratch
        x, y, cc = _coords()
        me = 4 * x + 2 * y + cc
        buf[me] = v_ref[...]
        copies = []
        for k in range(1, 8):
            px, py, pc = x ^ (k >> 2), y ^ ((k >> 1) & 1), cc ^ (k & 1)
            copies.append(pltpu.make_async_remote_copy(
                src_ref=v_ref, dst_ref=buf.at[me], send_sem=send_sems.at[k - 1], recv_sem=recv_sems.at[k - 1],
                device_id=(px, py, pc), device_id_type=MESH))
        for cp in copies:
            cp.start()
        for k in range(1, 8):
            peer = (x ^ (k >> 2)) * 4 + (y ^ ((k >> 1) & 1)) * 2 + (cc ^ (k & 1))
            pltpu.make_async_remote_copy(
                src_ref=v_ref, dst_ref=buf.at[peer], send_sem=send_sems.at[k - 1], recv_sem=recv_sems.at[k - 1],
                device_id=(x, y, cc), device_id_type=MESH).wait_recv()
        for cp in copies:
            cp.wait_send()
        if reduce:
            acc = buf[0]
            for d in range(1, 8):
                acc = acc + buf[d]
            out_ref[...] = acc

    scratch = [pltpu.SemaphoreType.DMA((7,)), pltpu.SemaphoreType.DMA((7,))]
    if reduce:
        scratch = [pltpu.VMEM((8, r, c), F32)] + scratch
    return pl.pallas_call(
        body, name=name, in_specs=[VM], out_specs=VM,
        out_shape=jax.ShapeDtypeStruct((r, c) if reduce else (8, r, c), F32),
        scratch_shapes=scratch, compiler_params=_params(),
    )(v)


def _ag_weights(w):
    r, c = w.shape
    half = r // 2

    def body(w_ref, out_ref, send_sems, recv_sems, lsem):
        x, y, cc = _coords()
        p = 2 * x + y
        chips = _other_chips(x, y)

        def rows(chip, hf):
            return out_ref.at[chip, pl.ds(hf * half, half), :]

        mine = pltpu.make_async_copy(w_ref, out_ref.at[p], lsem)
        mine.start()
        first = [pltpu.make_async_remote_copy(
            src_ref=w_ref.at[pl.ds(cc * half, half), :], dst_ref=rows(p, cc), send_sem=send_sems.at[k],
            recv_sem=recv_sems.at[k], device_id=(cx, cy, cc), device_id_type=MESH) for k, (cx, cy, _) in enumerate(chips)]
        for cp in first:
            cp.start()
        passed = []
        for k, (_, _, blk) in enumerate(chips):
            pltpu.make_async_remote_copy(
                src_ref=rows(blk, cc), dst_ref=rows(blk, cc), send_sem=send_sems.at[k], recv_sem=recv_sems.at[k],
                device_id=(x, y, cc), device_id_type=MESH).wait_recv()
            fw = pltpu.make_async_remote_copy(
                src_ref=rows(blk, cc), dst_ref=rows(blk, cc), send_sem=send_sems.at[3 + k], recv_sem=recv_sems.at[3 + k],
                device_id=(x, y, 1 - cc), device_id_type=MESH)
            fw.start()
            passed.append(fw)
        for k, (_, _, blk) in enumerate(chips):
            pltpu.make_async_remote_copy(
                src_ref=rows(blk, 1 - cc), dst_ref=rows(blk, 1 - cc), send_sem=send_sems.at[3 + k], recv_sem=recv_sems.at[3 + k],
                device_id=(x, y, cc), device_id_type=MESH).wait_recv()
        for cp in first + passed:
            cp.wait_send()
        mine.wait()

    return pl.pallas_call(
        body, name="ag_weights", in_specs=[ANY], out_specs=ANY, out_shape=jax.ShapeDtypeStruct((4, r, c), w.dtype),
        scratch_shapes=[pltpu.SemaphoreType.DMA((6,)), pltpu.SemaphoreType.DMA((6,)), pltpu.SemaphoreType.DMA],
        compiler_params=_params(),
    )(w)


def _swap_halves(g, *, name):
    nb, r, c = g.shape
    half = r // 2

    def body(g_ref, out_ref, send_sem, recv_sem):
        x, y, cc = _coords()
        cp = pltpu.make_async_remote_copy(
            src_ref=g_ref.at[:, pl.ds((1 - cc) * half, half), :], dst_ref=out_ref, send_sem=send_sem, recv_sem=recv_sem,
            device_id=(x, y, 1 - cc), device_id_type=MESH)
        cp.start()
        cp.wait()

    return pl.pallas_call(
        body, name=name, in_specs=[ANY], out_specs=ANY, out_shape=jax.ShapeDtypeStruct((nb, half, c), g.dtype),
        scratch_shapes=[pltpu.SemaphoreType.DMA, pltpu.SemaphoreType.DMA], compiler_params=_params(),
    )(g)


def _my_half_index():
    return lax.axis_index("c").astype(jnp.int32).reshape(1)


def _add_halves(g, got):
    nb, r, c = g.shape
    half = r // 2
    tr = _tile(half, 512, 16)
    nt_ = half // tr

    def body(c_ref, a_ref, b_ref, o_ref):
        o_ref[...] = (a_ref[...].astype(F32) + b_ref[...].astype(F32)).astype(BF16)

    return pl.pallas_call(
        body, name="rs_add_sibling",
        grid_spec=pltpu.PrefetchScalarGridSpec(
            num_scalar_prefetch=1, grid=(nb, nt_),
            in_specs=[pl.BlockSpec((1, tr, c), lambda b, i, cr: (b, cr[0] * nt_ + i, 0)),
                      pl.BlockSpec((1, tr, c), lambda b, i, cr: (b, i, 0))],
            out_specs=pl.BlockSpec((1, tr, c), lambda b, i, cr: (b, i, 0))),
        out_shape=jax.ShapeDtypeStruct((nb, half, c), BF16), compiler_params=_params(("parallel", "parallel")),
    )(_my_half_index(), g, got)


def _scatter_chips(s):
    nb, hrows, c = s.shape

    def body(s_ref, out_ref, send_sems, recv_sems):
        x, y, cc = _coords()
        p = 2 * x + y
        chips = _other_chips(x, y)
        sends = [pltpu.make_async_remote_copy(
            src_ref=s_ref.at[blk], dst_ref=out_ref.at[k], send_sem=send_sems.at[k], recv_sem=recv_sems.at[k],
            device_id=(cx, cy, cc), device_id_type=MESH) for k, (cx, cy, blk) in enumerate(chips)]
        for cp in sends:
            cp.start()
        for k in range(3):
            pltpu.make_async_remote_copy(
                src_ref=s_ref.at[p], dst_ref=out_ref.at[k], send_sem=send_sems.at[k], recv_sem=recv_sems.at[k],
                device_id=(x, y, cc), device_id_type=MESH).wait_recv()
        for cp in sends:
            cp.wait_send()

    return pl.pallas_call(
        body, name="rs_scatter", in_specs=[ANY], out_specs=ANY, out_shape=jax.ShapeDtypeStruct((3, hrows, c), s.dtype),
        scratch_shapes=[pltpu.SemaphoreType.DMA((3,)), pltpu.SemaphoreType.DMA((3,))], compiler_params=_params(),
    )(s)


def _sum_chips(s, got):
    nb, hrows, c = s.shape
    tr = _tile(hrows, 512, 16)

    def body(idx_ref, own_ref, got_ref, o_ref):
        p = idx_ref[0]
        own = own_ref[0].astype(F32)
        parts = [got_ref[k].astype(F32) for k in range(3)]
        acc = jnp.zeros_like(own)
        for q in range(4):
            val = own
            for k, rel in enumerate((2, 1, 3)):
                val = jnp.where((p ^ rel) == q, parts[k], val)
            acc = acc + val
        o_ref[...] = acc

    idx = (2 * lax.axis_index("x") + lax.axis_index("y")).astype(jnp.int32).reshape(1)
    return pl.pallas_call(
        body, name="rs_sum_chips",
        grid_spec=pltpu.PrefetchScalarGridSpec(
            num_scalar_prefetch=1, grid=(hrows // tr,),
            in_specs=[pl.BlockSpec((1, tr, c), lambda i, pr: (pr[0], i, 0)), pl.BlockSpec((3, tr, c), lambda i, pr: (0, i, 0))],
            out_specs=pl.BlockSpec((tr, c), lambda i, pr: (i, 0))),
        out_shape=jax.ShapeDtypeStruct((hrows, c), F32), compiler_params=_params(("parallel",)),
    )(idx, s, got)


def _join_halves(t):
    hrows, c = t.shape

    def body(t_ref, out_ref, send_sem, recv_sem, lsem):
        x, y, cc = _coords()
        mine = pltpu.make_async_copy(t_ref, out_ref.at[cc], lsem)
        mine.start()
        cp = pltpu.make_async_remote_copy(
            src_ref=t_ref, dst_ref=out_ref.at[cc], send_sem=send_sem, recv_sem=recv_sem,
            device_id=(x, y, 1 - cc), device_id_type=MESH)
        cp.start()
        pltpu.make_async_remote_copy(
            src_ref=t_ref, dst_ref=out_ref.at[1 - cc], send_sem=send_sem, recv_sem=recv_sem,
            device_id=(x, y, cc), device_id_type=MESH).wait_recv()
        cp.wait_send()
        mine.wait()

    return pl.pallas_call(
        body, name="rs_join", in_specs=[ANY], out_specs=ANY, out_shape=jax.ShapeDtypeStruct((2, hrows, c), t.dtype),
        scratch_shapes=[pltpu.SemaphoreType.DMA, pltpu.SemaphoreType.DMA, pltpu.SemaphoreType.DMA], compiler_params=_params(),
    )(t)


def _reduce_scatter(g):
    got = _swap_halves(g, name="rs_swap")
    s = _add_halves(g, got)
    recv = _scatter_chips(s)
    t = _sum_chips(s, recv)
    return _join_halves(t).reshape(g.shape[1], g.shape[2])


_BIG = (("w_gate_up", 2), ("w_down", 1), ("fox_w_in", 2), ("fox_w_out", 1), ("gla_w_in", 2), ("gla_w_out", 1),
        ("gdn_w_in", 2), ("gdn_w_out", 1))
_SMALL_SHARDED = (("meta_tokens", 1), ("gla_w_alpha2", 2), ("gdn_conv_w", 3))
_REPLICATED = ("norm_mix", "norm_ffn", "fox_b_f", "fox_q_gain", "fox_k_gain", "gla_b_alpha", "gla_o_gain",
               "gdn_a_log", "gdn_dt_bias", "gdn_o_gain")
_WEIGHTS = ("meta_tokens", "norm_mix", "norm_ffn", "w_gate_up", "w_down", "fox_w_in", "fox_b_f", "fox_q_gain",
            "fox_k_gain", "fox_w_out", "gla_w_in", "gla_w_alpha2", "gla_b_alpha", "gla_o_gain", "gla_w_out",
            "gdn_w_in", "gdn_conv_w", "gdn_a_log", "gdn_dt_bias", "gdn_o_gain", "gdn_w_out")
_PACK_ROWS = 512


def _pack(arrays, width, row_mult, dtype):
    flat = jnp.concatenate([a.astype(dtype).reshape(-1) for a in arrays])
    per = width * row_mult
    n = -(-flat.shape[0] // per) * per
    return jnp.pad(flat, (0, n - flat.shape[0])).reshape(n // width, width)


def _unpack(flat, shapes):
    out, off = [], 0
    for s in shapes:
        n = 1
        for d in s:
            n *= d
        out.append(flat[off:off + n].reshape(s))
        off += n
    return out


def _unpack_cols(flat2, shapes):
    out, off = [], 0
    for s in shapes:
        n = 1
        for d in s:
            n *= d
        out.append(flat2[:, off:off + n].reshape((flat2.shape[0],) + tuple(s)))
        off += n
    return out


def _pad_cols(w, n):
    return jnp.pad(w, [(0, 0)] * (w.ndim - 1) + [(0, n - w.shape[-1])])


def kernel(x, meta_tokens, norm_mix, norm_ffn, w_gate_up, w_down, fox_w_in, fox_b_f, fox_q_gain, fox_k_gain, fox_w_out, gla_w_in, gla_w_alpha2, gla_b_alpha, gla_o_gain, gla_w_out, gdn_w_in, gdn_conv_w, gdn_a_log, gdn_dt_bias, gdn_o_gain, gdn_w_out, loss_target, m_meta_tokens, m_norm_mix, m_norm_ffn, m_w_gate_up, m_w_down, m_fox_w_in, m_fox_b_f, m_fox_q_gain, m_fox_k_gain, m_fox_w_out, m_gla_w_in, m_gla_w_alpha2, m_gla_b_alpha, m_gla_o_gain, m_gla_w_out, m_gdn_w_in, m_gdn_conv_w, m_gdn_a_log, m_gdn_dt_bias, m_gdn_o_gain, m_gdn_w_out, v_meta_tokens, v_norm_mix, v_norm_ffn, v_w_gate_up, v_w_down, v_fox_w_in, v_fox_b_f, v_fox_q_gain, v_fox_k_gain, v_fox_w_out, v_gla_w_in, v_gla_w_alpha2, v_gla_b_alpha, v_gla_o_gain, v_gla_w_out, v_gdn_w_in, v_gdn_conv_w, v_gdn_a_log, v_gdn_dt_bias, v_gdn_o_gain, v_gdn_w_out):
    W = dict(meta_tokens=meta_tokens, norm_mix=norm_mix, norm_ffn=norm_ffn, w_gate_up=w_gate_up, w_down=w_down,
             fox_w_in=fox_w_in, fox_b_f=fox_b_f, fox_q_gain=fox_q_gain, fox_k_gain=fox_k_gain, fox_w_out=fox_w_out,
             gla_w_in=gla_w_in, gla_w_alpha2=gla_w_alpha2, gla_b_alpha=gla_b_alpha, gla_o_gain=gla_o_gain,
             gla_w_out=gla_w_out, gdn_w_in=gdn_w_in, gdn_conv_w=gdn_conv_w, gdn_a_log=gdn_a_log,
             gdn_dt_bias=gdn_dt_bias, gdn_o_gain=gdn_o_gain, gdn_w_out=gdn_w_out)
    M = dict(meta_tokens=m_meta_tokens, norm_mix=m_norm_mix, norm_ffn=m_norm_ffn, w_gate_up=m_w_gate_up, w_down=m_w_down,
             fox_w_in=m_fox_w_in, fox_b_f=m_fox_b_f, fox_q_gain=m_fox_q_gain, fox_k_gain=m_fox_k_gain,
             fox_w_out=m_fox_w_out, gla_w_in=m_gla_w_in, gla_w_alpha2=m_gla_w_alpha2, gla_b_alpha=m_gla_b_alpha,
             gla_o_gain=m_gla_o_gain, gla_w_out=m_gla_w_out, gdn_w_in=m_gdn_w_in, gdn_conv_w=m_gdn_conv_w,
             gdn_a_log=m_gdn_a_log, gdn_dt_bias=m_gdn_dt_bias, gdn_o_gain=m_gdn_o_gain, gdn_w_out=m_gdn_w_out)
    V = dict(meta_tokens=v_meta_tokens, norm_mix=v_norm_mix, norm_ffn=v_norm_ffn, w_gate_up=v_w_gate_up, w_down=v_w_down,
             fox_w_in=v_fox_w_in, fox_b_f=v_fox_b_f, fox_q_gain=v_fox_q_gain, fox_k_gain=v_fox_k_gain,
             fox_w_out=v_fox_w_out, gla_w_in=v_gla_w_in, gla_w_alpha2=v_gla_w_alpha2, gla_b_alpha=v_gla_b_alpha,
             gla_o_gain=v_gla_o_gain, gla_w_out=v_gla_w_out, gdn_w_in=v_gdn_w_in, gdn_conv_w=v_gdn_conv_w,
             gdn_a_log=v_gdn_a_log, gdn_dt_bias=v_gdn_dt_bias, gdn_o_gain=v_gdn_o_gain, gdn_w_out=v_gdn_w_out)
    chip = 2 * lax.axis_index("x") + lax.axis_index("y")

    others = [(n, ax) for n, ax in _BIG if n not in ("w_gate_up", "w_down")]
    packed = _pack([jnp.swapaxes(w_gate_up, 1, 2), w_down] + [W[n] for n, _ in others], D, _PACK_ROWS, BF16)
    wpk = _ag_weights(packed)
    rows = wpk.shape[1]
    gathered = wpk[:, FFN_ROWS:].reshape(4, -1)
    full = {}
    for (n, ax), seg in zip(others, _unpack_cols(gathered, [W[n].shape for n, _ in others])):
        full[n] = jnp.concatenate([seg[q] for q in range(4)], axis=ax)
    small = _pack([W[n] for n, _ in _SMALL_SHARDED], LANES, 8, F32)
    small_all = _gather8(small, reduce=False, name="gather_small").reshape(8, -1)
    for (n, ax), seg in zip(_SMALL_SHARDED, _unpack_cols(small_all, [W[n].shape for n, _ in _SMALL_SHARDED])):
        full[n] = jnp.concatenate([seg[2 * q] for q in range(4)], axis=ax)
    fox_in = _pad_cols(full["fox_w_in"], FOX_INP)
    gla_in = _pad_cols(full["gla_w_in"], GLA_INP)
    gdn_in = _pad_cols(full["gdn_w_in"], GDN_INP)
    w_alpha2, conv_w = full["gla_w_alpha2"][0], full["gdn_conv_w"][0]

    h = jnp.concatenate([jnp.zeros((META0, D), F32), full["meta_tokens"], x[0]], axis=0)
    saved = []
    for i in range(DEPTH):
        kind, j = i % 3, i // 3
        y = _rms_fwd(h, norm_mix[i], name=f"norm_mix{i}")
        if kind == 0:
            proj = _mm(y, fox_in[j], name=f"fox_in{j}")
            qa, ka, va = _fox_prep(proj, fox_b_f[j], fox_q_gain[j], fox_k_gain[j])
            o, og, lse = _fox_attn_fwd(qa, ka, va, proj)
            w_out, mix = full["fox_w_out"][j], (proj, qa, ka, va, o, lse)
        elif kind == 1:
            proj = _mm(y, gla_in[j], name=f"gla_in{j}")
            o, og, states = _gla_fwd(proj, w_alpha2, gla_b_alpha[j], gla_o_gain[j])
            w_out, mix = full["gla_w_out"][j], (proj, o, states)
        else:
            proj = _mm(y, gdn_in[j], name=f"gdn_in{j}")
            o, og, states = _gdn_fwd(proj, conv_w, gdn_a_log[j], gdn_dt_bias[j], gdn_o_gain[j])
            w_out, mix = full["gdn_w_out"][j], (proj, o, states)
        hm = _mm(og, w_out, add=h, name=f"mix_out{i}")
        yf = _rms_fwd(hm, norm_ffn[i], name=f"norm_ffn{i}")
        gate, up, act = _ffn_up(yf, wpk, i)
        hn = _ffn_down(act, wpk, i, hm)
        saved.append((h, y, mix, og, w_out, hm, yf, gate, up, act))
        h = hn
    dh, loss_tile = _loss_head(h, loss_target[0])

    G = {n: [None] * W[n].shape[0] for n in _WEIGHTS if n not in ("meta_tokens", "w_gate_up", "w_down")}
    gpk = jnp.zeros((4, rows, D), BF16)
    for i in reversed(range(DEPTH)):
        kind, j = i % 3, i // 3
        h_in, y, mix, og, w_out, hm, yf, gate, up, act = saved[i]
        dg, du = _ffn_dact(dh, wpk, i, gate, up)
        gpk = _ffn_dw_down(act, dh, gpk, i)
        dyf = _ffn_dyf(dg, du, wpk, i)
        gpk = _ffn_dw_gu(dg, du, yf, gpk, i)
        dhm, dnf = _rms_bwd(hm, norm_ffn[i], dyf, dh, name=f"d_norm_ffn{i}")
        G["norm_ffn"][i] = dnf[0]
        dog = _mm(dhm, w_out, tb=True, name=f"d_og{i}")
        dw_out = _mm(og, dhm, ta=True, out_dtype=BF16, name=f"d_w_out{i}")
        if kind == 0:
            proj, qa, ka, va, o, lse = mix
            doa, q2, dgate = _fox_gate_bwd(dog, o, proj, lse, qa)
            dqn, dkn, dv, dct = _fox_attn_bwd(q2, ka, va, doa)
            dproj, dqg, dkg, dbf = _fox_prep_bwd(proj, fox_b_f[j], fox_q_gain[j], fox_k_gain[j], dqn, dkn, dv, dgate, dct)
            G["fox_w_out"][j] = dw_out
            G["fox_q_gain"][j] = dqg.reshape(FOX_H, FOX_DH).sum(0)
            G["fox_k_gain"][j] = dkg.reshape(FOX_H, FOX_DH).sum(0)
            G["fox_b_f"][j] = dbf[0, :FOX_H]
            w_in, wname, n_in = fox_in[j], "fox_w_in", fox_w_in.shape[2] * 4
        elif kind == 1:
            proj, o, states = mix
            dproj, dwa, dba, dogain = _gla_bwd(proj, w_alpha2, gla_b_alpha[j], gla_o_gain[j], o, states, dog)
            G["gla_w_out"][j] = dw_out
            G["gla_w_alpha2"][j] = dwa[:GLA_RANK]
            G["gla_b_alpha"][j] = dba[0]
            G["gla_o_gain"][j] = dogain[0]
            w_in, wname, n_in = gla_in[j], "gla_w_in", gla_w_in.shape[2] * 4
        else:
            proj, o, states = mix
            dproj, dcw, dal, ddt, dogain = _gdn_bwd(proj, conv_w, gdn_a_log[j], gdn_dt_bias[j], gdn_o_gain[j], o, states, dog)
            G["gdn_w_out"][j] = dw_out
            G["gdn_conv_w"][j] = dcw[:4].reshape(4, 1, GDN_CONV)
            G["gdn_a_log"][j] = dal[0, :GDN_H]
            G["gdn_dt_bias"][j] = ddt[0, :GDN_H]
            G["gdn_o_gain"][j] = dogain[0]
            w_in, wname, n_in = gdn_in[j], "gdn_w_in", gdn_w_in.shape[2] * 4
        dy = _mm(dproj, w_in, tb=True, name=f"d_y{i}")
        G[wname][j] = _mm(y, dproj, ta=True, out_dtype=BF16, name=f"d_w_in{i}")[:, :n_in]
        dh, dnm = _rms_bwd(h_in, norm_mix[i], dy, dhm, name=f"d_norm_mix{i}")
        G["norm_mix"][i] = dnm[0]
    grad_x = dh[ROW0:][None]
    G = {n: jnp.stack(v) for n, v in G.items()}
    G["meta_tokens"] = dh[META0:ROW0]

    blocks = []
    for q in range(4):
        parts = []
        for n, ax in others:
            sz = W[n].shape[ax]
            parts.append(lax.slice_in_dim(G[n], q * sz, (q + 1) * sz, axis=ax))
        blk = _pack(parts, D, 1, BF16)
        blocks.append(jnp.pad(blk, ((0, rows - FFN_ROWS - blk.shape[0]), (0, 0))))
    gpk = lax.dynamic_update_slice(gpk, jnp.stack(blocks), (0, FFN_ROWS, 0))
    reduced = _reduce_scatter(gpk)
    grads = dict(zip([n for n, _ in others], _unpack(reduced[FFN_ROWS:].reshape(-1), [W[n].shape for n, _ in others])))
    small_names = [n for n, _ in _SMALL_SHARDED] + list(_REPLICATED)
    small_g = _pack([G[n] for n in small_names] + [loss_tile[0, 0:1]], LANES, 8, F32)
    small_sum = _gather8(small_g, reduce=True, name="allreduce_small").reshape(-1)
    small_shapes = [G[n].shape for n in small_names] + [(1,)]
    small_vals = _unpack(small_sum, small_shapes)
    loss = small_vals[-1][0]
    for n, val in zip(small_names, small_vals[:-1]):
        grads[n] = val
    for n, ax in _SMALL_SHARDED:
        sz = W[n].shape[ax]
        grads[n] = lax.dynamic_slice_in_dim(grads[n], chip * sz, sz, axis=ax)

    delta, new_m, new_v = {}, {}, {}
    for n, off, tr_ in (("w_gate_up", OFF_GU, True), ("w_down", OFF_DOWN, False)):
        grads[n], delta[n], new_m[n], new_v[n] = _adamw_packed(W[n], reduced, M[n], V[n], row_off=off, transposed=tr_,
                                                               name=f"adamw_{n}")
    for n, _ in others:
        delta[n], new_m[n], new_v[n] = _adamw(W[n], grads[n], M[n], V[n], name=f"adamw_{n}")
    tiny = [n for n in _WEIGHTS if n not in dict(_BIG)]
    packs = [_pack([T[n] for n in tiny], LANES, 8, F32) for T in (W, grads, M, V)]
    outs = _adamw(*packs, name="adamw_small")
    shapes = [W[n].shape for n in tiny]
    for dst, o in zip((delta, new_m, new_v), outs):
        for n, val in zip(tiny, _unpack(o.reshape(-1), shapes)):
            dst[n] = val
    return (loss, grad_x, *[grads[n] for n in _WEIGHTS], *[delta[n] for n in _WEIGHTS],
            *[new_m[n] for n in _WEIGHTS], *[new_v[n] for n in _WEIGHTS])
```

```python
import functools

import jax
import jax.numpy as jnp
from jax import lax
from jax.experimental import pallas as pl
from jax.experimental.pallas import tpu as pltpu

F32, BF16 = jnp.float32, jnp.bfloat16
D = 1024
N_META = 16
ROW0 = 128
META0 = ROW0 - N_META
EPS = 1e-6
LANES = 128
VMEM_LIMIT = 56 * 1024 * 1024

FOX_H, FOX_DH = 16, 64
FOX_INP = 4224
GLA_H, GLA_DK, GLA_DV, GLA_RANK = 4, 128, 256, 16
GLA_QK, GLA_V = 512, 1024
GLA_INP = 3200
GLA_NORM = 16.0
GDN_H, GDN_DK, GDN_DV = 8, 128, 128
GDN_CONV = 3072
GDN_INP = 4224
CHUNK = 64
D_FF = 2816
DEPTH = 4

ADAM_LR, ADAM_B1, ADAM_B2, ADAM_EPS, ADAM_WD, ADAM_STEP = 0.001, 0.9, 0.999, 1e-08, 0.01, 10

MESH = pl.DeviceIdType.MESH
ANY = pl.BlockSpec(memory_space=pl.ANY)
VM = pl.BlockSpec(memory_space=pltpu.VMEM)


def _params(sem=None, **kw):
    if sem is not None:
        kw["dimension_semantics"] = sem
    return pltpu.CompilerParams(vmem_limit_bytes=VMEM_LIMIT, **kw)


def _tile(n, cap, mult=LANES):
    best = None
    for t in range(mult, min(n, cap) + 1, mult):
        if n % t == 0:
            best = t
    return best if best is not None else n


def nn(a, b, **kw):
    return jnp.dot(a, b, preferred_element_type=F32, **kw)


def nt(a, b, **kw):
    return lax.dot_general(a, b, (((1,), (1,)), ((), ())), preferred_element_type=F32, **kw)


def tn(a, b, **kw):
    return lax.dot_general(a, b, (((0,), (0,)), ((), ())), preferred_element_type=F32, **kw)


def _split3(x):
    hi = x.astype(BF16)
    r = x - hi.astype(F32)
    mid = r.astype(BF16)
    lo = (r - mid.astype(F32)).astype(BF16)
    return hi, mid, lo


def _sel_l(sel, x):
    a, b, c = _split3(x)
    return nn(sel, a) + nn(sel, b) + nn(sel, c)


def _sel_r(x, sel):
    a, b, c = _split3(x)
    return nn(a, sel) + nn(b, sel) + nn(c, sel)


def _iota(shape, dim):
    return lax.broadcasted_iota(jnp.int32, shape, dim)


def _tri(n, upper=False, strict=False):
    i, j = _iota((n, n), 0), _iota((n, n), 1)
    if upper:
        m = (j > i) if strict else (j >= i)
    else:
        m = (j < i) if strict else (j <= i)
    return m


def _sigmoid(x):
    return 1.0 / (1.0 + jnp.exp(-x))


def _log_sigmoid(x):
    return jnp.minimum(x, 0.0) - jnp.log(1.0 + jnp.exp(-jnp.abs(x)))


def _softplus(x):
    return jnp.maximum(x, 0.0) + jnp.log(1.0 + jnp.exp(-jnp.abs(x)))


def _silu(x):
    return x * _sigmoid(x)


def _dsilu(x):
    s = _sigmoid(x)
    return s * (1.0 + x * (1.0 - s))


def _mm(a, b, *, ta=False, tb=False, add=None, out_dtype=F32, name):
    m, k = (a.shape[1], a.shape[0]) if ta else a.shape
    n = b.shape[0] if tb else b.shape[1]
    assert k == (b.shape[1] if tb else b.shape[0])
    tm, tn_, tk = _tile(m, 1408, 16), _tile(n, 768), _tile(k, 1408)
    nk = k // tk

    def body(*refs):
        if add is None:
            a_ref, b_ref, o_ref, acc = refs
        else:
            a_ref, b_ref, r_ref, o_ref, acc = refs
        kk = pl.program_id(2)

        @pl.when(kk == 0)
        def _():
            acc[...] = jnp.zeros_like(acc)

        av, bv = a_ref[...].astype(BF16), b_ref[...].astype(BF16)
        dims = (((0,) if ta else (1,), (1,) if tb else (0,)), ((), ()))
        acc[...] += lax.dot_general(av, bv, dims, preferred_element_type=F32)

        @pl.when(kk == nk - 1)
        def _():
            r = acc[...]
            if add is not None:
                r = r + r_ref[...].astype(F32)
            o_ref[...] = r.astype(out_dtype)

    a_spec = pl.BlockSpec((tk, tm), lambda i, j, q: (q, i)) if ta else pl.BlockSpec((tm, tk), lambda i, j, q: (i, q))
    b_spec = pl.BlockSpec((tn_, tk), lambda i, j, q: (j, q)) if tb else pl.BlockSpec((tk, tn_), lambda i, j, q: (q, j))
    o_spec = pl.BlockSpec((tm, tn_), lambda i, j, q: (i, j))
    ins, specs = [a, b], [a_spec, b_spec]
    if add is not None:
        ins.append(add)
        specs.append(o_spec)
    return pl.pallas_call(
        body, name=name, grid=(m // tm, n // tn_, nk), in_specs=specs, out_specs=o_spec,
        out_shape=jax.ShapeDtypeStruct((m, n), out_dtype),
        scratch_shapes=[pltpu.VMEM((tm, tn_), F32)],
        compiler_params=_params(("parallel", "parallel", "arbitrary")),
    )(*ins)


def _rms_fwd(h, g, *, name):
    lp = h.shape[0]
    tr = _tile(lp, 512)

    def body(h_ref, g_ref, y_ref):
        x = h_ref[...]
        r = lax.rsqrt(jnp.mean(x * x, axis=-1, keepdims=True) + EPS)
        y_ref[...] = (x * r * g_ref[...]).astype(BF16)

    return pl.pallas_call(
        body, name=name, grid=(lp // tr,),
        in_specs=[pl.BlockSpec((tr, D), lambda i: (i, 0)), pl.BlockSpec((1, D), lambda i: (0, 0))],
        out_specs=pl.BlockSpec((tr, D), lambda i: (i, 0)),
        out_shape=jax.ShapeDtypeStruct((lp, D), BF16), compiler_params=_params(("parallel",)),
    )(h, g.reshape(1, D))


def _rms_bwd(h, g, dy, dres, *, name):
    lp = h.shape[0]
    tr = _tile(lp, 512)

    def body(h_ref, g_ref, dy_ref, dr_ref, dh_ref, dg_ref):
        @pl.when(pl.program_id(0) == 0)
        def _():
            dg_ref[...] = jnp.zeros_like(dg_ref)

        x, dyv = h_ref[...], dy_ref[...].astype(F32)
        r = lax.rsqrt(jnp.mean(x * x, axis=-1, keepdims=True) + EPS)
        u = dyv * g_ref[...]
        dx = r * u - x * (r * r * r) * jnp.mean(x * u, axis=-1, keepdims=True)
        dh_ref[...] = dr_ref[...] + dx
        dg_ref[...] += jnp.sum(dyv * x * r, axis=0, keepdims=True)

    return pl.pallas_call(
        body, name=name, grid=(lp // tr,),
        in_specs=[pl.BlockSpec((tr, D), lambda i: (i, 0)), pl.BlockSpec((1, D), lambda i: (0, 0)),
                  pl.BlockSpec((tr, D), lambda i: (i, 0)), pl.BlockSpec((tr, D), lambda i: (i, 0))],
        out_specs=[pl.BlockSpec((tr, D), lambda i: (i, 0)), pl.BlockSpec((1, D), lambda i: (0, 0))],
        out_shape=[jax.ShapeDtypeStruct((lp, D), F32), jax.ShapeDtypeStruct((1, D), F32)],
        compiler_params=_params(("arbitrary",)),
    )(h, g.reshape(1, D), dy, dres)


GU_ROWS, DOWN_ROWS = 1408, 704
OFF_GU, OFF_DOWN = 0, DEPTH * GU_ROWS
FFN_ROWS = DEPTH * (GU_ROWS + DOWN_ROWS)
FFN_TM = 704


def _gu_spec(fn):
    return pl.BlockSpec((None, GU_ROWS, D), fn)


def _down_spec(fn):
    return pl.BlockSpec((None, DOWN_ROWS, D), fn)


def _down_pair(w0_ref, w1_ref):
    return jnp.concatenate([w0_ref[...], w1_ref[...]], axis=0)


def _ffn_up(yf, wpk, layer):
    lp = yf.shape[0]
    tm = _tile(lp, FFN_TM, 16)

    def body(y_ref, wg_ref, wu_ref, g_ref, u_ref, a_ref):
        y = y_ref[...]
        g, u = nt(y, wg_ref[...]), nt(y, wu_ref[...])
        g_ref[...] = g.astype(BF16)
        u_ref[...] = u.astype(BF16)
        a_ref[...] = (_silu(g) * u).astype(BF16)

    o = pl.BlockSpec((tm, GU_ROWS), lambda i, j: (i, j))
    return pl.pallas_call(
        body, name=f"ffn_up{layer}", grid=(lp // tm, 2),
        in_specs=[pl.BlockSpec((tm, D), lambda i, j: (i, 0)), _gu_spec(lambda i, j: (j, OFF_GU // GU_ROWS + layer, 0)),
                  _gu_spec(lambda i, j: (2 + j, OFF_GU // GU_ROWS + layer, 0))],
        out_specs=[o, o, o], out_shape=[jax.ShapeDtypeStruct((lp, D_FF), BF16)] * 3,
        compiler_params=_params(("parallel", "parallel")),
    )(yf, wpk, wpk)


def _ffn_down(act, wpk, layer, res):
    lp = act.shape[0]
    tm = _tile(lp, FFN_TM, 16)

    def body(a_ref, w0_ref, w1_ref, r_ref, o_ref, acc):
        kk = pl.program_id(1)

        @pl.when(kk == 0)
        def _():
            acc[...] = r_ref[...]

        acc[...] += nn(a_ref[...], _down_pair(w0_ref, w1_ref))

        @pl.when(kk == 1)
        def _():
            o_ref[...] = acc[...]

    o = pl.BlockSpec((tm, D), lambda i, kk: (i, 0))
    blk = OFF_DOWN // DOWN_ROWS + layer
    return pl.pallas_call(
        body, name=f"ffn_down{layer}", grid=(lp // tm, 2),
        in_specs=[pl.BlockSpec((tm, GU_ROWS), lambda i, kk: (i, kk)), _down_spec(lambda i, kk: (2 * kk, blk, 0)),
                  _down_spec(lambda i, kk: (2 * kk + 1, blk, 0)), o],
        out_specs=o, out_shape=jax.ShapeDtypeStruct((lp, D), F32), scratch_shapes=[pltpu.VMEM((tm, D), F32)],
        compiler_params=_params(("parallel", "arbitrary")),
    )(act, wpk, wpk, res)


def _ffn_dact(dh, wpk, layer, gate, up):
    lp = dh.shape[0]
    tm = _tile(lp, FFN_TM, 16)

    def body(d_ref, w0_ref, w1_ref, g_ref, u_ref, dg_ref, du_ref):
        da = nt(d_ref[...].astype(BF16), _down_pair(w0_ref, w1_ref))
        g, u = g_ref[...].astype(F32), u_ref[...].astype(F32)
        dg_ref[...] = (da * u * _dsilu(g)).astype(BF16)
        du_ref[...] = (da * _silu(g)).astype(BF16)

    o = pl.BlockSpec((tm, GU_ROWS), lambda i, j: (i, j))
    blk = OFF_DOWN // DOWN_ROWS + layer
    return pl.pallas_call(
        body, name=f"d_act{layer}", grid=(lp // tm, 2),
        in_specs=[pl.BlockSpec((tm, D), lambda i, j: (i, 0)), _down_spec(lambda i, j: (2 * j, blk, 0)),
                  _down_spec(lambda i, j: (2 * j + 1, blk, 0)), o, o],
        out_specs=[o, o], out_shape=[jax.ShapeDtypeStruct((lp, D_FF), BF16)] * 2,
        compiler_params=_params(("parallel", "parallel")),
    )(dh, wpk, wpk, gate, up)


def _ffn_dyf(dg, du, wpk, layer):
    lp = dg.shape[0]
    tm = _tile(lp, FFN_TM, 16)

    def body(dg_ref, du_ref, w_ref, o_ref, acc):
        kk = pl.program_id(1)

        @pl.when(kk == 0)
        def _():
            acc[...] = jnp.zeros_like(acc)

        @pl.when(kk < 2)
        def _():
            acc[...] += nn(dg_ref[...], w_ref[...])

        @pl.when(kk >= 2)
        def _():
            acc[...] += nn(du_ref[...], w_ref[...])

        @pl.when(kk == 3)
        def _():
            o_ref[...] = acc[...]

    return pl.pallas_call(
        body, name=f"d_yf{layer}", grid=(lp // tm, 4),
        in_specs=[pl.BlockSpec((tm, GU_ROWS), lambda i, kk: (i, jnp.minimum(kk, 1))),
                  pl.BlockSpec((tm, GU_ROWS), lambda i, kk: (i, jnp.maximum(kk - 2, 0))),
                  _gu_spec(lambda i, kk: (kk, OFF_GU // GU_ROWS + layer, 0))],
        out_specs=pl.BlockSpec((tm, D), lambda i, kk: (i, 0)), out_shape=jax.ShapeDtypeStruct((lp, D), F32),
        scratch_shapes=[pltpu.VMEM((tm, D), F32)], compiler_params=_params(("parallel", "arbitrary")),
    )(dg, du, wpk)


def _ffn_dw_down(act, dh, gpk, layer):
    lp = act.shape[0]
    tk = _tile(lp, 1408, 16)
    nk = lp // tk
    row = OFF_DOWN + layer * DOWN_ROWS

    def body(a_ref, d_ref, g_in, g_out, acc, stage, sems):
        jp, kk = pl.program_id(0), pl.program_id(1)

        @pl.when(kk == 0)
        def _():
            acc[...] = jnp.zeros_like(acc)

        acc[...] += tn(a_ref[...], d_ref[...].astype(BF16))

        @pl.when(kk == nk - 1)
        def _():
            stage[...] = acc[...].astype(BF16)
            copies = [pltpu.make_async_copy(stage.at[pl.ds(hf * DOWN_ROWS, DOWN_ROWS), :],
                                            g_out.at[2 * jp + hf, pl.ds(row, DOWN_ROWS), :], sems.at[hf]) for hf in range(2)]
            for cp in copies:
                cp.start()
            for cp in copies:
                cp.wait()

    return pl.pallas_call(
        body, name=f"d_w_down{layer}", grid=(2, nk),
        in_specs=[pl.BlockSpec((tk, GU_ROWS), lambda jp, kk: (kk, jp)), pl.BlockSpec((tk, D), lambda jp, kk: (kk, 0)), ANY],
        out_specs=ANY, out_shape=jax.ShapeDtypeStruct(gpk.shape, gpk.dtype),
        scratch_shapes=[pltpu.VMEM((GU_ROWS, D), F32), pltpu.VMEM((GU_ROWS, D), BF16), pltpu.SemaphoreType.DMA((2,))],
        input_output_aliases={2: 0}, compiler_params=_params(("arbitrary", "arbitrary")),
    )(act, dh, gpk)


def _ffn_dw_gu(dg, du, yf, gpk, layer):
    lp = dg.shape[0]
    tk = _tile(lp, 1408, 16)
    nk = lp // tk

    def body(dg_ref, du_ref, y_ref, g_in, o_ref, acc):
        c, kk = pl.program_id(0), pl.program_id(1)

        @pl.when(kk == 0)
        def _():
            acc[...] = jnp.zeros_like(acc)

        @pl.when(c < 2)
        def _():
            acc[...] += tn(dg_ref[...], y_ref[...])

        @pl.when(c >= 2)
        def _():
            acc[...] += tn(du_ref[...], y_ref[...])

        @pl.when(kk == nk - 1)
        def _():
            o_ref[...] = acc[...].astype(BF16)

    return pl.pallas_call(
        body, name=f"d_w_gate_up{layer}", grid=(4, nk),
        in_specs=[pl.BlockSpec((tk, GU_ROWS), lambda c, kk: (kk, jnp.minimum(c, 1))),
                  pl.BlockSpec((tk, GU_ROWS), lambda c, kk: (kk, jnp.maximum(c - 2, 0))),
                  pl.BlockSpec((tk, D), lambda c, kk: (kk, 0)), ANY],
        out_specs=_gu_spec(lambda c, kk: (c, OFF_GU // GU_ROWS + layer, 0)),
        out_shape=jax.ShapeDtypeStruct(gpk.shape, gpk.dtype),
        scratch_shapes=[pltpu.VMEM((GU_ROWS, D), F32)], input_output_aliases={3: 0},
        compiler_params=_params(("parallel", "arbitrary")),
    )(dg, du, yf, gpk)


def _loss_head(h, target):
    lp = h.shape[0]
    nb = lp // ROW0

    def body(h_ref, t_ref, dh_ref, l_ref):
        i = pl.program_id(0)

        @pl.when(i == 0)
        def _():
            l_ref[...] = jnp.zeros_like(l_ref)
            dh_ref[...] = jnp.zeros_like(dh_ref)

        @pl.when(i > 0)
        def _():
            err = h_ref[...] - t_ref[...]
            dh_ref[...] = err * (1.0 / D)
            l_ref[...] += jnp.sum(err * err) * (0.5 / D)

    return pl.pallas_call(
        body, name="loss_head", grid=(nb,),
        in_specs=[pl.BlockSpec((ROW0, D), lambda i: (i, 0)), pl.BlockSpec((ROW0, D), lambda i: (jnp.maximum(i - 1, 0), 0))],
        out_specs=[pl.BlockSpec((ROW0, D), lambda i: (i, 0)), pl.BlockSpec((8, LANES), lambda i: (0, 0))],
        out_shape=[jax.ShapeDtypeStruct((lp, D), F32), jax.ShapeDtypeStruct((8, LANES), F32)],
        compiler_params=_params(("arbitrary",)),
    )(h, target)


def _adamw(w, g, m, v, *, name):
    shape = w.shape
    c = shape[-1]
    r = w.size // c
    w2, g2, m2, v2 = (t.reshape(r, c) for t in (w, g, m, v))
    tr = _tile(r, max(8, (1 << 19) // c), 8)

    def body(w_ref, g_ref, m_ref, v_ref, d_ref, nm_ref, nv_ref):
        gv = g_ref[...]
        nm = ADAM_B1 * m_ref[...] + (1.0 - ADAM_B1) * gv
        nv = ADAM_B2 * v_ref[...] + (1.0 - ADAM_B2) * (gv * gv)
        m_hat = nm / (1.0 - ADAM_B1 ** ADAM_STEP)
        v_hat = nv / (1.0 - ADAM_B2 ** ADAM_STEP)
        d_ref[...] = -ADAM_LR * (m_hat / (jnp.sqrt(v_hat) + ADAM_EPS) + ADAM_WD * w_ref[...])
        nm_ref[...] = nm
        nv_ref[...] = nv

    spec = pl.BlockSpec((tr, c), lambda i: (i, 0))
    outs = pl.pallas_call(
        body, name=name, grid=(r // tr,), in_specs=[spec] * 4, out_specs=[spec] * 3,
        out_shape=[jax.ShapeDtypeStruct((r, c), F32)] * 3, compiler_params=_params(("parallel",)),
    )(w2, g2, m2, v2)
    return tuple(o.reshape(shape) for o in outs)


def _adam_math(w, g, m, v):
    nm = ADAM_B1 * m + (1.0 - ADAM_B1) * g
    nv = ADAM_B2 * v + (1.0 - ADAM_B2) * (g * g)
    m_hat = nm / (1.0 - ADAM_B1 ** ADAM_STEP)
    v_hat = nv / (1.0 - ADAM_B2 ** ADAM_STEP)
    return -ADAM_LR * (m_hat / (jnp.sqrt(v_hat) + ADAM_EPS) + ADAM_WD * w), nm, nv


def _adamw_packed(w, gred, m, v, *, row_off, transposed, name):
    nl, a, b = w.shape
    if transposed:
        ta = _tile(a, 256)
        wspec = pl.BlockSpec((1, ta, b), lambda l, r: (l, r, 0))
        gspec = pl.BlockSpec((b, ta), lambda l, r: (row_off // b + l, r))
        grid = (nl, a // ta)
    else:
        wspec = pl.BlockSpec((1, a, b), lambda l, r: (l, 0, 0))
        gspec = pl.BlockSpec((a, b), lambda l, r: (row_off // a + l, 0))
        grid = (nl, 1)

    def body(w_ref, g_ref, m_ref, v_ref, go_ref, d_ref, nm_ref, nv_ref):
        g = g_ref[...].T if transposed else g_ref[...]
        d, nm, nv = _adam_math(w_ref[0], g, m_ref[0], v_ref[0])
        go_ref[0], d_ref[0], nm_ref[0], nv_ref[0] = g, d, nm, nv

    return pl.pallas_call(
        body, name=name, grid=grid, in_specs=[wspec, gspec, wspec, wspec], out_specs=[wspec] * 4,
        out_shape=[jax.ShapeDtypeStruct(w.shape, F32)] * 4, compiler_params=_params(("parallel", "parallel")),
    )(w, gred, m, v)


FOX_AUG = FOX_H * LANES
L_C = 64
L_K = 67
L_LSE = 70
PAD_KEY = -30000.0
FOX_TQ = 384


def _head_sel(n_heads, width, lanes=LANES):
    r, c = _iota((n_heads * width, lanes), 0), _iota((n_heads * width, lanes), 1)
    down = (r // width == c).astype(BF16)
    r2, c2 = _iota((lanes, n_heads * width), 0), _iota((lanes, n_heads * width), 1)
    up = (c2 // width == r2).astype(BF16)
    return down, up


def _place(lane0):
    r, c = _iota((LANES, FOX_AUG), 0), _iota((LANES, FOX_AUG), 1)
    return [((c // LANES == r) & (c % LANES == lane0 + m)).astype(BF16) for m in range(3)]


def _placed(x, lane0):
    pcs = _split3(x)
    mats = _place(lane0)
    return nn(pcs[0], mats[0]) + nn(pcs[1], mats[1]) + nn(pcs[2], mats[2])


def _ones_at(rows, lanes):
    c = _iota((rows, FOX_AUG), 1) % LANES
    m = c == lanes[0]
    for l in lanes[1:]:
        m = m | (c == l)
    return m.astype(F32)


def _spread(x, extras, out_ref):
    rows = x.shape[0]
    left = _iota((rows, LANES), 1) < FOX_DH
    for p in range(FOX_H // 2):
        slab = x[:, p * LANES:(p + 1) * LANES]
        a = jnp.where(left, slab, extras[:, 2 * p * LANES:(2 * p + 1) * LANES])
        b = jnp.where(left, pltpu.roll(slab, FOX_DH, 1), extras[:, (2 * p + 1) * LANES:(2 * p + 2) * LANES])
        out_ref[:, 2 * p * LANES:(2 * p + 1) * LANES] = a.astype(BF16)
        out_ref[:, (2 * p + 1) * LANES:(2 * p + 2) * LANES] = b.astype(BF16)


def _fox_prep(proj, b_f, q_gain, k_gain):
    lp = proj.shape[0]
    nb = lp // LANES

    def body(p_ref, bf_ref, qg_ref, kg_ref, q_ref, k_ref, v_ref, carry):
        i = pl.program_id(0)

        @pl.when(i == 0)
        def _():
            carry[...] = jnp.zeros_like(carry)

        down, up = _head_sel(FOX_H, FOX_DH)

        def normed(x, gain):
            ms = _sel_r(x * x, down) * (1.0 / FOX_DH)
            r = _sel_r(lax.rsqrt(ms + EPS), up)
            return x * r * gain

        lane = _iota((LANES, LANES), 1)
        lf = jnp.where(lane < FOX_H, _log_sigmoid(p_ref[:, 4 * D:4 * D + LANES] + bf_ref[...]), 0.0)
        c = _sel_l(_tri(LANES).astype(BF16), lf) + carry[0:1, :]
        carry[...] = jnp.broadcast_to(c[LANES - 1:LANES, :], carry.shape)
        q_extra = _placed(c, L_C) + _ones_at(LANES, (L_K, L_K + 1, L_K + 2))
        row = i * LANES + _iota((LANES, FOX_AUG), 0)
        lane_a = _iota((LANES, FOX_AUG), 1) % LANES
        k_extra = -_placed(c, L_K) + _ones_at(LANES, (L_C, L_C + 1, L_C + 2, L_LSE, L_LSE + 1, L_LSE + 2))
        pad_val = jnp.where(lane_a == L_K, PAD_KEY, 0.0)
        k_extra = jnp.where((row < META0) & (lane_a >= L_K) & (lane_a < L_K + 3), pad_val, k_extra)
        v_extra = _ones_at(LANES, (L_C, L_C + 1, L_C + 2))
        _spread(normed(p_ref[:, 0:D], qg_ref[...]) * (FOX_DH ** -0.5), q_extra, q_ref)
        _spread(normed(p_ref[:, D:2 * D], kg_ref[...]), k_extra, k_ref)
        _spread(p_ref[:, 2 * D:3 * D], v_extra, v_ref)

    row = pl.BlockSpec((1, D), lambda i: (0, 0))
    aug = pl.BlockSpec((LANES, FOX_AUG), lambda i: (i, 0))
    return pl.pallas_call(
        body, name="fox_prep", grid=(nb,),
        in_specs=[pl.BlockSpec((LANES, FOX_INP), lambda i: (i, 0)), pl.BlockSpec((1, LANES), lambda i: (0, 0)), row, row],
        out_specs=[aug] * 3, out_shape=[jax.ShapeDtypeStruct((lp, FOX_AUG), BF16)] * 3,
        scratch_shapes=[pltpu.VMEM((8, LANES), F32)],
        compiler_params=_params(("arbitrary",)),
    )(proj, jnp.pad(b_f, (0, LANES - FOX_H)).reshape(1, LANES), jnp.tile(q_gain, FOX_H).reshape(1, D),
      jnp.tile(k_gain, FOX_H).reshape(1, D))


def _fox_attn_fwd(qa, ka, va, proj):
    lp = qa.shape[0]
    tq = _tile(lp, FOX_TQ)
    nq = lp // tq

    def body(q_ref, k_ref, v_ref, gate_ref, o_ref, og_ref, lse_ref):
        i = pl.program_id(1)
        causal = _iota((tq, tq), 1) <= _iota((tq, tq), 0)
        qs = [q_ref[:, hh * LANES:(hh + 1) * LANES] for hh in range(2)]

        def block(j, carry, diag):
            off = pl.multiple_of(j * tq, tq)
            out = []
            for hh in range(2):
                m, acc = carry[hh]
                k = k_ref[pl.ds(off, tq), hh * LANES:(hh + 1) * LANES]
                v = v_ref[pl.ds(off, tq), hh * LANES:(hh + 1) * LANES]
                s = nt(qs[hh], k)
                if diag:
                    s = jnp.where(causal, s, -1e30)
                m2 = jnp.maximum(m, jnp.max(s, axis=-1, keepdims=True))
                p = jnp.exp(s - m2)
                p_hi = p.astype(BF16)
                p_lo = (p - p_hi.astype(F32)).astype(BF16)
                out.append((m2, jnp.exp(m - m2) * acc + nn(p_hi, v) + nn(p_lo, v)))
            return tuple(out)

        init = tuple((jnp.full((tq, 1), -1e30, F32), jnp.zeros((tq, LANES), F32)) for _ in range(2))
        carry = lax.fori_loop(0, i, lambda j, c: block(j, c, False), init)
        carry = block(i, carry, True)
        outs, lses = [], []
        for hh in range(2):
            m, acc = carry[hh]
            l = acc[:, L_C:L_C + 1]
            outs.append(acc / l)
            lses.append(jnp.broadcast_to(m + jnp.log(l), (tq, LANES)))
        left = _iota((tq, LANES), 1) < FOX_DH
        o = jnp.where(left, outs[0], pltpu.roll(outs[1], FOX_DH, 1))
        o_ref[...] = o
        og_ref[...] = (o * _sigmoid(gate_ref[...])).astype(BF16)
        lse_ref[...] = jnp.where(left, lses[0], lses[1])

    qspec = pl.BlockSpec((tq, 2 * LANES), lambda p, i: (i, p))
    kspec = pl.BlockSpec((lp, 2 * LANES), lambda p, i: (0, p))
    ospec = pl.BlockSpec((tq, LANES), lambda p, i: (i, p))
    return pl.pallas_call(
        body, name="fox_attn_fwd", grid=(FOX_H // 2, nq),
        in_specs=[qspec, kspec, kspec, pl.BlockSpec((tq, LANES), lambda p, i: (i, 3 * D // LANES + p))],
        out_specs=[ospec] * 3,
        out_shape=[jax.ShapeDtypeStruct((lp, D), F32), jax.ShapeDtypeStruct((lp, D), BF16), jax.ShapeDtypeStruct((lp, D), F32)],
        compiler_params=_params(("parallel", "arbitrary")),
    )(qa, ka, va, proj)


def _fox_gate_bwd(dog, o, proj, lse, qa):
    lp = o.shape[0]
    tr = LANES

    def body(d_ref, o_ref, g_ref, lse_ref, q_ref, do_ref, q2_ref, dgate_ref):
        down, _ = _head_sel(FOX_H, FOX_DH)
        sg = _sigmoid(g_ref[...])
        dv, ov = d_ref[...], o_ref[...]
        do = (dv * sg).astype(BF16).astype(F32)
        dgate_ref[...] = dv * ov * sg * (1.0 - sg)
        delta = _sel_r(do * ov, down)
        _spread(do, -_placed(delta, L_C), do_ref)
        r_, c_ = _iota((D, LANES), 0), _iota((D, LANES), 1)
        lse_c = _sel_r(lse_ref[...], (r_ == c_ * FOX_DH).astype(BF16))
        q2_ref[...] = (q_ref[...].astype(F32) - _placed(lse_c, L_LSE)).astype(BF16)

    spec = pl.BlockSpec((tr, D), lambda i: (i, 0))
    aug = pl.BlockSpec((tr, FOX_AUG), lambda i: (i, 0))
    return pl.pallas_call(
        body, name="fox_gate_bwd", grid=(lp // tr,),
        in_specs=[spec, spec, pl.BlockSpec((tr, D), lambda i: (i, 3)), spec, aug], out_specs=[aug, aug, spec],
        out_shape=[jax.ShapeDtypeStruct((lp, FOX_AUG), BF16), jax.ShapeDtypeStruct((lp, FOX_AUG), BF16),
                   jax.ShapeDtypeStruct((lp, D), F32)],
        compiler_params=_params(("parallel",)),
    )(dog, o, proj, lse, qa)


def _fox_attn_bwd(q2, ka, va, doa):
    lp = q2.shape[0]
    t = _tile(lp, FOX_TQ)
    nb = lp // t

    def body(q_ref, k_ref, v_ref, do_ref, dq_ref, dk_ref, dv_ref, dc_ref, dq_acc, dk_acc, dv_acc, dc_acc):
        j = pl.program_id(1)

        @pl.when(j == 0)
        def _():
            dq_acc[...] = jnp.zeros_like(dq_acc)

        causal = _iota((t, t), 1) <= _iota((t, t), 0)
        for hh in range(2):
            k = k_ref[:, hh * LANES:(hh + 1) * LANES]
            v = v_ref[:, hh * LANES:(hh + 1) * LANES]
            dk_acc[...] = jnp.zeros_like(dk_acc)
            dv_acc[...] = jnp.zeros_like(dv_acc)
            dc_acc[...] = jnp.zeros_like(dc_acc)

            def block(i, diag):
                off = pl.multiple_of(i * t, t)
                q = q_ref[pl.ds(off, t), hh * LANES:(hh + 1) * LANES]
                do = do_ref[pl.ds(off, t), hh * LANES:(hh + 1) * LANES]
                s = nt(q, k)
                if diag:
                    s = jnp.where(causal, s, -1e30)
                p = jnp.exp(s)
                ds = p * nt(do, v)
                dc_acc[...] += jnp.sum(ds, axis=0, keepdims=True)
                dsb = ds.astype(BF16)
                dv_acc[...] += tn(p.astype(BF16), do)
                dk_acc[...] += tn(dsb, q)
                dq_acc[hh, pl.ds(off, t), :] += nn(dsb, k)

            block(j, True)

            def step(i, c):
                block(i, False)
                return c

            lax.fori_loop(j + 1, nb, step, 0)
            left = _iota((t, LANES), 1) < FOX_DH
            if hh == 0:
                dk_ref[...] = dk_acc[...]
                dv_ref[...] = dv_acc[...]
            else:
                dk_ref[...] = jnp.where(left, dk_ref[...], pltpu.roll(dk_acc[...], FOX_DH, 1))
                dv_ref[...] = jnp.where(left, dv_ref[...], pltpu.roll(dv_acc[...], FOX_DH, 1))
            dc_ref[hh] = jnp.broadcast_to(-dc_acc[...], (8, t))

        @pl.when(j == nb - 1)
        def _():
            left = _iota((lp, LANES), 1) < FOX_DH
            dq_ref[...] = jnp.where(left, dq_acc[0], pltpu.roll(dq_acc[1], FOX_DH, 1))

    full = pl.BlockSpec((lp, 2 * LANES), lambda p, j: (0, p))
    kblk = pl.BlockSpec((t, 2 * LANES), lambda p, j: (j, p))
    oblk = pl.BlockSpec((t, LANES), lambda p, j: (j, p))
    return pl.pallas_call(
        body, name="fox_attn_bwd", grid=(FOX_H // 2, nb),
        in_specs=[full, kblk, kblk, full],
        out_specs=[pl.BlockSpec((lp, LANES), lambda p, j: (0, p)), oblk, oblk, pl.BlockSpec((2, 8, t), lambda p, j: (p, 0, j))],
        out_shape=[jax.ShapeDtypeStruct((lp, D), F32)] * 3 + [jax.ShapeDtypeStruct((FOX_H, 8, lp), F32)],
        scratch_shapes=[pltpu.VMEM((2, lp, LANES), F32), pltpu.VMEM((t, LANES), F32), pltpu.VMEM((t, LANES), F32),
                        pltpu.VMEM((1, t), F32)],
        compiler_params=_params(("parallel", "arbitrary")),
    )(q2, ka, va, doa)


def _fox_prep_bwd(proj, b_f, q_gain, k_gain, dqn, dkn, dv, dgate, dct):
    lp = proj.shape[0]
    nb = lp // LANES

    def body(p_ref, bf_ref, qg_ref, kg_ref, dq_ref, dk_ref, dv_ref, dg_ref, dc_ref,
             dp_ref, dqg_ref, dkg_ref, dbf_ref, carry):
        i = pl.program_id(0)

        @pl.when(i == 0)
        def _():
            carry[...] = jnp.zeros_like(carry)
            dqg_ref[...] = jnp.zeros_like(dqg_ref)
            dkg_ref[...] = jnp.zeros_like(dkg_ref)
            dbf_ref[...] = jnp.zeros_like(dbf_ref)

        down, up = _head_sel(FOX_H, FOX_DH)

        def norm_bwd(x, gain, dy, scale, dgain_ref):
            ms = _sel_r(x * x, down) * (1.0 / FOX_DH)
            r = _sel_r(lax.rsqrt(ms + EPS), up)
            u = dy * gain * scale
            mean_xu = _sel_r(_sel_r(x * u, down) * (1.0 / FOX_DH), up)
            dgain_ref[...] += jnp.sum(dy * scale * x * r, axis=0, keepdims=True)
            return r * u - x * (r * r * r) * mean_xu

        dp_ref[:, 0:D] = norm_bwd(p_ref[:, 0:D], qg_ref[...], dq_ref[...], FOX_DH ** -0.5, dqg_ref).astype(BF16)
        dp_ref[:, D:2 * D] = norm_bwd(p_ref[:, D:2 * D], kg_ref[...], dk_ref[...], 1.0, dkg_ref).astype(BF16)
        dp_ref[:, 2 * D:3 * D] = dv_ref[...].astype(BF16)
        dp_ref[:, 3 * D:4 * D] = dg_ref[...].astype(BF16)
        rows = jnp.concatenate([dc_ref[h, 0:1, :] for h in range(FOX_H)] + [jnp.zeros((LANES - FOX_H, LANES), F32)], axis=0)
        dlf = _sel_l(_tri(LANES, upper=True).astype(BF16), rows.T) + carry[0:1, :]
        carry[...] = jnp.broadcast_to(dlf[0:1, :], carry.shape)
        lane = _iota((LANES, LANES), 1)
        z = p_ref[:, 4 * D:4 * D + LANES] + bf_ref[...]
        df = jnp.where(lane < FOX_H, dlf * _sigmoid(-z), 0.0)
        dp_ref[:, 4 * D:4 * D + LANES] = df.astype(BF16)
        dbf_ref[...] += jnp.sum(df, axis=0, keepdims=True)

    rev = lambda i: (nb - 1 - i, 0)
    blk = pl.BlockSpec((LANES, D), rev)
    row = pl.BlockSpec((1, D), lambda i: (0, 0))
    row128 = pl.BlockSpec((1, LANES), lambda i: (0, 0))
    return pl.pallas_call(
        body, name="fox_prep_bwd", grid=(nb,),
        in_specs=[pl.BlockSpec((LANES, FOX_INP), rev), row128, row, row, blk, blk, blk, blk,
                  pl.BlockSpec((FOX_H, 8, LANES), lambda i: (0, 0, nb - 1 - i))],
        out_specs=[pl.BlockSpec((LANES, FOX_INP), rev), row, row, row128],
        out_shape=[jax.ShapeDtypeStruct((lp, FOX_INP), BF16), jax.ShapeDtypeStruct((1, D), F32),
                   jax.ShapeDtypeStruct((1, D), F32), jax.ShapeDtypeStruct((1, LANES), F32)],
        scratch_shapes=[pltpu.VMEM((8, LANES), F32)],
        compiler_params=_params(("arbitrary",)),
    )(proj, jnp.pad(b_f, (0, LANES - FOX_H)).reshape(1, LANES), jnp.tile(q_gain, FOX_H).reshape(1, D),
      jnp.tile(k_gain, FOX_H).reshape(1, D), dqn, dkn, dv, dgate, dct)


def _gla_gates(p_ref, wa_ref, ba_ref):
    a_lr = p_ref[:, 3072:3072 + LANES]
    z = nn(a_lr.astype(BF16), wa_ref[...].astype(BF16)) + ba_ref[...]
    g = _log_sigmoid(z) * (1.0 / GLA_NORM)
    b = _sel_l(_tri(CHUNK).astype(BF16), g)
    return a_lr, z, b


def _gla_head_fwd(q, k, v, b, st0):
    eb = jnp.exp(b)
    bl = b[CHUNK - 1:CHUNK, :]
    qe, ke, kd = q * eb, k * jnp.exp(-b), k * jnp.exp(bl - b)
    a = jnp.where(_tri(CHUNK), nt(qe, ke), 0.0)
    o = nn(a, v) + nt(qe, st0)
    st1 = st0 * jnp.exp(bl) + tn(v, kd)
    return o, st1, (qe, ke, kd, a, bl)


def _gla_fwd(proj, w_alpha2, b_alpha, o_gain):
    lp = proj.shape[0]
    nc = lp // CHUNK

    def body(p_ref, wa_ref, ba_ref, og_ref, o_ref, y_ref, s_ref, st):
        @pl.when(pl.program_id(0) == 0)
        def _():
            st[...] = jnp.zeros_like(st)

        _, _, b = _gla_gates(p_ref, wa_ref, ba_ref)
        for h in range(GLA_H):
            q = p_ref[:, h * GLA_DK:(h + 1) * GLA_DK] * (GLA_DK ** -0.5)
            k = p_ref[:, GLA_QK + h * GLA_DK:GLA_QK + (h + 1) * GLA_DK]
            v = p_ref[:, 2 * GLA_QK + h * GLA_DV:2 * GLA_QK + (h + 1) * GLA_DV]
            r = p_ref[:, 2 * GLA_QK + GLA_V + h * GLA_DV:2 * GLA_QK + GLA_V + (h + 1) * GLA_DV]
            st0 = st[h]
            s_ref[0, h] = st0
            o, st1, _ = _gla_head_fwd(q, k, v, b[:, h * GLA_DK:(h + 1) * GLA_DK], st0)
            st[h] = st1
            o_ref[:, h * GLA_DV:(h + 1) * GLA_DV] = o
            rs = lax.rsqrt(jnp.mean(o * o, axis=-1, keepdims=True) + EPS)
            y_ref[:, h * GLA_DV:(h + 1) * GLA_DV] = (o * rs * og_ref[...] * _silu(r)).astype(BF16)

    blk = pl.BlockSpec((CHUNK, D), lambda i: (i, 0))
    return pl.pallas_call(
        body, name="gla_fwd", grid=(nc,),
        in_specs=[pl.BlockSpec((CHUNK, GLA_INP), lambda i: (i, 0)), pl.BlockSpec((LANES, GLA_QK), lambda i: (0, 0)),
                  pl.BlockSpec((1, GLA_QK), lambda i: (0, 0)), pl.BlockSpec((1, GLA_DV), lambda i: (0, 0))],
        out_specs=[blk, blk, pl.BlockSpec((1, GLA_H, GLA_DV, GLA_DK), lambda i: (i, 0, 0, 0))],
        out_shape=[jax.ShapeDtypeStruct((lp, D), F32), jax.ShapeDtypeStruct((lp, D), BF16),
                   jax.ShapeDtypeStruct((nc, GLA_H, GLA_DV, GLA_DK), F32)],
        scratch_shapes=[pltpu.VMEM((GLA_H, GLA_DV, GLA_DK), F32)],
        compiler_params=_params(("arbitrary",)),
    )(proj, jnp.pad(w_alpha2, ((0, LANES - GLA_RANK), (0, 0))), b_alpha.reshape(1, GLA_QK), o_gain.reshape(1, GLA_DV))


def _gla_bwd(proj, w_alpha2, b_alpha, o_gain, o, states, dy):
    lp = proj.shape[0]
    nc = lp // CHUNK

    def body(p_ref, wa_ref, ba_ref, og_ref, o_ref, s_ref, dy_ref, dp_ref, dwa_ref, dba_ref, dog_ref, dst):
        @pl.when(pl.program_id(0) == 0)
        def _():
            dst[...] = jnp.zeros_like(dst)
            dwa_ref[...] = jnp.zeros_like(dwa_ref)
            dba_ref[...] = jnp.zeros_like(dba_ref)
            dog_ref[...] = jnp.zeros_like(dog_ref)

        a_lr, z, b_all = _gla_gates(p_ref, wa_ref, ba_ref)
        last_row = _iota((CHUNK, GLA_DK), 0) == CHUNK - 1
        rev = _tri(CHUNK, upper=True).astype(BF16)
        dg_parts = []
        for h in range(GLA_H):
            scale = GLA_DK ** -0.5
            q = p_ref[:, h * GLA_DK:(h + 1) * GLA_DK] * scale
            k = p_ref[:, GLA_QK + h * GLA_DK:GLA_QK + (h + 1) * GLA_DK]
            v = p_ref[:, 2 * GLA_QK + h * GLA_DV:2 * GLA_QK + (h + 1) * GLA_DV]
            r = p_ref[:, 2 * GLA_QK + GLA_V + h * GLA_DV:2 * GLA_QK + GLA_V + (h + 1) * GLA_DV]
            b = b_all[:, h * GLA_DK:(h + 1) * GLA_DK]
            st0 = s_ref[0, h]
            dst1 = dst[h]
            ov = o_ref[:, h * GLA_DV:(h + 1) * GLA_DV]
            dyv = dy_ref[:, h * GLA_DV:(h + 1) * GLA_DV]
            rs = lax.rsqrt(jnp.mean(ov * ov, axis=-1, keepdims=True) + EPS)
            on = ov * rs
            dr = dyv * on * og_ref[...] * _dsilu(r)
            don = dyv * _silu(r)
            dog_ref[...] += jnp.sum(don * on, axis=0, keepdims=True)
            u = don * og_ref[...]
            do = rs * u - ov * (rs * rs * rs) * jnp.mean(ov * u, axis=-1, keepdims=True)
            eb = jnp.exp(b)
            _, _, (qe, ke, kd, a, bl) = _gla_head_fwd(q, k, v, b, st0)
            da = jnp.where(_tri(CHUNK), nt(do, v), 0.0)
            dkd = nn(v, dst1)
            dvv = tn(a, do) + nt(kd, dst1)
            dqe = nn(da, ke) + nn(do, st0)
            dke = tn(da, qe)
            ebl = jnp.exp(bl)
            dst[h] = dst1 * ebl + tn(do, qe)
            db = dqe * qe - dke * ke - dkd * kd
            db_last = jnp.sum(dkd * kd, axis=0, keepdims=True) + jnp.sum(dst1 * st0, axis=0, keepdims=True) * ebl
            db = db + jnp.where(last_row, db_last, 0.0)
            dg_parts.append(_sel_l(rev, db))
            dp_ref[:, h * GLA_DK:(h + 1) * GLA_DK] = (dqe * eb * scale).astype(BF16)
            dp_ref[:, GLA_QK + h * GLA_DK:GLA_QK + (h + 1) * GLA_DK] = (dke * jnp.exp(-b) + dkd * jnp.exp(bl - b)).astype(BF16)
            dp_ref[:, 2 * GLA_QK + h * GLA_DV:2 * GLA_QK + (h + 1) * GLA_DV] = dvv.astype(BF16)
            dp_ref[:, 2 * GLA_QK + GLA_V + h * GLA_DV:2 * GLA_QK + GLA_V + (h + 1) * GLA_DV] = dr.astype(BF16)
        dg = jnp.concatenate(dg_parts, axis=1)
        dz = dg * (1.0 / GLA_NORM) * _sigmoid(-z)
        dzb = dz.astype(BF16)
        dp_ref[:, 3072:3072 + LANES] = nt(dzb, wa_ref[...].astype(BF16)).astype(BF16)
        dwa_ref[...] += tn(a_lr.astype(BF16), dzb)
        dba_ref[...] += jnp.sum(dz, axis=0, keepdims=True)

    rv = lambda i: (nc - 1 - i, 0)
    blk = pl.BlockSpec((CHUNK, D), rv)
    fixed = lambda r, c: pl.BlockSpec((r, c), lambda i: (0, 0))
    return pl.pallas_call(
        body, name="gla_bwd", grid=(nc,),
        in_specs=[pl.BlockSpec((CHUNK, GLA_INP), rv), fixed(LANES, GLA_QK), fixed(1, GLA_QK), fixed(1, GLA_DV), blk,
                  pl.BlockSpec((1, GLA_H, GLA_DV, GLA_DK), lambda i: (nc - 1 - i, 0, 0, 0)), blk],
        out_specs=[pl.BlockSpec((CHUNK, GLA_INP), rv), fixed(LANES, GLA_QK), fixed(1, GLA_QK), fixed(1, GLA_DV)],
        out_shape=[jax.ShapeDtypeStruct((lp, GLA_INP), BF16), jax.ShapeDtypeStruct((LANES, GLA_QK), F32),
                   jax.ShapeDtypeStruct((1, GLA_QK), F32), jax.ShapeDtypeStruct((1, GLA_DV), F32)],
        scratch_shapes=[pltpu.VMEM((GLA_H, GLA_DV, GLA_DK), F32)],
        compiler_params=_params(("arbitrary",)),
    )(proj, jnp.pad(w_alpha2, ((0, LANES - GLA_RANK), (0, 0))), b_alpha.reshape(1, GLA_QK), o_gain.reshape(1, GLA_DV),
      o, states, dy)


HI = lax.Precision.HIGHEST


def _gdn_pre(prev_ref, p_ref, cw_ref, al_ref, dt_ref):
    xc = jnp.concatenate([prev_ref[:, 0:GDN_CONV], p_ref[:, 0:GDN_CONV]], axis=0)
    shifted = [pltpu.roll(xc, 3 - j, 0)[CHUNK:, :] if j < 3 else xc[CHUNK:, :] for j in range(4)]
    conv = sum(shifted[j] * cw_ref[j:j + 1, :] for j in range(4))
    act = _silu(conv)
    slab = p_ref[:, 4096:4096 + LANES]
    lane = _iota((CHUNK, LANES), 1)
    zs = slab + dt_ref[...]
    g = jnp.where(lane < GDN_H, -jnp.exp(al_ref[...]) * _softplus(zs), 0.0)
    bs = _sel_l(_tri(CHUNK).astype(BF16), g)
    beta = _sigmoid(slab)
    return shifted, conv, act, slab, zs, g, bs, beta


def _l2n(x):
    r = lax.rsqrt(jnp.sum(x * x, axis=-1, keepdims=True) + EPS)
    return x * r, r


def _gdn_chunk_fwd(q, k, v, beta, bcol, brow, s0):
    hs = range(len(q))
    ii, jj = _iota((CHUNK, CHUNK), 0), _iota((CHUNK, CHUNK), 1)
    low, eye = ii >= jj, (ii == jj).astype(F32)
    dm = [jnp.where(low, jnp.exp(jnp.where(low, bcol[h] - brow[h], 0.0)), 0.0) for h in hs]
    dstrict = [jnp.where(ii > jj, dm[h], 0.0) for h in hs]
    eb = [jnp.exp(bcol[h]) for h in hs]
    bl = [bcol[h][CHUNK - 1:CHUNK, :] for h in hs]
    kb = [k[h] * beta[h] for h in hs]
    vb = [v[h] * beta[h] for h in hs]
    nmat = [nt(kb[h], k[h]) * dstrict[h] for h in hs]
    x = [eye - nmat[h] for h in hs]
    pw = [nn(nmat[h], nmat[h], precision=HI) for h in hs]
    for it in range(5):
        x = [x[h] + nn(x[h], pw[h], precision=HI) for h in hs]
        if it < 4:
            pw = [nn(pw[h], pw[h], precision=HI) for h in hs]
    kbe = [kb[h] * eb[h] for h in hs]
    u = [nn(x[h], vb[h], precision=HI) for h in hs]
    w = [nn(x[h], kbe[h], precision=HI) for h in hs]
    vn = [u[h] - nn(w[h], s0[h]) for h in hs]
    pm = [nt(q[h], k[h]) * dm[h] for h in hs]
    qe = [q[h] * eb[h] for h in hs]
    o = [nn(pm[h], vn[h]) + nn(qe[h], s0[h]) for h in hs]
    kd = [k[h] * jnp.exp(bl[h] - bcol[h]) for h in hs]
    s1 = [s0[h] * jnp.exp(bl[h]) + tn(kd[h], vn[h]) for h in hs]
    return o, s1, dict(dm=dm, dstrict=dstrict, eb=eb, bl=bl, kb=kb, vb=vb, nmat=nmat, tinv=x, kbe=kbe, u=u, w=w, vn=vn,
                       pm=pm, qe=qe, kd=kd)


def _gdn_heads(act, beta_slab, bs, h):
    qa = act[:, h * GDN_DK:(h + 1) * GDN_DK]
    ka = act[:, GDN_H * GDN_DK + h * GDN_DK:GDN_H * GDN_DK + (h + 1) * GDN_DK]
    v = act[:, 2 * GDN_H * GDN_DK + h * GDN_DV:2 * GDN_H * GDN_DK + (h + 1) * GDN_DV]
    return qa, ka, v, beta_slab[:, GDN_H + h:GDN_H + h + 1], bs[:, h:h + 1]


def _gdn_fwd(proj, conv_w, a_log, dt_bias, o_gain):
    lp = proj.shape[0]
    nc = lp // CHUNK

    def body(prev_ref, p_ref, cw_ref, al_ref, dt_ref, og_ref, o_ref, y_ref, s_ref, st):
        @pl.when(pl.program_id(0) == 0)
        def _():
            st[...] = jnp.zeros_like(st)

        _, _, act, _, _, _, bs, beta = _gdn_pre(prev_ref, p_ref, cw_ref, al_ref, dt_ref)
        bst = bs.T
        hs = range(GDN_H)
        parts = [_gdn_heads(act, beta, bs, h) for h in hs]
        q = [_l2n(parts[h][0])[0] * (GDN_DK ** -0.5) for h in hs]
        k = [_l2n(parts[h][1])[0] for h in hs]
        s0 = [st[h] for h in hs]
        for h in hs:
            s_ref[0, h] = s0[h]
        o, s1, _ = _gdn_chunk_fwd(q, k, [parts[h][2] for h in hs], [parts[h][3] for h in hs], [parts[h][4] for h in hs],
                                  [bst[h:h + 1, :] for h in hs], s0)
        for h in hs:
            st[h] = s1[h]
            o_ref[:, h * GDN_DV:(h + 1) * GDN_DV] = o[h]
            rs = lax.rsqrt(jnp.mean(o[h] * o[h], axis=-1, keepdims=True) + EPS)
            gate = p_ref[:, GDN_CONV + h * GDN_DV:GDN_CONV + (h + 1) * GDN_DV]
            y_ref[:, h * GDN_DV:(h + 1) * GDN_DV] = (o[h] * rs * og_ref[...] * _silu(gate)).astype(BF16)

    blk = pl.BlockSpec((CHUNK, D), lambda i: (i, 0))
    fixed = lambda r, c: pl.BlockSpec((r, c), lambda i: (0, 0))
    return pl.pallas_call(
        body, name="gdn_fwd", grid=(nc,),
        in_specs=[pl.BlockSpec((CHUNK, GDN_INP), lambda i: (jnp.maximum(i - 1, 0), 0)),
                  pl.BlockSpec((CHUNK, GDN_INP), lambda i: (i, 0)), fixed(8, GDN_CONV), fixed(1, LANES), fixed(1, LANES),
                  fixed(1, GDN_DV)],
        out_specs=[blk, blk, pl.BlockSpec((1, GDN_H, GDN_DK, GDN_DV), lambda i: (i, 0, 0, 0))],
        out_shape=[jax.ShapeDtypeStruct((lp, D), F32), jax.ShapeDtypeStruct((lp, D), BF16),
                   jax.ShapeDtypeStruct((nc, GDN_H, GDN_DK, GDN_DV), F32)],
        scratch_shapes=[pltpu.VMEM((GDN_H, GDN_DK, GDN_DV), F32)],
        compiler_params=_params(("arbitrary",)),
    )(proj, proj, jnp.pad(conv_w.reshape(4, GDN_CONV), ((0, 4), (0, 0))), jnp.pad(a_log, (0, LANES - GDN_H)).reshape(1, LANES),
      jnp.pad(dt_bias, (0, LANES - GDN_H)).reshape(1, LANES), o_gain.reshape(1, GDN_DV))


def _gdn_bwd(proj, conv_w, a_log, dt_bias, o_gain, o, states, dy):
    lp = proj.shape[0]
    nc = lp // CHUNK

    def body(prev_ref, p_ref, cw_ref, al_ref, dt_ref, og_ref, o_ref, s_ref, dy_ref,
             dp_ref, dcw_ref, dal_ref, ddt_ref, dog_ref, dst, dconv_next):
        @pl.when(pl.program_id(0) == 0)
        def _():
            dst[...] = jnp.zeros_like(dst)
            dconv_next[...] = jnp.zeros_like(dconv_next)
            dcw_ref[...] = jnp.zeros_like(dcw_ref)
            dal_ref[...] = jnp.zeros_like(dal_ref)
            ddt_ref[...] = jnp.zeros_like(ddt_ref)
            dog_ref[...] = jnp.zeros_like(dog_ref)

        shifted, conv, act, slab, zs, g, bs, beta = _gdn_pre(prev_ref, p_ref, cw_ref, al_ref, dt_ref)
        bst = bs.T
        lane = _iota((CHUNK, LANES), 1)
        ones = jnp.ones((CHUNK, LANES), F32)
        db_slab = jnp.zeros((CHUNK, LANES), F32)
        dbeta_slab = jnp.zeros((CHUNK, LANES), F32)
        last_row = _iota((CHUNK, 1), 0) == CHUNK - 1
        hs = range(GDN_H)
        scale = GDN_DK ** -0.5
        parts = [_gdn_heads(act, beta, bs, h) for h in hs]
        qa, ka, v = [parts[h][0] for h in hs], [parts[h][1] for h in hs], [parts[h][2] for h in hs]
        bet, bcol = [parts[h][3] for h in hs], [parts[h][4] for h in hs]
        qn_ = [_l2n(qa[h]) for h in hs]
        kn_ = [_l2n(ka[h]) for h in hs]
        q = [qn_[h][0] * scale for h in hs]
        k, rq, rk = [kn_[h][0] for h in hs], [qn_[h][1] for h in hs], [kn_[h][1] for h in hs]
        s0 = [s_ref[0, h] for h in hs]
        ds1 = [dst[h] for h in hs]
        do = []
        for h in hs:
            ov = o_ref[:, h * GDN_DV:(h + 1) * GDN_DV]
            dyv = dy_ref[:, h * GDN_DV:(h + 1) * GDN_DV]
            gate = p_ref[:, GDN_CONV + h * GDN_DV:GDN_CONV + (h + 1) * GDN_DV]
            rs = lax.rsqrt(jnp.mean(ov * ov, axis=-1, keepdims=True) + EPS)
            on = ov * rs
            dp_ref[:, GDN_CONV + h * GDN_DV:GDN_CONV + (h + 1) * GDN_DV] = (dyv * on * og_ref[...] * _dsilu(gate)).astype(BF16)
            don = dyv * _silu(gate)
            dog_ref[...] += jnp.sum(don * on, axis=0, keepdims=True)
            uu = don * og_ref[...]
            do.append(rs * uu - ov * (rs * rs * rs) * jnp.mean(ov * uu, axis=-1, keepdims=True))
        _, _, f = _gdn_chunk_fwd(q, k, v, bet, bcol, [bst[h:h + 1, :] for h in hs], s0)
        dm, dstrict, eb, bl, kb, nmat, tinv = f["dm"], f["dstrict"], f["eb"], f["bl"], f["kb"], f["nmat"], f["tinv"]
        kbe, u, w, vn, pm, qe, kd = f["kbe"], f["u"], f["w"], f["vn"], f["pm"], f["qe"], f["kd"]
        ebl = [jnp.exp(bl[h]) for h in hs]
        dvn = [tn(pm[h], do[h]) + nn(kd[h], ds1[h]) for h in hs]
        dpr = [nt(do[h], vn[h]) for h in hs]
        dqe = [nt(do[h], s0[h]) for h in hs]
        dkd = [nt(vn[h], ds1[h]) for h in hs]
        for h in hs:
            dst[h] = ds1[h] * ebl[h] + tn(qe[h], do[h]) - tn(w[h], dvn[h])
        du_ = [tn(tinv[h], dvn[h], precision=HI) for h in hs]
        dw_ = [tn(tinv[h], -nt(dvn[h], s0[h]), precision=HI) for h in hs]
        dn = [-(nt(du_[h], u[h]) + nt(dw_[h], w[h])) for h in hs]
        dqk = [dpr[h] * dm[h] for h in hs]
        dkk = [dn[h] * dstrict[h] for h in hs]
        gsum = [dpr[h] * pm[h] + dn[h] * nmat[h] for h in hs]
        dkb = [nn(dkk[h], k[h]) + dw_[h] * eb[h] for h in hs]
        dk = [tn(dkk[h], kb[h]) + tn(dqk[h], q[h]) + dkd[h] * jnp.exp(bl[h] - bcol[h]) + dkb[h] * bet[h] for h in hs]
        dq = [nn(dqk[h], k[h]) + dqe[h] * eb[h] for h in hs]
        colsum = [tn(gsum[h], ones, precision=HI)[:, 0:1] for h in hs]
        dact_q, dact_k, dact_v = [], [], []
        for h in hs:
            dbeta = jnp.sum(dkb[h] * k[h], axis=-1, keepdims=True) + jnp.sum(du_[h] * v[h], axis=-1, keepdims=True)
            skd = jnp.sum(dkd[h] * kd[h], axis=-1, keepdims=True)
            db = (jnp.sum(gsum[h], axis=-1, keepdims=True) - colsum[h] + jnp.sum(dqe[h] * qe[h], axis=-1, keepdims=True)
                  + jnp.sum(dw_[h] * kbe[h], axis=-1, keepdims=True) - skd)
            db_last = jnp.sum(skd, axis=0, keepdims=True) + jnp.sum(ds1[h] * s0[h]) * ebl[h]
            db = db + jnp.where(last_row, db_last, 0.0)
            db_slab = db_slab + jnp.where(lane == h, db, 0.0)
            dbeta_slab = dbeta_slab + jnp.where(lane == GDN_H + h, dbeta, 0.0)
            dqn = dq[h] * scale
            dact_q.append(rq[h] * dqn - qa[h] * (rq[h] * rq[h] * rq[h]) * jnp.sum(qa[h] * dqn, axis=-1, keepdims=True))
            dact_k.append(rk[h] * dk[h] - ka[h] * (rk[h] * rk[h] * rk[h]) * jnp.sum(ka[h] * dk[h], axis=-1, keepdims=True))
            dact_v.append(du_[h] * bet[h])
        dact = jnp.concatenate(dact_q + dact_k + dact_v, axis=1)
        dconv = dact * _dsilu(conv)
        for j in range(4):
            dcw_ref[j:j + 1, :] += jnp.sum(dconv * shifted[j], axis=0, keepdims=True)
        dcat = jnp.concatenate([dconv, dconv_next[...]], axis=0)
        dx = dconv * cw_ref[3:4, :]
        for j in range(3):
            dx = dx + pltpu.roll(dcat, 2 * CHUNK - (3 - j), 0)[:CHUNK, :] * cw_ref[j:j + 1, :]
        dconv_next[...] = dconv
        dp_ref[:, 0:GDN_CONV] = dx.astype(BF16)
        dg = _sel_l(_tri(CHUNK, upper=True).astype(BF16), db_slab)
        da = dg * (-jnp.exp(al_ref[...])) * _sigmoid(zs)
        da = jnp.where(lane < GDN_H, da, 0.0)
        dal_ref[...] += jnp.sum(dg * g, axis=0, keepdims=True)
        ddt_ref[...] += jnp.sum(da, axis=0, keepdims=True)
        dp_ref[:, 4096:4096 + LANES] = (da + dbeta_slab * beta * (1.0 - beta)).astype(BF16)

    rv = lambda i: (nc - 1 - i, 0)
    blk = pl.BlockSpec((CHUNK, D), rv)
    fixed = lambda r, c: pl.BlockSpec((r, c), lambda i: (0, 0))
    return pl.pallas_call(
        body, name="gdn_bwd", grid=(nc,),
        in_specs=[pl.BlockSpec((CHUNK, GDN_INP), lambda i: (jnp.maximum(nc - 2 - i, 0), 0)),
                  pl.BlockSpec((CHUNK, GDN_INP), rv), fixed(8, GDN_CONV), fixed(1, LANES), fixed(1, LANES), fixed(1, GDN_DV),
                  blk, pl.BlockSpec((1, GDN_H, GDN_DK, GDN_DV), lambda i: (nc - 1 - i, 0, 0, 0)), blk],
        out_specs=[pl.BlockSpec((CHUNK, GDN_INP), rv), fixed(8, GDN_CONV), fixed(1, LANES), fixed(1, LANES), fixed(1, GDN_DV)],
        out_shape=[jax.ShapeDtypeStruct((lp, GDN_INP), BF16), jax.ShapeDtypeStruct((8, GDN_CONV), F32),
                   jax.ShapeDtypeStruct((1, LANES), F32), jax.ShapeDtypeStruct((1, LANES), F32),
                   jax.ShapeDtypeStruct((1, GDN_DV), F32)],
        scratch_shapes=[pltpu.VMEM((GDN_H, GDN_DK, GDN_DV), F32), pltpu.VMEM((CHUNK, GDN_CONV), F32)],
        compiler_params=_params(("arbitrary",)),
    )(proj, proj, jnp.pad(conv_w.reshape(4, GDN_CONV), ((0, 4), (0, 0))), jnp.pad(a_log, (0, LANES - GDN_H)).reshape(1, LANES),
      jnp.pad(dt_bias, (0, LANES - GDN_H)).reshape(1, LANES), o_gain.reshape(1, GDN_DV), o, states, dy)


def _coords():
    return lax.axis_index("x"), lax.axis_index("y"), lax.axis_index("c")


def _other_chips(x, y):
    return [(1 - x, y, 2 * (1 - x) + y), (x, 1 - y, 2 * x + 1 - y), (1 - x, 1 - y, 2 * (1 - x) + 1 - y)]


def _gather8(v, *, reduce, name):
    r, c = v.shape

    def body(v_ref, out_ref, *scratch):
        if reduce:
            buf, send_sems, recv_sems = scratch
        else:
            buf = out_ref
            send_sems, recv_sems = scratch
        x, y, cc = _coords()
        me = 4 * x + 2 * y + cc
        buf[me] = v_ref[...]
        copies = []
        for k in range(1, 8):
            px, py, pc = x ^ (k >> 2), y ^ ((k >> 1) & 1), cc ^ (k & 1)
            copies.append(pltpu.make_async_remote_copy(
                src_ref=v_ref, dst_ref=buf.at[me], send_sem=send_sems.at[k - 1], recv_sem=recv_sems.at[k - 1],
                device_id=(px, py, pc), device_id_type=MESH))
        for cp in copies:
            cp.start()
        for k in range(1, 8):
            peer = (x ^ (k >> 2)) * 4 + (y ^ ((k >> 1) & 1)) * 2 + (cc ^ (k & 1))
            pltpu.make_async_remote_copy(
                src_ref=v_ref, dst_ref=buf.at[peer], send_sem=send_sems.at[k - 1], recv_sem=recv_sems.at[k - 1],
                device_id=(x, y, cc), device_id_type=MESH).wait_recv()
        for cp in copies:
            cp.wait_send()
        if reduce:
            acc = buf[0]
            for d in range(1, 8):
                acc = acc + buf[d]
            out_ref[...] = acc

    scratch = [pltpu.SemaphoreType.DMA((7,)), pltpu.SemaphoreType.DMA((7,))]
    if reduce:
        scratch = [pltpu.VMEM((8, r, c), F32)] + scratch
    return pl.pallas_call(
        body, name=name, in_specs=[VM], out_specs=VM,
        out_shape=jax.ShapeDtypeStruct((r, c) if reduce else (8, r, c), F32),
        scratch_shapes=scratch, compiler_params=_params(),
    )(v)


def _ag_weights(w4):
    _, r, c = w4.shape
    half = r // 2

    def body(w_ref, out_ref, send_sems, recv_sems):
        x, y, cc = _coords()
        p = 2 * x + y
        chips = _other_chips(x, y)

        def rows(chip, hf):
            return out_ref.at[chip, pl.ds(hf * half, half), :]

        first = [pltpu.make_async_remote_copy(
            src_ref=rows(p, cc), dst_ref=rows(p, cc), send_sem=send_sems.at[k],
            recv_sem=recv_sems.at[k], device_id=(cx, cy, cc), device_id_type=MESH) for k, (cx, cy, _) in enumerate(chips)]
        for cp in first:
            cp.start()
        passed = []
        for k, (_, _, blk) in enumerate(chips):
            pltpu.make_async_remote_copy(
                src_ref=rows(blk, cc), dst_ref=rows(blk, cc), send_sem=send_sems.at[k], recv_sem=recv_sems.at[k],
                device_id=(x, y, cc), device_id_type=MESH).wait_recv()
            fw = pltpu.make_async_remote_copy(
                src_ref=rows(blk, cc), dst_ref=rows(blk, cc), send_sem=send_sems.at[3 + k], recv_sem=recv_sems.at[3 + k],
                device_id=(x, y, 1 - cc), device_id_type=MESH)
            fw.start()
            passed.append(fw)
        for k, (_, _, blk) in enumerate(chips):
            pltpu.make_async_remote_copy(
                src_ref=rows(blk, 1 - cc), dst_ref=rows(blk, 1 - cc), send_sem=send_sems.at[3 + k], recv_sem=recv_sems.at[3 + k],
                device_id=(x, y, cc), device_id_type=MESH).wait_recv()
        for cp in first + passed:
            cp.wait_send()

    return pl.pallas_call(
        body, name="ag_weights", in_specs=[ANY], out_specs=ANY, out_shape=jax.ShapeDtypeStruct(w4.shape, w4.dtype),
        scratch_shapes=[pltpu.SemaphoreType.DMA((6,)), pltpu.SemaphoreType.DMA((6,))], input_output_aliases={0: 0},
        compiler_params=_params(),
    )(w4)


def _swap_halves(g, *, name):
    nb, r, c = g.shape
    half = r // 2

    def body(g_ref, out_ref, send_sem, recv_sem):
        x, y, cc = _coords()
        cp = pltpu.make_async_remote_copy(
            src_ref=g_ref.at[:, pl.ds((1 - cc) * half, half), :], dst_ref=out_ref, send_sem=send_sem, recv_sem=recv_sem,
            device_id=(x, y, 1 - cc), device_id_type=MESH)
        cp.start()
        cp.wait()

    return pl.pallas_call(
        body, name=name, in_specs=[ANY], out_specs=ANY, out_shape=jax.ShapeDtypeStruct((nb, half, c), g.dtype),
        scratch_shapes=[pltpu.SemaphoreType.DMA, pltpu.SemaphoreType.DMA], compiler_params=_params(),
    )(g)


def _my_half_index():
    return lax.axis_index("c").astype(jnp.int32).reshape(1)


def _add_halves(g, got):
    nb, r, c = g.shape
    half = r // 2
    tr = _tile(half, 512, 16)
    nt_ = half // tr

    def body(c_ref, a_ref, b_ref, o_ref):
        o_ref[...] = (a_ref[...].astype(F32) + b_ref[...].astype(F32)).astype(BF16)

    return pl.pallas_call(
        body, name="rs_add_sibling",
        grid_spec=pltpu.PrefetchScalarGridSpec(
            num_scalar_prefetch=1, grid=(nb, nt_),
            in_specs=[pl.BlockSpec((1, tr, c), lambda b, i, cr: (b, cr[0] * nt_ + i, 0)),
                      pl.BlockSpec((1, tr, c), lambda b, i, cr: (b, i, 0))],
            out_specs=pl.BlockSpec((1, tr, c), lambda b, i, cr: (b, i, 0))),
        out_shape=jax.ShapeDtypeStruct((nb, half, c), BF16), compiler_params=_params(("parallel", "parallel")),
    )(_my_half_index(), g, got)


def _scatter_chips(s):
    nb, hrows, c = s.shape

    def body(s_ref, out_ref, send_sems, recv_sems):
        x, y, cc = _coords()
        p = 2 * x + y
        chips = _other_chips(x, y)
        sends = [pltpu.make_async_remote_copy(
            src_ref=s_ref.at[blk], dst_ref=out_ref.at[k], send_sem=send_sems.at[k], recv_sem=recv_sems.at[k],
            device_id=(cx, cy, cc), device_id_type=MESH) for k, (cx, cy, blk) in enumerate(chips)]
        for cp in sends:
            cp.start()
        for k in range(3):
            pltpu.make_async_remote_copy(
                src_ref=s_ref.at[p], dst_ref=out_ref.at[k], send_sem=send_sems.at[k], recv_sem=recv_sems.at[k],
                device_id=(x, y, cc), device_id_type=MESH).wait_recv()
        for cp in sends:
            cp.wait_send()

    return pl.pallas_call(
        body, name="rs_scatter", in_specs=[ANY], out_specs=ANY, out_shape=jax.ShapeDtypeStruct((3, hrows, c), s.dtype),
        scratch_shapes=[pltpu.SemaphoreType.DMA((3,)), pltpu.SemaphoreType.DMA((3,))], compiler_params=_params(),
    )(s)


def _sum_chips(s, got):
    nb, hrows, c = s.shape
    tr = _tile(hrows, 512, 16)

    def body(idx_ref, own_ref, got_ref, o_ref):
        p = idx_ref[0]
        own = own_ref[0].astype(F32)
        parts = [got_ref[k].astype(F32) for k in range(3)]
        acc = jnp.zeros_like(own)
        for q in range(4):
            val = own
            for k, rel in enumerate((2, 1, 3)):
                val = jnp.where((p ^ rel) == q, parts[k], val)
            acc = acc + val
        o_ref[...] = acc

    idx = (2 * lax.axis_index("x") + lax.axis_index("y")).astype(jnp.int32).reshape(1)
    return pl.pallas_call(
        body, name="rs_sum_chips",
        grid_spec=pltpu.PrefetchScalarGridSpec(
            num_scalar_prefetch=1, grid=(hrows // tr,),
            in_specs=[pl.BlockSpec((1, tr, c), lambda i, pr: (pr[0], i, 0)), pl.BlockSpec((3, tr, c), lambda i, pr: (0, i, 0))],
            out_specs=pl.BlockSpec((tr, c), lambda i, pr: (i, 0))),
        out_shape=jax.ShapeDtypeStruct((hrows, c), F32), compiler_params=_params(("parallel",)),
    )(idx, s, got)


def _swap_sibling(t):
    def body(t_ref, out_ref, send_sem, recv_sem):
        x, y, cc = _coords()
        cp = pltpu.make_async_remote_copy(src_ref=t_ref, dst_ref=out_ref, send_sem=send_sem, recv_sem=recv_sem,
                                          device_id=(x, y, 1 - cc), device_id_type=MESH)
        cp.start()
        cp.wait()

    return pl.pallas_call(
        body, name="rs_join", in_specs=[ANY], out_specs=ANY, out_shape=jax.ShapeDtypeStruct(t.shape, t.dtype),
        scratch_shapes=[pltpu.SemaphoreType.DMA, pltpu.SemaphoreType.DMA], compiler_params=_params(),
    )(t)


def _reduce_scatter(g):
    got = _swap_halves(g, name="rs_swap")
    s = _add_halves(g, got)
    recv = _scatter_chips(s)
    t = _sum_chips(s, recv)
    r = _swap_sibling(t)
    first = lax.axis_index("c") == 0
    return jnp.concatenate([jnp.where(first, t, r), jnp.where(first, r, t)], axis=0)


_BIG = (("w_gate_up", 2), ("w_down", 1), ("fox_w_in", 2), ("fox_w_out", 1), ("gla_w_in", 2), ("gla_w_out", 1),
        ("gdn_w_in", 2), ("gdn_w_out", 1))
_SMALL_SHARDED = (("meta_tokens", 1), ("gla_w_alpha2", 2), ("gdn_conv_w", 3))
_REPLICATED = ("norm_mix", "norm_ffn", "fox_b_f", "fox_q_gain", "fox_k_gain", "gla_b_alpha", "gla_o_gain",
               "gdn_a_log", "gdn_dt_bias", "gdn_o_gain")
_WEIGHTS = ("meta_tokens", "norm_mix", "norm_ffn", "w_gate_up", "w_down", "fox_w_in", "fox_b_f", "fox_q_gain",
            "fox_k_gain", "fox_w_out", "gla_w_in", "gla_w_alpha2", "gla_b_alpha", "gla_o_gain", "gla_w_out",
            "gdn_w_in", "gdn_conv_w", "gdn_a_log", "gdn_dt_bias", "gdn_o_gain", "gdn_w_out")
_PACK_ROWS = 512


def _pack(arrays, width, row_mult, dtype):
    flat = jnp.concatenate([a.astype(dtype).reshape(-1) for a in arrays])
    per = width * row_mult
    n = -(-flat.shape[0] // per) * per
    return jnp.pad(flat, (0, n - flat.shape[0])).reshape(n // width, width)


def _unpack(flat, shapes):
    out, off = [], 0
    for s in shapes:
        n = 1
        for d in s:
            n *= d
        out.append(flat[off:off + n].reshape(s))
        off += n
    return out


def _unpack_cols(flat2, shapes):
    out, off = [], 0
    for s in shapes:
        n = 1
        for d in s:
            n *= d
        out.append(flat2[:, off:off + n].reshape((flat2.shape[0],) + tuple(s)))
        off += n
    return out


def _pad_cols(w, n):
    return jnp.pad(w, [(0, 0)] * (w.ndim - 1) + [(0, n - w.shape[-1])])


def kernel(x, meta_tokens, norm_mix, norm_ffn, w_gate_up, w_down, fox_w_in, fox_b_f, fox_q_gain, fox_k_gain, fox_w_out, gla_w_in, gla_w_alpha2, gla_b_alpha, gla_o_gain, gla_w_out, gdn_w_in, gdn_conv_w, gdn_a_log, gdn_dt_bias, gdn_o_gain, gdn_w_out, loss_target, m_meta_tokens, m_norm_mix, m_norm_ffn, m_w_gate_up, m_w_down, m_fox_w_in, m_fox_b_f, m_fox_q_gain, m_fox_k_gain, m_fox_w_out, m_gla_w_in, m_gla_w_alpha2, m_gla_b_alpha, m_gla_o_gain, m_gla_w_out, m_gdn_w_in, m_gdn_conv_w, m_gdn_a_log, m_gdn_dt_bias, m_gdn_o_gain, m_gdn_w_out, v_meta_tokens, v_norm_mix, v_norm_ffn, v_w_gate_up, v_w_down, v_fox_w_in, v_fox_b_f, v_fox_q_gain, v_fox_k_gain, v_fox_w_out, v_gla_w_in, v_gla_w_alpha2, v_gla_b_alpha, v_gla_o_gain, v_gla_w_out, v_gdn_w_in, v_gdn_conv_w, v_gdn_a_log, v_gdn_dt_bias, v_gdn_o_gain, v_gdn_w_out):
    W = dict(meta_tokens=meta_tokens, norm_mix=norm_mix, norm_ffn=norm_ffn, w_gate_up=w_gate_up, w_down=w_down,
             fox_w_in=fox_w_in, fox_b_f=fox_b_f, fox_q_gain=fox_q_gain, fox_k_gain=fox_k_gain, fox_w_out=fox_w_out,
             gla_w_in=gla_w_in, gla_w_alpha2=gla_w_alpha2, gla_b_alpha=gla_b_alpha, gla_o_gain=gla_o_gain,
             gla_w_out=gla_w_out, gdn_w_in=gdn_w_in, gdn_conv_w=gdn_conv_w, gdn_a_log=gdn_a_log,
             gdn_dt_bias=gdn_dt_bias, gdn_o_gain=gdn_o_gain, gdn_w_out=gdn_w_out)
    M = dict(meta_tokens=m_meta_tokens, norm_mix=m_norm_mix, norm_ffn=m_norm_ffn, w_gate_up=m_w_gate_up, w_down=m_w_down,
             fox_w_in=m_fox_w_in, fox_b_f=m_fox_b_f, fox_q_gain=m_fox_q_gain, fox_k_gain=m_fox_k_gain,
             fox_w_out=m_fox_w_out, gla_w_in=m_gla_w_in, gla_w_alpha2=m_gla_w_alpha2, gla_b_alpha=m_gla_b_alpha,
             gla_o_gain=m_gla_o_gain, gla_w_out=m_gla_w_out, gdn_w_in=m_gdn_w_in, gdn_conv_w=m_gdn_conv_w,
             gdn_a_log=m_gdn_a_log, gdn_dt_bias=m_gdn_dt_bias, gdn_o_gain=m_gdn_o_gain, gdn_w_out=m_gdn_w_out)
    V = dict(meta_tokens=v_meta_tokens, norm_mix=v_norm_mix, norm_ffn=v_norm_ffn, w_gate_up=v_w_gate_up, w_down=v_w_down,
             fox_w_in=v_fox_w_in, fox_b_f=v_fox_b_f, fox_q_gain=v_fox_q_gain, fox_k_gain=v_fox_k_gain,
             fox_w_out=v_fox_w_out, gla_w_in=v_gla_w_in, gla_w_alpha2=v_gla_w_alpha2, gla_b_alpha=v_gla_b_alpha,
             gla_o_gain=v_gla_o_gain, gla_w_out=v_gla_w_out, gdn_w_in=v_gdn_w_in, gdn_conv_w=v_gdn_conv_w,
             gdn_a_log=v_gdn_a_log, gdn_dt_bias=v_gdn_dt_bias, gdn_o_gain=v_gdn_o_gain, gdn_w_out=v_gdn_w_out)
    chip = 2 * lax.axis_index("x") + lax.axis_index("y")

    others = [(n, ax) for n, ax in _BIG if n not in ("w_gate_up", "w_down")]
    packed = _pack([jnp.swapaxes(w_gate_up, 1, 2), w_down] + [W[n] for n, _ in others], D, _PACK_ROWS, BF16)
    rows = packed.shape[0]
    wpk = _ag_weights(lax.dynamic_update_slice(lax.empty((4, rows, D), BF16), packed[None], (chip, 0, 0)))
    gathered = wpk[:, FFN_ROWS:].reshape(4, -1)
    full = {}
    for (n, ax), seg in zip(others, _unpack_cols(gathered, [W[n].shape for n, _ in others])):
        full[n] = jnp.concatenate([seg[q] for q in range(4)], axis=ax)
    small = _pack([W[n] for n, _ in _SMALL_SHARDED], LANES, 8, F32)
    small_all = _gather8(small, reduce=False, name="gather_small").reshape(8, -1)
    for (n, ax), seg in zip(_SMALL_SHARDED, _unpack_cols(small_all, [W[n].shape for n, _ in _SMALL_SHARDED])):
        full[n] = jnp.concatenate([seg[2 * q] for q in range(4)], axis=ax)
    fox_in = _pad_cols(full["fox_w_in"], FOX_INP)
    gla_in = _pad_cols(full["gla_w_in"], GLA_INP)
    gdn_in = _pad_cols(full["gdn_w_in"], GDN_INP)
    w_alpha2, conv_w = full["gla_w_alpha2"][0], full["gdn_conv_w"][0]

    h = jnp.concatenate([jnp.zeros((META0, D), F32), full["meta_tokens"], x[0]], axis=0)
    saved = []
    for i in range(DEPTH):
        kind, j = i % 3, i // 3
        y = _rms_fwd(h, norm_mix[i], name=f"norm_mix{i}")
        if kind == 0:
            proj = _mm(y, fox_in[j], name=f"fox_in{j}")
            qa, ka, va = _fox_prep(proj, fox_b_f[j], fox_q_gain[j], fox_k_gain[j])
            o, og, lse = _fox_attn_fwd(qa, ka, va, proj)
            w_out, mix = full["fox_w_out"][j], (proj, qa, ka, va, o, lse)
        elif kind == 1:
            proj = _mm(y, gla_in[j], name=f"gla_in{j}")
            o, og, states = _gla_fwd(proj, w_alpha2, gla_b_alpha[j], gla_o_gain[j])
            w_out, mix = full["gla_w_out"][j], (proj, o, states)
        else:
            proj = _mm(y, gdn_in[j], name=f"gdn_in{j}")
            o, og, states = _gdn_fwd(proj, conv_w, gdn_a_log[j], gdn_dt_bias[j], gdn_o_gain[j])
            w_out, mix = full["gdn_w_out"][j], (proj, o, states)
        hm = _mm(og, w_out, add=h, name=f"mix_out{i}")
        yf = _rms_fwd(hm, norm_ffn[i], name=f"norm_ffn{i}")
        gate, up, act = _ffn_up(yf, wpk, i)
        hn = _ffn_down(act, wpk, i, hm)
        saved.append((h, y, mix, og, w_out, hm, yf, gate, up, act))
        h = hn
    dh, loss_tile = _loss_head(h, loss_target[0])

    G = {n: [None] * W[n].shape[0] for n in _WEIGHTS if n not in ("meta_tokens", "w_gate_up", "w_down")}
    gpk = jnp.zeros((4, rows, D), BF16)
    for i in reversed(range(DEPTH)):
        kind, j = i % 3, i // 3
        h_in, y, mix, og, w_out, hm, yf, gate, up, act = saved[i]
        dg, du = _ffn_dact(dh, wpk, i, gate, up)
        gpk = _ffn_dw_down(act, dh, gpk, i)
        dyf = _ffn_dyf(dg, du, wpk, i)
        gpk = _ffn_dw_gu(dg, du, yf, gpk, i)
        dhm, dnf = _rms_bwd(hm, norm_ffn[i], dyf, dh, name=f"d_norm_ffn{i}")
        G["norm_ffn"][i] = dnf[0]
        dog = _mm(dhm, w_out, tb=True, name=f"d_og{i}")
        dw_out = _mm(og, dhm, ta=True, out_dtype=BF16, name=f"d_w_out{i}")
        if kind == 0:
            proj, qa, ka, va, o, lse = mix
            doa, q2, dgate = _fox_gate_bwd(dog, o, proj, lse, qa)
            dqn, dkn, dv, dct = _fox_attn_bwd(q2, ka, va, doa)
            dproj, dqg, dkg, dbf = _fox_prep_bwd(proj, fox_b_f[j], fox_q_gain[j], fox_k_gain[j], dqn, dkn, dv, dgate, dct)
            G["fox_w_out"][j] = dw_out
            G["fox_q_gain"][j] = dqg.reshape(FOX_H, FOX_DH).sum(0)
            G["fox_k_gain"][j] = dkg.reshape(FOX_H, FOX_DH).sum(0)
            G["fox_b_f"][j] = dbf[0, :FOX_H]
            w_in, wname, n_in = fox_in[j], "fox_w_in", fox_w_in.shape[2] * 4
        elif kind == 1:
            proj, o, states = mix
            dproj, dwa, dba, dogain = _gla_bwd(proj, w_alpha2, gla_b_alpha[j], gla_o_gain[j], o, states, dog)
            G["gla_w_out"][j] = dw_out
            G["gla_w_alpha2"][j] = dwa[:GLA_RANK]
            G["gla_b_alpha"][j] = dba[0]
            G["gla_o_gain"][j] = dogain[0]
            w_in, wname, n_in = gla_in[j], "gla_w_in", gla_w_in.shape[2] * 4
        else:
            proj, o, states = mix
            dproj, dcw, dal, ddt, dogain = _gdn_bwd(proj, conv_w, gdn_a_log[j], gdn_dt_bias[j], gdn_o_gain[j], o, states, dog)
            G["gdn_w_out"][j] = dw_out
            G["gdn_conv_w"][j] = dcw[:4].reshape(4, 1, GDN_CONV)
            G["gdn_a_log"][j] = dal[0, :GDN_H]
            G["gdn_dt_bias"][j] = ddt[0, :GDN_H]
            G["gdn_o_gain"][j] = dogain[0]
            w_in, wname, n_in = gdn_in[j], "gdn_w_in", gdn_w_in.shape[2] * 4
        dy = _mm(dproj, w_in, tb=True, name=f"d_y{i}")
        G[wname][j] = _mm(y, dproj, ta=True, out_dtype=BF16, name=f"d_w_in{i}")[:, :n_in]
        dh, dnm = _rms_bwd(h_in, norm_mix[i], dy, dhm, name=f"d_norm_mix{i}")
        G["norm_mix"][i] = dnm[0]
    grad_x = dh[ROW0:][None]
    G = {n: jnp.stack(v) for n, v in G.items()}
    G["meta_tokens"] = dh[META0:ROW0]

    blocks = []
    for q in range(4):
        parts = []
        for n, ax in others:
            sz = W[n].shape[ax]
            parts.append(lax.slice_in_dim(G[n], q * sz, (q + 1) * sz, axis=ax))
        blk = _pack(parts, D, 1, BF16)
        blocks.append(jnp.pad(blk, ((0, rows - FFN_ROWS - blk.shape[0]), (0, 0))))
    gpk = lax.dynamic_update_slice(gpk, jnp.stack(blocks), (0, FFN_ROWS, 0))
    reduced = _reduce_scatter(gpk)
    grads = dict(zip([n for n, _ in others], _unpack(reduced[FFN_ROWS:].reshape(-1), [W[n].shape for n, _ in others])))
    small_names = [n for n, _ in _SMALL_SHARDED] + list(_REPLICATED)
    small_g = _pack([G[n] for n in small_names] + [loss_tile[0, 0:1]], LANES, 8, F32)
    small_sum = _gather8(small_g, reduce=True, name="allreduce_small").reshape(-1)
    small_shapes = [G[n].shape for n in small_names] + [(1,)]
    small_vals = _unpack(small_sum, small_shapes)
    loss = small_vals[-1][0]
    for n, val in zip(small_names, small_vals[:-1]):
        grads[n] = val
    for n, ax in _SMALL_SHARDED:
        sz = W[n].shape[ax]
        grads[n] = lax.dynamic_slice_in_dim(grads[n], chip * sz, sz, axis=ax)

    delta, new_m, new_v = {}, {}, {}
    for n, off, tr_ in (("w_gate_up", OFF_GU, True), ("w_down", OFF_DOWN, False)):
        grads[n], delta[n], new_m[n], new_v[n] = _adamw_packed(W[n], reduced, M[n], V[n], row_off=off, transposed=tr_,
                                                               name=f"adamw_{n}")
    for n, _ in others:
        delta[n], new_m[n], new_v[n] = _adamw(W[n], grads[n], M[n], V[n], name=f"adamw_{n}")
    tiny = [n for n in _WEIGHTS if n not in dict(_BIG)]
    packs = [_pack([T[n] for n in tiny], LANES, 8, F32) for T in (W, grads, M, V)]
    outs = _adamw(*packs, name="adamw_small")
    shapes = [W[n].shape for n in tiny]
    for dst, o in zip((delta, new_m, new_v), outs):
        for n, val in zip(tiny, _unpack(o.reshape(-1), shapes)):
            dst[n] = val
    return (loss, grad_x, *[grads[n] for n in _WEIGHTS], *[delta[n] for n in _WEIGHTS],
            *[new_m[n] for n in _WEIGHTS], *[new_v[n] for n in _WEIGHTS])
```

```python
import functools

import jax
import jax.numpy as jnp
from jax import lax
from jax.experimental import pallas as pl
from jax.experimental.pallas import tpu as pltpu

F32, BF16 = jnp.float32, jnp.bfloat16
D = 1024
N_META = 16
ROW0 = 128
META0 = ROW0 - N_META
EPS = 1e-6
LANES = 128
VMEM_LIMIT = 56 * 1024 * 1024

FOX_H, FOX_DH = 16, 64
FOX_INP = 4224
GLA_H, GLA_DK, GLA_DV, GLA_RANK = 4, 128, 256, 16
GLA_QK, GLA_V = 512, 1024
GLA_INP = 3200
GLA_NORM = 16.0
GDN_H, GDN_DK, GDN_DV = 8, 128, 128
GDN_CONV = 3072
GDN_INP = 4224
CHUNK = 64
D_FF = 2816
DEPTH = 4

ADAM_LR, ADAM_B1, ADAM_B2, ADAM_EPS, ADAM_WD, ADAM_STEP = 0.001, 0.9, 0.999, 1e-08, 0.01, 10

MESH = pl.DeviceIdType.MESH
ANY = pl.BlockSpec(memory_space=pl.ANY)
VM = pl.BlockSpec(memory_space=pltpu.VMEM)


def _params(sem=None, **kw):
    if sem is not None:
        kw["dimension_semantics"] = sem
    return pltpu.CompilerParams(vmem_limit_bytes=VMEM_LIMIT, **kw)


def _tile(n, cap, mult=LANES):
    best = None
    for t in range(mult, min(n, cap) + 1, mult):
        if n % t == 0:
            best = t
    return best if best is not None else n


def nn(a, b, **kw):
    return jnp.dot(a, b, preferred_element_type=F32, **kw)


def nt(a, b, **kw):
    return lax.dot_general(a, b, (((1,), (1,)), ((), ())), preferred_element_type=F32, **kw)


def tn(a, b, **kw):
    return lax.dot_general(a, b, (((0,), (0,)), ((), ())), preferred_element_type=F32, **kw)


def _split3(x):
    hi = x.astype(BF16)
    r = x - hi.astype(F32)
    mid = r.astype(BF16)
    lo = (r - mid.astype(F32)).astype(BF16)
    return hi, mid, lo


def _sel_l(sel, x):
    a, b, c = _split3(x)
    return nn(sel, a) + nn(sel, b) + nn(sel, c)


def _sel_r(x, sel):
    a, b, c = _split3(x)
    return nn(a, sel) + nn(b, sel) + nn(c, sel)


def _iota(shape, dim):
    return lax.broadcasted_iota(jnp.int32, shape, dim)


def _tri(n, upper=False, strict=False):
    i, j = _iota((n, n), 0), _iota((n, n), 1)
    if upper:
        m = (j > i) if strict else (j >= i)
    else:
        m = (j < i) if strict else (j <= i)
    return m


def _sigmoid(x):
    return 1.0 / (1.0 + jnp.exp(-x))


def _log_sigmoid(x):
    return jnp.minimum(x, 0.0) - jnp.log(1.0 + jnp.exp(-jnp.abs(x)))


def _softplus(x):
    return jnp.maximum(x, 0.0) + jnp.log(1.0 + jnp.exp(-jnp.abs(x)))


def _silu(x):
    return x * _sigmoid(x)


def _dsilu(x):
    s = _sigmoid(x)
    return s * (1.0 + x * (1.0 - s))


def _mm(a, b, *, ta=False, tb=False, add=None, out_dtype=F32, name):
    m, k = (a.shape[1], a.shape[0]) if ta else a.shape
    n = b.shape[0] if tb else b.shape[1]
    assert k == (b.shape[1] if tb else b.shape[0])
    tm, tn_, tk = _tile(m, 1408, LANES if ta else 16), _tile(n, 1408), _tile(k, 1408)
    nk = k // tk

    def body(*refs):
        if add is None:
            a_ref, b_ref, o_ref, acc = refs
        else:
            a_ref, b_ref, r_ref, o_ref, acc = refs
        kk = pl.program_id(2)

        @pl.when(kk == 0)
        def _():
            acc[...] = jnp.zeros_like(acc)

        av, bv = a_ref[...].astype(BF16), b_ref[...].astype(BF16)
        dims = (((0,) if ta else (1,), (1,) if tb else (0,)), ((), ()))
        acc[...] += lax.dot_general(av, bv, dims, preferred_element_type=F32)

        @pl.when(kk == nk - 1)
        def _():
            r = acc[...]
            if add is not None:
                r = r + r_ref[...].astype(F32)
            o_ref[...] = r.astype(out_dtype)

    a_spec = pl.BlockSpec((tk, tm), lambda i, j, q: (q, i)) if ta else pl.BlockSpec((tm, tk), lambda i, j, q: (i, q))
    b_spec = pl.BlockSpec((tn_, tk), lambda i, j, q: (j, q)) if tb else pl.BlockSpec((tk, tn_), lambda i, j, q: (q, j))
    o_spec = pl.BlockSpec((tm, tn_), lambda i, j, q: (i, j))
    ins, specs = [a, b], [a_spec, b_spec]
    if add is not None:
        ins.append(add)
        specs.append(o_spec)
    return pl.pallas_call(
        body, name=name, grid=(m // tm, n // tn_, nk), in_specs=specs, out_specs=o_spec,
        out_shape=jax.ShapeDtypeStruct((m, n), out_dtype),
        scratch_shapes=[pltpu.VMEM((tm, tn_), F32)],
        compiler_params=_params(("parallel", "parallel", "arbitrary")),
    )(*ins)


def _rms_fwd(h, g, *, name):
    lp = h.shape[0]
    tr = _tile(lp, 512)

    def body(h_ref, g_ref, y_ref):
        x = h_ref[...]
        r = lax.rsqrt(jnp.mean(x * x, axis=-1, keepdims=True) + EPS)
        y_ref[...] = (x * r * g_ref[...]).astype(BF16)

    return pl.pallas_call(
        body, name=name, grid=(lp // tr,),
        in_specs=[pl.BlockSpec((tr, D), lambda i: (i, 0)), pl.BlockSpec((1, D), lambda i: (0, 0))],
        out_specs=pl.BlockSpec((tr, D), lambda i: (i, 0)),
        out_shape=jax.ShapeDtypeStruct((lp, D), BF16), compiler_params=_params(("parallel",)),
    )(h, g.reshape(1, D))


def _rms_bwd(h, g, dy, dres, *, name):
    lp = h.shape[0]
    tr = _tile(lp, 512)

    def body(h_ref, g_ref, dy_ref, dr_ref, dh_ref, dg_ref):
        @pl.when(pl.program_id(0) == 0)
        def _():
            dg_ref[...] = jnp.zeros_like(dg_ref)

        x, dyv = h_ref[...], dy_ref[...].astype(F32)
        r = lax.rsqrt(jnp.mean(x * x, axis=-1, keepdims=True) + EPS)
        u = dyv * g_ref[...]
        dx = r * u - x * (r * r * r) * jnp.mean(x * u, axis=-1, keepdims=True)
        dh_ref[...] = dr_ref[...] + dx
        dg_ref[...] += jnp.sum(dyv * x * r, axis=0, keepdims=True)

    return pl.pallas_call(
        body, name=name, grid=(lp // tr,),
        in_specs=[pl.BlockSpec((tr, D), lambda i: (i, 0)), pl.BlockSpec((1, D), lambda i: (0, 0)),
                  pl.BlockSpec((tr, D), lambda i: (i, 0)), pl.BlockSpec((tr, D), lambda i: (i, 0))],
        out_specs=[pl.BlockSpec((tr, D), lambda i: (i, 0)), pl.BlockSpec((1, D), lambda i: (0, 0))],
        out_shape=[jax.ShapeDtypeStruct((lp, D), F32), jax.ShapeDtypeStruct((1, D), F32)],
        compiler_params=_params(("arbitrary",)),
    )(h, g.reshape(1, D), dy, dres)


GU_ROWS, DOWN_ROWS = 1408, 704
OFF_GU, OFF_DOWN = 0, DEPTH * GU_ROWS
FFN_ROWS = DEPTH * (GU_ROWS + DOWN_ROWS)
FFN_TM = 704


def _gu_spec(fn):
    return pl.BlockSpec((None, GU_ROWS, D), fn)


def _down_spec(fn):
    return pl.BlockSpec((None, DOWN_ROWS, D), fn)


def _down_pair(w0_ref, w1_ref):
    return jnp.concatenate([w0_ref[...], w1_ref[...]], axis=0)


def _ffn_up(yf, wpk, layer):
    lp = yf.shape[0]
    tm = _tile(lp, FFN_TM, 16)

    def body(y_ref, wg_ref, wu_ref, g_ref, u_ref, a_ref):
        y = y_ref[...]
        g, u = nt(y, wg_ref[...]), nt(y, wu_ref[...])
        g_ref[...] = g.astype(BF16)
        u_ref[...] = u.astype(BF16)
        a_ref[...] = (_silu(g) * u).astype(BF16)

    o = pl.BlockSpec((tm, GU_ROWS), lambda i, j: (i, j))
    return pl.pallas_call(
        body, name=f"ffn_up{layer}", grid=(lp // tm, 2),
        in_specs=[pl.BlockSpec((tm, D), lambda i, j: (i, 0)), _gu_spec(lambda i, j: (j, OFF_GU // GU_ROWS + layer, 0)),
                  _gu_spec(lambda i, j: (2 + j, OFF_GU // GU_ROWS + layer, 0))],
        out_specs=[o, o, o], out_shape=[jax.ShapeDtypeStruct((lp, D_FF), BF16)] * 3,
        compiler_params=_params(("parallel", "parallel")),
    )(yf, wpk, wpk)


def _ffn_down(act, wpk, layer, res):
    lp = act.shape[0]
    tm = _tile(lp, FFN_TM, 16)

    def body(a_ref, w0_ref, w1_ref, r_ref, o_ref, acc):
        kk = pl.program_id(1)

        @pl.when(kk == 0)
        def _():
            acc[...] = r_ref[...]

        acc[...] += nn(a_ref[...], _down_pair(w0_ref, w1_ref))

        @pl.when(kk == 1)
        def _():
            o_ref[...] = acc[...]

    o = pl.BlockSpec((tm, D), lambda i, kk: (i, 0))
    blk = OFF_DOWN // DOWN_ROWS + layer
    return pl.pallas_call(
        body, name=f"ffn_down{layer}", grid=(lp // tm, 2),
        in_specs=[pl.BlockSpec((tm, GU_ROWS), lambda i, kk: (i, kk)), _down_spec(lambda i, kk: (2 * kk, blk, 0)),
                  _down_spec(lambda i, kk: (2 * kk + 1, blk, 0)), o],
        out_specs=o, out_shape=jax.ShapeDtypeStruct((lp, D), F32), scratch_shapes=[pltpu.VMEM((tm, D), F32)],
        compiler_params=_params(("parallel", "arbitrary")),
    )(act, wpk, wpk, res)


def _ffn_dact(dh, wpk, layer, gate, up):
    lp = dh.shape[0]
    tm = _tile(lp, FFN_TM, 16)

    def body(d_ref, w0_ref, w1_ref, g_ref, u_ref, dg_ref, du_ref):
        da = nt(d_ref[...].astype(BF16), _down_pair(w0_ref, w1_ref))
        g, u = g_ref[...].astype(F32), u_ref[...].astype(F32)
        dg_ref[...] = (da * u * _dsilu(g)).astype(BF16)
        du_ref[...] = (da * _silu(g)).astype(BF16)

    o = pl.BlockSpec((tm, GU_ROWS), lambda i, j: (i, j))
    blk = OFF_DOWN // DOWN_ROWS + layer
    return pl.pallas_call(
        body, name=f"d_act{layer}", grid=(lp // tm, 2),
        in_specs=[pl.BlockSpec((tm, D), lambda i, j: (i, 0)), _down_spec(lambda i, j: (2 * j, blk, 0)),
                  _down_spec(lambda i, j: (2 * j + 1, blk, 0)), o, o],
        out_specs=[o, o], out_shape=[jax.ShapeDtypeStruct((lp, D_FF), BF16)] * 2,
        compiler_params=_params(("parallel", "parallel")),
    )(dh, wpk, wpk, gate, up)


def _ffn_dyf(dg, du, wpk, layer):
    lp = dg.shape[0]
    tm = _tile(lp, FFN_TM, 16)

    def body(dg_ref, du_ref, w_ref, o_ref, acc):
        kk = pl.program_id(1)

        @pl.when(kk == 0)
        def _():
            acc[...] = jnp.zeros_like(acc)

        @pl.when(kk < 2)
        def _():
            acc[...] += nn(dg_ref[...], w_ref[...])

        @pl.when(kk >= 2)
        def _():
            acc[...] += nn(du_ref[...], w_ref[...])

        @pl.when(kk == 3)
        def _():
            o_ref[...] = acc[...]

    return pl.pallas_call(
        body, name=f"d_yf{layer}", grid=(lp // tm, 4),
        in_specs=[pl.BlockSpec((tm, GU_ROWS), lambda i, kk: (i, jnp.minimum(kk, 1))),
                  pl.BlockSpec((tm, GU_ROWS), lambda i, kk: (i, jnp.maximum(kk - 2, 0))),
                  _gu_spec(lambda i, kk: (kk, OFF_GU // GU_ROWS + layer, 0))],
        out_specs=pl.BlockSpec((tm, D), lambda i, kk: (i, 0)), out_shape=jax.ShapeDtypeStruct((lp, D), F32),
        scratch_shapes=[pltpu.VMEM((tm, D), F32)], compiler_params=_params(("parallel", "arbitrary")),
    )(dg, du, wpk)


def _ffn_dw_down(act, dh, gpk, layer):
    lp = act.shape[0]
    tk = _tile(lp, 1408, 16)
    nk = lp // tk
    row = OFF_DOWN + layer * DOWN_ROWS

    def body(a_ref, d_ref, g_in, g_out, acc, stage, sems):
        jp, kk = pl.program_id(0), pl.program_id(1)

        @pl.when(kk == 0)
        def _():
            acc[...] = jnp.zeros_like(acc)

        acc[...] += tn(a_ref[...], d_ref[...].astype(BF16))

        @pl.when(kk == nk - 1)
        def _():
            stage[...] = acc[...].astype(BF16)
            copies = [pltpu.make_async_copy(stage.at[pl.ds(hf * DOWN_ROWS, DOWN_ROWS), :],
                                            g_out.at[2 * jp + hf, pl.ds(row, DOWN_ROWS), :], sems.at[hf]) for hf in range(2)]
            for cp in copies:
                cp.start()
            for cp in copies:
                cp.wait()

    return pl.pallas_call(
        body, name=f"d_w_down{layer}", grid=(2, nk),
        in_specs=[pl.BlockSpec((tk, GU_ROWS), lambda jp, kk: (kk, jp)), pl.BlockSpec((tk, D), lambda jp, kk: (kk, 0)), ANY],
        out_specs=ANY, out_shape=jax.ShapeDtypeStruct(gpk.shape, gpk.dtype),
        scratch_shapes=[pltpu.VMEM((GU_ROWS, D), F32), pltpu.VMEM((GU_ROWS, D), BF16), pltpu.SemaphoreType.DMA((2,))],
        input_output_aliases={2: 0}, compiler_params=_params(("arbitrary", "arbitrary")),
    )(act, dh, gpk)


def _ffn_dw_gu(dg, du, yf, gpk, layer):
    lp = dg.shape[0]
    tk = _tile(lp, 1408, 16)
    nk = lp // tk

    def body(dg_ref, du_ref, y_ref, g_in, o_ref, acc):
        c, kk = pl.program_id(0), pl.program_id(1)

        @pl.when(kk == 0)
        def _():
            acc[...] = jnp.zeros_like(acc)

        @pl.when(c < 2)
        def _():
            acc[...] += tn(dg_ref[...], y_ref[...])

        @pl.when(c >= 2)
        def _():
            acc[...] += tn(du_ref[...], y_ref[...])

        @pl.when(kk == nk - 1)
        def _():
            o_ref[...] = acc[...].astype(BF16)

    return pl.pallas_call(
        body, name=f"d_w_gate_up{layer}", grid=(4, nk),
        in_specs=[pl.BlockSpec((tk, GU_ROWS), lambda c, kk: (kk, jnp.minimum(c, 1))),
                  pl.BlockSpec((tk, GU_ROWS), lambda c, kk: (kk, jnp.maximum(c - 2, 0))),
                  pl.BlockSpec((tk, D), lambda c, kk: (kk, 0)), ANY],
        out_specs=_gu_spec(lambda c, kk: (c, OFF_GU // GU_ROWS + layer, 0)),
        out_shape=jax.ShapeDtypeStruct(gpk.shape, gpk.dtype),
        scratch_shapes=[pltpu.VMEM((GU_ROWS, D), F32)], input_output_aliases={3: 0},
        compiler_params=_params(("parallel", "arbitrary")),
    )(dg, du, yf, gpk)


def _loss_head(h, target):
    lp = h.shape[0]
    nb = lp // ROW0

    def body(h_ref, t_ref, dh_ref, l_ref):
        i = pl.program_id(0)

        @pl.when(i == 0)
        def _():
            l_ref[...] = jnp.zeros_like(l_ref)
            dh_ref[...] = jnp.zeros_like(dh_ref)

        @pl.when(i > 0)
        def _():
            err = h_ref[...] - t_ref[...]
            dh_ref[...] = err * (1.0 / D)
            l_ref[...] += jnp.sum(err * err) * (0.5 / D)

    return pl.pallas_call(
        body, name="loss_head", grid=(nb,),
        in_specs=[pl.BlockSpec((ROW0, D), lambda i: (i, 0)), pl.BlockSpec((ROW0, D), lambda i: (jnp.maximum(i - 1, 0), 0))],
        out_specs=[pl.BlockSpec((ROW0, D), lambda i: (i, 0)), pl.BlockSpec((8, LANES), lambda i: (0, 0))],
        out_shape=[jax.ShapeDtypeStruct((lp, D), F32), jax.ShapeDtypeStruct((8, LANES), F32)],
        compiler_params=_params(("arbitrary",)),
    )(h, target)


def _adamw(w, g, m, v, *, name):
    shape = w.shape
    c = shape[-1]
    r = w.size // c
    w2, g2, m2, v2 = (t.reshape(r, c) for t in (w, g, m, v))
    tr = _tile(r, max(8, (1 << 19) // c), 8)

    def body(w_ref, g_ref, m_ref, v_ref, d_ref, nm_ref, nv_ref):
        gv = g_ref[...]
        nm = ADAM_B1 * m_ref[...] + (1.0 - ADAM_B1) * gv
        nv = ADAM_B2 * v_ref[...] + (1.0 - ADAM_B2) * (gv * gv)
        m_hat = nm / (1.0 - ADAM_B1 ** ADAM_STEP)
        v_hat = nv / (1.0 - ADAM_B2 ** ADAM_STEP)
        d_ref[...] = -ADAM_LR * (m_hat / (jnp.sqrt(v_hat) + ADAM_EPS) + ADAM_WD * w_ref[...])
        nm_ref[...] = nm
        nv_ref[...] = nv

    spec = pl.BlockSpec((tr, c), lambda i: (i, 0))
    outs = pl.pallas_call(
        body, name=name, grid=(r // tr,), in_specs=[spec] * 4, out_specs=[spec] * 3,
        out_shape=[jax.ShapeDtypeStruct((r, c), F32)] * 3, compiler_params=_params(("parallel",)),
    )(w2, g2, m2, v2)
    return tuple(o.reshape(shape) for o in outs)


def _adam_math(w, g, m, v):
    nm = ADAM_B1 * m + (1.0 - ADAM_B1) * g
    nv = ADAM_B2 * v + (1.0 - ADAM_B2) * (g * g)
    m_hat = nm / (1.0 - ADAM_B1 ** ADAM_STEP)
    v_hat = nv / (1.0 - ADAM_B2 ** ADAM_STEP)
    return -ADAM_LR * (m_hat / (jnp.sqrt(v_hat) + ADAM_EPS) + ADAM_WD * w), nm, nv


def _adamw_packed(w, gred, m, v, *, row_off, transposed, name):
    nl, a, b = w.shape
    if transposed:
        ta = _tile(a, 256)
        wspec = pl.BlockSpec((1, ta, b), lambda l, r: (l, r, 0))
        gspec = pl.BlockSpec((b, ta), lambda l, r: (row_off // b + l, r))
        grid = (nl, a // ta)
    else:
        wspec = pl.BlockSpec((1, a, b), lambda l, r: (l, 0, 0))
        gspec = pl.BlockSpec((a, b), lambda l, r: (row_off // a + l, 0))
        grid = (nl, 1)

    def body(w_ref, g_ref, m_ref, v_ref, go_ref, d_ref, nm_ref, nv_ref):
        g = g_ref[...].T if transposed else g_ref[...]
        d, nm, nv = _adam_math(w_ref[0], g, m_ref[0], v_ref[0])
        go_ref[0], d_ref[0], nm_ref[0], nv_ref[0] = g, d, nm, nv

    return pl.pallas_call(
        body, name=name, grid=grid, in_specs=[wspec, gspec, wspec, wspec], out_specs=[wspec] * 4,
        out_shape=[jax.ShapeDtypeStruct(w.shape, F32)] * 4, compiler_params=_params(("parallel", "parallel")),
    )(w, gred, m, v)


FOX_AUG = FOX_H * LANES
L_C = 64
L_K = 67
L_LSE = 70
PAD_KEY = -30000.0
FOX_TQ = 384


def _head_sel(n_heads, width, lanes=LANES):
    r, c = _iota((n_heads * width, lanes), 0), _iota((n_heads * width, lanes), 1)
    down = (r // width == c).astype(BF16)
    r2, c2 = _iota((lanes, n_heads * width), 0), _iota((lanes, n_heads * width), 1)
    up = (c2 // width == r2).astype(BF16)
    return down, up


def _place(lane0):
    r, c = _iota((LANES, FOX_AUG), 0), _iota((LANES, FOX_AUG), 1)
    return [((c // LANES == r) & (c % LANES == lane0 + m)).astype(BF16) for m in range(3)]


def _placed(x, lane0):
    pcs = _split3(x)
    mats = _place(lane0)
    return nn(pcs[0], mats[0]) + nn(pcs[1], mats[1]) + nn(pcs[2], mats[2])


def _ones_at(rows, lanes):
    c = _iota((rows, FOX_AUG), 1) % LANES
    m = c == lanes[0]
    for l in lanes[1:]:
        m = m | (c == l)
    return m.astype(F32)


def _spread(x, extras, out_ref):
    rows = x.shape[0]
    left = _iota((rows, LANES), 1) < FOX_DH
    for p in range(FOX_H // 2):
        slab = x[:, p * LANES:(p + 1) * LANES]
        a = jnp.where(left, slab, extras[:, 2 * p * LANES:(2 * p + 1) * LANES])
        b = jnp.where(left, pltpu.roll(slab, FOX_DH, 1), extras[:, (2 * p + 1) * LANES:(2 * p + 2) * LANES])
        out_ref[:, 2 * p * LANES:(2 * p + 1) * LANES] = a.astype(BF16)
        out_ref[:, (2 * p + 1) * LANES:(2 * p + 2) * LANES] = b.astype(BF16)


def _fox_prep(proj, b_f, q_gain, k_gain):
    lp = proj.shape[0]
    nb = lp // LANES

    def body(p_ref, bf_ref, qg_ref, kg_ref, q_ref, k_ref, v_ref, carry):
        i = pl.program_id(0)

        @pl.when(i == 0)
        def _():
            carry[...] = jnp.zeros_like(carry)

        down, up = _head_sel(FOX_H, FOX_DH)

        def normed(x, gain):
            ms = _sel_r(x * x, down) * (1.0 / FOX_DH)
            r = _sel_r(lax.rsqrt(ms + EPS), up)
            return x * r * gain

        lane = _iota((LANES, LANES), 1)
        lf = jnp.where(lane < FOX_H, _log_sigmoid(p_ref[:, 4 * D:4 * D + LANES] + bf_ref[...]), 0.0)
        c = _sel_l(_tri(LANES).astype(BF16), lf) + carry[0:1, :]
        carry[...] = jnp.broadcast_to(c[LANES - 1:LANES, :], carry.shape)
        q_extra = _placed(c, L_C) + _ones_at(LANES, (L_K, L_K + 1, L_K + 2))
        row = i * LANES + _iota((LANES, FOX_AUG), 0)
        lane_a = _iota((LANES, FOX_AUG), 1) % LANES
        k_extra = -_placed(c, L_K) + _ones_at(LANES, (L_C, L_C + 1, L_C + 2, L_LSE, L_LSE + 1, L_LSE + 2))
        pad_val = jnp.where(lane_a == L_K, PAD_KEY, 0.0)
        k_extra = jnp.where((row < META0) & (lane_a >= L_K) & (lane_a < L_K + 3), pad_val, k_extra)
        v_extra = _ones_at(LANES, (L_C, L_C + 1, L_C + 2))
        _spread(normed(p_ref[:, 0:D], qg_ref[...]) * (FOX_DH ** -0.5), q_extra, q_ref)
        _spread(normed(p_ref[:, D:2 * D], kg_ref[...]), k_extra, k_ref)
        _spread(p_ref[:, 2 * D:3 * D], v_extra, v_ref)

    row = pl.BlockSpec((1, D), lambda i: (0, 0))
    aug = pl.BlockSpec((LANES, FOX_AUG), lambda i: (i, 0))
    return pl.pallas_call(
        body, name="fox_prep", grid=(nb,),
        in_specs=[pl.BlockSpec((LANES, FOX_INP), lambda i: (i, 0)), pl.BlockSpec((1, LANES), lambda i: (0, 0)), row, row],
        out_specs=[aug] * 3, out_shape=[jax.ShapeDtypeStruct((lp, FOX_AUG), BF16)] * 3,
        scratch_shapes=[pltpu.VMEM((8, LANES), F32)],
        compiler_params=_params(("arbitrary",)),
    )(proj, jnp.pad(b_f, (0, LANES - FOX_H)).reshape(1, LANES), jnp.tile(q_gain, FOX_H).reshape(1, D),
      jnp.tile(k_gain, FOX_H).reshape(1, D))


def _fox_attn_fwd(qa, ka, va, proj):
    lp = qa.shape[0]
    tq = _tile(lp, FOX_TQ)
    nq = lp // tq

    def body(q_ref, k_ref, v_ref, gate_ref, o_ref, og_ref, lse_ref):
        i = pl.program_id(1)
        causal = _iota((tq, tq), 1) <= _iota((tq, tq), 0)
        qs = [q_ref[:, hh * LANES:(hh + 1) * LANES] for hh in range(2)]

        def block(j, carry, diag):
            off = pl.multiple_of(j * tq, tq)
            out = []
            for hh in range(2):
                m, acc = carry[hh]
                k = k_ref[pl.ds(off, tq), hh * LANES:(hh + 1) * LANES]
                v = v_ref[pl.ds(off, tq), hh * LANES:(hh + 1) * LANES]
                s = nt(qs[hh], k)
                if diag:
                    s = jnp.where(causal, s, -1e30)
                m2 = jnp.maximum(m, jnp.max(s, axis=-1, keepdims=True))
                p = jnp.exp(s - m2)
                p_hi = p.astype(BF16)
                p_lo = (p - p_hi.astype(F32)).astype(BF16)
                out.append((m2, jnp.exp(m - m2) * acc + nn(p_hi, v) + nn(p_lo, v)))
            return tuple(out)

        init = tuple((jnp.full((tq, 1), -1e30, F32), jnp.zeros((tq, LANES), F32)) for _ in range(2))
        carry = lax.fori_loop(0, i, lambda j, c: block(j, c, False), init)
        carry = block(i, carry, True)
        outs, lses = [], []
        for hh in range(2):
            m, acc = carry[hh]
            l = acc[:, L_C:L_C + 1]
            outs.append(acc / l)
            lses.append(jnp.broadcast_to(m + jnp.log(l), (tq, LANES)))
        left = _iota((tq, LANES), 1) < FOX_DH
        o = jnp.where(left, outs[0], pltpu.roll(outs[1], FOX_DH, 1))
        o_ref[...] = o
        og_ref[...] = (o * _sigmoid(gate_ref[...])).astype(BF16)
        lse_ref[...] = jnp.where(left, lses[0], lses[1])

    qspec = pl.BlockSpec((tq, 2 * LANES), lambda p, i: (i, p))
    kspec = pl.BlockSpec((lp, 2 * LANES), lambda p, i: (0, p))
    ospec = pl.BlockSpec((tq, LANES), lambda p, i: (i, p))
    return pl.pallas_call(
        body, name="fox_attn_fwd", grid=(FOX_H // 2, nq),
        in_specs=[qspec, kspec, kspec, pl.BlockSpec((tq, LANES), lambda p, i: (i, 3 * D // LANES + p))],
        out_specs=[ospec] * 3,
        out_shape=[jax.ShapeDtypeStruct((lp, D), F32), jax.ShapeDtypeStruct((lp, D), BF16), jax.ShapeDtypeStruct((lp, D), F32)],
        compiler_params=_params(("parallel", "arbitrary")),
    )(qa, ka, va, proj)


def _fox_gate_bwd(dog, o, proj, lse, qa):
    lp = o.shape[0]
    tr = LANES

    def body(d_ref, o_ref, g_ref, lse_ref, q_ref, do_ref, q2_ref, dgate_ref):
        down, _ = _head_sel(FOX_H, FOX_DH)
        sg = _sigmoid(g_ref[...])
        dv, ov = d_ref[...], o_ref[...]
        do = (dv * sg).astype(BF16).astype(F32)
        dgate_ref[...] = dv * ov * sg * (1.0 - sg)
        delta = _sel_r(do * ov, down)
        _spread(do, -_placed(delta, L_C), do_ref)
        r_, c_ = _iota((D, LANES), 0), _iota((D, LANES), 1)
        lse_c = _sel_r(lse_ref[...], (r_ == c_ * FOX_DH).astype(BF16))
        q2_ref[...] = (q_ref[...].astype(F32) - _placed(lse_c, L_LSE)).astype(BF16)

    spec = pl.BlockSpec((tr, D), lambda i: (i, 0))
    aug = pl.BlockSpec((tr, FOX_AUG), lambda i: (i, 0))
    return pl.pallas_call(
        body, name="fox_gate_bwd", grid=(lp // tr,),
        in_specs=[spec, spec, pl.BlockSpec((tr, D), lambda i: (i, 3)), spec, aug], out_specs=[aug, aug, spec],
        out_shape=[jax.ShapeDtypeStruct((lp, FOX_AUG), BF16), jax.ShapeDtypeStruct((lp, FOX_AUG), BF16),
                   jax.ShapeDtypeStruct((lp, D), F32)],
        compiler_params=_params(("parallel",)),
    )(dog, o, proj, lse, qa)


def _fox_attn_bwd(q2, ka, va, doa):
    lp = q2.shape[0]
    t = _tile(lp, FOX_TQ)
    nb = lp // t

    def body(q_ref, k_ref, v_ref, do_ref, dq_ref, dk_ref, dv_ref, dc_ref, dq_acc, dk_acc, dv_acc, dc_acc):
        j = pl.program_id(1)

        @pl.when(j == 0)
        def _():
            dq_acc[...] = jnp.zeros_like(dq_acc)

        causal = _iota((t, t), 1) <= _iota((t, t), 0)
        for hh in range(2):
            k = k_ref[:, hh * LANES:(hh + 1) * LANES]
            v = v_ref[:, hh * LANES:(hh + 1) * LANES]
            dk_acc[...] = jnp.zeros_like(dk_acc)
            dv_acc[...] = jnp.zeros_like(dv_acc)
            dc_acc[...] = jnp.zeros_like(dc_acc)

            def block(i, diag):
                off = pl.multiple_of(i * t, t)
                q = q_ref[pl.ds(off, t), hh * LANES:(hh + 1) * LANES]
                do = do_ref[pl.ds(off, t), hh * LANES:(hh + 1) * LANES]
                s = nt(q, k)
                if diag:
                    s = jnp.where(causal, s, -1e30)
                p = jnp.exp(s)
                ds = p * nt(do, v)
                dc_acc[...] += jnp.sum(ds, axis=0, keepdims=True)
                dsb = ds.astype(BF16)
                dv_acc[...] += tn(p.astype(BF16), do)
                dk_acc[...] += tn(dsb, q)
                dq_acc[hh, pl.ds(off, t), :] += nn(dsb, k)

            block(j, True)

            def step(i, c):
                block(i, False)
                return c

            lax.fori_loop(j + 1, nb, step, 0)
            left = _iota((t, LANES), 1) < FOX_DH
            if hh == 0:
                dk_ref[...] = dk_acc[...]
                dv_ref[...] = dv_acc[...]
            else:
                dk_ref[...] = jnp.where(left, dk_ref[...], pltpu.roll(dk_acc[...], FOX_DH, 1))
                dv_ref[...] = jnp.where(left, dv_ref[...], pltpu.roll(dv_acc[...], FOX_DH, 1))
            dc_ref[hh] = jnp.broadcast_to(-dc_acc[...], (8, t))

        @pl.when(j == nb - 1)
        def _():
            left = _iota((lp, LANES), 1) < FOX_DH
            dq_ref[...] = jnp.where(left, dq_acc[0], pltpu.roll(dq_acc[1], FOX_DH, 1))

    full = pl.BlockSpec((lp, 2 * LANES), lambda p, j: (0, p))
    kblk = pl.BlockSpec((t, 2 * LANES), lambda p, j: (j, p))
    oblk = pl.BlockSpec((t, LANES), lambda p, j: (j, p))
    return pl.pallas_call(
        body, name="fox_attn_bwd", grid=(FOX_H // 2, nb),
        in_specs=[full, kblk, kblk, full],
        out_specs=[pl.BlockSpec((lp, LANES), lambda p, j: (0, p)), oblk, oblk, pl.BlockSpec((2, 8, t), lambda p, j: (p, 0, j))],
        out_shape=[jax.ShapeDtypeStruct((lp, D), F32)] * 3 + [jax.ShapeDtypeStruct((FOX_H, 8, lp), F32)],
        scratch_shapes=[pltpu.VMEM((2, lp, LANES), F32), pltpu.VMEM((t, LANES), F32), pltpu.VMEM((t, LANES), F32),
                        pltpu.VMEM((1, t), F32)],
        compiler_params=_params(("parallel", "arbitrary")),
    )(q2, ka, va, doa)


def _fox_prep_bwd(proj, b_f, q_gain, k_gain, dqn, dkn, dv, dgate, dct):
    lp = proj.shape[0]
    nb = lp // LANES

    def body(p_ref, bf_ref, qg_ref, kg_ref, dq_ref, dk_ref, dv_ref, dg_ref, dc_ref,
             dp_ref, dqg_ref, dkg_ref, dbf_ref, carry):
        i = pl.program_id(0)

        @pl.when(i == 0)
        def _():
            carry[...] = jnp.zeros_like(carry)
            dqg_ref[...] = jnp.zeros_like(dqg_ref)
            dkg_ref[...] = jnp.zeros_like(dkg_ref)
            dbf_ref[...] = jnp.zeros_like(dbf_ref)

        down, up = _head_sel(FOX_H, FOX_DH)

        def norm_bwd(x, gain, dy, scale, dgain_ref):
            ms = _sel_r(x * x, down) * (1.0 / FOX_DH)
            r = _sel_r(lax.rsqrt(ms + EPS), up)
            u = dy * gain * scale
            mean_xu = _sel_r(_sel_r(x * u, down) * (1.0 / FOX_DH), up)
            dgain_ref[...] += jnp.sum(dy * scale * x * r, axis=0, keepdims=True)
            return r * u - x * (r * r * r) * mean_xu

        dp_ref[:, 0:D] = norm_bwd(p_ref[:, 0:D], qg_ref[...], dq_ref[...], FOX_DH ** -0.5, dqg_ref).astype(BF16)
        dp_ref[:, D:2 * D] = norm_bwd(p_ref[:, D:2 * D], kg_ref[...], dk_ref[...], 1.0, dkg_ref).astype(BF16)
        dp_ref[:, 2 * D:3 * D] = dv_ref[...].astype(BF16)
        dp_ref[:, 3 * D:4 * D] = dg_ref[...].astype(BF16)
        rows = jnp.concatenate([dc_ref[h, 0:1, :] for h in range(FOX_H)] + [jnp.zeros((LANES - FOX_H, LANES), F32)], axis=0)
        dlf = _sel_l(_tri(LANES, upper=True).astype(BF16), rows.T) + carry[0:1, :]
        carry[...] = jnp.broadcast_to(dlf[0:1, :], carry.shape)
        lane = _iota((LANES, LANES), 1)
        z = p_ref[:, 4 * D:4 * D + LANES] + bf_ref[...]
        df = jnp.where(lane < FOX_H, dlf * _sigmoid(-z), 0.0)
        dp_ref[:, 4 * D:4 * D + LANES] = df.astype(BF16)
        dbf_ref[...] += jnp.sum(df, axis=0, keepdims=True)

    rev = lambda i: (nb - 1 - i, 0)
    blk = pl.BlockSpec((LANES, D), rev)
    row = pl.BlockSpec((1, D), lambda i: (0, 0))
    row128 = pl.BlockSpec((1, LANES), lambda i: (0, 0))
    return pl.pallas_call(
        body, name="fox_prep_bwd", grid=(nb,),
        in_specs=[pl.BlockSpec((LANES, FOX_INP), rev), row128, row, row, blk, blk, blk, blk,
                  pl.BlockSpec((FOX_H, 8, LANES), lambda i: (0, 0, nb - 1 - i))],
        out_specs=[pl.BlockSpec((LANES, FOX_INP), rev), row, row, row128],
        out_shape=[jax.ShapeDtypeStruct((lp, FOX_INP), BF16), jax.ShapeDtypeStruct((1, D), F32),
                   jax.ShapeDtypeStruct((1, D), F32), jax.ShapeDtypeStruct((1, LANES), F32)],
        scratch_shapes=[pltpu.VMEM((8, LANES), F32)],
        compiler_params=_params(("arbitrary",)),
    )(proj, jnp.pad(b_f, (0, LANES - FOX_H)).reshape(1, LANES), jnp.tile(q_gain, FOX_H).reshape(1, D),
      jnp.tile(k_gain, FOX_H).reshape(1, D), dqn, dkn, dv, dgate, dct)


def _gla_gates(p_ref, wa_ref, ba_ref):
    a_lr = p_ref[:, 3072:3072 + LANES]
    z = nn(a_lr.astype(BF16), wa_ref[...].astype(BF16)) + ba_ref[...]
    g = _log_sigmoid(z) * (1.0 / GLA_NORM)
    b = _sel_l(_tri(CHUNK).astype(BF16), g)
    return a_lr, z, b


def _gla_head_fwd(q, k, v, b, st0):
    eb = jnp.exp(b)
    bl = b[CHUNK - 1:CHUNK, :]
    qe, ke, kd = q * eb, k * jnp.exp(-b), k * jnp.exp(bl - b)
    a = jnp.where(_tri(CHUNK), nt(qe, ke), 0.0)
    o = nn(a, v) + nt(qe, st0)
    st1 = st0 * jnp.exp(bl) + tn(v, kd)
    return o, st1, (qe, ke, kd, a, bl)


def _gla_fwd(proj, w_alpha2, b_alpha, o_gain):
    lp = proj.shape[0]
    nc = lp // CHUNK

    def body(p_ref, wa_ref, ba_ref, og_ref, o_ref, y_ref, s_ref, st):
        @pl.when(pl.program_id(0) == 0)
        def _():
            st[...] = jnp.zeros_like(st)

        _, _, b = _gla_gates(p_ref, wa_ref, ba_ref)
        for h in range(GLA_H):
            q = p_ref[:, h * GLA_DK:(h + 1) * GLA_DK] * (GLA_DK ** -0.5)
            k = p_ref[:, GLA_QK + h * GLA_DK:GLA_QK + (h + 1) * GLA_DK]
            v = p_ref[:, 2 * GLA_QK + h * GLA_DV:2 * GLA_QK + (h + 1) * GLA_DV]
            r = p_ref[:, 2 * GLA_QK + GLA_V + h * GLA_DV:2 * GLA_QK + GLA_V + (h + 1) * GLA_DV]
            st0 = st[h]
            s_ref[0, h] = st0
            o, st1, _ = _gla_head_fwd(q, k, v, b[:, h * GLA_DK:(h + 1) * GLA_DK], st0)
            st[h] = st1
            o_ref[:, h * GLA_DV:(h + 1) * GLA_DV] = o
            rs = lax.rsqrt(jnp.mean(o * o, axis=-1, keepdims=True) + EPS)
            y_ref[:, h * GLA_DV:(h + 1) * GLA_DV] = (o * rs * og_ref[...] * _silu(r)).astype(BF16)

    blk = pl.BlockSpec((CHUNK, D), lambda i: (i, 0))
    return pl.pallas_call(
        body, name="gla_fwd", grid=(nc,),
        in_specs=[pl.BlockSpec((CHUNK, GLA_INP), lambda i: (i, 0)), pl.BlockSpec((LANES, GLA_QK), lambda i: (0, 0)),
                  pl.BlockSpec((1, GLA_QK), lambda i: (0, 0)), pl.BlockSpec((1, GLA_DV), lambda i: (0, 0))],
        out_specs=[blk, blk, pl.BlockSpec((1, GLA_H, GLA_DV, GLA_DK), lambda i: (i, 0, 0, 0))],
        out_shape=[jax.ShapeDtypeStruct((lp, D), F32), jax.ShapeDtypeStruct((lp, D), BF16),
                   jax.ShapeDtypeStruct((nc, GLA_H, GLA_DV, GLA_DK), F32)],
        scratch_shapes=[pltpu.VMEM((GLA_H, GLA_DV, GLA_DK), F32)],
        compiler_params=_params(("arbitrary",)),
    )(proj, jnp.pad(w_alpha2, ((0, LANES - GLA_RANK), (0, 0))), b_alpha.reshape(1, GLA_QK), o_gain.reshape(1, GLA_DV))


def _gla_bwd(proj, w_alpha2, b_alpha, o_gain, o, states, dy):
    lp = proj.shape[0]
    nc = lp // CHUNK

    def body(p_ref, wa_ref, ba_ref, og_ref, o_ref, s_ref, dy_ref, dp_ref, dwa_ref, dba_ref, dog_ref, dst):
        @pl.when(pl.program_id(0) == 0)
        def _():
            dst[...] = jnp.zeros_like(dst)
            dwa_ref[...] = jnp.zeros_like(dwa_ref)
            dba_ref[...] = jnp.zeros_like(dba_ref)
            dog_ref[...] = jnp.zeros_like(dog_ref)

        a_lr, z, b_all = _gla_gates(p_ref, wa_ref, ba_ref)
        last_row = _iota((CHUNK, GLA_DK), 0) == CHUNK - 1
        rev = _tri(CHUNK, upper=True).astype(BF16)
        dg_parts = []
        for h in range(GLA_H):
            scale = GLA_DK ** -0.5
            q = p_ref[:, h * GLA_DK:(h + 1) * GLA_DK] * scale
            k = p_ref[:, GLA_QK + h * GLA_DK:GLA_QK + (h + 1) * GLA_DK]
            v = p_ref[:, 2 * GLA_QK + h * GLA_DV:2 * GLA_QK + (h + 1) * GLA_DV]
            r = p_ref[:, 2 * GLA_QK + GLA_V + h * GLA_DV:2 * GLA_QK + GLA_V + (h + 1) * GLA_DV]
            b = b_all[:, h * GLA_DK:(h + 1) * GLA_DK]
            st0 = s_ref[0, h]
            dst1 = dst[h]
            ov = o_ref[:, h * GLA_DV:(h + 1) * GLA_DV]
            dyv = dy_ref[:, h * GLA_DV:(h + 1) * GLA_DV]
            rs = lax.rsqrt(jnp.mean(ov * ov, axis=-1, keepdims=True) + EPS)
            on = ov * rs
            dr = dyv * on * og_ref[...] * _dsilu(r)
            don = dyv * _silu(r)
            dog_ref[...] += jnp.sum(don * on, axis=0, keepdims=True)
            u = don * og_ref[...]
            do = rs * u - ov * (rs * rs * rs) * jnp.mean(ov * u, axis=-1, keepdims=True)
            eb = jnp.exp(b)
            _, _, (qe, ke, kd, a, bl) = _gla_head_fwd(q, k, v, b, st0)
            da = jnp.where(_tri(CHUNK), nt(do, v), 0.0)
            dkd = nn(v, dst1)
            dvv = tn(a, do) + nt(kd, dst1)
            dqe = nn(da, ke) + nn(do, st0)
            dke = tn(da, qe)
            ebl = jnp.exp(bl)
            dst[h] = dst1 * ebl + tn(do, qe)
            db = dqe * qe - dke * ke - dkd * kd
            db_last = jnp.sum(dkd * kd, axis=0, keepdims=True) + jnp.sum(dst1 * st0, axis=0, keepdims=True) * ebl
            db = db + jnp.where(last_row, db_last, 0.0)
            dg_parts.append(_sel_l(rev, db))
            dp_ref[:, h * GLA_DK:(h + 1) * GLA_DK] = (dqe * eb * scale).astype(BF16)
            dp_ref[:, GLA_QK + h * GLA_DK:GLA_QK + (h + 1) * GLA_DK] = (dke * jnp.exp(-b) + dkd * jnp.exp(bl - b)).astype(BF16)
            dp_ref[:, 2 * GLA_QK + h * GLA_DV:2 * GLA_QK + (h + 1) * GLA_DV] = dvv.astype(BF16)
            dp_ref[:, 2 * GLA_QK + GLA_V + h * GLA_DV:2 * GLA_QK + GLA_V + (h + 1) * GLA_DV] = dr.astype(BF16)
        dg = jnp.concatenate(dg_parts, axis=1)
        dz = dg * (1.0 / GLA_NORM) * _sigmoid(-z)
        dzb = dz.astype(BF16)
        dp_ref[:, 3072:3072 + LANES] = nt(dzb, wa_ref[...].astype(BF16)).astype(BF16)
        dwa_ref[...] += tn(a_lr.astype(BF16), dzb)
        dba_ref[...] += jnp.sum(dz, axis=0, keepdims=True)

    rv = lambda i: (nc - 1 - i, 0)
    blk = pl.BlockSpec((CHUNK, D), rv)
    fixed = lambda r, c: pl.BlockSpec((r, c), lambda i: (0, 0))
    return pl.pallas_call(
        body, name="gla_bwd", grid=(nc,),
        in_specs=[pl.BlockSpec((CHUNK, GLA_INP), rv), fixed(LANES, GLA_QK), fixed(1, GLA_QK), fixed(1, GLA_DV), blk,
                  pl.BlockSpec((1, GLA_H, GLA_DV, GLA_DK), lambda i: (nc - 1 - i, 0, 0, 0)), blk],
        out_specs=[pl.BlockSpec((CHUNK, GLA_INP), rv), fixed(LANES, GLA_QK), fixed(1, GLA_QK), fixed(1, GLA_DV)],
        out_shape=[jax.ShapeDtypeStruct((lp, GLA_INP), BF16), jax.ShapeDtypeStruct((LANES, GLA_QK), F32),
                   jax.ShapeDtypeStruct((1, GLA_QK), F32), jax.ShapeDtypeStruct((1, GLA_DV), F32)],
        scratch_shapes=[pltpu.VMEM((GLA_H, GLA_DV, GLA_DK), F32)],
        compiler_params=_params(("arbitrary",)),
    )(proj, jnp.pad(w_alpha2, ((0, LANES - GLA_RANK), (0, 0))), b_alpha.reshape(1, GLA_QK), o_gain.reshape(1, GLA_DV),
      o, states, dy)


HI = lax.Precision.HIGHEST


def _gdn_pre(prev_ref, p_ref, cw_ref, al_ref, dt_ref):
    xc = jnp.concatenate([prev_ref[:, 0:GDN_CONV], p_ref[:, 0:GDN_CONV]], axis=0)
    shifted = [pltpu.roll(xc, 3 - j, 0)[CHUNK:, :] if j < 3 else xc[CHUNK:, :] for j in range(4)]
    conv = sum(shifted[j] * cw_ref[j:j + 1, :] for j in range(4))
    act = _silu(conv)
    slab = p_ref[:, 4096:4096 + LANES]
    lane = _iota((CHUNK, LANES), 1)
    zs = slab + dt_ref[...]
    g = jnp.where(lane < GDN_H, -jnp.exp(al_ref[...]) * _softplus(zs), 0.0)
    bs = _sel_l(_tri(CHUNK).astype(BF16), g)
    beta = _sigmoid(slab)
    return shifted, conv, act, slab, zs, g, bs, beta


def _l2n(x):
    r = lax.rsqrt(jnp.sum(x * x, axis=-1, keepdims=True) + EPS)
    return x * r, r


def _gdn_chunk_fwd(q, k, v, beta, bcol, brow, s0):
    hs = range(len(q))
    ii, jj = _iota((CHUNK, CHUNK), 0), _iota((CHUNK, CHUNK), 1)
    low, eye = ii >= jj, (ii == jj).astype(F32)
    dm = [jnp.where(low, jnp.exp(jnp.where(low, bcol[h] - brow[h], 0.0)), 0.0) for h in hs]
    dstrict = [jnp.where(ii > jj, dm[h], 0.0) for h in hs]
    eb = [jnp.exp(bcol[h]) for h in hs]
    bl = [bcol[h][CHUNK - 1:CHUNK, :] for h in hs]
    kb = [k[h] * beta[h] for h in hs]
    vb = [v[h] * beta[h] for h in hs]
    nmat = [nt(kb[h], k[h]) * dstrict[h] for h in hs]
    x = [eye - nmat[h] for h in hs]
    pw = [nn(nmat[h], nmat[h], precision=HI) for h in hs]
    for it in range(5):
        x = [x[h] + nn(x[h], pw[h], precision=HI) for h in hs]
        if it < 4:
            pw = [nn(pw[h], pw[h], precision=HI) for h in hs]
    kbe = [kb[h] * eb[h] for h in hs]
    u = [nn(x[h], vb[h], precision=HI) for h in hs]
    w = [nn(x[h], kbe[h], precision=HI) for h in hs]
    vn = [u[h] - nn(w[h], s0[h]) for h in hs]
    pm = [nt(q[h], k[h]) * dm[h] for h in hs]
    qe = [q[h] * eb[h] for h in hs]
    o = [nn(pm[h], vn[h]) + nn(qe[h], s0[h]) for h in hs]
    kd = [k[h] * jnp.exp(bl[h] - bcol[h]) for h in hs]
    s1 = [s0[h] * jnp.exp(bl[h]) + tn(kd[h], vn[h]) for h in hs]
    return o, s1, dict(dm=dm, dstrict=dstrict, eb=eb, bl=bl, kb=kb, vb=vb, nmat=nmat, tinv=x, kbe=kbe, u=u, w=w, vn=vn,
                       pm=pm, qe=qe, kd=kd)


def _gdn_heads(act, beta_slab, bs, h):
    qa = act[:, h * GDN_DK:(h + 1) * GDN_DK]
    ka = act[:, GDN_H * GDN_DK + h * GDN_DK:GDN_H * GDN_DK + (h + 1) * GDN_DK]
    v = act[:, 2 * GDN_H * GDN_DK + h * GDN_DV:2 * GDN_H * GDN_DK + (h + 1) * GDN_DV]
    return qa, ka, v, beta_slab[:, GDN_H + h:GDN_H + h + 1], bs[:, h:h + 1]


def _gdn_fwd(proj, conv_w, a_log, dt_bias, o_gain):
    lp = proj.shape[0]
    nc = lp // CHUNK

    def body(prev_ref, p_ref, cw_ref, al_ref, dt_ref, og_ref, o_ref, y_ref, s_ref, st):
        @pl.when(pl.program_id(0) == 0)
        def _():
            st[...] = jnp.zeros_like(st)

        _, _, act, _, _, _, bs, beta = _gdn_pre(prev_ref, p_ref, cw_ref, al_ref, dt_ref)
        bst = bs.T
        hs = range(GDN_H)
        parts = [_gdn_heads(act, beta, bs, h) for h in hs]
        q = [_l2n(parts[h][0])[0] * (GDN_DK ** -0.5) for h in hs]
        k = [_l2n(parts[h][1])[0] for h in hs]
        s0 = [st[h] for h in hs]
        for h in hs:
            s_ref[0, h] = s0[h]
        o, s1, _ = _gdn_chunk_fwd(q, k, [parts[h][2] for h in hs], [parts[h][3] for h in hs], [parts[h][4] for h in hs],
                                  [bst[h:h + 1, :] for h in hs], s0)
        for h in hs:
            st[h] = s1[h]
            o_ref[:, h * GDN_DV:(h + 1) * GDN_DV] = o[h]
            rs = lax.rsqrt(jnp.mean(o[h] * o[h], axis=-1, keepdims=True) + EPS)
            gate = p_ref[:, GDN_CONV + h * GDN_DV:GDN_CONV + (h + 1) * GDN_DV]
            y_ref[:, h * GDN_DV:(h + 1) * GDN_DV] = (o[h] * rs * og_ref[...] * _silu(gate)).astype(BF16)

    blk = pl.BlockSpec((CHUNK, D), lambda i: (i, 0))
    fixed = lambda r, c: pl.BlockSpec((r, c), lambda i: (0, 0))
    return pl.pallas_call(
        body, name="gdn_fwd", grid=(nc,),
        in_specs=[pl.BlockSpec((CHUNK, GDN_INP), lambda i: (jnp.maximum(i - 1, 0), 0)),
                  pl.BlockSpec((CHUNK, GDN_INP), lambda i: (i, 0)), fixed(8, GDN_CONV), fixed(1, LANES), fixed(1, LANES),
                  fixed(1, GDN_DV)],
        out_specs=[blk, blk, pl.BlockSpec((1, GDN_H, GDN_DK, GDN_DV), lambda i: (i, 0, 0, 0))],
        out_shape=[jax.ShapeDtypeStruct((lp, D), F32), jax.ShapeDtypeStruct((lp, D), BF16),
                   jax.ShapeDtypeStruct((nc, GDN_H, GDN_DK, GDN_DV), F32)],
        scratch_shapes=[pltpu.VMEM((GDN_H, GDN_DK, GDN_DV), F32)],
        compiler_params=_params(("arbitrary",)),
    )(proj, proj, jnp.pad(conv_w.reshape(4, GDN_CONV), ((0, 4), (0, 0))), jnp.pad(a_log, (0, LANES - GDN_H)).reshape(1, LANES),
      jnp.pad(dt_bias, (0, LANES - GDN_H)).reshape(1, LANES), o_gain.reshape(1, GDN_DV))


def _gdn_bwd(proj, conv_w, a_log, dt_bias, o_gain, o, states, dy):
    lp = proj.shape[0]
    nc = lp // CHUNK

    def body(prev_ref, p_ref, cw_ref, al_ref, dt_ref, og_ref, o_ref, s_ref, dy_ref,
             dp_ref, dcw_ref, dal_ref, ddt_ref, dog_ref, dst, dconv_next):
        @pl.when(pl.program_id(0) == 0)
        def _():
            dst[...] = jnp.zeros_like(dst)
            dconv_next[...] = jnp.zeros_like(dconv_next)
            dcw_ref[...] = jnp.zeros_like(dcw_ref)
            dal_ref[...] = jnp.zeros_like(dal_ref)
            ddt_ref[...] = jnp.zeros_like(ddt_ref)
            dog_ref[...] = jnp.zeros_like(dog_ref)

        shifted, conv, act, slab, zs, g, bs, beta = _gdn_pre(prev_ref, p_ref, cw_ref, al_ref, dt_ref)
        bst = bs.T
        lane = _iota((CHUNK, LANES), 1)
        ones = jnp.ones((CHUNK, LANES), F32)
        db_slab = jnp.zeros((CHUNK, LANES), F32)
        dbeta_slab = jnp.zeros((CHUNK, LANES), F32)
        last_row = _iota((CHUNK, 1), 0) == CHUNK - 1
        hs = range(GDN_H)
        scale = GDN_DK ** -0.5
        parts = [_gdn_heads(act, beta, bs, h) for h in hs]
        qa, ka, v = [parts[h][0] for h in hs], [parts[h][1] for h in hs], [parts[h][2] for h in hs]
        bet, bcol = [parts[h][3] for h in hs], [parts[h][4] for h in hs]
        qn_ = [_l2n(qa[h]) for h in hs]
        kn_ = [_l2n(ka[h]) for h in hs]
        q = [qn_[h][0] * scale for h in hs]
        k, rq, rk = [kn_[h][0] for h in hs], [qn_[h][1] for h in hs], [kn_[h][1] for h in hs]
        s0 = [s_ref[0, h] for h in hs]
        ds1 = [dst[h] for h in hs]
        do = []
        for h in hs:
            ov = o_ref[:, h * GDN_DV:(h + 1) * GDN_DV]
            dyv = dy_ref[:, h * GDN_DV:(h + 1) * GDN_DV]
            gate = p_ref[:, GDN_CONV + h * GDN_DV:GDN_CONV + (h + 1) * GDN_DV]
            rs = lax.rsqrt(jnp.mean(ov * ov, axis=-1, keepdims=True) + EPS)
            on = ov * rs
            dp_ref[:, GDN_CONV + h * GDN_DV:GDN_CONV + (h + 1) * GDN_DV] = (dyv * on * og_ref[...] * _dsilu(gate)).astype(BF16)
            don = dyv * _silu(gate)
            dog_ref[...] += jnp.sum(don * on, axis=0, keepdims=True)
            uu = don * og_ref[...]
            do.append(rs * uu - ov * (rs * rs * rs) * jnp.mean(ov * uu, axis=-1, keepdims=True))
        _, _, f = _gdn_chunk_fwd(q, k, v, bet, bcol, [bst[h:h + 1, :] for h in hs], s0)
        dm, dstrict, eb, bl, kb, nmat, tinv = f["dm"], f["dstrict"], f["eb"], f["bl"], f["kb"], f["nmat"], f["tinv"]
        kbe, u, w, vn, pm, qe, kd = f["kbe"], f["u"], f["w"], f["vn"], f["pm"], f["qe"], f["kd"]
        ebl = [jnp.exp(bl[h]) for h in hs]
        dvn = [tn(pm[h], do[h]) + nn(kd[h], ds1[h]) for h in hs]
        dpr = [nt(do[h], vn[h]) for h in hs]
        dqe = [nt(do[h], s0[h]) for h in hs]
        dkd = [nt(vn[h], ds1[h]) for h in hs]
        for h in hs:
            dst[h] = ds1[h] * ebl[h] + tn(qe[h], do[h]) - tn(w[h], dvn[h])
        du_ = [tn(tinv[h], dvn[h], precision=HI) for h in hs]
        dw_ = [tn(tinv[h], -nt(dvn[h], s0[h]), precision=HI) for h in hs]
        dn = [-(nt(du_[h], u[h]) + nt(dw_[h], w[h])) for h in hs]
        dqk = [dpr[h] * dm[h] for h in hs]
        dkk = [dn[h] * dstrict[h] for h in hs]
        gsum = [dpr[h] * pm[h] + dn[h] * nmat[h] for h in hs]
        dkb = [nn(dkk[h], k[h]) + dw_[h] * eb[h] for h in hs]
        dk = [tn(dkk[h], kb[h]) + tn(dqk[h], q[h]) + dkd[h] * jnp.exp(bl[h] - bcol[h]) + dkb[h] * bet[h] for h in hs]
        dq = [nn(dqk[h], k[h]) + dqe[h] * eb[h] for h in hs]
        colsum = [tn(gsum[h], ones, precision=HI)[:, 0:1] for h in hs]
        dact_q, dact_k, dact_v = [], [], []
        for h in hs:
            dbeta = jnp.sum(dkb[h] * k[h], axis=-1, keepdims=True) + jnp.sum(du_[h] * v[h], axis=-1, keepdims=True)
            skd = jnp.sum(dkd[h] * kd[h], axis=-1, keepdims=True)
            db = (jnp.sum(gsum[h], axis=-1, keepdims=True) - colsum[h] + jnp.sum(dqe[h] * qe[h], axis=-1, keepdims=True)
                  + jnp.sum(dw_[h] * kbe[h], axis=-1, keepdims=True) - skd)
            db_last = jnp.sum(skd, axis=0, keepdims=True) + jnp.sum(ds1[h] * s0[h]) * ebl[h]
            db = db + jnp.where(last_row, db_last, 0.0)
            db_slab = db_slab + jnp.where(lane == h, db, 0.0)
            dbeta_slab = dbeta_slab + jnp.where(lane == GDN_H + h, dbeta, 0.0)
            dqn = dq[h] * scale
            dact_q.append(rq[h] * dqn - qa[h] * (rq[h] * rq[h] * rq[h]) * jnp.sum(qa[h] * dqn, axis=-1, keepdims=True))
            dact_k.append(rk[h] * dk[h] - ka[h] * (rk[h] * rk[h] * rk[h]) * jnp.sum(ka[h] * dk[h], axis=-1, keepdims=True))
            dact_v.append(du_[h] * bet[h])
        dact = jnp.concatenate(dact_q + dact_k + dact_v, axis=1)
        dconv = dact * _dsilu(conv)
        for j in range(4):
            dcw_ref[j:j + 1, :] += jnp.sum(dconv * shifted[j], axis=0, keepdims=True)
        dcat = jnp.concatenate([dconv, dconv_next[...]], axis=0)
        dx = dconv * cw_ref[3:4, :]
        for j in range(3):
            dx = dx + pltpu.roll(dcat, 2 * CHUNK - (3 - j), 0)[:CHUNK, :] * cw_ref[j:j + 1, :]
        dconv_next[...] = dconv
        dp_ref[:, 0:GDN_CONV] = dx.astype(BF16)
        dg = _sel_l(_tri(CHUNK, upper=True).astype(BF16), db_slab)
        da = dg * (-jnp.exp(al_ref[...])) * _sigmoid(zs)
        da = jnp.where(lane < GDN_H, da, 0.0)
        dal_ref[...] += jnp.sum(dg * g, axis=0, keepdims=True)
        ddt_ref[...] += jnp.sum(da, axis=0, keepdims=True)
        dp_ref[:, 4096:4096 + LANES] = (da + dbeta_slab * beta * (1.0 - beta)).astype(BF16)

    rv = lambda i: (nc - 1 - i, 0)
    blk = pl.BlockSpec((CHUNK, D), rv)
    fixed = lambda r, c: pl.BlockSpec((r, c), lambda i: (0, 0))
    return pl.pallas_call(
        body, name="gdn_bwd", grid=(nc,),
        in_specs=[pl.BlockSpec((CHUNK, GDN_INP), lambda i: (jnp.maximum(nc - 2 - i, 0), 0)),
                  pl.BlockSpec((CHUNK, GDN_INP), rv), fixed(8, GDN_CONV), fixed(1, LANES), fixed(1, LANES), fixed(1, GDN_DV),
                  blk, pl.BlockSpec((1, GDN_H, GDN_DK, GDN_DV), lambda i: (nc - 1 - i, 0, 0, 0)), blk],
        out_specs=[pl.BlockSpec((CHUNK, GDN_INP), rv), fixed(8, GDN_CONV), fixed(1, LANES), fixed(1, LANES), fixed(1, GDN_DV)],
        out_shape=[jax.ShapeDtypeStruct((lp, GDN_INP), BF16), jax.ShapeDtypeStruct((8, GDN_CONV), F32),
                   jax.ShapeDtypeStruct((1, LANES), F32), jax.ShapeDtypeStruct((1, LANES), F32),
                   jax.ShapeDtypeStruct((1, GDN_DV), F32)],
        scratch_shapes=[pltpu.VMEM((GDN_H, GDN_DK, GDN_DV), F32), pltpu.VMEM((CHUNK, GDN_CONV), F32)],
        compiler_params=_params(("arbitrary",)),
    )(proj, proj, jnp.pad(conv_w.reshape(4, GDN_CONV), ((0, 4), (0, 0))), jnp.pad(a_log, (0, LANES - GDN_H)).reshape(1, LANES),
      jnp.pad(dt_bias, (0, LANES - GDN_H)).reshape(1, LANES), o_gain.reshape(1, GDN_DV), o, states, dy)


def _coords():
    return lax.axis_index("x"), lax.axis_index("y"), lax.axis_index("c")


def _other_chips(x, y):
    return [(1 - x, y, 2 * (1 - x) + y), (x, 1 - y, 2 * x + 1 - y), (1 - x, 1 - y, 2 * (1 - x) + 1 - y)]


def _gather8(v, *, reduce, name):
    r, c = v.shape

    def body(v_ref, out_ref, *scratch):
        if reduce:
            buf, send_sems, recv_sems = scratch
        else:
            buf = out_ref
            send_sems, recv_sems = scratch
        x, y, cc = _coords()
        me = 4 * x + 2 * y + cc
        buf[me] = v_ref[...]
        copies = []
        for k in range(1, 8):
            px, py, pc = x ^ (k >> 2), y ^ ((k >> 1) & 1), cc ^ (k & 1)
            copies.append(pltpu.make_async_remote_copy(
                src_ref=v_ref, dst_ref=buf.at[me], send_sem=send_sems.at[k - 1], recv_sem=recv_sems.at[k - 1],
                device_id=(px, py, pc), device_id_type=MESH))
        for cp in copies:
            cp.start()
        for k in range(1, 8):
            peer = (x ^ (k >> 2)) * 4 + (y ^ ((k >> 1) & 1)) * 2 + (cc ^ (k & 1))
            pltpu.make_async_remote_copy(
                src_ref=v_ref, dst_ref=buf.at[peer], send_sem=send_sems.at[k - 1], recv_sem=recv_sems.at[k - 1],
                device_id=(x, y, cc), device_id_type=MESH).wait_recv()
        for cp in copies:
            cp.wait_send()
        if reduce:
            acc = buf[0]
            for d in range(1, 8):
                acc = acc + buf[d]
            out_ref[...] = acc

    scratch = [pltpu.SemaphoreType.DMA((7,)), pltpu.SemaphoreType.DMA((7,))]
    if reduce:
        scratch = [pltpu.VMEM((8, r, c), F32)] + scratch
    return pl.pallas_call(
        body, name=name, in_specs=[VM], out_specs=VM,
        out_shape=jax.ShapeDtypeStruct((r, c) if reduce else (8, r, c), F32),
        scratch_shapes=scratch, compiler_params=_params(),
    )(v)


def _ag_weights(w4):
    _, r, c = w4.shape
    half = r // 2

    def body(w_ref, out_ref, send_sems, recv_sems):
        x, y, cc = _coords()
        p = 2 * x + y
        chips = _other_chips(x, y)

        def rows(chip, hf):
            return out_ref.at[chip, pl.ds(hf * half, half), :]

        first = [pltpu.make_async_remote_copy(
            src_ref=rows(p, cc), dst_ref=rows(p, cc), send_sem=send_sems.at[k],
            recv_sem=recv_sems.at[k], device_id=(cx, cy, cc), device_id_type=MESH) for k, (cx, cy, _) in enumerate(chips)]
        for cp in first:
            cp.start()
        passed = []
        for k, (_, _, blk) in enumerate(chips):
            pltpu.make_async_remote_copy(
                src_ref=rows(blk, cc), dst_ref=rows(blk, cc), send_sem=send_sems.at[k], recv_sem=recv_sems.at[k],
                device_id=(x, y, cc), device_id_type=MESH).wait_recv()
            fw = pltpu.make_async_remote_copy(
                src_ref=rows(blk, cc), dst_ref=rows(blk, cc), send_sem=send_sems.at[3 + k], recv_sem=recv_sems.at[3 + k],
                device_id=(x, y, 1 - cc), device_id_type=MESH)
            fw.start()
            passed.append(fw)
        for k, (_, _, blk) in enumerate(chips):
            pltpu.make_async_remote_copy(
                src_ref=rows(blk, 1 - cc), dst_ref=rows(blk, 1 - cc), send_sem=send_sems.at[3 + k], recv_sem=recv_sems.at[3 + k],
                device_id=(x, y, cc), device_id_type=MESH).wait_recv()
        for cp in first + passed:
            cp.wait_send()

    return pl.pallas_call(
        body, name="ag_weights", in_specs=[ANY], out_specs=ANY, out_shape=jax.ShapeDtypeStruct(w4.shape, w4.dtype),
        scratch_shapes=[pltpu.SemaphoreType.DMA((6,)), pltpu.SemaphoreType.DMA((6,))], input_output_aliases={0: 0},
        compiler_params=_params(),
    )(w4)


def _swap_halves(g, *, name):
    nb, r, c = g.shape
    half = r // 2

    def body(g_ref, out_ref, send_sem, recv_sem):
        x, y, cc = _coords()
        cp = pltpu.make_async_remote_copy(
            src_ref=g_ref.at[:, pl.ds((1 - cc) * half, half), :], dst_ref=out_ref, send_sem=send_sem, recv_sem=recv_sem,
            device_id=(x, y, 1 - cc), device_id_type=MESH)
        cp.start()
        cp.wait()

    return pl.pallas_call(
        body, name=name, in_specs=[ANY], out_specs=ANY, out_shape=jax.ShapeDtypeStruct((nb, half, c), g.dtype),
        scratch_shapes=[pltpu.SemaphoreType.DMA, pltpu.SemaphoreType.DMA], compiler_params=_params(),
    )(g)


def _my_half_index():
    return lax.axis_index("c").astype(jnp.int32).reshape(1)


def _add_halves(g, got):
    nb, r, c = g.shape
    half = r // 2
    tr = _tile(half, 512, 16)
    nt_ = half // tr

    def body(c_ref, a_ref, b_ref, o_ref):
        o_ref[...] = (a_ref[...].astype(F32) + b_ref[...].astype(F32)).astype(BF16)

    return pl.pallas_call(
        body, name="rs_add_sibling",
        grid_spec=pltpu.PrefetchScalarGridSpec(
            num_scalar_prefetch=1, grid=(nb, nt_),
            in_specs=[pl.BlockSpec((1, tr, c), lambda b, i, cr: (b, cr[0] * nt_ + i, 0)),
                      pl.BlockSpec((1, tr, c), lambda b, i, cr: (b, i, 0))],
            out_specs=pl.BlockSpec((1, tr, c), lambda b, i, cr: (b, i, 0))),
        out_shape=jax.ShapeDtypeStruct((nb, half, c), BF16), compiler_params=_params(("parallel", "parallel")),
    )(_my_half_index(), g, got)


def _scatter_chips(s):
    nb, hrows, c = s.shape

    def body(s_ref, out_ref, send_sems, recv_sems):
        x, y, cc = _coords()
        p = 2 * x + y
        chips = _other_chips(x, y)
        sends = [pltpu.make_async_remote_copy(
            src_ref=s_ref.at[blk], dst_ref=out_ref.at[k], send_sem=send_sems.at[k], recv_sem=recv_sems.at[k],
            device_id=(cx, cy, cc), device_id_type=MESH) for k, (cx, cy, blk) in enumerate(chips)]
        for cp in sends:
            cp.start()
        for k in range(3):
            pltpu.make_async_remote_copy(
                src_ref=s_ref.at[p], dst_ref=out_ref.at[k], send_sem=send_sems.at[k], recv_sem=recv_sems.at[k],
                device_id=(x, y, cc), device_id_type=MESH).wait_recv()
        for cp in sends:
            cp.wait_send()

    return pl.pallas_call(
        body, name="rs_scatter", in_specs=[ANY], out_specs=ANY, out_shape=jax.ShapeDtypeStruct((3, hrows, c), s.dtype),
        scratch_shapes=[pltpu.SemaphoreType.DMA((3,)), pltpu.SemaphoreType.DMA((3,))], compiler_params=_params(),
    )(s)


def _sum_chips(s, got):
    nb, hrows, c = s.shape
    tr = _tile(hrows, 512, 16)

    def body(idx_ref, own_ref, got_ref, o_ref):
        p = idx_ref[0]
        own = own_ref[0].astype(F32)
        parts = [got_ref[k].astype(F32) for k in range(3)]
        acc = jnp.zeros_like(own)
        for q in range(4):
            val = own
            for k, rel in enumerate((2, 1, 3)):
                val = jnp.where((p ^ rel) == q, parts[k], val)
            acc = acc + val
        o_ref[...] = acc

    idx = (2 * lax.axis_index("x") + lax.axis_index("y")).astype(jnp.int32).reshape(1)
    return pl.pallas_call(
        body, name="rs_sum_chips",
        grid_spec=pltpu.PrefetchScalarGridSpec(
            num_scalar_prefetch=1, grid=(hrows // tr,),
            in_specs=[pl.BlockSpec((1, tr, c), lambda i, pr: (pr[0], i, 0)), pl.BlockSpec((3, tr, c), lambda i, pr: (0, i, 0))],
            out_specs=pl.BlockSpec((tr, c), lambda i, pr: (i, 0))),
        out_shape=jax.ShapeDtypeStruct((hrows, c), F32), compiler_params=_params(("parallel",)),
    )(idx, s, got)


def _swap_sibling(t):
    def body(t_ref, out_ref, send_sem, recv_sem):
        x, y, cc = _coords()
        cp = pltpu.make_async_remote_copy(src_ref=t_ref, dst_ref=out_ref, send_sem=send_sem, recv_sem=recv_sem,
                                          device_id=(x, y, 1 - cc), device_id_type=MESH)
        cp.start()
        cp.wait()

    return pl.pallas_call(
        body, name="rs_join", in_specs=[ANY], out_specs=ANY, out_shape=jax.ShapeDtypeStruct(t.shape, t.dtype),
        scratch_shapes=[pltpu.SemaphoreType.DMA, pltpu.SemaphoreType.DMA], compiler_params=_params(),
    )(t)


def _reduce_scatter(g):
    got = _swap_halves(g, name="rs_swap")
    s = _add_halves(g, got)
    recv = _scatter_chips(s)
    t = _sum_chips(s, recv)
    r = _swap_sibling(t)
    first = lax.axis_index("c") == 0
    return jnp.concatenate([jnp.where(first, t, r), jnp.where(first, r, t)], axis=0)


_BIG = (("w_gate_up", 2), ("w_down", 1), ("fox_w_in", 2), ("fox_w_out", 1), ("gla_w_in", 2), ("gla_w_out", 1),
        ("gdn_w_in", 2), ("gdn_w_out", 1))
_SMALL_SHARDED = (("meta_tokens", 1), ("gla_w_alpha2", 2), ("gdn_conv_w", 3))
_REPLICATED = ("norm_mix", "norm_ffn", "fox_b_f", "fox_q_gain", "fox_k_gain", "gla_b_alpha", "gla_o_gain",
               "gdn_a_log", "gdn_dt_bias", "gdn_o_gain")
_WEIGHTS = ("meta_tokens", "norm_mix", "norm_ffn", "w_gate_up", "w_down", "fox_w_in", "fox_b_f", "fox_q_gain",
            "fox_k_gain", "fox_w_out", "gla_w_in", "gla_w_alpha2", "gla_b_alpha", "gla_o_gain", "gla_w_out",
            "gdn_w_in", "gdn_conv_w", "gdn_a_log", "gdn_dt_bias", "gdn_o_gain", "gdn_w_out")
_PACK_ROWS = 512
_IN_W = ("fox_w_in", "gla_w_in", "gdn_w_in")
_OUT_W = ("fox_w_out", "gla_w_out", "gdn_w_out")


def _rows16(n):
    return -(-n // 16) * 16


def _pack(arrays, width, row_mult, dtype):
    flat = jnp.concatenate([a.astype(dtype).reshape(-1) for a in arrays])
    per = width * row_mult
    n = -(-flat.shape[0] // per) * per
    return jnp.pad(flat, (0, n - flat.shape[0])).reshape(n // width, width)


def _unpack(flat, shapes):
    out, off = [], 0
    for s in shapes:
        n = 1
        for d in s:
            n *= d
        out.append(flat[off:off + n].reshape(s))
        off += n
    return out


def _unpack_cols(flat2, shapes):
    out, off = [], 0
    for s in shapes:
        n = 1
        for d in s:
            n *= d
        out.append(flat2[:, off:off + n].reshape((flat2.shape[0],) + tuple(s)))
        off += n
    return out


def _pad_cols(w, n):
    return jnp.pad(w, [(0, 0)] * (w.ndim - 1) + [(0, n - w.shape[-1])])


def kernel(x, meta_tokens, norm_mix, norm_ffn, w_gate_up, w_down, fox_w_in, fox_b_f, fox_q_gain, fox_k_gain, fox_w_out, gla_w_in, gla_w_alpha2, gla_b_alpha, gla_o_gain, gla_w_out, gdn_w_in, gdn_conv_w, gdn_a_log, gdn_dt_bias, gdn_o_gain, gdn_w_out, loss_target, m_meta_tokens, m_norm_mix, m_norm_ffn, m_w_gate_up, m_w_down, m_fox_w_in, m_fox_b_f, m_fox_q_gain, m_fox_k_gain, m_fox_w_out, m_gla_w_in, m_gla_w_alpha2, m_gla_b_alpha, m_gla_o_gain, m_gla_w_out, m_gdn_w_in, m_gdn_conv_w, m_gdn_a_log, m_gdn_dt_bias, m_gdn_o_gain, m_gdn_w_out, v_meta_tokens, v_norm_mix, v_norm_ffn, v_w_gate_up, v_w_down, v_fox_w_in, v_fox_b_f, v_fox_q_gain, v_fox_k_gain, v_fox_w_out, v_gla_w_in, v_gla_w_alpha2, v_gla_b_alpha, v_gla_o_gain, v_gla_w_out, v_gdn_w_in, v_gdn_conv_w, v_gdn_a_log, v_gdn_dt_bias, v_gdn_o_gain, v_gdn_w_out):
    W = dict(meta_tokens=meta_tokens, norm_mix=norm_mix, norm_ffn=norm_ffn, w_gate_up=w_gate_up, w_down=w_down,
             fox_w_in=fox_w_in, fox_b_f=fox_b_f, fox_q_gain=fox_q_gain, fox_k_gain=fox_k_gain, fox_w_out=fox_w_out,
             gla_w_in=gla_w_in, gla_w_alpha2=gla_w_alpha2, gla_b_alpha=gla_b_alpha, gla_o_gain=gla_o_gain,
             gla_w_out=gla_w_out, gdn_w_in=gdn_w_in, gdn_conv_w=gdn_conv_w, gdn_a_log=gdn_a_log,
             gdn_dt_bias=gdn_dt_bias, gdn_o_gain=gdn_o_gain, gdn_w_out=gdn_w_out)
    M = dict(meta_tokens=m_meta_tokens, norm_mix=m_norm_mix, norm_ffn=m_norm_ffn, w_gate_up=m_w_gate_up, w_down=m_w_down,
             fox_w_in=m_fox_w_in, fox_b_f=m_fox_b_f, fox_q_gain=m_fox_q_gain, fox_k_gain=m_fox_k_gain,
             fox_w_out=m_fox_w_out, gla_w_in=m_gla_w_in, gla_w_alpha2=m_gla_w_alpha2, gla_b_alpha=m_gla_b_alpha,
             gla_o_gain=m_gla_o_gain, gla_w_out=m_gla_w_out, gdn_w_in=m_gdn_w_in, gdn_conv_w=m_gdn_conv_w,
             gdn_a_log=m_gdn_a_log, gdn_dt_bias=m_gdn_dt_bias, gdn_o_gain=m_gdn_o_gain, gdn_w_out=m_gdn_w_out)
    V = dict(meta_tokens=v_meta_tokens, norm_mix=v_norm_mix, norm_ffn=v_norm_ffn, w_gate_up=v_w_gate_up, w_down=v_w_down,
             fox_w_in=v_fox_w_in, fox_b_f=v_fox_b_f, fox_q_gain=v_fox_q_gain, fox_k_gain=v_fox_k_gain,
             fox_w_out=v_fox_w_out, gla_w_in=v_gla_w_in, gla_w_alpha2=v_gla_w_alpha2, gla_b_alpha=v_gla_b_alpha,
             gla_o_gain=v_gla_o_gain, gla_w_out=v_gla_w_out, gdn_w_in=v_gdn_w_in, gdn_conv_w=v_gdn_conv_w,
             gdn_a_log=v_gdn_a_log, gdn_dt_bias=v_gdn_dt_bias, gdn_o_gain=v_gdn_o_gain, gdn_w_out=v_gdn_w_out)
    chip = 2 * lax.axis_index("x") + lax.axis_index("y")

    pieces, offs, r = [], {}, FFN_ROWS
    for n in _IN_W:
        nc = W[n].shape[2]
        for l in range(W[n].shape[0]):
            pieces.append(jnp.pad(W[n][l].T.astype(BF16), ((0, _rows16(nc) - nc), (0, 0))))
            offs[n, l] = r
            r += _rows16(nc)
    for n in _OUT_W:
        for l in range(W[n].shape[0]):
            pieces.append(W[n][l].astype(BF16))
            offs[n, l] = r
            r += W[n].shape[1]
    rows = -(-r // _PACK_ROWS) * _PACK_ROWS
    packed = jnp.concatenate([jnp.swapaxes(w_gate_up, 1, 2).reshape(-1, D).astype(BF16), w_down.reshape(-1, D).astype(BF16)]
                             + pieces + [jnp.zeros((rows - r, D), BF16)], axis=0)
    wpk = _ag_weights(lax.dynamic_update_slice(lax.empty((4, rows, D), BF16), packed[None], (chip, 0, 0)))

    def in_t(n, l, npad):
        nc = W[n].shape[2]
        return jnp.concatenate([wpk[q, offs[n, l]:offs[n, l] + nc] for q in range(4)] + [jnp.zeros((npad - 4 * nc, D), BF16)], 0)

    def out_w(n, l):
        return jnp.concatenate([wpk[q, offs[n, l]:offs[n, l] + W[n].shape[1]] for q in range(4)], axis=0)

    full = {}
    small = _pack([W[n] for n, _ in _SMALL_SHARDED], LANES, 8, F32)
    small_all = _gather8(small, reduce=False, name="gather_small").reshape(8, -1)
    for (n, ax), seg in zip(_SMALL_SHARDED, _unpack_cols(small_all, [W[n].shape for n, _ in _SMALL_SHARDED])):
        full[n] = jnp.concatenate([seg[2 * q] for q in range(4)], axis=ax)
    fox_in = [in_t("fox_w_in", l, FOX_INP) for l in range(fox_w_in.shape[0])]
    gla_in = [in_t("gla_w_in", l, GLA_INP) for l in range(gla_w_in.shape[0])]
    gdn_in = [in_t("gdn_w_in", l, GDN_INP) for l in range(gdn_w_in.shape[0])]
    for n in _OUT_W:
        full[n] = [out_w(n, l) for l in range(W[n].shape[0])]
    w_alpha2, conv_w = full["gla_w_alpha2"][0], full["gdn_conv_w"][0]

    h = jnp.concatenate([jnp.zeros((META0, D), F32), full["meta_tokens"], x[0]], axis=0)
    saved = []
    for i in range(DEPTH):
        kind, j = i % 3, i // 3
        y = _rms_fwd(h, norm_mix[i], name=f"norm_mix{i}")
        if kind == 0:
            proj = _mm(y, fox_in[j], tb=True, name=f"fox_in{j}")
            qa, ka, va = _fox_prep(proj, fox_b_f[j], fox_q_gain[j], fox_k_gain[j])
            o, og, lse = _fox_attn_fwd(qa, ka, va, proj)
            w_out, mix = full["fox_w_out"][j], (proj, qa, ka, va, o, lse)
        elif kind == 1:
            proj = _mm(y, gla_in[j], tb=True, name=f"gla_in{j}")
            o, og, states = _gla_fwd(proj, w_alpha2, gla_b_alpha[j], gla_o_gain[j])
            w_out, mix = full["gla_w_out"][j], (proj, o, states)
        else:
            proj = _mm(y, gdn_in[j], tb=True, name=f"gdn_in{j}")
            o, og, states = _gdn_fwd(proj, conv_w, gdn_a_log[j], gdn_dt_bias[j], gdn_o_gain[j])
            w_out, mix = full["gdn_w_out"][j], (proj, o, states)
        hm = _mm(og, w_out, add=h, name=f"mix_out{i}")
        yf = _rms_fwd(hm, norm_ffn[i], name=f"norm_ffn{i}")
        gate, up, act = _ffn_up(yf, wpk, i)
        hn = _ffn_down(act, wpk, i, hm)
        saved.append((h, y, mix, og, w_out, hm, yf, gate, up, act))
        h = hn
    dh, loss_tile = _loss_head(h, loss_target[0])

    G = {n: [None] * W[n].shape[0] for n in _WEIGHTS if n not in ("meta_tokens", "w_gate_up", "w_down") + _IN_W}
    GT = {}
    gpk = jnp.zeros((4, rows, D), BF16)
    for i in reversed(range(DEPTH)):
        kind, j = i % 3, i // 3
        h_in, y, mix, og, w_out, hm, yf, gate, up, act = saved[i]
        dg, du = _ffn_dact(dh, wpk, i, gate, up)
        gpk = _ffn_dw_down(act, dh, gpk, i)
        dyf = _ffn_dyf(dg, du, wpk, i)
        gpk = _ffn_dw_gu(dg, du, yf, gpk, i)
        dhm, dnf = _rms_bwd(hm, norm_ffn[i], dyf, dh, name=f"d_norm_ffn{i}")
        G["norm_ffn"][i] = dnf[0]
        dog = _mm(dhm, w_out, tb=True, name=f"d_og{i}")
        dw_out = _mm(og, dhm, ta=True, out_dtype=BF16, name=f"d_w_out{i}")
        if kind == 0:
            proj, qa, ka, va, o, lse = mix
            doa, q2, dgate = _fox_gate_bwd(dog, o, proj, lse, qa)
            dqn, dkn, dv, dct = _fox_attn_bwd(q2, ka, va, doa)
            dproj, dqg, dkg, dbf = _fox_prep_bwd(proj, fox_b_f[j], fox_q_gain[j], fox_k_gain[j], dqn, dkn, dv, dgate, dct)
            G["fox_w_out"][j] = dw_out
            G["fox_q_gain"][j] = dqg.reshape(FOX_H, FOX_DH).sum(0)
            G["fox_k_gain"][j] = dkg.reshape(FOX_H, FOX_DH).sum(0)
            G["fox_b_f"][j] = dbf[0, :FOX_H]
            w_in, wname = fox_in[j], "fox_w_in"
        elif kind == 1:
            proj, o, states = mix
            dproj, dwa, dba, dogain = _gla_bwd(proj, w_alpha2, gla_b_alpha[j], gla_o_gain[j], o, states, dog)
            G["gla_w_out"][j] = dw_out
            G["gla_w_alpha2"][j] = dwa[:GLA_RANK]
            G["gla_b_alpha"][j] = dba[0]
            G["gla_o_gain"][j] = dogain[0]
            w_in, wname = gla_in[j], "gla_w_in"
        else:
            proj, o, states = mix
            dproj, dcw, dal, ddt, dogain = _gdn_bwd(proj, conv_w, gdn_a_log[j], gdn_dt_bias[j], gdn_o_gain[j], o, states, dog)
            G["gdn_w_out"][j] = dw_out
            G["gdn_conv_w"][j] = dcw[:4].reshape(4, 1, GDN_CONV)
            G["gdn_a_log"][j] = dal[0, :GDN_H]
            G["gdn_dt_bias"][j] = ddt[0, :GDN_H]
            G["gdn_o_gain"][j] = dogain[0]
            w_in, wname = gdn_in[j], "gdn_w_in"
        dy = _mm(dproj, w_in, name=f"d_y{i}")
        GT[wname, j] = _mm(dproj, y, ta=True, out_dtype=BF16, name=f"d_w_in{i}")
        dh, dnm = _rms_bwd(h_in, norm_mix[i], dy, dhm, name=f"d_norm_mix{i}")
        G["norm_mix"][i] = dnm[0]
    grad_x = dh[ROW0:][None]
    G = {n: (v if n in _OUT_W else jnp.stack(v)) for n, v in G.items()}
    G["meta_tokens"] = dh[META0:ROW0]

    blocks = []
    for q in range(4):
        parts = []
        for n in _IN_W:
            nc = W[n].shape[2]
            for l in range(W[n].shape[0]):
                parts.append(jnp.pad(GT[n, l][q * nc:(q + 1) * nc], ((0, _rows16(nc) - nc), (0, 0))))
        for n in _OUT_W:
            nr = W[n].shape[1]
            for l in range(W[n].shape[0]):
                parts.append(G[n][l][q * nr:(q + 1) * nr])
        blocks.append(jnp.concatenate(parts + [jnp.zeros((rows - r, D), BF16)], axis=0))
    gpk = lax.dynamic_update_slice(gpk, jnp.stack(blocks), (0, FFN_ROWS, 0))
    reduced = _reduce_scatter(gpk)
    grads = {}
    for n in _IN_W:
        grads[n] = jnp.stack([reduced[offs[n, l]:offs[n, l] + W[n].shape[2]].T for l in range(W[n].shape[0])])
    for n in _OUT_W:
        grads[n] = jnp.stack([reduced[offs[n, l]:offs[n, l] + W[n].shape[1]] for l in range(W[n].shape[0])])
    small_names = [n for n, _ in _SMALL_SHARDED] + list(_REPLICATED)
    small_g = _pack([G[n] for n in small_names] + [loss_tile[0, 0:1]], LANES, 8, F32)
    small_sum = _gather8(small_g, reduce=True, name="allreduce_small").reshape(-1)
    small_shapes = [G[n].shape for n in small_names] + [(1,)]
    small_vals = _unpack(small_sum, small_shapes)
    loss = small_vals[-1][0]
    for n, val in zip(small_names, small_vals[:-1]):
        grads[n] = val
    for n, ax in _SMALL_SHARDED:
        sz = W[n].shape[ax]
        grads[n] = lax.dynamic_slice_in_dim(grads[n], chip * sz, sz, axis=ax)

    delta, new_m, new_v = {}, {}, {}
    for n, off, tr_ in (("w_gate_up", OFF_GU, True), ("w_down", OFF_DOWN, False)):
        grads[n], delta[n], new_m[n], new_v[n] = _adamw_packed(W[n], reduced, M[n], V[n], row_off=off, transposed=tr_,
                                                               name=f"adamw_{n}")
    for n in _IN_W + _OUT_W:
        delta[n], new_m[n], new_v[n] = _adamw(W[n], grads[n], M[n], V[n], name=f"adamw_{n}")
    tiny = [n for n in _WEIGHTS if n not in dict(_BIG)]
    packs = [_pack([T[n] for n in tiny], LANES, 8, F32) for T in (W, grads, M, V)]
    outs = _adamw(*packs, name="adamw_small")
    shapes = [W[n].shape for n in tiny]
    for dst, o in zip((delta, new_m, new_v), outs):
        for n, val in zip(tiny, _unpack(o.reshape(-1), shapes)):
            dst[n] = val
    return (loss, grad_x, *[grads[n] for n in _WEIGHTS], *[delta[n] for n in _WEIGHTS],
            *[new_m[n] for n in _WEIGHTS], *[new_v[n] for n in _WEIGHTS])
```

```python
import functools

import jax
import jax.numpy as jnp
from jax import lax
from jax.experimental import pallas as pl
from jax.experimental.pallas import tpu as pltpu

F32, BF16 = jnp.float32, jnp.bfloat16
D = 1024
N_META = 16
ROW0 = 128
META0 = ROW0 - N_META
EPS = 1e-6
LANES = 128
VMEM_LIMIT = 56 * 1024 * 1024

FOX_H, FOX_DH = 16, 64
FOX_INP = 4224
GLA_H, GLA_DK, GLA_DV, GLA_RANK = 4, 128, 256, 16
GLA_QK, GLA_V = 512, 1024
GLA_INP = 3200
GLA_NORM = 16.0
GDN_H, GDN_DK, GDN_DV = 8, 128, 128
GDN_CONV = 3072
GDN_INP = 4224
CHUNK = 64
D_FF = 2816
DEPTH = 4

ADAM_LR, ADAM_B1, ADAM_B2, ADAM_EPS, ADAM_WD, ADAM_STEP = 0.001, 0.9, 0.999, 1e-08, 0.01, 10

MESH = pl.DeviceIdType.MESH
ANY = pl.BlockSpec(memory_space=pl.ANY)
VM = pl.BlockSpec(memory_space=pltpu.VMEM)


def _params(sem=None, **kw):
    if sem is not None:
        kw["dimension_semantics"] = sem
    return pltpu.CompilerParams(vmem_limit_bytes=VMEM_LIMIT, **kw)


def _tile(n, cap, mult=LANES):
    best = None
    for t in range(mult, min(n, cap) + 1, mult):
        if n % t == 0:
            best = t
    return best if best is not None else n


def nn(a, b, **kw):
    return jnp.dot(a, b, preferred_element_type=F32, **kw)


def nt(a, b, **kw):
    return lax.dot_general(a, b, (((1,), (1,)), ((), ())), preferred_element_type=F32, **kw)


def tn(a, b, **kw):
    return lax.dot_general(a, b, (((0,), (0,)), ((), ())), preferred_element_type=F32, **kw)


def _split3(x):
    hi = x.astype(BF16)
    r = x - hi.astype(F32)
    mid = r.astype(BF16)
    lo = (r - mid.astype(F32)).astype(BF16)
    return hi, mid, lo


def _sel_l(sel, x):
    a, b, c = _split3(x)
    return nn(sel, a) + nn(sel, b) + nn(sel, c)


def _sel_r(x, sel):
    a, b, c = _split3(x)
    return nn(a, sel) + nn(b, sel) + nn(c, sel)


def _iota(shape, dim):
    return lax.broadcasted_iota(jnp.int32, shape, dim)


def _tri(n, upper=False, strict=False):
    i, j = _iota((n, n), 0), _iota((n, n), 1)
    if upper:
        m = (j > i) if strict else (j >= i)
    else:
        m = (j < i) if strict else (j <= i)
    return m


def _sigmoid(x):
    return 1.0 / (1.0 + jnp.exp(-x))


def _log_sigmoid(x):
    return jnp.minimum(x, 0.0) - jnp.log(1.0 + jnp.exp(-jnp.abs(x)))


def _softplus(x):
    return jnp.maximum(x, 0.0) + jnp.log(1.0 + jnp.exp(-jnp.abs(x)))


def _silu(x):
    return x * _sigmoid(x)


def _dsilu(x):
    s = _sigmoid(x)
    return s * (1.0 + x * (1.0 - s))


def _mm(a, b, *, ta=False, tb=False, add=None, out_dtype=F32, name):
    m, k = (a.shape[1], a.shape[0]) if ta else a.shape
    n = b.shape[0] if tb else b.shape[1]
    assert k == (b.shape[1] if tb else b.shape[0])
    tm, tn_, tk = _tile(m, 1408, LANES if ta else 16), _tile(n, 1408), _tile(k, 1408)
    nk = k // tk

    def body(*refs):
        if add is None:
            a_ref, b_ref, o_ref, acc = refs
        else:
            a_ref, b_ref, r_ref, o_ref, acc = refs
        kk = pl.program_id(2)

        @pl.when(kk == 0)
        def _():
            acc[...] = jnp.zeros_like(acc)

        av, bv = a_ref[...].astype(BF16), b_ref[...].astype(BF16)
        dims = (((0,) if ta else (1,), (1,) if tb else (0,)), ((), ()))
        acc[...] += lax.dot_general(av, bv, dims, preferred_element_type=F32)

        @pl.when(kk == nk - 1)
        def _():
            r = acc[...]
            if add is not None:
                r = r + r_ref[...].astype(F32)
            o_ref[...] = r.astype(out_dtype)

    a_spec = pl.BlockSpec((tk, tm), lambda i, j, q: (q, i)) if ta else pl.BlockSpec((tm, tk), lambda i, j, q: (i, q))
    b_spec = pl.BlockSpec((tn_, tk), lambda i, j, q: (j, q)) if tb else pl.BlockSpec((tk, tn_), lambda i, j, q: (q, j))
    o_spec = pl.BlockSpec((tm, tn_), lambda i, j, q: (i, j))
    ins, specs = [a, b], [a_spec, b_spec]
    if add is not None:
        ins.append(add)
        specs.append(o_spec)
    return pl.pallas_call(
        body, name=name, grid=(m // tm, n // tn_, nk), in_specs=specs, out_specs=o_spec,
        out_shape=jax.ShapeDtypeStruct((m, n), out_dtype),
        scratch_shapes=[pltpu.VMEM((tm, tn_), F32)],
        compiler_params=_params(("parallel", "parallel", "arbitrary")),
    )(*ins)


def _rms_fwd(h, g, *, name):
    lp = h.shape[0]
    tr = _tile(lp, 512)

    def body(h_ref, g_ref, y_ref):
        x = h_ref[...]
        r = lax.rsqrt(jnp.mean(x * x, axis=-1, keepdims=True) + EPS)
        y_ref[...] = (x * r * g_ref[...]).astype(BF16)

    return pl.pallas_call(
        body, name=name, grid=(lp // tr,),
        in_specs=[pl.BlockSpec((tr, D), lambda i: (i, 0)), pl.BlockSpec((1, D), lambda i: (0, 0))],
        out_specs=pl.BlockSpec((tr, D), lambda i: (i, 0)),
        out_shape=jax.ShapeDtypeStruct((lp, D), BF16), compiler_params=_params(("parallel",)),
    )(h, g.reshape(1, D))


def _rms_bwd(h, g, dy, dres, *, name):
    lp = h.shape[0]
    tr = _tile(lp, 512)

    def body(h_ref, g_ref, dy_ref, dr_ref, dh_ref, dg_ref):
        @pl.when(pl.program_id(0) == 0)
        def _():
            dg_ref[...] = jnp.zeros_like(dg_ref)

        x, dyv = h_ref[...], dy_ref[...].astype(F32)
        r = lax.rsqrt(jnp.mean(x * x, axis=-1, keepdims=True) + EPS)
        u = dyv * g_ref[...]
        dx = r * u - x * (r * r * r) * jnp.mean(x * u, axis=-1, keepdims=True)
        dh_ref[...] = dr_ref[...] + dx
        dg_ref[...] += jnp.sum(dyv * x * r, axis=0, keepdims=True)

    return pl.pallas_call(
        body, name=name, grid=(lp // tr,),
        in_specs=[pl.BlockSpec((tr, D), lambda i: (i, 0)), pl.BlockSpec((1, D), lambda i: (0, 0)),
                  pl.BlockSpec((tr, D), lambda i: (i, 0)), pl.BlockSpec((tr, D), lambda i: (i, 0))],
        out_specs=[pl.BlockSpec((tr, D), lambda i: (i, 0)), pl.BlockSpec((1, D), lambda i: (0, 0))],
        out_shape=[jax.ShapeDtypeStruct((lp, D), F32), jax.ShapeDtypeStruct((1, D), F32)],
        compiler_params=_params(("arbitrary",)),
    )(h, g.reshape(1, D), dy, dres)


GU_ROWS, DOWN_ROWS = 1408, 704
OFF_GU, OFF_DOWN = 0, DEPTH * GU_ROWS
FFN_ROWS = DEPTH * (GU_ROWS + DOWN_ROWS)
FFN_TM = 704


def _gu_spec(fn):
    return pl.BlockSpec((None, GU_ROWS, D), fn)


def _down_spec(fn):
    return pl.BlockSpec((None, DOWN_ROWS, D), fn)


def _down_pair(w0_ref, w1_ref):
    return jnp.concatenate([w0_ref[...], w1_ref[...]], axis=0)


def _ffn_up(yf, wpk, layer):
    lp = yf.shape[0]
    tm = _tile(lp, FFN_TM, 16)

    def body(y_ref, wg_ref, wu_ref, g_ref, u_ref, a_ref):
        y = y_ref[...]
        g, u = nt(y, wg_ref[...]), nt(y, wu_ref[...])
        g_ref[...] = g.astype(BF16)
        u_ref[...] = u.astype(BF16)
        a_ref[...] = (_silu(g) * u).astype(BF16)

    o = pl.BlockSpec((tm, GU_ROWS), lambda i, j: (i, j))
    return pl.pallas_call(
        body, name=f"ffn_up{layer}", grid=(lp // tm, 2),
        in_specs=[pl.BlockSpec((tm, D), lambda i, j: (i, 0)), _gu_spec(lambda i, j: (j, OFF_GU // GU_ROWS + layer, 0)),
                  _gu_spec(lambda i, j: (2 + j, OFF_GU // GU_ROWS + layer, 0))],
        out_specs=[o, o, o], out_shape=[jax.ShapeDtypeStruct((lp, D_FF), BF16)] * 3,
        compiler_params=_params(("parallel", "parallel")),
    )(yf, wpk, wpk)


def _ffn_down(act, wpk, layer, res):
    lp = act.shape[0]
    tm = _tile(lp, FFN_TM, 16)

    def body(a_ref, w0_ref, w1_ref, r_ref, o_ref, acc):
        kk = pl.program_id(1)

        @pl.when(kk == 0)
        def _():
            acc[...] = r_ref[...]

        acc[...] += nn(a_ref[...], _down_pair(w0_ref, w1_ref))

        @pl.when(kk == 1)
        def _():
            o_ref[...] = acc[...]

    o = pl.BlockSpec((tm, D), lambda i, kk: (i, 0))
    blk = OFF_DOWN // DOWN_ROWS + layer
    return pl.pallas_call(
        body, name=f"ffn_down{layer}", grid=(lp // tm, 2),
        in_specs=[pl.BlockSpec((tm, GU_ROWS), lambda i, kk: (i, kk)), _down_spec(lambda i, kk: (2 * kk, blk, 0)),
                  _down_spec(lambda i, kk: (2 * kk + 1, blk, 0)), o],
        out_specs=o, out_shape=jax.ShapeDtypeStruct((lp, D), F32), scratch_shapes=[pltpu.VMEM((tm, D), F32)],
        compiler_params=_params(("parallel", "arbitrary")),
    )(act, wpk, wpk, res)


def _ffn_dact(dh, wpk, layer, gate, up):
    lp = dh.shape[0]
    tm = _tile(lp, FFN_TM, 16)

    def body(d_ref, w0_ref, w1_ref, g_ref, u_ref, dg_ref, du_ref):
        da = nt(d_ref[...].astype(BF16), _down_pair(w0_ref, w1_ref))
        g, u = g_ref[...].astype(F32), u_ref[...].astype(F32)
        dg_ref[...] = (da * u * _dsilu(g)).astype(BF16)
        du_ref[...] = (da * _silu(g)).astype(BF16)

    o = pl.BlockSpec((tm, GU_ROWS), lambda i, j: (i, j))
    blk = OFF_DOWN // DOWN_ROWS + layer
    return pl.pallas_call(
        body, name=f"d_act{layer}", grid=(lp // tm, 2),
        in_specs=[pl.BlockSpec((tm, D), lambda i, j: (i, 0)), _down_spec(lambda i, j: (2 * j, blk, 0)),
                  _down_spec(lambda i, j: (2 * j + 1, blk, 0)), o, o],
        out_specs=[o, o], out_shape=[jax.ShapeDtypeStruct((lp, D_FF), BF16)] * 2,
        compiler_params=_params(("parallel", "parallel")),
    )(dh, wpk, wpk, gate, up)


def _ffn_dyf(dg, du, wpk, layer):
    lp = dg.shape[0]
    tm = _tile(lp, FFN_TM, 16)

    def body(dg_ref, du_ref, w_ref, o_ref, acc):
        kk = pl.program_id(1)

        @pl.when(kk == 0)
        def _():
            acc[...] = jnp.zeros_like(acc)

        @pl.when(kk < 2)
        def _():
            acc[...] += nn(dg_ref[...], w_ref[...])

        @pl.when(kk >= 2)
        def _():
            acc[...] += nn(du_ref[...], w_ref[...])

        @pl.when(kk == 3)
        def _():
            o_ref[...] = acc[...]

    return pl.pallas_call(
        body, name=f"d_yf{layer}", grid=(lp // tm, 4),
        in_specs=[pl.BlockSpec((tm, GU_ROWS), lambda i, kk: (i, jnp.minimum(kk, 1))),
                  pl.BlockSpec((tm, GU_ROWS), lambda i, kk: (i, jnp.maximum(kk - 2, 0))),
                  _gu_spec(lambda i, kk: (kk, OFF_GU // GU_ROWS + layer, 0))],
        out_specs=pl.BlockSpec((tm, D), lambda i, kk: (i, 0)), out_shape=jax.ShapeDtypeStruct((lp, D), F32),
        scratch_shapes=[pltpu.VMEM((tm, D), F32)], compiler_params=_params(("parallel", "arbitrary")),
    )(dg, du, wpk)


def _ffn_dw_down(act, dh, gpk, layer):
    lp = act.shape[0]
    tk = _tile(lp, 1408, 16)
    nk = lp // tk
    row = OFF_DOWN + layer * DOWN_ROWS

    def body(a_ref, d_ref, g_in, g_out, acc, stage, sems):
        jp, kk = pl.program_id(0), pl.program_id(1)

        @pl.when(kk == 0)
        def _():
            acc[...] = jnp.zeros_like(acc)

        acc[...] += tn(a_ref[...], d_ref[...].astype(BF16))

        @pl.when(kk == nk - 1)
        def _():
            stage[...] = acc[...].astype(BF16)
            copies = [pltpu.make_async_copy(stage.at[pl.ds(hf * DOWN_ROWS, DOWN_ROWS), :],
                                            g_out.at[2 * jp + hf, pl.ds(row, DOWN_ROWS), :], sems.at[hf]) for hf in range(2)]
            for cp in copies:
                cp.start()
            for cp in copies:
                cp.wait()

    return pl.pallas_call(
        body, name=f"d_w_down{layer}", grid=(2, nk),
        in_specs=[pl.BlockSpec((tk, GU_ROWS), lambda jp, kk: (kk, jp)), pl.BlockSpec((tk, D), lambda jp, kk: (kk, 0)), ANY],
        out_specs=ANY, out_shape=jax.ShapeDtypeStruct(gpk.shape, gpk.dtype),
        scratch_shapes=[pltpu.VMEM((GU_ROWS, D), F32), pltpu.VMEM((GU_ROWS, D), BF16), pltpu.SemaphoreType.DMA((2,))],
        input_output_aliases={2: 0}, compiler_params=_params(("arbitrary", "arbitrary")),
    )(act, dh, gpk)


def _ffn_dw_gu(dg, du, yf, gpk, layer):
    lp = dg.shape[0]
    tk = _tile(lp, 1408, 16)
    nk = lp // tk

    def body(dg_ref, du_ref, y_ref, g_in, o_ref, acc):
        c, kk = pl.program_id(0), pl.program_id(1)

        @pl.when(kk == 0)
        def _():
            acc[...] = jnp.zeros_like(acc)

        @pl.when(c < 2)
        def _():
            acc[...] += tn(dg_ref[...], y_ref[...])

        @pl.when(c >= 2)
        def _():
            acc[...] += tn(du_ref[...], y_ref[...])

        @pl.when(kk == nk - 1)
        def _():
            o_ref[...] = acc[...].astype(BF16)

    return pl.pallas_call(
        body, name=f"d_w_gate_up{layer}", grid=(4, nk),
        in_specs=[pl.BlockSpec((tk, GU_ROWS), lambda c, kk: (kk, jnp.minimum(c, 1))),
                  pl.BlockSpec((tk, GU_ROWS), lambda c, kk: (kk, jnp.maximum(c - 2, 0))),
                  pl.BlockSpec((tk, D), lambda c, kk: (kk, 0)), ANY],
        out_specs=_gu_spec(lambda c, kk: (c, OFF_GU // GU_ROWS + layer, 0)),
        out_shape=jax.ShapeDtypeStruct(gpk.shape, gpk.dtype),
        scratch_shapes=[pltpu.VMEM((GU_ROWS, D), F32)], input_output_aliases={3: 0},
        compiler_params=_params(("parallel", "arbitrary")),
    )(dg, du, yf, gpk)


def _loss_head(h, target):
    lp = h.shape[0]
    nb = lp // ROW0

    def body(h_ref, t_ref, dh_ref, l_ref):
        i = pl.program_id(0)

        @pl.when(i == 0)
        def _():
            l_ref[...] = jnp.zeros_like(l_ref)
            dh_ref[...] = jnp.zeros_like(dh_ref)

        @pl.when(i > 0)
        def _():
            err = h_ref[...] - t_ref[...]
            dh_ref[...] = err * (1.0 / D)
            l_ref[...] += jnp.sum(err * err) * (0.5 / D)

    return pl.pallas_call(
        body, name="loss_head", grid=(nb,),
        in_specs=[pl.BlockSpec((ROW0, D), lambda i: (i, 0)), pl.BlockSpec((ROW0, D), lambda i: (jnp.maximum(i - 1, 0), 0))],
        out_specs=[pl.BlockSpec((ROW0, D), lambda i: (i, 0)), pl.BlockSpec((8, LANES), lambda i: (0, 0))],
        out_shape=[jax.ShapeDtypeStruct((lp, D), F32), jax.ShapeDtypeStruct((8, LANES), F32)],
        compiler_params=_params(("arbitrary",)),
    )(h, target)


def _adamw(w, g, m, v, *, name):
    shape = w.shape
    c = shape[-1]
    r = w.size // c
    w2, g2, m2, v2 = (t.reshape(r, c) for t in (w, g, m, v))
    tr = _tile(r, max(8, (1 << 19) // c), 8)

    def body(w_ref, g_ref, m_ref, v_ref, d_ref, nm_ref, nv_ref):
        gv = g_ref[...]
        nm = ADAM_B1 * m_ref[...] + (1.0 - ADAM_B1) * gv
        nv = ADAM_B2 * v_ref[...] + (1.0 - ADAM_B2) * (gv * gv)
        m_hat = nm / (1.0 - ADAM_B1 ** ADAM_STEP)
        v_hat = nv / (1.0 - ADAM_B2 ** ADAM_STEP)
        d_ref[...] = -ADAM_LR * (m_hat / (jnp.sqrt(v_hat) + ADAM_EPS) + ADAM_WD * w_ref[...])
        nm_ref[...] = nm
        nv_ref[...] = nv

    spec = pl.BlockSpec((tr, c), lambda i: (i, 0))
    outs = pl.pallas_call(
        body, name=name, grid=(r // tr,), in_specs=[spec] * 4, out_specs=[spec] * 3,
        out_shape=[jax.ShapeDtypeStruct((r, c), F32)] * 3, compiler_params=_params(("parallel",)),
    )(w2, g2, m2, v2)
    return tuple(o.reshape(shape) for o in outs)


def _adam_math(w, g, m, v):
    nm = ADAM_B1 * m + (1.0 - ADAM_B1) * g
    nv = ADAM_B2 * v + (1.0 - ADAM_B2) * (g * g)
    m_hat = nm / (1.0 - ADAM_B1 ** ADAM_STEP)
    v_hat = nv / (1.0 - ADAM_B2 ** ADAM_STEP)
    return -ADAM_LR * (m_hat / (jnp.sqrt(v_hat) + ADAM_EPS) + ADAM_WD * w), nm, nv


def _adamw_packed(w, gred, m, v, *, row_off, transposed, name):
    nl, a, b = w.shape
    if transposed:
        ta = _tile(a, 256)
        wspec = pl.BlockSpec((1, ta, b), lambda l, r: (l, r, 0))
        gspec = pl.BlockSpec((b, ta), lambda l, r: (row_off // b + l, r))
        grid = (nl, a // ta)
    else:
        wspec = pl.BlockSpec((1, a, b), lambda l, r: (l, 0, 0))
        gspec = pl.BlockSpec((a, b), lambda l, r: (row_off // a + l, 0))
        grid = (nl, 1)

    def body(w_ref, g_ref, m_ref, v_ref, go_ref, d_ref, nm_ref, nv_ref):
        g = g_ref[...].T if transposed else g_ref[...]
        d, nm, nv = _adam_math(w_ref[0], g, m_ref[0], v_ref[0])
        go_ref[0], d_ref[0], nm_ref[0], nv_ref[0] = g, d, nm, nv

    return pl.pallas_call(
        body, name=name, grid=grid, in_specs=[wspec, gspec, wspec, wspec], out_specs=[wspec] * 4,
        out_shape=[jax.ShapeDtypeStruct(w.shape, F32)] * 4, compiler_params=_params(("parallel", "parallel")),
    )(w, gred, m, v)


FOX_AUG = FOX_H * LANES
L_C = 64
L_K = 67
L_LSE = 70
PAD_KEY = -30000.0
FOX_TQ = 384


def _head_sel(n_heads, width, lanes=LANES):
    r, c = _iota((n_heads * width, lanes), 0), _iota((n_heads * width, lanes), 1)
    down = (r // width == c).astype(BF16)
    r2, c2 = _iota((lanes, n_heads * width), 0), _iota((lanes, n_heads * width), 1)
    up = (c2 // width == r2).astype(BF16)
    return down, up


def _place(lane0):
    r, c = _iota((LANES, FOX_AUG), 0), _iota((LANES, FOX_AUG), 1)
    return [((c // LANES == r) & (c % LANES == lane0 + m)).astype(BF16) for m in range(3)]


def _placed(x, lane0):
    pcs = _split3(x)
    mats = _place(lane0)
    return nn(pcs[0], mats[0]) + nn(pcs[1], mats[1]) + nn(pcs[2], mats[2])


def _ones_at(rows, lanes):
    c = _iota((rows, FOX_AUG), 1) % LANES
    m = c == lanes[0]
    for l in lanes[1:]:
        m = m | (c == l)
    return m.astype(F32)


def _spread(x, extras, out_ref):
    rows = x.shape[0]
    left = _iota((rows, LANES), 1) < FOX_DH
    for p in range(FOX_H // 2):
        slab = x[:, p * LANES:(p + 1) * LANES]
        a = jnp.where(left, slab, extras[:, 2 * p * LANES:(2 * p + 1) * LANES])
        b = jnp.where(left, pltpu.roll(slab, FOX_DH, 1), extras[:, (2 * p + 1) * LANES:(2 * p + 2) * LANES])
        out_ref[:, 2 * p * LANES:(2 * p + 1) * LANES] = a.astype(BF16)
        out_ref[:, (2 * p + 1) * LANES:(2 * p + 2) * LANES] = b.astype(BF16)


def _fox_prep(proj, b_f, q_gain, k_gain):
    lp = proj.shape[0]
    nb = lp // LANES

    def body(p_ref, bf_ref, qg_ref, kg_ref, q_ref, k_ref, v_ref, carry):
        i = pl.program_id(0)

        @pl.when(i == 0)
        def _():
            carry[...] = jnp.zeros_like(carry)

        down, up = _head_sel(FOX_H, FOX_DH)

        def normed(x, gain):
            ms = _sel_r(x * x, down) * (1.0 / FOX_DH)
            r = _sel_r(lax.rsqrt(ms + EPS), up)
            return x * r * gain

        lane = _iota((LANES, LANES), 1)
        lf = jnp.where(lane < FOX_H, _log_sigmoid(p_ref[:, 4 * D:4 * D + LANES] + bf_ref[...]), 0.0)
        c = _sel_l(_tri(LANES).astype(BF16), lf) + carry[0:1, :]
        carry[...] = jnp.broadcast_to(c[LANES - 1:LANES, :], carry.shape)
        q_extra = _placed(c, L_C) + _ones_at(LANES, (L_K, L_K + 1, L_K + 2))
        row = i * LANES + _iota((LANES, FOX_AUG), 0)
        lane_a = _iota((LANES, FOX_AUG), 1) % LANES
        k_extra = -_placed(c, L_K) + _ones_at(LANES, (L_C, L_C + 1, L_C + 2, L_LSE, L_LSE + 1, L_LSE + 2))
        pad_val = jnp.where(lane_a == L_K, PAD_KEY, 0.0)
        k_extra = jnp.where((row < META0) & (lane_a >= L_K) & (lane_a < L_K + 3), pad_val, k_extra)
        v_extra = _ones_at(LANES, (L_C, L_C + 1, L_C + 2))
        _spread(normed(p_ref[:, 0:D], qg_ref[...]) * (FOX_DH ** -0.5), q_extra, q_ref)
        _spread(normed(p_ref[:, D:2 * D], kg_ref[...]), k_extra, k_ref)
        _spread(p_ref[:, 2 * D:3 * D], v_extra, v_ref)

    row = pl.BlockSpec((1, D), lambda i: (0, 0))
    aug = pl.BlockSpec((LANES, FOX_AUG), lambda i: (i, 0))
    return pl.pallas_call(
        body, name="fox_prep", grid=(nb,),
        in_specs=[pl.BlockSpec((LANES, FOX_INP), lambda i: (i, 0)), pl.BlockSpec((1, LANES), lambda i: (0, 0)), row, row],
        out_specs=[aug] * 3, out_shape=[jax.ShapeDtypeStruct((lp, FOX_AUG), BF16)] * 3,
        scratch_shapes=[pltpu.VMEM((8, LANES), F32)],
        compiler_params=_params(("arbitrary",)),
    )(proj, jnp.pad(b_f, (0, LANES - FOX_H)).reshape(1, LANES), jnp.tile(q_gain, FOX_H).reshape(1, D),
      jnp.tile(k_gain, FOX_H).reshape(1, D))


def _fox_attn_fwd(qa, ka, va, proj, ag=None):
    lp = qa.shape[0]
    tq = _tile(lp, FOX_TQ)
    nq = lp // tq
    npair = FOX_H // 2

    def body(q_ref, k_ref, v_ref, gate_ref, *rest):
        if ag is None:
            o_ref, og_ref, lse_ref = rest
        else:
            _, o_ref, og_ref, lse_ref, w_out, send_sems, recv_sems = rest
            copies = _AgCopies(w_out, ag[1], send_sems, recv_sems)

            @pl.when((pl.program_id(0) == 0) & (pl.program_id(1) == 0))
            def _():
                for r, k in copies.pairs():
                    copies.ici(r, k).start()

        i = pl.program_id(1)
        causal = _iota((tq, tq), 1) <= _iota((tq, tq), 0)
        qs = [q_ref[:, hh * LANES:(hh + 1) * LANES] for hh in range(2)]

        def block(j, carry, diag):
            off = pl.multiple_of(j * tq, tq)
            out = []
            for hh in range(2):
                m, acc = carry[hh]
                k = k_ref[pl.ds(off, tq), hh * LANES:(hh + 1) * LANES]
                v = v_ref[pl.ds(off, tq), hh * LANES:(hh + 1) * LANES]
                s = nt(qs[hh], k)
                if diag:
                    s = jnp.where(causal, s, -1e30)
                m2 = jnp.maximum(m, jnp.max(s, axis=-1, keepdims=True))
                p = jnp.exp(s - m2)
                p_hi = p.astype(BF16)
                p_lo = (p - p_hi.astype(F32)).astype(BF16)
                out.append((m2, jnp.exp(m - m2) * acc + nn(p_hi, v) + nn(p_lo, v)))
            return tuple(out)

        init = tuple((jnp.full((tq, 1), -1e30, F32), jnp.zeros((tq, LANES), F32)) for _ in range(2))
        carry = lax.fori_loop(0, i, lambda j, c: block(j, c, False), init)
        carry = block(i, carry, True)
        outs, lses = [], []
        for hh in range(2):
            m, acc = carry[hh]
            l = acc[:, L_C:L_C + 1]
            outs.append(acc / l)
            lses.append(jnp.broadcast_to(m + jnp.log(l), (tq, LANES)))
        left = _iota((tq, LANES), 1) < FOX_DH
        o = jnp.where(left, outs[0], pltpu.roll(outs[1], FOX_DH, 1))
        o_ref[...] = o
        og_ref[...] = (o * _sigmoid(gate_ref[...])).astype(BF16)
        lse_ref[...] = jnp.where(left, lses[0], lses[1])

        if ag is not None:
            @pl.when((pl.program_id(0) == npair - 1) & (pl.program_id(1) == nq - 1))
            def _():
                for r, k in copies.pairs():
                    copies.ici_arrival(r, k).wait_recv()
                for r, k in copies.pairs():
                    copies.ici(r, k).wait_send()

    qspec = pl.BlockSpec((tq, 2 * LANES), lambda p, i: (i, p))
    kspec = pl.BlockSpec((lp, 2 * LANES), lambda p, i: (0, p))
    ospec = pl.BlockSpec((tq, LANES), lambda p, i: (i, p))
    ins, in_specs = [qa, ka, va, proj], [qspec, kspec, kspec, pl.BlockSpec((tq, LANES), lambda p, i: (i, 3 * D // LANES + p))]
    out_specs = [ospec] * 3
    out_shape = [jax.ShapeDtypeStruct((lp, D), F32), jax.ShapeDtypeStruct((lp, D), BF16), jax.ShapeDtypeStruct((lp, D), F32)]
    if ag is None:
        return pl.pallas_call(body, name="fox_attn_fwd", grid=(npair, nq), in_specs=in_specs, out_specs=out_specs,
                              out_shape=out_shape, compiler_params=_params(("parallel", "arbitrary")))(*ins)
    n = 3 * len(ag[1])
    return pl.pallas_call(
        body, name="fox_attn_fwd_ag", grid=(npair, nq), in_specs=in_specs + [ANY], out_specs=out_specs + [ANY],
        out_shape=out_shape + [jax.ShapeDtypeStruct(ag[0].shape, ag[0].dtype)],
        scratch_shapes=[pltpu.SemaphoreType.DMA((n,))] * 2, input_output_aliases={4: 3},
        compiler_params=_params(("arbitrary", "arbitrary")),
    )(*ins, ag[0])


def _fox_gate_bwd(dog, o, proj, lse, qa):
    lp = o.shape[0]
    tr = LANES

    def body(d_ref, o_ref, g_ref, lse_ref, q_ref, do_ref, q2_ref, dgate_ref):
        down, _ = _head_sel(FOX_H, FOX_DH)
        sg = _sigmoid(g_ref[...])
        dv, ov = d_ref[...], o_ref[...]
        do = (dv * sg).astype(BF16).astype(F32)
        dgate_ref[...] = dv * ov * sg * (1.0 - sg)
        delta = _sel_r(do * ov, down)
        _spread(do, -_placed(delta, L_C), do_ref)
        r_, c_ = _iota((D, LANES), 0), _iota((D, LANES), 1)
        lse_c = _sel_r(lse_ref[...], (r_ == c_ * FOX_DH).astype(BF16))
        q2_ref[...] = (q_ref[...].astype(F32) - _placed(lse_c, L_LSE)).astype(BF16)

    spec = pl.BlockSpec((tr, D), lambda i: (i, 0))
    aug = pl.BlockSpec((tr, FOX_AUG), lambda i: (i, 0))
    return pl.pallas_call(
        body, name="fox_gate_bwd", grid=(lp // tr,),
        in_specs=[spec, spec, pl.BlockSpec((tr, D), lambda i: (i, 3)), spec, aug], out_specs=[aug, aug, spec],
        out_shape=[jax.ShapeDtypeStruct((lp, FOX_AUG), BF16), jax.ShapeDtypeStruct((lp, FOX_AUG), BF16),
                   jax.ShapeDtypeStruct((lp, D), F32)],
        compiler_params=_params(("parallel",)),
    )(dog, o, proj, lse, qa)


def _fox_attn_bwd(q2, ka, va, doa):
    lp = q2.shape[0]
    t = _tile(lp, FOX_TQ)
    nb = lp // t

    def body(q_ref, k_ref, v_ref, do_ref, dq_ref, dk_ref, dv_ref, dc_ref, dq_acc, dk_acc, dv_acc, dc_acc):
        j = pl.program_id(1)

        @pl.when(j == 0)
        def _():
            dq_acc[...] = jnp.zeros_like(dq_acc)

        causal = _iota((t, t), 1) <= _iota((t, t), 0)
        ks = [k_ref[:, hh * LANES:(hh + 1) * LANES] for hh in range(2)]
        vs = [v_ref[:, hh * LANES:(hh + 1) * LANES] for hh in range(2)]
        dk_acc[...] = jnp.zeros_like(dk_acc)
        dv_acc[...] = jnp.zeros_like(dv_acc)
        dc_acc[...] = jnp.zeros_like(dc_acc)

        def block(i, diag):
            off = pl.multiple_of(i * t, t)
            for hh in range(2):
                q = q_ref[pl.ds(off, t), hh * LANES:(hh + 1) * LANES]
                do = do_ref[pl.ds(off, t), hh * LANES:(hh + 1) * LANES]
                s = nt(q, ks[hh])
                if diag:
                    s = jnp.where(causal, s, -1e30)
                p = jnp.exp(s)
                ds = p * nt(do, vs[hh])
                dc_acc[hh] += jnp.sum(ds, axis=0, keepdims=True)
                dsb = ds.astype(BF16)
                dv_acc[hh] += tn(p.astype(BF16), do)
                dk_acc[hh] += tn(dsb, q)
                dq_acc[hh, pl.ds(off, t), :] += nn(dsb, ks[hh])

        block(j, True)

        def step(i, c):
            block(i, False)
            return c

        lax.fori_loop(j + 1, nb, step, 0)
        left = _iota((t, LANES), 1) < FOX_DH
        dk_ref[...] = jnp.where(left, dk_acc[0], pltpu.roll(dk_acc[1], FOX_DH, 1))
        dv_ref[...] = jnp.where(left, dv_acc[0], pltpu.roll(dv_acc[1], FOX_DH, 1))
        for hh in range(2):
            dc_ref[hh] = jnp.broadcast_to(-dc_acc[hh], (8, t))

        @pl.when(j == nb - 1)
        def _():
            left = _iota((lp, LANES), 1) < FOX_DH
            dq_ref[...] = jnp.where(left, dq_acc[0], pltpu.roll(dq_acc[1], FOX_DH, 1))

    full = pl.BlockSpec((lp, 2 * LANES), lambda p, j: (0, p))
    kblk = pl.BlockSpec((t, 2 * LANES), lambda p, j: (j, p))
    oblk = pl.BlockSpec((t, LANES), lambda p, j: (j, p))
    return pl.pallas_call(
        body, name="fox_attn_bwd", grid=(FOX_H // 2, nb),
        in_specs=[full, kblk, kblk, full],
        out_specs=[pl.BlockSpec((lp, LANES), lambda p, j: (0, p)), oblk, oblk, pl.BlockSpec((2, 8, t), lambda p, j: (p, 0, j))],
        out_shape=[jax.ShapeDtypeStruct((lp, D), F32)] * 3 + [jax.ShapeDtypeStruct((FOX_H, 8, lp), F32)],
        scratch_shapes=[pltpu.VMEM((2, lp, LANES), F32), pltpu.VMEM((2, t, LANES), F32), pltpu.VMEM((2, t, LANES), F32),
                        pltpu.VMEM((2, 1, t), F32)],
        compiler_params=_params(("parallel", "arbitrary")),
    )(q2, ka, va, doa)


def _fox_prep_bwd(proj, b_f, q_gain, k_gain, dqn, dkn, dv, dgate, dct):
    lp = proj.shape[0]
    nb = lp // LANES

    def body(p_ref, bf_ref, qg_ref, kg_ref, dq_ref, dk_ref, dv_ref, dg_ref, dc_ref,
             dp_ref, dqg_ref, dkg_ref, dbf_ref, carry):
        i = pl.program_id(0)

        @pl.when(i == 0)
        def _():
            carry[...] = jnp.zeros_like(carry)
            dqg_ref[...] = jnp.zeros_like(dqg_ref)
            dkg_ref[...] = jnp.zeros_like(dkg_ref)
            dbf_ref[...] = jnp.zeros_like(dbf_ref)

        down, up = _head_sel(FOX_H, FOX_DH)

        def norm_bwd(x, gain, dy, scale, dgain_ref):
            ms = _sel_r(x * x, down) * (1.0 / FOX_DH)
            r = _sel_r(lax.rsqrt(ms + EPS), up)
            u = dy * gain * scale
            mean_xu = _sel_r(_sel_r(x * u, down) * (1.0 / FOX_DH), up)
            dgain_ref[...] += jnp.sum(dy * scale * x * r, axis=0, keepdims=True)
            return r * u - x * (r * r * r) * mean_xu

        dp_ref[:, 0:D] = norm_bwd(p_ref[:, 0:D], qg_ref[...], dq_ref[...], FOX_DH ** -0.5, dqg_ref).astype(BF16)
        dp_ref[:, D:2 * D] = norm_bwd(p_ref[:, D:2 * D], kg_ref[...], dk_ref[...], 1.0, dkg_ref).astype(BF16)
        dp_ref[:, 2 * D:3 * D] = dv_ref[...].astype(BF16)
        dp_ref[:, 3 * D:4 * D] = dg_ref[...].astype(BF16)
        rows = jnp.concatenate([dc_ref[h, 0:1, :] for h in range(FOX_H)] + [jnp.zeros((LANES - FOX_H, LANES), F32)], axis=0)
        dlf = _sel_l(_tri(LANES, upper=True).astype(BF16), rows.T) + carry[0:1, :]
        carry[...] = jnp.broadcast_to(dlf[0:1, :], carry.shape)
        lane = _iota((LANES, LANES), 1)
        z = p_ref[:, 4 * D:4 * D + LANES] + bf_ref[...]
        df = jnp.where(lane < FOX_H, dlf * _sigmoid(-z), 0.0)
        dp_ref[:, 4 * D:4 * D + LANES] = df.astype(BF16)
        dbf_ref[...] += jnp.sum(df, axis=0, keepdims=True)

    rev = lambda i: (nb - 1 - i, 0)
    blk = pl.BlockSpec((LANES, D), rev)
    row = pl.BlockSpec((1, D), lambda i: (0, 0))
    row128 = pl.BlockSpec((1, LANES), lambda i: (0, 0))
    return pl.pallas_call(
        body, name="fox_prep_bwd", grid=(nb,),
        in_specs=[pl.BlockSpec((LANES, FOX_INP), rev), row128, row, row, blk, blk, blk, blk,
                  pl.BlockSpec((FOX_H, 8, LANES), lambda i: (0, 0, nb - 1 - i))],
        out_specs=[pl.BlockSpec((LANES, FOX_INP), rev), row, row, row128],
        out_shape=[jax.ShapeDtypeStruct((lp, FOX_INP), BF16), jax.ShapeDtypeStruct((1, D), F32),
                   jax.ShapeDtypeStruct((1, D), F32), jax.ShapeDtypeStruct((1, LANES), F32)],
        scratch_shapes=[pltpu.VMEM((8, LANES), F32)],
        compiler_params=_params(("arbitrary",)),
    )(proj, jnp.pad(b_f, (0, LANES - FOX_H)).reshape(1, LANES), jnp.tile(q_gain, FOX_H).reshape(1, D),
      jnp.tile(k_gain, FOX_H).reshape(1, D), dqn, dkn, dv, dgate, dct)


def _gla_gates(p_ref, wa_ref, ba_ref):
    a_lr = p_ref[:, 3072:3072 + LANES]
    z = nn(a_lr.astype(BF16), wa_ref[...].astype(BF16)) + ba_ref[...]
    g = _log_sigmoid(z) * (1.0 / GLA_NORM)
    b = _sel_l(_tri(CHUNK).astype(BF16), g)
    return a_lr, z, b


def _gla_head_fwd(q, k, v, b, st0):
    eb = jnp.exp(b)
    bl = b[CHUNK - 1:CHUNK, :]
    qe, ke, kd = q * eb, k * jnp.exp(-b), k * jnp.exp(bl - b)
    a = jnp.where(_tri(CHUNK), nt(qe, ke), 0.0)
    o = nn(a, v) + nt(qe, st0)
    st1 = st0 * jnp.exp(bl) + tn(v, kd)
    return o, st1, (qe, ke, kd, a, bl)


def _gla_fwd(proj, w_alpha2, b_alpha, o_gain):
    lp = proj.shape[0]
    nc = lp // CHUNK

    def body(p_ref, wa_ref, ba_ref, og_ref, o_ref, y_ref, s_ref, st):
        @pl.when(pl.program_id(0) == 0)
        def _():
            st[...] = jnp.zeros_like(st)

        _, _, b = _gla_gates(p_ref, wa_ref, ba_ref)
        for h in range(GLA_H):
            q = p_ref[:, h * GLA_DK:(h + 1) * GLA_DK] * (GLA_DK ** -0.5)
            k = p_ref[:, GLA_QK + h * GLA_DK:GLA_QK + (h + 1) * GLA_DK]
            v = p_ref[:, 2 * GLA_QK + h * GLA_DV:2 * GLA_QK + (h + 1) * GLA_DV]
            r = p_ref[:, 2 * GLA_QK + GLA_V + h * GLA_DV:2 * GLA_QK + GLA_V + (h + 1) * GLA_DV]
            st0 = st[h]
            s_ref[0, h] = st0
            o, st1, _ = _gla_head_fwd(q, k, v, b[:, h * GLA_DK:(h + 1) * GLA_DK], st0)
            st[h] = st1
            o_ref[:, h * GLA_DV:(h + 1) * GLA_DV] = o
            rs = lax.rsqrt(jnp.mean(o * o, axis=-1, keepdims=True) + EPS)
            y_ref[:, h * GLA_DV:(h + 1) * GLA_DV] = (o * rs * og_ref[...] * _silu(r)).astype(BF16)

    blk = pl.BlockSpec((CHUNK, D), lambda i: (i, 0))
    return pl.pallas_call(
        body, name="gla_fwd", grid=(nc,),
        in_specs=[pl.BlockSpec((CHUNK, GLA_INP), lambda i: (i, 0)), pl.BlockSpec((LANES, GLA_QK), lambda i: (0, 0)),
                  pl.BlockSpec((1, GLA_QK), lambda i: (0, 0)), pl.BlockSpec((1, GLA_DV), lambda i: (0, 0))],
        out_specs=[blk, blk, pl.BlockSpec((1, GLA_H, GLA_DV, GLA_DK), lambda i: (i, 0, 0, 0))],
        out_shape=[jax.ShapeDtypeStruct((lp, D), F32), jax.ShapeDtypeStruct((lp, D), BF16),
                   jax.ShapeDtypeStruct((nc, GLA_H, GLA_DV, GLA_DK), F32)],
        scratch_shapes=[pltpu.VMEM((GLA_H, GLA_DV, GLA_DK), F32)],
        compiler_params=_params(("arbitrary",)),
    )(proj, jnp.pad(w_alpha2, ((0, LANES - GLA_RANK), (0, 0))), b_alpha.reshape(1, GLA_QK), o_gain.reshape(1, GLA_DV))


def _gla_bwd(proj, w_alpha2, b_alpha, o_gain, o, states, dy):
    lp = proj.shape[0]
    nc = lp // CHUNK

    def body(p_ref, wa_ref, ba_ref, og_ref, o_ref, s_ref, dy_ref, dp_ref, dwa_ref, dba_ref, dog_ref, dst):
        @pl.when(pl.program_id(0) == 0)
        def _():
            dst[...] = jnp.zeros_like(dst)
            dwa_ref[...] = jnp.zeros_like(dwa_ref)
            dba_ref[...] = jnp.zeros_like(dba_ref)
            dog_ref[...] = jnp.zeros_like(dog_ref)

        a_lr, z, b_all = _gla_gates(p_ref, wa_ref, ba_ref)
        last_row = _iota((CHUNK, GLA_DK), 0) == CHUNK - 1
        rev = _tri(CHUNK, upper=True).astype(BF16)
        dg_parts = []
        for h in range(GLA_H):
            scale = GLA_DK ** -0.5
            q = p_ref[:, h * GLA_DK:(h + 1) * GLA_DK] * scale
            k = p_ref[:, GLA_QK + h * GLA_DK:GLA_QK + (h + 1) * GLA_DK]
            v = p_ref[:, 2 * GLA_QK + h * GLA_DV:2 * GLA_QK + (h + 1) * GLA_DV]
            r = p_ref[:, 2 * GLA_QK + GLA_V + h * GLA_DV:2 * GLA_QK + GLA_V + (h + 1) * GLA_DV]
            b = b_all[:, h * GLA_DK:(h + 1) * GLA_DK]
            st0 = s_ref[0, h]
            dst1 = dst[h]
            ov = o_ref[:, h * GLA_DV:(h + 1) * GLA_DV]
            dyv = dy_ref[:, h * GLA_DV:(h + 1) * GLA_DV]
            rs = lax.rsqrt(jnp.mean(ov * ov, axis=-1, keepdims=True) + EPS)
            on = ov * rs
            dr = dyv * on * og_ref[...] * _dsilu(r)
            don = dyv * _silu(r)
            dog_ref[...] += jnp.sum(don * on, axis=0, keepdims=True)
            u = don * og_ref[...]
            do = rs * u - ov * (rs * rs * rs) * jnp.mean(ov * u, axis=-1, keepdims=True)
            eb = jnp.exp(b)
            _, _, (qe, ke, kd, a, bl) = _gla_head_fwd(q, k, v, b, st0)
            da = jnp.where(_tri(CHUNK), nt(do, v), 0.0)
            dkd = nn(v, dst1)
            dvv = tn(a, do) + nt(kd, dst1)
            dqe = nn(da, ke) + nn(do, st0)
            dke = tn(da, qe)
            ebl = jnp.exp(bl)
            dst[h] = dst1 * ebl + tn(do, qe)
            db = dqe * qe - dke * ke - dkd * kd
            db_last = jnp.sum(dkd * kd, axis=0, keepdims=True) + jnp.sum(dst1 * st0, axis=0, keepdims=True) * ebl
            db = db + jnp.where(last_row, db_last, 0.0)
            dg_parts.append(_sel_l(rev, db))
            dp_ref[:, h * GLA_DK:(h + 1) * GLA_DK] = (dqe * eb * scale).astype(BF16)
            dp_ref[:, GLA_QK + h * GLA_DK:GLA_QK + (h + 1) * GLA_DK] = (dke * jnp.exp(-b) + dkd * jnp.exp(bl - b)).astype(BF16)
            dp_ref[:, 2 * GLA_QK + h * GLA_DV:2 * GLA_QK + (h + 1) * GLA_DV] = dvv.astype(BF16)
            dp_ref[:, 2 * GLA_QK + GLA_V + h * GLA_DV:2 * GLA_QK + GLA_V + (h + 1) * GLA_DV] = dr.astype(BF16)
        dg = jnp.concatenate(dg_parts, axis=1)
        dz = dg * (1.0 / GLA_NORM) * _sigmoid(-z)
        dzb = dz.astype(BF16)
        dp_ref[:, 3072:3072 + LANES] = nt(dzb, wa_ref[...].astype(BF16)).astype(BF16)
        dwa_ref[...] += tn(a_lr.astype(BF16), dzb)
        dba_ref[...] += jnp.sum(dz, axis=0, keepdims=True)

    rv = lambda i: (nc - 1 - i, 0)
    blk = pl.BlockSpec((CHUNK, D), rv)
    fixed = lambda r, c: pl.BlockSpec((r, c), lambda i: (0, 0))
    return pl.pallas_call(
        body, name="gla_bwd", grid=(nc,),
        in_specs=[pl.BlockSpec((CHUNK, GLA_INP), rv), fixed(LANES, GLA_QK), fixed(1, GLA_QK), fixed(1, GLA_DV), blk,
                  pl.BlockSpec((1, GLA_H, GLA_DV, GLA_DK), lambda i: (nc - 1 - i, 0, 0, 0)), blk],
        out_specs=[pl.BlockSpec((CHUNK, GLA_INP), rv), fixed(LANES, GLA_QK), fixed(1, GLA_QK), fixed(1, GLA_DV)],
        out_shape=[jax.ShapeDtypeStruct((lp, GLA_INP), BF16), jax.ShapeDtypeStruct((LANES, GLA_QK), F32),
                   jax.ShapeDtypeStruct((1, GLA_QK), F32), jax.ShapeDtypeStruct((1, GLA_DV), F32)],
        scratch_shapes=[pltpu.VMEM((GLA_H, GLA_DV, GLA_DK), F32)],
        compiler_params=_params(("arbitrary",)),
    )(proj, jnp.pad(w_alpha2, ((0, LANES - GLA_RANK), (0, 0))), b_alpha.reshape(1, GLA_QK), o_gain.reshape(1, GLA_DV),
      o, states, dy)


HI = lax.Precision.HIGHEST


def _gdn_pre(prev_ref, p_ref, cw_ref, al_ref, dt_ref):
    xc = jnp.concatenate([prev_ref[:, 0:GDN_CONV], p_ref[:, 0:GDN_CONV]], axis=0)
    shifted = [pltpu.roll(xc, 3 - j, 0)[CHUNK:, :] if j < 3 else xc[CHUNK:, :] for j in range(4)]
    conv = sum(shifted[j] * cw_ref[j:j + 1, :] for j in range(4))
    act = _silu(conv)
    slab = p_ref[:, 4096:4096 + LANES]
    lane = _iota((CHUNK, LANES), 1)
    zs = slab + dt_ref[...]
    g = jnp.where(lane < GDN_H, -jnp.exp(al_ref[...]) * _softplus(zs), 0.0)
    bs = _sel_l(_tri(CHUNK).astype(BF16), g)
    beta = _sigmoid(slab)
    return shifted, conv, act, slab, zs, g, bs, beta


def _l2n(x):
    r = lax.rsqrt(jnp.sum(x * x, axis=-1, keepdims=True) + EPS)
    return x * r, r


def _gdn_chunk_fwd(q, k, v, beta, bcol, brow, s0):
    hs = range(len(q))
    ii, jj = _iota((CHUNK, CHUNK), 0), _iota((CHUNK, CHUNK), 1)
    low, eye = ii >= jj, (ii == jj).astype(F32)
    dm = [jnp.where(low, jnp.exp(jnp.where(low, bcol[h] - brow[h], 0.0)), 0.0) for h in hs]
    dstrict = [jnp.where(ii > jj, dm[h], 0.0) for h in hs]
    eb = [jnp.exp(bcol[h]) for h in hs]
    bl = [bcol[h][CHUNK - 1:CHUNK, :] for h in hs]
    kb = [k[h] * beta[h] for h in hs]
    vb = [v[h] * beta[h] for h in hs]
    nmat = [nt(kb[h], k[h]) * dstrict[h] for h in hs]
    x = [eye - nmat[h] for h in hs]
    pw = [nn(nmat[h], nmat[h], precision=HI) for h in hs]
    for it in range(5):
        x = [x[h] + nn(x[h], pw[h], precision=HI) for h in hs]
        if it < 4:
            pw = [nn(pw[h], pw[h], precision=HI) for h in hs]
    kbe = [kb[h] * eb[h] for h in hs]
    u = [nn(x[h], vb[h], precision=HI) for h in hs]
    w = [nn(x[h], kbe[h], precision=HI) for h in hs]
    vn = [u[h] - nn(w[h], s0[h]) for h in hs]
    pm = [nt(q[h], k[h]) * dm[h] for h in hs]
    qe = [q[h] * eb[h] for h in hs]
    o = [nn(pm[h], vn[h]) + nn(qe[h], s0[h]) for h in hs]
    kd = [k[h] * jnp.exp(bl[h] - bcol[h]) for h in hs]
    s1 = [s0[h] * jnp.exp(bl[h]) + tn(kd[h], vn[h]) for h in hs]
    return o, s1, dict(dm=dm, dstrict=dstrict, eb=eb, bl=bl, kb=kb, vb=vb, nmat=nmat, tinv=x, kbe=kbe, u=u, w=w, vn=vn,
                       pm=pm, qe=qe, kd=kd)


def _gdn_heads(act, beta_slab, bs, h):
    qa = act[:, h * GDN_DK:(h + 1) * GDN_DK]
    ka = act[:, GDN_H * GDN_DK + h * GDN_DK:GDN_H * GDN_DK + (h + 1) * GDN_DK]
    v = act[:, 2 * GDN_H * GDN_DK + h * GDN_DV:2 * GDN_H * GDN_DK + (h + 1) * GDN_DV]
    return qa, ka, v, beta_slab[:, GDN_H + h:GDN_H + h + 1], bs[:, h:h + 1]


def _gdn_fwd(proj, conv_w, a_log, dt_bias, o_gain):
    lp = proj.shape[0]
    nc = lp // CHUNK

    def body(prev_ref, p_ref, cw_ref, al_ref, dt_ref, og_ref, o_ref, y_ref, s_ref, st):
        @pl.when(pl.program_id(0) == 0)
        def _():
            st[...] = jnp.zeros_like(st)

        _, _, act, _, _, _, bs, beta = _gdn_pre(prev_ref, p_ref, cw_ref, al_ref, dt_ref)
        bst = bs.T
        hs = range(GDN_H)
        parts = [_gdn_heads(act, beta, bs, h) for h in hs]
        q = [_l2n(parts[h][0])[0] * (GDN_DK ** -0.5) for h in hs]
        k = [_l2n(parts[h][1])[0] for h in hs]
        s0 = [st[h] for h in hs]
        for h in hs:
            s_ref[0, h] = s0[h]
        o, s1, _ = _gdn_chunk_fwd(q, k, [parts[h][2] for h in hs], [parts[h][3] for h in hs], [parts[h][4] for h in hs],
                                  [bst[h:h + 1, :] for h in hs], s0)
        for h in hs:
            st[h] = s1[h]
            o_ref[:, h * GDN_DV:(h + 1) * GDN_DV] = o[h]
            rs = lax.rsqrt(jnp.mean(o[h] * o[h], axis=-1, keepdims=True) + EPS)
            gate = p_ref[:, GDN_CONV + h * GDN_DV:GDN_CONV + (h + 1) * GDN_DV]
            y_ref[:, h * GDN_DV:(h + 1) * GDN_DV] = (o[h] * rs * og_ref[...] * _silu(gate)).astype(BF16)

    blk = pl.BlockSpec((CHUNK, D), lambda i: (i, 0))
    fixed = lambda r, c: pl.BlockSpec((r, c), lambda i: (0, 0))
    return pl.pallas_call(
        body, name="gdn_fwd", grid=(nc,),
        in_specs=[pl.BlockSpec((CHUNK, GDN_INP), lambda i: (jnp.maximum(i - 1, 0), 0)),
                  pl.BlockSpec((CHUNK, GDN_INP), lambda i: (i, 0)), fixed(8, GDN_CONV), fixed(1, LANES), fixed(1, LANES),
                  fixed(1, GDN_DV)],
        out_specs=[blk, blk, pl.BlockSpec((1, GDN_H, GDN_DK, GDN_DV), lambda i: (i, 0, 0, 0))],
        out_shape=[jax.ShapeDtypeStruct((lp, D), F32), jax.ShapeDtypeStruct((lp, D), BF16),
                   jax.ShapeDtypeStruct((nc, GDN_H, GDN_DK, GDN_DV), F32)],
        scratch_shapes=[pltpu.VMEM((GDN_H, GDN_DK, GDN_DV), F32)],
        compiler_params=_params(("arbitrary",)),
    )(proj, proj, jnp.pad(conv_w.reshape(4, GDN_CONV), ((0, 4), (0, 0))), jnp.pad(a_log, (0, LANES - GDN_H)).reshape(1, LANES),
      jnp.pad(dt_bias, (0, LANES - GDN_H)).reshape(1, LANES), o_gain.reshape(1, GDN_DV))


def _gdn_bwd(proj, conv_w, a_log, dt_bias, o_gain, o, states, dy):
    lp = proj.shape[0]
    nc = lp // CHUNK

    def body(prev_ref, p_ref, cw_ref, al_ref, dt_ref, og_ref, o_ref, s_ref, dy_ref,
             dp_ref, dcw_ref, dal_ref, ddt_ref, dog_ref, dst, dconv_next):
        @pl.when(pl.program_id(0) == 0)
        def _():
            dst[...] = jnp.zeros_like(dst)
            dconv_next[...] = jnp.zeros_like(dconv_next)
            dcw_ref[...] = jnp.zeros_like(dcw_ref)
            dal_ref[...] = jnp.zeros_like(dal_ref)
            ddt_ref[...] = jnp.zeros_like(ddt_ref)
            dog_ref[...] = jnp.zeros_like(dog_ref)

        shifted, conv, act, slab, zs, g, bs, beta = _gdn_pre(prev_ref, p_ref, cw_ref, al_ref, dt_ref)
        bst = bs.T
        lane = _iota((CHUNK, LANES), 1)
        ones = jnp.ones((CHUNK, LANES), F32)
        db_slab = jnp.zeros((CHUNK, LANES), F32)
        dbeta_slab = jnp.zeros((CHUNK, LANES), F32)
        last_row = _iota((CHUNK, 1), 0) == CHUNK - 1
        hs = range(GDN_H)
        scale = GDN_DK ** -0.5
        parts = [_gdn_heads(act, beta, bs, h) for h in hs]
        qa, ka, v = [parts[h][0] for h in hs], [parts[h][1] for h in hs], [parts[h][2] for h in hs]
        bet, bcol = [parts[h][3] for h in hs], [parts[h][4] for h in hs]
        qn_ = [_l2n(qa[h]) for h in hs]
        kn_ = [_l2n(ka[h]) for h in hs]
        q = [qn_[h][0] * scale for h in hs]
        k, rq, rk = [kn_[h][0] for h in hs], [qn_[h][1] for h in hs], [kn_[h][1] for h in hs]
        s0 = [s_ref[0, h] for h in hs]
        ds1 = [dst[h] for h in hs]
        do = []
        for h in hs:
            ov = o_ref[:, h * GDN_DV:(h + 1) * GDN_DV]
            dyv = dy_ref[:, h * GDN_DV:(h + 1) * GDN_DV]
            gate = p_ref[:, GDN_CONV + h * GDN_DV:GDN_CONV + (h + 1) * GDN_DV]
            rs = lax.rsqrt(jnp.mean(ov * ov, axis=-1, keepdims=True) + EPS)
            on = ov * rs
            dp_ref[:, GDN_CONV + h * GDN_DV:GDN_CONV + (h + 1) * GDN_DV] = (dyv * on * og_ref[...] * _dsilu(gate)).astype(BF16)
            don = dyv * _silu(gate)
            dog_ref[...] += jnp.sum(don * on, axis=0, keepdims=True)
            uu = don * og_ref[...]
            do.append(rs * uu - ov * (rs * rs * rs) * jnp.mean(ov * uu, axis=-1, keepdims=True))
        _, _, f = _gdn_chunk_fwd(q, k, v, bet, bcol, [bst[h:h + 1, :] for h in hs], s0)
        dm, dstrict, eb, bl, kb, nmat, tinv = f["dm"], f["dstrict"], f["eb"], f["bl"], f["kb"], f["nmat"], f["tinv"]
        kbe, u, w, vn, pm, qe, kd = f["kbe"], f["u"], f["w"], f["vn"], f["pm"], f["qe"], f["kd"]
        ebl = [jnp.exp(bl[h]) for h in hs]
        dvn = [tn(pm[h], do[h]) + nn(kd[h], ds1[h]) for h in hs]
        dpr = [nt(do[h], vn[h]) for h in hs]
        dqe = [nt(do[h], s0[h]) for h in hs]
        dkd = [nt(vn[h], ds1[h]) for h in hs]
        for h in hs:
            dst[h] = ds1[h] * ebl[h] + tn(qe[h], do[h]) - tn(w[h], dvn[h])
        du_ = [tn(tinv[h], dvn[h], precision=HI) for h in hs]
        dw_ = [tn(tinv[h], -nt(dvn[h], s0[h]), precision=HI) for h in hs]
        dn = [-(nt(du_[h], u[h]) + nt(dw_[h], w[h])) for h in hs]
        dqk = [dpr[h] * dm[h] for h in hs]
        dkk = [dn[h] * dstrict[h] for h in hs]
        gsum = [dpr[h] * pm[h] + dn[h] * nmat[h] for h in hs]
        dkb = [nn(dkk[h], k[h]) + dw_[h] * eb[h] for h in hs]
        dk = [tn(dkk[h], kb[h]) + tn(dqk[h], q[h]) + dkd[h] * jnp.exp(bl[h] - bcol[h]) + dkb[h] * bet[h] for h in hs]
        dq = [nn(dqk[h], k[h]) + dqe[h] * eb[h] for h in hs]
        colsum = [tn(gsum[h], ones, precision=HI)[:, 0:1] for h in hs]
        dact_q, dact_k, dact_v = [], [], []
        for h in hs:
            dbeta = jnp.sum(dkb[h] * k[h], axis=-1, keepdims=True) + jnp.sum(du_[h] * v[h], axis=-1, keepdims=True)
            skd = jnp.sum(dkd[h] * kd[h], axis=-1, keepdims=True)
            db = (jnp.sum(gsum[h], axis=-1, keepdims=True) - colsum[h] + jnp.sum(dqe[h] * qe[h], axis=-1, keepdims=True)
                  + jnp.sum(dw_[h] * kbe[h], axis=-1, keepdims=True) - skd)
            db_last = jnp.sum(skd, axis=0, keepdims=True) + jnp.sum(ds1[h] * s0[h]) * ebl[h]
            db = db + jnp.where(last_row, db_last, 0.0)
            db_slab = db_slab + jnp.where(lane == h, db, 0.0)
            dbeta_slab = dbeta_slab + jnp.where(lane == GDN_H + h, dbeta, 0.0)
            dqn = dq[h] * scale
            dact_q.append(rq[h] * dqn - qa[h] * (rq[h] * rq[h] * rq[h]) * jnp.sum(qa[h] * dqn, axis=-1, keepdims=True))
            dact_k.append(rk[h] * dk[h] - ka[h] * (rk[h] * rk[h] * rk[h]) * jnp.sum(ka[h] * dk[h], axis=-1, keepdims=True))
            dact_v.append(du_[h] * bet[h])
        dact = jnp.concatenate(dact_q + dact_k + dact_v, axis=1)
        dconv = dact * _dsilu(conv)
        for j in range(4):
            dcw_ref[j:j + 1, :] += jnp.sum(dconv * shifted[j], axis=0, keepdims=True)
        dcat = jnp.concatenate([dconv, dconv_next[...]], axis=0)
        dx = dconv * cw_ref[3:4, :]
        for j in range(3):
            dx = dx + pltpu.roll(dcat, 2 * CHUNK - (3 - j), 0)[:CHUNK, :] * cw_ref[j:j + 1, :]
        dconv_next[...] = dconv
        dp_ref[:, 0:GDN_CONV] = dx.astype(BF16)
        dg = _sel_l(_tri(CHUNK, upper=True).astype(BF16), db_slab)
        da = dg * (-jnp.exp(al_ref[...])) * _sigmoid(zs)
        da = jnp.where(lane < GDN_H, da, 0.0)
        dal_ref[...] += jnp.sum(dg * g, axis=0, keepdims=True)
        ddt_ref[...] += jnp.sum(da, axis=0, keepdims=True)
        dp_ref[:, 4096:4096 + LANES] = (da + dbeta_slab * beta * (1.0 - beta)).astype(BF16)

    rv = lambda i: (nc - 1 - i, 0)
    blk = pl.BlockSpec((CHUNK, D), rv)
    fixed = lambda r, c: pl.BlockSpec((r, c), lambda i: (0, 0))
    return pl.pallas_call(
        body, name="gdn_bwd", grid=(nc,),
        in_specs=[pl.BlockSpec((CHUNK, GDN_INP), lambda i: (jnp.maximum(nc - 2 - i, 0), 0)),
                  pl.BlockSpec((CHUNK, GDN_INP), rv), fixed(8, GDN_CONV), fixed(1, LANES), fixed(1, LANES), fixed(1, GDN_DV),
                  blk, pl.BlockSpec((1, GDN_H, GDN_DK, GDN_DV), lambda i: (nc - 1 - i, 0, 0, 0)), blk],
        out_specs=[pl.BlockSpec((CHUNK, GDN_INP), rv), fixed(8, GDN_CONV), fixed(1, LANES), fixed(1, LANES), fixed(1, GDN_DV)],
        out_shape=[jax.ShapeDtypeStruct((lp, GDN_INP), BF16), jax.ShapeDtypeStruct((8, GDN_CONV), F32),
                   jax.ShapeDtypeStruct((1, LANES), F32), jax.ShapeDtypeStruct((1, LANES), F32),
                   jax.ShapeDtypeStruct((1, GDN_DV), F32)],
        scratch_shapes=[pltpu.VMEM((GDN_H, GDN_DK, GDN_DV), F32), pltpu.VMEM((CHUNK, GDN_CONV), F32)],
        compiler_params=_params(("arbitrary",)),
    )(proj, proj, jnp.pad(conv_w.reshape(4, GDN_CONV), ((0, 4), (0, 0))), jnp.pad(a_log, (0, LANES - GDN_H)).reshape(1, LANES),
      jnp.pad(dt_bias, (0, LANES - GDN_H)).reshape(1, LANES), o_gain.reshape(1, GDN_DV), o, states, dy)


def _coords():
    return lax.axis_index("x"), lax.axis_index("y"), lax.axis_index("c")


def _other_chips(x, y):
    return [(1 - x, y, 2 * (1 - x) + y), (x, 1 - y, 2 * x + 1 - y), (1 - x, 1 - y, 2 * (1 - x) + 1 - y)]


def _gather8(v, *, reduce, name):
    r, c = v.shape

    def body(v_ref, out_ref, *scratch):
        if reduce:
            buf, send_sems, recv_sems = scratch
        else:
            buf = out_ref
            send_sems, recv_sems = scratch
        x, y, cc = _coords()
        me = 4 * x + 2 * y + cc
        buf[me] = v_ref[...]
        copies = []
        for k in range(1, 8):
            px, py, pc = x ^ (k >> 2), y ^ ((k >> 1) & 1), cc ^ (k & 1)
            copies.append(pltpu.make_async_remote_copy(
                src_ref=v_ref, dst_ref=buf.at[me], send_sem=send_sems.at[k - 1], recv_sem=recv_sems.at[k - 1],
                device_id=(px, py, pc), device_id_type=MESH))
        for cp in copies:
            cp.start()
        for k in range(1, 8):
            peer = (x ^ (k >> 2)) * 4 + (y ^ ((k >> 1) & 1)) * 2 + (cc ^ (k & 1))
            pltpu.make_async_remote_copy(
                src_ref=v_ref, dst_ref=buf.at[peer], send_sem=send_sems.at[k - 1], recv_sem=recv_sems.at[k - 1],
                device_id=(x, y, cc), device_id_type=MESH).wait_recv()
        for cp in copies:
            cp.wait_send()
        if reduce:
            acc = buf[0]
            for d in range(1, 8):
                acc = acc + buf[d]
            out_ref[...] = acc

    scratch = [pltpu.SemaphoreType.DMA((7,)), pltpu.SemaphoreType.DMA((7,))]
    if reduce:
        scratch = [pltpu.VMEM((8, r, c), F32)] + scratch
    return pl.pallas_call(
        body, name=name, in_specs=[VM], out_specs=VM,
        out_shape=jax.ShapeDtypeStruct((r, c) if reduce else (8, r, c), F32),
        scratch_shapes=scratch, compiler_params=_params(),
    )(v)


class _AgCopies:
    def __init__(self, buf, ranges, send_sems, recv_sems):
        self.buf, self.ranges, self.send_sems, self.recv_sems = buf, ranges, send_sems, recv_sems
        self.x, self.y, self.cc = _coords()
        self.p = 2 * self.x + self.y
        self.chips = _other_chips(self.x, self.y)

    def rows(self, chip, r, hf):
        start, n = self.ranges[r]
        return self.buf.at[chip, pl.ds(start + hf * (n // 2), n // 2), :]

    def _copy(self, r, k, chip, hf, to):
        return pltpu.make_async_remote_copy(
            src_ref=self.rows(chip, r, hf), dst_ref=self.rows(chip, r, hf), send_sem=self.send_sems.at[3 * r + k],
            recv_sem=self.recv_sems.at[3 * r + k], device_id=to, device_id_type=MESH)

    def pairs(self):
        return [(r, k) for r in range(len(self.ranges)) for k in range(3)]

    def ici(self, r, k):
        cx, cy, _ = self.chips[k]
        return self._copy(r, k, self.p, self.cc, (cx, cy, self.cc))

    def ici_arrival(self, r, k):
        return self._copy(r, k, self.chips[k][2], self.cc, (self.x, self.y, self.cc))

    def forward(self, r, k):
        return self._copy(r, k, self.chips[k][2], self.cc, (self.x, self.y, 1 - self.cc))

    def forward_arrival(self, r, k):
        return self._copy(r, k, self.chips[k][2], 1 - self.cc, (self.x, self.y, self.cc))


def _ag_weights(w4, ranges):
    n = 3 * len(ranges)

    def body(w_ref, out_ref, send1, recv1, send2, recv2):
        ici, fwd = _AgCopies(out_ref, ranges, send1, recv1), _AgCopies(out_ref, ranges, send2, recv2)
        for r, k in ici.pairs():
            ici.ici(r, k).start()
        for r, k in ici.pairs():
            ici.ici_arrival(r, k).wait_recv()
            fwd.forward(r, k).start()
        for r, k in ici.pairs():
            fwd.forward_arrival(r, k).wait_recv()
        for r, k in ici.pairs():
            ici.ici(r, k).wait_send()
            fwd.forward(r, k).wait_send()

    return pl.pallas_call(
        body, name="ag_weights", in_specs=[ANY], out_specs=ANY, out_shape=jax.ShapeDtypeStruct(w4.shape, w4.dtype),
        scratch_shapes=[pltpu.SemaphoreType.DMA((n,))] * 4, input_output_aliases={0: 0}, compiler_params=_params(),
    )(w4)


def _ag_forward(w4, ranges):
    n = 3 * len(ranges)

    def body(w_ref, out_ref, send2, recv2):
        fwd = _AgCopies(out_ref, ranges, send2, recv2)
        for r, k in fwd.pairs():
            fwd.forward(r, k).start()
        for r, k in fwd.pairs():
            fwd.forward_arrival(r, k).wait_recv()
        for r, k in fwd.pairs():
            fwd.forward(r, k).wait_send()

    return pl.pallas_call(
        body, name="ag_forward", in_specs=[ANY], out_specs=ANY, out_shape=jax.ShapeDtypeStruct(w4.shape, w4.dtype),
        scratch_shapes=[pltpu.SemaphoreType.DMA((n,))] * 2, input_output_aliases={0: 0}, compiler_params=_params(),
    )(w4)


def _swap_halves(g, *, name):
    nb, r, c = g.shape
    half = r // 2

    def body(g_ref, out_ref, send_sem, recv_sem):
        x, y, cc = _coords()
        cp = pltpu.make_async_remote_copy(
            src_ref=g_ref.at[:, pl.ds((1 - cc) * half, half), :], dst_ref=out_ref, send_sem=send_sem, recv_sem=recv_sem,
            device_id=(x, y, 1 - cc), device_id_type=MESH)
        cp.start()
        cp.wait()

    return pl.pallas_call(
        body, name=name, in_specs=[ANY], out_specs=ANY, out_shape=jax.ShapeDtypeStruct((nb, half, c), g.dtype),
        scratch_shapes=[pltpu.SemaphoreType.DMA, pltpu.SemaphoreType.DMA], compiler_params=_params(),
    )(g)


def _my_half_index():
    return lax.axis_index("c").astype(jnp.int32).reshape(1)


def _add_halves(g, got):
    nb, r, c = g.shape
    half = r // 2
    tr = _tile(half, 512, 16)
    nt_ = half // tr

    def body(c_ref, a_ref, b_ref, o_ref):
        o_ref[...] = (a_ref[...].astype(F32) + b_ref[...].astype(F32)).astype(BF16)

    return pl.pallas_call(
        body, name="rs_add_sibling",
        grid_spec=pltpu.PrefetchScalarGridSpec(
            num_scalar_prefetch=1, grid=(nb, nt_),
            in_specs=[pl.BlockSpec((1, tr, c), lambda b, i, cr: (b, cr[0] * nt_ + i, 0)),
                      pl.BlockSpec((1, tr, c), lambda b, i, cr: (b, i, 0))],
            out_specs=pl.BlockSpec((1, tr, c), lambda b, i, cr: (b, i, 0))),
        out_shape=jax.ShapeDtypeStruct((nb, half, c), BF16), compiler_params=_params(("parallel", "parallel")),
    )(_my_half_index(), g, got)


def _scatter_chips(s):
    nb, hrows, c = s.shape

    def body(s_ref, out_ref, send_sems, recv_sems):
        x, y, cc = _coords()
        p = 2 * x + y
        chips = _other_chips(x, y)
        sends = [pltpu.make_async_remote_copy(
            src_ref=s_ref.at[blk], dst_ref=out_ref.at[k], send_sem=send_sems.at[k], recv_sem=recv_sems.at[k],
            device_id=(cx, cy, cc), device_id_type=MESH) for k, (cx, cy, blk) in enumerate(chips)]
        for cp in sends:
            cp.start()
        for k in range(3):
            pltpu.make_async_remote_copy(
                src_ref=s_ref.at[p], dst_ref=out_ref.at[k], send_sem=send_sems.at[k], recv_sem=recv_sems.at[k],
                device_id=(x, y, cc), device_id_type=MESH).wait_recv()
        for cp in sends:
            cp.wait_send()

    return pl.pallas_call(
        body, name="rs_scatter", in_specs=[ANY], out_specs=ANY, out_shape=jax.ShapeDtypeStruct((3, hrows, c), s.dtype),
        scratch_shapes=[pltpu.SemaphoreType.DMA((3,)), pltpu.SemaphoreType.DMA((3,))], compiler_params=_params(),
    )(s)


def _sum_chips(s, got):
    nb, hrows, c = s.shape
    tr = _tile(hrows, 512, 16)

    def body(idx_ref, own_ref, got_ref, o_ref):
        p = idx_ref[0]
        own = own_ref[0].astype(F32)
        parts = [got_ref[k].astype(F32) for k in range(3)]
        acc = jnp.zeros_like(own)
        for q in range(4):
            val = own
            for k, rel in enumerate((2, 1, 3)):
                val = jnp.where((p ^ rel) == q, parts[k], val)
            acc = acc + val
        o_ref[...] = acc

    idx = (2 * lax.axis_index("x") + lax.axis_index("y")).astype(jnp.int32).reshape(1)
    return pl.pallas_call(
        body, name="rs_sum_chips",
        grid_spec=pltpu.PrefetchScalarGridSpec(
            num_scalar_prefetch=1, grid=(hrows // tr,),
            in_specs=[pl.BlockSpec((1, tr, c), lambda i, pr: (pr[0], i, 0)), pl.BlockSpec((3, tr, c), lambda i, pr: (0, i, 0))],
            out_specs=pl.BlockSpec((tr, c), lambda i, pr: (i, 0))),
        out_shape=jax.ShapeDtypeStruct((hrows, c), F32), compiler_params=_params(("parallel",)),
    )(idx, s, got)


def _swap_sibling(t):
    def body(t_ref, out_ref, send_sem, recv_sem):
        x, y, cc = _coords()
        cp = pltpu.make_async_remote_copy(src_ref=t_ref, dst_ref=out_ref, send_sem=send_sem, recv_sem=recv_sem,
                                          device_id=(x, y, 1 - cc), device_id_type=MESH)
        cp.start()
        cp.wait()

    return pl.pallas_call(
        body, name="rs_join", in_specs=[ANY], out_specs=ANY, out_shape=jax.ShapeDtypeStruct(t.shape, t.dtype),
        scratch_shapes=[pltpu.SemaphoreType.DMA, pltpu.SemaphoreType.DMA], compiler_params=_params(),
    )(t)


def _reduce_scatter(g):
    got = _swap_halves(g, name="rs_swap")
    s = _add_halves(g, got)
    recv = _scatter_chips(s)
    t = _sum_chips(s, recv)
    r = _swap_sibling(t)
    first = lax.axis_index("c") == 0
    return jnp.concatenate([jnp.where(first, t, r), jnp.where(first, r, t)], axis=0)


_BIG = (("w_gate_up", 2), ("w_down", 1), ("fox_w_in", 2), ("fox_w_out", 1), ("gla_w_in", 2), ("gla_w_out", 1),
        ("gdn_w_in", 2), ("gdn_w_out", 1))
_SMALL_SHARDED = (("meta_tokens", 1), ("gla_w_alpha2", 2), ("gdn_conv_w", 3))
_REPLICATED = ("norm_mix", "norm_ffn", "fox_b_f", "fox_q_gain", "fox_k_gain", "gla_b_alpha", "gla_o_gain",
               "gdn_a_log", "gdn_dt_bias", "gdn_o_gain")
_WEIGHTS = ("meta_tokens", "norm_mix", "norm_ffn", "w_gate_up", "w_down", "fox_w_in", "fox_b_f", "fox_q_gain",
            "fox_k_gain", "fox_w_out", "gla_w_in", "gla_w_alpha2", "gla_b_alpha", "gla_o_gain", "gla_w_out",
            "gdn_w_in", "gdn_conv_w", "gdn_a_log", "gdn_dt_bias", "gdn_o_gain", "gdn_w_out")
_PACK_ROWS = 512
_IN_W = ("fox_w_in", "gla_w_in", "gdn_w_in")
_OUT_W = ("fox_w_out", "gla_w_out", "gdn_w_out")


def _piece_rows(n):
    return -(-n // 32) * 32


def _pack(arrays, width, row_mult, dtype):
    flat = jnp.concatenate([a.astype(dtype).reshape(-1) for a in arrays])
    per = width * row_mult
    n = -(-flat.shape[0] // per) * per
    return jnp.pad(flat, (0, n - flat.shape[0])).reshape(n // width, width)


def _unpack(flat, shapes):
    out, off = [], 0
    for s in shapes:
        n = 1
        for d in s:
            n *= d
        out.append(flat[off:off + n].reshape(s))
        off += n
    return out


def _unpack_cols(flat2, shapes):
    out, off = [], 0
    for s in shapes:
        n = 1
        for d in s:
            n *= d
        out.append(flat2[:, off:off + n].reshape((flat2.shape[0],) + tuple(s)))
        off += n
    return out


def _pad_cols(w, n):
    return jnp.pad(w, [(0, 0)] * (w.ndim - 1) + [(0, n - w.shape[-1])])


def kernel(x, meta_tokens, norm_mix, norm_ffn, w_gate_up, w_down, fox_w_in, fox_b_f, fox_q_gain, fox_k_gain, fox_w_out, gla_w_in, gla_w_alpha2, gla_b_alpha, gla_o_gain, gla_w_out, gdn_w_in, gdn_conv_w, gdn_a_log, gdn_dt_bias, gdn_o_gain, gdn_w_out, loss_target, m_meta_tokens, m_norm_mix, m_norm_ffn, m_w_gate_up, m_w_down, m_fox_w_in, m_fox_b_f, m_fox_q_gain, m_fox_k_gain, m_fox_w_out, m_gla_w_in, m_gla_w_alpha2, m_gla_b_alpha, m_gla_o_gain, m_gla_w_out, m_gdn_w_in, m_gdn_conv_w, m_gdn_a_log, m_gdn_dt_bias, m_gdn_o_gain, m_gdn_w_out, v_meta_tokens, v_norm_mix, v_norm_ffn, v_w_gate_up, v_w_down, v_fox_w_in, v_fox_b_f, v_fox_q_gain, v_fox_k_gain, v_fox_w_out, v_gla_w_in, v_gla_w_alpha2, v_gla_b_alpha, v_gla_o_gain, v_gla_w_out, v_gdn_w_in, v_gdn_conv_w, v_gdn_a_log, v_gdn_dt_bias, v_gdn_o_gain, v_gdn_w_out):
    W = dict(meta_tokens=meta_tokens, norm_mix=norm_mix, norm_ffn=norm_ffn, w_gate_up=w_gate_up, w_down=w_down,
             fox_w_in=fox_w_in, fox_b_f=fox_b_f, fox_q_gain=fox_q_gain, fox_k_gain=fox_k_gain, fox_w_out=fox_w_out,
             gla_w_in=gla_w_in, gla_w_alpha2=gla_w_alpha2, gla_b_alpha=gla_b_alpha, gla_o_gain=gla_o_gain,
             gla_w_out=gla_w_out, gdn_w_in=gdn_w_in, gdn_conv_w=gdn_conv_w, gdn_a_log=gdn_a_log,
             gdn_dt_bias=gdn_dt_bias, gdn_o_gain=gdn_o_gain, gdn_w_out=gdn_w_out)
    M = dict(meta_tokens=m_meta_tokens, norm_mix=m_norm_mix, norm_ffn=m_norm_ffn, w_gate_up=m_w_gate_up, w_down=m_w_down,
             fox_w_in=m_fox_w_in, fox_b_f=m_fox_b_f, fox_q_gain=m_fox_q_gain, fox_k_gain=m_fox_k_gain,
             fox_w_out=m_fox_w_out, gla_w_in=m_gla_w_in, gla_w_alpha2=m_gla_w_alpha2, gla_b_alpha=m_gla_b_alpha,
             gla_o_gain=m_gla_o_gain, gla_w_out=m_gla_w_out, gdn_w_in=m_gdn_w_in, gdn_conv_w=m_gdn_conv_w,
             gdn_a_log=m_gdn_a_log, gdn_dt_bias=m_gdn_dt_bias, gdn_o_gain=m_gdn_o_gain, gdn_w_out=m_gdn_w_out)
    V = dict(meta_tokens=v_meta_tokens, norm_mix=v_norm_mix, norm_ffn=v_norm_ffn, w_gate_up=v_w_gate_up, w_down=v_w_down,
             fox_w_in=v_fox_w_in, fox_b_f=v_fox_b_f, fox_q_gain=v_fox_q_gain, fox_k_gain=v_fox_k_gain,
             fox_w_out=v_fox_w_out, gla_w_in=v_gla_w_in, gla_w_alpha2=v_gla_w_alpha2, gla_b_alpha=v_gla_b_alpha,
             gla_o_gain=v_gla_o_gain, gla_w_out=v_gla_w_out, gdn_w_in=v_gdn_w_in, gdn_conv_w=v_gdn_conv_w,
             gdn_a_log=v_gdn_a_log, gdn_dt_bias=v_gdn_dt_bias, gdn_o_gain=v_gdn_o_gain, gdn_w_out=v_gdn_w_out)
    chip = 2 * lax.axis_index("x") + lax.axis_index("y")

    pieces, offs, r = [], {}, FFN_ROWS
    for n in _IN_W:
        nc = W[n].shape[2]
        for l in range(W[n].shape[0]):
            pieces.append(jnp.pad(W[n][l].T.astype(BF16), ((0, _piece_rows(nc) - nc), (0, 0))))
            offs[n, l] = r
            r += _piece_rows(nc)
    for n in _OUT_W:
        for l in range(W[n].shape[0]):
            pieces.append(W[n][l].astype(BF16))
            offs[n, l] = r
            r += W[n].shape[1]
    rows = -(-r // _PACK_ROWS) * _PACK_ROWS
    packed = jnp.concatenate([jnp.swapaxes(w_gate_up, 1, 2).reshape(-1, D).astype(BF16), w_down.reshape(-1, D).astype(BF16)]
                             + pieces + [jnp.zeros((rows - r, D), BF16)], axis=0)
    first_rows = [(OFF_GU, GU_ROWS), (OFF_DOWN, DOWN_ROWS), (offs["fox_w_in", 0], offs["fox_w_in", 1] - offs["fox_w_in", 0]),
                  (offs["fox_w_out", 0], offs["fox_w_out", 1] - offs["fox_w_out", 0])]
    later_rows = [(OFF_GU + GU_ROWS, (DEPTH - 1) * GU_ROWS), (OFF_DOWN + DOWN_ROWS, (DEPTH - 1) * DOWN_ROWS),
                  (offs["fox_w_in", 1], offs["fox_w_out", 0] - offs["fox_w_in", 1]), (offs["fox_w_out", 1], r - offs["fox_w_out", 1])]
    wpk = _ag_weights(lax.dynamic_update_slice(lax.empty((4, rows, D), BF16), packed[None], (chip, 0, 0)), first_rows)

    def in_t(buf, n, l, npad):
        nc = W[n].shape[2]
        return jnp.concatenate([buf[q, offs[n, l]:offs[n, l] + nc] for q in range(4)] + [jnp.zeros((npad - 4 * nc, D), BF16)], 0)

    def out_w(buf, n, l):
        return jnp.concatenate([buf[q, offs[n, l]:offs[n, l] + W[n].shape[1]] for q in range(4)], axis=0)

    fox_in0, fox_out0 = in_t(wpk, "fox_w_in", 0, FOX_INP), out_w(wpk, "fox_w_out", 0)
    full = {}
    small = _pack([W[n] for n, _ in _SMALL_SHARDED], LANES, 8, F32)
    small_all = _gather8(small, reduce=False, name="gather_small").reshape(8, -1)
    for (n, ax), seg in zip(_SMALL_SHARDED, _unpack_cols(small_all, [W[n].shape for n, _ in _SMALL_SHARDED])):
        full[n] = jnp.concatenate([seg[2 * q] for q in range(4)], axis=ax)
    fox_in, full["fox_w_out"] = [fox_in0], [fox_out0]
    w_alpha2, conv_w = full["gla_w_alpha2"][0], full["gdn_conv_w"][0]

    h = jnp.concatenate([jnp.zeros((META0, D), F32), full["meta_tokens"], x[0]], axis=0)
    saved = []
    for i in range(DEPTH):
        kind, j = i % 3, i // 3
        y = _rms_fwd(h, norm_mix[i], name=f"norm_mix{i}")
        if kind == 0:
            proj = _mm(y, fox_in[j], tb=True, name=f"fox_in{j}")
            qa, ka, va = _fox_prep(proj, fox_b_f[j], fox_q_gain[j], fox_k_gain[j])
            if i == 0:
                o, og, lse, wpk = _fox_attn_fwd(qa, ka, va, proj, ag=(wpk, later_rows))
                wpk = _ag_forward(wpk, later_rows)
                fox_in += [in_t(wpk, "fox_w_in", l, FOX_INP) for l in range(1, fox_w_in.shape[0])]
                full["fox_w_out"] += [out_w(wpk, "fox_w_out", l) for l in range(1, fox_w_out.shape[0])]
                gla_in = [in_t(wpk, "gla_w_in", l, GLA_INP) for l in range(gla_w_in.shape[0])]
                gdn_in = [in_t(wpk, "gdn_w_in", l, GDN_INP) for l in range(gdn_w_in.shape[0])]
                for n in ("gla_w_out", "gdn_w_out"):
                    full[n] = [out_w(wpk, n, l) for l in range(W[n].shape[0])]
            else:
                o, og, lse = _fox_attn_fwd(qa, ka, va, proj)
            w_out, mix = full["fox_w_out"][j], (proj, qa, ka, va, o, lse)
        elif kind == 1:
            proj = _mm(y, gla_in[j], tb=True, name=f"gla_in{j}")
            o, og, states = _gla_fwd(proj, w_alpha2, gla_b_alpha[j], gla_o_gain[j])
            w_out, mix = full["gla_w_out"][j], (proj, o, states)
        else:
            proj = _mm(y, gdn_in[j], tb=True, name=f"gdn_in{j}")
            o, og, states = _gdn_fwd(proj, conv_w, gdn_a_log[j], gdn_dt_bias[j], gdn_o_gain[j])
            w_out, mix = full["gdn_w_out"][j], (proj, o, states)
        hm = _mm(og, w_out, add=h, name=f"mix_out{i}")
        yf = _rms_fwd(hm, norm_ffn[i], name=f"norm_ffn{i}")
        gate, up, act = _ffn_up(yf, wpk, i)
        hn = _ffn_down(act, wpk, i, hm)
        saved.append((h, y, mix, og, w_out, hm, yf, gate, up, act))
        h = hn
    dh, loss_tile = _loss_head(h, loss_target[0])

    G = {n: [None] * W[n].shape[0] for n in _WEIGHTS if n not in ("meta_tokens", "w_gate_up", "w_down") + _IN_W}
    GT = {}
    gpk = jnp.zeros((4, rows, D), BF16)
    for i in reversed(range(DEPTH)):
        kind, j = i % 3, i // 3
        h_in, y, mix, og, w_out, hm, yf, gate, up, act = saved[i]
        dg, du = _ffn_dact(dh, wpk, i, gate, up)
        gpk = _ffn_dw_down(act, dh, gpk, i)
        dyf = _ffn_dyf(dg, du, wpk, i)
        gpk = _ffn_dw_gu(dg, du, yf, gpk, i)
        dhm, dnf = _rms_bwd(hm, norm_ffn[i], dyf, dh, name=f"d_norm_ffn{i}")
        G["norm_ffn"][i] = dnf[0]
        dog = _mm(dhm, w_out, tb=True, name=f"d_og{i}")
        dw_out = _mm(og, dhm, ta=True, out_dtype=BF16, name=f"d_w_out{i}")
        if kind == 0:
            proj, qa, ka, va, o, lse = mix
            doa, q2, dgate = _fox_gate_bwd(dog, o, proj, lse, qa)
            dqn, dkn, dv, dct = _fox_attn_bwd(q2, ka, va, doa)
            dproj, dqg, dkg, dbf = _fox_prep_bwd(proj, fox_b_f[j], fox_q_gain[j], fox_k_gain[j], dqn, dkn, dv, dgate, dct)
            G["fox_w_out"][j] = dw_out
            G["fox_q_gain"][j] = dqg.reshape(FOX_H, FOX_DH).sum(0)
            G["fox_k_gain"][j] = dkg.reshape(FOX_H, FOX_DH).sum(0)
            G["fox_b_f"][j] = dbf[0, :FOX_H]
            w_in, wname = fox_in[j], "fox_w_in"
        elif kind == 1:
            proj, o, states = mix
            dproj, dwa, dba, dogain = _gla_bwd(proj, w_alpha2, gla_b_alpha[j], gla_o_gain[j], o, states, dog)
            G["gla_w_out"][j] = dw_out
            G["gla_w_alpha2"][j] = dwa[:GLA_RANK]
            G["gla_b_alpha"][j] = dba[0]
            G["gla_o_gain"][j] = dogain[0]
            w_in, wname = gla_in[j], "gla_w_in"
        else:
            proj, o, states = mix
            dproj, dcw, dal, ddt, dogain = _gdn_bwd(proj, conv_w, gdn_a_log[j], gdn_dt_bias[j], gdn_o_gain[j], o, states, dog)
            G["gdn_w_out"][j] = dw_out
            G["gdn_conv_w"][j] = dcw[:4].reshape(4, 1, GDN_CONV)
            G["gdn_a_log"][j] = dal[0, :GDN_H]
            G["gdn_dt_bias"][j] = ddt[0, :GDN_H]
            G["gdn_o_gain"][j] = dogain[0]
            w_in, wname = gdn_in[j], "gdn_w_in"
        dy = _mm(dproj, w_in, name=f"d_y{i}")
        GT[wname, j] = _mm(dproj, y, ta=True, out_dtype=BF16, name=f"d_w_in{i}")
        dh, dnm = _rms_bwd(h_in, norm_mix[i], dy, dhm, name=f"d_norm_mix{i}")
        G["norm_mix"][i] = dnm[0]
    grad_x = dh[ROW0:][None]
    G = {n: (v if n in _OUT_W else jnp.stack(v)) for n, v in G.items()}
    G["meta_tokens"] = dh[META0:ROW0]

    blocks = []
    for q in range(4):
        parts = []
        for n in _IN_W:
            nc = W[n].shape[2]
            for l in range(W[n].shape[0]):
                parts.append(jnp.pad(GT[n, l][q * nc:(q + 1) * nc], ((0, _piece_rows(nc) - nc), (0, 0))))
        for n in _OUT_W:
            nr = W[n].shape[1]
            for l in range(W[n].shape[0]):
                parts.append(G[n][l][q * nr:(q + 1) * nr])
        blocks.append(jnp.concatenate(parts + [jnp.zeros((rows - r, D), BF16)], axis=0))
    gpk = lax.dynamic_update_slice(gpk, jnp.stack(blocks), (0, FFN_ROWS, 0))
    reduced = _reduce_scatter(gpk)
    grads = {}
    for n in _IN_W:
        grads[n] = jnp.stack([reduced[offs[n, l]:offs[n, l] + W[n].shape[2]].T for l in range(W[n].shape[0])])
    for n in _OUT_W:
        grads[n] = jnp.stack([reduced[offs[n, l]:offs[n, l] + W[n].shape[1]] for l in range(W[n].shape[0])])
    small_names = [n for n, _ in _SMALL_SHARDED] + list(_REPLICATED)
    small_g = _pack([G[n] for n in small_names] + [loss_tile[0, 0:1]], LANES, 8, F32)
    small_sum = _gather8(small_g, reduce=True, name="allreduce_small").reshape(-1)
    small_shapes = [G[n].shape for n in small_names] + [(1,)]
    small_vals = _unpack(small_sum, small_shapes)
    loss = small_vals[-1][0]
    for n, val in zip(small_names, small_vals[:-1]):
        grads[n] = val
    for n, ax in _SMALL_SHARDED:
        sz = W[n].shape[ax]
        grads[n] = lax.dynamic_slice_in_dim(grads[n], chip * sz, sz, axis=ax)

    delta, new_m, new_v = {}, {}, {}
    for n, off, tr_ in (("w_gate_up", OFF_GU, True), ("w_down", OFF_DOWN, False)):
        grads[n], delta[n], new_m[n], new_v[n] = _adamw_packed(W[n], reduced, M[n], V[n], row_off=off, transposed=tr_,
                                                               name=f"adamw_{n}")
    for n in _IN_W + _OUT_W:
        delta[n], new_m[n], new_v[n] = _adamw(W[n], grads[n], M[n], V[n], name=f"adamw_{n}")
    tiny = [n for n in _WEIGHTS if n not in dict(_BIG)]
    packs = [_pack([T[n] for n in tiny], LANES, 8, F32) for T in (W, grads, M, V)]
    outs = _adamw(*packs, name="adamw_small")
    shapes = [W[n].shape for n in tiny]
    for dst, o in zip((delta, new_m, new_v), outs):
        for n, val in zip(tiny, _unpack(o.reshape(-1), shapes)):
            dst[n] = val
    return (loss, grad_x, *[grads[n] for n in _WEIGHTS], *[delta[n] for n in _WEIGHTS],
            *[new_m[n] for n in _WEIGHTS], *[new_v[n] for n in _WEIGHTS])
```

```python
import functools

import jax
import jax.numpy as jnp
from jax import lax
from jax.experimental import pallas as pl
from jax.experimental.pallas import tpu as pltpu

F32, BF16 = jnp.float32, jnp.bfloat16
D = 1024
N_META = 16
ROW0 = 128
META0 = ROW0 - N_META
EPS = 1e-6
LANES = 128
VMEM_LIMIT = 56 * 1024 * 1024

FOX_H, FOX_DH = 16, 64
FOX_INP = 4224
GLA_H, GLA_DK, GLA_DV, GLA_RANK = 4, 128, 256, 16
GLA_QK, GLA_V = 512, 1024
GLA_INP = 3200
GLA_NORM = 16.0
GDN_H, GDN_DK, GDN_DV = 8, 128, 128
GDN_CONV = 3072
GDN_INP = 4224
CHUNK = 64
D_FF = 2816
DEPTH = 4

ADAM_LR, ADAM_B1, ADAM_B2, ADAM_EPS, ADAM_WD, ADAM_STEP = 0.001, 0.9, 0.999, 1e-08, 0.01, 10

MESH = pl.DeviceIdType.MESH
ANY = pl.BlockSpec(memory_space=pl.ANY)
VM = pl.BlockSpec(memory_space=pltpu.VMEM)


def _params(sem=None, **kw):
    if sem is not None:
        kw["dimension_semantics"] = sem
    return pltpu.CompilerParams(vmem_limit_bytes=VMEM_LIMIT, **kw)


def _tile(n, cap, mult=LANES):
    best = None
    for t in range(mult, min(n, cap) + 1, mult):
        if n % t == 0:
            best = t
    return best if best is not None else n


def nn(a, b, **kw):
    return jnp.dot(a, b, preferred_element_type=F32, **kw)


def nt(a, b, **kw):
    return lax.dot_general(a, b, (((1,), (1,)), ((), ())), preferred_element_type=F32, **kw)


def tn(a, b, **kw):
    return lax.dot_general(a, b, (((0,), (0,)), ((), ())), preferred_element_type=F32, **kw)


def _split3(x):
    hi = x.astype(BF16)
    r = x - hi.astype(F32)
    mid = r.astype(BF16)
    lo = (r - mid.astype(F32)).astype(BF16)
    return hi, mid, lo


def _sel_l(sel, x):
    a, b, c = _split3(x)
    return nn(sel, a) + nn(sel, b) + nn(sel, c)


def _sel_r(x, sel):
    a, b, c = _split3(x)
    return nn(a, sel) + nn(b, sel) + nn(c, sel)


def _iota(shape, dim):
    return lax.broadcasted_iota(jnp.int32, shape, dim)


def _tri(n, upper=False, strict=False):
    i, j = _iota((n, n), 0), _iota((n, n), 1)
    if upper:
        m = (j > i) if strict else (j >= i)
    else:
        m = (j < i) if strict else (j <= i)
    return m


def _sigmoid(x):
    return 1.0 / (1.0 + jnp.exp(-x))


def _log_sigmoid(x):
    return jnp.minimum(x, 0.0) - jnp.log(1.0 + jnp.exp(-jnp.abs(x)))


def _softplus(x):
    return jnp.maximum(x, 0.0) + jnp.log(1.0 + jnp.exp(-jnp.abs(x)))


def _silu(x):
    return x * _sigmoid(x)


def _dsilu(x):
    s = _sigmoid(x)
    return s * (1.0 + x * (1.0 - s))


def _mm(a, b, *, ta=False, tb=False, add=None, out_dtype=F32, name):
    m, k = (a.shape[1], a.shape[0]) if ta else a.shape
    n = b.shape[0] if tb else b.shape[1]
    assert k == (b.shape[1] if tb else b.shape[0])
    tm, tn_, tk = _tile(m, 1408, LANES if ta else 16), _tile(n, 1408), _tile(k, 1408)
    nk = k // tk

    def body(*refs):
        if add is None:
            a_ref, b_ref, o_ref, acc = refs
        else:
            a_ref, b_ref, r_ref, o_ref, acc = refs
        kk = pl.program_id(2)

        @pl.when(kk == 0)
        def _():
            acc[...] = jnp.zeros_like(acc)

        av, bv = a_ref[...].astype(BF16), b_ref[...].astype(BF16)
        dims = (((0,) if ta else (1,), (1,) if tb else (0,)), ((), ()))
        acc[...] += lax.dot_general(av, bv, dims, preferred_element_type=F32)

        @pl.when(kk == nk - 1)
        def _():
            r = acc[...]
            if add is not None:
                r = r + r_ref[...].astype(F32)
            o_ref[...] = r.astype(out_dtype)

    a_spec = pl.BlockSpec((tk, tm), lambda i, j, q: (q, i)) if ta else pl.BlockSpec((tm, tk), lambda i, j, q: (i, q))
    b_spec = pl.BlockSpec((tn_, tk), lambda i, j, q: (j, q)) if tb else pl.BlockSpec((tk, tn_), lambda i, j, q: (q, j))
    o_spec = pl.BlockSpec((tm, tn_), lambda i, j, q: (i, j))
    ins, specs = [a, b], [a_spec, b_spec]
    if add is not None:
        ins.append(add)
        specs.append(o_spec)
    return pl.pallas_call(
        body, name=name, grid=(m // tm, n // tn_, nk), in_specs=specs, out_specs=o_spec,
        out_shape=jax.ShapeDtypeStruct((m, n), out_dtype),
        scratch_shapes=[pltpu.VMEM((tm, tn_), F32)],
        compiler_params=_params(("parallel", "parallel", "arbitrary")),
    )(*ins)


def _rms_fwd(h, g, *, name):
    lp = h.shape[0]
    tr = _tile(lp, 512)

    def body(h_ref, g_ref, y_ref):
        x = h_ref[...]
        r = lax.rsqrt(jnp.mean(x * x, axis=-1, keepdims=True) + EPS)
        y_ref[...] = (x * r * g_ref[...]).astype(BF16)

    return pl.pallas_call(
        body, name=name, grid=(lp // tr,),
        in_specs=[pl.BlockSpec((tr, D), lambda i: (i, 0)), pl.BlockSpec((1, D), lambda i: (0, 0))],
        out_specs=pl.BlockSpec((tr, D), lambda i: (i, 0)),
        out_shape=jax.ShapeDtypeStruct((lp, D), BF16), compiler_params=_params(("parallel",)),
    )(h, g.reshape(1, D))


def _rms_bwd(h, g, dy, dres, *, name):
    lp = h.shape[0]
    tr = _tile(lp, 512)

    def body(h_ref, g_ref, dy_ref, dr_ref, dh_ref, dg_ref):
        @pl.when(pl.program_id(0) == 0)
        def _():
            dg_ref[...] = jnp.zeros_like(dg_ref)

        x, dyv = h_ref[...], dy_ref[...].astype(F32)
        r = lax.rsqrt(jnp.mean(x * x, axis=-1, keepdims=True) + EPS)
        u = dyv * g_ref[...]
        dx = r * u - x * (r * r * r) * jnp.mean(x * u, axis=-1, keepdims=True)
        dh_ref[...] = dr_ref[...] + dx
        dg_ref[...] += jnp.sum(dyv * x * r, axis=0, keepdims=True)

    return pl.pallas_call(
        body, name=name, grid=(lp // tr,),
        in_specs=[pl.BlockSpec((tr, D), lambda i: (i, 0)), pl.BlockSpec((1, D), lambda i: (0, 0)),
                  pl.BlockSpec((tr, D), lambda i: (i, 0)), pl.BlockSpec((tr, D), lambda i: (i, 0))],
        out_specs=[pl.BlockSpec((tr, D), lambda i: (i, 0)), pl.BlockSpec((1, D), lambda i: (0, 0))],
        out_shape=[jax.ShapeDtypeStruct((lp, D), F32), jax.ShapeDtypeStruct((1, D), F32)],
        compiler_params=_params(("arbitrary",)),
    )(h, g.reshape(1, D), dy, dres)


GU_ROWS, DOWN_ROWS = 1408, 704
OFF_GU, OFF_DOWN = 0, DEPTH * GU_ROWS
FFN_ROWS = DEPTH * (GU_ROWS + DOWN_ROWS)
FFN_TM = 704


def _gu_spec(fn):
    return pl.BlockSpec((None, GU_ROWS, D), fn)


def _down_spec(fn):
    return pl.BlockSpec((None, DOWN_ROWS, D), fn)


def _down_pair(w0_ref, w1_ref):
    return jnp.concatenate([w0_ref[...], w1_ref[...]], axis=0)


def _ffn_up(yf, wpk, layer):
    lp = yf.shape[0]
    tm = _tile(lp, FFN_TM, 16)

    def body(y_ref, wg_ref, wu_ref, g_ref, u_ref, a_ref):
        y = y_ref[...]
        g, u = nt(y, wg_ref[...]), nt(y, wu_ref[...])
        g_ref[...] = g.astype(BF16)
        u_ref[...] = u.astype(BF16)
        a_ref[...] = (_silu(g) * u).astype(BF16)

    o = pl.BlockSpec((tm, GU_ROWS), lambda i, j: (i, j))
    return pl.pallas_call(
        body, name=f"ffn_up{layer}", grid=(lp // tm, 2),
        in_specs=[pl.BlockSpec((tm, D), lambda i, j: (i, 0)), _gu_spec(lambda i, j: (j, OFF_GU // GU_ROWS + layer, 0)),
                  _gu_spec(lambda i, j: (2 + j, OFF_GU // GU_ROWS + layer, 0))],
        out_specs=[o, o, o], out_shape=[jax.ShapeDtypeStruct((lp, D_FF), BF16)] * 3,
        compiler_params=_params(("parallel", "parallel")),
    )(yf, wpk, wpk)


def _ffn_down(act, wpk, layer, res):
    lp = act.shape[0]
    tm = _tile(lp, FFN_TM, 16)

    def body(a_ref, w0_ref, w1_ref, r_ref, o_ref, acc):
        kk = pl.program_id(1)

        @pl.when(kk == 0)
        def _():
            acc[...] = r_ref[...]

        acc[...] += nn(a_ref[...], _down_pair(w0_ref, w1_ref))

        @pl.when(kk == 1)
        def _():
            o_ref[...] = acc[...]

    o = pl.BlockSpec((tm, D), lambda i, kk: (i, 0))
    blk = OFF_DOWN // DOWN_ROWS + layer
    return pl.pallas_call(
        body, name=f"ffn_down{layer}", grid=(lp // tm, 2),
        in_specs=[pl.BlockSpec((tm, GU_ROWS), lambda i, kk: (i, kk)), _down_spec(lambda i, kk: (2 * kk, blk, 0)),
                  _down_spec(lambda i, kk: (2 * kk + 1, blk, 0)), o],
        out_specs=o, out_shape=jax.ShapeDtypeStruct((lp, D), F32), scratch_shapes=[pltpu.VMEM((tm, D), F32)],
        compiler_params=_params(("parallel", "arbitrary")),
    )(act, wpk, wpk, res)


def _ffn_dact(dh, wpk, layer, gate, up):
    lp = dh.shape[0]
    tm = _tile(lp, FFN_TM, 16)

    def body(d_ref, w0_ref, w1_ref, g_ref, u_ref, dg_ref, du_ref):
        da = nt(d_ref[...].astype(BF16), _down_pair(w0_ref, w1_ref))
        g, u = g_ref[...].astype(F32), u_ref[...].astype(F32)
        dg_ref[...] = (da * u * _dsilu(g)).astype(BF16)
        du_ref[...] = (da * _silu(g)).astype(BF16)

    o = pl.BlockSpec((tm, GU_ROWS), lambda i, j: (i, j))
    blk = OFF_DOWN // DOWN_ROWS + layer
    return pl.pallas_call(
        body, name=f"d_act{layer}", grid=(lp // tm, 2),
        in_specs=[pl.BlockSpec((tm, D), lambda i, j: (i, 0)), _down_spec(lambda i, j: (2 * j, blk, 0)),
                  _down_spec(lambda i, j: (2 * j + 1, blk, 0)), o, o],
        out_specs=[o, o], out_shape=[jax.ShapeDtypeStruct((lp, D_FF), BF16)] * 2,
        compiler_params=_params(("parallel", "parallel")),
    )(dh, wpk, wpk, gate, up)


def _ffn_dyf(dg, du, wpk, layer):
    lp = dg.shape[0]
    tm = _tile(lp, FFN_TM, 16)

    def body(dg_ref, du_ref, w_ref, o_ref, acc):
        kk = pl.program_id(1)

        @pl.when(kk == 0)
        def _():
            acc[...] = jnp.zeros_like(acc)

        @pl.when(kk < 2)
        def _():
            acc[...] += nn(dg_ref[...], w_ref[...])

        @pl.when(kk >= 2)
        def _():
            acc[...] += nn(du_ref[...], w_ref[...])

        @pl.when(kk == 3)
        def _():
            o_ref[...] = acc[...]

    return pl.pallas_call(
        body, name=f"d_yf{layer}", grid=(lp // tm, 4),
        in_specs=[pl.BlockSpec((tm, GU_ROWS), lambda i, kk: (i, jnp.minimum(kk, 1))),
                  pl.BlockSpec((tm, GU_ROWS), lambda i, kk: (i, jnp.maximum(kk - 2, 0))),
                  _gu_spec(lambda i, kk: (kk, OFF_GU // GU_ROWS + layer, 0))],
        out_specs=pl.BlockSpec((tm, D), lambda i, kk: (i, 0)), out_shape=jax.ShapeDtypeStruct((lp, D), F32),
        scratch_shapes=[pltpu.VMEM((tm, D), F32)], compiler_params=_params(("parallel", "arbitrary")),
    )(dg, du, wpk)


def _ffn_dw_down(act, dh, gpk, layer, row):
    lp = act.shape[0]
    tk = _tile(lp, 1408, 16)
    nk = lp // tk

    def body(a_ref, d_ref, g_in, g_out, acc, stage, sems):
        jp, kk = pl.program_id(0), pl.program_id(1)

        @pl.when(kk == 0)
        def _():
            acc[...] = jnp.zeros_like(acc)

        acc[...] += tn(a_ref[...], d_ref[...].astype(BF16))

        @pl.when(kk == nk - 1)
        def _():
            stage[...] = acc[...].astype(BF16)
            copies = [pltpu.make_async_copy(stage.at[pl.ds(hf * DOWN_ROWS, DOWN_ROWS), :],
                                            g_out.at[2 * jp + hf, pl.ds(row, DOWN_ROWS), :], sems.at[hf]) for hf in range(2)]
            for cp in copies:
                cp.start()
            for cp in copies:
                cp.wait()

    return pl.pallas_call(
        body, name=f"d_w_down{layer}", grid=(2, nk),
        in_specs=[pl.BlockSpec((tk, GU_ROWS), lambda jp, kk: (kk, jp)), pl.BlockSpec((tk, D), lambda jp, kk: (kk, 0)), ANY],
        out_specs=ANY, out_shape=jax.ShapeDtypeStruct(gpk.shape, gpk.dtype),
        scratch_shapes=[pltpu.VMEM((GU_ROWS, D), F32), pltpu.VMEM((GU_ROWS, D), BF16), pltpu.SemaphoreType.DMA((2,))],
        input_output_aliases={2: 0}, compiler_params=_params(("arbitrary", "arbitrary")),
    )(act, dh, gpk)


def _ffn_dw_gu(dg, du, yf, gpk, layer, blk):
    lp = dg.shape[0]
    tk = _tile(lp, 1408, 16)
    nk = lp // tk

    def body(dg_ref, du_ref, y_ref, g_in, o_ref, acc):
        c, kk = pl.program_id(0), pl.program_id(1)

        @pl.when(kk == 0)
        def _():
            acc[...] = jnp.zeros_like(acc)

        @pl.when(c < 2)
        def _():
            acc[...] += tn(dg_ref[...], y_ref[...])

        @pl.when(c >= 2)
        def _():
            acc[...] += tn(du_ref[...], y_ref[...])

        @pl.when(kk == nk - 1)
        def _():
            o_ref[...] = acc[...].astype(BF16)

    return pl.pallas_call(
        body, name=f"d_w_gate_up{layer}", grid=(4, nk),
        in_specs=[pl.BlockSpec((tk, GU_ROWS), lambda c, kk: (kk, jnp.minimum(c, 1))),
                  pl.BlockSpec((tk, GU_ROWS), lambda c, kk: (kk, jnp.maximum(c - 2, 0))),
                  pl.BlockSpec((tk, D), lambda c, kk: (kk, 0)), ANY],
        out_specs=_gu_spec(lambda c, kk: (c, blk, 0)),
        out_shape=jax.ShapeDtypeStruct(gpk.shape, gpk.dtype),
        scratch_shapes=[pltpu.VMEM((GU_ROWS, D), F32)], input_output_aliases={3: 0},
        compiler_params=_params(("parallel", "arbitrary")),
    )(dg, du, yf, gpk)


def _loss_head(h, target):
    lp = h.shape[0]
    nb = lp // ROW0

    def body(h_ref, t_ref, dh_ref, l_ref):
        i = pl.program_id(0)

        @pl.when(i == 0)
        def _():
            l_ref[...] = jnp.zeros_like(l_ref)
            dh_ref[...] = jnp.zeros_like(dh_ref)

        @pl.when(i > 0)
        def _():
            err = h_ref[...] - t_ref[...]
            dh_ref[...] = err * (1.0 / D)
            l_ref[...] += jnp.sum(err * err) * (0.5 / D)

    return pl.pallas_call(
        body, name="loss_head", grid=(nb,),
        in_specs=[pl.BlockSpec((ROW0, D), lambda i: (i, 0)), pl.BlockSpec((ROW0, D), lambda i: (jnp.maximum(i - 1, 0), 0))],
        out_specs=[pl.BlockSpec((ROW0, D), lambda i: (i, 0)), pl.BlockSpec((8, LANES), lambda i: (0, 0))],
        out_shape=[jax.ShapeDtypeStruct((lp, D), F32), jax.ShapeDtypeStruct((8, LANES), F32)],
        compiler_params=_params(("arbitrary",)),
    )(h, target)


def _adamw(w, g, m, v, *, name):
    shape = w.shape
    c = shape[-1]
    r = w.size // c
    w2, g2, m2, v2 = (t.reshape(r, c) for t in (w, g, m, v))
    tr = _tile(r, max(8, (1 << 19) // c), 8)

    def body(w_ref, g_ref, m_ref, v_ref, d_ref, nm_ref, nv_ref):
        gv = g_ref[...]
        nm = ADAM_B1 * m_ref[...] + (1.0 - ADAM_B1) * gv
        nv = ADAM_B2 * v_ref[...] + (1.0 - ADAM_B2) * (gv * gv)
        m_hat = nm / (1.0 - ADAM_B1 ** ADAM_STEP)
        v_hat = nv / (1.0 - ADAM_B2 ** ADAM_STEP)
        d_ref[...] = -ADAM_LR * (m_hat / (jnp.sqrt(v_hat) + ADAM_EPS) + ADAM_WD * w_ref[...])
        nm_ref[...] = nm
        nv_ref[...] = nv

    spec = pl.BlockSpec((tr, c), lambda i: (i, 0))
    outs = pl.pallas_call(
        body, name=name, grid=(r // tr,), in_specs=[spec] * 4, out_specs=[spec] * 3,
        out_shape=[jax.ShapeDtypeStruct((r, c), F32)] * 3, compiler_params=_params(("parallel",)),
    )(w2, g2, m2, v2)
    return tuple(o.reshape(shape) for o in outs)


def _adam_math(w, g, m, v):
    nm = ADAM_B1 * m + (1.0 - ADAM_B1) * g
    nv = ADAM_B2 * v + (1.0 - ADAM_B2) * (g * g)
    m_hat = nm / (1.0 - ADAM_B1 ** ADAM_STEP)
    v_hat = nv / (1.0 - ADAM_B2 ** ADAM_STEP)
    return -ADAM_LR * (m_hat / (jnp.sqrt(v_hat) + ADAM_EPS) + ADAM_WD * w), nm, nv


def _adamw_packed(w, gred0, gred, m, v, *, row0, row_off, transposed, name):
    nl, a, b = w.shape
    nr = b if transposed else a
    later = lambda l: row_off // nr + jnp.maximum(l - 1, 0)
    if transposed:
        ta = _tile(a, 256)
        wspec = pl.BlockSpec((1, ta, b), lambda l, r: (l, r, 0))
        g0spec = pl.BlockSpec((b, ta), lambda l, r: (row0 // nr, r))
        gspec = pl.BlockSpec((b, ta), lambda l, r: (later(l), r))
        grid = (nl, a // ta)
    else:
        wspec = pl.BlockSpec((1, a, b), lambda l, r: (l, 0, 0))
        g0spec = pl.BlockSpec((a, b), lambda l, r: (row0 // nr, 0))
        gspec = pl.BlockSpec((a, b), lambda l, r: (later(l), 0))
        grid = (nl, 1)

    def body(w_ref, g0_ref, g_ref, m_ref, v_ref, go_ref, d_ref, nm_ref, nv_ref):
        g = jnp.where(pl.program_id(0) == 0, g0_ref[...], g_ref[...])
        g = g.T if transposed else g
        d, nm, nv = _adam_math(w_ref[0], g, m_ref[0], v_ref[0])
        go_ref[0], d_ref[0], nm_ref[0], nv_ref[0] = g, d, nm, nv

    return pl.pallas_call(
        body, name=name, grid=grid, in_specs=[wspec, g0spec, gspec, wspec, wspec], out_specs=[wspec] * 4,
        out_shape=[jax.ShapeDtypeStruct(w.shape, F32)] * 4, compiler_params=_params(("parallel", "parallel")),
    )(w, gred0, gred, m, v)


FOX_AUG = FOX_H * LANES
L_C = 64
L_K = 67
L_LSE = 70
PAD_KEY = -30000.0
FOX_TQ = 384


def _head_sel(n_heads, width, lanes=LANES):
    r, c = _iota((n_heads * width, lanes), 0), _iota((n_heads * width, lanes), 1)
    down = (r // width == c).astype(BF16)
    r2, c2 = _iota((lanes, n_heads * width), 0), _iota((lanes, n_heads * width), 1)
    up = (c2 // width == r2).astype(BF16)
    return down, up


def _place(lane0):
    r, c = _iota((LANES, FOX_AUG), 0), _iota((LANES, FOX_AUG), 1)
    return [((c // LANES == r) & (c % LANES == lane0 + m)).astype(BF16) for m in range(3)]


def _placed(x, lane0):
    pcs = _split3(x)
    mats = _place(lane0)
    return nn(pcs[0], mats[0]) + nn(pcs[1], mats[1]) + nn(pcs[2], mats[2])


def _ones_at(rows, lanes):
    c = _iota((rows, FOX_AUG), 1) % LANES
    m = c == lanes[0]
    for l in lanes[1:]:
        m = m | (c == l)
    return m.astype(F32)


def _spread(x, extras, out_ref):
    rows = x.shape[0]
    left = _iota((rows, LANES), 1) < FOX_DH
    for p in range(FOX_H // 2):
        slab = x[:, p * LANES:(p + 1) * LANES]
        a = jnp.where(left, slab, extras[:, 2 * p * LANES:(2 * p + 1) * LANES])
        b = jnp.where(left, pltpu.roll(slab, FOX_DH, 1), extras[:, (2 * p + 1) * LANES:(2 * p + 2) * LANES])
        out_ref[:, 2 * p * LANES:(2 * p + 1) * LANES] = a.astype(BF16)
        out_ref[:, (2 * p + 1) * LANES:(2 * p + 2) * LANES] = b.astype(BF16)


def _fox_prep(proj, b_f, q_gain, k_gain):
    lp = proj.shape[0]
    nb = lp // LANES

    def body(p_ref, bf_ref, qg_ref, kg_ref, q_ref, k_ref, v_ref, carry):
        i = pl.program_id(0)

        @pl.when(i == 0)
        def _():
            carry[...] = jnp.zeros_like(carry)

        down, up = _head_sel(FOX_H, FOX_DH)

        def normed(x, gain):
            ms = _sel_r(x * x, down) * (1.0 / FOX_DH)
            r = _sel_r(lax.rsqrt(ms + EPS), up)
            return x * r * gain

        lane = _iota((LANES, LANES), 1)
        lf = jnp.where(lane < FOX_H, _log_sigmoid(p_ref[:, 4 * D:4 * D + LANES] + bf_ref[...]), 0.0)
        c = _sel_l(_tri(LANES).astype(BF16), lf) + carry[0:1, :]
        carry[...] = jnp.broadcast_to(c[LANES - 1:LANES, :], carry.shape)
        q_extra = _placed(c, L_C) + _ones_at(LANES, (L_K, L_K + 1, L_K + 2))
        row = i * LANES + _iota((LANES, FOX_AUG), 0)
        lane_a = _iota((LANES, FOX_AUG), 1) % LANES
        k_extra = -_placed(c, L_K) + _ones_at(LANES, (L_C, L_C + 1, L_C + 2, L_LSE, L_LSE + 1, L_LSE + 2))
        pad_val = jnp.where(lane_a == L_K, PAD_KEY, 0.0)
        k_extra = jnp.where((row < META0) & (lane_a >= L_K) & (lane_a < L_K + 3), pad_val, k_extra)
        v_extra = _ones_at(LANES, (L_C, L_C + 1, L_C + 2))
        _spread(normed(p_ref[:, 0:D], qg_ref[...]) * (FOX_DH ** -0.5), q_extra, q_ref)
        _spread(normed(p_ref[:, D:2 * D], kg_ref[...]), k_extra, k_ref)
        _spread(p_ref[:, 2 * D:3 * D], v_extra, v_ref)

    row = pl.BlockSpec((1, D), lambda i: (0, 0))
    aug = pl.BlockSpec((LANES, FOX_AUG), lambda i: (i, 0))
    return pl.pallas_call(
        body, name="fox_prep", grid=(nb,),
        in_specs=[pl.BlockSpec((LANES, FOX_INP), lambda i: (i, 0)), pl.BlockSpec((1, LANES), lambda i: (0, 0)), row, row],
        out_specs=[aug] * 3, out_shape=[jax.ShapeDtypeStruct((lp, FOX_AUG), BF16)] * 3,
        scratch_shapes=[pltpu.VMEM((8, LANES), F32)],
        compiler_params=_params(("arbitrary",)),
    )(proj, jnp.pad(b_f, (0, LANES - FOX_H)).reshape(1, LANES), jnp.tile(q_gain, FOX_H).reshape(1, D),
      jnp.tile(k_gain, FOX_H).reshape(1, D))


def _fox_attn_fwd(qa, ka, va, proj, ag=None):
    lp = qa.shape[0]
    tq = _tile(lp, FOX_TQ)
    nq = lp // tq
    npair = FOX_H // 2

    def body(q_ref, k_ref, v_ref, gate_ref, *rest):
        if ag is None:
            o_ref, og_ref, lse_ref = rest
        else:
            _, o_ref, og_ref, lse_ref, w_out, send_sems, recv_sems = rest
            copies = _AgCopies(w_out, ag[1], send_sems, recv_sems)

            @pl.when((pl.program_id(0) == 0) & (pl.program_id(1) == 0))
            def _():
                for r, k in copies.pairs():
                    copies.ici(r, k).start()

        i = pl.program_id(1)
        causal = _iota((tq, tq), 1) <= _iota((tq, tq), 0)
        qs = [q_ref[:, hh * LANES:(hh + 1) * LANES] for hh in range(2)]

        def block(j, carry, diag):
            off = pl.multiple_of(j * tq, tq)
            out = []
            for hh in range(2):
                m, acc = carry[hh]
                k = k_ref[pl.ds(off, tq), hh * LANES:(hh + 1) * LANES]
                v = v_ref[pl.ds(off, tq), hh * LANES:(hh + 1) * LANES]
                s = nt(qs[hh], k)
                if diag:
                    s = jnp.where(causal, s, -1e30)
                m2 = jnp.maximum(m, jnp.max(s, axis=-1, keepdims=True))
                p = jnp.exp(s - m2)
                p_hi = p.astype(BF16)
                p_lo = (p - p_hi.astype(F32)).astype(BF16)
                out.append((m2, jnp.exp(m - m2) * acc + nn(p_hi, v) + nn(p_lo, v)))
            return tuple(out)

        init = tuple((jnp.full((tq, 1), -1e30, F32), jnp.zeros((tq, LANES), F32)) for _ in range(2))
        carry = lax.fori_loop(0, i, lambda j, c: block(j, c, False), init)
        carry = block(i, carry, True)
        outs, lses = [], []
        for hh in range(2):
            m, acc = carry[hh]
            l = acc[:, L_C:L_C + 1]
            outs.append(acc / l)
            lses.append(jnp.broadcast_to(m + jnp.log(l), (tq, LANES)))
        left = _iota((tq, LANES), 1) < FOX_DH
        o = jnp.where(left, outs[0], pltpu.roll(outs[1], FOX_DH, 1))
        o_ref[...] = o
        og_ref[...] = (o * _sigmoid(gate_ref[...])).astype(BF16)
        lse_ref[...] = jnp.where(left, lses[0], lses[1])

        if ag is not None:
            @pl.when((pl.program_id(0) == npair - 1) & (pl.program_id(1) == nq - 1))
            def _():
                for r, k in copies.pairs():
                    copies.ici_arrival(r, k).wait_recv()
                for r, k in copies.pairs():
                    copies.ici(r, k).wait_send()

    qspec = pl.BlockSpec((tq, 2 * LANES), lambda p, i: (i, p))
    kspec = pl.BlockSpec((lp, 2 * LANES), lambda p, i: (0, p))
    ospec = pl.BlockSpec((tq, LANES), lambda p, i: (i, p))
    ins, in_specs = [qa, ka, va, proj], [qspec, kspec, kspec, pl.BlockSpec((tq, LANES), lambda p, i: (i, 3 * D // LANES + p))]
    out_specs = [ospec] * 3
    out_shape = [jax.ShapeDtypeStruct((lp, D), F32), jax.ShapeDtypeStruct((lp, D), BF16), jax.ShapeDtypeStruct((lp, D), F32)]
    if ag is None:
        return pl.pallas_call(body, name="fox_attn_fwd", grid=(npair, nq), in_specs=in_specs, out_specs=out_specs,
                              out_shape=out_shape, compiler_params=_params(("parallel", "arbitrary")))(*ins)
    n = 3 * len(ag[1])
    return pl.pallas_call(
        body, name="fox_attn_fwd_ag", grid=(npair, nq), in_specs=in_specs + [ANY], out_specs=out_specs + [ANY],
        out_shape=out_shape + [jax.ShapeDtypeStruct(ag[0].shape, ag[0].dtype)],
        scratch_shapes=[pltpu.SemaphoreType.DMA((n,))] * 2, input_output_aliases={4: 3},
        compiler_params=_params(("arbitrary", "arbitrary")),
    )(*ins, ag[0])


def _fox_gate_bwd(dog, o, proj, lse, qa):
    lp = o.shape[0]
    tr = LANES

    def body(d_ref, o_ref, g_ref, lse_ref, q_ref, do_ref, q2_ref, dgate_ref):
        down, _ = _head_sel(FOX_H, FOX_DH)
        sg = _sigmoid(g_ref[...])
        dv, ov = d_ref[...], o_ref[...]
        do = (dv * sg).astype(BF16).astype(F32)
        dgate_ref[...] = dv * ov * sg * (1.0 - sg)
        delta = _sel_r(do * ov, down)
        _spread(do, -_placed(delta, L_C), do_ref)
        r_, c_ = _iota((D, LANES), 0), _iota((D, LANES), 1)
        lse_c = _sel_r(lse_ref[...], (r_ == c_ * FOX_DH).astype(BF16))
        q2_ref[...] = (q_ref[...].astype(F32) - _placed(lse_c, L_LSE)).astype(BF16)

    spec = pl.BlockSpec((tr, D), lambda i: (i, 0))
    aug = pl.BlockSpec((tr, FOX_AUG), lambda i: (i, 0))
    return pl.pallas_call(
        body, name="fox_gate_bwd", grid=(lp // tr,),
        in_specs=[spec, spec, pl.BlockSpec((tr, D), lambda i: (i, 3)), spec, aug], out_specs=[aug, aug, spec],
        out_shape=[jax.ShapeDtypeStruct((lp, FOX_AUG), BF16), jax.ShapeDtypeStruct((lp, FOX_AUG), BF16),
                   jax.ShapeDtypeStruct((lp, D), F32)],
        compiler_params=_params(("parallel",)),
    )(dog, o, proj, lse, qa)


def _fox_attn_bwd(q2, ka, va, doa, rs=None):
    lp = q2.shape[0]
    t = _tile(lp, FOX_TQ)
    nb = lp // t
    npair = FOX_H // 2

    def body(q_ref, k_ref, v_ref, do_ref, *rest):
        if rs is None:
            dq_ref, dk_ref, dv_ref, dc_ref, dq_acc, dk_acc, dv_acc, dc_acc = rest
        else:
            s_ref, dq_ref, dk_ref, dv_ref, dc_ref, got_ref, dq_acc, dk_acc, dv_acc, dc_acc, send_sems, recv_sems = rest
            sends, arrivals = _scatter_copies(s_ref, got_ref, send_sems, recv_sems)

            @pl.when((pl.program_id(0) == 0) & (pl.program_id(1) == 0))
            def _():
                for cp in sends:
                    cp.start()

            @pl.when((pl.program_id(0) == npair - 1) & (pl.program_id(1) == nb - 1))
            def _():
                for cp in arrivals:
                    cp.wait_recv()
                for cp in sends:
                    cp.wait_send()

        j = pl.program_id(1)

        @pl.when(j == 0)
        def _():
            dq_acc[...] = jnp.zeros_like(dq_acc)

        causal = _iota((t, t), 1) <= _iota((t, t), 0)
        ks = [k_ref[:, hh * LANES:(hh + 1) * LANES] for hh in range(2)]
        vs = [v_ref[:, hh * LANES:(hh + 1) * LANES] for hh in range(2)]
        dk_acc[...] = jnp.zeros_like(dk_acc)
        dv_acc[...] = jnp.zeros_like(dv_acc)
        dc_acc[...] = jnp.zeros_like(dc_acc)

        def block(i, diag):
            off = pl.multiple_of(i * t, t)
            for hh in range(2):
                q = q_ref[pl.ds(off, t), hh * LANES:(hh + 1) * LANES]
                do = do_ref[pl.ds(off, t), hh * LANES:(hh + 1) * LANES]
                s = nt(q, ks[hh])
                if diag:
                    s = jnp.where(causal, s, -1e30)
                p = jnp.exp(s)
                ds = p * nt(do, vs[hh])
                dc_acc[hh] += jnp.sum(ds, axis=0, keepdims=True)
                dsb = ds.astype(BF16)
                dv_acc[hh] += tn(p.astype(BF16), do)
                dk_acc[hh] += tn(dsb, q)
                dq_acc[hh, pl.ds(off, t), :] += nn(dsb, ks[hh])

        block(j, True)

        def step(i, c):
            block(i, False)
            return c

        lax.fori_loop(j + 1, nb, step, 0)
        left = _iota((t, LANES), 1) < FOX_DH
        dk_ref[...] = jnp.where(left, dk_acc[0], pltpu.roll(dk_acc[1], FOX_DH, 1))
        dv_ref[...] = jnp.where(left, dv_acc[0], pltpu.roll(dv_acc[1], FOX_DH, 1))
        for hh in range(2):
            dc_ref[hh] = jnp.broadcast_to(-dc_acc[hh], (8, t))

        @pl.when(j == nb - 1)
        def _():
            left = _iota((lp, LANES), 1) < FOX_DH
            dq_ref[...] = jnp.where(left, dq_acc[0], pltpu.roll(dq_acc[1], FOX_DH, 1))

    full = pl.BlockSpec((lp, 2 * LANES), lambda p, j: (0, p))
    kblk = pl.BlockSpec((t, 2 * LANES), lambda p, j: (j, p))
    oblk = pl.BlockSpec((t, LANES), lambda p, j: (j, p))
    in_specs = [full, kblk, kblk, full]
    out_specs = [pl.BlockSpec((lp, LANES), lambda p, j: (0, p)), oblk, oblk, pl.BlockSpec((2, 8, t), lambda p, j: (p, 0, j))]
    out_shape = [jax.ShapeDtypeStruct((lp, D), F32)] * 3 + [jax.ShapeDtypeStruct((FOX_H, 8, lp), F32)]
    scratch = [pltpu.VMEM((2, lp, LANES), F32), pltpu.VMEM((2, t, LANES), F32), pltpu.VMEM((2, t, LANES), F32),
               pltpu.VMEM((2, 1, t), F32)]
    if rs is None:
        return pl.pallas_call(body, name="fox_attn_bwd", grid=(npair, nb), in_specs=in_specs, out_specs=out_specs,
                              out_shape=out_shape, scratch_shapes=scratch,
                              compiler_params=_params(("parallel", "arbitrary")))(q2, ka, va, doa)
    return pl.pallas_call(
        body, name="fox_attn_bwd_rs", grid=(npair, nb), in_specs=in_specs + [ANY], out_specs=out_specs + [ANY],
        out_shape=out_shape + [jax.ShapeDtypeStruct((3,) + rs.shape[1:], rs.dtype)],
        scratch_shapes=scratch + [pltpu.SemaphoreType.DMA((3,)), pltpu.SemaphoreType.DMA((3,))],
        compiler_params=_params(("arbitrary", "arbitrary")),
    )(q2, ka, va, doa, rs)


def _fox_prep_bwd(proj, b_f, q_gain, k_gain, dqn, dkn, dv, dgate, dct):
    lp = proj.shape[0]
    nb = lp // LANES

    def body(p_ref, bf_ref, qg_ref, kg_ref, dq_ref, dk_ref, dv_ref, dg_ref, dc_ref,
             dp_ref, dqg_ref, dkg_ref, dbf_ref, carry):
        i = pl.program_id(0)

        @pl.when(i == 0)
        def _():
            carry[...] = jnp.zeros_like(carry)
            dqg_ref[...] = jnp.zeros_like(dqg_ref)
            dkg_ref[...] = jnp.zeros_like(dkg_ref)
            dbf_ref[...] = jnp.zeros_like(dbf_ref)

        down, up = _head_sel(FOX_H, FOX_DH)

        def norm_bwd(x, gain, dy, scale, dgain_ref):
            ms = _sel_r(x * x, down) * (1.0 / FOX_DH)
            r = _sel_r(lax.rsqrt(ms + EPS), up)
            u = dy * gain * scale
            mean_xu = _sel_r(_sel_r(x * u, down) * (1.0 / FOX_DH), up)
            dgain_ref[...] += jnp.sum(dy * scale * x * r, axis=0, keepdims=True)
            return r * u - x * (r * r * r) * mean_xu

        dp_ref[:, 0:D] = norm_bwd(p_ref[:, 0:D], qg_ref[...], dq_ref[...], FOX_DH ** -0.5, dqg_ref).astype(BF16)
        dp_ref[:, D:2 * D] = norm_bwd(p_ref[:, D:2 * D], kg_ref[...], dk_ref[...], 1.0, dkg_ref).astype(BF16)
        dp_ref[:, 2 * D:3 * D] = dv_ref[...].astype(BF16)
        dp_ref[:, 3 * D:4 * D] = dg_ref[...].astype(BF16)
        rows = jnp.concatenate([dc_ref[h, 0:1, :] for h in range(FOX_H)] + [jnp.zeros((LANES - FOX_H, LANES), F32)], axis=0)
        dlf = _sel_l(_tri(LANES, upper=True).astype(BF16), rows.T) + carry[0:1, :]
        carry[...] = jnp.broadcast_to(dlf[0:1, :], carry.shape)
        lane = _iota((LANES, LANES), 1)
        z = p_ref[:, 4 * D:4 * D + LANES] + bf_ref[...]
        df = jnp.where(lane < FOX_H, dlf * _sigmoid(-z), 0.0)
        dp_ref[:, 4 * D:4 * D + LANES] = df.astype(BF16)
        dbf_ref[...] += jnp.sum(df, axis=0, keepdims=True)

    rev = lambda i: (nb - 1 - i, 0)
    blk = pl.BlockSpec((LANES, D), rev)
    row = pl.BlockSpec((1, D), lambda i: (0, 0))
    row128 = pl.BlockSpec((1, LANES), lambda i: (0, 0))
    return pl.pallas_call(
        body, name="fox_prep_bwd", grid=(nb,),
        in_specs=[pl.BlockSpec((LANES, FOX_INP), rev), row128, row, row, blk, blk, blk, blk,
                  pl.BlockSpec((FOX_H, 8, LANES), lambda i: (0, 0, nb - 1 - i))],
        out_specs=[pl.BlockSpec((LANES, FOX_INP), rev), row, row, row128],
        out_shape=[jax.ShapeDtypeStruct((lp, FOX_INP), BF16), jax.ShapeDtypeStruct((1, D), F32),
                   jax.ShapeDtypeStruct((1, D), F32), jax.ShapeDtypeStruct((1, LANES), F32)],
        scratch_shapes=[pltpu.VMEM((8, LANES), F32)],
        compiler_params=_params(("arbitrary",)),
    )(proj, jnp.pad(b_f, (0, LANES - FOX_H)).reshape(1, LANES), jnp.tile(q_gain, FOX_H).reshape(1, D),
      jnp.tile(k_gain, FOX_H).reshape(1, D), dqn, dkn, dv, dgate, dct)


def _gla_gates(p_ref, wa_ref, ba_ref):
    a_lr = p_ref[:, 3072:3072 + LANES]
    z = nn(a_lr.astype(BF16), wa_ref[...].astype(BF16)) + ba_ref[...]
    g = _log_sigmoid(z) * (1.0 / GLA_NORM)
    b = _sel_l(_tri(CHUNK).astype(BF16), g)
    return a_lr, z, b


def _gla_head_fwd(q, k, v, b, st0):
    eb = jnp.exp(b)
    bl = b[CHUNK - 1:CHUNK, :]
    qe, ke, kd = q * eb, k * jnp.exp(-b), k * jnp.exp(bl - b)
    a = jnp.where(_tri(CHUNK), nt(qe, ke), 0.0)
    o = nn(a, v) + nt(qe, st0)
    st1 = st0 * jnp.exp(bl) + tn(v, kd)
    return o, st1, (qe, ke, kd, a, bl)


def _gla_fwd(proj, w_alpha2, b_alpha, o_gain):
    lp = proj.shape[0]
    nc = lp // CHUNK

    def body(p_ref, wa_ref, ba_ref, og_ref, o_ref, y_ref, s_ref, st):
        @pl.when(pl.program_id(0) == 0)
        def _():
            st[...] = jnp.zeros_like(st)

        _, _, b = _gla_gates(p_ref, wa_ref, ba_ref)
        for h in range(GLA_H):
            q = p_ref[:, h * GLA_DK:(h + 1) * GLA_DK] * (GLA_DK ** -0.5)
            k = p_ref[:, GLA_QK + h * GLA_DK:GLA_QK + (h + 1) * GLA_DK]
            v = p_ref[:, 2 * GLA_QK + h * GLA_DV:2 * GLA_QK + (h + 1) * GLA_DV]
            r = p_ref[:, 2 * GLA_QK + GLA_V + h * GLA_DV:2 * GLA_QK + GLA_V + (h + 1) * GLA_DV]
            st0 = st[h]
            s_ref[0, h] = st0
            o, st1, _ = _gla_head_fwd(q, k, v, b[:, h * GLA_DK:(h + 1) * GLA_DK], st0)
            st[h] = st1
            o_ref[:, h * GLA_DV:(h + 1) * GLA_DV] = o
            rs = lax.rsqrt(jnp.mean(o * o, axis=-1, keepdims=True) + EPS)
            y_ref[:, h * GLA_DV:(h + 1) * GLA_DV] = (o * rs * og_ref[...] * _silu(r)).astype(BF16)

    blk = pl.BlockSpec((CHUNK, D), lambda i: (i, 0))
    return pl.pallas_call(
        body, name="gla_fwd", grid=(nc,),
        in_specs=[pl.BlockSpec((CHUNK, GLA_INP), lambda i: (i, 0)), pl.BlockSpec((LANES, GLA_QK), lambda i: (0, 0)),
                  pl.BlockSpec((1, GLA_QK), lambda i: (0, 0)), pl.BlockSpec((1, GLA_DV), lambda i: (0, 0))],
        out_specs=[blk, blk, pl.BlockSpec((1, GLA_H, GLA_DV, GLA_DK), lambda i: (i, 0, 0, 0))],
        out_shape=[jax.ShapeDtypeStruct((lp, D), F32), jax.ShapeDtypeStruct((lp, D), BF16),
                   jax.ShapeDtypeStruct((nc, GLA_H, GLA_DV, GLA_DK), F32)],
        scratch_shapes=[pltpu.VMEM((GLA_H, GLA_DV, GLA_DK), F32)],
        compiler_params=_params(("arbitrary",)),
    )(proj, jnp.pad(w_alpha2, ((0, LANES - GLA_RANK), (0, 0))), b_alpha.reshape(1, GLA_QK), o_gain.reshape(1, GLA_DV))


def _gla_bwd(proj, w_alpha2, b_alpha, o_gain, o, states, dy):
    lp = proj.shape[0]
    nc = lp // CHUNK

    def body(p_ref, wa_ref, ba_ref, og_ref, o_ref, s_ref, dy_ref, dp_ref, dwa_ref, dba_ref, dog_ref, dst):
        @pl.when(pl.program_id(0) == 0)
        def _():
            dst[...] = jnp.zeros_like(dst)
            dwa_ref[...] = jnp.zeros_like(dwa_ref)
            dba_ref[...] = jnp.zeros_like(dba_ref)
            dog_ref[...] = jnp.zeros_like(dog_ref)

        a_lr, z, b_all = _gla_gates(p_ref, wa_ref, ba_ref)
        last_row = _iota((CHUNK, GLA_DK), 0) == CHUNK - 1
        rev = _tri(CHUNK, upper=True).astype(BF16)
        dg_parts = []
        for h in range(GLA_H):
            scale = GLA_DK ** -0.5
            q = p_ref[:, h * GLA_DK:(h + 1) * GLA_DK] * scale
            k = p_ref[:, GLA_QK + h * GLA_DK:GLA_QK + (h + 1) * GLA_DK]
            v = p_ref[:, 2 * GLA_QK + h * GLA_DV:2 * GLA_QK + (h + 1) * GLA_DV]
            r = p_ref[:, 2 * GLA_QK + GLA_V + h * GLA_DV:2 * GLA_QK + GLA_V + (h + 1) * GLA_DV]
            b = b_all[:, h * GLA_DK:(h + 1) * GLA_DK]
            st0 = s_ref[0, h]
            dst1 = dst[h]
            ov = o_ref[:, h * GLA_DV:(h + 1) * GLA_DV]
            dyv = dy_ref[:, h * GLA_DV:(h + 1) * GLA_DV]
            rs = lax.rsqrt(jnp.mean(ov * ov, axis=-1, keepdims=True) + EPS)
            on = ov * rs
            dr = dyv * on * og_ref[...] * _dsilu(r)
            don = dyv * _silu(r)
            dog_ref[...] += jnp.sum(don * on, axis=0, keepdims=True)
            u = don * og_ref[...]
            do = rs * u - ov * (rs * rs * rs) * jnp.mean(ov * u, axis=-1, keepdims=True)
            eb = jnp.exp(b)
            _, _, (qe, ke, kd, a, bl) = _gla_head_fwd(q, k, v, b, st0)
            da = jnp.where(_tri(CHUNK), nt(do, v), 0.0)
            dkd = nn(v, dst1)
            dvv = tn(a, do) + nt(kd, dst1)
            dqe = nn(da, ke) + nn(do, st0)
            dke = tn(da, qe)
            ebl = jnp.exp(bl)
            dst[h] = dst1 * ebl + tn(do, qe)
            db = dqe * qe - dke * ke - dkd * kd
            db_last = jnp.sum(dkd * kd, axis=0, keepdims=True) + jnp.sum(dst1 * st0, axis=0, keepdims=True) * ebl
            db = db + jnp.where(last_row, db_last, 0.0)
            dg_parts.append(_sel_l(rev, db))
            dp_ref[:, h * GLA_DK:(h + 1) * GLA_DK] = (dqe * eb * scale).astype(BF16)
            dp_ref[:, GLA_QK + h * GLA_DK:GLA_QK + (h + 1) * GLA_DK] = (dke * jnp.exp(-b) + dkd * jnp.exp(bl - b)).astype(BF16)
            dp_ref[:, 2 * GLA_QK + h * GLA_DV:2 * GLA_QK + (h + 1) * GLA_DV] = dvv.astype(BF16)
            dp_ref[:, 2 * GLA_QK + GLA_V + h * GLA_DV:2 * GLA_QK + GLA_V + (h + 1) * GLA_DV] = dr.astype(BF16)
        dg = jnp.concatenate(dg_parts, axis=1)
        dz = dg * (1.0 / GLA_NORM) * _sigmoid(-z)
        dzb = dz.astype(BF16)
        dp_ref[:, 3072:3072 + LANES] = nt(dzb, wa_ref[...].astype(BF16)).astype(BF16)
        dwa_ref[...] += tn(a_lr.astype(BF16), dzb)
        dba_ref[...] += jnp.sum(dz, axis=0, keepdims=True)

    rv = lambda i: (nc - 1 - i, 0)
    blk = pl.BlockSpec((CHUNK, D), rv)
    fixed = lambda r, c: pl.BlockSpec((r, c), lambda i: (0, 0))
    return pl.pallas_call(
        body, name="gla_bwd", grid=(nc,),
        in_specs=[pl.BlockSpec((CHUNK, GLA_INP), rv), fixed(LANES, GLA_QK), fixed(1, GLA_QK), fixed(1, GLA_DV), blk,
                  pl.BlockSpec((1, GLA_H, GLA_DV, GLA_DK), lambda i: (nc - 1 - i, 0, 0, 0)), blk],
        out_specs=[pl.BlockSpec((CHUNK, GLA_INP), rv), fixed(LANES, GLA_QK), fixed(1, GLA_QK), fixed(1, GLA_DV)],
        out_shape=[jax.ShapeDtypeStruct((lp, GLA_INP), BF16), jax.ShapeDtypeStruct((LANES, GLA_QK), F32),
                   jax.ShapeDtypeStruct((1, GLA_QK), F32), jax.ShapeDtypeStruct((1, GLA_DV), F32)],
        scratch_shapes=[pltpu.VMEM((GLA_H, GLA_DV, GLA_DK), F32)],
        compiler_params=_params(("arbitrary",)),
    )(proj, jnp.pad(w_alpha2, ((0, LANES - GLA_RANK), (0, 0))), b_alpha.reshape(1, GLA_QK), o_gain.reshape(1, GLA_DV),
      o, states, dy)


HI = lax.Precision.HIGHEST


def _gdn_pre(prev_ref, p_ref, cw_ref, al_ref, dt_ref):
    xc = jnp.concatenate([prev_ref[:, 0:GDN_CONV], p_ref[:, 0:GDN_CONV]], axis=0)
    shifted = [pltpu.roll(xc, 3 - j, 0)[CHUNK:, :] if j < 3 else xc[CHUNK:, :] for j in range(4)]
    conv = sum(shifted[j] * cw_ref[j:j + 1, :] for j in range(4))
    act = _silu(conv)
    slab = p_ref[:, 4096:4096 + LANES]
    lane = _iota((CHUNK, LANES), 1)
    zs = slab + dt_ref[...]
    g = jnp.where(lane < GDN_H, -jnp.exp(al_ref[...]) * _softplus(zs), 0.0)
    bs = _sel_l(_tri(CHUNK).astype(BF16), g)
    beta = _sigmoid(slab)
    return shifted, conv, act, slab, zs, g, bs, beta


def _l2n(x):
    r = lax.rsqrt(jnp.sum(x * x, axis=-1, keepdims=True) + EPS)
    return x * r, r


def _gdn_chunk_fwd(q, k, v, beta, bcol, brow, s0):
    hs = range(len(q))
    ii, jj = _iota((CHUNK, CHUNK), 0), _iota((CHUNK, CHUNK), 1)
    low, eye = ii >= jj, (ii == jj).astype(F32)
    dm = [jnp.where(low, jnp.exp(jnp.where(low, bcol[h] - brow[h], 0.0)), 0.0) for h in hs]
    dstrict = [jnp.where(ii > jj, dm[h], 0.0) for h in hs]
    eb = [jnp.exp(bcol[h]) for h in hs]
    bl = [bcol[h][CHUNK - 1:CHUNK, :] for h in hs]
    kb = [k[h] * beta[h] for h in hs]
    vb = [v[h] * beta[h] for h in hs]
    nmat = [nt(kb[h], k[h]) * dstrict[h] for h in hs]
    x = [eye - nmat[h] for h in hs]
    pw = [nn(nmat[h], nmat[h], precision=HI) for h in hs]
    for it in range(5):
        x = [x[h] + nn(x[h], pw[h], precision=HI) for h in hs]
        if it < 4:
            pw = [nn(pw[h], pw[h], precision=HI) for h in hs]
    kbe = [kb[h] * eb[h] for h in hs]
    u = [nn(x[h], vb[h], precision=HI) for h in hs]
    w = [nn(x[h], kbe[h], precision=HI) for h in hs]
    vn = [u[h] - nn(w[h], s0[h]) for h in hs]
    pm = [nt(q[h], k[h]) * dm[h] for h in hs]
    qe = [q[h] * eb[h] for h in hs]
    o = [nn(pm[h], vn[h]) + nn(qe[h], s0[h]) for h in hs]
    kd = [k[h] * jnp.exp(bl[h] - bcol[h]) for h in hs]
    s1 = [s0[h] * jnp.exp(bl[h]) + tn(kd[h], vn[h]) for h in hs]
    return o, s1, dict(dm=dm, dstrict=dstrict, eb=eb, bl=bl, kb=kb, vb=vb, nmat=nmat, tinv=x, kbe=kbe, u=u, w=w, vn=vn,
                       pm=pm, qe=qe, kd=kd)


def _gdn_heads(act, beta_slab, bs, h):
    qa = act[:, h * GDN_DK:(h + 1) * GDN_DK]
    ka = act[:, GDN_H * GDN_DK + h * GDN_DK:GDN_H * GDN_DK + (h + 1) * GDN_DK]
    v = act[:, 2 * GDN_H * GDN_DK + h * GDN_DV:2 * GDN_H * GDN_DK + (h + 1) * GDN_DV]
    return qa, ka, v, beta_slab[:, GDN_H + h:GDN_H + h + 1], bs[:, h:h + 1]


def _gdn_fwd(proj, conv_w, a_log, dt_bias, o_gain):
    lp = proj.shape[0]
    nc = lp // CHUNK

    def body(prev_ref, p_ref, cw_ref, al_ref, dt_ref, og_ref, o_ref, y_ref, s_ref, st):
        @pl.when(pl.program_id(0) == 0)
        def _():
            st[...] = jnp.zeros_like(st)

        _, _, act, _, _, _, bs, beta = _gdn_pre(prev_ref, p_ref, cw_ref, al_ref, dt_ref)
        bst = bs.T
        hs = range(GDN_H)
        parts = [_gdn_heads(act, beta, bs, h) for h in hs]
        q = [_l2n(parts[h][0])[0] * (GDN_DK ** -0.5) for h in hs]
        k = [_l2n(parts[h][1])[0] for h in hs]
        s0 = [st[h] for h in hs]
        for h in hs:
            s_ref[0, h] = s0[h]
        o, s1, _ = _gdn_chunk_fwd(q, k, [parts[h][2] for h in hs], [parts[h][3] for h in hs], [parts[h][4] for h in hs],
                                  [bst[h:h + 1, :] for h in hs], s0)
        for h in hs:
            st[h] = s1[h]
            o_ref[:, h * GDN_DV:(h + 1) * GDN_DV] = o[h]
            rs = lax.rsqrt(jnp.mean(o[h] * o[h], axis=-1, keepdims=True) + EPS)
            gate = p_ref[:, GDN_CONV + h * GDN_DV:GDN_CONV + (h + 1) * GDN_DV]
            y_ref[:, h * GDN_DV:(h + 1) * GDN_DV] = (o[h] * rs * og_ref[...] * _silu(gate)).astype(BF16)

    blk = pl.BlockSpec((CHUNK, D), lambda i: (i, 0))
    fixed = lambda r, c: pl.BlockSpec((r, c), lambda i: (0, 0))
    return pl.pallas_call(
        body, name="gdn_fwd", grid=(nc,),
        in_specs=[pl.BlockSpec((CHUNK, GDN_INP), lambda i: (jnp.maximum(i - 1, 0), 0)),
                  pl.BlockSpec((CHUNK, GDN_INP), lambda i: (i, 0)), fixed(8, GDN_CONV), fixed(1, LANES), fixed(1, LANES),
                  fixed(1, GDN_DV)],
        out_specs=[blk, blk, pl.BlockSpec((1, GDN_H, GDN_DK, GDN_DV), lambda i: (i, 0, 0, 0))],
        out_shape=[jax.ShapeDtypeStruct((lp, D), F32), jax.ShapeDtypeStruct((lp, D), BF16),
                   jax.ShapeDtypeStruct((nc, GDN_H, GDN_DK, GDN_DV), F32)],
        scratch_shapes=[pltpu.VMEM((GDN_H, GDN_DK, GDN_DV), F32)],
        compiler_params=_params(("arbitrary",)),
    )(proj, proj, jnp.pad(conv_w.reshape(4, GDN_CONV), ((0, 4), (0, 0))), jnp.pad(a_log, (0, LANES - GDN_H)).reshape(1, LANES),
      jnp.pad(dt_bias, (0, LANES - GDN_H)).reshape(1, LANES), o_gain.reshape(1, GDN_DV))


def _gdn_bwd(proj, conv_w, a_log, dt_bias, o_gain, o, states, dy):
    lp = proj.shape[0]
    nc = lp // CHUNK

    def body(prev_ref, p_ref, cw_ref, al_ref, dt_ref, og_ref, o_ref, s_ref, dy_ref,
             dp_ref, dcw_ref, dal_ref, ddt_ref, dog_ref, dst, dconv_next):
        @pl.when(pl.program_id(0) == 0)
        def _():
            dst[...] = jnp.zeros_like(dst)
            dconv_next[...] = jnp.zeros_like(dconv_next)
            dcw_ref[...] = jnp.zeros_like(dcw_ref)
            dal_ref[...] = jnp.zeros_like(dal_ref)
            ddt_ref[...] = jnp.zeros_like(ddt_ref)
            dog_ref[...] = jnp.zeros_like(dog_ref)

        shifted, conv, act, slab, zs, g, bs, beta = _gdn_pre(prev_ref, p_ref, cw_ref, al_ref, dt_ref)
        bst = bs.T
        lane = _iota((CHUNK, LANES), 1)
        ones = jnp.ones((CHUNK, LANES), F32)
        db_slab = jnp.zeros((CHUNK, LANES), F32)
        dbeta_slab = jnp.zeros((CHUNK, LANES), F32)
        last_row = _iota((CHUNK, 1), 0) == CHUNK - 1
        hs = range(GDN_H)
        scale = GDN_DK ** -0.5
        parts = [_gdn_heads(act, beta, bs, h) for h in hs]
        qa, ka, v = [parts[h][0] for h in hs], [parts[h][1] for h in hs], [parts[h][2] for h in hs]
        bet, bcol = [parts[h][3] for h in hs], [parts[h][4] for h in hs]
        qn_ = [_l2n(qa[h]) for h in hs]
        kn_ = [_l2n(ka[h]) for h in hs]
        q = [qn_[h][0] * scale for h in hs]
        k, rq, rk = [kn_[h][0] for h in hs], [qn_[h][1] for h in hs], [kn_[h][1] for h in hs]
        s0 = [s_ref[0, h] for h in hs]
        ds1 = [dst[h] for h in hs]
        do = []
        for h in hs:
            ov = o_ref[:, h * GDN_DV:(h + 1) * GDN_DV]
            dyv = dy_ref[:, h * GDN_DV:(h + 1) * GDN_DV]
            gate = p_ref[:, GDN_CONV + h * GDN_DV:GDN_CONV + (h + 1) * GDN_DV]
            rs = lax.rsqrt(jnp.mean(ov * ov, axis=-1, keepdims=True) + EPS)
            on = ov * rs
            dp_ref[:, GDN_CONV + h * GDN_DV:GDN_CONV + (h + 1) * GDN_DV] = (dyv * on * og_ref[...] * _dsilu(gate)).astype(BF16)
            don = dyv * _silu(gate)
            dog_ref[...] += jnp.sum(don * on, axis=0, keepdims=True)
            uu = don * og_ref[...]
            do.append(rs * uu - ov * (rs * rs * rs) * jnp.mean(ov * uu, axis=-1, keepdims=True))
        _, _, f = _gdn_chunk_fwd(q, k, v, bet, bcol, [bst[h:h + 1, :] for h in hs], s0)
        dm, dstrict, eb, bl, kb, nmat, tinv = f["dm"], f["dstrict"], f["eb"], f["bl"], f["kb"], f["nmat"], f["tinv"]
        kbe, u, w, vn, pm, qe, kd = f["kbe"], f["u"], f["w"], f["vn"], f["pm"], f["qe"], f["kd"]
        ebl = [jnp.exp(bl[h]) for h in hs]
        dvn = [tn(pm[h], do[h]) + nn(kd[h], ds1[h]) for h in hs]
        dpr = [nt(do[h], vn[h]) for h in hs]
        dqe = [nt(do[h], s0[h]) for h in hs]
        dkd = [nt(vn[h], ds1[h]) for h in hs]
        for h in hs:
            dst[h] = ds1[h] * ebl[h] + tn(qe[h], do[h]) - tn(w[h], dvn[h])
        du_ = [tn(tinv[h], dvn[h], precision=HI) for h in hs]
        dw_ = [tn(tinv[h], -nt(dvn[h], s0[h]), precision=HI) for h in hs]
        dn = [-(nt(du_[h], u[h]) + nt(dw_[h], w[h])) for h in hs]
        dqk = [dpr[h] * dm[h] for h in hs]
        dkk = [dn[h] * dstrict[h] for h in hs]
        gsum = [dpr[h] * pm[h] + dn[h] * nmat[h] for h in hs]
        dkb = [nn(dkk[h], k[h]) + dw_[h] * eb[h] for h in hs]
        dk = [tn(dkk[h], kb[h]) + tn(dqk[h], q[h]) + dkd[h] * jnp.exp(bl[h] - bcol[h]) + dkb[h] * bet[h] for h in hs]
        dq = [nn(dqk[h], k[h]) + dqe[h] * eb[h] for h in hs]
        colsum = [tn(gsum[h], ones, precision=HI)[:, 0:1] for h in hs]
        dact_q, dact_k, dact_v = [], [], []
        for h in hs:
            dbeta = jnp.sum(dkb[h] * k[h], axis=-1, keepdims=True) + jnp.sum(du_[h] * v[h], axis=-1, keepdims=True)
            skd = jnp.sum(dkd[h] * kd[h], axis=-1, keepdims=True)
            db = (jnp.sum(gsum[h], axis=-1, keepdims=True) - colsum[h] + jnp.sum(dqe[h] * qe[h], axis=-1, keepdims=True)
                  + jnp.sum(dw_[h] * kbe[h], axis=-1, keepdims=True) - skd)
            db_last = jnp.sum(skd, axis=0, keepdims=True) + jnp.sum(ds1[h] * s0[h]) * ebl[h]
            db = db + jnp.where(last_row, db_last, 0.0)
            db_slab = db_slab + jnp.where(lane == h, db, 0.0)
            dbeta_slab = dbeta_slab + jnp.where(lane == GDN_H + h, dbeta, 0.0)
            dqn = dq[h] * scale
            dact_q.append(rq[h] * dqn - qa[h] * (rq[h] * rq[h] * rq[h]) * jnp.sum(qa[h] * dqn, axis=-1, keepdims=True))
            dact_k.append(rk[h] * dk[h] - ka[h] * (rk[h] * rk[h] * rk[h]) * jnp.sum(ka[h] * dk[h], axis=-1, keepdims=True))
            dact_v.append(du_[h] * bet[h])
        dact = jnp.concatenate(dact_q + dact_k + dact_v, axis=1)
        dconv = dact * _dsilu(conv)
        for j in range(4):
            dcw_ref[j:j + 1, :] += jnp.sum(dconv * shifted[j], axis=0, keepdims=True)
        dcat = jnp.concatenate([dconv, dconv_next[...]], axis=0)
        dx = dconv * cw_ref[3:4, :]
        for j in range(3):
            dx = dx + pltpu.roll(dcat, 2 * CHUNK - (3 - j), 0)[:CHUNK, :] * cw_ref[j:j + 1, :]
        dconv_next[...] = dconv
        dp_ref[:, 0:GDN_CONV] = dx.astype(BF16)
        dg = _sel_l(_tri(CHUNK, upper=True).astype(BF16), db_slab)
        da = dg * (-jnp.exp(al_ref[...])) * _sigmoid(zs)
        da = jnp.where(lane < GDN_H, da, 0.0)
        dal_ref[...] += jnp.sum(dg * g, axis=0, keepdims=True)
        ddt_ref[...] += jnp.sum(da, axis=0, keepdims=True)
        dp_ref[:, 4096:4096 + LANES] = (da + dbeta_slab * beta * (1.0 - beta)).astype(BF16)

    rv = lambda i: (nc - 1 - i, 0)
    blk = pl.BlockSpec((CHUNK, D), rv)
    fixed = lambda r, c: pl.BlockSpec((r, c), lambda i: (0, 0))
    return pl.pallas_call(
        body, name="gdn_bwd", grid=(nc,),
        in_specs=[pl.BlockSpec((CHUNK, GDN_INP), lambda i: (jnp.maximum(nc - 2 - i, 0), 0)),
                  pl.BlockSpec((CHUNK, GDN_INP), rv), fixed(8, GDN_CONV), fixed(1, LANES), fixed(1, LANES), fixed(1, GDN_DV),
                  blk, pl.BlockSpec((1, GDN_H, GDN_DK, GDN_DV), lambda i: (nc - 1 - i, 0, 0, 0)), blk],
        out_specs=[pl.BlockSpec((CHUNK, GDN_INP), rv), fixed(8, GDN_CONV), fixed(1, LANES), fixed(1, LANES), fixed(1, GDN_DV)],
        out_shape=[jax.ShapeDtypeStruct((lp, GDN_INP), BF16), jax.ShapeDtypeStruct((8, GDN_CONV), F32),
                   jax.ShapeDtypeStruct((1, LANES), F32), jax.ShapeDtypeStruct((1, LANES), F32),
                   jax.ShapeDtypeStruct((1, GDN_DV), F32)],
        scratch_shapes=[pltpu.VMEM((GDN_H, GDN_DK, GDN_DV), F32), pltpu.VMEM((CHUNK, GDN_CONV), F32)],
        compiler_params=_params(("arbitrary",)),
    )(proj, proj, jnp.pad(conv_w.reshape(4, GDN_CONV), ((0, 4), (0, 0))), jnp.pad(a_log, (0, LANES - GDN_H)).reshape(1, LANES),
      jnp.pad(dt_bias, (0, LANES - GDN_H)).reshape(1, LANES), o_gain.reshape(1, GDN_DV), o, states, dy)


def _coords():
    return lax.axis_index("x"), lax.axis_index("y"), lax.axis_index("c")


def _other_chips(x, y):
    return [(1 - x, y, 2 * (1 - x) + y), (x, 1 - y, 2 * x + 1 - y), (1 - x, 1 - y, 2 * (1 - x) + 1 - y)]


def _gather8(v, *, reduce, name):
    r, c = v.shape

    def body(v_ref, out_ref, *scratch):
        if reduce:
            buf, send_sems, recv_sems = scratch
        else:
            buf = out_ref
            send_sems, recv_sems = scratch
        x, y, cc = _coords()
        me = 4 * x + 2 * y + cc
        buf[me] = v_ref[...]
        copies = []
        for k in range(1, 8):
            px, py, pc = x ^ (k >> 2), y ^ ((k >> 1) & 1), cc ^ (k & 1)
            copies.append(pltpu.make_async_remote_copy(
                src_ref=v_ref, dst_ref=buf.at[me], send_sem=send_sems.at[k - 1], recv_sem=recv_sems.at[k - 1],
                device_id=(px, py, pc), device_id_type=MESH))
        for cp in copies:
            cp.start()
        for k in range(1, 8):
            peer = (x ^ (k >> 2)) * 4 + (y ^ ((k >> 1) & 1)) * 2 + (cc ^ (k & 1))
            pltpu.make_async_remote_copy(
                src_ref=v_ref, dst_ref=buf.at[peer], send_sem=send_sems.at[k - 1], recv_sem=recv_sems.at[k - 1],
                device_id=(x, y, cc), device_id_type=MESH).wait_recv()
        for cp in copies:
            cp.wait_send()
        if reduce:
            acc = buf[0]
            for d in range(1, 8):
                acc = acc + buf[d]
            out_ref[...] = acc

    scratch = [pltpu.SemaphoreType.DMA((7,)), pltpu.SemaphoreType.DMA((7,))]
    if reduce:
        scratch = [pltpu.VMEM((8, r, c), F32)] + scratch
    return pl.pallas_call(
        body, name=name, in_specs=[VM], out_specs=VM,
        out_shape=jax.ShapeDtypeStruct((r, c) if reduce else (8, r, c), F32),
        scratch_shapes=scratch, compiler_params=_params(),
    )(v)


class _AgCopies:
    def __init__(self, buf, ranges, send_sems, recv_sems):
        self.buf, self.ranges, self.send_sems, self.recv_sems = buf, ranges, send_sems, recv_sems
        self.x, self.y, self.cc = _coords()
        self.p = 2 * self.x + self.y
        self.chips = _other_chips(self.x, self.y)

    def rows(self, chip, r, hf):
        start, n = self.ranges[r]
        return self.buf.at[chip, pl.ds(start + hf * (n // 2), n // 2), :]

    def _copy(self, r, k, chip, hf, to):
        return pltpu.make_async_remote_copy(
            src_ref=self.rows(chip, r, hf), dst_ref=self.rows(chip, r, hf), send_sem=self.send_sems.at[3 * r + k],
            recv_sem=self.recv_sems.at[3 * r + k], device_id=to, device_id_type=MESH)

    def pairs(self):
        return [(r, k) for r in range(len(self.ranges)) for k in range(3)]

    def ici(self, r, k):
        cx, cy, _ = self.chips[k]
        return self._copy(r, k, self.p, self.cc, (cx, cy, self.cc))

    def ici_arrival(self, r, k):
        return self._copy(r, k, self.chips[k][2], self.cc, (self.x, self.y, self.cc))

    def forward(self, r, k):
        return self._copy(r, k, self.chips[k][2], self.cc, (self.x, self.y, 1 - self.cc))

    def forward_arrival(self, r, k):
        return self._copy(r, k, self.chips[k][2], 1 - self.cc, (self.x, self.y, self.cc))


def _ag_weights(w4, ranges):
    n = 3 * len(ranges)

    def body(w_ref, out_ref, send1, recv1, send2, recv2):
        ici, fwd = _AgCopies(out_ref, ranges, send1, recv1), _AgCopies(out_ref, ranges, send2, recv2)
        for r, k in ici.pairs():
            ici.ici(r, k).start()
        for r, k in ici.pairs():
            ici.ici_arrival(r, k).wait_recv()
            fwd.forward(r, k).start()
        for r, k in ici.pairs():
            fwd.forward_arrival(r, k).wait_recv()
        for r, k in ici.pairs():
            ici.ici(r, k).wait_send()
            fwd.forward(r, k).wait_send()

    return pl.pallas_call(
        body, name="ag_weights", in_specs=[ANY], out_specs=ANY, out_shape=jax.ShapeDtypeStruct(w4.shape, w4.dtype),
        scratch_shapes=[pltpu.SemaphoreType.DMA((n,))] * 4, input_output_aliases={0: 0}, compiler_params=_params(),
    )(w4)


def _ag_forward(w4, ranges):
    n = 3 * len(ranges)

    def body(w_ref, out_ref, send2, recv2):
        fwd = _AgCopies(out_ref, ranges, send2, recv2)
        for r, k in fwd.pairs():
            fwd.forward(r, k).start()
        for r, k in fwd.pairs():
            fwd.forward_arrival(r, k).wait_recv()
        for r, k in fwd.pairs():
            fwd.forward(r, k).wait_send()

    return pl.pallas_call(
        body, name="ag_forward", in_specs=[ANY], out_specs=ANY, out_shape=jax.ShapeDtypeStruct(w4.shape, w4.dtype),
        scratch_shapes=[pltpu.SemaphoreType.DMA((n,))] * 2, input_output_aliases={0: 0}, compiler_params=_params(),
    )(w4)


def _swap_halves(g, *, name):
    nb, r, c = g.shape
    half = r // 2

    def body(g_ref, out_ref, send_sem, recv_sem):
        x, y, cc = _coords()
        cp = pltpu.make_async_remote_copy(
            src_ref=g_ref.at[:, pl.ds((1 - cc) * half, half), :], dst_ref=out_ref, send_sem=send_sem, recv_sem=recv_sem,
            device_id=(x, y, 1 - cc), device_id_type=MESH)
        cp.start()
        cp.wait()

    return pl.pallas_call(
        body, name=name, in_specs=[ANY], out_specs=ANY, out_shape=jax.ShapeDtypeStruct((nb, half, c), g.dtype),
        scratch_shapes=[pltpu.SemaphoreType.DMA, pltpu.SemaphoreType.DMA], compiler_params=_params(),
    )(g)


def _my_half_index():
    return lax.axis_index("c").astype(jnp.int32).reshape(1)


def _add_halves(g, got, tag):
    nb, r, c = g.shape
    half = r // 2
    tr = _tile(half, 512, 16)
    nt_ = half // tr

    def body(c_ref, a_ref, b_ref, o_ref):
        o_ref[...] = (a_ref[...].astype(F32) + b_ref[...].astype(F32)).astype(BF16)

    return pl.pallas_call(
        body, name=f"rs_add_sibling{tag}",
        grid_spec=pltpu.PrefetchScalarGridSpec(
            num_scalar_prefetch=1, grid=(nb, nt_),
            in_specs=[pl.BlockSpec((1, tr, c), lambda b, i, cr: (b, cr[0] * nt_ + i, 0)),
                      pl.BlockSpec((1, tr, c), lambda b, i, cr: (b, i, 0))],
            out_specs=pl.BlockSpec((1, tr, c), lambda b, i, cr: (b, i, 0))),
        out_shape=jax.ShapeDtypeStruct((nb, half, c), BF16), compiler_params=_params(("parallel", "parallel")),
    )(_my_half_index(), g, got)


def _scatter_copies(s_ref, out_ref, send_sems, recv_sems):
    x, y, cc = _coords()
    sends = [pltpu.make_async_remote_copy(
        src_ref=s_ref.at[blk], dst_ref=out_ref.at[k], send_sem=send_sems.at[k], recv_sem=recv_sems.at[k],
        device_id=(cx, cy, cc), device_id_type=MESH) for k, (cx, cy, blk) in enumerate(_other_chips(x, y))]
    arrivals = [pltpu.make_async_remote_copy(
        src_ref=s_ref.at[2 * x + y], dst_ref=out_ref.at[k], send_sem=send_sems.at[k], recv_sem=recv_sems.at[k],
        device_id=(x, y, cc), device_id_type=MESH) for k in range(3)]
    return sends, arrivals


def _scatter_chips(s, tag):
    nb, hrows, c = s.shape

    def body(s_ref, out_ref, send_sems, recv_sems):
        sends, arrivals = _scatter_copies(s_ref, out_ref, send_sems, recv_sems)
        for cp in sends:
            cp.start()
        for cp in arrivals:
            cp.wait_recv()
        for cp in sends:
            cp.wait_send()

    return pl.pallas_call(
        body, name=f"rs_scatter{tag}", in_specs=[ANY], out_specs=ANY, out_shape=jax.ShapeDtypeStruct((3, hrows, c), s.dtype),
        scratch_shapes=[pltpu.SemaphoreType.DMA((3,)), pltpu.SemaphoreType.DMA((3,))], compiler_params=_params(),
    )(s)


def _sum_chips(s, got, tag):
    nb, hrows, c = s.shape
    tr = _tile(hrows, 512, 16)

    def body(idx_ref, own_ref, got_ref, o_ref):
        p = idx_ref[0]
        own = own_ref[0].astype(F32)
        parts = [got_ref[k].astype(F32) for k in range(3)]
        acc = jnp.zeros_like(own)
        for q in range(4):
            val = own
            for k, rel in enumerate((2, 1, 3)):
                val = jnp.where((p ^ rel) == q, parts[k], val)
            acc = acc + val
        o_ref[...] = acc

    idx = (2 * lax.axis_index("x") + lax.axis_index("y")).astype(jnp.int32).reshape(1)
    return pl.pallas_call(
        body, name=f"rs_sum_chips{tag}",
        grid_spec=pltpu.PrefetchScalarGridSpec(
            num_scalar_prefetch=1, grid=(hrows // tr,),
            in_specs=[pl.BlockSpec((1, tr, c), lambda i, pr: (pr[0], i, 0)), pl.BlockSpec((3, tr, c), lambda i, pr: (0, i, 0))],
            out_specs=pl.BlockSpec((tr, c), lambda i, pr: (i, 0))),
        out_shape=jax.ShapeDtypeStruct((hrows, c), F32), compiler_params=_params(("parallel",)),
    )(idx, s, got)


def _swap_sibling(t, tag):
    def body(t_ref, out_ref, send_sem, recv_sem):
        x, y, cc = _coords()
        cp = pltpu.make_async_remote_copy(src_ref=t_ref, dst_ref=out_ref, send_sem=send_sem, recv_sem=recv_sem,
                                          device_id=(x, y, 1 - cc), device_id_type=MESH)
        cp.start()
        cp.wait()

    return pl.pallas_call(
        body, name=f"rs_join{tag}", in_specs=[ANY], out_specs=ANY, out_shape=jax.ShapeDtypeStruct(t.shape, t.dtype),
        scratch_shapes=[pltpu.SemaphoreType.DMA, pltpu.SemaphoreType.DMA], compiler_params=_params(),
    )(t)


def _rs_local(g, tag):
    return _add_halves(g, _swap_halves(g, name=f"rs_swap{tag}"), tag)


def _rs_finish(s, recv, tag):
    t = _sum_chips(s, recv, tag)
    r = _swap_sibling(t, tag)
    first = lax.axis_index("c") == 0
    return jnp.concatenate([jnp.where(first, t, r), jnp.where(first, r, t)], axis=0)


_BIG = (("w_gate_up", 2), ("w_down", 1), ("fox_w_in", 2), ("fox_w_out", 1), ("gla_w_in", 2), ("gla_w_out", 1),
        ("gdn_w_in", 2), ("gdn_w_out", 1))
_SMALL_SHARDED = (("meta_tokens", 1), ("gla_w_alpha2", 2), ("gdn_conv_w", 3))
_REPLICATED = ("norm_mix", "norm_ffn", "fox_b_f", "fox_q_gain", "fox_k_gain", "gla_b_alpha", "gla_o_gain",
               "gdn_a_log", "gdn_dt_bias", "gdn_o_gain")
_WEIGHTS = ("meta_tokens", "norm_mix", "norm_ffn", "w_gate_up", "w_down", "fox_w_in", "fox_b_f", "fox_q_gain",
            "fox_k_gain", "fox_w_out", "gla_w_in", "gla_w_alpha2", "gla_b_alpha", "gla_o_gain", "gla_w_out",
            "gdn_w_in", "gdn_conv_w", "gdn_a_log", "gdn_dt_bias", "gdn_o_gain", "gdn_w_out")
_PACK_ROWS = 512
_IN_W = ("fox_w_in", "gla_w_in", "gdn_w_in")
_OUT_W = ("fox_w_out", "gla_w_out", "gdn_w_out")


def _piece_rows(n):
    return -(-n // 32) * 32


def _pack(arrays, width, row_mult, dtype):
    flat = jnp.concatenate([a.astype(dtype).reshape(-1) for a in arrays])
    per = width * row_mult
    n = -(-flat.shape[0] // per) * per
    return jnp.pad(flat, (0, n - flat.shape[0])).reshape(n // width, width)


def _unpack(flat, shapes):
    out, off = [], 0
    for s in shapes:
        n = 1
        for d in s:
            n *= d
        out.append(flat[off:off + n].reshape(s))
        off += n
    return out


def _unpack_cols(flat2, shapes):
    out, off = [], 0
    for s in shapes:
        n = 1
        for d in s:
            n *= d
        out.append(flat2[:, off:off + n].reshape((flat2.shape[0],) + tuple(s)))
        off += n
    return out


def _pad_cols(w, n):
    return jnp.pad(w, [(0, 0)] * (w.ndim - 1) + [(0, n - w.shape[-1])])


def kernel(x, meta_tokens, norm_mix, norm_ffn, w_gate_up, w_down, fox_w_in, fox_b_f, fox_q_gain, fox_k_gain, fox_w_out, gla_w_in, gla_w_alpha2, gla_b_alpha, gla_o_gain, gla_w_out, gdn_w_in, gdn_conv_w, gdn_a_log, gdn_dt_bias, gdn_o_gain, gdn_w_out, loss_target, m_meta_tokens, m_norm_mix, m_norm_ffn, m_w_gate_up, m_w_down, m_fox_w_in, m_fox_b_f, m_fox_q_gain, m_fox_k_gain, m_fox_w_out, m_gla_w_in, m_gla_w_alpha2, m_gla_b_alpha, m_gla_o_gain, m_gla_w_out, m_gdn_w_in, m_gdn_conv_w, m_gdn_a_log, m_gdn_dt_bias, m_gdn_o_gain, m_gdn_w_out, v_meta_tokens, v_norm_mix, v_norm_ffn, v_w_gate_up, v_w_down, v_fox_w_in, v_fox_b_f, v_fox_q_gain, v_fox_k_gain, v_fox_w_out, v_gla_w_in, v_gla_w_alpha2, v_gla_b_alpha, v_gla_o_gain, v_gla_w_out, v_gdn_w_in, v_gdn_conv_w, v_gdn_a_log, v_gdn_dt_bias, v_gdn_o_gain, v_gdn_w_out):
    W = dict(meta_tokens=meta_tokens, norm_mix=norm_mix, norm_ffn=norm_ffn, w_gate_up=w_gate_up, w_down=w_down,
             fox_w_in=fox_w_in, fox_b_f=fox_b_f, fox_q_gain=fox_q_gain, fox_k_gain=fox_k_gain, fox_w_out=fox_w_out,
             gla_w_in=gla_w_in, gla_w_alpha2=gla_w_alpha2, gla_b_alpha=gla_b_alpha, gla_o_gain=gla_o_gain,
             gla_w_out=gla_w_out, gdn_w_in=gdn_w_in, gdn_conv_w=gdn_conv_w, gdn_a_log=gdn_a_log,
             gdn_dt_bias=gdn_dt_bias, gdn_o_gain=gdn_o_gain, gdn_w_out=gdn_w_out)
    M = dict(meta_tokens=m_meta_tokens, norm_mix=m_norm_mix, norm_ffn=m_norm_ffn, w_gate_up=m_w_gate_up, w_down=m_w_down,
             fox_w_in=m_fox_w_in, fox_b_f=m_fox_b_f, fox_q_gain=m_fox_q_gain, fox_k_gain=m_fox_k_gain,
             fox_w_out=m_fox_w_out, gla_w_in=m_gla_w_in, gla_w_alpha2=m_gla_w_alpha2, gla_b_alpha=m_gla_b_alpha,
             gla_o_gain=m_gla_o_gain, gla_w_out=m_gla_w_out, gdn_w_in=m_gdn_w_in, gdn_conv_w=m_gdn_conv_w,
             gdn_a_log=m_gdn_a_log, gdn_dt_bias=m_gdn_dt_bias, gdn_o_gain=m_gdn_o_gain, gdn_w_out=m_gdn_w_out)
    V = dict(meta_tokens=v_meta_tokens, norm_mix=v_norm_mix, norm_ffn=v_norm_ffn, w_gate_up=v_w_gate_up, w_down=v_w_down,
             fox_w_in=v_fox_w_in, fox_b_f=v_fox_b_f, fox_q_gain=v_fox_q_gain, fox_k_gain=v_fox_k_gain,
             fox_w_out=v_fox_w_out, gla_w_in=v_gla_w_in, gla_w_alpha2=v_gla_w_alpha2, gla_b_alpha=v_gla_b_alpha,
             gla_o_gain=v_gla_o_gain, gla_w_out=v_gla_w_out, gdn_w_in=v_gdn_w_in, gdn_conv_w=v_gdn_conv_w,
             gdn_a_log=v_gdn_a_log, gdn_dt_bias=v_gdn_dt_bias, gdn_o_gain=v_gdn_o_gain, gdn_w_out=v_gdn_w_out)
    chip = 2 * lax.axis_index("x") + lax.axis_index("y")

    pieces, offs, r = [], {}, FFN_ROWS
    for n in _IN_W:
        nc = W[n].shape[2]
        for l in range(W[n].shape[0]):
            pieces.append(jnp.pad(W[n][l].T.astype(BF16), ((0, _piece_rows(nc) - nc), (0, 0))))
            offs[n, l] = r
            r += _piece_rows(nc)
    for n in _OUT_W:
        for l in range(W[n].shape[0]):
            pieces.append(W[n][l].astype(BF16))
            offs[n, l] = r
            r += W[n].shape[1]
    rows = -(-r // _PACK_ROWS) * _PACK_ROWS
    packed = jnp.concatenate([jnp.swapaxes(w_gate_up, 1, 2).reshape(-1, D).astype(BF16), w_down.reshape(-1, D).astype(BF16)]
                             + pieces + [jnp.zeros((rows - r, D), BF16)], axis=0)
    first_rows = [(OFF_GU, GU_ROWS), (OFF_DOWN, DOWN_ROWS), (offs["fox_w_in", 0], offs["fox_w_in", 1] - offs["fox_w_in", 0]),
                  (offs["fox_w_out", 0], offs["fox_w_out", 1] - offs["fox_w_out", 0])]
    later_rows = [(OFF_GU + GU_ROWS, (DEPTH - 1) * GU_ROWS), (OFF_DOWN + DOWN_ROWS, (DEPTH - 1) * DOWN_ROWS),
                  (offs["fox_w_in", 1], offs["fox_w_out", 0] - offs["fox_w_in", 1]), (offs["fox_w_out", 1], r - offs["fox_w_out", 1])]
    wpk = _ag_weights(lax.dynamic_update_slice(lax.empty((4, rows, D), BF16), packed[None], (chip, 0, 0)), first_rows)

    def in_t(buf, n, l, npad):
        nc = W[n].shape[2]
        return jnp.concatenate([buf[q, offs[n, l]:offs[n, l] + nc] for q in range(4)] + [jnp.zeros((npad - 4 * nc, D), BF16)], 0)

    def out_w(buf, n, l):
        return jnp.concatenate([buf[q, offs[n, l]:offs[n, l] + W[n].shape[1]] for q in range(4)], axis=0)

    fox_in0, fox_out0 = in_t(wpk, "fox_w_in", 0, FOX_INP), out_w(wpk, "fox_w_out", 0)
    full = {}
    small = _pack([W[n] for n, _ in _SMALL_SHARDED], LANES, 8, F32)
    small_all = _gather8(small, reduce=False, name="gather_small").reshape(8, -1)
    for (n, ax), seg in zip(_SMALL_SHARDED, _unpack_cols(small_all, [W[n].shape for n, _ in _SMALL_SHARDED])):
        full[n] = jnp.concatenate([seg[2 * q] for q in range(4)], axis=ax)
    fox_in, full["fox_w_out"] = [fox_in0], [fox_out0]
    w_alpha2, conv_w = full["gla_w_alpha2"][0], full["gdn_conv_w"][0]

    h = jnp.concatenate([jnp.zeros((META0, D), F32), full["meta_tokens"], x[0]], axis=0)
    saved = []
    for i in range(DEPTH):
        kind, j = i % 3, i // 3
        y = _rms_fwd(h, norm_mix[i], name=f"norm_mix{i}")
        if kind == 0:
            proj = _mm(y, fox_in[j], tb=True, name=f"fox_in{j}")
            qa, ka, va = _fox_prep(proj, fox_b_f[j], fox_q_gain[j], fox_k_gain[j])
            if i == 0:
                o, og, lse, wpk = _fox_attn_fwd(qa, ka, va, proj, ag=(wpk, later_rows))
                wpk = _ag_forward(wpk, later_rows)
                fox_in += [in_t(wpk, "fox_w_in", l, FOX_INP) for l in range(1, fox_w_in.shape[0])]
                full["fox_w_out"] += [out_w(wpk, "fox_w_out", l) for l in range(1, fox_w_out.shape[0])]
                gla_in = [in_t(wpk, "gla_w_in", l, GLA_INP) for l in range(gla_w_in.shape[0])]
                gdn_in = [in_t(wpk, "gdn_w_in", l, GDN_INP) for l in range(gdn_w_in.shape[0])]
                for n in ("gla_w_out", "gdn_w_out"):
                    full[n] = [out_w(wpk, n, l) for l in range(W[n].shape[0])]
            else:
                o, og, lse = _fox_attn_fwd(qa, ka, va, proj)
            w_out, mix = full["fox_w_out"][j], (proj, qa, ka, va, o, lse)
        elif kind == 1:
            proj = _mm(y, gla_in[j], tb=True, name=f"gla_in{j}")
            o, og, states = _gla_fwd(proj, w_alpha2, gla_b_alpha[j], gla_o_gain[j])
            w_out, mix = full["gla_w_out"][j], (proj, o, states)
        else:
            proj = _mm(y, gdn_in[j], tb=True, name=f"gdn_in{j}")
            o, og, states = _gdn_fwd(proj, conv_w, gdn_a_log[j], gdn_dt_bias[j], gdn_o_gain[j])
            w_out, mix = full["gdn_w_out"][j], (proj, o, states)
        hm = _mm(og, w_out, add=h, name=f"mix_out{i}")
        yf = _rms_fwd(hm, norm_ffn[i], name=f"norm_ffn{i}")
        gate, up, act = _ffn_up(yf, wpk, i)
        hn = _ffn_down(act, wpk, i, hm)
        saved.append((h, y, mix, og, w_out, hm, yf, gate, up, act))
        h = hn
    dh, loss_tile = _loss_head(h, loss_target[0])

    G = {n: [None] * W[n].shape[0] for n in _WEIGHTS if n not in ("meta_tokens", "w_gate_up", "w_down") + _IN_W}
    GT = {}

    def grad_layout(ffn_layers, pieces):
        off, end = {}, 0
        for l in ffn_layers:
            off["gu", l] = end
            end += GU_ROWS
        for l in ffn_layers:
            off["down", l] = end
            end += DOWN_ROWS
        for n, l in pieces:
            off[n, l] = end
            end += _piece_rows(W[n].shape[2]) if n in _IN_W else W[n].shape[1]
        return off, end, -(-end // _PACK_ROWS) * _PACK_ROWS

    first_pieces = [("fox_w_in", 0), ("fox_w_out", 0)]
    later_pieces = [(n, l) for n in _IN_W + _OUT_W for l in range(W[n].shape[0]) if (n, l) not in first_pieces]
    layouts = [grad_layout([0], first_pieces), grad_layout(list(range(1, DEPTH)), later_pieces)]
    gbuf = [jnp.zeros((4, lay[2], D), BF16) for lay in layouts]

    def with_pieces(buf, lay, pieces):
        off, end, total = lay
        blocks = []
        for q in range(4):
            parts = []
            for n, l in pieces:
                if n in _IN_W:
                    nc = W[n].shape[2]
                    parts.append(jnp.pad(GT[n, l][q * nc:(q + 1) * nc], ((0, _piece_rows(nc) - nc), (0, 0))))
                else:
                    nr = W[n].shape[1]
                    parts.append(G[n][l][q * nr:(q + 1) * nr])
            blocks.append(jnp.concatenate(parts + [jnp.zeros((total - end, D), BF16)], axis=0))
        return lax.dynamic_update_slice(buf, jnp.stack(blocks), (0, off[pieces[0]], 0))

    s_later = None
    for i in reversed(range(DEPTH)):
        kind, j = i % 3, i // 3
        h_in, y, mix, og, w_out, hm, yf, gate, up, act = saved[i]
        b = 0 if i == 0 else 1
        dg, du = _ffn_dact(dh, wpk, i, gate, up)
        gbuf[b] = _ffn_dw_down(act, dh, gbuf[b], i, layouts[b][0]["down", i])
        dyf = _ffn_dyf(dg, du, wpk, i)
        gbuf[b] = _ffn_dw_gu(dg, du, yf, gbuf[b], i, layouts[b][0]["gu", i] // GU_ROWS)
        dhm, dnf = _rms_bwd(hm, norm_ffn[i], dyf, dh, name=f"d_norm_ffn{i}")
        G["norm_ffn"][i] = dnf[0]
        dog = _mm(dhm, w_out, tb=True, name=f"d_og{i}")
        dw_out = _mm(og, dhm, ta=True, out_dtype=BF16, name=f"d_w_out{i}")
        if kind == 0:
            proj, qa, ka, va, o, lse = mix
            doa, q2, dgate = _fox_gate_bwd(dog, o, proj, lse, qa)
            if i == 0:
                dqn, dkn, dv, dct, recv_later = _fox_attn_bwd(q2, ka, va, doa, rs=s_later)
            else:
                dqn, dkn, dv, dct = _fox_attn_bwd(q2, ka, va, doa)
            dproj, dqg, dkg, dbf = _fox_prep_bwd(proj, fox_b_f[j], fox_q_gain[j], fox_k_gain[j], dqn, dkn, dv, dgate, dct)
            G["fox_w_out"][j] = dw_out
            G["fox_q_gain"][j] = dqg.reshape(FOX_H, FOX_DH).sum(0)
            G["fox_k_gain"][j] = dkg.reshape(FOX_H, FOX_DH).sum(0)
            G["fox_b_f"][j] = dbf[0, :FOX_H]
            w_in, wname = fox_in[j], "fox_w_in"
        elif kind == 1:
            proj, o, states = mix
            dproj, dwa, dba, dogain = _gla_bwd(proj, w_alpha2, gla_b_alpha[j], gla_o_gain[j], o, states, dog)
            G["gla_w_out"][j] = dw_out
            G["gla_w_alpha2"][j] = dwa[:GLA_RANK]
            G["gla_b_alpha"][j] = dba[0]
            G["gla_o_gain"][j] = dogain[0]
            w_in, wname = gla_in[j], "gla_w_in"
        else:
            proj, o, states = mix
            dproj, dcw, dal, ddt, dogain = _gdn_bwd(proj, conv_w, gdn_a_log[j], gdn_dt_bias[j], gdn_o_gain[j], o, states, dog)
            G["gdn_w_out"][j] = dw_out
            G["gdn_conv_w"][j] = dcw[:4].reshape(4, 1, GDN_CONV)
            G["gdn_a_log"][j] = dal[0, :GDN_H]
            G["gdn_dt_bias"][j] = ddt[0, :GDN_H]
            G["gdn_o_gain"][j] = dogain[0]
            w_in, wname = gdn_in[j], "gdn_w_in"
        dy = _mm(dproj, w_in, name=f"d_y{i}")
        GT[wname, j] = _mm(dproj, y, ta=True, out_dtype=BF16, name=f"d_w_in{i}")
        dh, dnm = _rms_bwd(h_in, norm_mix[i], dy, dhm, name=f"d_norm_mix{i}")
        G["norm_mix"][i] = dnm[0]
        if i == 1:
            s_later = _rs_local(with_pieces(gbuf[1], layouts[1], later_pieces), "_later")
    grad_x = dh[ROW0:][None]
    G = {n: (v if n in _OUT_W else jnp.stack(v)) for n, v in G.items()}
    G["meta_tokens"] = dh[META0:ROW0]

    s_first = _rs_local(with_pieces(gbuf[0], layouts[0], first_pieces), "_first")
    reduced = [_rs_finish(s_first, _scatter_chips(s_first, "_first"), "_first"), _rs_finish(s_later, recv_later, "_later")]

    def reduced_piece(n, l):
        b = 0 if (n, l) in first_pieces else 1
        start = layouts[b][0][n, l]
        return reduced[b][start:start + (W[n].shape[2] if n in _IN_W else W[n].shape[1])]

    grads = {}
    for n in _IN_W:
        grads[n] = jnp.stack([reduced_piece(n, l).T for l in range(W[n].shape[0])])
    for n in _OUT_W:
        grads[n] = jnp.stack([reduced_piece(n, l) for l in range(W[n].shape[0])])
    small_names = [n for n, _ in _SMALL_SHARDED] + list(_REPLICATED)
    small_g = _pack([G[n] for n in small_names] + [loss_tile[0, 0:1]], LANES, 8, F32)
    small_sum = _gather8(small_g, reduce=True, name="allreduce_small").reshape(-1)
    small_shapes = [G[n].shape for n in small_names] + [(1,)]
    small_vals = _unpack(small_sum, small_shapes)
    loss = small_vals[-1][0]
    for n, val in zip(small_names, small_vals[:-1]):
        grads[n] = val
    for n, ax in _SMALL_SHARDED:
        sz = W[n].shape[ax]
        grads[n] = lax.dynamic_slice_in_dim(grads[n], chip * sz, sz, axis=ax)

    delta, new_m, new_v = {}, {}, {}
    for n, key, tr_ in (("w_gate_up", "gu", True), ("w_down", "down", False)):
        grads[n], delta[n], new_m[n], new_v[n] = _adamw_packed(
            W[n], reduced[0], reduced[1], M[n], V[n], row0=layouts[0][0][key, 0], row_off=layouts[1][0][key, 1],
            transposed=tr_, name=f"adamw_{n}")
    for n in _IN_W + _OUT_W:
        delta[n], new_m[n], new_v[n] = _adamw(W[n], grads[n], M[n], V[n], name=f"adamw_{n}")
    tiny = [n for n in _WEIGHTS if n not in dict(_BIG)]
    packs = [_pack([T[n] for n in tiny], LANES, 8, F32) for T in (W, grads, M, V)]
    outs = _adamw(*packs, name="adamw_small")
    shapes = [W[n].shape for n in tiny]
    for dst, o in zip((delta, new_m, new_v), outs):
        for n, val in zip(tiny, _unpack(o.reshape(-1), shapes)):
            dst[n] = val
    return (loss, grad_x, *[grads[n] for n in _WEIGHTS], *[delta[n] for n in _WEIGHTS],
            *[new_m[n] for n in _WEIGHTS], *[new_v[n] for n in _WEIGHTS])
```

```python
import functools

import jax
import jax.numpy as jnp
from jax import lax
from jax.experimental import pallas as pl
from jax.experimental.pallas import tpu as pltpu

F32, BF16 = jnp.float32, jnp.bfloat16
D = 1024
N_META = 16
ROW0 = 128
META0 = ROW0 - N_META
EPS = 1e-6
LANES = 128
VMEM_LIMIT = 56 * 1024 * 1024

FOX_H, FOX_DH = 16, 64
FOX_INP = 4224
GLA_H, GLA_DK, GLA_DV, GLA_RANK = 4, 128, 256, 16
GLA_QK, GLA_V = 512, 1024
GLA_INP = 3200
GLA_NORM = 16.0
GDN_H, GDN_DK, GDN_DV = 8, 128, 128
GDN_CONV = 3072
GDN_INP = 4224
CHUNK = 64
D_FF = 2816
DEPTH = 4

ADAM_LR, ADAM_B1, ADAM_B2, ADAM_EPS, ADAM_WD, ADAM_STEP = 0.001, 0.9, 0.999, 1e-08, 0.01, 10

MESH = pl.DeviceIdType.MESH
ANY = pl.BlockSpec(memory_space=pl.ANY)
VM = pl.BlockSpec(memory_space=pltpu.VMEM)


def _params(sem=None, **kw):
    if sem is not None:
        kw["dimension_semantics"] = sem
    return pltpu.CompilerParams(vmem_limit_bytes=VMEM_LIMIT, **kw)


def _tile(n, cap, mult=LANES):
    best = None
    for t in range(mult, min(n, cap) + 1, mult):
        if n % t == 0:
            best = t
    return best if best is not None else n


def nn(a, b, **kw):
    return jnp.dot(a, b, preferred_element_type=F32, **kw)


def nt(a, b, **kw):
    return lax.dot_general(a, b, (((1,), (1,)), ((), ())), preferred_element_type=F32, **kw)


def tn(a, b, **kw):
    return lax.dot_general(a, b, (((0,), (0,)), ((), ())), preferred_element_type=F32, **kw)


def _split3(x):
    hi = x.astype(BF16)
    r = x - hi.astype(F32)
    mid = r.astype(BF16)
    lo = (r - mid.astype(F32)).astype(BF16)
    return hi, mid, lo


def _sel_l(sel, x):
    a, b, c = _split3(x)
    return nn(sel, a) + nn(sel, b) + nn(sel, c)


def _sel_r(x, sel):
    a, b, c = _split3(x)
    return nn(a, sel) + nn(b, sel) + nn(c, sel)


def _iota(shape, dim):
    return lax.broadcasted_iota(jnp.int32, shape, dim)


def _tri(n, upper=False, strict=False):
    i, j = _iota((n, n), 0), _iota((n, n), 1)
    if upper:
        m = (j > i) if strict else (j >= i)
    else:
        m = (j < i) if strict else (j <= i)
    return m


def _sigmoid(x):
    return 1.0 / (1.0 + jnp.exp(-x))


def _log_sigmoid(x):
    return jnp.minimum(x, 0.0) - jnp.log(1.0 + jnp.exp(-jnp.abs(x)))


def _softplus(x):
    return jnp.maximum(x, 0.0) + jnp.log(1.0 + jnp.exp(-jnp.abs(x)))


def _silu(x):
    return x * _sigmoid(x)


def _dsilu(x):
    s = _sigmoid(x)
    return s * (1.0 + x * (1.0 - s))


def _mm(a, b, *, ta=False, tb=False, add=None, out_dtype=F32, name):
    m, k = (a.shape[1], a.shape[0]) if ta else a.shape
    n = b.shape[0] if tb else b.shape[1]
    assert k == (b.shape[1] if tb else b.shape[0])
    tm, tn_, tk = _tile(m, 1408, LANES if ta else 16), _tile(n, 1408), _tile(k, 1408)
    nk = k // tk

    def body(*refs):
        if add is None:
            a_ref, b_ref, o_ref, acc = refs
        else:
            a_ref, b_ref, r_ref, o_ref, acc = refs
        kk = pl.program_id(2)

        @pl.when(kk == 0)
        def _():
            acc[...] = jnp.zeros_like(acc)

        av, bv = a_ref[...].astype(BF16), b_ref[...].astype(BF16)
        dims = (((0,) if ta else (1,), (1,) if tb else (0,)), ((), ()))
        acc[...] += lax.dot_general(av, bv, dims, preferred_element_type=F32)

        @pl.when(kk == nk - 1)
        def _():
            r = acc[...]
            if add is not None:
                r = r + r_ref[...].astype(F32)
            o_ref[...] = r.astype(out_dtype)

    a_spec = pl.BlockSpec((tk, tm), lambda i, j, q: (q, i)) if ta else pl.BlockSpec((tm, tk), lambda i, j, q: (i, q))
    b_spec = pl.BlockSpec((tn_, tk), lambda i, j, q: (j, q)) if tb else pl.BlockSpec((tk, tn_), lambda i, j, q: (q, j))
    o_spec = pl.BlockSpec((tm, tn_), lambda i, j, q: (i, j))
    ins, specs = [a, b], [a_spec, b_spec]
    if add is not None:
        ins.append(add)
        specs.append(o_spec)
    return pl.pallas_call(
        body, name=name, grid=(m // tm, n // tn_, nk), in_specs=specs, out_specs=o_spec,
        out_shape=jax.ShapeDtypeStruct((m, n), out_dtype),
        scratch_shapes=[pltpu.VMEM((tm, tn_), F32)],
        compiler_params=_params(("parallel", "parallel", "arbitrary")),
    )(*ins)


def _rms_fwd(h, g, *, name):
    lp = h.shape[0]
    tr = _tile(lp, 512)

    def body(h_ref, g_ref, y_ref):
        x = h_ref[...]
        r = lax.rsqrt(jnp.mean(x * x, axis=-1, keepdims=True) + EPS)
        y_ref[...] = (x * r * g_ref[...]).astype(BF16)

    return pl.pallas_call(
        body, name=name, grid=(lp // tr,),
        in_specs=[pl.BlockSpec((tr, D), lambda i: (i, 0)), pl.BlockSpec((1, D), lambda i: (0, 0))],
        out_specs=pl.BlockSpec((tr, D), lambda i: (i, 0)),
        out_shape=jax.ShapeDtypeStruct((lp, D), BF16), compiler_params=_params(("parallel",)),
    )(h, g.reshape(1, D))


def _rms_bwd(h, g, dy, dres, *, name):
    lp = h.shape[0]
    tr = _tile(lp, 512)

    def body(h_ref, g_ref, dy_ref, dr_ref, dh_ref, dg_ref):
        @pl.when(pl.program_id(0) == 0)
        def _():
            dg_ref[...] = jnp.zeros_like(dg_ref)

        x, dyv = h_ref[...], dy_ref[...].astype(F32)
        r = lax.rsqrt(jnp.mean(x * x, axis=-1, keepdims=True) + EPS)
        u = dyv * g_ref[...]
        dx = r * u - x * (r * r * r) * jnp.mean(x * u, axis=-1, keepdims=True)
        dh_ref[...] = dr_ref[...] + dx
        dg_ref[...] += jnp.sum(dyv * x * r, axis=0, keepdims=True)

    return pl.pallas_call(
        body, name=name, grid=(lp // tr,),
        in_specs=[pl.BlockSpec((tr, D), lambda i: (i, 0)), pl.BlockSpec((1, D), lambda i: (0, 0)),
                  pl.BlockSpec((tr, D), lambda i: (i, 0)), pl.BlockSpec((tr, D), lambda i: (i, 0))],
        out_specs=[pl.BlockSpec((tr, D), lambda i: (i, 0)), pl.BlockSpec((1, D), lambda i: (0, 0))],
        out_shape=[jax.ShapeDtypeStruct((lp, D), F32), jax.ShapeDtypeStruct((1, D), F32)],
        compiler_params=_params(("arbitrary",)),
    )(h, g.reshape(1, D), dy, dres)


GU_ROWS, DOWN_ROWS = 1408, 704
OFF_GU, OFF_DOWN = 0, DEPTH * GU_ROWS
FFN_ROWS = DEPTH * (GU_ROWS + DOWN_ROWS)
FFN_TM = 704


def _gu_spec(fn):
    return pl.BlockSpec((None, GU_ROWS, D), fn)


def _down_spec(fn):
    return pl.BlockSpec((None, DOWN_ROWS, D), fn)


def _down_pair(w0_ref, w1_ref):
    return jnp.concatenate([w0_ref[...], w1_ref[...]], axis=0)


def _ffn_up(yf, wpk, layer):
    lp = yf.shape[0]
    tm = _tile(lp, FFN_TM, 16)

    def body(y_ref, wg_ref, wu_ref, g_ref, u_ref, a_ref):
        y = y_ref[...]
        g, u = nt(y, wg_ref[...]), nt(y, wu_ref[...])
        g_ref[...] = g.astype(BF16)
        u_ref[...] = u.astype(BF16)
        a_ref[...] = (_silu(g) * u).astype(BF16)

    o = pl.BlockSpec((tm, GU_ROWS), lambda i, j: (i, j))
    return pl.pallas_call(
        body, name=f"ffn_up{layer}", grid=(lp // tm, 2),
        in_specs=[pl.BlockSpec((tm, D), lambda i, j: (i, 0)), _gu_spec(lambda i, j: (j, OFF_GU // GU_ROWS + layer, 0)),
                  _gu_spec(lambda i, j: (2 + j, OFF_GU // GU_ROWS + layer, 0))],
        out_specs=[o, o, o], out_shape=[jax.ShapeDtypeStruct((lp, D_FF), BF16)] * 3,
        compiler_params=_params(("parallel", "parallel")),
    )(yf, wpk, wpk)


def _ffn_down(act, wpk, layer, res):
    lp = act.shape[0]
    tm = _tile(lp, FFN_TM, 16)

    def body(a_ref, w0_ref, w1_ref, r_ref, o_ref, acc):
        kk = pl.program_id(1)

        @pl.when(kk == 0)
        def _():
            acc[...] = r_ref[...]

        acc[...] += nn(a_ref[...], _down_pair(w0_ref, w1_ref))

        @pl.when(kk == 1)
        def _():
            o_ref[...] = acc[...]

    o = pl.BlockSpec((tm, D), lambda i, kk: (i, 0))
    blk = OFF_DOWN // DOWN_ROWS + layer
    return pl.pallas_call(
        body, name=f"ffn_down{layer}", grid=(lp // tm, 2),
        in_specs=[pl.BlockSpec((tm, GU_ROWS), lambda i, kk: (i, kk)), _down_spec(lambda i, kk: (2 * kk, blk, 0)),
                  _down_spec(lambda i, kk: (2 * kk + 1, blk, 0)), o],
        out_specs=o, out_shape=jax.ShapeDtypeStruct((lp, D), F32), scratch_shapes=[pltpu.VMEM((tm, D), F32)],
        compiler_params=_params(("parallel", "arbitrary")),
    )(act, wpk, wpk, res)


def _ffn_dact(dh, wpk, layer, gate, up):
    lp = dh.shape[0]
    tm = _tile(lp, FFN_TM, 16)

    def body(d_ref, w0_ref, w1_ref, g_ref, u_ref, dg_ref, du_ref):
        da = nt(d_ref[...].astype(BF16), _down_pair(w0_ref, w1_ref))
        g, u = g_ref[...].astype(F32), u_ref[...].astype(F32)
        dg_ref[...] = (da * u * _dsilu(g)).astype(BF16)
        du_ref[...] = (da * _silu(g)).astype(BF16)

    o = pl.BlockSpec((tm, GU_ROWS), lambda i, j: (i, j))
    blk = OFF_DOWN // DOWN_ROWS + layer
    return pl.pallas_call(
        body, name=f"d_act{layer}", grid=(lp // tm, 2),
        in_specs=[pl.BlockSpec((tm, D), lambda i, j: (i, 0)), _down_spec(lambda i, j: (2 * j, blk, 0)),
                  _down_spec(lambda i, j: (2 * j + 1, blk, 0)), o, o],
        out_specs=[o, o], out_shape=[jax.ShapeDtypeStruct((lp, D_FF), BF16)] * 2,
        compiler_params=_params(("parallel", "parallel")),
    )(dh, wpk, wpk, gate, up)


def _ffn_dyf(dg, du, wpk, layer):
    lp = dg.shape[0]
    tm = _tile(lp, FFN_TM, 16)

    def body(dg_ref, du_ref, w_ref, o_ref, acc):
        kk = pl.program_id(1)

        @pl.when(kk == 0)
        def _():
            acc[...] = jnp.zeros_like(acc)

        @pl.when(kk < 2)
        def _():
            acc[...] += nn(dg_ref[...], w_ref[...])

        @pl.when(kk >= 2)
        def _():
            acc[...] += nn(du_ref[...], w_ref[...])

        @pl.when(kk == 3)
        def _():
            o_ref[...] = acc[...]

    return pl.pallas_call(
        body, name=f"d_yf{layer}", grid=(lp // tm, 4),
        in_specs=[pl.BlockSpec((tm, GU_ROWS), lambda i, kk: (i, jnp.minimum(kk, 1))),
                  pl.BlockSpec((tm, GU_ROWS), lambda i, kk: (i, jnp.maximum(kk - 2, 0))),
                  _gu_spec(lambda i, kk: (kk, OFF_GU // GU_ROWS + layer, 0))],
        out_specs=pl.BlockSpec((tm, D), lambda i, kk: (i, 0)), out_shape=jax.ShapeDtypeStruct((lp, D), F32),
        scratch_shapes=[pltpu.VMEM((tm, D), F32)], compiler_params=_params(("parallel", "arbitrary")),
    )(dg, du, wpk)


def _ffn_dw_down(act, dh, gpk, layer, row):
    lp = act.shape[0]
    tk = _tile(lp, 1408, 16)
    nk = lp // tk

    def body(a_ref, d_ref, g_in, g_out, acc, stage, sems):
        jp, kk = pl.program_id(0), pl.program_id(1)

        @pl.when(kk == 0)
        def _():
            acc[...] = jnp.zeros_like(acc)

        acc[...] += tn(a_ref[...], d_ref[...].astype(BF16))

        @pl.when(kk == nk - 1)
        def _():
            stage[...] = acc[...].astype(BF16)
            copies = [pltpu.make_async_copy(stage.at[pl.ds(hf * DOWN_ROWS, DOWN_ROWS), :],
                                            g_out.at[2 * jp + hf, pl.ds(row, DOWN_ROWS), :], sems.at[hf]) for hf in range(2)]
            for cp in copies:
                cp.start()
            for cp in copies:
                cp.wait()

    return pl.pallas_call(
        body, name=f"d_w_down{layer}", grid=(2, nk),
        in_specs=[pl.BlockSpec((tk, GU_ROWS), lambda jp, kk: (kk, jp)), pl.BlockSpec((tk, D), lambda jp, kk: (kk, 0)), ANY],
        out_specs=ANY, out_shape=jax.ShapeDtypeStruct(gpk.shape, gpk.dtype),
        scratch_shapes=[pltpu.VMEM((GU_ROWS, D), F32), pltpu.VMEM((GU_ROWS, D), BF16), pltpu.SemaphoreType.DMA((2,))],
        input_output_aliases={2: 0}, compiler_params=_params(("arbitrary", "arbitrary")),
    )(act, dh, gpk)


def _ffn_dw_gu(dg, du, yf, gpk, layer, blk):
    lp = dg.shape[0]
    tk = _tile(lp, 1408, 16)
    nk = lp // tk

    def body(dg_ref, du_ref, y_ref, g_in, o_ref, acc):
        c, kk = pl.program_id(0), pl.program_id(1)

        @pl.when(kk == 0)
        def _():
            acc[...] = jnp.zeros_like(acc)

        @pl.when(c < 2)
        def _():
            acc[...] += tn(dg_ref[...], y_ref[...])

        @pl.when(c >= 2)
        def _():
            acc[...] += tn(du_ref[...], y_ref[...])

        @pl.when(kk == nk - 1)
        def _():
            o_ref[...] = acc[...].astype(BF16)

    return pl.pallas_call(
        body, name=f"d_w_gate_up{layer}", grid=(4, nk),
        in_specs=[pl.BlockSpec((tk, GU_ROWS), lambda c, kk: (kk, jnp.minimum(c, 1))),
                  pl.BlockSpec((tk, GU_ROWS), lambda c, kk: (kk, jnp.maximum(c - 2, 0))),
                  pl.BlockSpec((tk, D), lambda c, kk: (kk, 0)), ANY],
        out_specs=_gu_spec(lambda c, kk: (c, blk, 0)),
        out_shape=jax.ShapeDtypeStruct(gpk.shape, gpk.dtype),
        scratch_shapes=[pltpu.VMEM((GU_ROWS, D), F32)], input_output_aliases={3: 0},
        compiler_params=_params(("parallel", "arbitrary")),
    )(dg, du, yf, gpk)


def _loss_head(h, target):
    lp = h.shape[0]
    nb = lp // ROW0

    def body(h_ref, t_ref, dh_ref, l_ref):
        i = pl.program_id(0)

        @pl.when(i == 0)
        def _():
            l_ref[...] = jnp.zeros_like(l_ref)
            dh_ref[...] = jnp.zeros_like(dh_ref)

        @pl.when(i > 0)
        def _():
            err = h_ref[...] - t_ref[...]
            dh_ref[...] = err * (1.0 / D)
            l_ref[...] += jnp.sum(err * err) * (0.5 / D)

    return pl.pallas_call(
        body, name="loss_head", grid=(nb,),
        in_specs=[pl.BlockSpec((ROW0, D), lambda i: (i, 0)), pl.BlockSpec((ROW0, D), lambda i: (jnp.maximum(i - 1, 0), 0))],
        out_specs=[pl.BlockSpec((ROW0, D), lambda i: (i, 0)), pl.BlockSpec((8, LANES), lambda i: (0, 0))],
        out_shape=[jax.ShapeDtypeStruct((lp, D), F32), jax.ShapeDtypeStruct((8, LANES), F32)],
        compiler_params=_params(("arbitrary",)),
    )(h, target)


def _adamw(w, g, m, v, *, name):
    if w.ndim == 2:
        w, g, m, v = (t[None] for t in (w, g, m, v))
        return tuple(o[0] for o in _adamw(w, g, m, v, name=name))
    nl, r, c = w.shape
    tr = _tile(r, max(8, (1 << 19) // c), 8)

    def body(w_ref, g_ref, m_ref, v_ref, d_ref, nm_ref, nv_ref):
        d_ref[...], nm_ref[...], nv_ref[...] = _adam_math(w_ref[...], g_ref[...], m_ref[...], v_ref[...])

    spec = pl.BlockSpec((1, tr, c), lambda l, i: (l, i, 0))
    return tuple(pl.pallas_call(
        body, name=name, grid=(nl, r // tr), in_specs=[spec] * 4, out_specs=[spec] * 3,
        out_shape=[jax.ShapeDtypeStruct(w.shape, F32)] * 3, compiler_params=_params(("parallel", "parallel")),
    )(w, g, m, v))


def _adam_math(w, g, m, v):
    nm = ADAM_B1 * m + (1.0 - ADAM_B1) * g
    nv = ADAM_B2 * v + (1.0 - ADAM_B2) * (g * g)
    m_hat = nm / (1.0 - ADAM_B1 ** ADAM_STEP)
    v_hat = nv / (1.0 - ADAM_B2 ** ADAM_STEP)
    return -ADAM_LR * (m_hat / (jnp.sqrt(v_hat) + ADAM_EPS) + ADAM_WD * w), nm, nv


def _adamw_packed(w, gred0, gred, m, v, *, row0, row_off, transposed, name):
    nl, a, b = w.shape
    nr = b if transposed else a
    later = lambda l: row_off // nr + jnp.maximum(l - 1, 0)
    if transposed:
        ta = _tile(a, 256)
        wspec = pl.BlockSpec((1, ta, b), lambda l, r: (l, r, 0))
        g0spec = pl.BlockSpec((b, ta), lambda l, r: (row0 // nr, r))
        gspec = pl.BlockSpec((b, ta), lambda l, r: (later(l), r))
        grid = (nl, a // ta)
    else:
        wspec = pl.BlockSpec((1, a, b), lambda l, r: (l, 0, 0))
        g0spec = pl.BlockSpec((a, b), lambda l, r: (row0 // nr, 0))
        gspec = pl.BlockSpec((a, b), lambda l, r: (later(l), 0))
        grid = (nl, 1)

    def body(w_ref, g0_ref, g_ref, m_ref, v_ref, go_ref, d_ref, nm_ref, nv_ref):
        g = jnp.where(pl.program_id(0) == 0, g0_ref[...], g_ref[...])
        g = g.T if transposed else g
        d, nm, nv = _adam_math(w_ref[0], g, m_ref[0], v_ref[0])
        go_ref[0], d_ref[0], nm_ref[0], nv_ref[0] = g, d, nm, nv

    return pl.pallas_call(
        body, name=name, grid=grid, in_specs=[wspec, g0spec, gspec, wspec, wspec], out_specs=[wspec] * 4,
        out_shape=[jax.ShapeDtypeStruct(w.shape, F32)] * 4, compiler_params=_params(("parallel", "parallel")),
    )(w, gred0, gred, m, v)


FOX_AUG = FOX_H * LANES
L_C = 64
L_K = 67
L_LSE = 70
PAD_KEY = -30000.0
FOX_TQ = 384


def _head_sel(n_heads, width, lanes=LANES):
    r, c = _iota((n_heads * width, lanes), 0), _iota((n_heads * width, lanes), 1)
    down = (r // width == c).astype(BF16)
    r2, c2 = _iota((lanes, n_heads * width), 0), _iota((lanes, n_heads * width), 1)
    up = (c2 // width == r2).astype(BF16)
    return down, up


def _place(lane0):
    r, c = _iota((LANES, FOX_AUG), 0), _iota((LANES, FOX_AUG), 1)
    return [((c // LANES == r) & (c % LANES == lane0 + m)).astype(BF16) for m in range(3)]


def _placed(x, lane0):
    pcs = _split3(x)
    mats = _place(lane0)
    return nn(pcs[0], mats[0]) + nn(pcs[1], mats[1]) + nn(pcs[2], mats[2])


def _ones_at(rows, lanes):
    c = _iota((rows, FOX_AUG), 1) % LANES
    m = c == lanes[0]
    for l in lanes[1:]:
        m = m | (c == l)
    return m.astype(F32)


def _spread(x, extras, out_ref):
    rows = x.shape[0]
    left = _iota((rows, LANES), 1) < FOX_DH
    for p in range(FOX_H // 2):
        slab = x[:, p * LANES:(p + 1) * LANES]
        a = jnp.where(left, slab, extras[:, 2 * p * LANES:(2 * p + 1) * LANES])
        b = jnp.where(left, pltpu.roll(slab, FOX_DH, 1), extras[:, (2 * p + 1) * LANES:(2 * p + 2) * LANES])
        out_ref[:, 2 * p * LANES:(2 * p + 1) * LANES] = a.astype(BF16)
        out_ref[:, (2 * p + 1) * LANES:(2 * p + 2) * LANES] = b.astype(BF16)


def _fox_prep(proj, b_f, q_gain, k_gain):
    lp = proj.shape[0]
    nb = lp // LANES

    def body(p_ref, bf_ref, qg_ref, kg_ref, q_ref, k_ref, v_ref, carry):
        i = pl.program_id(0)

        @pl.when(i == 0)
        def _():
            carry[...] = jnp.zeros_like(carry)

        down, up = _head_sel(FOX_H, FOX_DH)

        def normed(x, gain):
            ms = _sel_r(x * x, down) * (1.0 / FOX_DH)
            r = _sel_r(lax.rsqrt(ms + EPS), up)
            return x * r * gain

        lane = _iota((LANES, LANES), 1)
        lf = jnp.where(lane < FOX_H, _log_sigmoid(p_ref[:, 4 * D:4 * D + LANES] + bf_ref[...]), 0.0)
        c = _sel_l(_tri(LANES).astype(BF16), lf) + carry[0:1, :]
        carry[...] = jnp.broadcast_to(c[LANES - 1:LANES, :], carry.shape)
        q_extra = _placed(c, L_C) + _ones_at(LANES, (L_K, L_K + 1, L_K + 2))
        row = i * LANES + _iota((LANES, FOX_AUG), 0)
        lane_a = _iota((LANES, FOX_AUG), 1) % LANES
        k_extra = -_placed(c, L_K) + _ones_at(LANES, (L_C, L_C + 1, L_C + 2, L_LSE, L_LSE + 1, L_LSE + 2))
        pad_val = jnp.where(lane_a == L_K, PAD_KEY, 0.0)
        k_extra = jnp.where((row < META0) & (lane_a >= L_K) & (lane_a < L_K + 3), pad_val, k_extra)
        v_extra = _ones_at(LANES, (L_C, L_C + 1, L_C + 2))
        _spread(normed(p_ref[:, 0:D], qg_ref[...]) * (FOX_DH ** -0.5), q_extra, q_ref)
        _spread(normed(p_ref[:, D:2 * D], kg_ref[...]), k_extra, k_ref)
        _spread(p_ref[:, 2 * D:3 * D], v_extra, v_ref)

    row = pl.BlockSpec((1, D), lambda i: (0, 0))
    aug = pl.BlockSpec((LANES, FOX_AUG), lambda i: (i, 0))
    return pl.pallas_call(
        body, name="fox_prep", grid=(nb,),
        in_specs=[pl.BlockSpec((LANES, FOX_INP), lambda i: (i, 0)), pl.BlockSpec((1, LANES), lambda i: (0, 0)), row, row],
        out_specs=[aug] * 3, out_shape=[jax.ShapeDtypeStruct((lp, FOX_AUG), BF16)] * 3,
        scratch_shapes=[pltpu.VMEM((8, LANES), F32)],
        compiler_params=_params(("arbitrary",)),
    )(proj, jnp.pad(b_f, (0, LANES - FOX_H)).reshape(1, LANES), jnp.tile(q_gain, FOX_H).reshape(1, D),
      jnp.tile(k_gain, FOX_H).reshape(1, D))


def _fox_attn_fwd(qa, ka, va, proj, ag=None):
    lp = qa.shape[0]
    tq = _tile(lp, FOX_TQ)
    nq = lp // tq
    npair = FOX_H // 2

    def body(q_ref, k_ref, v_ref, gate_ref, *rest):
        if ag is None:
            o_ref, og_ref, lse_ref = rest
        else:
            _, o_ref, og_ref, lse_ref, w_out, send_sems, recv_sems = rest
            copies = _AgCopies(w_out, ag[1], send_sems, recv_sems)

            @pl.when((pl.program_id(0) == 0) & (pl.program_id(1) == 0))
            def _():
                for r, k in copies.pairs():
                    copies.ici(r, k).start()

        i = pl.program_id(1)
        causal = _iota((tq, tq), 1) <= _iota((tq, tq), 0)
        qs = [q_ref[:, hh * LANES:(hh + 1) * LANES] for hh in range(2)]

        def block(j, carry, diag):
            off = pl.multiple_of(j * tq, tq)
            out = []
            for hh in range(2):
                m, acc = carry[hh]
                k = k_ref[pl.ds(off, tq), hh * LANES:(hh + 1) * LANES]
                v = v_ref[pl.ds(off, tq), hh * LANES:(hh + 1) * LANES]
                s = nt(qs[hh], k)
                if diag:
                    s = jnp.where(causal, s, -1e30)
                m2 = jnp.maximum(m, jnp.max(s, axis=-1, keepdims=True))
                p = jnp.exp(s - m2)
                p_hi = p.astype(BF16)
                p_lo = (p - p_hi.astype(F32)).astype(BF16)
                out.append((m2, jnp.exp(m - m2) * acc + nn(p_hi, v) + nn(p_lo, v)))
            return tuple(out)

        init = tuple((jnp.full((tq, 1), -1e30, F32), jnp.zeros((tq, LANES), F32)) for _ in range(2))
        carry = lax.fori_loop(0, i, lambda j, c: block(j, c, False), init)
        carry = block(i, carry, True)
        outs, lses = [], []
        for hh in range(2):
            m, acc = carry[hh]
            l = acc[:, L_C:L_C + 1]
            outs.append(acc / l)
            lses.append(jnp.broadcast_to(m + jnp.log(l), (tq, LANES)))
        left = _iota((tq, LANES), 1) < FOX_DH
        o = jnp.where(left, outs[0], pltpu.roll(outs[1], FOX_DH, 1))
        o_ref[...] = o
        og_ref[...] = (o * _sigmoid(gate_ref[...])).astype(BF16)
        lse_ref[...] = jnp.where(left, lses[0], lses[1])

        if ag is not None:
            @pl.when((pl.program_id(0) == npair - 1) & (pl.program_id(1) == nq - 1))
            def _():
                for r, k in copies.pairs():
                    copies.ici_arrival(r, k).wait_recv()
                for r, k in copies.pairs():
                    copies.ici(r, k).wait_send()

    qspec = pl.BlockSpec((tq, 2 * LANES), lambda p, i: (i, p))
    kspec = pl.BlockSpec((lp, 2 * LANES), lambda p, i: (0, p))
    ospec = pl.BlockSpec((tq, LANES), lambda p, i: (i, p))
    ins, in_specs = [qa, ka, va, proj], [qspec, kspec, kspec, pl.BlockSpec((tq, LANES), lambda p, i: (i, 3 * D // LANES + p))]
    out_specs = [ospec] * 3
    out_shape = [jax.ShapeDtypeStruct((lp, D), F32), jax.ShapeDtypeStruct((lp, D), BF16), jax.ShapeDtypeStruct((lp, D), F32)]
    if ag is None:
        return pl.pallas_call(body, name="fox_attn_fwd", grid=(npair, nq), in_specs=in_specs, out_specs=out_specs,
                              out_shape=out_shape, compiler_params=_params(("parallel", "arbitrary")))(*ins)
    n = 3 * len(ag[1])
    return pl.pallas_call(
        body, name="fox_attn_fwd_ag", grid=(npair, nq), in_specs=in_specs + [ANY], out_specs=out_specs + [ANY],
        out_shape=out_shape + [jax.ShapeDtypeStruct(ag[0].shape, ag[0].dtype)],
        scratch_shapes=[pltpu.SemaphoreType.DMA((n,))] * 2, input_output_aliases={4: 3},
        compiler_params=_params(("arbitrary", "arbitrary")),
    )(*ins, ag[0])


def _fox_gate_bwd(dog, o, proj, lse, qa):
    lp = o.shape[0]
    tr = LANES

    def body(d_ref, o_ref, g_ref, lse_ref, q_ref, do_ref, q2_ref, dgate_ref):
        down, _ = _head_sel(FOX_H, FOX_DH)
        sg = _sigmoid(g_ref[...])
        dv, ov = d_ref[...], o_ref[...]
        do = (dv * sg).astype(BF16).astype(F32)
        dgate_ref[...] = dv * ov * sg * (1.0 - sg)
        delta = _sel_r(do * ov, down)
        _spread(do, -_placed(delta, L_C), do_ref)
        r_, c_ = _iota((D, LANES), 0), _iota((D, LANES), 1)
        lse_c = _sel_r(lse_ref[...], (r_ == c_ * FOX_DH).astype(BF16))
        q2_ref[...] = (q_ref[...].astype(F32) - _placed(lse_c, L_LSE)).astype(BF16)

    spec = pl.BlockSpec((tr, D), lambda i: (i, 0))
    aug = pl.BlockSpec((tr, FOX_AUG), lambda i: (i, 0))
    return pl.pallas_call(
        body, name="fox_gate_bwd", grid=(lp // tr,),
        in_specs=[spec, spec, pl.BlockSpec((tr, D), lambda i: (i, 3)), spec, aug], out_specs=[aug, aug, spec],
        out_shape=[jax.ShapeDtypeStruct((lp, FOX_AUG), BF16), jax.ShapeDtypeStruct((lp, FOX_AUG), BF16),
                   jax.ShapeDtypeStruct((lp, D), F32)],
        compiler_params=_params(("parallel",)),
    )(dog, o, proj, lse, qa)


def _fox_attn_bwd(q2, ka, va, doa, rs=None):
    lp = q2.shape[0]
    t = _tile(lp, FOX_TQ)
    nb = lp // t
    npair = FOX_H // 2

    def body(q_ref, k_ref, v_ref, do_ref, *rest):
        if rs is None:
            dq_ref, dk_ref, dv_ref, dc_ref, dq_acc, dk_acc, dv_acc, dc_acc = rest
        else:
            s_ref, dq_ref, dk_ref, dv_ref, dc_ref, got_ref, dq_acc, dk_acc, dv_acc, dc_acc, send_sems, recv_sems = rest
            sends, arrivals = _scatter_copies(s_ref, got_ref, send_sems, recv_sems)

            @pl.when((pl.program_id(0) == 0) & (pl.program_id(1) == 0))
            def _():
                for cp in sends:
                    cp.start()

            @pl.when((pl.program_id(0) == npair - 1) & (pl.program_id(1) == nb - 1))
            def _():
                for cp in arrivals:
                    cp.wait_recv()
                for cp in sends:
                    cp.wait_send()

        j = pl.program_id(1)

        @pl.when(j == 0)
        def _():
            dq_acc[...] = jnp.zeros_like(dq_acc)

        causal = _iota((t, t), 1) <= _iota((t, t), 0)
        ks = [k_ref[:, hh * LANES:(hh + 1) * LANES] for hh in range(2)]
        vs = [v_ref[:, hh * LANES:(hh + 1) * LANES] for hh in range(2)]
        dk_acc[...] = jnp.zeros_like(dk_acc)
        dv_acc[...] = jnp.zeros_like(dv_acc)
        dc_acc[...] = jnp.zeros_like(dc_acc)

        def block(i, diag):
            off = pl.multiple_of(i * t, t)
            for hh in range(2):
                q = q_ref[pl.ds(off, t), hh * LANES:(hh + 1) * LANES]
                do = do_ref[pl.ds(off, t), hh * LANES:(hh + 1) * LANES]
                s = nt(q, ks[hh])
                if diag:
                    s = jnp.where(causal, s, -1e30)
                p = jnp.exp(s)
                ds = p * nt(do, vs[hh])
                dc_acc[hh] += jnp.sum(ds, axis=0, keepdims=True)
                dsb = ds.astype(BF16)
                dv_acc[hh] += tn(p.astype(BF16), do)
                dk_acc[hh] += tn(dsb, q)
                dq_acc[hh, pl.ds(off, t), :] += nn(dsb, ks[hh])

        block(j, True)

        def step(i, c):
            block(i, False)
            return c

        lax.fori_loop(j + 1, nb, step, 0)
        left = _iota((t, LANES), 1) < FOX_DH
        dk_ref[...] = jnp.where(left, dk_acc[0], pltpu.roll(dk_acc[1], FOX_DH, 1))
        dv_ref[...] = jnp.where(left, dv_acc[0], pltpu.roll(dv_acc[1], FOX_DH, 1))
        for hh in range(2):
            dc_ref[hh] = jnp.broadcast_to(-dc_acc[hh], (8, t))

        @pl.when(j == nb - 1)
        def _():
            left = _iota((lp, LANES), 1) < FOX_DH
            dq_ref[...] = jnp.where(left, dq_acc[0], pltpu.roll(dq_acc[1], FOX_DH, 1))

    full = pl.BlockSpec((lp, 2 * LANES), lambda p, j: (0, p))
    kblk = pl.BlockSpec((t, 2 * LANES), lambda p, j: (j, p))
    oblk = pl.BlockSpec((t, LANES), lambda p, j: (j, p))
    in_specs = [full, kblk, kblk, full]
    out_specs = [pl.BlockSpec((lp, LANES), lambda p, j: (0, p)), oblk, oblk, pl.BlockSpec((2, 8, t), lambda p, j: (p, 0, j))]
    out_shape = [jax.ShapeDtypeStruct((lp, D), F32)] * 3 + [jax.ShapeDtypeStruct((FOX_H, 8, lp), F32)]
    scratch = [pltpu.VMEM((2, lp, LANES), F32), pltpu.VMEM((2, t, LANES), F32), pltpu.VMEM((2, t, LANES), F32),
               pltpu.VMEM((2, 1, t), F32)]
    if rs is None:
        return pl.pallas_call(body, name="fox_attn_bwd", grid=(npair, nb), in_specs=in_specs, out_specs=out_specs,
                              out_shape=out_shape, scratch_shapes=scratch,
                              compiler_params=_params(("parallel", "arbitrary")))(q2, ka, va, doa)
    return pl.pallas_call(
        body, name="fox_attn_bwd_rs", grid=(npair, nb), in_specs=in_specs + [ANY], out_specs=out_specs + [ANY],
        out_shape=out_shape + [jax.ShapeDtypeStruct((3,) + rs.shape[1:], rs.dtype)],
        scratch_shapes=scratch + [pltpu.SemaphoreType.DMA((3,)), pltpu.SemaphoreType.DMA((3,))],
        compiler_params=_params(("arbitrary", "arbitrary")),
    )(q2, ka, va, doa, rs)


def _fox_prep_bwd(proj, b_f, q_gain, k_gain, dqn, dkn, dv, dgate, dct):
    lp = proj.shape[0]
    nb = lp // LANES

    def body(p_ref, bf_ref, qg_ref, kg_ref, dq_ref, dk_ref, dv_ref, dg_ref, dc_ref,
             dp_ref, dqg_ref, dkg_ref, dbf_ref, carry):
        i = pl.program_id(0)

        @pl.when(i == 0)
        def _():
            carry[...] = jnp.zeros_like(carry)
            dqg_ref[...] = jnp.zeros_like(dqg_ref)
            dkg_ref[...] = jnp.zeros_like(dkg_ref)
            dbf_ref[...] = jnp.zeros_like(dbf_ref)

        down, up = _head_sel(FOX_H, FOX_DH)

        def norm_bwd(x, gain, dy, scale, dgain_ref):
            ms = _sel_r(x * x, down) * (1.0 / FOX_DH)
            r = _sel_r(lax.rsqrt(ms + EPS), up)
            u = dy * gain * scale
            mean_xu = _sel_r(_sel_r(x * u, down) * (1.0 / FOX_DH), up)
            dgain_ref[...] += jnp.sum(dy * scale * x * r, axis=0, keepdims=True)
            return r * u - x * (r * r * r) * mean_xu

        dp_ref[:, 0:D] = norm_bwd(p_ref[:, 0:D], qg_ref[...], dq_ref[...], FOX_DH ** -0.5, dqg_ref).astype(BF16)
        dp_ref[:, D:2 * D] = norm_bwd(p_ref[:, D:2 * D], kg_ref[...], dk_ref[...], 1.0, dkg_ref).astype(BF16)
        dp_ref[:, 2 * D:3 * D] = dv_ref[...].astype(BF16)
        dp_ref[:, 3 * D:4 * D] = dg_ref[...].astype(BF16)
        rows = jnp.concatenate([dc_ref[h, 0:1, :] for h in range(FOX_H)] + [jnp.zeros((LANES - FOX_H, LANES), F32)], axis=0)
        dlf = _sel_l(_tri(LANES, upper=True).astype(BF16), rows.T) + carry[0:1, :]
        carry[...] = jnp.broadcast_to(dlf[0:1, :], carry.shape)
        lane = _iota((LANES, LANES), 1)
        z = p_ref[:, 4 * D:4 * D + LANES] + bf_ref[...]
        df = jnp.where(lane < FOX_H, dlf * _sigmoid(-z), 0.0)
        dp_ref[:, 4 * D:4 * D + LANES] = df.astype(BF16)
        dbf_ref[...] += jnp.sum(df, axis=0, keepdims=True)

    rev = lambda i: (nb - 1 - i, 0)
    blk = pl.BlockSpec((LANES, D), rev)
    row = pl.BlockSpec((1, D), lambda i: (0, 0))
    row128 = pl.BlockSpec((1, LANES), lambda i: (0, 0))
    return pl.pallas_call(
        body, name="fox_prep_bwd", grid=(nb,),
        in_specs=[pl.BlockSpec((LANES, FOX_INP), rev), row128, row, row, blk, blk, blk, blk,
                  pl.BlockSpec((FOX_H, 8, LANES), lambda i: (0, 0, nb - 1 - i))],
        out_specs=[pl.BlockSpec((LANES, FOX_INP), rev), row, row, row128],
        out_shape=[jax.ShapeDtypeStruct((lp, FOX_INP), BF16), jax.ShapeDtypeStruct((1, D), F32),
                   jax.ShapeDtypeStruct((1, D), F32), jax.ShapeDtypeStruct((1, LANES), F32)],
        scratch_shapes=[pltpu.VMEM((8, LANES), F32)],
        compiler_params=_params(("arbitrary",)),
    )(proj, jnp.pad(b_f, (0, LANES - FOX_H)).reshape(1, LANES), jnp.tile(q_gain, FOX_H).reshape(1, D),
      jnp.tile(k_gain, FOX_H).reshape(1, D), dqn, dkn, dv, dgate, dct)


def _gla_gates(p_ref, wa_ref, ba_ref):
    a_lr = p_ref[:, 3072:3072 + LANES]
    z = nn(a_lr.astype(BF16), wa_ref[...].astype(BF16)) + ba_ref[...]
    g = _log_sigmoid(z) * (1.0 / GLA_NORM)
    b = _sel_l(_tri(CHUNK).astype(BF16), g)
    return a_lr, z, b


def _gla_head_fwd(q, k, v, b, st0):
    eb = jnp.exp(b)
    bl = b[CHUNK - 1:CHUNK, :]
    qe, ke, kd = q * eb, k * jnp.exp(-b), k * jnp.exp(bl - b)
    a = jnp.where(_tri(CHUNK), nt(qe, ke), 0.0)
    o = nn(a, v) + nt(qe, st0)
    st1 = st0 * jnp.exp(bl) + tn(v, kd)
    return o, st1, (qe, ke, kd, a, bl)


def _gla_fwd(proj, w_alpha2, b_alpha, o_gain):
    lp = proj.shape[0]
    nc = lp // CHUNK

    def body(p_ref, wa_ref, ba_ref, og_ref, o_ref, y_ref, s_ref, st):
        @pl.when(pl.program_id(0) == 0)
        def _():
            st[...] = jnp.zeros_like(st)

        _, _, b = _gla_gates(p_ref, wa_ref, ba_ref)
        for h in range(GLA_H):
            q = p_ref[:, h * GLA_DK:(h + 1) * GLA_DK] * (GLA_DK ** -0.5)
            k = p_ref[:, GLA_QK + h * GLA_DK:GLA_QK + (h + 1) * GLA_DK]
            v = p_ref[:, 2 * GLA_QK + h * GLA_DV:2 * GLA_QK + (h + 1) * GLA_DV]
            r = p_ref[:, 2 * GLA_QK + GLA_V + h * GLA_DV:2 * GLA_QK + GLA_V + (h + 1) * GLA_DV]
            st0 = st[h]
            s_ref[0, h] = st0
            o, st1, _ = _gla_head_fwd(q, k, v, b[:, h * GLA_DK:(h + 1) * GLA_DK], st0)
            st[h] = st1
            o_ref[:, h * GLA_DV:(h + 1) * GLA_DV] = o
            rs = lax.rsqrt(jnp.mean(o * o, axis=-1, keepdims=True) + EPS)
            y_ref[:, h * GLA_DV:(h + 1) * GLA_DV] = (o * rs * og_ref[...] * _silu(r)).astype(BF16)

    blk = pl.BlockSpec((CHUNK, D), lambda i: (i, 0))
    return pl.pallas_call(
        body, name="gla_fwd", grid=(nc,),
        in_specs=[pl.BlockSpec((CHUNK, GLA_INP), lambda i: (i, 0)), pl.BlockSpec((LANES, GLA_QK), lambda i: (0, 0)),
                  pl.BlockSpec((1, GLA_QK), lambda i: (0, 0)), pl.BlockSpec((1, GLA_DV), lambda i: (0, 0))],
        out_specs=[blk, blk, pl.BlockSpec((1, GLA_H, GLA_DV, GLA_DK), lambda i: (i, 0, 0, 0))],
        out_shape=[jax.ShapeDtypeStruct((lp, D), F32), jax.ShapeDtypeStruct((lp, D), BF16),
                   jax.ShapeDtypeStruct((nc, GLA_H, GLA_DV, GLA_DK), F32)],
        scratch_shapes=[pltpu.VMEM((GLA_H, GLA_DV, GLA_DK), F32)],
        compiler_params=_params(("arbitrary",)),
    )(proj, jnp.pad(w_alpha2, ((0, LANES - GLA_RANK), (0, 0))), b_alpha.reshape(1, GLA_QK), o_gain.reshape(1, GLA_DV))


def _gla_bwd(proj, w_alpha2, b_alpha, o_gain, o, states, dy):
    lp = proj.shape[0]
    nc = lp // CHUNK

    def body(p_ref, wa_ref, ba_ref, og_ref, o_ref, s_ref, dy_ref, dp_ref, dwa_ref, dba_ref, dog_ref, dst):
        @pl.when(pl.program_id(0) == 0)
        def _():
            dst[...] = jnp.zeros_like(dst)
            dwa_ref[...] = jnp.zeros_like(dwa_ref)
            dba_ref[...] = jnp.zeros_like(dba_ref)
            dog_ref[...] = jnp.zeros_like(dog_ref)

        a_lr, z, b_all = _gla_gates(p_ref, wa_ref, ba_ref)
        last_row = _iota((CHUNK, GLA_DK), 0) == CHUNK - 1
        rev = _tri(CHUNK, upper=True).astype(BF16)
        dg_parts = []
        for h in range(GLA_H):
            scale = GLA_DK ** -0.5
            q = p_ref[:, h * GLA_DK:(h + 1) * GLA_DK] * scale
            k = p_ref[:, GLA_QK + h * GLA_DK:GLA_QK + (h + 1) * GLA_DK]
            v = p_ref[:, 2 * GLA_QK + h * GLA_DV:2 * GLA_QK + (h + 1) * GLA_DV]
            r = p_ref[:, 2 * GLA_QK + GLA_V + h * GLA_DV:2 * GLA_QK + GLA_V + (h + 1) * GLA_DV]
            b = b_all[:, h * GLA_DK:(h + 1) * GLA_DK]
            st0 = s_ref[0, h]
            dst1 = dst[h]
            ov = o_ref[:, h * GLA_DV:(h + 1) * GLA_DV]
            dyv = dy_ref[:, h * GLA_DV:(h + 1) * GLA_DV]
            rs = lax.rsqrt(jnp.mean(ov * ov, axis=-1, keepdims=True) + EPS)
            on = ov * rs
            dr = dyv * on * og_ref[...] * _dsilu(r)
            don = dyv * _silu(r)
            dog_ref[...] += jnp.sum(don * on, axis=0, keepdims=True)
            u = don * og_ref[...]
            do = rs * u - ov * (rs * rs * rs) * jnp.mean(ov * u, axis=-1, keepdims=True)
            eb = jnp.exp(b)
            _, _, (qe, ke, kd, a, bl) = _gla_head_fwd(q, k, v, b, st0)
            da = jnp.where(_tri(CHUNK), nt(do, v), 0.0)
            dkd = nn(v, dst1)
            dvv = tn(a, do) + nt(kd, dst1)
            dqe = nn(da, ke) + nn(do, st0)
            dke = tn(da, qe)
            ebl = jnp.exp(bl)
            dst[h] = dst1 * ebl + tn(do, qe)
            db = dqe * qe - dke * ke - dkd * kd
            db_last = jnp.sum(dkd * kd, axis=0, keepdims=True) + jnp.sum(dst1 * st0, axis=0, keepdims=True) * ebl
            db = db + jnp.where(last_row, db_last, 0.0)
            dg_parts.append(_sel_l(rev, db))
            dp_ref[:, h * GLA_DK:(h + 1) * GLA_DK] = (dqe * eb * scale).astype(BF16)
            dp_ref[:, GLA_QK + h * GLA_DK:GLA_QK + (h + 1) * GLA_DK] = (dke * jnp.exp(-b) + dkd * jnp.exp(bl - b)).astype(BF16)
            dp_ref[:, 2 * GLA_QK + h * GLA_DV:2 * GLA_QK + (h + 1) * GLA_DV] = dvv.astype(BF16)
            dp_ref[:, 2 * GLA_QK + GLA_V + h * GLA_DV:2 * GLA_QK + GLA_V + (h + 1) * GLA_DV] = dr.astype(BF16)
        dg = jnp.concatenate(dg_parts, axis=1)
        dz = dg * (1.0 / GLA_NORM) * _sigmoid(-z)
        dzb = dz.astype(BF16)
        dp_ref[:, 3072:3072 + LANES] = nt(dzb, wa_ref[...].astype(BF16)).astype(BF16)
        dwa_ref[...] += tn(a_lr.astype(BF16), dzb)
        dba_ref[...] += jnp.sum(dz, axis=0, keepdims=True)

    rv = lambda i: (nc - 1 - i, 0)
    blk = pl.BlockSpec((CHUNK, D), rv)
    fixed = lambda r, c: pl.BlockSpec((r, c), lambda i: (0, 0))
    return pl.pallas_call(
        body, name="gla_bwd", grid=(nc,),
        in_specs=[pl.BlockSpec((CHUNK, GLA_INP), rv), fixed(LANES, GLA_QK), fixed(1, GLA_QK), fixed(1, GLA_DV), blk,
                  pl.BlockSpec((1, GLA_H, GLA_DV, GLA_DK), lambda i: (nc - 1 - i, 0, 0, 0)), blk],
        out_specs=[pl.BlockSpec((CHUNK, GLA_INP), rv), fixed(LANES, GLA_QK), fixed(1, GLA_QK), fixed(1, GLA_DV)],
        out_shape=[jax.ShapeDtypeStruct((lp, GLA_INP), BF16), jax.ShapeDtypeStruct((LANES, GLA_QK), F32),
                   jax.ShapeDtypeStruct((1, GLA_QK), F32), jax.ShapeDtypeStruct((1, GLA_DV), F32)],
        scratch_shapes=[pltpu.VMEM((GLA_H, GLA_DV, GLA_DK), F32)],
        compiler_params=_params(("arbitrary",)),
    )(proj, jnp.pad(w_alpha2, ((0, LANES - GLA_RANK), (0, 0))), b_alpha.reshape(1, GLA_QK), o_gain.reshape(1, GLA_DV),
      o, states, dy)


HI = lax.Precision.HIGHEST


def _gdn_pre(prev_ref, p_ref, cw_ref, al_ref, dt_ref):
    xc = jnp.concatenate([prev_ref[:, 0:GDN_CONV], p_ref[:, 0:GDN_CONV]], axis=0)
    shifted = [pltpu.roll(xc, 3 - j, 0)[CHUNK:, :] if j < 3 else xc[CHUNK:, :] for j in range(4)]
    conv = sum(shifted[j] * cw_ref[j:j + 1, :] for j in range(4))
    act = _silu(conv)
    slab = p_ref[:, 4096:4096 + LANES]
    lane = _iota((CHUNK, LANES), 1)
    zs = slab + dt_ref[...]
    g = jnp.where(lane < GDN_H, -jnp.exp(al_ref[...]) * _softplus(zs), 0.0)
    bs = _sel_l(_tri(CHUNK).astype(BF16), g)
    beta = _sigmoid(slab)
    return shifted, conv, act, slab, zs, g, bs, beta


def _l2n(x):
    r = lax.rsqrt(jnp.sum(x * x, axis=-1, keepdims=True) + EPS)
    return x * r, r


def _gdn_chunk_fwd(q, k, v, beta, bcol, brow, s0):
    hs = range(len(q))
    ii, jj = _iota((CHUNK, CHUNK), 0), _iota((CHUNK, CHUNK), 1)
    low, eye = ii >= jj, (ii == jj).astype(F32)
    dm = [jnp.where(low, jnp.exp(jnp.where(low, bcol[h] - brow[h], 0.0)), 0.0) for h in hs]
    dstrict = [jnp.where(ii > jj, dm[h], 0.0) for h in hs]
    eb = [jnp.exp(bcol[h]) for h in hs]
    bl = [bcol[h][CHUNK - 1:CHUNK, :] for h in hs]
    kb = [k[h] * beta[h] for h in hs]
    vb = [v[h] * beta[h] for h in hs]
    nmat = [nt(kb[h], k[h]) * dstrict[h] for h in hs]
    x = [eye - nmat[h] for h in hs]
    pw = [nn(nmat[h], nmat[h], precision=HI) for h in hs]
    for it in range(5):
        x = [x[h] + nn(x[h], pw[h], precision=HI) for h in hs]
        if it < 4:
            pw = [nn(pw[h], pw[h], precision=HI) for h in hs]
    kbe = [kb[h] * eb[h] for h in hs]
    u = [nn(x[h], vb[h], precision=HI) for h in hs]
    w = [nn(x[h], kbe[h], precision=HI) for h in hs]
    vn = [u[h] - nn(w[h], s0[h]) for h in hs]
    pm = [nt(q[h], k[h]) * dm[h] for h in hs]
    qe = [q[h] * eb[h] for h in hs]
    o = [nn(pm[h], vn[h]) + nn(qe[h], s0[h]) for h in hs]
    kd = [k[h] * jnp.exp(bl[h] - bcol[h]) for h in hs]
    s1 = [s0[h] * jnp.exp(bl[h]) + tn(kd[h], vn[h]) for h in hs]
    return o, s1, dict(dm=dm, dstrict=dstrict, eb=eb, bl=bl, kb=kb, vb=vb, nmat=nmat, tinv=x, kbe=kbe, u=u, w=w, vn=vn,
                       pm=pm, qe=qe, kd=kd)


def _gdn_heads(act, beta_slab, bs, h):
    qa = act[:, h * GDN_DK:(h + 1) * GDN_DK]
    ka = act[:, GDN_H * GDN_DK + h * GDN_DK:GDN_H * GDN_DK + (h + 1) * GDN_DK]
    v = act[:, 2 * GDN_H * GDN_DK + h * GDN_DV:2 * GDN_H * GDN_DK + (h + 1) * GDN_DV]
    return qa, ka, v, beta_slab[:, GDN_H + h:GDN_H + h + 1], bs[:, h:h + 1]


def _gdn_fwd(proj, conv_w, a_log, dt_bias, o_gain):
    lp = proj.shape[0]
    nc = lp // CHUNK

    def body(prev_ref, p_ref, cw_ref, al_ref, dt_ref, og_ref, o_ref, y_ref, s_ref, st):
        @pl.when(pl.program_id(0) == 0)
        def _():
            st[...] = jnp.zeros_like(st)

        _, _, act, _, _, _, bs, beta = _gdn_pre(prev_ref, p_ref, cw_ref, al_ref, dt_ref)
        bst = bs.T
        hs = range(GDN_H)
        parts = [_gdn_heads(act, beta, bs, h) for h in hs]
        q = [_l2n(parts[h][0])[0] * (GDN_DK ** -0.5) for h in hs]
        k = [_l2n(parts[h][1])[0] for h in hs]
        s0 = [st[h] for h in hs]
        for h in hs:
            s_ref[0, h] = s0[h]
        o, s1, _ = _gdn_chunk_fwd(q, k, [parts[h][2] for h in hs], [parts[h][3] for h in hs], [parts[h][4] for h in hs],
                                  [bst[h:h + 1, :] for h in hs], s0)
        for h in hs:
            st[h] = s1[h]
            o_ref[:, h * GDN_DV:(h + 1) * GDN_DV] = o[h]
            rs = lax.rsqrt(jnp.mean(o[h] * o[h], axis=-1, keepdims=True) + EPS)
            gate = p_ref[:, GDN_CONV + h * GDN_DV:GDN_CONV + (h + 1) * GDN_DV]
            y_ref[:, h * GDN_DV:(h + 1) * GDN_DV] = (o[h] * rs * og_ref[...] * _silu(gate)).astype(BF16)

    blk = pl.BlockSpec((CHUNK, D), lambda i: (i, 0))
    fixed = lambda r, c: pl.BlockSpec((r, c), lambda i: (0, 0))
    return pl.pallas_call(
        body, name="gdn_fwd", grid=(nc,),
        in_specs=[pl.BlockSpec((CHUNK, GDN_INP), lambda i: (jnp.maximum(i - 1, 0), 0)),
                  pl.BlockSpec((CHUNK, GDN_INP), lambda i: (i, 0)), fixed(8, GDN_CONV), fixed(1, LANES), fixed(1, LANES),
                  fixed(1, GDN_DV)],
        out_specs=[blk, blk, pl.BlockSpec((1, GDN_H, GDN_DK, GDN_DV), lambda i: (i, 0, 0, 0))],
        out_shape=[jax.ShapeDtypeStruct((lp, D), F32), jax.ShapeDtypeStruct((lp, D), BF16),
                   jax.ShapeDtypeStruct((nc, GDN_H, GDN_DK, GDN_DV), F32)],
        scratch_shapes=[pltpu.VMEM((GDN_H, GDN_DK, GDN_DV), F32)],
        compiler_params=_params(("arbitrary",)),
    )(proj, proj, jnp.pad(conv_w.reshape(4, GDN_CONV), ((0, 4), (0, 0))), jnp.pad(a_log, (0, LANES - GDN_H)).reshape(1, LANES),
      jnp.pad(dt_bias, (0, LANES - GDN_H)).reshape(1, LANES), o_gain.reshape(1, GDN_DV))


def _gdn_bwd(proj, conv_w, a_log, dt_bias, o_gain, o, states, dy):
    lp = proj.shape[0]
    nc = lp // CHUNK

    def body(prev_ref, p_ref, cw_ref, al_ref, dt_ref, og_ref, o_ref, s_ref, dy_ref,
             dp_ref, dcw_ref, dal_ref, ddt_ref, dog_ref, dst, dconv_next):
        @pl.when(pl.program_id(0) == 0)
        def _():
            dst[...] = jnp.zeros_like(dst)
            dconv_next[...] = jnp.zeros_like(dconv_next)
            dcw_ref[...] = jnp.zeros_like(dcw_ref)
            dal_ref[...] = jnp.zeros_like(dal_ref)
            ddt_ref[...] = jnp.zeros_like(ddt_ref)
            dog_ref[...] = jnp.zeros_like(dog_ref)

        shifted, conv, act, slab, zs, g, bs, beta = _gdn_pre(prev_ref, p_ref, cw_ref, al_ref, dt_ref)
        bst = bs.T
        lane = _iota((CHUNK, LANES), 1)
        ones = jnp.ones((CHUNK, LANES), F32)
        db_slab = jnp.zeros((CHUNK, LANES), F32)
        dbeta_slab = jnp.zeros((CHUNK, LANES), F32)
        last_row = _iota((CHUNK, 1), 0) == CHUNK - 1
        hs = range(GDN_H)
        scale = GDN_DK ** -0.5
        parts = [_gdn_heads(act, beta, bs, h) for h in hs]
        qa, ka, v = [parts[h][0] for h in hs], [parts[h][1] for h in hs], [parts[h][2] for h in hs]
        bet, bcol = [parts[h][3] for h in hs], [parts[h][4] for h in hs]
        qn_ = [_l2n(qa[h]) for h in hs]
        kn_ = [_l2n(ka[h]) for h in hs]
        q = [qn_[h][0] * scale for h in hs]
        k, rq, rk = [kn_[h][0] for h in hs], [qn_[h][1] for h in hs], [kn_[h][1] for h in hs]
        s0 = [s_ref[0, h] for h in hs]
        ds1 = [dst[h] for h in hs]
        do = []
        for h in hs:
            ov = o_ref[:, h * GDN_DV:(h + 1) * GDN_DV]
            dyv = dy_ref[:, h * GDN_DV:(h + 1) * GDN_DV]
            gate = p_ref[:, GDN_CONV + h * GDN_DV:GDN_CONV + (h + 1) * GDN_DV]
            rs = lax.rsqrt(jnp.mean(ov * ov, axis=-1, keepdims=True) + EPS)
            on = ov * rs
            dp_ref[:, GDN_CONV + h * GDN_DV:GDN_CONV + (h + 1) * GDN_DV] = (dyv * on * og_ref[...] * _dsilu(gate)).astype(BF16)
            don = dyv * _silu(gate)
            dog_ref[...] += jnp.sum(don * on, axis=0, keepdims=True)
            uu = don * og_ref[...]
            do.append(rs * uu - ov * (rs * rs * rs) * jnp.mean(ov * uu, axis=-1, keepdims=True))
        _, _, f = _gdn_chunk_fwd(q, k, v, bet, bcol, [bst[h:h + 1, :] for h in hs], s0)
        dm, dstrict, eb, bl, kb, nmat, tinv = f["dm"], f["dstrict"], f["eb"], f["bl"], f["kb"], f["nmat"], f["tinv"]
        kbe, u, w, vn, pm, qe, kd = f["kbe"], f["u"], f["w"], f["vn"], f["pm"], f["qe"], f["kd"]
        ebl = [jnp.exp(bl[h]) for h in hs]
        dvn = [tn(pm[h], do[h]) + nn(kd[h], ds1[h]) for h in hs]
        dpr = [nt(do[h], vn[h]) for h in hs]
        dqe = [nt(do[h], s0[h]) for h in hs]
        dkd = [nt(vn[h], ds1[h]) for h in hs]
        for h in hs:
            dst[h] = ds1[h] * ebl[h] + tn(qe[h], do[h]) - tn(w[h], dvn[h])
        du_ = [tn(tinv[h], dvn[h], precision=HI) for h in hs]
        dw_ = [tn(tinv[h], -nt(dvn[h], s0[h]), precision=HI) for h in hs]
        dn = [-(nt(du_[h], u[h]) + nt(dw_[h], w[h])) for h in hs]
        dqk = [dpr[h] * dm[h] for h in hs]
        dkk = [dn[h] * dstrict[h] for h in hs]
        gsum = [dpr[h] * pm[h] + dn[h] * nmat[h] for h in hs]
        dkb = [nn(dkk[h], k[h]) + dw_[h] * eb[h] for h in hs]
        dk = [tn(dkk[h], kb[h]) + tn(dqk[h], q[h]) + dkd[h] * jnp.exp(bl[h] - bcol[h]) + dkb[h] * bet[h] for h in hs]
        dq = [nn(dqk[h], k[h]) + dqe[h] * eb[h] for h in hs]
        colsum = [tn(gsum[h], ones, precision=HI)[:, 0:1] for h in hs]
        dact_q, dact_k, dact_v = [], [], []
        for h in hs:
            dbeta = jnp.sum(dkb[h] * k[h], axis=-1, keepdims=True) + jnp.sum(du_[h] * v[h], axis=-1, keepdims=True)
            skd = jnp.sum(dkd[h] * kd[h], axis=-1, keepdims=True)
            db = (jnp.sum(gsum[h], axis=-1, keepdims=True) - colsum[h] + jnp.sum(dqe[h] * qe[h], axis=-1, keepdims=True)
                  + jnp.sum(dw_[h] * kbe[h], axis=-1, keepdims=True) - skd)
            db_last = jnp.sum(skd, axis=0, keepdims=True) + jnp.sum(ds1[h] * s0[h]) * ebl[h]
            db = db + jnp.where(last_row, db_last, 0.0)
            db_slab = db_slab + jnp.where(lane == h, db, 0.0)
            dbeta_slab = dbeta_slab + jnp.where(lane == GDN_H + h, dbeta, 0.0)
            dqn = dq[h] * scale
            dact_q.append(rq[h] * dqn - qa[h] * (rq[h] * rq[h] * rq[h]) * jnp.sum(qa[h] * dqn, axis=-1, keepdims=True))
            dact_k.append(rk[h] * dk[h] - ka[h] * (rk[h] * rk[h] * rk[h]) * jnp.sum(ka[h] * dk[h], axis=-1, keepdims=True))
            dact_v.append(du_[h] * bet[h])
        dact = jnp.concatenate(dact_q + dact_k + dact_v, axis=1)
        dconv = dact * _dsilu(conv)
        for j in range(4):
            dcw_ref[j:j + 1, :] += jnp.sum(dconv * shifted[j], axis=0, keepdims=True)
        dcat = jnp.concatenate([dconv, dconv_next[...]], axis=0)
        dx = dconv * cw_ref[3:4, :]
        for j in range(3):
            dx = dx + pltpu.roll(dcat, 2 * CHUNK - (3 - j), 0)[:CHUNK, :] * cw_ref[j:j + 1, :]
        dconv_next[...] = dconv
        dp_ref[:, 0:GDN_CONV] = dx.astype(BF16)
        dg = _sel_l(_tri(CHUNK, upper=True).astype(BF16), db_slab)
        da = dg * (-jnp.exp(al_ref[...])) * _sigmoid(zs)
        da = jnp.where(lane < GDN_H, da, 0.0)
        dal_ref[...] += jnp.sum(dg * g, axis=0, keepdims=True)
        ddt_ref[...] += jnp.sum(da, axis=0, keepdims=True)
        dp_ref[:, 4096:4096 + LANES] = (da + dbeta_slab * beta * (1.0 - beta)).astype(BF16)

    rv = lambda i: (nc - 1 - i, 0)
    blk = pl.BlockSpec((CHUNK, D), rv)
    fixed = lambda r, c: pl.BlockSpec((r, c), lambda i: (0, 0))
    return pl.pallas_call(
        body, name="gdn_bwd", grid=(nc,),
        in_specs=[pl.BlockSpec((CHUNK, GDN_INP), lambda i: (jnp.maximum(nc - 2 - i, 0), 0)),
                  pl.BlockSpec((CHUNK, GDN_INP), rv), fixed(8, GDN_CONV), fixed(1, LANES), fixed(1, LANES), fixed(1, GDN_DV),
                  blk, pl.BlockSpec((1, GDN_H, GDN_DK, GDN_DV), lambda i: (nc - 1 - i, 0, 0, 0)), blk],
        out_specs=[pl.BlockSpec((CHUNK, GDN_INP), rv), fixed(8, GDN_CONV), fixed(1, LANES), fixed(1, LANES), fixed(1, GDN_DV)],
        out_shape=[jax.ShapeDtypeStruct((lp, GDN_INP), BF16), jax.ShapeDtypeStruct((8, GDN_CONV), F32),
                   jax.ShapeDtypeStruct((1, LANES), F32), jax.ShapeDtypeStruct((1, LANES), F32),
                   jax.ShapeDtypeStruct((1, GDN_DV), F32)],
        scratch_shapes=[pltpu.VMEM((GDN_H, GDN_DK, GDN_DV), F32), pltpu.VMEM((CHUNK, GDN_CONV), F32)],
        compiler_params=_params(("arbitrary",)),
    )(proj, proj, jnp.pad(conv_w.reshape(4, GDN_CONV), ((0, 4), (0, 0))), jnp.pad(a_log, (0, LANES - GDN_H)).reshape(1, LANES),
      jnp.pad(dt_bias, (0, LANES - GDN_H)).reshape(1, LANES), o_gain.reshape(1, GDN_DV), o, states, dy)


def _coords():
    return lax.axis_index("x"), lax.axis_index("y"), lax.axis_index("c")


def _other_chips(x, y):
    return [(1 - x, y, 2 * (1 - x) + y), (x, 1 - y, 2 * x + 1 - y), (1 - x, 1 - y, 2 * (1 - x) + 1 - y)]


def _gather8(v, *, reduce, name):
    r, c = v.shape

    def body(v_ref, out_ref, *scratch):
        if reduce:
            buf, send_sems, recv_sems = scratch
        else:
            buf = out_ref
            send_sems, recv_sems = scratch
        x, y, cc = _coords()
        me = 4 * x + 2 * y + cc
        buf[me] = v_ref[...]
        copies = []
        for k in range(1, 8):
            px, py, pc = x ^ (k >> 2), y ^ ((k >> 1) & 1), cc ^ (k & 1)
            copies.append(pltpu.make_async_remote_copy(
                src_ref=v_ref, dst_ref=buf.at[me], send_sem=send_sems.at[k - 1], recv_sem=recv_sems.at[k - 1],
                device_id=(px, py, pc), device_id_type=MESH))
        for cp in copies:
            cp.start()
        for k in range(1, 8):
            peer = (x ^ (k >> 2)) * 4 + (y ^ ((k >> 1) & 1)) * 2 + (cc ^ (k & 1))
            pltpu.make_async_remote_copy(
                src_ref=v_ref, dst_ref=buf.at[peer], send_sem=send_sems.at[k - 1], recv_sem=recv_sems.at[k - 1],
                device_id=(x, y, cc), device_id_type=MESH).wait_recv()
        for cp in copies:
            cp.wait_send()
        if reduce:
            acc = buf[0]
            for d in range(1, 8):
                acc = acc + buf[d]
            out_ref[...] = acc

    scratch = [pltpu.SemaphoreType.DMA((7,)), pltpu.SemaphoreType.DMA((7,))]
    if reduce:
        scratch = [pltpu.VMEM((8, r, c), F32)] + scratch
    return pl.pallas_call(
        body, name=name, in_specs=[VM], out_specs=VM,
        out_shape=jax.ShapeDtypeStruct((r, c) if reduce else (8, r, c), F32),
        scratch_shapes=scratch, compiler_params=_params(),
    )(v)


class _AgCopies:
    def __init__(self, buf, ranges, send_sems, recv_sems):
        self.buf, self.ranges, self.send_sems, self.recv_sems = buf, ranges, send_sems, recv_sems
        self.x, self.y, self.cc = _coords()
        self.p = 2 * self.x + self.y
        self.chips = _other_chips(self.x, self.y)

    def rows(self, chip, r, hf):
        start, n = self.ranges[r]
        return self.buf.at[chip, pl.ds(start + hf * (n // 2), n // 2), :]

    def _copy(self, r, k, chip, hf, to):
        return pltpu.make_async_remote_copy(
            src_ref=self.rows(chip, r, hf), dst_ref=self.rows(chip, r, hf), send_sem=self.send_sems.at[3 * r + k],
            recv_sem=self.recv_sems.at[3 * r + k], device_id=to, device_id_type=MESH)

    def pairs(self):
        return [(r, k) for r in range(len(self.ranges)) for k in range(3)]

    def ici(self, r, k):
        cx, cy, _ = self.chips[k]
        return self._copy(r, k, self.p, self.cc, (cx, cy, self.cc))

    def ici_arrival(self, r, k):
        return self._copy(r, k, self.chips[k][2], self.cc, (self.x, self.y, self.cc))

    def forward(self, r, k):
        return self._copy(r, k, self.chips[k][2], self.cc, (self.x, self.y, 1 - self.cc))

    def forward_arrival(self, r, k):
        return self._copy(r, k, self.chips[k][2], 1 - self.cc, (self.x, self.y, self.cc))


def _ag_weights(w4, ranges):
    n = 3 * len(ranges)

    def body(w_ref, out_ref, send1, recv1, send2, recv2):
        ici, fwd = _AgCopies(out_ref, ranges, send1, recv1), _AgCopies(out_ref, ranges, send2, recv2)
        for r, k in ici.pairs():
            ici.ici(r, k).start()
        for r, k in ici.pairs():
            ici.ici_arrival(r, k).wait_recv()
            fwd.forward(r, k).start()
        for r, k in ici.pairs():
            fwd.forward_arrival(r, k).wait_recv()
        for r, k in ici.pairs():
            ici.ici(r, k).wait_send()
            fwd.forward(r, k).wait_send()

    return pl.pallas_call(
        body, name="ag_weights", in_specs=[ANY], out_specs=ANY, out_shape=jax.ShapeDtypeStruct(w4.shape, w4.dtype),
        scratch_shapes=[pltpu.SemaphoreType.DMA((n,))] * 4, input_output_aliases={0: 0}, compiler_params=_params(),
    )(w4)


def _ag_forward(w4, ranges):
    n = 3 * len(ranges)

    def body(w_ref, out_ref, send2, recv2):
        fwd = _AgCopies(out_ref, ranges, send2, recv2)
        for r, k in fwd.pairs():
            fwd.forward(r, k).start()
        for r, k in fwd.pairs():
            fwd.forward_arrival(r, k).wait_recv()
        for r, k in fwd.pairs():
            fwd.forward(r, k).wait_send()

    return pl.pallas_call(
        body, name="ag_forward", in_specs=[ANY], out_specs=ANY, out_shape=jax.ShapeDtypeStruct(w4.shape, w4.dtype),
        scratch_shapes=[pltpu.SemaphoreType.DMA((n,))] * 2, input_output_aliases={0: 0}, compiler_params=_params(),
    )(w4)


def _swap_halves(g, *, name):
    nb, r, c = g.shape
    half = r // 2

    def body(g_ref, out_ref, send_sem, recv_sem):
        x, y, cc = _coords()
        cp = pltpu.make_async_remote_copy(
            src_ref=g_ref.at[:, pl.ds((1 - cc) * half, half), :], dst_ref=out_ref, send_sem=send_sem, recv_sem=recv_sem,
            device_id=(x, y, 1 - cc), device_id_type=MESH)
        cp.start()
        cp.wait()

    return pl.pallas_call(
        body, name=name, in_specs=[ANY], out_specs=ANY, out_shape=jax.ShapeDtypeStruct((nb, half, c), g.dtype),
        scratch_shapes=[pltpu.SemaphoreType.DMA, pltpu.SemaphoreType.DMA], compiler_params=_params(),
    )(g)


def _my_half_index():
    return lax.axis_index("c").astype(jnp.int32).reshape(1)


def _add_halves(g, got, tag):
    nb, r, c = g.shape
    half = r // 2
    tr = _tile(half, 512, 16)
    nt_ = half // tr

    def body(c_ref, a_ref, b_ref, o_ref):
        o_ref[...] = (a_ref[...].astype(F32) + b_ref[...].astype(F32)).astype(BF16)

    return pl.pallas_call(
        body, name=f"rs_add_sibling{tag}",
        grid_spec=pltpu.PrefetchScalarGridSpec(
            num_scalar_prefetch=1, grid=(nb, nt_),
            in_specs=[pl.BlockSpec((1, tr, c), lambda b, i, cr: (b, cr[0] * nt_ + i, 0)),
                      pl.BlockSpec((1, tr, c), lambda b, i, cr: (b, i, 0))],
            out_specs=pl.BlockSpec((1, tr, c), lambda b, i, cr: (b, i, 0))),
        out_shape=jax.ShapeDtypeStruct((nb, half, c), BF16), compiler_params=_params(("parallel", "parallel")),
    )(_my_half_index(), g, got)


def _scatter_copies(s_ref, out_ref, send_sems, recv_sems):
    x, y, cc = _coords()
    sends = [pltpu.make_async_remote_copy(
        src_ref=s_ref.at[blk], dst_ref=out_ref.at[k], send_sem=send_sems.at[k], recv_sem=recv_sems.at[k],
        device_id=(cx, cy, cc), device_id_type=MESH) for k, (cx, cy, blk) in enumerate(_other_chips(x, y))]
    arrivals = [pltpu.make_async_remote_copy(
        src_ref=s_ref.at[2 * x + y], dst_ref=out_ref.at[k], send_sem=send_sems.at[k], recv_sem=recv_sems.at[k],
        device_id=(x, y, cc), device_id_type=MESH) for k in range(3)]
    return sends, arrivals


def _scatter_chips(s, tag):
    nb, hrows, c = s.shape

    def body(s_ref, out_ref, send_sems, recv_sems):
        sends, arrivals = _scatter_copies(s_ref, out_ref, send_sems, recv_sems)
        for cp in sends:
            cp.start()
        for cp in arrivals:
            cp.wait_recv()
        for cp in sends:
            cp.wait_send()

    return pl.pallas_call(
        body, name=f"rs_scatter{tag}", in_specs=[ANY], out_specs=ANY, out_shape=jax.ShapeDtypeStruct((3, hrows, c), s.dtype),
        scratch_shapes=[pltpu.SemaphoreType.DMA((3,)), pltpu.SemaphoreType.DMA((3,))], compiler_params=_params(),
    )(s)


def _sum_chips(s, got, tag):
    nb, hrows, c = s.shape
    tr = _tile(hrows, 512, 16)

    def body(idx_ref, own_ref, got_ref, o_ref):
        p = idx_ref[0]
        own = own_ref[0].astype(F32)
        parts = [got_ref[k].astype(F32) for k in range(3)]
        acc = jnp.zeros_like(own)
        for q in range(4):
            val = own
            for k, rel in enumerate((2, 1, 3)):
                val = jnp.where((p ^ rel) == q, parts[k], val)
            acc = acc + val
        o_ref[...] = acc

    idx = (2 * lax.axis_index("x") + lax.axis_index("y")).astype(jnp.int32).reshape(1)
    return pl.pallas_call(
        body, name=f"rs_sum_chips{tag}",
        grid_spec=pltpu.PrefetchScalarGridSpec(
            num_scalar_prefetch=1, grid=(hrows // tr,),
            in_specs=[pl.BlockSpec((1, tr, c), lambda i, pr: (pr[0], i, 0)), pl.BlockSpec((3, tr, c), lambda i, pr: (0, i, 0))],
            out_specs=pl.BlockSpec((tr, c), lambda i, pr: (i, 0))),
        out_shape=jax.ShapeDtypeStruct((hrows, c), F32), compiler_params=_params(("parallel",)),
    )(idx, s, got)


def _swap_sibling(t, tag):
    def body(t_ref, out_ref, send_sem, recv_sem):
        x, y, cc = _coords()
        cp = pltpu.make_async_remote_copy(src_ref=t_ref, dst_ref=out_ref, send_sem=send_sem, recv_sem=recv_sem,
                                          device_id=(x, y, 1 - cc), device_id_type=MESH)
        cp.start()
        cp.wait()

    return pl.pallas_call(
        body, name=f"rs_join{tag}", in_specs=[ANY], out_specs=ANY, out_shape=jax.ShapeDtypeStruct(t.shape, t.dtype),
        scratch_shapes=[pltpu.SemaphoreType.DMA, pltpu.SemaphoreType.DMA], compiler_params=_params(),
    )(t)


def _rs_local(g, tag):
    return _add_halves(g, _swap_halves(g, name=f"rs_swap{tag}"), tag)


def _rs_finish(s, recv, tag):
    t = _sum_chips(s, recv, tag)
    r = _swap_sibling(t, tag)
    first = lax.axis_index("c") == 0
    return jnp.concatenate([jnp.where(first, t, r), jnp.where(first, r, t)], axis=0)


_BIG = (("w_gate_up", 2), ("w_down", 1), ("fox_w_in", 2), ("fox_w_out", 1), ("gla_w_in", 2), ("gla_w_out", 1),
        ("gdn_w_in", 2), ("gdn_w_out", 1))
_SMALL_SHARDED = (("meta_tokens", 1), ("gla_w_alpha2", 2), ("gdn_conv_w", 3))
_REPLICATED = ("norm_mix", "norm_ffn", "fox_b_f", "fox_q_gain", "fox_k_gain", "gla_b_alpha", "gla_o_gain",
               "gdn_a_log", "gdn_dt_bias", "gdn_o_gain")
_WEIGHTS = ("meta_tokens", "norm_mix", "norm_ffn", "w_gate_up", "w_down", "fox_w_in", "fox_b_f", "fox_q_gain",
            "fox_k_gain", "fox_w_out", "gla_w_in", "gla_w_alpha2", "gla_b_alpha", "gla_o_gain", "gla_w_out",
            "gdn_w_in", "gdn_conv_w", "gdn_a_log", "gdn_dt_bias", "gdn_o_gain", "gdn_w_out")
_PACK_ROWS = 512
_IN_W = ("fox_w_in", "gla_w_in", "gdn_w_in")
_OUT_W = ("fox_w_out", "gla_w_out", "gdn_w_out")


def _piece_rows(n):
    return -(-n // 32) * 32


def _pack(arrays, width, row_mult, dtype):
    flat = jnp.concatenate([a.astype(dtype).reshape(-1) for a in arrays])
    per = width * row_mult
    n = -(-flat.shape[0] // per) * per
    return jnp.pad(flat, (0, n - flat.shape[0])).reshape(n // width, width)


def _unpack(flat, shapes):
    out, off = [], 0
    for s in shapes:
        n = 1
        for d in s:
            n *= d
        out.append(flat[off:off + n].reshape(s))
        off += n
    return out


def _unpack_cols(flat2, shapes):
    out, off = [], 0
    for s in shapes:
        n = 1
        for d in s:
            n *= d
        out.append(flat2[:, off:off + n].reshape((flat2.shape[0],) + tuple(s)))
        off += n
    return out


def _pad_cols(w, n):
    return jnp.pad(w, [(0, 0)] * (w.ndim - 1) + [(0, n - w.shape[-1])])


def kernel(x, meta_tokens, norm_mix, norm_ffn, w_gate_up, w_down, fox_w_in, fox_b_f, fox_q_gain, fox_k_gain, fox_w_out, gla_w_in, gla_w_alpha2, gla_b_alpha, gla_o_gain, gla_w_out, gdn_w_in, gdn_conv_w, gdn_a_log, gdn_dt_bias, gdn_o_gain, gdn_w_out, loss_target, m_meta_tokens, m_norm_mix, m_norm_ffn, m_w_gate_up, m_w_down, m_fox_w_in, m_fox_b_f, m_fox_q_gain, m_fox_k_gain, m_fox_w_out, m_gla_w_in, m_gla_w_alpha2, m_gla_b_alpha, m_gla_o_gain, m_gla_w_out, m_gdn_w_in, m_gdn_conv_w, m_gdn_a_log, m_gdn_dt_bias, m_gdn_o_gain, m_gdn_w_out, v_meta_tokens, v_norm_mix, v_norm_ffn, v_w_gate_up, v_w_down, v_fox_w_in, v_fox_b_f, v_fox_q_gain, v_fox_k_gain, v_fox_w_out, v_gla_w_in, v_gla_w_alpha2, v_gla_b_alpha, v_gla_o_gain, v_gla_w_out, v_gdn_w_in, v_gdn_conv_w, v_gdn_a_log, v_gdn_dt_bias, v_gdn_o_gain, v_gdn_w_out):
    W = dict(meta_tokens=meta_tokens, norm_mix=norm_mix, norm_ffn=norm_ffn, w_gate_up=w_gate_up, w_down=w_down,
             fox_w_in=fox_w_in, fox_b_f=fox_b_f, fox_q_gain=fox_q_gain, fox_k_gain=fox_k_gain, fox_w_out=fox_w_out,
             gla_w_in=gla_w_in, gla_w_alpha2=gla_w_alpha2, gla_b_alpha=gla_b_alpha, gla_o_gain=gla_o_gain,
             gla_w_out=gla_w_out, gdn_w_in=gdn_w_in, gdn_conv_w=gdn_conv_w, gdn_a_log=gdn_a_log,
             gdn_dt_bias=gdn_dt_bias, gdn_o_gain=gdn_o_gain, gdn_w_out=gdn_w_out)
    M = dict(meta_tokens=m_meta_tokens, norm_mix=m_norm_mix, norm_ffn=m_norm_ffn, w_gate_up=m_w_gate_up, w_down=m_w_down,
             fox_w_in=m_fox_w_in, fox_b_f=m_fox_b_f, fox_q_gain=m_fox_q_gain, fox_k_gain=m_fox_k_gain,
             fox_w_out=m_fox_w_out, gla_w_in=m_gla_w_in, gla_w_alpha2=m_gla_w_alpha2, gla_b_alpha=m_gla_b_alpha,
             gla_o_gain=m_gla_o_gain, gla_w_out=m_gla_w_out, gdn_w_in=m_gdn_w_in, gdn_conv_w=m_gdn_conv_w,
             gdn_a_log=m_gdn_a_log, gdn_dt_bias=m_gdn_dt_bias, gdn_o_gain=m_gdn_o_gain, gdn_w_out=m_gdn_w_out)
    V = dict(meta_tokens=v_meta_tokens, norm_mix=v_norm_mix, norm_ffn=v_norm_ffn, w_gate_up=v_w_gate_up, w_down=v_w_down,
             fox_w_in=v_fox_w_in, fox_b_f=v_fox_b_f, fox_q_gain=v_fox_q_gain, fox_k_gain=v_fox_k_gain,
             fox_w_out=v_fox_w_out, gla_w_in=v_gla_w_in, gla_w_alpha2=v_gla_w_alpha2, gla_b_alpha=v_gla_b_alpha,
             gla_o_gain=v_gla_o_gain, gla_w_out=v_gla_w_out, gdn_w_in=v_gdn_w_in, gdn_conv_w=v_gdn_conv_w,
             gdn_a_log=v_gdn_a_log, gdn_dt_bias=v_gdn_dt_bias, gdn_o_gain=v_gdn_o_gain, gdn_w_out=v_gdn_w_out)
    chip = 2 * lax.axis_index("x") + lax.axis_index("y")

    pieces, offs, r = [], {}, FFN_ROWS
    for n in _IN_W:
        nc = W[n].shape[2]
        for l in range(W[n].shape[0]):
            pieces.append(jnp.pad(W[n][l].T.astype(BF16), ((0, _piece_rows(nc) - nc), (0, 0))))
            offs[n, l] = r
            r += _piece_rows(nc)
    for n in _OUT_W:
        for l in range(W[n].shape[0]):
            pieces.append(W[n][l].astype(BF16))
            offs[n, l] = r
            r += W[n].shape[1]
    rows = -(-r // _PACK_ROWS) * _PACK_ROWS
    packed = jnp.concatenate([jnp.swapaxes(w_gate_up, 1, 2).reshape(-1, D).astype(BF16), w_down.reshape(-1, D).astype(BF16)]
                             + pieces + [jnp.zeros((rows - r, D), BF16)], axis=0)
    first_rows = [(offs["fox_w_in", 0], offs["fox_w_in", 1] - offs["fox_w_in", 0]),
                  (offs["fox_w_out", 0], offs["fox_w_out", 1] - offs["fox_w_out", 0])]
    later_rows = [(0, FFN_ROWS), (offs["fox_w_in", 1], offs["fox_w_out", 0] - offs["fox_w_in", 1]),
                  (offs["fox_w_out", 1], r - offs["fox_w_out", 1])]
    wpk = _ag_weights(lax.dynamic_update_slice(lax.empty((4, rows, D), BF16), packed[None], (chip, 0, 0)), first_rows)

    def in_t(buf, n, l, npad):
        nc = W[n].shape[2]
        return jnp.concatenate([buf[q, offs[n, l]:offs[n, l] + nc] for q in range(4)] + [jnp.zeros((npad - 4 * nc, D), BF16)], 0)

    def out_w(buf, n, l):
        return jnp.concatenate([buf[q, offs[n, l]:offs[n, l] + W[n].shape[1]] for q in range(4)], axis=0)

    fox_in0, fox_out0 = in_t(wpk, "fox_w_in", 0, FOX_INP), out_w(wpk, "fox_w_out", 0)
    full = {}
    small = _pack([W[n] for n, _ in _SMALL_SHARDED], LANES, 8, F32)
    small_all = _gather8(small, reduce=False, name="gather_small").reshape(8, -1)
    for (n, ax), seg in zip(_SMALL_SHARDED, _unpack_cols(small_all, [W[n].shape for n, _ in _SMALL_SHARDED])):
        full[n] = jnp.concatenate([seg[2 * q] for q in range(4)], axis=ax)
    fox_in, full["fox_w_out"] = [fox_in0], [fox_out0]
    w_alpha2, conv_w = full["gla_w_alpha2"][0], full["gdn_conv_w"][0]

    h = jnp.concatenate([jnp.zeros((META0, D), F32), full["meta_tokens"], x[0]], axis=0)
    saved = []
    for i in range(DEPTH):
        kind, j = i % 3, i // 3
        y = _rms_fwd(h, norm_mix[i], name=f"norm_mix{i}")
        if kind == 0:
            proj = _mm(y, fox_in[j], tb=True, name=f"fox_in{j}")
            qa, ka, va = _fox_prep(proj, fox_b_f[j], fox_q_gain[j], fox_k_gain[j])
            if i == 0:
                o, og, lse, wpk = _fox_attn_fwd(qa, ka, va, proj, ag=(wpk, later_rows))
                wpk = _ag_forward(wpk, later_rows)
                fox_in += [in_t(wpk, "fox_w_in", l, FOX_INP) for l in range(1, fox_w_in.shape[0])]
                full["fox_w_out"] += [out_w(wpk, "fox_w_out", l) for l in range(1, fox_w_out.shape[0])]
                gla_in = [in_t(wpk, "gla_w_in", l, GLA_INP) for l in range(gla_w_in.shape[0])]
                gdn_in = [in_t(wpk, "gdn_w_in", l, GDN_INP) for l in range(gdn_w_in.shape[0])]
                for n in ("gla_w_out", "gdn_w_out"):
                    full[n] = [out_w(wpk, n, l) for l in range(W[n].shape[0])]
            else:
                o, og, lse = _fox_attn_fwd(qa, ka, va, proj)
            w_out, mix = full["fox_w_out"][j], (proj, qa, ka, va, o, lse)
        elif kind == 1:
            proj = _mm(y, gla_in[j], tb=True, name=f"gla_in{j}")
            o, og, states = _gla_fwd(proj, w_alpha2, gla_b_alpha[j], gla_o_gain[j])
            w_out, mix = full["gla_w_out"][j], (proj, o, states)
        else:
            proj = _mm(y, gdn_in[j], tb=True, name=f"gdn_in{j}")
            o, og, states = _gdn_fwd(proj, conv_w, gdn_a_log[j], gdn_dt_bias[j], gdn_o_gain[j])
            w_out, mix = full["gdn_w_out"][j], (proj, o, states)
        hm = _mm(og, w_out, add=h, name=f"mix_out{i}")
        yf = _rms_fwd(hm, norm_ffn[i], name=f"norm_ffn{i}")
        gate, up, act = _ffn_up(yf, wpk, i)
        hn = _ffn_down(act, wpk, i, hm)
        saved.append((h, y, mix, og, w_out, hm, yf, gate, up, act))
        h = hn
    dh, loss_tile = _loss_head(h, loss_target[0])

    G = {n: [None] * W[n].shape[0] for n in _WEIGHTS if n not in ("meta_tokens", "w_gate_up", "w_down") + _IN_W}
    GT = {}

    def grad_layout(ffn_layers, pieces):
        off, end = {}, 0
        for l in ffn_layers:
            off["gu", l] = end
            end += GU_ROWS
        for l in ffn_layers:
            off["down", l] = end
            end += DOWN_ROWS
        for n, l in pieces:
            off[n, l] = end
            end += _piece_rows(W[n].shape[2]) if n in _IN_W else W[n].shape[1]
        return off, end, -(-end // _PACK_ROWS) * _PACK_ROWS

    first_pieces = [("fox_w_in", 0)]
    later_pieces = [(n, l) for n in _IN_W + _OUT_W for l in range(W[n].shape[0]) if (n, l) not in first_pieces]
    layouts = [grad_layout([], first_pieces), grad_layout(list(range(DEPTH)), later_pieces)]
    gbuf = [jnp.zeros((4, lay[2], D), BF16) for lay in layouts]

    def with_pieces(buf, lay, pieces):
        off, end, total = lay
        blocks = []
        for q in range(4):
            parts = []
            for n, l in pieces:
                if n in _IN_W:
                    nc = W[n].shape[2]
                    parts.append(jnp.pad(GT[n, l][q * nc:(q + 1) * nc], ((0, _piece_rows(nc) - nc), (0, 0))))
                else:
                    nr = W[n].shape[1]
                    parts.append(G[n][l][q * nr:(q + 1) * nr])
            blocks.append(jnp.concatenate(parts + [jnp.zeros((total - end, D), BF16)], axis=0))
        return lax.dynamic_update_slice(buf, jnp.stack(blocks), (0, off[pieces[0]], 0))

    s_later = None
    for i in reversed(range(DEPTH)):
        kind, j = i % 3, i // 3
        h_in, y, mix, og, w_out, hm, yf, gate, up, act = saved[i]
        b = 1
        dg, du = _ffn_dact(dh, wpk, i, gate, up)
        gbuf[b] = _ffn_dw_down(act, dh, gbuf[b], i, layouts[b][0]["down", i])
        dyf = _ffn_dyf(dg, du, wpk, i)
        gbuf[b] = _ffn_dw_gu(dg, du, yf, gbuf[b], i, layouts[b][0]["gu", i] // GU_ROWS)
        dhm, dnf = _rms_bwd(hm, norm_ffn[i], dyf, dh, name=f"d_norm_ffn{i}")
        G["norm_ffn"][i] = dnf[0]
        dog = _mm(dhm, w_out, tb=True, name=f"d_og{i}")
        dw_out = _mm(og, dhm, ta=True, out_dtype=BF16, name=f"d_w_out{i}")
        if kind == 0:
            proj, qa, ka, va, o, lse = mix
            doa, q2, dgate = _fox_gate_bwd(dog, o, proj, lse, qa)
            G["fox_w_out"][j] = dw_out
            if i == 0:
                s_later = _rs_local(with_pieces(gbuf[1], layouts[1], later_pieces), "_later")
                dqn, dkn, dv, dct, recv_later = _fox_attn_bwd(q2, ka, va, doa, rs=s_later)
            else:
                dqn, dkn, dv, dct = _fox_attn_bwd(q2, ka, va, doa)
            dproj, dqg, dkg, dbf = _fox_prep_bwd(proj, fox_b_f[j], fox_q_gain[j], fox_k_gain[j], dqn, dkn, dv, dgate, dct)
            G["fox_q_gain"][j] = dqg.reshape(FOX_H, FOX_DH).sum(0)
            G["fox_k_gain"][j] = dkg.reshape(FOX_H, FOX_DH).sum(0)
            G["fox_b_f"][j] = dbf[0, :FOX_H]
            w_in, wname = fox_in[j], "fox_w_in"
        elif kind == 1:
            proj, o, states = mix
            dproj, dwa, dba, dogain = _gla_bwd(proj, w_alpha2, gla_b_alpha[j], gla_o_gain[j], o, states, dog)
            G["gla_w_out"][j] = dw_out
            G["gla_w_alpha2"][j] = dwa[:GLA_RANK]
            G["gla_b_alpha"][j] = dba[0]
            G["gla_o_gain"][j] = dogain[0]
            w_in, wname = gla_in[j], "gla_w_in"
        else:
            proj, o, states = mix
            dproj, dcw, dal, ddt, dogain = _gdn_bwd(proj, conv_w, gdn_a_log[j], gdn_dt_bias[j], gdn_o_gain[j], o, states, dog)
            G["gdn_w_out"][j] = dw_out
            G["gdn_conv_w"][j] = dcw[:4].reshape(4, 1, GDN_CONV)
            G["gdn_a_log"][j] = dal[0, :GDN_H]
            G["gdn_dt_bias"][j] = ddt[0, :GDN_H]
            G["gdn_o_gain"][j] = dogain[0]
            w_in, wname = gdn_in[j], "gdn_w_in"
        dy = _mm(dproj, w_in, name=f"d_y{i}")
        GT[wname, j] = _mm(dproj, y, ta=True, out_dtype=BF16, name=f"d_w_in{i}")
        dh, dnm = _rms_bwd(h_in, norm_mix[i], dy, dhm, name=f"d_norm_mix{i}")
        G["norm_mix"][i] = dnm[0]
    grad_x = dh[ROW0:][None]
    G = {n: (v if n in _OUT_W else jnp.stack(v)) for n, v in G.items()}
    G["meta_tokens"] = dh[META0:ROW0]

    s_first = _rs_local(with_pieces(gbuf[0], layouts[0], first_pieces), "_first")
    reduced = [_rs_finish(s_first, _scatter_chips(s_first, "_first"), "_first"), _rs_finish(s_later, recv_later, "_later")]

    def reduced_piece(n, l):
        b = 0 if (n, l) in first_pieces else 1
        start = layouts[b][0][n, l]
        return reduced[b][start:start + (W[n].shape[2] if n in _IN_W else W[n].shape[1])]

    grads = {}
    for n in _IN_W:
        grads[n] = jnp.stack([reduced_piece(n, l).T for l in range(W[n].shape[0])])
    for n in _OUT_W:
        grads[n] = jnp.stack([reduced_piece(n, l) for l in range(W[n].shape[0])])
    small_names = [n for n, _ in _SMALL_SHARDED] + list(_REPLICATED)
    small_g = _pack([G[n] for n in small_names] + [loss_tile[0, 0:1]], LANES, 8, F32)
    small_sum = _gather8(small_g, reduce=True, name="allreduce_small").reshape(-1)
    small_shapes = [G[n].shape for n in small_names] + [(1,)]
    small_vals = _unpack(small_sum, small_shapes)
    loss = small_vals[-1][0]
    for n, val in zip(small_names, small_vals[:-1]):
        grads[n] = val
    for n, ax in _SMALL_SHARDED:
        sz = W[n].shape[ax]
        grads[n] = lax.dynamic_slice_in_dim(grads[n], chip * sz, sz, axis=ax)

    delta, new_m, new_v = {}, {}, {}
    for n, key, tr_ in (("w_gate_up", "gu", True), ("w_down", "down", False)):
        grads[n], delta[n], new_m[n], new_v[n] = _adamw_packed(
            W[n], reduced[1], reduced[1], M[n], V[n], row0=layouts[1][0][key, 0], row_off=layouts[1][0][key, 1],
            transposed=tr_, name=f"adamw_{n}")
    for n in _IN_W + _OUT_W:
        delta[n], new_m[n], new_v[n] = _adamw(W[n], grads[n], M[n], V[n], name=f"adamw_{n}")
    tiny = [n for n in _WEIGHTS if n not in dict(_BIG)]
    packs = [_pack([T[n] for n in tiny], LANES, 8, F32) for T in (W, grads, M, V)]
    outs = _adamw(*packs, name="adamw_small")
    shapes = [W[n].shape for n in tiny]
    for dst, o in zip((delta, new_m, new_v), outs):
        for n, val in zip(tiny, _unpack(o.reshape(-1), shapes)):
            dst[n] = val
    return (loss, grad_x, *[grads[n] for n in _WEIGHTS], *[delta[n] for n in _WEIGHTS],
            *[new_m[n] for n in _WEIGHTS], *[new_v[n] for n in _WEIGHTS])
```

```python
import functools

import jax
import jax.numpy as jnp
from jax import lax
from jax.experimental import pallas as pl
from jax.experimental.pallas import tpu as pltpu

F32, BF16 = jnp.float32, jnp.bfloat16
D = 1024
N_META = 16
ROW0 = 128
META0 = ROW0 - N_META
EPS = 1e-6
LANES = 128
VMEM_LIMIT = 56 * 1024 * 1024

FOX_H, FOX_DH = 16, 64
FOX_INP = 4224
GLA_H, GLA_DK, GLA_DV, GLA_RANK = 4, 128, 256, 16
GLA_QK, GLA_V = 512, 1024
GLA_INP = 3200
GLA_NORM = 16.0
GDN_H, GDN_DK, GDN_DV = 8, 128, 128
GDN_CONV = 3072
GDN_INP = 4224
CHUNK = 64
D_FF = 2816
DEPTH = 4

ADAM_LR, ADAM_B1, ADAM_B2, ADAM_EPS, ADAM_WD, ADAM_STEP = 0.001, 0.9, 0.999, 1e-08, 0.01, 10

MESH = pl.DeviceIdType.MESH
ANY = pl.BlockSpec(memory_space=pl.ANY)
VM = pl.BlockSpec(memory_space=pltpu.VMEM)


def _params(sem=None, **kw):
    if sem is not None:
        kw["dimension_semantics"] = sem
    return pltpu.CompilerParams(vmem_limit_bytes=VMEM_LIMIT, **kw)


def _tile(n, cap, mult=LANES):
    best = None
    for t in range(mult, min(n, cap) + 1, mult):
        if n % t == 0:
            best = t
    return best if best is not None else n


def nn(a, b, **kw):
    return jnp.dot(a, b, preferred_element_type=F32, **kw)


def nt(a, b, **kw):
    return lax.dot_general(a, b, (((1,), (1,)), ((), ())), preferred_element_type=F32, **kw)


def tn(a, b, **kw):
    return lax.dot_general(a, b, (((0,), (0,)), ((), ())), preferred_element_type=F32, **kw)


def _split3(x):
    hi = x.astype(BF16)
    r = x - hi.astype(F32)
    mid = r.astype(BF16)
    lo = (r - mid.astype(F32)).astype(BF16)
    return hi, mid, lo


def _sel_l(sel, x):
    a, b, c = _split3(x)
    return nn(sel, a) + nn(sel, b) + nn(sel, c)


def _sel_r(x, sel):
    a, b, c = _split3(x)
    return nn(a, sel) + nn(b, sel) + nn(c, sel)


def _sel_r2(x, sel):
    a = x.astype(BF16)
    return nn(a, sel) + nn((x - a.astype(F32)).astype(BF16), sel)


def _iota(shape, dim):
    return lax.broadcasted_iota(jnp.int32, shape, dim)


def _tri(n, upper=False, strict=False):
    i, j = _iota((n, n), 0), _iota((n, n), 1)
    if upper:
        m = (j > i) if strict else (j >= i)
    else:
        m = (j < i) if strict else (j <= i)
    return m


def _sigmoid(x):
    return 1.0 / (1.0 + jnp.exp(-x))


def _log_sigmoid(x):
    return jnp.minimum(x, 0.0) - jnp.log(1.0 + jnp.exp(-jnp.abs(x)))


def _softplus(x):
    return jnp.maximum(x, 0.0) + jnp.log(1.0 + jnp.exp(-jnp.abs(x)))


def _silu(x):
    return x * _sigmoid(x)


def _dsilu(x):
    s = _sigmoid(x)
    return s * (1.0 + x * (1.0 - s))


def _mm(a, b, *, ta=False, tb=False, add=None, out_dtype=F32, name):
    m, k = (a.shape[1], a.shape[0]) if ta else a.shape
    n = b.shape[0] if tb else b.shape[1]
    assert k == (b.shape[1] if tb else b.shape[0])
    tm, tn_, tk = _tile(m, 1408, LANES if ta else 16), _tile(n, 1408), _tile(k, 1408)
    nk = k // tk

    def body(*refs):
        if add is None:
            a_ref, b_ref, o_ref, acc = refs
        else:
            a_ref, b_ref, r_ref, o_ref, acc = refs
        kk = pl.program_id(2)

        @pl.when(kk == 0)
        def _():
            acc[...] = jnp.zeros_like(acc)

        av, bv = a_ref[...].astype(BF16), b_ref[...].astype(BF16)
        dims = (((0,) if ta else (1,), (1,) if tb else (0,)), ((), ()))
        acc[...] += lax.dot_general(av, bv, dims, preferred_element_type=F32)

        @pl.when(kk == nk - 1)
        def _():
            r = acc[...]
            if add is not None:
                r = r + r_ref[...].astype(F32)
            o_ref[...] = r.astype(out_dtype)

    a_spec = pl.BlockSpec((tk, tm), lambda i, j, q: (q, i)) if ta else pl.BlockSpec((tm, tk), lambda i, j, q: (i, q))
    b_spec = pl.BlockSpec((tn_, tk), lambda i, j, q: (j, q)) if tb else pl.BlockSpec((tk, tn_), lambda i, j, q: (q, j))
    o_spec = pl.BlockSpec((tm, tn_), lambda i, j, q: (i, j))
    ins, specs = [a, b], [a_spec, b_spec]
    if add is not None:
        ins.append(add)
        specs.append(o_spec)
    return pl.pallas_call(
        body, name=name, grid=(m // tm, n // tn_, nk), in_specs=specs, out_specs=o_spec,
        out_shape=jax.ShapeDtypeStruct((m, n), out_dtype),
        scratch_shapes=[pltpu.VMEM((tm, tn_), F32)],
        compiler_params=_params(("parallel", "parallel", "arbitrary")),
    )(*ins)


def _rms_fwd(h, g, *, name):
    lp = h.shape[0]
    tr = _tile(lp, 512)

    def body(h_ref, g_ref, y_ref):
        x = h_ref[...]
        r = lax.rsqrt(jnp.mean(x * x, axis=-1, keepdims=True) + EPS)
        y_ref[...] = (x * r * g_ref[...]).astype(BF16)

    return pl.pallas_call(
        body, name=name, grid=(lp // tr,),
        in_specs=[pl.BlockSpec((tr, D), lambda i: (i, 0)), pl.BlockSpec((1, D), lambda i: (0, 0))],
        out_specs=pl.BlockSpec((tr, D), lambda i: (i, 0)),
        out_shape=jax.ShapeDtypeStruct((lp, D), BF16), compiler_params=_params(("parallel",)),
    )(h, g.reshape(1, D))


def _rms_bwd(h, g, dy, dres, *, name):
    lp = h.shape[0]
    tr = _tile(lp, 512)

    def body(h_ref, g_ref, dy_ref, dr_ref, dh_ref, dg_ref):
        @pl.when(pl.program_id(0) == 0)
        def _():
            dg_ref[...] = jnp.zeros_like(dg_ref)

        x, dyv = h_ref[...], dy_ref[...].astype(F32)
        r = lax.rsqrt(jnp.mean(x * x, axis=-1, keepdims=True) + EPS)
        u = dyv * g_ref[...]
        dx = r * u - x * (r * r * r) * jnp.mean(x * u, axis=-1, keepdims=True)
        dh_ref[...] = dr_ref[...] + dx
        dg_ref[...] += jnp.sum(dyv * x * r, axis=0, keepdims=True)

    return pl.pallas_call(
        body, name=name, grid=(lp // tr,),
        in_specs=[pl.BlockSpec((tr, D), lambda i: (i, 0)), pl.BlockSpec((1, D), lambda i: (0, 0)),
                  pl.BlockSpec((tr, D), lambda i: (i, 0)), pl.BlockSpec((tr, D), lambda i: (i, 0))],
        out_specs=[pl.BlockSpec((tr, D), lambda i: (i, 0)), pl.BlockSpec((1, D), lambda i: (0, 0))],
        out_shape=[jax.ShapeDtypeStruct((lp, D), F32), jax.ShapeDtypeStruct((1, D), F32)],
        compiler_params=_params(("arbitrary",)),
    )(h, g.reshape(1, D), dy, dres)


GU_ROWS, DOWN_ROWS = 1408, 704
OFF_GU, OFF_DOWN = 0, DEPTH * GU_ROWS
FFN_ROWS = DEPTH * (GU_ROWS + DOWN_ROWS)
FFN_TM = 704


def _gu_spec(fn):
    return pl.BlockSpec((None, GU_ROWS, D), fn)


def _down_spec(fn):
    return pl.BlockSpec((None, DOWN_ROWS, D), fn)


def _down_pair(w0_ref, w1_ref):
    return jnp.concatenate([w0_ref[...], w1_ref[...]], axis=0)


def _ffn_up(yf, wpk, layer):
    lp = yf.shape[0]
    tm = _tile(lp, FFN_TM, 16)

    def body(y_ref, wg_ref, wu_ref, g_ref, u_ref, a_ref):
        y = y_ref[...]
        g, u = nt(y, wg_ref[...]), nt(y, wu_ref[...])
        g_ref[...] = g.astype(BF16)
        u_ref[...] = u.astype(BF16)
        a_ref[...] = (_silu(g) * u).astype(BF16)

    o = pl.BlockSpec((tm, GU_ROWS), lambda i, j: (i, j))
    return pl.pallas_call(
        body, name=f"ffn_up{layer}", grid=(lp // tm, 2),
        in_specs=[pl.BlockSpec((tm, D), lambda i, j: (i, 0)), _gu_spec(lambda i, j: (j, OFF_GU // GU_ROWS + layer, 0)),
                  _gu_spec(lambda i, j: (2 + j, OFF_GU // GU_ROWS + layer, 0))],
        out_specs=[o, o, o], out_shape=[jax.ShapeDtypeStruct((lp, D_FF), BF16)] * 3,
        compiler_params=_params(("parallel", "parallel")),
    )(yf, wpk, wpk)


def _ffn_down(act, wpk, layer, res):
    lp = act.shape[0]
    tm = _tile(lp, FFN_TM, 16)

    def body(a_ref, w0_ref, w1_ref, r_ref, o_ref, acc):
        kk = pl.program_id(1)

        @pl.when(kk == 0)
        def _():
            acc[...] = r_ref[...]

        acc[...] += nn(a_ref[...], _down_pair(w0_ref, w1_ref))

        @pl.when(kk == 1)
        def _():
            o_ref[...] = acc[...]

    o = pl.BlockSpec((tm, D), lambda i, kk: (i, 0))
    blk = OFF_DOWN // DOWN_ROWS + layer
    return pl.pallas_call(
        body, name=f"ffn_down{layer}", grid=(lp // tm, 2),
        in_specs=[pl.BlockSpec((tm, GU_ROWS), lambda i, kk: (i, kk)), _down_spec(lambda i, kk: (2 * kk, blk, 0)),
                  _down_spec(lambda i, kk: (2 * kk + 1, blk, 0)), o],
        out_specs=o, out_shape=jax.ShapeDtypeStruct((lp, D), F32), scratch_shapes=[pltpu.VMEM((tm, D), F32)],
        compiler_params=_params(("parallel", "arbitrary")),
    )(act, wpk, wpk, res)


def _ffn_dact(dh, wpk, layer, gate, up):
    lp = dh.shape[0]
    tm = _tile(lp, FFN_TM, 16)

    def body(d_ref, w0_ref, w1_ref, g_ref, u_ref, dg_ref, du_ref):
        da = nt(d_ref[...].astype(BF16), _down_pair(w0_ref, w1_ref))
        g, u = g_ref[...].astype(F32), u_ref[...].astype(F32)
        dg_ref[...] = (da * u * _dsilu(g)).astype(BF16)
        du_ref[...] = (da * _silu(g)).astype(BF16)

    o = pl.BlockSpec((tm, GU_ROWS), lambda i, j: (i, j))
    blk = OFF_DOWN // DOWN_ROWS + layer
    return pl.pallas_call(
        body, name=f"d_act{layer}", grid=(lp // tm, 2),
        in_specs=[pl.BlockSpec((tm, D), lambda i, j: (i, 0)), _down_spec(lambda i, j: (2 * j, blk, 0)),
                  _down_spec(lambda i, j: (2 * j + 1, blk, 0)), o, o],
        out_specs=[o, o], out_shape=[jax.ShapeDtypeStruct((lp, D_FF), BF16)] * 2,
        compiler_params=_params(("parallel", "parallel")),
    )(dh, wpk, wpk, gate, up)


def _ffn_dyf(dg, du, wpk, layer):
    lp = dg.shape[0]
    tm = _tile(lp, FFN_TM, 16)

    def body(dg_ref, du_ref, w_ref, o_ref, acc):
        kk = pl.program_id(1)

        @pl.when(kk == 0)
        def _():
            acc[...] = jnp.zeros_like(acc)

        @pl.when(kk < 2)
        def _():
            acc[...] += nn(dg_ref[...], w_ref[...])

        @pl.when(kk >= 2)
        def _():
            acc[...] += nn(du_ref[...], w_ref[...])

        @pl.when(kk == 3)
        def _():
            o_ref[...] = acc[...]

    return pl.pallas_call(
        body, name=f"d_yf{layer}", grid=(lp // tm, 4),
        in_specs=[pl.BlockSpec((tm, GU_ROWS), lambda i, kk: (i, jnp.minimum(kk, 1))),
                  pl.BlockSpec((tm, GU_ROWS), lambda i, kk: (i, jnp.maximum(kk - 2, 0))),
                  _gu_spec(lambda i, kk: (kk, OFF_GU // GU_ROWS + layer, 0))],
        out_specs=pl.BlockSpec((tm, D), lambda i, kk: (i, 0)), out_shape=jax.ShapeDtypeStruct((lp, D), F32),
        scratch_shapes=[pltpu.VMEM((tm, D), F32)], compiler_params=_params(("parallel", "arbitrary")),
    )(dg, du, wpk)


def _ffn_dw_down(act, dh, gpk, layer, row):
    lp = act.shape[0]
    tk = _tile(lp, 1408, 16)
    nk = lp // tk

    def body(a_ref, d_ref, g_in, g_out, acc, stage, sems):
        jp, kk = pl.program_id(0), pl.program_id(1)

        @pl.when(kk == 0)
        def _():
            acc[...] = jnp.zeros_like(acc)

        acc[...] += tn(a_ref[...], d_ref[...].astype(BF16))

        @pl.when(kk == nk - 1)
        def _():
            stage[...] = acc[...].astype(BF16)
            copies = [pltpu.make_async_copy(stage.at[pl.ds(hf * DOWN_ROWS, DOWN_ROWS), :],
                                            g_out.at[2 * jp + hf, pl.ds(row, DOWN_ROWS), :], sems.at[hf]) for hf in range(2)]
            for cp in copies:
                cp.start()
            for cp in copies:
                cp.wait()

    return pl.pallas_call(
        body, name=f"d_w_down{layer}", grid=(2, nk),
        in_specs=[pl.BlockSpec((tk, GU_ROWS), lambda jp, kk: (kk, jp)), pl.BlockSpec((tk, D), lambda jp, kk: (kk, 0)), ANY],
        out_specs=ANY, out_shape=jax.ShapeDtypeStruct(gpk.shape, gpk.dtype),
        scratch_shapes=[pltpu.VMEM((GU_ROWS, D), F32), pltpu.VMEM((GU_ROWS, D), BF16), pltpu.SemaphoreType.DMA((2,))],
        input_output_aliases={2: 0}, compiler_params=_params(("arbitrary", "arbitrary")),
    )(act, dh, gpk)


def _ffn_dw_gu(dg, du, yf, gpk, layer, blk):
    lp = dg.shape[0]
    tk = _tile(lp, 1408, 16)
    nk = lp // tk

    def body(dg_ref, du_ref, y_ref, g_in, o_ref, acc):
        c, kk = pl.program_id(0), pl.program_id(1)

        @pl.when(kk == 0)
        def _():
            acc[...] = jnp.zeros_like(acc)

        @pl.when(c < 2)
        def _():
            acc[...] += tn(dg_ref[...], y_ref[...])

        @pl.when(c >= 2)
        def _():
            acc[...] += tn(du_ref[...], y_ref[...])

        @pl.when(kk == nk - 1)
        def _():
            o_ref[...] = acc[...].astype(BF16)

    return pl.pallas_call(
        body, name=f"d_w_gate_up{layer}", grid=(4, nk),
        in_specs=[pl.BlockSpec((tk, GU_ROWS), lambda c, kk: (kk, jnp.minimum(c, 1))),
                  pl.BlockSpec((tk, GU_ROWS), lambda c, kk: (kk, jnp.maximum(c - 2, 0))),
                  pl.BlockSpec((tk, D), lambda c, kk: (kk, 0)), ANY],
        out_specs=_gu_spec(lambda c, kk: (c, blk, 0)),
        out_shape=jax.ShapeDtypeStruct(gpk.shape, gpk.dtype),
        scratch_shapes=[pltpu.VMEM((GU_ROWS, D), F32)], input_output_aliases={3: 0},
        compiler_params=_params(("parallel", "arbitrary")),
    )(dg, du, yf, gpk)


def _loss_head(h, target):
    lp = h.shape[0]
    nb = lp // ROW0

    def body(h_ref, t_ref, dh_ref, l_ref):
        i = pl.program_id(0)

        @pl.when(i == 0)
        def _():
            l_ref[...] = jnp.zeros_like(l_ref)
            dh_ref[...] = jnp.zeros_like(dh_ref)

        @pl.when(i > 0)
        def _():
            err = h_ref[...] - t_ref[...]
            dh_ref[...] = err * (1.0 / D)
            l_ref[...] += jnp.sum(err * err) * (0.5 / D)

    return pl.pallas_call(
        body, name="loss_head", grid=(nb,),
        in_specs=[pl.BlockSpec((ROW0, D), lambda i: (i, 0)), pl.BlockSpec((ROW0, D), lambda i: (jnp.maximum(i - 1, 0), 0))],
        out_specs=[pl.BlockSpec((ROW0, D), lambda i: (i, 0)), pl.BlockSpec((8, LANES), lambda i: (0, 0))],
        out_shape=[jax.ShapeDtypeStruct((lp, D), F32), jax.ShapeDtypeStruct((8, LANES), F32)],
        compiler_params=_params(("arbitrary",)),
    )(h, target)


def _adamw(w, g, m, v, *, name):
    if w.ndim == 2:
        w, g, m, v = (t[None] for t in (w, g, m, v))
        return tuple(o[0] for o in _adamw(w, g, m, v, name=name))
    nl, r, c = w.shape
    tr = _tile(r, max(8, (1 << 19) // c), 8)

    def body(w_ref, g_ref, m_ref, v_ref, d_ref, nm_ref, nv_ref):
        d_ref[...], nm_ref[...], nv_ref[...] = _adam_math(w_ref[...], g_ref[...], m_ref[...], v_ref[...])

    spec = pl.BlockSpec((1, tr, c), lambda l, i: (l, i, 0))
    return tuple(pl.pallas_call(
        body, name=name, grid=(nl, r // tr), in_specs=[spec] * 4, out_specs=[spec] * 3,
        out_shape=[jax.ShapeDtypeStruct(w.shape, F32)] * 3, compiler_params=_params(("parallel", "parallel")),
    )(w, g, m, v))


def _adam_math(w, g, m, v):
    nm = ADAM_B1 * m + (1.0 - ADAM_B1) * g
    nv = ADAM_B2 * v + (1.0 - ADAM_B2) * (g * g)
    m_hat = nm / (1.0 - ADAM_B1 ** ADAM_STEP)
    v_hat = nv / (1.0 - ADAM_B2 ** ADAM_STEP)
    return -ADAM_LR * (m_hat / (jnp.sqrt(v_hat) + ADAM_EPS) + ADAM_WD * w), nm, nv


def _adamw_packed(w, gred0, gred, m, v, *, row0, row_off, transposed, name):
    nl, a, b = w.shape
    nr = b if transposed else a
    later = lambda l: row_off // nr + jnp.maximum(l - 1, 0)
    if transposed:
        ta = _tile(a, 256)
        wspec = pl.BlockSpec((1, ta, b), lambda l, r: (l, r, 0))
        g0spec = pl.BlockSpec((b, ta), lambda l, r: (row0 // nr, r))
        gspec = pl.BlockSpec((b, ta), lambda l, r: (later(l), r))
        grid = (nl, a // ta)
    else:
        wspec = pl.BlockSpec((1, a, b), lambda l, r: (l, 0, 0))
        g0spec = pl.BlockSpec((a, b), lambda l, r: (row0 // nr, 0))
        gspec = pl.BlockSpec((a, b), lambda l, r: (later(l), 0))
        grid = (nl, 1)

    def body(w_ref, g0_ref, g_ref, m_ref, v_ref, go_ref, d_ref, nm_ref, nv_ref):
        g = jnp.where(pl.program_id(0) == 0, g0_ref[...], g_ref[...])
        g = g.T if transposed else g
        d, nm, nv = _adam_math(w_ref[0], g, m_ref[0], v_ref[0])
        go_ref[0], d_ref[0], nm_ref[0], nv_ref[0] = g, d, nm, nv

    return pl.pallas_call(
        body, name=name, grid=grid, in_specs=[wspec, g0spec, gspec, wspec, wspec], out_specs=[wspec] * 4,
        out_shape=[jax.ShapeDtypeStruct(w.shape, F32)] * 4, compiler_params=_params(("parallel", "parallel")),
    )(w, gred0, gred, m, v)


FOX_AUG = FOX_H * LANES
L_C = 64
L_K = 67
L_LSE = 70
PAD_KEY = -30000.0
FOX_TQ = 384


def _head_sel(n_heads, width, lanes=LANES):
    r, c = _iota((n_heads * width, lanes), 0), _iota((n_heads * width, lanes), 1)
    down = (r // width == c).astype(BF16)
    r2, c2 = _iota((lanes, n_heads * width), 0), _iota((lanes, n_heads * width), 1)
    up = (c2 // width == r2).astype(BF16)
    return down, up


def _place(lane0):
    r, c = _iota((LANES, FOX_AUG), 0), _iota((LANES, FOX_AUG), 1)
    return [((c // LANES == r) & (c % LANES == lane0 + m)).astype(BF16) for m in range(3)]


def _placed(x, lane0):
    pcs = _split3(x)
    mats = _place(lane0)
    return nn(pcs[0], mats[0]) + nn(pcs[1], mats[1]) + nn(pcs[2], mats[2])


def _ones_at(rows, lanes):
    c = _iota((rows, FOX_AUG), 1) % LANES
    m = c == lanes[0]
    for l in lanes[1:]:
        m = m | (c == l)
    return m.astype(F32)


def _spread(x, extras, out_ref):
    rows = x.shape[0]
    left = _iota((rows, LANES), 1) < FOX_DH
    for p in range(FOX_H // 2):
        slab = x[:, p * LANES:(p + 1) * LANES]
        a = jnp.where(left, slab, extras[:, 2 * p * LANES:(2 * p + 1) * LANES])
        b = jnp.where(left, pltpu.roll(slab, FOX_DH, 1), extras[:, (2 * p + 1) * LANES:(2 * p + 2) * LANES])
        out_ref[:, 2 * p * LANES:(2 * p + 1) * LANES] = a.astype(BF16)
        out_ref[:, (2 * p + 1) * LANES:(2 * p + 2) * LANES] = b.astype(BF16)


def _fox_prep(proj, b_f, q_gain, k_gain):
    lp = proj.shape[0]
    nb = lp // LANES

    def body(p_ref, bf_ref, qg_ref, kg_ref, q_ref, k_ref, v_ref, carry):
        i = pl.program_id(0)

        @pl.when(i == 0)
        def _():
            carry[...] = jnp.zeros_like(carry)

        down, up = _head_sel(FOX_H, FOX_DH)

        def normed(x, gain):
            ms = _sel_r2(x * x, down) * (1.0 / FOX_DH)
            r = _sel_r2(lax.rsqrt(ms + EPS), up)
            return x * r * gain

        lane = _iota((LANES, LANES), 1)
        lf = jnp.where(lane < FOX_H, _log_sigmoid(p_ref[:, 4 * D:4 * D + LANES] + bf_ref[...]), 0.0)
        c = _sel_l(_tri(LANES).astype(BF16), lf) + carry[0:1, :]
        carry[...] = jnp.broadcast_to(c[LANES - 1:LANES, :], carry.shape)
        q_extra = _placed(c, L_C) + _ones_at(LANES, (L_K, L_K + 1, L_K + 2))
        row = i * LANES + _iota((LANES, FOX_AUG), 0)
        lane_a = _iota((LANES, FOX_AUG), 1) % LANES
        k_extra = -_placed(c, L_K) + _ones_at(LANES, (L_C, L_C + 1, L_C + 2, L_LSE, L_LSE + 1, L_LSE + 2))
        pad_val = jnp.where(lane_a == L_K, PAD_KEY, 0.0)
        k_extra = jnp.where((row < META0) & (lane_a >= L_K) & (lane_a < L_K + 3), pad_val, k_extra)
        v_extra = _ones_at(LANES, (L_C, L_C + 1, L_C + 2))
        _spread(normed(p_ref[:, 0:D], qg_ref[...]) * (FOX_DH ** -0.5), q_extra, q_ref)
        _spread(normed(p_ref[:, D:2 * D], kg_ref[...]), k_extra, k_ref)
        _spread(p_ref[:, 2 * D:3 * D], v_extra, v_ref)

    row = pl.BlockSpec((1, D), lambda i: (0, 0))
    aug = pl.BlockSpec((LANES, FOX_AUG), lambda i: (i, 0))
    return pl.pallas_call(
        body, name="fox_prep", grid=(nb,),
        in_specs=[pl.BlockSpec((LANES, FOX_INP), lambda i: (i, 0)), pl.BlockSpec((1, LANES), lambda i: (0, 0)), row, row],
        out_specs=[aug] * 3, out_shape=[jax.ShapeDtypeStruct((lp, FOX_AUG), BF16)] * 3,
        scratch_shapes=[pltpu.VMEM((8, LANES), F32)],
        compiler_params=_params(("arbitrary",)),
    )(proj, jnp.pad(b_f, (0, LANES - FOX_H)).reshape(1, LANES), jnp.tile(q_gain, FOX_H).reshape(1, D),
      jnp.tile(k_gain, FOX_H).reshape(1, D))


def _fox_attn_fwd(qa, ka, va, proj, ag=None):
    lp = qa.shape[0]
    tq = _tile(lp, FOX_TQ)
    nq = lp // tq
    npair = FOX_H // 2

    def body(q_ref, k_ref, v_ref, gate_ref, *rest):
        if ag is None:
            o_ref, og_ref, lse_ref = rest
        else:
            _, o_ref, og_ref, lse_ref, w_out, send_sems, recv_sems = rest
            copies = _AgCopies(w_out, ag[1], send_sems, recv_sems)

            @pl.when((pl.program_id(0) == 0) & (pl.program_id(1) == 0))
            def _():
                for r, k in copies.pairs():
                    copies.ici(r, k).start()

        i = pl.program_id(1)
        causal = _iota((tq, tq), 1) <= _iota((tq, tq), 0)
        qs = [q_ref[:, hh * LANES:(hh + 1) * LANES] for hh in range(2)]

        def block(j, carry, diag):
            off = pl.multiple_of(j * tq, tq)
            out = []
            for hh in range(2):
                m, acc = carry[hh]
                k = k_ref[pl.ds(off, tq), hh * LANES:(hh + 1) * LANES]
                v = v_ref[pl.ds(off, tq), hh * LANES:(hh + 1) * LANES]
                s = nt(qs[hh], k)
                if diag:
                    s = jnp.where(causal, s, -1e30)
                m2 = jnp.maximum(m, jnp.max(s, axis=-1, keepdims=True))
                p = jnp.exp(s - m2)
                p_hi = p.astype(BF16)
                p_lo = (p - p_hi.astype(F32)).astype(BF16)
                out.append((m2, jnp.exp(m - m2) * acc + nn(p_hi, v) + nn(p_lo, v)))
            return tuple(out)

        init = tuple((jnp.full((tq, 1), -1e30, F32), jnp.zeros((tq, LANES), F32)) for _ in range(2))
        carry = lax.fori_loop(0, i // 2, lambda j, c: block(2 * j + 1, block(2 * j, c, False), False), init)
        carry = lax.cond(i % 2 == 1, lambda c: block(i - 1, c, False), lambda c: c, carry)
        carry = block(i, carry, True)
        outs, lses = [], []
        for hh in range(2):
            m, acc = carry[hh]
            l = acc[:, L_C:L_C + 1]
            outs.append(acc / l)
            lses.append(jnp.broadcast_to(m + jnp.log(l), (tq, LANES)))
        left = _iota((tq, LANES), 1) < FOX_DH
        o = jnp.where(left, outs[0], pltpu.roll(outs[1], FOX_DH, 1))
        o_ref[...] = o
        og_ref[...] = (o * _sigmoid(gate_ref[...])).astype(BF16)
        lse_ref[...] = jnp.where(left, lses[0], lses[1])

        if ag is not None:
            @pl.when((pl.program_id(0) == npair - 1) & (pl.program_id(1) == nq - 1))
            def _():
                for r, k in copies.pairs():
                    copies.ici_arrival(r, k).wait_recv()
                for r, k in copies.pairs():
                    copies.ici(r, k).wait_send()

    qspec = pl.BlockSpec((tq, 2 * LANES), lambda p, i: (i, p))
    kspec = pl.BlockSpec((lp, 2 * LANES), lambda p, i: (0, p))
    ospec = pl.BlockSpec((tq, LANES), lambda p, i: (i, p))
    ins, in_specs = [qa, ka, va, proj], [qspec, kspec, kspec, pl.BlockSpec((tq, LANES), lambda p, i: (i, 3 * D // LANES + p))]
    out_specs = [ospec] * 3
    out_shape = [jax.ShapeDtypeStruct((lp, D), F32), jax.ShapeDtypeStruct((lp, D), BF16), jax.ShapeDtypeStruct((lp, D), F32)]
    if ag is None:
        return pl.pallas_call(body, name="fox_attn_fwd", grid=(npair, nq), in_specs=in_specs, out_specs=out_specs,
                              out_shape=out_shape, compiler_params=_params(("parallel", "arbitrary")))(*ins)
    n = 3 * len(ag[1])
    return pl.pallas_call(
        body, name="fox_attn_fwd_ag", grid=(npair, nq), in_specs=in_specs + [ANY], out_specs=out_specs + [ANY],
        out_shape=out_shape + [jax.ShapeDtypeStruct(ag[0].shape, ag[0].dtype)],
        scratch_shapes=[pltpu.SemaphoreType.DMA((n,))] * 2, input_output_aliases={4: 3},
        compiler_params=_params(("arbitrary", "arbitrary")),
    )(*ins, ag[0])


def _fox_gate_bwd(dog, o, proj, lse, qa):
    lp = o.shape[0]
    tr = LANES

    def body(d_ref, o_ref, g_ref, lse_ref, q_ref, do_ref, q2_ref, dgate_ref):
        down, _ = _head_sel(FOX_H, FOX_DH)
        sg = _sigmoid(g_ref[...])
        dv, ov = d_ref[...], o_ref[...]
        do = (dv * sg).astype(BF16).astype(F32)
        dgate_ref[...] = dv * ov * sg * (1.0 - sg)
        delta = _sel_r(do * ov, down)
        _spread(do, -_placed(delta, L_C), do_ref)
        r_, c_ = _iota((D, LANES), 0), _iota((D, LANES), 1)
        lse_c = _sel_r(lse_ref[...], (r_ == c_ * FOX_DH).astype(BF16))
        q2_ref[...] = (q_ref[...].astype(F32) - _placed(lse_c, L_LSE)).astype(BF16)

    spec = pl.BlockSpec((tr, D), lambda i: (i, 0))
    aug = pl.BlockSpec((tr, FOX_AUG), lambda i: (i, 0))
    return pl.pallas_call(
        body, name="fox_gate_bwd", grid=(lp // tr,),
        in_specs=[spec, spec, pl.BlockSpec((tr, D), lambda i: (i, 3)), spec, aug], out_specs=[aug, aug, spec],
        out_shape=[jax.ShapeDtypeStruct((lp, FOX_AUG), BF16), jax.ShapeDtypeStruct((lp, FOX_AUG), BF16),
                   jax.ShapeDtypeStruct((lp, D), F32)],
        compiler_params=_params(("parallel",)),
    )(dog, o, proj, lse, qa)


def _fox_attn_bwd(q2, ka, va, doa, rs=None):
    lp = q2.shape[0]
    t = _tile(lp, FOX_TQ)
    nb = lp // t
    npair = FOX_H // 2

    def body(q_ref, k_ref, v_ref, do_ref, *rest):
        if rs is None:
            dq_ref, dk_ref, dv_ref, dc_ref, dq_acc, dk_acc, dv_acc, dc_acc = rest
        else:
            s_ref, dq_ref, dk_ref, dv_ref, dc_ref, got_ref, dq_acc, dk_acc, dv_acc, dc_acc, send_sems, recv_sems = rest
            sends, arrivals = _scatter_copies(s_ref, got_ref, send_sems, recv_sems)

            @pl.when((pl.program_id(0) == 0) & (pl.program_id(1) == 0))
            def _():
                for cp in sends:
                    cp.start()

            @pl.when((pl.program_id(0) == npair - 1) & (pl.program_id(1) == nb - 1))
            def _():
                for cp in arrivals:
                    cp.wait_recv()
                for cp in sends:
                    cp.wait_send()

        j = pl.program_id(1)

        @pl.when(j == 0)
        def _():
            dq_acc[...] = jnp.zeros_like(dq_acc)

        causal = _iota((t, t), 1) <= _iota((t, t), 0)
        ks = [k_ref[:, hh * LANES:(hh + 1) * LANES] for hh in range(2)]
        vs = [v_ref[:, hh * LANES:(hh + 1) * LANES] for hh in range(2)]
        dk_acc[...] = jnp.zeros_like(dk_acc)
        dv_acc[...] = jnp.zeros_like(dv_acc)
        dc_acc[...] = jnp.zeros_like(dc_acc)

        def block(i, diag):
            off = pl.multiple_of(i * t, t)
            for hh in range(2):
                q = q_ref[pl.ds(off, t), hh * LANES:(hh + 1) * LANES]
                do = do_ref[pl.ds(off, t), hh * LANES:(hh + 1) * LANES]
                s = nt(q, ks[hh])
                if diag:
                    s = jnp.where(causal, s, -1e30)
                p = jnp.exp(s)
                ds = p * nt(do, vs[hh])
                dc_acc[hh] += jnp.sum(ds, axis=0, keepdims=True)
                dsb = ds.astype(BF16)
                dv_acc[hh] += tn(p.astype(BF16), do)
                dk_acc[hh] += tn(dsb, q)
                dq_acc[hh, pl.ds(off, t), :] += nn(dsb, ks[hh])

        block(j, True)

        def step(i, c):
            block(i, False)
            return c

        lax.fori_loop(j + 1, nb, step, 0)
        left = _iota((t, LANES), 1) < FOX_DH
        dk_ref[...] = jnp.where(left, dk_acc[0], pltpu.roll(dk_acc[1], FOX_DH, 1))
        dv_ref[...] = jnp.where(left, dv_acc[0], pltpu.roll(dv_acc[1], FOX_DH, 1))
        for hh in range(2):
            dc_ref[hh] = jnp.broadcast_to(-dc_acc[hh], (8, t))

        @pl.when(j == nb - 1)
        def _():
            left = _iota((lp, LANES), 1) < FOX_DH
            dq_ref[...] = jnp.where(left, dq_acc[0], pltpu.roll(dq_acc[1], FOX_DH, 1))

    full = pl.BlockSpec((lp, 2 * LANES), lambda p, j: (0, p))
    kblk = pl.BlockSpec((t, 2 * LANES), lambda p, j: (j, p))
    oblk = pl.BlockSpec((t, LANES), lambda p, j: (j, p))
    in_specs = [full, kblk, kblk, full]
    out_specs = [pl.BlockSpec((lp, LANES), lambda p, j: (0, p)), oblk, oblk, pl.BlockSpec((2, 8, t), lambda p, j: (p, 0, j))]
    out_shape = [jax.ShapeDtypeStruct((lp, D), F32)] * 3 + [jax.ShapeDtypeStruct((FOX_H, 8, lp), F32)]
    scratch = [pltpu.VMEM((2, lp, LANES), F32), pltpu.VMEM((2, t, LANES), F32), pltpu.VMEM((2, t, LANES), F32),
               pltpu.VMEM((2, 1, t), F32)]
    if rs is None:
        return pl.pallas_call(body, name="fox_attn_bwd", grid=(npair, nb), in_specs=in_specs, out_specs=out_specs,
                              out_shape=out_shape, scratch_shapes=scratch,
                              compiler_params=_params(("parallel", "arbitrary")))(q2, ka, va, doa)
    return pl.pallas_call(
        body, name="fox_attn_bwd_rs", grid=(npair, nb), in_specs=in_specs + [ANY], out_specs=out_specs + [ANY],
        out_shape=out_shape + [jax.ShapeDtypeStruct((3,) + rs.shape[1:], rs.dtype)],
        scratch_shapes=scratch + [pltpu.SemaphoreType.DMA((3,)), pltpu.SemaphoreType.DMA((3,))],
        compiler_params=_params(("arbitrary", "arbitrary")),
    )(q2, ka, va, doa, rs)


def _fox_prep_bwd(proj, b_f, q_gain, k_gain, dqn, dkn, dv, dgate, dct):
    lp = proj.shape[0]
    nb = lp // LANES

    def body(p_ref, bf_ref, qg_ref, kg_ref, dq_ref, dk_ref, dv_ref, dg_ref, dc_ref,
             dp_ref, dqg_ref, dkg_ref, dbf_ref, carry):
        i = pl.program_id(0)

        @pl.when(i == 0)
        def _():
            carry[...] = jnp.zeros_like(carry)
            dqg_ref[...] = jnp.zeros_like(dqg_ref)
            dkg_ref[...] = jnp.zeros_like(dkg_ref)
            dbf_ref[...] = jnp.zeros_like(dbf_ref)

        down, up = _head_sel(FOX_H, FOX_DH)

        def norm_bwd(x, gain, dy, scale, dgain_ref):
            ms = _sel_r2(x * x, down) * (1.0 / FOX_DH)
            r = _sel_r2(lax.rsqrt(ms + EPS), up)
            u = dy * gain * scale
            mean_xu = _sel_r2(_sel_r2(x * u, down) * (1.0 / FOX_DH), up)
            dgain_ref[...] += jnp.sum(dy * scale * x * r, axis=0, keepdims=True)
            return r * u - x * (r * r * r) * mean_xu

        dp_ref[:, 0:D] = norm_bwd(p_ref[:, 0:D], qg_ref[...], dq_ref[...], FOX_DH ** -0.5, dqg_ref).astype(BF16)
        dp_ref[:, D:2 * D] = norm_bwd(p_ref[:, D:2 * D], kg_ref[...], dk_ref[...], 1.0, dkg_ref).astype(BF16)
        dp_ref[:, 2 * D:3 * D] = dv_ref[...].astype(BF16)
        dp_ref[:, 3 * D:4 * D] = dg_ref[...].astype(BF16)
        rows = jnp.concatenate([dc_ref[h, 0:1, :] for h in range(FOX_H)] + [jnp.zeros((LANES - FOX_H, LANES), F32)], axis=0)
        dlf = _sel_l(_tri(LANES, upper=True).astype(BF16), rows.T) + carry[0:1, :]
        carry[...] = jnp.broadcast_to(dlf[0:1, :], carry.shape)
        lane = _iota((LANES, LANES), 1)
        z = p_ref[:, 4 * D:4 * D + LANES] + bf_ref[...]
        df = jnp.where(lane < FOX_H, dlf * _sigmoid(-z), 0.0)
        dp_ref[:, 4 * D:4 * D + LANES] = df.astype(BF16)
        dbf_ref[...] += jnp.sum(df, axis=0, keepdims=True)

    rev = lambda i: (nb - 1 - i, 0)
    blk = pl.BlockSpec((LANES, D), rev)
    row = pl.BlockSpec((1, D), lambda i: (0, 0))
    row128 = pl.BlockSpec((1, LANES), lambda i: (0, 0))
    return pl.pallas_call(
        body, name="fox_prep_bwd", grid=(nb,),
        in_specs=[pl.BlockSpec((LANES, FOX_INP), rev), row128, row, row, blk, blk, blk, blk,
                  pl.BlockSpec((FOX_H, 8, LANES), lambda i: (0, 0, nb - 1 - i))],
        out_specs=[pl.BlockSpec((LANES, FOX_INP), rev), row, row, row128],
        out_shape=[jax.ShapeDtypeStruct((lp, FOX_INP), BF16), jax.ShapeDtypeStruct((1, D), F32),
                   jax.ShapeDtypeStruct((1, D), F32), jax.ShapeDtypeStruct((1, LANES), F32)],
        scratch_shapes=[pltpu.VMEM((8, LANES), F32)],
        compiler_params=_params(("arbitrary",)),
    )(proj, jnp.pad(b_f, (0, LANES - FOX_H)).reshape(1, LANES), jnp.tile(q_gain, FOX_H).reshape(1, D),
      jnp.tile(k_gain, FOX_H).reshape(1, D), dqn, dkn, dv, dgate, dct)


def _gla_gates(p_ref, wa_ref, ba_ref):
    a_lr = p_ref[:, 3072:3072 + LANES]
    z = nn(a_lr.astype(BF16), wa_ref[...].astype(BF16)) + ba_ref[...]
    g = _log_sigmoid(z) * (1.0 / GLA_NORM)
    b = _sel_l(_tri(CHUNK).astype(BF16), g)
    return a_lr, z, b


def _gla_chunk_fwd(q, k, v, b, st0):
    hs = range(len(q))
    low = _tri(CHUNK)
    bl = [b[h][CHUNK - 1:CHUNK, :] for h in hs]
    qe = [q[h] * jnp.exp(b[h]) for h in hs]
    ke = [k[h] * jnp.exp(-b[h]) for h in hs]
    kd = [k[h] * jnp.exp(bl[h] - b[h]) for h in hs]
    a = [jnp.where(low, nt(qe[h], ke[h]), 0.0) for h in hs]
    o = [nn(a[h], v[h]) + nt(qe[h], st0[h]) for h in hs]
    st1 = [st0[h] * jnp.exp(bl[h]) + tn(v[h], kd[h]) for h in hs]
    return o, st1, (qe, ke, kd, a, bl)


def _gla_slices(p_ref, b_all, h):
    q = p_ref[:, h * GLA_DK:(h + 1) * GLA_DK] * (GLA_DK ** -0.5)
    k = p_ref[:, GLA_QK + h * GLA_DK:GLA_QK + (h + 1) * GLA_DK]
    v = p_ref[:, 2 * GLA_QK + h * GLA_DV:2 * GLA_QK + (h + 1) * GLA_DV]
    r = p_ref[:, 2 * GLA_QK + GLA_V + h * GLA_DV:2 * GLA_QK + GLA_V + (h + 1) * GLA_DV]
    return q, k, v, r, b_all[:, h * GLA_DK:(h + 1) * GLA_DK]


def _gla_fwd(proj, w_alpha2, b_alpha, o_gain):
    lp = proj.shape[0]
    nc = lp // CHUNK

    def body(p_ref, wa_ref, ba_ref, og_ref, o_ref, y_ref, s_ref, st):
        @pl.when(pl.program_id(0) == 0)
        def _():
            st[...] = jnp.zeros_like(st)

        _, _, b_all = _gla_gates(p_ref, wa_ref, ba_ref)
        hs = range(GLA_H)
        parts = [_gla_slices(p_ref, b_all, h) for h in hs]
        st0 = [st[h] for h in hs]
        for h in hs:
            s_ref[0, h] = st0[h]
        o, st1, _ = _gla_chunk_fwd([p[0] for p in parts], [p[1] for p in parts], [p[2] for p in parts],
                                   [p[4] for p in parts], st0)
        for h in hs:
            st[h] = st1[h]
            o_ref[:, h * GLA_DV:(h + 1) * GLA_DV] = o[h]
            rs = lax.rsqrt(jnp.mean(o[h] * o[h], axis=-1, keepdims=True) + EPS)
            y_ref[:, h * GLA_DV:(h + 1) * GLA_DV] = (o[h] * rs * og_ref[...] * _silu(parts[h][3])).astype(BF16)

    blk = pl.BlockSpec((CHUNK, D), lambda i: (i, 0))
    return pl.pallas_call(
        body, name="gla_fwd", grid=(nc,),
        in_specs=[pl.BlockSpec((CHUNK, GLA_INP), lambda i: (i, 0)), pl.BlockSpec((LANES, GLA_QK), lambda i: (0, 0)),
                  pl.BlockSpec((1, GLA_QK), lambda i: (0, 0)), pl.BlockSpec((1, GLA_DV), lambda i: (0, 0))],
        out_specs=[blk, blk, pl.BlockSpec((1, GLA_H, GLA_DV, GLA_DK), lambda i: (i, 0, 0, 0))],
        out_shape=[jax.ShapeDtypeStruct((lp, D), F32), jax.ShapeDtypeStruct((lp, D), BF16),
                   jax.ShapeDtypeStruct((nc, GLA_H, GLA_DV, GLA_DK), F32)],
        scratch_shapes=[pltpu.VMEM((GLA_H, GLA_DV, GLA_DK), F32)],
        compiler_params=_params(("arbitrary",)),
    )(proj, jnp.pad(w_alpha2, ((0, LANES - GLA_RANK), (0, 0))), b_alpha.reshape(1, GLA_QK), o_gain.reshape(1, GLA_DV))


def _gla_bwd(proj, w_alpha2, b_alpha, o_gain, o, states, dy):
    lp = proj.shape[0]
    nc = lp // CHUNK

    def body(p_ref, wa_ref, ba_ref, og_ref, o_ref, s_ref, dy_ref, dp_ref, dwa_ref, dba_ref, dog_ref, dst):
        @pl.when(pl.program_id(0) == 0)
        def _():
            dst[...] = jnp.zeros_like(dst)
            dwa_ref[...] = jnp.zeros_like(dwa_ref)
            dba_ref[...] = jnp.zeros_like(dba_ref)
            dog_ref[...] = jnp.zeros_like(dog_ref)

        a_lr, z, b_all = _gla_gates(p_ref, wa_ref, ba_ref)
        last_row = _iota((CHUNK, GLA_DK), 0) == CHUNK - 1
        rev = _tri(CHUNK, upper=True).astype(BF16)
        hs = range(GLA_H)
        scale = GLA_DK ** -0.5
        parts = [_gla_slices(p_ref, b_all, h) for h in hs]
        q, k, v, b = [p[0] for p in parts], [p[1] for p in parts], [p[2] for p in parts], [p[4] for p in parts]
        st0 = [s_ref[0, h] for h in hs]
        dst1 = [dst[h] for h in hs]
        do = []
        for h in hs:
            r = parts[h][3]
            ov = o_ref[:, h * GLA_DV:(h + 1) * GLA_DV]
            dyv = dy_ref[:, h * GLA_DV:(h + 1) * GLA_DV]
            rs = lax.rsqrt(jnp.mean(ov * ov, axis=-1, keepdims=True) + EPS)
            on = ov * rs
            dp_ref[:, 2 * GLA_QK + GLA_V + h * GLA_DV:2 * GLA_QK + GLA_V + (h + 1) * GLA_DV] = (
                dyv * on * og_ref[...] * _dsilu(r)).astype(BF16)
            don = dyv * _silu(r)
            dog_ref[...] += jnp.sum(don * on, axis=0, keepdims=True)
            u = don * og_ref[...]
            do.append(rs * u - ov * (rs * rs * rs) * jnp.mean(ov * u, axis=-1, keepdims=True))
        _, _, (qe, ke, kd, a, bl) = _gla_chunk_fwd(q, k, v, b, st0)
        low = _tri(CHUNK)
        da = [jnp.where(low, nt(do[h], v[h]), 0.0) for h in hs]
        dkd = [nn(v[h], dst1[h]) for h in hs]
        dvv = [tn(a[h], do[h]) + nt(kd[h], dst1[h]) for h in hs]
        dqe = [nn(da[h], ke[h]) + nn(do[h], st0[h]) for h in hs]
        dke = [tn(da[h], qe[h]) for h in hs]
        dg_parts = []
        for h in hs:
            ebl = jnp.exp(bl[h])
            dst[h] = dst1[h] * ebl + tn(do[h], qe[h])
            db = dqe[h] * qe[h] - dke[h] * ke[h] - dkd[h] * kd[h]
            db_last = (jnp.sum(dkd[h] * kd[h], axis=0, keepdims=True)
                       + jnp.sum(dst1[h] * st0[h], axis=0, keepdims=True) * ebl)
            db = db + jnp.where(last_row, db_last, 0.0)
            dg_parts.append(_sel_l(rev, db))
            dp_ref[:, h * GLA_DK:(h + 1) * GLA_DK] = (dqe[h] * jnp.exp(b[h]) * scale).astype(BF16)
            dp_ref[:, GLA_QK + h * GLA_DK:GLA_QK + (h + 1) * GLA_DK] = (
                dke[h] * jnp.exp(-b[h]) + dkd[h] * jnp.exp(bl[h] - b[h])).astype(BF16)
            dp_ref[:, 2 * GLA_QK + h * GLA_DV:2 * GLA_QK + (h + 1) * GLA_DV] = dvv[h].astype(BF16)
        dg = jnp.concatenate(dg_parts, axis=1)
        dz = dg * (1.0 / GLA_NORM) * _sigmoid(-z)
        dzb = dz.astype(BF16)
        dp_ref[:, 3072:3072 + LANES] = nt(dzb, wa_ref[...].astype(BF16)).astype(BF16)
        dwa_ref[...] += tn(a_lr.astype(BF16), dzb)
        dba_ref[...] += jnp.sum(dz, axis=0, keepdims=True)

    rv = lambda i: (nc - 1 - i, 0)
    blk = pl.BlockSpec((CHUNK, D), rv)
    fixed = lambda r, c: pl.BlockSpec((r, c), lambda i: (0, 0))
    return pl.pallas_call(
        body, name="gla_bwd", grid=(nc,),
        in_specs=[pl.BlockSpec((CHUNK, GLA_INP), rv), fixed(LANES, GLA_QK), fixed(1, GLA_QK), fixed(1, GLA_DV), blk,
                  pl.BlockSpec((1, GLA_H, GLA_DV, GLA_DK), lambda i: (nc - 1 - i, 0, 0, 0)), blk],
        out_specs=[pl.BlockSpec((CHUNK, GLA_INP), rv), fixed(LANES, GLA_QK), fixed(1, GLA_QK), fixed(1, GLA_DV)],
        out_shape=[jax.ShapeDtypeStruct((lp, GLA_INP), BF16), jax.ShapeDtypeStruct((LANES, GLA_QK), F32),
                   jax.ShapeDtypeStruct((1, GLA_QK), F32), jax.ShapeDtypeStruct((1, GLA_DV), F32)],
        scratch_shapes=[pltpu.VMEM((GLA_H, GLA_DV, GLA_DK), F32)],
        compiler_params=_params(("arbitrary",)),
    )(proj, jnp.pad(w_alpha2, ((0, LANES - GLA_RANK), (0, 0))), b_alpha.reshape(1, GLA_QK), o_gain.reshape(1, GLA_DV),
      o, states, dy)


HI = lax.Precision.HIGHEST


def _gdn_pre(prev_ref, p_ref, cw_ref, al_ref, dt_ref):
    xc = jnp.concatenate([prev_ref[:, 0:GDN_CONV], p_ref[:, 0:GDN_CONV]], axis=0)
    shifted = [pltpu.roll(xc, 3 - j, 0)[CHUNK:, :] if j < 3 else xc[CHUNK:, :] for j in range(4)]
    conv = sum(shifted[j] * cw_ref[j:j + 1, :] for j in range(4))
    act = _silu(conv)
    slab = p_ref[:, 4096:4096 + LANES]
    lane = _iota((CHUNK, LANES), 1)
    zs = slab + dt_ref[...]
    g = jnp.where(lane < GDN_H, -jnp.exp(al_ref[...]) * _softplus(zs), 0.0)
    bs = _sel_l(_tri(CHUNK).astype(BF16), g)
    beta = _sigmoid(slab)
    return shifted, conv, act, slab, zs, g, bs, beta


def _l2n(x):
    r = lax.rsqrt(jnp.sum(x * x, axis=-1, keepdims=True) + EPS)
    return x * r, r


def _gdn_chunk_fwd(q, k, v, beta, bcol, brow, s0):
    hs = range(len(q))
    ii, jj = _iota((CHUNK, CHUNK), 0), _iota((CHUNK, CHUNK), 1)
    low, eye = ii >= jj, (ii == jj).astype(F32)
    dm = [jnp.where(low, jnp.exp(jnp.where(low, bcol[h] - brow[h], 0.0)), 0.0) for h in hs]
    dstrict = [jnp.where(ii > jj, dm[h], 0.0) for h in hs]
    eb = [jnp.exp(bcol[h]) for h in hs]
    bl = [bcol[h][CHUNK - 1:CHUNK, :] for h in hs]
    kb = [k[h] * beta[h] for h in hs]
    vb = [v[h] * beta[h] for h in hs]
    nmat = [nt(kb[h], k[h]) * dstrict[h] for h in hs]
    x = [eye - nmat[h] for h in hs]
    pw = [nn(nmat[h], nmat[h], precision=HI) for h in hs]
    for it in range(5):
        x = [x[h] + nn(x[h], pw[h], precision=HI) for h in hs]
        if it < 4:
            pw = [nn(pw[h], pw[h], precision=HI) for h in hs]
    kbe = [kb[h] * eb[h] for h in hs]
    u = [nn(x[h], vb[h], precision=HI) for h in hs]
    w = [nn(x[h], kbe[h], precision=HI) for h in hs]
    vn = [u[h] - nn(w[h], s0[h]) for h in hs]
    pm = [nt(q[h], k[h]) * dm[h] for h in hs]
    qe = [q[h] * eb[h] for h in hs]
    o = [nn(pm[h], vn[h]) + nn(qe[h], s0[h]) for h in hs]
    kd = [k[h] * jnp.exp(bl[h] - bcol[h]) for h in hs]
    s1 = [s0[h] * jnp.exp(bl[h]) + tn(kd[h], vn[h]) for h in hs]
    return o, s1, dict(dm=dm, dstrict=dstrict, eb=eb, bl=bl, kb=kb, vb=vb, nmat=nmat, tinv=x, kbe=kbe, u=u, w=w, vn=vn,
                       pm=pm, qe=qe, kd=kd)


def _gdn_heads(act, beta_slab, bs, h):
    qa = act[:, h * GDN_DK:(h + 1) * GDN_DK]
    ka = act[:, GDN_H * GDN_DK + h * GDN_DK:GDN_H * GDN_DK + (h + 1) * GDN_DK]
    v = act[:, 2 * GDN_H * GDN_DK + h * GDN_DV:2 * GDN_H * GDN_DK + (h + 1) * GDN_DV]
    return qa, ka, v, beta_slab[:, GDN_H + h:GDN_H + h + 1], bs[:, h:h + 1]


def _gdn_fwd(proj, conv_w, a_log, dt_bias, o_gain):
    lp = proj.shape[0]
    nc = lp // CHUNK

    def body(prev_ref, p_ref, cw_ref, al_ref, dt_ref, og_ref, o_ref, y_ref, s_ref, st):
        @pl.when(pl.program_id(0) == 0)
        def _():
            st[...] = jnp.zeros_like(st)

        _, _, act, _, _, _, bs, beta = _gdn_pre(prev_ref, p_ref, cw_ref, al_ref, dt_ref)
        bst = bs.T
        hs = range(GDN_H)
        parts = [_gdn_heads(act, beta, bs, h) for h in hs]
        q = [_l2n(parts[h][0])[0] * (GDN_DK ** -0.5) for h in hs]
        k = [_l2n(parts[h][1])[0] for h in hs]
        s0 = [st[h] for h in hs]
        for h in hs:
            s_ref[0, h] = s0[h]
        o, s1, _ = _gdn_chunk_fwd(q, k, [parts[h][2] for h in hs], [parts[h][3] for h in hs], [parts[h][4] for h in hs],
                                  [bst[h:h + 1, :] for h in hs], s0)
        for h in hs:
            st[h] = s1[h]
            o_ref[:, h * GDN_DV:(h + 1) * GDN_DV] = o[h]
            rs = lax.rsqrt(jnp.mean(o[h] * o[h], axis=-1, keepdims=True) + EPS)
            gate = p_ref[:, GDN_CONV + h * GDN_DV:GDN_CONV + (h + 1) * GDN_DV]
            y_ref[:, h * GDN_DV:(h + 1) * GDN_DV] = (o[h] * rs * og_ref[...] * _silu(gate)).astype(BF16)

    blk = pl.BlockSpec((CHUNK, D), lambda i: (i, 0))
    fixed = lambda r, c: pl.BlockSpec((r, c), lambda i: (0, 0))
    return pl.pallas_call(
        body, name="gdn_fwd", grid=(nc,),
        in_specs=[pl.BlockSpec((CHUNK, GDN_INP), lambda i: (jnp.maximum(i - 1, 0), 0)),
                  pl.BlockSpec((CHUNK, GDN_INP), lambda i: (i, 0)), fixed(8, GDN_CONV), fixed(1, LANES), fixed(1, LANES),
                  fixed(1, GDN_DV)],
        out_specs=[blk, blk, pl.BlockSpec((1, GDN_H, GDN_DK, GDN_DV), lambda i: (i, 0, 0, 0))],
        out_shape=[jax.ShapeDtypeStruct((lp, D), F32), jax.ShapeDtypeStruct((lp, D), BF16),
                   jax.ShapeDtypeStruct((nc, GDN_H, GDN_DK, GDN_DV), F32)],
        scratch_shapes=[pltpu.VMEM((GDN_H, GDN_DK, GDN_DV), F32)],
        compiler_params=_params(("arbitrary",)),
    )(proj, proj, jnp.pad(conv_w.reshape(4, GDN_CONV), ((0, 4), (0, 0))), jnp.pad(a_log, (0, LANES - GDN_H)).reshape(1, LANES),
      jnp.pad(dt_bias, (0, LANES - GDN_H)).reshape(1, LANES), o_gain.reshape(1, GDN_DV))


def _gdn_bwd(proj, conv_w, a_log, dt_bias, o_gain, o, states, dy):
    lp = proj.shape[0]
    nc = lp // CHUNK

    def body(prev_ref, p_ref, cw_ref, al_ref, dt_ref, og_ref, o_ref, s_ref, dy_ref,
             dp_ref, dcw_ref, dal_ref, ddt_ref, dog_ref, dst, dconv_next):
        @pl.when(pl.program_id(0) == 0)
        def _():
            dst[...] = jnp.zeros_like(dst)
            dconv_next[...] = jnp.zeros_like(dconv_next)
            dcw_ref[...] = jnp.zeros_like(dcw_ref)
            dal_ref[...] = jnp.zeros_like(dal_ref)
            ddt_ref[...] = jnp.zeros_like(ddt_ref)
            dog_ref[...] = jnp.zeros_like(dog_ref)

        shifted, conv, act, slab, zs, g, bs, beta = _gdn_pre(prev_ref, p_ref, cw_ref, al_ref, dt_ref)
        bst = bs.T
        lane = _iota((CHUNK, LANES), 1)
        ones = jnp.ones((CHUNK, LANES), F32)
        db_slab = jnp.zeros((CHUNK, LANES), F32)
        dbeta_slab = jnp.zeros((CHUNK, LANES), F32)
        last_row = _iota((CHUNK, 1), 0) == CHUNK - 1
        hs = range(GDN_H)
        scale = GDN_DK ** -0.5
        parts = [_gdn_heads(act, beta, bs, h) for h in hs]
        qa, ka, v = [parts[h][0] for h in hs], [parts[h][1] for h in hs], [parts[h][2] for h in hs]
        bet, bcol = [parts[h][3] for h in hs], [parts[h][4] for h in hs]
        qn_ = [_l2n(qa[h]) for h in hs]
        kn_ = [_l2n(ka[h]) for h in hs]
        q = [qn_[h][0] * scale for h in hs]
        k, rq, rk = [kn_[h][0] for h in hs], [qn_[h][1] for h in hs], [kn_[h][1] for h in hs]
        s0 = [s_ref[0, h] for h in hs]
        ds1 = [dst[h] for h in hs]
        do = []
        for h in hs:
            ov = o_ref[:, h * GDN_DV:(h + 1) * GDN_DV]
            dyv = dy_ref[:, h * GDN_DV:(h + 1) * GDN_DV]
            gate = p_ref[:, GDN_CONV + h * GDN_DV:GDN_CONV + (h + 1) * GDN_DV]
            rs = lax.rsqrt(jnp.mean(ov * ov, axis=-1, keepdims=True) + EPS)
            on = ov * rs
            dp_ref[:, GDN_CONV + h * GDN_DV:GDN_CONV + (h + 1) * GDN_DV] = (dyv * on * og_ref[...] * _dsilu(gate)).astype(BF16)
            don = dyv * _silu(gate)
            dog_ref[...] += jnp.sum(don * on, axis=0, keepdims=True)
            uu = don * og_ref[...]
            do.append(rs * uu - ov * (rs * rs * rs) * jnp.mean(ov * uu, axis=-1, keepdims=True))
        _, _, f = _gdn_chunk_fwd(q, k, v, bet, bcol, [bst[h:h + 1, :] for h in hs], s0)
        dm, dstrict, eb, bl, kb, nmat, tinv = f["dm"], f["dstrict"], f["eb"], f["bl"], f["kb"], f["nmat"], f["tinv"]
        kbe, u, w, vn, pm, qe, kd = f["kbe"], f["u"], f["w"], f["vn"], f["pm"], f["qe"], f["kd"]
        ebl = [jnp.exp(bl[h]) for h in hs]
        dvn = [tn(pm[h], do[h]) + nn(kd[h], ds1[h]) for h in hs]
        dpr = [nt(do[h], vn[h]) for h in hs]
        dqe = [nt(do[h], s0[h]) for h in hs]
        dkd = [nt(vn[h], ds1[h]) for h in hs]
        for h in hs:
            dst[h] = ds1[h] * ebl[h] + tn(qe[h], do[h]) - tn(w[h], dvn[h])
        du_ = [tn(tinv[h], dvn[h], precision=HI) for h in hs]
        dw_ = [tn(tinv[h], -nt(dvn[h], s0[h]), precision=HI) for h in hs]
        dn = [-(nt(du_[h], u[h]) + nt(dw_[h], w[h])) for h in hs]
        dqk = [dpr[h] * dm[h] for h in hs]
        dkk = [dn[h] * dstrict[h] for h in hs]
        gsum = [dpr[h] * pm[h] + dn[h] * nmat[h] for h in hs]
        dkb = [nn(dkk[h], k[h]) + dw_[h] * eb[h] for h in hs]
        dk = [tn(dkk[h], kb[h]) + tn(dqk[h], q[h]) + dkd[h] * jnp.exp(bl[h] - bcol[h]) + dkb[h] * bet[h] for h in hs]
        dq = [nn(dqk[h], k[h]) + dqe[h] * eb[h] for h in hs]
        colsum = [tn(gsum[h], ones, precision=HI)[:, 0:1] for h in hs]
        dact_q, dact_k, dact_v = [], [], []
        for h in hs:
            dbeta = jnp.sum(dkb[h] * k[h], axis=-1, keepdims=True) + jnp.sum(du_[h] * v[h], axis=-1, keepdims=True)
            skd = jnp.sum(dkd[h] * kd[h], axis=-1, keepdims=True)
            db = (jnp.sum(gsum[h], axis=-1, keepdims=True) - colsum[h] + jnp.sum(dqe[h] * qe[h], axis=-1, keepdims=True)
                  + jnp.sum(dw_[h] * kbe[h], axis=-1, keepdims=True) - skd)
            db_last = jnp.sum(skd, axis=0, keepdims=True) + jnp.sum(ds1[h] * s0[h]) * ebl[h]
            db = db + jnp.where(last_row, db_last, 0.0)
            db_slab = db_slab + jnp.where(lane == h, db, 0.0)
            dbeta_slab = dbeta_slab + jnp.where(lane == GDN_H + h, dbeta, 0.0)
            dqn = dq[h] * scale
            dact_q.append(rq[h] * dqn - qa[h] * (rq[h] * rq[h] * rq[h]) * jnp.sum(qa[h] * dqn, axis=-1, keepdims=True))
            dact_k.append(rk[h] * dk[h] - ka[h] * (rk[h] * rk[h] * rk[h]) * jnp.sum(ka[h] * dk[h], axis=-1, keepdims=True))
            dact_v.append(du_[h] * bet[h])
        dact = jnp.concatenate(dact_q + dact_k + dact_v, axis=1)
        dconv = dact * _dsilu(conv)
        for j in range(4):
            dcw_ref[j:j + 1, :] += jnp.sum(dconv * shifted[j], axis=0, keepdims=True)
        dcat = jnp.concatenate([dconv, dconv_next[...]], axis=0)
        dx = dconv * cw_ref[3:4, :]
        for j in range(3):
            dx = dx + pltpu.roll(dcat, 2 * CHUNK - (3 - j), 0)[:CHUNK, :] * cw_ref[j:j + 1, :]
        dconv_next[...] = dconv
        dp_ref[:, 0:GDN_CONV] = dx.astype(BF16)
        dg = _sel_l(_tri(CHUNK, upper=True).astype(BF16), db_slab)
        da = dg * (-jnp.exp(al_ref[...])) * _sigmoid(zs)
        da = jnp.where(lane < GDN_H, da, 0.0)
        dal_ref[...] += jnp.sum(dg * g, axis=0, keepdims=True)
        ddt_ref[...] += jnp.sum(da, axis=0, keepdims=True)
        dp_ref[:, 4096:4096 + LANES] = (da + dbeta_slab * beta * (1.0 - beta)).astype(BF16)

    rv = lambda i: (nc - 1 - i, 0)
    blk = pl.BlockSpec((CHUNK, D), rv)
    fixed = lambda r, c: pl.BlockSpec((r, c), lambda i: (0, 0))
    return pl.pallas_call(
        body, name="gdn_bwd", grid=(nc,),
        in_specs=[pl.BlockSpec((CHUNK, GDN_INP), lambda i: (jnp.maximum(nc - 2 - i, 0), 0)),
                  pl.BlockSpec((CHUNK, GDN_INP), rv), fixed(8, GDN_CONV), fixed(1, LANES), fixed(1, LANES), fixed(1, GDN_DV),
                  blk, pl.BlockSpec((1, GDN_H, GDN_DK, GDN_DV), lambda i: (nc - 1 - i, 0, 0, 0)), blk],
        out_specs=[pl.BlockSpec((CHUNK, GDN_INP), rv), fixed(8, GDN_CONV), fixed(1, LANES), fixed(1, LANES), fixed(1, GDN_DV)],
        out_shape=[jax.ShapeDtypeStruct((lp, GDN_INP), BF16), jax.ShapeDtypeStruct((8, GDN_CONV), F32),
                   jax.ShapeDtypeStruct((1, LANES), F32), jax.ShapeDtypeStruct((1, LANES), F32),
                   jax.ShapeDtypeStruct((1, GDN_DV), F32)],
        scratch_shapes=[pltpu.VMEM((GDN_H, GDN_DK, GDN_DV), F32), pltpu.VMEM((CHUNK, GDN_CONV), F32)],
        compiler_params=_params(("arbitrary",)),
    )(proj, proj, jnp.pad(conv_w.reshape(4, GDN_CONV), ((0, 4), (0, 0))), jnp.pad(a_log, (0, LANES - GDN_H)).reshape(1, LANES),
      jnp.pad(dt_bias, (0, LANES - GDN_H)).reshape(1, LANES), o_gain.reshape(1, GDN_DV), o, states, dy)


def _coords():
    return lax.axis_index("x"), lax.axis_index("y"), lax.axis_index("c")


def _other_chips(x, y):
    return [(1 - x, y, 2 * (1 - x) + y), (x, 1 - y, 2 * x + 1 - y), (1 - x, 1 - y, 2 * (1 - x) + 1 - y)]


def _gather8(v, *, reduce, name):
    r, c = v.shape

    def body(v_ref, out_ref, *scratch):
        if reduce:
            buf, send_sems, recv_sems = scratch
        else:
            buf = out_ref
            send_sems, recv_sems = scratch
        x, y, cc = _coords()
        me = 4 * x + 2 * y + cc
        buf[me] = v_ref[...]
        copies = []
        for k in range(1, 8):
            px, py, pc = x ^ (k >> 2), y ^ ((k >> 1) & 1), cc ^ (k & 1)
            copies.append(pltpu.make_async_remote_copy(
                src_ref=v_ref, dst_ref=buf.at[me], send_sem=send_sems.at[k - 1], recv_sem=recv_sems.at[k - 1],
                device_id=(px, py, pc), device_id_type=MESH))
        for cp in copies:
            cp.start()
        for k in range(1, 8):
            peer = (x ^ (k >> 2)) * 4 + (y ^ ((k >> 1) & 1)) * 2 + (cc ^ (k & 1))
            pltpu.make_async_remote_copy(
                src_ref=v_ref, dst_ref=buf.at[peer], send_sem=send_sems.at[k - 1], recv_sem=recv_sems.at[k - 1],
                device_id=(x, y, cc), device_id_type=MESH).wait_recv()
        for cp in copies:
            cp.wait_send()
        if reduce:
            acc = buf[0]
            for d in range(1, 8):
                acc = acc + buf[d]
            out_ref[...] = acc

    scratch = [pltpu.SemaphoreType.DMA((7,)), pltpu.SemaphoreType.DMA((7,))]
    if reduce:
        scratch = [pltpu.VMEM((8, r, c), F32)] + scratch
    return pl.pallas_call(
        body, name=name, in_specs=[VM], out_specs=VM,
        out_shape=jax.ShapeDtypeStruct((r, c) if reduce else (8, r, c), F32),
        scratch_shapes=scratch, compiler_params=_params(),
    )(v)


class _AgCopies:
    def __init__(self, buf, ranges, send_sems, recv_sems):
        self.buf, self.ranges, self.send_sems, self.recv_sems = buf, ranges, send_sems, recv_sems
        self.x, self.y, self.cc = _coords()
        self.p = 2 * self.x + self.y
        self.chips = _other_chips(self.x, self.y)

    def rows(self, chip, r, hf):
        start, n = self.ranges[r]
        return self.buf.at[chip, pl.ds(start + hf * (n // 2), n // 2), :]

    def _copy(self, r, k, chip, hf, to):
        return pltpu.make_async_remote_copy(
            src_ref=self.rows(chip, r, hf), dst_ref=self.rows(chip, r, hf), send_sem=self.send_sems.at[3 * r + k],
            recv_sem=self.recv_sems.at[3 * r + k], device_id=to, device_id_type=MESH)

    def pairs(self):
        return [(r, k) for r in range(len(self.ranges)) for k in range(3)]

    def ici(self, r, k):
        cx, cy, _ = self.chips[k]
        return self._copy(r, k, self.p, self.cc, (cx, cy, self.cc))

    def ici_arrival(self, r, k):
        return self._copy(r, k, self.chips[k][2], self.cc, (self.x, self.y, self.cc))

    def forward(self, r, k):
        return self._copy(r, k, self.chips[k][2], self.cc, (self.x, self.y, 1 - self.cc))

    def forward_arrival(self, r, k):
        return self._copy(r, k, self.chips[k][2], 1 - self.cc, (self.x, self.y, self.cc))


def _ag_weights(w4, ranges):
    n = 3 * len(ranges)

    def body(w_ref, out_ref, send1, recv1, send2, recv2):
        ici, fwd = _AgCopies(out_ref, ranges, send1, recv1), _AgCopies(out_ref, ranges, send2, recv2)
        for r, k in ici.pairs():
            ici.ici(r, k).start()
        for r, k in ici.pairs():
            ici.ici_arrival(r, k).wait_recv()
            fwd.forward(r, k).start()
        for r, k in ici.pairs():
            fwd.forward_arrival(r, k).wait_recv()
        for r, k in ici.pairs():
            ici.ici(r, k).wait_send()
            fwd.forward(r, k).wait_send()

    return pl.pallas_call(
        body, name="ag_weights", in_specs=[ANY], out_specs=ANY, out_shape=jax.ShapeDtypeStruct(w4.shape, w4.dtype),
        scratch_shapes=[pltpu.SemaphoreType.DMA((n,))] * 4, input_output_aliases={0: 0}, compiler_params=_params(),
    )(w4)


def _ag_forward(w4, ranges):
    n = 3 * len(ranges)

    def body(w_ref, out_ref, send2, recv2):
        fwd = _AgCopies(out_ref, ranges, send2, recv2)
        for r, k in fwd.pairs():
            fwd.forward(r, k).start()
        for r, k in fwd.pairs():
            fwd.forward_arrival(r, k).wait_recv()
        for r, k in fwd.pairs():
            fwd.forward(r, k).wait_send()

    return pl.pallas_call(
        body, name="ag_forward", in_specs=[ANY], out_specs=ANY, out_shape=jax.ShapeDtypeStruct(w4.shape, w4.dtype),
        scratch_shapes=[pltpu.SemaphoreType.DMA((n,))] * 2, input_output_aliases={0: 0}, compiler_params=_params(),
    )(w4)


def _swap_halves(g, *, name):
    nb, r, c = g.shape
    half = r // 2

    def body(g_ref, out_ref, send_sem, recv_sem):
        x, y, cc = _coords()
        cp = pltpu.make_async_remote_copy(
            src_ref=g_ref.at[:, pl.ds((1 - cc) * half, half), :], dst_ref=out_ref, send_sem=send_sem, recv_sem=recv_sem,
            device_id=(x, y, 1 - cc), device_id_type=MESH)
        cp.start()
        cp.wait()

    return pl.pallas_call(
        body, name=name, in_specs=[ANY], out_specs=ANY, out_shape=jax.ShapeDtypeStruct((nb, half, c), g.dtype),
        scratch_shapes=[pltpu.SemaphoreType.DMA, pltpu.SemaphoreType.DMA], compiler_params=_params(),
    )(g)


def _my_half_index():
    return lax.axis_index("c").astype(jnp.int32).reshape(1)


def _add_halves(g, got, tag):
    nb, r, c = g.shape
    half = r // 2
    tr = _tile(half, 512, 16)
    nt_ = half // tr

    def body(c_ref, a_ref, b_ref, o_ref):
        o_ref[...] = (a_ref[...].astype(F32) + b_ref[...].astype(F32)).astype(BF16)

    return pl.pallas_call(
        body, name=f"rs_add_sibling{tag}",
        grid_spec=pltpu.PrefetchScalarGridSpec(
            num_scalar_prefetch=1, grid=(nb, nt_),
            in_specs=[pl.BlockSpec((1, tr, c), lambda b, i, cr: (b, cr[0] * nt_ + i, 0)),
                      pl.BlockSpec((1, tr, c), lambda b, i, cr: (b, i, 0))],
            out_specs=pl.BlockSpec((1, tr, c), lambda b, i, cr: (b, i, 0))),
        out_shape=jax.ShapeDtypeStruct((nb, half, c), BF16), compiler_params=_params(("parallel", "parallel")),
    )(_my_half_index(), g, got)


def _scatter_copies(s_ref, out_ref, send_sems, recv_sems):
    x, y, cc = _coords()
    sends = [pltpu.make_async_remote_copy(
        src_ref=s_ref.at[blk], dst_ref=out_ref.at[k], send_sem=send_sems.at[k], recv_sem=recv_sems.at[k],
        device_id=(cx, cy, cc), device_id_type=MESH) for k, (cx, cy, blk) in enumerate(_other_chips(x, y))]
    arrivals = [pltpu.make_async_remote_copy(
        src_ref=s_ref.at[2 * x + y], dst_ref=out_ref.at[k], send_sem=send_sems.at[k], recv_sem=recv_sems.at[k],
        device_id=(x, y, cc), device_id_type=MESH) for k in range(3)]
    return sends, arrivals


def _scatter_chips(s, tag):
    nb, hrows, c = s.shape

    def body(s_ref, out_ref, send_sems, recv_sems):
        sends, arrivals = _scatter_copies(s_ref, out_ref, send_sems, recv_sems)
        for cp in sends:
            cp.start()
        for cp in arrivals:
            cp.wait_recv()
        for cp in sends:
            cp.wait_send()

    return pl.pallas_call(
        body, name=f"rs_scatter{tag}", in_specs=[ANY], out_specs=ANY, out_shape=jax.ShapeDtypeStruct((3, hrows, c), s.dtype),
        scratch_shapes=[pltpu.SemaphoreType.DMA((3,)), pltpu.SemaphoreType.DMA((3,))], compiler_params=_params(),
    )(s)


def _sum_chips(s, got, tag):
    nb, hrows, c = s.shape
    tr = _tile(hrows, 512, 16)

    def body(idx_ref, own_ref, got_ref, o_ref):
        p = idx_ref[0]
        own = own_ref[0].astype(F32)
        parts = [got_ref[k].astype(F32) for k in range(3)]
        acc = jnp.zeros_like(own)
        for q in range(4):
            val = own
            for k, rel in enumerate((2, 1, 3)):
                val = jnp.where((p ^ rel) == q, parts[k], val)
            acc = acc + val
        o_ref[...] = acc

    idx = (2 * lax.axis_index("x") + lax.axis_index("y")).astype(jnp.int32).reshape(1)
    return pl.pallas_call(
        body, name=f"rs_sum_chips{tag}",
        grid_spec=pltpu.PrefetchScalarGridSpec(
            num_scalar_prefetch=1, grid=(hrows // tr,),
            in_specs=[pl.BlockSpec((1, tr, c), lambda i, pr: (pr[0], i, 0)), pl.BlockSpec((3, tr, c), lambda i, pr: (0, i, 0))],
            out_specs=pl.BlockSpec((tr, c), lambda i, pr: (i, 0))),
        out_shape=jax.ShapeDtypeStruct((hrows, c), F32), compiler_params=_params(("parallel",)),
    )(idx, s, got)


def _swap_sibling(t, tag):
    def body(t_ref, out_ref, send_sem, recv_sem):
        x, y, cc = _coords()
        cp = pltpu.make_async_remote_copy(src_ref=t_ref, dst_ref=out_ref, send_sem=send_sem, recv_sem=recv_sem,
                                          device_id=(x, y, 1 - cc), device_id_type=MESH)
        cp.start()
        cp.wait()

    return pl.pallas_call(
        body, name=f"rs_join{tag}", in_specs=[ANY], out_specs=ANY, out_shape=jax.ShapeDtypeStruct(t.shape, t.dtype),
        scratch_shapes=[pltpu.SemaphoreType.DMA, pltpu.SemaphoreType.DMA], compiler_params=_params(),
    )(t)


def _rs_local(g, tag):
    return _add_halves(g, _swap_halves(g, name=f"rs_swap{tag}"), tag)


def _rs_finish(s, recv, tag):
    t = _sum_chips(s, recv, tag)
    r = _swap_sibling(t, tag)
    first = lax.axis_index("c") == 0
    return jnp.concatenate([jnp.where(first, t, r), jnp.where(first, r, t)], axis=0)


_BIG = (("w_gate_up", 2), ("w_down", 1), ("fox_w_in", 2), ("fox_w_out", 1), ("gla_w_in", 2), ("gla_w_out", 1),
        ("gdn_w_in", 2), ("gdn_w_out", 1))
_SMALL_SHARDED = (("meta_tokens", 1), ("gla_w_alpha2", 2), ("gdn_conv_w", 3))
_REPLICATED = ("norm_mix", "norm_ffn", "fox_b_f", "fox_q_gain", "fox_k_gain", "gla_b_alpha", "gla_o_gain",
               "gdn_a_log", "gdn_dt_bias", "gdn_o_gain")
_WEIGHTS = ("meta_tokens", "norm_mix", "norm_ffn", "w_gate_up", "w_down", "fox_w_in", "fox_b_f", "fox_q_gain",
            "fox_k_gain", "fox_w_out", "gla_w_in", "gla_w_alpha2", "gla_b_alpha", "gla_o_gain", "gla_w_out",
            "gdn_w_in", "gdn_conv_w", "gdn_a_log", "gdn_dt_bias", "gdn_o_gain", "gdn_w_out")
_PACK_ROWS = 512
_IN_W = ("fox_w_in", "gla_w_in", "gdn_w_in")
_OUT_W = ("fox_w_out", "gla_w_out", "gdn_w_out")


def _piece_rows(n):
    return -(-n // 32) * 32


def _pack(arrays, width, row_mult, dtype):
    flat = jnp.concatenate([a.astype(dtype).reshape(-1) for a in arrays])
    per = width * row_mult
    n = -(-flat.shape[0] // per) * per
    return jnp.pad(flat, (0, n - flat.shape[0])).reshape(n // width, width)


def _unpack(flat, shapes):
    out, off = [], 0
    for s in shapes:
        n = 1
        for d in s:
            n *= d
        out.append(flat[off:off + n].reshape(s))
        off += n
    return out


def _unpack_cols(flat2, shapes):
    out, off = [], 0
    for s in shapes:
        n = 1
        for d in s:
            n *= d
        out.append(flat2[:, off:off + n].reshape((flat2.shape[0],) + tuple(s)))
        off += n
    return out


def _pad_cols(w, n):
    return jnp.pad(w, [(0, 0)] * (w.ndim - 1) + [(0, n - w.shape[-1])])


def kernel(x, meta_tokens, norm_mix, norm_ffn, w_gate_up, w_down, fox_w_in, fox_b_f, fox_q_gain, fox_k_gain, fox_w_out, gla_w_in, gla_w_alpha2, gla_b_alpha, gla_o_gain, gla_w_out, gdn_w_in, gdn_conv_w, gdn_a_log, gdn_dt_bias, gdn_o_gain, gdn_w_out, loss_target, m_meta_tokens, m_norm_mix, m_norm_ffn, m_w_gate_up, m_w_down, m_fox_w_in, m_fox_b_f, m_fox_q_gain, m_fox_k_gain, m_fox_w_out, m_gla_w_in, m_gla_w_alpha2, m_gla_b_alpha, m_gla_o_gain, m_gla_w_out, m_gdn_w_in, m_gdn_conv_w, m_gdn_a_log, m_gdn_dt_bias, m_gdn_o_gain, m_gdn_w_out, v_meta_tokens, v_norm_mix, v_norm_ffn, v_w_gate_up, v_w_down, v_fox_w_in, v_fox_b_f, v_fox_q_gain, v_fox_k_gain, v_fox_w_out, v_gla_w_in, v_gla_w_alpha2, v_gla_b_alpha, v_gla_o_gain, v_gla_w_out, v_gdn_w_in, v_gdn_conv_w, v_gdn_a_log, v_gdn_dt_bias, v_gdn_o_gain, v_gdn_w_out):
    W = dict(meta_tokens=meta_tokens, norm_mix=norm_mix, norm_ffn=norm_ffn, w_gate_up=w_gate_up, w_down=w_down,
             fox_w_in=fox_w_in, fox_b_f=fox_b_f, fox_q_gain=fox_q_gain, fox_k_gain=fox_k_gain, fox_w_out=fox_w_out,
             gla_w_in=gla_w_in, gla_w_alpha2=gla_w_alpha2, gla_b_alpha=gla_b_alpha, gla_o_gain=gla_o_gain,
             gla_w_out=gla_w_out, gdn_w_in=gdn_w_in, gdn_conv_w=gdn_conv_w, gdn_a_log=gdn_a_log,
             gdn_dt_bias=gdn_dt_bias, gdn_o_gain=gdn_o_gain, gdn_w_out=gdn_w_out)
    M = dict(meta_tokens=m_meta_tokens, norm_mix=m_norm_mix, norm_ffn=m_norm_ffn, w_gate_up=m_w_gate_up, w_down=m_w_down,
             fox_w_in=m_fox_w_in, fox_b_f=m_fox_b_f, fox_q_gain=m_fox_q_gain, fox_k_gain=m_fox_k_gain,
             fox_w_out=m_fox_w_out, gla_w_in=m_gla_w_in, gla_w_alpha2=m_gla_w_alpha2, gla_b_alpha=m_gla_b_alpha,
             gla_o_gain=m_gla_o_gain, gla_w_out=m_gla_w_out, gdn_w_in=m_gdn_w_in, gdn_conv_w=m_gdn_conv_w,
             gdn_a_log=m_gdn_a_log, gdn_dt_bias=m_gdn_dt_bias, gdn_o_gain=m_gdn_o_gain, gdn_w_out=m_gdn_w_out)
    V = dict(meta_tokens=v_meta_tokens, norm_mix=v_norm_mix, norm_ffn=v_norm_ffn, w_gate_up=v_w_gate_up, w_down=v_w_down,
             fox_w_in=v_fox_w_in, fox_b_f=v_fox_b_f, fox_q_gain=v_fox_q_gain, fox_k_gain=v_fox_k_gain,
             fox_w_out=v_fox_w_out, gla_w_in=v_gla_w_in, gla_w_alpha2=v_gla_w_alpha2, gla_b_alpha=v_gla_b_alpha,
             gla_o_gain=v_gla_o_gain, gla_w_out=v_gla_w_out, gdn_w_in=v_gdn_w_in, gdn_conv_w=v_gdn_conv_w,
             gdn_a_log=v_gdn_a_log, gdn_dt_bias=v_gdn_dt_bias, gdn_o_gain=v_gdn_o_gain, gdn_w_out=v_gdn_w_out)
    chip = 2 * lax.axis_index("x") + lax.axis_index("y")

    pieces, offs, r = [], {}, FFN_ROWS
    for n in _IN_W:
        nc = W[n].shape[2]
        for l in range(W[n].shape[0]):
            pieces.append(jnp.pad(W[n][l].T.astype(BF16), ((0, _piece_rows(nc) - nc), (0, 0))))
            offs[n, l] = r
            r += _piece_rows(nc)
    for n in _OUT_W:
        for l in range(W[n].shape[0]):
            pieces.append(W[n][l].astype(BF16))
            offs[n, l] = r
            r += W[n].shape[1]
    rows = -(-r // _PACK_ROWS) * _PACK_ROWS
    packed = jnp.concatenate([jnp.swapaxes(w_gate_up, 1, 2).reshape(-1, D).astype(BF16), w_down.reshape(-1, D).astype(BF16)]
                             + pieces + [jnp.zeros((rows - r, D), BF16)], axis=0)
    first_rows = [(offs["fox_w_in", 0], offs["fox_w_in", 1] - offs["fox_w_in", 0]),
                  (offs["fox_w_out", 0], offs["fox_w_out", 1] - offs["fox_w_out", 0])]
    later_rows = [(0, FFN_ROWS), (offs["fox_w_in", 1], offs["fox_w_out", 0] - offs["fox_w_in", 1]),
                  (offs["fox_w_out", 1], r - offs["fox_w_out", 1])]
    wpk = _ag_weights(lax.dynamic_update_slice(lax.empty((4, rows, D), BF16), packed[None], (chip, 0, 0)), first_rows)

    def in_t(buf, n, l, npad):
        nc = W[n].shape[2]
        return jnp.concatenate([buf[q, offs[n, l]:offs[n, l] + nc] for q in range(4)] + [jnp.zeros((npad - 4 * nc, D), BF16)], 0)

    def out_w(buf, n, l):
        return jnp.concatenate([buf[q, offs[n, l]:offs[n, l] + W[n].shape[1]] for q in range(4)], axis=0)

    fox_in0, fox_out0 = in_t(wpk, "fox_w_in", 0, FOX_INP), out_w(wpk, "fox_w_out", 0)
    full = {}
    small = _pack([W[n] for n, _ in _SMALL_SHARDED], LANES, 8, F32)
    small_all = _gather8(small, reduce=False, name="gather_small").reshape(8, -1)
    for (n, ax), seg in zip(_SMALL_SHARDED, _unpack_cols(small_all, [W[n].shape for n, _ in _SMALL_SHARDED])):
        full[n] = jnp.concatenate([seg[2 * q] for q in range(4)], axis=ax)
    fox_in, full["fox_w_out"] = [fox_in0], [fox_out0]
    w_alpha2, conv_w = full["gla_w_alpha2"][0], full["gdn_conv_w"][0]

    h = jnp.concatenate([jnp.zeros((META0, D), F32), full["meta_tokens"], x[0]], axis=0)
    saved = []
    for i in range(DEPTH):
        kind, j = i % 3, i // 3
        y = _rms_fwd(h, norm_mix[i], name=f"norm_mix{i}")
        if kind == 0:
            proj = _mm(y, fox_in[j], tb=True, name=f"fox_in{j}")
            qa, ka, va = _fox_prep(proj, fox_b_f[j], fox_q_gain[j], fox_k_gain[j])
            if i == 0:
                o, og, lse, wpk = _fox_attn_fwd(qa, ka, va, proj, ag=(wpk, later_rows))
                wpk = _ag_forward(wpk, later_rows)
                fox_in += [in_t(wpk, "fox_w_in", l, FOX_INP) for l in range(1, fox_w_in.shape[0])]
                full["fox_w_out"] += [out_w(wpk, "fox_w_out", l) for l in range(1, fox_w_out.shape[0])]
                gla_in = [in_t(wpk, "gla_w_in", l, GLA_INP) for l in range(gla_w_in.shape[0])]
                gdn_in = [in_t(wpk, "gdn_w_in", l, GDN_INP) for l in range(gdn_w_in.shape[0])]
                for n in ("gla_w_out", "gdn_w_out"):
                    full[n] = [out_w(wpk, n, l) for l in range(W[n].shape[0])]
            else:
                o, og, lse = _fox_attn_fwd(qa, ka, va, proj)
            w_out, mix = full["fox_w_out"][j], (proj, qa, ka, va, o, lse)
        elif kind == 1:
            proj = _mm(y, gla_in[j], tb=True, name=f"gla_in{j}")
            o, og, states = _gla_fwd(proj, w_alpha2, gla_b_alpha[j], gla_o_gain[j])
            w_out, mix = full["gla_w_out"][j], (proj, o, states)
        else:
            proj = _mm(y, gdn_in[j], tb=True, name=f"gdn_in{j}")
            o, og, states = _gdn_fwd(proj, conv_w, gdn_a_log[j], gdn_dt_bias[j], gdn_o_gain[j])
            w_out, mix = full["gdn_w_out"][j], (proj, o, states)
        hm = _mm(og, w_out, add=h, name=f"mix_out{i}")
        yf = _rms_fwd(hm, norm_ffn[i], name=f"norm_ffn{i}")
        gate, up, act = _ffn_up(yf, wpk, i)
        hn = _ffn_down(act, wpk, i, hm)
        saved.append((h, y, mix, og, w_out, hm, yf, gate, up, act))
        h = hn
    dh, loss_tile = _loss_head(h, loss_target[0])

    G = {n: [None] * W[n].shape[0] for n in _WEIGHTS if n not in ("meta_tokens", "w_gate_up", "w_down") + _IN_W}
    GT = {}

    def grad_layout(ffn_layers, pieces):
        off, end = {}, 0
        for l in ffn_layers:
            off["gu", l] = end
            end += GU_ROWS
        for l in ffn_layers:
            off["down", l] = end
            end += DOWN_ROWS
        for n, l in pieces:
            off[n, l] = end
            end += _piece_rows(W[n].shape[2]) if n in _IN_W else W[n].shape[1]
        return off, end, -(-end // _PACK_ROWS) * _PACK_ROWS

    first_pieces = [("fox_w_in", 0)]
    later_pieces = [(n, l) for n in _IN_W + _OUT_W for l in range(W[n].shape[0]) if (n, l) not in first_pieces]
    layouts = [grad_layout([], first_pieces), grad_layout(list(range(DEPTH)), later_pieces)]
    gbuf = [jnp.zeros((4, lay[2], D), BF16) for lay in layouts]

    def with_pieces(buf, lay, pieces):
        off, end, total = lay
        blocks = []
        for q in range(4):
            parts = []
            for n, l in pieces:
                if n in _IN_W:
                    nc = W[n].shape[2]
                    parts.append(jnp.pad(GT[n, l][q * nc:(q + 1) * nc], ((0, _piece_rows(nc) - nc), (0, 0))))
                else:
                    nr = W[n].shape[1]
                    parts.append(G[n][l][q * nr:(q + 1) * nr])
            blocks.append(jnp.concatenate(parts + [jnp.zeros((total - end, D), BF16)], axis=0))
        return lax.dynamic_update_slice(buf, jnp.stack(blocks), (0, off[pieces[0]], 0))

    s_later = None
    for i in reversed(range(DEPTH)):
        kind, j = i % 3, i // 3
        h_in, y, mix, og, w_out, hm, yf, gate, up, act = saved[i]
        b = 1
        dg, du = _ffn_dact(dh, wpk, i, gate, up)
        gbuf[b] = _ffn_dw_down(act, dh, gbuf[b], i, layouts[b][0]["down", i])
        dyf = _ffn_dyf(dg, du, wpk, i)
        gbuf[b] = _ffn_dw_gu(dg, du, yf, gbuf[b], i, layouts[b][0]["gu", i] // GU_ROWS)
        dhm, dnf = _rms_bwd(hm, norm_ffn[i], dyf, dh, name=f"d_norm_ffn{i}")
        G["norm_ffn"][i] = dnf[0]
        dog = _mm(dhm, w_out, tb=True, name=f"d_og{i}")
        dw_out = _mm(og, dhm, ta=True, out_dtype=BF16, name=f"d_w_out{i}")
        if kind == 0:
            proj, qa, ka, va, o, lse = mix
            doa, q2, dgate = _fox_gate_bwd(dog, o, proj, lse, qa)
            G["fox_w_out"][j] = dw_out
            if i == 0:
                s_later = _rs_local(with_pieces(gbuf[1], layouts[1], later_pieces), "_later")
                dqn, dkn, dv, dct, recv_later = _fox_attn_bwd(q2, ka, va, doa, rs=s_later)
            else:
                dqn, dkn, dv, dct = _fox_attn_bwd(q2, ka, va, doa)
            dproj, dqg, dkg, dbf = _fox_prep_bwd(proj, fox_b_f[j], fox_q_gain[j], fox_k_gain[j], dqn, dkn, dv, dgate, dct)
            G["fox_q_gain"][j] = dqg.reshape(FOX_H, FOX_DH).sum(0)
            G["fox_k_gain"][j] = dkg.reshape(FOX_H, FOX_DH).sum(0)
            G["fox_b_f"][j] = dbf[0, :FOX_H]
            w_in, wname = fox_in[j], "fox_w_in"
        elif kind == 1:
            proj, o, states = mix
            dproj, dwa, dba, dogain = _gla_bwd(proj, w_alpha2, gla_b_alpha[j], gla_o_gain[j], o, states, dog)
            G["gla_w_out"][j] = dw_out
            G["gla_w_alpha2"][j] = dwa[:GLA_RANK]
            G["gla_b_alpha"][j] = dba[0]
            G["gla_o_gain"][j] = dogain[0]
            w_in, wname = gla_in[j], "gla_w_in"
        else:
            proj, o, states = mix
            dproj, dcw, dal, ddt, dogain = _gdn_bwd(proj, conv_w, gdn_a_log[j], gdn_dt_bias[j], gdn_o_gain[j], o, states, dog)
            G["gdn_w_out"][j] = dw_out
            G["gdn_conv_w"][j] = dcw[:4].reshape(4, 1, GDN_CONV)
            G["gdn_a_log"][j] = dal[0, :GDN_H]
            G["gdn_dt_bias"][j] = ddt[0, :GDN_H]
            G["gdn_o_gain"][j] = dogain[0]
            w_in, wname = gdn_in[j], "gdn_w_in"
        dy = _mm(dproj, w_in, name=f"d_y{i}")
        GT[wname, j] = _mm(dproj, y, ta=True, out_dtype=BF16, name=f"d_w_in{i}")
        dh, dnm = _rms_bwd(h_in, norm_mix[i], dy, dhm, name=f"d_norm_mix{i}")
        G["norm_mix"][i] = dnm[0]
    grad_x = dh[ROW0:][None]
    G = {n: (v if n in _OUT_W else jnp.stack(v)) for n, v in G.items()}
    G["meta_tokens"] = dh[META0:ROW0]

    s_first = _rs_local(with_pieces(gbuf[0], layouts[0], first_pieces), "_first")
    reduced = [_rs_finish(s_first, _scatter_chips(s_first, "_first"), "_first"), _rs_finish(s_later, recv_later, "_later")]

    def reduced_piece(n, l):
        b = 0 if (n, l) in first_pieces else 1
        start = layouts[b][0][n, l]
        return reduced[b][start:start + (W[n].shape[2] if n in _IN_W else W[n].shape[1])]

    grads = {}
    for n in _IN_W:
        grads[n] = jnp.stack([reduced_piece(n, l).T for l in range(W[n].shape[0])])
    for n in _OUT_W:
        grads[n] = jnp.stack([reduced_piece(n, l) for l in range(W[n].shape[0])])
    small_names = [n for n, _ in _SMALL_SHARDED] + list(_REPLICATED)
    small_g = _pack([G[n] for n in small_names] + [loss_tile[0, 0:1]], LANES, 8, F32)
    small_sum = _gather8(small_g, reduce=True, name="allreduce_small").reshape(-1)
    small_shapes = [G[n].shape for n in small_names] + [(1,)]
    small_vals = _unpack(small_sum, small_shapes)
    loss = small_vals[-1][0]
    for n, val in zip(small_names, small_vals[:-1]):
        grads[n] = val
    for n, ax in _SMALL_SHARDED:
        sz = W[n].shape[ax]
        grads[n] = lax.dynamic_slice_in_dim(grads[n], chip * sz, sz, axis=ax)

    delta, new_m, new_v = {}, {}, {}
    for n, key, tr_ in (("w_gate_up", "gu", True), ("w_down", "down", False)):
        grads[n], delta[n], new_m[n], new_v[n] = _adamw_packed(
            W[n], reduced[1], reduced[1], M[n], V[n], row0=layouts[1][0][key, 0], row_off=layouts[1][0][key, 1],
            transposed=tr_, name=f"adamw_{n}")
    for n in _IN_W + _OUT_W:
        delta[n], new_m[n], new_v[n] = _adamw(W[n], grads[n], M[n], V[n], name=f"adamw_{n}")
    tiny = [n for n in _WEIGHTS if n not in dict(_BIG)]
    packs = [_pack([T[n] for n in tiny], LANES, 8, F32) for T in (W, grads, M, V)]
    outs = _adamw(*packs, name="adamw_small")
    shapes = [W[n].shape for n in tiny]
    for dst, o in zip((delta, new_m, new_v), outs):
        for n, val in zip(tiny, _unpack(o.reshape(-1), shapes)):
            dst[n] = val
    return (loss, grad_x, *[grads[n] for n in _WEIGHTS], *[delta[n] for n in _WEIGHTS],
            *[new_m[n] for n in _WEIGHTS], *[new_v[n] for n in _WEIGHTS])
```

```python
import functools

import jax
import jax.numpy as jnp
from jax import lax
from jax.experimental import pallas as pl
from jax.experimental.pallas import tpu as pltpu

F32, BF16 = jnp.float32, jnp.bfloat16
D = 1024
N_META = 16
ROW0 = 128
META0 = ROW0 - N_META
EPS = 1e-6
LANES = 128
VMEM_LIMIT = 56 * 1024 * 1024

FOX_H, FOX_DH = 16, 64
FOX_INP = 4224
GLA_H, GLA_DK, GLA_DV, GLA_RANK = 4, 128, 256, 16
GLA_QK, GLA_V = 512, 1024
GLA_INP = 3200
GLA_NORM = 16.0
GDN_H, GDN_DK, GDN_DV = 8, 128, 128
GDN_CONV = 3072
GDN_INP = 4224
CHUNK = 64
D_FF = 2816
DEPTH = 4

ADAM_LR, ADAM_B1, ADAM_B2, ADAM_EPS, ADAM_WD, ADAM_STEP = 0.001, 0.9, 0.999, 1e-08, 0.01, 10

MESH = pl.DeviceIdType.MESH
ANY = pl.BlockSpec(memory_space=pl.ANY)
VM = pl.BlockSpec(memory_space=pltpu.VMEM)


def _params(sem=None, **kw):
    if sem is not None:
        kw["dimension_semantics"] = sem
    return pltpu.CompilerParams(vmem_limit_bytes=VMEM_LIMIT, **kw)


def _tile(n, cap, mult=LANES):
    best = None
    for t in range(mult, min(n, cap) + 1, mult):
        if n % t == 0:
            best = t
    return best if best is not None else n


def nn(a, b, **kw):
    return jnp.dot(a, b, preferred_element_type=F32, **kw)


def nt(a, b, **kw):
    return lax.dot_general(a, b, (((1,), (1,)), ((), ())), preferred_element_type=F32, **kw)


def tn(a, b, **kw):
    return lax.dot_general(a, b, (((0,), (0,)), ((), ())), preferred_element_type=F32, **kw)


def _split3(x):
    hi = x.astype(BF16)
    r = x - hi.astype(F32)
    mid = r.astype(BF16)
    lo = (r - mid.astype(F32)).astype(BF16)
    return hi, mid, lo


def _sel_l(sel, x):
    a, b, c = _split3(x)
    return nn(sel, a) + nn(sel, b) + nn(sel, c)


def _sel_r(x, sel):
    a, b, c = _split3(x)
    return nn(a, sel) + nn(b, sel) + nn(c, sel)


def _sel_r2(x, sel):
    a = x.astype(BF16)
    return nn(a, sel) + nn((x - a.astype(F32)).astype(BF16), sel)


def _iota(shape, dim):
    return lax.broadcasted_iota(jnp.int32, shape, dim)


def _tri(n, upper=False, strict=False):
    i, j = _iota((n, n), 0), _iota((n, n), 1)
    if upper:
        m = (j > i) if strict else (j >= i)
    else:
        m = (j < i) if strict else (j <= i)
    return m


def _sigmoid(x):
    return 1.0 / (1.0 + jnp.exp(-x))


def _log_sigmoid(x):
    return jnp.minimum(x, 0.0) - jnp.log(1.0 + jnp.exp(-jnp.abs(x)))


def _softplus(x):
    return jnp.maximum(x, 0.0) + jnp.log(1.0 + jnp.exp(-jnp.abs(x)))


def _silu(x):
    return x * _sigmoid(x)


def _dsilu(x):
    s = _sigmoid(x)
    return s * (1.0 + x * (1.0 - s))


def _rms(x, g):
    return (x * lax.rsqrt(jnp.mean(x * x, axis=-1, keepdims=True) + EPS) * g).astype(BF16)


def _rms_grad(x, g, dy):
    r = lax.rsqrt(jnp.mean(x * x, axis=-1, keepdims=True) + EPS)
    u = dy * g
    return r * u - x * (r * r * r) * jnp.mean(x * u, axis=-1, keepdims=True), jnp.sum(dy * x * r, axis=0, keepdims=True)


def _mm(a, b, *, ta=False, tb=False, add=None, norm=None, rms_bwd=None, out_dtype=F32, name):
    m, k = (a.shape[1], a.shape[0]) if ta else a.shape
    n = b.shape[0] if tb else b.shape[1]
    assert k == (b.shape[1] if tb else b.shape[0])
    rows_whole = norm is not None or rms_bwd is not None
    tm, tn_, tk = _tile(m, 704 if rows_whole else 1408, LANES if ta else 16), _tile(n, 1408), _tile(k, 1408)
    nk = k // tk
    assert not rows_whole or tn_ == n

    def body(*refs):
        refs = list(refs)
        a_ref, b_ref = refs[:2]
        extra = refs[2:-1]
        acc = refs[-1]
        i, kk = pl.program_id(0), pl.program_id(2)

        @pl.when(kk == 0)
        def _():
            acc[...] = jnp.zeros_like(acc)

        av, bv = a_ref[...].astype(BF16), b_ref[...].astype(BF16)
        dims = (((0,) if ta else (1,), (1,) if tb else (0,)), ((), ()))
        acc[...] += lax.dot_general(av, bv, dims, preferred_element_type=F32)

        @pl.when(kk == nk - 1)
        def _():
            r = acc[...]
            if rms_bwd is not None:
                h_ref, g_ref, dres_ref, o_ref, dg_ref = extra
                dx, dgain = _rms_grad(h_ref[...], g_ref[...], r)
                o_ref[...] = dres_ref[...] + dx

                @pl.when(i == 0)
                def _():
                    dg_ref[...] = jnp.zeros_like(dg_ref)

                dg_ref[...] += dgain
                return
            if add is not None:
                r = r + extra[0][...].astype(F32)
            if norm is not None:
                g_ref, o_ref, y_ref = extra[-3:]
                y_ref[...] = _rms(r, g_ref[...])
            else:
                o_ref = extra[-1]
            o_ref[...] = r.astype(out_dtype)

    a_spec = pl.BlockSpec((tk, tm), lambda i, j, q: (q, i)) if ta else pl.BlockSpec((tm, tk), lambda i, j, q: (i, q))
    b_spec = pl.BlockSpec((tn_, tk), lambda i, j, q: (j, q)) if tb else pl.BlockSpec((tk, tn_), lambda i, j, q: (q, j))
    o_spec = pl.BlockSpec((tm, tn_), lambda i, j, q: (i, j))
    g_spec = pl.BlockSpec((1, n), lambda i, j, q: (0, 0))
    ins, specs = [a, b], [a_spec, b_spec]
    out_specs, out_shape = o_spec, jax.ShapeDtypeStruct((m, n), out_dtype)
    sem = ("parallel", "parallel", "arbitrary")
    if rms_bwd is not None:
        ins += [rms_bwd[0], rms_bwd[1].reshape(1, n), rms_bwd[2]]
        specs += [o_spec, g_spec, o_spec]
        out_specs, out_shape = [o_spec, g_spec], [jax.ShapeDtypeStruct((m, n), F32), jax.ShapeDtypeStruct((1, n), F32)]
        sem = ("arbitrary", "arbitrary", "arbitrary")
    else:
        if add is not None:
            ins.append(add)
            specs.append(o_spec)
        if norm is not None:
            ins.append(norm.reshape(1, n))
            specs.append(g_spec)
            out_specs, out_shape = [o_spec, o_spec], [out_shape, jax.ShapeDtypeStruct((m, n), BF16)]
    return pl.pallas_call(
        body, name=name, grid=(m // tm, n // tn_, nk), in_specs=specs, out_specs=out_specs, out_shape=out_shape,
        scratch_shapes=[pltpu.VMEM((tm, tn_), F32)], compiler_params=_params(sem),
    )(*ins)


def _rms_fwd(h, g, *, name):
    lp = h.shape[0]
    tr = _tile(lp, 512)

    def body(h_ref, g_ref, y_ref):
        x = h_ref[...]
        r = lax.rsqrt(jnp.mean(x * x, axis=-1, keepdims=True) + EPS)
        y_ref[...] = (x * r * g_ref[...]).astype(BF16)

    return pl.pallas_call(
        body, name=name, grid=(lp // tr,),
        in_specs=[pl.BlockSpec((tr, D), lambda i: (i, 0)), pl.BlockSpec((1, D), lambda i: (0, 0))],
        out_specs=pl.BlockSpec((tr, D), lambda i: (i, 0)),
        out_shape=jax.ShapeDtypeStruct((lp, D), BF16), compiler_params=_params(("parallel",)),
    )(h, g.reshape(1, D))


def _rms_bwd(h, g, dy, dres, *, name):
    lp = h.shape[0]
    tr = _tile(lp, 512)

    def body(h_ref, g_ref, dy_ref, dr_ref, dh_ref, dg_ref):
        @pl.when(pl.program_id(0) == 0)
        def _():
            dg_ref[...] = jnp.zeros_like(dg_ref)

        x, dyv = h_ref[...], dy_ref[...].astype(F32)
        r = lax.rsqrt(jnp.mean(x * x, axis=-1, keepdims=True) + EPS)
        u = dyv * g_ref[...]
        dx = r * u - x * (r * r * r) * jnp.mean(x * u, axis=-1, keepdims=True)
        dh_ref[...] = dr_ref[...] + dx
        dg_ref[...] += jnp.sum(dyv * x * r, axis=0, keepdims=True)

    return pl.pallas_call(
        body, name=name, grid=(lp // tr,),
        in_specs=[pl.BlockSpec((tr, D), lambda i: (i, 0)), pl.BlockSpec((1, D), lambda i: (0, 0)),
                  pl.BlockSpec((tr, D), lambda i: (i, 0)), pl.BlockSpec((tr, D), lambda i: (i, 0))],
        out_specs=[pl.BlockSpec((tr, D), lambda i: (i, 0)), pl.BlockSpec((1, D), lambda i: (0, 0))],
        out_shape=[jax.ShapeDtypeStruct((lp, D), F32), jax.ShapeDtypeStruct((1, D), F32)],
        compiler_params=_params(("arbitrary",)),
    )(h, g.reshape(1, D), dy, dres)


GU_ROWS, DOWN_ROWS = 1408, 704
OFF_GU, OFF_DOWN = 0, DEPTH * GU_ROWS
FFN_ROWS = DEPTH * (GU_ROWS + DOWN_ROWS)
FFN_TM = 704


def _gu_spec(fn):
    return pl.BlockSpec((None, GU_ROWS, D), fn)


def _down_spec(fn):
    return pl.BlockSpec((None, DOWN_ROWS, D), fn)


def _down_pair(w0_ref, w1_ref):
    return jnp.concatenate([w0_ref[...], w1_ref[...]], axis=0)


def _ffn_up(yf, wpk, layer):
    lp = yf.shape[0]
    tm = _tile(lp, FFN_TM, 16)

    def body(y_ref, wg_ref, wu_ref, g_ref, u_ref, a_ref):
        y = y_ref[...]
        g, u = nt(y, wg_ref[...]), nt(y, wu_ref[...])
        g_ref[...] = g.astype(BF16)
        u_ref[...] = u.astype(BF16)
        a_ref[...] = (_silu(g) * u).astype(BF16)

    o = pl.BlockSpec((tm, GU_ROWS), lambda i, j: (i, j))
    return pl.pallas_call(
        body, name=f"ffn_up{layer}", grid=(lp // tm, 2),
        in_specs=[pl.BlockSpec((tm, D), lambda i, j: (i, 0)), _gu_spec(lambda i, j: (j, OFF_GU // GU_ROWS + layer, 0)),
                  _gu_spec(lambda i, j: (2 + j, OFF_GU // GU_ROWS + layer, 0))],
        out_specs=[o, o, o], out_shape=[jax.ShapeDtypeStruct((lp, D_FF), BF16)] * 3,
        compiler_params=_params(("parallel", "parallel")),
    )(yf, wpk, wpk)


def _ffn_down(act, wpk, layer, res, norm):
    lp = act.shape[0]
    tm = _tile(lp, FFN_TM, 16)

    def body(a_ref, w0_ref, w1_ref, r_ref, g_ref, o_ref, y_ref, acc):
        kk = pl.program_id(1)

        @pl.when(kk == 0)
        def _():
            acc[...] = r_ref[...]

        acc[...] += nn(a_ref[...], _down_pair(w0_ref, w1_ref))

        @pl.when(kk == 1)
        def _():
            o_ref[...] = acc[...]
            y_ref[...] = _rms(acc[...], g_ref[...])

    o = pl.BlockSpec((tm, D), lambda i, kk: (i, 0))
    blk = OFF_DOWN // DOWN_ROWS + layer
    return pl.pallas_call(
        body, name=f"ffn_down{layer}", grid=(lp // tm, 2),
        in_specs=[pl.BlockSpec((tm, GU_ROWS), lambda i, kk: (i, kk)), _down_spec(lambda i, kk: (2 * kk, blk, 0)),
                  _down_spec(lambda i, kk: (2 * kk + 1, blk, 0)), o, pl.BlockSpec((1, D), lambda i, kk: (0, 0))],
        out_specs=[o, o], out_shape=[jax.ShapeDtypeStruct((lp, D), F32), jax.ShapeDtypeStruct((lp, D), BF16)],
        scratch_shapes=[pltpu.VMEM((tm, D), F32)], compiler_params=_params(("parallel", "arbitrary")),
    )(act, wpk, wpk, res, norm.reshape(1, D))


def _ffn_dact(dh, wpk, layer, gate, up):
    lp = dh.shape[0]
    tm = _tile(lp, FFN_TM, 16)

    def body(d_ref, w0_ref, w1_ref, g_ref, u_ref, dg_ref, du_ref):
        da = nt(d_ref[...].astype(BF16), _down_pair(w0_ref, w1_ref))
        g, u = g_ref[...].astype(F32), u_ref[...].astype(F32)
        dg_ref[...] = (da * u * _dsilu(g)).astype(BF16)
        du_ref[...] = (da * _silu(g)).astype(BF16)

    o = pl.BlockSpec((tm, GU_ROWS), lambda i, j: (i, j))
    blk = OFF_DOWN // DOWN_ROWS + layer
    return pl.pallas_call(
        body, name=f"d_act{layer}", grid=(lp // tm, 2),
        in_specs=[pl.BlockSpec((tm, D), lambda i, j: (i, 0)), _down_spec(lambda i, j: (2 * j, blk, 0)),
                  _down_spec(lambda i, j: (2 * j + 1, blk, 0)), o, o],
        out_specs=[o, o], out_shape=[jax.ShapeDtypeStruct((lp, D_FF), BF16)] * 2,
        compiler_params=_params(("parallel", "parallel")),
    )(dh, wpk, wpk, gate, up)


def _ffn_dyf(dg, du, wpk, layer, hm, norm, dres):
    lp = dg.shape[0]
    tm = _tile(lp, FFN_TM, 16)

    def body(dg_ref, du_ref, w_ref, h_ref, g_ref, dres_ref, o_ref, dgain_ref, acc):
        i, kk = pl.program_id(0), pl.program_id(1)

        @pl.when(kk == 0)
        def _():
            acc[...] = jnp.zeros_like(acc)

        @pl.when(kk < 2)
        def _():
            acc[...] += nn(dg_ref[...], w_ref[...])

        @pl.when(kk >= 2)
        def _():
            acc[...] += nn(du_ref[...], w_ref[...])

        @pl.when(kk == 3)
        def _():
            dx, dgain = _rms_grad(h_ref[...], g_ref[...], acc[...])
            o_ref[...] = dres_ref[...] + dx

            @pl.when(i == 0)
            def _():
                dgain_ref[...] = jnp.zeros_like(dgain_ref)

            dgain_ref[...] += dgain

    o = pl.BlockSpec((tm, D), lambda i, kk: (i, 0))
    row = pl.BlockSpec((1, D), lambda i, kk: (0, 0))
    return pl.pallas_call(
        body, name=f"d_yf{layer}", grid=(lp // tm, 4),
        in_specs=[pl.BlockSpec((tm, GU_ROWS), lambda i, kk: (i, jnp.minimum(kk, 1))),
                  pl.BlockSpec((tm, GU_ROWS), lambda i, kk: (i, jnp.maximum(kk - 2, 0))),
                  _gu_spec(lambda i, kk: (kk, OFF_GU // GU_ROWS + layer, 0)), o, row, o],
        out_specs=[o, row], out_shape=[jax.ShapeDtypeStruct((lp, D), F32), jax.ShapeDtypeStruct((1, D), F32)],
        scratch_shapes=[pltpu.VMEM((tm, D), F32)], compiler_params=_params(("arbitrary", "arbitrary")),
    )(dg, du, wpk, hm, norm.reshape(1, D), dres)


def _ffn_dw_down(act, dh, gpk, layer, row):
    lp = act.shape[0]
    tk = _tile(lp, 1408, 16)
    nk = lp // tk

    def body(a_ref, d_ref, g_in, g_out, acc, stage, sems):
        jp, kk = pl.program_id(0), pl.program_id(1)

        @pl.when(kk == 0)
        def _():
            acc[...] = jnp.zeros_like(acc)

        acc[...] += tn(a_ref[...], d_ref[...].astype(BF16))

        @pl.when(kk == nk - 1)
        def _():
            stage[...] = acc[...].astype(BF16)
            copies = [pltpu.make_async_copy(stage.at[pl.ds(hf * DOWN_ROWS, DOWN_ROWS), :],
                                            g_out.at[2 * jp + hf, pl.ds(row, DOWN_ROWS), :], sems.at[hf]) for hf in range(2)]
            for cp in copies:
                cp.start()
            for cp in copies:
                cp.wait()

    return pl.pallas_call(
        body, name=f"d_w_down{layer}", grid=(2, nk),
        in_specs=[pl.BlockSpec((tk, GU_ROWS), lambda jp, kk: (kk, jp)), pl.BlockSpec((tk, D), lambda jp, kk: (kk, 0)), ANY],
        out_specs=ANY, out_shape=jax.ShapeDtypeStruct(gpk.shape, gpk.dtype),
        scratch_shapes=[pltpu.VMEM((GU_ROWS, D), F32), pltpu.VMEM((GU_ROWS, D), BF16), pltpu.SemaphoreType.DMA((2,))],
        input_output_aliases={2: 0}, compiler_params=_params(("arbitrary", "arbitrary")),
    )(act, dh, gpk)


def _ffn_dw_gu(dg, du, yf, gpk, layer, blk):
    lp = dg.shape[0]
    tk = _tile(lp, 1408, 16)
    nk = lp // tk

    def body(dg_ref, du_ref, y_ref, g_in, o_ref, acc):
        c, kk = pl.program_id(0), pl.program_id(1)

        @pl.when(kk == 0)
        def _():
            acc[...] = jnp.zeros_like(acc)

        @pl.when(c < 2)
        def _():
            acc[...] += tn(dg_ref[...], y_ref[...])

        @pl.when(c >= 2)
        def _():
            acc[...] += tn(du_ref[...], y_ref[...])

        @pl.when(kk == nk - 1)
        def _():
            o_ref[...] = acc[...].astype(BF16)

    return pl.pallas_call(
        body, name=f"d_w_gate_up{layer}", grid=(4, nk),
        in_specs=[pl.BlockSpec((tk, GU_ROWS), lambda c, kk: (kk, jnp.minimum(c, 1))),
                  pl.BlockSpec((tk, GU_ROWS), lambda c, kk: (kk, jnp.maximum(c - 2, 0))),
                  pl.BlockSpec((tk, D), lambda c, kk: (kk, 0)), ANY],
        out_specs=_gu_spec(lambda c, kk: (c, blk, 0)),
        out_shape=jax.ShapeDtypeStruct(gpk.shape, gpk.dtype),
        scratch_shapes=[pltpu.VMEM((GU_ROWS, D), F32)], input_output_aliases={3: 0},
        compiler_params=_params(("parallel", "arbitrary")),
    )(dg, du, yf, gpk)


def _loss_head(h, target):
    lp = h.shape[0]
    nb = lp // ROW0

    def body(h_ref, t_ref, dh_ref, l_ref):
        i = pl.program_id(0)

        @pl.when(i == 0)
        def _():
            l_ref[...] = jnp.zeros_like(l_ref)
            dh_ref[...] = jnp.zeros_like(dh_ref)

        @pl.when(i > 0)
        def _():
            err = h_ref[...] - t_ref[...]
            dh_ref[...] = err * (1.0 / D)
            l_ref[...] += jnp.sum(err * err) * (0.5 / D)

    return pl.pallas_call(
        body, name="loss_head", grid=(nb,),
        in_specs=[pl.BlockSpec((ROW0, D), lambda i: (i, 0)), pl.BlockSpec((ROW0, D), lambda i: (jnp.maximum(i - 1, 0), 0))],
        out_specs=[pl.BlockSpec((ROW0, D), lambda i: (i, 0)), pl.BlockSpec((8, LANES), lambda i: (0, 0))],
        out_shape=[jax.ShapeDtypeStruct((lp, D), F32), jax.ShapeDtypeStruct((8, LANES), F32)],
        compiler_params=_params(("arbitrary",)),
    )(h, target)


def _adamw(w, g, m, v, *, name):
    if w.ndim == 2:
        w, g, m, v = (t[None] for t in (w, g, m, v))
        return tuple(o[0] for o in _adamw(w, g, m, v, name=name))
    nl, r, c = w.shape
    tr = _tile(r, max(8, (1 << 19) // c), 8)

    def body(w_ref, g_ref, m_ref, v_ref, d_ref, nm_ref, nv_ref):
        d_ref[...], nm_ref[...], nv_ref[...] = _adam_math(w_ref[...], g_ref[...], m_ref[...], v_ref[...])

    spec = pl.BlockSpec((1, tr, c), lambda l, i: (l, i, 0))
    return tuple(pl.pallas_call(
        body, name=name, grid=(nl, r // tr), in_specs=[spec] * 4, out_specs=[spec] * 3,
        out_shape=[jax.ShapeDtypeStruct(w.shape, F32)] * 3, compiler_params=_params(("parallel", "parallel")),
    )(w, g, m, v))


def _adam_math(w, g, m, v):
    nm = ADAM_B1 * m + (1.0 - ADAM_B1) * g
    nv = ADAM_B2 * v + (1.0 - ADAM_B2) * (g * g)
    m_hat = nm / (1.0 - ADAM_B1 ** ADAM_STEP)
    v_hat = nv / (1.0 - ADAM_B2 ** ADAM_STEP)
    return -ADAM_LR * (m_hat / (jnp.sqrt(v_hat) + ADAM_EPS) + ADAM_WD * w), nm, nv


def _adamw_packed(w, gred0, gred, m, v, *, row0, row_off, transposed, name):
    nl, a, b = w.shape
    nr = b if transposed else a
    later = lambda l: row_off // nr + jnp.maximum(l - 1, 0)
    if transposed:
        ta = _tile(a, 256)
        wspec = pl.BlockSpec((1, ta, b), lambda l, r: (l, r, 0))
        g0spec = pl.BlockSpec((b, ta), lambda l, r: (row0 // nr, r))
        gspec = pl.BlockSpec((b, ta), lambda l, r: (later(l), r))
        grid = (nl, a // ta)
    else:
        wspec = pl.BlockSpec((1, a, b), lambda l, r: (l, 0, 0))
        g0spec = pl.BlockSpec((a, b), lambda l, r: (row0 // nr, 0))
        gspec = pl.BlockSpec((a, b), lambda l, r: (later(l), 0))
        grid = (nl, 1)

    def body(w_ref, g0_ref, g_ref, m_ref, v_ref, go_ref, d_ref, nm_ref, nv_ref):
        g = jnp.where(pl.program_id(0) == 0, g0_ref[...], g_ref[...])
        g = g.T if transposed else g
        d, nm, nv = _adam_math(w_ref[0], g, m_ref[0], v_ref[0])
        go_ref[0], d_ref[0], nm_ref[0], nv_ref[0] = g, d, nm, nv

    return pl.pallas_call(
        body, name=name, grid=grid, in_specs=[wspec, g0spec, gspec, wspec, wspec], out_specs=[wspec] * 4,
        out_shape=[jax.ShapeDtypeStruct(w.shape, F32)] * 4, compiler_params=_params(("parallel", "parallel")),
    )(w, gred0, gred, m, v)


FOX_AUG = FOX_H * LANES
L_C = 64
L_K = 67
L_LSE = 70
PAD_KEY = -30000.0
FOX_TQ = 384


def _head_sel(n_heads, width, lanes=LANES):
    r, c = _iota((n_heads * width, lanes), 0), _iota((n_heads * width, lanes), 1)
    down = (r // width == c).astype(BF16)
    r2, c2 = _iota((lanes, n_heads * width), 0), _iota((lanes, n_heads * width), 1)
    up = (c2 // width == r2).astype(BF16)
    return down, up


def _place(lane0):
    r, c = _iota((LANES, FOX_AUG), 0), _iota((LANES, FOX_AUG), 1)
    return [((c // LANES == r) & (c % LANES == lane0 + m)).astype(BF16) for m in range(3)]


def _placed(x, lane0):
    pcs = _split3(x)
    mats = _place(lane0)
    return nn(pcs[0], mats[0]) + nn(pcs[1], mats[1]) + nn(pcs[2], mats[2])


def _ones_at(rows, lanes):
    c = _iota((rows, FOX_AUG), 1) % LANES
    m = c == lanes[0]
    for l in lanes[1:]:
        m = m | (c == l)
    return m.astype(F32)


def _spread(x, extras, out_ref):
    rows = x.shape[0]
    left = _iota((rows, LANES), 1) < FOX_DH
    for p in range(FOX_H // 2):
        slab = x[:, p * LANES:(p + 1) * LANES]
        a = jnp.where(left, slab, extras[:, 2 * p * LANES:(2 * p + 1) * LANES])
        b = jnp.where(left, pltpu.roll(slab, FOX_DH, 1), extras[:, (2 * p + 1) * LANES:(2 * p + 2) * LANES])
        out_ref[:, 2 * p * LANES:(2 * p + 1) * LANES] = a.astype(BF16)
        out_ref[:, (2 * p + 1) * LANES:(2 * p + 2) * LANES] = b.astype(BF16)


def _fox_prep(proj, b_f, q_gain, k_gain):
    lp = proj.shape[0]
    nb = lp // LANES

    def body(p_ref, bf_ref, qg_ref, kg_ref, q_ref, k_ref, v_ref, carry):
        i = pl.program_id(0)

        @pl.when(i == 0)
        def _():
            carry[...] = jnp.zeros_like(carry)

        down, up = _head_sel(FOX_H, FOX_DH)

        def normed(x, gain):
            ms = _sel_r2(x * x, down) * (1.0 / FOX_DH)
            r = _sel_r2(lax.rsqrt(ms + EPS), up)
            return x * r * gain

        lane = _iota((LANES, LANES), 1)
        lf = jnp.where(lane < FOX_H, _log_sigmoid(p_ref[:, 4 * D:4 * D + LANES] + bf_ref[...]), 0.0)
        c = _sel_l(_tri(LANES).astype(BF16), lf) + carry[0:1, :]
        carry[...] = jnp.broadcast_to(c[LANES - 1:LANES, :], carry.shape)
        q_extra = _placed(c, L_C) + _ones_at(LANES, (L_K, L_K + 1, L_K + 2))
        row = i * LANES + _iota((LANES, FOX_AUG), 0)
        lane_a = _iota((LANES, FOX_AUG), 1) % LANES
        k_extra = -_placed(c, L_K) + _ones_at(LANES, (L_C, L_C + 1, L_C + 2, L_LSE, L_LSE + 1, L_LSE + 2))
        pad_val = jnp.where(lane_a == L_K, PAD_KEY, 0.0)
        k_extra = jnp.where((row < META0) & (lane_a >= L_K) & (lane_a < L_K + 3), pad_val, k_extra)
        v_extra = _ones_at(LANES, (L_C, L_C + 1, L_C + 2))
        _spread(normed(p_ref[:, 0:D], qg_ref[...]) * (FOX_DH ** -0.5), q_extra, q_ref)
        _spread(normed(p_ref[:, D:2 * D], kg_ref[...]), k_extra, k_ref)
        _spread(p_ref[:, 2 * D:3 * D], v_extra, v_ref)

    row = pl.BlockSpec((1, D), lambda i: (0, 0))
    aug = pl.BlockSpec((LANES, FOX_AUG), lambda i: (i, 0))
    return pl.pallas_call(
        body, name="fox_prep", grid=(nb,),
        in_specs=[pl.BlockSpec((LANES, FOX_INP), lambda i: (i, 0)), pl.BlockSpec((1, LANES), lambda i: (0, 0)), row, row],
        out_specs=[aug] * 3, out_shape=[jax.ShapeDtypeStruct((lp, FOX_AUG), BF16)] * 3,
        scratch_shapes=[pltpu.VMEM((8, LANES), F32)],
        compiler_params=_params(("arbitrary",)),
    )(proj, jnp.pad(b_f, (0, LANES - FOX_H)).reshape(1, LANES), jnp.tile(q_gain, FOX_H).reshape(1, D),
      jnp.tile(k_gain, FOX_H).reshape(1, D))


def _fox_attn_fwd(qa, ka, va, proj, ag=None):
    lp = qa.shape[0]
    tq = _tile(lp, FOX_TQ)
    nq = lp // tq
    npair = FOX_H // 2

    def body(q_ref, k_ref, v_ref, gate_ref, *rest):
        if ag is None:
            o_ref, og_ref, lse_ref = rest
        else:
            _, o_ref, og_ref, lse_ref, w_out, send_sems, recv_sems = rest
            copies = _AgCopies(w_out, ag[1], send_sems, recv_sems)

            @pl.when((pl.program_id(0) == 0) & (pl.program_id(1) == 0))
            def _():
                for r, k in copies.pairs():
                    copies.ici(r, k).start()

        i = pl.program_id(1)
        causal = _iota((tq, tq), 1) <= _iota((tq, tq), 0)
        qs = [q_ref[:, hh * LANES:(hh + 1) * LANES] for hh in range(2)]

        def block(j, carry, diag):
            off = pl.multiple_of(j * tq, tq)
            out = []
            for hh in range(2):
                m, acc = carry[hh]
                k = k_ref[pl.ds(off, tq), hh * LANES:(hh + 1) * LANES]
                v = v_ref[pl.ds(off, tq), hh * LANES:(hh + 1) * LANES]
                s = nt(qs[hh], k)
                if diag:
                    s = jnp.where(causal, s, -1e30)
                m2 = jnp.maximum(m, jnp.max(s, axis=-1, keepdims=True))
                p = jnp.exp(s - m2)
                p_hi = p.astype(BF16)
                p_lo = (p - p_hi.astype(F32)).astype(BF16)
                out.append((m2, jnp.exp(m - m2) * acc + nn(p_hi, v) + nn(p_lo, v)))
            return tuple(out)

        init = tuple((jnp.full((tq, 1), -1e30, F32), jnp.zeros((tq, LANES), F32)) for _ in range(2))
        carry = lax.fori_loop(0, i // 2, lambda j, c: block(2 * j + 1, block(2 * j, c, False), False), init)
        carry = lax.cond(i % 2 == 1, lambda c: block(i - 1, c, False), lambda c: c, carry)
        carry = block(i, carry, True)
        outs, lses = [], []
        for hh in range(2):
            m, acc = carry[hh]
            l = acc[:, L_C:L_C + 1]
            outs.append(acc / l)
            lses.append(jnp.broadcast_to(m + jnp.log(l), (tq, LANES)))
        left = _iota((tq, LANES), 1) < FOX_DH
        o = jnp.where(left, outs[0], pltpu.roll(outs[1], FOX_DH, 1))
        o_ref[...] = o
        og_ref[...] = (o * _sigmoid(gate_ref[...])).astype(BF16)
        lse_ref[...] = jnp.where(left, lses[0], lses[1])

        if ag is not None:
            @pl.when((pl.program_id(0) == npair - 1) & (pl.program_id(1) == nq - 1))
            def _():
                for r, k in copies.pairs():
                    copies.ici_arrival(r, k).wait_recv()
                for r, k in copies.pairs():
                    copies.ici(r, k).wait_send()

    qspec = pl.BlockSpec((tq, 2 * LANES), lambda p, i: (i, p))
    kspec = pl.BlockSpec((lp, 2 * LANES), lambda p, i: (0, p))
    ospec = pl.BlockSpec((tq, LANES), lambda p, i: (i, p))
    ins, in_specs = [qa, ka, va, proj], [qspec, kspec, kspec, pl.BlockSpec((tq, LANES), lambda p, i: (i, 3 * D // LANES + p))]
    out_specs = [ospec] * 3
    out_shape = [jax.ShapeDtypeStruct((lp, D), F32), jax.ShapeDtypeStruct((lp, D), BF16), jax.ShapeDtypeStruct((lp, D), F32)]
    if ag is None:
        return pl.pallas_call(body, name="fox_attn_fwd", grid=(npair, nq), in_specs=in_specs, out_specs=out_specs,
                              out_shape=out_shape, compiler_params=_params(("parallel", "arbitrary")))(*ins)
    n = 3 * len(ag[1])
    return pl.pallas_call(
        body, name="fox_attn_fwd_ag", grid=(npair, nq), in_specs=in_specs + [ANY], out_specs=out_specs + [ANY],
        out_shape=out_shape + [jax.ShapeDtypeStruct(ag[0].shape, ag[0].dtype)],
        scratch_shapes=[pltpu.SemaphoreType.DMA((n,))] * 2, input_output_aliases={4: 3},
        compiler_params=_params(("arbitrary", "arbitrary")),
    )(*ins, ag[0])


def _fox_gate_bwd(dog, o, proj, lse, qa):
    lp = o.shape[0]
    tr = LANES

    def body(d_ref, o_ref, g_ref, lse_ref, q_ref, do_ref, q2_ref, dgate_ref):
        down, _ = _head_sel(FOX_H, FOX_DH)
        sg = _sigmoid(g_ref[...])
        dv, ov = d_ref[...], o_ref[...]
        do = (dv * sg).astype(BF16).astype(F32)
        dgate_ref[...] = dv * ov * sg * (1.0 - sg)
        delta = _sel_r(do * ov, down)
        _spread(do, -_placed(delta, L_C), do_ref)
        r_, c_ = _iota((D, LANES), 0), _iota((D, LANES), 1)
        lse_c = _sel_r(lse_ref[...], (r_ == c_ * FOX_DH).astype(BF16))
        q2_ref[...] = (q_ref[...].astype(F32) - _placed(lse_c, L_LSE)).astype(BF16)

    spec = pl.BlockSpec((tr, D), lambda i: (i, 0))
    aug = pl.BlockSpec((tr, FOX_AUG), lambda i: (i, 0))
    return pl.pallas_call(
        body, name="fox_gate_bwd", grid=(lp // tr,),
        in_specs=[spec, spec, pl.BlockSpec((tr, D), lambda i: (i, 3)), spec, aug], out_specs=[aug, aug, spec],
        out_shape=[jax.ShapeDtypeStruct((lp, FOX_AUG), BF16), jax.ShapeDtypeStruct((lp, FOX_AUG), BF16),
                   jax.ShapeDtypeStruct((lp, D), F32)],
        compiler_params=_params(("parallel",)),
    )(dog, o, proj, lse, qa)


def _fox_attn_bwd(q2, ka, va, doa, rs=None):
    lp = q2.shape[0]
    t = _tile(lp, FOX_TQ)
    nb = lp // t
    npair = FOX_H // 2

    def body(q_ref, k_ref, v_ref, do_ref, *rest):
        if rs is None:
            dq_ref, dk_ref, dv_ref, dc_ref, dq_acc, dk_acc, dv_acc, dc_acc = rest
        else:
            s_ref, dq_ref, dk_ref, dv_ref, dc_ref, got_ref, dq_acc, dk_acc, dv_acc, dc_acc, send_sems, recv_sems = rest
            sends, arrivals = _scatter_copies(s_ref, got_ref, send_sems, recv_sems)

            @pl.when((pl.program_id(0) == 0) & (pl.program_id(1) == 0))
            def _():
                for cp in sends:
                    cp.start()

            @pl.when((pl.program_id(0) == npair - 1) & (pl.program_id(1) == nb - 1))
            def _():
                for cp in arrivals:
                    cp.wait_recv()
                for cp in sends:
                    cp.wait_send()

        j = pl.program_id(1)

        @pl.when(j == 0)
        def _():
            dq_acc[...] = jnp.zeros_like(dq_acc)

        causal = _iota((t, t), 1) <= _iota((t, t), 0)
        ks = [k_ref[:, hh * LANES:(hh + 1) * LANES] for hh in range(2)]
        vs = [v_ref[:, hh * LANES:(hh + 1) * LANES] for hh in range(2)]
        dk_acc[...] = jnp.zeros_like(dk_acc)
        dv_acc[...] = jnp.zeros_like(dv_acc)
        dc_acc[...] = jnp.zeros_like(dc_acc)

        def block(i, diag):
            off = pl.multiple_of(i * t, t)
            for hh in range(2):
                q = q_ref[pl.ds(off, t), hh * LANES:(hh + 1) * LANES]
                do = do_ref[pl.ds(off, t), hh * LANES:(hh + 1) * LANES]
                s = nt(q, ks[hh])
                if diag:
                    s = jnp.where(causal, s, -1e30)
                p = jnp.exp(s)
                ds = p * nt(do, vs[hh])
                dc_acc[hh] += jnp.sum(ds, axis=0, keepdims=True)
                dsb = ds.astype(BF16)
                dv_acc[hh] += tn(p.astype(BF16), do)
                dk_acc[hh] += tn(dsb, q)
                dq_acc[hh, pl.ds(off, t), :] += nn(dsb, ks[hh])

        block(j, True)

        def step(i, c):
            block(i, False)
            return c

        lax.fori_loop(j + 1, nb, step, 0)
        left = _iota((t, LANES), 1) < FOX_DH
        dk_ref[...] = jnp.where(left, dk_acc[0], pltpu.roll(dk_acc[1], FOX_DH, 1))
        dv_ref[...] = jnp.where(left, dv_acc[0], pltpu.roll(dv_acc[1], FOX_DH, 1))
        for hh in range(2):
            dc_ref[hh] = jnp.broadcast_to(-dc_acc[hh], (8, t))

        @pl.when(j == nb - 1)
        def _():
            left = _iota((lp, LANES), 1) < FOX_DH
            dq_ref[...] = jnp.where(left, dq_acc[0], pltpu.roll(dq_acc[1], FOX_DH, 1))

    full = pl.BlockSpec((lp, 2 * LANES), lambda p, j: (0, p))
    kblk = pl.BlockSpec((t, 2 * LANES), lambda p, j: (j, p))
    oblk = pl.BlockSpec((t, LANES), lambda p, j: (j, p))
    in_specs = [full, kblk, kblk, full]
    out_specs = [pl.BlockSpec((lp, LANES), lambda p, j: (0, p)), oblk, oblk, pl.BlockSpec((2, 8, t), lambda p, j: (p, 0, j))]
    out_shape = [jax.ShapeDtypeStruct((lp, D), F32)] * 3 + [jax.ShapeDtypeStruct((FOX_H, 8, lp), F32)]
    scratch = [pltpu.VMEM((2, lp, LANES), F32), pltpu.VMEM((2, t, LANES), F32), pltpu.VMEM((2, t, LANES), F32),
               pltpu.VMEM((2, 1, t), F32)]
    if rs is None:
        return pl.pallas_call(body, name="fox_attn_bwd", grid=(npair, nb), in_specs=in_specs, out_specs=out_specs,
                              out_shape=out_shape, scratch_shapes=scratch,
                              compiler_params=_params(("parallel", "arbitrary")))(q2, ka, va, doa)
    return pl.pallas_call(
        body, name="fox_attn_bwd_rs", grid=(npair, nb), in_specs=in_specs + [ANY], out_specs=out_specs + [ANY],
        out_shape=out_shape + [jax.ShapeDtypeStruct((3,) + rs.shape[1:], rs.dtype)],
        scratch_shapes=scratch + [pltpu.SemaphoreType.DMA((3,)), pltpu.SemaphoreType.DMA((3,))],
        compiler_params=_params(("arbitrary", "arbitrary")),
    )(q2, ka, va, doa, rs)


def _fox_prep_bwd(proj, b_f, q_gain, k_gain, dqn, dkn, dv, dgate, dct):
    lp = proj.shape[0]
    nb = lp // LANES

    def body(p_ref, bf_ref, qg_ref, kg_ref, dq_ref, dk_ref, dv_ref, dg_ref, dc_ref,
             dp_ref, dqg_ref, dkg_ref, dbf_ref, carry):
        i = pl.program_id(0)

        @pl.when(i == 0)
        def _():
            carry[...] = jnp.zeros_like(carry)
            dqg_ref[...] = jnp.zeros_like(dqg_ref)
            dkg_ref[...] = jnp.zeros_like(dkg_ref)
            dbf_ref[...] = jnp.zeros_like(dbf_ref)

        down, up = _head_sel(FOX_H, FOX_DH)

        def norm_bwd(x, gain, dy, scale, dgain_ref):
            ms = _sel_r2(x * x, down) * (1.0 / FOX_DH)
            r = _sel_r2(lax.rsqrt(ms + EPS), up)
            u = dy * gain * scale
            mean_xu = _sel_r2(_sel_r2(x * u, down) * (1.0 / FOX_DH), up)
            dgain_ref[...] += jnp.sum(dy * scale * x * r, axis=0, keepdims=True)
            return r * u - x * (r * r * r) * mean_xu

        dp_ref[:, 0:D] = norm_bwd(p_ref[:, 0:D], qg_ref[...], dq_ref[...], FOX_DH ** -0.5, dqg_ref).astype(BF16)
        dp_ref[:, D:2 * D] = norm_bwd(p_ref[:, D:2 * D], kg_ref[...], dk_ref[...], 1.0, dkg_ref).astype(BF16)
        dp_ref[:, 2 * D:3 * D] = dv_ref[...].astype(BF16)
        dp_ref[:, 3 * D:4 * D] = dg_ref[...].astype(BF16)
        rows = jnp.concatenate([dc_ref[h, 0:1, :] for h in range(FOX_H)] + [jnp.zeros((LANES - FOX_H, LANES), F32)], axis=0)
        dlf = _sel_l(_tri(LANES, upper=True).astype(BF16), rows.T) + carry[0:1, :]
        carry[...] = jnp.broadcast_to(dlf[0:1, :], carry.shape)
        lane = _iota((LANES, LANES), 1)
        z = p_ref[:, 4 * D:4 * D + LANES] + bf_ref[...]
        df = jnp.where(lane < FOX_H, dlf * _sigmoid(-z), 0.0)
        dp_ref[:, 4 * D:4 * D + LANES] = df.astype(BF16)
        dbf_ref[...] += jnp.sum(df, axis=0, keepdims=True)

    rev = lambda i: (nb - 1 - i, 0)
    blk = pl.BlockSpec((LANES, D), rev)
    row = pl.BlockSpec((1, D), lambda i: (0, 0))
    row128 = pl.BlockSpec((1, LANES), lambda i: (0, 0))
    return pl.pallas_call(
        body, name="fox_prep_bwd", grid=(nb,),
        in_specs=[pl.BlockSpec((LANES, FOX_INP), rev), row128, row, row, blk, blk, blk, blk,
                  pl.BlockSpec((FOX_H, 8, LANES), lambda i: (0, 0, nb - 1 - i))],
        out_specs=[pl.BlockSpec((LANES, FOX_INP), rev), row, row, row128],
        out_shape=[jax.ShapeDtypeStruct((lp, FOX_INP), BF16), jax.ShapeDtypeStruct((1, D), F32),
                   jax.ShapeDtypeStruct((1, D), F32), jax.ShapeDtypeStruct((1, LANES), F32)],
        scratch_shapes=[pltpu.VMEM((8, LANES), F32)],
        compiler_params=_params(("arbitrary",)),
    )(proj, jnp.pad(b_f, (0, LANES - FOX_H)).reshape(1, LANES), jnp.tile(q_gain, FOX_H).reshape(1, D),
      jnp.tile(k_gain, FOX_H).reshape(1, D), dqn, dkn, dv, dgate, dct)


def _gla_gates(p_ref, wa_ref, ba_ref):
    a_lr = p_ref[:, 3072:3072 + LANES]
    z = nn(a_lr.astype(BF16), wa_ref[...].astype(BF16)) + ba_ref[...]
    g = _log_sigmoid(z) * (1.0 / GLA_NORM)
    b = _sel_l(_tri(CHUNK).astype(BF16), g)
    return a_lr, z, b


def _gla_chunk_fwd(q, k, v, b, st0):
    hs = range(len(q))
    low = _tri(CHUNK)
    bl = [b[h][CHUNK - 1:CHUNK, :] for h in hs]
    qe = [q[h] * jnp.exp(b[h]) for h in hs]
    ke = [k[h] * jnp.exp(-b[h]) for h in hs]
    kd = [k[h] * jnp.exp(bl[h] - b[h]) for h in hs]
    a = [jnp.where(low, nt(qe[h], ke[h]), 0.0) for h in hs]
    o = [nn(a[h], v[h]) + nt(qe[h], st0[h]) for h in hs]
    st1 = [st0[h] * jnp.exp(bl[h]) + tn(v[h], kd[h]) for h in hs]
    return o, st1, (qe, ke, kd, a, bl)


def _gla_slices(p_ref, b_all, h):
    q = p_ref[:, h * GLA_DK:(h + 1) * GLA_DK] * (GLA_DK ** -0.5)
    k = p_ref[:, GLA_QK + h * GLA_DK:GLA_QK + (h + 1) * GLA_DK]
    v = p_ref[:, 2 * GLA_QK + h * GLA_DV:2 * GLA_QK + (h + 1) * GLA_DV]
    r = p_ref[:, 2 * GLA_QK + GLA_V + h * GLA_DV:2 * GLA_QK + GLA_V + (h + 1) * GLA_DV]
    return q, k, v, r, b_all[:, h * GLA_DK:(h + 1) * GLA_DK]


def _gla_fwd(proj, w_alpha2, b_alpha, o_gain):
    lp = proj.shape[0]
    nc = lp // CHUNK

    def body(p_ref, wa_ref, ba_ref, og_ref, o_ref, y_ref, s_ref, st):
        @pl.when(pl.program_id(0) == 0)
        def _():
            st[...] = jnp.zeros_like(st)

        _, _, b_all = _gla_gates(p_ref, wa_ref, ba_ref)
        hs = range(GLA_H)
        parts = [_gla_slices(p_ref, b_all, h) for h in hs]
        st0 = [st[h] for h in hs]
        for h in hs:
            s_ref[0, h] = st0[h]
        o, st1, _ = _gla_chunk_fwd([p[0] for p in parts], [p[1] for p in parts], [p[2] for p in parts],
                                   [p[4] for p in parts], st0)
        for h in hs:
            st[h] = st1[h]
            o_ref[:, h * GLA_DV:(h + 1) * GLA_DV] = o[h]
            rs = lax.rsqrt(jnp.mean(o[h] * o[h], axis=-1, keepdims=True) + EPS)
            y_ref[:, h * GLA_DV:(h + 1) * GLA_DV] = (o[h] * rs * og_ref[...] * _silu(parts[h][3])).astype(BF16)

    blk = pl.BlockSpec((CHUNK, D), lambda i: (i, 0))
    return pl.pallas_call(
        body, name="gla_fwd", grid=(nc,),
        in_specs=[pl.BlockSpec((CHUNK, GLA_INP), lambda i: (i, 0)), pl.BlockSpec((LANES, GLA_QK), lambda i: (0, 0)),
                  pl.BlockSpec((1, GLA_QK), lambda i: (0, 0)), pl.BlockSpec((1, GLA_DV), lambda i: (0, 0))],
        out_specs=[blk, blk, pl.BlockSpec((1, GLA_H, GLA_DV, GLA_DK), lambda i: (i, 0, 0, 0))],
        out_shape=[jax.ShapeDtypeStruct((lp, D), F32), jax.ShapeDtypeStruct((lp, D), BF16),
                   jax.ShapeDtypeStruct((nc, GLA_H, GLA_DV, GLA_DK), F32)],
        scratch_shapes=[pltpu.VMEM((GLA_H, GLA_DV, GLA_DK), F32)],
        compiler_params=_params(("arbitrary",)),
    )(proj, jnp.pad(w_alpha2, ((0, LANES - GLA_RANK), (0, 0))), b_alpha.reshape(1, GLA_QK), o_gain.reshape(1, GLA_DV))


def _gla_bwd(proj, w_alpha2, b_alpha, o_gain, o, states, dy):
    lp = proj.shape[0]
    nc = lp // CHUNK

    def body(p_ref, wa_ref, ba_ref, og_ref, o_ref, s_ref, dy_ref, dp_ref, dwa_ref, dba_ref, dog_ref, dst):
        @pl.when(pl.program_id(0) == 0)
        def _():
            dst[...] = jnp.zeros_like(dst)
            dwa_ref[...] = jnp.zeros_like(dwa_ref)
            dba_ref[...] = jnp.zeros_like(dba_ref)
            dog_ref[...] = jnp.zeros_like(dog_ref)

        a_lr, z, b_all = _gla_gates(p_ref, wa_ref, ba_ref)
        last_row = _iota((CHUNK, GLA_DK), 0) == CHUNK - 1
        rev = _tri(CHUNK, upper=True).astype(BF16)
        hs = range(GLA_H)
        scale = GLA_DK ** -0.5
        parts = [_gla_slices(p_ref, b_all, h) for h in hs]
        q, k, v, b = [p[0] for p in parts], [p[1] for p in parts], [p[2] for p in parts], [p[4] for p in parts]
        st0 = [s_ref[0, h] for h in hs]
        dst1 = [dst[h] for h in hs]
        do = []
        for h in hs:
            r = parts[h][3]
            ov = o_ref[:, h * GLA_DV:(h + 1) * GLA_DV]
            dyv = dy_ref[:, h * GLA_DV:(h + 1) * GLA_DV]
            rs = lax.rsqrt(jnp.mean(ov * ov, axis=-1, keepdims=True) + EPS)
            on = ov * rs
            dp_ref[:, 2 * GLA_QK + GLA_V + h * GLA_DV:2 * GLA_QK + GLA_V + (h + 1) * GLA_DV] = (
                dyv * on * og_ref[...] * _dsilu(r)).astype(BF16)
            don = dyv * _silu(r)
            dog_ref[...] += jnp.sum(don * on, axis=0, keepdims=True)
            u = don * og_ref[...]
            do.append(rs * u - ov * (rs * rs * rs) * jnp.mean(ov * u, axis=-1, keepdims=True))
        _, _, (qe, ke, kd, a, bl) = _gla_chunk_fwd(q, k, v, b, st0)
        low = _tri(CHUNK)
        da = [jnp.where(low, nt(do[h], v[h]), 0.0) for h in hs]
        dkd = [nn(v[h], dst1[h]) for h in hs]
        dvv = [tn(a[h], do[h]) + nt(kd[h], dst1[h]) for h in hs]
        dqe = [nn(da[h], ke[h]) + nn(do[h], st0[h]) for h in hs]
        dke = [tn(da[h], qe[h]) for h in hs]
        dg_parts = []
        for h in hs:
            ebl = jnp.exp(bl[h])
            dst[h] = dst1[h] * ebl + tn(do[h], qe[h])
            db = dqe[h] * qe[h] - dke[h] * ke[h] - dkd[h] * kd[h]
            db_last = (jnp.sum(dkd[h] * kd[h], axis=0, keepdims=True)
                       + jnp.sum(dst1[h] * st0[h], axis=0, keepdims=True) * ebl)
            db = db + jnp.where(last_row, db_last, 0.0)
            dg_parts.append(_sel_l(rev, db))
            dp_ref[:, h * GLA_DK:(h + 1) * GLA_DK] = (dqe[h] * jnp.exp(b[h]) * scale).astype(BF16)
            dp_ref[:, GLA_QK + h * GLA_DK:GLA_QK + (h + 1) * GLA_DK] = (
                dke[h] * jnp.exp(-b[h]) + dkd[h] * jnp.exp(bl[h] - b[h])).astype(BF16)
            dp_ref[:, 2 * GLA_QK + h * GLA_DV:2 * GLA_QK + (h + 1) * GLA_DV] = dvv[h].astype(BF16)
        dg = jnp.concatenate(dg_parts, axis=1)
        dz = dg * (1.0 / GLA_NORM) * _sigmoid(-z)
        dzb = dz.astype(BF16)
        dp_ref[:, 3072:3072 + LANES] = nt(dzb, wa_ref[...].astype(BF16)).astype(BF16)
        dwa_ref[...] += tn(a_lr.astype(BF16), dzb)
        dba_ref[...] += jnp.sum(dz, axis=0, keepdims=True)

    rv = lambda i: (nc - 1 - i, 0)
    blk = pl.BlockSpec((CHUNK, D), rv)
    fixed = lambda r, c: pl.BlockSpec((r, c), lambda i: (0, 0))
    return pl.pallas_call(
        body, name="gla_bwd", grid=(nc,),
        in_specs=[pl.BlockSpec((CHUNK, GLA_INP), rv), fixed(LANES, GLA_QK), fixed(1, GLA_QK), fixed(1, GLA_DV), blk,
                  pl.BlockSpec((1, GLA_H, GLA_DV, GLA_DK), lambda i: (nc - 1 - i, 0, 0, 0)), blk],
        out_specs=[pl.BlockSpec((CHUNK, GLA_INP), rv), fixed(LANES, GLA_QK), fixed(1, GLA_QK), fixed(1, GLA_DV)],
        out_shape=[jax.ShapeDtypeStruct((lp, GLA_INP), BF16), jax.ShapeDtypeStruct((LANES, GLA_QK), F32),
                   jax.ShapeDtypeStruct((1, GLA_QK), F32), jax.ShapeDtypeStruct((1, GLA_DV), F32)],
        scratch_shapes=[pltpu.VMEM((GLA_H, GLA_DV, GLA_DK), F32)],
        compiler_params=_params(("arbitrary",)),
    )(proj, jnp.pad(w_alpha2, ((0, LANES - GLA_RANK), (0, 0))), b_alpha.reshape(1, GLA_QK), o_gain.reshape(1, GLA_DV),
      o, states, dy)


HI = lax.Precision.HIGHEST


def _gdn_pre(prev_ref, p_ref, cw_ref, al_ref, dt_ref):
    xc = jnp.concatenate([prev_ref[:, 0:GDN_CONV], p_ref[:, 0:GDN_CONV]], axis=0)
    shifted = [pltpu.roll(xc, 3 - j, 0)[CHUNK:, :] if j < 3 else xc[CHUNK:, :] for j in range(4)]
    conv = sum(shifted[j] * cw_ref[j:j + 1, :] for j in range(4))
    act = _silu(conv)
    slab = p_ref[:, 4096:4096 + LANES]
    lane = _iota((CHUNK, LANES), 1)
    zs = slab + dt_ref[...]
    g = jnp.where(lane < GDN_H, -jnp.exp(al_ref[...]) * _softplus(zs), 0.0)
    bs = _sel_l(_tri(CHUNK).astype(BF16), g)
    beta = _sigmoid(slab)
    return shifted, conv, act, slab, zs, g, bs, beta


def _l2n(x):
    r = lax.rsqrt(jnp.sum(x * x, axis=-1, keepdims=True) + EPS)
    return x * r, r


def _gdn_chunk_fwd(q, k, v, beta, bcol, brow, s0):
    hs = range(len(q))
    ii, jj = _iota((CHUNK, CHUNK), 0), _iota((CHUNK, CHUNK), 1)
    low, eye = ii >= jj, (ii == jj).astype(F32)
    dm = [jnp.where(low, jnp.exp(jnp.where(low, bcol[h] - brow[h], 0.0)), 0.0) for h in hs]
    dstrict = [jnp.where(ii > jj, dm[h], 0.0) for h in hs]
    eb = [jnp.exp(bcol[h]) for h in hs]
    bl = [bcol[h][CHUNK - 1:CHUNK, :] for h in hs]
    kb = [k[h] * beta[h] for h in hs]
    vb = [v[h] * beta[h] for h in hs]
    nmat = [nt(kb[h], k[h]) * dstrict[h] for h in hs]
    x = [eye - nmat[h] for h in hs]
    pw = [nn(nmat[h], nmat[h], precision=HI) for h in hs]
    for it in range(5):
        x = [x[h] + nn(x[h], pw[h], precision=HI) for h in hs]
        if it < 4:
            pw = [nn(pw[h], pw[h], precision=HI) for h in hs]
    kbe = [kb[h] * eb[h] for h in hs]
    u = [nn(x[h], vb[h], precision=HI) for h in hs]
    w = [nn(x[h], kbe[h], precision=HI) for h in hs]
    vn = [u[h] - nn(w[h], s0[h]) for h in hs]
    pm = [nt(q[h], k[h]) * dm[h] for h in hs]
    qe = [q[h] * eb[h] for h in hs]
    o = [nn(pm[h], vn[h]) + nn(qe[h], s0[h]) for h in hs]
    kd = [k[h] * jnp.exp(bl[h] - bcol[h]) for h in hs]
    s1 = [s0[h] * jnp.exp(bl[h]) + tn(kd[h], vn[h]) for h in hs]
    return o, s1, dict(dm=dm, dstrict=dstrict, eb=eb, bl=bl, kb=kb, vb=vb, nmat=nmat, tinv=x, kbe=kbe, u=u, w=w, vn=vn,
                       pm=pm, qe=qe, kd=kd)


def _gdn_heads(act, beta_slab, bs, h):
    qa = act[:, h * GDN_DK:(h + 1) * GDN_DK]
    ka = act[:, GDN_H * GDN_DK + h * GDN_DK:GDN_H * GDN_DK + (h + 1) * GDN_DK]
    v = act[:, 2 * GDN_H * GDN_DK + h * GDN_DV:2 * GDN_H * GDN_DK + (h + 1) * GDN_DV]
    return qa, ka, v, beta_slab[:, GDN_H + h:GDN_H + h + 1], bs[:, h:h + 1]


def _gdn_fwd(proj, conv_w, a_log, dt_bias, o_gain):
    lp = proj.shape[0]
    nc = lp // CHUNK

    def body(prev_ref, p_ref, cw_ref, al_ref, dt_ref, og_ref, o_ref, y_ref, s_ref, st):
        @pl.when(pl.program_id(0) == 0)
        def _():
            st[...] = jnp.zeros_like(st)

        _, _, act, _, _, _, bs, beta = _gdn_pre(prev_ref, p_ref, cw_ref, al_ref, dt_ref)
        bst = bs.T
        hs = range(GDN_H)
        parts = [_gdn_heads(act, beta, bs, h) for h in hs]
        q = [_l2n(parts[h][0])[0] * (GDN_DK ** -0.5) for h in hs]
        k = [_l2n(parts[h][1])[0] for h in hs]
        s0 = [st[h] for h in hs]
        for h in hs:
            s_ref[0, h] = s0[h]
        o, s1, _ = _gdn_chunk_fwd(q, k, [parts[h][2] for h in hs], [parts[h][3] for h in hs], [parts[h][4] for h in hs],
                                  [bst[h:h + 1, :] for h in hs], s0)
        for h in hs:
            st[h] = s1[h]
            o_ref[:, h * GDN_DV:(h + 1) * GDN_DV] = o[h]
            rs = lax.rsqrt(jnp.mean(o[h] * o[h], axis=-1, keepdims=True) + EPS)
            gate = p_ref[:, GDN_CONV + h * GDN_DV:GDN_CONV + (h + 1) * GDN_DV]
            y_ref[:, h * GDN_DV:(h + 1) * GDN_DV] = (o[h] * rs * og_ref[...] * _silu(gate)).astype(BF16)

    blk = pl.BlockSpec((CHUNK, D), lambda i: (i, 0))
    fixed = lambda r, c: pl.BlockSpec((r, c), lambda i: (0, 0))
    return pl.pallas_call(
        body, name="gdn_fwd", grid=(nc,),
        in_specs=[pl.BlockSpec((CHUNK, GDN_INP), lambda i: (jnp.maximum(i - 1, 0), 0)),
                  pl.BlockSpec((CHUNK, GDN_INP), lambda i: (i, 0)), fixed(8, GDN_CONV), fixed(1, LANES), fixed(1, LANES),
                  fixed(1, GDN_DV)],
        out_specs=[blk, blk, pl.BlockSpec((1, GDN_H, GDN_DK, GDN_DV), lambda i: (i, 0, 0, 0))],
        out_shape=[jax.ShapeDtypeStruct((lp, D), F32), jax.ShapeDtypeStruct((lp, D), BF16),
                   jax.ShapeDtypeStruct((nc, GDN_H, GDN_DK, GDN_DV), F32)],
        scratch_shapes=[pltpu.VMEM((GDN_H, GDN_DK, GDN_DV), F32)],
        compiler_params=_params(("arbitrary",)),
    )(proj, proj, jnp.pad(conv_w.reshape(4, GDN_CONV), ((0, 4), (0, 0))), jnp.pad(a_log, (0, LANES - GDN_H)).reshape(1, LANES),
      jnp.pad(dt_bias, (0, LANES - GDN_H)).reshape(1, LANES), o_gain.reshape(1, GDN_DV))


def _gdn_bwd(proj, conv_w, a_log, dt_bias, o_gain, o, states, dy):
    lp = proj.shape[0]
    nc = lp // CHUNK

    def body(prev_ref, p_ref, cw_ref, al_ref, dt_ref, og_ref, o_ref, s_ref, dy_ref,
             dp_ref, dcw_ref, dal_ref, ddt_ref, dog_ref, dst, dconv_next):
        @pl.when(pl.program_id(0) == 0)
        def _():
            dst[...] = jnp.zeros_like(dst)
            dconv_next[...] = jnp.zeros_like(dconv_next)
            dcw_ref[...] = jnp.zeros_like(dcw_ref)
            dal_ref[...] = jnp.zeros_like(dal_ref)
            ddt_ref[...] = jnp.zeros_like(ddt_ref)
            dog_ref[...] = jnp.zeros_like(dog_ref)

        shifted, conv, act, slab, zs, g, bs, beta = _gdn_pre(prev_ref, p_ref, cw_ref, al_ref, dt_ref)
        bst = bs.T
        lane = _iota((CHUNK, LANES), 1)
        ones = jnp.ones((CHUNK, LANES), F32)
        db_slab = jnp.zeros((CHUNK, LANES), F32)
        dbeta_slab = jnp.zeros((CHUNK, LANES), F32)
        last_row = _iota((CHUNK, 1), 0) == CHUNK - 1
        hs = range(GDN_H)
        scale = GDN_DK ** -0.5
        parts = [_gdn_heads(act, beta, bs, h) for h in hs]
        qa, ka, v = [parts[h][0] for h in hs], [parts[h][1] for h in hs], [parts[h][2] for h in hs]
        bet, bcol = [parts[h][3] for h in hs], [parts[h][4] for h in hs]
        qn_ = [_l2n(qa[h]) for h in hs]
        kn_ = [_l2n(ka[h]) for h in hs]
        q = [qn_[h][0] * scale for h in hs]
        k, rq, rk = [kn_[h][0] for h in hs], [qn_[h][1] for h in hs], [kn_[h][1] for h in hs]
        s0 = [s_ref[0, h] for h in hs]
        ds1 = [dst[h] for h in hs]
        do = []
        for h in hs:
            ov = o_ref[:, h * GDN_DV:(h + 1) * GDN_DV]
            dyv = dy_ref[:, h * GDN_DV:(h + 1) * GDN_DV]
            gate = p_ref[:, GDN_CONV + h * GDN_DV:GDN_CONV + (h + 1) * GDN_DV]
            rs = lax.rsqrt(jnp.mean(ov * ov, axis=-1, keepdims=True) + EPS)
            on = ov * rs
            dp_ref[:, GDN_CONV + h * GDN_DV:GDN_CONV + (h + 1) * GDN_DV] = (dyv * on * og_ref[...] * _dsilu(gate)).astype(BF16)
            don = dyv * _silu(gate)
            dog_ref[...] += jnp.sum(don * on, axis=0, keepdims=True)
            uu = don * og_ref[...]
            do.append(rs * uu - ov * (rs * rs * rs) * jnp.mean(ov * uu, axis=-1, keepdims=True))
        _, _, f = _gdn_chunk_fwd(q, k, v, bet, bcol, [bst[h:h + 1, :] for h in hs], s0)
        dm, dstrict, eb, bl, kb, nmat, tinv = f["dm"], f["dstrict"], f["eb"], f["bl"], f["kb"], f["nmat"], f["tinv"]
        kbe, u, w, vn, pm, qe, kd = f["kbe"], f["u"], f["w"], f["vn"], f["pm"], f["qe"], f["kd"]
        ebl = [jnp.exp(bl[h]) for h in hs]
        dvn = [tn(pm[h], do[h]) + nn(kd[h], ds1[h]) for h in hs]
        dpr = [nt(do[h], vn[h]) for h in hs]
        dqe = [nt(do[h], s0[h]) for h in hs]
        dkd = [nt(vn[h], ds1[h]) for h in hs]
        for h in hs:
            dst[h] = ds1[h] * ebl[h] + tn(qe[h], do[h]) - tn(w[h], dvn[h])
        du_ = [tn(tinv[h], dvn[h], precision=HI) for h in hs]
        dw_ = [tn(tinv[h], -nt(dvn[h], s0[h]), precision=HI) for h in hs]
        dn = [-(nt(du_[h], u[h]) + nt(dw_[h], w[h])) for h in hs]
        dqk = [dpr[h] * dm[h] for h in hs]
        dkk = [dn[h] * dstrict[h] for h in hs]
        gsum = [dpr[h] * pm[h] + dn[h] * nmat[h] for h in hs]
        dkb = [nn(dkk[h], k[h]) + dw_[h] * eb[h] for h in hs]
        dk = [tn(dkk[h], kb[h]) + tn(dqk[h], q[h]) + dkd[h] * jnp.exp(bl[h] - bcol[h]) + dkb[h] * bet[h] for h in hs]
        dq = [nn(dqk[h], k[h]) + dqe[h] * eb[h] for h in hs]
        colsum = [tn(gsum[h], ones, precision=HI)[:, 0:1] for h in hs]
        dact_q, dact_k, dact_v = [], [], []
        for h in hs:
            dbeta = jnp.sum(dkb[h] * k[h], axis=-1, keepdims=True) + jnp.sum(du_[h] * v[h], axis=-1, keepdims=True)
            skd = jnp.sum(dkd[h] * kd[h], axis=-1, keepdims=True)
            db = (jnp.sum(gsum[h], axis=-1, keepdims=True) - colsum[h] + jnp.sum(dqe[h] * qe[h], axis=-1, keepdims=True)
                  + jnp.sum(dw_[h] * kbe[h], axis=-1, keepdims=True) - skd)
            db_last = jnp.sum(skd, axis=0, keepdims=True) + jnp.sum(ds1[h] * s0[h]) * ebl[h]
            db = db + jnp.where(last_row, db_last, 0.0)
            db_slab = db_slab + jnp.where(lane == h, db, 0.0)
            dbeta_slab = dbeta_slab + jnp.where(lane == GDN_H + h, dbeta, 0.0)
            dqn = dq[h] * scale
            dact_q.append(rq[h] * dqn - qa[h] * (rq[h] * rq[h] * rq[h]) * jnp.sum(qa[h] * dqn, axis=-1, keepdims=True))
            dact_k.append(rk[h] * dk[h] - ka[h] * (rk[h] * rk[h] * rk[h]) * jnp.sum(ka[h] * dk[h], axis=-1, keepdims=True))
            dact_v.append(du_[h] * bet[h])
        dact = jnp.concatenate(dact_q + dact_k + dact_v, axis=1)
        dconv = dact * _dsilu(conv)
        for j in range(4):
            dcw_ref[j:j + 1, :] += jnp.sum(dconv * shifted[j], axis=0, keepdims=True)
        dcat = jnp.concatenate([dconv, dconv_next[...]], axis=0)
        dx = dconv * cw_ref[3:4, :]
        for j in range(3):
            dx = dx + pltpu.roll(dcat, 2 * CHUNK - (3 - j), 0)[:CHUNK, :] * cw_ref[j:j + 1, :]
        dconv_next[...] = dconv
        dp_ref[:, 0:GDN_CONV] = dx.astype(BF16)
        dg = _sel_l(_tri(CHUNK, upper=True).astype(BF16), db_slab)
        da = dg * (-jnp.exp(al_ref[...])) * _sigmoid(zs)
        da = jnp.where(lane < GDN_H, da, 0.0)
        dal_ref[...] += jnp.sum(dg * g, axis=0, keepdims=True)
        ddt_ref[...] += jnp.sum(da, axis=0, keepdims=True)
        dp_ref[:, 4096:4096 + LANES] = (da + dbeta_slab * beta * (1.0 - beta)).astype(BF16)

    rv = lambda i: (nc - 1 - i, 0)
    blk = pl.BlockSpec((CHUNK, D), rv)
    fixed = lambda r, c: pl.BlockSpec((r, c), lambda i: (0, 0))
    return pl.pallas_call(
        body, name="gdn_bwd", grid=(nc,),
        in_specs=[pl.BlockSpec((CHUNK, GDN_INP), lambda i: (jnp.maximum(nc - 2 - i, 0), 0)),
                  pl.BlockSpec((CHUNK, GDN_INP), rv), fixed(8, GDN_CONV), fixed(1, LANES), fixed(1, LANES), fixed(1, GDN_DV),
                  blk, pl.BlockSpec((1, GDN_H, GDN_DK, GDN_DV), lambda i: (nc - 1 - i, 0, 0, 0)), blk],
        out_specs=[pl.BlockSpec((CHUNK, GDN_INP), rv), fixed(8, GDN_CONV), fixed(1, LANES), fixed(1, LANES), fixed(1, GDN_DV)],
        out_shape=[jax.ShapeDtypeStruct((lp, GDN_INP), BF16), jax.ShapeDtypeStruct((8, GDN_CONV), F32),
                   jax.ShapeDtypeStruct((1, LANES), F32), jax.ShapeDtypeStruct((1, LANES), F32),
                   jax.ShapeDtypeStruct((1, GDN_DV), F32)],
        scratch_shapes=[pltpu.VMEM((GDN_H, GDN_DK, GDN_DV), F32), pltpu.VMEM((CHUNK, GDN_CONV), F32)],
        compiler_params=_params(("arbitrary",)),
    )(proj, proj, jnp.pad(conv_w.reshape(4, GDN_CONV), ((0, 4), (0, 0))), jnp.pad(a_log, (0, LANES - GDN_H)).reshape(1, LANES),
      jnp.pad(dt_bias, (0, LANES - GDN_H)).reshape(1, LANES), o_gain.reshape(1, GDN_DV), o, states, dy)


def _coords():
    return lax.axis_index("x"), lax.axis_index("y"), lax.axis_index("c")


def _other_chips(x, y):
    return [(1 - x, y, 2 * (1 - x) + y), (x, 1 - y, 2 * x + 1 - y), (1 - x, 1 - y, 2 * (1 - x) + 1 - y)]


def _gather8(v, *, reduce, name):
    r, c = v.shape

    def body(v_ref, out_ref, *scratch):
        if reduce:
            buf, send_sems, recv_sems = scratch
        else:
            buf = out_ref
            send_sems, recv_sems = scratch
        x, y, cc = _coords()
        me = 4 * x + 2 * y + cc
        buf[me] = v_ref[...]
        copies = []
        for k in range(1, 8):
            px, py, pc = x ^ (k >> 2), y ^ ((k >> 1) & 1), cc ^ (k & 1)
            copies.append(pltpu.make_async_remote_copy(
                src_ref=v_ref, dst_ref=buf.at[me], send_sem=send_sems.at[k - 1], recv_sem=recv_sems.at[k - 1],
                device_id=(px, py, pc), device_id_type=MESH))
        for cp in copies:
            cp.start()
        for k in range(1, 8):
            peer = (x ^ (k >> 2)) * 4 + (y ^ ((k >> 1) & 1)) * 2 + (cc ^ (k & 1))
            pltpu.make_async_remote_copy(
                src_ref=v_ref, dst_ref=buf.at[peer], send_sem=send_sems.at[k - 1], recv_sem=recv_sems.at[k - 1],
                device_id=(x, y, cc), device_id_type=MESH).wait_recv()
        for cp in copies:
            cp.wait_send()
        if reduce:
            acc = buf[0]
            for d in range(1, 8):
                acc = acc + buf[d]
            out_ref[...] = acc

    scratch = [pltpu.SemaphoreType.DMA((7,)), pltpu.SemaphoreType.DMA((7,))]
    if reduce:
        scratch = [pltpu.VMEM((8, r, c), F32)] + scratch
    return pl.pallas_call(
        body, name=name, in_specs=[VM], out_specs=VM,
        out_shape=jax.ShapeDtypeStruct((r, c) if reduce else (8, r, c), F32),
        scratch_shapes=scratch, compiler_params=_params(),
    )(v)


class _AgCopies:
    def __init__(self, buf, ranges, send_sems, recv_sems):
        self.buf, self.ranges, self.send_sems, self.recv_sems = buf, ranges, send_sems, recv_sems
        self.x, self.y, self.cc = _coords()
        self.p = 2 * self.x + self.y
        self.chips = _other_chips(self.x, self.y)

    def rows(self, chip, r, hf):
        start, n = self.ranges[r]
        return self.buf.at[chip, pl.ds(start + hf * (n // 2), n // 2), :]

    def _copy(self, r, k, chip, hf, to):
        return pltpu.make_async_remote_copy(
            src_ref=self.rows(chip, r, hf), dst_ref=self.rows(chip, r, hf), send_sem=self.send_sems.at[3 * r + k],
            recv_sem=self.recv_sems.at[3 * r + k], device_id=to, device_id_type=MESH)

    def pairs(self):
        return [(r, k) for r in range(len(self.ranges)) for k in range(3)]

    def ici(self, r, k):
        cx, cy, _ = self.chips[k]
        return self._copy(r, k, self.p, self.cc, (cx, cy, self.cc))

    def ici_arrival(self, r, k):
        return self._copy(r, k, self.chips[k][2], self.cc, (self.x, self.y, self.cc))

    def forward(self, r, k):
        return self._copy(r, k, self.chips[k][2], self.cc, (self.x, self.y, 1 - self.cc))

    def forward_arrival(self, r, k):
        return self._copy(r, k, self.chips[k][2], 1 - self.cc, (self.x, self.y, self.cc))


def _ag_weights(w4, ranges):
    n = 3 * len(ranges)

    def body(w_ref, out_ref, send1, recv1, send2, recv2):
        ici, fwd = _AgCopies(out_ref, ranges, send1, recv1), _AgCopies(out_ref, ranges, send2, recv2)
        for r, k in ici.pairs():
            ici.ici(r, k).start()
        for r, k in ici.pairs():
            ici.ici_arrival(r, k).wait_recv()
            fwd.forward(r, k).start()
        for r, k in ici.pairs():
            fwd.forward_arrival(r, k).wait_recv()
        for r, k in ici.pairs():
            ici.ici(r, k).wait_send()
            fwd.forward(r, k).wait_send()

    return pl.pallas_call(
        body, name="ag_weights", in_specs=[ANY], out_specs=ANY, out_shape=jax.ShapeDtypeStruct(w4.shape, w4.dtype),
        scratch_shapes=[pltpu.SemaphoreType.DMA((n,))] * 4, input_output_aliases={0: 0}, compiler_params=_params(),
    )(w4)


def _ag_forward(w4, ranges):
    n = 3 * len(ranges)

    def body(w_ref, out_ref, send2, recv2):
        fwd = _AgCopies(out_ref, ranges, send2, recv2)
        for r, k in fwd.pairs():
            fwd.forward(r, k).start()
        for r, k in fwd.pairs():
            fwd.forward_arrival(r, k).wait_recv()
        for r, k in fwd.pairs():
            fwd.forward(r, k).wait_send()

    return pl.pallas_call(
        body, name="ag_forward", in_specs=[ANY], out_specs=ANY, out_shape=jax.ShapeDtypeStruct(w4.shape, w4.dtype),
        scratch_shapes=[pltpu.SemaphoreType.DMA((n,))] * 2, input_output_aliases={0: 0}, compiler_params=_params(),
    )(w4)


def _swap_halves(g, *, name):
    nb, r, c = g.shape
    half = r // 2

    def body(g_ref, out_ref, send_sem, recv_sem):
        x, y, cc = _coords()
        cp = pltpu.make_async_remote_copy(
            src_ref=g_ref.at[:, pl.ds((1 - cc) * half, half), :], dst_ref=out_ref, send_sem=send_sem, recv_sem=recv_sem,
            device_id=(x, y, 1 - cc), device_id_type=MESH)
        cp.start()
        cp.wait()

    return pl.pallas_call(
        body, name=name, in_specs=[ANY], out_specs=ANY, out_shape=jax.ShapeDtypeStruct((nb, half, c), g.dtype),
        scratch_shapes=[pltpu.SemaphoreType.DMA, pltpu.SemaphoreType.DMA], compiler_params=_params(),
    )(g)


def _my_half_index():
    return lax.axis_index("c").astype(jnp.int32).reshape(1)


def _add_halves(g, got, tag):
    nb, r, c = g.shape
    half = r // 2
    tr = _tile(half, 512, 16)
    nt_ = half // tr

    def body(c_ref, a_ref, b_ref, o_ref):
        o_ref[...] = (a_ref[...].astype(F32) + b_ref[...].astype(F32)).astype(BF16)

    return pl.pallas_call(
        body, name=f"rs_add_sibling{tag}",
        grid_spec=pltpu.PrefetchScalarGridSpec(
            num_scalar_prefetch=1, grid=(nb, nt_),
            in_specs=[pl.BlockSpec((1, tr, c), lambda b, i, cr: (b, cr[0] * nt_ + i, 0)),
                      pl.BlockSpec((1, tr, c), lambda b, i, cr: (b, i, 0))],
            out_specs=pl.BlockSpec((1, tr, c), lambda b, i, cr: (b, i, 0))),
        out_shape=jax.ShapeDtypeStruct((nb, half, c), BF16), compiler_params=_params(("parallel", "parallel")),
    )(_my_half_index(), g, got)


def _scatter_copies(s_ref, out_ref, send_sems, recv_sems):
    x, y, cc = _coords()
    sends = [pltpu.make_async_remote_copy(
        src_ref=s_ref.at[blk], dst_ref=out_ref.at[k], send_sem=send_sems.at[k], recv_sem=recv_sems.at[k],
        device_id=(cx, cy, cc), device_id_type=MESH) for k, (cx, cy, blk) in enumerate(_other_chips(x, y))]
    arrivals = [pltpu.make_async_remote_copy(
        src_ref=s_ref.at[2 * x + y], dst_ref=out_ref.at[k], send_sem=send_sems.at[k], recv_sem=recv_sems.at[k],
        device_id=(x, y, cc), device_id_type=MESH) for k in range(3)]
    return sends, arrivals


def _scatter_chips(s, tag):
    nb, hrows, c = s.shape

    def body(s_ref, out_ref, send_sems, recv_sems):
        sends, arrivals = _scatter_copies(s_ref, out_ref, send_sems, recv_sems)
        for cp in sends:
            cp.start()
        for cp in arrivals:
            cp.wait_recv()
        for cp in sends:
            cp.wait_send()

    return pl.pallas_call(
        body, name=f"rs_scatter{tag}", in_specs=[ANY], out_specs=ANY, out_shape=jax.ShapeDtypeStruct((3, hrows, c), s.dtype),
        scratch_shapes=[pltpu.SemaphoreType.DMA((3,)), pltpu.SemaphoreType.DMA((3,))], compiler_params=_params(),
    )(s)


def _sum_chips(s, got, tag):
    nb, hrows, c = s.shape
    tr = _tile(hrows, 512, 16)

    def body(idx_ref, own_ref, got_ref, o_ref):
        p = idx_ref[0]
        own = own_ref[0].astype(F32)
        parts = [got_ref[k].astype(F32) for k in range(3)]
        acc = jnp.zeros_like(own)
        for q in range(4):
            val = own
            for k, rel in enumerate((2, 1, 3)):
                val = jnp.where((p ^ rel) == q, parts[k], val)
            acc = acc + val
        o_ref[...] = acc

    idx = (2 * lax.axis_index("x") + lax.axis_index("y")).astype(jnp.int32).reshape(1)
    return pl.pallas_call(
        body, name=f"rs_sum_chips{tag}",
        grid_spec=pltpu.PrefetchScalarGridSpec(
            num_scalar_prefetch=1, grid=(hrows // tr,),
            in_specs=[pl.BlockSpec((1, tr, c), lambda i, pr: (pr[0], i, 0)), pl.BlockSpec((3, tr, c), lambda i, pr: (0, i, 0))],
            out_specs=pl.BlockSpec((tr, c), lambda i, pr: (i, 0))),
        out_shape=jax.ShapeDtypeStruct((hrows, c), F32), compiler_params=_params(("parallel",)),
    )(idx, s, got)


def _swap_sibling(t, tag):
    def body(t_ref, out_ref, send_sem, recv_sem):
        x, y, cc = _coords()
        cp = pltpu.make_async_remote_copy(src_ref=t_ref, dst_ref=out_ref, send_sem=send_sem, recv_sem=recv_sem,
                                          device_id=(x, y, 1 - cc), device_id_type=MESH)
        cp.start()
        cp.wait()

    return pl.pallas_call(
        body, name=f"rs_join{tag}", in_specs=[ANY], out_specs=ANY, out_shape=jax.ShapeDtypeStruct(t.shape, t.dtype),
        scratch_shapes=[pltpu.SemaphoreType.DMA, pltpu.SemaphoreType.DMA], compiler_params=_params(),
    )(t)


def _rs_local(g, tag):
    return _add_halves(g, _swap_halves(g, name=f"rs_swap{tag}"), tag)


def _rs_finish(s, recv, tag):
    t = _sum_chips(s, recv, tag)
    r = _swap_sibling(t, tag)
    first = lax.axis_index("c") == 0
    return jnp.concatenate([jnp.where(first, t, r), jnp.where(first, r, t)], axis=0)


_BIG = (("w_gate_up", 2), ("w_down", 1), ("fox_w_in", 2), ("fox_w_out", 1), ("gla_w_in", 2), ("gla_w_out", 1),
        ("gdn_w_in", 2), ("gdn_w_out", 1))
_SMALL_SHARDED = (("meta_tokens", 1), ("gla_w_alpha2", 2), ("gdn_conv_w", 3))
_REPLICATED = ("norm_mix", "norm_ffn", "fox_b_f", "fox_q_gain", "fox_k_gain", "gla_b_alpha", "gla_o_gain",
               "gdn_a_log", "gdn_dt_bias", "gdn_o_gain")
_WEIGHTS = ("meta_tokens", "norm_mix", "norm_ffn", "w_gate_up", "w_down", "fox_w_in", "fox_b_f", "fox_q_gain",
            "fox_k_gain", "fox_w_out", "gla_w_in", "gla_w_alpha2", "gla_b_alpha", "gla_o_gain", "gla_w_out",
            "gdn_w_in", "gdn_conv_w", "gdn_a_log", "gdn_dt_bias", "gdn_o_gain", "gdn_w_out")
_PACK_ROWS = 512
_IN_W = ("fox_w_in", "gla_w_in", "gdn_w_in")
_OUT_W = ("fox_w_out", "gla_w_out", "gdn_w_out")


def _piece_rows(n):
    return -(-n // 32) * 32


def _pack(arrays, width, row_mult, dtype):
    flat = jnp.concatenate([a.astype(dtype).reshape(-1) for a in arrays])
    per = width * row_mult
    n = -(-flat.shape[0] // per) * per
    return jnp.pad(flat, (0, n - flat.shape[0])).reshape(n // width, width)


def _unpack(flat, shapes):
    out, off = [], 0
    for s in shapes:
        n = 1
        for d in s:
            n *= d
        out.append(flat[off:off + n].reshape(s))
        off += n
    return out


def _unpack_cols(flat2, shapes):
    out, off = [], 0
    for s in shapes:
        n = 1
        for d in s:
            n *= d
        out.append(flat2[:, off:off + n].reshape((flat2.shape[0],) + tuple(s)))
        off += n
    return out


def _pad_cols(w, n):
    return jnp.pad(w, [(0, 0)] * (w.ndim - 1) + [(0, n - w.shape[-1])])


def kernel(x, meta_tokens, norm_mix, norm_ffn, w_gate_up, w_down, fox_w_in, fox_b_f, fox_q_gain, fox_k_gain, fox_w_out, gla_w_in, gla_w_alpha2, gla_b_alpha, gla_o_gain, gla_w_out, gdn_w_in, gdn_conv_w, gdn_a_log, gdn_dt_bias, gdn_o_gain, gdn_w_out, loss_target, m_meta_tokens, m_norm_mix, m_norm_ffn, m_w_gate_up, m_w_down, m_fox_w_in, m_fox_b_f, m_fox_q_gain, m_fox_k_gain, m_fox_w_out, m_gla_w_in, m_gla_w_alpha2, m_gla_b_alpha, m_gla_o_gain, m_gla_w_out, m_gdn_w_in, m_gdn_conv_w, m_gdn_a_log, m_gdn_dt_bias, m_gdn_o_gain, m_gdn_w_out, v_meta_tokens, v_norm_mix, v_norm_ffn, v_w_gate_up, v_w_down, v_fox_w_in, v_fox_b_f, v_fox_q_gain, v_fox_k_gain, v_fox_w_out, v_gla_w_in, v_gla_w_alpha2, v_gla_b_alpha, v_gla_o_gain, v_gla_w_out, v_gdn_w_in, v_gdn_conv_w, v_gdn_a_log, v_gdn_dt_bias, v_gdn_o_gain, v_gdn_w_out):
    W = dict(meta_tokens=meta_tokens, norm_mix=norm_mix, norm_ffn=norm_ffn, w_gate_up=w_gate_up, w_down=w_down,
             fox_w_in=fox_w_in, fox_b_f=fox_b_f, fox_q_gain=fox_q_gain, fox_k_gain=fox_k_gain, fox_w_out=fox_w_out,
             gla_w_in=gla_w_in, gla_w_alpha2=gla_w_alpha2, gla_b_alpha=gla_b_alpha, gla_o_gain=gla_o_gain,
             gla_w_out=gla_w_out, gdn_w_in=gdn_w_in, gdn_conv_w=gdn_conv_w, gdn_a_log=gdn_a_log,
             gdn_dt_bias=gdn_dt_bias, gdn_o_gain=gdn_o_gain, gdn_w_out=gdn_w_out)
    M = dict(meta_tokens=m_meta_tokens, norm_mix=m_norm_mix, norm_ffn=m_norm_ffn, w_gate_up=m_w_gate_up, w_down=m_w_down,
             fox_w_in=m_fox_w_in, fox_b_f=m_fox_b_f, fox_q_gain=m_fox_q_gain, fox_k_gain=m_fox_k_gain,
             fox_w_out=m_fox_w_out, gla_w_in=m_gla_w_in, gla_w_alpha2=m_gla_w_alpha2, gla_b_alpha=m_gla_b_alpha,
             gla_o_gain=m_gla_o_gain, gla_w_out=m_gla_w_out, gdn_w_in=m_gdn_w_in, gdn_conv_w=m_gdn_conv_w,
             gdn_a_log=m_gdn_a_log, gdn_dt_bias=m_gdn_dt_bias, gdn_o_gain=m_gdn_o_gain, gdn_w_out=m_gdn_w_out)
    V = dict(meta_tokens=v_meta_tokens, norm_mix=v_norm_mix, norm_ffn=v_norm_ffn, w_gate_up=v_w_gate_up, w_down=v_w_down,
             fox_w_in=v_fox_w_in, fox_b_f=v_fox_b_f, fox_q_gain=v_fox_q_gain, fox_k_gain=v_fox_k_gain,
             fox_w_out=v_fox_w_out, gla_w_in=v_gla_w_in, gla_w_alpha2=v_gla_w_alpha2, gla_b_alpha=v_gla_b_alpha,
             gla_o_gain=v_gla_o_gain, gla_w_out=v_gla_w_out, gdn_w_in=v_gdn_w_in, gdn_conv_w=v_gdn_conv_w,
             gdn_a_log=v_gdn_a_log, gdn_dt_bias=v_gdn_dt_bias, gdn_o_gain=v_gdn_o_gain, gdn_w_out=v_gdn_w_out)
    chip = 2 * lax.axis_index("x") + lax.axis_index("y")

    pieces, offs, r = [], {}, FFN_ROWS
    for n in _IN_W:
        nc = W[n].shape[2]
        for l in range(W[n].shape[0]):
            pieces.append(jnp.pad(W[n][l].T.astype(BF16), ((0, _piece_rows(nc) - nc), (0, 0))))
            offs[n, l] = r
            r += _piece_rows(nc)
    for n in _OUT_W:
        for l in range(W[n].shape[0]):
            pieces.append(W[n][l].astype(BF16))
            offs[n, l] = r
            r += W[n].shape[1]
    rows = -(-r // _PACK_ROWS) * _PACK_ROWS
    packed = jnp.concatenate([jnp.swapaxes(w_gate_up, 1, 2).reshape(-1, D).astype(BF16), w_down.reshape(-1, D).astype(BF16)]
                             + pieces + [jnp.zeros((rows - r, D), BF16)], axis=0)
    first_rows = [(offs["fox_w_in", 0], offs["fox_w_in", 1] - offs["fox_w_in", 0]),
                  (offs["fox_w_out", 0], offs["fox_w_out", 1] - offs["fox_w_out", 0])]
    later_rows = [(0, FFN_ROWS), (offs["fox_w_in", 1], offs["fox_w_out", 0] - offs["fox_w_in", 1]),
                  (offs["fox_w_out", 1], r - offs["fox_w_out", 1])]
    wpk = _ag_weights(lax.dynamic_update_slice(lax.empty((4, rows, D), BF16), packed[None], (chip, 0, 0)), first_rows)

    def in_t(buf, n, l, npad):
        nc = W[n].shape[2]
        return jnp.concatenate([buf[q, offs[n, l]:offs[n, l] + nc] for q in range(4)] + [jnp.zeros((npad - 4 * nc, D), BF16)], 0)

    def out_w(buf, n, l):
        return jnp.concatenate([buf[q, offs[n, l]:offs[n, l] + W[n].shape[1]] for q in range(4)], axis=0)

    fox_in0, fox_out0 = in_t(wpk, "fox_w_in", 0, FOX_INP), out_w(wpk, "fox_w_out", 0)
    full = {}
    small = _pack([W[n] for n, _ in _SMALL_SHARDED], LANES, 8, F32)
    small_all = _gather8(small, reduce=False, name="gather_small").reshape(8, -1)
    for (n, ax), seg in zip(_SMALL_SHARDED, _unpack_cols(small_all, [W[n].shape for n, _ in _SMALL_SHARDED])):
        full[n] = jnp.concatenate([seg[2 * q] for q in range(4)], axis=ax)
    fox_in, full["fox_w_out"] = [fox_in0], [fox_out0]
    w_alpha2, conv_w = full["gla_w_alpha2"][0], full["gdn_conv_w"][0]

    h = jnp.concatenate([jnp.zeros((META0, D), F32), full["meta_tokens"], x[0]], axis=0)
    saved = []
    y = _rms_fwd(h, norm_mix[0], name="norm_mix0")
    for i in range(DEPTH):
        kind, j = i % 3, i // 3
        if kind == 0:
            proj = _mm(y, fox_in[j], tb=True, name=f"fox_in{j}")
            qa, ka, va = _fox_prep(proj, fox_b_f[j], fox_q_gain[j], fox_k_gain[j])
            if i == 0:
                o, og, lse, wpk = _fox_attn_fwd(qa, ka, va, proj, ag=(wpk, later_rows))
                wpk = _ag_forward(wpk, later_rows)
                fox_in += [in_t(wpk, "fox_w_in", l, FOX_INP) for l in range(1, fox_w_in.shape[0])]
                full["fox_w_out"] += [out_w(wpk, "fox_w_out", l) for l in range(1, fox_w_out.shape[0])]
                gla_in = [in_t(wpk, "gla_w_in", l, GLA_INP) for l in range(gla_w_in.shape[0])]
                gdn_in = [in_t(wpk, "gdn_w_in", l, GDN_INP) for l in range(gdn_w_in.shape[0])]
                for n in ("gla_w_out", "gdn_w_out"):
                    full[n] = [out_w(wpk, n, l) for l in range(W[n].shape[0])]
            else:
                o, og, lse = _fox_attn_fwd(qa, ka, va, proj)
            w_out, mix = full["fox_w_out"][j], (proj, qa, ka, va, o, lse)
        elif kind == 1:
            proj = _mm(y, gla_in[j], tb=True, name=f"gla_in{j}")
            o, og, states = _gla_fwd(proj, w_alpha2, gla_b_alpha[j], gla_o_gain[j])
            w_out, mix = full["gla_w_out"][j], (proj, o, states)
        else:
            proj = _mm(y, gdn_in[j], tb=True, name=f"gdn_in{j}")
            o, og, states = _gdn_fwd(proj, conv_w, gdn_a_log[j], gdn_dt_bias[j], gdn_o_gain[j])
            w_out, mix = full["gdn_w_out"][j], (proj, o, states)
        hm, yf = _mm(og, w_out, add=h, norm=norm_ffn[i], name=f"mix_out{i}")
        gate, up, act = _ffn_up(yf, wpk, i)
        hn, y_next = _ffn_down(act, wpk, i, hm, norm_mix[(i + 1) % DEPTH])
        saved.append((h, y, mix, og, w_out, hm, yf, gate, up, act))
        h, y = hn, y_next
    dh, loss_tile = _loss_head(h, loss_target[0])

    G = {n: [None] * W[n].shape[0] for n in _WEIGHTS if n not in ("meta_tokens", "w_gate_up", "w_down") + _IN_W}
    GT = {}

    def grad_layout(ffn_layers, pieces):
        off, end = {}, 0
        for l in ffn_layers:
            off["gu", l] = end
            end += GU_ROWS
        for l in ffn_layers:
            off["down", l] = end
            end += DOWN_ROWS
        for n, l in pieces:
            off[n, l] = end
            end += _piece_rows(W[n].shape[2]) if n in _IN_W else W[n].shape[1]
        return off, end, -(-end // _PACK_ROWS) * _PACK_ROWS

    first_pieces = [("fox_w_in", 0)]
    later_pieces = [(n, l) for n in _IN_W + _OUT_W for l in range(W[n].shape[0]) if (n, l) not in first_pieces]
    layouts = [grad_layout([], first_pieces), grad_layout(list(range(DEPTH)), later_pieces)]
    gbuf = [jnp.zeros((4, lay[2], D), BF16) for lay in layouts]

    def with_pieces(buf, lay, pieces):
        off, end, total = lay
        blocks = []
        for q in range(4):
            parts = []
            for n, l in pieces:
                if n in _IN_W:
                    nc = W[n].shape[2]
                    parts.append(jnp.pad(GT[n, l][q * nc:(q + 1) * nc], ((0, _piece_rows(nc) - nc), (0, 0))))
                else:
                    nr = W[n].shape[1]
                    parts.append(G[n][l][q * nr:(q + 1) * nr])
            blocks.append(jnp.concatenate(parts + [jnp.zeros((total - end, D), BF16)], axis=0))
        return lax.dynamic_update_slice(buf, jnp.stack(blocks), (0, off[pieces[0]], 0))

    s_later = None
    for i in reversed(range(DEPTH)):
        kind, j = i % 3, i // 3
        h_in, y, mix, og, w_out, hm, yf, gate, up, act = saved[i]
        b = 1
        dg, du = _ffn_dact(dh, wpk, i, gate, up)
        gbuf[b] = _ffn_dw_down(act, dh, gbuf[b], i, layouts[b][0]["down", i])
        dhm, dnf = _ffn_dyf(dg, du, wpk, i, hm, norm_ffn[i], dh)
        gbuf[b] = _ffn_dw_gu(dg, du, yf, gbuf[b], i, layouts[b][0]["gu", i] // GU_ROWS)
        G["norm_ffn"][i] = dnf[0]
        dog = _mm(dhm, w_out, tb=True, name=f"d_og{i}")
        dw_out = _mm(og, dhm, ta=True, out_dtype=BF16, name=f"d_w_out{i}")
        if kind == 0:
            proj, qa, ka, va, o, lse = mix
            doa, q2, dgate = _fox_gate_bwd(dog, o, proj, lse, qa)
            G["fox_w_out"][j] = dw_out
            if i == 0:
                s_later = _rs_local(with_pieces(gbuf[1], layouts[1], later_pieces), "_later")
                dqn, dkn, dv, dct, recv_later = _fox_attn_bwd(q2, ka, va, doa, rs=s_later)
            else:
                dqn, dkn, dv, dct = _fox_attn_bwd(q2, ka, va, doa)
            dproj, dqg, dkg, dbf = _fox_prep_bwd(proj, fox_b_f[j], fox_q_gain[j], fox_k_gain[j], dqn, dkn, dv, dgate, dct)
            G["fox_q_gain"][j] = dqg.reshape(FOX_H, FOX_DH).sum(0)
            G["fox_k_gain"][j] = dkg.reshape(FOX_H, FOX_DH).sum(0)
            G["fox_b_f"][j] = dbf[0, :FOX_H]
            w_in, wname = fox_in[j], "fox_w_in"
        elif kind == 1:
            proj, o, states = mix
            dproj, dwa, dba, dogain = _gla_bwd(proj, w_alpha2, gla_b_alpha[j], gla_o_gain[j], o, states, dog)
            G["gla_w_out"][j] = dw_out
            G["gla_w_alpha2"][j] = dwa[:GLA_RANK]
            G["gla_b_alpha"][j] = dba[0]
            G["gla_o_gain"][j] = dogain[0]
            w_in, wname = gla_in[j], "gla_w_in"
        else:
            proj, o, states = mix
            dproj, dcw, dal, ddt, dogain = _gdn_bwd(proj, conv_w, gdn_a_log[j], gdn_dt_bias[j], gdn_o_gain[j], o, states, dog)
            G["gdn_w_out"][j] = dw_out
            G["gdn_conv_w"][j] = dcw[:4].reshape(4, 1, GDN_CONV)
            G["gdn_a_log"][j] = dal[0, :GDN_H]
            G["gdn_dt_bias"][j] = ddt[0, :GDN_H]
            G["gdn_o_gain"][j] = dogain[0]
            w_in, wname = gdn_in[j], "gdn_w_in"
        dh, dnm = _mm(dproj, w_in, rms_bwd=(h_in, norm_mix[i], dhm), name=f"d_y{i}")
        GT[wname, j] = _mm(dproj, y, ta=True, out_dtype=BF16, name=f"d_w_in{i}")
        G["norm_mix"][i] = dnm[0]
    grad_x = dh[ROW0:][None]
    G = {n: (v if n in _OUT_W else jnp.stack(v)) for n, v in G.items()}
    G["meta_tokens"] = dh[META0:ROW0]

    s_first = _rs_local(with_pieces(gbuf[0], layouts[0], first_pieces), "_first")
    reduced = [_rs_finish(s_first, _scatter_chips(s_first, "_first"), "_first"), _rs_finish(s_later, recv_later, "_later")]

    def reduced_piece(n, l):
        b = 0 if (n, l) in first_pieces else 1
        start = layouts[b][0][n, l]
        return reduced[b][start:start + (W[n].shape[2] if n in _IN_W else W[n].shape[1])]

    grads = {}
    for n in _IN_W:
        grads[n] = jnp.stack([reduced_piece(n, l).T for l in range(W[n].shape[0])])
    for n in _OUT_W:
        grads[n] = jnp.stack([reduced_piece(n, l) for l in range(W[n].shape[0])])
    small_names = [n for n, _ in _SMALL_SHARDED] + list(_REPLICATED)
    small_g = _pack([G[n] for n in small_names] + [loss_tile[0, 0:1]], LANES, 8, F32)
    small_sum = _gather8(small_g, reduce=True, name="allreduce_small").reshape(-1)
    small_shapes = [G[n].shape for n in small_names] + [(1,)]
    small_vals = _unpack(small_sum, small_shapes)
    loss = small_vals[-1][0]
    for n, val in zip(small_names, small_vals[:-1]):
        grads[n] = val
    for n, ax in _SMALL_SHARDED:
        sz = W[n].shape[ax]
        grads[n] = lax.dynamic_slice_in_dim(grads[n], chip * sz, sz, axis=ax)

    delta, new_m, new_v = {}, {}, {}
    for n, key, tr_ in (("w_gate_up", "gu", True), ("w_down", "down", False)):
        grads[n], delta[n], new_m[n], new_v[n] = _adamw_packed(
            W[n], reduced[1], reduced[1], M[n], V[n], row0=layouts[1][0][key, 0], row_off=layouts[1][0][key, 1],
            transposed=tr_, name=f"adamw_{n}")
    for n in _IN_W + _OUT_W:
        delta[n], new_m[n], new_v[n] = _adamw(W[n], grads[n], M[n], V[n], name=f"adamw_{n}")
    tiny = [n for n in _WEIGHTS if n not in dict(_BIG)]
    packs = [_pack([T[n] for n in tiny], LANES, 8, F32) for T in (W, grads, M, V)]
    outs = _adamw(*packs, name="adamw_small")
    shapes = [W[n].shape for n in tiny]
    for dst, o in zip((delta, new_m, new_v), outs):
        for n, val in zip(tiny, _unpack(o.reshape(-1), shapes)):
            dst[n] = val
    return (loss, grad_x, *[grads[n] for n in _WEIGHTS], *[delta[n] for n in _WEIGHTS],
            *[new_m[n] for n in _WEIGHTS], *[new_v[n] for n in _WEIGHTS])
```

```python
import jax
import jax.numpy as jnp
from jax import lax
from jax.experimental import pallas as pl
from jax.experimental.pallas import tpu as pltpu

F32, BF16 = jnp.float32, jnp.bfloat16
D = 1024
N_META = 16
ROW0 = 128
META0 = ROW0 - N_META
EPS = 1e-6
LANES = 128
VMEM_LIMIT = 56 * 1024 * 1024

FOX_H, FOX_DH = 16, 64
FOX_INP = 4224
GLA_H, GLA_DK, GLA_DV, GLA_RANK = 4, 128, 256, 16
GLA_QK, GLA_V = 512, 1024
GLA_INP = 3200
GLA_NORM = 16.0
GDN_H, GDN_DK, GDN_DV = 8, 128, 128
GDN_CONV = 3072
GDN_INP = 4224
CHUNK = 64
D_FF = 2816
DEPTH = 4

ADAM_LR, ADAM_B1, ADAM_B2, ADAM_EPS, ADAM_WD, ADAM_STEP = 0.001, 0.9, 0.999, 1e-08, 0.01, 10

MESH = pl.DeviceIdType.MESH
ANY = pl.BlockSpec(memory_space=pl.ANY)
VM = pl.BlockSpec(memory_space=pltpu.VMEM)


def _params(sem=None, **kw):
    if sem is not None:
        kw["dimension_semantics"] = sem
    return pltpu.CompilerParams(vmem_limit_bytes=VMEM_LIMIT, **kw)


def _tile(n, cap, mult=LANES):
    best = None
    for t in range(mult, min(n, cap) + 1, mult):
        if n % t == 0:
            best = t
    return best if best is not None else n


def nn(a, b, **kw):
    return jnp.dot(a, b, preferred_element_type=F32, **kw)


def nt(a, b, **kw):
    return lax.dot_general(a, b, (((1,), (1,)), ((), ())), preferred_element_type=F32, **kw)


def tn(a, b, **kw):
    return lax.dot_general(a, b, (((0,), (0,)), ((), ())), preferred_element_type=F32, **kw)


def _split3(x):
    hi = x.astype(BF16)
    r = x - hi.astype(F32)
    mid = r.astype(BF16)
    lo = (r - mid.astype(F32)).astype(BF16)
    return hi, mid, lo


def _sel_l(sel, x):
    a, b, c = _split3(x)
    return nn(sel, a) + nn(sel, b) + nn(sel, c)


def _sel_r(x, sel):
    a, b, c = _split3(x)
    return nn(a, sel) + nn(b, sel) + nn(c, sel)


def _sel_r2(x, sel):
    a = x.astype(BF16)
    return nn(a, sel) + nn((x - a.astype(F32)).astype(BF16), sel)


def _iota(shape, dim):
    return lax.broadcasted_iota(jnp.int32, shape, dim)


def _tri(n, upper=False, strict=False):
    i, j = _iota((n, n), 0), _iota((n, n), 1)
    if upper:
        m = (j > i) if strict else (j >= i)
    else:
        m = (j < i) if strict else (j <= i)
    return m


def _sigmoid(x):
    return 1.0 / (1.0 + jnp.exp(-x))


def _log_sigmoid(x):
    return jnp.minimum(x, 0.0) - jnp.log(1.0 + jnp.exp(-jnp.abs(x)))


def _softplus(x):
    return jnp.maximum(x, 0.0) + jnp.log(1.0 + jnp.exp(-jnp.abs(x)))


def _silu(x):
    return x * _sigmoid(x)


def _dsilu(x):
    s = _sigmoid(x)
    return s * (1.0 + x * (1.0 - s))


def _rms(x, g):
    return (x * lax.rsqrt(jnp.mean(x * x, axis=-1, keepdims=True) + EPS) * g).astype(BF16)


def _rms_grad(x, g, dy):
    r = lax.rsqrt(jnp.mean(x * x, axis=-1, keepdims=True) + EPS)
    u = dy * g
    return r * u - x * (r * r * r) * jnp.mean(x * u, axis=-1, keepdims=True), jnp.sum(dy * x * r, axis=0, keepdims=True)


def _mm(a, b, *, ta=False, tb=False, add=None, norm=None, rms_bwd=None, out_dtype=F32, name):
    m, k = (a.shape[1], a.shape[0]) if ta else a.shape
    n = b.shape[0] if tb else b.shape[1]
    assert k == (b.shape[1] if tb else b.shape[0])
    rows_whole = norm is not None or rms_bwd is not None
    tm, tn_, tk = _tile(m, 704 if rows_whole else 1408, LANES if ta else 16), _tile(n, 1408), _tile(k, 1408)
    nk = k // tk
    assert not rows_whole or tn_ == n

    def body(*refs):
        refs = list(refs)
        a_ref, b_ref = refs[:2]
        extra = refs[2:-1]
        acc = refs[-1]
        i, kk = pl.program_id(0), pl.program_id(2)

        @pl.when(kk == 0)
        def _():
            acc[...] = jnp.zeros_like(acc)

        av, bv = a_ref[...].astype(BF16), b_ref[...].astype(BF16)
        dims = (((0,) if ta else (1,), (1,) if tb else (0,)), ((), ()))
        acc[...] += lax.dot_general(av, bv, dims, preferred_element_type=F32)

        @pl.when(kk == nk - 1)
        def _():
            r = acc[...]
            if rms_bwd is not None:
                h_ref, g_ref, dres_ref, o_ref, dg_ref = extra
                dx, dgain = _rms_grad(h_ref[...], g_ref[...], r)
                o_ref[...] = dres_ref[...] + dx

                @pl.when(i == 0)
                def _():
                    dg_ref[...] = jnp.zeros_like(dg_ref)

                dg_ref[...] += dgain
                return
            if add is not None:
                r = r + extra[0][...].astype(F32)
            if norm is not None:
                g_ref, o_ref, y_ref = extra[-3:]
                y_ref[...] = _rms(r, g_ref[...])
            else:
                o_ref = extra[-1]
            o_ref[...] = r.astype(out_dtype)

    a_spec = pl.BlockSpec((tk, tm), lambda i, j, q: (q, i)) if ta else pl.BlockSpec((tm, tk), lambda i, j, q: (i, q))
    b_spec = pl.BlockSpec((tn_, tk), lambda i, j, q: (j, q)) if tb else pl.BlockSpec((tk, tn_), lambda i, j, q: (q, j))
    o_spec = pl.BlockSpec((tm, tn_), lambda i, j, q: (i, j))
    g_spec = pl.BlockSpec((1, n), lambda i, j, q: (0, 0))
    ins, specs = [a, b], [a_spec, b_spec]
    out_specs, out_shape = o_spec, jax.ShapeDtypeStruct((m, n), out_dtype)
    sem = ("parallel", "parallel", "arbitrary")
    if rms_bwd is not None:
        ins += [rms_bwd[0], rms_bwd[1].reshape(1, n), rms_bwd[2]]
        specs += [o_spec, g_spec, o_spec]
        out_specs, out_shape = [o_spec, g_spec], [jax.ShapeDtypeStruct((m, n), F32), jax.ShapeDtypeStruct((1, n), F32)]
        sem = ("arbitrary", "arbitrary", "arbitrary")
    else:
        if add is not None:
            ins.append(add)
            specs.append(o_spec)
        if norm is not None:
            ins.append(norm.reshape(1, n))
            specs.append(g_spec)
            out_specs, out_shape = [o_spec, o_spec], [out_shape, jax.ShapeDtypeStruct((m, n), BF16)]
    return pl.pallas_call(
        body, name=name, grid=(m // tm, n // tn_, nk), in_specs=specs, out_specs=out_specs, out_shape=out_shape,
        scratch_shapes=[pltpu.VMEM((tm, tn_), F32)], compiler_params=_params(sem),
    )(*ins)


def _rms_fwd(h, g, *, name):
    lp = h.shape[0]
    tr = _tile(lp, 512)

    def body(h_ref, g_ref, y_ref):
        x = h_ref[...]
        r = lax.rsqrt(jnp.mean(x * x, axis=-1, keepdims=True) + EPS)
        y_ref[...] = (x * r * g_ref[...]).astype(BF16)

    return pl.pallas_call(
        body, name=name, grid=(lp // tr,),
        in_specs=[pl.BlockSpec((tr, D), lambda i: (i, 0)), pl.BlockSpec((1, D), lambda i: (0, 0))],
        out_specs=pl.BlockSpec((tr, D), lambda i: (i, 0)),
        out_shape=jax.ShapeDtypeStruct((lp, D), BF16), compiler_params=_params(("parallel",)),
    )(h, g.reshape(1, D))


GU_ROWS, DOWN_ROWS = 1408, 704
OFF_GU, OFF_DOWN = 0, DEPTH * GU_ROWS
FFN_ROWS = DEPTH * (GU_ROWS + DOWN_ROWS)
FFN_TM = 704


def _gu_spec(fn):
    return pl.BlockSpec((None, GU_ROWS, D), fn)


def _down_spec(fn):
    return pl.BlockSpec((None, DOWN_ROWS, D), fn)


def _down_pair(w0_ref, w1_ref):
    return jnp.concatenate([w0_ref[...], w1_ref[...]], axis=0)


def _ffn_up(yf, wpk, layer):
    lp = yf.shape[0]
    tm = _tile(lp, FFN_TM, 16)

    def body(y_ref, wg_ref, wu_ref, g_ref, u_ref, a_ref):
        y = y_ref[...]
        g, u = nt(y, wg_ref[...]), nt(y, wu_ref[...])
        g_ref[...] = g.astype(BF16)
        u_ref[...] = u.astype(BF16)
        a_ref[...] = (_silu(g) * u).astype(BF16)

    o = pl.BlockSpec((tm, GU_ROWS), lambda i, j: (i, j))
    return pl.pallas_call(
        body, name=f"ffn_up{layer}", grid=(lp // tm, 2),
        in_specs=[pl.BlockSpec((tm, D), lambda i, j: (i, 0)), _gu_spec(lambda i, j: (j, OFF_GU // GU_ROWS + layer, 0)),
                  _gu_spec(lambda i, j: (2 + j, OFF_GU // GU_ROWS + layer, 0))],
        out_specs=[o, o, o], out_shape=[jax.ShapeDtypeStruct((lp, D_FF), BF16)] * 3,
        compiler_params=_params(("parallel", "parallel")),
    )(yf, wpk, wpk)


def _ffn_down(act, wpk, layer, res, norm):
    lp = act.shape[0]
    tm = _tile(lp, FFN_TM, 16)

    def body(a_ref, w0_ref, w1_ref, r_ref, g_ref, o_ref, y_ref, acc):
        kk = pl.program_id(1)

        @pl.when(kk == 0)
        def _():
            acc[...] = r_ref[...]

        acc[...] += nn(a_ref[...], _down_pair(w0_ref, w1_ref))

        @pl.when(kk == 1)
        def _():
            o_ref[...] = acc[...]
            y_ref[...] = _rms(acc[...], g_ref[...])

    o = pl.BlockSpec((tm, D), lambda i, kk: (i, 0))
    blk = OFF_DOWN // DOWN_ROWS + layer
    return pl.pallas_call(
        body, name=f"ffn_down{layer}", grid=(lp // tm, 2),
        in_specs=[pl.BlockSpec((tm, GU_ROWS), lambda i, kk: (i, kk)), _down_spec(lambda i, kk: (2 * kk, blk, 0)),
                  _down_spec(lambda i, kk: (2 * kk + 1, blk, 0)), o, pl.BlockSpec((1, D), lambda i, kk: (0, 0))],
        out_specs=[o, o], out_shape=[jax.ShapeDtypeStruct((lp, D), F32), jax.ShapeDtypeStruct((lp, D), BF16)],
        scratch_shapes=[pltpu.VMEM((tm, D), F32)], compiler_params=_params(("parallel", "arbitrary")),
    )(act, wpk, wpk, res, norm.reshape(1, D))


def _ffn_dact(dh, wpk, layer, gate, up):
    lp = dh.shape[0]
    tm = _tile(lp, FFN_TM, 16)

    def body(d_ref, w0_ref, w1_ref, g_ref, u_ref, dg_ref, du_ref):
        da = nt(d_ref[...].astype(BF16), _down_pair(w0_ref, w1_ref))
        g, u = g_ref[...].astype(F32), u_ref[...].astype(F32)
        sg = _sigmoid(g)
        dg_ref[...] = (da * u * (sg * (1.0 + g * (1.0 - sg)))).astype(BF16)
        du_ref[...] = (da * (g * sg)).astype(BF16)

    o = pl.BlockSpec((tm, GU_ROWS), lambda i, j: (i, j))
    blk = OFF_DOWN // DOWN_ROWS + layer
    return pl.pallas_call(
        body, name=f"d_act{layer}", grid=(lp // tm, 2),
        in_specs=[pl.BlockSpec((tm, D), lambda i, j: (i, 0)), _down_spec(lambda i, j: (2 * j, blk, 0)),
                  _down_spec(lambda i, j: (2 * j + 1, blk, 0)), o, o],
        out_specs=[o, o], out_shape=[jax.ShapeDtypeStruct((lp, D_FF), BF16)] * 2,
        compiler_params=_params(("parallel", "parallel")),
    )(dh, wpk, wpk, gate, up)


def _ffn_dyf(dg, du, wpk, layer, hm, norm, dres):
    lp = dg.shape[0]
    tm = _tile(lp, FFN_TM, 16)

    def body(dg_ref, du_ref, w_ref, h_ref, g_ref, dres_ref, o_ref, dgain_ref, acc):
        i, kk = pl.program_id(0), pl.program_id(1)

        @pl.when(kk == 0)
        def _():
            acc[...] = jnp.zeros_like(acc)

        @pl.when(kk < 2)
        def _():
            acc[...] += nn(dg_ref[...], w_ref[...])

        @pl.when(kk >= 2)
        def _():
            acc[...] += nn(du_ref[...], w_ref[...])

        @pl.when(kk == 3)
        def _():
            dx, dgain = _rms_grad(h_ref[...], g_ref[...], acc[...])
            o_ref[...] = dres_ref[...] + dx

            @pl.when(i == 0)
            def _():
                dgain_ref[...] = jnp.zeros_like(dgain_ref)

            dgain_ref[...] += dgain

    o = pl.BlockSpec((tm, D), lambda i, kk: (i, 0))
    row = pl.BlockSpec((1, D), lambda i, kk: (0, 0))
    return pl.pallas_call(
        body, name=f"d_yf{layer}", grid=(lp // tm, 4),
        in_specs=[pl.BlockSpec((tm, GU_ROWS), lambda i, kk: (i, jnp.minimum(kk, 1))),
                  pl.BlockSpec((tm, GU_ROWS), lambda i, kk: (i, jnp.maximum(kk - 2, 0))),
                  _gu_spec(lambda i, kk: (kk, OFF_GU // GU_ROWS + layer, 0)), o, row, o],
        out_specs=[o, row], out_shape=[jax.ShapeDtypeStruct((lp, D), F32), jax.ShapeDtypeStruct((1, D), F32)],
        scratch_shapes=[pltpu.VMEM((tm, D), F32)], compiler_params=_params(("arbitrary", "arbitrary")),
    )(dg, du, wpk, hm, norm.reshape(1, D), dres)


def _ffn_dw_down(act, dh, gpk, layer, row):
    lp = act.shape[0]
    tk = _tile(lp, 1408, 16)
    nk = lp // tk

    def body(a_ref, d_ref, g_in, g_out, acc, stage, sems):
        jp, kk = pl.program_id(0), pl.program_id(1)

        @pl.when(kk == 0)
        def _():
            acc[...] = jnp.zeros_like(acc)

        acc[...] += tn(a_ref[...], d_ref[...].astype(BF16))

        @pl.when(kk == nk - 1)
        def _():
            stage[...] = acc[...].astype(BF16)
            copies = [pltpu.make_async_copy(stage.at[pl.ds(hf * DOWN_ROWS, DOWN_ROWS), :],
                                            g_out.at[2 * jp + hf, pl.ds(row, DOWN_ROWS), :], sems.at[hf]) for hf in range(2)]
            for cp in copies:
                cp.start()
            for cp in copies:
                cp.wait()

    return pl.pallas_call(
        body, name=f"d_w_down{layer}", grid=(2, nk),
        in_specs=[pl.BlockSpec((tk, GU_ROWS), lambda jp, kk: (kk, jp)), pl.BlockSpec((tk, D), lambda jp, kk: (kk, 0)), ANY],
        out_specs=ANY, out_shape=jax.ShapeDtypeStruct(gpk.shape, gpk.dtype),
        scratch_shapes=[pltpu.VMEM((GU_ROWS, D), F32), pltpu.VMEM((GU_ROWS, D), BF16), pltpu.SemaphoreType.DMA((2,))],
        input_output_aliases={2: 0}, compiler_params=_params(("arbitrary", "arbitrary")),
    )(act, dh, gpk)


def _ffn_dw_gu(dg, du, yf, gpk, layer, blk):
    lp = dg.shape[0]
    tk = _tile(lp, 1408, 16)
    nk = lp // tk

    def body(dg_ref, du_ref, y_ref, g_in, o_ref, acc):
        c, kk = pl.program_id(0), pl.program_id(1)

        @pl.when(kk == 0)
        def _():
            acc[...] = jnp.zeros_like(acc)

        @pl.when(c < 2)
        def _():
            acc[...] += tn(dg_ref[...], y_ref[...])

        @pl.when(c >= 2)
        def _():
            acc[...] += tn(du_ref[...], y_ref[...])

        @pl.when(kk == nk - 1)
        def _():
            o_ref[...] = acc[...].astype(BF16)

    return pl.pallas_call(
        body, name=f"d_w_gate_up{layer}", grid=(4, nk),
        in_specs=[pl.BlockSpec((tk, GU_ROWS), lambda c, kk: (kk, jnp.minimum(c, 1))),
                  pl.BlockSpec((tk, GU_ROWS), lambda c, kk: (kk, jnp.maximum(c - 2, 0))),
                  pl.BlockSpec((tk, D), lambda c, kk: (kk, 0)), ANY],
        out_specs=_gu_spec(lambda c, kk: (c, blk, 0)),
        out_shape=jax.ShapeDtypeStruct(gpk.shape, gpk.dtype),
        scratch_shapes=[pltpu.VMEM((GU_ROWS, D), F32)], input_output_aliases={3: 0},
        compiler_params=_params(("parallel", "arbitrary")),
    )(dg, du, yf, gpk)


def _loss_head(h, target):
    lp = h.shape[0]
    nb = lp // ROW0

    def body(h_ref, t_ref, dh_ref, l_ref):
        i = pl.program_id(0)

        @pl.when(i == 0)
        def _():
            l_ref[...] = jnp.zeros_like(l_ref)
            dh_ref[...] = jnp.zeros_like(dh_ref)

        @pl.when(i > 0)
        def _():
            err = h_ref[...] - t_ref[...]
            dh_ref[...] = err * (1.0 / D)
            l_ref[...] += jnp.sum(err * err) * (0.5 / D)

    return pl.pallas_call(
        body, name="loss_head", grid=(nb,),
        in_specs=[pl.BlockSpec((ROW0, D), lambda i: (i, 0)), pl.BlockSpec((ROW0, D), lambda i: (jnp.maximum(i - 1, 0), 0))],
        out_specs=[pl.BlockSpec((ROW0, D), lambda i: (i, 0)), pl.BlockSpec((8, LANES), lambda i: (0, 0))],
        out_shape=[jax.ShapeDtypeStruct((lp, D), F32), jax.ShapeDtypeStruct((8, LANES), F32)],
        compiler_params=_params(("arbitrary",)),
    )(h, target)


def _adamw(w, g, m, v, *, name):
    if w.ndim == 2:
        w, g, m, v = (t[None] for t in (w, g, m, v))
        return tuple(o[0] for o in _adamw(w, g, m, v, name=name))
    nl, r, c = w.shape
    tr = _tile(r, max(8, (1 << 19) // c), 8)

    def body(w_ref, g_ref, m_ref, v_ref, d_ref, nm_ref, nv_ref):
        d_ref[...], nm_ref[...], nv_ref[...] = _adam_math(w_ref[...], g_ref[...], m_ref[...], v_ref[...])

    spec = pl.BlockSpec((1, tr, c), lambda l, i: (l, i, 0))
    return tuple(pl.pallas_call(
        body, name=name, grid=(nl, r // tr), in_specs=[spec] * 4, out_specs=[spec] * 3,
        out_shape=[jax.ShapeDtypeStruct(w.shape, F32)] * 3, compiler_params=_params(("parallel", "parallel")),
    )(w, g, m, v))


def _adam_math(w, g, m, v):
    nm = ADAM_B1 * m + (1.0 - ADAM_B1) * g
    nv = ADAM_B2 * v + (1.0 - ADAM_B2) * (g * g)
    m_hat = nm / (1.0 - ADAM_B1 ** ADAM_STEP)
    v_hat = nv / (1.0 - ADAM_B2 ** ADAM_STEP)
    return -ADAM_LR * (m_hat / (jnp.sqrt(v_hat) + ADAM_EPS) + ADAM_WD * w), nm, nv


def _adamw_packed(w, gred0, gred, m, v, *, row0, row_off, transposed, name):
    nl, a, b = w.shape
    nr = b if transposed else a
    later = lambda l: row_off // nr + jnp.maximum(l - 1, 0)
    if transposed:
        ta = _tile(a, 256)
        wspec = pl.BlockSpec((1, ta, b), lambda l, r: (l, r, 0))
        g0spec = pl.BlockSpec((b, ta), lambda l, r: (row0 // nr, r))
        gspec = pl.BlockSpec((b, ta), lambda l, r: (later(l), r))
        grid = (nl, a // ta)
    else:
        wspec = pl.BlockSpec((1, a, b), lambda l, r: (l, 0, 0))
        g0spec = pl.BlockSpec((a, b), lambda l, r: (row0 // nr, 0))
        gspec = pl.BlockSpec((a, b), lambda l, r: (later(l), 0))
        grid = (nl, 1)

    def body(w_ref, g0_ref, g_ref, m_ref, v_ref, go_ref, d_ref, nm_ref, nv_ref):
        g = jnp.where(pl.program_id(0) == 0, g0_ref[...], g_ref[...])
        g = g.T if transposed else g
        d, nm, nv = _adam_math(w_ref[0], g, m_ref[0], v_ref[0])
        go_ref[0], d_ref[0], nm_ref[0], nv_ref[0] = g, d, nm, nv

    return pl.pallas_call(
        body, name=name, grid=grid, in_specs=[wspec, g0spec, gspec, wspec, wspec], out_specs=[wspec] * 4,
        out_shape=[jax.ShapeDtypeStruct(w.shape, F32)] * 4, compiler_params=_params(("parallel", "parallel")),
    )(w, gred0, gred, m, v)


FOX_AUG = FOX_H * LANES
L_C = 64
L_K = 67
L_LSE = 70
PAD_KEY = -30000.0
FOX_TQ = 384


def _head_sel(n_heads, width, lanes=LANES):
    r, c = _iota((n_heads * width, lanes), 0), _iota((n_heads * width, lanes), 1)
    down = (r // width == c).astype(BF16)
    r2, c2 = _iota((lanes, n_heads * width), 0), _iota((lanes, n_heads * width), 1)
    up = (c2 // width == r2).astype(BF16)
    return down, up


def _place(lane0):
    r, c = _iota((LANES, FOX_AUG), 0), _iota((LANES, FOX_AUG), 1)
    return [((c // LANES == r) & (c % LANES == lane0 + m)).astype(BF16) for m in range(3)]


def _placed(x, lane0):
    pcs = _split3(x)
    mats = _place(lane0)
    return nn(pcs[0], mats[0]) + nn(pcs[1], mats[1]) + nn(pcs[2], mats[2])


def _ones_at(rows, lanes):
    c = _iota((rows, FOX_AUG), 1) % LANES
    m = c == lanes[0]
    for l in lanes[1:]:
        m = m | (c == l)
    return m.astype(F32)


def _spread(x, extras, out_ref):
    rows = x.shape[0]
    left = _iota((rows, LANES), 1) < FOX_DH
    for p in range(FOX_H // 2):
        slab = x[:, p * LANES:(p + 1) * LANES]
        a = jnp.where(left, slab, extras[:, 2 * p * LANES:(2 * p + 1) * LANES])
        b = jnp.where(left, pltpu.roll(slab, FOX_DH, 1), extras[:, (2 * p + 1) * LANES:(2 * p + 2) * LANES])
        out_ref[:, 2 * p * LANES:(2 * p + 1) * LANES] = a.astype(BF16)
        out_ref[:, (2 * p + 1) * LANES:(2 * p + 2) * LANES] = b.astype(BF16)


def _fox_prep(proj, b_f, q_gain, k_gain):
    lp = proj.shape[0]
    nb = lp // LANES

    def body(p_ref, bf_ref, qg_ref, kg_ref, q_ref, k_ref, v_ref, carry):
        i = pl.program_id(0)

        @pl.when(i == 0)
        def _():
            carry[...] = jnp.zeros_like(carry)

        down, up = _head_sel(FOX_H, FOX_DH)

        def normed(x, gain):
            ms = _sel_r2(x * x, down) * (1.0 / FOX_DH)
            r = _sel_r2(lax.rsqrt(ms + EPS), up)
            return x * r * gain

        lane = _iota((LANES, LANES), 1)
        lf = jnp.where(lane < FOX_H, _log_sigmoid(p_ref[:, 4 * D:4 * D + LANES] + bf_ref[...]), 0.0)
        c = _sel_l(_tri(LANES).astype(BF16), lf) + carry[0:1, :]
        carry[...] = jnp.broadcast_to(c[LANES - 1:LANES, :], carry.shape)
        q_extra = _placed(c, L_C) + _ones_at(LANES, (L_K, L_K + 1, L_K + 2))
        row = i * LANES + _iota((LANES, FOX_AUG), 0)
        lane_a = _iota((LANES, FOX_AUG), 1) % LANES
        k_extra = -_placed(c, L_K) + _ones_at(LANES, (L_C, L_C + 1, L_C + 2, L_LSE, L_LSE + 1, L_LSE + 2))
        pad_val = jnp.where(lane_a == L_K, PAD_KEY, 0.0)
        k_extra = jnp.where((row < META0) & (lane_a >= L_K) & (lane_a < L_K + 3), pad_val, k_extra)
        v_extra = _ones_at(LANES, (L_C, L_C + 1, L_C + 2))
        _spread(normed(p_ref[:, 0:D], qg_ref[...]) * (FOX_DH ** -0.5), q_extra, q_ref)
        _spread(normed(p_ref[:, D:2 * D], kg_ref[...]), k_extra, k_ref)
        _spread(p_ref[:, 2 * D:3 * D], v_extra, v_ref)

    row = pl.BlockSpec((1, D), lambda i: (0, 0))
    aug = pl.BlockSpec((LANES, FOX_AUG), lambda i: (i, 0))
    return pl.pallas_call(
        body, name="fox_prep", grid=(nb,),
        in_specs=[pl.BlockSpec((LANES, FOX_INP), lambda i: (i, 0)), pl.BlockSpec((1, LANES), lambda i: (0, 0)), row, row],
        out_specs=[aug] * 3, out_shape=[jax.ShapeDtypeStruct((lp, FOX_AUG), BF16)] * 3,
        scratch_shapes=[pltpu.VMEM((8, LANES), F32)],
        compiler_params=_params(("arbitrary",)),
    )(proj, jnp.pad(b_f, (0, LANES - FOX_H)).reshape(1, LANES), jnp.tile(q_gain, FOX_H).reshape(1, D),
      jnp.tile(k_gain, FOX_H).reshape(1, D))


def _fox_attn_fwd(qa, ka, va, proj, ag=None):
    lp = qa.shape[0]
    tq = _tile(lp, FOX_TQ)
    nq = lp // tq
    npair = FOX_H // 2

    def body(q_ref, k_ref, v_ref, gate_ref, *rest):
        if ag is None:
            o_ref, og_ref, lse_ref = rest
        else:
            _, o_ref, og_ref, lse_ref, w_out, send_sems, recv_sems = rest
            copies = _AgCopies(w_out, ag[1], send_sems, recv_sems)

            @pl.when((pl.program_id(0) == 0) & (pl.program_id(1) == 0))
            def _():
                for r, k in copies.pairs():
                    copies.ici(r, k).start()

        i = pl.program_id(1)
        causal = _iota((tq, tq), 1) <= _iota((tq, tq), 0)
        qs = [q_ref[:, hh * LANES:(hh + 1) * LANES] for hh in range(2)]

        def block(j, carry, diag):
            off = pl.multiple_of(j * tq, tq)
            out = []
            for hh in range(2):
                m, acc = carry[hh]
                k = k_ref[pl.ds(off, tq), hh * LANES:(hh + 1) * LANES]
                v = v_ref[pl.ds(off, tq), hh * LANES:(hh + 1) * LANES]
                s = nt(qs[hh], k)
                if diag:
                    s = jnp.where(causal, s, -1e30)
                m2 = jnp.maximum(m, jnp.max(s, axis=-1, keepdims=True))
                p = jnp.exp(s - m2)
                p_hi = p.astype(BF16)
                p_lo = (p - p_hi.astype(F32)).astype(BF16)
                out.append((m2, jnp.exp(m - m2) * acc + nn(p_hi, v) + nn(p_lo, v)))
            return tuple(out)

        init = tuple((jnp.full((tq, 1), -1e30, F32), jnp.zeros((tq, LANES), F32)) for _ in range(2))
        carry = lax.fori_loop(0, i // 2, lambda j, c: block(2 * j + 1, block(2 * j, c, False), False), init)
        carry = lax.cond(i % 2 == 1, lambda c: block(i - 1, c, False), lambda c: c, carry)
        carry = block(i, carry, True)
        outs, lses = [], []
        for hh in range(2):
            m, acc = carry[hh]
            l = acc[:, L_C:L_C + 1]
            outs.append(acc / l)
            lses.append(jnp.broadcast_to(m + jnp.log(l), (tq, LANES)))
        left = _iota((tq, LANES), 1) < FOX_DH
        o = jnp.where(left, outs[0], pltpu.roll(outs[1], FOX_DH, 1))
        o_ref[...] = o
        og_ref[...] = (o * _sigmoid(gate_ref[...])).astype(BF16)
        lse_ref[...] = jnp.where(left, lses[0], lses[1])

        if ag is not None:
            @pl.when((pl.program_id(0) == npair - 1) & (pl.program_id(1) == nq - 1))
            def _():
                for r, k in copies.pairs():
                    copies.ici_arrival(r, k).wait_recv()
                for r, k in copies.pairs():
                    copies.ici(r, k).wait_send()

    qspec = pl.BlockSpec((tq, 2 * LANES), lambda p, i: (i, p))
    kspec = pl.BlockSpec((lp, 2 * LANES), lambda p, i: (0, p))
    ospec = pl.BlockSpec((tq, LANES), lambda p, i: (i, p))
    ins, in_specs = [qa, ka, va, proj], [qspec, kspec, kspec, pl.BlockSpec((tq, LANES), lambda p, i: (i, 3 * D // LANES + p))]
    out_specs = [ospec] * 3
    out_shape = [jax.ShapeDtypeStruct((lp, D), F32), jax.ShapeDtypeStruct((lp, D), BF16), jax.ShapeDtypeStruct((lp, D), F32)]
    if ag is None:
        return pl.pallas_call(body, name="fox_attn_fwd", grid=(npair, nq), in_specs=in_specs, out_specs=out_specs,
                              out_shape=out_shape, compiler_params=_params(("parallel", "arbitrary")))(*ins)
    n = 3 * len(ag[1])
    return pl.pallas_call(
        body, name="fox_attn_fwd_ag", grid=(npair, nq), in_specs=in_specs + [ANY], out_specs=out_specs + [ANY],
        out_shape=out_shape + [jax.ShapeDtypeStruct(ag[0].shape, ag[0].dtype)],
        scratch_shapes=[pltpu.SemaphoreType.DMA((n,))] * 2, input_output_aliases={4: 3},
        compiler_params=_params(("arbitrary", "arbitrary")),
    )(*ins, ag[0])


def _fox_gate_bwd(dog, o, proj, lse, qa):
    lp = o.shape[0]
    tr = LANES

    def body(d_ref, o_ref, g_ref, lse_ref, q_ref, do_ref, q2_ref, dgate_ref):
        down, _ = _head_sel(FOX_H, FOX_DH)
        sg = _sigmoid(g_ref[...])
        dv, ov = d_ref[...], o_ref[...]
        do = (dv * sg).astype(BF16).astype(F32)
        dgate_ref[...] = dv * ov * sg * (1.0 - sg)
        delta = _sel_r(do * ov, down)
        _spread(do, -_placed(delta, L_C), do_ref)
        r_, c_ = _iota((D, LANES), 0), _iota((D, LANES), 1)
        lse_c = _sel_r(lse_ref[...], (r_ == c_ * FOX_DH).astype(BF16))
        q2_ref[...] = (q_ref[...].astype(F32) - _placed(lse_c, L_LSE)).astype(BF16)

    spec = pl.BlockSpec((tr, D), lambda i: (i, 0))
    aug = pl.BlockSpec((tr, FOX_AUG), lambda i: (i, 0))
    return pl.pallas_call(
        body, name="fox_gate_bwd", grid=(lp // tr,),
        in_specs=[spec, spec, pl.BlockSpec((tr, D), lambda i: (i, 3)), spec, aug], out_specs=[aug, aug, spec],
        out_shape=[jax.ShapeDtypeStruct((lp, FOX_AUG), BF16), jax.ShapeDtypeStruct((lp, FOX_AUG), BF16),
                   jax.ShapeDtypeStruct((lp, D), F32)],
        compiler_params=_params(("parallel",)),
    )(dog, o, proj, lse, qa)


def _fox_attn_bwd(q2, ka, va, doa, rs=None):
    lp = q2.shape[0]
    t = _tile(lp, FOX_TQ)
    nb = lp // t
    npair = FOX_H // 2

    def body(q_ref, k_ref, v_ref, do_ref, *rest):
        if rs is None:
            dq_ref, dk_ref, dv_ref, dc_ref, dq_acc, dk_acc, dv_acc, dc_acc = rest
        else:
            s_ref, dq_ref, dk_ref, dv_ref, dc_ref, got_ref, dq_acc, dk_acc, dv_acc, dc_acc, send_sems, recv_sems = rest
            sends, arrivals = _scatter_copies(s_ref, got_ref, send_sems, recv_sems)

            @pl.when((pl.program_id(0) == 0) & (pl.program_id(1) == 0))
            def _():
                for cp in sends:
                    cp.start()

            @pl.when((pl.program_id(0) == npair - 1) & (pl.program_id(1) == nb - 1))
            def _():
                for cp in arrivals:
                    cp.wait_recv()
                for cp in sends:
                    cp.wait_send()

        j = pl.program_id(1)

        @pl.when(j == 0)
        def _():
            dq_acc[...] = jnp.zeros_like(dq_acc)

        causal = _iota((t, t), 1) <= _iota((t, t), 0)
        ks = [k_ref[:, hh * LANES:(hh + 1) * LANES] for hh in range(2)]
        vs = [v_ref[:, hh * LANES:(hh + 1) * LANES] for hh in range(2)]
        dk_acc[...] = jnp.zeros_like(dk_acc)
        dv_acc[...] = jnp.zeros_like(dv_acc)
        dc_acc[...] = jnp.zeros_like(dc_acc)

        def block(i, diag):
            off = pl.multiple_of(i * t, t)
            for hh in range(2):
                q = q_ref[pl.ds(off, t), hh * LANES:(hh + 1) * LANES]
                do = do_ref[pl.ds(off, t), hh * LANES:(hh + 1) * LANES]
                s = nt(q, ks[hh])
                if diag:
                    s = jnp.where(causal, s, -1e30)
                p = jnp.exp(s)
                ds = p * nt(do, vs[hh])
                dc_acc[hh] += jnp.sum(ds, axis=0, keepdims=True)
                dsb = ds.astype(BF16)
                dv_acc[hh] += tn(p.astype(BF16), do)
                dk_acc[hh] += tn(dsb, q)
                dq_acc[hh, pl.ds(off, t), :] += nn(dsb, ks[hh])

        block(j, True)
        below = nb - 1 - j

        def step(u, c):
            block(j + 1 + 2 * u, False)
            block(j + 2 + 2 * u, False)
            return c

        lax.fori_loop(0, below // 2, step, 0)

        @pl.when(below % 2 == 1)
        def _():
            block(nb - 1, False)
        left = _iota((t, LANES), 1) < FOX_DH
        dk_ref[...] = jnp.where(left, dk_acc[0], pltpu.roll(dk_acc[1], FOX_DH, 1))
        dv_ref[...] = jnp.where(left, dv_acc[0], pltpu.roll(dv_acc[1], FOX_DH, 1))
        for hh in range(2):
            dc_ref[hh] = jnp.broadcast_to(-dc_acc[hh], (8, t))

        @pl.when(j == nb - 1)
        def _():
            left = _iota((lp, LANES), 1) < FOX_DH
            dq_ref[...] = jnp.where(left, dq_acc[0], pltpu.roll(dq_acc[1], FOX_DH, 1))

    full = pl.BlockSpec((lp, 2 * LANES), lambda p, j: (0, p))
    kblk = pl.BlockSpec((t, 2 * LANES), lambda p, j: (j, p))
    oblk = pl.BlockSpec((t, LANES), lambda p, j: (j, p))
    in_specs = [full, kblk, kblk, full]
    out_specs = [pl.BlockSpec((lp, LANES), lambda p, j: (0, p)), oblk, oblk, pl.BlockSpec((2, 8, t), lambda p, j: (p, 0, j))]
    out_shape = [jax.ShapeDtypeStruct((lp, D), F32)] * 3 + [jax.ShapeDtypeStruct((FOX_H, 8, lp), F32)]
    scratch = [pltpu.VMEM((2, lp, LANES), F32), pltpu.VMEM((2, t, LANES), F32), pltpu.VMEM((2, t, LANES), F32),
               pltpu.VMEM((2, 1, t), F32)]
    if rs is None:
        return pl.pallas_call(body, name="fox_attn_bwd", grid=(npair, nb), in_specs=in_specs, out_specs=out_specs,
                              out_shape=out_shape, scratch_shapes=scratch,
                              compiler_params=_params(("parallel", "arbitrary")))(q2, ka, va, doa)
    return pl.pallas_call(
        body, name="fox_attn_bwd_rs", grid=(npair, nb), in_specs=in_specs + [ANY], out_specs=out_specs + [ANY],
        out_shape=out_shape + [jax.ShapeDtypeStruct((3,) + rs.shape[1:], rs.dtype)],
        scratch_shapes=scratch + [pltpu.SemaphoreType.DMA((3,)), pltpu.SemaphoreType.DMA((3,))],
        compiler_params=_params(("arbitrary", "arbitrary")),
    )(q2, ka, va, doa, rs)


def _fox_prep_bwd(proj, b_f, q_gain, k_gain, dqn, dkn, dv, dgate, dct):
    lp = proj.shape[0]
    nb = lp // LANES

    def body(p_ref, bf_ref, qg_ref, kg_ref, dq_ref, dk_ref, dv_ref, dg_ref, dc_ref,
             dp_ref, dqg_ref, dkg_ref, dbf_ref, carry):
        i = pl.program_id(0)

        @pl.when(i == 0)
        def _():
            carry[...] = jnp.zeros_like(carry)
            dqg_ref[...] = jnp.zeros_like(dqg_ref)
            dkg_ref[...] = jnp.zeros_like(dkg_ref)
            dbf_ref[...] = jnp.zeros_like(dbf_ref)

        down, up = _head_sel(FOX_H, FOX_DH)

        def norm_bwd(x, gain, dy, scale, dgain_ref):
            ms = _sel_r2(x * x, down) * (1.0 / FOX_DH)
            r = _sel_r2(lax.rsqrt(ms + EPS), up)
            u = dy * gain * scale
            mean_xu = _sel_r2(_sel_r2(x * u, down) * (1.0 / FOX_DH), up)
            dgain_ref[...] += jnp.sum(dy * scale * x * r, axis=0, keepdims=True)
            return r * u - x * (r * r * r) * mean_xu

        dp_ref[:, 0:D] = norm_bwd(p_ref[:, 0:D], qg_ref[...], dq_ref[...], FOX_DH ** -0.5, dqg_ref).astype(BF16)
        dp_ref[:, D:2 * D] = norm_bwd(p_ref[:, D:2 * D], kg_ref[...], dk_ref[...], 1.0, dkg_ref).astype(BF16)
        dp_ref[:, 2 * D:3 * D] = dv_ref[...].astype(BF16)
        dp_ref[:, 3 * D:4 * D] = dg_ref[...].astype(BF16)
        rows = jnp.concatenate([dc_ref[h, 0:1, :] for h in range(FOX_H)] + [jnp.zeros((LANES - FOX_H, LANES), F32)], axis=0)
        dlf = _sel_l(_tri(LANES, upper=True).astype(BF16), rows.T) + carry[0:1, :]
        carry[...] = jnp.broadcast_to(dlf[0:1, :], carry.shape)
        lane = _iota((LANES, LANES), 1)
        z = p_ref[:, 4 * D:4 * D + LANES] + bf_ref[...]
        df = jnp.where(lane < FOX_H, dlf * _sigmoid(-z), 0.0)
        dp_ref[:, 4 * D:4 * D + LANES] = df.astype(BF16)
        dbf_ref[...] += jnp.sum(df, axis=0, keepdims=True)

    rev = lambda i: (nb - 1 - i, 0)
    blk = pl.BlockSpec((LANES, D), rev)
    row = pl.BlockSpec((1, D), lambda i: (0, 0))
    row128 = pl.BlockSpec((1, LANES), lambda i: (0, 0))
    return pl.pallas_call(
        body, name="fox_prep_bwd", grid=(nb,),
        in_specs=[pl.BlockSpec((LANES, FOX_INP), rev), row128, row, row, blk, blk, blk, blk,
                  pl.BlockSpec((FOX_H, 8, LANES), lambda i: (0, 0, nb - 1 - i))],
        out_specs=[pl.BlockSpec((LANES, FOX_INP), rev), row, row, row128],
        out_shape=[jax.ShapeDtypeStruct((lp, FOX_INP), BF16), jax.ShapeDtypeStruct((1, D), F32),
                   jax.ShapeDtypeStruct((1, D), F32), jax.ShapeDtypeStruct((1, LANES), F32)],
        scratch_shapes=[pltpu.VMEM((8, LANES), F32)],
        compiler_params=_params(("arbitrary",)),
    )(proj, jnp.pad(b_f, (0, LANES - FOX_H)).reshape(1, LANES), jnp.tile(q_gain, FOX_H).reshape(1, D),
      jnp.tile(k_gain, FOX_H).reshape(1, D), dqn, dkn, dv, dgate, dct)


def _gla_gates(p_ref, wa_ref, ba_ref):
    a_lr = p_ref[:, 3072:3072 + LANES]
    z = nn(a_lr.astype(BF16), wa_ref[...].astype(BF16)) + ba_ref[...]
    g = _log_sigmoid(z) * (1.0 / GLA_NORM)
    b = _sel_l(_tri(CHUNK).astype(BF16), g)
    return a_lr, z, b


def _gla_chunk_fwd(q, k, v, b, st0):
    hs = range(len(q))
    low = _tri(CHUNK)
    bl = [b[h][CHUNK - 1:CHUNK, :] for h in hs]
    qe = [q[h] * jnp.exp(b[h]) for h in hs]
    ke = [k[h] * jnp.exp(-b[h]) for h in hs]
    kd = [k[h] * jnp.exp(bl[h] - b[h]) for h in hs]
    a = [jnp.where(low, nt(qe[h], ke[h]), 0.0) for h in hs]
    o = [nn(a[h], v[h]) + nt(qe[h], st0[h]) for h in hs]
    st1 = [st0[h] * jnp.exp(bl[h]) + tn(v[h], kd[h]) for h in hs]
    return o, st1, (qe, ke, kd, a, bl)


def _gla_slices(p_ref, b_all, h):
    q = p_ref[:, h * GLA_DK:(h + 1) * GLA_DK] * (GLA_DK ** -0.5)
    k = p_ref[:, GLA_QK + h * GLA_DK:GLA_QK + (h + 1) * GLA_DK]
    v = p_ref[:, 2 * GLA_QK + h * GLA_DV:2 * GLA_QK + (h + 1) * GLA_DV]
    r = p_ref[:, 2 * GLA_QK + GLA_V + h * GLA_DV:2 * GLA_QK + GLA_V + (h + 1) * GLA_DV]
    return q, k, v, r, b_all[:, h * GLA_DK:(h + 1) * GLA_DK]


def _gla_fwd(proj, w_alpha2, b_alpha, o_gain):
    lp = proj.shape[0]
    nc = lp // CHUNK

    def body(p_ref, wa_ref, ba_ref, og_ref, o_ref, y_ref, s_ref, st):
        @pl.when(pl.program_id(0) == 0)
        def _():
            st[...] = jnp.zeros_like(st)

        _, _, b_all = _gla_gates(p_ref, wa_ref, ba_ref)
        hs = range(GLA_H)
        parts = [_gla_slices(p_ref, b_all, h) for h in hs]
        st0 = [st[h] for h in hs]
        for h in hs:
            s_ref[0, h] = st0[h]
        o, st1, _ = _gla_chunk_fwd([p[0] for p in parts], [p[1] for p in parts], [p[2] for p in parts],
                                   [p[4] for p in parts], st0)
        for h in hs:
            st[h] = st1[h]
            o_ref[:, h * GLA_DV:(h + 1) * GLA_DV] = o[h]
            rs = lax.rsqrt(jnp.mean(o[h] * o[h], axis=-1, keepdims=True) + EPS)
            y_ref[:, h * GLA_DV:(h + 1) * GLA_DV] = (o[h] * rs * og_ref[...] * _silu(parts[h][3])).astype(BF16)

    blk = pl.BlockSpec((CHUNK, D), lambda i: (i, 0))
    return pl.pallas_call(
        body, name="gla_fwd", grid=(nc,),
        in_specs=[pl.BlockSpec((CHUNK, GLA_INP), lambda i: (i, 0)), pl.BlockSpec((LANES, GLA_QK), lambda i: (0, 0)),
                  pl.BlockSpec((1, GLA_QK), lambda i: (0, 0)), pl.BlockSpec((1, GLA_DV), lambda i: (0, 0))],
        out_specs=[blk, blk, pl.BlockSpec((1, GLA_H, GLA_DV, GLA_DK), lambda i: (i, 0, 0, 0))],
        out_shape=[jax.ShapeDtypeStruct((lp, D), F32), jax.ShapeDtypeStruct((lp, D), BF16),
                   jax.ShapeDtypeStruct((nc, GLA_H, GLA_DV, GLA_DK), F32)],
        scratch_shapes=[pltpu.VMEM((GLA_H, GLA_DV, GLA_DK), F32)],
        compiler_params=_params(("arbitrary",)),
    )(proj, jnp.pad(w_alpha2, ((0, LANES - GLA_RANK), (0, 0))), b_alpha.reshape(1, GLA_QK), o_gain.reshape(1, GLA_DV))


def _gla_bwd(proj, w_alpha2, b_alpha, o_gain, o, states, dy):
    lp = proj.shape[0]
    nc = lp // CHUNK

    def body(p_ref, wa_ref, ba_ref, og_ref, o_ref, s_ref, dy_ref, dp_ref, dwa_ref, dba_ref, dog_ref, dst):
        @pl.when(pl.program_id(0) == 0)
        def _():
            dst[...] = jnp.zeros_like(dst)
            dwa_ref[...] = jnp.zeros_like(dwa_ref)
            dba_ref[...] = jnp.zeros_like(dba_ref)
            dog_ref[...] = jnp.zeros_like(dog_ref)

        a_lr, z, b_all = _gla_gates(p_ref, wa_ref, ba_ref)
        last_row = _iota((CHUNK, GLA_DK), 0) == CHUNK - 1
        rev = _tri(CHUNK, upper=True).astype(BF16)
        hs = range(GLA_H)
        scale = GLA_DK ** -0.5
        parts = [_gla_slices(p_ref, b_all, h) for h in hs]
        q, k, v, b = [p[0] for p in parts], [p[1] for p in parts], [p[2] for p in parts], [p[4] for p in parts]
        st0 = [s_ref[0, h] for h in hs]
        dst1 = [dst[h] for h in hs]
        do = []
        for h in hs:
            r = parts[h][3]
            ov = o_ref[:, h * GLA_DV:(h + 1) * GLA_DV]
            dyv = dy_ref[:, h * GLA_DV:(h + 1) * GLA_DV]
            rs = lax.rsqrt(jnp.mean(ov * ov, axis=-1, keepdims=True) + EPS)
            on = ov * rs
            dp_ref[:, 2 * GLA_QK + GLA_V + h * GLA_DV:2 * GLA_QK + GLA_V + (h + 1) * GLA_DV] = (
                dyv * on * og_ref[...] * _dsilu(r)).astype(BF16)
            don = dyv * _silu(r)
            dog_ref[...] += jnp.sum(don * on, axis=0, keepdims=True)
            u = don * og_ref[...]
            do.append(rs * u - ov * (rs * rs * rs) * jnp.mean(ov * u, axis=-1, keepdims=True))
        _, _, (qe, ke, kd, a, bl) = _gla_chunk_fwd(q, k, v, b, st0)
        low = _tri(CHUNK)
        da = [jnp.where(low, nt(do[h], v[h]), 0.0) for h in hs]
        dkd = [nn(v[h], dst1[h]) for h in hs]
        dvv = [tn(a[h], do[h]) + nt(kd[h], dst1[h]) for h in hs]
        dqe = [nn(da[h], ke[h]) + nn(do[h], st0[h]) for h in hs]
        dke = [tn(da[h], qe[h]) for h in hs]
        dg_parts = []
        for h in hs:
            ebl = jnp.exp(bl[h])
            dst[h] = dst1[h] * ebl + tn(do[h], qe[h])
            db = dqe[h] * qe[h] - dke[h] * ke[h] - dkd[h] * kd[h]
            db_last = (jnp.sum(dkd[h] * kd[h], axis=0, keepdims=True)
                       + jnp.sum(dst1[h] * st0[h], axis=0, keepdims=True) * ebl)
            db = db + jnp.where(last_row, db_last, 0.0)
            dg_parts.append(_sel_l(rev, db))
            dp_ref[:, h * GLA_DK:(h + 1) * GLA_DK] = (dqe[h] * jnp.exp(b[h]) * scale).astype(BF16)
            dp_ref[:, GLA_QK + h * GLA_DK:GLA_QK + (h + 1) * GLA_DK] = (
                dke[h] * jnp.exp(-b[h]) + dkd[h] * jnp.exp(bl[h] - b[h])).astype(BF16)
            dp_ref[:, 2 * GLA_QK + h * GLA_DV:2 * GLA_QK + (h + 1) * GLA_DV] = dvv[h].astype(BF16)
        dg = jnp.concatenate(dg_parts, axis=1)
        dz = dg * (1.0 / GLA_NORM) * _sigmoid(-z)
        dzb = dz.astype(BF16)
        dp_ref[:, 3072:3072 + LANES] = nt(dzb, wa_ref[...].astype(BF16)).astype(BF16)
        dwa_ref[...] += tn(a_lr.astype(BF16), dzb)
        dba_ref[...] += jnp.sum(dz, axis=0, keepdims=True)

    rv = lambda i: (nc - 1 - i, 0)
    blk = pl.BlockSpec((CHUNK, D), rv)
    fixed = lambda r, c: pl.BlockSpec((r, c), lambda i: (0, 0))
    return pl.pallas_call(
        body, name="gla_bwd", grid=(nc,),
        in_specs=[pl.BlockSpec((CHUNK, GLA_INP), rv), fixed(LANES, GLA_QK), fixed(1, GLA_QK), fixed(1, GLA_DV), blk,
                  pl.BlockSpec((1, GLA_H, GLA_DV, GLA_DK), lambda i: (nc - 1 - i, 0, 0, 0)), blk],
        out_specs=[pl.BlockSpec((CHUNK, GLA_INP), rv), fixed(LANES, GLA_QK), fixed(1, GLA_QK), fixed(1, GLA_DV)],
        out_shape=[jax.ShapeDtypeStruct((lp, GLA_INP), BF16), jax.ShapeDtypeStruct((LANES, GLA_QK), F32),
                   jax.ShapeDtypeStruct((1, GLA_QK), F32), jax.ShapeDtypeStruct((1, GLA_DV), F32)],
        scratch_shapes=[pltpu.VMEM((GLA_H, GLA_DV, GLA_DK), F32)],
        compiler_params=_params(("arbitrary",)),
    )(proj, jnp.pad(w_alpha2, ((0, LANES - GLA_RANK), (0, 0))), b_alpha.reshape(1, GLA_QK), o_gain.reshape(1, GLA_DV),
      o, states, dy)


HI = lax.Precision.HIGHEST


def _gdn_pre(prev_ref, p_ref, cw_ref, al_ref, dt_ref):
    xc = jnp.concatenate([prev_ref[:, 0:GDN_CONV], p_ref[:, 0:GDN_CONV]], axis=0)
    shifted = [pltpu.roll(xc, 3 - j, 0)[CHUNK:, :] if j < 3 else xc[CHUNK:, :] for j in range(4)]
    conv = sum(shifted[j] * cw_ref[j:j + 1, :] for j in range(4))
    act = _silu(conv)
    slab = p_ref[:, 4096:4096 + LANES]
    lane = _iota((CHUNK, LANES), 1)
    zs = slab + dt_ref[...]
    g = jnp.where(lane < GDN_H, -jnp.exp(al_ref[...]) * _softplus(zs), 0.0)
    bs = _sel_l(_tri(CHUNK).astype(BF16), g)
    beta = _sigmoid(slab)
    return shifted, conv, act, slab, zs, g, bs, beta


def _l2n(x):
    r = lax.rsqrt(jnp.sum(x * x, axis=-1, keepdims=True) + EPS)
    return x * r, r


def _gdn_chunk_fwd(q, k, v, beta, bcol, brow, s0):
    hs = range(len(q))
    ii, jj = _iota((CHUNK, CHUNK), 0), _iota((CHUNK, CHUNK), 1)
    low, eye = ii >= jj, (ii == jj).astype(F32)
    dm = [jnp.where(low, jnp.exp(jnp.where(low, bcol[h] - brow[h], 0.0)), 0.0) for h in hs]
    dstrict = [jnp.where(ii > jj, dm[h], 0.0) for h in hs]
    eb = [jnp.exp(bcol[h]) for h in hs]
    bl = [bcol[h][CHUNK - 1:CHUNK, :] for h in hs]
    kb = [k[h] * beta[h] for h in hs]
    vb = [v[h] * beta[h] for h in hs]
    nmat = [nt(kb[h], k[h]) * dstrict[h] for h in hs]
    x = [eye - nmat[h] for h in hs]
    pw = [nn(nmat[h], nmat[h], precision=HI) for h in hs]
    for it in range(5):
        x = [x[h] + nn(x[h], pw[h], precision=HI) for h in hs]
        if it < 4:
            pw = [nn(pw[h], pw[h], precision=HI) for h in hs]
    kbe = [kb[h] * eb[h] for h in hs]
    u = [nn(x[h], vb[h], precision=HI) for h in hs]
    w = [nn(x[h], kbe[h], precision=HI) for h in hs]
    vn = [u[h] - nn(w[h], s0[h]) for h in hs]
    pm = [nt(q[h], k[h]) * dm[h] for h in hs]
    qe = [q[h] * eb[h] for h in hs]
    o = [nn(pm[h], vn[h]) + nn(qe[h], s0[h]) for h in hs]
    kd = [k[h] * jnp.exp(bl[h] - bcol[h]) for h in hs]
    s1 = [s0[h] * jnp.exp(bl[h]) + tn(kd[h], vn[h]) for h in hs]
    return o, s1, dict(dm=dm, dstrict=dstrict, eb=eb, bl=bl, kb=kb, vb=vb, nmat=nmat, tinv=x, kbe=kbe, u=u, w=w, vn=vn,
                       pm=pm, qe=qe, kd=kd)


def _gdn_heads(act, beta_slab, bs, h):
    qa = act[:, h * GDN_DK:(h + 1) * GDN_DK]
    ka = act[:, GDN_H * GDN_DK + h * GDN_DK:GDN_H * GDN_DK + (h + 1) * GDN_DK]
    v = act[:, 2 * GDN_H * GDN_DK + h * GDN_DV:2 * GDN_H * GDN_DK + (h + 1) * GDN_DV]
    return qa, ka, v, beta_slab[:, GDN_H + h:GDN_H + h + 1], bs[:, h:h + 1]


def _gdn_fwd(proj, conv_w, a_log, dt_bias, o_gain):
    lp = proj.shape[0]
    nc = lp // CHUNK

    def body(prev_ref, p_ref, cw_ref, al_ref, dt_ref, og_ref, o_ref, y_ref, s_ref, st):
        @pl.when(pl.program_id(0) == 0)
        def _():
            st[...] = jnp.zeros_like(st)

        _, _, act, _, _, _, bs, beta = _gdn_pre(prev_ref, p_ref, cw_ref, al_ref, dt_ref)
        bst = bs.T
        hs = range(GDN_H)
        parts = [_gdn_heads(act, beta, bs, h) for h in hs]
        q = [_l2n(parts[h][0])[0] * (GDN_DK ** -0.5) for h in hs]
        k = [_l2n(parts[h][1])[0] for h in hs]
        s0 = [st[h] for h in hs]
        for h in hs:
            s_ref[0, h] = s0[h]
        o, s1, _ = _gdn_chunk_fwd(q, k, [parts[h][2] for h in hs], [parts[h][3] for h in hs], [parts[h][4] for h in hs],
                                  [bst[h:h + 1, :] for h in hs], s0)
        for h in hs:
            st[h] = s1[h]
            o_ref[:, h * GDN_DV:(h + 1) * GDN_DV] = o[h]
            rs = lax.rsqrt(jnp.mean(o[h] * o[h], axis=-1, keepdims=True) + EPS)
            gate = p_ref[:, GDN_CONV + h * GDN_DV:GDN_CONV + (h + 1) * GDN_DV]
            y_ref[:, h * GDN_DV:(h + 1) * GDN_DV] = (o[h] * rs * og_ref[...] * _silu(gate)).astype(BF16)

    blk = pl.BlockSpec((CHUNK, D), lambda i: (i, 0))
    fixed = lambda r, c: pl.BlockSpec((r, c), lambda i: (0, 0))
    return pl.pallas_call(
        body, name="gdn_fwd", grid=(nc,),
        in_specs=[pl.BlockSpec((CHUNK, GDN_INP), lambda i: (jnp.maximum(i - 1, 0), 0)),
                  pl.BlockSpec((CHUNK, GDN_INP), lambda i: (i, 0)), fixed(8, GDN_CONV), fixed(1, LANES), fixed(1, LANES),
                  fixed(1, GDN_DV)],
        out_specs=[blk, blk, pl.BlockSpec((1, GDN_H, GDN_DK, GDN_DV), lambda i: (i, 0, 0, 0))],
        out_shape=[jax.ShapeDtypeStruct((lp, D), F32), jax.ShapeDtypeStruct((lp, D), BF16),
                   jax.ShapeDtypeStruct((nc, GDN_H, GDN_DK, GDN_DV), F32)],
        scratch_shapes=[pltpu.VMEM((GDN_H, GDN_DK, GDN_DV), F32)],
        compiler_params=_params(("arbitrary",)),
    )(proj, proj, jnp.pad(conv_w.reshape(4, GDN_CONV), ((0, 4), (0, 0))), jnp.pad(a_log, (0, LANES - GDN_H)).reshape(1, LANES),
      jnp.pad(dt_bias, (0, LANES - GDN_H)).reshape(1, LANES), o_gain.reshape(1, GDN_DV))


def _gdn_bwd(proj, conv_w, a_log, dt_bias, o_gain, o, states, dy):
    lp = proj.shape[0]
    nc = lp // CHUNK

    def body(prev_ref, p_ref, cw_ref, al_ref, dt_ref, og_ref, o_ref, s_ref, dy_ref,
             dp_ref, dcw_ref, dal_ref, ddt_ref, dog_ref, dst, dconv_next):
        @pl.when(pl.program_id(0) == 0)
        def _():
            dst[...] = jnp.zeros_like(dst)
            dconv_next[...] = jnp.zeros_like(dconv_next)
            dcw_ref[...] = jnp.zeros_like(dcw_ref)
            dal_ref[...] = jnp.zeros_like(dal_ref)
            ddt_ref[...] = jnp.zeros_like(ddt_ref)
            dog_ref[...] = jnp.zeros_like(dog_ref)

        shifted, conv, act, slab, zs, g, bs, beta = _gdn_pre(prev_ref, p_ref, cw_ref, al_ref, dt_ref)
        bst = bs.T
        lane = _iota((CHUNK, LANES), 1)
        ones = jnp.ones((CHUNK, LANES), F32)
        db_slab = jnp.zeros((CHUNK, LANES), F32)
        dbeta_slab = jnp.zeros((CHUNK, LANES), F32)
        last_row = _iota((CHUNK, 1), 0) == CHUNK - 1
        hs = range(GDN_H)
        scale = GDN_DK ** -0.5
        parts = [_gdn_heads(act, beta, bs, h) for h in hs]
        qa, ka, v = [parts[h][0] for h in hs], [parts[h][1] for h in hs], [parts[h][2] for h in hs]
        bet, bcol = [parts[h][3] for h in hs], [parts[h][4] for h in hs]
        qn_ = [_l2n(qa[h]) for h in hs]
        kn_ = [_l2n(ka[h]) for h in hs]
        q = [qn_[h][0] * scale for h in hs]
        k, rq, rk = [kn_[h][0] for h in hs], [qn_[h][1] for h in hs], [kn_[h][1] for h in hs]
        s0 = [s_ref[0, h] for h in hs]
        ds1 = [dst[h] for h in hs]
        do = []
        for h in hs:
            ov = o_ref[:, h * GDN_DV:(h + 1) * GDN_DV]
            dyv = dy_ref[:, h * GDN_DV:(h + 1) * GDN_DV]
            gate = p_ref[:, GDN_CONV + h * GDN_DV:GDN_CONV + (h + 1) * GDN_DV]
            rs = lax.rsqrt(jnp.mean(ov * ov, axis=-1, keepdims=True) + EPS)
            on = ov * rs
            dp_ref[:, GDN_CONV + h * GDN_DV:GDN_CONV + (h + 1) * GDN_DV] = (dyv * on * og_ref[...] * _dsilu(gate)).astype(BF16)
            don = dyv * _silu(gate)
            dog_ref[...] += jnp.sum(don * on, axis=0, keepdims=True)
            uu = don * og_ref[...]
            do.append(rs * uu - ov * (rs * rs * rs) * jnp.mean(ov * uu, axis=-1, keepdims=True))
        _, _, f = _gdn_chunk_fwd(q, k, v, bet, bcol, [bst[h:h + 1, :] for h in hs], s0)
        dm, dstrict, eb, bl, kb, nmat, tinv = f["dm"], f["dstrict"], f["eb"], f["bl"], f["kb"], f["nmat"], f["tinv"]
        kbe, u, w, vn, pm, qe, kd = f["kbe"], f["u"], f["w"], f["vn"], f["pm"], f["qe"], f["kd"]
        ebl = [jnp.exp(bl[h]) for h in hs]
        dvn = [tn(pm[h], do[h]) + nn(kd[h], ds1[h]) for h in hs]
        dpr = [nt(do[h], vn[h]) for h in hs]
        dqe = [nt(do[h], s0[h]) for h in hs]
        dkd = [nt(vn[h], ds1[h]) for h in hs]
        for h in hs:
            dst[h] = ds1[h] * ebl[h] + tn(qe[h], do[h]) - tn(w[h], dvn[h])
        du_ = [tn(tinv[h], dvn[h], precision=HI) for h in hs]
        dw_ = [tn(tinv[h], -nt(dvn[h], s0[h]), precision=HI) for h in hs]
        dn = [-(nt(du_[h], u[h]) + nt(dw_[h], w[h])) for h in hs]
        dqk = [dpr[h] * dm[h] for h in hs]
        dkk = [dn[h] * dstrict[h] for h in hs]
        gsum = [dpr[h] * pm[h] + dn[h] * nmat[h] for h in hs]
        dkb = [nn(dkk[h], k[h]) + dw_[h] * eb[h] for h in hs]
        dk = [tn(dkk[h], kb[h]) + tn(dqk[h], q[h]) + dkd[h] * jnp.exp(bl[h] - bcol[h]) + dkb[h] * bet[h] for h in hs]
        dq = [nn(dqk[h], k[h]) + dqe[h] * eb[h] for h in hs]
        colsum = [tn(gsum[h], ones, precision=HI)[:, 0:1] for h in hs]
        dact_q, dact_k, dact_v = [], [], []
        for h in hs:
            dbeta = jnp.sum(dkb[h] * k[h], axis=-1, keepdims=True) + jnp.sum(du_[h] * v[h], axis=-1, keepdims=True)
            skd = jnp.sum(dkd[h] * kd[h], axis=-1, keepdims=True)
            db = (jnp.sum(gsum[h], axis=-1, keepdims=True) - colsum[h] + jnp.sum(dqe[h] * qe[h], axis=-1, keepdims=True)
                  + jnp.sum(dw_[h] * kbe[h], axis=-1, keepdims=True) - skd)
            db_last = jnp.sum(skd, axis=0, keepdims=True) + jnp.sum(ds1[h] * s0[h]) * ebl[h]
            db = db + jnp.where(last_row, db_last, 0.0)
            db_slab = db_slab + jnp.where(lane == h, db, 0.0)
            dbeta_slab = dbeta_slab + jnp.where(lane == GDN_H + h, dbeta, 0.0)
            dqn = dq[h] * scale
            dact_q.append(rq[h] * dqn - qa[h] * (rq[h] * rq[h] * rq[h]) * jnp.sum(qa[h] * dqn, axis=-1, keepdims=True))
            dact_k.append(rk[h] * dk[h] - ka[h] * (rk[h] * rk[h] * rk[h]) * jnp.sum(ka[h] * dk[h], axis=-1, keepdims=True))
            dact_v.append(du_[h] * bet[h])
        dact = jnp.concatenate(dact_q + dact_k + dact_v, axis=1)
        dconv = dact * _dsilu(conv)
        for j in range(4):
            dcw_ref[j:j + 1, :] += jnp.sum(dconv * shifted[j], axis=0, keepdims=True)
        dcat = jnp.concatenate([dconv, dconv_next[...]], axis=0)
        dx = dconv * cw_ref[3:4, :]
        for j in range(3):
            dx = dx + pltpu.roll(dcat, 2 * CHUNK - (3 - j), 0)[:CHUNK, :] * cw_ref[j:j + 1, :]
        dconv_next[...] = dconv
        dp_ref[:, 0:GDN_CONV] = dx.astype(BF16)
        dg = _sel_l(_tri(CHUNK, upper=True).astype(BF16), db_slab)
        da = dg * (-jnp.exp(al_ref[...])) * _sigmoid(zs)
        da = jnp.where(lane < GDN_H, da, 0.0)
        dal_ref[...] += jnp.sum(dg * g, axis=0, keepdims=True)
        ddt_ref[...] += jnp.sum(da, axis=0, keepdims=True)
        dp_ref[:, 4096:4096 + LANES] = (da + dbeta_slab * beta * (1.0 - beta)).astype(BF16)

    rv = lambda i: (nc - 1 - i, 0)
    blk = pl.BlockSpec((CHUNK, D), rv)
    fixed = lambda r, c: pl.BlockSpec((r, c), lambda i: (0, 0))
    return pl.pallas_call(
        body, name="gdn_bwd", grid=(nc,),
        in_specs=[pl.BlockSpec((CHUNK, GDN_INP), lambda i: (jnp.maximum(nc - 2 - i, 0), 0)),
                  pl.BlockSpec((CHUNK, GDN_INP), rv), fixed(8, GDN_CONV), fixed(1, LANES), fixed(1, LANES), fixed(1, GDN_DV),
                  blk, pl.BlockSpec((1, GDN_H, GDN_DK, GDN_DV), lambda i: (nc - 1 - i, 0, 0, 0)), blk],
        out_specs=[pl.BlockSpec((CHUNK, GDN_INP), rv), fixed(8, GDN_CONV), fixed(1, LANES), fixed(1, LANES), fixed(1, GDN_DV)],
        out_shape=[jax.ShapeDtypeStruct((lp, GDN_INP), BF16), jax.ShapeDtypeStruct((8, GDN_CONV), F32),
                   jax.ShapeDtypeStruct((1, LANES), F32), jax.ShapeDtypeStruct((1, LANES), F32),
                   jax.ShapeDtypeStruct((1, GDN_DV), F32)],
        scratch_shapes=[pltpu.VMEM((GDN_H, GDN_DK, GDN_DV), F32), pltpu.VMEM((CHUNK, GDN_CONV), F32)],
        compiler_params=_params(("arbitrary",)),
    )(proj, proj, jnp.pad(conv_w.reshape(4, GDN_CONV), ((0, 4), (0, 0))), jnp.pad(a_log, (0, LANES - GDN_H)).reshape(1, LANES),
      jnp.pad(dt_bias, (0, LANES - GDN_H)).reshape(1, LANES), o_gain.reshape(1, GDN_DV), o, states, dy)


def _coords():
    return lax.axis_index("x"), lax.axis_index("y"), lax.axis_index("c")


def _other_chips(x, y):
    return [(1 - x, y, 2 * (1 - x) + y), (x, 1 - y, 2 * x + 1 - y), (1 - x, 1 - y, 2 * (1 - x) + 1 - y)]


def _gather8(v, *, reduce, name):
    r, c = v.shape

    def body(v_ref, out_ref, *scratch):
        if reduce:
            buf, send_sems, recv_sems = scratch
        else:
            buf = out_ref
            send_sems, recv_sems = scratch
        x, y, cc = _coords()
        me = 4 * x + 2 * y + cc
        buf[me] = v_ref[...]
        copies = []
        for k in range(1, 8):
            px, py, pc = x ^ (k >> 2), y ^ ((k >> 1) & 1), cc ^ (k & 1)
            copies.append(pltpu.make_async_remote_copy(
                src_ref=v_ref, dst_ref=buf.at[me], send_sem=send_sems.at[k - 1], recv_sem=recv_sems.at[k - 1],
                device_id=(px, py, pc), device_id_type=MESH))
        for cp in copies:
            cp.start()
        for k in range(1, 8):
            peer = (x ^ (k >> 2)) * 4 + (y ^ ((k >> 1) & 1)) * 2 + (cc ^ (k & 1))
            pltpu.make_async_remote_copy(
                src_ref=v_ref, dst_ref=buf.at[peer], send_sem=send_sems.at[k - 1], recv_sem=recv_sems.at[k - 1],
                device_id=(x, y, cc), device_id_type=MESH).wait_recv()
        for cp in copies:
            cp.wait_send()
        if reduce:
            acc = buf[0]
            for d in range(1, 8):
                acc = acc + buf[d]
            out_ref[...] = acc

    scratch = [pltpu.SemaphoreType.DMA((7,)), pltpu.SemaphoreType.DMA((7,))]
    if reduce:
        scratch = [pltpu.VMEM((8, r, c), F32)] + scratch
    return pl.pallas_call(
        body, name=name, in_specs=[VM], out_specs=VM,
        out_shape=jax.ShapeDtypeStruct((r, c) if reduce else (8, r, c), F32),
        scratch_shapes=scratch, compiler_params=_params(),
    )(v)


class _AgCopies:
    def __init__(self, buf, ranges, send_sems, recv_sems):
        self.buf, self.ranges, self.send_sems, self.recv_sems = buf, ranges, send_sems, recv_sems
        self.x, self.y, self.cc = _coords()
        self.p = 2 * self.x + self.y
        self.chips = _other_chips(self.x, self.y)

    def rows(self, chip, r, hf):
        start, n = self.ranges[r]
        return self.buf.at[chip, pl.ds(start + hf * (n // 2), n // 2), :]

    def _copy(self, r, k, chip, hf, to):
        return pltpu.make_async_remote_copy(
            src_ref=self.rows(chip, r, hf), dst_ref=self.rows(chip, r, hf), send_sem=self.send_sems.at[3 * r + k],
            recv_sem=self.recv_sems.at[3 * r + k], device_id=to, device_id_type=MESH)

    def pairs(self):
        return [(r, k) for r in range(len(self.ranges)) for k in range(3)]

    def ici(self, r, k):
        cx, cy, _ = self.chips[k]
        return self._copy(r, k, self.p, self.cc, (cx, cy, self.cc))

    def ici_arrival(self, r, k):
        return self._copy(r, k, self.chips[k][2], self.cc, (self.x, self.y, self.cc))

    def forward(self, r, k):
        return self._copy(r, k, self.chips[k][2], self.cc, (self.x, self.y, 1 - self.cc))

    def forward_arrival(self, r, k):
        return self._copy(r, k, self.chips[k][2], 1 - self.cc, (self.x, self.y, self.cc))


def _ag_weights(w4, ranges):
    n = 3 * len(ranges)

    def body(w_ref, out_ref, send1, recv1, send2, recv2):
        ici, fwd = _AgCopies(out_ref, ranges, send1, recv1), _AgCopies(out_ref, ranges, send2, recv2)
        for r, k in ici.pairs():
            ici.ici(r, k).start()
        for r, k in ici.pairs():
            ici.ici_arrival(r, k).wait_recv()
            fwd.forward(r, k).start()
        for r, k in ici.pairs():
            fwd.forward_arrival(r, k).wait_recv()
        for r, k in ici.pairs():
            ici.ici(r, k).wait_send()
            fwd.forward(r, k).wait_send()

    return pl.pallas_call(
        body, name="ag_weights", in_specs=[ANY], out_specs=ANY, out_shape=jax.ShapeDtypeStruct(w4.shape, w4.dtype),
        scratch_shapes=[pltpu.SemaphoreType.DMA((n,))] * 4, input_output_aliases={0: 0}, compiler_params=_params(),
    )(w4)


def _ag_forward(w4, ranges):
    n = 3 * len(ranges)

    def body(w_ref, out_ref, send2, recv2):
        fwd = _AgCopies(out_ref, ranges, send2, recv2)
        for r, k in fwd.pairs():
            fwd.forward(r, k).start()
        for r, k in fwd.pairs():
            fwd.forward_arrival(r, k).wait_recv()
        for r, k in fwd.pairs():
            fwd.forward(r, k).wait_send()

    return pl.pallas_call(
        body, name="ag_forward", in_specs=[ANY], out_specs=ANY, out_shape=jax.ShapeDtypeStruct(w4.shape, w4.dtype),
        scratch_shapes=[pltpu.SemaphoreType.DMA((n,))] * 2, input_output_aliases={0: 0}, compiler_params=_params(),
    )(w4)


def _swap_halves(g, *, name):
    nb, r, c = g.shape
    half = r // 2

    def body(g_ref, out_ref, send_sem, recv_sem):
        x, y, cc = _coords()
        cp = pltpu.make_async_remote_copy(
            src_ref=g_ref.at[:, pl.ds((1 - cc) * half, half), :], dst_ref=out_ref, send_sem=send_sem, recv_sem=recv_sem,
            device_id=(x, y, 1 - cc), device_id_type=MESH)
        cp.start()
        cp.wait()

    return pl.pallas_call(
        body, name=name, in_specs=[ANY], out_specs=ANY, out_shape=jax.ShapeDtypeStruct((nb, half, c), g.dtype),
        scratch_shapes=[pltpu.SemaphoreType.DMA, pltpu.SemaphoreType.DMA], compiler_params=_params(),
    )(g)


def _my_half_index():
    return lax.axis_index("c").astype(jnp.int32).reshape(1)


def _add_halves(g, got, tag):
    nb, r, c = g.shape
    half = r // 2
    tr = _tile(half, 512, 16)
    nt_ = half // tr

    def body(c_ref, a_ref, b_ref, o_ref):
        o_ref[...] = (a_ref[...].astype(F32) + b_ref[...].astype(F32)).astype(BF16)

    return pl.pallas_call(
        body, name=f"rs_add_sibling{tag}",
        grid_spec=pltpu.PrefetchScalarGridSpec(
            num_scalar_prefetch=1, grid=(nb, nt_),
            in_specs=[pl.BlockSpec((1, tr, c), lambda b, i, cr: (b, cr[0] * nt_ + i, 0)),
                      pl.BlockSpec((1, tr, c), lambda b, i, cr: (b, i, 0))],
            out_specs=pl.BlockSpec((1, tr, c), lambda b, i, cr: (b, i, 0))),
        out_shape=jax.ShapeDtypeStruct((nb, half, c), BF16), compiler_params=_params(("parallel", "parallel")),
    )(_my_half_index(), g, got)


def _scatter_copies(s_ref, out_ref, send_sems, recv_sems):
    x, y, cc = _coords()
    sends = [pltpu.make_async_remote_copy(
        src_ref=s_ref.at[blk], dst_ref=out_ref.at[k], send_sem=send_sems.at[k], recv_sem=recv_sems.at[k],
        device_id=(cx, cy, cc), device_id_type=MESH) for k, (cx, cy, blk) in enumerate(_other_chips(x, y))]
    arrivals = [pltpu.make_async_remote_copy(
        src_ref=s_ref.at[2 * x + y], dst_ref=out_ref.at[k], send_sem=send_sems.at[k], recv_sem=recv_sems.at[k],
        device_id=(x, y, cc), device_id_type=MESH) for k in range(3)]
    return sends, arrivals


def _scatter_chips(s, tag):
    nb, hrows, c = s.shape

    def body(s_ref, out_ref, send_sems, recv_sems):
        sends, arrivals = _scatter_copies(s_ref, out_ref, send_sems, recv_sems)
        for cp in sends:
            cp.start()
        for cp in arrivals:
            cp.wait_recv()
        for cp in sends:
            cp.wait_send()

    return pl.pallas_call(
        body, name=f"rs_scatter{tag}", in_specs=[ANY], out_specs=ANY, out_shape=jax.ShapeDtypeStruct((3, hrows, c), s.dtype),
        scratch_shapes=[pltpu.SemaphoreType.DMA((3,)), pltpu.SemaphoreType.DMA((3,))], compiler_params=_params(),
    )(s)


def _sum_chips(s, got, tag):
    nb, hrows, c = s.shape
    tr = _tile(hrows, 512, 16)

    def body(idx_ref, own_ref, got_ref, o_ref):
        p = idx_ref[0]
        own = own_ref[0].astype(F32)
        parts = [got_ref[k].astype(F32) for k in range(3)]
        acc = jnp.zeros_like(own)
        for q in range(4):
            val = own
            for k, rel in enumerate((2, 1, 3)):
                val = jnp.where((p ^ rel) == q, parts[k], val)
            acc = acc + val
        o_ref[...] = acc

    idx = (2 * lax.axis_index("x") + lax.axis_index("y")).astype(jnp.int32).reshape(1)
    return pl.pallas_call(
        body, name=f"rs_sum_chips{tag}",
        grid_spec=pltpu.PrefetchScalarGridSpec(
            num_scalar_prefetch=1, grid=(hrows // tr,),
            in_specs=[pl.BlockSpec((1, tr, c), lambda i, pr: (pr[0], i, 0)), pl.BlockSpec((3, tr, c), lambda i, pr: (0, i, 0))],
            out_specs=pl.BlockSpec((tr, c), lambda i, pr: (i, 0))),
        out_shape=jax.ShapeDtypeStruct((hrows, c), F32), compiler_params=_params(("parallel",)),
    )(idx, s, got)


def _swap_sibling(t, tag):
    def body(t_ref, out_ref, send_sem, recv_sem):
        x, y, cc = _coords()
        cp = pltpu.make_async_remote_copy(src_ref=t_ref, dst_ref=out_ref, send_sem=send_sem, recv_sem=recv_sem,
                                          device_id=(x, y, 1 - cc), device_id_type=MESH)
        cp.start()
        cp.wait()

    return pl.pallas_call(
        body, name=f"rs_join{tag}", in_specs=[ANY], out_specs=ANY, out_shape=jax.ShapeDtypeStruct(t.shape, t.dtype),
        scratch_shapes=[pltpu.SemaphoreType.DMA, pltpu.SemaphoreType.DMA], compiler_params=_params(),
    )(t)


def _rs_local(g, tag):
    return _add_halves(g, _swap_halves(g, name=f"rs_swap{tag}"), tag)


def _rs_finish(s, recv, tag):
    t = _sum_chips(s, recv, tag)
    r = _swap_sibling(t, tag)
    first = lax.axis_index("c") == 0
    return jnp.concatenate([jnp.where(first, t, r), jnp.where(first, r, t)], axis=0)


_SMALL_SHARDED = (("meta_tokens", 1), ("gla_w_alpha2", 2), ("gdn_conv_w", 3))
_REPLICATED = ("norm_mix", "norm_ffn", "fox_b_f", "fox_q_gain", "fox_k_gain", "gla_b_alpha", "gla_o_gain",
               "gdn_a_log", "gdn_dt_bias", "gdn_o_gain")
_WEIGHTS = ("meta_tokens", "norm_mix", "norm_ffn", "w_gate_up", "w_down", "fox_w_in", "fox_b_f", "fox_q_gain",
            "fox_k_gain", "fox_w_out", "gla_w_in", "gla_w_alpha2", "gla_b_alpha", "gla_o_gain", "gla_w_out",
            "gdn_w_in", "gdn_conv_w", "gdn_a_log", "gdn_dt_bias", "gdn_o_gain", "gdn_w_out")
_PACK_ROWS = 512
_IN_W = ("fox_w_in", "gla_w_in", "gdn_w_in")
_OUT_W = ("fox_w_out", "gla_w_out", "gdn_w_out")


def _piece_rows(n):
    return -(-n // 32) * 32


def _pack(arrays, width, row_mult, dtype):
    flat = jnp.concatenate([a.astype(dtype).reshape(-1) for a in arrays])
    per = width * row_mult
    n = -(-flat.shape[0] // per) * per
    return jnp.pad(flat, (0, n - flat.shape[0])).reshape(n // width, width)


def _unpack(flat, shapes):
    out, off = [], 0
    for s in shapes:
        n = 1
        for d in s:
            n *= d
        out.append(flat[off:off + n].reshape(s))
        off += n
    return out


def _unpack_cols(flat2, shapes):
    out, off = [], 0
    for s in shapes:
        n = 1
        for d in s:
            n *= d
        out.append(flat2[:, off:off + n].reshape((flat2.shape[0],) + tuple(s)))
        off += n
    return out


def kernel(x, meta_tokens, norm_mix, norm_ffn, w_gate_up, w_down, fox_w_in, fox_b_f, fox_q_gain, fox_k_gain, fox_w_out, gla_w_in, gla_w_alpha2, gla_b_alpha, gla_o_gain, gla_w_out, gdn_w_in, gdn_conv_w, gdn_a_log, gdn_dt_bias, gdn_o_gain, gdn_w_out, loss_target, m_meta_tokens, m_norm_mix, m_norm_ffn, m_w_gate_up, m_w_down, m_fox_w_in, m_fox_b_f, m_fox_q_gain, m_fox_k_gain, m_fox_w_out, m_gla_w_in, m_gla_w_alpha2, m_gla_b_alpha, m_gla_o_gain, m_gla_w_out, m_gdn_w_in, m_gdn_conv_w, m_gdn_a_log, m_gdn_dt_bias, m_gdn_o_gain, m_gdn_w_out, v_meta_tokens, v_norm_mix, v_norm_ffn, v_w_gate_up, v_w_down, v_fox_w_in, v_fox_b_f, v_fox_q_gain, v_fox_k_gain, v_fox_w_out, v_gla_w_in, v_gla_w_alpha2, v_gla_b_alpha, v_gla_o_gain, v_gla_w_out, v_gdn_w_in, v_gdn_conv_w, v_gdn_a_log, v_gdn_dt_bias, v_gdn_o_gain, v_gdn_w_out):
    W = dict(meta_tokens=meta_tokens, norm_mix=norm_mix, norm_ffn=norm_ffn, w_gate_up=w_gate_up, w_down=w_down,
             fox_w_in=fox_w_in, fox_b_f=fox_b_f, fox_q_gain=fox_q_gain, fox_k_gain=fox_k_gain, fox_w_out=fox_w_out,
             gla_w_in=gla_w_in, gla_w_alpha2=gla_w_alpha2, gla_b_alpha=gla_b_alpha, gla_o_gain=gla_o_gain,
             gla_w_out=gla_w_out, gdn_w_in=gdn_w_in, gdn_conv_w=gdn_conv_w, gdn_a_log=gdn_a_log,
             gdn_dt_bias=gdn_dt_bias, gdn_o_gain=gdn_o_gain, gdn_w_out=gdn_w_out)
    M = dict(meta_tokens=m_meta_tokens, norm_mix=m_norm_mix, norm_ffn=m_norm_ffn, w_gate_up=m_w_gate_up, w_down=m_w_down,
             fox_w_in=m_fox_w_in, fox_b_f=m_fox_b_f, fox_q_gain=m_fox_q_gain, fox_k_gain=m_fox_k_gain,
             fox_w_out=m_fox_w_out, gla_w_in=m_gla_w_in, gla_w_alpha2=m_gla_w_alpha2, gla_b_alpha=m_gla_b_alpha,
             gla_o_gain=m_gla_o_gain, gla_w_out=m_gla_w_out, gdn_w_in=m_gdn_w_in, gdn_conv_w=m_gdn_conv_w,
             gdn_a_log=m_gdn_a_log, gdn_dt_bias=m_gdn_dt_bias, gdn_o_gain=m_gdn_o_gain, gdn_w_out=m_gdn_w_out)
    V = dict(meta_tokens=v_meta_tokens, norm_mix=v_norm_mix, norm_ffn=v_norm_ffn, w_gate_up=v_w_gate_up, w_down=v_w_down,
             fox_w_in=v_fox_w_in, fox_b_f=v_fox_b_f, fox_q_gain=v_fox_q_gain, fox_k_gain=v_fox_k_gain,
             fox_w_out=v_fox_w_out, gla_w_in=v_gla_w_in, gla_w_alpha2=v_gla_w_alpha2, gla_b_alpha=v_gla_b_alpha,
             gla_o_gain=v_gla_o_gain, gla_w_out=v_gla_w_out, gdn_w_in=v_gdn_w_in, gdn_conv_w=v_gdn_conv_w,
             gdn_a_log=v_gdn_a_log, gdn_dt_bias=v_gdn_dt_bias, gdn_o_gain=v_gdn_o_gain, gdn_w_out=v_gdn_w_out)
    chip = 2 * lax.axis_index("x") + lax.axis_index("y")

    pieces, offs, r = [], {}, FFN_ROWS
    for n in _IN_W:
        nc = W[n].shape[2]
        for l in range(W[n].shape[0]):
            pieces.append(jnp.pad(W[n][l].T.astype(BF16), ((0, _piece_rows(nc) - nc), (0, 0))))
            offs[n, l] = r
            r += _piece_rows(nc)
    for n in _OUT_W:
        for l in range(W[n].shape[0]):
            pieces.append(W[n][l].astype(BF16))
            offs[n, l] = r
            r += W[n].shape[1]
    rows = -(-r // _PACK_ROWS) * _PACK_ROWS
    packed = jnp.concatenate([jnp.swapaxes(w_gate_up, 1, 2).reshape(-1, D).astype(BF16), w_down.reshape(-1, D).astype(BF16)]
                             + pieces + [jnp.zeros((rows - r, D), BF16)], axis=0)
    first_rows = [(offs["fox_w_in", 0], offs["fox_w_in", 1] - offs["fox_w_in", 0]),
                  (offs["fox_w_out", 0], offs["fox_w_out", 1] - offs["fox_w_out", 0])]
    later_rows = [(0, FFN_ROWS), (offs["fox_w_in", 1], offs["fox_w_out", 0] - offs["fox_w_in", 1]),
                  (offs["fox_w_out", 1], r - offs["fox_w_out", 1])]
    wpk = _ag_weights(lax.dynamic_update_slice(lax.empty((4, rows, D), BF16), packed[None], (chip, 0, 0)), first_rows)

    def in_t(buf, n, l, npad):
        nc = W[n].shape[2]
        return jnp.concatenate([buf[q, offs[n, l]:offs[n, l] + nc] for q in range(4)] + [jnp.zeros((npad - 4 * nc, D), BF16)], 0)

    def out_w(buf, n, l):
        return jnp.concatenate([buf[q, offs[n, l]:offs[n, l] + W[n].shape[1]] for q in range(4)], axis=0)

    fox_in0, fox_out0 = in_t(wpk, "fox_w_in", 0, FOX_INP), out_w(wpk, "fox_w_out", 0)
    full = {}
    small = _pack([W[n] for n, _ in _SMALL_SHARDED], LANES, 8, F32)
    small_all = _gather8(small, reduce=False, name="gather_small").reshape(8, -1)
    for (n, ax), seg in zip(_SMALL_SHARDED, _unpack_cols(small_all, [W[n].shape for n, _ in _SMALL_SHARDED])):
        full[n] = jnp.concatenate([seg[2 * q] for q in range(4)], axis=ax)
    fox_in, full["fox_w_out"] = [fox_in0], [fox_out0]
    w_alpha2, conv_w = full["gla_w_alpha2"][0], full["gdn_conv_w"][0]

    h = jnp.concatenate([jnp.zeros((META0, D), F32), full["meta_tokens"], x[0]], axis=0)
    saved = []
    y = _rms_fwd(h, norm_mix[0], name="norm_mix0")
    for i in range(DEPTH):
        kind, j = i % 3, i // 3
        if kind == 0:
            proj = _mm(y, fox_in[j], tb=True, name=f"fox_in{j}")
            qa, ka, va = _fox_prep(proj, fox_b_f[j], fox_q_gain[j], fox_k_gain[j])
            if i == 0:
                o, og, lse, wpk = _fox_attn_fwd(qa, ka, va, proj, ag=(wpk, later_rows))
                wpk = _ag_forward(wpk, later_rows)
                fox_in += [in_t(wpk, "fox_w_in", l, FOX_INP) for l in range(1, fox_w_in.shape[0])]
                full["fox_w_out"] += [out_w(wpk, "fox_w_out", l) for l in range(1, fox_w_out.shape[0])]
                gla_in = [in_t(wpk, "gla_w_in", l, GLA_INP) for l in range(gla_w_in.shape[0])]
                gdn_in = [in_t(wpk, "gdn_w_in", l, GDN_INP) for l in range(gdn_w_in.shape[0])]
                for n in ("gla_w_out", "gdn_w_out"):
                    full[n] = [out_w(wpk, n, l) for l in range(W[n].shape[0])]
            else:
                o, og, lse = _fox_attn_fwd(qa, ka, va, proj)
            w_out, mix = full["fox_w_out"][j], (proj, qa, ka, va, o, lse)
        elif kind == 1:
            proj = _mm(y, gla_in[j], tb=True, name=f"gla_in{j}")
            o, og, states = _gla_fwd(proj, w_alpha2, gla_b_alpha[j], gla_o_gain[j])
            w_out, mix = full["gla_w_out"][j], (proj, o, states)
        else:
            proj = _mm(y, gdn_in[j], tb=True, name=f"gdn_in{j}")
            o, og, states = _gdn_fwd(proj, conv_w, gdn_a_log[j], gdn_dt_bias[j], gdn_o_gain[j])
            w_out, mix = full["gdn_w_out"][j], (proj, o, states)
        hm, yf = _mm(og, w_out, add=h, norm=norm_ffn[i], name=f"mix_out{i}")
        gate, up, act = _ffn_up(yf, wpk, i)
        hn, y_next = _ffn_down(act, wpk, i, hm, norm_mix[(i + 1) % DEPTH])
        saved.append((h, y, mix, og, w_out, hm, yf, gate, up, act))
        h, y = hn, y_next
    dh, loss_tile = _loss_head(h, loss_target[0])

    G = {n: [None] * W[n].shape[0] for n in _WEIGHTS if n not in ("meta_tokens", "w_gate_up", "w_down") + _IN_W}
    GT = {}

    def grad_layout(ffn_layers, pieces):
        off, end = {}, 0
        for l in ffn_layers:
            off["gu", l] = end
            end += GU_ROWS
        for l in ffn_layers:
            off["down", l] = end
            end += DOWN_ROWS
        for n, l in pieces:
            off[n, l] = end
            end += _piece_rows(W[n].shape[2]) if n in _IN_W else W[n].shape[1]
        return off, end, -(-end // _PACK_ROWS) * _PACK_ROWS

    first_pieces = [("fox_w_in", 0)]
    later_pieces = [(n, l) for n in _IN_W + _OUT_W for l in range(W[n].shape[0]) if (n, l) not in first_pieces]
    layouts = [grad_layout([], first_pieces), grad_layout(list(range(DEPTH)), later_pieces)]
    gbuf = [jnp.zeros((4, lay[2], D), BF16) for lay in layouts]

    def with_pieces(buf, lay, pieces):
        off, end, total = lay
        blocks = []
        for q in range(4):
            parts = []
            for n, l in pieces:
                if n in _IN_W:
                    nc = W[n].shape[2]
                    parts.append(jnp.pad(GT[n, l][q * nc:(q + 1) * nc], ((0, _piece_rows(nc) - nc), (0, 0))))
                else:
                    nr = W[n].shape[1]
                    parts.append(G[n][l][q * nr:(q + 1) * nr])
            blocks.append(jnp.concatenate(parts + [jnp.zeros((total - end, D), BF16)], axis=0))
        return lax.dynamic_update_slice(buf, jnp.stack(blocks), (0, off[pieces[0]], 0))

    s_later = None
    for i in reversed(range(DEPTH)):
        kind, j = i % 3, i // 3
        h_in, y, mix, og, w_out, hm, yf, gate, up, act = saved[i]
        b = 1
        dg, du = _ffn_dact(dh, wpk, i, gate, up)
        gbuf[b] = _ffn_dw_down(act, dh, gbuf[b], i, layouts[b][0]["down", i])
        dhm, dnf = _ffn_dyf(dg, du, wpk, i, hm, norm_ffn[i], dh)
        gbuf[b] = _ffn_dw_gu(dg, du, yf, gbuf[b], i, layouts[b][0]["gu", i] // GU_ROWS)
        G["norm_ffn"][i] = dnf[0]
        dog = _mm(dhm, w_out, tb=True, name=f"d_og{i}")
        dw_out = _mm(og, dhm, ta=True, out_dtype=BF16, name=f"d_w_out{i}")
        if kind == 0:
            proj, qa, ka, va, o, lse = mix
            doa, q2, dgate = _fox_gate_bwd(dog, o, proj, lse, qa)
            G["fox_w_out"][j] = dw_out
            if i == 0:
                s_later = _rs_local(with_pieces(gbuf[1], layouts[1], later_pieces), "_later")
                dqn, dkn, dv, dct, recv_later = _fox_attn_bwd(q2, ka, va, doa, rs=s_later)
            else:
                dqn, dkn, dv, dct = _fox_attn_bwd(q2, ka, va, doa)
            dproj, dqg, dkg, dbf = _fox_prep_bwd(proj, fox_b_f[j], fox_q_gain[j], fox_k_gain[j], dqn, dkn, dv, dgate, dct)
            G["fox_q_gain"][j] = dqg.reshape(FOX_H, FOX_DH).sum(0)
            G["fox_k_gain"][j] = dkg.reshape(FOX_H, FOX_DH).sum(0)
            G["fox_b_f"][j] = dbf[0, :FOX_H]
            w_in, wname = fox_in[j], "fox_w_in"
        elif kind == 1:
            proj, o, states = mix
            dproj, dwa, dba, dogain = _gla_bwd(proj, w_alpha2, gla_b_alpha[j], gla_o_gain[j], o, states, dog)
            G["gla_w_out"][j] = dw_out
            G["gla_w_alpha2"][j] = dwa[:GLA_RANK]
            G["gla_b_alpha"][j] = dba[0]
            G["gla_o_gain"][j] = dogain[0]
            w_in, wname = gla_in[j], "gla_w_in"
        else:
            proj, o, states = mix
            dproj, dcw, dal, ddt, dogain = _gdn_bwd(proj, conv_w, gdn_a_log[j], gdn_dt_bias[j], gdn_o_gain[j], o, states, dog)
            G["gdn_w_out"][j] = dw_out
            G["gdn_conv_w"][j] = dcw[:4].reshape(4, 1, GDN_CONV)
            G["gdn_a_log"][j] = dal[0, :GDN_H]
            G["gdn_dt_bias"][j] = ddt[0, :GDN_H]
            G["gdn_o_gain"][j] = dogain[0]
            w_in, wname = gdn_in[j], "gdn_w_in"
        dh, dnm = _mm(dproj, w_in, rms_bwd=(h_in, norm_mix[i], dhm), name=f"d_y{i}")
        GT[wname, j] = _mm(dproj, y, ta=True, out_dtype=BF16, name=f"d_w_in{i}")
        G["norm_mix"][i] = dnm[0]
    grad_x = dh[ROW0:][None]
    G = {n: (v if n in _OUT_W else jnp.stack(v)) for n, v in G.items()}
    G["meta_tokens"] = dh[META0:ROW0]

    s_first = _rs_local(with_pieces(gbuf[0], layouts[0], first_pieces), "_first")
    reduced = [_rs_finish(s_first, _scatter_chips(s_first, "_first"), "_first"), _rs_finish(s_later, recv_later, "_later")]

    def reduced_piece(n, l):
        b = 0 if (n, l) in first_pieces else 1
        start = layouts[b][0][n, l]
        return reduced[b][start:start + (W[n].shape[2] if n in _IN_W else W[n].shape[1])]

    grads = {}
    for n in _IN_W:
        grads[n] = jnp.stack([reduced_piece(n, l).T for l in range(W[n].shape[0])])
    for n in _OUT_W:
        grads[n] = jnp.stack([reduced_piece(n, l) for l in range(W[n].shape[0])])
    small_names = [n for n, _ in _SMALL_SHARDED] + list(_REPLICATED)
    small_g = _pack([G[n] for n in small_names] + [loss_tile[0, 0:1]], LANES, 8, F32)
    small_sum = _gather8(small_g, reduce=True, name="allreduce_small").reshape(-1)
    small_shapes = [G[n].shape for n in small_names] + [(1,)]
    small_vals = _unpack(small_sum, small_shapes)
    loss = small_vals[-1][0]
    for n, val in zip(small_names, small_vals[:-1]):
        grads[n] = val
    for n, ax in _SMALL_SHARDED:
        sz = W[n].shape[ax]
        grads[n] = lax.dynamic_slice_in_dim(grads[n], chip * sz, sz, axis=ax)

    delta, new_m, new_v = {}, {}, {}
    for n, key, tr_ in (("w_gate_up", "gu", True), ("w_down", "down", False)):
        grads[n], delta[n], new_m[n], new_v[n] = _adamw_packed(
            W[n], reduced[1], reduced[1], M[n], V[n], row0=layouts[1][0][key, 0], row_off=layouts[1][0][key, 1],
            transposed=tr_, name=f"adamw_{n}")
    for n in _IN_W + _OUT_W:
        delta[n], new_m[n], new_v[n] = _adamw(W[n], grads[n], M[n], V[n], name=f"adamw_{n}")
    tiny = [n for n in _WEIGHTS if n not in ("w_gate_up", "w_down") + _IN_W + _OUT_W]
    packs = [_pack([T[n] for n in tiny], LANES, 8, F32) for T in (W, grads, M, V)]
    outs = _adamw(*packs, name="adamw_small")
    shapes = [W[n].shape for n in tiny]
    for dst, o in zip((delta, new_m, new_v), outs):
        for n, val in zip(tiny, _unpack(o.reshape(-1), shapes)):
            dst[n] = val
    return (loss, grad_x, *[grads[n] for n in _WEIGHTS], *[delta[n] for n in _WEIGHTS],
            *[new_m[n] for n in _WEIGHTS], *[new_v[n] for n in _WEIGHTS])
```

```python
import jax
import jax.numpy as jnp
from jax import lax
from jax.experimental import pallas as pl
from jax.experimental.pallas import tpu as pltpu

F32, BF16 = jnp.float32, jnp.bfloat16
D = 1024
N_META = 16
ROW0 = 128
META0 = ROW0 - N_META
EPS = 1e-6
LANES = 128
VMEM_LIMIT = 56 * 1024 * 1024

FOX_H, FOX_DH = 16, 64
FOX_INP = 4224
GLA_H, GLA_DK, GLA_DV, GLA_RANK = 4, 128, 256, 16
GLA_QK, GLA_V = 512, 1024
GLA_INP = 3200
GLA_NORM = 16.0
GDN_H, GDN_DK, GDN_DV = 8, 128, 128
GDN_CONV = 3072
GDN_INP = 4224
CHUNK = 64
D_FF = 2816
DEPTH = 4

ADAM_LR, ADAM_B1, ADAM_B2, ADAM_EPS, ADAM_WD, ADAM_STEP = 0.001, 0.9, 0.999, 1e-08, 0.01, 10

MESH = pl.DeviceIdType.MESH
ANY = pl.BlockSpec(memory_space=pl.ANY)
VM = pl.BlockSpec(memory_space=pltpu.VMEM)


def _params(sem=None, **kw):
    if sem is not None:
        kw["dimension_semantics"] = sem
    return pltpu.CompilerParams(vmem_limit_bytes=VMEM_LIMIT, **kw)


def _tile(n, cap, mult=LANES):
    best = None
    for t in range(mult, min(n, cap) + 1, mult):
        if n % t == 0:
            best = t
    return best if best is not None else n


def nn(a, b, **kw):
    return jnp.dot(a, b, preferred_element_type=F32, **kw)


def nt(a, b, **kw):
    return lax.dot_general(a, b, (((1,), (1,)), ((), ())), preferred_element_type=F32, **kw)


def tn(a, b, **kw):
    return lax.dot_general(a, b, (((0,), (0,)), ((), ())), preferred_element_type=F32, **kw)


def _split3(x):
    hi = x.astype(BF16)
    r = x - hi.astype(F32)
    mid = r.astype(BF16)
    lo = (r - mid.astype(F32)).astype(BF16)
    return hi, mid, lo


def _sel_l(sel, x):
    a, b, c = _split3(x)
    return nn(sel, a) + nn(sel, b) + nn(sel, c)


def _sel_r(x, sel):
    a, b, c = _split3(x)
    return nn(a, sel) + nn(b, sel) + nn(c, sel)


def _sel_r2(x, sel):
    a = x.astype(BF16)
    return nn(a, sel) + nn((x - a.astype(F32)).astype(BF16), sel)


def _iota(shape, dim):
    return lax.broadcasted_iota(jnp.int32, shape, dim)


def _tri(n, upper=False, strict=False):
    i, j = _iota((n, n), 0), _iota((n, n), 1)
    if upper:
        m = (j > i) if strict else (j >= i)
    else:
        m = (j < i) if strict else (j <= i)
    return m


def _sigmoid(x):
    return 1.0 / (1.0 + jnp.exp(-x))


def _log_sigmoid(x):
    return jnp.minimum(x, 0.0) - jnp.log(1.0 + jnp.exp(-jnp.abs(x)))


def _softplus(x):
    return jnp.maximum(x, 0.0) + jnp.log(1.0 + jnp.exp(-jnp.abs(x)))


def _silu(x):
    return x * _sigmoid(x)


def _dsilu(x):
    s = _sigmoid(x)
    return s * (1.0 + x * (1.0 - s))


def _rms(x, g):
    return (x * lax.rsqrt(jnp.mean(x * x, axis=-1, keepdims=True) + EPS) * g).astype(BF16)


def _rms_grad(x, g, dy):
    r = lax.rsqrt(jnp.mean(x * x, axis=-1, keepdims=True) + EPS)
    u = dy * g
    return r * u - x * (r * r * r) * jnp.mean(x * u, axis=-1, keepdims=True), jnp.sum(dy * x * r, axis=0, keepdims=True)


def _mm(a, b, *, ta=False, tb=False, add=None, norm=None, rms_bwd=None, out_dtype=F32, name):
    m, k = (a.shape[1], a.shape[0]) if ta else a.shape
    n = b.shape[0] if tb else b.shape[1]
    assert k == (b.shape[1] if tb else b.shape[0])
    rows_whole = norm is not None or rms_bwd is not None
    tm, tn_, tk = _tile(m, 704 if rows_whole else 1408, LANES if ta else 16), _tile(n, 1408), _tile(k, 1408)
    nk = k // tk
    assert not rows_whole or tn_ == n

    def body(*refs):
        refs = list(refs)
        a_ref, b_ref = refs[:2]
        extra = refs[2:-1]
        acc = refs[-1]
        i, kk = pl.program_id(0), pl.program_id(2)

        @pl.when(kk == 0)
        def _():
            acc[...] = jnp.zeros_like(acc)

        av, bv = a_ref[...].astype(BF16), b_ref[...].astype(BF16)
        dims = (((0,) if ta else (1,), (1,) if tb else (0,)), ((), ()))
        acc[...] += lax.dot_general(av, bv, dims, preferred_element_type=F32)

        @pl.when(kk == nk - 1)
        def _():
            r = acc[...]
            if rms_bwd is not None:
                h_ref, g_ref, dres_ref, o_ref, dg_ref = extra
                dx, dgain = _rms_grad(h_ref[...], g_ref[...], r)
                o_ref[...] = dres_ref[...] + dx

                @pl.when(i == 0)
                def _():
                    dg_ref[...] = jnp.zeros_like(dg_ref)

                dg_ref[...] += dgain
                return
            if add is not None:
                r = r + extra[0][...].astype(F32)
            if norm is not None:
                g_ref, o_ref, y_ref = extra[-3:]
                y_ref[...] = _rms(r, g_ref[...])
            else:
                o_ref = extra[-1]
            o_ref[...] = r.astype(out_dtype)

    a_spec = pl.BlockSpec((tk, tm), lambda i, j, q: (q, i)) if ta else pl.BlockSpec((tm, tk), lambda i, j, q: (i, q))
    b_spec = pl.BlockSpec((tn_, tk), lambda i, j, q: (j, q)) if tb else pl.BlockSpec((tk, tn_), lambda i, j, q: (q, j))
    o_spec = pl.BlockSpec((tm, tn_), lambda i, j, q: (i, j))
    g_spec = pl.BlockSpec((1, n), lambda i, j, q: (0, 0))
    ins, specs = [a, b], [a_spec, b_spec]
    out_specs, out_shape = o_spec, jax.ShapeDtypeStruct((m, n), out_dtype)
    sem = ("parallel", "parallel", "arbitrary")
    if rms_bwd is not None:
        ins += [rms_bwd[0], rms_bwd[1].reshape(1, n), rms_bwd[2]]
        specs += [o_spec, g_spec, o_spec]
        out_specs, out_shape = [o_spec, g_spec], [jax.ShapeDtypeStruct((m, n), F32), jax.ShapeDtypeStruct((1, n), F32)]
        sem = ("arbitrary", "arbitrary", "arbitrary")
    else:
        if add is not None:
            ins.append(add)
            specs.append(o_spec)
        if norm is not None:
            ins.append(norm.reshape(1, n))
            specs.append(g_spec)
            out_specs, out_shape = [o_spec, o_spec], [out_shape, jax.ShapeDtypeStruct((m, n), BF16)]
    return pl.pallas_call(
        body, name=name, grid=(m // tm, n // tn_, nk), in_specs=specs, out_specs=out_specs, out_shape=out_shape,
        scratch_shapes=[pltpu.VMEM((tm, tn_), F32)], compiler_params=_params(sem),
    )(*ins)


def _rms_fwd(h, g, *, name):
    lp = h.shape[0]
    tr = _tile(lp, 512)

    def body(h_ref, g_ref, y_ref):
        x = h_ref[...]
        r = lax.rsqrt(jnp.mean(x * x, axis=-1, keepdims=True) + EPS)
        y_ref[...] = (x * r * g_ref[...]).astype(BF16)

    return pl.pallas_call(
        body, name=name, grid=(lp // tr,),
        in_specs=[pl.BlockSpec((tr, D), lambda i: (i, 0)), pl.BlockSpec((1, D), lambda i: (0, 0))],
        out_specs=pl.BlockSpec((tr, D), lambda i: (i, 0)),
        out_shape=jax.ShapeDtypeStruct((lp, D), BF16), compiler_params=_params(("parallel",)),
    )(h, g.reshape(1, D))


GU_ROWS, DOWN_ROWS = 1408, 704
OFF_GU, OFF_DOWN = 0, DEPTH * GU_ROWS
FFN_ROWS = DEPTH * (GU_ROWS + DOWN_ROWS)
FFN_TM = 704


def _gu_spec(fn):
    return pl.BlockSpec((None, GU_ROWS, D), fn)


def _down_spec(fn):
    return pl.BlockSpec((None, DOWN_ROWS, D), fn)


def _down_pair(w0_ref, w1_ref):
    return jnp.concatenate([w0_ref[...], w1_ref[...]], axis=0)


def _ffn_up(yf, wpk, layer):
    lp = yf.shape[0]
    tm = _tile(lp, FFN_TM, 16)

    def body(y_ref, wg_ref, wu_ref, g_ref, u_ref, a_ref):
        y = y_ref[...]
        g, u = nt(y, wg_ref[...]), nt(y, wu_ref[...])
        g_ref[...] = g.astype(BF16)
        u_ref[...] = u.astype(BF16)
        a_ref[...] = (_silu(g) * u).astype(BF16)

    o = pl.BlockSpec((tm, GU_ROWS), lambda i, j: (i, j))
    return pl.pallas_call(
        body, name=f"ffn_up{layer}", grid=(lp // tm, 2),
        in_specs=[pl.BlockSpec((tm, D), lambda i, j: (i, 0)), _gu_spec(lambda i, j: (j, OFF_GU // GU_ROWS + layer, 0)),
                  _gu_spec(lambda i, j: (2 + j, OFF_GU // GU_ROWS + layer, 0))],
        out_specs=[o, o, o], out_shape=[jax.ShapeDtypeStruct((lp, D_FF), BF16)] * 3,
        compiler_params=_params(("parallel", "parallel")),
    )(yf, wpk, wpk)


def _ffn_down(act, wpk, layer, res, norm):
    lp = act.shape[0]
    tm = _tile(lp, FFN_TM, 16)

    def body(a_ref, w0_ref, w1_ref, r_ref, g_ref, o_ref, y_ref, acc):
        kk = pl.program_id(1)

        @pl.when(kk == 0)
        def _():
            acc[...] = r_ref[...]

        acc[...] += nn(a_ref[...], _down_pair(w0_ref, w1_ref))

        @pl.when(kk == 1)
        def _():
            o_ref[...] = acc[...]
            y_ref[...] = _rms(acc[...], g_ref[...])

    o = pl.BlockSpec((tm, D), lambda i, kk: (i, 0))
    blk = OFF_DOWN // DOWN_ROWS + layer
    return pl.pallas_call(
        body, name=f"ffn_down{layer}", grid=(lp // tm, 2),
        in_specs=[pl.BlockSpec((tm, GU_ROWS), lambda i, kk: (i, kk)), _down_spec(lambda i, kk: (2 * kk, blk, 0)),
                  _down_spec(lambda i, kk: (2 * kk + 1, blk, 0)), o, pl.BlockSpec((1, D), lambda i, kk: (0, 0))],
        out_specs=[o, o], out_shape=[jax.ShapeDtypeStruct((lp, D), F32), jax.ShapeDtypeStruct((lp, D), BF16)],
        scratch_shapes=[pltpu.VMEM((tm, D), F32)], compiler_params=_params(("parallel", "arbitrary")),
    )(act, wpk, wpk, res, norm.reshape(1, D))


def _ffn_dact(dh, wpk, layer, gate, up):
    lp = dh.shape[0]
    tm = _tile(lp, FFN_TM, 16)

    def body(d_ref, w0_ref, w1_ref, g_ref, u_ref, dg_ref, du_ref):
        da = nt(d_ref[...].astype(BF16), _down_pair(w0_ref, w1_ref))
        g, u = g_ref[...].astype(F32), u_ref[...].astype(F32)
        sg = _sigmoid(g)
        dg_ref[...] = (da * u * (sg * (1.0 + g * (1.0 - sg)))).astype(BF16)
        du_ref[...] = (da * (g * sg)).astype(BF16)

    o = pl.BlockSpec((tm, GU_ROWS), lambda i, j: (i, j))
    blk = OFF_DOWN // DOWN_ROWS + layer
    return pl.pallas_call(
        body, name=f"d_act{layer}", grid=(lp // tm, 2),
        in_specs=[pl.BlockSpec((tm, D), lambda i, j: (i, 0)), _down_spec(lambda i, j: (2 * j, blk, 0)),
                  _down_spec(lambda i, j: (2 * j + 1, blk, 0)), o, o],
        out_specs=[o, o], out_shape=[jax.ShapeDtypeStruct((lp, D_FF), BF16)] * 2,
        compiler_params=_params(("parallel", "parallel")),
    )(dh, wpk, wpk, gate, up)


def _ffn_dyf(dg, du, wpk, layer, hm, norm, dres):
    lp = dg.shape[0]
    tm = _tile(lp, FFN_TM, 16)

    def body(dg_ref, du_ref, w_ref, h_ref, g_ref, dres_ref, o_ref, dgain_ref, acc):
        i, kk = pl.program_id(0), pl.program_id(1)

        @pl.when(kk == 0)
        def _():
            acc[...] = jnp.zeros_like(acc)

        @pl.when(kk < 2)
        def _():
            acc[...] += nn(dg_ref[...], w_ref[...])

        @pl.when(kk >= 2)
        def _():
            acc[...] += nn(du_ref[...], w_ref[...])

        @pl.when(kk == 3)
        def _():
            dx, dgain = _rms_grad(h_ref[...], g_ref[...], acc[...])
            o_ref[...] = dres_ref[...] + dx

            @pl.when(i == 0)
            def _():
                dgain_ref[...] = jnp.zeros_like(dgain_ref)

            dgain_ref[...] += dgain

    o = pl.BlockSpec((tm, D), lambda i, kk: (i, 0))
    row = pl.BlockSpec((1, D), lambda i, kk: (0, 0))
    return pl.pallas_call(
        body, name=f"d_yf{layer}", grid=(lp // tm, 4),
        in_specs=[pl.BlockSpec((tm, GU_ROWS), lambda i, kk: (i, jnp.minimum(kk, 1))),
                  pl.BlockSpec((tm, GU_ROWS), lambda i, kk: (i, jnp.maximum(kk - 2, 0))),
                  _gu_spec(lambda i, kk: (kk, OFF_GU // GU_ROWS + layer, 0)), o, row, o],
        out_specs=[o, row], out_shape=[jax.ShapeDtypeStruct((lp, D), F32), jax.ShapeDtypeStruct((1, D), F32)],
        scratch_shapes=[pltpu.VMEM((tm, D), F32)], compiler_params=_params(("arbitrary", "arbitrary")),
    )(dg, du, wpk, hm, norm.reshape(1, D), dres)


def _ffn_dw_down(act, dh, gpk, layer, row):
    lp = act.shape[0]
    tk = _tile(lp, 1408, 16)
    nk = lp // tk

    def body(a_ref, d_ref, g_in, g_out, acc, stage, sems):
        jp, kk = pl.program_id(0), pl.program_id(1)

        @pl.when(kk == 0)
        def _():
            acc[...] = jnp.zeros_like(acc)

        acc[...] += tn(a_ref[...], d_ref[...].astype(BF16))

        @pl.when(kk == nk - 1)
        def _():
            stage[...] = acc[...].astype(BF16)
            copies = [pltpu.make_async_copy(stage.at[pl.ds(hf * DOWN_ROWS, DOWN_ROWS), :],
                                            g_out.at[2 * jp + hf, pl.ds(row, DOWN_ROWS), :], sems.at[hf]) for hf in range(2)]
            for cp in copies:
                cp.start()
            for cp in copies:
                cp.wait()

    return pl.pallas_call(
        body, name=f"d_w_down{layer}", grid=(2, nk),
        in_specs=[pl.BlockSpec((tk, GU_ROWS), lambda jp, kk: (kk, jp)), pl.BlockSpec((tk, D), lambda jp, kk: (kk, 0)), ANY],
        out_specs=ANY, out_shape=jax.ShapeDtypeStruct(gpk.shape, gpk.dtype),
        scratch_shapes=[pltpu.VMEM((GU_ROWS, D), F32), pltpu.VMEM((GU_ROWS, D), BF16), pltpu.SemaphoreType.DMA((2,))],
        input_output_aliases={2: 0}, compiler_params=_params(("arbitrary", "arbitrary")),
    )(act, dh, gpk)


def _ffn_dw_gu(dg, du, yf, gpk, layer, blk):
    lp = dg.shape[0]
    tk = _tile(lp, 1408, 16)
    nk = lp // tk

    def body(dg_ref, du_ref, y_ref, g_in, o_ref, acc):
        c, kk = pl.program_id(0), pl.program_id(1)

        @pl.when(kk == 0)
        def _():
            acc[...] = jnp.zeros_like(acc)

        @pl.when(c < 2)
        def _():
            acc[...] += tn(dg_ref[...], y_ref[...])

        @pl.when(c >= 2)
        def _():
            acc[...] += tn(du_ref[...], y_ref[...])

        @pl.when(kk == nk - 1)
        def _():
            o_ref[...] = acc[...].astype(BF16)

    return pl.pallas_call(
        body, name=f"d_w_gate_up{layer}", grid=(4, nk),
        in_specs=[pl.BlockSpec((tk, GU_ROWS), lambda c, kk: (kk, jnp.minimum(c, 1))),
                  pl.BlockSpec((tk, GU_ROWS), lambda c, kk: (kk, jnp.maximum(c - 2, 0))),
                  pl.BlockSpec((tk, D), lambda c, kk: (kk, 0)), ANY],
        out_specs=_gu_spec(lambda c, kk: (c, blk, 0)),
        out_shape=jax.ShapeDtypeStruct(gpk.shape, gpk.dtype),
        scratch_shapes=[pltpu.VMEM((GU_ROWS, D), F32)], input_output_aliases={3: 0},
        compiler_params=_params(("parallel", "arbitrary")),
    )(dg, du, yf, gpk)


def _loss_head(h, target):
    lp = h.shape[0]
    nb = lp // ROW0

    def body(h_ref, t_ref, dh_ref, l_ref):
        i = pl.program_id(0)

        @pl.when(i == 0)
        def _():
            l_ref[...] = jnp.zeros_like(l_ref)
            dh_ref[...] = jnp.zeros_like(dh_ref)

        @pl.when(i > 0)
        def _():
            err = h_ref[...] - t_ref[...]
            dh_ref[...] = err * (1.0 / D)
            l_ref[...] += jnp.sum(err * err) * (0.5 / D)

    return pl.pallas_call(
        body, name="loss_head", grid=(nb,),
        in_specs=[pl.BlockSpec((ROW0, D), lambda i: (i, 0)), pl.BlockSpec((ROW0, D), lambda i: (jnp.maximum(i - 1, 0), 0))],
        out_specs=[pl.BlockSpec((ROW0, D), lambda i: (i, 0)), pl.BlockSpec((8, LANES), lambda i: (0, 0))],
        out_shape=[jax.ShapeDtypeStruct((lp, D), F32), jax.ShapeDtypeStruct((8, LANES), F32)],
        compiler_params=_params(("arbitrary",)),
    )(h, target)


def _adamw(w, g, m, v, *, name):
    if w.ndim == 2:
        w, g, m, v = (t[None] for t in (w, g, m, v))
        return tuple(o[0] for o in _adamw(w, g, m, v, name=name))
    nl, r, c = w.shape
    tr = _tile(r, max(8, (1 << 19) // c), 8)

    def body(w_ref, g_ref, m_ref, v_ref, d_ref, nm_ref, nv_ref):
        d_ref[...], nm_ref[...], nv_ref[...] = _adam_math(w_ref[...], g_ref[...], m_ref[...], v_ref[...])

    spec = pl.BlockSpec((1, tr, c), lambda l, i: (l, i, 0))
    return tuple(pl.pallas_call(
        body, name=name, grid=(nl, r // tr), in_specs=[spec] * 4, out_specs=[spec] * 3,
        out_shape=[jax.ShapeDtypeStruct(w.shape, F32)] * 3, compiler_params=_params(("parallel", "parallel")),
    )(w, g, m, v))


def _adam_math(w, g, m, v):
    nm = ADAM_B1 * m + (1.0 - ADAM_B1) * g
    nv = ADAM_B2 * v + (1.0 - ADAM_B2) * (g * g)
    m_hat = nm / (1.0 - ADAM_B1 ** ADAM_STEP)
    v_hat = nv / (1.0 - ADAM_B2 ** ADAM_STEP)
    return -ADAM_LR * (m_hat / (jnp.sqrt(v_hat) + ADAM_EPS) + ADAM_WD * w), nm, nv


def _adamw_packed(w, gred0, gred, m, v, *, row0, row_off, transposed, name):
    nl, a, b = w.shape
    nr = b if transposed else a
    later = lambda l: row_off // nr + jnp.maximum(l - 1, 0)
    if transposed:
        ta = _tile(a, 256)
        wspec = pl.BlockSpec((1, ta, b), lambda l, r: (l, r, 0))
        g0spec = pl.BlockSpec((b, ta), lambda l, r: (row0 // nr, r))
        gspec = pl.BlockSpec((b, ta), lambda l, r: (later(l), r))
        grid = (nl, a // ta)
    else:
        wspec = pl.BlockSpec((1, a, b), lambda l, r: (l, 0, 0))
        g0spec = pl.BlockSpec((a, b), lambda l, r: (row0 // nr, 0))
        gspec = pl.BlockSpec((a, b), lambda l, r: (later(l), 0))
        grid = (nl, 1)

    def body(w_ref, g0_ref, g_ref, m_ref, v_ref, go_ref, d_ref, nm_ref, nv_ref):
        g = jnp.where(pl.program_id(0) == 0, g0_ref[...], g_ref[...])
        g = g.T if transposed else g
        d, nm, nv = _adam_math(w_ref[0], g, m_ref[0], v_ref[0])
        go_ref[0], d_ref[0], nm_ref[0], nv_ref[0] = g, d, nm, nv

    return pl.pallas_call(
        body, name=name, grid=grid, in_specs=[wspec, g0spec, gspec, wspec, wspec], out_specs=[wspec] * 4,
        out_shape=[jax.ShapeDtypeStruct(w.shape, F32)] * 4, compiler_params=_params(("parallel", "parallel")),
    )(w, gred0, gred, m, v)


FOX_AUG = FOX_H * LANES
L_C = 64
L_K = 67
L_LSE = 70
PAD_KEY = -30000.0
FOX_TQ = 384


def _head_sel(n_heads, width, lanes=LANES):
    r, c = _iota((n_heads * width, lanes), 0), _iota((n_heads * width, lanes), 1)
    down = (r // width == c).astype(BF16)
    r2, c2 = _iota((lanes, n_heads * width), 0), _iota((lanes, n_heads * width), 1)
    up = (c2 // width == r2).astype(BF16)
    return down, up


def _place(lane0):
    r, c = _iota((LANES, FOX_AUG), 0), _iota((LANES, FOX_AUG), 1)
    return [((c // LANES == r) & (c % LANES == lane0 + m)).astype(BF16) for m in range(3)]


def _placed(x, lane0):
    pcs = _split3(x)
    mats = _place(lane0)
    return nn(pcs[0], mats[0]) + nn(pcs[1], mats[1]) + nn(pcs[2], mats[2])


def _ones_at(rows, lanes):
    c = _iota((rows, FOX_AUG), 1) % LANES
    m = c == lanes[0]
    for l in lanes[1:]:
        m = m | (c == l)
    return m.astype(F32)


def _spread(x, extras, out_ref):
    rows = x.shape[0]
    left = _iota((rows, LANES), 1) < FOX_DH
    for p in range(FOX_H // 2):
        slab = x[:, p * LANES:(p + 1) * LANES]
        a = jnp.where(left, slab, extras[:, 2 * p * LANES:(2 * p + 1) * LANES])
        b = jnp.where(left, pltpu.roll(slab, FOX_DH, 1), extras[:, (2 * p + 1) * LANES:(2 * p + 2) * LANES])
        out_ref[:, 2 * p * LANES:(2 * p + 1) * LANES] = a.astype(BF16)
        out_ref[:, (2 * p + 1) * LANES:(2 * p + 2) * LANES] = b.astype(BF16)


def _fox_prep(proj, b_f, q_gain, k_gain):
    lp = proj.shape[0]
    nb = lp // LANES

    def body(p_ref, bf_ref, qg_ref, kg_ref, q_ref, k_ref, v_ref, carry):
        i = pl.program_id(0)

        @pl.when(i == 0)
        def _():
            carry[...] = jnp.zeros_like(carry)

        down, up = _head_sel(FOX_H, FOX_DH)

        def normed(x, gain):
            ms = _sel_r2(x * x, down) * (1.0 / FOX_DH)
            r = _sel_r2(lax.rsqrt(ms + EPS), up)
            return x * r * gain

        lane = _iota((LANES, LANES), 1)
        lf = jnp.where(lane < FOX_H, _log_sigmoid(p_ref[:, 4 * D:4 * D + LANES] + bf_ref[...]), 0.0)
        c = _sel_l(_tri(LANES).astype(BF16), lf) + carry[0:1, :]
        carry[...] = jnp.broadcast_to(c[LANES - 1:LANES, :], carry.shape)
        q_extra = _placed(c, L_C) + _ones_at(LANES, (L_K, L_K + 1, L_K + 2))
        row = i * LANES + _iota((LANES, FOX_AUG), 0)
        lane_a = _iota((LANES, FOX_AUG), 1) % LANES
        k_extra = -_placed(c, L_K) + _ones_at(LANES, (L_C, L_C + 1, L_C + 2, L_LSE, L_LSE + 1, L_LSE + 2))
        pad_val = jnp.where(lane_a == L_K, PAD_KEY, 0.0)
        k_extra = jnp.where((row < META0) & (lane_a >= L_K) & (lane_a < L_K + 3), pad_val, k_extra)
        v_extra = _ones_at(LANES, (L_C, L_C + 1, L_C + 2))
        _spread(normed(p_ref[:, 0:D], qg_ref[...]) * (FOX_DH ** -0.5), q_extra, q_ref)
        _spread(normed(p_ref[:, D:2 * D], kg_ref[...]), k_extra, k_ref)
        _spread(p_ref[:, 2 * D:3 * D], v_extra, v_ref)

    row = pl.BlockSpec((1, D), lambda i: (0, 0))
    aug = pl.BlockSpec((LANES, FOX_AUG), lambda i: (i, 0))
    return pl.pallas_call(
        body, name="fox_prep", grid=(nb,),
        in_specs=[pl.BlockSpec((LANES, FOX_INP), lambda i: (i, 0)), pl.BlockSpec((1, LANES), lambda i: (0, 0)), row, row],
        out_specs=[aug] * 3, out_shape=[jax.ShapeDtypeStruct((lp, FOX_AUG), BF16)] * 3,
        scratch_shapes=[pltpu.VMEM((8, LANES), F32)],
        compiler_params=_params(("arbitrary",)),
    )(proj, jnp.pad(b_f, (0, LANES - FOX_H)).reshape(1, LANES), jnp.tile(q_gain, FOX_H).reshape(1, D),
      jnp.tile(k_gain, FOX_H).reshape(1, D))


def _fox_attn_fwd(qa, ka, va, proj, ag=None):
    lp = qa.shape[0]
    tq = _tile(lp, FOX_TQ)
    nq = lp // tq
    npair = FOX_H // 2

    def body(q_ref, k_ref, v_ref, gate_ref, *rest):
        if ag is None:
            o_ref, og_ref, lse_ref = rest
        else:
            _, o_ref, og_ref, lse_ref, w_out, send_sems, recv_sems = rest
            copies = _AgCopies(w_out, ag[1], send_sems, recv_sems)

            @pl.when((pl.program_id(0) == 0) & (pl.program_id(1) == 0))
            def _():
                for r, k in copies.pairs():
                    copies.ici(r, k).start()

        i = pl.program_id(1)
        causal = _iota((tq, tq), 1) <= _iota((tq, tq), 0)
        qs = [q_ref[:, hh * LANES:(hh + 1) * LANES] for hh in range(2)]

        def block(j, carry, diag):
            off = pl.multiple_of(j * tq, tq)
            out = []
            for hh in range(2):
                m, acc = carry[hh]
                k = k_ref[pl.ds(off, tq), hh * LANES:(hh + 1) * LANES]
                v = v_ref[pl.ds(off, tq), hh * LANES:(hh + 1) * LANES]
                s = nt(qs[hh], k)
                if diag:
                    s = jnp.where(causal, s, -1e30)
                m2 = jnp.maximum(m, jnp.max(s, axis=-1, keepdims=True))
                p = jnp.exp(s - m2)
                p_hi = p.astype(BF16)
                p_lo = (p - p_hi.astype(F32)).astype(BF16)
                out.append((m2, jnp.exp(m - m2) * acc + nn(p_hi, v) + nn(p_lo, v)))
            return tuple(out)

        init = tuple((jnp.full((tq, 1), -1e30, F32), jnp.zeros((tq, LANES), F32)) for _ in range(2))
        carry = lax.fori_loop(0, i // 2, lambda j, c: block(2 * j + 1, block(2 * j, c, False), False), init)
        carry = lax.cond(i % 2 == 1, lambda c: block(i - 1, c, False), lambda c: c, carry)
        carry = block(i, carry, True)
        outs, lses = [], []
        for hh in range(2):
            m, acc = carry[hh]
            l = acc[:, L_C:L_C + 1]
            outs.append(acc / l)
            lses.append(jnp.broadcast_to(m + jnp.log(l), (tq, LANES)))
        left = _iota((tq, LANES), 1) < FOX_DH
        o = jnp.where(left, outs[0], pltpu.roll(outs[1], FOX_DH, 1))
        o_ref[...] = o
        og_ref[...] = (o * _sigmoid(gate_ref[...])).astype(BF16)
        lse_ref[...] = jnp.where(left, lses[0], lses[1])

        if ag is not None:
            @pl.when((pl.program_id(0) == npair - 1) & (pl.program_id(1) == nq - 1))
            def _():
                for r, k in copies.pairs():
                    copies.ici_arrival(r, k).wait_recv()
                for r, k in copies.pairs():
                    copies.ici(r, k).wait_send()

    qspec = pl.BlockSpec((tq, 2 * LANES), lambda p, i: (i, p))
    kspec = pl.BlockSpec((lp, 2 * LANES), lambda p, i: (0, p))
    ospec = pl.BlockSpec((tq, LANES), lambda p, i: (i, p))
    ins, in_specs = [qa, ka, va, proj], [qspec, kspec, kspec, pl.BlockSpec((tq, LANES), lambda p, i: (i, 3 * D // LANES + p))]
    out_specs = [ospec] * 3
    out_shape = [jax.ShapeDtypeStruct((lp, D), F32), jax.ShapeDtypeStruct((lp, D), BF16), jax.ShapeDtypeStruct((lp, D), F32)]
    if ag is None:
        return pl.pallas_call(body, name="fox_attn_fwd", grid=(npair, nq), in_specs=in_specs, out_specs=out_specs,
                              out_shape=out_shape, compiler_params=_params(("parallel", "arbitrary")))(*ins)
    n = 3 * len(ag[1])
    return pl.pallas_call(
        body, name="fox_attn_fwd_ag", grid=(npair, nq), in_specs=in_specs + [ANY], out_specs=out_specs + [ANY],
        out_shape=out_shape + [jax.ShapeDtypeStruct(ag[0].shape, ag[0].dtype)],
        scratch_shapes=[pltpu.SemaphoreType.DMA((n,))] * 2, input_output_aliases={4: 3},
        compiler_params=_params(("arbitrary", "arbitrary")),
    )(*ins, ag[0])


def _fox_gate_bwd(dog, o, proj, lse, qa, swap=None):
    lp = o.shape[0]
    tr = LANES
    steps = lp // tr

    def body(d_ref, o_ref, g_ref, lse_ref, q_ref, *rest):
        if swap is None:
            do_ref, q2_ref, dgate_ref = rest
        else:
            src_ref, do_ref, q2_ref, dgate_ref, got_ref, send_sem, recv_sem = rest
            cp = _swap_copy(src_ref, got_ref, send_sem, recv_sem)

            @pl.when(pl.program_id(0) == 0)
            def _():
                cp.start()

            @pl.when(pl.program_id(0) == steps - 1)
            def _():
                cp.wait()

        down, _ = _head_sel(FOX_H, FOX_DH)
        sg = _sigmoid(g_ref[...])
        dv, ov = d_ref[...], o_ref[...]
        do = (dv * sg).astype(BF16).astype(F32)
        dgate_ref[...] = dv * ov * sg * (1.0 - sg)
        delta = _sel_r(do * ov, down)
        _spread(do, -_placed(delta, L_C), do_ref)
        r_, c_ = _iota((D, LANES), 0), _iota((D, LANES), 1)
        lse_c = _sel_r(lse_ref[...], (r_ == c_ * FOX_DH).astype(BF16))
        q2_ref[...] = (q_ref[...].astype(F32) - _placed(lse_c, L_LSE)).astype(BF16)

    spec = pl.BlockSpec((tr, D), lambda i: (i, 0))
    aug = pl.BlockSpec((tr, FOX_AUG), lambda i: (i, 0))
    in_specs = [spec, spec, pl.BlockSpec((tr, D), lambda i: (i, 3)), spec, aug]
    out_specs = [aug, aug, spec]
    out_shape = [jax.ShapeDtypeStruct((lp, FOX_AUG), BF16), jax.ShapeDtypeStruct((lp, FOX_AUG), BF16),
                 jax.ShapeDtypeStruct((lp, D), F32)]
    if swap is None:
        return pl.pallas_call(body, name="fox_gate_bwd", grid=(steps,), in_specs=in_specs, out_specs=out_specs,
                              out_shape=out_shape, compiler_params=_params(("parallel",)))(dog, o, proj, lse, qa)
    nb, r, c = swap.shape
    return pl.pallas_call(
        body, name="fox_gate_bwd_swap", grid=(steps,), in_specs=in_specs + [ANY], out_specs=out_specs + [ANY],
        out_shape=out_shape + [jax.ShapeDtypeStruct((nb, r // 2, c), swap.dtype)],
        scratch_shapes=[pltpu.SemaphoreType.DMA, pltpu.SemaphoreType.DMA], compiler_params=_params(("arbitrary",)),
    )(dog, o, proj, lse, qa, swap)


def _fox_attn_bwd(q2, ka, va, doa, rs=None):
    lp = q2.shape[0]
    t = _tile(lp, FOX_TQ)
    nb = lp // t
    npair = FOX_H // 2

    def body(q_ref, k_ref, v_ref, do_ref, *rest):
        if rs is None:
            dq_ref, dk_ref, dv_ref, dc_ref, dq_acc, dk_acc, dv_acc, dc_acc = rest
        else:
            s_ref, dq_ref, dk_ref, dv_ref, dc_ref, got_ref, dq_acc, dk_acc, dv_acc, dc_acc, send_sems, recv_sems = rest
            sends, arrivals = _scatter_copies(s_ref, got_ref, send_sems, recv_sems)

            @pl.when((pl.program_id(0) == 0) & (pl.program_id(1) == 0))
            def _():
                for cp in sends:
                    cp.start()

            @pl.when((pl.program_id(0) == npair - 1) & (pl.program_id(1) == nb - 1))
            def _():
                for cp in arrivals:
                    cp.wait_recv()
                for cp in sends:
                    cp.wait_send()

        j = pl.program_id(1)

        @pl.when(j == 0)
        def _():
            dq_acc[...] = jnp.zeros_like(dq_acc)

        causal = _iota((t, t), 1) <= _iota((t, t), 0)
        ks = [k_ref[:, hh * LANES:(hh + 1) * LANES] for hh in range(2)]
        vs = [v_ref[:, hh * LANES:(hh + 1) * LANES] for hh in range(2)]
        dk_acc[...] = jnp.zeros_like(dk_acc)
        dv_acc[...] = jnp.zeros_like(dv_acc)
        dc_acc[...] = jnp.zeros_like(dc_acc)

        def block(i, diag):
            off = pl.multiple_of(i * t, t)
            for hh in range(2):
                q = q_ref[pl.ds(off, t), hh * LANES:(hh + 1) * LANES]
                do = do_ref[pl.ds(off, t), hh * LANES:(hh + 1) * LANES]
                s = nt(q, ks[hh])
                if diag:
                    s = jnp.where(causal, s, -1e30)
                p = jnp.exp(s)
                ds = p * nt(do, vs[hh])
                dc_acc[hh] += jnp.sum(ds, axis=0, keepdims=True)
                dsb = ds.astype(BF16)
                dv_acc[hh] += tn(p.astype(BF16), do)
                dk_acc[hh] += tn(dsb, q)
                dq_acc[hh, pl.ds(off, t), :] += nn(dsb, ks[hh])

        block(j, True)
        below = nb - 1 - j

        def step(u, c):
            block(j + 1 + 2 * u, False)
            block(j + 2 + 2 * u, False)
            return c

        lax.fori_loop(0, below // 2, step, 0)

        @pl.when(below % 2 == 1)
        def _():
            block(nb - 1, False)
        left = _iota((t, LANES), 1) < FOX_DH
        dk_ref[...] = jnp.where(left, dk_acc[0], pltpu.roll(dk_acc[1], FOX_DH, 1))
        dv_ref[...] = jnp.where(left, dv_acc[0], pltpu.roll(dv_acc[1], FOX_DH, 1))
        for hh in range(2):
            dc_ref[hh] = jnp.broadcast_to(-dc_acc[hh], (8, t))

        @pl.when(j == nb - 1)
        def _():
            left = _iota((lp, LANES), 1) < FOX_DH
            dq_ref[...] = jnp.where(left, dq_acc[0], pltpu.roll(dq_acc[1], FOX_DH, 1))

    full = pl.BlockSpec((lp, 2 * LANES), lambda p, j: (0, p))
    kblk = pl.BlockSpec((t, 2 * LANES), lambda p, j: (j, p))
    oblk = pl.BlockSpec((t, LANES), lambda p, j: (j, p))
    in_specs = [full, kblk, kblk, full]
    out_specs = [pl.BlockSpec((lp, LANES), lambda p, j: (0, p)), oblk, oblk, pl.BlockSpec((2, 8, t), lambda p, j: (p, 0, j))]
    out_shape = [jax.ShapeDtypeStruct((lp, D), F32)] * 3 + [jax.ShapeDtypeStruct((FOX_H, 8, lp), F32)]
    scratch = [pltpu.VMEM((2, lp, LANES), F32), pltpu.VMEM((2, t, LANES), F32), pltpu.VMEM((2, t, LANES), F32),
               pltpu.VMEM((2, 1, t), F32)]
    if rs is None:
        return pl.pallas_call(body, name="fox_attn_bwd", grid=(npair, nb), in_specs=in_specs, out_specs=out_specs,
                              out_shape=out_shape, scratch_shapes=scratch,
                              compiler_params=_params(("parallel", "arbitrary")))(q2, ka, va, doa)
    return pl.pallas_call(
        body, name="fox_attn_bwd_rs", grid=(npair, nb), in_specs=in_specs + [ANY], out_specs=out_specs + [ANY],
        out_shape=out_shape + [jax.ShapeDtypeStruct((3,) + rs.shape[1:], rs.dtype)],
        scratch_shapes=scratch + [pltpu.SemaphoreType.DMA((3,)), pltpu.SemaphoreType.DMA((3,))],
        compiler_params=_params(("arbitrary", "arbitrary")),
    )(q2, ka, va, doa, rs)


def _fox_prep_bwd(proj, b_f, q_gain, k_gain, dqn, dkn, dv, dgate, dct):
    lp = proj.shape[0]
    nb = lp // LANES

    def body(p_ref, bf_ref, qg_ref, kg_ref, dq_ref, dk_ref, dv_ref, dg_ref, dc_ref,
             dp_ref, dqg_ref, dkg_ref, dbf_ref, carry):
        i = pl.program_id(0)

        @pl.when(i == 0)
        def _():
            carry[...] = jnp.zeros_like(carry)
            dqg_ref[...] = jnp.zeros_like(dqg_ref)
            dkg_ref[...] = jnp.zeros_like(dkg_ref)
            dbf_ref[...] = jnp.zeros_like(dbf_ref)

        down, up = _head_sel(FOX_H, FOX_DH)

        def norm_bwd(x, gain, dy, scale, dgain_ref):
            ms = _sel_r2(x * x, down) * (1.0 / FOX_DH)
            r = _sel_r2(lax.rsqrt(ms + EPS), up)
            u = dy * gain * scale
            mean_xu = _sel_r2(_sel_r2(x * u, down) * (1.0 / FOX_DH), up)
            dgain_ref[...] += jnp.sum(dy * scale * x * r, axis=0, keepdims=True)
            return r * u - x * (r * r * r) * mean_xu

        dp_ref[:, 0:D] = norm_bwd(p_ref[:, 0:D], qg_ref[...], dq_ref[...], FOX_DH ** -0.5, dqg_ref).astype(BF16)
        dp_ref[:, D:2 * D] = norm_bwd(p_ref[:, D:2 * D], kg_ref[...], dk_ref[...], 1.0, dkg_ref).astype(BF16)
        dp_ref[:, 2 * D:3 * D] = dv_ref[...].astype(BF16)
        dp_ref[:, 3 * D:4 * D] = dg_ref[...].astype(BF16)
        rows = jnp.concatenate([dc_ref[h, 0:1, :] for h in range(FOX_H)] + [jnp.zeros((LANES - FOX_H, LANES), F32)], axis=0)
        dlf = _sel_l(_tri(LANES, upper=True).astype(BF16), rows.T) + carry[0:1, :]
        carry[...] = jnp.broadcast_to(dlf[0:1, :], carry.shape)
        lane = _iota((LANES, LANES), 1)
        z = p_ref[:, 4 * D:4 * D + LANES] + bf_ref[...]
        df = jnp.where(lane < FOX_H, dlf * _sigmoid(-z), 0.0)
        dp_ref[:, 4 * D:4 * D + LANES] = df.astype(BF16)
        dbf_ref[...] += jnp.sum(df, axis=0, keepdims=True)

    rev = lambda i: (nb - 1 - i, 0)
    blk = pl.BlockSpec((LANES, D), rev)
    row = pl.BlockSpec((1, D), lambda i: (0, 0))
    row128 = pl.BlockSpec((1, LANES), lambda i: (0, 0))
    return pl.pallas_call(
        body, name="fox_prep_bwd", grid=(nb,),
        in_specs=[pl.BlockSpec((LANES, FOX_INP), rev), row128, row, row, blk, blk, blk, blk,
                  pl.BlockSpec((FOX_H, 8, LANES), lambda i: (0, 0, nb - 1 - i))],
        out_specs=[pl.BlockSpec((LANES, FOX_INP), rev), row, row, row128],
        out_shape=[jax.ShapeDtypeStruct((lp, FOX_INP), BF16), jax.ShapeDtypeStruct((1, D), F32),
                   jax.ShapeDtypeStruct((1, D), F32), jax.ShapeDtypeStruct((1, LANES), F32)],
        scratch_shapes=[pltpu.VMEM((8, LANES), F32)],
        compiler_params=_params(("arbitrary",)),
    )(proj, jnp.pad(b_f, (0, LANES - FOX_H)).reshape(1, LANES), jnp.tile(q_gain, FOX_H).reshape(1, D),
      jnp.tile(k_gain, FOX_H).reshape(1, D), dqn, dkn, dv, dgate, dct)


def _gla_gates(p_ref, wa_ref, ba_ref):
    a_lr = p_ref[:, 3072:3072 + LANES]
    z = nn(a_lr.astype(BF16), wa_ref[...].astype(BF16)) + ba_ref[...]
    g = _log_sigmoid(z) * (1.0 / GLA_NORM)
    b = _sel_l(_tri(CHUNK).astype(BF16), g)
    return a_lr, z, b


def _gla_chunk_fwd(q, k, v, b, st0):
    hs = range(len(q))
    low = _tri(CHUNK)
    bl = [b[h][CHUNK - 1:CHUNK, :] for h in hs]
    qe = [q[h] * jnp.exp(b[h]) for h in hs]
    ke = [k[h] * jnp.exp(-b[h]) for h in hs]
    kd = [k[h] * jnp.exp(bl[h] - b[h]) for h in hs]
    a = [jnp.where(low, nt(qe[h], ke[h]), 0.0) for h in hs]
    o = [nn(a[h], v[h]) + nt(qe[h], st0[h]) for h in hs]
    st1 = [st0[h] * jnp.exp(bl[h]) + tn(v[h], kd[h]) for h in hs]
    return o, st1, (qe, ke, kd, a, bl)


def _gla_slices(p_ref, b_all, h):
    q = p_ref[:, h * GLA_DK:(h + 1) * GLA_DK] * (GLA_DK ** -0.5)
    k = p_ref[:, GLA_QK + h * GLA_DK:GLA_QK + (h + 1) * GLA_DK]
    v = p_ref[:, 2 * GLA_QK + h * GLA_DV:2 * GLA_QK + (h + 1) * GLA_DV]
    r = p_ref[:, 2 * GLA_QK + GLA_V + h * GLA_DV:2 * GLA_QK + GLA_V + (h + 1) * GLA_DV]
    return q, k, v, r, b_all[:, h * GLA_DK:(h + 1) * GLA_DK]


def _gla_fwd(proj, w_alpha2, b_alpha, o_gain):
    lp = proj.shape[0]
    nc = lp // CHUNK

    def body(p_ref, wa_ref, ba_ref, og_ref, o_ref, y_ref, s_ref, st):
        @pl.when(pl.program_id(0) == 0)
        def _():
            st[...] = jnp.zeros_like(st)

        _, _, b_all = _gla_gates(p_ref, wa_ref, ba_ref)
        hs = range(GLA_H)
        parts = [_gla_slices(p_ref, b_all, h) for h in hs]
        st0 = [st[h] for h in hs]
        for h in hs:
            s_ref[0, h] = st0[h]
        o, st1, _ = _gla_chunk_fwd([p[0] for p in parts], [p[1] for p in parts], [p[2] for p in parts],
                                   [p[4] for p in parts], st0)
        for h in hs:
            st[h] = st1[h]
            o_ref[:, h * GLA_DV:(h + 1) * GLA_DV] = o[h]
            rs = lax.rsqrt(jnp.mean(o[h] * o[h], axis=-1, keepdims=True) + EPS)
            y_ref[:, h * GLA_DV:(h + 1) * GLA_DV] = (o[h] * rs * og_ref[...] * _silu(parts[h][3])).astype(BF16)

    blk = pl.BlockSpec((CHUNK, D), lambda i: (i, 0))
    return pl.pallas_call(
        body, name="gla_fwd", grid=(nc,),
        in_specs=[pl.BlockSpec((CHUNK, GLA_INP), lambda i: (i, 0)), pl.BlockSpec((LANES, GLA_QK), lambda i: (0, 0)),
                  pl.BlockSpec((1, GLA_QK), lambda i: (0, 0)), pl.BlockSpec((1, GLA_DV), lambda i: (0, 0))],
        out_specs=[blk, blk, pl.BlockSpec((1, GLA_H, GLA_DV, GLA_DK), lambda i: (i, 0, 0, 0))],
        out_shape=[jax.ShapeDtypeStruct((lp, D), F32), jax.ShapeDtypeStruct((lp, D), BF16),
                   jax.ShapeDtypeStruct((nc, GLA_H, GLA_DV, GLA_DK), F32)],
        scratch_shapes=[pltpu.VMEM((GLA_H, GLA_DV, GLA_DK), F32)],
        compiler_params=_params(("arbitrary",)),
    )(proj, jnp.pad(w_alpha2, ((0, LANES - GLA_RANK), (0, 0))), b_alpha.reshape(1, GLA_QK), o_gain.reshape(1, GLA_DV))


def _gla_bwd(proj, w_alpha2, b_alpha, o_gain, o, states, dy):
    lp = proj.shape[0]
    nc = lp // CHUNK

    def body(p_ref, wa_ref, ba_ref, og_ref, o_ref, s_ref, dy_ref, dp_ref, dwa_ref, dba_ref, dog_ref, dst):
        @pl.when(pl.program_id(0) == 0)
        def _():
            dst[...] = jnp.zeros_like(dst)
            dwa_ref[...] = jnp.zeros_like(dwa_ref)
            dba_ref[...] = jnp.zeros_like(dba_ref)
            dog_ref[...] = jnp.zeros_like(dog_ref)

        a_lr, z, b_all = _gla_gates(p_ref, wa_ref, ba_ref)
        last_row = _iota((CHUNK, GLA_DK), 0) == CHUNK - 1
        rev = _tri(CHUNK, upper=True).astype(BF16)
        hs = range(GLA_H)
        scale = GLA_DK ** -0.5
        parts = [_gla_slices(p_ref, b_all, h) for h in hs]
        q, k, v, b = [p[0] for p in parts], [p[1] for p in parts], [p[2] for p in parts], [p[4] for p in parts]
        st0 = [s_ref[0, h] for h in hs]
        dst1 = [dst[h] for h in hs]
        do = []
        for h in hs:
            r = parts[h][3]
            ov = o_ref[:, h * GLA_DV:(h + 1) * GLA_DV]
            dyv = dy_ref[:, h * GLA_DV:(h + 1) * GLA_DV]
            rs = lax.rsqrt(jnp.mean(ov * ov, axis=-1, keepdims=True) + EPS)
            on = ov * rs
            dp_ref[:, 2 * GLA_QK + GLA_V + h * GLA_DV:2 * GLA_QK + GLA_V + (h + 1) * GLA_DV] = (
                dyv * on * og_ref[...] * _dsilu(r)).astype(BF16)
            don = dyv * _silu(r)
            dog_ref[...] += jnp.sum(don * on, axis=0, keepdims=True)
            u = don * og_ref[...]
            do.append(rs * u - ov * (rs * rs * rs) * jnp.mean(ov * u, axis=-1, keepdims=True))
        _, _, (qe, ke, kd, a, bl) = _gla_chunk_fwd(q, k, v, b, st0)
        low = _tri(CHUNK)
        da = [jnp.where(low, nt(do[h], v[h]), 0.0) for h in hs]
        dkd = [nn(v[h], dst1[h]) for h in hs]
        dvv = [tn(a[h], do[h]) + nt(kd[h], dst1[h]) for h in hs]
        dqe = [nn(da[h], ke[h]) + nn(do[h], st0[h]) for h in hs]
        dke = [tn(da[h], qe[h]) for h in hs]
        dg_parts = []
        for h in hs:
            ebl = jnp.exp(bl[h])
            dst[h] = dst1[h] * ebl + tn(do[h], qe[h])
            db = dqe[h] * qe[h] - dke[h] * ke[h] - dkd[h] * kd[h]
            db_last = (jnp.sum(dkd[h] * kd[h], axis=0, keepdims=True)
                       + jnp.sum(dst1[h] * st0[h], axis=0, keepdims=True) * ebl)
            db = db + jnp.where(last_row, db_last, 0.0)
            dg_parts.append(_sel_l(rev, db))
            dp_ref[:, h * GLA_DK:(h + 1) * GLA_DK] = (dqe[h] * jnp.exp(b[h]) * scale).astype(BF16)
            dp_ref[:, GLA_QK + h * GLA_DK:GLA_QK + (h + 1) * GLA_DK] = (
                dke[h] * jnp.exp(-b[h]) + dkd[h] * jnp.exp(bl[h] - b[h])).astype(BF16)
            dp_ref[:, 2 * GLA_QK + h * GLA_DV:2 * GLA_QK + (h + 1) * GLA_DV] = dvv[h].astype(BF16)
        dg = jnp.concatenate(dg_parts, axis=1)
        dz = dg * (1.0 / GLA_NORM) * _sigmoid(-z)
        dzb = dz.astype(BF16)
        dp_ref[:, 3072:3072 + LANES] = nt(dzb, wa_ref[...].astype(BF16)).astype(BF16)
        dwa_ref[...] += tn(a_lr.astype(BF16), dzb)
        dba_ref[...] += jnp.sum(dz, axis=0, keepdims=True)

    rv = lambda i: (nc - 1 - i, 0)
    blk = pl.BlockSpec((CHUNK, D), rv)
    fixed = lambda r, c: pl.BlockSpec((r, c), lambda i: (0, 0))
    return pl.pallas_call(
        body, name="gla_bwd", grid=(nc,),
        in_specs=[pl.BlockSpec((CHUNK, GLA_INP), rv), fixed(LANES, GLA_QK), fixed(1, GLA_QK), fixed(1, GLA_DV), blk,
                  pl.BlockSpec((1, GLA_H, GLA_DV, GLA_DK), lambda i: (nc - 1 - i, 0, 0, 0)), blk],
        out_specs=[pl.BlockSpec((CHUNK, GLA_INP), rv), fixed(LANES, GLA_QK), fixed(1, GLA_QK), fixed(1, GLA_DV)],
        out_shape=[jax.ShapeDtypeStruct((lp, GLA_INP), BF16), jax.ShapeDtypeStruct((LANES, GLA_QK), F32),
                   jax.ShapeDtypeStruct((1, GLA_QK), F32), jax.ShapeDtypeStruct((1, GLA_DV), F32)],
        scratch_shapes=[pltpu.VMEM((GLA_H, GLA_DV, GLA_DK), F32)],
        compiler_params=_params(("arbitrary",)),
    )(proj, jnp.pad(w_alpha2, ((0, LANES - GLA_RANK), (0, 0))), b_alpha.reshape(1, GLA_QK), o_gain.reshape(1, GLA_DV),
      o, states, dy)


HI = lax.Precision.HIGHEST


def _gdn_pre(prev_ref, p_ref, cw_ref, al_ref, dt_ref):
    xc = jnp.concatenate([prev_ref[:, 0:GDN_CONV], p_ref[:, 0:GDN_CONV]], axis=0)
    shifted = [pltpu.roll(xc, 3 - j, 0)[CHUNK:, :] if j < 3 else xc[CHUNK:, :] for j in range(4)]
    conv = sum(shifted[j] * cw_ref[j:j + 1, :] for j in range(4))
    act = _silu(conv)
    slab = p_ref[:, 4096:4096 + LANES]
    lane = _iota((CHUNK, LANES), 1)
    zs = slab + dt_ref[...]
    g = jnp.where(lane < GDN_H, -jnp.exp(al_ref[...]) * _softplus(zs), 0.0)
    bs = _sel_l(_tri(CHUNK).astype(BF16), g)
    beta = _sigmoid(slab)
    return shifted, conv, act, slab, zs, g, bs, beta


def _l2n(x):
    r = lax.rsqrt(jnp.sum(x * x, axis=-1, keepdims=True) + EPS)
    return x * r, r


def _gdn_chunk_fwd(q, k, v, beta, bcol, brow, s0):
    hs = range(len(q))
    ii, jj = _iota((CHUNK, CHUNK), 0), _iota((CHUNK, CHUNK), 1)
    low, eye = ii >= jj, (ii == jj).astype(F32)
    dm = [jnp.where(low, jnp.exp(jnp.where(low, bcol[h] - brow[h], 0.0)), 0.0) for h in hs]
    dstrict = [jnp.where(ii > jj, dm[h], 0.0) for h in hs]
    eb = [jnp.exp(bcol[h]) for h in hs]
    bl = [bcol[h][CHUNK - 1:CHUNK, :] for h in hs]
    kb = [k[h] * beta[h] for h in hs]
    vb = [v[h] * beta[h] for h in hs]
    nmat = [nt(kb[h], k[h]) * dstrict[h] for h in hs]
    x = [eye - nmat[h] for h in hs]
    pw = [nn(nmat[h], nmat[h], precision=HI) for h in hs]
    for it in range(5):
        x = [x[h] + nn(x[h], pw[h], precision=HI) for h in hs]
        if it < 4:
            pw = [nn(pw[h], pw[h], precision=HI) for h in hs]
    kbe = [kb[h] * eb[h] for h in hs]
    u = [nn(x[h], vb[h], precision=HI) for h in hs]
    w = [nn(x[h], kbe[h], precision=HI) for h in hs]
    vn = [u[h] - nn(w[h], s0[h]) for h in hs]
    pm = [nt(q[h], k[h]) * dm[h] for h in hs]
    qe = [q[h] * eb[h] for h in hs]
    o = [nn(pm[h], vn[h]) + nn(qe[h], s0[h]) for h in hs]
    kd = [k[h] * jnp.exp(bl[h] - bcol[h]) for h in hs]
    s1 = [s0[h] * jnp.exp(bl[h]) + tn(kd[h], vn[h]) for h in hs]
    return o, s1, dict(dm=dm, dstrict=dstrict, eb=eb, bl=bl, kb=kb, vb=vb, nmat=nmat, tinv=x, kbe=kbe, u=u, w=w, vn=vn,
                       pm=pm, qe=qe, kd=kd)


def _gdn_heads(act, beta_slab, bs, h):
    qa = act[:, h * GDN_DK:(h + 1) * GDN_DK]
    ka = act[:, GDN_H * GDN_DK + h * GDN_DK:GDN_H * GDN_DK + (h + 1) * GDN_DK]
    v = act[:, 2 * GDN_H * GDN_DK + h * GDN_DV:2 * GDN_H * GDN_DK + (h + 1) * GDN_DV]
    return qa, ka, v, beta_slab[:, GDN_H + h:GDN_H + h + 1], bs[:, h:h + 1]


def _gdn_fwd(proj, conv_w, a_log, dt_bias, o_gain):
    lp = proj.shape[0]
    nc = lp // CHUNK

    def body(prev_ref, p_ref, cw_ref, al_ref, dt_ref, og_ref, o_ref, y_ref, s_ref, st):
        @pl.when(pl.program_id(0) == 0)
        def _():
            st[...] = jnp.zeros_like(st)

        _, _, act, _, _, _, bs, beta = _gdn_pre(prev_ref, p_ref, cw_ref, al_ref, dt_ref)
        bst = bs.T
        hs = range(GDN_H)
        parts = [_gdn_heads(act, beta, bs, h) for h in hs]
        q = [_l2n(parts[h][0])[0] * (GDN_DK ** -0.5) for h in hs]
        k = [_l2n(parts[h][1])[0] for h in hs]
        s0 = [st[h] for h in hs]
        for h in hs:
            s_ref[0, h] = s0[h]
        o, s1, _ = _gdn_chunk_fwd(q, k, [parts[h][2] for h in hs], [parts[h][3] for h in hs], [parts[h][4] for h in hs],
                                  [bst[h:h + 1, :] for h in hs], s0)
        for h in hs:
            st[h] = s1[h]
            o_ref[:, h * GDN_DV:(h + 1) * GDN_DV] = o[h]
            rs = lax.rsqrt(jnp.mean(o[h] * o[h], axis=-1, keepdims=True) + EPS)
            gate = p_ref[:, GDN_CONV + h * GDN_DV:GDN_CONV + (h + 1) * GDN_DV]
            y_ref[:, h * GDN_DV:(h + 1) * GDN_DV] = (o[h] * rs * og_ref[...] * _silu(gate)).astype(BF16)

    blk = pl.BlockSpec((CHUNK, D), lambda i: (i, 0))
    fixed = lambda r, c: pl.BlockSpec((r, c), lambda i: (0, 0))
    return pl.pallas_call(
        body, name="gdn_fwd", grid=(nc,),
        in_specs=[pl.BlockSpec((CHUNK, GDN_INP), lambda i: (jnp.maximum(i - 1, 0), 0)),
                  pl.BlockSpec((CHUNK, GDN_INP), lambda i: (i, 0)), fixed(8, GDN_CONV), fixed(1, LANES), fixed(1, LANES),
                  fixed(1, GDN_DV)],
        out_specs=[blk, blk, pl.BlockSpec((1, GDN_H, GDN_DK, GDN_DV), lambda i: (i, 0, 0, 0))],
        out_shape=[jax.ShapeDtypeStruct((lp, D), F32), jax.ShapeDtypeStruct((lp, D), BF16),
                   jax.ShapeDtypeStruct((nc, GDN_H, GDN_DK, GDN_DV), F32)],
        scratch_shapes=[pltpu.VMEM((GDN_H, GDN_DK, GDN_DV), F32)],
        compiler_params=_params(("arbitrary",)),
    )(proj, proj, jnp.pad(conv_w.reshape(4, GDN_CONV), ((0, 4), (0, 0))), jnp.pad(a_log, (0, LANES - GDN_H)).reshape(1, LANES),
      jnp.pad(dt_bias, (0, LANES - GDN_H)).reshape(1, LANES), o_gain.reshape(1, GDN_DV))


def _gdn_bwd(proj, conv_w, a_log, dt_bias, o_gain, o, states, dy):
    lp = proj.shape[0]
    nc = lp // CHUNK

    def body(prev_ref, p_ref, cw_ref, al_ref, dt_ref, og_ref, o_ref, s_ref, dy_ref,
             dp_ref, dcw_ref, dal_ref, ddt_ref, dog_ref, dst, dconv_next):
        @pl.when(pl.program_id(0) == 0)
        def _():
            dst[...] = jnp.zeros_like(dst)
            dconv_next[...] = jnp.zeros_like(dconv_next)
            dcw_ref[...] = jnp.zeros_like(dcw_ref)
            dal_ref[...] = jnp.zeros_like(dal_ref)
            ddt_ref[...] = jnp.zeros_like(ddt_ref)
            dog_ref[...] = jnp.zeros_like(dog_ref)

        shifted, conv, act, slab, zs, g, bs, beta = _gdn_pre(prev_ref, p_ref, cw_ref, al_ref, dt_ref)
        bst = bs.T
        lane = _iota((CHUNK, LANES), 1)
        ones = jnp.ones((CHUNK, LANES), F32)
        db_slab = jnp.zeros((CHUNK, LANES), F32)
        dbeta_slab = jnp.zeros((CHUNK, LANES), F32)
        last_row = _iota((CHUNK, 1), 0) == CHUNK - 1
        hs = range(GDN_H)
        scale = GDN_DK ** -0.5
        parts = [_gdn_heads(act, beta, bs, h) for h in hs]
        qa, ka, v = [parts[h][0] for h in hs], [parts[h][1] for h in hs], [parts[h][2] for h in hs]
        bet, bcol = [parts[h][3] for h in hs], [parts[h][4] for h in hs]
        qn_ = [_l2n(qa[h]) for h in hs]
        kn_ = [_l2n(ka[h]) for h in hs]
        q = [qn_[h][0] * scale for h in hs]
        k, rq, rk = [kn_[h][0] for h in hs], [qn_[h][1] for h in hs], [kn_[h][1] for h in hs]
        s0 = [s_ref[0, h] for h in hs]
        ds1 = [dst[h] for h in hs]
        do = []
        for h in hs:
            ov = o_ref[:, h * GDN_DV:(h + 1) * GDN_DV]
            dyv = dy_ref[:, h * GDN_DV:(h + 1) * GDN_DV]
            gate = p_ref[:, GDN_CONV + h * GDN_DV:GDN_CONV + (h + 1) * GDN_DV]
            rs = lax.rsqrt(jnp.mean(ov * ov, axis=-1, keepdims=True) + EPS)
            on = ov * rs
            dp_ref[:, GDN_CONV + h * GDN_DV:GDN_CONV + (h + 1) * GDN_DV] = (dyv * on * og_ref[...] * _dsilu(gate)).astype(BF16)
            don = dyv * _silu(gate)
            dog_ref[...] += jnp.sum(don * on, axis=0, keepdims=True)
            uu = don * og_ref[...]
            do.append(rs * uu - ov * (rs * rs * rs) * jnp.mean(ov * uu, axis=-1, keepdims=True))
        _, _, f = _gdn_chunk_fwd(q, k, v, bet, bcol, [bst[h:h + 1, :] for h in hs], s0)
        dm, dstrict, eb, bl, kb, nmat, tinv = f["dm"], f["dstrict"], f["eb"], f["bl"], f["kb"], f["nmat"], f["tinv"]
        kbe, u, w, vn, pm, qe, kd = f["kbe"], f["u"], f["w"], f["vn"], f["pm"], f["qe"], f["kd"]
        ebl = [jnp.exp(bl[h]) for h in hs]
        dvn = [tn(pm[h], do[h]) + nn(kd[h], ds1[h]) for h in hs]
        dpr = [nt(do[h], vn[h]) for h in hs]
        dqe = [nt(do[h], s0[h]) for h in hs]
        dkd = [nt(vn[h], ds1[h]) for h in hs]
        for h in hs:
            dst[h] = ds1[h] * ebl[h] + tn(qe[h], do[h]) - tn(w[h], dvn[h])
        du_ = [tn(tinv[h], dvn[h], precision=HI) for h in hs]
        dw_ = [tn(tinv[h], -nt(dvn[h], s0[h]), precision=HI) for h in hs]
        dn = [-(nt(du_[h], u[h]) + nt(dw_[h], w[h])) for h in hs]
        dqk = [dpr[h] * dm[h] for h in hs]
        dkk = [dn[h] * dstrict[h] for h in hs]
        gsum = [dpr[h] * pm[h] + dn[h] * nmat[h] for h in hs]
        dkb = [nn(dkk[h], k[h]) + dw_[h] * eb[h] for h in hs]
        dk = [tn(dkk[h], kb[h]) + tn(dqk[h], q[h]) + dkd[h] * jnp.exp(bl[h] - bcol[h]) + dkb[h] * bet[h] for h in hs]
        dq = [nn(dqk[h], k[h]) + dqe[h] * eb[h] for h in hs]
        colsum = [tn(gsum[h], ones, precision=HI)[:, 0:1] for h in hs]
        dact_q, dact_k, dact_v = [], [], []
        for h in hs:
            dbeta = jnp.sum(dkb[h] * k[h], axis=-1, keepdims=True) + jnp.sum(du_[h] * v[h], axis=-1, keepdims=True)
            skd = jnp.sum(dkd[h] * kd[h], axis=-1, keepdims=True)
            db = (jnp.sum(gsum[h], axis=-1, keepdims=True) - colsum[h] + jnp.sum(dqe[h] * qe[h], axis=-1, keepdims=True)
                  + jnp.sum(dw_[h] * kbe[h], axis=-1, keepdims=True) - skd)
            db_last = jnp.sum(skd, axis=0, keepdims=True) + jnp.sum(ds1[h] * s0[h]) * ebl[h]
            db = db + jnp.where(last_row, db_last, 0.0)
            db_slab = db_slab + jnp.where(lane == h, db, 0.0)
            dbeta_slab = dbeta_slab + jnp.where(lane == GDN_H + h, dbeta, 0.0)
            dqn = dq[h] * scale
            dact_q.append(rq[h] * dqn - qa[h] * (rq[h] * rq[h] * rq[h]) * jnp.sum(qa[h] * dqn, axis=-1, keepdims=True))
            dact_k.append(rk[h] * dk[h] - ka[h] * (rk[h] * rk[h] * rk[h]) * jnp.sum(ka[h] * dk[h], axis=-1, keepdims=True))
            dact_v.append(du_[h] * bet[h])
        dact = jnp.concatenate(dact_q + dact_k + dact_v, axis=1)
        dconv = dact * _dsilu(conv)
        for j in range(4):
            dcw_ref[j:j + 1, :] += jnp.sum(dconv * shifted[j], axis=0, keepdims=True)
        dcat = jnp.concatenate([dconv, dconv_next[...]], axis=0)
        dx = dconv * cw_ref[3:4, :]
        for j in range(3):
            dx = dx + pltpu.roll(dcat, 2 * CHUNK - (3 - j), 0)[:CHUNK, :] * cw_ref[j:j + 1, :]
        dconv_next[...] = dconv
        dp_ref[:, 0:GDN_CONV] = dx.astype(BF16)
        dg = _sel_l(_tri(CHUNK, upper=True).astype(BF16), db_slab)
        da = dg * (-jnp.exp(al_ref[...])) * _sigmoid(zs)
        da = jnp.where(lane < GDN_H, da, 0.0)
        dal_ref[...] += jnp.sum(dg * g, axis=0, keepdims=True)
        ddt_ref[...] += jnp.sum(da, axis=0, keepdims=True)
        dp_ref[:, 4096:4096 + LANES] = (da + dbeta_slab * beta * (1.0 - beta)).astype(BF16)

    rv = lambda i: (nc - 1 - i, 0)
    blk = pl.BlockSpec((CHUNK, D), rv)
    fixed = lambda r, c: pl.BlockSpec((r, c), lambda i: (0, 0))
    return pl.pallas_call(
        body, name="gdn_bwd", grid=(nc,),
        in_specs=[pl.BlockSpec((CHUNK, GDN_INP), lambda i: (jnp.maximum(nc - 2 - i, 0), 0)),
                  pl.BlockSpec((CHUNK, GDN_INP), rv), fixed(8, GDN_CONV), fixed(1, LANES), fixed(1, LANES), fixed(1, GDN_DV),
                  blk, pl.BlockSpec((1, GDN_H, GDN_DK, GDN_DV), lambda i: (nc - 1 - i, 0, 0, 0)), blk],
        out_specs=[pl.BlockSpec((CHUNK, GDN_INP), rv), fixed(8, GDN_CONV), fixed(1, LANES), fixed(1, LANES), fixed(1, GDN_DV)],
        out_shape=[jax.ShapeDtypeStruct((lp, GDN_INP), BF16), jax.ShapeDtypeStruct((8, GDN_CONV), F32),
                   jax.ShapeDtypeStruct((1, LANES), F32), jax.ShapeDtypeStruct((1, LANES), F32),
                   jax.ShapeDtypeStruct((1, GDN_DV), F32)],
        scratch_shapes=[pltpu.VMEM((GDN_H, GDN_DK, GDN_DV), F32), pltpu.VMEM((CHUNK, GDN_CONV), F32)],
        compiler_params=_params(("arbitrary",)),
    )(proj, proj, jnp.pad(conv_w.reshape(4, GDN_CONV), ((0, 4), (0, 0))), jnp.pad(a_log, (0, LANES - GDN_H)).reshape(1, LANES),
      jnp.pad(dt_bias, (0, LANES - GDN_H)).reshape(1, LANES), o_gain.reshape(1, GDN_DV), o, states, dy)


def _coords():
    return lax.axis_index("x"), lax.axis_index("y"), lax.axis_index("c")


def _other_chips(x, y):
    return [(1 - x, y, 2 * (1 - x) + y), (x, 1 - y, 2 * x + 1 - y), (1 - x, 1 - y, 2 * (1 - x) + 1 - y)]


def _gather8(v, *, reduce, name):
    r, c = v.shape

    def body(v_ref, out_ref, *scratch):
        if reduce:
            buf, send_sems, recv_sems = scratch
        else:
            buf = out_ref
            send_sems, recv_sems = scratch
        x, y, cc = _coords()
        me = 4 * x + 2 * y + cc
        buf[me] = v_ref[...]
        copies = []
        for k in range(1, 8):
            px, py, pc = x ^ (k >> 2), y ^ ((k >> 1) & 1), cc ^ (k & 1)
            copies.append(pltpu.make_async_remote_copy(
                src_ref=v_ref, dst_ref=buf.at[me], send_sem=send_sems.at[k - 1], recv_sem=recv_sems.at[k - 1],
                device_id=(px, py, pc), device_id_type=MESH))
        for cp in copies:
            cp.start()
        for k in range(1, 8):
            peer = (x ^ (k >> 2)) * 4 + (y ^ ((k >> 1) & 1)) * 2 + (cc ^ (k & 1))
            pltpu.make_async_remote_copy(
                src_ref=v_ref, dst_ref=buf.at[peer], send_sem=send_sems.at[k - 1], recv_sem=recv_sems.at[k - 1],
                device_id=(x, y, cc), device_id_type=MESH).wait_recv()
        for cp in copies:
            cp.wait_send()
        if reduce:
            acc = buf[0]
            for d in range(1, 8):
                acc = acc + buf[d]
            out_ref[...] = acc

    scratch = [pltpu.SemaphoreType.DMA((7,)), pltpu.SemaphoreType.DMA((7,))]
    if reduce:
        scratch = [pltpu.VMEM((8, r, c), F32)] + scratch
    return pl.pallas_call(
        body, name=name, in_specs=[VM], out_specs=VM,
        out_shape=jax.ShapeDtypeStruct((r, c) if reduce else (8, r, c), F32),
        scratch_shapes=scratch, compiler_params=_params(),
    )(v)


class _AgCopies:
    def __init__(self, buf, ranges, send_sems, recv_sems):
        self.buf, self.ranges, self.send_sems, self.recv_sems = buf, ranges, send_sems, recv_sems
        self.x, self.y, self.cc = _coords()
        self.p = 2 * self.x + self.y
        self.chips = _other_chips(self.x, self.y)

    def rows(self, chip, r, hf):
        start, n = self.ranges[r]
        return self.buf.at[chip, pl.ds(start + hf * (n // 2), n // 2), :]

    def _copy(self, r, k, chip, hf, to):
        return pltpu.make_async_remote_copy(
            src_ref=self.rows(chip, r, hf), dst_ref=self.rows(chip, r, hf), send_sem=self.send_sems.at[3 * r + k],
            recv_sem=self.recv_sems.at[3 * r + k], device_id=to, device_id_type=MESH)

    def pairs(self):
        return [(r, k) for r in range(len(self.ranges)) for k in range(3)]

    def ici(self, r, k):
        cx, cy, _ = self.chips[k]
        return self._copy(r, k, self.p, self.cc, (cx, cy, self.cc))

    def ici_arrival(self, r, k):
        return self._copy(r, k, self.chips[k][2], self.cc, (self.x, self.y, self.cc))

    def forward(self, r, k):
        return self._copy(r, k, self.chips[k][2], self.cc, (self.x, self.y, 1 - self.cc))

    def forward_arrival(self, r, k):
        return self._copy(r, k, self.chips[k][2], 1 - self.cc, (self.x, self.y, self.cc))


def _ag_weights(w4, ranges):
    n = 3 * len(ranges)

    def body(w_ref, out_ref, send1, recv1, send2, recv2):
        ici, fwd = _AgCopies(out_ref, ranges, send1, recv1), _AgCopies(out_ref, ranges, send2, recv2)
        for r, k in ici.pairs():
            ici.ici(r, k).start()
        for r, k in ici.pairs():
            ici.ici_arrival(r, k).wait_recv()
            fwd.forward(r, k).start()
        for r, k in ici.pairs():
            fwd.forward_arrival(r, k).wait_recv()
        for r, k in ici.pairs():
            ici.ici(r, k).wait_send()
            fwd.forward(r, k).wait_send()

    return pl.pallas_call(
        body, name="ag_weights", in_specs=[ANY], out_specs=ANY, out_shape=jax.ShapeDtypeStruct(w4.shape, w4.dtype),
        scratch_shapes=[pltpu.SemaphoreType.DMA((n,))] * 4, input_output_aliases={0: 0}, compiler_params=_params(),
    )(w4)


def _ag_forward(w4, ranges):
    n = 3 * len(ranges)

    def body(w_ref, out_ref, send2, recv2):
        fwd = _AgCopies(out_ref, ranges, send2, recv2)
        for r, k in fwd.pairs():
            fwd.forward(r, k).start()
        for r, k in fwd.pairs():
            fwd.forward_arrival(r, k).wait_recv()
        for r, k in fwd.pairs():
            fwd.forward(r, k).wait_send()

    return pl.pallas_call(
        body, name="ag_forward", in_specs=[ANY], out_specs=ANY, out_shape=jax.ShapeDtypeStruct(w4.shape, w4.dtype),
        scratch_shapes=[pltpu.SemaphoreType.DMA((n,))] * 2, input_output_aliases={0: 0}, compiler_params=_params(),
    )(w4)


def _swap_copy(g_ref, out_ref, send_sem, recv_sem):
    x, y, cc = _coords()
    half = g_ref.shape[1] // 2
    return pltpu.make_async_remote_copy(
        src_ref=g_ref.at[:, pl.ds((1 - cc) * half, half), :], dst_ref=out_ref, send_sem=send_sem, recv_sem=recv_sem,
        device_id=(x, y, 1 - cc), device_id_type=MESH)


def _swap_halves(g, *, name):
    nb, r, c = g.shape
    half = r // 2

    def body(g_ref, out_ref, send_sem, recv_sem):
        cp = _swap_copy(g_ref, out_ref, send_sem, recv_sem)
        cp.start()
        cp.wait()

    return pl.pallas_call(
        body, name=name, in_specs=[ANY], out_specs=ANY, out_shape=jax.ShapeDtypeStruct((nb, half, c), g.dtype),
        scratch_shapes=[pltpu.SemaphoreType.DMA, pltpu.SemaphoreType.DMA], compiler_params=_params(),
    )(g)


def _my_half_index():
    return lax.axis_index("c").astype(jnp.int32).reshape(1)


def _add_halves(g, got, tag):
    nb, r, c = g.shape
    half = r // 2
    tr = _tile(half, 512, 16)
    nt_ = half // tr

    def body(c_ref, a_ref, b_ref, o_ref):
        o_ref[...] = (a_ref[...].astype(F32) + b_ref[...].astype(F32)).astype(BF16)

    return pl.pallas_call(
        body, name=f"rs_add_sibling{tag}",
        grid_spec=pltpu.PrefetchScalarGridSpec(
            num_scalar_prefetch=1, grid=(nb, nt_),
            in_specs=[pl.BlockSpec((1, tr, c), lambda b, i, cr: (b, cr[0] * nt_ + i, 0)),
                      pl.BlockSpec((1, tr, c), lambda b, i, cr: (b, i, 0))],
            out_specs=pl.BlockSpec((1, tr, c), lambda b, i, cr: (b, i, 0))),
        out_shape=jax.ShapeDtypeStruct((nb, half, c), BF16), compiler_params=_params(("parallel", "parallel")),
    )(_my_half_index(), g, got)


def _scatter_copies(s_ref, out_ref, send_sems, recv_sems):
    x, y, cc = _coords()
    sends = [pltpu.make_async_remote_copy(
        src_ref=s_ref.at[blk], dst_ref=out_ref.at[k], send_sem=send_sems.at[k], recv_sem=recv_sems.at[k],
        device_id=(cx, cy, cc), device_id_type=MESH) for k, (cx, cy, blk) in enumerate(_other_chips(x, y))]
    arrivals = [pltpu.make_async_remote_copy(
        src_ref=s_ref.at[2 * x + y], dst_ref=out_ref.at[k], send_sem=send_sems.at[k], recv_sem=recv_sems.at[k],
        device_id=(x, y, cc), device_id_type=MESH) for k in range(3)]
    return sends, arrivals


def _scatter_chips(s, tag):
    nb, hrows, c = s.shape

    def body(s_ref, out_ref, send_sems, recv_sems):
        sends, arrivals = _scatter_copies(s_ref, out_ref, send_sems, recv_sems)
        for cp in sends:
            cp.start()
        for cp in arrivals:
            cp.wait_recv()
        for cp in sends:
            cp.wait_send()

    return pl.pallas_call(
        body, name=f"rs_scatter{tag}", in_specs=[ANY], out_specs=ANY, out_shape=jax.ShapeDtypeStruct((3, hrows, c), s.dtype),
        scratch_shapes=[pltpu.SemaphoreType.DMA((3,)), pltpu.SemaphoreType.DMA((3,))], compiler_params=_params(),
    )(s)


def _sum_chips(s, got, tag):
    nb, hrows, c = s.shape
    tr = _tile(hrows, 512, 16)

    def body(idx_ref, own_ref, got_ref, o_ref):
        p = idx_ref[0]
        own = own_ref[0].astype(F32)
        parts = [got_ref[k].astype(F32) for k in range(3)]
        acc = jnp.zeros_like(own)
        for q in range(4):
            val = own
            for k, rel in enumerate((2, 1, 3)):
                val = jnp.where((p ^ rel) == q, parts[k], val)
            acc = acc + val
        o_ref[...] = acc

    idx = (2 * lax.axis_index("x") + lax.axis_index("y")).astype(jnp.int32).reshape(1)
    return pl.pallas_call(
        body, name=f"rs_sum_chips{tag}",
        grid_spec=pltpu.PrefetchScalarGridSpec(
            num_scalar_prefetch=1, grid=(hrows // tr,),
            in_specs=[pl.BlockSpec((1, tr, c), lambda i, pr: (pr[0], i, 0)), pl.BlockSpec((3, tr, c), lambda i, pr: (0, i, 0))],
            out_specs=pl.BlockSpec((tr, c), lambda i, pr: (i, 0))),
        out_shape=jax.ShapeDtypeStruct((hrows, c), F32), compiler_params=_params(("parallel",)),
    )(idx, s, got)


def _swap_sibling(t, tag):
    def body(t_ref, out_ref, send_sem, recv_sem):
        x, y, cc = _coords()
        cp = pltpu.make_async_remote_copy(src_ref=t_ref, dst_ref=out_ref, send_sem=send_sem, recv_sem=recv_sem,
                                          device_id=(x, y, 1 - cc), device_id_type=MESH)
        cp.start()
        cp.wait()

    return pl.pallas_call(
        body, name=f"rs_join{tag}", in_specs=[ANY], out_specs=ANY, out_shape=jax.ShapeDtypeStruct(t.shape, t.dtype),
        scratch_shapes=[pltpu.SemaphoreType.DMA, pltpu.SemaphoreType.DMA], compiler_params=_params(),
    )(t)


def _rs_local(g, tag):
    return _add_halves(g, _swap_halves(g, name=f"rs_swap{tag}"), tag)


def _rs_finish(s, recv, tag):
    t = _sum_chips(s, recv, tag)
    r = _swap_sibling(t, tag)
    first = lax.axis_index("c") == 0
    return jnp.concatenate([jnp.where(first, t, r), jnp.where(first, r, t)], axis=0)


_SMALL_SHARDED = (("meta_tokens", 1), ("gla_w_alpha2", 2), ("gdn_conv_w", 3))
_REPLICATED = ("norm_mix", "norm_ffn", "fox_b_f", "fox_q_gain", "fox_k_gain", "gla_b_alpha", "gla_o_gain",
               "gdn_a_log", "gdn_dt_bias", "gdn_o_gain")
_WEIGHTS = ("meta_tokens", "norm_mix", "norm_ffn", "w_gate_up", "w_down", "fox_w_in", "fox_b_f", "fox_q_gain",
            "fox_k_gain", "fox_w_out", "gla_w_in", "gla_w_alpha2", "gla_b_alpha", "gla_o_gain", "gla_w_out",
            "gdn_w_in", "gdn_conv_w", "gdn_a_log", "gdn_dt_bias", "gdn_o_gain", "gdn_w_out")
_PACK_ROWS = 512
_IN_W = ("fox_w_in", "gla_w_in", "gdn_w_in")
_OUT_W = ("fox_w_out", "gla_w_out", "gdn_w_out")


def _piece_rows(n):
    return -(-n // 32) * 32


def _pack(arrays, width, row_mult, dtype):
    flat = jnp.concatenate([a.astype(dtype).reshape(-1) for a in arrays])
    per = width * row_mult
    n = -(-flat.shape[0] // per) * per
    return jnp.pad(flat, (0, n - flat.shape[0])).reshape(n // width, width)


def _unpack(flat, shapes):
    out, off = [], 0
    for s in shapes:
        n = 1
        for d in s:
            n *= d
        out.append(flat[off:off + n].reshape(s))
        off += n
    return out


def _unpack_cols(flat2, shapes):
    out, off = [], 0
    for s in shapes:
        n = 1
        for d in s:
            n *= d
        out.append(flat2[:, off:off + n].reshape((flat2.shape[0],) + tuple(s)))
        off += n
    return out


def kernel(x, meta_tokens, norm_mix, norm_ffn, w_gate_up, w_down, fox_w_in, fox_b_f, fox_q_gain, fox_k_gain, fox_w_out, gla_w_in, gla_w_alpha2, gla_b_alpha, gla_o_gain, gla_w_out, gdn_w_in, gdn_conv_w, gdn_a_log, gdn_dt_bias, gdn_o_gain, gdn_w_out, loss_target, m_meta_tokens, m_norm_mix, m_norm_ffn, m_w_gate_up, m_w_down, m_fox_w_in, m_fox_b_f, m_fox_q_gain, m_fox_k_gain, m_fox_w_out, m_gla_w_in, m_gla_w_alpha2, m_gla_b_alpha, m_gla_o_gain, m_gla_w_out, m_gdn_w_in, m_gdn_conv_w, m_gdn_a_log, m_gdn_dt_bias, m_gdn_o_gain, m_gdn_w_out, v_meta_tokens, v_norm_mix, v_norm_ffn, v_w_gate_up, v_w_down, v_fox_w_in, v_fox_b_f, v_fox_q_gain, v_fox_k_gain, v_fox_w_out, v_gla_w_in, v_gla_w_alpha2, v_gla_b_alpha, v_gla_o_gain, v_gla_w_out, v_gdn_w_in, v_gdn_conv_w, v_gdn_a_log, v_gdn_dt_bias, v_gdn_o_gain, v_gdn_w_out):
    W = dict(meta_tokens=meta_tokens, norm_mix=norm_mix, norm_ffn=norm_ffn, w_gate_up=w_gate_up, w_down=w_down,
             fox_w_in=fox_w_in, fox_b_f=fox_b_f, fox_q_gain=fox_q_gain, fox_k_gain=fox_k_gain, fox_w_out=fox_w_out,
             gla_w_in=gla_w_in, gla_w_alpha2=gla_w_alpha2, gla_b_alpha=gla_b_alpha, gla_o_gain=gla_o_gain,
             gla_w_out=gla_w_out, gdn_w_in=gdn_w_in, gdn_conv_w=gdn_conv_w, gdn_a_log=gdn_a_log,
             gdn_dt_bias=gdn_dt_bias, gdn_o_gain=gdn_o_gain, gdn_w_out=gdn_w_out)
    M = dict(meta_tokens=m_meta_tokens, norm_mix=m_norm_mix, norm_ffn=m_norm_ffn, w_gate_up=m_w_gate_up, w_down=m_w_down,
             fox_w_in=m_fox_w_in, fox_b_f=m_fox_b_f, fox_q_gain=m_fox_q_gain, fox_k_gain=m_fox_k_gain,
             fox_w_out=m_fox_w_out, gla_w_in=m_gla_w_in, gla_w_alpha2=m_gla_w_alpha2, gla_b_alpha=m_gla_b_alpha,
             gla_o_gain=m_gla_o_gain, gla_w_out=m_gla_w_out, gdn_w_in=m_gdn_w_in, gdn_conv_w=m_gdn_conv_w,
             gdn_a_log=m_gdn_a_log, gdn_dt_bias=m_gdn_dt_bias, gdn_o_gain=m_gdn_o_gain, gdn_w_out=m_gdn_w_out)
    V = dict(meta_tokens=v_meta_tokens, norm_mix=v_norm_mix, norm_ffn=v_norm_ffn, w_gate_up=v_w_gate_up, w_down=v_w_down,
             fox_w_in=v_fox_w_in, fox_b_f=v_fox_b_f, fox_q_gain=v_fox_q_gain, fox_k_gain=v_fox_k_gain,
             fox_w_out=v_fox_w_out, gla_w_in=v_gla_w_in, gla_w_alpha2=v_gla_w_alpha2, gla_b_alpha=v_gla_b_alpha,
             gla_o_gain=v_gla_o_gain, gla_w_out=v_gla_w_out, gdn_w_in=v_gdn_w_in, gdn_conv_w=v_gdn_conv_w,
             gdn_a_log=v_gdn_a_log, gdn_dt_bias=v_gdn_dt_bias, gdn_o_gain=v_gdn_o_gain, gdn_w_out=v_gdn_w_out)
    chip = 2 * lax.axis_index("x") + lax.axis_index("y")

    pieces, offs, r = [], {}, FFN_ROWS
    for n in _IN_W:
        nc = W[n].shape[2]
        for l in range(W[n].shape[0]):
            pieces.append(jnp.pad(W[n][l].T.astype(BF16), ((0, _piece_rows(nc) - nc), (0, 0))))
            offs[n, l] = r
            r += _piece_rows(nc)
    for n in _OUT_W:
        for l in range(W[n].shape[0]):
            pieces.append(W[n][l].astype(BF16))
            offs[n, l] = r
            r += W[n].shape[1]
    rows = -(-r // _PACK_ROWS) * _PACK_ROWS
    packed = jnp.concatenate([jnp.swapaxes(w_gate_up, 1, 2).reshape(-1, D).astype(BF16), w_down.reshape(-1, D).astype(BF16)]
                             + pieces + [jnp.zeros((rows - r, D), BF16)], axis=0)
    first_rows = [(offs["fox_w_in", 0], offs["fox_w_in", 1] - offs["fox_w_in", 0]),
                  (offs["fox_w_out", 0], offs["fox_w_out", 1] - offs["fox_w_out", 0])]
    later_rows = [(0, FFN_ROWS), (offs["fox_w_in", 1], offs["fox_w_out", 0] - offs["fox_w_in", 1]),
                  (offs["fox_w_out", 1], r - offs["fox_w_out", 1])]
    wpk = _ag_weights(lax.dynamic_update_slice(lax.empty((4, rows, D), BF16), packed[None], (chip, 0, 0)), first_rows)

    def in_t(buf, n, l, npad):
        nc = W[n].shape[2]
        return jnp.concatenate([buf[q, offs[n, l]:offs[n, l] + nc] for q in range(4)] + [jnp.zeros((npad - 4 * nc, D), BF16)], 0)

    def out_w(buf, n, l):
        return jnp.concatenate([buf[q, offs[n, l]:offs[n, l] + W[n].shape[1]] for q in range(4)], axis=0)

    fox_in0, fox_out0 = in_t(wpk, "fox_w_in", 0, FOX_INP), out_w(wpk, "fox_w_out", 0)
    full = {}
    small = _pack([W[n] for n, _ in _SMALL_SHARDED], LANES, 8, F32)
    small_all = _gather8(small, reduce=False, name="gather_small").reshape(8, -1)
    for (n, ax), seg in zip(_SMALL_SHARDED, _unpack_cols(small_all, [W[n].shape for n, _ in _SMALL_SHARDED])):
        full[n] = jnp.concatenate([seg[2 * q] for q in range(4)], axis=ax)
    fox_in, full["fox_w_out"] = [fox_in0], [fox_out0]
    w_alpha2, conv_w = full["gla_w_alpha2"][0], full["gdn_conv_w"][0]

    h = jnp.concatenate([jnp.zeros((META0, D), F32), full["meta_tokens"], x[0]], axis=0)
    saved = []
    y = _rms_fwd(h, norm_mix[0], name="norm_mix0")
    for i in range(DEPTH):
        kind, j = i % 3, i // 3
        if kind == 0:
            proj = _mm(y, fox_in[j], tb=True, name=f"fox_in{j}")
            qa, ka, va = _fox_prep(proj, fox_b_f[j], fox_q_gain[j], fox_k_gain[j])
            if i == 0:
                o, og, lse, wpk = _fox_attn_fwd(qa, ka, va, proj, ag=(wpk, later_rows))
                wpk = _ag_forward(wpk, later_rows)
                fox_in += [in_t(wpk, "fox_w_in", l, FOX_INP) for l in range(1, fox_w_in.shape[0])]
                full["fox_w_out"] += [out_w(wpk, "fox_w_out", l) for l in range(1, fox_w_out.shape[0])]
                gla_in = [in_t(wpk, "gla_w_in", l, GLA_INP) for l in range(gla_w_in.shape[0])]
                gdn_in = [in_t(wpk, "gdn_w_in", l, GDN_INP) for l in range(gdn_w_in.shape[0])]
                for n in ("gla_w_out", "gdn_w_out"):
                    full[n] = [out_w(wpk, n, l) for l in range(W[n].shape[0])]
            else:
                o, og, lse = _fox_attn_fwd(qa, ka, va, proj)
            w_out, mix = full["fox_w_out"][j], (proj, qa, ka, va, o, lse)
        elif kind == 1:
            proj = _mm(y, gla_in[j], tb=True, name=f"gla_in{j}")
            o, og, states = _gla_fwd(proj, w_alpha2, gla_b_alpha[j], gla_o_gain[j])
            w_out, mix = full["gla_w_out"][j], (proj, o, states)
        else:
            proj = _mm(y, gdn_in[j], tb=True, name=f"gdn_in{j}")
            o, og, states = _gdn_fwd(proj, conv_w, gdn_a_log[j], gdn_dt_bias[j], gdn_o_gain[j])
            w_out, mix = full["gdn_w_out"][j], (proj, o, states)
        hm, yf = _mm(og, w_out, add=h, norm=norm_ffn[i], name=f"mix_out{i}")
        gate, up, act = _ffn_up(yf, wpk, i)
        hn, y_next = _ffn_down(act, wpk, i, hm, norm_mix[(i + 1) % DEPTH])
        saved.append((h, y, mix, og, w_out, hm, yf, gate, up, act))
        h, y = hn, y_next
    dh, loss_tile = _loss_head(h, loss_target[0])

    G = {n: [None] * W[n].shape[0] for n in _WEIGHTS if n not in ("meta_tokens", "w_gate_up", "w_down") + _IN_W}
    GT = {}

    def grad_layout(ffn_layers, pieces):
        off, end = {}, 0
        for l in ffn_layers:
            off["gu", l] = end
            end += GU_ROWS
        for l in ffn_layers:
            off["down", l] = end
            end += DOWN_ROWS
        for n, l in pieces:
            off[n, l] = end
            end += _piece_rows(W[n].shape[2]) if n in _IN_W else W[n].shape[1]
        return off, end, -(-end // _PACK_ROWS) * _PACK_ROWS

    first_pieces = [("fox_w_in", 0)]
    later_pieces = [(n, l) for n in _IN_W + _OUT_W for l in range(W[n].shape[0]) if (n, l) not in first_pieces]
    layouts = [grad_layout([], first_pieces), grad_layout(list(range(DEPTH)), later_pieces)]
    gbuf = [jnp.zeros((4, lay[2], D), BF16) for lay in layouts]

    def with_pieces(buf, lay, pieces):
        off, end, total = lay
        blocks = []
        for q in range(4):
            parts = []
            for n, l in pieces:
                if n in _IN_W:
                    nc = W[n].shape[2]
                    parts.append(jnp.pad(GT[n, l][q * nc:(q + 1) * nc], ((0, _piece_rows(nc) - nc), (0, 0))))
                else:
                    nr = W[n].shape[1]
                    parts.append(G[n][l][q * nr:(q + 1) * nr])
            blocks.append(jnp.concatenate(parts + [jnp.zeros((total - end, D), BF16)], axis=0))
        return lax.dynamic_update_slice(buf, jnp.stack(blocks), (0, off[pieces[0]], 0))

    s_later = None
    for i in reversed(range(DEPTH)):
        kind, j = i % 3, i // 3
        h_in, y, mix, og, w_out, hm, yf, gate, up, act = saved[i]
        b = 1
        dg, du = _ffn_dact(dh, wpk, i, gate, up)
        gbuf[b] = _ffn_dw_down(act, dh, gbuf[b], i, layouts[b][0]["down", i])
        dhm, dnf = _ffn_dyf(dg, du, wpk, i, hm, norm_ffn[i], dh)
        gbuf[b] = _ffn_dw_gu(dg, du, yf, gbuf[b], i, layouts[b][0]["gu", i] // GU_ROWS)
        G["norm_ffn"][i] = dnf[0]
        dog = _mm(dhm, w_out, tb=True, name=f"d_og{i}")
        dw_out = _mm(og, dhm, ta=True, out_dtype=BF16, name=f"d_w_out{i}")
        if kind == 0:
            proj, qa, ka, va, o, lse = mix
            G["fox_w_out"][j] = dw_out
            if i == 0:
                g_later = with_pieces(gbuf[1], layouts[1], later_pieces)
                doa, q2, dgate, got = _fox_gate_bwd(dog, o, proj, lse, qa, swap=g_later)
                s_later = _add_halves(g_later, got, "_later")
                dqn, dkn, dv, dct, recv_later = _fox_attn_bwd(q2, ka, va, doa, rs=s_later)
            else:
                doa, q2, dgate = _fox_gate_bwd(dog, o, proj, lse, qa)
                dqn, dkn, dv, dct = _fox_attn_bwd(q2, ka, va, doa)
            dproj, dqg, dkg, dbf = _fox_prep_bwd(proj, fox_b_f[j], fox_q_gain[j], fox_k_gain[j], dqn, dkn, dv, dgate, dct)
            G["fox_q_gain"][j] = dqg.reshape(FOX_H, FOX_DH).sum(0)
            G["fox_k_gain"][j] = dkg.reshape(FOX_H, FOX_DH).sum(0)
            G["fox_b_f"][j] = dbf[0, :FOX_H]
            w_in, wname = fox_in[j], "fox_w_in"
        elif kind == 1:
            proj, o, states = mix
            dproj, dwa, dba, dogain = _gla_bwd(proj, w_alpha2, gla_b_alpha[j], gla_o_gain[j], o, states, dog)
            G["gla_w_out"][j] = dw_out
            G["gla_w_alpha2"][j] = dwa[:GLA_RANK]
            G["gla_b_alpha"][j] = dba[0]
            G["gla_o_gain"][j] = dogain[0]
            w_in, wname = gla_in[j], "gla_w_in"
        else:
            proj, o, states = mix
            dproj, dcw, dal, ddt, dogain = _gdn_bwd(proj, conv_w, gdn_a_log[j], gdn_dt_bias[j], gdn_o_gain[j], o, states, dog)
            G["gdn_w_out"][j] = dw_out
            G["gdn_conv_w"][j] = dcw[:4].reshape(4, 1, GDN_CONV)
            G["gdn_a_log"][j] = dal[0, :GDN_H]
            G["gdn_dt_bias"][j] = ddt[0, :GDN_H]
            G["gdn_o_gain"][j] = dogain[0]
            w_in, wname = gdn_in[j], "gdn_w_in"
        dh, dnm = _mm(dproj, w_in, rms_bwd=(h_in, norm_mix[i], dhm), name=f"d_y{i}")
        GT[wname, j] = _mm(dproj, y, ta=True, out_dtype=BF16, name=f"d_w_in{i}")
        G["norm_mix"][i] = dnm[0]
    grad_x = dh[ROW0:][None]
    G = {n: (v if n in _OUT_W else jnp.stack(v)) for n, v in G.items()}
    G["meta_tokens"] = dh[META0:ROW0]

    s_first = _rs_local(with_pieces(gbuf[0], layouts[0], first_pieces), "_first")
    reduced = [_rs_finish(s_first, _scatter_chips(s_first, "_first"), "_first"), _rs_finish(s_later, recv_later, "_later")]

    def reduced_piece(n, l):
        b = 0 if (n, l) in first_pieces else 1
        start = layouts[b][0][n, l]
        return reduced[b][start:start + (W[n].shape[2] if n in _IN_W else W[n].shape[1])]

    grads = {}
    for n in _IN_W:
        grads[n] = jnp.stack([reduced_piece(n, l).T for l in range(W[n].shape[0])])
    for n in _OUT_W:
        grads[n] = jnp.stack([reduced_piece(n, l) for l in range(W[n].shape[0])])
    small_names = [n for n, _ in _SMALL_SHARDED] + list(_REPLICATED)
    small_g = _pack([G[n] for n in small_names] + [loss_tile[0, 0:1]], LANES, 8, F32)
    small_sum = _gather8(small_g, reduce=True, name="allreduce_small").reshape(-1)
    small_shapes = [G[n].shape for n in small_names] + [(1,)]
    small_vals = _unpack(small_sum, small_shapes)
    loss = small_vals[-1][0]
    for n, val in zip(small_names, small_vals[:-1]):
        grads[n] = val
    for n, ax in _SMALL_SHARDED:
        sz = W[n].shape[ax]
        grads[n] = lax.dynamic_slice_in_dim(grads[n], chip * sz, sz, axis=ax)

    delta, new_m, new_v = {}, {}, {}
    for n, key, tr_ in (("w_gate_up", "gu", True), ("w_down", "down", False)):
        grads[n], delta[n], new_m[n], new_v[n] = _adamw_packed(
            W[n], reduced[1], reduced[1], M[n], V[n], row0=layouts[1][0][key, 0], row_off=layouts[1][0][key, 1],
            transposed=tr_, name=f"adamw_{n}")
    for n in _IN_W + _OUT_W:
        delta[n], new_m[n], new_v[n] = _adamw(W[n], grads[n], M[n], V[n], name=f"adamw_{n}")
    tiny = [n for n in _WEIGHTS if n not in ("w_gate_up", "w_down") + _IN_W + _OUT_W]
    packs = [_pack([T[n] for n in tiny], LANES, 8, F32) for T in (W, grads, M, V)]
    outs = _adamw(*packs, name="adamw_small")
    shapes = [W[n].shape for n in tiny]
    for dst, o in zip((delta, new_m, new_v), outs):
        for n, val in zip(tiny, _unpack(o.reshape(-1), shapes)):
            dst[n] = val
    return (loss, grad_x, *[grads[n] for n in _WEIGHTS], *[delta[n] for n in _WEIGHTS],
            *[new_m[n] for n in _WEIGHTS], *[new_v[n] for n in _WEIGHTS])
```

```python
import jax
import jax.numpy as jnp
from jax import lax
from jax.experimental import pallas as pl
from jax.experimental.pallas import tpu as pltpu

F32, BF16 = jnp.float32, jnp.bfloat16
D = 1024
N_META = 16
ROW0 = 128
META0 = ROW0 - N_META
EPS = 1e-6
LANES = 128
VMEM_LIMIT = 56 * 1024 * 1024

FOX_H, FOX_DH = 16, 64
FOX_INP = 4224
GLA_H, GLA_DK, GLA_DV, GLA_RANK = 4, 128, 256, 16
GLA_QK, GLA_V = 512, 1024
GLA_INP = 3200
GLA_NORM = 16.0
GDN_H, GDN_DK, GDN_DV = 8, 128, 128
GDN_CONV = 3072
GDN_INP = 4224
CHUNK = 64
D_FF = 2816
DEPTH = 4

ADAM_LR, ADAM_B1, ADAM_B2, ADAM_EPS, ADAM_WD, ADAM_STEP = 0.001, 0.9, 0.999, 1e-08, 0.01, 10

MESH = pl.DeviceIdType.MESH
ANY = pl.BlockSpec(memory_space=pl.ANY)
VM = pl.BlockSpec(memory_space=pltpu.VMEM)


def _params(sem=None, **kw):
    if sem is not None:
        kw["dimension_semantics"] = sem
    return pltpu.CompilerParams(vmem_limit_bytes=VMEM_LIMIT, **kw)


def _tile(n, cap, mult=LANES):
    best = None
    for t in range(mult, min(n, cap) + 1, mult):
        if n % t == 0:
            best = t
    return best if best is not None else n


def nn(a, b, **kw):
    return jnp.dot(a, b, preferred_element_type=F32, **kw)


def nt(a, b, **kw):
    return lax.dot_general(a, b, (((1,), (1,)), ((), ())), preferred_element_type=F32, **kw)


def tn(a, b, **kw):
    return lax.dot_general(a, b, (((0,), (0,)), ((), ())), preferred_element_type=F32, **kw)


def _split3(x):
    hi = x.astype(BF16)
    r = x - hi.astype(F32)
    mid = r.astype(BF16)
    lo = (r - mid.astype(F32)).astype(BF16)
    return hi, mid, lo


def _sel_l(sel, x):
    a, b, c = _split3(x)
    return nn(sel, a) + nn(sel, b) + nn(sel, c)


def _sel_r(x, sel):
    a, b, c = _split3(x)
    return nn(a, sel) + nn(b, sel) + nn(c, sel)


def _sel_r2(x, sel):
    a = x.astype(BF16)
    return nn(a, sel) + nn((x - a.astype(F32)).astype(BF16), sel)


def _iota(shape, dim):
    return lax.broadcasted_iota(jnp.int32, shape, dim)


def _tri(n, upper=False, strict=False):
    i, j = _iota((n, n), 0), _iota((n, n), 1)
    if upper:
        m = (j > i) if strict else (j >= i)
    else:
        m = (j < i) if strict else (j <= i)
    return m


def _sigmoid(x):
    return 1.0 / (1.0 + jnp.exp(-x))


def _log_sigmoid(x):
    return jnp.minimum(x, 0.0) - jnp.log(1.0 + jnp.exp(-jnp.abs(x)))


def _softplus(x):
    return jnp.maximum(x, 0.0) + jnp.log(1.0 + jnp.exp(-jnp.abs(x)))


def _silu(x):
    return x * _sigmoid(x)


def _dsilu(x):
    s = _sigmoid(x)
    return s * (1.0 + x * (1.0 - s))


def _rms(x, g):
    return (x * lax.rsqrt(jnp.mean(x * x, axis=-1, keepdims=True) + EPS) * g).astype(BF16)


def _rms_grad(x, g, dy):
    r = lax.rsqrt(jnp.mean(x * x, axis=-1, keepdims=True) + EPS)
    u = dy * g
    return r * u - x * (r * r * r) * jnp.mean(x * u, axis=-1, keepdims=True), jnp.sum(dy * x * r, axis=0, keepdims=True)


def _mm(a, b, *, ta=False, tb=False, add=None, norm=None, rms_bwd=None, out_dtype=F32, name):
    m, k = (a.shape[1], a.shape[0]) if ta else a.shape
    n = b.shape[0] if tb else b.shape[1]
    assert k == (b.shape[1] if tb else b.shape[0])
    rows_whole = norm is not None or rms_bwd is not None
    tm, tn_, tk = _tile(m, 704 if rows_whole else 1408, LANES if ta else 16), _tile(n, 1408), _tile(k, 1408)
    nk = k // tk
    assert not rows_whole or tn_ == n

    def body(*refs):
        refs = list(refs)
        a_ref, b_ref = refs[:2]
        extra = refs[2:-1]
        acc = refs[-1]
        i, kk = pl.program_id(0), pl.program_id(2)

        @pl.when(kk == 0)
        def _():
            acc[...] = jnp.zeros_like(acc)

        av, bv = a_ref[...].astype(BF16), b_ref[...].astype(BF16)
        dims = (((0,) if ta else (1,), (1,) if tb else (0,)), ((), ()))
        acc[...] += lax.dot_general(av, bv, dims, preferred_element_type=F32)

        @pl.when(kk == nk - 1)
        def _():
            r = acc[...]
            if rms_bwd is not None:
                h_ref, g_ref, dres_ref, o_ref, dg_ref = extra
                dx, dgain = _rms_grad(h_ref[...], g_ref[...], r)
                o_ref[...] = dres_ref[...] + dx

                @pl.when(i == 0)
                def _():
                    dg_ref[...] = jnp.zeros_like(dg_ref)

                dg_ref[...] += dgain
                return
            if add is not None:
                r = r + extra[0][...].astype(F32)
            if norm is not None:
                g_ref, o_ref, y_ref = extra[-3:]
                y_ref[...] = _rms(r, g_ref[...])
            else:
                o_ref = extra[-1]
            o_ref[...] = r.astype(out_dtype)

    a_spec = pl.BlockSpec((tk, tm), lambda i, j, q: (q, i)) if ta else pl.BlockSpec((tm, tk), lambda i, j, q: (i, q))
    b_spec = pl.BlockSpec((tn_, tk), lambda i, j, q: (j, q)) if tb else pl.BlockSpec((tk, tn_), lambda i, j, q: (q, j))
    o_spec = pl.BlockSpec((tm, tn_), lambda i, j, q: (i, j))
    g_spec = pl.BlockSpec((1, n), lambda i, j, q: (0, 0))
    ins, specs = [a, b], [a_spec, b_spec]
    out_specs, out_shape = o_spec, jax.ShapeDtypeStruct((m, n), out_dtype)
    sem = ("parallel", "parallel", "arbitrary")
    if rms_bwd is not None:
        ins += [rms_bwd[0], rms_bwd[1].reshape(1, n), rms_bwd[2]]
        specs += [o_spec, g_spec, o_spec]
        out_specs, out_shape = [o_spec, g_spec], [jax.ShapeDtypeStruct((m, n), F32), jax.ShapeDtypeStruct((1, n), F32)]
        sem = ("arbitrary", "arbitrary", "arbitrary")
    else:
        if add is not None:
            ins.append(add)
            specs.append(o_spec)
        if norm is not None:
            ins.append(norm.reshape(1, n))
            specs.append(g_spec)
            out_specs, out_shape = [o_spec, o_spec], [out_shape, jax.ShapeDtypeStruct((m, n), BF16)]
    return pl.pallas_call(
        body, name=name, grid=(m // tm, n // tn_, nk), in_specs=specs, out_specs=out_specs, out_shape=out_shape,
        scratch_shapes=[pltpu.VMEM((tm, tn_), F32)], compiler_params=_params(sem),
    )(*ins)


def _rms_fwd(h, g, *, name):
    lp = h.shape[0]
    tr = _tile(lp, 512)

    def body(h_ref, g_ref, y_ref):
        x = h_ref[...]
        r = lax.rsqrt(jnp.mean(x * x, axis=-1, keepdims=True) + EPS)
        y_ref[...] = (x * r * g_ref[...]).astype(BF16)

    return pl.pallas_call(
        body, name=name, grid=(lp // tr,),
        in_specs=[pl.BlockSpec((tr, D), lambda i: (i, 0)), pl.BlockSpec((1, D), lambda i: (0, 0))],
        out_specs=pl.BlockSpec((tr, D), lambda i: (i, 0)),
        out_shape=jax.ShapeDtypeStruct((lp, D), BF16), compiler_params=_params(("parallel",)),
    )(h, g.reshape(1, D))


GU_ROWS, DOWN_ROWS = 1408, 704
OFF_GU, OFF_DOWN = 0, DEPTH * GU_ROWS
FFN_ROWS = DEPTH * (GU_ROWS + DOWN_ROWS)
FFN_TM = 704


def _gu_spec(fn):
    return pl.BlockSpec((None, GU_ROWS, D), fn)


def _down_spec(fn):
    return pl.BlockSpec((None, DOWN_ROWS, D), fn)


def _down_pair(w0_ref, w1_ref):
    return jnp.concatenate([w0_ref[...], w1_ref[...]], axis=0)


def _ffn_up(yf, wpk, layer):
    lp = yf.shape[0]
    tm = _tile(lp, FFN_TM, 16)

    def body(y_ref, wg_ref, wu_ref, g_ref, u_ref, a_ref):
        y = y_ref[...]
        g, u = nt(y, wg_ref[...]), nt(y, wu_ref[...])
        g_ref[...] = g.astype(BF16)
        u_ref[...] = u.astype(BF16)
        a_ref[...] = (_silu(g) * u).astype(BF16)

    o = pl.BlockSpec((tm, GU_ROWS), lambda i, j: (i, j))
    return pl.pallas_call(
        body, name=f"ffn_up{layer}", grid=(lp // tm, 2),
        in_specs=[pl.BlockSpec((tm, D), lambda i, j: (i, 0)), _gu_spec(lambda i, j: (j, OFF_GU // GU_ROWS + layer, 0)),
                  _gu_spec(lambda i, j: (2 + j, OFF_GU // GU_ROWS + layer, 0))],
        out_specs=[o, o, o], out_shape=[jax.ShapeDtypeStruct((lp, D_FF), BF16)] * 3,
        compiler_params=_params(("parallel", "parallel")),
    )(yf, wpk, wpk)


def _ffn_down(act, wpk, layer, res, norm):
    lp = act.shape[0]
    tm = _tile(lp, FFN_TM, 16)

    def body(a_ref, w0_ref, w1_ref, r_ref, g_ref, o_ref, y_ref, acc):
        kk = pl.program_id(1)

        @pl.when(kk == 0)
        def _():
            acc[...] = r_ref[...]

        acc[...] += nn(a_ref[...], _down_pair(w0_ref, w1_ref))

        @pl.when(kk == 1)
        def _():
            o_ref[...] = acc[...]
            y_ref[...] = _rms(acc[...], g_ref[...])

    o = pl.BlockSpec((tm, D), lambda i, kk: (i, 0))
    blk = OFF_DOWN // DOWN_ROWS + layer
    return pl.pallas_call(
        body, name=f"ffn_down{layer}", grid=(lp // tm, 2),
        in_specs=[pl.BlockSpec((tm, GU_ROWS), lambda i, kk: (i, kk)), _down_spec(lambda i, kk: (2 * kk, blk, 0)),
                  _down_spec(lambda i, kk: (2 * kk + 1, blk, 0)), o, pl.BlockSpec((1, D), lambda i, kk: (0, 0))],
        out_specs=[o, o], out_shape=[jax.ShapeDtypeStruct((lp, D), F32), jax.ShapeDtypeStruct((lp, D), BF16)],
        scratch_shapes=[pltpu.VMEM((tm, D), F32)], compiler_params=_params(("parallel", "arbitrary")),
    )(act, wpk, wpk, res, norm.reshape(1, D))


def _ffn_dact(dh, wpk, layer, gate, up):
    lp = dh.shape[0]
    tm = _tile(lp, FFN_TM, 16)

    def body(d_ref, w0_ref, w1_ref, g_ref, u_ref, dg_ref, du_ref):
        da = nt(d_ref[...].astype(BF16), _down_pair(w0_ref, w1_ref))
        g, u = g_ref[...].astype(F32), u_ref[...].astype(F32)
        sg = _sigmoid(g)
        dg_ref[...] = (da * u * (sg * (1.0 + g * (1.0 - sg)))).astype(BF16)
        du_ref[...] = (da * (g * sg)).astype(BF16)

    o = pl.BlockSpec((tm, GU_ROWS), lambda i, j: (i, j))
    blk = OFF_DOWN // DOWN_ROWS + layer
    return pl.pallas_call(
        body, name=f"d_act{layer}", grid=(lp // tm, 2),
        in_specs=[pl.BlockSpec((tm, D), lambda i, j: (i, 0)), _down_spec(lambda i, j: (2 * j, blk, 0)),
                  _down_spec(lambda i, j: (2 * j + 1, blk, 0)), o, o],
        out_specs=[o, o], out_shape=[jax.ShapeDtypeStruct((lp, D_FF), BF16)] * 2,
        compiler_params=_params(("parallel", "parallel")),
    )(dh, wpk, wpk, gate, up)


def _ffn_dyf(dg, du, wpk, layer, hm, norm, dres):
    lp = dg.shape[0]
    tm = _tile(lp, FFN_TM, 16)

    def body(dg_ref, du_ref, w_ref, h_ref, g_ref, dres_ref, o_ref, dgain_ref, acc):
        i, kk = pl.program_id(0), pl.program_id(1)

        @pl.when(kk == 0)
        def _():
            acc[...] = jnp.zeros_like(acc)

        @pl.when(kk < 2)
        def _():
            acc[...] += nn(dg_ref[...], w_ref[...])

        @pl.when(kk >= 2)
        def _():
            acc[...] += nn(du_ref[...], w_ref[...])

        @pl.when(kk == 3)
        def _():
            dx, dgain = _rms_grad(h_ref[...], g_ref[...], acc[...])
            o_ref[...] = dres_ref[...] + dx

            @pl.when(i == 0)
            def _():
                dgain_ref[...] = jnp.zeros_like(dgain_ref)

            dgain_ref[...] += dgain

    o = pl.BlockSpec((tm, D), lambda i, kk: (i, 0))
    row = pl.BlockSpec((1, D), lambda i, kk: (0, 0))
    return pl.pallas_call(
        body, name=f"d_yf{layer}", grid=(lp // tm, 4),
        in_specs=[pl.BlockSpec((tm, GU_ROWS), lambda i, kk: (i, jnp.minimum(kk, 1))),
                  pl.BlockSpec((tm, GU_ROWS), lambda i, kk: (i, jnp.maximum(kk - 2, 0))),
                  _gu_spec(lambda i, kk: (kk, OFF_GU // GU_ROWS + layer, 0)), o, row, o],
        out_specs=[o, row], out_shape=[jax.ShapeDtypeStruct((lp, D), F32), jax.ShapeDtypeStruct((1, D), F32)],
        scratch_shapes=[pltpu.VMEM((tm, D), F32)], compiler_params=_params(("arbitrary", "arbitrary")),
    )(dg, du, wpk, hm, norm.reshape(1, D), dres)


def _ffn_dw_down(act, dh, gpk, layer, row):
    lp = act.shape[0]
    tk = _tile(lp, 1408, 16)
    nk = lp // tk

    def body(a_ref, d_ref, g_in, g_out, acc, stage, sems):
        jp, kk = pl.program_id(0), pl.program_id(1)

        @pl.when(kk == 0)
        def _():
            acc[...] = jnp.zeros_like(acc)

        acc[...] += tn(a_ref[...], d_ref[...].astype(BF16))

        @pl.when(kk == nk - 1)
        def _():
            stage[...] = acc[...].astype(BF16)
            copies = [pltpu.make_async_copy(stage.at[pl.ds(hf * DOWN_ROWS, DOWN_ROWS), :],
                                            g_out.at[2 * jp + hf, pl.ds(row, DOWN_ROWS), :], sems.at[hf]) for hf in range(2)]
            for cp in copies:
                cp.start()
            for cp in copies:
                cp.wait()

    return pl.pallas_call(
        body, name=f"d_w_down{layer}", grid=(2, nk),
        in_specs=[pl.BlockSpec((tk, GU_ROWS), lambda jp, kk: (kk, jp)), pl.BlockSpec((tk, D), lambda jp, kk: (kk, 0)), ANY],
        out_specs=ANY, out_shape=jax.ShapeDtypeStruct(gpk.shape, gpk.dtype),
        scratch_shapes=[pltpu.VMEM((GU_ROWS, D), F32), pltpu.VMEM((GU_ROWS, D), BF16), pltpu.SemaphoreType.DMA((2,))],
        input_output_aliases={2: 0}, compiler_params=_params(("arbitrary", "arbitrary")),
    )(act, dh, gpk)


def _ffn_dw_gu(dg, du, yf, gpk, layer, blk):
    lp = dg.shape[0]
    tk = _tile(lp, 1408, 16)
    nk = lp // tk

    def body(dg_ref, du_ref, y_ref, g_in, o_ref, acc):
        c, kk = pl.program_id(0), pl.program_id(1)

        @pl.when(kk == 0)
        def _():
            acc[...] = jnp.zeros_like(acc)

        @pl.when(c < 2)
        def _():
            acc[...] += tn(dg_ref[...], y_ref[...])

        @pl.when(c >= 2)
        def _():
            acc[...] += tn(du_ref[...], y_ref[...])

        @pl.when(kk == nk - 1)
        def _():
            o_ref[...] = acc[...].astype(BF16)

    return pl.pallas_call(
        body, name=f"d_w_gate_up{layer}", grid=(4, nk),
        in_specs=[pl.BlockSpec((tk, GU_ROWS), lambda c, kk: (kk, jnp.minimum(c, 1))),
                  pl.BlockSpec((tk, GU_ROWS), lambda c, kk: (kk, jnp.maximum(c - 2, 0))),
                  pl.BlockSpec((tk, D), lambda c, kk: (kk, 0)), ANY],
        out_specs=_gu_spec(lambda c, kk: (c, blk, 0)),
        out_shape=jax.ShapeDtypeStruct(gpk.shape, gpk.dtype),
        scratch_shapes=[pltpu.VMEM((GU_ROWS, D), F32)], input_output_aliases={3: 0},
        compiler_params=_params(("parallel", "arbitrary")),
    )(dg, du, yf, gpk)


def _loss_head(h, target):
    lp = h.shape[0]
    nb = lp // ROW0

    def body(h_ref, t_ref, dh_ref, l_ref):
        i = pl.program_id(0)

        @pl.when(i == 0)
        def _():
            l_ref[...] = jnp.zeros_like(l_ref)
            dh_ref[...] = jnp.zeros_like(dh_ref)

        @pl.when(i > 0)
        def _():
            err = h_ref[...] - t_ref[...]
            dh_ref[...] = err * (1.0 / D)
            l_ref[...] += jnp.sum(err * err) * (0.5 / D)

    return pl.pallas_call(
        body, name="loss_head", grid=(nb,),
        in_specs=[pl.BlockSpec((ROW0, D), lambda i: (i, 0)), pl.BlockSpec((ROW0, D), lambda i: (jnp.maximum(i - 1, 0), 0))],
        out_specs=[pl.BlockSpec((ROW0, D), lambda i: (i, 0)), pl.BlockSpec((8, LANES), lambda i: (0, 0))],
        out_shape=[jax.ShapeDtypeStruct((lp, D), F32), jax.ShapeDtypeStruct((8, LANES), F32)],
        compiler_params=_params(("arbitrary",)),
    )(h, target)


def _adamw(w, g, m, v, *, name):
    if w.ndim == 2:
        w, g, m, v = (t[None] for t in (w, g, m, v))
        return tuple(o[0] for o in _adamw(w, g, m, v, name=name))
    nl, r, c = w.shape
    tr = _tile(r, max(8, (1 << 19) // c), 8)

    def body(w_ref, g_ref, m_ref, v_ref, d_ref, nm_ref, nv_ref):
        d_ref[...], nm_ref[...], nv_ref[...] = _adam_math(w_ref[...], g_ref[...], m_ref[...], v_ref[...])

    spec = pl.BlockSpec((1, tr, c), lambda l, i: (l, i, 0))
    return tuple(pl.pallas_call(
        body, name=name, grid=(nl, r // tr), in_specs=[spec] * 4, out_specs=[spec] * 3,
        out_shape=[jax.ShapeDtypeStruct(w.shape, F32)] * 3, compiler_params=_params(("parallel", "parallel")),
    )(w, g, m, v))


def _adam_math(w, g, m, v):
    nm = ADAM_B1 * m + (1.0 - ADAM_B1) * g
    nv = ADAM_B2 * v + (1.0 - ADAM_B2) * (g * g)
    m_hat = nm / (1.0 - ADAM_B1 ** ADAM_STEP)
    v_hat = nv / (1.0 - ADAM_B2 ** ADAM_STEP)
    return -ADAM_LR * (m_hat / (jnp.sqrt(v_hat) + ADAM_EPS) + ADAM_WD * w), nm, nv


def _adamw_packed(w, gred0, gred, m, v, *, row0, row_off, transposed, name):
    nl, a, b = w.shape
    nr = b if transposed else a
    later = lambda l: row_off // nr + jnp.maximum(l - 1, 0)
    if transposed:
        ta = _tile(a, 256)
        wspec = pl.BlockSpec((1, ta, b), lambda l, r: (l, r, 0))
        g0spec = pl.BlockSpec((b, ta), lambda l, r: (row0 // nr, r))
        gspec = pl.BlockSpec((b, ta), lambda l, r: (later(l), r))
        grid = (nl, a // ta)
    else:
        wspec = pl.BlockSpec((1, a, b), lambda l, r: (l, 0, 0))
        g0spec = pl.BlockSpec((a, b), lambda l, r: (row0 // nr, 0))
        gspec = pl.BlockSpec((a, b), lambda l, r: (later(l), 0))
        grid = (nl, 1)

    def body(w_ref, g0_ref, g_ref, m_ref, v_ref, go_ref, d_ref, nm_ref, nv_ref):
        g = jnp.where(pl.program_id(0) == 0, g0_ref[...], g_ref[...])
        g = g.T if transposed else g
        d, nm, nv = _adam_math(w_ref[0], g, m_ref[0], v_ref[0])
        go_ref[0], d_ref[0], nm_ref[0], nv_ref[0] = g, d, nm, nv

    return pl.pallas_call(
        body, name=name, grid=grid, in_specs=[wspec, g0spec, gspec, wspec, wspec], out_specs=[wspec] * 4,
        out_shape=[jax.ShapeDtypeStruct(w.shape, F32)] * 4, compiler_params=_params(("parallel", "parallel")),
    )(w, gred0, gred, m, v)


FOX_AUG = FOX_H * LANES
L_C = 64
L_K = 67
L_LSE = 70
PAD_KEY = -30000.0
FOX_TQ = 384


def _head_sel(n_heads, width, lanes=LANES):
    r, c = _iota((n_heads * width, lanes), 0), _iota((n_heads * width, lanes), 1)
    down = (r // width == c).astype(BF16)
    r2, c2 = _iota((lanes, n_heads * width), 0), _iota((lanes, n_heads * width), 1)
    up = (c2 // width == r2).astype(BF16)
    return down, up


def _place(lane0):
    r, c = _iota((LANES, FOX_AUG), 0), _iota((LANES, FOX_AUG), 1)
    return [((c // LANES == r) & (c % LANES == lane0 + m)).astype(BF16) for m in range(3)]


def _placed(x, lane0):
    pcs = _split3(x)
    mats = _place(lane0)
    return nn(pcs[0], mats[0]) + nn(pcs[1], mats[1]) + nn(pcs[2], mats[2])


def _ones_at(rows, lanes):
    c = _iota((rows, FOX_AUG), 1) % LANES
    m = c == lanes[0]
    for l in lanes[1:]:
        m = m | (c == l)
    return m.astype(F32)


def _spread(x, extras, out_ref):
    rows = x.shape[0]
    left = _iota((rows, LANES), 1) < FOX_DH
    for p in range(FOX_H // 2):
        slab = x[:, p * LANES:(p + 1) * LANES]
        a = jnp.where(left, slab, extras[:, 2 * p * LANES:(2 * p + 1) * LANES])
        b = jnp.where(left, pltpu.roll(slab, FOX_DH, 1), extras[:, (2 * p + 1) * LANES:(2 * p + 2) * LANES])
        out_ref[:, 2 * p * LANES:(2 * p + 1) * LANES] = a.astype(BF16)
        out_ref[:, (2 * p + 1) * LANES:(2 * p + 2) * LANES] = b.astype(BF16)


def _fox_prep(proj, b_f, q_gain, k_gain):
    lp = proj.shape[0]
    nb = lp // LANES

    def body(p_ref, bf_ref, qg_ref, kg_ref, q_ref, k_ref, v_ref, carry):
        i = pl.program_id(0)

        @pl.when(i == 0)
        def _():
            carry[...] = jnp.zeros_like(carry)

        down, up = _head_sel(FOX_H, FOX_DH)

        def normed(x, gain):
            ms = _sel_r2(x * x, down) * (1.0 / FOX_DH)
            r = _sel_r2(lax.rsqrt(ms + EPS), up)
            return x * r * gain

        lane = _iota((LANES, LANES), 1)
        lf = jnp.where(lane < FOX_H, _log_sigmoid(p_ref[:, 4 * D:4 * D + LANES] + bf_ref[...]), 0.0)
        c = _sel_l(_tri(LANES).astype(BF16), lf) + carry[0:1, :]
        carry[...] = jnp.broadcast_to(c[LANES - 1:LANES, :], carry.shape)
        q_extra = _placed(c, L_C) + _ones_at(LANES, (L_K, L_K + 1, L_K + 2))
        row = i * LANES + _iota((LANES, FOX_AUG), 0)
        lane_a = _iota((LANES, FOX_AUG), 1) % LANES
        k_extra = -_placed(c, L_K) + _ones_at(LANES, (L_C, L_C + 1, L_C + 2, L_LSE, L_LSE + 1, L_LSE + 2))
        pad_val = jnp.where(lane_a == L_K, PAD_KEY, 0.0)
        k_extra = jnp.where((row < META0) & (lane_a >= L_K) & (lane_a < L_K + 3), pad_val, k_extra)
        v_extra = _ones_at(LANES, (L_C, L_C + 1, L_C + 2))
        _spread(normed(p_ref[:, 0:D], qg_ref[...]) * (FOX_DH ** -0.5), q_extra, q_ref)
        _spread(normed(p_ref[:, D:2 * D], kg_ref[...]), k_extra, k_ref)
        _spread(p_ref[:, 2 * D:3 * D], v_extra, v_ref)

    row = pl.BlockSpec((1, D), lambda i: (0, 0))
    aug = pl.BlockSpec((LANES, FOX_AUG), lambda i: (i, 0))
    return pl.pallas_call(
        body, name="fox_prep", grid=(nb,),
        in_specs=[pl.BlockSpec((LANES, FOX_INP), lambda i: (i, 0)), pl.BlockSpec((1, LANES), lambda i: (0, 0)), row, row],
        out_specs=[aug] * 3, out_shape=[jax.ShapeDtypeStruct((lp, FOX_AUG), BF16)] * 3,
        scratch_shapes=[pltpu.VMEM((8, LANES), F32)],
        compiler_params=_params(("arbitrary",)),
    )(proj, jnp.pad(b_f, (0, LANES - FOX_H)).reshape(1, LANES), jnp.tile(q_gain, FOX_H).reshape(1, D),
      jnp.tile(k_gain, FOX_H).reshape(1, D))


def _fox_attn_fwd(qa, ka, va, proj, ag=None):
    lp = qa.shape[0]
    tq = _tile(lp, FOX_TQ)
    nq = lp // tq
    npair = FOX_H // 2

    def body(q_ref, k_ref, v_ref, gate_ref, *rest):
        if ag is None:
            o_ref, og_ref, lse_ref = rest
        else:
            _, o_ref, og_ref, lse_ref, w_out, send_sems, recv_sems = rest
            copies = _AgCopies(w_out, ag[1], send_sems, recv_sems)

            @pl.when((pl.program_id(0) == 0) & (pl.program_id(1) == 0))
            def _():
                for r, k in copies.pairs():
                    copies.ici(r, k).start()

        i = pl.program_id(1)
        causal = _iota((tq, tq), 1) <= _iota((tq, tq), 0)
        qs = [q_ref[:, hh * LANES:(hh + 1) * LANES] for hh in range(2)]

        def block(j, carry, diag):
            off = pl.multiple_of(j * tq, tq)
            out = []
            for hh in range(2):
                m, acc = carry[hh]
                k = k_ref[pl.ds(off, tq), hh * LANES:(hh + 1) * LANES]
                v = v_ref[pl.ds(off, tq), hh * LANES:(hh + 1) * LANES]
                s = nt(qs[hh], k)
                if diag:
                    s = jnp.where(causal, s, -1e30)
                m2 = jnp.maximum(m, jnp.max(s, axis=-1, keepdims=True))
                p = jnp.exp(s - m2)
                p_hi = p.astype(BF16)
                p_lo = (p - p_hi.astype(F32)).astype(BF16)
                out.append((m2, jnp.exp(m - m2) * acc + nn(p_hi, v) + nn(p_lo, v)))
            return tuple(out)

        init = tuple((jnp.full((tq, 1), -1e30, F32), jnp.zeros((tq, LANES), F32)) for _ in range(2))
        carry = lax.fori_loop(0, i // 2, lambda j, c: block(2 * j + 1, block(2 * j, c, False), False), init)
        carry = lax.cond(i % 2 == 1, lambda c: block(i - 1, c, False), lambda c: c, carry)
        carry = block(i, carry, True)
        outs, lses = [], []
        for hh in range(2):
            m, acc = carry[hh]
            l = acc[:, L_C:L_C + 1]
            outs.append(acc / l)
            lses.append(jnp.broadcast_to(m + jnp.log(l), (tq, LANES)))
        left = _iota((tq, LANES), 1) < FOX_DH
        o = jnp.where(left, outs[0], pltpu.roll(outs[1], FOX_DH, 1))
        o_ref[...] = o
        og_ref[...] = (o * _sigmoid(gate_ref[...])).astype(BF16)
        lse_ref[...] = jnp.where(left, lses[0], lses[1])

        if ag is not None:
            @pl.when((pl.program_id(0) == npair - 1) & (pl.program_id(1) == nq - 1))
            def _():
                for r, k in copies.pairs():
                    copies.ici_arrival(r, k).wait_recv()
                for r, k in copies.pairs():
                    copies.ici(r, k).wait_send()

    qspec = pl.BlockSpec((tq, 2 * LANES), lambda p, i: (i, p))
    kspec = pl.BlockSpec((lp, 2 * LANES), lambda p, i: (0, p))
    ospec = pl.BlockSpec((tq, LANES), lambda p, i: (i, p))
    ins, in_specs = [qa, ka, va, proj], [qspec, kspec, kspec, pl.BlockSpec((tq, LANES), lambda p, i: (i, 3 * D // LANES + p))]
    out_specs = [ospec] * 3
    out_shape = [jax.ShapeDtypeStruct((lp, D), F32), jax.ShapeDtypeStruct((lp, D), BF16), jax.ShapeDtypeStruct((lp, D), F32)]
    if ag is None:
        return pl.pallas_call(body, name="fox_attn_fwd", grid=(npair, nq), in_specs=in_specs, out_specs=out_specs,
                              out_shape=out_shape, compiler_params=_params(("parallel", "arbitrary")))(*ins)
    n = 3 * len(ag[1])
    return pl.pallas_call(
        body, name="fox_attn_fwd_ag", grid=(npair, nq), in_specs=in_specs + [ANY], out_specs=out_specs + [ANY],
        out_shape=out_shape + [jax.ShapeDtypeStruct(ag[0].shape, ag[0].dtype)],
        scratch_shapes=[pltpu.SemaphoreType.DMA((n,))] * 2, input_output_aliases={4: 3},
        compiler_params=_params(("arbitrary", "arbitrary")),
    )(*ins, ag[0])


def _fox_gate_bwd(dog, o, proj, lse, qa, swap=None):
    lp = o.shape[0]
    tr = LANES
    steps = lp // tr

    def body(d_ref, o_ref, g_ref, lse_ref, q_ref, *rest):
        if swap is None:
            do_ref, q2_ref, dgate_ref = rest
        else:
            src_ref, do_ref, q2_ref, dgate_ref, got_ref, send_sem, recv_sem = rest
            cp = _swap_copy(src_ref, got_ref, send_sem, recv_sem)

            @pl.when(pl.program_id(0) == 0)
            def _():
                cp.start()

            @pl.when(pl.program_id(0) == steps - 1)
            def _():
                cp.wait()

        down, _ = _head_sel(FOX_H, FOX_DH)
        sg = _sigmoid(g_ref[...])
        dv, ov = d_ref[...], o_ref[...]
        do = (dv * sg).astype(BF16).astype(F32)
        dgate_ref[...] = dv * ov * sg * (1.0 - sg)
        delta = _sel_r(do * ov, down)
        _spread(do, -_placed(delta, L_C), do_ref)
        r_, c_ = _iota((D, LANES), 0), _iota((D, LANES), 1)
        lse_c = _sel_r(lse_ref[...], (r_ == c_ * FOX_DH).astype(BF16))
        q2_ref[...] = (q_ref[...].astype(F32) - _placed(lse_c, L_LSE)).astype(BF16)

    spec = pl.BlockSpec((tr, D), lambda i: (i, 0))
    aug = pl.BlockSpec((tr, FOX_AUG), lambda i: (i, 0))
    in_specs = [spec, spec, pl.BlockSpec((tr, D), lambda i: (i, 3)), spec, aug]
    out_specs = [aug, aug, spec]
    out_shape = [jax.ShapeDtypeStruct((lp, FOX_AUG), BF16), jax.ShapeDtypeStruct((lp, FOX_AUG), BF16),
                 jax.ShapeDtypeStruct((lp, D), F32)]
    if swap is None:
        return pl.pallas_call(body, name="fox_gate_bwd", grid=(steps,), in_specs=in_specs, out_specs=out_specs,
                              out_shape=out_shape, compiler_params=_params(("parallel",)))(dog, o, proj, lse, qa)
    nb, r, c = swap.shape
    return pl.pallas_call(
        body, name="fox_gate_bwd_swap", grid=(steps,), in_specs=in_specs + [ANY], out_specs=out_specs + [ANY],
        out_shape=out_shape + [jax.ShapeDtypeStruct((nb, r // 2, c), swap.dtype)],
        scratch_shapes=[pltpu.SemaphoreType.DMA, pltpu.SemaphoreType.DMA], compiler_params=_params(("arbitrary",)),
    )(dog, o, proj, lse, qa, swap)


def _fox_attn_bwd(q2, ka, va, doa, rs=None):
    lp = q2.shape[0]
    t = _tile(lp, FOX_TQ)
    nb = lp // t
    npair = FOX_H // 2

    def body(q_ref, k_ref, v_ref, do_ref, *rest):
        if rs is None:
            dq_ref, dk_ref, dv_ref, dc_ref, dq_acc, dk_acc, dv_acc, dc_acc = rest
        else:
            s_ref, dq_ref, dk_ref, dv_ref, dc_ref, got_ref, dq_acc, dk_acc, dv_acc, dc_acc, send_sems, recv_sems = rest
            sends, arrivals = _scatter_copies(s_ref, got_ref, send_sems, recv_sems)

            @pl.when((pl.program_id(0) == 0) & (pl.program_id(1) == 0))
            def _():
                for cp in sends:
                    cp.start()

            @pl.when((pl.program_id(0) == npair - 1) & (pl.program_id(1) == nb - 1))
            def _():
                for cp in arrivals:
                    cp.wait_recv()
                for cp in sends:
                    cp.wait_send()

        j = pl.program_id(1)

        @pl.when(j == 0)
        def _():
            dq_acc[...] = jnp.zeros_like(dq_acc)

        causal = _iota((t, t), 1) <= _iota((t, t), 0)
        ks = [k_ref[:, hh * LANES:(hh + 1) * LANES] for hh in range(2)]
        vs = [v_ref[:, hh * LANES:(hh + 1) * LANES] for hh in range(2)]
        dk_acc[...] = jnp.zeros_like(dk_acc)
        dv_acc[...] = jnp.zeros_like(dv_acc)
        dc_acc[...] = jnp.zeros_like(dc_acc)

        def block(i, diag):
            off = pl.multiple_of(i * t, t)
            for hh in range(2):
                q = q_ref[pl.ds(off, t), hh * LANES:(hh + 1) * LANES]
                do = do_ref[pl.ds(off, t), hh * LANES:(hh + 1) * LANES]
                s = nt(q, ks[hh])
                if diag:
                    s = jnp.where(causal, s, -1e30)
                p = jnp.exp(s)
                ds = p * nt(do, vs[hh])
                dc_acc[hh] += jnp.sum(ds, axis=0, keepdims=True)
                dsb = ds.astype(BF16)
                dv_acc[hh] += tn(p.astype(BF16), do)
                dk_acc[hh] += tn(dsb, q)
                dq_acc[hh, pl.ds(off, t), :] += nn(dsb, ks[hh])

        block(j, True)
        below = nb - 1 - j

        def step(u, c):
            block(j + 1 + 2 * u, False)
            block(j + 2 + 2 * u, False)
            return c

        lax.fori_loop(0, below // 2, step, 0)

        @pl.when(below % 2 == 1)
        def _():
            block(nb - 1, False)
        left = _iota((t, LANES), 1) < FOX_DH
        dk_ref[...] = jnp.where(left, dk_acc[0], pltpu.roll(dk_acc[1], FOX_DH, 1))
        dv_ref[...] = jnp.where(left, dv_acc[0], pltpu.roll(dv_acc[1], FOX_DH, 1))
        for hh in range(2):
            dc_ref[hh] = jnp.broadcast_to(-dc_acc[hh], (8, t))

        @pl.when(j == nb - 1)
        def _():
            left = _iota((lp, LANES), 1) < FOX_DH
            dq_ref[...] = jnp.where(left, dq_acc[0], pltpu.roll(dq_acc[1], FOX_DH, 1))

    full = pl.BlockSpec((lp, 2 * LANES), lambda p, j: (0, p))
    kblk = pl.BlockSpec((t, 2 * LANES), lambda p, j: (j, p))
    oblk = pl.BlockSpec((t, LANES), lambda p, j: (j, p))
    in_specs = [full, kblk, kblk, full]
    out_specs = [pl.BlockSpec((lp, LANES), lambda p, j: (0, p)), oblk, oblk, pl.BlockSpec((2, 8, t), lambda p, j: (p, 0, j))]
    out_shape = [jax.ShapeDtypeStruct((lp, D), F32)] * 3 + [jax.ShapeDtypeStruct((FOX_H, 8, lp), F32)]
    scratch = [pltpu.VMEM((2, lp, LANES), F32), pltpu.VMEM((2, t, LANES), F32), pltpu.VMEM((2, t, LANES), F32),
               pltpu.VMEM((2, 1, t), F32)]
    if rs is None:
        return pl.pallas_call(body, name="fox_attn_bwd", grid=(npair, nb), in_specs=in_specs, out_specs=out_specs,
                              out_shape=out_shape, scratch_shapes=scratch,
                              compiler_params=_params(("parallel", "arbitrary")))(q2, ka, va, doa)
    return pl.pallas_call(
        body, name="fox_attn_bwd_rs", grid=(npair, nb), in_specs=in_specs + [ANY], out_specs=out_specs + [ANY],
        out_shape=out_shape + [jax.ShapeDtypeStruct((3,) + rs.shape[1:], rs.dtype)],
        scratch_shapes=scratch + [pltpu.SemaphoreType.DMA((3,)), pltpu.SemaphoreType.DMA((3,))],
        compiler_params=_params(("arbitrary", "arbitrary")),
    )(q2, ka, va, doa, rs)


def _fox_prep_bwd(proj, b_f, q_gain, k_gain, dqn, dkn, dv, dgate, dct):
    lp = proj.shape[0]
    nb = lp // LANES

    def body(p_ref, bf_ref, qg_ref, kg_ref, dq_ref, dk_ref, dv_ref, dg_ref, dc_ref,
             dp_ref, dqg_ref, dkg_ref, dbf_ref, carry):
        i = pl.program_id(0)

        @pl.when(i == 0)
        def _():
            carry[...] = jnp.zeros_like(carry)
            dqg_ref[...] = jnp.zeros_like(dqg_ref)
            dkg_ref[...] = jnp.zeros_like(dkg_ref)
            dbf_ref[...] = jnp.zeros_like(dbf_ref)

        down, up = _head_sel(FOX_H, FOX_DH)

        def norm_bwd(x, gain, dy, scale, dgain_ref):
            ms = _sel_r2(x * x, down) * (1.0 / FOX_DH)
            r = _sel_r2(lax.rsqrt(ms + EPS), up)
            u = dy * gain * scale
            mean_xu = _sel_r2(_sel_r2(x * u, down) * (1.0 / FOX_DH), up)
            dgain_ref[...] += jnp.sum(dy * scale * x * r, axis=0, keepdims=True)
            return r * u - x * (r * r * r) * mean_xu

        dp_ref[:, 0:D] = norm_bwd(p_ref[:, 0:D], qg_ref[...], dq_ref[...], FOX_DH ** -0.5, dqg_ref).astype(BF16)
        dp_ref[:, D:2 * D] = norm_bwd(p_ref[:, D:2 * D], kg_ref[...], dk_ref[...], 1.0, dkg_ref).astype(BF16)
        dp_ref[:, 2 * D:3 * D] = dv_ref[...].astype(BF16)
        dp_ref[:, 3 * D:4 * D] = dg_ref[...].astype(BF16)
        rows = jnp.concatenate([dc_ref[h, 0:1, :] for h in range(FOX_H)] + [jnp.zeros((LANES - FOX_H, LANES), F32)], axis=0)
        dlf = _sel_l(_tri(LANES, upper=True).astype(BF16), rows.T) + carry[0:1, :]
        carry[...] = jnp.broadcast_to(dlf[0:1, :], carry.shape)
        lane = _iota((LANES, LANES), 1)
        z = p_ref[:, 4 * D:4 * D + LANES] + bf_ref[...]
        df = jnp.where(lane < FOX_H, dlf * _sigmoid(-z), 0.0)
        dp_ref[:, 4 * D:4 * D + LANES] = df.astype(BF16)
        dbf_ref[...] += jnp.sum(df, axis=0, keepdims=True)

    rev = lambda i: (nb - 1 - i, 0)
    blk = pl.BlockSpec((LANES, D), rev)
    row = pl.BlockSpec((1, D), lambda i: (0, 0))
    row128 = pl.BlockSpec((1, LANES), lambda i: (0, 0))
    return pl.pallas_call(
        body, name="fox_prep_bwd", grid=(nb,),
        in_specs=[pl.BlockSpec((LANES, FOX_INP), rev), row128, row, row, blk, blk, blk, blk,
                  pl.BlockSpec((FOX_H, 8, LANES), lambda i: (0, 0, nb - 1 - i))],
        out_specs=[pl.BlockSpec((LANES, FOX_INP), rev), row, row, row128],
        out_shape=[jax.ShapeDtypeStruct((lp, FOX_INP), BF16), jax.ShapeDtypeStruct((1, D), F32),
                   jax.ShapeDtypeStruct((1, D), F32), jax.ShapeDtypeStruct((1, LANES), F32)],
        scratch_shapes=[pltpu.VMEM((8, LANES), F32)],
        compiler_params=_params(("arbitrary",)),
    )(proj, jnp.pad(b_f, (0, LANES - FOX_H)).reshape(1, LANES), jnp.tile(q_gain, FOX_H).reshape(1, D),
      jnp.tile(k_gain, FOX_H).reshape(1, D), dqn, dkn, dv, dgate, dct)


def _gla_gates(p_ref, wa_ref, ba_ref):
    a_lr = p_ref[:, 3072:3072 + LANES]
    z = nn(a_lr.astype(BF16), wa_ref[...].astype(BF16)) + ba_ref[...]
    g = _log_sigmoid(z) * (1.0 / GLA_NORM)
    b = _sel_l(_tri(CHUNK).astype(BF16), g)
    return a_lr, z, b


def _gla_chunk_fwd(q, k, v, b, st0):
    hs = range(len(q))
    low = _tri(CHUNK)
    bl = [b[h][CHUNK - 1:CHUNK, :] for h in hs]
    qe = [q[h] * jnp.exp(b[h]) for h in hs]
    ke = [k[h] * jnp.exp(-b[h]) for h in hs]
    kd = [k[h] * jnp.exp(bl[h] - b[h]) for h in hs]
    a = [jnp.where(low, nt(qe[h], ke[h]), 0.0) for h in hs]
    o = [nn(a[h], v[h]) + nt(qe[h], st0[h]) for h in hs]
    st1 = [st0[h] * jnp.exp(bl[h]) + tn(v[h], kd[h]) for h in hs]
    return o, st1, (qe, ke, kd, a, bl)


def _gla_slices(p_ref, b_all, h):
    q = p_ref[:, h * GLA_DK:(h + 1) * GLA_DK] * (GLA_DK ** -0.5)
    k = p_ref[:, GLA_QK + h * GLA_DK:GLA_QK + (h + 1) * GLA_DK]
    v = p_ref[:, 2 * GLA_QK + h * GLA_DV:2 * GLA_QK + (h + 1) * GLA_DV]
    r = p_ref[:, 2 * GLA_QK + GLA_V + h * GLA_DV:2 * GLA_QK + GLA_V + (h + 1) * GLA_DV]
    return q, k, v, r, b_all[:, h * GLA_DK:(h + 1) * GLA_DK]


def _gla_fwd(proj, w_alpha2, b_alpha, o_gain):
    lp = proj.shape[0]
    nc = lp // CHUNK

    def body(p_ref, wa_ref, ba_ref, og_ref, o_ref, y_ref, s_ref, st):
        @pl.when(pl.program_id(0) == 0)
        def _():
            st[...] = jnp.zeros_like(st)

        _, _, b_all = _gla_gates(p_ref, wa_ref, ba_ref)
        hs = range(GLA_H)
        parts = [_gla_slices(p_ref, b_all, h) for h in hs]
        st0 = [st[h] for h in hs]
        for h in hs:
            s_ref[0, h] = st0[h]
        o, st1, _ = _gla_chunk_fwd([p[0] for p in parts], [p[1] for p in parts], [p[2] for p in parts],
                                   [p[4] for p in parts], st0)
        for h in hs:
            st[h] = st1[h]
            o_ref[:, h * GLA_DV:(h + 1) * GLA_DV] = o[h]
            rs = lax.rsqrt(jnp.mean(o[h] * o[h], axis=-1, keepdims=True) + EPS)
            y_ref[:, h * GLA_DV:(h + 1) * GLA_DV] = (o[h] * rs * og_ref[...] * _silu(parts[h][3])).astype(BF16)

    blk = pl.BlockSpec((CHUNK, D), lambda i: (i, 0))
    return pl.pallas_call(
        body, name="gla_fwd", grid=(nc,),
        in_specs=[pl.BlockSpec((CHUNK, GLA_INP), lambda i: (i, 0)), pl.BlockSpec((LANES, GLA_QK), lambda i: (0, 0)),
                  pl.BlockSpec((1, GLA_QK), lambda i: (0, 0)), pl.BlockSpec((1, GLA_DV), lambda i: (0, 0))],
        out_specs=[blk, blk, pl.BlockSpec((1, GLA_H, GLA_DV, GLA_DK), lambda i: (i, 0, 0, 0))],
        out_shape=[jax.ShapeDtypeStruct((lp, D), F32), jax.ShapeDtypeStruct((lp, D), BF16),
                   jax.ShapeDtypeStruct((nc, GLA_H, GLA_DV, GLA_DK), F32)],
        scratch_shapes=[pltpu.VMEM((GLA_H, GLA_DV, GLA_DK), F32)],
        compiler_params=_params(("arbitrary",)),
    )(proj, jnp.pad(w_alpha2, ((0, LANES - GLA_RANK), (0, 0))), b_alpha.reshape(1, GLA_QK), o_gain.reshape(1, GLA_DV))


def _gla_bwd(proj, w_alpha2, b_alpha, o_gain, o, states, dy):
    lp = proj.shape[0]
    nc = lp // CHUNK

    def body(p_ref, wa_ref, ba_ref, og_ref, o_ref, s_ref, dy_ref, dp_ref, dwa_ref, dba_ref, dog_ref, dst):
        @pl.when(pl.program_id(0) == 0)
        def _():
            dst[...] = jnp.zeros_like(dst)
            dwa_ref[...] = jnp.zeros_like(dwa_ref)
            dba_ref[...] = jnp.zeros_like(dba_ref)
            dog_ref[...] = jnp.zeros_like(dog_ref)

        a_lr, z, b_all = _gla_gates(p_ref, wa_ref, ba_ref)
        last_row = _iota((CHUNK, GLA_DK), 0) == CHUNK - 1
        rev = _tri(CHUNK, upper=True).astype(BF16)
        hs = range(GLA_H)
        scale = GLA_DK ** -0.5
        parts = [_gla_slices(p_ref, b_all, h) for h in hs]
        q, k, v, b = [p[0] for p in parts], [p[1] for p in parts], [p[2] for p in parts], [p[4] for p in parts]
        st0 = [s_ref[0, h] for h in hs]
        dst1 = [dst[h] for h in hs]
        do = []
        for h in hs:
            r = parts[h][3]
            ov = o_ref[:, h * GLA_DV:(h + 1) * GLA_DV]
            dyv = dy_ref[:, h * GLA_DV:(h + 1) * GLA_DV]
            rs = lax.rsqrt(jnp.mean(ov * ov, axis=-1, keepdims=True) + EPS)
            on = ov * rs
            dp_ref[:, 2 * GLA_QK + GLA_V + h * GLA_DV:2 * GLA_QK + GLA_V + (h + 1) * GLA_DV] = (
                dyv * on * og_ref[...] * _dsilu(r)).astype(BF16)
            don = dyv * _silu(r)
            dog_ref[...] += jnp.sum(don * on, axis=0, keepdims=True)
            u = don * og_ref[...]
            do.append(rs * u - ov * (rs * rs * rs) * jnp.mean(ov * u, axis=-1, keepdims=True))
        _, _, (qe, ke, kd, a, bl) = _gla_chunk_fwd(q, k, v, b, st0)
        low = _tri(CHUNK)
        da = [jnp.where(low, nt(do[h], v[h]), 0.0) for h in hs]
        dkd = [nn(v[h], dst1[h]) for h in hs]
        dvv = [tn(a[h], do[h]) + nt(kd[h], dst1[h]) for h in hs]
        dqe = [nn(da[h], ke[h]) + nn(do[h], st0[h]) for h in hs]
        dke = [tn(da[h], qe[h]) for h in hs]
        dg_parts = []
        for h in hs:
            ebl = jnp.exp(bl[h])
            dst[h] = dst1[h] * ebl + tn(do[h], qe[h])
            db = dqe[h] * qe[h] - dke[h] * ke[h] - dkd[h] * kd[h]
            db_last = (jnp.sum(dkd[h] * kd[h], axis=0, keepdims=True)
                       + jnp.sum(dst1[h] * st0[h], axis=0, keepdims=True) * ebl)
            db = db + jnp.where(last_row, db_last, 0.0)
            dg_parts.append(_sel_l(rev, db))
            dp_ref[:, h * GLA_DK:(h + 1) * GLA_DK] = (dqe[h] * jnp.exp(b[h]) * scale).astype(BF16)
            dp_ref[:, GLA_QK + h * GLA_DK:GLA_QK + (h + 1) * GLA_DK] = (
                dke[h] * jnp.exp(-b[h]) + dkd[h] * jnp.exp(bl[h] - b[h])).astype(BF16)
            dp_ref[:, 2 * GLA_QK + h * GLA_DV:2 * GLA_QK + (h + 1) * GLA_DV] = dvv[h].astype(BF16)
        dg = jnp.concatenate(dg_parts, axis=1)
        dz = dg * (1.0 / GLA_NORM) * _sigmoid(-z)
        dzb = dz.astype(BF16)
        dp_ref[:, 3072:3072 + LANES] = nt(dzb, wa_ref[...].astype(BF16)).astype(BF16)
        dwa_ref[...] += tn(a_lr.astype(BF16), dzb)
        dba_ref[...] += jnp.sum(dz, axis=0, keepdims=True)

    rv = lambda i: (nc - 1 - i, 0)
    blk = pl.BlockSpec((CHUNK, D), rv)
    fixed = lambda r, c: pl.BlockSpec((r, c), lambda i: (0, 0))
    return pl.pallas_call(
        body, name="gla_bwd", grid=(nc,),
        in_specs=[pl.BlockSpec((CHUNK, GLA_INP), rv), fixed(LANES, GLA_QK), fixed(1, GLA_QK), fixed(1, GLA_DV), blk,
                  pl.BlockSpec((1, GLA_H, GLA_DV, GLA_DK), lambda i: (nc - 1 - i, 0, 0, 0)), blk],
        out_specs=[pl.BlockSpec((CHUNK, GLA_INP), rv), fixed(LANES, GLA_QK), fixed(1, GLA_QK), fixed(1, GLA_DV)],
        out_shape=[jax.ShapeDtypeStruct((lp, GLA_INP), BF16), jax.ShapeDtypeStruct((LANES, GLA_QK), F32),
                   jax.ShapeDtypeStruct((1, GLA_QK), F32), jax.ShapeDtypeStruct((1, GLA_DV), F32)],
        scratch_shapes=[pltpu.VMEM((GLA_H, GLA_DV, GLA_DK), F32)],
        compiler_params=_params(("arbitrary",)),
    )(proj, jnp.pad(w_alpha2, ((0, LANES - GLA_RANK), (0, 0))), b_alpha.reshape(1, GLA_QK), o_gain.reshape(1, GLA_DV),
      o, states, dy)


HI = lax.Precision.HIGH


def _gdn_pre(prev_ref, p_ref, cw_ref, al_ref, dt_ref):
    xc = jnp.concatenate([prev_ref[:, 0:GDN_CONV], p_ref[:, 0:GDN_CONV]], axis=0)
    shifted = [pltpu.roll(xc, 3 - j, 0)[CHUNK:, :] if j < 3 else xc[CHUNK:, :] for j in range(4)]
    conv = sum(shifted[j] * cw_ref[j:j + 1, :] for j in range(4))
    act = _silu(conv)
    slab = p_ref[:, 4096:4096 + LANES]
    lane = _iota((CHUNK, LANES), 1)
    zs = slab + dt_ref[...]
    g = jnp.where(lane < GDN_H, -jnp.exp(al_ref[...]) * _softplus(zs), 0.0)
    bs = _sel_l(_tri(CHUNK).astype(BF16), g)
    beta = _sigmoid(slab)
    return shifted, conv, act, slab, zs, g, bs, beta


def _l2n(x):
    r = lax.rsqrt(jnp.sum(x * x, axis=-1, keepdims=True) + EPS)
    return x * r, r


def _gdn_chunk_fwd(q, k, v, beta, bcol, brow, s0):
    hs = range(len(q))
    ii, jj = _iota((CHUNK, CHUNK), 0), _iota((CHUNK, CHUNK), 1)
    low, eye = ii >= jj, (ii == jj).astype(F32)
    dm = [jnp.where(low, jnp.exp(jnp.where(low, bcol[h] - brow[h], 0.0)), 0.0) for h in hs]
    dstrict = [jnp.where(ii > jj, dm[h], 0.0) for h in hs]
    eb = [jnp.exp(bcol[h]) for h in hs]
    bl = [bcol[h][CHUNK - 1:CHUNK, :] for h in hs]
    kb = [k[h] * beta[h] for h in hs]
    vb = [v[h] * beta[h] for h in hs]
    nmat = [nt(kb[h], k[h]) * dstrict[h] for h in hs]
    x = [eye - nmat[h] for h in hs]
    pw = [nn(nmat[h], nmat[h], precision=HI) for h in hs]
    for it in range(5):
        x = [x[h] + nn(x[h], pw[h], precision=HI) for h in hs]
        if it < 4:
            pw = [nn(pw[h], pw[h], precision=HI) for h in hs]
    kbe = [kb[h] * eb[h] for h in hs]
    u = [nn(x[h], vb[h], precision=HI) for h in hs]
    w = [nn(x[h], kbe[h], precision=HI) for h in hs]
    vn = [u[h] - nn(w[h], s0[h]) for h in hs]
    pm = [nt(q[h], k[h]) * dm[h] for h in hs]
    qe = [q[h] * eb[h] for h in hs]
    o = [nn(pm[h], vn[h]) + nn(qe[h], s0[h]) for h in hs]
    kd = [k[h] * jnp.exp(bl[h] - bcol[h]) for h in hs]
    s1 = [s0[h] * jnp.exp(bl[h]) + tn(kd[h], vn[h]) for h in hs]
    return o, s1, dict(dm=dm, dstrict=dstrict, eb=eb, bl=bl, kb=kb, vb=vb, nmat=nmat, tinv=x, kbe=kbe, u=u, w=w, vn=vn,
                       pm=pm, qe=qe, kd=kd)


def _gdn_heads(act, beta_slab, bs, h):
    qa = act[:, h * GDN_DK:(h + 1) * GDN_DK]
    ka = act[:, GDN_H * GDN_DK + h * GDN_DK:GDN_H * GDN_DK + (h + 1) * GDN_DK]
    v = act[:, 2 * GDN_H * GDN_DK + h * GDN_DV:2 * GDN_H * GDN_DK + (h + 1) * GDN_DV]
    return qa, ka, v, beta_slab[:, GDN_H + h:GDN_H + h + 1], bs[:, h:h + 1]


def _gdn_fwd(proj, conv_w, a_log, dt_bias, o_gain):
    lp = proj.shape[0]
    nc = lp // CHUNK

    def body(prev_ref, p_ref, cw_ref, al_ref, dt_ref, og_ref, o_ref, y_ref, s_ref, st):
        @pl.when(pl.program_id(0) == 0)
        def _():
            st[...] = jnp.zeros_like(st)

        _, _, act, _, _, _, bs, beta = _gdn_pre(prev_ref, p_ref, cw_ref, al_ref, dt_ref)
        bst = bs.T
        hs = range(GDN_H)
        parts = [_gdn_heads(act, beta, bs, h) for h in hs]
        q = [_l2n(parts[h][0])[0] * (GDN_DK ** -0.5) for h in hs]
        k = [_l2n(parts[h][1])[0] for h in hs]
        s0 = [st[h] for h in hs]
        for h in hs:
            s_ref[0, h] = s0[h]
        o, s1, _ = _gdn_chunk_fwd(q, k, [parts[h][2] for h in hs], [parts[h][3] for h in hs], [parts[h][4] for h in hs],
                                  [bst[h:h + 1, :] for h in hs], s0)
        for h in hs:
            st[h] = s1[h]
            o_ref[:, h * GDN_DV:(h + 1) * GDN_DV] = o[h]
            rs = lax.rsqrt(jnp.mean(o[h] * o[h], axis=-1, keepdims=True) + EPS)
            gate = p_ref[:, GDN_CONV + h * GDN_DV:GDN_CONV + (h + 1) * GDN_DV]
            y_ref[:, h * GDN_DV:(h + 1) * GDN_DV] = (o[h] * rs * og_ref[...] * _silu(gate)).astype(BF16)

    blk = pl.BlockSpec((CHUNK, D), lambda i: (i, 0))
    fixed = lambda r, c: pl.BlockSpec((r, c), lambda i: (0, 0))
    return pl.pallas_call(
        body, name="gdn_fwd", grid=(nc,),
        in_specs=[pl.BlockSpec((CHUNK, GDN_INP), lambda i: (jnp.maximum(i - 1, 0), 0)),
                  pl.BlockSpec((CHUNK, GDN_INP), lambda i: (i, 0)), fixed(8, GDN_CONV), fixed(1, LANES), fixed(1, LANES),
                  fixed(1, GDN_DV)],
        out_specs=[blk, blk, pl.BlockSpec((1, GDN_H, GDN_DK, GDN_DV), lambda i: (i, 0, 0, 0))],
        out_shape=[jax.ShapeDtypeStruct((lp, D), F32), jax.ShapeDtypeStruct((lp, D), BF16),
                   jax.ShapeDtypeStruct((nc, GDN_H, GDN_DK, GDN_DV), F32)],
        scratch_shapes=[pltpu.VMEM((GDN_H, GDN_DK, GDN_DV), F32)],
        compiler_params=_params(("arbitrary",)),
    )(proj, proj, jnp.pad(conv_w.reshape(4, GDN_CONV), ((0, 4), (0, 0))), jnp.pad(a_log, (0, LANES - GDN_H)).reshape(1, LANES),
      jnp.pad(dt_bias, (0, LANES - GDN_H)).reshape(1, LANES), o_gain.reshape(1, GDN_DV))


def _gdn_bwd(proj, conv_w, a_log, dt_bias, o_gain, o, states, dy):
    lp = proj.shape[0]
    nc = lp // CHUNK

    def body(prev_ref, p_ref, cw_ref, al_ref, dt_ref, og_ref, o_ref, s_ref, dy_ref,
             dp_ref, dcw_ref, dal_ref, ddt_ref, dog_ref, dst, dconv_next):
        @pl.when(pl.program_id(0) == 0)
        def _():
            dst[...] = jnp.zeros_like(dst)
            dconv_next[...] = jnp.zeros_like(dconv_next)
            dcw_ref[...] = jnp.zeros_like(dcw_ref)
            dal_ref[...] = jnp.zeros_like(dal_ref)
            ddt_ref[...] = jnp.zeros_like(ddt_ref)
            dog_ref[...] = jnp.zeros_like(dog_ref)

        shifted, conv, act, slab, zs, g, bs, beta = _gdn_pre(prev_ref, p_ref, cw_ref, al_ref, dt_ref)
        bst = bs.T
        lane = _iota((CHUNK, LANES), 1)
        ones = jnp.ones((CHUNK, LANES), F32)
        db_slab = jnp.zeros((CHUNK, LANES), F32)
        dbeta_slab = jnp.zeros((CHUNK, LANES), F32)
        last_row = _iota((CHUNK, 1), 0) == CHUNK - 1
        hs = range(GDN_H)
        scale = GDN_DK ** -0.5
        parts = [_gdn_heads(act, beta, bs, h) for h in hs]
        qa, ka, v = [parts[h][0] for h in hs], [parts[h][1] for h in hs], [parts[h][2] for h in hs]
        bet, bcol = [parts[h][3] for h in hs], [parts[h][4] for h in hs]
        qn_ = [_l2n(qa[h]) for h in hs]
        kn_ = [_l2n(ka[h]) for h in hs]
        q = [qn_[h][0] * scale for h in hs]
        k, rq, rk = [kn_[h][0] for h in hs], [qn_[h][1] for h in hs], [kn_[h][1] for h in hs]
        s0 = [s_ref[0, h] for h in hs]
        ds1 = [dst[h] for h in hs]
        do = []
        for h in hs:
            ov = o_ref[:, h * GDN_DV:(h + 1) * GDN_DV]
            dyv = dy_ref[:, h * GDN_DV:(h + 1) * GDN_DV]
            gate = p_ref[:, GDN_CONV + h * GDN_DV:GDN_CONV + (h + 1) * GDN_DV]
            rs = lax.rsqrt(jnp.mean(ov * ov, axis=-1, keepdims=True) + EPS)
            on = ov * rs
            dp_ref[:, GDN_CONV + h * GDN_DV:GDN_CONV + (h + 1) * GDN_DV] = (dyv * on * og_ref[...] * _dsilu(gate)).astype(BF16)
            don = dyv * _silu(gate)
            dog_ref[...] += jnp.sum(don * on, axis=0, keepdims=True)
            uu = don * og_ref[...]
            do.append(rs * uu - ov * (rs * rs * rs) * jnp.mean(ov * uu, axis=-1, keepdims=True))
        _, _, f = _gdn_chunk_fwd(q, k, v, bet, bcol, [bst[h:h + 1, :] for h in hs], s0)
        dm, dstrict, eb, bl, kb, nmat, tinv = f["dm"], f["dstrict"], f["eb"], f["bl"], f["kb"], f["nmat"], f["tinv"]
        kbe, u, w, vn, pm, qe, kd = f["kbe"], f["u"], f["w"], f["vn"], f["pm"], f["qe"], f["kd"]
        ebl = [jnp.exp(bl[h]) for h in hs]
        dvn = [tn(pm[h], do[h]) + nn(kd[h], ds1[h]) for h in hs]
        dpr = [nt(do[h], vn[h]) for h in hs]
        dqe = [nt(do[h], s0[h]) for h in hs]
        dkd = [nt(vn[h], ds1[h]) for h in hs]
        for h in hs:
            dst[h] = ds1[h] * ebl[h] + tn(qe[h], do[h]) - tn(w[h], dvn[h])
        du_ = [tn(tinv[h], dvn[h], precision=HI) for h in hs]
        dw_ = [tn(tinv[h], -nt(dvn[h], s0[h]), precision=HI) for h in hs]
        dn = [-(nt(du_[h], u[h]) + nt(dw_[h], w[h])) for h in hs]
        dqk = [dpr[h] * dm[h] for h in hs]
        dkk = [dn[h] * dstrict[h] for h in hs]
        gsum = [dpr[h] * pm[h] + dn[h] * nmat[h] for h in hs]
        dkb = [nn(dkk[h], k[h]) + dw_[h] * eb[h] for h in hs]
        dk = [tn(dkk[h], kb[h]) + tn(dqk[h], q[h]) + dkd[h] * jnp.exp(bl[h] - bcol[h]) + dkb[h] * bet[h] for h in hs]
        dq = [nn(dqk[h], k[h]) + dqe[h] * eb[h] for h in hs]
        colsum = [tn(gsum[h], ones, precision=HI)[:, 0:1] for h in hs]
        dact_q, dact_k, dact_v = [], [], []
        for h in hs:
            dbeta = jnp.sum(dkb[h] * k[h], axis=-1, keepdims=True) + jnp.sum(du_[h] * v[h], axis=-1, keepdims=True)
            skd = jnp.sum(dkd[h] * kd[h], axis=-1, keepdims=True)
            db = (jnp.sum(gsum[h], axis=-1, keepdims=True) - colsum[h] + jnp.sum(dqe[h] * qe[h], axis=-1, keepdims=True)
                  + jnp.sum(dw_[h] * kbe[h], axis=-1, keepdims=True) - skd)
            db_last = jnp.sum(skd, axis=0, keepdims=True) + jnp.sum(ds1[h] * s0[h]) * ebl[h]
            db = db + jnp.where(last_row, db_last, 0.0)
            db_slab = db_slab + jnp.where(lane == h, db, 0.0)
            dbeta_slab = dbeta_slab + jnp.where(lane == GDN_H + h, dbeta, 0.0)
            dqn = dq[h] * scale
            dact_q.append(rq[h] * dqn - qa[h] * (rq[h] * rq[h] * rq[h]) * jnp.sum(qa[h] * dqn, axis=-1, keepdims=True))
            dact_k.append(rk[h] * dk[h] - ka[h] * (rk[h] * rk[h] * rk[h]) * jnp.sum(ka[h] * dk[h], axis=-1, keepdims=True))
            dact_v.append(du_[h] * bet[h])
        dact = jnp.concatenate(dact_q + dact_k + dact_v, axis=1)
        dconv = dact * _dsilu(conv)
        for j in range(4):
            dcw_ref[j:j + 1, :] += jnp.sum(dconv * shifted[j], axis=0, keepdims=True)
        dcat = jnp.concatenate([dconv, dconv_next[...]], axis=0)
        dx = dconv * cw_ref[3:4, :]
        for j in range(3):
            dx = dx + pltpu.roll(dcat, 2 * CHUNK - (3 - j), 0)[:CHUNK, :] * cw_ref[j:j + 1, :]
        dconv_next[...] = dconv
        dp_ref[:, 0:GDN_CONV] = dx.astype(BF16)
        dg = _sel_l(_tri(CHUNK, upper=True).astype(BF16), db_slab)
        da = dg * (-jnp.exp(al_ref[...])) * _sigmoid(zs)
        da = jnp.where(lane < GDN_H, da, 0.0)
        dal_ref[...] += jnp.sum(dg * g, axis=0, keepdims=True)
        ddt_ref[...] += jnp.sum(da, axis=0, keepdims=True)
        dp_ref[:, 4096:4096 + LANES] = (da + dbeta_slab * beta * (1.0 - beta)).astype(BF16)

    rv = lambda i: (nc - 1 - i, 0)
    blk = pl.BlockSpec((CHUNK, D), rv)
    fixed = lambda r, c: pl.BlockSpec((r, c), lambda i: (0, 0))
    return pl.pallas_call(
        body, name="gdn_bwd", grid=(nc,),
        in_specs=[pl.BlockSpec((CHUNK, GDN_INP), lambda i: (jnp.maximum(nc - 2 - i, 0), 0)),
                  pl.BlockSpec((CHUNK, GDN_INP), rv), fixed(8, GDN_CONV), fixed(1, LANES), fixed(1, LANES), fixed(1, GDN_DV),
                  blk, pl.BlockSpec((1, GDN_H, GDN_DK, GDN_DV), lambda i: (nc - 1 - i, 0, 0, 0)), blk],
        out_specs=[pl.BlockSpec((CHUNK, GDN_INP), rv), fixed(8, GDN_CONV), fixed(1, LANES), fixed(1, LANES), fixed(1, GDN_DV)],
        out_shape=[jax.ShapeDtypeStruct((lp, GDN_INP), BF16), jax.ShapeDtypeStruct((8, GDN_CONV), F32),
                   jax.ShapeDtypeStruct((1, LANES), F32), jax.ShapeDtypeStruct((1, LANES), F32),
                   jax.ShapeDtypeStruct((1, GDN_DV), F32)],
        scratch_shapes=[pltpu.VMEM((GDN_H, GDN_DK, GDN_DV), F32), pltpu.VMEM((CHUNK, GDN_CONV), F32)],
        compiler_params=_params(("arbitrary",)),
    )(proj, proj, jnp.pad(conv_w.reshape(4, GDN_CONV), ((0, 4), (0, 0))), jnp.pad(a_log, (0, LANES - GDN_H)).reshape(1, LANES),
      jnp.pad(dt_bias, (0, LANES - GDN_H)).reshape(1, LANES), o_gain.reshape(1, GDN_DV), o, states, dy)


def _coords():
    return lax.axis_index("x"), lax.axis_index("y"), lax.axis_index("c")


def _other_chips(x, y):
    return [(1 - x, y, 2 * (1 - x) + y), (x, 1 - y, 2 * x + 1 - y), (1 - x, 1 - y, 2 * (1 - x) + 1 - y)]


def _gather8(v, *, reduce, name):
    r, c = v.shape

    def body(v_ref, out_ref, *scratch):
        if reduce:
            buf, send_sems, recv_sems = scratch
        else:
            buf = out_ref
            send_sems, recv_sems = scratch
        x, y, cc = _coords()
        me = 4 * x + 2 * y + cc
        buf[me] = v_ref[...]
        copies = []
        for k in range(1, 8):
            px, py, pc = x ^ (k >> 2), y ^ ((k >> 1) & 1), cc ^ (k & 1)
            copies.append(pltpu.make_async_remote_copy(
                src_ref=v_ref, dst_ref=buf.at[me], send_sem=send_sems.at[k - 1], recv_sem=recv_sems.at[k - 1],
                device_id=(px, py, pc), device_id_type=MESH))
        for cp in copies:
            cp.start()
        for k in range(1, 8):
            peer = (x ^ (k >> 2)) * 4 + (y ^ ((k >> 1) & 1)) * 2 + (cc ^ (k & 1))
            pltpu.make_async_remote_copy(
                src_ref=v_ref, dst_ref=buf.at[peer], send_sem=send_sems.at[k - 1], recv_sem=recv_sems.at[k - 1],
                device_id=(x, y, cc), device_id_type=MESH).wait_recv()
        for cp in copies:
            cp.wait_send()
        if reduce:
            acc = buf[0]
            for d in range(1, 8):
                acc = acc + buf[d]
            out_ref[...] = acc

    scratch = [pltpu.SemaphoreType.DMA((7,)), pltpu.SemaphoreType.DMA((7,))]
    if reduce:
        scratch = [pltpu.VMEM((8, r, c), F32)] + scratch
    return pl.pallas_call(
        body, name=name, in_specs=[VM], out_specs=VM,
        out_shape=jax.ShapeDtypeStruct((r, c) if reduce else (8, r, c), F32),
        scratch_shapes=scratch, compiler_params=_params(),
    )(v)


class _AgCopies:
    def __init__(self, buf, ranges, send_sems, recv_sems):
        self.buf, self.ranges, self.send_sems, self.recv_sems = buf, ranges, send_sems, recv_sems
        self.x, self.y, self.cc = _coords()
        self.p = 2 * self.x + self.y
        self.chips = _other_chips(self.x, self.y)

    def rows(self, chip, r, hf):
        start, n = self.ranges[r]
        return self.buf.at[chip, pl.ds(start + hf * (n // 2), n // 2), :]

    def _copy(self, r, k, chip, hf, to):
        return pltpu.make_async_remote_copy(
            src_ref=self.rows(chip, r, hf), dst_ref=self.rows(chip, r, hf), send_sem=self.send_sems.at[3 * r + k],
            recv_sem=self.recv_sems.at[3 * r + k], device_id=to, device_id_type=MESH)

    def pairs(self):
        return [(r, k) for r in range(len(self.ranges)) for k in range(3)]

    def ici(self, r, k):
        cx, cy, _ = self.chips[k]
        return self._copy(r, k, self.p, self.cc, (cx, cy, self.cc))

    def ici_arrival(self, r, k):
        return self._copy(r, k, self.chips[k][2], self.cc, (self.x, self.y, self.cc))

    def forward(self, r, k):
        return self._copy(r, k, self.chips[k][2], self.cc, (self.x, self.y, 1 - self.cc))

    def forward_arrival(self, r, k):
        return self._copy(r, k, self.chips[k][2], 1 - self.cc, (self.x, self.y, self.cc))


def _ag_weights(w4, ranges):
    n = 3 * len(ranges)

    def body(w_ref, out_ref, send1, recv1, send2, recv2):
        ici, fwd = _AgCopies(out_ref, ranges, send1, recv1), _AgCopies(out_ref, ranges, send2, recv2)
        for r, k in ici.pairs():
            ici.ici(r, k).start()
        for r, k in ici.pairs():
            ici.ici_arrival(r, k).wait_recv()
            fwd.forward(r, k).start()
        for r, k in ici.pairs():
            fwd.forward_arrival(r, k).wait_recv()
        for r, k in ici.pairs():
            ici.ici(r, k).wait_send()
            fwd.forward(r, k).wait_send()

    return pl.pallas_call(
        body, name="ag_weights", in_specs=[ANY], out_specs=ANY, out_shape=jax.ShapeDtypeStruct(w4.shape, w4.dtype),
        scratch_shapes=[pltpu.SemaphoreType.DMA((n,))] * 4, input_output_aliases={0: 0}, compiler_params=_params(),
    )(w4)


def _ag_forward(w4, ranges):
    n = 3 * len(ranges)

    def body(w_ref, out_ref, send2, recv2):
        fwd = _AgCopies(out_ref, ranges, send2, recv2)
        for r, k in fwd.pairs():
            fwd.forward(r, k).start()
        for r, k in fwd.pairs():
            fwd.forward_arrival(r, k).wait_recv()
        for r, k in fwd.pairs():
            fwd.forward(r, k).wait_send()

    return pl.pallas_call(
        body, name="ag_forward", in_specs=[ANY], out_specs=ANY, out_shape=jax.ShapeDtypeStruct(w4.shape, w4.dtype),
        scratch_shapes=[pltpu.SemaphoreType.DMA((n,))] * 2, input_output_aliases={0: 0}, compiler_params=_params(),
    )(w4)


def _swap_copy(g_ref, out_ref, send_sem, recv_sem):
    x, y, cc = _coords()
    half = g_ref.shape[1] // 2
    return pltpu.make_async_remote_copy(
        src_ref=g_ref.at[:, pl.ds((1 - cc) * half, half), :], dst_ref=out_ref, send_sem=send_sem, recv_sem=recv_sem,
        device_id=(x, y, 1 - cc), device_id_type=MESH)


def _swap_halves(g, *, name):
    nb, r, c = g.shape
    half = r // 2

    def body(g_ref, out_ref, send_sem, recv_sem):
        cp = _swap_copy(g_ref, out_ref, send_sem, recv_sem)
        cp.start()
        cp.wait()

    return pl.pallas_call(
        body, name=name, in_specs=[ANY], out_specs=ANY, out_shape=jax.ShapeDtypeStruct((nb, half, c), g.dtype),
        scratch_shapes=[pltpu.SemaphoreType.DMA, pltpu.SemaphoreType.DMA], compiler_params=_params(),
    )(g)


def _my_half_index():
    return lax.axis_index("c").astype(jnp.int32).reshape(1)


def _add_halves(g, got, tag):
    nb, r, c = g.shape
    half = r // 2
    tr = _tile(half, 512, 16)
    nt_ = half // tr

    def body(c_ref, a_ref, b_ref, o_ref):
        o_ref[...] = (a_ref[...].astype(F32) + b_ref[...].astype(F32)).astype(BF16)

    return pl.pallas_call(
        body, name=f"rs_add_sibling{tag}",
        grid_spec=pltpu.PrefetchScalarGridSpec(
            num_scalar_prefetch=1, grid=(nb, nt_),
            in_specs=[pl.BlockSpec((1, tr, c), lambda b, i, cr: (b, cr[0] * nt_ + i, 0)),
                      pl.BlockSpec((1, tr, c), lambda b, i, cr: (b, i, 0))],
            out_specs=pl.BlockSpec((1, tr, c), lambda b, i, cr: (b, i, 0))),
        out_shape=jax.ShapeDtypeStruct((nb, half, c), BF16), compiler_params=_params(("parallel", "parallel")),
    )(_my_half_index(), g, got)


def _scatter_copies(s_ref, out_ref, send_sems, recv_sems):
    x, y, cc = _coords()
    sends = [pltpu.make_async_remote_copy(
        src_ref=s_ref.at[blk], dst_ref=out_ref.at[k], send_sem=send_sems.at[k], recv_sem=recv_sems.at[k],
        device_id=(cx, cy, cc), device_id_type=MESH) for k, (cx, cy, blk) in enumerate(_other_chips(x, y))]
    arrivals = [pltpu.make_async_remote_copy(
        src_ref=s_ref.at[2 * x + y], dst_ref=out_ref.at[k], send_sem=send_sems.at[k], recv_sem=recv_sems.at[k],
        device_id=(x, y, cc), device_id_type=MESH) for k in range(3)]
    return sends, arrivals


def _scatter_chips(s, tag):
    nb, hrows, c = s.shape

    def body(s_ref, out_ref, send_sems, recv_sems):
        sends, arrivals = _scatter_copies(s_ref, out_ref, send_sems, recv_sems)
        for cp in sends:
            cp.start()
        for cp in arrivals:
            cp.wait_recv()
        for cp in sends:
            cp.wait_send()

    return pl.pallas_call(
        body, name=f"rs_scatter{tag}", in_specs=[ANY], out_specs=ANY, out_shape=jax.ShapeDtypeStruct((3, hrows, c), s.dtype),
        scratch_shapes=[pltpu.SemaphoreType.DMA((3,)), pltpu.SemaphoreType.DMA((3,))], compiler_params=_params(),
    )(s)


def _sum_chips(s, got, tag):
    nb, hrows, c = s.shape
    tr = _tile(hrows, 512, 16)

    def body(idx_ref, own_ref, got_ref, o_ref):
        p = idx_ref[0]
        own = own_ref[0].astype(F32)
        parts = [got_ref[k].astype(F32) for k in range(3)]
        acc = jnp.zeros_like(own)
        for q in range(4):
            val = own
            for k, rel in enumerate((2, 1, 3)):
                val = jnp.where((p ^ rel) == q, parts[k], val)
            acc = acc + val
        o_ref[...] = acc

    idx = (2 * lax.axis_index("x") + lax.axis_index("y")).astype(jnp.int32).reshape(1)
    return pl.pallas_call(
        body, name=f"rs_sum_chips{tag}",
        grid_spec=pltpu.PrefetchScalarGridSpec(
            num_scalar_prefetch=1, grid=(hrows // tr,),
            in_specs=[pl.BlockSpec((1, tr, c), lambda i, pr: (pr[0], i, 0)), pl.BlockSpec((3, tr, c), lambda i, pr: (0, i, 0))],
            out_specs=pl.BlockSpec((tr, c), lambda i, pr: (i, 0))),
        out_shape=jax.ShapeDtypeStruct((hrows, c), F32), compiler_params=_params(("parallel",)),
    )(idx, s, got)


def _swap_sibling(t, tag):
    def body(t_ref, out_ref, send_sem, recv_sem):
        x, y, cc = _coords()
        cp = pltpu.make_async_remote_copy(src_ref=t_ref, dst_ref=out_ref, send_sem=send_sem, recv_sem=recv_sem,
                                          device_id=(x, y, 1 - cc), device_id_type=MESH)
        cp.start()
        cp.wait()

    return pl.pallas_call(
        body, name=f"rs_join{tag}", in_specs=[ANY], out_specs=ANY, out_shape=jax.ShapeDtypeStruct(t.shape, t.dtype),
        scratch_shapes=[pltpu.SemaphoreType.DMA, pltpu.SemaphoreType.DMA], compiler_params=_params(),
    )(t)


def _rs_local(g, tag):
    return _add_halves(g, _swap_halves(g, name=f"rs_swap{tag}"), tag)


def _rs_finish(s, recv, tag):
    t = _sum_chips(s, recv, tag)
    r = _swap_sibling(t, tag)
    first = lax.axis_index("c") == 0
    return jnp.concatenate([jnp.where(first, t, r), jnp.where(first, r, t)], axis=0)


_SMALL_SHARDED = (("meta_tokens", 1), ("gla_w_alpha2", 2), ("gdn_conv_w", 3))
_REPLICATED = ("norm_mix", "norm_ffn", "fox_b_f", "fox_q_gain", "fox_k_gain", "gla_b_alpha", "gla_o_gain",
               "gdn_a_log", "gdn_dt_bias", "gdn_o_gain")
_WEIGHTS = ("meta_tokens", "norm_mix", "norm_ffn", "w_gate_up", "w_down", "fox_w_in", "fox_b_f", "fox_q_gain",
            "fox_k_gain", "fox_w_out", "gla_w_in", "gla_w_alpha2", "gla_b_alpha", "gla_o_gain", "gla_w_out",
            "gdn_w_in", "gdn_conv_w", "gdn_a_log", "gdn_dt_bias", "gdn_o_gain", "gdn_w_out")
_PACK_ROWS = 512
_IN_W = ("fox_w_in", "gla_w_in", "gdn_w_in")
_OUT_W = ("fox_w_out", "gla_w_out", "gdn_w_out")


def _piece_rows(n):
    return -(-n // 32) * 32


def _pack(arrays, width, row_mult, dtype):
    flat = jnp.concatenate([a.astype(dtype).reshape(-1) for a in arrays])
    per = width * row_mult
    n = -(-flat.shape[0] // per) * per
    return jnp.pad(flat, (0, n - flat.shape[0])).reshape(n // width, width)


def _unpack(flat, shapes):
    out, off = [], 0
    for s in shapes:
        n = 1
        for d in s:
            n *= d
        out.append(flat[off:off + n].reshape(s))
        off += n
    return out


def _unpack_cols(flat2, shapes):
    out, off = [], 0
    for s in shapes:
        n = 1
        for d in s:
            n *= d
        out.append(flat2[:, off:off + n].reshape((flat2.shape[0],) + tuple(s)))
        off += n
    return out


def kernel(x, meta_tokens, norm_mix, norm_ffn, w_gate_up, w_down, fox_w_in, fox_b_f, fox_q_gain, fox_k_gain, fox_w_out, gla_w_in, gla_w_alpha2, gla_b_alpha, gla_o_gain, gla_w_out, gdn_w_in, gdn_conv_w, gdn_a_log, gdn_dt_bias, gdn_o_gain, gdn_w_out, loss_target, m_meta_tokens, m_norm_mix, m_norm_ffn, m_w_gate_up, m_w_down, m_fox_w_in, m_fox_b_f, m_fox_q_gain, m_fox_k_gain, m_fox_w_out, m_gla_w_in, m_gla_w_alpha2, m_gla_b_alpha, m_gla_o_gain, m_gla_w_out, m_gdn_w_in, m_gdn_conv_w, m_gdn_a_log, m_gdn_dt_bias, m_gdn_o_gain, m_gdn_w_out, v_meta_tokens, v_norm_mix, v_norm_ffn, v_w_gate_up, v_w_down, v_fox_w_in, v_fox_b_f, v_fox_q_gain, v_fox_k_gain, v_fox_w_out, v_gla_w_in, v_gla_w_alpha2, v_gla_b_alpha, v_gla_o_gain, v_gla_w_out, v_gdn_w_in, v_gdn_conv_w, v_gdn_a_log, v_gdn_dt_bias, v_gdn_o_gain, v_gdn_w_out):
    W = dict(meta_tokens=meta_tokens, norm_mix=norm_mix, norm_ffn=norm_ffn, w_gate_up=w_gate_up, w_down=w_down,
             fox_w_in=fox_w_in, fox_b_f=fox_b_f, fox_q_gain=fox_q_gain, fox_k_gain=fox_k_gain, fox_w_out=fox_w_out,
             gla_w_in=gla_w_in, gla_w_alpha2=gla_w_alpha2, gla_b_alpha=gla_b_alpha, gla_o_gain=gla_o_gain,
             gla_w_out=gla_w_out, gdn_w_in=gdn_w_in, gdn_conv_w=gdn_conv_w, gdn_a_log=gdn_a_log,
             gdn_dt_bias=gdn_dt_bias, gdn_o_gain=gdn_o_gain, gdn_w_out=gdn_w_out)
    M = dict(meta_tokens=m_meta_tokens, norm_mix=m_norm_mix, norm_ffn=m_norm_ffn, w_gate_up=m_w_gate_up, w_down=m_w_down,
             fox_w_in=m_fox_w_in, fox_b_f=m_fox_b_f, fox_q_gain=m_fox_q_gain, fox_k_gain=m_fox_k_gain,
             fox_w_out=m_fox_w_out, gla_w_in=m_gla_w_in, gla_w_alpha2=m_gla_w_alpha2, gla_b_alpha=m_gla_b_alpha,
             gla_o_gain=m_gla_o_gain, gla_w_out=m_gla_w_out, gdn_w_in=m_gdn_w_in, gdn_conv_w=m_gdn_conv_w,
             gdn_a_log=m_gdn_a_log, gdn_dt_bias=m_gdn_dt_bias, gdn_o_gain=m_gdn_o_gain, gdn_w_out=m_gdn_w_out)
    V = dict(meta_tokens=v_meta_tokens, norm_mix=v_norm_mix, norm_ffn=v_norm_ffn, w_gate_up=v_w_gate_up, w_down=v_w_down,
             fox_w_in=v_fox_w_in, fox_b_f=v_fox_b_f, fox_q_gain=v_fox_q_gain, fox_k_gain=v_fox_k_gain,
             fox_w_out=v_fox_w_out, gla_w_in=v_gla_w_in, gla_w_alpha2=v_gla_w_alpha2, gla_b_alpha=v_gla_b_alpha,
             gla_o_gain=v_gla_o_gain, gla_w_out=v_gla_w_out, gdn_w_in=v_gdn_w_in, gdn_conv_w=v_gdn_conv_w,
             gdn_a_log=v_gdn_a_log, gdn_dt_bias=v_gdn_dt_bias, gdn_o_gain=v_gdn_o_gain, gdn_w_out=v_gdn_w_out)
    chip = 2 * lax.axis_index("x") + lax.axis_index("y")

    pieces, offs, r = [], {}, FFN_ROWS
    for n in _IN_W:
        nc = W[n].shape[2]
        for l in range(W[n].shape[0]):
            pieces.append(jnp.pad(W[n][l].T.astype(BF16), ((0, _piece_rows(nc) - nc), (0, 0))))
            offs[n, l] = r
            r += _piece_rows(nc)
    for n in _OUT_W:
        for l in range(W[n].shape[0]):
            pieces.append(W[n][l].astype(BF16))
            offs[n, l] = r
            r += W[n].shape[1]
    rows = -(-r // _PACK_ROWS) * _PACK_ROWS
    packed = jnp.concatenate([jnp.swapaxes(w_gate_up, 1, 2).reshape(-1, D).astype(BF16), w_down.reshape(-1, D).astype(BF16)]
                             + pieces + [jnp.zeros((rows - r, D), BF16)], axis=0)
    first_rows = [(offs["fox_w_in", 0], offs["fox_w_in", 1] - offs["fox_w_in", 0]),
                  (offs["fox_w_out", 0], offs["fox_w_out", 1] - offs["fox_w_out", 0])]
    later_rows = [(0, FFN_ROWS), (offs["fox_w_in", 1], offs["fox_w_out", 0] - offs["fox_w_in", 1]),
                  (offs["fox_w_out", 1], r - offs["fox_w_out", 1])]
    wpk = _ag_weights(lax.dynamic_update_slice(lax.empty((4, rows, D), BF16), packed[None], (chip, 0, 0)), first_rows)

    def in_t(buf, n, l, npad):
        nc = W[n].shape[2]
        return jnp.concatenate([buf[q, offs[n, l]:offs[n, l] + nc] for q in range(4)] + [jnp.zeros((npad - 4 * nc, D), BF16)], 0)

    def out_w(buf, n, l):
        return jnp.concatenate([buf[q, offs[n, l]:offs[n, l] + W[n].shape[1]] for q in range(4)], axis=0)

    fox_in0, fox_out0 = in_t(wpk, "fox_w_in", 0, FOX_INP), out_w(wpk, "fox_w_out", 0)
    full = {}
    small = _pack([W[n] for n, _ in _SMALL_SHARDED], LANES, 8, F32)
    small_all = _gather8(small, reduce=False, name="gather_small").reshape(8, -1)
    for (n, ax), seg in zip(_SMALL_SHARDED, _unpack_cols(small_all, [W[n].shape for n, _ in _SMALL_SHARDED])):
        full[n] = jnp.concatenate([seg[2 * q] for q in range(4)], axis=ax)
    fox_in, full["fox_w_out"] = [fox_in0], [fox_out0]
    w_alpha2, conv_w = full["gla_w_alpha2"][0], full["gdn_conv_w"][0]

    h = jnp.concatenate([jnp.zeros((META0, D), F32), full["meta_tokens"], x[0]], axis=0)
    saved = []
    y = _rms_fwd(h, norm_mix[0], name="norm_mix0")
    for i in range(DEPTH):
        kind, j = i % 3, i // 3
        if kind == 0:
            proj = _mm(y, fox_in[j], tb=True, name=f"fox_in{j}")
            qa, ka, va = _fox_prep(proj, fox_b_f[j], fox_q_gain[j], fox_k_gain[j])
            if i == 0:
                o, og, lse, wpk = _fox_attn_fwd(qa, ka, va, proj, ag=(wpk, later_rows))
                wpk = _ag_forward(wpk, later_rows)
                fox_in += [in_t(wpk, "fox_w_in", l, FOX_INP) for l in range(1, fox_w_in.shape[0])]
                full["fox_w_out"] += [out_w(wpk, "fox_w_out", l) for l in range(1, fox_w_out.shape[0])]
                gla_in = [in_t(wpk, "gla_w_in", l, GLA_INP) for l in range(gla_w_in.shape[0])]
                gdn_in = [in_t(wpk, "gdn_w_in", l, GDN_INP) for l in range(gdn_w_in.shape[0])]
                for n in ("gla_w_out", "gdn_w_out"):
                    full[n] = [out_w(wpk, n, l) for l in range(W[n].shape[0])]
            else:
                o, og, lse = _fox_attn_fwd(qa, ka, va, proj)
            w_out, mix = full["fox_w_out"][j], (proj, qa, ka, va, o, lse)
        elif kind == 1:
            proj = _mm(y, gla_in[j], tb=True, name=f"gla_in{j}")
            o, og, states = _gla_fwd(proj, w_alpha2, gla_b_alpha[j], gla_o_gain[j])
            w_out, mix = full["gla_w_out"][j], (proj, o, states)
        else:
            proj = _mm(y, gdn_in[j], tb=True, name=f"gdn_in{j}")
            o, og, states = _gdn_fwd(proj, conv_w, gdn_a_log[j], gdn_dt_bias[j], gdn_o_gain[j])
            w_out, mix = full["gdn_w_out"][j], (proj, o, states)
        hm, yf = _mm(og, w_out, add=h, norm=norm_ffn[i], name=f"mix_out{i}")
        gate, up, act = _ffn_up(yf, wpk, i)
        hn, y_next = _ffn_down(act, wpk, i, hm, norm_mix[(i + 1) % DEPTH])
        saved.append((h, y, mix, og, w_out, hm, yf, gate, up, act))
        h, y = hn, y_next
    dh, loss_tile = _loss_head(h, loss_target[0])

    G = {n: [None] * W[n].shape[0] for n in _WEIGHTS if n not in ("meta_tokens", "w_gate_up", "w_down") + _IN_W}
    GT = {}

    def grad_layout(ffn_layers, pieces):
        off, end = {}, 0
        for l in ffn_layers:
            off["gu", l] = end
            end += GU_ROWS
        for l in ffn_layers:
            off["down", l] = end
            end += DOWN_ROWS
        for n, l in pieces:
            off[n, l] = end
            end += _piece_rows(W[n].shape[2]) if n in _IN_W else W[n].shape[1]
        return off, end, -(-end // _PACK_ROWS) * _PACK_ROWS

    first_pieces = [("fox_w_in", 0)]
    later_pieces = [(n, l) for n in _IN_W + _OUT_W for l in range(W[n].shape[0]) if (n, l) not in first_pieces]
    layouts = [grad_layout([], first_pieces), grad_layout(list(range(DEPTH)), later_pieces)]
    gbuf = [jnp.zeros((4, lay[2], D), BF16) for lay in layouts]

    def with_pieces(buf, lay, pieces):
        off, end, total = lay
        blocks = []
        for q in range(4):
            parts = []
            for n, l in pieces:
                if n in _IN_W:
                    nc = W[n].shape[2]
                    parts.append(jnp.pad(GT[n, l][q * nc:(q + 1) * nc], ((0, _piece_rows(nc) - nc), (0, 0))))
                else:
                    nr = W[n].shape[1]
                    parts.append(G[n][l][q * nr:(q + 1) * nr])
            blocks.append(jnp.concatenate(parts + [jnp.zeros((total - end, D), BF16)], axis=0))
        return lax.dynamic_update_slice(buf, jnp.stack(blocks), (0, off[pieces[0]], 0))

    s_later = None
    for i in reversed(range(DEPTH)):
        kind, j = i % 3, i // 3
        h_in, y, mix, og, w_out, hm, yf, gate, up, act = saved[i]
        b = 1
        dg, du = _ffn_dact(dh, wpk, i, gate, up)
        gbuf[b] = _ffn_dw_down(act, dh, gbuf[b], i, layouts[b][0]["down", i])
        dhm, dnf = _ffn_dyf(dg, du, wpk, i, hm, norm_ffn[i], dh)
        gbuf[b] = _ffn_dw_gu(dg, du, yf, gbuf[b], i, layouts[b][0]["gu", i] // GU_ROWS)
        G["norm_ffn"][i] = dnf[0]
        dog = _mm(dhm, w_out, tb=True, name=f"d_og{i}")
        dw_out = _mm(og, dhm, ta=True, out_dtype=BF16, name=f"d_w_out{i}")
        if kind == 0:
            proj, qa, ka, va, o, lse = mix
            G["fox_w_out"][j] = dw_out
            if i == 0:
                g_later = with_pieces(gbuf[1], layouts[1], later_pieces)
                doa, q2, dgate, got = _fox_gate_bwd(dog, o, proj, lse, qa, swap=g_later)
                s_later = _add_halves(g_later, got, "_later")
                dqn, dkn, dv, dct, recv_later = _fox_attn_bwd(q2, ka, va, doa, rs=s_later)
            else:
                doa, q2, dgate = _fox_gate_bwd(dog, o, proj, lse, qa)
                dqn, dkn, dv, dct = _fox_attn_bwd(q2, ka, va, doa)
            dproj, dqg, dkg, dbf = _fox_prep_bwd(proj, fox_b_f[j], fox_q_gain[j], fox_k_gain[j], dqn, dkn, dv, dgate, dct)
            G["fox_q_gain"][j] = dqg.reshape(FOX_H, FOX_DH).sum(0)
            G["fox_k_gain"][j] = dkg.reshape(FOX_H, FOX_DH).sum(0)
            G["fox_b_f"][j] = dbf[0, :FOX_H]
            w_in, wname = fox_in[j], "fox_w_in"
        elif kind == 1:
            proj, o, states = mix
            dproj, dwa, dba, dogain = _gla_bwd(proj, w_alpha2, gla_b_alpha[j], gla_o_gain[j], o, states, dog)
            G["gla_w_out"][j] = dw_out
            G["gla_w_alpha2"][j] = dwa[:GLA_RANK]
            G["gla_b_alpha"][j] = dba[0]
            G["gla_o_gain"][j] = dogain[0]
            w_in, wname = gla_in[j], "gla_w_in"
        else:
            proj, o, states = mix
            dproj, dcw, dal, ddt, dogain = _gdn_bwd(proj, conv_w, gdn_a_log[j], gdn_dt_bias[j], gdn_o_gain[j], o, states, dog)
            G["gdn_w_out"][j] = dw_out
            G["gdn_conv_w"][j] = dcw[:4].reshape(4, 1, GDN_CONV)
            G["gdn_a_log"][j] = dal[0, :GDN_H]
            G["gdn_dt_bias"][j] = ddt[0, :GDN_H]
            G["gdn_o_gain"][j] = dogain[0]
            w_in, wname = gdn_in[j], "gdn_w_in"
        dh, dnm = _mm(dproj, w_in, rms_bwd=(h_in, norm_mix[i], dhm), name=f"d_y{i}")
        GT[wname, j] = _mm(dproj, y, ta=True, out_dtype=BF16, name=f"d_w_in{i}")
        G["norm_mix"][i] = dnm[0]
    grad_x = dh[ROW0:][None]
    G = {n: (v if n in _OUT_W else jnp.stack(v)) for n, v in G.items()}
    G["meta_tokens"] = dh[META0:ROW0]

    s_first = _rs_local(with_pieces(gbuf[0], layouts[0], first_pieces), "_first")
    reduced = [_rs_finish(s_first, _scatter_chips(s_first, "_first"), "_first"), _rs_finish(s_later, recv_later, "_later")]

    def reduced_piece(n, l):
        b = 0 if (n, l) in first_pieces else 1
        start = layouts[b][0][n, l]
        return reduced[b][start:start + (W[n].shape[2] if n in _IN_W else W[n].shape[1])]

    grads = {}
    for n in _IN_W:
        grads[n] = jnp.stack([reduced_piece(n, l).T for l in range(W[n].shape[0])])
    for n in _OUT_W:
        grads[n] = jnp.stack([reduced_piece(n, l) for l in range(W[n].shape[0])])
    small_names = [n for n, _ in _SMALL_SHARDED] + list(_REPLICATED)
    small_g = _pack([G[n] for n in small_names] + [loss_tile[0, 0:1]], LANES, 8, F32)
    small_sum = _gather8(small_g, reduce=True, name="allreduce_small").reshape(-1)
    small_shapes = [G[n].shape for n in small_names] + [(1,)]
    small_vals = _unpack(small_sum, small_shapes)
    loss = small_vals[-1][0]
    for n, val in zip(small_names, small_vals[:-1]):
        grads[n] = val
    for n, ax in _SMALL_SHARDED:
        sz = W[n].shape[ax]
        grads[n] = lax.dynamic_slice_in_dim(grads[n], chip * sz, sz, axis=ax)

    delta, new_m, new_v = {}, {}, {}
    for n, key, tr_ in (("w_gate_up", "gu", True), ("w_down", "down", False)):
        grads[n], delta[n], new_m[n], new_v[n] = _adamw_packed(
            W[n], reduced[1], reduced[1], M[n], V[n], row0=layouts[1][0][key, 0], row_off=layouts[1][0][key, 1],
            transposed=tr_, name=f"adamw_{n}")
    for n in _IN_W + _OUT_W:
        delta[n], new_m[n], new_v[n] = _adamw(W[n], grads[n], M[n], V[n], name=f"adamw_{n}")
    tiny = [n for n in _WEIGHTS if n not in ("w_gate_up", "w_down") + _IN_W + _OUT_W]
    packs = [_pack([T[n] for n in tiny], LANES, 8, F32) for T in (W, grads, M, V)]
    outs = _adamw(*packs, name="adamw_small")
    shapes = [W[n].shape for n in tiny]
    for dst, o in zip((delta, new_m, new_v), outs):
        for n, val in zip(tiny, _unpack(o.reshape(-1), shapes)):
            dst[n] = val
    return (loss, grad_x, *[grads[n] for n in _WEIGHTS], *[delta[n] for n in _WEIGHTS],
            *[new_m[n] for n in _WEIGHTS], *[new_v[n] for n in _WEIGHTS])
```

```python
import jax
import jax.numpy as jnp
from jax import lax
from jax.experimental import pallas as pl
from jax.experimental.pallas import tpu as pltpu

F32, BF16 = jnp.float32, jnp.bfloat16
D = 1024
N_META = 16
ROW0 = 128
META0 = ROW0 - N_META
EPS = 1e-6
LANES = 128
VMEM_LIMIT = 56 * 1024 * 1024

FOX_H, FOX_DH = 16, 64
FOX_INP = 4224
GLA_H, GLA_DK, GLA_DV, GLA_RANK = 4, 128, 256, 16
GLA_QK, GLA_V = 512, 1024
GLA_INP = 3200
GLA_NORM = 16.0
GDN_H, GDN_DK, GDN_DV = 8, 128, 128
GDN_CONV = 3072
GDN_INP = 4224
CHUNK = 64
D_FF = 2816
DEPTH = 4

ADAM_LR, ADAM_B1, ADAM_B2, ADAM_EPS, ADAM_WD, ADAM_STEP = 0.001, 0.9, 0.999, 1e-08, 0.01, 10

MESH = pl.DeviceIdType.MESH
ANY = pl.BlockSpec(memory_space=pl.ANY)
VM = pl.BlockSpec(memory_space=pltpu.VMEM)


def _params(sem=None, **kw):
    if sem is not None:
        kw["dimension_semantics"] = sem
    return pltpu.CompilerParams(vmem_limit_bytes=VMEM_LIMIT, **kw)


def _tile(n, cap, mult=LANES):
    best = None
    for t in range(mult, min(n, cap) + 1, mult):
        if n % t == 0:
            best = t
    return best if best is not None else n


def nn(a, b, **kw):
    return jnp.dot(a, b, preferred_element_type=F32, **kw)


def nt(a, b, **kw):
    return lax.dot_general(a, b, (((1,), (1,)), ((), ())), preferred_element_type=F32, **kw)


def tn(a, b, **kw):
    return lax.dot_general(a, b, (((0,), (0,)), ((), ())), preferred_element_type=F32, **kw)


def _split3(x):
    hi = x.astype(BF16)
    r = x - hi.astype(F32)
    mid = r.astype(BF16)
    lo = (r - mid.astype(F32)).astype(BF16)
    return hi, mid, lo


def _sel_l(sel, x):
    a, b, c = _split3(x)
    return nn(sel, a) + nn(sel, b) + nn(sel, c)


def _sel_r(x, sel):
    a, b, c = _split3(x)
    return nn(a, sel) + nn(b, sel) + nn(c, sel)


def _sel_r2(x, sel):
    a = x.astype(BF16)
    return nn(a, sel) + nn((x - a.astype(F32)).astype(BF16), sel)


def _iota(shape, dim):
    return lax.broadcasted_iota(jnp.int32, shape, dim)


def _tri(n, upper=False, strict=False):
    i, j = _iota((n, n), 0), _iota((n, n), 1)
    if upper:
        m = (j > i) if strict else (j >= i)
    else:
        m = (j < i) if strict else (j <= i)
    return m


def _sigmoid(x):
    return 1.0 / (1.0 + jnp.exp(-x))


def _log_sigmoid(x):
    return jnp.minimum(x, 0.0) - jnp.log(1.0 + jnp.exp(-jnp.abs(x)))


def _softplus(x):
    return jnp.maximum(x, 0.0) + jnp.log(1.0 + jnp.exp(-jnp.abs(x)))


def _silu(x):
    return x * _sigmoid(x)


def _dsilu(x):
    s = _sigmoid(x)
    return s * (1.0 + x * (1.0 - s))


def _rms(x, g):
    return (x * lax.rsqrt(jnp.mean(x * x, axis=-1, keepdims=True) + EPS) * g).astype(BF16)


def _rms_grad(x, g, dy):
    r = lax.rsqrt(jnp.mean(x * x, axis=-1, keepdims=True) + EPS)
    u = dy * g
    return r * u - x * (r * r * r) * jnp.mean(x * u, axis=-1, keepdims=True), jnp.sum(dy * x * r, axis=0, keepdims=True)


def _mm(a, b, *, ta=False, tb=False, add=None, norm=None, rms_bwd=None, out_dtype=F32, name):
    m, k = (a.shape[1], a.shape[0]) if ta else a.shape
    n = b.shape[0] if tb else b.shape[1]
    assert k == (b.shape[1] if tb else b.shape[0])
    rows_whole = norm is not None or rms_bwd is not None
    tm, tn_, tk = _tile(m, 704 if rows_whole else 1408, LANES if ta else 16), _tile(n, 1408), _tile(k, 1408)
    nk = k // tk
    assert not rows_whole or tn_ == n

    def body(*refs):
        refs = list(refs)
        a_ref, b_ref = refs[:2]
        extra = refs[2:-1]
        acc = refs[-1]
        i, kk = pl.program_id(0), pl.program_id(2)

        @pl.when(kk == 0)
        def _():
            acc[...] = jnp.zeros_like(acc)

        av, bv = a_ref[...].astype(BF16), b_ref[...].astype(BF16)
        dims = (((0,) if ta else (1,), (1,) if tb else (0,)), ((), ()))
        acc[...] += lax.dot_general(av, bv, dims, preferred_element_type=F32)

        @pl.when(kk == nk - 1)
        def _():
            r = acc[...]
            if rms_bwd is not None:
                h_ref, g_ref, dres_ref, o_ref, dg_ref = extra
                dx, dgain = _rms_grad(h_ref[...], g_ref[...], r)
                o_ref[...] = dres_ref[...] + dx

                @pl.when(i == 0)
                def _():
                    dg_ref[...] = jnp.zeros_like(dg_ref)

                dg_ref[...] += dgain
                return
            if add is not None:
                r = r + extra[0][...].astype(F32)
            if norm is not None:
                g_ref, o_ref, y_ref = extra[-3:]
                y_ref[...] = _rms(r, g_ref[...])
            else:
                o_ref = extra[-1]
            o_ref[...] = r.astype(out_dtype)

    a_spec = pl.BlockSpec((tk, tm), lambda i, j, q: (q, i)) if ta else pl.BlockSpec((tm, tk), lambda i, j, q: (i, q))
    b_spec = pl.BlockSpec((tn_, tk), lambda i, j, q: (j, q)) if tb else pl.BlockSpec((tk, tn_), lambda i, j, q: (q, j))
    o_spec = pl.BlockSpec((tm, tn_), lambda i, j, q: (i, j))
    g_spec = pl.BlockSpec((1, n), lambda i, j, q: (0, 0))
    ins, specs = [a, b], [a_spec, b_spec]
    out_specs, out_shape = o_spec, jax.ShapeDtypeStruct((m, n), out_dtype)
    sem = ("parallel", "parallel", "arbitrary")
    if rms_bwd is not None:
        ins += [rms_bwd[0], rms_bwd[1].reshape(1, n), rms_bwd[2]]
        specs += [o_spec, g_spec, o_spec]
        out_specs, out_shape = [o_spec, g_spec], [jax.ShapeDtypeStruct((m, n), F32), jax.ShapeDtypeStruct((1, n), F32)]
        sem = ("arbitrary", "arbitrary", "arbitrary")
    else:
        if add is not None:
            ins.append(add)
            specs.append(o_spec)
        if norm is not None:
            ins.append(norm.reshape(1, n))
            specs.append(g_spec)
            out_specs, out_shape = [o_spec, o_spec], [out_shape, jax.ShapeDtypeStruct((m, n), BF16)]
    return pl.pallas_call(
        body, name=name, grid=(m // tm, n // tn_, nk), in_specs=specs, out_specs=out_specs, out_shape=out_shape,
        scratch_shapes=[pltpu.VMEM((tm, tn_), F32)], compiler_params=_params(sem),
    )(*ins)


def _rms_fwd(h, g, *, name):
    lp = h.shape[0]
    tr = _tile(lp, 512)

    def body(h_ref, g_ref, y_ref):
        x = h_ref[...]
        r = lax.rsqrt(jnp.mean(x * x, axis=-1, keepdims=True) + EPS)
        y_ref[...] = (x * r * g_ref[...]).astype(BF16)

    return pl.pallas_call(
        body, name=name, grid=(lp // tr,),
        in_specs=[pl.BlockSpec((tr, D), lambda i: (i, 0)), pl.BlockSpec((1, D), lambda i: (0, 0))],
        out_specs=pl.BlockSpec((tr, D), lambda i: (i, 0)),
        out_shape=jax.ShapeDtypeStruct((lp, D), BF16), compiler_params=_params(("parallel",)),
    )(h, g.reshape(1, D))


GU_ROWS, DOWN_ROWS = 1408, 704
OFF_GU, OFF_DOWN = 0, DEPTH * GU_ROWS
FFN_ROWS = DEPTH * (GU_ROWS + DOWN_ROWS)
FFN_TM = 704


def _gu_spec(fn):
    return pl.BlockSpec((None, GU_ROWS, D), fn)


def _down_spec(fn):
    return pl.BlockSpec((None, DOWN_ROWS, D), fn)


def _down_pair(w0_ref, w1_ref):
    return jnp.concatenate([w0_ref[...], w1_ref[...]], axis=0)


def _ffn_up(yf, wpk, layer):
    lp = yf.shape[0]
    tm = _tile(lp, FFN_TM, 16)

    def body(y_ref, wg_ref, wu_ref, g_ref, u_ref, a_ref):
        y = y_ref[...]
        g, u = nt(y, wg_ref[...]), nt(y, wu_ref[...])
        g_ref[...] = g.astype(BF16)
        u_ref[...] = u.astype(BF16)
        a_ref[...] = (_silu(g) * u).astype(BF16)

    o = pl.BlockSpec((tm, GU_ROWS), lambda i, j: (i, j))
    return pl.pallas_call(
        body, name=f"ffn_up{layer}", grid=(lp // tm, 2),
        in_specs=[pl.BlockSpec((tm, D), lambda i, j: (i, 0)), _gu_spec(lambda i, j: (j, OFF_GU // GU_ROWS + layer, 0)),
                  _gu_spec(lambda i, j: (2 + j, OFF_GU // GU_ROWS + layer, 0))],
        out_specs=[o, o, o], out_shape=[jax.ShapeDtypeStruct((lp, D_FF), BF16)] * 3,
        compiler_params=_params(("parallel", "parallel")),
    )(yf, wpk, wpk)


def _ffn_down(act, wpk, layer, res, norm):
    lp = act.shape[0]
    tm = _tile(lp, FFN_TM, 16)

    def body(a_ref, w0_ref, w1_ref, r_ref, g_ref, o_ref, y_ref, acc):
        kk = pl.program_id(1)

        @pl.when(kk == 0)
        def _():
            acc[...] = r_ref[...]

        acc[...] += nn(a_ref[...], _down_pair(w0_ref, w1_ref))

        @pl.when(kk == 1)
        def _():
            o_ref[...] = acc[...]
            y_ref[...] = _rms(acc[...], g_ref[...])

    o = pl.BlockSpec((tm, D), lambda i, kk: (i, 0))
    blk = OFF_DOWN // DOWN_ROWS + layer
    return pl.pallas_call(
        body, name=f"ffn_down{layer}", grid=(lp // tm, 2),
        in_specs=[pl.BlockSpec((tm, GU_ROWS), lambda i, kk: (i, kk)), _down_spec(lambda i, kk: (2 * kk, blk, 0)),
                  _down_spec(lambda i, kk: (2 * kk + 1, blk, 0)), o, pl.BlockSpec((1, D), lambda i, kk: (0, 0))],
        out_specs=[o, o], out_shape=[jax.ShapeDtypeStruct((lp, D), F32), jax.ShapeDtypeStruct((lp, D), BF16)],
        scratch_shapes=[pltpu.VMEM((tm, D), F32)], compiler_params=_params(("parallel", "arbitrary")),
    )(act, wpk, wpk, res, norm.reshape(1, D))


def _ffn_dact(dh, wpk, layer, gate, up):
    lp = dh.shape[0]
    tm = _tile(lp, FFN_TM, 16)

    def body(d_ref, w0_ref, w1_ref, g_ref, u_ref, dg_ref, du_ref):
        da = nt(d_ref[...].astype(BF16), _down_pair(w0_ref, w1_ref))
        g, u = g_ref[...].astype(F32), u_ref[...].astype(F32)
        sg = _sigmoid(g)
        dg_ref[...] = (da * u * (sg * (1.0 + g * (1.0 - sg)))).astype(BF16)
        du_ref[...] = (da * (g * sg)).astype(BF16)

    o = pl.BlockSpec((tm, GU_ROWS), lambda i, j: (i, j))
    blk = OFF_DOWN // DOWN_ROWS + layer
    return pl.pallas_call(
        body, name=f"d_act{layer}", grid=(lp // tm, 2),
        in_specs=[pl.BlockSpec((tm, D), lambda i, j: (i, 0)), _down_spec(lambda i, j: (2 * j, blk, 0)),
                  _down_spec(lambda i, j: (2 * j + 1, blk, 0)), o, o],
        out_specs=[o, o], out_shape=[jax.ShapeDtypeStruct((lp, D_FF), BF16)] * 2,
        compiler_params=_params(("parallel", "parallel")),
    )(dh, wpk, wpk, gate, up)


def _ffn_dyf(dg, du, wpk, layer, hm, norm, dres):
    lp = dg.shape[0]
    tm = _tile(lp, FFN_TM, 16)

    def body(dg_ref, du_ref, w_ref, h_ref, g_ref, dres_ref, o_ref, dgain_ref, acc):
        i, kk = pl.program_id(0), pl.program_id(1)

        @pl.when(kk == 0)
        def _():
            acc[...] = jnp.zeros_like(acc)

        @pl.when(kk < 2)
        def _():
            acc[...] += nn(dg_ref[...], w_ref[...])

        @pl.when(kk >= 2)
        def _():
            acc[...] += nn(du_ref[...], w_ref[...])

        @pl.when(kk == 3)
        def _():
            dx, dgain = _rms_grad(h_ref[...], g_ref[...], acc[...])
            o_ref[...] = dres_ref[...] + dx

            @pl.when(i == 0)
            def _():
                dgain_ref[...] = jnp.zeros_like(dgain_ref)

            dgain_ref[...] += dgain

    o = pl.BlockSpec((tm, D), lambda i, kk: (i, 0))
    row = pl.BlockSpec((1, D), lambda i, kk: (0, 0))
    return pl.pallas_call(
        body, name=f"d_yf{layer}", grid=(lp // tm, 4),
        in_specs=[pl.BlockSpec((tm, GU_ROWS), lambda i, kk: (i, jnp.minimum(kk, 1))),
                  pl.BlockSpec((tm, GU_ROWS), lambda i, kk: (i, jnp.maximum(kk - 2, 0))),
                  _gu_spec(lambda i, kk: (kk, OFF_GU // GU_ROWS + layer, 0)), o, row, o],
        out_specs=[o, row], out_shape=[jax.ShapeDtypeStruct((lp, D), F32), jax.ShapeDtypeStruct((1, D), F32)],
        scratch_shapes=[pltpu.VMEM((tm, D), F32)], compiler_params=_params(("arbitrary", "arbitrary")),
    )(dg, du, wpk, hm, norm.reshape(1, D), dres)


def _ffn_dw_down(act, dh, gpk, layer, row):
    lp = act.shape[0]
    tk = _tile(lp, 1408, 16)
    nk = lp // tk

    def body(a_ref, d_ref, g_in, g_out, acc, stage, sems):
        jp, kk = pl.program_id(0), pl.program_id(1)

        @pl.when(kk == 0)
        def _():
            acc[...] = jnp.zeros_like(acc)

        acc[...] += tn(a_ref[...], d_ref[...].astype(BF16))

        @pl.when(kk == nk - 1)
        def _():
            stage[...] = acc[...].astype(BF16)
            copies = [pltpu.make_async_copy(stage.at[pl.ds(hf * DOWN_ROWS, DOWN_ROWS), :],
                                            g_out.at[2 * jp + hf, pl.ds(row, DOWN_ROWS), :], sems.at[hf]) for hf in range(2)]
            for cp in copies:
                cp.start()
            for cp in copies:
                cp.wait()

    return pl.pallas_call(
        body, name=f"d_w_down{layer}", grid=(2, nk),
        in_specs=[pl.BlockSpec((tk, GU_ROWS), lambda jp, kk: (kk, jp)), pl.BlockSpec((tk, D), lambda jp, kk: (kk, 0)), ANY],
        out_specs=ANY, out_shape=jax.ShapeDtypeStruct(gpk.shape, gpk.dtype),
        scratch_shapes=[pltpu.VMEM((GU_ROWS, D), F32), pltpu.VMEM((GU_ROWS, D), BF16), pltpu.SemaphoreType.DMA((2,))],
        input_output_aliases={2: 0}, compiler_params=_params(("arbitrary", "arbitrary")),
    )(act, dh, gpk)


def _ffn_dw_gu(dg, du, yf, gpk, layer, blk):
    lp = dg.shape[0]
    tk = _tile(lp, 1408, 16)
    nk = lp // tk

    def body(dg_ref, du_ref, y_ref, g_in, o_ref, acc):
        c, kk = pl.program_id(0), pl.program_id(1)

        @pl.when(kk == 0)
        def _():
            acc[...] = jnp.zeros_like(acc)

        @pl.when(c < 2)
        def _():
            acc[...] += tn(dg_ref[...], y_ref[...])

        @pl.when(c >= 2)
        def _():
            acc[...] += tn(du_ref[...], y_ref[...])

        @pl.when(kk == nk - 1)
        def _():
            o_ref[...] = acc[...].astype(BF16)

    return pl.pallas_call(
        body, name=f"d_w_gate_up{layer}", grid=(4, nk),
        in_specs=[pl.BlockSpec((tk, GU_ROWS), lambda c, kk: (kk, jnp.minimum(c, 1))),
                  pl.BlockSpec((tk, GU_ROWS), lambda c, kk: (kk, jnp.maximum(c - 2, 0))),
                  pl.BlockSpec((tk, D), lambda c, kk: (kk, 0)), ANY],
        out_specs=_gu_spec(lambda c, kk: (c, blk, 0)),
        out_shape=jax.ShapeDtypeStruct(gpk.shape, gpk.dtype),
        scratch_shapes=[pltpu.VMEM((GU_ROWS, D), F32)], input_output_aliases={3: 0},
        compiler_params=_params(("parallel", "arbitrary")),
    )(dg, du, yf, gpk)


def _loss_head(h, target):
    lp = h.shape[0]
    nb = lp // ROW0

    def body(h_ref, t_ref, dh_ref, l_ref):
        i = pl.program_id(0)

        @pl.when(i == 0)
        def _():
            l_ref[...] = jnp.zeros_like(l_ref)
            dh_ref[...] = jnp.zeros_like(dh_ref)

        @pl.when(i > 0)
        def _():
            err = h_ref[...] - t_ref[...]
            dh_ref[...] = err * (1.0 / D)
            l_ref[...] += jnp.sum(err * err) * (0.5 / D)

    return pl.pallas_call(
        body, name="loss_head", grid=(nb,),
        in_specs=[pl.BlockSpec((ROW0, D), lambda i: (i, 0)), pl.BlockSpec((ROW0, D), lambda i: (jnp.maximum(i - 1, 0), 0))],
        out_specs=[pl.BlockSpec((ROW0, D), lambda i: (i, 0)), pl.BlockSpec((8, LANES), lambda i: (0, 0))],
        out_shape=[jax.ShapeDtypeStruct((lp, D), F32), jax.ShapeDtypeStruct((8, LANES), F32)],
        compiler_params=_params(("arbitrary",)),
    )(h, target)


def _adamw(w, g, m, v, *, name):
    if w.ndim == 2:
        w, g, m, v = (t[None] for t in (w, g, m, v))
        return tuple(o[0] for o in _adamw(w, g, m, v, name=name))
    nl, r, c = w.shape
    tr = _tile(r, max(8, (1 << 19) // c), 8)

    def body(w_ref, g_ref, m_ref, v_ref, d_ref, nm_ref, nv_ref):
        d_ref[...], nm_ref[...], nv_ref[...] = _adam_math(w_ref[...], g_ref[...], m_ref[...], v_ref[...])

    spec = pl.BlockSpec((1, tr, c), lambda l, i: (l, i, 0))
    return tuple(pl.pallas_call(
        body, name=name, grid=(nl, r // tr), in_specs=[spec] * 4, out_specs=[spec] * 3,
        out_shape=[jax.ShapeDtypeStruct(w.shape, F32)] * 3, compiler_params=_params(("parallel", "parallel")),
    )(w, g, m, v))


def _adam_math(w, g, m, v):
    nm = ADAM_B1 * m + (1.0 - ADAM_B1) * g
    nv = ADAM_B2 * v + (1.0 - ADAM_B2) * (g * g)
    m_hat = nm / (1.0 - ADAM_B1 ** ADAM_STEP)
    v_hat = nv / (1.0 - ADAM_B2 ** ADAM_STEP)
    return -ADAM_LR * (m_hat / (jnp.sqrt(v_hat) + ADAM_EPS) + ADAM_WD * w), nm, nv


def _adamw_packed(w, gred0, gred, m, v, *, row0, row_off, transposed, name):
    nl, a, b = w.shape
    nr = b if transposed else a
    later = lambda l: row_off // nr + jnp.maximum(l - 1, 0)
    if transposed:
        ta = _tile(a, 256)
        wspec = pl.BlockSpec((1, ta, b), lambda l, r: (l, r, 0))
        g0spec = pl.BlockSpec((b, ta), lambda l, r: (row0 // nr, r))
        gspec = pl.BlockSpec((b, ta), lambda l, r: (later(l), r))
        grid = (nl, a // ta)
    else:
        wspec = pl.BlockSpec((1, a, b), lambda l, r: (l, 0, 0))
        g0spec = pl.BlockSpec((a, b), lambda l, r: (row0 // nr, 0))
        gspec = pl.BlockSpec((a, b), lambda l, r: (later(l), 0))
        grid = (nl, 1)

    def body(w_ref, g0_ref, g_ref, m_ref, v_ref, go_ref, d_ref, nm_ref, nv_ref):
        g = jnp.where(pl.program_id(0) == 0, g0_ref[...], g_ref[...])
        g = g.T if transposed else g
        d, nm, nv = _adam_math(w_ref[0], g, m_ref[0], v_ref[0])
        go_ref[0], d_ref[0], nm_ref[0], nv_ref[0] = g, d, nm, nv

    return pl.pallas_call(
        body, name=name, grid=grid, in_specs=[wspec, g0spec, gspec, wspec, wspec], out_specs=[wspec] * 4,
        out_shape=[jax.ShapeDtypeStruct(w.shape, F32)] * 4, compiler_params=_params(("parallel", "parallel")),
    )(w, gred0, gred, m, v)


FOX_AUG = FOX_H * LANES
L_C = 64
L_K = 67
L_LSE = 70
PAD_KEY = -30000.0
FOX_TQ = 384


def _head_sel(n_heads, width, lanes=LANES):
    r, c = _iota((n_heads * width, lanes), 0), _iota((n_heads * width, lanes), 1)
    down = (r // width == c).astype(BF16)
    r2, c2 = _iota((lanes, n_heads * width), 0), _iota((lanes, n_heads * width), 1)
    up = (c2 // width == r2).astype(BF16)
    return down, up


def _place(lane0):
    r, c = _iota((LANES, FOX_AUG), 0), _iota((LANES, FOX_AUG), 1)
    return [((c // LANES == r) & (c % LANES == lane0 + m)).astype(BF16) for m in range(3)]


def _placed(x, lane0):
    pcs = _split3(x)
    mats = _place(lane0)
    return nn(pcs[0], mats[0]) + nn(pcs[1], mats[1]) + nn(pcs[2], mats[2])


def _ones_at(rows, lanes):
    c = _iota((rows, FOX_AUG), 1) % LANES
    m = c == lanes[0]
    for l in lanes[1:]:
        m = m | (c == l)
    return m.astype(F32)


def _spread(x, extras, out_ref):
    rows = x.shape[0]
    left = _iota((rows, LANES), 1) < FOX_DH
    for p in range(FOX_H // 2):
        slab = x[:, p * LANES:(p + 1) * LANES]
        a = jnp.where(left, slab, extras[:, 2 * p * LANES:(2 * p + 1) * LANES])
        b = jnp.where(left, pltpu.roll(slab, FOX_DH, 1), extras[:, (2 * p + 1) * LANES:(2 * p + 2) * LANES])
        out_ref[:, 2 * p * LANES:(2 * p + 1) * LANES] = a.astype(BF16)
        out_ref[:, (2 * p + 1) * LANES:(2 * p + 2) * LANES] = b.astype(BF16)


def _fox_prep(proj, b_f, q_gain, k_gain):
    lp = proj.shape[0]
    nb = lp // LANES

    def body(p_ref, bf_ref, qg_ref, kg_ref, q_ref, k_ref, v_ref, carry):
        i = pl.program_id(0)

        @pl.when(i == 0)
        def _():
            carry[...] = jnp.zeros_like(carry)

        down, up = _head_sel(FOX_H, FOX_DH)

        def normed(x, gain):
            ms = _sel_r2(x * x, down) * (1.0 / FOX_DH)
            r = _sel_r2(lax.rsqrt(ms + EPS), up)
            return x * r * gain

        lane = _iota((LANES, LANES), 1)
        lf = jnp.where(lane < FOX_H, _log_sigmoid(p_ref[:, 4 * D:4 * D + LANES] + bf_ref[...]), 0.0)
        c = _sel_l(_tri(LANES).astype(BF16), lf) + carry[0:1, :]
        carry[...] = jnp.broadcast_to(c[LANES - 1:LANES, :], carry.shape)
        q_extra = _placed(c, L_C) + _ones_at(LANES, (L_K, L_K + 1, L_K + 2))
        row = i * LANES + _iota((LANES, FOX_AUG), 0)
        lane_a = _iota((LANES, FOX_AUG), 1) % LANES
        k_extra = -_placed(c, L_K) + _ones_at(LANES, (L_C, L_C + 1, L_C + 2, L_LSE, L_LSE + 1, L_LSE + 2))
        pad_val = jnp.where(lane_a == L_K, PAD_KEY, 0.0)
        k_extra = jnp.where((row < META0) & (lane_a >= L_K) & (lane_a < L_K + 3), pad_val, k_extra)
        v_extra = _ones_at(LANES, (L_C, L_C + 1, L_C + 2))
        _spread(normed(p_ref[:, 0:D], qg_ref[...]) * (FOX_DH ** -0.5), q_extra, q_ref)
        _spread(normed(p_ref[:, D:2 * D], kg_ref[...]), k_extra, k_ref)
        _spread(p_ref[:, 2 * D:3 * D], v_extra, v_ref)

    row = pl.BlockSpec((1, D), lambda i: (0, 0))
    aug = pl.BlockSpec((LANES, FOX_AUG), lambda i: (i, 0))
    return pl.pallas_call(
        body, name="fox_prep", grid=(nb,),
        in_specs=[pl.BlockSpec((LANES, FOX_INP), lambda i: (i, 0)), pl.BlockSpec((1, LANES), lambda i: (0, 0)), row, row],
        out_specs=[aug] * 3, out_shape=[jax.ShapeDtypeStruct((lp, FOX_AUG), BF16)] * 3,
        scratch_shapes=[pltpu.VMEM((8, LANES), F32)],
        compiler_params=_params(("arbitrary",)),
    )(proj, jnp.pad(b_f, (0, LANES - FOX_H)).reshape(1, LANES), jnp.tile(q_gain, FOX_H).reshape(1, D),
      jnp.tile(k_gain, FOX_H).reshape(1, D))


def _fox_attn_fwd(qa, ka, va, proj, ag=None):
    lp = qa.shape[0]
    tq = _tile(lp, FOX_TQ)
    nq = lp // tq
    npair = FOX_H // 2

    def body(q_ref, k_ref, v_ref, gate_ref, *rest):
        if ag is None:
            o_ref, og_ref, lse_ref = rest
        else:
            _, o_ref, og_ref, lse_ref, w_out, send_sems, recv_sems = rest
            copies = _AgCopies(w_out, ag[1], send_sems, recv_sems)

            @pl.when((pl.program_id(0) == 0) & (pl.program_id(1) == 0))
            def _():
                for r, k in copies.pairs():
                    copies.ici(r, k).start()

        i = pl.program_id(1)
        causal = _iota((tq, tq), 1) <= _iota((tq, tq), 0)
        qs = [q_ref[:, hh * LANES:(hh + 1) * LANES] for hh in range(2)]

        def block(j, carry, diag):
            off = pl.multiple_of(j * tq, tq)
            out = []
            for hh in range(2):
                m, acc = carry[hh]
                k = k_ref[pl.ds(off, tq), hh * LANES:(hh + 1) * LANES]
                v = v_ref[pl.ds(off, tq), hh * LANES:(hh + 1) * LANES]
                s = nt(qs[hh], k)
                if diag:
                    s = jnp.where(causal, s, -1e30)
                m2 = jnp.maximum(m, jnp.max(s, axis=-1, keepdims=True))
                p = jnp.exp(s - m2)
                p_hi = p.astype(BF16)
                p_lo = (p - p_hi.astype(F32)).astype(BF16)
                out.append((m2, jnp.exp(m - m2) * acc + nn(p_hi, v) + nn(p_lo, v)))
            return tuple(out)

        init = tuple((jnp.full((tq, 1), -1e30, F32), jnp.zeros((tq, LANES), F32)) for _ in range(2))
        carry = lax.fori_loop(0, i // 2, lambda j, c: block(2 * j + 1, block(2 * j, c, False), False), init)
        carry = lax.cond(i % 2 == 1, lambda c: block(i - 1, c, False), lambda c: c, carry)
        carry = block(i, carry, True)
        outs, lses = [], []
        for hh in range(2):
            m, acc = carry[hh]
            l = acc[:, L_C:L_C + 1]
            outs.append(acc / l)
            lses.append(jnp.broadcast_to(m + jnp.log(l), (tq, LANES)))
        left = _iota((tq, LANES), 1) < FOX_DH
        o = jnp.where(left, outs[0], pltpu.roll(outs[1], FOX_DH, 1))
        o_ref[...] = o
        og_ref[...] = (o * _sigmoid(gate_ref[...])).astype(BF16)
        lse_ref[...] = jnp.where(left, lses[0], lses[1])

        if ag is not None:
            @pl.when((pl.program_id(0) == npair - 1) & (pl.program_id(1) == nq - 1))
            def _():
                for r, k in copies.pairs():
                    copies.ici_arrival(r, k).wait_recv()
                for r, k in copies.pairs():
                    copies.ici(r, k).wait_send()

    qspec = pl.BlockSpec((tq, 2 * LANES), lambda p, i: (i, p))
    kspec = pl.BlockSpec((lp, 2 * LANES), lambda p, i: (0, p))
    ospec = pl.BlockSpec((tq, LANES), lambda p, i: (i, p))
    ins, in_specs = [qa, ka, va, proj], [qspec, kspec, kspec, pl.BlockSpec((tq, LANES), lambda p, i: (i, 3 * D // LANES + p))]
    out_specs = [ospec] * 3
    out_shape = [jax.ShapeDtypeStruct((lp, D), F32), jax.ShapeDtypeStruct((lp, D), BF16), jax.ShapeDtypeStruct((lp, D), F32)]
    if ag is None:
        return pl.pallas_call(body, name="fox_attn_fwd", grid=(npair, nq), in_specs=in_specs, out_specs=out_specs,
                              out_shape=out_shape, compiler_params=_params(("parallel", "arbitrary")))(*ins)
    n = 3 * len(ag[1])
    return pl.pallas_call(
        body, name="fox_attn_fwd_ag", grid=(npair, nq), in_specs=in_specs + [ANY], out_specs=out_specs + [ANY],
        out_shape=out_shape + [jax.ShapeDtypeStruct(ag[0].shape, ag[0].dtype)],
        scratch_shapes=[pltpu.SemaphoreType.DMA((n,))] * 2, input_output_aliases={4: 3},
        compiler_params=_params(("arbitrary", "arbitrary")),
    )(*ins, ag[0])


def _fox_gate_bwd(dog, o, proj, lse, qa, swap=None):
    lp = o.shape[0]
    tr = LANES
    steps = lp // tr

    def body(d_ref, o_ref, g_ref, lse_ref, q_ref, *rest):
        if swap is None:
            do_ref, q2_ref, dgate_ref = rest
        else:
            src_ref, do_ref, q2_ref, dgate_ref, got_ref, send_sem, recv_sem = rest
            cp = _swap_copy(src_ref, got_ref, send_sem, recv_sem)

            @pl.when(pl.program_id(0) == 0)
            def _():
                cp.start()

            @pl.when(pl.program_id(0) == steps - 1)
            def _():
                cp.wait()

        down, _ = _head_sel(FOX_H, FOX_DH)
        sg = _sigmoid(g_ref[...])
        dv, ov = d_ref[...], o_ref[...]
        do = (dv * sg).astype(BF16).astype(F32)
        dgate_ref[...] = dv * ov * sg * (1.0 - sg)
        delta = _sel_r(do * ov, down)
        _spread(do, -_placed(delta, L_C), do_ref)
        r_, c_ = _iota((D, LANES), 0), _iota((D, LANES), 1)
        lse_c = _sel_r(lse_ref[...], (r_ == c_ * FOX_DH).astype(BF16))
        q2_ref[...] = (q_ref[...].astype(F32) - _placed(lse_c, L_LSE)).astype(BF16)

    spec = pl.BlockSpec((tr, D), lambda i: (i, 0))
    aug = pl.BlockSpec((tr, FOX_AUG), lambda i: (i, 0))
    in_specs = [spec, spec, pl.BlockSpec((tr, D), lambda i: (i, 3)), spec, aug]
    out_specs = [aug, aug, spec]
    out_shape = [jax.ShapeDtypeStruct((lp, FOX_AUG), BF16), jax.ShapeDtypeStruct((lp, FOX_AUG), BF16),
                 jax.ShapeDtypeStruct((lp, D), F32)]
    if swap is None:
        return pl.pallas_call(body, name="fox_gate_bwd", grid=(steps,), in_specs=in_specs, out_specs=out_specs,
                              out_shape=out_shape, compiler_params=_params(("parallel",)))(dog, o, proj, lse, qa)
    nb, r, c = swap.shape
    return pl.pallas_call(
        body, name="fox_gate_bwd_swap", grid=(steps,), in_specs=in_specs + [ANY], out_specs=out_specs + [ANY],
        out_shape=out_shape + [jax.ShapeDtypeStruct((nb, r // 2, c), swap.dtype)],
        scratch_shapes=[pltpu.SemaphoreType.DMA, pltpu.SemaphoreType.DMA], compiler_params=_params(("arbitrary",)),
    )(dog, o, proj, lse, qa, swap)


def _fox_attn_bwd(q2, ka, va, doa, rs=None):
    lp = q2.shape[0]
    t = _tile(lp, FOX_TQ)
    nb = lp // t
    npair = FOX_H // 2

    def body(q_ref, k_ref, v_ref, do_ref, *rest):
        if rs is None:
            dq_ref, dk_ref, dv_ref, dc_ref, dq_acc, dk_acc, dv_acc, dc_acc = rest
        else:
            s_ref, dq_ref, dk_ref, dv_ref, dc_ref, got_ref, dq_acc, dk_acc, dv_acc, dc_acc, send_sems, recv_sems = rest
            sends, arrivals = _scatter_copies(s_ref, got_ref, send_sems, recv_sems)

            @pl.when((pl.program_id(0) == 0) & (pl.program_id(1) == 0))
            def _():
                for cp in sends:
                    cp.start()

            @pl.when((pl.program_id(0) == npair - 1) & (pl.program_id(1) == nb - 1))
            def _():
                for cp in arrivals:
                    cp.wait_recv()
                for cp in sends:
                    cp.wait_send()

        j = pl.program_id(1)

        @pl.when(j == 0)
        def _():
            dq_acc[...] = jnp.zeros_like(dq_acc)

        causal = _iota((t, t), 1) <= _iota((t, t), 0)
        ks = [k_ref[:, hh * LANES:(hh + 1) * LANES] for hh in range(2)]
        vs = [v_ref[:, hh * LANES:(hh + 1) * LANES] for hh in range(2)]
        dk_acc[...] = jnp.zeros_like(dk_acc)
        dv_acc[...] = jnp.zeros_like(dv_acc)
        dc_acc[...] = jnp.zeros_like(dc_acc)

        def block(i, diag):
            off = pl.multiple_of(i * t, t)
            for hh in range(2):
                q = q_ref[pl.ds(off, t), hh * LANES:(hh + 1) * LANES]
                do = do_ref[pl.ds(off, t), hh * LANES:(hh + 1) * LANES]
                s = nt(q, ks[hh])
                if diag:
                    s = jnp.where(causal, s, -1e30)
                p = jnp.exp(s)
                ds = p * nt(do, vs[hh])
                dc_acc[hh] += jnp.sum(ds, axis=0, keepdims=True)
                dsb = ds.astype(BF16)
                dv_acc[hh] += tn(p.astype(BF16), do)
                dk_acc[hh] += tn(dsb, q)
                dq_acc[hh, pl.ds(off, t), :] += nn(dsb, ks[hh])

        block(j, True)
        below = nb - 1 - j

        def step(u, c):
            block(j + 1 + 2 * u, False)
            block(j + 2 + 2 * u, False)
            return c

        lax.fori_loop(0, below // 2, step, 0)

        @pl.when(below % 2 == 1)
        def _():
            block(nb - 1, False)
        left = _iota((t, LANES), 1) < FOX_DH
        dk_ref[...] = jnp.where(left, dk_acc[0], pltpu.roll(dk_acc[1], FOX_DH, 1))
        dv_ref[...] = jnp.where(left, dv_acc[0], pltpu.roll(dv_acc[1], FOX_DH, 1))
        for hh in range(2):
            dc_ref[hh] = jnp.broadcast_to(-dc_acc[hh], (8, t))

        @pl.when(j == nb - 1)
        def _():
            left = _iota((lp, LANES), 1) < FOX_DH
            dq_ref[...] = jnp.where(left, dq_acc[0], pltpu.roll(dq_acc[1], FOX_DH, 1))

    full = pl.BlockSpec((lp, 2 * LANES), lambda p, j: (0, p))
    kblk = pl.BlockSpec((t, 2 * LANES), lambda p, j: (j, p))
    oblk = pl.BlockSpec((t, LANES), lambda p, j: (j, p))
    in_specs = [full, kblk, kblk, full]
    out_specs = [pl.BlockSpec((lp, LANES), lambda p, j: (0, p)), oblk, oblk, pl.BlockSpec((2, 8, t), lambda p, j: (p, 0, j))]
    out_shape = [jax.ShapeDtypeStruct((lp, D), F32)] * 3 + [jax.ShapeDtypeStruct((FOX_H, 8, lp), F32)]
    scratch = [pltpu.VMEM((2, lp, LANES), F32), pltpu.VMEM((2, t, LANES), F32), pltpu.VMEM((2, t, LANES), F32),
               pltpu.VMEM((2, 1, t), F32)]
    if rs is None:
        return pl.pallas_call(body, name="fox_attn_bwd", grid=(npair, nb), in_specs=in_specs, out_specs=out_specs,
                              out_shape=out_shape, scratch_shapes=scratch,
                              compiler_params=_params(("parallel", "arbitrary")))(q2, ka, va, doa)
    return pl.pallas_call(
        body, name="fox_attn_bwd_rs", grid=(npair, nb), in_specs=in_specs + [ANY], out_specs=out_specs + [ANY],
        out_shape=out_shape + [jax.ShapeDtypeStruct((3,) + rs.shape[1:], rs.dtype)],
        scratch_shapes=scratch + [pltpu.SemaphoreType.DMA((3,)), pltpu.SemaphoreType.DMA((3,))],
        compiler_params=_params(("arbitrary", "arbitrary")),
    )(q2, ka, va, doa, rs)


def _fox_prep_bwd(proj, b_f, q_gain, k_gain, dqn, dkn, dv, dgate, dct):
    lp = proj.shape[0]
    nb = lp // LANES

    def body(p_ref, bf_ref, qg_ref, kg_ref, dq_ref, dk_ref, dv_ref, dg_ref, dc_ref,
             dp_ref, dqg_ref, dkg_ref, dbf_ref, carry):
        i = pl.program_id(0)

        @pl.when(i == 0)
        def _():
            carry[...] = jnp.zeros_like(carry)
            dqg_ref[...] = jnp.zeros_like(dqg_ref)
            dkg_ref[...] = jnp.zeros_like(dkg_ref)
            dbf_ref[...] = jnp.zeros_like(dbf_ref)

        down, up = _head_sel(FOX_H, FOX_DH)

        def norm_bwd(x, gain, dy, scale, dgain_ref):
            ms = _sel_r2(x * x, down) * (1.0 / FOX_DH)
            r = _sel_r2(lax.rsqrt(ms + EPS), up)
            u = dy * gain * scale
            mean_xu = _sel_r2(_sel_r2(x * u, down) * (1.0 / FOX_DH), up)
            dgain_ref[...] += jnp.sum(dy * scale * x * r, axis=0, keepdims=True)
            return r * u - x * (r * r * r) * mean_xu

        dp_ref[:, 0:D] = norm_bwd(p_ref[:, 0:D], qg_ref[...], dq_ref[...], FOX_DH ** -0.5, dqg_ref).astype(BF16)
        dp_ref[:, D:2 * D] = norm_bwd(p_ref[:, D:2 * D], kg_ref[...], dk_ref[...], 1.0, dkg_ref).astype(BF16)
        dp_ref[:, 2 * D:3 * D] = dv_ref[...].astype(BF16)
        dp_ref[:, 3 * D:4 * D] = dg_ref[...].astype(BF16)
        rows = jnp.concatenate([dc_ref[h, 0:1, :] for h in range(FOX_H)] + [jnp.zeros((LANES - FOX_H, LANES), F32)], axis=0)
        dlf = _sel_l(_tri(LANES, upper=True).astype(BF16), rows.T) + carry[0:1, :]
        carry[...] = jnp.broadcast_to(dlf[0:1, :], carry.shape)
        lane = _iota((LANES, LANES), 1)
        z = p_ref[:, 4 * D:4 * D + LANES] + bf_ref[...]
        df = jnp.where(lane < FOX_H, dlf * _sigmoid(-z), 0.0)
        dp_ref[:, 4 * D:4 * D + LANES] = df.astype(BF16)
        dbf_ref[...] += jnp.sum(df, axis=0, keepdims=True)

    rev = lambda i: (nb - 1 - i, 0)
    blk = pl.BlockSpec((LANES, D), rev)
    row = pl.BlockSpec((1, D), lambda i: (0, 0))
    row128 = pl.BlockSpec((1, LANES), lambda i: (0, 0))
    return pl.pallas_call(
        body, name="fox_prep_bwd", grid=(nb,),
        in_specs=[pl.BlockSpec((LANES, FOX_INP), rev), row128, row, row, blk, blk, blk, blk,
                  pl.BlockSpec((FOX_H, 8, LANES), lambda i: (0, 0, nb - 1 - i))],
        out_specs=[pl.BlockSpec((LANES, FOX_INP), rev), row, row, row128],
        out_shape=[jax.ShapeDtypeStruct((lp, FOX_INP), BF16), jax.ShapeDtypeStruct((1, D), F32),
                   jax.ShapeDtypeStruct((1, D), F32), jax.ShapeDtypeStruct((1, LANES), F32)],
        scratch_shapes=[pltpu.VMEM((8, LANES), F32)],
        compiler_params=_params(("arbitrary",)),
    )(proj, jnp.pad(b_f, (0, LANES - FOX_H)).reshape(1, LANES), jnp.tile(q_gain, FOX_H).reshape(1, D),
      jnp.tile(k_gain, FOX_H).reshape(1, D), dqn, dkn, dv, dgate, dct)


def _gla_gates(p_ref, wa_ref, ba_ref):
    a_lr = p_ref[:, 3072:3072 + LANES]
    z = nn(a_lr.astype(BF16), wa_ref[...].astype(BF16)) + ba_ref[...]
    g = _log_sigmoid(z) * (1.0 / GLA_NORM)
    b = _sel_l(_tri(CHUNK).astype(BF16), g)
    return a_lr, z, b


def _gla_chunk_fwd(q, k, v, b, st0):
    hs = range(len(q))
    low = _tri(CHUNK)
    bl = [b[h][CHUNK - 1:CHUNK, :] for h in hs]
    qe = [q[h] * jnp.exp(b[h]) for h in hs]
    ke = [k[h] * jnp.exp(-b[h]) for h in hs]
    kd = [k[h] * jnp.exp(bl[h] - b[h]) for h in hs]
    a = [jnp.where(low, nt(qe[h], ke[h]), 0.0) for h in hs]
    o = [nn(a[h], v[h]) + nt(qe[h], st0[h]) for h in hs]
    st1 = [st0[h] * jnp.exp(bl[h]) + tn(v[h], kd[h]) for h in hs]
    return o, st1, (qe, ke, kd, a, bl)


def _gla_slices(p_ref, b_all, h):
    q = p_ref[:, h * GLA_DK:(h + 1) * GLA_DK] * (GLA_DK ** -0.5)
    k = p_ref[:, GLA_QK + h * GLA_DK:GLA_QK + (h + 1) * GLA_DK]
    v = p_ref[:, 2 * GLA_QK + h * GLA_DV:2 * GLA_QK + (h + 1) * GLA_DV]
    r = p_ref[:, 2 * GLA_QK + GLA_V + h * GLA_DV:2 * GLA_QK + GLA_V + (h + 1) * GLA_DV]
    return q, k, v, r, b_all[:, h * GLA_DK:(h + 1) * GLA_DK]


def _gla_fwd(proj, w_alpha2, b_alpha, o_gain):
    lp = proj.shape[0]
    nc = lp // CHUNK

    def body(p_ref, wa_ref, ba_ref, og_ref, o_ref, y_ref, s_ref, st):
        @pl.when(pl.program_id(0) == 0)
        def _():
            st[...] = jnp.zeros_like(st)

        _, _, b_all = _gla_gates(p_ref, wa_ref, ba_ref)
        hs = range(GLA_H)
        parts = [_gla_slices(p_ref, b_all, h) for h in hs]
        st0 = [st[h] for h in hs]
        for h in hs:
            s_ref[0, h] = st0[h]
        o, st1, _ = _gla_chunk_fwd([p[0] for p in parts], [p[1] for p in parts], [p[2] for p in parts],
                                   [p[4] for p in parts], st0)
        for h in hs:
            st[h] = st1[h]
            o_ref[:, h * GLA_DV:(h + 1) * GLA_DV] = o[h]
            rs = lax.rsqrt(jnp.mean(o[h] * o[h], axis=-1, keepdims=True) + EPS)
            y_ref[:, h * GLA_DV:(h + 1) * GLA_DV] = (o[h] * rs * og_ref[...] * _silu(parts[h][3])).astype(BF16)

    blk = pl.BlockSpec((CHUNK, D), lambda i: (i, 0))
    return pl.pallas_call(
        body, name="gla_fwd", grid=(nc,),
        in_specs=[pl.BlockSpec((CHUNK, GLA_INP), lambda i: (i, 0)), pl.BlockSpec((LANES, GLA_QK), lambda i: (0, 0)),
                  pl.BlockSpec((1, GLA_QK), lambda i: (0, 0)), pl.BlockSpec((1, GLA_DV), lambda i: (0, 0))],
        out_specs=[blk, blk, pl.BlockSpec((1, GLA_H, GLA_DV, GLA_DK), lambda i: (i, 0, 0, 0))],
        out_shape=[jax.ShapeDtypeStruct((lp, D), F32), jax.ShapeDtypeStruct((lp, D), BF16),
                   jax.ShapeDtypeStruct((nc, GLA_H, GLA_DV, GLA_DK), F32)],
        scratch_shapes=[pltpu.VMEM((GLA_H, GLA_DV, GLA_DK), F32)],
        compiler_params=_params(("arbitrary",)),
    )(proj, jnp.pad(w_alpha2, ((0, LANES - GLA_RANK), (0, 0))), b_alpha.reshape(1, GLA_QK), o_gain.reshape(1, GLA_DV))


def _gla_bwd(proj, w_alpha2, b_alpha, o_gain, o, states, dy):
    lp = proj.shape[0]
    nc = lp // CHUNK

    def body(p_ref, wa_ref, ba_ref, og_ref, o_ref, s_ref, dy_ref, dp_ref, dwa_ref, dba_ref, dog_ref, dst):
        @pl.when(pl.program_id(0) == 0)
        def _():
            dst[...] = jnp.zeros_like(dst)
            dwa_ref[...] = jnp.zeros_like(dwa_ref)
            dba_ref[...] = jnp.zeros_like(dba_ref)
            dog_ref[...] = jnp.zeros_like(dog_ref)

        a_lr, z, b_all = _gla_gates(p_ref, wa_ref, ba_ref)
        last_row = _iota((CHUNK, GLA_DK), 0) == CHUNK - 1
        rev = _tri(CHUNK, upper=True).astype(BF16)
        hs = range(GLA_H)
        scale = GLA_DK ** -0.5
        parts = [_gla_slices(p_ref, b_all, h) for h in hs]
        q, k, v, b = [p[0] for p in parts], [p[1] for p in parts], [p[2] for p in parts], [p[4] for p in parts]
        st0 = [s_ref[0, h] for h in hs]
        dst1 = [dst[h] for h in hs]
        do = []
        for h in hs:
            r = parts[h][3]
            ov = o_ref[:, h * GLA_DV:(h + 1) * GLA_DV]
            dyv = dy_ref[:, h * GLA_DV:(h + 1) * GLA_DV]
            rs = lax.rsqrt(jnp.mean(ov * ov, axis=-1, keepdims=True) + EPS)
            on = ov * rs
            dp_ref[:, 2 * GLA_QK + GLA_V + h * GLA_DV:2 * GLA_QK + GLA_V + (h + 1) * GLA_DV] = (
                dyv * on * og_ref[...] * _dsilu(r)).astype(BF16)
            don = dyv * _silu(r)
            dog_ref[...] += jnp.sum(don * on, axis=0, keepdims=True)
            u = don * og_ref[...]
            do.append(rs * u - ov * (rs * rs * rs) * jnp.mean(ov * u, axis=-1, keepdims=True))
        _, _, (qe, ke, kd, a, bl) = _gla_chunk_fwd(q, k, v, b, st0)
        low = _tri(CHUNK)
        da = [jnp.where(low, nt(do[h], v[h]), 0.0) for h in hs]
        dkd = [nn(v[h], dst1[h]) for h in hs]
        dvv = [tn(a[h], do[h]) + nt(kd[h], dst1[h]) for h in hs]
        dqe = [nn(da[h], ke[h]) + nn(do[h], st0[h]) for h in hs]
        dke = [tn(da[h], qe[h]) for h in hs]
        dg_parts = []
        for h in hs:
            ebl = jnp.exp(bl[h])
            dst[h] = dst1[h] * ebl + tn(do[h], qe[h])
            db = dqe[h] * qe[h] - dke[h] * ke[h] - dkd[h] * kd[h]
            db_last = (jnp.sum(dkd[h] * kd[h], axis=0, keepdims=True)
                       + jnp.sum(dst1[h] * st0[h], axis=0, keepdims=True) * ebl)
            db = db + jnp.where(last_row, db_last, 0.0)
            dg_parts.append(_sel_l(rev, db))
            dp_ref[:, h * GLA_DK:(h + 1) * GLA_DK] = (dqe[h] * jnp.exp(b[h]) * scale).astype(BF16)
            dp_ref[:, GLA_QK + h * GLA_DK:GLA_QK + (h + 1) * GLA_DK] = (
                dke[h] * jnp.exp(-b[h]) + dkd[h] * jnp.exp(bl[h] - b[h])).astype(BF16)
            dp_ref[:, 2 * GLA_QK + h * GLA_DV:2 * GLA_QK + (h + 1) * GLA_DV] = dvv[h].astype(BF16)
        dg = jnp.concatenate(dg_parts, axis=1)
        dz = dg * (1.0 / GLA_NORM) * _sigmoid(-z)
        dzb = dz.astype(BF16)
        dp_ref[:, 3072:3072 + LANES] = nt(dzb, wa_ref[...].astype(BF16)).astype(BF16)
        dwa_ref[...] += tn(a_lr.astype(BF16), dzb)
        dba_ref[...] += jnp.sum(dz, axis=0, keepdims=True)

    rv = lambda i: (nc - 1 - i, 0)
    blk = pl.BlockSpec((CHUNK, D), rv)
    fixed = lambda r, c: pl.BlockSpec((r, c), lambda i: (0, 0))
    return pl.pallas_call(
        body, name="gla_bwd", grid=(nc,),
        in_specs=[pl.BlockSpec((CHUNK, GLA_INP), rv), fixed(LANES, GLA_QK), fixed(1, GLA_QK), fixed(1, GLA_DV), blk,
                  pl.BlockSpec((1, GLA_H, GLA_DV, GLA_DK), lambda i: (nc - 1 - i, 0, 0, 0)), blk],
        out_specs=[pl.BlockSpec((CHUNK, GLA_INP), rv), fixed(LANES, GLA_QK), fixed(1, GLA_QK), fixed(1, GLA_DV)],
        out_shape=[jax.ShapeDtypeStruct((lp, GLA_INP), BF16), jax.ShapeDtypeStruct((LANES, GLA_QK), F32),
                   jax.ShapeDtypeStruct((1, GLA_QK), F32), jax.ShapeDtypeStruct((1, GLA_DV), F32)],
        scratch_shapes=[pltpu.VMEM((GLA_H, GLA_DV, GLA_DK), F32)],
        compiler_params=_params(("arbitrary",)),
    )(proj, jnp.pad(w_alpha2, ((0, LANES - GLA_RANK), (0, 0))), b_alpha.reshape(1, GLA_QK), o_gain.reshape(1, GLA_DV),
      o, states, dy)


HI = lax.Precision.HIGH


def _gdn_pre(prev_ref, p_ref, cw_ref, al_ref, dt_ref):
    xc = jnp.concatenate([prev_ref[:, 0:GDN_CONV], p_ref[:, 0:GDN_CONV]], axis=0)
    shifted = [pltpu.roll(xc, 3 - j, 0)[CHUNK:, :] if j < 3 else xc[CHUNK:, :] for j in range(4)]
    conv = sum(shifted[j] * cw_ref[j:j + 1, :] for j in range(4))
    act = _silu(conv)
    slab = p_ref[:, 4096:4096 + LANES]
    lane = _iota((CHUNK, LANES), 1)
    zs = slab + dt_ref[...]
    g = jnp.where(lane < GDN_H, -jnp.exp(al_ref[...]) * _softplus(zs), 0.0)
    bs = _sel_l(_tri(CHUNK).astype(BF16), g)
    beta = _sigmoid(slab)
    return shifted, conv, act, slab, zs, g, bs, beta


def _l2n(x):
    r = lax.rsqrt(jnp.sum(x * x, axis=-1, keepdims=True) + EPS)
    return x * r, r


def _gdn_chunk_fwd(q, k, v, beta, bcol, brow, s0, tinv=None):
    hs = range(len(q))
    ii, jj = _iota((CHUNK, CHUNK), 0), _iota((CHUNK, CHUNK), 1)
    low, eye = ii >= jj, (ii == jj).astype(F32)
    dm = [jnp.where(low, jnp.exp(jnp.where(low, bcol[h] - brow[h], 0.0)), 0.0) for h in hs]
    dstrict = [jnp.where(ii > jj, dm[h], 0.0) for h in hs]
    eb = [jnp.exp(bcol[h]) for h in hs]
    bl = [bcol[h][CHUNK - 1:CHUNK, :] for h in hs]
    kb = [k[h] * beta[h] for h in hs]
    vb = [v[h] * beta[h] for h in hs]
    nmat = [nt(kb[h], k[h]) * dstrict[h] for h in hs]
    if tinv is None:
        x = [eye - nmat[h] for h in hs]
        pw = [nn(nmat[h], nmat[h], precision=HI) for h in hs]
        for it in range(5):
            x = [x[h] + nn(x[h], pw[h], precision=HI) for h in hs]
            if it < 4:
                pw = [nn(pw[h], pw[h], precision=HI) for h in hs]
    else:
        x = tinv
    kbe = [kb[h] * eb[h] for h in hs]
    u = [nn(x[h], vb[h], precision=HI) for h in hs]
    w = [nn(x[h], kbe[h], precision=HI) for h in hs]
    vn = [u[h] - nn(w[h], s0[h]) for h in hs]
    pm = [nt(q[h], k[h]) * dm[h] for h in hs]
    qe = [q[h] * eb[h] for h in hs]
    o = [nn(pm[h], vn[h]) + nn(qe[h], s0[h]) for h in hs]
    kd = [k[h] * jnp.exp(bl[h] - bcol[h]) for h in hs]
    s1 = [s0[h] * jnp.exp(bl[h]) + tn(kd[h], vn[h]) for h in hs]
    return o, s1, dict(dm=dm, dstrict=dstrict, eb=eb, bl=bl, kb=kb, vb=vb, nmat=nmat, tinv=x, kbe=kbe, u=u, w=w, vn=vn,
                       pm=pm, qe=qe, kd=kd)


def _gdn_heads(act, beta_slab, bs, h):
    qa = act[:, h * GDN_DK:(h + 1) * GDN_DK]
    ka = act[:, GDN_H * GDN_DK + h * GDN_DK:GDN_H * GDN_DK + (h + 1) * GDN_DK]
    v = act[:, 2 * GDN_H * GDN_DK + h * GDN_DV:2 * GDN_H * GDN_DK + (h + 1) * GDN_DV]
    return qa, ka, v, beta_slab[:, GDN_H + h:GDN_H + h + 1], bs[:, h:h + 1]


def _gdn_fwd(proj, conv_w, a_log, dt_bias, o_gain):
    lp = proj.shape[0]
    nc = lp // CHUNK

    def body(prev_ref, p_ref, cw_ref, al_ref, dt_ref, og_ref, o_ref, y_ref, s_ref, t_ref, st):
        @pl.when(pl.program_id(0) == 0)
        def _():
            st[...] = jnp.zeros_like(st)

        _, _, act, _, _, _, bs, beta = _gdn_pre(prev_ref, p_ref, cw_ref, al_ref, dt_ref)
        bst = bs.T
        hs = range(GDN_H)
        parts = [_gdn_heads(act, beta, bs, h) for h in hs]
        q = [_l2n(parts[h][0])[0] * (GDN_DK ** -0.5) for h in hs]
        k = [_l2n(parts[h][1])[0] for h in hs]
        s0 = [st[h] for h in hs]
        for h in hs:
            s_ref[0, h] = s0[h]
        o, s1, f = _gdn_chunk_fwd(q, k, [parts[h][2] for h in hs], [parts[h][3] for h in hs], [parts[h][4] for h in hs],
                                  [bst[h:h + 1, :] for h in hs], s0)
        for h in hs:
            t_ref[0, h] = f["tinv"][h]
            st[h] = s1[h]
            o_ref[:, h * GDN_DV:(h + 1) * GDN_DV] = o[h]
            rs = lax.rsqrt(jnp.mean(o[h] * o[h], axis=-1, keepdims=True) + EPS)
            gate = p_ref[:, GDN_CONV + h * GDN_DV:GDN_CONV + (h + 1) * GDN_DV]
            y_ref[:, h * GDN_DV:(h + 1) * GDN_DV] = (o[h] * rs * og_ref[...] * _silu(gate)).astype(BF16)

    blk = pl.BlockSpec((CHUNK, D), lambda i: (i, 0))
    fixed = lambda r, c: pl.BlockSpec((r, c), lambda i: (0, 0))
    return pl.pallas_call(
        body, name="gdn_fwd", grid=(nc,),
        in_specs=[pl.BlockSpec((CHUNK, GDN_INP), lambda i: (jnp.maximum(i - 1, 0), 0)),
                  pl.BlockSpec((CHUNK, GDN_INP), lambda i: (i, 0)), fixed(8, GDN_CONV), fixed(1, LANES), fixed(1, LANES),
                  fixed(1, GDN_DV)],
        out_specs=[blk, blk, pl.BlockSpec((1, GDN_H, GDN_DK, GDN_DV), lambda i: (i, 0, 0, 0)),
                   pl.BlockSpec((1, GDN_H, CHUNK, CHUNK), lambda i: (i, 0, 0, 0))],
        out_shape=[jax.ShapeDtypeStruct((lp, D), F32), jax.ShapeDtypeStruct((lp, D), BF16),
                   jax.ShapeDtypeStruct((nc, GDN_H, GDN_DK, GDN_DV), F32), jax.ShapeDtypeStruct((nc, GDN_H, CHUNK, CHUNK), F32)],
        scratch_shapes=[pltpu.VMEM((GDN_H, GDN_DK, GDN_DV), F32)],
        compiler_params=_params(("arbitrary",)),
    )(proj, proj, jnp.pad(conv_w.reshape(4, GDN_CONV), ((0, 4), (0, 0))), jnp.pad(a_log, (0, LANES - GDN_H)).reshape(1, LANES),
      jnp.pad(dt_bias, (0, LANES - GDN_H)).reshape(1, LANES), o_gain.reshape(1, GDN_DV))


def _gdn_bwd(proj, conv_w, a_log, dt_bias, o_gain, o, states, tinvs, dy):
    lp = proj.shape[0]
    nc = lp // CHUNK

    def body(prev_ref, p_ref, cw_ref, al_ref, dt_ref, og_ref, o_ref, s_ref, t_ref, dy_ref,
             dp_ref, dcw_ref, dal_ref, ddt_ref, dog_ref, dst, dconv_next):
        @pl.when(pl.program_id(0) == 0)
        def _():
            dst[...] = jnp.zeros_like(dst)
            dconv_next[...] = jnp.zeros_like(dconv_next)
            dcw_ref[...] = jnp.zeros_like(dcw_ref)
            dal_ref[...] = jnp.zeros_like(dal_ref)
            ddt_ref[...] = jnp.zeros_like(ddt_ref)
            dog_ref[...] = jnp.zeros_like(dog_ref)

        shifted, conv, act, slab, zs, g, bs, beta = _gdn_pre(prev_ref, p_ref, cw_ref, al_ref, dt_ref)
        bst = bs.T
        lane = _iota((CHUNK, LANES), 1)
        ones = jnp.ones((CHUNK, LANES), F32)
        db_slab = jnp.zeros((CHUNK, LANES), F32)
        dbeta_slab = jnp.zeros((CHUNK, LANES), F32)
        last_row = _iota((CHUNK, 1), 0) == CHUNK - 1
        hs = range(GDN_H)
        scale = GDN_DK ** -0.5
        parts = [_gdn_heads(act, beta, bs, h) for h in hs]
        qa, ka, v = [parts[h][0] for h in hs], [parts[h][1] for h in hs], [parts[h][2] for h in hs]
        bet, bcol = [parts[h][3] for h in hs], [parts[h][4] for h in hs]
        qn_ = [_l2n(qa[h]) for h in hs]
        kn_ = [_l2n(ka[h]) for h in hs]
        q = [qn_[h][0] * scale for h in hs]
        k, rq, rk = [kn_[h][0] for h in hs], [qn_[h][1] for h in hs], [kn_[h][1] for h in hs]
        s0 = [s_ref[0, h] for h in hs]
        ds1 = [dst[h] for h in hs]
        do = []
        for h in hs:
            ov = o_ref[:, h * GDN_DV:(h + 1) * GDN_DV]
            dyv = dy_ref[:, h * GDN_DV:(h + 1) * GDN_DV]
            gate = p_ref[:, GDN_CONV + h * GDN_DV:GDN_CONV + (h + 1) * GDN_DV]
            rs = lax.rsqrt(jnp.mean(ov * ov, axis=-1, keepdims=True) + EPS)
            on = ov * rs
            dp_ref[:, GDN_CONV + h * GDN_DV:GDN_CONV + (h + 1) * GDN_DV] = (dyv * on * og_ref[...] * _dsilu(gate)).astype(BF16)
            don = dyv * _silu(gate)
            dog_ref[...] += jnp.sum(don * on, axis=0, keepdims=True)
            uu = don * og_ref[...]
            do.append(rs * uu - ov * (rs * rs * rs) * jnp.mean(ov * uu, axis=-1, keepdims=True))
        _, _, f = _gdn_chunk_fwd(q, k, v, bet, bcol, [bst[h:h + 1, :] for h in hs], s0, tinv=[t_ref[0, h] for h in hs])
        dm, dstrict, eb, bl, kb, nmat, tinv = f["dm"], f["dstrict"], f["eb"], f["bl"], f["kb"], f["nmat"], f["tinv"]
        kbe, u, w, vn, pm, qe, kd = f["kbe"], f["u"], f["w"], f["vn"], f["pm"], f["qe"], f["kd"]
        ebl = [jnp.exp(bl[h]) for h in hs]
        dvn = [tn(pm[h], do[h]) + nn(kd[h], ds1[h]) for h in hs]
        dpr = [nt(do[h], vn[h]) for h in hs]
        dqe = [nt(do[h], s0[h]) for h in hs]
        dkd = [nt(vn[h], ds1[h]) for h in hs]
        for h in hs:
            dst[h] = ds1[h] * ebl[h] + tn(qe[h], do[h]) - tn(w[h], dvn[h])
        du_ = [tn(tinv[h], dvn[h], precision=HI) for h in hs]
        dw_ = [tn(tinv[h], -nt(dvn[h], s0[h]), precision=HI) for h in hs]
        dn = [-(nt(du_[h], u[h]) + nt(dw_[h], w[h])) for h in hs]
        dqk = [dpr[h] * dm[h] for h in hs]
        dkk = [dn[h] * dstrict[h] for h in hs]
        gsum = [dpr[h] * pm[h] + dn[h] * nmat[h] for h in hs]
        dkb = [nn(dkk[h], k[h]) + dw_[h] * eb[h] for h in hs]
        dk = [tn(dkk[h], kb[h]) + tn(dqk[h], q[h]) + dkd[h] * jnp.exp(bl[h] - bcol[h]) + dkb[h] * bet[h] for h in hs]
        dq = [nn(dqk[h], k[h]) + dqe[h] * eb[h] for h in hs]
        colsum = [tn(gsum[h], ones, precision=HI)[:, 0:1] for h in hs]
        dact_q, dact_k, dact_v = [], [], []
        for h in hs:
            dbeta = jnp.sum(dkb[h] * k[h], axis=-1, keepdims=True) + jnp.sum(du_[h] * v[h], axis=-1, keepdims=True)
            skd = jnp.sum(dkd[h] * kd[h], axis=-1, keepdims=True)
            db = (jnp.sum(gsum[h], axis=-1, keepdims=True) - colsum[h] + jnp.sum(dqe[h] * qe[h], axis=-1, keepdims=True)
                  + jnp.sum(dw_[h] * kbe[h], axis=-1, keepdims=True) - skd)
            db_last = jnp.sum(skd, axis=0, keepdims=True) + jnp.sum(ds1[h] * s0[h]) * ebl[h]
            db = db + jnp.where(last_row, db_last, 0.0)
            db_slab = db_slab + jnp.where(lane == h, db, 0.0)
            dbeta_slab = dbeta_slab + jnp.where(lane == GDN_H + h, dbeta, 0.0)
            dqn = dq[h] * scale
            dact_q.append(rq[h] * dqn - qa[h] * (rq[h] * rq[h] * rq[h]) * jnp.sum(qa[h] * dqn, axis=-1, keepdims=True))
            dact_k.append(rk[h] * dk[h] - ka[h] * (rk[h] * rk[h] * rk[h]) * jnp.sum(ka[h] * dk[h], axis=-1, keepdims=True))
            dact_v.append(du_[h] * bet[h])
        dact = jnp.concatenate(dact_q + dact_k + dact_v, axis=1)
        dconv = dact * _dsilu(conv)
        for j in range(4):
            dcw_ref[j:j + 1, :] += jnp.sum(dconv * shifted[j], axis=0, keepdims=True)
        dcat = jnp.concatenate([dconv, dconv_next[...]], axis=0)
        dx = dconv * cw_ref[3:4, :]
        for j in range(3):
            dx = dx + pltpu.roll(dcat, 2 * CHUNK - (3 - j), 0)[:CHUNK, :] * cw_ref[j:j + 1, :]
        dconv_next[...] = dconv
        dp_ref[:, 0:GDN_CONV] = dx.astype(BF16)
        dg = _sel_l(_tri(CHUNK, upper=True).astype(BF16), db_slab)
        da = dg * (-jnp.exp(al_ref[...])) * _sigmoid(zs)
        da = jnp.where(lane < GDN_H, da, 0.0)
        dal_ref[...] += jnp.sum(dg * g, axis=0, keepdims=True)
        ddt_ref[...] += jnp.sum(da, axis=0, keepdims=True)
        dp_ref[:, 4096:4096 + LANES] = (da + dbeta_slab * beta * (1.0 - beta)).astype(BF16)

    rv = lambda i: (nc - 1 - i, 0)
    blk = pl.BlockSpec((CHUNK, D), rv)
    fixed = lambda r, c: pl.BlockSpec((r, c), lambda i: (0, 0))
    return pl.pallas_call(
        body, name="gdn_bwd", grid=(nc,),
        in_specs=[pl.BlockSpec((CHUNK, GDN_INP), lambda i: (jnp.maximum(nc - 2 - i, 0), 0)),
                  pl.BlockSpec((CHUNK, GDN_INP), rv), fixed(8, GDN_CONV), fixed(1, LANES), fixed(1, LANES), fixed(1, GDN_DV),
                  blk, pl.BlockSpec((1, GDN_H, GDN_DK, GDN_DV), lambda i: (nc - 1 - i, 0, 0, 0)),
                  pl.BlockSpec((1, GDN_H, CHUNK, CHUNK), lambda i: (nc - 1 - i, 0, 0, 0)), blk],
        out_specs=[pl.BlockSpec((CHUNK, GDN_INP), rv), fixed(8, GDN_CONV), fixed(1, LANES), fixed(1, LANES), fixed(1, GDN_DV)],
        out_shape=[jax.ShapeDtypeStruct((lp, GDN_INP), BF16), jax.ShapeDtypeStruct((8, GDN_CONV), F32),
                   jax.ShapeDtypeStruct((1, LANES), F32), jax.ShapeDtypeStruct((1, LANES), F32),
                   jax.ShapeDtypeStruct((1, GDN_DV), F32)],
        scratch_shapes=[pltpu.VMEM((GDN_H, GDN_DK, GDN_DV), F32), pltpu.VMEM((CHUNK, GDN_CONV), F32)],
        compiler_params=_params(("arbitrary",)),
    )(proj, proj, jnp.pad(conv_w.reshape(4, GDN_CONV), ((0, 4), (0, 0))), jnp.pad(a_log, (0, LANES - GDN_H)).reshape(1, LANES),
      jnp.pad(dt_bias, (0, LANES - GDN_H)).reshape(1, LANES), o_gain.reshape(1, GDN_DV), o, states, tinvs, dy)


def _coords():
    return lax.axis_index("x"), lax.axis_index("y"), lax.axis_index("c")


def _other_chips(x, y):
    return [(1 - x, y, 2 * (1 - x) + y), (x, 1 - y, 2 * x + 1 - y), (1 - x, 1 - y, 2 * (1 - x) + 1 - y)]


def _gather8(v, *, reduce, name):
    r, c = v.shape

    def body(v_ref, out_ref, *scratch):
        if reduce:
            buf, send_sems, recv_sems = scratch
        else:
            buf = out_ref
            send_sems, recv_sems = scratch
        x, y, cc = _coords()
        me = 4 * x + 2 * y + cc
        buf[me] = v_ref[...]
        copies = []
        for k in range(1, 8):
            px, py, pc = x ^ (k >> 2), y ^ ((k >> 1) & 1), cc ^ (k & 1)
            copies.append(pltpu.make_async_remote_copy(
                src_ref=v_ref, dst_ref=buf.at[me], send_sem=send_sems.at[k - 1], recv_sem=recv_sems.at[k - 1],
                device_id=(px, py, pc), device_id_type=MESH))
        for cp in copies:
            cp.start()
        for k in range(1, 8):
            peer = (x ^ (k >> 2)) * 4 + (y ^ ((k >> 1) & 1)) * 2 + (cc ^ (k & 1))
            pltpu.make_async_remote_copy(
                src_ref=v_ref, dst_ref=buf.at[peer], send_sem=send_sems.at[k - 1], recv_sem=recv_sems.at[k - 1],
                device_id=(x, y, cc), device_id_type=MESH).wait_recv()
        for cp in copies:
            cp.wait_send()
        if reduce:
            acc = buf[0]
            for d in range(1, 8):
                acc = acc + buf[d]
            out_ref[...] = acc

    scratch = [pltpu.SemaphoreType.DMA((7,)), pltpu.SemaphoreType.DMA((7,))]
    if reduce:
        scratch = [pltpu.VMEM((8, r, c), F32)] + scratch
    return pl.pallas_call(
        body, name=name, in_specs=[VM], out_specs=VM,
        out_shape=jax.ShapeDtypeStruct((r, c) if reduce else (8, r, c), F32),
        scratch_shapes=scratch, compiler_params=_params(),
    )(v)


class _AgCopies:
    def __init__(self, buf, ranges, send_sems, recv_sems):
        self.buf, self.ranges, self.send_sems, self.recv_sems = buf, ranges, send_sems, recv_sems
        self.x, self.y, self.cc = _coords()
        self.p = 2 * self.x + self.y
        self.chips = _other_chips(self.x, self.y)

    def rows(self, chip, r, hf):
        start, n = self.ranges[r]
        return self.buf.at[chip, pl.ds(start + hf * (n // 2), n // 2), :]

    def _copy(self, r, k, chip, hf, to):
        return pltpu.make_async_remote_copy(
            src_ref=self.rows(chip, r, hf), dst_ref=self.rows(chip, r, hf), send_sem=self.send_sems.at[3 * r + k],
            recv_sem=self.recv_sems.at[3 * r + k], device_id=to, device_id_type=MESH)

    def pairs(self):
        return [(r, k) for r in range(len(self.ranges)) for k in range(3)]

    def ici(self, r, k):
        cx, cy, _ = self.chips[k]
        return self._copy(r, k, self.p, self.cc, (cx, cy, self.cc))

    def ici_arrival(self, r, k):
        return self._copy(r, k, self.chips[k][2], self.cc, (self.x, self.y, self.cc))

    def forward(self, r, k):
        return self._copy(r, k, self.chips[k][2], self.cc, (self.x, self.y, 1 - self.cc))

    def forward_arrival(self, r, k):
        return self._copy(r, k, self.chips[k][2], 1 - self.cc, (self.x, self.y, self.cc))


def _ag_weights(w4, ranges):
    n = 3 * len(ranges)

    def body(w_ref, out_ref, send1, recv1, send2, recv2):
        ici, fwd = _AgCopies(out_ref, ranges, send1, recv1), _AgCopies(out_ref, ranges, send2, recv2)
        for r, k in ici.pairs():
            ici.ici(r, k).start()
        for r, k in ici.pairs():
            ici.ici_arrival(r, k).wait_recv()
            fwd.forward(r, k).start()
        for r, k in ici.pairs():
            fwd.forward_arrival(r, k).wait_recv()
        for r, k in ici.pairs():
            ici.ici(r, k).wait_send()
            fwd.forward(r, k).wait_send()

    return pl.pallas_call(
        body, name="ag_weights", in_specs=[ANY], out_specs=ANY, out_shape=jax.ShapeDtypeStruct(w4.shape, w4.dtype),
        scratch_shapes=[pltpu.SemaphoreType.DMA((n,))] * 4, input_output_aliases={0: 0}, compiler_params=_params(),
    )(w4)


def _ag_forward(w4, ranges):
    n = 3 * len(ranges)

    def body(w_ref, out_ref, send2, recv2):
        fwd = _AgCopies(out_ref, ranges, send2, recv2)
        for r, k in fwd.pairs():
            fwd.forward(r, k).start()
        for r, k in fwd.pairs():
            fwd.forward_arrival(r, k).wait_recv()
        for r, k in fwd.pairs():
            fwd.forward(r, k).wait_send()

    return pl.pallas_call(
        body, name="ag_forward", in_specs=[ANY], out_specs=ANY, out_shape=jax.ShapeDtypeStruct(w4.shape, w4.dtype),
        scratch_shapes=[pltpu.SemaphoreType.DMA((n,))] * 2, input_output_aliases={0: 0}, compiler_params=_params(),
    )(w4)


def _swap_copy(g_ref, out_ref, send_sem, recv_sem):
    x, y, cc = _coords()
    half = g_ref.shape[1] // 2
    return pltpu.make_async_remote_copy(
        src_ref=g_ref.at[:, pl.ds((1 - cc) * half, half), :], dst_ref=out_ref, send_sem=send_sem, recv_sem=recv_sem,
        device_id=(x, y, 1 - cc), device_id_type=MESH)


def _swap_halves(g, *, name):
    nb, r, c = g.shape
    half = r // 2

    def body(g_ref, out_ref, send_sem, recv_sem):
        cp = _swap_copy(g_ref, out_ref, send_sem, recv_sem)
        cp.start()
        cp.wait()

    return pl.pallas_call(
        body, name=name, in_specs=[ANY], out_specs=ANY, out_shape=jax.ShapeDtypeStruct((nb, half, c), g.dtype),
        scratch_shapes=[pltpu.SemaphoreType.DMA, pltpu.SemaphoreType.DMA], compiler_params=_params(),
    )(g)


def _my_half_index():
    return lax.axis_index("c").astype(jnp.int32).reshape(1)


def _add_halves(g, got, tag):
    nb, r, c = g.shape
    half = r // 2
    tr = _tile(half, 512, 16)
    nt_ = half // tr

    def body(c_ref, a_ref, b_ref, o_ref):
        o_ref[...] = (a_ref[...].astype(F32) + b_ref[...].astype(F32)).astype(BF16)

    return pl.pallas_call(
        body, name=f"rs_add_sibling{tag}",
        grid_spec=pltpu.PrefetchScalarGridSpec(
            num_scalar_prefetch=1, grid=(nb, nt_),
            in_specs=[pl.BlockSpec((1, tr, c), lambda b, i, cr: (b, cr[0] * nt_ + i, 0)),
                      pl.BlockSpec((1, tr, c), lambda b, i, cr: (b, i, 0))],
            out_specs=pl.BlockSpec((1, tr, c), lambda b, i, cr: (b, i, 0))),
        out_shape=jax.ShapeDtypeStruct((nb, half, c), BF16), compiler_params=_params(("parallel", "parallel")),
    )(_my_half_index(), g, got)


def _scatter_copies(s_ref, out_ref, send_sems, recv_sems):
    x, y, cc = _coords()
    sends = [pltpu.make_async_remote_copy(
        src_ref=s_ref.at[blk], dst_ref=out_ref.at[k], send_sem=send_sems.at[k], recv_sem=recv_sems.at[k],
        device_id=(cx, cy, cc), device_id_type=MESH) for k, (cx, cy, blk) in enumerate(_other_chips(x, y))]
    arrivals = [pltpu.make_async_remote_copy(
        src_ref=s_ref.at[2 * x + y], dst_ref=out_ref.at[k], send_sem=send_sems.at[k], recv_sem=recv_sems.at[k],
        device_id=(x, y, cc), device_id_type=MESH) for k in range(3)]
    return sends, arrivals


def _scatter_chips(s, tag):
    nb, hrows, c = s.shape

    def body(s_ref, out_ref, send_sems, recv_sems):
        sends, arrivals = _scatter_copies(s_ref, out_ref, send_sems, recv_sems)
        for cp in sends:
            cp.start()
        for cp in arrivals:
            cp.wait_recv()
        for cp in sends:
            cp.wait_send()

    return pl.pallas_call(
        body, name=f"rs_scatter{tag}", in_specs=[ANY], out_specs=ANY, out_shape=jax.ShapeDtypeStruct((3, hrows, c), s.dtype),
        scratch_shapes=[pltpu.SemaphoreType.DMA((3,)), pltpu.SemaphoreType.DMA((3,))], compiler_params=_params(),
    )(s)


def _sum_chips(s, got, tag):
    nb, hrows, c = s.shape
    tr = _tile(hrows, 512, 16)

    def body(idx_ref, own_ref, got_ref, o_ref):
        p = idx_ref[0]
        own = own_ref[0].astype(F32)
        parts = [got_ref[k].astype(F32) for k in range(3)]
        acc = jnp.zeros_like(own)
        for q in range(4):
            val = own
            for k, rel in enumerate((2, 1, 3)):
                val = jnp.where((p ^ rel) == q, parts[k], val)
            acc = acc + val
        o_ref[...] = acc

    idx = (2 * lax.axis_index("x") + lax.axis_index("y")).astype(jnp.int32).reshape(1)
    return pl.pallas_call(
        body, name=f"rs_sum_chips{tag}",
        grid_spec=pltpu.PrefetchScalarGridSpec(
            num_scalar_prefetch=1, grid=(hrows // tr,),
            in_specs=[pl.BlockSpec((1, tr, c), lambda i, pr: (pr[0], i, 0)), pl.BlockSpec((3, tr, c), lambda i, pr: (0, i, 0))],
            out_specs=pl.BlockSpec((tr, c), lambda i, pr: (i, 0))),
        out_shape=jax.ShapeDtypeStruct((hrows, c), F32), compiler_params=_params(("parallel",)),
    )(idx, s, got)


def _swap_sibling(t, tag):
    def body(t_ref, out_ref, send_sem, recv_sem):
        x, y, cc = _coords()
        cp = pltpu.make_async_remote_copy(src_ref=t_ref, dst_ref=out_ref, send_sem=send_sem, recv_sem=recv_sem,
                                          device_id=(x, y, 1 - cc), device_id_type=MESH)
        cp.start()
        cp.wait()

    return pl.pallas_call(
        body, name=f"rs_join{tag}", in_specs=[ANY], out_specs=ANY, out_shape=jax.ShapeDtypeStruct(t.shape, t.dtype),
        scratch_shapes=[pltpu.SemaphoreType.DMA, pltpu.SemaphoreType.DMA], compiler_params=_params(),
    )(t)


def _rs_local(g, tag):
    return _add_halves(g, _swap_halves(g, name=f"rs_swap{tag}"), tag)


def _rs_finish(s, recv, tag):
    t = _sum_chips(s, recv, tag)
    r = _swap_sibling(t, tag)
    first = lax.axis_index("c") == 0
    return jnp.concatenate([jnp.where(first, t, r), jnp.where(first, r, t)], axis=0)


_SMALL_SHARDED = (("meta_tokens", 1), ("gla_w_alpha2", 2), ("gdn_conv_w", 3))
_REPLICATED = ("norm_mix", "norm_ffn", "fox_b_f", "fox_q_gain", "fox_k_gain", "gla_b_alpha", "gla_o_gain",
               "gdn_a_log", "gdn_dt_bias", "gdn_o_gain")
_WEIGHTS = ("meta_tokens", "norm_mix", "norm_ffn", "w_gate_up", "w_down", "fox_w_in", "fox_b_f", "fox_q_gain",
            "fox_k_gain", "fox_w_out", "gla_w_in", "gla_w_alpha2", "gla_b_alpha", "gla_o_gain", "gla_w_out",
            "gdn_w_in", "gdn_conv_w", "gdn_a_log", "gdn_dt_bias", "gdn_o_gain", "gdn_w_out")
_PACK_ROWS = 512
_IN_W = ("fox_w_in", "gla_w_in", "gdn_w_in")
_OUT_W = ("fox_w_out", "gla_w_out", "gdn_w_out")


def _piece_rows(n):
    return -(-n // 32) * 32


def _pack(arrays, width, row_mult, dtype):
    flat = jnp.concatenate([a.astype(dtype).reshape(-1) for a in arrays])
    per = width * row_mult
    n = -(-flat.shape[0] // per) * per
    return jnp.pad(flat, (0, n - flat.shape[0])).reshape(n // width, width)


def _unpack(flat, shapes):
    out, off = [], 0
    for s in shapes:
        n = 1
        for d in s:
            n *= d
        out.append(flat[off:off + n].reshape(s))
        off += n
    return out


def _unpack_cols(flat2, shapes):
    out, off = [], 0
    for s in shapes:
        n = 1
        for d in s:
            n *= d
        out.append(flat2[:, off:off + n].reshape((flat2.shape[0],) + tuple(s)))
        off += n
    return out


def kernel(x, meta_tokens, norm_mix, norm_ffn, w_gate_up, w_down, fox_w_in, fox_b_f, fox_q_gain, fox_k_gain, fox_w_out, gla_w_in, gla_w_alpha2, gla_b_alpha, gla_o_gain, gla_w_out, gdn_w_in, gdn_conv_w, gdn_a_log, gdn_dt_bias, gdn_o_gain, gdn_w_out, loss_target, m_meta_tokens, m_norm_mix, m_norm_ffn, m_w_gate_up, m_w_down, m_fox_w_in, m_fox_b_f, m_fox_q_gain, m_fox_k_gain, m_fox_w_out, m_gla_w_in, m_gla_w_alpha2, m_gla_b_alpha, m_gla_o_gain, m_gla_w_out, m_gdn_w_in, m_gdn_conv_w, m_gdn_a_log, m_gdn_dt_bias, m_gdn_o_gain, m_gdn_w_out, v_meta_tokens, v_norm_mix, v_norm_ffn, v_w_gate_up, v_w_down, v_fox_w_in, v_fox_b_f, v_fox_q_gain, v_fox_k_gain, v_fox_w_out, v_gla_w_in, v_gla_w_alpha2, v_gla_b_alpha, v_gla_o_gain, v_gla_w_out, v_gdn_w_in, v_gdn_conv_w, v_gdn_a_log, v_gdn_dt_bias, v_gdn_o_gain, v_gdn_w_out):
    W = dict(meta_tokens=meta_tokens, norm_mix=norm_mix, norm_ffn=norm_ffn, w_gate_up=w_gate_up, w_down=w_down,
             fox_w_in=fox_w_in, fox_b_f=fox_b_f, fox_q_gain=fox_q_gain, fox_k_gain=fox_k_gain, fox_w_out=fox_w_out,
             gla_w_in=gla_w_in, gla_w_alpha2=gla_w_alpha2, gla_b_alpha=gla_b_alpha, gla_o_gain=gla_o_gain,
             gla_w_out=gla_w_out, gdn_w_in=gdn_w_in, gdn_conv_w=gdn_conv_w, gdn_a_log=gdn_a_log,
             gdn_dt_bias=gdn_dt_bias, gdn_o_gain=gdn_o_gain, gdn_w_out=gdn_w_out)
    M = dict(meta_tokens=m_meta_tokens, norm_mix=m_norm_mix, norm_ffn=m_norm_ffn, w_gate_up=m_w_gate_up, w_down=m_w_down,
             fox_w_in=m_fox_w_in, fox_b_f=m_fox_b_f, fox_q_gain=m_fox_q_gain, fox_k_gain=m_fox_k_gain,
             fox_w_out=m_fox_w_out, gla_w_in=m_gla_w_in, gla_w_alpha2=m_gla_w_alpha2, gla_b_alpha=m_gla_b_alpha,
             gla_o_gain=m_gla_o_gain, gla_w_out=m_gla_w_out, gdn_w_in=m_gdn_w_in, gdn_conv_w=m_gdn_conv_w,
             gdn_a_log=m_gdn_a_log, gdn_dt_bias=m_gdn_dt_bias, gdn_o_gain=m_gdn_o_gain, gdn_w_out=m_gdn_w_out)
    V = dict(meta_tokens=v_meta_tokens, norm_mix=v_norm_mix, norm_ffn=v_norm_ffn, w_gate_up=v_w_gate_up, w_down=v_w_down,
             fox_w_in=v_fox_w_in, fox_b_f=v_fox_b_f, fox_q_gain=v_fox_q_gain, fox_k_gain=v_fox_k_gain,
             fox_w_out=v_fox_w_out, gla_w_in=v_gla_w_in, gla_w_alpha2=v_gla_w_alpha2, gla_b_alpha=v_gla_b_alpha,
             gla_o_gain=v_gla_o_gain, gla_w_out=v_gla_w_out, gdn_w_in=v_gdn_w_in, gdn_conv_w=v_gdn_conv_w,
             gdn_a_log=v_gdn_a_log, gdn_dt_bias=v_gdn_dt_bias, gdn_o_gain=v_gdn_o_gain, gdn_w_out=v_gdn_w_out)
    chip = 2 * lax.axis_index("x") + lax.axis_index("y")

    pieces, offs, r = [], {}, FFN_ROWS
    for n in _IN_W:
        nc = W[n].shape[2]
        for l in range(W[n].shape[0]):
            pieces.append(jnp.pad(W[n][l].T.astype(BF16), ((0, _piece_rows(nc) - nc), (0, 0))))
            offs[n, l] = r
            r += _piece_rows(nc)
    for n in _OUT_W:
        for l in range(W[n].shape[0]):
            pieces.append(W[n][l].astype(BF16))
            offs[n, l] = r
            r += W[n].shape[1]
    rows = -(-r // _PACK_ROWS) * _PACK_ROWS
    packed = jnp.concatenate([jnp.swapaxes(w_gate_up, 1, 2).reshape(-1, D).astype(BF16), w_down.reshape(-1, D).astype(BF16)]
                             + pieces + [jnp.zeros((rows - r, D), BF16)], axis=0)
    first_rows = [(offs["fox_w_in", 0], offs["fox_w_in", 1] - offs["fox_w_in", 0]),
                  (offs["fox_w_out", 0], offs["fox_w_out", 1] - offs["fox_w_out", 0])]
    later_rows = [(0, FFN_ROWS), (offs["fox_w_in", 1], offs["fox_w_out", 0] - offs["fox_w_in", 1]),
                  (offs["fox_w_out", 1], r - offs["fox_w_out", 1])]
    wpk = _ag_weights(lax.dynamic_update_slice(lax.empty((4, rows, D), BF16), packed[None], (chip, 0, 0)), first_rows)

    def in_t(buf, n, l, npad):
        nc = W[n].shape[2]
        return jnp.concatenate([buf[q, offs[n, l]:offs[n, l] + nc] for q in range(4)] + [jnp.zeros((npad - 4 * nc, D), BF16)], 0)

    def out_w(buf, n, l):
        return jnp.concatenate([buf[q, offs[n, l]:offs[n, l] + W[n].shape[1]] for q in range(4)], axis=0)

    fox_in0, fox_out0 = in_t(wpk, "fox_w_in", 0, FOX_INP), out_w(wpk, "fox_w_out", 0)
    full = {}
    small = _pack([W[n] for n, _ in _SMALL_SHARDED], LANES, 8, F32)
    small_all = _gather8(small, reduce=False, name="gather_small").reshape(8, -1)
    for (n, ax), seg in zip(_SMALL_SHARDED, _unpack_cols(small_all, [W[n].shape for n, _ in _SMALL_SHARDED])):
        full[n] = jnp.concatenate([seg[2 * q] for q in range(4)], axis=ax)
    fox_in, full["fox_w_out"] = [fox_in0], [fox_out0]
    w_alpha2, conv_w = full["gla_w_alpha2"][0], full["gdn_conv_w"][0]

    h = jnp.concatenate([jnp.zeros((META0, D), F32), full["meta_tokens"], x[0]], axis=0)
    saved = []
    y = _rms_fwd(h, norm_mix[0], name="norm_mix0")
    for i in range(DEPTH):
        kind, j = i % 3, i // 3
        if kind == 0:
            proj = _mm(y, fox_in[j], tb=True, name=f"fox_in{j}")
            qa, ka, va = _fox_prep(proj, fox_b_f[j], fox_q_gain[j], fox_k_gain[j])
            if i == 0:
                o, og, lse, wpk = _fox_attn_fwd(qa, ka, va, proj, ag=(wpk, later_rows))
                wpk = _ag_forward(wpk, later_rows)
                fox_in += [in_t(wpk, "fox_w_in", l, FOX_INP) for l in range(1, fox_w_in.shape[0])]
                full["fox_w_out"] += [out_w(wpk, "fox_w_out", l) for l in range(1, fox_w_out.shape[0])]
                gla_in = [in_t(wpk, "gla_w_in", l, GLA_INP) for l in range(gla_w_in.shape[0])]
                gdn_in = [in_t(wpk, "gdn_w_in", l, GDN_INP) for l in range(gdn_w_in.shape[0])]
                for n in ("gla_w_out", "gdn_w_out"):
                    full[n] = [out_w(wpk, n, l) for l in range(W[n].shape[0])]
            else:
                o, og, lse = _fox_attn_fwd(qa, ka, va, proj)
            w_out, mix = full["fox_w_out"][j], (proj, qa, ka, va, o, lse)
        elif kind == 1:
            proj = _mm(y, gla_in[j], tb=True, name=f"gla_in{j}")
            o, og, states = _gla_fwd(proj, w_alpha2, gla_b_alpha[j], gla_o_gain[j])
            w_out, mix = full["gla_w_out"][j], (proj, o, states)
        else:
            proj = _mm(y, gdn_in[j], tb=True, name=f"gdn_in{j}")
            o, og, states, tinvs = _gdn_fwd(proj, conv_w, gdn_a_log[j], gdn_dt_bias[j], gdn_o_gain[j])
            w_out, mix = full["gdn_w_out"][j], (proj, o, states, tinvs)
        hm, yf = _mm(og, w_out, add=h, norm=norm_ffn[i], name=f"mix_out{i}")
        gate, up, act = _ffn_up(yf, wpk, i)
        hn, y_next = _ffn_down(act, wpk, i, hm, norm_mix[(i + 1) % DEPTH])
        saved.append((h, y, mix, og, w_out, hm, yf, gate, up, act))
        h, y = hn, y_next
    dh, loss_tile = _loss_head(h, loss_target[0])

    G = {n: [None] * W[n].shape[0] for n in _WEIGHTS if n not in ("meta_tokens", "w_gate_up", "w_down") + _IN_W}
    GT = {}

    def grad_layout(ffn_layers, pieces):
        off, end = {}, 0
        for l in ffn_layers:
            off["gu", l] = end
            end += GU_ROWS
        for l in ffn_layers:
            off["down", l] = end
            end += DOWN_ROWS
        for n, l in pieces:
            off[n, l] = end
            end += _piece_rows(W[n].shape[2]) if n in _IN_W else W[n].shape[1]
        return off, end, -(-end // _PACK_ROWS) * _PACK_ROWS

    first_pieces = [("fox_w_in", 0)]
    later_pieces = [(n, l) for n in _IN_W + _OUT_W for l in range(W[n].shape[0]) if (n, l) not in first_pieces]
    layouts = [grad_layout([], first_pieces), grad_layout(list(range(DEPTH)), later_pieces)]
    gbuf = [jnp.zeros((4, lay[2], D), BF16) for lay in layouts]

    def with_pieces(buf, lay, pieces):
        off, end, total = lay
        blocks = []
        for q in range(4):
            parts = []
            for n, l in pieces:
                if n in _IN_W:
                    nc = W[n].shape[2]
                    parts.append(jnp.pad(GT[n, l][q * nc:(q + 1) * nc], ((0, _piece_rows(nc) - nc), (0, 0))))
                else:
                    nr = W[n].shape[1]
                    parts.append(G[n][l][q * nr:(q + 1) * nr])
            blocks.append(jnp.concatenate(parts + [jnp.zeros((total - end, D), BF16)], axis=0))
        return lax.dynamic_update_slice(buf, jnp.stack(blocks), (0, off[pieces[0]], 0))

    s_later = None
    for i in reversed(range(DEPTH)):
        kind, j = i % 3, i // 3
        h_in, y, mix, og, w_out, hm, yf, gate, up, act = saved[i]
        b = 1
        dg, du = _ffn_dact(dh, wpk, i, gate, up)
        gbuf[b] = _ffn_dw_down(act, dh, gbuf[b], i, layouts[b][0]["down", i])
        dhm, dnf = _ffn_dyf(dg, du, wpk, i, hm, norm_ffn[i], dh)
        gbuf[b] = _ffn_dw_gu(dg, du, yf, gbuf[b], i, layouts[b][0]["gu", i] // GU_ROWS)
        G["norm_ffn"][i] = dnf[0]
        dog = _mm(dhm, w_out, tb=True, name=f"d_og{i}")
        dw_out = _mm(og, dhm, ta=True, out_dtype=BF16, name=f"d_w_out{i}")
        if kind == 0:
            proj, qa, ka, va, o, lse = mix
            G["fox_w_out"][j] = dw_out
            if i == 0:
                g_later = with_pieces(gbuf[1], layouts[1], later_pieces)
                doa, q2, dgate, got = _fox_gate_bwd(dog, o, proj, lse, qa, swap=g_later)
                s_later = _add_halves(g_later, got, "_later")
                dqn, dkn, dv, dct, recv_later = _fox_attn_bwd(q2, ka, va, doa, rs=s_later)
            else:
                doa, q2, dgate = _fox_gate_bwd(dog, o, proj, lse, qa)
                dqn, dkn, dv, dct = _fox_attn_bwd(q2, ka, va, doa)
            dproj, dqg, dkg, dbf = _fox_prep_bwd(proj, fox_b_f[j], fox_q_gain[j], fox_k_gain[j], dqn, dkn, dv, dgate, dct)
            G["fox_q_gain"][j] = dqg.reshape(FOX_H, FOX_DH).sum(0)
            G["fox_k_gain"][j] = dkg.reshape(FOX_H, FOX_DH).sum(0)
            G["fox_b_f"][j] = dbf[0, :FOX_H]
            w_in, wname = fox_in[j], "fox_w_in"
        elif kind == 1:
            proj, o, states = mix
            dproj, dwa, dba, dogain = _gla_bwd(proj, w_alpha2, gla_b_alpha[j], gla_o_gain[j], o, states, dog)
            G["gla_w_out"][j] = dw_out
            G["gla_w_alpha2"][j] = dwa[:GLA_RANK]
            G["gla_b_alpha"][j] = dba[0]
            G["gla_o_gain"][j] = dogain[0]
            w_in, wname = gla_in[j], "gla_w_in"
        else:
            proj, o, states, tinvs = mix
            dproj, dcw, dal, ddt, dogain = _gdn_bwd(proj, conv_w, gdn_a_log[j], gdn_dt_bias[j], gdn_o_gain[j], o, states,
                                                    tinvs, dog)
            G["gdn_w_out"][j] = dw_out
            G["gdn_conv_w"][j] = dcw[:4].reshape(4, 1, GDN_CONV)
            G["gdn_a_log"][j] = dal[0, :GDN_H]
            G["gdn_dt_bias"][j] = ddt[0, :GDN_H]
            G["gdn_o_gain"][j] = dogain[0]
            w_in, wname = gdn_in[j], "gdn_w_in"
        dh, dnm = _mm(dproj, w_in, rms_bwd=(h_in, norm_mix[i], dhm), name=f"d_y{i}")
        GT[wname, j] = _mm(dproj, y, ta=True, out_dtype=BF16, name=f"d_w_in{i}")
        G["norm_mix"][i] = dnm[0]
    grad_x = dh[ROW0:][None]
    G = {n: (v if n in _OUT_W else jnp.stack(v)) for n, v in G.items()}
    G["meta_tokens"] = dh[META0:ROW0]

    s_first = _rs_local(with_pieces(gbuf[0], layouts[0], first_pieces), "_first")
    reduced = [_rs_finish(s_first, _scatter_chips(s_first, "_first"), "_first"), _rs_finish(s_later, recv_later, "_later")]

    def reduced_piece(n, l):
        b = 0 if (n, l) in first_pieces else 1
        start = layouts[b][0][n, l]
        return reduced[b][start:start + (W[n].shape[2] if n in _IN_W else W[n].shape[1])]

    grads = {}
    for n in _IN_W:
        grads[n] = jnp.stack([reduced_piece(n, l).T for l in range(W[n].shape[0])])
    for n in _OUT_W:
        grads[n] = jnp.stack([reduced_piece(n, l) for l in range(W[n].shape[0])])
    small_names = [n for n, _ in _SMALL_SHARDED] + list(_REPLICATED)
    small_g = _pack([G[n] for n in small_names] + [loss_tile[0, 0:1]], LANES, 8, F32)
    small_sum = _gather8(small_g, reduce=True, name="allreduce_small").reshape(-1)
    small_shapes = [G[n].shape for n in small_names] + [(1,)]
    small_vals = _unpack(small_sum, small_shapes)
    loss = small_vals[-1][0]
    for n, val in zip(small_names, small_vals[:-1]):
        grads[n] = val
    for n, ax in _SMALL_SHARDED:
        sz = W[n].shape[ax]
        grads[n] = lax.dynamic_slice_in_dim(grads[n], chip * sz, sz, axis=ax)

    delta, new_m, new_v = {}, {}, {}
    for n, key, tr_ in (("w_gate_up", "gu", True), ("w_down", "down", False)):
        grads[n], delta[n], new_m[n], new_v[n] = _adamw_packed(
            W[n], reduced[1], reduced[1], M[n], V[n], row0=layouts[1][0][key, 0], row_off=layouts[1][0][key, 1],
            transposed=tr_, name=f"adamw_{n}")
    for n in _IN_W + _OUT_W:
        delta[n], new_m[n], new_v[n] = _adamw(W[n], grads[n], M[n], V[n], name=f"adamw_{n}")
    tiny = [n for n in _WEIGHTS if n not in ("w_gate_up", "w_down") + _IN_W + _OUT_W]
    packs = [_pack([T[n] for n in tiny], LANES, 8, F32) for T in (W, grads, M, V)]
    outs = _adamw(*packs, name="adamw_small")
    shapes = [W[n].shape for n in tiny]
    for dst, o in zip((delta, new_m, new_v), outs):
        for n, val in zip(tiny, _unpack(o.reshape(-1), shapes)):
            dst[n] = val
    return (loss, grad_x, *[grads[n] for n in _WEIGHTS], *[delta[n] for n in _WEIGHTS],
            *[new_m[n] for n in _WEIGHTS], *[new_v[n] for n in _WEIGHTS])
```

```python
import jax
import jax.numpy as jnp
from jax import lax
from jax.experimental import pallas as pl
from jax.experimental.pallas import tpu as pltpu

F32, BF16 = jnp.float32, jnp.bfloat16
D = 1024
N_META = 16
ROW0 = 128
META0 = ROW0 - N_META
EPS = 1e-6
LANES = 128
VMEM_LIMIT = 56 * 1024 * 1024

FOX_H, FOX_DH = 16, 64
FOX_INP = 4224
GLA_H, GLA_DK, GLA_DV, GLA_RANK = 4, 128, 256, 16
GLA_QK, GLA_V = 512, 1024
GLA_INP = 3200
GLA_NORM = 16.0
GDN_H, GDN_DK, GDN_DV = 8, 128, 128
GDN_CONV = 3072
GDN_INP = 4224
CHUNK = 64
D_FF = 2816
DEPTH = 4

ADAM_LR, ADAM_B1, ADAM_B2, ADAM_EPS, ADAM_WD, ADAM_STEP = 0.001, 0.9, 0.999, 1e-08, 0.01, 10

MESH = pl.DeviceIdType.MESH
ANY = pl.BlockSpec(memory_space=pl.ANY)
VM = pl.BlockSpec(memory_space=pltpu.VMEM)


def _params(sem=None, **kw):
    if sem is not None:
        kw["dimension_semantics"] = sem
    return pltpu.CompilerParams(vmem_limit_bytes=VMEM_LIMIT, **kw)


def _tile(n, cap, mult=LANES):
    best = None
    for t in range(mult, min(n, cap) + 1, mult):
        if n % t == 0:
            best = t
    return best if best is not None else n


def nn(a, b, **kw):
    return jnp.dot(a, b, preferred_element_type=F32, **kw)


def nt(a, b, **kw):
    return lax.dot_general(a, b, (((1,), (1,)), ((), ())), preferred_element_type=F32, **kw)


def tn(a, b, **kw):
    return lax.dot_general(a, b, (((0,), (0,)), ((), ())), preferred_element_type=F32, **kw)


def _split3(x):
    hi = x.astype(BF16)
    r = x - hi.astype(F32)
    mid = r.astype(BF16)
    lo = (r - mid.astype(F32)).astype(BF16)
    return hi, mid, lo


def _sel_l(sel, x):
    a, b, c = _split3(x)
    return nn(sel, a) + nn(sel, b) + nn(sel, c)


def _sel_r(x, sel):
    a, b, c = _split3(x)
    return nn(a, sel) + nn(b, sel) + nn(c, sel)


def _sel_r2(x, sel):
    a = x.astype(BF16)
    return nn(a, sel) + nn((x - a.astype(F32)).astype(BF16), sel)


def _iota(shape, dim):
    return lax.broadcasted_iota(jnp.int32, shape, dim)


def _tri(n, upper=False, strict=False):
    i, j = _iota((n, n), 0), _iota((n, n), 1)
    if upper:
        m = (j > i) if strict else (j >= i)
    else:
        m = (j < i) if strict else (j <= i)
    return m


def _sigmoid(x):
    return 1.0 / (1.0 + jnp.exp(-x))


def _log_sigmoid(x):
    return jnp.minimum(x, 0.0) - jnp.log(1.0 + jnp.exp(-jnp.abs(x)))


def _softplus(x):
    return jnp.maximum(x, 0.0) + jnp.log(1.0 + jnp.exp(-jnp.abs(x)))


def _silu(x):
    return x * _sigmoid(x)


def _dsilu(x):
    s = _sigmoid(x)
    return s * (1.0 + x * (1.0 - s))


def _rms(x, g):
    return (x * lax.rsqrt(jnp.mean(x * x, axis=-1, keepdims=True) + EPS) * g).astype(BF16)


def _rms_grad(x, g, dy):
    r = lax.rsqrt(jnp.mean(x * x, axis=-1, keepdims=True) + EPS)
    u = dy * g
    return r * u - x * (r * r * r) * jnp.mean(x * u, axis=-1, keepdims=True), jnp.sum(dy * x * r, axis=0, keepdims=True)


def _mm(a, b, *, ta=False, tb=False, add=None, norm=None, rms_bwd=None, out_dtype=F32, name):
    m, k = (a.shape[1], a.shape[0]) if ta else a.shape
    n = b.shape[0] if tb else b.shape[1]
    assert k == (b.shape[1] if tb else b.shape[0])
    rows_whole = norm is not None or rms_bwd is not None
    tm, tn_, tk = _tile(m, 704 if rows_whole else 1408, LANES if ta else 16), _tile(n, 1408), _tile(k, 1408)
    nk = k // tk
    assert not rows_whole or tn_ == n

    def body(*refs):
        refs = list(refs)
        a_ref, b_ref = refs[:2]
        extra = refs[2:-1]
        acc = refs[-1]
        i, kk = pl.program_id(0), pl.program_id(2)

        @pl.when(kk == 0)
        def _():
            acc[...] = jnp.zeros_like(acc)

        av, bv = a_ref[...].astype(BF16), b_ref[...].astype(BF16)
        dims = (((0,) if ta else (1,), (1,) if tb else (0,)), ((), ()))
        acc[...] += lax.dot_general(av, bv, dims, preferred_element_type=F32)

        @pl.when(kk == nk - 1)
        def _():
            r = acc[...]
            if rms_bwd is not None:
                h_ref, g_ref, dres_ref, o_ref, dg_ref = extra
                dx, dgain = _rms_grad(h_ref[...], g_ref[...], r)
                o_ref[...] = dres_ref[...] + dx

                @pl.when(i == 0)
                def _():
                    dg_ref[...] = jnp.zeros_like(dg_ref)

                dg_ref[...] += dgain
                return
            if add is not None:
                r = r + extra[0][...].astype(F32)
            if norm is not None:
                g_ref, o_ref, y_ref = extra[-3:]
                y_ref[...] = _rms(r, g_ref[...])
            else:
                o_ref = extra[-1]
            o_ref[...] = r.astype(out_dtype)

    a_spec = pl.BlockSpec((tk, tm), lambda i, j, q: (q, i)) if ta else pl.BlockSpec((tm, tk), lambda i, j, q: (i, q))
    b_spec = pl.BlockSpec((tn_, tk), lambda i, j, q: (j, q)) if tb else pl.BlockSpec((tk, tn_), lambda i, j, q: (q, j))
    o_spec = pl.BlockSpec((tm, tn_), lambda i, j, q: (i, j))
    g_spec = pl.BlockSpec((1, n), lambda i, j, q: (0, 0))
    ins, specs = [a, b], [a_spec, b_spec]
    out_specs, out_shape = o_spec, jax.ShapeDtypeStruct((m, n), out_dtype)
    sem = ("parallel", "parallel", "arbitrary")
    if rms_bwd is not None:
        ins += [rms_bwd[0], rms_bwd[1].reshape(1, n), rms_bwd[2]]
        specs += [o_spec, g_spec, o_spec]
        out_specs, out_shape = [o_spec, g_spec], [jax.ShapeDtypeStruct((m, n), F32), jax.ShapeDtypeStruct((1, n), F32)]
        sem = ("arbitrary", "arbitrary", "arbitrary")
    else:
        if add is not None:
            ins.append(add)
            specs.append(o_spec)
        if norm is not None:
            ins.append(norm.reshape(1, n))
            specs.append(g_spec)
            out_specs, out_shape = [o_spec, o_spec], [out_shape, jax.ShapeDtypeStruct((m, n), BF16)]
    return pl.pallas_call(
        body, name=name, grid=(m // tm, n // tn_, nk), in_specs=specs, out_specs=out_specs, out_shape=out_shape,
        scratch_shapes=[pltpu.VMEM((tm, tn_), F32)], compiler_params=_params(sem),
    )(*ins)


def _rms_fwd(h, g, *, name):
    lp = h.shape[0]
    tr = _tile(lp, 512)

    def body(h_ref, g_ref, y_ref):
        x = h_ref[...]
        r = lax.rsqrt(jnp.mean(x * x, axis=-1, keepdims=True) + EPS)
        y_ref[...] = (x * r * g_ref[...]).astype(BF16)

    return pl.pallas_call(
        body, name=name, grid=(lp // tr,),
        in_specs=[pl.BlockSpec((tr, D), lambda i: (i, 0)), pl.BlockSpec((1, D), lambda i: (0, 0))],
        out_specs=pl.BlockSpec((tr, D), lambda i: (i, 0)),
        out_shape=jax.ShapeDtypeStruct((lp, D), BF16), compiler_params=_params(("parallel",)),
    )(h, g.reshape(1, D))


GU_ROWS, DOWN_ROWS = 1408, 704
OFF_GU, OFF_DOWN = 0, DEPTH * GU_ROWS
FFN_ROWS = DEPTH * (GU_ROWS + DOWN_ROWS)
FFN_TM = 704


def _gu_spec(fn):
    return pl.BlockSpec((None, GU_ROWS, D), fn)


def _down_spec(fn):
    return pl.BlockSpec((None, DOWN_ROWS, D), fn)


def _down_pair(w0_ref, w1_ref):
    return jnp.concatenate([w0_ref[...], w1_ref[...]], axis=0)


def _ffn_up(yf, wpk, layer):
    lp = yf.shape[0]
    tm = _tile(lp, FFN_TM, 16)

    def body(y_ref, wg_ref, wu_ref, g_ref, u_ref, a_ref):
        y = y_ref[...]
        g, u = nt(y, wg_ref[...]), nt(y, wu_ref[...])
        g_ref[...] = g.astype(BF16)
        u_ref[...] = u.astype(BF16)
        a_ref[...] = (_silu(g) * u).astype(BF16)

    o = pl.BlockSpec((tm, GU_ROWS), lambda i, j: (i, j))
    return pl.pallas_call(
        body, name=f"ffn_up{layer}", grid=(lp // tm, 2),
        in_specs=[pl.BlockSpec((tm, D), lambda i, j: (i, 0)), _gu_spec(lambda i, j: (j, OFF_GU // GU_ROWS + layer, 0)),
                  _gu_spec(lambda i, j: (2 + j, OFF_GU // GU_ROWS + layer, 0))],
        out_specs=[o, o, o], out_shape=[jax.ShapeDtypeStruct((lp, D_FF), BF16)] * 3,
        compiler_params=_params(("parallel", "parallel")),
    )(yf, wpk, wpk)


def _ffn_down(act, wpk, layer, res, norm):
    lp = act.shape[0]
    tm = _tile(lp, FFN_TM, 16)

    def body(a_ref, w0_ref, w1_ref, r_ref, g_ref, o_ref, y_ref, acc):
        kk = pl.program_id(1)

        @pl.when(kk == 0)
        def _():
            acc[...] = r_ref[...]

        acc[...] += nn(a_ref[...], _down_pair(w0_ref, w1_ref))

        @pl.when(kk == 1)
        def _():
            o_ref[...] = acc[...]
            y_ref[...] = _rms(acc[...], g_ref[...])

    o = pl.BlockSpec((tm, D), lambda i, kk: (i, 0))
    blk = OFF_DOWN // DOWN_ROWS + layer
    return pl.pallas_call(
        body, name=f"ffn_down{layer}", grid=(lp // tm, 2),
        in_specs=[pl.BlockSpec((tm, GU_ROWS), lambda i, kk: (i, kk)), _down_spec(lambda i, kk: (2 * kk, blk, 0)),
                  _down_spec(lambda i, kk: (2 * kk + 1, blk, 0)), o, pl.BlockSpec((1, D), lambda i, kk: (0, 0))],
        out_specs=[o, o], out_shape=[jax.ShapeDtypeStruct((lp, D), F32), jax.ShapeDtypeStruct((lp, D), BF16)],
        scratch_shapes=[pltpu.VMEM((tm, D), F32)], compiler_params=_params(("parallel", "arbitrary")),
    )(act, wpk, wpk, res, norm.reshape(1, D))


def _ffn_dact(dh, wpk, layer, gate, up):
    lp = dh.shape[0]
    tm = _tile(lp, FFN_TM, 16)

    def body(d_ref, w0_ref, w1_ref, g_ref, u_ref, dg_ref, du_ref):
        da = nt(d_ref[...].astype(BF16), _down_pair(w0_ref, w1_ref))
        g, u = g_ref[...].astype(F32), u_ref[...].astype(F32)
        sg = _sigmoid(g)
        dg_ref[...] = (da * u * (sg * (1.0 + g * (1.0 - sg)))).astype(BF16)
        du_ref[...] = (da * (g * sg)).astype(BF16)

    o = pl.BlockSpec((tm, GU_ROWS), lambda i, j: (i, j))
    blk = OFF_DOWN // DOWN_ROWS + layer
    return pl.pallas_call(
        body, name=f"d_act{layer}", grid=(lp // tm, 2),
        in_specs=[pl.BlockSpec((tm, D), lambda i, j: (i, 0)), _down_spec(lambda i, j: (2 * j, blk, 0)),
                  _down_spec(lambda i, j: (2 * j + 1, blk, 0)), o, o],
        out_specs=[o, o], out_shape=[jax.ShapeDtypeStruct((lp, D_FF), BF16)] * 2,
        compiler_params=_params(("parallel", "parallel")),
    )(dh, wpk, wpk, gate, up)


def _ffn_dyf(dg, du, wpk, layer, hm, norm, dres):
    lp = dg.shape[0]
    tm = _tile(lp, FFN_TM, 16)

    def body(dg_ref, du_ref, w_ref, h_ref, g_ref, dres_ref, o_ref, dgain_ref, acc):
        i, kk = pl.program_id(0), pl.program_id(1)

        @pl.when(kk == 0)
        def _():
            acc[...] = jnp.zeros_like(acc)

        @pl.when(kk < 2)
        def _():
            acc[...] += nn(dg_ref[...], w_ref[...])

        @pl.when(kk >= 2)
        def _():
            acc[...] += nn(du_ref[...], w_ref[...])

        @pl.when(kk == 3)
        def _():
            dx, dgain = _rms_grad(h_ref[...], g_ref[...], acc[...])
            o_ref[...] = dres_ref[...] + dx

            @pl.when(i == 0)
            def _():
                dgain_ref[...] = jnp.zeros_like(dgain_ref)

            dgain_ref[...] += dgain

    o = pl.BlockSpec((tm, D), lambda i, kk: (i, 0))
    row = pl.BlockSpec((1, D), lambda i, kk: (0, 0))
    return pl.pallas_call(
        body, name=f"d_yf{layer}", grid=(lp // tm, 4),
        in_specs=[pl.BlockSpec((tm, GU_ROWS), lambda i, kk: (i, jnp.minimum(kk, 1))),
                  pl.BlockSpec((tm, GU_ROWS), lambda i, kk: (i, jnp.maximum(kk - 2, 0))),
                  _gu_spec(lambda i, kk: (kk, OFF_GU // GU_ROWS + layer, 0)), o, row, o],
        out_specs=[o, row], out_shape=[jax.ShapeDtypeStruct((lp, D), F32), jax.ShapeDtypeStruct((1, D), F32)],
        scratch_shapes=[pltpu.VMEM((tm, D), F32)], compiler_params=_params(("arbitrary", "arbitrary")),
    )(dg, du, wpk, hm, norm.reshape(1, D), dres)


def _ffn_dw_down(act, dh, gpk, layer, row):
    lp = act.shape[0]
    tk = _tile(lp, 1408, 16)
    nk = lp // tk

    def body(a_ref, d_ref, g_in, g_out, acc, stage, sems):
        jp, kk = pl.program_id(0), pl.program_id(1)

        @pl.when(kk == 0)
        def _():
            acc[...] = jnp.zeros_like(acc)

        acc[...] += tn(a_ref[...], d_ref[...].astype(BF16))

        @pl.when(kk == nk - 1)
        def _():
            stage[...] = acc[...].astype(BF16)
            copies = [pltpu.make_async_copy(stage.at[pl.ds(hf * DOWN_ROWS, DOWN_ROWS), :],
                                            g_out.at[2 * jp + hf, pl.ds(row, DOWN_ROWS), :], sems.at[hf]) for hf in range(2)]
            for cp in copies:
                cp.start()
            for cp in copies:
                cp.wait()

    return pl.pallas_call(
        body, name=f"d_w_down{layer}", grid=(2, nk),
        in_specs=[pl.BlockSpec((tk, GU_ROWS), lambda jp, kk: (kk, jp)), pl.BlockSpec((tk, D), lambda jp, kk: (kk, 0)), ANY],
        out_specs=ANY, out_shape=jax.ShapeDtypeStruct(gpk.shape, gpk.dtype),
        scratch_shapes=[pltpu.VMEM((GU_ROWS, D), F32), pltpu.VMEM((GU_ROWS, D), BF16), pltpu.SemaphoreType.DMA((2,))],
        input_output_aliases={2: 0}, compiler_params=_params(("arbitrary", "arbitrary")),
    )(act, dh, gpk)


def _ffn_dw_gu(dg, du, yf, gpk, layer, blk):
    lp = dg.shape[0]
    tk = _tile(lp, 1408, 16)
    nk = lp // tk

    def body(dg_ref, du_ref, y_ref, g_in, o_ref, acc):
        c, kk = pl.program_id(0), pl.program_id(1)

        @pl.when(kk == 0)
        def _():
            acc[...] = jnp.zeros_like(acc)

        @pl.when(c < 2)
        def _():
            acc[...] += tn(dg_ref[...], y_ref[...])

        @pl.when(c >= 2)
        def _():
            acc[...] += tn(du_ref[...], y_ref[...])

        @pl.when(kk == nk - 1)
        def _():
            o_ref[...] = acc[...].astype(BF16)

    return pl.pallas_call(
        body, name=f"d_w_gate_up{layer}", grid=(4, nk),
        in_specs=[pl.BlockSpec((tk, GU_ROWS), lambda c, kk: (kk, jnp.minimum(c, 1))),
                  pl.BlockSpec((tk, GU_ROWS), lambda c, kk: (kk, jnp.maximum(c - 2, 0))),
                  pl.BlockSpec((tk, D), lambda c, kk: (kk, 0)), ANY],
        out_specs=_gu_spec(lambda c, kk: (c, blk, 0)),
        out_shape=jax.ShapeDtypeStruct(gpk.shape, gpk.dtype),
        scratch_shapes=[pltpu.VMEM((GU_ROWS, D), F32)], input_output_aliases={3: 0},
        compiler_params=_params(("parallel", "arbitrary")),
    )(dg, du, yf, gpk)


def _loss_head(h, target):
    lp = h.shape[0]
    nb = lp // ROW0

    def body(h_ref, t_ref, dh_ref, l_ref):
        i = pl.program_id(0)

        @pl.when(i == 0)
        def _():
            l_ref[...] = jnp.zeros_like(l_ref)
            dh_ref[...] = jnp.zeros_like(dh_ref)

        @pl.when(i > 0)
        def _():
            err = h_ref[...] - t_ref[...]
            dh_ref[...] = err * (1.0 / D)
            l_ref[...] += jnp.sum(err * err) * (0.5 / D)

    return pl.pallas_call(
        body, name="loss_head", grid=(nb,),
        in_specs=[pl.BlockSpec((ROW0, D), lambda i: (i, 0)), pl.BlockSpec((ROW0, D), lambda i: (jnp.maximum(i - 1, 0), 0))],
        out_specs=[pl.BlockSpec((ROW0, D), lambda i: (i, 0)), pl.BlockSpec((8, LANES), lambda i: (0, 0))],
        out_shape=[jax.ShapeDtypeStruct((lp, D), F32), jax.ShapeDtypeStruct((8, LANES), F32)],
        compiler_params=_params(("arbitrary",)),
    )(h, target)


def _adamw(w, g, m, v, *, name):
    if w.ndim == 2:
        w, g, m, v = (t[None] for t in (w, g, m, v))
        return tuple(o[0] for o in _adamw(w, g, m, v, name=name))
    nl, r, c = w.shape
    tr = _tile(r, max(8, (1 << 19) // c), 8)

    def body(w_ref, g_ref, m_ref, v_ref, d_ref, nm_ref, nv_ref):
        d_ref[...], nm_ref[...], nv_ref[...] = _adam_math(w_ref[...], g_ref[...], m_ref[...], v_ref[...])

    spec = pl.BlockSpec((1, tr, c), lambda l, i: (l, i, 0))
    return tuple(pl.pallas_call(
        body, name=name, grid=(nl, r // tr), in_specs=[spec] * 4, out_specs=[spec] * 3,
        out_shape=[jax.ShapeDtypeStruct(w.shape, F32)] * 3, compiler_params=_params(("parallel", "parallel")),
    )(w, g, m, v))


def _adam_math(w, g, m, v):
    nm = ADAM_B1 * m + (1.0 - ADAM_B1) * g
    nv = ADAM_B2 * v + (1.0 - ADAM_B2) * (g * g)
    m_hat = nm / (1.0 - ADAM_B1 ** ADAM_STEP)
    v_hat = nv / (1.0 - ADAM_B2 ** ADAM_STEP)
    return -ADAM_LR * (m_hat / (jnp.sqrt(v_hat) + ADAM_EPS) + ADAM_WD * w), nm, nv


def _adamw_packed(w, gred0, gred, m, v, *, row0, row_off, transposed, name):
    nl, a, b = w.shape
    nr = b if transposed else a
    later = lambda l: row_off // nr + jnp.maximum(l - 1, 0)
    if transposed:
        ta = _tile(a, 256)
        wspec = pl.BlockSpec((1, ta, b), lambda l, r: (l, r, 0))
        g0spec = pl.BlockSpec((b, ta), lambda l, r: (row0 // nr, r))
        gspec = pl.BlockSpec((b, ta), lambda l, r: (later(l), r))
        grid = (nl, a // ta)
    else:
        wspec = pl.BlockSpec((1, a, b), lambda l, r: (l, 0, 0))
        g0spec = pl.BlockSpec((a, b), lambda l, r: (row0 // nr, 0))
        gspec = pl.BlockSpec((a, b), lambda l, r: (later(l), 0))
        grid = (nl, 1)

    def body(w_ref, g0_ref, g_ref, m_ref, v_ref, go_ref, d_ref, nm_ref, nv_ref):
        g = jnp.where(pl.program_id(0) == 0, g0_ref[...], g_ref[...])
        g = g.T if transposed else g
        d, nm, nv = _adam_math(w_ref[0], g, m_ref[0], v_ref[0])
        go_ref[0], d_ref[0], nm_ref[0], nv_ref[0] = g, d, nm, nv

    return pl.pallas_call(
        body, name=name, grid=grid, in_specs=[wspec, g0spec, gspec, wspec, wspec], out_specs=[wspec] * 4,
        out_shape=[jax.ShapeDtypeStruct(w.shape, F32)] * 4, compiler_params=_params(("parallel", "parallel")),
    )(w, gred0, gred, m, v)


FOX_AUG = FOX_H * LANES
L_C = 64
L_K = 67
L_LSE = 70
PAD_KEY = -30000.0
FOX_TQ = 384


def _head_sel(n_heads, width, lanes=LANES):
    r, c = _iota((n_heads * width, lanes), 0), _iota((n_heads * width, lanes), 1)
    down = (r // width == c).astype(BF16)
    r2, c2 = _iota((lanes, n_heads * width), 0), _iota((lanes, n_heads * width), 1)
    up = (c2 // width == r2).astype(BF16)
    return down, up


def _place(lane0):
    r, c = _iota((LANES, FOX_AUG), 0), _iota((LANES, FOX_AUG), 1)
    return [((c // LANES == r) & (c % LANES == lane0 + m)).astype(BF16) for m in range(3)]


def _placed(x, lane0):
    pcs = _split3(x)
    mats = _place(lane0)
    return nn(pcs[0], mats[0]) + nn(pcs[1], mats[1]) + nn(pcs[2], mats[2])


def _ones_at(rows, lanes):
    c = _iota((rows, FOX_AUG), 1) % LANES
    m = c == lanes[0]
    for l in lanes[1:]:
        m = m | (c == l)
    return m.astype(F32)


def _spread(x, extras, out_ref):
    rows = x.shape[0]
    left = _iota((rows, LANES), 1) < FOX_DH
    for p in range(FOX_H // 2):
        slab = x[:, p * LANES:(p + 1) * LANES]
        a = jnp.where(left, slab, extras[:, 2 * p * LANES:(2 * p + 1) * LANES])
        b = jnp.where(left, pltpu.roll(slab, FOX_DH, 1), extras[:, (2 * p + 1) * LANES:(2 * p + 2) * LANES])
        out_ref[:, 2 * p * LANES:(2 * p + 1) * LANES] = a.astype(BF16)
        out_ref[:, (2 * p + 1) * LANES:(2 * p + 2) * LANES] = b.astype(BF16)


def _fox_prep(proj, b_f, q_gain, k_gain):
    lp = proj.shape[0]
    nb = lp // LANES

    def body(p_ref, bf_ref, qg_ref, kg_ref, q_ref, k_ref, v_ref, carry):
        i = pl.program_id(0)

        @pl.when(i == 0)
        def _():
            carry[...] = jnp.zeros_like(carry)

        down, up = _head_sel(FOX_H, FOX_DH)

        def normed(x, gain):
            ms = _sel_r2(x * x, down) * (1.0 / FOX_DH)
            r = _sel_r2(lax.rsqrt(ms + EPS), up)
            return x * r * gain

        lane = _iota((LANES, LANES), 1)
        lf = jnp.where(lane < FOX_H, _log_sigmoid(p_ref[:, 4 * D:4 * D + LANES] + bf_ref[...]), 0.0)
        c = _sel_l(_tri(LANES).astype(BF16), lf) + carry[0:1, :]
        carry[...] = jnp.broadcast_to(c[LANES - 1:LANES, :], carry.shape)
        q_extra = _placed(c, L_C) + _ones_at(LANES, (L_K, L_K + 1, L_K + 2))
        row = i * LANES + _iota((LANES, FOX_AUG), 0)
        lane_a = _iota((LANES, FOX_AUG), 1) % LANES
        k_extra = -_placed(c, L_K) + _ones_at(LANES, (L_C, L_C + 1, L_C + 2, L_LSE, L_LSE + 1, L_LSE + 2))
        pad_val = jnp.where(lane_a == L_K, PAD_KEY, 0.0)
        k_extra = jnp.where((row < META0) & (lane_a >= L_K) & (lane_a < L_K + 3), pad_val, k_extra)
        v_extra = _ones_at(LANES, (L_C, L_C + 1, L_C + 2))
        _spread(normed(p_ref[:, 0:D], qg_ref[...]) * (FOX_DH ** -0.5), q_extra, q_ref)
        _spread(normed(p_ref[:, D:2 * D], kg_ref[...]), k_extra, k_ref)
        _spread(p_ref[:, 2 * D:3 * D], v_extra, v_ref)

    row = pl.BlockSpec((1, D), lambda i: (0, 0))
    aug = pl.BlockSpec((LANES, FOX_AUG), lambda i: (i, 0))
    return pl.pallas_call(
        body, name="fox_prep", grid=(nb,),
        in_specs=[pl.BlockSpec((LANES, FOX_INP), lambda i: (i, 0)), pl.BlockSpec((1, LANES), lambda i: (0, 0)), row, row],
        out_specs=[aug] * 3, out_shape=[jax.ShapeDtypeStruct((lp, FOX_AUG), BF16)] * 3,
        scratch_shapes=[pltpu.VMEM((8, LANES), F32)],
        compiler_params=_params(("arbitrary",)),
    )(proj, jnp.pad(b_f, (0, LANES - FOX_H)).reshape(1, LANES), jnp.tile(q_gain, FOX_H).reshape(1, D),
      jnp.tile(k_gain, FOX_H).reshape(1, D))


def _fox_attn_fwd(qa, ka, va, proj, ag=None):
    lp = qa.shape[0]
    tq = _tile(lp, FOX_TQ)
    nq = lp // tq
    npair = FOX_H // 2

    def body(q_ref, k_ref, v_ref, gate_ref, *rest):
        if ag is None:
            o_ref, og_ref, lse_ref = rest
        else:
            _, o_ref, og_ref, lse_ref, w_out, send_sems, recv_sems, send2, recv2 = rest
            copies, fwd = _AgCopies(w_out, ag[1], send_sems, recv_sems), _AgCopies(w_out, ag[1], send2, recv2)

            @pl.when((pl.program_id(0) == 0) & (pl.program_id(1) == 0))
            def _():
                for r, k in copies.pairs():
                    copies.ici(r, k).start()

            @pl.when((pl.program_id(0) == npair - 1) & (pl.program_id(1) == 0))
            def _():
                for r, k in copies.pairs():
                    copies.ici_arrival(r, k).wait_recv()
                    fwd.forward(r, k).start()

        i = pl.program_id(1)
        causal = _iota((tq, tq), 1) <= _iota((tq, tq), 0)
        qs = [q_ref[:, hh * LANES:(hh + 1) * LANES] for hh in range(2)]

        def block(j, carry, diag):
            off = pl.multiple_of(j * tq, tq)
            out = []
            for hh in range(2):
                m, acc = carry[hh]
                k = k_ref[pl.ds(off, tq), hh * LANES:(hh + 1) * LANES]
                v = v_ref[pl.ds(off, tq), hh * LANES:(hh + 1) * LANES]
                s = nt(qs[hh], k)
                if diag:
                    s = jnp.where(causal, s, -1e30)
                m2 = jnp.maximum(m, jnp.max(s, axis=-1, keepdims=True))
                p = jnp.exp(s - m2)
                p_hi = p.astype(BF16)
                p_lo = (p - p_hi.astype(F32)).astype(BF16)
                out.append((m2, jnp.exp(m - m2) * acc + nn(p_hi, v) + nn(p_lo, v)))
            return tuple(out)

        init = tuple((jnp.full((tq, 1), -1e30, F32), jnp.zeros((tq, LANES), F32)) for _ in range(2))
        carry = lax.fori_loop(0, i // 2, lambda j, c: block(2 * j + 1, block(2 * j, c, False), False), init)
        carry = lax.cond(i % 2 == 1, lambda c: block(i - 1, c, False), lambda c: c, carry)
        carry = block(i, carry, True)
        outs, lses = [], []
        for hh in range(2):
            m, acc = carry[hh]
            l = acc[:, L_C:L_C + 1]
            outs.append(acc / l)
            lses.append(jnp.broadcast_to(m + jnp.log(l), (tq, LANES)))
        left = _iota((tq, LANES), 1) < FOX_DH
        o = jnp.where(left, outs[0], pltpu.roll(outs[1], FOX_DH, 1))
        o_ref[...] = o
        og_ref[...] = (o * _sigmoid(gate_ref[...])).astype(BF16)
        lse_ref[...] = jnp.where(left, lses[0], lses[1])

        if ag is not None:
            @pl.when((pl.program_id(0) == npair - 1) & (pl.program_id(1) == nq - 1))
            def _():
                for r, k in copies.pairs():
                    fwd.forward_arrival(r, k).wait_recv()
                for r, k in copies.pairs():
                    copies.ici(r, k).wait_send()
                    fwd.forward(r, k).wait_send()

    qspec = pl.BlockSpec((tq, 2 * LANES), lambda p, i: (i, p))
    kspec = pl.BlockSpec((lp, 2 * LANES), lambda p, i: (0, p))
    ospec = pl.BlockSpec((tq, LANES), lambda p, i: (i, p))
    ins, in_specs = [qa, ka, va, proj], [qspec, kspec, kspec, pl.BlockSpec((tq, LANES), lambda p, i: (i, 3 * D // LANES + p))]
    out_specs = [ospec] * 3
    out_shape = [jax.ShapeDtypeStruct((lp, D), F32), jax.ShapeDtypeStruct((lp, D), BF16), jax.ShapeDtypeStruct((lp, D), F32)]
    if ag is None:
        return pl.pallas_call(body, name="fox_attn_fwd", grid=(npair, nq), in_specs=in_specs, out_specs=out_specs,
                              out_shape=out_shape, compiler_params=_params(("parallel", "arbitrary")))(*ins)
    n = 3 * len(ag[1])
    return pl.pallas_call(
        body, name="fox_attn_fwd_ag", grid=(npair, nq), in_specs=in_specs + [ANY], out_specs=out_specs + [ANY],
        out_shape=out_shape + [jax.ShapeDtypeStruct(ag[0].shape, ag[0].dtype)],
        scratch_shapes=[pltpu.SemaphoreType.DMA((n,))] * 4, input_output_aliases={4: 3},
        compiler_params=_params(("arbitrary", "arbitrary")),
    )(*ins, ag[0])


def _fox_gate_bwd(dog, o, proj, lse, qa, swap=None):
    lp = o.shape[0]
    tr = LANES
    steps = lp // tr

    def body(d_ref, o_ref, g_ref, lse_ref, q_ref, *rest):
        if swap is None:
            do_ref, q2_ref, dgate_ref = rest
        else:
            src_ref, do_ref, q2_ref, dgate_ref, got_ref, send_sem, recv_sem = rest
            cp = _swap_copy(src_ref, got_ref, send_sem, recv_sem)

            @pl.when(pl.program_id(0) == 0)
            def _():
                cp.start()

            @pl.when(pl.program_id(0) == steps - 1)
            def _():
                cp.wait()

        down, _ = _head_sel(FOX_H, FOX_DH)
        sg = _sigmoid(g_ref[...])
        dv, ov = d_ref[...], o_ref[...]
        do = (dv * sg).astype(BF16).astype(F32)
        dgate_ref[...] = dv * ov * sg * (1.0 - sg)
        delta = _sel_r(do * ov, down)
        _spread(do, -_placed(delta, L_C), do_ref)
        r_, c_ = _iota((D, LANES), 0), _iota((D, LANES), 1)
        lse_c = _sel_r(lse_ref[...], (r_ == c_ * FOX_DH).astype(BF16))
        q2_ref[...] = (q_ref[...].astype(F32) - _placed(lse_c, L_LSE)).astype(BF16)

    spec = pl.BlockSpec((tr, D), lambda i: (i, 0))
    aug = pl.BlockSpec((tr, FOX_AUG), lambda i: (i, 0))
    in_specs = [spec, spec, pl.BlockSpec((tr, D), lambda i: (i, 3)), spec, aug]
    out_specs = [aug, aug, spec]
    out_shape = [jax.ShapeDtypeStruct((lp, FOX_AUG), BF16), jax.ShapeDtypeStruct((lp, FOX_AUG), BF16),
                 jax.ShapeDtypeStruct((lp, D), F32)]
    if swap is None:
        return pl.pallas_call(body, name="fox_gate_bwd", grid=(steps,), in_specs=in_specs, out_specs=out_specs,
                              out_shape=out_shape, compiler_params=_params(("parallel",)))(dog, o, proj, lse, qa)
    nb, r, c = swap.shape
    return pl.pallas_call(
        body, name="fox_gate_bwd_swap", grid=(steps,), in_specs=in_specs + [ANY], out_specs=out_specs + [ANY],
        out_shape=out_shape + [jax.ShapeDtypeStruct((nb, r // 2, c), swap.dtype)],
        scratch_shapes=[pltpu.SemaphoreType.DMA, pltpu.SemaphoreType.DMA], compiler_params=_params(("arbitrary",)),
    )(dog, o, proj, lse, qa, swap)


def _fox_attn_bwd(q2, ka, va, doa, rs=None):
    lp = q2.shape[0]
    t = _tile(lp, FOX_TQ)
    nb = lp // t
    npair = FOX_H // 2

    def body(q_ref, k_ref, v_ref, do_ref, *rest):
        if rs is None:
            dq_ref, dk_ref, dv_ref, dc_ref, dq_acc, dk_acc, dv_acc, dc_acc = rest
        else:
            s_ref, dq_ref, dk_ref, dv_ref, dc_ref, got_ref, dq_acc, dk_acc, dv_acc, dc_acc, send_sems, recv_sems = rest
            sends, arrivals = _scatter_copies(s_ref, got_ref, send_sems, recv_sems)

            @pl.when((pl.program_id(0) == 0) & (pl.program_id(1) == 0))
            def _():
                for cp in sends:
                    cp.start()

            @pl.when((pl.program_id(0) == npair - 1) & (pl.program_id(1) == nb - 1))
            def _():
                for cp in arrivals:
                    cp.wait_recv()
                for cp in sends:
                    cp.wait_send()

        j = pl.program_id(1)

        @pl.when(j == 0)
        def _():
            dq_acc[...] = jnp.zeros_like(dq_acc)

        causal = _iota((t, t), 1) <= _iota((t, t), 0)
        ks = [k_ref[:, hh * LANES:(hh + 1) * LANES] for hh in range(2)]
        vs = [v_ref[:, hh * LANES:(hh + 1) * LANES] for hh in range(2)]
        dk_acc[...] = jnp.zeros_like(dk_acc)
        dv_acc[...] = jnp.zeros_like(dv_acc)
        dc_acc[...] = jnp.zeros_like(dc_acc)

        def block(i, diag):
            off = pl.multiple_of(i * t, t)
            for hh in range(2):
                q = q_ref[pl.ds(off, t), hh * LANES:(hh + 1) * LANES]
                do = do_ref[pl.ds(off, t), hh * LANES:(hh + 1) * LANES]
                s = nt(q, ks[hh])
                if diag:
                    s = jnp.where(causal, s, -1e30)
                p = jnp.exp(s)
                ds = p * nt(do, vs[hh])
                dc_acc[hh] += jnp.sum(ds, axis=0, keepdims=True)
                dsb = ds.astype(BF16)
                dv_acc[hh] += tn(p.astype(BF16), do)
                dk_acc[hh] += tn(dsb, q)
                dq_acc[hh, pl.ds(off, t), :] += nn(dsb, ks[hh])

        block(j, True)
        below = nb - 1 - j

        def step(u, c):
            block(j + 1 + 2 * u, False)
            block(j + 2 + 2 * u, False)
            return c

        lax.fori_loop(0, below // 2, step, 0)

        @pl.when(below % 2 == 1)
        def _():
            block(nb - 1, False)
        left = _iota((t, LANES), 1) < FOX_DH
        dk_ref[...] = jnp.where(left, dk_acc[0], pltpu.roll(dk_acc[1], FOX_DH, 1))
        dv_ref[...] = jnp.where(left, dv_acc[0], pltpu.roll(dv_acc[1], FOX_DH, 1))
        for hh in range(2):
            dc_ref[hh] = jnp.broadcast_to(-dc_acc[hh], (8, t))

        @pl.when(j == nb - 1)
        def _():
            left = _iota((lp, LANES), 1) < FOX_DH
            dq_ref[...] = jnp.where(left, dq_acc[0], pltpu.roll(dq_acc[1], FOX_DH, 1))

    full = pl.BlockSpec((lp, 2 * LANES), lambda p, j: (0, p))
    kblk = pl.BlockSpec((t, 2 * LANES), lambda p, j: (j, p))
    oblk = pl.BlockSpec((t, LANES), lambda p, j: (j, p))
    in_specs = [full, kblk, kblk, full]
    out_specs = [pl.BlockSpec((lp, LANES), lambda p, j: (0, p)), oblk, oblk, pl.BlockSpec((2, 8, t), lambda p, j: (p, 0, j))]
    out_shape = [jax.ShapeDtypeStruct((lp, D), F32)] * 3 + [jax.ShapeDtypeStruct((FOX_H, 8, lp), F32)]
    scratch = [pltpu.VMEM((2, lp, LANES), F32), pltpu.VMEM((2, t, LANES), F32), pltpu.VMEM((2, t, LANES), F32),
               pltpu.VMEM((2, 1, t), F32)]
    if rs is None:
        return pl.pallas_call(body, name="fox_attn_bwd", grid=(npair, nb), in_specs=in_specs, out_specs=out_specs,
                              out_shape=out_shape, scratch_shapes=scratch,
                              compiler_params=_params(("parallel", "arbitrary")))(q2, ka, va, doa)
    return pl.pallas_call(
        body, name="fox_attn_bwd_rs", grid=(npair, nb), in_specs=in_specs + [ANY], out_specs=out_specs + [ANY],
        out_shape=out_shape + [jax.ShapeDtypeStruct((3,) + rs.shape[1:], rs.dtype)],
        scratch_shapes=scratch + [pltpu.SemaphoreType.DMA((3,)), pltpu.SemaphoreType.DMA((3,))],
        compiler_params=_params(("arbitrary", "arbitrary")),
    )(q2, ka, va, doa, rs)


def _fox_prep_bwd(proj, b_f, q_gain, k_gain, dqn, dkn, dv, dgate, dct):
    lp = proj.shape[0]
    nb = lp // LANES

    def body(p_ref, bf_ref, qg_ref, kg_ref, dq_ref, dk_ref, dv_ref, dg_ref, dc_ref,
             dp_ref, dqg_ref, dkg_ref, dbf_ref, carry):
        i = pl.program_id(0)

        @pl.when(i == 0)
        def _():
            carry[...] = jnp.zeros_like(carry)
            dqg_ref[...] = jnp.zeros_like(dqg_ref)
            dkg_ref[...] = jnp.zeros_like(dkg_ref)
            dbf_ref[...] = jnp.zeros_like(dbf_ref)

        down, up = _head_sel(FOX_H, FOX_DH)

        def norm_bwd(x, gain, dy, scale, dgain_ref):
            ms = _sel_r2(x * x, down) * (1.0 / FOX_DH)
            r = _sel_r2(lax.rsqrt(ms + EPS), up)
            u = dy * gain * scale
            mean_xu = _sel_r2(_sel_r2(x * u, down) * (1.0 / FOX_DH), up)
            dgain_ref[...] += jnp.sum(dy * scale * x * r, axis=0, keepdims=True)
            return r * u - x * (r * r * r) * mean_xu

        dp_ref[:, 0:D] = norm_bwd(p_ref[:, 0:D], qg_ref[...], dq_ref[...], FOX_DH ** -0.5, dqg_ref).astype(BF16)
        dp_ref[:, D:2 * D] = norm_bwd(p_ref[:, D:2 * D], kg_ref[...], dk_ref[...], 1.0, dkg_ref).astype(BF16)
        dp_ref[:, 2 * D:3 * D] = dv_ref[...].astype(BF16)
        dp_ref[:, 3 * D:4 * D] = dg_ref[...].astype(BF16)
        rows = jnp.concatenate([dc_ref[h, 0:1, :] for h in range(FOX_H)] + [jnp.zeros((LANES - FOX_H, LANES), F32)], axis=0)
        dlf = _sel_l(_tri(LANES, upper=True).astype(BF16), rows.T) + carry[0:1, :]
        carry[...] = jnp.broadcast_to(dlf[0:1, :], carry.shape)
        lane = _iota((LANES, LANES), 1)
        z = p_ref[:, 4 * D:4 * D + LANES] + bf_ref[...]
        df = jnp.where(lane < FOX_H, dlf * _sigmoid(-z), 0.0)
        dp_ref[:, 4 * D:4 * D + LANES] = df.astype(BF16)
        dbf_ref[...] += jnp.sum(df, axis=0, keepdims=True)

    rev = lambda i: (nb - 1 - i, 0)
    blk = pl.BlockSpec((LANES, D), rev)
    row = pl.BlockSpec((1, D), lambda i: (0, 0))
    row128 = pl.BlockSpec((1, LANES), lambda i: (0, 0))
    return pl.pallas_call(
        body, name="fox_prep_bwd", grid=(nb,),
        in_specs=[pl.BlockSpec((LANES, FOX_INP), rev), row128, row, row, blk, blk, blk, blk,
                  pl.BlockSpec((FOX_H, 8, LANES), lambda i: (0, 0, nb - 1 - i))],
        out_specs=[pl.BlockSpec((LANES, FOX_INP), rev), row, row, row128],
        out_shape=[jax.ShapeDtypeStruct((lp, FOX_INP), BF16), jax.ShapeDtypeStruct((1, D), F32),
                   jax.ShapeDtypeStruct((1, D), F32), jax.ShapeDtypeStruct((1, LANES), F32)],
        scratch_shapes=[pltpu.VMEM((8, LANES), F32)],
        compiler_params=_params(("arbitrary",)),
    )(proj, jnp.pad(b_f, (0, LANES - FOX_H)).reshape(1, LANES), jnp.tile(q_gain, FOX_H).reshape(1, D),
      jnp.tile(k_gain, FOX_H).reshape(1, D), dqn, dkn, dv, dgate, dct)


def _gla_gates(p_ref, wa_ref, ba_ref):
    a_lr = p_ref[:, 3072:3072 + LANES]
    z = nn(a_lr.astype(BF16), wa_ref[...].astype(BF16)) + ba_ref[...]
    g = _log_sigmoid(z) * (1.0 / GLA_NORM)
    b = _sel_l(_tri(CHUNK).astype(BF16), g)
    return a_lr, z, b


def _gla_chunk_fwd(q, k, v, b, st0):
    hs = range(len(q))
    low = _tri(CHUNK)
    bl = [b[h][CHUNK - 1:CHUNK, :] for h in hs]
    qe = [q[h] * jnp.exp(b[h]) for h in hs]
    ke = [k[h] * jnp.exp(-b[h]) for h in hs]
    kd = [k[h] * jnp.exp(bl[h] - b[h]) for h in hs]
    a = [jnp.where(low, nt(qe[h], ke[h]), 0.0) for h in hs]
    o = [nn(a[h], v[h]) + nt(qe[h], st0[h]) for h in hs]
    st1 = [st0[h] * jnp.exp(bl[h]) + tn(v[h], kd[h]) for h in hs]
    return o, st1, (qe, ke, kd, a, bl)


def _gla_slices(p_ref, b_all, h):
    q = p_ref[:, h * GLA_DK:(h + 1) * GLA_DK] * (GLA_DK ** -0.5)
    k = p_ref[:, GLA_QK + h * GLA_DK:GLA_QK + (h + 1) * GLA_DK]
    v = p_ref[:, 2 * GLA_QK + h * GLA_DV:2 * GLA_QK + (h + 1) * GLA_DV]
    r = p_ref[:, 2 * GLA_QK + GLA_V + h * GLA_DV:2 * GLA_QK + GLA_V + (h + 1) * GLA_DV]
    return q, k, v, r, b_all[:, h * GLA_DK:(h + 1) * GLA_DK]


def _gla_fwd(proj, w_alpha2, b_alpha, o_gain):
    lp = proj.shape[0]
    nc = lp // CHUNK

    def body(p_ref, wa_ref, ba_ref, og_ref, o_ref, y_ref, s_ref, st):
        @pl.when(pl.program_id(0) == 0)
        def _():
            st[...] = jnp.zeros_like(st)

        _, _, b_all = _gla_gates(p_ref, wa_ref, ba_ref)
        hs = range(GLA_H)
        parts = [_gla_slices(p_ref, b_all, h) for h in hs]
        st0 = [st[h] for h in hs]
        for h in hs:
            s_ref[0, h] = st0[h]
        o, st1, _ = _gla_chunk_fwd([p[0] for p in parts], [p[1] for p in parts], [p[2] for p in parts],
                                   [p[4] for p in parts], st0)
        for h in hs:
            st[h] = st1[h]
            o_ref[:, h * GLA_DV:(h + 1) * GLA_DV] = o[h]
            rs = lax.rsqrt(jnp.mean(o[h] * o[h], axis=-1, keepdims=True) + EPS)
            y_ref[:, h * GLA_DV:(h + 1) * GLA_DV] = (o[h] * rs * og_ref[...] * _silu(parts[h][3])).astype(BF16)

    blk = pl.BlockSpec((CHUNK, D), lambda i: (i, 0))
    return pl.pallas_call(
        body, name="gla_fwd", grid=(nc,),
        in_specs=[pl.BlockSpec((CHUNK, GLA_INP), lambda i: (i, 0)), pl.BlockSpec((LANES, GLA_QK), lambda i: (0, 0)),
                  pl.BlockSpec((1, GLA_QK), lambda i: (0, 0)), pl.BlockSpec((1, GLA_DV), lambda i: (0, 0))],
        out_specs=[blk, blk, pl.BlockSpec((1, GLA_H, GLA_DV, GLA_DK), lambda i: (i, 0, 0, 0))],
        out_shape=[jax.ShapeDtypeStruct((lp, D), F32), jax.ShapeDtypeStruct((lp, D), BF16),
                   jax.ShapeDtypeStruct((nc, GLA_H, GLA_DV, GLA_DK), F32)],
        scratch_shapes=[pltpu.VMEM((GLA_H, GLA_DV, GLA_DK), F32)],
        compiler_params=_params(("arbitrary",)),
    )(proj, jnp.pad(w_alpha2, ((0, LANES - GLA_RANK), (0, 0))), b_alpha.reshape(1, GLA_QK), o_gain.reshape(1, GLA_DV))


def _gla_bwd(proj, w_alpha2, b_alpha, o_gain, o, states, dy):
    lp = proj.shape[0]
    nc = lp // CHUNK

    def body(p_ref, wa_ref, ba_ref, og_ref, o_ref, s_ref, dy_ref, dp_ref, dwa_ref, dba_ref, dog_ref, dst):
        @pl.when(pl.program_id(0) == 0)
        def _():
            dst[...] = jnp.zeros_like(dst)
            dwa_ref[...] = jnp.zeros_like(dwa_ref)
            dba_ref[...] = jnp.zeros_like(dba_ref)
            dog_ref[...] = jnp.zeros_like(dog_ref)

        a_lr, z, b_all = _gla_gates(p_ref, wa_ref, ba_ref)
        last_row = _iota((CHUNK, GLA_DK), 0) == CHUNK - 1
        rev = _tri(CHUNK, upper=True).astype(BF16)
        hs = range(GLA_H)
        scale = GLA_DK ** -0.5
        parts = [_gla_slices(p_ref, b_all, h) for h in hs]
        q, k, v, b = [p[0] for p in parts], [p[1] for p in parts], [p[2] for p in parts], [p[4] for p in parts]
        st0 = [s_ref[0, h] for h in hs]
        dst1 = [dst[h] for h in hs]
        do = []
        for h in hs:
            r = parts[h][3]
            ov = o_ref[:, h * GLA_DV:(h + 1) * GLA_DV]
            dyv = dy_ref[:, h * GLA_DV:(h + 1) * GLA_DV]
            rs = lax.rsqrt(jnp.mean(ov * ov, axis=-1, keepdims=True) + EPS)
            on = ov * rs
            dp_ref[:, 2 * GLA_QK + GLA_V + h * GLA_DV:2 * GLA_QK + GLA_V + (h + 1) * GLA_DV] = (
                dyv * on * og_ref[...] * _dsilu(r)).astype(BF16)
            don = dyv * _silu(r)
            dog_ref[...] += jnp.sum(don * on, axis=0, keepdims=True)
            u = don * og_ref[...]
            do.append(rs * u - ov * (rs * rs * rs) * jnp.mean(ov * u, axis=-1, keepdims=True))
        _, _, (qe, ke, kd, a, bl) = _gla_chunk_fwd(q, k, v, b, st0)
        low = _tri(CHUNK)
        da = [jnp.where(low, nt(do[h], v[h]), 0.0) for h in hs]
        dkd = [nn(v[h], dst1[h]) for h in hs]
        dvv = [tn(a[h], do[h]) + nt(kd[h], dst1[h]) for h in hs]
        dqe = [nn(da[h], ke[h]) + nn(do[h], st0[h]) for h in hs]
        dke = [tn(da[h], qe[h]) for h in hs]
        dg_parts = []
        for h in hs:
            ebl = jnp.exp(bl[h])
            dst[h] = dst1[h] * ebl + tn(do[h], qe[h])
            db = dqe[h] * qe[h] - dke[h] * ke[h] - dkd[h] * kd[h]
            db_last = (jnp.sum(dkd[h] * kd[h], axis=0, keepdims=True)
                       + jnp.sum(dst1[h] * st0[h], axis=0, keepdims=True) * ebl)
            db = db + jnp.where(last_row, db_last, 0.0)
            dg_parts.append(_sel_l(rev, db))
            dp_ref[:, h * GLA_DK:(h + 1) * GLA_DK] = (dqe[h] * jnp.exp(b[h]) * scale).astype(BF16)
            dp_ref[:, GLA_QK + h * GLA_DK:GLA_QK + (h + 1) * GLA_DK] = (
                dke[h] * jnp.exp(-b[h]) + dkd[h] * jnp.exp(bl[h] - b[h])).astype(BF16)
            dp_ref[:, 2 * GLA_QK + h * GLA_DV:2 * GLA_QK + (h + 1) * GLA_DV] = dvv[h].astype(BF16)
        dg = jnp.concatenate(dg_parts, axis=1)
        dz = dg * (1.0 / GLA_NORM) * _sigmoid(-z)
        dzb = dz.astype(BF16)
        dp_ref[:, 3072:3072 + LANES] = nt(dzb, wa_ref[...].astype(BF16)).astype(BF16)
        dwa_ref[...] += tn(a_lr.astype(BF16), dzb)
        dba_ref[...] += jnp.sum(dz, axis=0, keepdims=True)

    rv = lambda i: (nc - 1 - i, 0)
    blk = pl.BlockSpec((CHUNK, D), rv)
    fixed = lambda r, c: pl.BlockSpec((r, c), lambda i: (0, 0))
    return pl.pallas_call(
        body, name="gla_bwd", grid=(nc,),
        in_specs=[pl.BlockSpec((CHUNK, GLA_INP), rv), fixed(LANES, GLA_QK), fixed(1, GLA_QK), fixed(1, GLA_DV), blk,
                  pl.BlockSpec((1, GLA_H, GLA_DV, GLA_DK), lambda i: (nc - 1 - i, 0, 0, 0)), blk],
        out_specs=[pl.BlockSpec((CHUNK, GLA_INP), rv), fixed(LANES, GLA_QK), fixed(1, GLA_QK), fixed(1, GLA_DV)],
        out_shape=[jax.ShapeDtypeStruct((lp, GLA_INP), BF16), jax.ShapeDtypeStruct((LANES, GLA_QK), F32),
                   jax.ShapeDtypeStruct((1, GLA_QK), F32), jax.ShapeDtypeStruct((1, GLA_DV), F32)],
        scratch_shapes=[pltpu.VMEM((GLA_H, GLA_DV, GLA_DK), F32)],
        compiler_params=_params(("arbitrary",)),
    )(proj, jnp.pad(w_alpha2, ((0, LANES - GLA_RANK), (0, 0))), b_alpha.reshape(1, GLA_QK), o_gain.reshape(1, GLA_DV),
      o, states, dy)


HI = lax.Precision.HIGH


def _gdn_pre(prev_ref, p_ref, cw_ref, al_ref, dt_ref):
    xc = jnp.concatenate([prev_ref[:, 0:GDN_CONV], p_ref[:, 0:GDN_CONV]], axis=0)
    shifted = [pltpu.roll(xc, 3 - j, 0)[CHUNK:, :] if j < 3 else xc[CHUNK:, :] for j in range(4)]
    conv = sum(shifted[j] * cw_ref[j:j + 1, :] for j in range(4))
    act = _silu(conv)
    slab = p_ref[:, 4096:4096 + LANES]
    lane = _iota((CHUNK, LANES), 1)
    zs = slab + dt_ref[...]
    g = jnp.where(lane < GDN_H, -jnp.exp(al_ref[...]) * _softplus(zs), 0.0)
    bs = _sel_l(_tri(CHUNK).astype(BF16), g)
    beta = _sigmoid(slab)
    return shifted, conv, act, slab, zs, g, bs, beta


def _l2n(x):
    r = lax.rsqrt(jnp.sum(x * x, axis=-1, keepdims=True) + EPS)
    return x * r, r


def _gdn_chunk_fwd(q, k, v, beta, bcol, brow, s0, tinv=None):
    hs = range(len(q))
    ii, jj = _iota((CHUNK, CHUNK), 0), _iota((CHUNK, CHUNK), 1)
    low, eye = ii >= jj, (ii == jj).astype(F32)
    dm = [jnp.where(low, jnp.exp(jnp.where(low, bcol[h] - brow[h], 0.0)), 0.0) for h in hs]
    dstrict = [jnp.where(ii > jj, dm[h], 0.0) for h in hs]
    eb = [jnp.exp(bcol[h]) for h in hs]
    bl = [bcol[h][CHUNK - 1:CHUNK, :] for h in hs]
    kb = [k[h] * beta[h] for h in hs]
    vb = [v[h] * beta[h] for h in hs]
    nmat = [nt(kb[h], k[h]) * dstrict[h] for h in hs]
    if tinv is None:
        x = [eye - nmat[h] for h in hs]
        pw = [nn(nmat[h], nmat[h], precision=HI) for h in hs]
        for it in range(5):
            x = [x[h] + nn(x[h], pw[h], precision=HI) for h in hs]
            if it < 4:
                pw = [nn(pw[h], pw[h], precision=HI) for h in hs]
    else:
        x = tinv
    kbe = [kb[h] * eb[h] for h in hs]
    u = [nn(x[h], vb[h], precision=HI) for h in hs]
    w = [nn(x[h], kbe[h], precision=HI) for h in hs]
    vn = [u[h] - nn(w[h], s0[h]) for h in hs]
    pm = [nt(q[h], k[h]) * dm[h] for h in hs]
    qe = [q[h] * eb[h] for h in hs]
    o = [nn(pm[h], vn[h]) + nn(qe[h], s0[h]) for h in hs]
    kd = [k[h] * jnp.exp(bl[h] - bcol[h]) for h in hs]
    s1 = [s0[h] * jnp.exp(bl[h]) + tn(kd[h], vn[h]) for h in hs]
    return o, s1, dict(dm=dm, dstrict=dstrict, eb=eb, bl=bl, kb=kb, vb=vb, nmat=nmat, tinv=x, kbe=kbe, u=u, w=w, vn=vn,
                       pm=pm, qe=qe, kd=kd)


def _gdn_heads(act, beta_slab, bs, h):
    qa = act[:, h * GDN_DK:(h + 1) * GDN_DK]
    ka = act[:, GDN_H * GDN_DK + h * GDN_DK:GDN_H * GDN_DK + (h + 1) * GDN_DK]
    v = act[:, 2 * GDN_H * GDN_DK + h * GDN_DV:2 * GDN_H * GDN_DK + (h + 1) * GDN_DV]
    return qa, ka, v, beta_slab[:, GDN_H + h:GDN_H + h + 1], bs[:, h:h + 1]


def _gdn_fwd(proj, conv_w, a_log, dt_bias, o_gain):
    lp = proj.shape[0]
    nc = lp // CHUNK

    def body(prev_ref, p_ref, cw_ref, al_ref, dt_ref, og_ref, o_ref, y_ref, s_ref, t_ref, st):
        @pl.when(pl.program_id(0) == 0)
        def _():
            st[...] = jnp.zeros_like(st)

        _, _, act, _, _, _, bs, beta = _gdn_pre(prev_ref, p_ref, cw_ref, al_ref, dt_ref)
        bst = bs.T
        hs = range(GDN_H)
        parts = [_gdn_heads(act, beta, bs, h) for h in hs]
        q = [_l2n(parts[h][0])[0] * (GDN_DK ** -0.5) for h in hs]
        k = [_l2n(parts[h][1])[0] for h in hs]
        s0 = [st[h] for h in hs]
        for h in hs:
            s_ref[0, h] = s0[h]
        o, s1, f = _gdn_chunk_fwd(q, k, [parts[h][2] for h in hs], [parts[h][3] for h in hs], [parts[h][4] for h in hs],
                                  [bst[h:h + 1, :] for h in hs], s0)
        for h in hs:
            t_ref[0, h] = f["tinv"][h]
            st[h] = s1[h]
            o_ref[:, h * GDN_DV:(h + 1) * GDN_DV] = o[h]
            rs = lax.rsqrt(jnp.mean(o[h] * o[h], axis=-1, keepdims=True) + EPS)
            gate = p_ref[:, GDN_CONV + h * GDN_DV:GDN_CONV + (h + 1) * GDN_DV]
            y_ref[:, h * GDN_DV:(h + 1) * GDN_DV] = (o[h] * rs * og_ref[...] * _silu(gate)).astype(BF16)

    blk = pl.BlockSpec((CHUNK, D), lambda i: (i, 0))
    fixed = lambda r, c: pl.BlockSpec((r, c), lambda i: (0, 0))
    return pl.pallas_call(
        body, name="gdn_fwd", grid=(nc,),
        in_specs=[pl.BlockSpec((CHUNK, GDN_INP), lambda i: (jnp.maximum(i - 1, 0), 0)),
                  pl.BlockSpec((CHUNK, GDN_INP), lambda i: (i, 0)), fixed(8, GDN_CONV), fixed(1, LANES), fixed(1, LANES),
                  fixed(1, GDN_DV)],
        out_specs=[blk, blk, pl.BlockSpec((1, GDN_H, GDN_DK, GDN_DV), lambda i: (i, 0, 0, 0)),
                   pl.BlockSpec((1, GDN_H, CHUNK, CHUNK), lambda i: (i, 0, 0, 0))],
        out_shape=[jax.ShapeDtypeStruct((lp, D), F32), jax.ShapeDtypeStruct((lp, D), BF16),
                   jax.ShapeDtypeStruct((nc, GDN_H, GDN_DK, GDN_DV), F32), jax.ShapeDtypeStruct((nc, GDN_H, CHUNK, CHUNK), F32)],
        scratch_shapes=[pltpu.VMEM((GDN_H, GDN_DK, GDN_DV), F32)],
        compiler_params=_params(("arbitrary",)),
    )(proj, proj, jnp.pad(conv_w.reshape(4, GDN_CONV), ((0, 4), (0, 0))), jnp.pad(a_log, (0, LANES - GDN_H)).reshape(1, LANES),
      jnp.pad(dt_bias, (0, LANES - GDN_H)).reshape(1, LANES), o_gain.reshape(1, GDN_DV))


def _gdn_bwd(proj, conv_w, a_log, dt_bias, o_gain, o, states, tinvs, dy):
    lp = proj.shape[0]
    nc = lp // CHUNK

    def body(prev_ref, p_ref, cw_ref, al_ref, dt_ref, og_ref, o_ref, s_ref, t_ref, dy_ref,
             dp_ref, dcw_ref, dal_ref, ddt_ref, dog_ref, dst, dconv_next):
        @pl.when(pl.program_id(0) == 0)
        def _():
            dst[...] = jnp.zeros_like(dst)
            dconv_next[...] = jnp.zeros_like(dconv_next)
            dcw_ref[...] = jnp.zeros_like(dcw_ref)
            dal_ref[...] = jnp.zeros_like(dal_ref)
            ddt_ref[...] = jnp.zeros_like(ddt_ref)
            dog_ref[...] = jnp.zeros_like(dog_ref)

        shifted, conv, act, slab, zs, g, bs, beta = _gdn_pre(prev_ref, p_ref, cw_ref, al_ref, dt_ref)
        bst = bs.T
        lane = _iota((CHUNK, LANES), 1)
        ones = jnp.ones((CHUNK, LANES), F32)
        db_slab = jnp.zeros((CHUNK, LANES), F32)
        dbeta_slab = jnp.zeros((CHUNK, LANES), F32)
        last_row = _iota((CHUNK, 1), 0) == CHUNK - 1
        hs = range(GDN_H)
        scale = GDN_DK ** -0.5
        parts = [_gdn_heads(act, beta, bs, h) for h in hs]
        qa, ka, v = [parts[h][0] for h in hs], [parts[h][1] for h in hs], [parts[h][2] for h in hs]
        bet, bcol = [parts[h][3] for h in hs], [parts[h][4] for h in hs]
        qn_ = [_l2n(qa[h]) for h in hs]
        kn_ = [_l2n(ka[h]) for h in hs]
        q = [qn_[h][0] * scale for h in hs]
        k, rq, rk = [kn_[h][0] for h in hs], [qn_[h][1] for h in hs], [kn_[h][1] for h in hs]
        s0 = [s_ref[0, h] for h in hs]
        ds1 = [dst[h] for h in hs]
        do = []
        for h in hs:
            ov = o_ref[:, h * GDN_DV:(h + 1) * GDN_DV]
            dyv = dy_ref[:, h * GDN_DV:(h + 1) * GDN_DV]
            gate = p_ref[:, GDN_CONV + h * GDN_DV:GDN_CONV + (h + 1) * GDN_DV]
            rs = lax.rsqrt(jnp.mean(ov * ov, axis=-1, keepdims=True) + EPS)
            on = ov * rs
            dp_ref[:, GDN_CONV + h * GDN_DV:GDN_CONV + (h + 1) * GDN_DV] = (dyv * on * og_ref[...] * _dsilu(gate)).astype(BF16)
            don = dyv * _silu(gate)
            dog_ref[...] += jnp.sum(don * on, axis=0, keepdims=True)
            uu = don * og_ref[...]
            do.append(rs * uu - ov * (rs * rs * rs) * jnp.mean(ov * uu, axis=-1, keepdims=True))
        _, _, f = _gdn_chunk_fwd(q, k, v, bet, bcol, [bst[h:h + 1, :] for h in hs], s0, tinv=[t_ref[0, h] for h in hs])
        dm, dstrict, eb, bl, kb, nmat, tinv = f["dm"], f["dstrict"], f["eb"], f["bl"], f["kb"], f["nmat"], f["tinv"]
        kbe, u, w, vn, pm, qe, kd = f["kbe"], f["u"], f["w"], f["vn"], f["pm"], f["qe"], f["kd"]
        ebl = [jnp.exp(bl[h]) for h in hs]
        dvn = [tn(pm[h], do[h]) + nn(kd[h], ds1[h]) for h in hs]
        dpr = [nt(do[h], vn[h]) for h in hs]
        dqe = [nt(do[h], s0[h]) for h in hs]
        dkd = [nt(vn[h], ds1[h]) for h in hs]
        for h in hs:
            dst[h] = ds1[h] * ebl[h] + tn(qe[h], do[h]) - tn(w[h], dvn[h])
        du_ = [tn(tinv[h], dvn[h], precision=HI) for h in hs]
        dw_ = [tn(tinv[h], -nt(dvn[h], s0[h]), precision=HI) for h in hs]
        dn = [-(nt(du_[h], u[h]) + nt(dw_[h], w[h])) for h in hs]
        dqk = [dpr[h] * dm[h] for h in hs]
        dkk = [dn[h] * dstrict[h] for h in hs]
        gsum = [dpr[h] * pm[h] + dn[h] * nmat[h] for h in hs]
        dkb = [nn(dkk[h], k[h]) + dw_[h] * eb[h] for h in hs]
        dk = [tn(dkk[h], kb[h]) + tn(dqk[h], q[h]) + dkd[h] * jnp.exp(bl[h] - bcol[h]) + dkb[h] * bet[h] for h in hs]
        dq = [nn(dqk[h], k[h]) + dqe[h] * eb[h] for h in hs]
        colsum = [tn(gsum[h], ones, precision=HI)[:, 0:1] for h in hs]
        dact_q, dact_k, dact_v = [], [], []
        for h in hs:
            dbeta = jnp.sum(dkb[h] * k[h], axis=-1, keepdims=True) + jnp.sum(du_[h] * v[h], axis=-1, keepdims=True)
            skd = jnp.sum(dkd[h] * kd[h], axis=-1, keepdims=True)
            db = (jnp.sum(gsum[h], axis=-1, keepdims=True) - colsum[h] + jnp.sum(dqe[h] * qe[h], axis=-1, keepdims=True)
                  + jnp.sum(dw_[h] * kbe[h], axis=-1, keepdims=True) - skd)
            db_last = jnp.sum(skd, axis=0, keepdims=True) + jnp.sum(ds1[h] * s0[h]) * ebl[h]
            db = db + jnp.where(last_row, db_last, 0.0)
            db_slab = db_slab + jnp.where(lane == h, db, 0.0)
            dbeta_slab = dbeta_slab + jnp.where(lane == GDN_H + h, dbeta, 0.0)
            dqn = dq[h] * scale
            dact_q.append(rq[h] * dqn - qa[h] * (rq[h] * rq[h] * rq[h]) * jnp.sum(qa[h] * dqn, axis=-1, keepdims=True))
            dact_k.append(rk[h] * dk[h] - ka[h] * (rk[h] * rk[h] * rk[h]) * jnp.sum(ka[h] * dk[h], axis=-1, keepdims=True))
            dact_v.append(du_[h] * bet[h])
        dact = jnp.concatenate(dact_q + dact_k + dact_v, axis=1)
        dconv = dact * _dsilu(conv)
        for j in range(4):
            dcw_ref[j:j + 1, :] += jnp.sum(dconv * shifted[j], axis=0, keepdims=True)
        dcat = jnp.concatenate([dconv, dconv_next[...]], axis=0)
        dx = dconv * cw_ref[3:4, :]
        for j in range(3):
            dx = dx + pltpu.roll(dcat, 2 * CHUNK - (3 - j), 0)[:CHUNK, :] * cw_ref[j:j + 1, :]
        dconv_next[...] = dconv
        dp_ref[:, 0:GDN_CONV] = dx.astype(BF16)
        dg = _sel_l(_tri(CHUNK, upper=True).astype(BF16), db_slab)
        da = dg * (-jnp.exp(al_ref[...])) * _sigmoid(zs)
        da = jnp.where(lane < GDN_H, da, 0.0)
        dal_ref[...] += jnp.sum(dg * g, axis=0, keepdims=True)
        ddt_ref[...] += jnp.sum(da, axis=0, keepdims=True)
        dp_ref[:, 4096:4096 + LANES] = (da + dbeta_slab * beta * (1.0 - beta)).astype(BF16)

    rv = lambda i: (nc - 1 - i, 0)
    blk = pl.BlockSpec((CHUNK, D), rv)
    fixed = lambda r, c: pl.BlockSpec((r, c), lambda i: (0, 0))
    return pl.pallas_call(
        body, name="gdn_bwd", grid=(nc,),
        in_specs=[pl.BlockSpec((CHUNK, GDN_INP), lambda i: (jnp.maximum(nc - 2 - i, 0), 0)),
                  pl.BlockSpec((CHUNK, GDN_INP), rv), fixed(8, GDN_CONV), fixed(1, LANES), fixed(1, LANES), fixed(1, GDN_DV),
                  blk, pl.BlockSpec((1, GDN_H, GDN_DK, GDN_DV), lambda i: (nc - 1 - i, 0, 0, 0)),
                  pl.BlockSpec((1, GDN_H, CHUNK, CHUNK), lambda i: (nc - 1 - i, 0, 0, 0)), blk],
        out_specs=[pl.BlockSpec((CHUNK, GDN_INP), rv), fixed(8, GDN_CONV), fixed(1, LANES), fixed(1, LANES), fixed(1, GDN_DV)],
        out_shape=[jax.ShapeDtypeStruct((lp, GDN_INP), BF16), jax.ShapeDtypeStruct((8, GDN_CONV), F32),
                   jax.ShapeDtypeStruct((1, LANES), F32), jax.ShapeDtypeStruct((1, LANES), F32),
                   jax.ShapeDtypeStruct((1, GDN_DV), F32)],
        scratch_shapes=[pltpu.VMEM((GDN_H, GDN_DK, GDN_DV), F32), pltpu.VMEM((CHUNK, GDN_CONV), F32)],
        compiler_params=_params(("arbitrary",)),
    )(proj, proj, jnp.pad(conv_w.reshape(4, GDN_CONV), ((0, 4), (0, 0))), jnp.pad(a_log, (0, LANES - GDN_H)).reshape(1, LANES),
      jnp.pad(dt_bias, (0, LANES - GDN_H)).reshape(1, LANES), o_gain.reshape(1, GDN_DV), o, states, tinvs, dy)


def _coords():
    return lax.axis_index("x"), lax.axis_index("y"), lax.axis_index("c")


def _other_chips(x, y):
    return [(1 - x, y, 2 * (1 - x) + y), (x, 1 - y, 2 * x + 1 - y), (1 - x, 1 - y, 2 * (1 - x) + 1 - y)]


def _gather8(v, *, reduce, name):
    r, c = v.shape

    def body(v_ref, out_ref, *scratch):
        if reduce:
            buf, send_sems, recv_sems = scratch
        else:
            buf = out_ref
            send_sems, recv_sems = scratch
        x, y, cc = _coords()
        me = 4 * x + 2 * y + cc
        buf[me] = v_ref[...]
        copies = []
        for k in range(1, 8):
            px, py, pc = x ^ (k >> 2), y ^ ((k >> 1) & 1), cc ^ (k & 1)
            copies.append(pltpu.make_async_remote_copy(
                src_ref=v_ref, dst_ref=buf.at[me], send_sem=send_sems.at[k - 1], recv_sem=recv_sems.at[k - 1],
                device_id=(px, py, pc), device_id_type=MESH))
        for cp in copies:
            cp.start()
        for k in range(1, 8):
            peer = (x ^ (k >> 2)) * 4 + (y ^ ((k >> 1) & 1)) * 2 + (cc ^ (k & 1))
            pltpu.make_async_remote_copy(
                src_ref=v_ref, dst_ref=buf.at[peer], send_sem=send_sems.at[k - 1], recv_sem=recv_sems.at[k - 1],
                device_id=(x, y, cc), device_id_type=MESH).wait_recv()
        for cp in copies:
            cp.wait_send()
        if reduce:
            acc = buf[0]
            for d in range(1, 8):
                acc = acc + buf[d]
            out_ref[...] = acc

    scratch = [pltpu.SemaphoreType.DMA((7,)), pltpu.SemaphoreType.DMA((7,))]
    if reduce:
        scratch = [pltpu.VMEM((8, r, c), F32)] + scratch
    return pl.pallas_call(
        body, name=name, in_specs=[VM], out_specs=VM,
        out_shape=jax.ShapeDtypeStruct((r, c) if reduce else (8, r, c), F32),
        scratch_shapes=scratch, compiler_params=_params(),
    )(v)


class _AgCopies:
    def __init__(self, buf, ranges, send_sems, recv_sems):
        self.buf, self.ranges, self.send_sems, self.recv_sems = buf, ranges, send_sems, recv_sems
        self.x, self.y, self.cc = _coords()
        self.p = 2 * self.x + self.y
        self.chips = _other_chips(self.x, self.y)

    def rows(self, chip, r, hf):
        start, n = self.ranges[r]
        return self.buf.at[chip, pl.ds(start + hf * (n // 2), n // 2), :]

    def _copy(self, r, k, chip, hf, to):
        return pltpu.make_async_remote_copy(
            src_ref=self.rows(chip, r, hf), dst_ref=self.rows(chip, r, hf), send_sem=self.send_sems.at[3 * r + k],
            recv_sem=self.recv_sems.at[3 * r + k], device_id=to, device_id_type=MESH)

    def pairs(self):
        return [(r, k) for r in range(len(self.ranges)) for k in range(3)]

    def ici(self, r, k):
        cx, cy, _ = self.chips[k]
        return self._copy(r, k, self.p, self.cc, (cx, cy, self.cc))

    def ici_arrival(self, r, k):
        return self._copy(r, k, self.chips[k][2], self.cc, (self.x, self.y, self.cc))

    def forward(self, r, k):
        return self._copy(r, k, self.chips[k][2], self.cc, (self.x, self.y, 1 - self.cc))

    def forward_arrival(self, r, k):
        return self._copy(r, k, self.chips[k][2], 1 - self.cc, (self.x, self.y, self.cc))


def _ag_weights(w4, ranges):
    n = 3 * len(ranges)

    def body(w_ref, out_ref, send1, recv1, send2, recv2):
        ici, fwd = _AgCopies(out_ref, ranges, send1, recv1), _AgCopies(out_ref, ranges, send2, recv2)
        for r, k in ici.pairs():
            ici.ici(r, k).start()
        for r, k in ici.pairs():
            ici.ici_arrival(r, k).wait_recv()
            fwd.forward(r, k).start()
        for r, k in ici.pairs():
            fwd.forward_arrival(r, k).wait_recv()
        for r, k in ici.pairs():
            ici.ici(r, k).wait_send()
            fwd.forward(r, k).wait_send()

    return pl.pallas_call(
        body, name="ag_weights", in_specs=[ANY], out_specs=ANY, out_shape=jax.ShapeDtypeStruct(w4.shape, w4.dtype),
        scratch_shapes=[pltpu.SemaphoreType.DMA((n,))] * 4, input_output_aliases={0: 0}, compiler_params=_params(),
    )(w4)


def _swap_copy(g_ref, out_ref, send_sem, recv_sem):
    x, y, cc = _coords()
    half = g_ref.shape[1] // 2
    return pltpu.make_async_remote_copy(
        src_ref=g_ref.at[:, pl.ds((1 - cc) * half, half), :], dst_ref=out_ref, send_sem=send_sem, recv_sem=recv_sem,
        device_id=(x, y, 1 - cc), device_id_type=MESH)


def _swap_halves(g, *, name):
    nb, r, c = g.shape
    half = r // 2

    def body(g_ref, out_ref, send_sem, recv_sem):
        cp = _swap_copy(g_ref, out_ref, send_sem, recv_sem)
        cp.start()
        cp.wait()

    return pl.pallas_call(
        body, name=name, in_specs=[ANY], out_specs=ANY, out_shape=jax.ShapeDtypeStruct((nb, half, c), g.dtype),
        scratch_shapes=[pltpu.SemaphoreType.DMA, pltpu.SemaphoreType.DMA], compiler_params=_params(),
    )(g)


def _my_half_index():
    return lax.axis_index("c").astype(jnp.int32).reshape(1)


def _add_halves(g, got, tag):
    nb, r, c = g.shape
    half = r // 2
    tr = _tile(half, 512, 16)
    nt_ = half // tr

    def body(c_ref, a_ref, b_ref, o_ref):
        o_ref[...] = (a_ref[...].astype(F32) + b_ref[...].astype(F32)).astype(BF16)

    return pl.pallas_call(
        body, name=f"rs_add_sibling{tag}",
        grid_spec=pltpu.PrefetchScalarGridSpec(
            num_scalar_prefetch=1, grid=(nb, nt_),
            in_specs=[pl.BlockSpec((1, tr, c), lambda b, i, cr: (b, cr[0] * nt_ + i, 0)),
                      pl.BlockSpec((1, tr, c), lambda b, i, cr: (b, i, 0))],
            out_specs=pl.BlockSpec((1, tr, c), lambda b, i, cr: (b, i, 0))),
        out_shape=jax.ShapeDtypeStruct((nb, half, c), BF16), compiler_params=_params(("parallel", "parallel")),
    )(_my_half_index(), g, got)


def _scatter_copies(s_ref, out_ref, send_sems, recv_sems):
    x, y, cc = _coords()
    sends = [pltpu.make_async_remote_copy(
        src_ref=s_ref.at[blk], dst_ref=out_ref.at[k], send_sem=send_sems.at[k], recv_sem=recv_sems.at[k],
        device_id=(cx, cy, cc), device_id_type=MESH) for k, (cx, cy, blk) in enumerate(_other_chips(x, y))]
    arrivals = [pltpu.make_async_remote_copy(
        src_ref=s_ref.at[2 * x + y], dst_ref=out_ref.at[k], send_sem=send_sems.at[k], recv_sem=recv_sems.at[k],
        device_id=(x, y, cc), device_id_type=MESH) for k in range(3)]
    return sends, arrivals


def _scatter_chips(s, tag):
    nb, hrows, c = s.shape

    def body(s_ref, out_ref, send_sems, recv_sems):
        sends, arrivals = _scatter_copies(s_ref, out_ref, send_sems, recv_sems)
        for cp in sends:
            cp.start()
        for cp in arrivals:
            cp.wait_recv()
        for cp in sends:
            cp.wait_send()

    return pl.pallas_call(
        body, name=f"rs_scatter{tag}", in_specs=[ANY], out_specs=ANY, out_shape=jax.ShapeDtypeStruct((3, hrows, c), s.dtype),
        scratch_shapes=[pltpu.SemaphoreType.DMA((3,)), pltpu.SemaphoreType.DMA((3,))], compiler_params=_params(),
    )(s)


def _sum_chips(s, got, tag):
    nb, hrows, c = s.shape
    tr = _tile(hrows, 512, 16)

    def body(idx_ref, own_ref, got_ref, o_ref):
        p = idx_ref[0]
        own = own_ref[0].astype(F32)
        parts = [got_ref[k].astype(F32) for k in range(3)]
        acc = jnp.zeros_like(own)
        for q in range(4):
            val = own
            for k, rel in enumerate((2, 1, 3)):
                val = jnp.where((p ^ rel) == q, parts[k], val)
            acc = acc + val
        o_ref[...] = acc

    idx = (2 * lax.axis_index("x") + lax.axis_index("y")).astype(jnp.int32).reshape(1)
    return pl.pallas_call(
        body, name=f"rs_sum_chips{tag}",
        grid_spec=pltpu.PrefetchScalarGridSpec(
            num_scalar_prefetch=1, grid=(hrows // tr,),
            in_specs=[pl.BlockSpec((1, tr, c), lambda i, pr: (pr[0], i, 0)), pl.BlockSpec((3, tr, c), lambda i, pr: (0, i, 0))],
            out_specs=pl.BlockSpec((tr, c), lambda i, pr: (i, 0))),
        out_shape=jax.ShapeDtypeStruct((hrows, c), F32), compiler_params=_params(("parallel",)),
    )(idx, s, got)


def _swap_sibling(t, tag):
    def body(t_ref, out_ref, send_sem, recv_sem):
        x, y, cc = _coords()
        cp = pltpu.make_async_remote_copy(src_ref=t_ref, dst_ref=out_ref, send_sem=send_sem, recv_sem=recv_sem,
                                          device_id=(x, y, 1 - cc), device_id_type=MESH)
        cp.start()
        cp.wait()

    return pl.pallas_call(
        body, name=f"rs_join{tag}", in_specs=[ANY], out_specs=ANY, out_shape=jax.ShapeDtypeStruct(t.shape, t.dtype),
        scratch_shapes=[pltpu.SemaphoreType.DMA, pltpu.SemaphoreType.DMA], compiler_params=_params(),
    )(t)


def _rs_local(g, tag):
    return _add_halves(g, _swap_halves(g, name=f"rs_swap{tag}"), tag)


def _rs_finish(s, recv, tag):
    t = _sum_chips(s, recv, tag)
    r = _swap_sibling(t, tag)
    first = lax.axis_index("c") == 0
    return jnp.concatenate([jnp.where(first, t, r), jnp.where(first, r, t)], axis=0)


_SMALL_SHARDED = (("meta_tokens", 1), ("gla_w_alpha2", 2), ("gdn_conv_w", 3))
_REPLICATED = ("norm_mix", "norm_ffn", "fox_b_f", "fox_q_gain", "fox_k_gain", "gla_b_alpha", "gla_o_gain",
               "gdn_a_log", "gdn_dt_bias", "gdn_o_gain")
_WEIGHTS = ("meta_tokens", "norm_mix", "norm_ffn", "w_gate_up", "w_down", "fox_w_in", "fox_b_f", "fox_q_gain",
            "fox_k_gain", "fox_w_out", "gla_w_in", "gla_w_alpha2", "gla_b_alpha", "gla_o_gain", "gla_w_out",
            "gdn_w_in", "gdn_conv_w", "gdn_a_log", "gdn_dt_bias", "gdn_o_gain", "gdn_w_out")
_PACK_ROWS = 512
_IN_W = ("fox_w_in", "gla_w_in", "gdn_w_in")
_OUT_W = ("fox_w_out", "gla_w_out", "gdn_w_out")


def _piece_rows(n):
    return -(-n // 32) * 32


def _pack(arrays, width, row_mult, dtype):
    flat = jnp.concatenate([a.astype(dtype).reshape(-1) for a in arrays])
    per = width * row_mult
    n = -(-flat.shape[0] // per) * per
    return jnp.pad(flat, (0, n - flat.shape[0])).reshape(n // width, width)


def _unpack(flat, shapes):
    out, off = [], 0
    for s in shapes:
        n = 1
        for d in s:
            n *= d
        out.append(flat[off:off + n].reshape(s))
        off += n
    return out


def _unpack_cols(flat2, shapes):
    out, off = [], 0
    for s in shapes:
        n = 1
        for d in s:
            n *= d
        out.append(flat2[:, off:off + n].reshape((flat2.shape[0],) + tuple(s)))
        off += n
    return out


def kernel(x, meta_tokens, norm_mix, norm_ffn, w_gate_up, w_down, fox_w_in, fox_b_f, fox_q_gain, fox_k_gain, fox_w_out, gla_w_in, gla_w_alpha2, gla_b_alpha, gla_o_gain, gla_w_out, gdn_w_in, gdn_conv_w, gdn_a_log, gdn_dt_bias, gdn_o_gain, gdn_w_out, loss_target, m_meta_tokens, m_norm_mix, m_norm_ffn, m_w_gate_up, m_w_down, m_fox_w_in, m_fox_b_f, m_fox_q_gain, m_fox_k_gain, m_fox_w_out, m_gla_w_in, m_gla_w_alpha2, m_gla_b_alpha, m_gla_o_gain, m_gla_w_out, m_gdn_w_in, m_gdn_conv_w, m_gdn_a_log, m_gdn_dt_bias, m_gdn_o_gain, m_gdn_w_out, v_meta_tokens, v_norm_mix, v_norm_ffn, v_w_gate_up, v_w_down, v_fox_w_in, v_fox_b_f, v_fox_q_gain, v_fox_k_gain, v_fox_w_out, v_gla_w_in, v_gla_w_alpha2, v_gla_b_alpha, v_gla_o_gain, v_gla_w_out, v_gdn_w_in, v_gdn_conv_w, v_gdn_a_log, v_gdn_dt_bias, v_gdn_o_gain, v_gdn_w_out):
    W = dict(meta_tokens=meta_tokens, norm_mix=norm_mix, norm_ffn=norm_ffn, w_gate_up=w_gate_up, w_down=w_down,
             fox_w_in=fox_w_in, fox_b_f=fox_b_f, fox_q_gain=fox_q_gain, fox_k_gain=fox_k_gain, fox_w_out=fox_w_out,
             gla_w_in=gla_w_in, gla_w_alpha2=gla_w_alpha2, gla_b_alpha=gla_b_alpha, gla_o_gain=gla_o_gain,
             gla_w_out=gla_w_out, gdn_w_in=gdn_w_in, gdn_conv_w=gdn_conv_w, gdn_a_log=gdn_a_log,
             gdn_dt_bias=gdn_dt_bias, gdn_o_gain=gdn_o_gain, gdn_w_out=gdn_w_out)
    M = dict(meta_tokens=m_meta_tokens, norm_mix=m_norm_mix, norm_ffn=m_norm_ffn, w_gate_up=m_w_gate_up, w_down=m_w_down,
             fox_w_in=m_fox_w_in, fox_b_f=m_fox_b_f, fox_q_gain=m_fox_q_gain, fox_k_gain=m_fox_k_gain,
             fox_w_out=m_fox_w_out, gla_w_in=m_gla_w_in, gla_w_alpha2=m_gla_w_alpha2, gla_b_alpha=m_gla_b_alpha,
             gla_o_gain=m_gla_o_gain, gla_w_out=m_gla_w_out, gdn_w_in=m_gdn_w_in, gdn_conv_w=m_gdn_conv_w,
             gdn_a_log=m_gdn_a_log, gdn_dt_bias=m_gdn_dt_bias, gdn_o_gain=m_gdn_o_gain, gdn_w_out=m_gdn_w_out)
    V = dict(meta_tokens=v_meta_tokens, norm_mix=v_norm_mix, norm_ffn=v_norm_ffn, w_gate_up=v_w_gate_up, w_down=v_w_down,
             fox_w_in=v_fox_w_in, fox_b_f=v_fox_b_f, fox_q_gain=v_fox_q_gain, fox_k_gain=v_fox_k_gain,
             fox_w_out=v_fox_w_out, gla_w_in=v_gla_w_in, gla_w_alpha2=v_gla_w_alpha2, gla_b_alpha=v_gla_b_alpha,
             gla_o_gain=v_gla_o_gain, gla_w_out=v_gla_w_out, gdn_w_in=v_gdn_w_in, gdn_conv_w=v_gdn_conv_w,
             gdn_a_log=v_gdn_a_log, gdn_dt_bias=v_gdn_dt_bias, gdn_o_gain=v_gdn_o_gain, gdn_w_out=v_gdn_w_out)
    chip = 2 * lax.axis_index("x") + lax.axis_index("y")

    pieces, offs, r = [], {}, FFN_ROWS
    for n in _IN_W:
        nc = W[n].shape[2]
        for l in range(W[n].shape[0]):
            pieces.append(jnp.pad(W[n][l].T.astype(BF16), ((0, _piece_rows(nc) - nc), (0, 0))))
            offs[n, l] = r
            r += _piece_rows(nc)
    for n in _OUT_W:
        for l in range(W[n].shape[0]):
            pieces.append(W[n][l].astype(BF16))
            offs[n, l] = r
            r += W[n].shape[1]
    rows = -(-r // _PACK_ROWS) * _PACK_ROWS
    packed = jnp.concatenate([jnp.swapaxes(w_gate_up, 1, 2).reshape(-1, D).astype(BF16), w_down.reshape(-1, D).astype(BF16)]
                             + pieces + [jnp.zeros((rows - r, D), BF16)], axis=0)
    first_rows = [(offs["fox_w_in", 0], offs["fox_w_in", 1] - offs["fox_w_in", 0]),
                  (offs["fox_w_out", 0], offs["fox_w_out", 1] - offs["fox_w_out", 0])]
    later_rows = [(0, FFN_ROWS), (offs["fox_w_in", 1], offs["fox_w_out", 0] - offs["fox_w_in", 1]),
                  (offs["fox_w_out", 1], r - offs["fox_w_out", 1])]
    wpk = _ag_weights(lax.dynamic_update_slice(lax.empty((4, rows, D), BF16), packed[None], (chip, 0, 0)), first_rows)

    def in_t(buf, n, l, npad):
        nc = W[n].shape[2]
        return jnp.concatenate([buf[q, offs[n, l]:offs[n, l] + nc] for q in range(4)] + [jnp.zeros((npad - 4 * nc, D), BF16)], 0)

    def out_w(buf, n, l):
        return jnp.concatenate([buf[q, offs[n, l]:offs[n, l] + W[n].shape[1]] for q in range(4)], axis=0)

    fox_in0, fox_out0 = in_t(wpk, "fox_w_in", 0, FOX_INP), out_w(wpk, "fox_w_out", 0)
    full = {}
    small = _pack([W[n] for n, _ in _SMALL_SHARDED], LANES, 8, F32)
    small_all = _gather8(small, reduce=False, name="gather_small").reshape(8, -1)
    for (n, ax), seg in zip(_SMALL_SHARDED, _unpack_cols(small_all, [W[n].shape for n, _ in _SMALL_SHARDED])):
        full[n] = jnp.concatenate([seg[2 * q] for q in range(4)], axis=ax)
    fox_in, full["fox_w_out"] = [fox_in0], [fox_out0]
    w_alpha2, conv_w = full["gla_w_alpha2"][0], full["gdn_conv_w"][0]

    h = jnp.concatenate([jnp.zeros((META0, D), F32), full["meta_tokens"], x[0]], axis=0)
    saved = []
    y = _rms_fwd(h, norm_mix[0], name="norm_mix0")
    for i in range(DEPTH):
        kind, j = i % 3, i // 3
        if kind == 0:
            proj = _mm(y, fox_in[j], tb=True, name=f"fox_in{j}")
            qa, ka, va = _fox_prep(proj, fox_b_f[j], fox_q_gain[j], fox_k_gain[j])
            if i == 0:
                o, og, lse, wpk = _fox_attn_fwd(qa, ka, va, proj, ag=(wpk, later_rows))
                fox_in += [in_t(wpk, "fox_w_in", l, FOX_INP) for l in range(1, fox_w_in.shape[0])]
                full["fox_w_out"] += [out_w(wpk, "fox_w_out", l) for l in range(1, fox_w_out.shape[0])]
                gla_in = [in_t(wpk, "gla_w_in", l, GLA_INP) for l in range(gla_w_in.shape[0])]
                gdn_in = [in_t(wpk, "gdn_w_in", l, GDN_INP) for l in range(gdn_w_in.shape[0])]
                for n in ("gla_w_out", "gdn_w_out"):
                    full[n] = [out_w(wpk, n, l) for l in range(W[n].shape[0])]
            else:
                o, og, lse = _fox_attn_fwd(qa, ka, va, proj)
            w_out, mix = full["fox_w_out"][j], (proj, qa, ka, va, o, lse)
        elif kind == 1:
            proj = _mm(y, gla_in[j], tb=True, name=f"gla_in{j}")
            o, og, states = _gla_fwd(proj, w_alpha2, gla_b_alpha[j], gla_o_gain[j])
            w_out, mix = full["gla_w_out"][j], (proj, o, states)
        else:
            proj = _mm(y, gdn_in[j], tb=True, name=f"gdn_in{j}")
            o, og, states, tinvs = _gdn_fwd(proj, conv_w, gdn_a_log[j], gdn_dt_bias[j], gdn_o_gain[j])
            w_out, mix = full["gdn_w_out"][j], (proj, o, states, tinvs)
        hm, yf = _mm(og, w_out, add=h, norm=norm_ffn[i], name=f"mix_out{i}")
        gate, up, act = _ffn_up(yf, wpk, i)
        hn, y_next = _ffn_down(act, wpk, i, hm, norm_mix[(i + 1) % DEPTH])
        saved.append((h, y, mix, og, w_out, hm, yf, gate, up, act))
        h, y = hn, y_next
    dh, loss_tile = _loss_head(h, loss_target[0])

    G = {n: [None] * W[n].shape[0] for n in _WEIGHTS if n not in ("meta_tokens", "w_gate_up", "w_down") + _IN_W}
    GT = {}

    def grad_layout(ffn_layers, pieces):
        off, end = {}, 0
        for l in ffn_layers:
            off["gu", l] = end
            end += GU_ROWS
        for l in ffn_layers:
            off["down", l] = end
            end += DOWN_ROWS
        for n, l in pieces:
            off[n, l] = end
            end += _piece_rows(W[n].shape[2]) if n in _IN_W else W[n].shape[1]
        return off, end, -(-end // _PACK_ROWS) * _PACK_ROWS

    first_pieces = [("fox_w_in", 0)]
    later_pieces = [(n, l) for n in _IN_W + _OUT_W for l in range(W[n].shape[0]) if (n, l) not in first_pieces]
    layouts = [grad_layout([], first_pieces), grad_layout(list(range(DEPTH)), later_pieces)]
    gbuf = [jnp.zeros((4, lay[2], D), BF16) for lay in layouts]

    def with_pieces(buf, lay, pieces):
        off, end, total = lay
        blocks = []
        for q in range(4):
            parts = []
            for n, l in pieces:
                if n in _IN_W:
                    nc = W[n].shape[2]
                    parts.append(jnp.pad(GT[n, l][q * nc:(q + 1) * nc], ((0, _piece_rows(nc) - nc), (0, 0))))
                else:
                    nr = W[n].shape[1]
                    parts.append(G[n][l][q * nr:(q + 1) * nr])
            blocks.append(jnp.concatenate(parts + [jnp.zeros((total - end, D), BF16)], axis=0))
        return lax.dynamic_update_slice(buf, jnp.stack(blocks), (0, off[pieces[0]], 0))

    s_later = None
    for i in reversed(range(DEPTH)):
        kind, j = i % 3, i // 3
        h_in, y, mix, og, w_out, hm, yf, gate, up, act = saved[i]
        b = 1
        dg, du = _ffn_dact(dh, wpk, i, gate, up)
        gbuf[b] = _ffn_dw_down(act, dh, gbuf[b], i, layouts[b][0]["down", i])
        dhm, dnf = _ffn_dyf(dg, du, wpk, i, hm, norm_ffn[i], dh)
        gbuf[b] = _ffn_dw_gu(dg, du, yf, gbuf[b], i, layouts[b][0]["gu", i] // GU_ROWS)
        G["norm_ffn"][i] = dnf[0]
        dog = _mm(dhm, w_out, tb=True, name=f"d_og{i}")
        dw_out = _mm(og, dhm, ta=True, out_dtype=BF16, name=f"d_w_out{i}")
        if kind == 0:
            proj, qa, ka, va, o, lse = mix
            G["fox_w_out"][j] = dw_out
            if i == 0:
                g_later = with_pieces(gbuf[1], layouts[1], later_pieces)
                doa, q2, dgate, got = _fox_gate_bwd(dog, o, proj, lse, qa, swap=g_later)
                s_later = _add_halves(g_later, got, "_later")
                dqn, dkn, dv, dct, recv_later = _fox_attn_bwd(q2, ka, va, doa, rs=s_later)
            else:
                doa, q2, dgate = _fox_gate_bwd(dog, o, proj, lse, qa)
                dqn, dkn, dv, dct = _fox_attn_bwd(q2, ka, va, doa)
            dproj, dqg, dkg, dbf = _fox_prep_bwd(proj, fox_b_f[j], fox_q_gain[j], fox_k_gain[j], dqn, dkn, dv, dgate, dct)
            G["fox_q_gain"][j] = dqg.reshape(FOX_H, FOX_DH).sum(0)
            G["fox_k_gain"][j] = dkg.reshape(FOX_H, FOX_DH).sum(0)
            G["fox_b_f"][j] = dbf[0, :FOX_H]
            w_in, wname = fox_in[j], "fox_w_in"
        elif kind == 1:
            proj, o, states = mix
            dproj, dwa, dba, dogain = _gla_bwd(proj, w_alpha2, gla_b_alpha[j], gla_o_gain[j], o, states, dog)
            G["gla_w_out"][j] = dw_out
            G["gla_w_alpha2"][j] = dwa[:GLA_RANK]
            G["gla_b_alpha"][j] = dba[0]
            G["gla_o_gain"][j] = dogain[0]
            w_in, wname = gla_in[j], "gla_w_in"
        else:
            proj, o, states, tinvs = mix
            dproj, dcw, dal, ddt, dogain = _gdn_bwd(proj, conv_w, gdn_a_log[j], gdn_dt_bias[j], gdn_o_gain[j], o, states,
                                                    tinvs, dog)
            G["gdn_w_out"][j] = dw_out
            G["gdn_conv_w"][j] = dcw[:4].reshape(4, 1, GDN_CONV)
            G["gdn_a_log"][j] = dal[0, :GDN_H]
            G["gdn_dt_bias"][j] = ddt[0, :GDN_H]
            G["gdn_o_gain"][j] = dogain[0]
            w_in, wname = gdn_in[j], "gdn_w_in"
        dh, dnm = _mm(dproj, w_in, rms_bwd=(h_in, norm_mix[i], dhm), name=f"d_y{i}")
        GT[wname, j] = _mm(dproj, y, ta=True, out_dtype=BF16, name=f"d_w_in{i}")
        G["norm_mix"][i] = dnm[0]
    grad_x = dh[ROW0:][None]
    G = {n: (v if n in _OUT_W else jnp.stack(v)) for n, v in G.items()}
    G["meta_tokens"] = dh[META0:ROW0]

    s_first = _rs_local(with_pieces(gbuf[0], layouts[0], first_pieces), "_first")
    reduced = [_rs_finish(s_first, _scatter_chips(s_first, "_first"), "_first"), _rs_finish(s_later, recv_later, "_later")]

    def reduced_piece(n, l):
        b = 0 if (n, l) in first_pieces else 1
        start = layouts[b][0][n, l]
        return reduced[b][start:start + (W[n].shape[2] if n in _IN_W else W[n].shape[1])]

    grads = {}
    for n in _IN_W:
        grads[n] = jnp.stack([reduced_piece(n, l).T for l in range(W[n].shape[0])])
    for n in _OUT_W:
        grads[n] = jnp.stack([reduced_piece(n, l) for l in range(W[n].shape[0])])
    small_names = [n for n, _ in _SMALL_SHARDED] + list(_REPLICATED)
    small_g = _pack([G[n] for n in small_names] + [loss_tile[0, 0:1]], LANES, 8, F32)
    small_sum = _gather8(small_g, reduce=True, name="allreduce_small").reshape(-1)
    small_shapes = [G[n].shape for n in small_names] + [(1,)]
    small_vals = _unpack(small_sum, small_shapes)
    loss = small_vals[-1][0]
    for n, val in zip(small_names, small_vals[:-1]):
        grads[n] = val
    for n, ax in _SMALL_SHARDED:
        sz = W[n].shape[ax]
        grads[n] = lax.dynamic_slice_in_dim(grads[n], chip * sz, sz, axis=ax)

    delta, new_m, new_v = {}, {}, {}
    for n, key, tr_ in (("w_gate_up", "gu", True), ("w_down", "down", False)):
        grads[n], delta[n], new_m[n], new_v[n] = _adamw_packed(
            W[n], reduced[1], reduced[1], M[n], V[n], row0=layouts[1][0][key, 0], row_off=layouts[1][0][key, 1],
            transposed=tr_, name=f"adamw_{n}")
    for n in _IN_W + _OUT_W:
        delta[n], new_m[n], new_v[n] = _adamw(W[n], grads[n], M[n], V[n], name=f"adamw_{n}")
    tiny = [n for n in _WEIGHTS if n not in ("w_gate_up", "w_down") + _IN_W + _OUT_W]
    packs = [_pack([T[n] for n in tiny], LANES, 8, F32) for T in (W, grads, M, V)]
    outs = _adamw(*packs, name="adamw_small")
    shapes = [W[n].shape for n in tiny]
    for dst, o in zip((delta, new_m, new_v), outs):
        for n, val in zip(tiny, _unpack(o.reshape(-1), shapes)):
            dst[n] = val
    return (loss, grad_x, *[grads[n] for n in _WEIGHTS], *[delta[n] for n in _WEIGHTS],
            *[new_m[n] for n in _WEIGHTS], *[new_v[n] for n in _WEIGHTS])
```

```python
import jax
import jax.numpy as jnp
from jax import lax
from jax.experimental import pallas as pl
from jax.experimental.pallas import tpu as pltpu

F32, BF16 = jnp.float32, jnp.bfloat16
D = 1024
N_META = 16
ROW0 = 128
META0 = ROW0 - N_META
EPS = 1e-6
LANES = 128
VMEM_LIMIT = 56 * 1024 * 1024

FOX_H, FOX_DH = 16, 64
FOX_INP = 4224
GLA_H, GLA_DK, GLA_DV, GLA_RANK = 4, 128, 256, 16
GLA_QK, GLA_V = 512, 1024
GLA_INP = 3200
GLA_NORM = 16.0
GDN_H, GDN_DK, GDN_DV = 8, 128, 128
GDN_CONV = 3072
GDN_INP = 4224
CHUNK = 64
D_FF = 2816
DEPTH = 4

ADAM_LR, ADAM_B1, ADAM_B2, ADAM_EPS, ADAM_WD, ADAM_STEP = 0.001, 0.9, 0.999, 1e-08, 0.01, 10

MESH = pl.DeviceIdType.MESH
ANY = pl.BlockSpec(memory_space=pl.ANY)
VM = pl.BlockSpec(memory_space=pltpu.VMEM)


def _params(sem=None, **kw):
    if sem is not None:
        kw["dimension_semantics"] = sem
    return pltpu.CompilerParams(vmem_limit_bytes=VMEM_LIMIT, **kw)


def _tile(n, cap, mult=LANES):
    best = None
    for t in range(mult, min(n, cap) + 1, mult):
        if n % t == 0:
            best = t
    return best if best is not None else n


def nn(a, b, **kw):
    return jnp.dot(a, b, preferred_element_type=F32, **kw)


def nt(a, b, **kw):
    return lax.dot_general(a, b, (((1,), (1,)), ((), ())), preferred_element_type=F32, **kw)


def tn(a, b, **kw):
    return lax.dot_general(a, b, (((0,), (0,)), ((), ())), preferred_element_type=F32, **kw)


def _split3(x):
    hi = x.astype(BF16)
    r = x - hi.astype(F32)
    mid = r.astype(BF16)
    lo = (r - mid.astype(F32)).astype(BF16)
    return hi, mid, lo


def _sel_l(sel, x):
    a, b, c = _split3(x)
    return nn(sel, a) + nn(sel, b) + nn(sel, c)


def _sel_r(x, sel):
    a, b, c = _split3(x)
    return nn(a, sel) + nn(b, sel) + nn(c, sel)


def _sel_r2(x, sel):
    a = x.astype(BF16)
    return nn(a, sel) + nn((x - a.astype(F32)).astype(BF16), sel)


def _iota(shape, dim):
    return lax.broadcasted_iota(jnp.int32, shape, dim)


def _tri(n, upper=False, strict=False):
    i, j = _iota((n, n), 0), _iota((n, n), 1)
    if upper:
        m = (j > i) if strict else (j >= i)
    else:
        m = (j < i) if strict else (j <= i)
    return m


def _sigmoid(x):
    return 1.0 / (1.0 + jnp.exp(-x))


def _log_sigmoid(x):
    return jnp.minimum(x, 0.0) - jnp.log(1.0 + jnp.exp(-jnp.abs(x)))


def _softplus(x):
    return jnp.maximum(x, 0.0) + jnp.log(1.0 + jnp.exp(-jnp.abs(x)))


def _silu(x):
    return x * _sigmoid(x)


def _dsilu(x):
    s = _sigmoid(x)
    return s * (1.0 + x * (1.0 - s))


def _rms(x, g):
    return (x * lax.rsqrt(jnp.mean(x * x, axis=-1, keepdims=True) + EPS) * g).astype(BF16)


def _rms_grad(x, g, dy):
    r = lax.rsqrt(jnp.mean(x * x, axis=-1, keepdims=True) + EPS)
    u = dy * g
    return r * u - x * (r * r * r) * jnp.mean(x * u, axis=-1, keepdims=True), jnp.sum(dy * x * r, axis=0, keepdims=True)


def _mm(a, b, *, ta=False, tb=False, add=None, norm=None, rms_bwd=None, rs=None, out_dtype=F32, name):
    m, k = (a.shape[1], a.shape[0]) if ta else a.shape
    n = b.shape[0] if tb else b.shape[1]
    assert k == (b.shape[1] if tb else b.shape[0])
    rows_whole = norm is not None or rms_bwd is not None
    tm, tn_, tk = _tile(m, 704 if rows_whole else 1408, LANES if ta else 16), _tile(n, 1408), _tile(k, 1408)
    nk = k // tk
    assert not rows_whole or tn_ == n
    assert rs is None or rms_bwd is not None

    def body(*refs):
        refs = list(refs)
        if rs is not None:
            send_sems, recv_sems = refs[-2:]
            refs = refs[:-2]
            s_ref, got_ref = refs.pop(5), refs.pop(-2)
            sends, arrivals = _scatter_copies(s_ref, got_ref, send_sems, recv_sems)

            @pl.when((pl.program_id(0) == 0) & (pl.program_id(2) == 0))
            def _():
                for cp in sends:
                    cp.start()

            @pl.when((pl.program_id(0) == m // tm - 1) & (pl.program_id(2) == nk - 1))
            def _():
                for cp in arrivals:
                    cp.wait_recv()
                for cp in sends:
                    cp.wait_send()

        a_ref, b_ref = refs[:2]
        extra = refs[2:-1]
        acc = refs[-1]
        i, kk = pl.program_id(0), pl.program_id(2)

        @pl.when(kk == 0)
        def _():
            acc[...] = jnp.zeros_like(acc)

        av, bv = a_ref[...].astype(BF16), b_ref[...].astype(BF16)
        dims = (((0,) if ta else (1,), (1,) if tb else (0,)), ((), ()))
        acc[...] += lax.dot_general(av, bv, dims, preferred_element_type=F32)

        @pl.when(kk == nk - 1)
        def _():
            r = acc[...]
            if rms_bwd is not None:
                h_ref, g_ref, dres_ref, o_ref, dg_ref = extra
                dx, dgain = _rms_grad(h_ref[...], g_ref[...], r)
                o_ref[...] = dres_ref[...] + dx

                @pl.when(i == 0)
                def _():
                    dg_ref[...] = jnp.zeros_like(dg_ref)

                dg_ref[...] += dgain
                return
            if add is not None:
                r = r + extra[0][...].astype(F32)
            if norm is not None:
                g_ref, o_ref, y_ref = extra[-3:]
                y_ref[...] = _rms(r, g_ref[...])
            else:
                o_ref = extra[-1]
            o_ref[...] = r.astype(out_dtype)

    a_spec = pl.BlockSpec((tk, tm), lambda i, j, q: (q, i)) if ta else pl.BlockSpec((tm, tk), lambda i, j, q: (i, q))
    b_spec = pl.BlockSpec((tn_, tk), lambda i, j, q: (j, q)) if tb else pl.BlockSpec((tk, tn_), lambda i, j, q: (q, j))
    o_spec = pl.BlockSpec((tm, tn_), lambda i, j, q: (i, j))
    g_spec = pl.BlockSpec((1, n), lambda i, j, q: (0, 0))
    ins, specs = [a, b], [a_spec, b_spec]
    out_specs, out_shape = o_spec, jax.ShapeDtypeStruct((m, n), out_dtype)
    sem = ("parallel", "parallel", "arbitrary")
    scratch = [pltpu.VMEM((tm, tn_), F32)]
    if rms_bwd is not None:
        ins += [rms_bwd[0], rms_bwd[1].reshape(1, n), rms_bwd[2]]
        specs += [o_spec, g_spec, o_spec]
        out_specs, out_shape = [o_spec, g_spec], [jax.ShapeDtypeStruct((m, n), F32), jax.ShapeDtypeStruct((1, n), F32)]
        sem = ("arbitrary", "arbitrary", "arbitrary")
        if rs is not None:
            ins.append(rs)
            specs.append(ANY)
            out_specs.append(ANY)
            out_shape.append(jax.ShapeDtypeStruct((3,) + rs.shape[1:], rs.dtype))
            scratch += [pltpu.SemaphoreType.DMA((3,)), pltpu.SemaphoreType.DMA((3,))]
    else:
        if add is not None:
            ins.append(add)
            specs.append(o_spec)
        if norm is not None:
            ins.append(norm.reshape(1, n))
            specs.append(g_spec)
            out_specs, out_shape = [o_spec, o_spec], [out_shape, jax.ShapeDtypeStruct((m, n), BF16)]
    return pl.pallas_call(
        body, name=name, grid=(m // tm, n // tn_, nk), in_specs=specs, out_specs=out_specs, out_shape=out_shape,
        scratch_shapes=scratch, compiler_params=_params(sem),
    )(*ins)


def _rms_fwd(h, g, *, name):
    lp = h.shape[0]
    tr = _tile(lp, 512)

    def body(h_ref, g_ref, y_ref):
        x = h_ref[...]
        r = lax.rsqrt(jnp.mean(x * x, axis=-1, keepdims=True) + EPS)
        y_ref[...] = (x * r * g_ref[...]).astype(BF16)

    return pl.pallas_call(
        body, name=name, grid=(lp // tr,),
        in_specs=[pl.BlockSpec((tr, D), lambda i: (i, 0)), pl.BlockSpec((1, D), lambda i: (0, 0))],
        out_specs=pl.BlockSpec((tr, D), lambda i: (i, 0)),
        out_shape=jax.ShapeDtypeStruct((lp, D), BF16), compiler_params=_params(("parallel",)),
    )(h, g.reshape(1, D))


GU_ROWS, DOWN_ROWS = 1408, 704
OFF_GU, OFF_DOWN = 0, DEPTH * GU_ROWS
FFN_ROWS = DEPTH * (GU_ROWS + DOWN_ROWS)
FFN_TM = 704


def _gu_spec(fn):
    return pl.BlockSpec((None, GU_ROWS, D), fn)


def _down_spec(fn):
    return pl.BlockSpec((None, DOWN_ROWS, D), fn)


def _down_pair(w0_ref, w1_ref):
    return jnp.concatenate([w0_ref[...], w1_ref[...]], axis=0)


def _ffn_up(yf, wpk, layer):
    lp = yf.shape[0]
    tm = _tile(lp, FFN_TM, 16)

    def body(y_ref, wg_ref, wu_ref, g_ref, u_ref, a_ref):
        y = y_ref[...]
        g, u = nt(y, wg_ref[...]), nt(y, wu_ref[...])
        g_ref[...] = g.astype(BF16)
        u_ref[...] = u.astype(BF16)
        a_ref[...] = (_silu(g) * u).astype(BF16)

    o = pl.BlockSpec((tm, GU_ROWS), lambda i, j: (i, j))
    return pl.pallas_call(
        body, name=f"ffn_up{layer}", grid=(lp // tm, 2),
        in_specs=[pl.BlockSpec((tm, D), lambda i, j: (i, 0)), _gu_spec(lambda i, j: (j, OFF_GU // GU_ROWS + layer, 0)),
                  _gu_spec(lambda i, j: (2 + j, OFF_GU // GU_ROWS + layer, 0))],
        out_specs=[o, o, o], out_shape=[jax.ShapeDtypeStruct((lp, D_FF), BF16)] * 3,
        compiler_params=_params(("parallel", "parallel")),
    )(yf, wpk, wpk)


def _ffn_down(act, wpk, layer, res, norm):
    lp = act.shape[0]
    tm = _tile(lp, FFN_TM, 16)

    def body(a_ref, w0_ref, w1_ref, r_ref, g_ref, o_ref, y_ref, acc):
        kk = pl.program_id(1)

        @pl.when(kk == 0)
        def _():
            acc[...] = r_ref[...]

        acc[...] += nn(a_ref[...], _down_pair(w0_ref, w1_ref))

        @pl.when(kk == 1)
        def _():
            o_ref[...] = acc[...]
            y_ref[...] = _rms(acc[...], g_ref[...])

    o = pl.BlockSpec((tm, D), lambda i, kk: (i, 0))
    blk = OFF_DOWN // DOWN_ROWS + layer
    return pl.pallas_call(
        body, name=f"ffn_down{layer}", grid=(lp // tm, 2),
        in_specs=[pl.BlockSpec((tm, GU_ROWS), lambda i, kk: (i, kk)), _down_spec(lambda i, kk: (2 * kk, blk, 0)),
                  _down_spec(lambda i, kk: (2 * kk + 1, blk, 0)), o, pl.BlockSpec((1, D), lambda i, kk: (0, 0))],
        out_specs=[o, o], out_shape=[jax.ShapeDtypeStruct((lp, D), F32), jax.ShapeDtypeStruct((lp, D), BF16)],
        scratch_shapes=[pltpu.VMEM((tm, D), F32)], compiler_params=_params(("parallel", "arbitrary")),
    )(act, wpk, wpk, res, norm.reshape(1, D))


def _ffn_dact(dh, wpk, layer, gate, up):
    lp = dh.shape[0]
    tm = _tile(lp, FFN_TM, 16)

    def body(d_ref, w0_ref, w1_ref, g_ref, u_ref, dg_ref, du_ref):
        da = nt(d_ref[...].astype(BF16), _down_pair(w0_ref, w1_ref))
        g, u = g_ref[...].astype(F32), u_ref[...].astype(F32)
        sg = _sigmoid(g)
        dg_ref[...] = (da * u * (sg * (1.0 + g * (1.0 - sg)))).astype(BF16)
        du_ref[...] = (da * (g * sg)).astype(BF16)

    o = pl.BlockSpec((tm, GU_ROWS), lambda i, j: (i, j))
    blk = OFF_DOWN // DOWN_ROWS + layer
    return pl.pallas_call(
        body, name=f"d_act{layer}", grid=(lp // tm, 2),
        in_specs=[pl.BlockSpec((tm, D), lambda i, j: (i, 0)), _down_spec(lambda i, j: (2 * j, blk, 0)),
                  _down_spec(lambda i, j: (2 * j + 1, blk, 0)), o, o],
        out_specs=[o, o], out_shape=[jax.ShapeDtypeStruct((lp, D_FF), BF16)] * 2,
        compiler_params=_params(("parallel", "parallel")),
    )(dh, wpk, wpk, gate, up)


def _ffn_dyf(dg, du, wpk, layer, hm, norm, dres):
    lp = dg.shape[0]
    tm = _tile(lp, FFN_TM, 16)

    def body(dg_ref, du_ref, w_ref, h_ref, g_ref, dres_ref, o_ref, dgain_ref, acc):
        i, kk = pl.program_id(0), pl.program_id(1)

        @pl.when(kk == 0)
        def _():
            acc[...] = jnp.zeros_like(acc)

        @pl.when(kk < 2)
        def _():
            acc[...] += nn(dg_ref[...], w_ref[...])

        @pl.when(kk >= 2)
        def _():
            acc[...] += nn(du_ref[...], w_ref[...])

        @pl.when(kk == 3)
        def _():
            dx, dgain = _rms_grad(h_ref[...], g_ref[...], acc[...])
            o_ref[...] = dres_ref[...] + dx

            @pl.when(i == 0)
            def _():
                dgain_ref[...] = jnp.zeros_like(dgain_ref)

            dgain_ref[...] += dgain

    o = pl.BlockSpec((tm, D), lambda i, kk: (i, 0))
    row = pl.BlockSpec((1, D), lambda i, kk: (0, 0))
    return pl.pallas_call(
        body, name=f"d_yf{layer}", grid=(lp // tm, 4),
        in_specs=[pl.BlockSpec((tm, GU_ROWS), lambda i, kk: (i, jnp.minimum(kk, 1))),
                  pl.BlockSpec((tm, GU_ROWS), lambda i, kk: (i, jnp.maximum(kk - 2, 0))),
                  _gu_spec(lambda i, kk: (kk, OFF_GU // GU_ROWS + layer, 0)), o, row, o],
        out_specs=[o, row], out_shape=[jax.ShapeDtypeStruct((lp, D), F32), jax.ShapeDtypeStruct((1, D), F32)],
        scratch_shapes=[pltpu.VMEM((tm, D), F32)], compiler_params=_params(("arbitrary", "arbitrary")),
    )(dg, du, wpk, hm, norm.reshape(1, D), dres)


def _ffn_dw_down(act, dh, gpk, layer, row):
    lp = act.shape[0]
    tk = _tile(lp, 1408, 16)
    nk = lp // tk

    def body(a_ref, d_ref, g_in, g_out, acc, stage, sems):
        jp, kk = pl.program_id(0), pl.program_id(1)

        @pl.when(kk == 0)
        def _():
            acc[...] = jnp.zeros_like(acc)

        acc[...] += tn(a_ref[...], d_ref[...].astype(BF16))

        @pl.when(kk == nk - 1)
        def _():
            stage[...] = acc[...].astype(BF16)
            copies = [pltpu.make_async_copy(stage.at[pl.ds(hf * DOWN_ROWS, DOWN_ROWS), :],
                                            g_out.at[2 * jp + hf, pl.ds(row, DOWN_ROWS), :], sems.at[hf]) for hf in range(2)]
            for cp in copies:
                cp.start()
            for cp in copies:
                cp.wait()

    return pl.pallas_call(
        body, name=f"d_w_down{layer}", grid=(2, nk),
        in_specs=[pl.BlockSpec((tk, GU_ROWS), lambda jp, kk: (kk, jp)), pl.BlockSpec((tk, D), lambda jp, kk: (kk, 0)), ANY],
        out_specs=ANY, out_shape=jax.ShapeDtypeStruct(gpk.shape, gpk.dtype),
        scratch_shapes=[pltpu.VMEM((GU_ROWS, D), F32), pltpu.VMEM((GU_ROWS, D), BF16), pltpu.SemaphoreType.DMA((2,))],
        input_output_aliases={2: 0}, compiler_params=_params(("arbitrary", "arbitrary")),
    )(act, dh, gpk)


def _ffn_dw_gu(dg, du, yf, gpk, layer, blk):
    lp = dg.shape[0]
    tk = _tile(lp, 1408, 16)
    nk = lp // tk

    def body(dg_ref, du_ref, y_ref, g_in, o_ref, acc):
        c, kk = pl.program_id(0), pl.program_id(1)

        @pl.when(kk == 0)
        def _():
            acc[...] = jnp.zeros_like(acc)

        @pl.when(c < 2)
        def _():
            acc[...] += tn(dg_ref[...], y_ref[...])

        @pl.when(c >= 2)
        def _():
            acc[...] += tn(du_ref[...], y_ref[...])

        @pl.when(kk == nk - 1)
        def _():
            o_ref[...] = acc[...].astype(BF16)

    return pl.pallas_call(
        body, name=f"d_w_gate_up{layer}", grid=(4, nk),
        in_specs=[pl.BlockSpec((tk, GU_ROWS), lambda c, kk: (kk, jnp.minimum(c, 1))),
                  pl.BlockSpec((tk, GU_ROWS), lambda c, kk: (kk, jnp.maximum(c - 2, 0))),
                  pl.BlockSpec((tk, D), lambda c, kk: (kk, 0)), ANY],
        out_specs=_gu_spec(lambda c, kk: (c, blk, 0)),
        out_shape=jax.ShapeDtypeStruct(gpk.shape, gpk.dtype),
        scratch_shapes=[pltpu.VMEM((GU_ROWS, D), F32)], input_output_aliases={3: 0},
        compiler_params=_params(("parallel", "arbitrary")),
    )(dg, du, yf, gpk)


def _loss_head(h, target):
    lp = h.shape[0]
    nb = lp // ROW0

    def body(h_ref, t_ref, dh_ref, l_ref):
        i = pl.program_id(0)

        @pl.when(i == 0)
        def _():
            l_ref[...] = jnp.zeros_like(l_ref)
            dh_ref[...] = jnp.zeros_like(dh_ref)

        @pl.when(i > 0)
        def _():
            err = h_ref[...] - t_ref[...]
            dh_ref[...] = err * (1.0 / D)
            l_ref[...] += jnp.sum(err * err) * (0.5 / D)

    return pl.pallas_call(
        body, name="loss_head", grid=(nb,),
        in_specs=[pl.BlockSpec((ROW0, D), lambda i: (i, 0)), pl.BlockSpec((ROW0, D), lambda i: (jnp.maximum(i - 1, 0), 0))],
        out_specs=[pl.BlockSpec((ROW0, D), lambda i: (i, 0)), pl.BlockSpec((8, LANES), lambda i: (0, 0))],
        out_shape=[jax.ShapeDtypeStruct((lp, D), F32), jax.ShapeDtypeStruct((8, LANES), F32)],
        compiler_params=_params(("arbitrary",)),
    )(h, target)


def _adamw(w, g, m, v, *, name):
    if w.ndim == 2:
        w, g, m, v = (t[None] for t in (w, g, m, v))
        return tuple(o[0] for o in _adamw(w, g, m, v, name=name))
    nl, r, c = w.shape
    tr = _tile(r, max(8, (1 << 19) // c), 8)

    def body(w_ref, g_ref, m_ref, v_ref, d_ref, nm_ref, nv_ref):
        d_ref[...], nm_ref[...], nv_ref[...] = _adam_math(w_ref[...], g_ref[...], m_ref[...], v_ref[...])

    spec = pl.BlockSpec((1, tr, c), lambda l, i: (l, i, 0))
    return tuple(pl.pallas_call(
        body, name=name, grid=(nl, r // tr), in_specs=[spec] * 4, out_specs=[spec] * 3,
        out_shape=[jax.ShapeDtypeStruct(w.shape, F32)] * 3, compiler_params=_params(("parallel", "parallel")),
    )(w, g, m, v))


def _adam_math(w, g, m, v):
    nm = ADAM_B1 * m + (1.0 - ADAM_B1) * g
    nv = ADAM_B2 * v + (1.0 - ADAM_B2) * (g * g)
    m_hat = nm / (1.0 - ADAM_B1 ** ADAM_STEP)
    v_hat = nv / (1.0 - ADAM_B2 ** ADAM_STEP)
    return -ADAM_LR * (m_hat / (jnp.sqrt(v_hat) + ADAM_EPS) + ADAM_WD * w), nm, nv


def _adamw_packed(w, gred0, gred, m, v, *, row0, row_off, transposed, name):
    nl, a, b = w.shape
    nr = b if transposed else a
    later = lambda l: row_off // nr + jnp.maximum(l - 1, 0)
    if transposed:
        ta = _tile(a, 256)
        wspec = pl.BlockSpec((1, ta, b), lambda l, r: (l, r, 0))
        g0spec = pl.BlockSpec((b, ta), lambda l, r: (row0 // nr, r))
        gspec = pl.BlockSpec((b, ta), lambda l, r: (later(l), r))
        grid = (nl, a // ta)
    else:
        wspec = pl.BlockSpec((1, a, b), lambda l, r: (l, 0, 0))
        g0spec = pl.BlockSpec((a, b), lambda l, r: (row0 // nr, 0))
        gspec = pl.BlockSpec((a, b), lambda l, r: (later(l), 0))
        grid = (nl, 1)

    def body(w_ref, g0_ref, g_ref, m_ref, v_ref, go_ref, d_ref, nm_ref, nv_ref):
        g = jnp.where(pl.program_id(0) == 0, g0_ref[...], g_ref[...])
        g = g.T if transposed else g
        d, nm, nv = _adam_math(w_ref[0], g, m_ref[0], v_ref[0])
        go_ref[0], d_ref[0], nm_ref[0], nv_ref[0] = g, d, nm, nv

    return pl.pallas_call(
        body, name=name, grid=grid, in_specs=[wspec, g0spec, gspec, wspec, wspec], out_specs=[wspec] * 4,
        out_shape=[jax.ShapeDtypeStruct(w.shape, F32)] * 4, compiler_params=_params(("parallel", "parallel")),
    )(w, gred0, gred, m, v)


FOX_AUG = FOX_H * LANES
L_C = 64
L_K = 67
L_LSE = 70
PAD_KEY = -30000.0
FOX_TQ = 384


def _head_sel(n_heads, width, lanes=LANES):
    r, c = _iota((n_heads * width, lanes), 0), _iota((n_heads * width, lanes), 1)
    down = (r // width == c).astype(BF16)
    r2, c2 = _iota((lanes, n_heads * width), 0), _iota((lanes, n_heads * width), 1)
    up = (c2 // width == r2).astype(BF16)
    return down, up


def _place(lane0):
    r, c = _iota((LANES, FOX_AUG), 0), _iota((LANES, FOX_AUG), 1)
    return [((c // LANES == r) & (c % LANES == lane0 + m)).astype(BF16) for m in range(3)]


def _placed(x, lane0):
    pcs = _split3(x)
    mats = _place(lane0)
    return nn(pcs[0], mats[0]) + nn(pcs[1], mats[1]) + nn(pcs[2], mats[2])


def _ones_at(rows, lanes):
    c = _iota((rows, FOX_AUG), 1) % LANES
    m = c == lanes[0]
    for l in lanes[1:]:
        m = m | (c == l)
    return m.astype(F32)


def _spread(x, extras, out_ref):
    rows = x.shape[0]
    left = _iota((rows, LANES), 1) < FOX_DH
    for p in range(FOX_H // 2):
        slab = x[:, p * LANES:(p + 1) * LANES]
        a = jnp.where(left, slab, extras[:, 2 * p * LANES:(2 * p + 1) * LANES])
        b = jnp.where(left, pltpu.roll(slab, FOX_DH, 1), extras[:, (2 * p + 1) * LANES:(2 * p + 2) * LANES])
        out_ref[:, 2 * p * LANES:(2 * p + 1) * LANES] = a.astype(BF16)
        out_ref[:, (2 * p + 1) * LANES:(2 * p + 2) * LANES] = b.astype(BF16)


def _fox_prep(proj, b_f, q_gain, k_gain):
    lp = proj.shape[0]
    nb = lp // LANES

    def body(p_ref, bf_ref, qg_ref, kg_ref, q_ref, k_ref, v_ref, carry):
        i = pl.program_id(0)

        @pl.when(i == 0)
        def _():
            carry[...] = jnp.zeros_like(carry)

        down, up = _head_sel(FOX_H, FOX_DH)

        def normed(x, gain):
            ms = _sel_r2(x * x, down) * (1.0 / FOX_DH)
            r = _sel_r2(lax.rsqrt(ms + EPS), up)
            return x * r * gain

        lane = _iota((LANES, LANES), 1)
        lf = jnp.where(lane < FOX_H, _log_sigmoid(p_ref[:, 4 * D:4 * D + LANES] + bf_ref[...]), 0.0)
        c = _sel_l(_tri(LANES).astype(BF16), lf) + carry[0:1, :]
        carry[...] = jnp.broadcast_to(c[LANES - 1:LANES, :], carry.shape)
        q_extra = _placed(c, L_C) + _ones_at(LANES, (L_K, L_K + 1, L_K + 2))
        row = i * LANES + _iota((LANES, FOX_AUG), 0)
        lane_a = _iota((LANES, FOX_AUG), 1) % LANES
        k_extra = -_placed(c, L_K) + _ones_at(LANES, (L_C, L_C + 1, L_C + 2, L_LSE, L_LSE + 1, L_LSE + 2))
        pad_val = jnp.where(lane_a == L_K, PAD_KEY, 0.0)
        k_extra = jnp.where((row < META0) & (lane_a >= L_K) & (lane_a < L_K + 3), pad_val, k_extra)
        v_extra = _ones_at(LANES, (L_C, L_C + 1, L_C + 2))
        _spread(normed(p_ref[:, 0:D], qg_ref[...]) * (FOX_DH ** -0.5), q_extra, q_ref)
        _spread(normed(p_ref[:, D:2 * D], kg_ref[...]), k_extra, k_ref)
        _spread(p_ref[:, 2 * D:3 * D], v_extra, v_ref)

    row = pl.BlockSpec((1, D), lambda i: (0, 0))
    aug = pl.BlockSpec((LANES, FOX_AUG), lambda i: (i, 0))
    return pl.pallas_call(
        body, name="fox_prep", grid=(nb,),
        in_specs=[pl.BlockSpec((LANES, FOX_INP), lambda i: (i, 0)), pl.BlockSpec((1, LANES), lambda i: (0, 0)), row, row],
        out_specs=[aug] * 3, out_shape=[jax.ShapeDtypeStruct((lp, FOX_AUG), BF16)] * 3,
        scratch_shapes=[pltpu.VMEM((8, LANES), F32)],
        compiler_params=_params(("arbitrary",)),
    )(proj, jnp.pad(b_f, (0, LANES - FOX_H)).reshape(1, LANES), jnp.tile(q_gain, FOX_H).reshape(1, D),
      jnp.tile(k_gain, FOX_H).reshape(1, D))


def _fox_attn_fwd(qa, ka, va, proj, ag=None):
    lp = qa.shape[0]
    tq = _tile(lp, FOX_TQ)
    nq = lp // tq
    npair = FOX_H // 2

    def body(q_ref, k_ref, v_ref, gate_ref, *rest):
        if ag is None:
            o_ref, og_ref, lse_ref = rest
        else:
            _, o_ref, og_ref, lse_ref, w_out, send_sems, recv_sems, send2, recv2 = rest
            copies, fwd = _AgCopies(w_out, ag[1], send_sems, recv_sems), _AgCopies(w_out, ag[1], send2, recv2)

            @pl.when((pl.program_id(0) == 0) & (pl.program_id(1) == 0))
            def _():
                for r, k in copies.pairs():
                    copies.ici(r, k).start()

            @pl.when((pl.program_id(0) == npair - 1) & (pl.program_id(1) == 0))
            def _():
                for r, k in copies.pairs():
                    copies.ici_arrival(r, k).wait_recv()
                    fwd.forward(r, k).start()

        i = pl.program_id(1)
        causal = _iota((tq, tq), 1) <= _iota((tq, tq), 0)
        qs = [q_ref[:, hh * LANES:(hh + 1) * LANES] for hh in range(2)]

        def block(j, carry, diag):
            off = pl.multiple_of(j * tq, tq)
            out = []
            for hh in range(2):
                m, acc = carry[hh]
                k = k_ref[pl.ds(off, tq), hh * LANES:(hh + 1) * LANES]
                v = v_ref[pl.ds(off, tq), hh * LANES:(hh + 1) * LANES]
                s = nt(qs[hh], k)
                if diag:
                    s = jnp.where(causal, s, -1e30)
                m2 = jnp.maximum(m, jnp.max(s, axis=-1, keepdims=True))
                p = jnp.exp(s - m2)
                p_hi = p.astype(BF16)
                p_lo = (p - p_hi.astype(F32)).astype(BF16)
                out.append((m2, jnp.exp(m - m2) * acc + nn(p_hi, v) + nn(p_lo, v)))
            return tuple(out)

        init = tuple((jnp.full((tq, 1), -1e30, F32), jnp.zeros((tq, LANES), F32)) for _ in range(2))
        carry = lax.fori_loop(0, i // 2, lambda j, c: block(2 * j + 1, block(2 * j, c, False), False), init)
        carry = lax.cond(i % 2 == 1, lambda c: block(i - 1, c, False), lambda c: c, carry)
        carry = block(i, carry, True)
        outs, lses = [], []
        for hh in range(2):
            m, acc = carry[hh]
            l = acc[:, L_C:L_C + 1]
            outs.append(acc / l)
            lses.append(jnp.broadcast_to(m + jnp.log(l), (tq, LANES)))
        left = _iota((tq, LANES), 1) < FOX_DH
        o = jnp.where(left, outs[0], pltpu.roll(outs[1], FOX_DH, 1))
        o_ref[...] = o
        og_ref[...] = (o * _sigmoid(gate_ref[...])).astype(BF16)
        lse_ref[...] = jnp.where(left, lses[0], lses[1])

        if ag is not None:
            @pl.when((pl.program_id(0) == npair - 1) & (pl.program_id(1) == nq - 1))
            def _():
                for r, k in copies.pairs():
                    fwd.forward_arrival(r, k).wait_recv()
                for r, k in copies.pairs():
                    copies.ici(r, k).wait_send()
                    fwd.forward(r, k).wait_send()

    qspec = pl.BlockSpec((tq, 2 * LANES), lambda p, i: (i, p))
    kspec = pl.BlockSpec((lp, 2 * LANES), lambda p, i: (0, p))
    ospec = pl.BlockSpec((tq, LANES), lambda p, i: (i, p))
    ins, in_specs = [qa, ka, va, proj], [qspec, kspec, kspec, pl.BlockSpec((tq, LANES), lambda p, i: (i, 3 * D // LANES + p))]
    out_specs = [ospec] * 3
    out_shape = [jax.ShapeDtypeStruct((lp, D), F32), jax.ShapeDtypeStruct((lp, D), BF16), jax.ShapeDtypeStruct((lp, D), F32)]
    if ag is None:
        return pl.pallas_call(body, name="fox_attn_fwd", grid=(npair, nq), in_specs=in_specs, out_specs=out_specs,
                              out_shape=out_shape, compiler_params=_params(("parallel", "arbitrary")))(*ins)
    n = 3 * len(ag[1])
    return pl.pallas_call(
        body, name="fox_attn_fwd_ag", grid=(npair, nq), in_specs=in_specs + [ANY], out_specs=out_specs + [ANY],
        out_shape=out_shape + [jax.ShapeDtypeStruct(ag[0].shape, ag[0].dtype)],
        scratch_shapes=[pltpu.SemaphoreType.DMA((n,))] * 4, input_output_aliases={4: 3},
        compiler_params=_params(("arbitrary", "arbitrary")),
    )(*ins, ag[0])


def _fox_gate_bwd(dog, o, proj, lse, qa, swap=None):
    lp = o.shape[0]
    tr = LANES
    steps = lp // tr

    def body(d_ref, o_ref, g_ref, lse_ref, q_ref, *rest):
        if swap is None:
            do_ref, q2_ref, dgate_ref = rest
        else:
            src_ref, do_ref, q2_ref, dgate_ref, got_ref, send_sem, recv_sem = rest
            cp = _swap_copy(src_ref, got_ref, send_sem, recv_sem)

            @pl.when(pl.program_id(0) == 0)
            def _():
                cp.start()

            @pl.when(pl.program_id(0) == steps - 1)
            def _():
                cp.wait()

        down, _ = _head_sel(FOX_H, FOX_DH)
        sg = _sigmoid(g_ref[...])
        dv, ov = d_ref[...], o_ref[...]
        do = (dv * sg).astype(BF16).astype(F32)
        dgate_ref[...] = dv * ov * sg * (1.0 - sg)
        delta = _sel_r(do * ov, down)
        _spread(do, -_placed(delta, L_C), do_ref)
        r_, c_ = _iota((D, LANES), 0), _iota((D, LANES), 1)
        lse_c = _sel_r(lse_ref[...], (r_ == c_ * FOX_DH).astype(BF16))
        q2_ref[...] = (q_ref[...].astype(F32) - _placed(lse_c, L_LSE)).astype(BF16)

    spec = pl.BlockSpec((tr, D), lambda i: (i, 0))
    aug = pl.BlockSpec((tr, FOX_AUG), lambda i: (i, 0))
    in_specs = [spec, spec, pl.BlockSpec((tr, D), lambda i: (i, 3)), spec, aug]
    out_specs = [aug, aug, spec]
    out_shape = [jax.ShapeDtypeStruct((lp, FOX_AUG), BF16), jax.ShapeDtypeStruct((lp, FOX_AUG), BF16),
                 jax.ShapeDtypeStruct((lp, D), F32)]
    if swap is None:
        return pl.pallas_call(body, name="fox_gate_bwd", grid=(steps,), in_specs=in_specs, out_specs=out_specs,
                              out_shape=out_shape, compiler_params=_params(("parallel",)))(dog, o, proj, lse, qa)
    nb, r, c = swap.shape
    return pl.pallas_call(
        body, name="fox_gate_bwd_swap", grid=(steps,), in_specs=in_specs + [ANY], out_specs=out_specs + [ANY],
        out_shape=out_shape + [jax.ShapeDtypeStruct((nb, r // 2, c), swap.dtype)],
        scratch_shapes=[pltpu.SemaphoreType.DMA, pltpu.SemaphoreType.DMA], compiler_params=_params(("arbitrary",)),
    )(dog, o, proj, lse, qa, swap)


def _fox_attn_bwd(q2, ka, va, doa, rs=None):
    lp = q2.shape[0]
    t = _tile(lp, FOX_TQ)
    nb = lp // t
    npair = FOX_H // 2

    def body(q_ref, k_ref, v_ref, do_ref, *rest):
        if rs is None:
            dq_ref, dk_ref, dv_ref, dc_ref, dq_acc, dk_acc, dv_acc, dc_acc = rest
        else:
            s_ref, dq_ref, dk_ref, dv_ref, dc_ref, got_ref, dq_acc, dk_acc, dv_acc, dc_acc, send_sems, recv_sems = rest
            sends, arrivals = _scatter_copies(s_ref, got_ref, send_sems, recv_sems)

            @pl.when((pl.program_id(0) == 0) & (pl.program_id(1) == 0))
            def _():
                for cp in sends:
                    cp.start()

            @pl.when((pl.program_id(0) == npair - 1) & (pl.program_id(1) == nb - 1))
            def _():
                for cp in arrivals:
                    cp.wait_recv()
                for cp in sends:
                    cp.wait_send()

        j = pl.program_id(1)

        @pl.when(j == 0)
        def _():
            dq_acc[...] = jnp.zeros_like(dq_acc)

        causal = _iota((t, t), 1) <= _iota((t, t), 0)
        ks = [k_ref[:, hh * LANES:(hh + 1) * LANES] for hh in range(2)]
        vs = [v_ref[:, hh * LANES:(hh + 1) * LANES] for hh in range(2)]
        dk_acc[...] = jnp.zeros_like(dk_acc)
        dv_acc[...] = jnp.zeros_like(dv_acc)
        dc_acc[...] = jnp.zeros_like(dc_acc)

        def block(i, diag):
            off = pl.multiple_of(i * t, t)
            for hh in range(2):
                q = q_ref[pl.ds(off, t), hh * LANES:(hh + 1) * LANES]
                do = do_ref[pl.ds(off, t), hh * LANES:(hh + 1) * LANES]
                s = nt(q, ks[hh])
                if diag:
                    s = jnp.where(causal, s, -1e30)
                p = jnp.exp(s)
                ds = p * nt(do, vs[hh])
                dc_acc[hh] += jnp.sum(ds, axis=0, keepdims=True)
                dsb = ds.astype(BF16)
                dv_acc[hh] += tn(p.astype(BF16), do)
                dk_acc[hh] += tn(dsb, q)
                dq_acc[hh, pl.ds(off, t), :] += nn(dsb, ks[hh])

        block(j, True)
        below = nb - 1 - j

        def step(u, c):
            block(j + 1 + 2 * u, False)
            block(j + 2 + 2 * u, False)
            return c

        lax.fori_loop(0, below // 2, step, 0)

        @pl.when(below % 2 == 1)
        def _():
            block(nb - 1, False)
        left = _iota((t, LANES), 1) < FOX_DH
        dk_ref[...] = jnp.where(left, dk_acc[0], pltpu.roll(dk_acc[1], FOX_DH, 1))
        dv_ref[...] = jnp.where(left, dv_acc[0], pltpu.roll(dv_acc[1], FOX_DH, 1))
        for hh in range(2):
            dc_ref[hh] = jnp.broadcast_to(-dc_acc[hh], (8, t))

        @pl.when(j == nb - 1)
        def _():
            left = _iota((lp, LANES), 1) < FOX_DH
            dq_ref[...] = jnp.where(left, dq_acc[0], pltpu.roll(dq_acc[1], FOX_DH, 1))

    full = pl.BlockSpec((lp, 2 * LANES), lambda p, j: (0, p))
    kblk = pl.BlockSpec((t, 2 * LANES), lambda p, j: (j, p))
    oblk = pl.BlockSpec((t, LANES), lambda p, j: (j, p))
    in_specs = [full, kblk, kblk, full]
    out_specs = [pl.BlockSpec((lp, LANES), lambda p, j: (0, p)), oblk, oblk, pl.BlockSpec((2, 8, t), lambda p, j: (p, 0, j))]
    out_shape = [jax.ShapeDtypeStruct((lp, D), F32)] * 3 + [jax.ShapeDtypeStruct((FOX_H, 8, lp), F32)]
    scratch = [pltpu.VMEM((2, lp, LANES), F32), pltpu.VMEM((2, t, LANES), F32), pltpu.VMEM((2, t, LANES), F32),
               pltpu.VMEM((2, 1, t), F32)]
    if rs is None:
        return pl.pallas_call(body, name="fox_attn_bwd", grid=(npair, nb), in_specs=in_specs, out_specs=out_specs,
                              out_shape=out_shape, scratch_shapes=scratch,
                              compiler_params=_params(("parallel", "arbitrary")))(q2, ka, va, doa)
    return pl.pallas_call(
        body, name="fox_attn_bwd_rs", grid=(npair, nb), in_specs=in_specs + [ANY], out_specs=out_specs + [ANY],
        out_shape=out_shape + [jax.ShapeDtypeStruct((3,) + rs.shape[1:], rs.dtype)],
        scratch_shapes=scratch + [pltpu.SemaphoreType.DMA((3,)), pltpu.SemaphoreType.DMA((3,))],
        compiler_params=_params(("arbitrary", "arbitrary")),
    )(q2, ka, va, doa, rs)


def _fox_prep_bwd(proj, b_f, q_gain, k_gain, dqn, dkn, dv, dgate, dct):
    lp = proj.shape[0]
    nb = lp // LANES

    def body(p_ref, bf_ref, qg_ref, kg_ref, dq_ref, dk_ref, dv_ref, dg_ref, dc_ref,
             dp_ref, dqg_ref, dkg_ref, dbf_ref, carry):
        i = pl.program_id(0)

        @pl.when(i == 0)
        def _():
            carry[...] = jnp.zeros_like(carry)
            dqg_ref[...] = jnp.zeros_like(dqg_ref)
            dkg_ref[...] = jnp.zeros_like(dkg_ref)
            dbf_ref[...] = jnp.zeros_like(dbf_ref)

        down, up = _head_sel(FOX_H, FOX_DH)

        def norm_bwd(x, gain, dy, scale, dgain_ref):
            ms = _sel_r2(x * x, down) * (1.0 / FOX_DH)
            r = _sel_r2(lax.rsqrt(ms + EPS), up)
            u = dy * gain * scale
            mean_xu = _sel_r2(_sel_r2(x * u, down) * (1.0 / FOX_DH), up)
            dgain_ref[...] += jnp.sum(dy * scale * x * r, axis=0, keepdims=True)
            return r * u - x * (r * r * r) * mean_xu

        dp_ref[:, 0:D] = norm_bwd(p_ref[:, 0:D], qg_ref[...], dq_ref[...], FOX_DH ** -0.5, dqg_ref).astype(BF16)
        dp_ref[:, D:2 * D] = norm_bwd(p_ref[:, D:2 * D], kg_ref[...], dk_ref[...], 1.0, dkg_ref).astype(BF16)
        dp_ref[:, 2 * D:3 * D] = dv_ref[...].astype(BF16)
        dp_ref[:, 3 * D:4 * D] = dg_ref[...].astype(BF16)
        rows = jnp.concatenate([dc_ref[h, 0:1, :] for h in range(FOX_H)] + [jnp.zeros((LANES - FOX_H, LANES), F32)], axis=0)
        dlf = _sel_l(_tri(LANES, upper=True).astype(BF16), rows.T) + carry[0:1, :]
        carry[...] = jnp.broadcast_to(dlf[0:1, :], carry.shape)
        lane = _iota((LANES, LANES), 1)
        z = p_ref[:, 4 * D:4 * D + LANES] + bf_ref[...]
        df = jnp.where(lane < FOX_H, dlf * _sigmoid(-z), 0.0)
        dp_ref[:, 4 * D:4 * D + LANES] = df.astype(BF16)
        dbf_ref[...] += jnp.sum(df, axis=0, keepdims=True)

    rev = lambda i: (nb - 1 - i, 0)
    blk = pl.BlockSpec((LANES, D), rev)
    row = pl.BlockSpec((1, D), lambda i: (0, 0))
    row128 = pl.BlockSpec((1, LANES), lambda i: (0, 0))
    return pl.pallas_call(
        body, name="fox_prep_bwd", grid=(nb,),
        in_specs=[pl.BlockSpec((LANES, FOX_INP), rev), row128, row, row, blk, blk, blk, blk,
                  pl.BlockSpec((FOX_H, 8, LANES), lambda i: (0, 0, nb - 1 - i))],
        out_specs=[pl.BlockSpec((LANES, FOX_INP), rev), row, row, row128],
        out_shape=[jax.ShapeDtypeStruct((lp, FOX_INP), BF16), jax.ShapeDtypeStruct((1, D), F32),
                   jax.ShapeDtypeStruct((1, D), F32), jax.ShapeDtypeStruct((1, LANES), F32)],
        scratch_shapes=[pltpu.VMEM((8, LANES), F32)],
        compiler_params=_params(("arbitrary",)),
    )(proj, jnp.pad(b_f, (0, LANES - FOX_H)).reshape(1, LANES), jnp.tile(q_gain, FOX_H).reshape(1, D),
      jnp.tile(k_gain, FOX_H).reshape(1, D), dqn, dkn, dv, dgate, dct)


def _gla_gates(p_ref, wa_ref, ba_ref):
    a_lr = p_ref[:, 3072:3072 + LANES]
    z = nn(a_lr.astype(BF16), wa_ref[...].astype(BF16)) + ba_ref[...]
    g = _log_sigmoid(z) * (1.0 / GLA_NORM)
    b = _sel_l(_tri(CHUNK).astype(BF16), g)
    return a_lr, z, b


def _gla_chunk_fwd(q, k, v, b, st0):
    hs = range(len(q))
    low = _tri(CHUNK)
    bl = [b[h][CHUNK - 1:CHUNK, :] for h in hs]
    qe = [q[h] * jnp.exp(b[h]) for h in hs]
    ke = [k[h] * jnp.exp(-b[h]) for h in hs]
    kd = [k[h] * jnp.exp(bl[h] - b[h]) for h in hs]
    a = [jnp.where(low, nt(qe[h], ke[h]), 0.0) for h in hs]
    o = [nn(a[h], v[h]) + nt(qe[h], st0[h]) for h in hs]
    st1 = [st0[h] * jnp.exp(bl[h]) + tn(v[h], kd[h]) for h in hs]
    return o, st1, (qe, ke, kd, a, bl)


def _gla_slices(p_ref, b_all, h):
    q = p_ref[:, h * GLA_DK:(h + 1) * GLA_DK] * (GLA_DK ** -0.5)
    k = p_ref[:, GLA_QK + h * GLA_DK:GLA_QK + (h + 1) * GLA_DK]
    v = p_ref[:, 2 * GLA_QK + h * GLA_DV:2 * GLA_QK + (h + 1) * GLA_DV]
    r = p_ref[:, 2 * GLA_QK + GLA_V + h * GLA_DV:2 * GLA_QK + GLA_V + (h + 1) * GLA_DV]
    return q, k, v, r, b_all[:, h * GLA_DK:(h + 1) * GLA_DK]


def _gla_fwd(proj, w_alpha2, b_alpha, o_gain):
    lp = proj.shape[0]
    nc = lp // CHUNK

    def body(p_ref, wa_ref, ba_ref, og_ref, o_ref, y_ref, s_ref, st):
        @pl.when(pl.program_id(0) == 0)
        def _():
            st[...] = jnp.zeros_like(st)

        _, _, b_all = _gla_gates(p_ref, wa_ref, ba_ref)
        hs = range(GLA_H)
        parts = [_gla_slices(p_ref, b_all, h) for h in hs]
        st0 = [st[h] for h in hs]
        for h in hs:
            s_ref[0, h] = st0[h]
        o, st1, _ = _gla_chunk_fwd([p[0] for p in parts], [p[1] for p in parts], [p[2] for p in parts],
                                   [p[4] for p in parts], st0)
        for h in hs:
            st[h] = st1[h]
            o_ref[:, h * GLA_DV:(h + 1) * GLA_DV] = o[h]
            rs = lax.rsqrt(jnp.mean(o[h] * o[h], axis=-1, keepdims=True) + EPS)
            y_ref[:, h * GLA_DV:(h + 1) * GLA_DV] = (o[h] * rs * og_ref[...] * _silu(parts[h][3])).astype(BF16)

    blk = pl.BlockSpec((CHUNK, D), lambda i: (i, 0))
    return pl.pallas_call(
        body, name="gla_fwd", grid=(nc,),
        in_specs=[pl.BlockSpec((CHUNK, GLA_INP), lambda i: (i, 0)), pl.BlockSpec((LANES, GLA_QK), lambda i: (0, 0)),
                  pl.BlockSpec((1, GLA_QK), lambda i: (0, 0)), pl.BlockSpec((1, GLA_DV), lambda i: (0, 0))],
        out_specs=[blk, blk, pl.BlockSpec((1, GLA_H, GLA_DV, GLA_DK), lambda i: (i, 0, 0, 0))],
        out_shape=[jax.ShapeDtypeStruct((lp, D), F32), jax.ShapeDtypeStruct((lp, D), BF16),
                   jax.ShapeDtypeStruct((nc, GLA_H, GLA_DV, GLA_DK), F32)],
        scratch_shapes=[pltpu.VMEM((GLA_H, GLA_DV, GLA_DK), F32)],
        compiler_params=_params(("arbitrary",)),
    )(proj, jnp.pad(w_alpha2, ((0, LANES - GLA_RANK), (0, 0))), b_alpha.reshape(1, GLA_QK), o_gain.reshape(1, GLA_DV))


def _gla_bwd(proj, w_alpha2, b_alpha, o_gain, o, states, dy):
    lp = proj.shape[0]
    nc = lp // CHUNK

    def body(p_ref, wa_ref, ba_ref, og_ref, o_ref, s_ref, dy_ref, dp_ref, dwa_ref, dba_ref, dog_ref, dst):
        @pl.when(pl.program_id(0) == 0)
        def _():
            dst[...] = jnp.zeros_like(dst)
            dwa_ref[...] = jnp.zeros_like(dwa_ref)
            dba_ref[...] = jnp.zeros_like(dba_ref)
            dog_ref[...] = jnp.zeros_like(dog_ref)

        a_lr, z, b_all = _gla_gates(p_ref, wa_ref, ba_ref)
        last_row = _iota((CHUNK, GLA_DK), 0) == CHUNK - 1
        rev = _tri(CHUNK, upper=True).astype(BF16)
        hs = range(GLA_H)
        scale = GLA_DK ** -0.5
        parts = [_gla_slices(p_ref, b_all, h) for h in hs]
        q, k, v, b = [p[0] for p in parts], [p[1] for p in parts], [p[2] for p in parts], [p[4] for p in parts]
        st0 = [s_ref[0, h] for h in hs]
        dst1 = [dst[h] for h in hs]
        do = []
        for h in hs:
            r = parts[h][3]
            ov = o_ref[:, h * GLA_DV:(h + 1) * GLA_DV]
            dyv = dy_ref[:, h * GLA_DV:(h + 1) * GLA_DV]
            rs = lax.rsqrt(jnp.mean(ov * ov, axis=-1, keepdims=True) + EPS)
            on = ov * rs
            dp_ref[:, 2 * GLA_QK + GLA_V + h * GLA_DV:2 * GLA_QK + GLA_V + (h + 1) * GLA_DV] = (
                dyv * on * og_ref[...] * _dsilu(r)).astype(BF16)
            don = dyv * _silu(r)
            dog_ref[...] += jnp.sum(don * on, axis=0, keepdims=True)
            u = don * og_ref[...]
            do.append(rs * u - ov * (rs * rs * rs) * jnp.mean(ov * u, axis=-1, keepdims=True))
        _, _, (qe, ke, kd, a, bl) = _gla_chunk_fwd(q, k, v, b, st0)
        low = _tri(CHUNK)
        da = [jnp.where(low, nt(do[h], v[h]), 0.0) for h in hs]
        dkd = [nn(v[h], dst1[h]) for h in hs]
        dvv = [tn(a[h], do[h]) + nt(kd[h], dst1[h]) for h in hs]
        dqe = [nn(da[h], ke[h]) + nn(do[h], st0[h]) for h in hs]
        dke = [tn(da[h], qe[h]) for h in hs]
        dg_parts = []
        for h in hs:
            ebl = jnp.exp(bl[h])
            dst[h] = dst1[h] * ebl + tn(do[h], qe[h])
            db = dqe[h] * qe[h] - dke[h] * ke[h] - dkd[h] * kd[h]
            db_last = (jnp.sum(dkd[h] * kd[h], axis=0, keepdims=True)
                       + jnp.sum(dst1[h] * st0[h], axis=0, keepdims=True) * ebl)
            db = db + jnp.where(last_row, db_last, 0.0)
            dg_parts.append(_sel_l(rev, db))
            dp_ref[:, h * GLA_DK:(h + 1) * GLA_DK] = (dqe[h] * jnp.exp(b[h]) * scale).astype(BF16)
            dp_ref[:, GLA_QK + h * GLA_DK:GLA_QK + (h + 1) * GLA_DK] = (
                dke[h] * jnp.exp(-b[h]) + dkd[h] * jnp.exp(bl[h] - b[h])).astype(BF16)
            dp_ref[:, 2 * GLA_QK + h * GLA_DV:2 * GLA_QK + (h + 1) * GLA_DV] = dvv[h].astype(BF16)
        dg = jnp.concatenate(dg_parts, axis=1)
        dz = dg * (1.0 / GLA_NORM) * _sigmoid(-z)
        dzb = dz.astype(BF16)
        dp_ref[:, 3072:3072 + LANES] = nt(dzb, wa_ref[...].astype(BF16)).astype(BF16)
        dwa_ref[...] += tn(a_lr.astype(BF16), dzb)
        dba_ref[...] += jnp.sum(dz, axis=0, keepdims=True)

    rv = lambda i: (nc - 1 - i, 0)
    blk = pl.BlockSpec((CHUNK, D), rv)
    fixed = lambda r, c: pl.BlockSpec((r, c), lambda i: (0, 0))
    return pl.pallas_call(
        body, name="gla_bwd", grid=(nc,),
        in_specs=[pl.BlockSpec((CHUNK, GLA_INP), rv), fixed(LANES, GLA_QK), fixed(1, GLA_QK), fixed(1, GLA_DV), blk,
                  pl.BlockSpec((1, GLA_H, GLA_DV, GLA_DK), lambda i: (nc - 1 - i, 0, 0, 0)), blk],
        out_specs=[pl.BlockSpec((CHUNK, GLA_INP), rv), fixed(LANES, GLA_QK), fixed(1, GLA_QK), fixed(1, GLA_DV)],
        out_shape=[jax.ShapeDtypeStruct((lp, GLA_INP), BF16), jax.ShapeDtypeStruct((LANES, GLA_QK), F32),
                   jax.ShapeDtypeStruct((1, GLA_QK), F32), jax.ShapeDtypeStruct((1, GLA_DV), F32)],
        scratch_shapes=[pltpu.VMEM((GLA_H, GLA_DV, GLA_DK), F32)],
        compiler_params=_params(("arbitrary",)),
    )(proj, jnp.pad(w_alpha2, ((0, LANES - GLA_RANK), (0, 0))), b_alpha.reshape(1, GLA_QK), o_gain.reshape(1, GLA_DV),
      o, states, dy)


HI = lax.Precision.HIGH


def _gdn_pre(prev_ref, p_ref, cw_ref, al_ref, dt_ref):
    xc = jnp.concatenate([prev_ref[:, 0:GDN_CONV], p_ref[:, 0:GDN_CONV]], axis=0)
    shifted = [pltpu.roll(xc, 3 - j, 0)[CHUNK:, :] if j < 3 else xc[CHUNK:, :] for j in range(4)]
    conv = sum(shifted[j] * cw_ref[j:j + 1, :] for j in range(4))
    act = _silu(conv)
    slab = p_ref[:, 4096:4096 + LANES]
    lane = _iota((CHUNK, LANES), 1)
    zs = slab + dt_ref[...]
    g = jnp.where(lane < GDN_H, -jnp.exp(al_ref[...]) * _softplus(zs), 0.0)
    bs = _sel_l(_tri(CHUNK).astype(BF16), g)
    beta = _sigmoid(slab)
    return shifted, conv, act, slab, zs, g, bs, beta


def _l2n(x):
    r = lax.rsqrt(jnp.sum(x * x, axis=-1, keepdims=True) + EPS)
    return x * r, r


def _gdn_chunk_fwd(q, k, v, beta, bcol, brow, s0, tinv=None):
    hs = range(len(q))
    ii, jj = _iota((CHUNK, CHUNK), 0), _iota((CHUNK, CHUNK), 1)
    low, eye = ii >= jj, (ii == jj).astype(F32)
    dm = [jnp.where(low, jnp.exp(jnp.where(low, bcol[h] - brow[h], 0.0)), 0.0) for h in hs]
    dstrict = [jnp.where(ii > jj, dm[h], 0.0) for h in hs]
    eb = [jnp.exp(bcol[h]) for h in hs]
    bl = [bcol[h][CHUNK - 1:CHUNK, :] for h in hs]
    kb = [k[h] * beta[h] for h in hs]
    vb = [v[h] * beta[h] for h in hs]
    nmat = [nt(kb[h], k[h]) * dstrict[h] for h in hs]
    if tinv is None:
        x = [eye - nmat[h] for h in hs]
        pw = [nn(nmat[h], nmat[h], precision=HI) for h in hs]
        for it in range(5):
            x = [x[h] + nn(x[h], pw[h], precision=HI) for h in hs]
            if it < 4:
                pw = [nn(pw[h], pw[h], precision=HI) for h in hs]
    else:
        x = tinv
    kbe = [kb[h] * eb[h] for h in hs]
    u = [nn(x[h], vb[h], precision=HI) for h in hs]
    w = [nn(x[h], kbe[h], precision=HI) for h in hs]
    vn = [u[h] - nn(w[h], s0[h]) for h in hs]
    pm = [nt(q[h], k[h]) * dm[h] for h in hs]
    qe = [q[h] * eb[h] for h in hs]
    o = [nn(pm[h], vn[h]) + nn(qe[h], s0[h]) for h in hs]
    kd = [k[h] * jnp.exp(bl[h] - bcol[h]) for h in hs]
    s1 = [s0[h] * jnp.exp(bl[h]) + tn(kd[h], vn[h]) for h in hs]
    return o, s1, dict(dm=dm, dstrict=dstrict, eb=eb, bl=bl, kb=kb, vb=vb, nmat=nmat, tinv=x, kbe=kbe, u=u, w=w, vn=vn,
                       pm=pm, qe=qe, kd=kd)


def _gdn_heads(act, beta_slab, bs, h):
    qa = act[:, h * GDN_DK:(h + 1) * GDN_DK]
    ka = act[:, GDN_H * GDN_DK + h * GDN_DK:GDN_H * GDN_DK + (h + 1) * GDN_DK]
    v = act[:, 2 * GDN_H * GDN_DK + h * GDN_DV:2 * GDN_H * GDN_DK + (h + 1) * GDN_DV]
    return qa, ka, v, beta_slab[:, GDN_H + h:GDN_H + h + 1], bs[:, h:h + 1]


def _gdn_fwd(proj, conv_w, a_log, dt_bias, o_gain):
    lp = proj.shape[0]
    nc = lp // CHUNK

    def body(prev_ref, p_ref, cw_ref, al_ref, dt_ref, og_ref, o_ref, y_ref, s_ref, t_ref, st):
        @pl.when(pl.program_id(0) == 0)
        def _():
            st[...] = jnp.zeros_like(st)

        _, _, act, _, _, _, bs, beta = _gdn_pre(prev_ref, p_ref, cw_ref, al_ref, dt_ref)
        bst = bs.T
        hs = range(GDN_H)
        parts = [_gdn_heads(act, beta, bs, h) for h in hs]
        q = [_l2n(parts[h][0])[0] * (GDN_DK ** -0.5) for h in hs]
        k = [_l2n(parts[h][1])[0] for h in hs]
        s0 = [st[h] for h in hs]
        for h in hs:
            s_ref[0, h] = s0[h]
        o, s1, f = _gdn_chunk_fwd(q, k, [parts[h][2] for h in hs], [parts[h][3] for h in hs], [parts[h][4] for h in hs],
                                  [bst[h:h + 1, :] for h in hs], s0)
        for h in hs:
            t_ref[0, h] = f["tinv"][h]
            st[h] = s1[h]
            o_ref[:, h * GDN_DV:(h + 1) * GDN_DV] = o[h]
            rs = lax.rsqrt(jnp.mean(o[h] * o[h], axis=-1, keepdims=True) + EPS)
            gate = p_ref[:, GDN_CONV + h * GDN_DV:GDN_CONV + (h + 1) * GDN_DV]
            y_ref[:, h * GDN_DV:(h + 1) * GDN_DV] = (o[h] * rs * og_ref[...] * _silu(gate)).astype(BF16)

    blk = pl.BlockSpec((CHUNK, D), lambda i: (i, 0))
    fixed = lambda r, c: pl.BlockSpec((r, c), lambda i: (0, 0))
    return pl.pallas_call(
        body, name="gdn_fwd", grid=(nc,),
        in_specs=[pl.BlockSpec((CHUNK, GDN_INP), lambda i: (jnp.maximum(i - 1, 0), 0)),
                  pl.BlockSpec((CHUNK, GDN_INP), lambda i: (i, 0)), fixed(8, GDN_CONV), fixed(1, LANES), fixed(1, LANES),
                  fixed(1, GDN_DV)],
        out_specs=[blk, blk, pl.BlockSpec((1, GDN_H, GDN_DK, GDN_DV), lambda i: (i, 0, 0, 0)),
                   pl.BlockSpec((1, GDN_H, CHUNK, CHUNK), lambda i: (i, 0, 0, 0))],
        out_shape=[jax.ShapeDtypeStruct((lp, D), F32), jax.ShapeDtypeStruct((lp, D), BF16),
                   jax.ShapeDtypeStruct((nc, GDN_H, GDN_DK, GDN_DV), F32), jax.ShapeDtypeStruct((nc, GDN_H, CHUNK, CHUNK), F32)],
        scratch_shapes=[pltpu.VMEM((GDN_H, GDN_DK, GDN_DV), F32)],
        compiler_params=_params(("arbitrary",)),
    )(proj, proj, jnp.pad(conv_w.reshape(4, GDN_CONV), ((0, 4), (0, 0))), jnp.pad(a_log, (0, LANES - GDN_H)).reshape(1, LANES),
      jnp.pad(dt_bias, (0, LANES - GDN_H)).reshape(1, LANES), o_gain.reshape(1, GDN_DV))


def _gdn_bwd(proj, conv_w, a_log, dt_bias, o_gain, o, states, tinvs, dy):
    lp = proj.shape[0]
    nc = lp // CHUNK

    def body(prev_ref, p_ref, cw_ref, al_ref, dt_ref, og_ref, o_ref, s_ref, t_ref, dy_ref,
             dp_ref, dcw_ref, dal_ref, ddt_ref, dog_ref, dst, dconv_next):
        @pl.when(pl.program_id(0) == 0)
        def _():
            dst[...] = jnp.zeros_like(dst)
            dconv_next[...] = jnp.zeros_like(dconv_next)
            dcw_ref[...] = jnp.zeros_like(dcw_ref)
            dal_ref[...] = jnp.zeros_like(dal_ref)
            ddt_ref[...] = jnp.zeros_like(ddt_ref)
            dog_ref[...] = jnp.zeros_like(dog_ref)

        shifted, conv, act, slab, zs, g, bs, beta = _gdn_pre(prev_ref, p_ref, cw_ref, al_ref, dt_ref)
        bst = bs.T
        lane = _iota((CHUNK, LANES), 1)
        ones = jnp.ones((CHUNK, LANES), F32)
        db_slab = jnp.zeros((CHUNK, LANES), F32)
        dbeta_slab = jnp.zeros((CHUNK, LANES), F32)
        last_row = _iota((CHUNK, 1), 0) == CHUNK - 1
        hs = range(GDN_H)
        scale = GDN_DK ** -0.5
        parts = [_gdn_heads(act, beta, bs, h) for h in hs]
        qa, ka, v = [parts[h][0] for h in hs], [parts[h][1] for h in hs], [parts[h][2] for h in hs]
        bet, bcol = [parts[h][3] for h in hs], [parts[h][4] for h in hs]
        qn_ = [_l2n(qa[h]) for h in hs]
        kn_ = [_l2n(ka[h]) for h in hs]
        q = [qn_[h][0] * scale for h in hs]
        k, rq, rk = [kn_[h][0] for h in hs], [qn_[h][1] for h in hs], [kn_[h][1] for h in hs]
        s0 = [s_ref[0, h] for h in hs]
        ds1 = [dst[h] for h in hs]
        do = []
        for h in hs:
            ov = o_ref[:, h * GDN_DV:(h + 1) * GDN_DV]
            dyv = dy_ref[:, h * GDN_DV:(h + 1) * GDN_DV]
            gate = p_ref[:, GDN_CONV + h * GDN_DV:GDN_CONV + (h + 1) * GDN_DV]
            rs = lax.rsqrt(jnp.mean(ov * ov, axis=-1, keepdims=True) + EPS)
            on = ov * rs
            dp_ref[:, GDN_CONV + h * GDN_DV:GDN_CONV + (h + 1) * GDN_DV] = (dyv * on * og_ref[...] * _dsilu(gate)).astype(BF16)
            don = dyv * _silu(gate)
            dog_ref[...] += jnp.sum(don * on, axis=0, keepdims=True)
            uu = don * og_ref[...]
            do.append(rs * uu - ov * (rs * rs * rs) * jnp.mean(ov * uu, axis=-1, keepdims=True))
        _, _, f = _gdn_chunk_fwd(q, k, v, bet, bcol, [bst[h:h + 1, :] for h in hs], s0, tinv=[t_ref[0, h] for h in hs])
        dm, dstrict, eb, bl, kb, nmat, tinv = f["dm"], f["dstrict"], f["eb"], f["bl"], f["kb"], f["nmat"], f["tinv"]
        kbe, u, w, vn, pm, qe, kd = f["kbe"], f["u"], f["w"], f["vn"], f["pm"], f["qe"], f["kd"]
        ebl = [jnp.exp(bl[h]) for h in hs]
        dvn = [tn(pm[h], do[h]) + nn(kd[h], ds1[h]) for h in hs]
        dpr = [nt(do[h], vn[h]) for h in hs]
        dqe = [nt(do[h], s0[h]) for h in hs]
        dkd = [nt(vn[h], ds1[h]) for h in hs]
        for h in hs:
            dst[h] = ds1[h] * ebl[h] + tn(qe[h], do[h]) - tn(w[h], dvn[h])
        du_ = [tn(tinv[h], dvn[h], precision=HI) for h in hs]
        dw_ = [tn(tinv[h], -nt(dvn[h], s0[h]), precision=HI) for h in hs]
        dn = [-(nt(du_[h], u[h]) + nt(dw_[h], w[h])) for h in hs]
        dqk = [dpr[h] * dm[h] for h in hs]
        dkk = [dn[h] * dstrict[h] for h in hs]
        gsum = [dpr[h] * pm[h] + dn[h] * nmat[h] for h in hs]
        dkb = [nn(dkk[h], k[h]) + dw_[h] * eb[h] for h in hs]
        dk = [tn(dkk[h], kb[h]) + tn(dqk[h], q[h]) + dkd[h] * jnp.exp(bl[h] - bcol[h]) + dkb[h] * bet[h] for h in hs]
        dq = [nn(dqk[h], k[h]) + dqe[h] * eb[h] for h in hs]
        colsum = [tn(gsum[h], ones, precision=HI)[:, 0:1] for h in hs]
        dact_q, dact_k, dact_v = [], [], []
        for h in hs:
            dbeta = jnp.sum(dkb[h] * k[h], axis=-1, keepdims=True) + jnp.sum(du_[h] * v[h], axis=-1, keepdims=True)
            skd = jnp.sum(dkd[h] * kd[h], axis=-1, keepdims=True)
            db = (jnp.sum(gsum[h], axis=-1, keepdims=True) - colsum[h] + jnp.sum(dqe[h] * qe[h], axis=-1, keepdims=True)
                  + jnp.sum(dw_[h] * kbe[h], axis=-1, keepdims=True) - skd)
            db_last = jnp.sum(skd, axis=0, keepdims=True) + jnp.sum(ds1[h] * s0[h]) * ebl[h]
            db = db + jnp.where(last_row, db_last, 0.0)
            db_slab = db_slab + jnp.where(lane == h, db, 0.0)
            dbeta_slab = dbeta_slab + jnp.where(lane == GDN_H + h, dbeta, 0.0)
            dqn = dq[h] * scale
            dact_q.append(rq[h] * dqn - qa[h] * (rq[h] * rq[h] * rq[h]) * jnp.sum(qa[h] * dqn, axis=-1, keepdims=True))
            dact_k.append(rk[h] * dk[h] - ka[h] * (rk[h] * rk[h] * rk[h]) * jnp.sum(ka[h] * dk[h], axis=-1, keepdims=True))
            dact_v.append(du_[h] * bet[h])
        dact = jnp.concatenate(dact_q + dact_k + dact_v, axis=1)
        dconv = dact * _dsilu(conv)
        for j in range(4):
            dcw_ref[j:j + 1, :] += jnp.sum(dconv * shifted[j], axis=0, keepdims=True)
        dcat = jnp.concatenate([dconv, dconv_next[...]], axis=0)
        dx = dconv * cw_ref[3:4, :]
        for j in range(3):
            dx = dx + pltpu.roll(dcat, 2 * CHUNK - (3 - j), 0)[:CHUNK, :] * cw_ref[j:j + 1, :]
        dconv_next[...] = dconv
        dp_ref[:, 0:GDN_CONV] = dx.astype(BF16)
        dg = _sel_l(_tri(CHUNK, upper=True).astype(BF16), db_slab)
        da = dg * (-jnp.exp(al_ref[...])) * _sigmoid(zs)
        da = jnp.where(lane < GDN_H, da, 0.0)
        dal_ref[...] += jnp.sum(dg * g, axis=0, keepdims=True)
        ddt_ref[...] += jnp.sum(da, axis=0, keepdims=True)
        dp_ref[:, 4096:4096 + LANES] = (da + dbeta_slab * beta * (1.0 - beta)).astype(BF16)

    rv = lambda i: (nc - 1 - i, 0)
    blk = pl.BlockSpec((CHUNK, D), rv)
    fixed = lambda r, c: pl.BlockSpec((r, c), lambda i: (0, 0))
    return pl.pallas_call(
        body, name="gdn_bwd", grid=(nc,),
        in_specs=[pl.BlockSpec((CHUNK, GDN_INP), lambda i: (jnp.maximum(nc - 2 - i, 0), 0)),
                  pl.BlockSpec((CHUNK, GDN_INP), rv), fixed(8, GDN_CONV), fixed(1, LANES), fixed(1, LANES), fixed(1, GDN_DV),
                  blk, pl.BlockSpec((1, GDN_H, GDN_DK, GDN_DV), lambda i: (nc - 1 - i, 0, 0, 0)),
                  pl.BlockSpec((1, GDN_H, CHUNK, CHUNK), lambda i: (nc - 1 - i, 0, 0, 0)), blk],
        out_specs=[pl.BlockSpec((CHUNK, GDN_INP), rv), fixed(8, GDN_CONV), fixed(1, LANES), fixed(1, LANES), fixed(1, GDN_DV)],
        out_shape=[jax.ShapeDtypeStruct((lp, GDN_INP), BF16), jax.ShapeDtypeStruct((8, GDN_CONV), F32),
                   jax.ShapeDtypeStruct((1, LANES), F32), jax.ShapeDtypeStruct((1, LANES), F32),
                   jax.ShapeDtypeStruct((1, GDN_DV), F32)],
        scratch_shapes=[pltpu.VMEM((GDN_H, GDN_DK, GDN_DV), F32), pltpu.VMEM((CHUNK, GDN_CONV), F32)],
        compiler_params=_params(("arbitrary",)),
    )(proj, proj, jnp.pad(conv_w.reshape(4, GDN_CONV), ((0, 4), (0, 0))), jnp.pad(a_log, (0, LANES - GDN_H)).reshape(1, LANES),
      jnp.pad(dt_bias, (0, LANES - GDN_H)).reshape(1, LANES), o_gain.reshape(1, GDN_DV), o, states, tinvs, dy)


def _coords():
    return lax.axis_index("x"), lax.axis_index("y"), lax.axis_index("c")


def _other_chips(x, y):
    return [(1 - x, y, 2 * (1 - x) + y), (x, 1 - y, 2 * x + 1 - y), (1 - x, 1 - y, 2 * (1 - x) + 1 - y)]


def _gather8(v, *, reduce, name):
    r, c = v.shape

    def body(v_ref, out_ref, *scratch):
        if reduce:
            buf, send_sems, recv_sems = scratch
        else:
            buf = out_ref
            send_sems, recv_sems = scratch
        x, y, cc = _coords()
        me = 4 * x + 2 * y + cc
        buf[me] = v_ref[...]
        copies = []
        for k in range(1, 8):
            px, py, pc = x ^ (k >> 2), y ^ ((k >> 1) & 1), cc ^ (k & 1)
            copies.append(pltpu.make_async_remote_copy(
                src_ref=v_ref, dst_ref=buf.at[me], send_sem=send_sems.at[k - 1], recv_sem=recv_sems.at[k - 1],
                device_id=(px, py, pc), device_id_type=MESH))
        for cp in copies:
            cp.start()
        for k in range(1, 8):
            peer = (x ^ (k >> 2)) * 4 + (y ^ ((k >> 1) & 1)) * 2 + (cc ^ (k & 1))
            pltpu.make_async_remote_copy(
                src_ref=v_ref, dst_ref=buf.at[peer], send_sem=send_sems.at[k - 1], recv_sem=recv_sems.at[k - 1],
                device_id=(x, y, cc), device_id_type=MESH).wait_recv()
        for cp in copies:
            cp.wait_send()
        if reduce:
            acc = buf[0]
            for d in range(1, 8):
                acc = acc + buf[d]
            out_ref[...] = acc

    scratch = [pltpu.SemaphoreType.DMA((7,)), pltpu.SemaphoreType.DMA((7,))]
    if reduce:
        scratch = [pltpu.VMEM((8, r, c), F32)] + scratch
    return pl.pallas_call(
        body, name=name, in_specs=[VM], out_specs=VM,
        out_shape=jax.ShapeDtypeStruct((r, c) if reduce else (8, r, c), F32),
        scratch_shapes=scratch, compiler_params=_params(),
    )(v)


class _AgCopies:
    def __init__(self, buf, ranges, send_sems, recv_sems):
        self.buf, self.ranges, self.send_sems, self.recv_sems = buf, ranges, send_sems, recv_sems
        self.x, self.y, self.cc = _coords()
        self.p = 2 * self.x + self.y
        self.chips = _other_chips(self.x, self.y)

    def rows(self, chip, r, hf):
        start, n = self.ranges[r]
        return self.buf.at[chip, pl.ds(start + hf * (n // 2), n // 2), :]

    def _copy(self, r, k, chip, hf, to):
        return pltpu.make_async_remote_copy(
            src_ref=self.rows(chip, r, hf), dst_ref=self.rows(chip, r, hf), send_sem=self.send_sems.at[3 * r + k],
            recv_sem=self.recv_sems.at[3 * r + k], device_id=to, device_id_type=MESH)

    def pairs(self):
        return [(r, k) for r in range(len(self.ranges)) for k in range(3)]

    def ici(self, r, k):
        cx, cy, _ = self.chips[k]
        return self._copy(r, k, self.p, self.cc, (cx, cy, self.cc))

    def ici_arrival(self, r, k):
        return self._copy(r, k, self.chips[k][2], self.cc, (self.x, self.y, self.cc))

    def forward(self, r, k):
        return self._copy(r, k, self.chips[k][2], self.cc, (self.x, self.y, 1 - self.cc))

    def forward_arrival(self, r, k):
        return self._copy(r, k, self.chips[k][2], 1 - self.cc, (self.x, self.y, self.cc))


def _ag_weights(w4, ranges):
    n = 3 * len(ranges)

    def body(w_ref, out_ref, send1, recv1, send2, recv2):
        ici, fwd = _AgCopies(out_ref, ranges, send1, recv1), _AgCopies(out_ref, ranges, send2, recv2)
        for r, k in ici.pairs():
            ici.ici(r, k).start()
        for r, k in ici.pairs():
            ici.ici_arrival(r, k).wait_recv()
            fwd.forward(r, k).start()
        for r, k in ici.pairs():
            fwd.forward_arrival(r, k).wait_recv()
        for r, k in ici.pairs():
            ici.ici(r, k).wait_send()
            fwd.forward(r, k).wait_send()

    return pl.pallas_call(
        body, name="ag_weights", in_specs=[ANY], out_specs=ANY, out_shape=jax.ShapeDtypeStruct(w4.shape, w4.dtype),
        scratch_shapes=[pltpu.SemaphoreType.DMA((n,))] * 4, input_output_aliases={0: 0}, compiler_params=_params(),
    )(w4)


def _swap_copy(g_ref, out_ref, send_sem, recv_sem):
    x, y, cc = _coords()
    half = g_ref.shape[1] // 2
    return pltpu.make_async_remote_copy(
        src_ref=g_ref.at[:, pl.ds((1 - cc) * half, half), :], dst_ref=out_ref, send_sem=send_sem, recv_sem=recv_sem,
        device_id=(x, y, 1 - cc), device_id_type=MESH)


def _swap_halves(g, *, name):
    nb, r, c = g.shape
    half = r // 2

    def body(g_ref, out_ref, send_sem, recv_sem):
        cp = _swap_copy(g_ref, out_ref, send_sem, recv_sem)
        cp.start()
        cp.wait()

    return pl.pallas_call(
        body, name=name, in_specs=[ANY], out_specs=ANY, out_shape=jax.ShapeDtypeStruct((nb, half, c), g.dtype),
        scratch_shapes=[pltpu.SemaphoreType.DMA, pltpu.SemaphoreType.DMA], compiler_params=_params(),
    )(g)


def _my_half_index():
    return lax.axis_index("c").astype(jnp.int32).reshape(1)


def _add_halves(g, got, tag):
    nb, r, c = g.shape
    half = r // 2
    tr = _tile(half, 512, 16)
    nt_ = half // tr

    def body(c_ref, a_ref, b_ref, o_ref):
        o_ref[...] = (a_ref[...].astype(F32) + b_ref[...].astype(F32)).astype(BF16)

    return pl.pallas_call(
        body, name=f"rs_add_sibling{tag}",
        grid_spec=pltpu.PrefetchScalarGridSpec(
            num_scalar_prefetch=1, grid=(nb, nt_),
            in_specs=[pl.BlockSpec((1, tr, c), lambda b, i, cr: (b, cr[0] * nt_ + i, 0)),
                      pl.BlockSpec((1, tr, c), lambda b, i, cr: (b, i, 0))],
            out_specs=pl.BlockSpec((1, tr, c), lambda b, i, cr: (b, i, 0))),
        out_shape=jax.ShapeDtypeStruct((nb, half, c), BF16), compiler_params=_params(("parallel", "parallel")),
    )(_my_half_index(), g, got)


def _scatter_copies(s_ref, out_ref, send_sems, recv_sems):
    x, y, cc = _coords()
    sends = [pltpu.make_async_remote_copy(
        src_ref=s_ref.at[blk], dst_ref=out_ref.at[k], send_sem=send_sems.at[k], recv_sem=recv_sems.at[k],
        device_id=(cx, cy, cc), device_id_type=MESH) for k, (cx, cy, blk) in enumerate(_other_chips(x, y))]
    arrivals = [pltpu.make_async_remote_copy(
        src_ref=s_ref.at[2 * x + y], dst_ref=out_ref.at[k], send_sem=send_sems.at[k], recv_sem=recv_sems.at[k],
        device_id=(x, y, cc), device_id_type=MESH) for k in range(3)]
    return sends, arrivals


def _sum_chips(s, got, tag):
    nb, hrows, c = s.shape
    tr = _tile(hrows, 512, 16)

    def body(idx_ref, own_ref, got_ref, o_ref):
        p = idx_ref[0]
        own = own_ref[0].astype(F32)
        parts = [got_ref[k].astype(F32) for k in range(3)]
        acc = jnp.zeros_like(own)
        for q in range(4):
            val = own
            for k, rel in enumerate((2, 1, 3)):
                val = jnp.where((p ^ rel) == q, parts[k], val)
            acc = acc + val
        o_ref[...] = acc

    idx = (2 * lax.axis_index("x") + lax.axis_index("y")).astype(jnp.int32).reshape(1)
    return pl.pallas_call(
        body, name=f"rs_sum_chips{tag}",
        grid_spec=pltpu.PrefetchScalarGridSpec(
            num_scalar_prefetch=1, grid=(hrows // tr,),
            in_specs=[pl.BlockSpec((1, tr, c), lambda i, pr: (pr[0], i, 0)), pl.BlockSpec((3, tr, c), lambda i, pr: (0, i, 0))],
            out_specs=pl.BlockSpec((tr, c), lambda i, pr: (i, 0))),
        out_shape=jax.ShapeDtypeStruct((hrows, c), F32), compiler_params=_params(("parallel",)),
    )(idx, s, got)


def _swap_sibling(t, tag):
    def body(t_ref, out_ref, send_sem, recv_sem):
        x, y, cc = _coords()
        cp = pltpu.make_async_remote_copy(src_ref=t_ref, dst_ref=out_ref, send_sem=send_sem, recv_sem=recv_sem,
                                          device_id=(x, y, 1 - cc), device_id_type=MESH)
        cp.start()
        cp.wait()

    return pl.pallas_call(
        body, name=f"rs_join{tag}", in_specs=[ANY], out_specs=ANY, out_shape=jax.ShapeDtypeStruct(t.shape, t.dtype),
        scratch_shapes=[pltpu.SemaphoreType.DMA, pltpu.SemaphoreType.DMA], compiler_params=_params(),
    )(t)


def _rs_local(g, tag):
    return _add_halves(g, _swap_halves(g, name=f"rs_swap{tag}"), tag)


def _rs_finish(s, recv, tag):
    t = _sum_chips(s, recv, tag)
    r = _swap_sibling(t, tag)
    first = lax.axis_index("c") == 0
    return jnp.concatenate([jnp.where(first, t, r), jnp.where(first, r, t)], axis=0)


_SMALL_SHARDED = (("meta_tokens", 1), ("gla_w_alpha2", 2), ("gdn_conv_w", 3))
_REPLICATED = ("norm_mix", "norm_ffn", "fox_b_f", "fox_q_gain", "fox_k_gain", "gla_b_alpha", "gla_o_gain",
               "gdn_a_log", "gdn_dt_bias", "gdn_o_gain")
_WEIGHTS = ("meta_tokens", "norm_mix", "norm_ffn", "w_gate_up", "w_down", "fox_w_in", "fox_b_f", "fox_q_gain",
            "fox_k_gain", "fox_w_out", "gla_w_in", "gla_w_alpha2", "gla_b_alpha", "gla_o_gain", "gla_w_out",
            "gdn_w_in", "gdn_conv_w", "gdn_a_log", "gdn_dt_bias", "gdn_o_gain", "gdn_w_out")
_PACK_ROWS = 512
_IN_W = ("fox_w_in", "gla_w_in", "gdn_w_in")
_OUT_W = ("fox_w_out", "gla_w_out", "gdn_w_out")


def _piece_rows(n):
    return -(-n // 32) * 32


def _pack(arrays, width, row_mult, dtype):
    flat = jnp.concatenate([a.astype(dtype).reshape(-1) for a in arrays])
    per = width * row_mult
    n = -(-flat.shape[0] // per) * per
    return jnp.pad(flat, (0, n - flat.shape[0])).reshape(n // width, width)


def _unpack(flat, shapes):
    out, off = [], 0
    for s in shapes:
        n = 1
        for d in s:
            n *= d
        out.append(flat[off:off + n].reshape(s))
        off += n
    return out


def _unpack_cols(flat2, shapes):
    out, off = [], 0
    for s in shapes:
        n = 1
        for d in s:
            n *= d
        out.append(flat2[:, off:off + n].reshape((flat2.shape[0],) + tuple(s)))
        off += n
    return out


def kernel(x, meta_tokens, norm_mix, norm_ffn, w_gate_up, w_down, fox_w_in, fox_b_f, fox_q_gain, fox_k_gain, fox_w_out, gla_w_in, gla_w_alpha2, gla_b_alpha, gla_o_gain, gla_w_out, gdn_w_in, gdn_conv_w, gdn_a_log, gdn_dt_bias, gdn_o_gain, gdn_w_out, loss_target, m_meta_tokens, m_norm_mix, m_norm_ffn, m_w_gate_up, m_w_down, m_fox_w_in, m_fox_b_f, m_fox_q_gain, m_fox_k_gain, m_fox_w_out, m_gla_w_in, m_gla_w_alpha2, m_gla_b_alpha, m_gla_o_gain, m_gla_w_out, m_gdn_w_in, m_gdn_conv_w, m_gdn_a_log, m_gdn_dt_bias, m_gdn_o_gain, m_gdn_w_out, v_meta_tokens, v_norm_mix, v_norm_ffn, v_w_gate_up, v_w_down, v_fox_w_in, v_fox_b_f, v_fox_q_gain, v_fox_k_gain, v_fox_w_out, v_gla_w_in, v_gla_w_alpha2, v_gla_b_alpha, v_gla_o_gain, v_gla_w_out, v_gdn_w_in, v_gdn_conv_w, v_gdn_a_log, v_gdn_dt_bias, v_gdn_o_gain, v_gdn_w_out):
    W = dict(meta_tokens=meta_tokens, norm_mix=norm_mix, norm_ffn=norm_ffn, w_gate_up=w_gate_up, w_down=w_down,
             fox_w_in=fox_w_in, fox_b_f=fox_b_f, fox_q_gain=fox_q_gain, fox_k_gain=fox_k_gain, fox_w_out=fox_w_out,
             gla_w_in=gla_w_in, gla_w_alpha2=gla_w_alpha2, gla_b_alpha=gla_b_alpha, gla_o_gain=gla_o_gain,
             gla_w_out=gla_w_out, gdn_w_in=gdn_w_in, gdn_conv_w=gdn_conv_w, gdn_a_log=gdn_a_log,
             gdn_dt_bias=gdn_dt_bias, gdn_o_gain=gdn_o_gain, gdn_w_out=gdn_w_out)
    M = dict(meta_tokens=m_meta_tokens, norm_mix=m_norm_mix, norm_ffn=m_norm_ffn, w_gate_up=m_w_gate_up, w_down=m_w_down,
             fox_w_in=m_fox_w_in, fox_b_f=m_fox_b_f, fox_q_gain=m_fox_q_gain, fox_k_gain=m_fox_k_gain,
             fox_w_out=m_fox_w_out, gla_w_in=m_gla_w_in, gla_w_alpha2=m_gla_w_alpha2, gla_b_alpha=m_gla_b_alpha,
             gla_o_gain=m_gla_o_gain, gla_w_out=m_gla_w_out, gdn_w_in=m_gdn_w_in, gdn_conv_w=m_gdn_conv_w,
             gdn_a_log=m_gdn_a_log, gdn_dt_bias=m_gdn_dt_bias, gdn_o_gain=m_gdn_o_gain, gdn_w_out=m_gdn_w_out)
    V = dict(meta_tokens=v_meta_tokens, norm_mix=v_norm_mix, norm_ffn=v_norm_ffn, w_gate_up=v_w_gate_up, w_down=v_w_down,
             fox_w_in=v_fox_w_in, fox_b_f=v_fox_b_f, fox_q_gain=v_fox_q_gain, fox_k_gain=v_fox_k_gain,
             fox_w_out=v_fox_w_out, gla_w_in=v_gla_w_in, gla_w_alpha2=v_gla_w_alpha2, gla_b_alpha=v_gla_b_alpha,
             gla_o_gain=v_gla_o_gain, gla_w_out=v_gla_w_out, gdn_w_in=v_gdn_w_in, gdn_conv_w=v_gdn_conv_w,
             gdn_a_log=v_gdn_a_log, gdn_dt_bias=v_gdn_dt_bias, gdn_o_gain=v_gdn_o_gain, gdn_w_out=v_gdn_w_out)
    chip = 2 * lax.axis_index("x") + lax.axis_index("y")

    pieces, offs, r = [], {}, FFN_ROWS
    for n in _IN_W:
        nc = W[n].shape[2]
        for l in range(W[n].shape[0]):
            pieces.append(jnp.pad(W[n][l].T.astype(BF16), ((0, _piece_rows(nc) - nc), (0, 0))))
            offs[n, l] = r
            r += _piece_rows(nc)
    for n in _OUT_W:
        for l in range(W[n].shape[0]):
            pieces.append(W[n][l].astype(BF16))
            offs[n, l] = r
            r += W[n].shape[1]
    rows = -(-r // _PACK_ROWS) * _PACK_ROWS
    packed = jnp.concatenate([jnp.swapaxes(w_gate_up, 1, 2).reshape(-1, D).astype(BF16), w_down.reshape(-1, D).astype(BF16)]
                             + pieces + [jnp.zeros((rows - r, D), BF16)], axis=0)
    first_rows = [(offs["fox_w_in", 0], offs["fox_w_in", 1] - offs["fox_w_in", 0]),
                  (offs["fox_w_out", 0], offs["fox_w_out", 1] - offs["fox_w_out", 0])]
    later_rows = [(0, FFN_ROWS), (offs["fox_w_in", 1], offs["fox_w_out", 0] - offs["fox_w_in", 1]),
                  (offs["fox_w_out", 1], r - offs["fox_w_out", 1])]
    wpk = _ag_weights(lax.dynamic_update_slice(lax.empty((4, rows, D), BF16), packed[None], (chip, 0, 0)), first_rows)

    def in_t(buf, n, l, npad):
        nc = W[n].shape[2]
        return jnp.concatenate([buf[q, offs[n, l]:offs[n, l] + nc] for q in range(4)] + [jnp.zeros((npad - 4 * nc, D), BF16)], 0)

    def out_w(buf, n, l):
        return jnp.concatenate([buf[q, offs[n, l]:offs[n, l] + W[n].shape[1]] for q in range(4)], axis=0)

    fox_in0, fox_out0 = in_t(wpk, "fox_w_in", 0, FOX_INP), out_w(wpk, "fox_w_out", 0)
    full = {}
    small = _pack([W[n] for n, _ in _SMALL_SHARDED], LANES, 8, F32)
    small_all = _gather8(small, reduce=False, name="gather_small").reshape(8, -1)
    for (n, ax), seg in zip(_SMALL_SHARDED, _unpack_cols(small_all, [W[n].shape for n, _ in _SMALL_SHARDED])):
        full[n] = jnp.concatenate([seg[2 * q] for q in range(4)], axis=ax)
    fox_in, full["fox_w_out"] = [fox_in0], [fox_out0]
    w_alpha2, conv_w = full["gla_w_alpha2"][0], full["gdn_conv_w"][0]

    h = jnp.concatenate([jnp.zeros((META0, D), F32), full["meta_tokens"], x[0]], axis=0)
    saved = []
    y = _rms_fwd(h, norm_mix[0], name="norm_mix0")
    for i in range(DEPTH):
        kind, j = i % 3, i // 3
        if kind == 0:
            proj = _mm(y, fox_in[j], tb=True, name=f"fox_in{j}")
            qa, ka, va = _fox_prep(proj, fox_b_f[j], fox_q_gain[j], fox_k_gain[j])
            if i == 0:
                o, og, lse, wpk = _fox_attn_fwd(qa, ka, va, proj, ag=(wpk, later_rows))
                fox_in += [in_t(wpk, "fox_w_in", l, FOX_INP) for l in range(1, fox_w_in.shape[0])]
                full["fox_w_out"] += [out_w(wpk, "fox_w_out", l) for l in range(1, fox_w_out.shape[0])]
                gla_in = [in_t(wpk, "gla_w_in", l, GLA_INP) for l in range(gla_w_in.shape[0])]
                gdn_in = [in_t(wpk, "gdn_w_in", l, GDN_INP) for l in range(gdn_w_in.shape[0])]
                for n in ("gla_w_out", "gdn_w_out"):
                    full[n] = [out_w(wpk, n, l) for l in range(W[n].shape[0])]
            else:
                o, og, lse = _fox_attn_fwd(qa, ka, va, proj)
            w_out, mix = full["fox_w_out"][j], (proj, qa, ka, va, o, lse)
        elif kind == 1:
            proj = _mm(y, gla_in[j], tb=True, name=f"gla_in{j}")
            o, og, states = _gla_fwd(proj, w_alpha2, gla_b_alpha[j], gla_o_gain[j])
            w_out, mix = full["gla_w_out"][j], (proj, o, states)
        else:
            proj = _mm(y, gdn_in[j], tb=True, name=f"gdn_in{j}")
            o, og, states, tinvs = _gdn_fwd(proj, conv_w, gdn_a_log[j], gdn_dt_bias[j], gdn_o_gain[j])
            w_out, mix = full["gdn_w_out"][j], (proj, o, states, tinvs)
        hm, yf = _mm(og, w_out, add=h, norm=norm_ffn[i], name=f"mix_out{i}")
        gate, up, act = _ffn_up(yf, wpk, i)
        hn, y_next = _ffn_down(act, wpk, i, hm, norm_mix[(i + 1) % DEPTH])
        saved.append((h, y, mix, og, w_out, hm, yf, gate, up, act))
        h, y = hn, y_next
    dh, loss_tile = _loss_head(h, loss_target[0])

    G = {n: [None] * W[n].shape[0] for n in _WEIGHTS if n not in ("meta_tokens", "w_gate_up", "w_down") + _IN_W}
    GT = {}

    def grad_layout(ffn_layers, pieces):
        off, end = {}, 0
        for l in ffn_layers:
            off["gu", l] = end
            end += GU_ROWS
        for l in ffn_layers:
            off["down", l] = end
            end += DOWN_ROWS
        for n, l in pieces:
            off[n, l] = end
            end += _piece_rows(W[n].shape[2]) if n in _IN_W else W[n].shape[1]
        return off, end, -(-end // _PACK_ROWS) * _PACK_ROWS

    first_pieces = [("fox_w_in", 0)]
    later_pieces = [(n, l) for n in _IN_W + _OUT_W for l in range(W[n].shape[0]) if (n, l) not in first_pieces]
    layouts = [grad_layout([], first_pieces), grad_layout(list(range(DEPTH)), later_pieces)]
    gbuf = [jnp.zeros((4, lay[2], D), BF16) for lay in layouts]

    def with_pieces(buf, lay, pieces):
        off, end, total = lay
        blocks = []
        for q in range(4):
            parts = []
            for n, l in pieces:
                if n in _IN_W:
                    nc = W[n].shape[2]
                    parts.append(jnp.pad(GT[n, l][q * nc:(q + 1) * nc], ((0, _piece_rows(nc) - nc), (0, 0))))
                else:
                    nr = W[n].shape[1]
                    parts.append(G[n][l][q * nr:(q + 1) * nr])
            blocks.append(jnp.concatenate(parts + [jnp.zeros((total - end, D), BF16)], axis=0))
        return lax.dynamic_update_slice(buf, jnp.stack(blocks), (0, off[pieces[0]], 0))

    s_later = None
    for i in reversed(range(DEPTH)):
        kind, j = i % 3, i // 3
        h_in, y, mix, og, w_out, hm, yf, gate, up, act = saved[i]
        b = 1
        dg, du = _ffn_dact(dh, wpk, i, gate, up)
        gbuf[b] = _ffn_dw_down(act, dh, gbuf[b], i, layouts[b][0]["down", i])
        dhm, dnf = _ffn_dyf(dg, du, wpk, i, hm, norm_ffn[i], dh)
        gbuf[b] = _ffn_dw_gu(dg, du, yf, gbuf[b], i, layouts[b][0]["gu", i] // GU_ROWS)
        G["norm_ffn"][i] = dnf[0]
        dog = _mm(dhm, w_out, tb=True, name=f"d_og{i}")
        dw_out = _mm(og, dhm, ta=True, out_dtype=BF16, name=f"d_w_out{i}")
        if kind == 0:
            proj, qa, ka, va, o, lse = mix
            G["fox_w_out"][j] = dw_out
            if i == 0:
                g_later = with_pieces(gbuf[1], layouts[1], later_pieces)
                doa, q2, dgate, got = _fox_gate_bwd(dog, o, proj, lse, qa, swap=g_later)
                s_later = _add_halves(g_later, got, "_later")
                dqn, dkn, dv, dct, recv_later = _fox_attn_bwd(q2, ka, va, doa, rs=s_later)
            else:
                doa, q2, dgate = _fox_gate_bwd(dog, o, proj, lse, qa)
                dqn, dkn, dv, dct = _fox_attn_bwd(q2, ka, va, doa)
            dproj, dqg, dkg, dbf = _fox_prep_bwd(proj, fox_b_f[j], fox_q_gain[j], fox_k_gain[j], dqn, dkn, dv, dgate, dct)
            G["fox_q_gain"][j] = dqg.reshape(FOX_H, FOX_DH).sum(0)
            G["fox_k_gain"][j] = dkg.reshape(FOX_H, FOX_DH).sum(0)
            G["fox_b_f"][j] = dbf[0, :FOX_H]
            w_in, wname = fox_in[j], "fox_w_in"
        elif kind == 1:
            proj, o, states = mix
            dproj, dwa, dba, dogain = _gla_bwd(proj, w_alpha2, gla_b_alpha[j], gla_o_gain[j], o, states, dog)
            G["gla_w_out"][j] = dw_out
            G["gla_w_alpha2"][j] = dwa[:GLA_RANK]
            G["gla_b_alpha"][j] = dba[0]
            G["gla_o_gain"][j] = dogain[0]
            w_in, wname = gla_in[j], "gla_w_in"
        else:
            proj, o, states, tinvs = mix
            dproj, dcw, dal, ddt, dogain = _gdn_bwd(proj, conv_w, gdn_a_log[j], gdn_dt_bias[j], gdn_o_gain[j], o, states,
                                                    tinvs, dog)
            G["gdn_w_out"][j] = dw_out
            G["gdn_conv_w"][j] = dcw[:4].reshape(4, 1, GDN_CONV)
            G["gdn_a_log"][j] = dal[0, :GDN_H]
            G["gdn_dt_bias"][j] = ddt[0, :GDN_H]
            G["gdn_o_gain"][j] = dogain[0]
            w_in, wname = gdn_in[j], "gdn_w_in"
        GT[wname, j] = _mm(dproj, y, ta=True, out_dtype=BF16, name=f"d_w_in{i}")
        if i == 0:
            s_first = _rs_local(with_pieces(gbuf[0], layouts[0], first_pieces), "_first")
            dh, dnm, recv_first = _mm(dproj, w_in, rms_bwd=(h_in, norm_mix[i], dhm), rs=s_first, name=f"d_y{i}")
        else:
            dh, dnm = _mm(dproj, w_in, rms_bwd=(h_in, norm_mix[i], dhm), name=f"d_y{i}")
        G["norm_mix"][i] = dnm[0]
    grad_x = dh[ROW0:][None]
    G = {n: (v if n in _OUT_W else jnp.stack(v)) for n, v in G.items()}
    G["meta_tokens"] = dh[META0:ROW0]

    reduced = [_rs_finish(s_first, recv_first, "_first"), _rs_finish(s_later, recv_later, "_later")]

    def reduced_piece(n, l):
        b = 0 if (n, l) in first_pieces else 1
        start = layouts[b][0][n, l]
        return reduced[b][start:start + (W[n].shape[2] if n in _IN_W else W[n].shape[1])]

    grads = {}
    for n in _IN_W:
        grads[n] = jnp.stack([reduced_piece(n, l).T for l in range(W[n].shape[0])])
    for n in _OUT_W:
        grads[n] = jnp.stack([reduced_piece(n, l) for l in range(W[n].shape[0])])
    small_names = [n for n, _ in _SMALL_SHARDED] + list(_REPLICATED)
    small_g = _pack([G[n] for n in small_names] + [loss_tile[0, 0:1]], LANES, 8, F32)
    small_sum = _gather8(small_g, reduce=True, name="allreduce_small").reshape(-1)
    small_shapes = [G[n].shape for n in small_names] + [(1,)]
    small_vals = _unpack(small_sum, small_shapes)
    loss = small_vals[-1][0]
    for n, val in zip(small_names, small_vals[:-1]):
        grads[n] = val
    for n, ax in _SMALL_SHARDED:
        sz = W[n].shape[ax]
        grads[n] = lax.dynamic_slice_in_dim(grads[n], chip * sz, sz, axis=ax)

    delta, new_m, new_v = {}, {}, {}
    for n, key, tr_ in (("w_gate_up", "gu", True), ("w_down", "down", False)):
        grads[n], delta[n], new_m[n], new_v[n] = _adamw_packed(
            W[n], reduced[1], reduced[1], M[n], V[n], row0=layouts[1][0][key, 0], row_off=layouts[1][0][key, 1],
            transposed=tr_, name=f"adamw_{n}")
    for n in _IN_W + _OUT_W:
        delta[n], new_m[n], new_v[n] = _adamw(W[n], grads[n], M[n], V[n], name=f"adamw_{n}")
    tiny = [n for n in _WEIGHTS if n not in ("w_gate_up", "w_down") + _IN_W + _OUT_W]
    packs = [_pack([T[n] for n in tiny], LANES, 8, F32) for T in (W, grads, M, V)]
    outs = _adamw(*packs, name="adamw_small")
    shapes = [W[n].shape for n in tiny]
    for dst, o in zip((delta, new_m, new_v), outs):
        for n, val in zip(tiny, _unpack(o.reshape(-1), shapes)):
            dst[n] = val
    return (loss, grad_x, *[grads[n] for n in _WEIGHTS], *[delta[n] for n in _WEIGHTS],
            *[new_m[n] for n in _WEIGHTS], *[new_v[n] for n in _WEIGHTS])
```

```python
import jax
import jax.numpy as jnp
from jax import lax
from jax.experimental import pallas as pl
from jax.experimental.pallas import tpu as pltpu

F32, BF16 = jnp.float32, jnp.bfloat16
D = 1024
N_META = 16
ROW0 = 128
META0 = ROW0 - N_META
EPS = 1e-6
LANES = 128
VMEM_LIMIT = 56 * 1024 * 1024

FOX_H, FOX_DH = 16, 64
FOX_INP = 4224
GLA_H, GLA_DK, GLA_DV, GLA_RANK = 4, 128, 256, 16
GLA_QK, GLA_V = 512, 1024
GLA_INP = 3200
GLA_NORM = 16.0
GDN_H, GDN_DK, GDN_DV = 8, 128, 128
GDN_CONV = 3072
GDN_INP = 4224
CHUNK = 64
D_FF = 2816
DEPTH = 4

ADAM_LR, ADAM_B1, ADAM_B2, ADAM_EPS, ADAM_WD, ADAM_STEP = 0.001, 0.9, 0.999, 1e-08, 0.01, 10

MESH = pl.DeviceIdType.MESH
ANY = pl.BlockSpec(memory_space=pl.ANY)
VM = pl.BlockSpec(memory_space=pltpu.VMEM)


def _params(sem=None, **kw):
    if sem is not None:
        kw["dimension_semantics"] = sem
    return pltpu.CompilerParams(vmem_limit_bytes=VMEM_LIMIT, **kw)


def _tile(n, cap, mult=LANES):
    best = None
    for t in range(mult, min(n, cap) + 1, mult):
        if n % t == 0:
            best = t
    return best if best is not None else n


def nn(a, b, **kw):
    return jnp.dot(a, b, preferred_element_type=F32, **kw)


def nt(a, b, **kw):
    return lax.dot_general(a, b, (((1,), (1,)), ((), ())), preferred_element_type=F32, **kw)


def tn(a, b, **kw):
    return lax.dot_general(a, b, (((0,), (0,)), ((), ())), preferred_element_type=F32, **kw)


def _split3(x):
    hi = x.astype(BF16)
    r = x - hi.astype(F32)
    mid = r.astype(BF16)
    lo = (r - mid.astype(F32)).astype(BF16)
    return hi, mid, lo


def _sel_l(sel, x):
    a, b, c = _split3(x)
    return nn(sel, a) + nn(sel, b) + nn(sel, c)


def _sel_r(x, sel):
    a, b, c = _split3(x)
    return nn(a, sel) + nn(b, sel) + nn(c, sel)


def _sel_r2(x, sel):
    a = x.astype(BF16)
    return nn(a, sel) + nn((x - a.astype(F32)).astype(BF16), sel)


def _iota(shape, dim):
    return lax.broadcasted_iota(jnp.int32, shape, dim)


def _tri(n, upper=False, strict=False):
    i, j = _iota((n, n), 0), _iota((n, n), 1)
    if upper:
        m = (j > i) if strict else (j >= i)
    else:
        m = (j < i) if strict else (j <= i)
    return m


def _sigmoid(x):
    return 1.0 / (1.0 + jnp.exp(-x))


def _log_sigmoid(x):
    return jnp.minimum(x, 0.0) - jnp.log(1.0 + jnp.exp(-jnp.abs(x)))


def _softplus(x):
    return jnp.maximum(x, 0.0) + jnp.log(1.0 + jnp.exp(-jnp.abs(x)))


def _silu(x):
    return x * _sigmoid(x)


def _dsilu(x):
    s = _sigmoid(x)
    return s * (1.0 + x * (1.0 - s))


def _rms(x, g):
    return (x * lax.rsqrt(jnp.mean(x * x, axis=-1, keepdims=True) + EPS) * g).astype(BF16)


def _rms_grad(x, g, dy):
    r = lax.rsqrt(jnp.mean(x * x, axis=-1, keepdims=True) + EPS)
    u = dy * g
    return r * u - x * (r * r * r) * jnp.mean(x * u, axis=-1, keepdims=True), jnp.sum(dy * x * r, axis=0, keepdims=True)


def _mm(a, b, *, ta=False, tb=False, add=None, norm=None, rms_bwd=None, rs=None, out_dtype=F32, name):
    m, k = (a.shape[1], a.shape[0]) if ta else a.shape
    n = b.shape[0] if tb else b.shape[1]
    assert k == (b.shape[1] if tb else b.shape[0])
    rows_whole = norm is not None or rms_bwd is not None
    tm, tn_, tk = _tile(m, 704 if rows_whole else 1408, LANES if ta else 16), _tile(n, 1408), _tile(k, 1408)
    nk = k // tk
    assert not rows_whole or tn_ == n
    assert rs is None or rms_bwd is not None

    def body(*refs):
        refs = list(refs)
        if rs is not None:
            send_sems, recv_sems = refs[-2:]
            refs = refs[:-2]
            s_ref, got_ref = refs.pop(5), refs.pop(-2)
            sends, arrivals = _scatter_copies(s_ref, got_ref, send_sems, recv_sems)

            @pl.when((pl.program_id(0) == 0) & (pl.program_id(2) == 0))
            def _():
                for cp in sends:
                    cp.start()

            @pl.when((pl.program_id(0) == m // tm - 1) & (pl.program_id(2) == nk - 1))
            def _():
                for cp in arrivals:
                    cp.wait_recv()
                for cp in sends:
                    cp.wait_send()

        a_ref, b_ref = refs[:2]
        extra = refs[2:-1]
        acc = refs[-1]
        i, kk = pl.program_id(0), pl.program_id(2)

        @pl.when(kk == 0)
        def _():
            acc[...] = jnp.zeros_like(acc)

        av, bv = a_ref[...].astype(BF16), b_ref[...].astype(BF16)
        dims = (((0,) if ta else (1,), (1,) if tb else (0,)), ((), ()))
        acc[...] += lax.dot_general(av, bv, dims, preferred_element_type=F32)

        @pl.when(kk == nk - 1)
        def _():
            r = acc[...]
            if rms_bwd is not None:
                h_ref, g_ref, dres_ref, o_ref, dg_ref = extra
                dx, dgain = _rms_grad(h_ref[...], g_ref[...], r)
                o_ref[...] = dres_ref[...] + dx

                @pl.when(i == 0)
                def _():
                    dg_ref[...] = jnp.zeros_like(dg_ref)

                dg_ref[...] += dgain
                return
            if add is not None:
                r = r + extra[0][...].astype(F32)
            if norm is not None:
                g_ref, o_ref, y_ref = extra[-3:]
                y_ref[...] = _rms(r, g_ref[...])
            else:
                o_ref = extra[-1]
            o_ref[...] = r.astype(out_dtype)

    a_spec = pl.BlockSpec((tk, tm), lambda i, j, q: (q, i)) if ta else pl.BlockSpec((tm, tk), lambda i, j, q: (i, q))
    b_spec = pl.BlockSpec((tn_, tk), lambda i, j, q: (j, q)) if tb else pl.BlockSpec((tk, tn_), lambda i, j, q: (q, j))
    o_spec = pl.BlockSpec((tm, tn_), lambda i, j, q: (i, j))
    g_spec = pl.BlockSpec((1, n), lambda i, j, q: (0, 0))
    ins, specs = [a, b], [a_spec, b_spec]
    out_specs, out_shape = o_spec, jax.ShapeDtypeStruct((m, n), out_dtype)
    sem = ("parallel", "parallel", "arbitrary")
    scratch = [pltpu.VMEM((tm, tn_), F32)]
    if rms_bwd is not None:
        ins += [rms_bwd[0], rms_bwd[1].reshape(1, n), rms_bwd[2]]
        specs += [o_spec, g_spec, o_spec]
        out_specs, out_shape = [o_spec, g_spec], [jax.ShapeDtypeStruct((m, n), F32), jax.ShapeDtypeStruct((1, n), F32)]
        sem = ("arbitrary", "arbitrary", "arbitrary")
        if rs is not None:
            ins.append(rs)
            specs.append(ANY)
            out_specs.append(ANY)
            out_shape.append(jax.ShapeDtypeStruct((3,) + rs.shape[1:], rs.dtype))
            scratch += [pltpu.SemaphoreType.DMA((3,)), pltpu.SemaphoreType.DMA((3,))]
    else:
        if add is not None:
            ins.append(add)
            specs.append(o_spec)
        if norm is not None:
            ins.append(norm.reshape(1, n))
            specs.append(g_spec)
            out_specs, out_shape = [o_spec, o_spec], [out_shape, jax.ShapeDtypeStruct((m, n), BF16)]
    return pl.pallas_call(
        body, name=name, grid=(m // tm, n // tn_, nk), in_specs=specs, out_specs=out_specs, out_shape=out_shape,
        scratch_shapes=scratch, compiler_params=_params(sem),
    )(*ins)


def _rms_fwd(h, g, *, name):
    lp = h.shape[0]
    tr = _tile(lp, 512)

    def body(h_ref, g_ref, y_ref):
        x = h_ref[...]
        r = lax.rsqrt(jnp.mean(x * x, axis=-1, keepdims=True) + EPS)
        y_ref[...] = (x * r * g_ref[...]).astype(BF16)

    return pl.pallas_call(
        body, name=name, grid=(lp // tr,),
        in_specs=[pl.BlockSpec((tr, D), lambda i: (i, 0)), pl.BlockSpec((1, D), lambda i: (0, 0))],
        out_specs=pl.BlockSpec((tr, D), lambda i: (i, 0)),
        out_shape=jax.ShapeDtypeStruct((lp, D), BF16), compiler_params=_params(("parallel",)),
    )(h, g.reshape(1, D))


GU_ROWS, DOWN_ROWS = 1408, 704
OFF_GU, OFF_DOWN = 0, DEPTH * GU_ROWS
FFN_ROWS = DEPTH * (GU_ROWS + DOWN_ROWS)
FFN_TM = 704


def _gu_spec(fn):
    return pl.BlockSpec((None, GU_ROWS, D), fn)


def _down_spec(fn):
    return pl.BlockSpec((None, DOWN_ROWS, D), fn)


def _down_pair(w0_ref, w1_ref):
    return jnp.concatenate([w0_ref[...], w1_ref[...]], axis=0)


def _ffn_up(yf, wpk, layer):
    lp = yf.shape[0]
    tm = _tile(lp, FFN_TM, 16)

    def body(y_ref, wg_ref, wu_ref, g_ref, u_ref, a_ref):
        y = y_ref[...]
        g, u = nt(y, wg_ref[...]), nt(y, wu_ref[...])
        g_ref[...] = g.astype(BF16)
        u_ref[...] = u.astype(BF16)
        a_ref[...] = (_silu(g) * u).astype(BF16)

    o = pl.BlockSpec((tm, GU_ROWS), lambda i, j: (i, j))
    return pl.pallas_call(
        body, name=f"ffn_up{layer}", grid=(lp // tm, 2),
        in_specs=[pl.BlockSpec((tm, D), lambda i, j: (i, 0)), _gu_spec(lambda i, j: (j, OFF_GU // GU_ROWS + layer, 0)),
                  _gu_spec(lambda i, j: (2 + j, OFF_GU // GU_ROWS + layer, 0))],
        out_specs=[o, o, o], out_shape=[jax.ShapeDtypeStruct((lp, D_FF), BF16)] * 3,
        compiler_params=_params(("parallel", "parallel")),
    )(yf, wpk, wpk)


def _ffn_down(act, wpk, layer, res, norm):
    lp = act.shape[0]
    tm = _tile(lp, FFN_TM, 16)

    def body(a_ref, w0_ref, w1_ref, r_ref, g_ref, o_ref, y_ref, acc):
        kk = pl.program_id(1)

        @pl.when(kk == 0)
        def _():
            acc[...] = r_ref[...]

        acc[...] += nn(a_ref[...], _down_pair(w0_ref, w1_ref))

        @pl.when(kk == 1)
        def _():
            o_ref[...] = acc[...]
            y_ref[...] = _rms(acc[...], g_ref[...])

    o = pl.BlockSpec((tm, D), lambda i, kk: (i, 0))
    blk = OFF_DOWN // DOWN_ROWS + layer
    return pl.pallas_call(
        body, name=f"ffn_down{layer}", grid=(lp // tm, 2),
        in_specs=[pl.BlockSpec((tm, GU_ROWS), lambda i, kk: (i, kk)), _down_spec(lambda i, kk: (2 * kk, blk, 0)),
                  _down_spec(lambda i, kk: (2 * kk + 1, blk, 0)), o, pl.BlockSpec((1, D), lambda i, kk: (0, 0))],
        out_specs=[o, o], out_shape=[jax.ShapeDtypeStruct((lp, D), F32), jax.ShapeDtypeStruct((lp, D), BF16)],
        scratch_shapes=[pltpu.VMEM((tm, D), F32)], compiler_params=_params(("parallel", "arbitrary")),
    )(act, wpk, wpk, res, norm.reshape(1, D))


def _ffn_dact(dh, wpk, layer, gate, up):
    lp = dh.shape[0]
    tm = _tile(lp, FFN_TM, 16)

    def body(d_ref, w0_ref, w1_ref, g_ref, u_ref, dg_ref, du_ref):
        da = nt(d_ref[...].astype(BF16), _down_pair(w0_ref, w1_ref))
        g, u = g_ref[...].astype(F32), u_ref[...].astype(F32)
        sg = _sigmoid(g)
        dg_ref[...] = (da * u * (sg * (1.0 + g * (1.0 - sg)))).astype(BF16)
        du_ref[...] = (da * (g * sg)).astype(BF16)

    o = pl.BlockSpec((tm, GU_ROWS), lambda i, j: (i, j))
    blk = OFF_DOWN // DOWN_ROWS + layer
    return pl.pallas_call(
        body, name=f"d_act{layer}", grid=(lp // tm, 2),
        in_specs=[pl.BlockSpec((tm, D), lambda i, j: (i, 0)), _down_spec(lambda i, j: (2 * j, blk, 0)),
                  _down_spec(lambda i, j: (2 * j + 1, blk, 0)), o, o],
        out_specs=[o, o], out_shape=[jax.ShapeDtypeStruct((lp, D_FF), BF16)] * 2,
        compiler_params=_params(("parallel", "parallel")),
    )(dh, wpk, wpk, gate, up)


def _ffn_dyf(dg, du, wpk, layer, hm, norm, dres):
    lp = dg.shape[0]
    tm = _tile(lp, FFN_TM, 16)

    def body(dg_ref, du_ref, w_ref, h_ref, g_ref, dres_ref, o_ref, dgain_ref, acc):
        i, kk = pl.program_id(0), pl.program_id(1)

        @pl.when(kk == 0)
        def _():
            acc[...] = jnp.zeros_like(acc)

        @pl.when(kk < 2)
        def _():
            acc[...] += nn(dg_ref[...], w_ref[...])

        @pl.when(kk >= 2)
        def _():
            acc[...] += nn(du_ref[...], w_ref[...])

        @pl.when(kk == 3)
        def _():
            dx, dgain = _rms_grad(h_ref[...], g_ref[...], acc[...])
            o_ref[...] = dres_ref[...] + dx

            @pl.when(i == 0)
            def _():
                dgain_ref[...] = jnp.zeros_like(dgain_ref)

            dgain_ref[...] += dgain

    o = pl.BlockSpec((tm, D), lambda i, kk: (i, 0))
    row = pl.BlockSpec((1, D), lambda i, kk: (0, 0))
    return pl.pallas_call(
        body, name=f"d_yf{layer}", grid=(lp // tm, 4),
        in_specs=[pl.BlockSpec((tm, GU_ROWS), lambda i, kk: (i, jnp.minimum(kk, 1))),
                  pl.BlockSpec((tm, GU_ROWS), lambda i, kk: (i, jnp.maximum(kk - 2, 0))),
                  _gu_spec(lambda i, kk: (kk, OFF_GU // GU_ROWS + layer, 0)), o, row, o],
        out_specs=[o, row], out_shape=[jax.ShapeDtypeStruct((lp, D), F32), jax.ShapeDtypeStruct((1, D), F32)],
        scratch_shapes=[pltpu.VMEM((tm, D), F32)], compiler_params=_params(("arbitrary", "arbitrary")),
    )(dg, du, wpk, hm, norm.reshape(1, D), dres)


def _ffn_dw_down(act, dh, gpk, layer, row):
    lp = act.shape[0]
    tk = _tile(lp, 1408, 16)
    nk = lp // tk

    def body(a_ref, d_ref, g_in, g_out, acc, stage, sems):
        jp, kk = pl.program_id(0), pl.program_id(1)

        @pl.when(kk == 0)
        def _():
            acc[...] = jnp.zeros_like(acc)

        acc[...] += tn(a_ref[...], d_ref[...].astype(BF16))

        @pl.when(kk == nk - 1)
        def _():
            stage[...] = acc[...].astype(BF16)
            copies = [pltpu.make_async_copy(stage.at[pl.ds(hf * DOWN_ROWS, DOWN_ROWS), :],
                                            g_out.at[2 * jp + hf, pl.ds(row, DOWN_ROWS), :], sems.at[hf]) for hf in range(2)]
            for cp in copies:
                cp.start()
            for cp in copies:
                cp.wait()

    return pl.pallas_call(
        body, name=f"d_w_down{layer}", grid=(2, nk),
        in_specs=[pl.BlockSpec((tk, GU_ROWS), lambda jp, kk: (kk, jp)), pl.BlockSpec((tk, D), lambda jp, kk: (kk, 0)), ANY],
        out_specs=ANY, out_shape=jax.ShapeDtypeStruct(gpk.shape, gpk.dtype),
        scratch_shapes=[pltpu.VMEM((GU_ROWS, D), F32), pltpu.VMEM((GU_ROWS, D), BF16), pltpu.SemaphoreType.DMA((2,))],
        input_output_aliases={2: 0}, compiler_params=_params(("arbitrary", "arbitrary")),
    )(act, dh, gpk)


def _ffn_dw_gu(dg, du, yf, gpk, layer, blk):
    lp = dg.shape[0]
    tk = _tile(lp, 1408, 16)
    nk = lp // tk

    def body(dg_ref, du_ref, y_ref, g_in, o_ref, acc):
        c, kk = pl.program_id(0), pl.program_id(1)

        @pl.when(kk == 0)
        def _():
            acc[...] = jnp.zeros_like(acc)

        @pl.when(c < 2)
        def _():
            acc[...] += tn(dg_ref[...], y_ref[...])

        @pl.when(c >= 2)
        def _():
            acc[...] += tn(du_ref[...], y_ref[...])

        @pl.when(kk == nk - 1)
        def _():
            o_ref[...] = acc[...].astype(BF16)

    return pl.pallas_call(
        body, name=f"d_w_gate_up{layer}", grid=(4, nk),
        in_specs=[pl.BlockSpec((tk, GU_ROWS), lambda c, kk: (kk, jnp.minimum(c, 1))),
                  pl.BlockSpec((tk, GU_ROWS), lambda c, kk: (kk, jnp.maximum(c - 2, 0))),
                  pl.BlockSpec((tk, D), lambda c, kk: (kk, 0)), ANY],
        out_specs=_gu_spec(lambda c, kk: (c, blk, 0)),
        out_shape=jax.ShapeDtypeStruct(gpk.shape, gpk.dtype),
        scratch_shapes=[pltpu.VMEM((GU_ROWS, D), F32)], input_output_aliases={3: 0},
        compiler_params=_params(("parallel", "arbitrary")),
    )(dg, du, yf, gpk)


def _loss_head(h, target):
    lp = h.shape[0]
    nb = lp // ROW0

    def body(h_ref, t_ref, dh_ref, l_ref):
        i = pl.program_id(0)

        @pl.when(i == 0)
        def _():
            l_ref[...] = jnp.zeros_like(l_ref)
            dh_ref[...] = jnp.zeros_like(dh_ref)

        @pl.when(i > 0)
        def _():
            err = h_ref[...] - t_ref[...]
            dh_ref[...] = err * (1.0 / D)
            l_ref[...] += jnp.sum(err * err) * (0.5 / D)

    return pl.pallas_call(
        body, name="loss_head", grid=(nb,),
        in_specs=[pl.BlockSpec((ROW0, D), lambda i: (i, 0)), pl.BlockSpec((ROW0, D), lambda i: (jnp.maximum(i - 1, 0), 0))],
        out_specs=[pl.BlockSpec((ROW0, D), lambda i: (i, 0)), pl.BlockSpec((8, LANES), lambda i: (0, 0))],
        out_shape=[jax.ShapeDtypeStruct((lp, D), F32), jax.ShapeDtypeStruct((8, LANES), F32)],
        compiler_params=_params(("arbitrary",)),
    )(h, target)


def _adamw(w, g, m, v, *, name):
    if w.ndim == 2:
        w, g, m, v = (t[None] for t in (w, g, m, v))
        return tuple(o[0] for o in _adamw(w, g, m, v, name=name))
    nl, r, c = w.shape
    tr = _tile(r, max(8, (1 << 19) // c), 8)

    def body(w_ref, g_ref, m_ref, v_ref, d_ref, nm_ref, nv_ref):
        d_ref[...], nm_ref[...], nv_ref[...] = _adam_math(w_ref[...], g_ref[...], m_ref[...], v_ref[...])

    spec = pl.BlockSpec((1, tr, c), lambda l, i: (l, i, 0))
    return tuple(pl.pallas_call(
        body, name=name, grid=(nl, r // tr), in_specs=[spec] * 4, out_specs=[spec] * 3,
        out_shape=[jax.ShapeDtypeStruct(w.shape, F32)] * 3, compiler_params=_params(("parallel", "parallel")),
    )(w, g, m, v))


def _adam_math(w, g, m, v):
    nm = ADAM_B1 * m + (1.0 - ADAM_B1) * g
    nv = ADAM_B2 * v + (1.0 - ADAM_B2) * (g * g)
    m_hat = nm / (1.0 - ADAM_B1 ** ADAM_STEP)
    v_hat = nv / (1.0 - ADAM_B2 ** ADAM_STEP)
    return -ADAM_LR * (m_hat / (jnp.sqrt(v_hat) + ADAM_EPS) + ADAM_WD * w), nm, nv


def _adamw_packed(w, gred0, gred, m, v, *, row0, row_off, transposed, name):
    nl, a, b = w.shape
    nr = b if transposed else a
    later = lambda l: row_off // nr + jnp.maximum(l - 1, 0)
    if transposed:
        ta = _tile(a, 256)
        wspec = pl.BlockSpec((1, ta, b), lambda l, r: (l, r, 0))
        g0spec = pl.BlockSpec((b, ta), lambda l, r: (row0 // nr, r))
        gspec = pl.BlockSpec((b, ta), lambda l, r: (later(l), r))
        grid = (nl, a // ta)
    else:
        wspec = pl.BlockSpec((1, a, b), lambda l, r: (l, 0, 0))
        g0spec = pl.BlockSpec((a, b), lambda l, r: (row0 // nr, 0))
        gspec = pl.BlockSpec((a, b), lambda l, r: (later(l), 0))
        grid = (nl, 1)

    def body(w_ref, g0_ref, g_ref, m_ref, v_ref, go_ref, d_ref, nm_ref, nv_ref):
        g = jnp.where(pl.program_id(0) == 0, g0_ref[...], g_ref[...])
        g = g.T if transposed else g
        d, nm, nv = _adam_math(w_ref[0], g, m_ref[0], v_ref[0])
        go_ref[0], d_ref[0], nm_ref[0], nv_ref[0] = g, d, nm, nv

    return pl.pallas_call(
        body, name=name, grid=grid, in_specs=[wspec, g0spec, gspec, wspec, wspec], out_specs=[wspec] * 4,
        out_shape=[jax.ShapeDtypeStruct(w.shape, F32)] * 4, compiler_params=_params(("parallel", "parallel")),
    )(w, gred0, gred, m, v)


FOX_AUG = FOX_H * LANES
L_C = 64
L_K = 67
L_LSE = 70
PAD_KEY = -30000.0
FOX_TQ = 384


def _head_sel(n_heads, width, lanes=LANES):
    r, c = _iota((n_heads * width, lanes), 0), _iota((n_heads * width, lanes), 1)
    down = (r // width == c).astype(BF16)
    r2, c2 = _iota((lanes, n_heads * width), 0), _iota((lanes, n_heads * width), 1)
    up = (c2 // width == r2).astype(BF16)
    return down, up


def _place(lane0):
    r, c = _iota((LANES, FOX_AUG), 0), _iota((LANES, FOX_AUG), 1)
    return [((c // LANES == r) & (c % LANES == lane0 + m)).astype(BF16) for m in range(3)]


def _placed(x, lane0):
    pcs = _split3(x)
    mats = _place(lane0)
    return nn(pcs[0], mats[0]) + nn(pcs[1], mats[1]) + nn(pcs[2], mats[2])


def _ones_at(rows, lanes):
    c = _iota((rows, FOX_AUG), 1) % LANES
    m = c == lanes[0]
    for l in lanes[1:]:
        m = m | (c == l)
    return m.astype(F32)


def _spread(x, extras, out_ref):
    rows = x.shape[0]
    left = _iota((rows, LANES), 1) < FOX_DH
    for p in range(FOX_H // 2):
        slab = x[:, p * LANES:(p + 1) * LANES]
        a = jnp.where(left, slab, extras[:, 2 * p * LANES:(2 * p + 1) * LANES])
        b = jnp.where(left, pltpu.roll(slab, FOX_DH, 1), extras[:, (2 * p + 1) * LANES:(2 * p + 2) * LANES])
        out_ref[:, 2 * p * LANES:(2 * p + 1) * LANES] = a.astype(BF16)
        out_ref[:, (2 * p + 1) * LANES:(2 * p + 2) * LANES] = b.astype(BF16)


def _fox_prep(proj, b_f, q_gain, k_gain):
    lp = proj.shape[0]
    nb = lp // LANES

    def body(p_ref, bf_ref, qg_ref, kg_ref, q_ref, k_ref, v_ref, carry):
        i = pl.program_id(0)

        @pl.when(i == 0)
        def _():
            carry[...] = jnp.zeros_like(carry)

        down, up = _head_sel(FOX_H, FOX_DH)

        def normed(x, gain):
            ms = _sel_r2(x * x, down) * (1.0 / FOX_DH)
            r = _sel_r2(lax.rsqrt(ms + EPS), up)
            return x * r * gain

        lane = _iota((LANES, LANES), 1)
        lf = jnp.where(lane < FOX_H, _log_sigmoid(p_ref[:, 4 * D:4 * D + LANES] + bf_ref[...]), 0.0)
        c = _sel_l(_tri(LANES).astype(BF16), lf) + carry[0:1, :]
        carry[...] = jnp.broadcast_to(c[LANES - 1:LANES, :], carry.shape)
        q_extra = _placed(c, L_C) + _ones_at(LANES, (L_K, L_K + 1, L_K + 2))
        row = i * LANES + _iota((LANES, FOX_AUG), 0)
        lane_a = _iota((LANES, FOX_AUG), 1) % LANES
        k_extra = -_placed(c, L_K) + _ones_at(LANES, (L_C, L_C + 1, L_C + 2, L_LSE, L_LSE + 1, L_LSE + 2))
        pad_val = jnp.where(lane_a == L_K, PAD_KEY, 0.0)
        k_extra = jnp.where((row < META0) & (lane_a >= L_K) & (lane_a < L_K + 3), pad_val, k_extra)
        v_extra = _ones_at(LANES, (L_C, L_C + 1, L_C + 2))
        _spread(normed(p_ref[:, 0:D], qg_ref[...]) * (FOX_DH ** -0.5), q_extra, q_ref)
        _spread(normed(p_ref[:, D:2 * D], kg_ref[...]), k_extra, k_ref)
        _spread(p_ref[:, 2 * D:3 * D], v_extra, v_ref)

    row = pl.BlockSpec((1, D), lambda i: (0, 0))
    aug = pl.BlockSpec((LANES, FOX_AUG), lambda i: (i, 0))
    return pl.pallas_call(
        body, name="fox_prep", grid=(nb,),
        in_specs=[pl.BlockSpec((LANES, FOX_INP), lambda i: (i, 0)), pl.BlockSpec((1, LANES), lambda i: (0, 0)), row, row],
        out_specs=[aug] * 3, out_shape=[jax.ShapeDtypeStruct((lp, FOX_AUG), BF16)] * 3,
        scratch_shapes=[pltpu.VMEM((8, LANES), F32)],
        compiler_params=_params(("arbitrary",)),
    )(proj, jnp.pad(b_f, (0, LANES - FOX_H)).reshape(1, LANES), jnp.tile(q_gain, FOX_H).reshape(1, D),
      jnp.tile(k_gain, FOX_H).reshape(1, D))


def _fox_attn_fwd(qa, ka, va, proj, ag=None):
    lp = qa.shape[0]
    tq = _tile(lp, FOX_TQ)
    nq = lp // tq
    npair = FOX_H // 2

    def body(q_ref, k_ref, v_ref, gate_ref, *rest):
        if ag is None:
            o_ref, og_ref, lse_ref = rest
        else:
            _, o_ref, og_ref, lse_ref, w_out, send_sems, recv_sems, send2, recv2 = rest
            copies, fwd = _AgCopies(w_out, ag[1], send_sems, recv_sems), _AgCopies(w_out, ag[1], send2, recv2)

            @pl.when((pl.program_id(0) == 0) & (pl.program_id(1) == 0))
            def _():
                for r, k in copies.pairs():
                    copies.ici(r, k).start()

            @pl.when((pl.program_id(0) == npair - 1) & (pl.program_id(1) == 0))
            def _():
                for r, k in copies.pairs():
                    copies.ici_arrival(r, k).wait_recv()
                    fwd.forward(r, k).start()

        i = pl.program_id(1)
        causal = _iota((tq, tq), 1) <= _iota((tq, tq), 0)
        qs = [q_ref[:, hh * LANES:(hh + 1) * LANES] for hh in range(2)]

        def block(j, carry, diag):
            off = pl.multiple_of(j * tq, tq)
            out = []
            for hh in range(2):
                m, acc = carry[hh]
                k = k_ref[pl.ds(off, tq), hh * LANES:(hh + 1) * LANES]
                v = v_ref[pl.ds(off, tq), hh * LANES:(hh + 1) * LANES]
                s = nt(qs[hh], k)
                if diag:
                    s = jnp.where(causal, s, -1e30)
                m2 = jnp.maximum(m, jnp.max(s, axis=-1, keepdims=True))
                p = jnp.exp(s - m2)
                p_hi = p.astype(BF16)
                p_lo = (p - p_hi.astype(F32)).astype(BF16)
                out.append((m2, jnp.exp(m - m2) * acc + nn(p_hi, v) + nn(p_lo, v)))
            return tuple(out)

        init = tuple((jnp.full((tq, 1), -1e30, F32), jnp.zeros((tq, LANES), F32)) for _ in range(2))
        carry = lax.fori_loop(0, i // 2, lambda j, c: block(2 * j + 1, block(2 * j, c, False), False), init)
        carry = lax.cond(i % 2 == 1, lambda c: block(i - 1, c, False), lambda c: c, carry)
        carry = block(i, carry, True)
        outs, lses = [], []
        for hh in range(2):
            m, acc = carry[hh]
            l = acc[:, L_C:L_C + 1]
            outs.append(acc / l)
            lses.append(jnp.broadcast_to(m + jnp.log(l), (tq, LANES)))
        left = _iota((tq, LANES), 1) < FOX_DH
        o = jnp.where(left, outs[0], pltpu.roll(outs[1], FOX_DH, 1))
        o_ref[...] = o
        og_ref[...] = (o * _sigmoid(gate_ref[...])).astype(BF16)
        lse_ref[...] = jnp.where(left, lses[0], lses[1])

        if ag is not None:
            @pl.when((pl.program_id(0) == npair - 1) & (pl.program_id(1) == nq - 1))
            def _():
                for r, k in copies.pairs():
                    fwd.forward_arrival(r, k).wait_recv()
                for r, k in copies.pairs():
                    copies.ici(r, k).wait_send()
                    fwd.forward(r, k).wait_send()

    qspec = pl.BlockSpec((tq, 2 * LANES), lambda p, i: (i, p))
    kspec = pl.BlockSpec((lp, 2 * LANES), lambda p, i: (0, p))
    ospec = pl.BlockSpec((tq, LANES), lambda p, i: (i, p))
    ins, in_specs = [qa, ka, va, proj], [qspec, kspec, kspec, pl.BlockSpec((tq, LANES), lambda p, i: (i, 3 * D // LANES + p))]
    out_specs = [ospec] * 3
    out_shape = [jax.ShapeDtypeStruct((lp, D), F32), jax.ShapeDtypeStruct((lp, D), BF16), jax.ShapeDtypeStruct((lp, D), F32)]
    if ag is None:
        return pl.pallas_call(body, name="fox_attn_fwd", grid=(npair, nq), in_specs=in_specs, out_specs=out_specs,
                              out_shape=out_shape, compiler_params=_params(("parallel", "arbitrary")))(*ins)
    n = 3 * len(ag[1])
    return pl.pallas_call(
        body, name="fox_attn_fwd_ag", grid=(npair, nq), in_specs=in_specs + [ANY], out_specs=out_specs + [ANY],
        out_shape=out_shape + [jax.ShapeDtypeStruct(ag[0].shape, ag[0].dtype)],
        scratch_shapes=[pltpu.SemaphoreType.DMA((n,))] * 4, input_output_aliases={4: 3},
        compiler_params=_params(("arbitrary", "arbitrary")),
    )(*ins, ag[0])


def _fox_gate_bwd(dog, o, proj, lse, qa, swap=None):
    lp = o.shape[0]
    tr = LANES
    steps = lp // tr

    def body(d_ref, o_ref, g_ref, lse_ref, q_ref, *rest):
        if swap is None:
            do_ref, q2_ref, dgate_ref = rest
        else:
            src_ref, do_ref, q2_ref, dgate_ref, got_ref, send_sem, recv_sem = rest
            cp = _swap_copy(src_ref, got_ref, send_sem, recv_sem)

            @pl.when(pl.program_id(0) == 0)
            def _():
                cp.start()

            @pl.when(pl.program_id(0) == steps - 1)
            def _():
                cp.wait()

        down, _ = _head_sel(FOX_H, FOX_DH)
        sg = _sigmoid(g_ref[...])
        dv, ov = d_ref[...], o_ref[...]
        do = (dv * sg).astype(BF16).astype(F32)
        dgate_ref[...] = dv * ov * sg * (1.0 - sg)
        delta = _sel_r(do * ov, down)
        _spread(do, -_placed(delta, L_C), do_ref)
        r_, c_ = _iota((D, LANES), 0), _iota((D, LANES), 1)
        lse_c = _sel_r(lse_ref[...], (r_ == c_ * FOX_DH).astype(BF16))
        q2_ref[...] = (q_ref[...].astype(F32) - _placed(lse_c, L_LSE)).astype(BF16)

    spec = pl.BlockSpec((tr, D), lambda i: (i, 0))
    aug = pl.BlockSpec((tr, FOX_AUG), lambda i: (i, 0))
    in_specs = [spec, spec, pl.BlockSpec((tr, D), lambda i: (i, 3)), spec, aug]
    out_specs = [aug, aug, spec]
    out_shape = [jax.ShapeDtypeStruct((lp, FOX_AUG), BF16), jax.ShapeDtypeStruct((lp, FOX_AUG), BF16),
                 jax.ShapeDtypeStruct((lp, D), F32)]
    if swap is None:
        return pl.pallas_call(body, name="fox_gate_bwd", grid=(steps,), in_specs=in_specs, out_specs=out_specs,
                              out_shape=out_shape, compiler_params=_params(("parallel",)))(dog, o, proj, lse, qa)
    nb, r, c = swap.shape
    return pl.pallas_call(
        body, name="fox_gate_bwd_swap", grid=(steps,), in_specs=in_specs + [ANY], out_specs=out_specs + [ANY],
        out_shape=out_shape + [jax.ShapeDtypeStruct((nb, r // 2, c), swap.dtype)],
        scratch_shapes=[pltpu.SemaphoreType.DMA, pltpu.SemaphoreType.DMA], compiler_params=_params(("arbitrary",)),
    )(dog, o, proj, lse, qa, swap)


def _fox_attn_bwd(q2, ka, va, doa, rs=None):
    lp = q2.shape[0]
    t = _tile(lp, FOX_TQ)
    nb = lp // t
    npair = FOX_H // 2

    def body(q_ref, k_ref, v_ref, do_ref, *rest):
        if rs is None:
            dq_ref, dk_ref, dv_ref, dc_ref, dq_acc, dk_acc, dv_acc, dc_acc = rest
        else:
            s_ref, dq_ref, dk_ref, dv_ref, dc_ref, got_ref, dq_acc, dk_acc, dv_acc, dc_acc, send_sems, recv_sems = rest
            sends, arrivals = _scatter_copies(s_ref, got_ref, send_sems, recv_sems)

            @pl.when((pl.program_id(0) == 0) & (pl.program_id(1) == 0))
            def _():
                for cp in sends:
                    cp.start()

            @pl.when((pl.program_id(0) == npair - 1) & (pl.program_id(1) == nb - 1))
            def _():
                for cp in arrivals:
                    cp.wait_recv()
                for cp in sends:
                    cp.wait_send()

        j = pl.program_id(1)

        @pl.when(j == 0)
        def _():
            dq_acc[...] = jnp.zeros_like(dq_acc)

        causal = _iota((t, t), 1) <= _iota((t, t), 0)
        ks = [k_ref[:, hh * LANES:(hh + 1) * LANES] for hh in range(2)]
        vs = [v_ref[:, hh * LANES:(hh + 1) * LANES] for hh in range(2)]
        dk_acc[...] = jnp.zeros_like(dk_acc)
        dv_acc[...] = jnp.zeros_like(dv_acc)
        dc_acc[...] = jnp.zeros_like(dc_acc)

        def block(i, diag):
            off = pl.multiple_of(i * t, t)
            for hh in range(2):
                q = q_ref[pl.ds(off, t), hh * LANES:(hh + 1) * LANES]
                do = do_ref[pl.ds(off, t), hh * LANES:(hh + 1) * LANES]
                s = nt(q, ks[hh])
                if diag:
                    s = jnp.where(causal, s, -1e30)
                p = jnp.exp(s)
                ds = p * nt(do, vs[hh])
                dc_acc[hh] += jnp.sum(ds, axis=0, keepdims=True)
                dsb = ds.astype(BF16)
                dv_acc[hh] += tn(p.astype(BF16), do)
                dk_acc[hh] += tn(dsb, q)
                dq_acc[hh, pl.ds(off, t), :] += nn(dsb, ks[hh])

        block(j, True)
        below = nb - 1 - j

        def step(u, c):
            block(j + 1 + 2 * u, False)
            block(j + 2 + 2 * u, False)
            return c

        lax.fori_loop(0, below // 2, step, 0)

        @pl.when(below % 2 == 1)
        def _():
            block(nb - 1, False)
        left = _iota((t, LANES), 1) < FOX_DH
        dk_ref[...] = jnp.where(left, dk_acc[0], pltpu.roll(dk_acc[1], FOX_DH, 1))
        dv_ref[...] = jnp.where(left, dv_acc[0], pltpu.roll(dv_acc[1], FOX_DH, 1))
        for hh in range(2):
            dc_ref[hh] = jnp.broadcast_to(-dc_acc[hh], (8, t))

        @pl.when(j == nb - 1)
        def _():
            left = _iota((lp, LANES), 1) < FOX_DH
            dq_ref[...] = jnp.where(left, dq_acc[0], pltpu.roll(dq_acc[1], FOX_DH, 1))

    full = pl.BlockSpec((lp, 2 * LANES), lambda p, j: (0, p))
    kblk = pl.BlockSpec((t, 2 * LANES), lambda p, j: (j, p))
    oblk = pl.BlockSpec((t, LANES), lambda p, j: (j, p))
    in_specs = [full, kblk, kblk, full]
    out_specs = [pl.BlockSpec((lp, LANES), lambda p, j: (0, p)), oblk, oblk, pl.BlockSpec((2, 8, t), lambda p, j: (p, 0, j))]
    out_shape = [jax.ShapeDtypeStruct((lp, D), F32)] * 3 + [jax.ShapeDtypeStruct((FOX_H, 8, lp), F32)]
    scratch = [pltpu.VMEM((2, lp, LANES), F32), pltpu.VMEM((2, t, LANES), F32), pltpu.VMEM((2, t, LANES), F32),
               pltpu.VMEM((2, 1, t), F32)]
    if rs is None:
        return pl.pallas_call(body, name="fox_attn_bwd", grid=(npair, nb), in_specs=in_specs, out_specs=out_specs,
                              out_shape=out_shape, scratch_shapes=scratch,
                              compiler_params=_params(("parallel", "arbitrary")))(q2, ka, va, doa)
    return pl.pallas_call(
        body, name="fox_attn_bwd_rs", grid=(npair, nb), in_specs=in_specs + [ANY], out_specs=out_specs + [ANY],
        out_shape=out_shape + [jax.ShapeDtypeStruct((3,) + rs.shape[1:], rs.dtype)],
        scratch_shapes=scratch + [pltpu.SemaphoreType.DMA((3,)), pltpu.SemaphoreType.DMA((3,))],
        compiler_params=_params(("arbitrary", "arbitrary")),
    )(q2, ka, va, doa, rs)


def _fox_prep_bwd(proj, b_f, q_gain, k_gain, dqn, dkn, dv, dgate, dct, join=None):
    lp = proj.shape[0]
    nb = lp // LANES

    def body(p_ref, bf_ref, qg_ref, kg_ref, dq_ref, dk_ref, dv_ref, dg_ref, dc_ref, *rest):
        if join is None:
            dp_ref, dqg_ref, dkg_ref, dbf_ref, carry = rest
        else:
            t_ref, dp_ref, dqg_ref, dkg_ref, dbf_ref, got_ref, carry, send_sem, recv_sem = rest
            cp = _join_copy(t_ref, got_ref, send_sem, recv_sem)

            @pl.when(pl.program_id(0) == 0)
            def _():
                cp.start()

            @pl.when(pl.program_id(0) == nb - 1)
            def _():
                cp.wait()

        i = pl.program_id(0)

        @pl.when(i == 0)
        def _():
            carry[...] = jnp.zeros_like(carry)
            dqg_ref[...] = jnp.zeros_like(dqg_ref)
            dkg_ref[...] = jnp.zeros_like(dkg_ref)
            dbf_ref[...] = jnp.zeros_like(dbf_ref)

        down, up = _head_sel(FOX_H, FOX_DH)

        def norm_bwd(x, gain, dy, scale, dgain_ref):
            ms = _sel_r2(x * x, down) * (1.0 / FOX_DH)
            r = _sel_r2(lax.rsqrt(ms + EPS), up)
            u = dy * gain * scale
            mean_xu = _sel_r2(_sel_r2(x * u, down) * (1.0 / FOX_DH), up)
            dgain_ref[...] += jnp.sum(dy * scale * x * r, axis=0, keepdims=True)
            return r * u - x * (r * r * r) * mean_xu

        dp_ref[:, 0:D] = norm_bwd(p_ref[:, 0:D], qg_ref[...], dq_ref[...], FOX_DH ** -0.5, dqg_ref).astype(BF16)
        dp_ref[:, D:2 * D] = norm_bwd(p_ref[:, D:2 * D], kg_ref[...], dk_ref[...], 1.0, dkg_ref).astype(BF16)
        dp_ref[:, 2 * D:3 * D] = dv_ref[...].astype(BF16)
        dp_ref[:, 3 * D:4 * D] = dg_ref[...].astype(BF16)
        rows = jnp.concatenate([dc_ref[h, 0:1, :] for h in range(FOX_H)] + [jnp.zeros((LANES - FOX_H, LANES), F32)], axis=0)
        dlf = _sel_l(_tri(LANES, upper=True).astype(BF16), rows.T) + carry[0:1, :]
        carry[...] = jnp.broadcast_to(dlf[0:1, :], carry.shape)
        lane = _iota((LANES, LANES), 1)
        z = p_ref[:, 4 * D:4 * D + LANES] + bf_ref[...]
        df = jnp.where(lane < FOX_H, dlf * _sigmoid(-z), 0.0)
        dp_ref[:, 4 * D:4 * D + LANES] = df.astype(BF16)
        dbf_ref[...] += jnp.sum(df, axis=0, keepdims=True)

    rev = lambda i: (nb - 1 - i, 0)
    blk = pl.BlockSpec((LANES, D), rev)
    row = pl.BlockSpec((1, D), lambda i: (0, 0))
    row128 = pl.BlockSpec((1, LANES), lambda i: (0, 0))
    ins = [proj, jnp.pad(b_f, (0, LANES - FOX_H)).reshape(1, LANES), jnp.tile(q_gain, FOX_H).reshape(1, D),
           jnp.tile(k_gain, FOX_H).reshape(1, D), dqn, dkn, dv, dgate, dct]
    in_specs = [pl.BlockSpec((LANES, FOX_INP), rev), row128, row, row, blk, blk, blk, blk,
                pl.BlockSpec((FOX_H, 8, LANES), lambda i: (0, 0, nb - 1 - i))]
    out_specs = [pl.BlockSpec((LANES, FOX_INP), rev), row, row, row128]
    out_shape = [jax.ShapeDtypeStruct((lp, FOX_INP), BF16), jax.ShapeDtypeStruct((1, D), F32),
                 jax.ShapeDtypeStruct((1, D), F32), jax.ShapeDtypeStruct((1, LANES), F32)]
    scratch = [pltpu.VMEM((8, LANES), F32)]
    name = "fox_prep_bwd"
    if join is not None:
        ins, in_specs, out_specs = ins + [join], in_specs + [ANY], out_specs + [ANY]
        out_shape = out_shape + [jax.ShapeDtypeStruct(join.shape, join.dtype)]
        scratch = scratch + [pltpu.SemaphoreType.DMA, pltpu.SemaphoreType.DMA]
        name = "fox_prep_bwd_join"
    return pl.pallas_call(body, name=name, grid=(nb,), in_specs=in_specs, out_specs=out_specs, out_shape=out_shape,
                          scratch_shapes=scratch, compiler_params=_params(("arbitrary",)))(*ins)


def _gla_gates(p_ref, wa_ref, ba_ref):
    a_lr = p_ref[:, 3072:3072 + LANES]
    z = nn(a_lr.astype(BF16), wa_ref[...].astype(BF16)) + ba_ref[...]
    g = _log_sigmoid(z) * (1.0 / GLA_NORM)
    b = _sel_l(_tri(CHUNK).astype(BF16), g)
    return a_lr, z, b


def _gla_chunk_fwd(q, k, v, b, st0):
    hs = range(len(q))
    low = _tri(CHUNK)
    bl = [b[h][CHUNK - 1:CHUNK, :] for h in hs]
    qe = [q[h] * jnp.exp(b[h]) for h in hs]
    ke = [k[h] * jnp.exp(-b[h]) for h in hs]
    kd = [k[h] * jnp.exp(bl[h] - b[h]) for h in hs]
    a = [jnp.where(low, nt(qe[h], ke[h]), 0.0) for h in hs]
    o = [nn(a[h], v[h]) + nt(qe[h], st0[h]) for h in hs]
    st1 = [st0[h] * jnp.exp(bl[h]) + tn(v[h], kd[h]) for h in hs]
    return o, st1, (qe, ke, kd, a, bl)


def _gla_slices(p_ref, b_all, h):
    q = p_ref[:, h * GLA_DK:(h + 1) * GLA_DK] * (GLA_DK ** -0.5)
    k = p_ref[:, GLA_QK + h * GLA_DK:GLA_QK + (h + 1) * GLA_DK]
    v = p_ref[:, 2 * GLA_QK + h * GLA_DV:2 * GLA_QK + (h + 1) * GLA_DV]
    r = p_ref[:, 2 * GLA_QK + GLA_V + h * GLA_DV:2 * GLA_QK + GLA_V + (h + 1) * GLA_DV]
    return q, k, v, r, b_all[:, h * GLA_DK:(h + 1) * GLA_DK]


def _gla_fwd(proj, w_alpha2, b_alpha, o_gain):
    lp = proj.shape[0]
    nc = lp // CHUNK

    def body(p_ref, wa_ref, ba_ref, og_ref, o_ref, y_ref, s_ref, st):
        @pl.when(pl.program_id(0) == 0)
        def _():
            st[...] = jnp.zeros_like(st)

        _, _, b_all = _gla_gates(p_ref, wa_ref, ba_ref)
        hs = range(GLA_H)
        parts = [_gla_slices(p_ref, b_all, h) for h in hs]
        st0 = [st[h] for h in hs]
        for h in hs:
            s_ref[0, h] = st0[h]
        o, st1, _ = _gla_chunk_fwd([p[0] for p in parts], [p[1] for p in parts], [p[2] for p in parts],
                                   [p[4] for p in parts], st0)
        for h in hs:
            st[h] = st1[h]
            o_ref[:, h * GLA_DV:(h + 1) * GLA_DV] = o[h]
            rs = lax.rsqrt(jnp.mean(o[h] * o[h], axis=-1, keepdims=True) + EPS)
            y_ref[:, h * GLA_DV:(h + 1) * GLA_DV] = (o[h] * rs * og_ref[...] * _silu(parts[h][3])).astype(BF16)

    blk = pl.BlockSpec((CHUNK, D), lambda i: (i, 0))
    return pl.pallas_call(
        body, name="gla_fwd", grid=(nc,),
        in_specs=[pl.BlockSpec((CHUNK, GLA_INP), lambda i: (i, 0)), pl.BlockSpec((LANES, GLA_QK), lambda i: (0, 0)),
                  pl.BlockSpec((1, GLA_QK), lambda i: (0, 0)), pl.BlockSpec((1, GLA_DV), lambda i: (0, 0))],
        out_specs=[blk, blk, pl.BlockSpec((1, GLA_H, GLA_DV, GLA_DK), lambda i: (i, 0, 0, 0))],
        out_shape=[jax.ShapeDtypeStruct((lp, D), F32), jax.ShapeDtypeStruct((lp, D), BF16),
                   jax.ShapeDtypeStruct((nc, GLA_H, GLA_DV, GLA_DK), F32)],
        scratch_shapes=[pltpu.VMEM((GLA_H, GLA_DV, GLA_DK), F32)],
        compiler_params=_params(("arbitrary",)),
    )(proj, jnp.pad(w_alpha2, ((0, LANES - GLA_RANK), (0, 0))), b_alpha.reshape(1, GLA_QK), o_gain.reshape(1, GLA_DV))


def _gla_bwd(proj, w_alpha2, b_alpha, o_gain, o, states, dy):
    lp = proj.shape[0]
    nc = lp // CHUNK

    def body(p_ref, wa_ref, ba_ref, og_ref, o_ref, s_ref, dy_ref, dp_ref, dwa_ref, dba_ref, dog_ref, dst):
        @pl.when(pl.program_id(0) == 0)
        def _():
            dst[...] = jnp.zeros_like(dst)
            dwa_ref[...] = jnp.zeros_like(dwa_ref)
            dba_ref[...] = jnp.zeros_like(dba_ref)
            dog_ref[...] = jnp.zeros_like(dog_ref)

        a_lr, z, b_all = _gla_gates(p_ref, wa_ref, ba_ref)
        last_row = _iota((CHUNK, GLA_DK), 0) == CHUNK - 1
        rev = _tri(CHUNK, upper=True).astype(BF16)
        hs = range(GLA_H)
        scale = GLA_DK ** -0.5
        parts = [_gla_slices(p_ref, b_all, h) for h in hs]
        q, k, v, b = [p[0] for p in parts], [p[1] for p in parts], [p[2] for p in parts], [p[4] for p in parts]
        st0 = [s_ref[0, h] for h in hs]
        dst1 = [dst[h] for h in hs]
        do = []
        for h in hs:
            r = parts[h][3]
            ov = o_ref[:, h * GLA_DV:(h + 1) * GLA_DV]
            dyv = dy_ref[:, h * GLA_DV:(h + 1) * GLA_DV]
            rs = lax.rsqrt(jnp.mean(ov * ov, axis=-1, keepdims=True) + EPS)
            on = ov * rs
            dp_ref[:, 2 * GLA_QK + GLA_V + h * GLA_DV:2 * GLA_QK + GLA_V + (h + 1) * GLA_DV] = (
                dyv * on * og_ref[...] * _dsilu(r)).astype(BF16)
            don = dyv * _silu(r)
            dog_ref[...] += jnp.sum(don * on, axis=0, keepdims=True)
            u = don * og_ref[...]
            do.append(rs * u - ov * (rs * rs * rs) * jnp.mean(ov * u, axis=-1, keepdims=True))
        _, _, (qe, ke, kd, a, bl) = _gla_chunk_fwd(q, k, v, b, st0)
        low = _tri(CHUNK)
        da = [jnp.where(low, nt(do[h], v[h]), 0.0) for h in hs]
        dkd = [nn(v[h], dst1[h]) for h in hs]
        dvv = [tn(a[h], do[h]) + nt(kd[h], dst1[h]) for h in hs]
        dqe = [nn(da[h], ke[h]) + nn(do[h], st0[h]) for h in hs]
        dke = [tn(da[h], qe[h]) for h in hs]
        dg_parts = []
        for h in hs:
            ebl = jnp.exp(bl[h])
            dst[h] = dst1[h] * ebl + tn(do[h], qe[h])
            db = dqe[h] * qe[h] - dke[h] * ke[h] - dkd[h] * kd[h]
            db_last = (jnp.sum(dkd[h] * kd[h], axis=0, keepdims=True)
                       + jnp.sum(dst1[h] * st0[h], axis=0, keepdims=True) * ebl)
            db = db + jnp.where(last_row, db_last, 0.0)
            dg_parts.append(_sel_l(rev, db))
            dp_ref[:, h * GLA_DK:(h + 1) * GLA_DK] = (dqe[h] * jnp.exp(b[h]) * scale).astype(BF16)
            dp_ref[:, GLA_QK + h * GLA_DK:GLA_QK + (h + 1) * GLA_DK] = (
                dke[h] * jnp.exp(-b[h]) + dkd[h] * jnp.exp(bl[h] - b[h])).astype(BF16)
            dp_ref[:, 2 * GLA_QK + h * GLA_DV:2 * GLA_QK + (h + 1) * GLA_DV] = dvv[h].astype(BF16)
        dg = jnp.concatenate(dg_parts, axis=1)
        dz = dg * (1.0 / GLA_NORM) * _sigmoid(-z)
        dzb = dz.astype(BF16)
        dp_ref[:, 3072:3072 + LANES] = nt(dzb, wa_ref[...].astype(BF16)).astype(BF16)
        dwa_ref[...] += tn(a_lr.astype(BF16), dzb)
        dba_ref[...] += jnp.sum(dz, axis=0, keepdims=True)

    rv = lambda i: (nc - 1 - i, 0)
    blk = pl.BlockSpec((CHUNK, D), rv)
    fixed = lambda r, c: pl.BlockSpec((r, c), lambda i: (0, 0))
    return pl.pallas_call(
        body, name="gla_bwd", grid=(nc,),
        in_specs=[pl.BlockSpec((CHUNK, GLA_INP), rv), fixed(LANES, GLA_QK), fixed(1, GLA_QK), fixed(1, GLA_DV), blk,
                  pl.BlockSpec((1, GLA_H, GLA_DV, GLA_DK), lambda i: (nc - 1 - i, 0, 0, 0)), blk],
        out_specs=[pl.BlockSpec((CHUNK, GLA_INP), rv), fixed(LANES, GLA_QK), fixed(1, GLA_QK), fixed(1, GLA_DV)],
        out_shape=[jax.ShapeDtypeStruct((lp, GLA_INP), BF16), jax.ShapeDtypeStruct((LANES, GLA_QK), F32),
                   jax.ShapeDtypeStruct((1, GLA_QK), F32), jax.ShapeDtypeStruct((1, GLA_DV), F32)],
        scratch_shapes=[pltpu.VMEM((GLA_H, GLA_DV, GLA_DK), F32)],
        compiler_params=_params(("arbitrary",)),
    )(proj, jnp.pad(w_alpha2, ((0, LANES - GLA_RANK), (0, 0))), b_alpha.reshape(1, GLA_QK), o_gain.reshape(1, GLA_DV),
      o, states, dy)


HI = lax.Precision.HIGH


def _gdn_pre(prev_ref, p_ref, cw_ref, al_ref, dt_ref):
    xc = jnp.concatenate([prev_ref[:, 0:GDN_CONV], p_ref[:, 0:GDN_CONV]], axis=0)
    shifted = [pltpu.roll(xc, 3 - j, 0)[CHUNK:, :] if j < 3 else xc[CHUNK:, :] for j in range(4)]
    conv = sum(shifted[j] * cw_ref[j:j + 1, :] for j in range(4))
    act = _silu(conv)
    slab = p_ref[:, 4096:4096 + LANES]
    lane = _iota((CHUNK, LANES), 1)
    zs = slab + dt_ref[...]
    g = jnp.where(lane < GDN_H, -jnp.exp(al_ref[...]) * _softplus(zs), 0.0)
    bs = _sel_l(_tri(CHUNK).astype(BF16), g)
    beta = _sigmoid(slab)
    return shifted, conv, act, slab, zs, g, bs, beta


def _l2n(x):
    r = lax.rsqrt(jnp.sum(x * x, axis=-1, keepdims=True) + EPS)
    return x * r, r


def _gdn_chunk_fwd(q, k, v, beta, bcol, brow, s0, tinv=None):
    hs = range(len(q))
    ii, jj = _iota((CHUNK, CHUNK), 0), _iota((CHUNK, CHUNK), 1)
    low, eye = ii >= jj, (ii == jj).astype(F32)
    dm = [jnp.where(low, jnp.exp(jnp.where(low, bcol[h] - brow[h], 0.0)), 0.0) for h in hs]
    dstrict = [jnp.where(ii > jj, dm[h], 0.0) for h in hs]
    eb = [jnp.exp(bcol[h]) for h in hs]
    bl = [bcol[h][CHUNK - 1:CHUNK, :] for h in hs]
    kb = [k[h] * beta[h] for h in hs]
    vb = [v[h] * beta[h] for h in hs]
    nmat = [nt(kb[h], k[h]) * dstrict[h] for h in hs]
    if tinv is None:
        x = [eye - nmat[h] for h in hs]
        pw = [nn(nmat[h], nmat[h], precision=HI) for h in hs]
        for it in range(5):
            x = [x[h] + nn(x[h], pw[h], precision=HI) for h in hs]
            if it < 4:
                pw = [nn(pw[h], pw[h], precision=HI) for h in hs]
    else:
        x = tinv
    kbe = [kb[h] * eb[h] for h in hs]
    u = [nn(x[h], vb[h], precision=HI) for h in hs]
    w = [nn(x[h], kbe[h], precision=HI) for h in hs]
    vn = [u[h] - nn(w[h], s0[h]) for h in hs]
    pm = [nt(q[h], k[h]) * dm[h] for h in hs]
    qe = [q[h] * eb[h] for h in hs]
    o = [nn(pm[h], vn[h]) + nn(qe[h], s0[h]) for h in hs]
    kd = [k[h] * jnp.exp(bl[h] - bcol[h]) for h in hs]
    s1 = [s0[h] * jnp.exp(bl[h]) + tn(kd[h], vn[h]) for h in hs]
    return o, s1, dict(dm=dm, dstrict=dstrict, eb=eb, bl=bl, kb=kb, vb=vb, nmat=nmat, tinv=x, kbe=kbe, u=u, w=w, vn=vn,
                       pm=pm, qe=qe, kd=kd)


def _gdn_heads(act, beta_slab, bs, h):
    qa = act[:, h * GDN_DK:(h + 1) * GDN_DK]
    ka = act[:, GDN_H * GDN_DK + h * GDN_DK:GDN_H * GDN_DK + (h + 1) * GDN_DK]
    v = act[:, 2 * GDN_H * GDN_DK + h * GDN_DV:2 * GDN_H * GDN_DK + (h + 1) * GDN_DV]
    return qa, ka, v, beta_slab[:, GDN_H + h:GDN_H + h + 1], bs[:, h:h + 1]


def _gdn_fwd(proj, conv_w, a_log, dt_bias, o_gain):
    lp = proj.shape[0]
    nc = lp // CHUNK

    def body(prev_ref, p_ref, cw_ref, al_ref, dt_ref, og_ref, o_ref, y_ref, s_ref, t_ref, st):
        @pl.when(pl.program_id(0) == 0)
        def _():
            st[...] = jnp.zeros_like(st)

        _, _, act, _, _, _, bs, beta = _gdn_pre(prev_ref, p_ref, cw_ref, al_ref, dt_ref)
        bst = bs.T
        hs = range(GDN_H)
        parts = [_gdn_heads(act, beta, bs, h) for h in hs]
        q = [_l2n(parts[h][0])[0] * (GDN_DK ** -0.5) for h in hs]
        k = [_l2n(parts[h][1])[0] for h in hs]
        s0 = [st[h] for h in hs]
        for h in hs:
            s_ref[0, h] = s0[h]
        o, s1, f = _gdn_chunk_fwd(q, k, [parts[h][2] for h in hs], [parts[h][3] for h in hs], [parts[h][4] for h in hs],
                                  [bst[h:h + 1, :] for h in hs], s0)
        for h in hs:
            t_ref[0, h] = f["tinv"][h]
            st[h] = s1[h]
            o_ref[:, h * GDN_DV:(h + 1) * GDN_DV] = o[h]
            rs = lax.rsqrt(jnp.mean(o[h] * o[h], axis=-1, keepdims=True) + EPS)
            gate = p_ref[:, GDN_CONV + h * GDN_DV:GDN_CONV + (h + 1) * GDN_DV]
            y_ref[:, h * GDN_DV:(h + 1) * GDN_DV] = (o[h] * rs * og_ref[...] * _silu(gate)).astype(BF16)

    blk = pl.BlockSpec((CHUNK, D), lambda i: (i, 0))
    fixed = lambda r, c: pl.BlockSpec((r, c), lambda i: (0, 0))
    return pl.pallas_call(
        body, name="gdn_fwd", grid=(nc,),
        in_specs=[pl.BlockSpec((CHUNK, GDN_INP), lambda i: (jnp.maximum(i - 1, 0), 0)),
                  pl.BlockSpec((CHUNK, GDN_INP), lambda i: (i, 0)), fixed(8, GDN_CONV), fixed(1, LANES), fixed(1, LANES),
                  fixed(1, GDN_DV)],
        out_specs=[blk, blk, pl.BlockSpec((1, GDN_H, GDN_DK, GDN_DV), lambda i: (i, 0, 0, 0)),
                   pl.BlockSpec((1, GDN_H, CHUNK, CHUNK), lambda i: (i, 0, 0, 0))],
        out_shape=[jax.ShapeDtypeStruct((lp, D), F32), jax.ShapeDtypeStruct((lp, D), BF16),
                   jax.ShapeDtypeStruct((nc, GDN_H, GDN_DK, GDN_DV), F32), jax.ShapeDtypeStruct((nc, GDN_H, CHUNK, CHUNK), F32)],
        scratch_shapes=[pltpu.VMEM((GDN_H, GDN_DK, GDN_DV), F32)],
        compiler_params=_params(("arbitrary",)),
    )(proj, proj, jnp.pad(conv_w.reshape(4, GDN_CONV), ((0, 4), (0, 0))), jnp.pad(a_log, (0, LANES - GDN_H)).reshape(1, LANES),
      jnp.pad(dt_bias, (0, LANES - GDN_H)).reshape(1, LANES), o_gain.reshape(1, GDN_DV))


def _gdn_bwd(proj, conv_w, a_log, dt_bias, o_gain, o, states, tinvs, dy):
    lp = proj.shape[0]
    nc = lp // CHUNK

    def body(prev_ref, p_ref, cw_ref, al_ref, dt_ref, og_ref, o_ref, s_ref, t_ref, dy_ref,
             dp_ref, dcw_ref, dal_ref, ddt_ref, dog_ref, dst, dconv_next):
        @pl.when(pl.program_id(0) == 0)
        def _():
            dst[...] = jnp.zeros_like(dst)
            dconv_next[...] = jnp.zeros_like(dconv_next)
            dcw_ref[...] = jnp.zeros_like(dcw_ref)
            dal_ref[...] = jnp.zeros_like(dal_ref)
            ddt_ref[...] = jnp.zeros_like(ddt_ref)
            dog_ref[...] = jnp.zeros_like(dog_ref)

        shifted, conv, act, slab, zs, g, bs, beta = _gdn_pre(prev_ref, p_ref, cw_ref, al_ref, dt_ref)
        bst = bs.T
        lane = _iota((CHUNK, LANES), 1)
        ones = jnp.ones((CHUNK, LANES), F32)
        db_slab = jnp.zeros((CHUNK, LANES), F32)
        dbeta_slab = jnp.zeros((CHUNK, LANES), F32)
        last_row = _iota((CHUNK, 1), 0) == CHUNK - 1
        hs = range(GDN_H)
        scale = GDN_DK ** -0.5
        parts = [_gdn_heads(act, beta, bs, h) for h in hs]
        qa, ka, v = [parts[h][0] for h in hs], [parts[h][1] for h in hs], [parts[h][2] for h in hs]
        bet, bcol = [parts[h][3] for h in hs], [parts[h][4] for h in hs]
        qn_ = [_l2n(qa[h]) for h in hs]
        kn_ = [_l2n(ka[h]) for h in hs]
        q = [qn_[h][0] * scale for h in hs]
        k, rq, rk = [kn_[h][0] for h in hs], [qn_[h][1] for h in hs], [kn_[h][1] for h in hs]
        s0 = [s_ref[0, h] for h in hs]
        ds1 = [dst[h] for h in hs]
        do = []
        for h in hs:
            ov = o_ref[:, h * GDN_DV:(h + 1) * GDN_DV]
            dyv = dy_ref[:, h * GDN_DV:(h + 1) * GDN_DV]
            gate = p_ref[:, GDN_CONV + h * GDN_DV:GDN_CONV + (h + 1) * GDN_DV]
            rs = lax.rsqrt(jnp.mean(ov * ov, axis=-1, keepdims=True) + EPS)
            on = ov * rs
            dp_ref[:, GDN_CONV + h * GDN_DV:GDN_CONV + (h + 1) * GDN_DV] = (dyv * on * og_ref[...] * _dsilu(gate)).astype(BF16)
            don = dyv * _silu(gate)
            dog_ref[...] += jnp.sum(don * on, axis=0, keepdims=True)
            uu = don * og_ref[...]
            do.append(rs * uu - ov * (rs * rs * rs) * jnp.mean(ov * uu, axis=-1, keepdims=True))
        _, _, f = _gdn_chunk_fwd(q, k, v, bet, bcol, [bst[h:h + 1, :] for h in hs], s0, tinv=[t_ref[0, h] for h in hs])
        dm, dstrict, eb, bl, kb, nmat, tinv = f["dm"], f["dstrict"], f["eb"], f["bl"], f["kb"], f["nmat"], f["tinv"]
        kbe, u, w, vn, pm, qe, kd = f["kbe"], f["u"], f["w"], f["vn"], f["pm"], f["qe"], f["kd"]
        ebl = [jnp.exp(bl[h]) for h in hs]
        dvn = [tn(pm[h], do[h]) + nn(kd[h], ds1[h]) for h in hs]
        dpr = [nt(do[h], vn[h]) for h in hs]
        dqe = [nt(do[h], s0[h]) for h in hs]
        dkd = [nt(vn[h], ds1[h]) for h in hs]
        for h in hs:
            dst[h] = ds1[h] * ebl[h] + tn(qe[h], do[h]) - tn(w[h], dvn[h])
        du_ = [tn(tinv[h], dvn[h], precision=HI) for h in hs]
        dw_ = [tn(tinv[h], -nt(dvn[h], s0[h]), precision=HI) for h in hs]
        dn = [-(nt(du_[h], u[h]) + nt(dw_[h], w[h])) for h in hs]
        dqk = [dpr[h] * dm[h] for h in hs]
        dkk = [dn[h] * dstrict[h] for h in hs]
        gsum = [dpr[h] * pm[h] + dn[h] * nmat[h] for h in hs]
        dkb = [nn(dkk[h], k[h]) + dw_[h] * eb[h] for h in hs]
        dk = [tn(dkk[h], kb[h]) + tn(dqk[h], q[h]) + dkd[h] * jnp.exp(bl[h] - bcol[h]) + dkb[h] * bet[h] for h in hs]
        dq = [nn(dqk[h], k[h]) + dqe[h] * eb[h] for h in hs]
        colsum = [tn(gsum[h], ones, precision=HI)[:, 0:1] for h in hs]
        dact_q, dact_k, dact_v = [], [], []
        for h in hs:
            dbeta = jnp.sum(dkb[h] * k[h], axis=-1, keepdims=True) + jnp.sum(du_[h] * v[h], axis=-1, keepdims=True)
            skd = jnp.sum(dkd[h] * kd[h], axis=-1, keepdims=True)
            db = (jnp.sum(gsum[h], axis=-1, keepdims=True) - colsum[h] + jnp.sum(dqe[h] * qe[h], axis=-1, keepdims=True)
                  + jnp.sum(dw_[h] * kbe[h], axis=-1, keepdims=True) - skd)
            db_last = jnp.sum(skd, axis=0, keepdims=True) + jnp.sum(ds1[h] * s0[h]) * ebl[h]
            db = db + jnp.where(last_row, db_last, 0.0)
            db_slab = db_slab + jnp.where(lane == h, db, 0.0)
            dbeta_slab = dbeta_slab + jnp.where(lane == GDN_H + h, dbeta, 0.0)
            dqn = dq[h] * scale
            dact_q.append(rq[h] * dqn - qa[h] * (rq[h] * rq[h] * rq[h]) * jnp.sum(qa[h] * dqn, axis=-1, keepdims=True))
            dact_k.append(rk[h] * dk[h] - ka[h] * (rk[h] * rk[h] * rk[h]) * jnp.sum(ka[h] * dk[h], axis=-1, keepdims=True))
            dact_v.append(du_[h] * bet[h])
        dact = jnp.concatenate(dact_q + dact_k + dact_v, axis=1)
        dconv = dact * _dsilu(conv)
        for j in range(4):
            dcw_ref[j:j + 1, :] += jnp.sum(dconv * shifted[j], axis=0, keepdims=True)
        dcat = jnp.concatenate([dconv, dconv_next[...]], axis=0)
        dx = dconv * cw_ref[3:4, :]
        for j in range(3):
            dx = dx + pltpu.roll(dcat, 2 * CHUNK - (3 - j), 0)[:CHUNK, :] * cw_ref[j:j + 1, :]
        dconv_next[...] = dconv
        dp_ref[:, 0:GDN_CONV] = dx.astype(BF16)
        dg = _sel_l(_tri(CHUNK, upper=True).astype(BF16), db_slab)
        da = dg * (-jnp.exp(al_ref[...])) * _sigmoid(zs)
        da = jnp.where(lane < GDN_H, da, 0.0)
        dal_ref[...] += jnp.sum(dg * g, axis=0, keepdims=True)
        ddt_ref[...] += jnp.sum(da, axis=0, keepdims=True)
        dp_ref[:, 4096:4096 + LANES] = (da + dbeta_slab * beta * (1.0 - beta)).astype(BF16)

    rv = lambda i: (nc - 1 - i, 0)
    blk = pl.BlockSpec((CHUNK, D), rv)
    fixed = lambda r, c: pl.BlockSpec((r, c), lambda i: (0, 0))
    return pl.pallas_call(
        body, name="gdn_bwd", grid=(nc,),
        in_specs=[pl.BlockSpec((CHUNK, GDN_INP), lambda i: (jnp.maximum(nc - 2 - i, 0), 0)),
                  pl.BlockSpec((CHUNK, GDN_INP), rv), fixed(8, GDN_CONV), fixed(1, LANES), fixed(1, LANES), fixed(1, GDN_DV),
                  blk, pl.BlockSpec((1, GDN_H, GDN_DK, GDN_DV), lambda i: (nc - 1 - i, 0, 0, 0)),
                  pl.BlockSpec((1, GDN_H, CHUNK, CHUNK), lambda i: (nc - 1 - i, 0, 0, 0)), blk],
        out_specs=[pl.BlockSpec((CHUNK, GDN_INP), rv), fixed(8, GDN_CONV), fixed(1, LANES), fixed(1, LANES), fixed(1, GDN_DV)],
        out_shape=[jax.ShapeDtypeStruct((lp, GDN_INP), BF16), jax.ShapeDtypeStruct((8, GDN_CONV), F32),
                   jax.ShapeDtypeStruct((1, LANES), F32), jax.ShapeDtypeStruct((1, LANES), F32),
                   jax.ShapeDtypeStruct((1, GDN_DV), F32)],
        scratch_shapes=[pltpu.VMEM((GDN_H, GDN_DK, GDN_DV), F32), pltpu.VMEM((CHUNK, GDN_CONV), F32)],
        compiler_params=_params(("arbitrary",)),
    )(proj, proj, jnp.pad(conv_w.reshape(4, GDN_CONV), ((0, 4), (0, 0))), jnp.pad(a_log, (0, LANES - GDN_H)).reshape(1, LANES),
      jnp.pad(dt_bias, (0, LANES - GDN_H)).reshape(1, LANES), o_gain.reshape(1, GDN_DV), o, states, tinvs, dy)


def _coords():
    return lax.axis_index("x"), lax.axis_index("y"), lax.axis_index("c")


def _other_chips(x, y):
    return [(1 - x, y, 2 * (1 - x) + y), (x, 1 - y, 2 * x + 1 - y), (1 - x, 1 - y, 2 * (1 - x) + 1 - y)]


def _gather8(v, *, reduce, name):
    r, c = v.shape

    def body(v_ref, out_ref, *scratch):
        if reduce:
            buf, send_sems, recv_sems = scratch
        else:
            buf = out_ref
            send_sems, recv_sems = scratch
        x, y, cc = _coords()
        me = 4 * x + 2 * y + cc
        buf[me] = v_ref[...]
        copies = []
        for k in range(1, 8):
            px, py, pc = x ^ (k >> 2), y ^ ((k >> 1) & 1), cc ^ (k & 1)
            copies.append(pltpu.make_async_remote_copy(
                src_ref=v_ref, dst_ref=buf.at[me], send_sem=send_sems.at[k - 1], recv_sem=recv_sems.at[k - 1],
                device_id=(px, py, pc), device_id_type=MESH))
        for cp in copies:
            cp.start()
        for k in range(1, 8):
            peer = (x ^ (k >> 2)) * 4 + (y ^ ((k >> 1) & 1)) * 2 + (cc ^ (k & 1))
            pltpu.make_async_remote_copy(
                src_ref=v_ref, dst_ref=buf.at[peer], send_sem=send_sems.at[k - 1], recv_sem=recv_sems.at[k - 1],
                device_id=(x, y, cc), device_id_type=MESH).wait_recv()
        for cp in copies:
            cp.wait_send()
        if reduce:
            acc = buf[0]
            for d in range(1, 8):
                acc = acc + buf[d]
            out_ref[...] = acc

    scratch = [pltpu.SemaphoreType.DMA((7,)), pltpu.SemaphoreType.DMA((7,))]
    if reduce:
        scratch = [pltpu.VMEM((8, r, c), F32)] + scratch
    return pl.pallas_call(
        body, name=name, in_specs=[VM], out_specs=VM,
        out_shape=jax.ShapeDtypeStruct((r, c) if reduce else (8, r, c), F32),
        scratch_shapes=scratch, compiler_params=_params(),
    )(v)


class _AgCopies:
    def __init__(self, buf, ranges, send_sems, recv_sems):
        self.buf, self.ranges, self.send_sems, self.recv_sems = buf, ranges, send_sems, recv_sems
        self.x, self.y, self.cc = _coords()
        self.p = 2 * self.x + self.y
        self.chips = _other_chips(self.x, self.y)

    def rows(self, chip, r, hf):
        start, n = self.ranges[r]
        return self.buf.at[chip, pl.ds(start + hf * (n // 2), n // 2), :]

    def _copy(self, r, k, chip, hf, to):
        return pltpu.make_async_remote_copy(
            src_ref=self.rows(chip, r, hf), dst_ref=self.rows(chip, r, hf), send_sem=self.send_sems.at[3 * r + k],
            recv_sem=self.recv_sems.at[3 * r + k], device_id=to, device_id_type=MESH)

    def pairs(self):
        return [(r, k) for r in range(len(self.ranges)) for k in range(3)]

    def ici(self, r, k):
        cx, cy, _ = self.chips[k]
        return self._copy(r, k, self.p, self.cc, (cx, cy, self.cc))

    def ici_arrival(self, r, k):
        return self._copy(r, k, self.chips[k][2], self.cc, (self.x, self.y, self.cc))

    def forward(self, r, k):
        return self._copy(r, k, self.chips[k][2], self.cc, (self.x, self.y, 1 - self.cc))

    def forward_arrival(self, r, k):
        return self._copy(r, k, self.chips[k][2], 1 - self.cc, (self.x, self.y, self.cc))


def _ag_weights(w4, ranges):
    n = 3 * len(ranges)

    def body(w_ref, out_ref, send1, recv1, send2, recv2):
        ici, fwd = _AgCopies(out_ref, ranges, send1, recv1), _AgCopies(out_ref, ranges, send2, recv2)
        for r, k in ici.pairs():
            ici.ici(r, k).start()
        for r, k in ici.pairs():
            ici.ici_arrival(r, k).wait_recv()
            fwd.forward(r, k).start()
        for r, k in ici.pairs():
            fwd.forward_arrival(r, k).wait_recv()
        for r, k in ici.pairs():
            ici.ici(r, k).wait_send()
            fwd.forward(r, k).wait_send()

    return pl.pallas_call(
        body, name="ag_weights", in_specs=[ANY], out_specs=ANY, out_shape=jax.ShapeDtypeStruct(w4.shape, w4.dtype),
        scratch_shapes=[pltpu.SemaphoreType.DMA((n,))] * 4, input_output_aliases={0: 0}, compiler_params=_params(),
    )(w4)


def _swap_copy(g_ref, out_ref, send_sem, recv_sem):
    x, y, cc = _coords()
    half = g_ref.shape[1] // 2
    return pltpu.make_async_remote_copy(
        src_ref=g_ref.at[:, pl.ds((1 - cc) * half, half), :], dst_ref=out_ref, send_sem=send_sem, recv_sem=recv_sem,
        device_id=(x, y, 1 - cc), device_id_type=MESH)


def _swap_halves(g, *, name):
    nb, r, c = g.shape
    half = r // 2

    def body(g_ref, out_ref, send_sem, recv_sem):
        cp = _swap_copy(g_ref, out_ref, send_sem, recv_sem)
        cp.start()
        cp.wait()

    return pl.pallas_call(
        body, name=name, in_specs=[ANY], out_specs=ANY, out_shape=jax.ShapeDtypeStruct((nb, half, c), g.dtype),
        scratch_shapes=[pltpu.SemaphoreType.DMA, pltpu.SemaphoreType.DMA], compiler_params=_params(),
    )(g)


def _my_half_index():
    return lax.axis_index("c").astype(jnp.int32).reshape(1)


def _add_halves(g, got, tag):
    nb, r, c = g.shape
    half = r // 2
    tr = _tile(half, 512, 16)
    nt_ = half // tr

    def body(c_ref, a_ref, b_ref, o_ref):
        o_ref[...] = (a_ref[...].astype(F32) + b_ref[...].astype(F32)).astype(BF16)

    return pl.pallas_call(
        body, name=f"rs_add_sibling{tag}",
        grid_spec=pltpu.PrefetchScalarGridSpec(
            num_scalar_prefetch=1, grid=(nb, nt_),
            in_specs=[pl.BlockSpec((1, tr, c), lambda b, i, cr: (b, cr[0] * nt_ + i, 0)),
                      pl.BlockSpec((1, tr, c), lambda b, i, cr: (b, i, 0))],
            out_specs=pl.BlockSpec((1, tr, c), lambda b, i, cr: (b, i, 0))),
        out_shape=jax.ShapeDtypeStruct((nb, half, c), BF16), compiler_params=_params(("parallel", "parallel")),
    )(_my_half_index(), g, got)


def _scatter_copies(s_ref, out_ref, send_sems, recv_sems):
    x, y, cc = _coords()
    sends = [pltpu.make_async_remote_copy(
        src_ref=s_ref.at[blk], dst_ref=out_ref.at[k], send_sem=send_sems.at[k], recv_sem=recv_sems.at[k],
        device_id=(cx, cy, cc), device_id_type=MESH) for k, (cx, cy, blk) in enumerate(_other_chips(x, y))]
    arrivals = [pltpu.make_async_remote_copy(
        src_ref=s_ref.at[2 * x + y], dst_ref=out_ref.at[k], send_sem=send_sems.at[k], recv_sem=recv_sems.at[k],
        device_id=(x, y, cc), device_id_type=MESH) for k in range(3)]
    return sends, arrivals


def _sum_chips(s, got, tag):
    nb, hrows, c = s.shape
    tr = _tile(hrows, 512, 16)

    def body(idx_ref, own_ref, got_ref, o_ref):
        p = idx_ref[0]
        own = own_ref[0].astype(F32)
        parts = [got_ref[k].astype(F32) for k in range(3)]
        acc = jnp.zeros_like(own)
        for q in range(4):
            val = own
            for k, rel in enumerate((2, 1, 3)):
                val = jnp.where((p ^ rel) == q, parts[k], val)
            acc = acc + val
        o_ref[...] = acc

    idx = (2 * lax.axis_index("x") + lax.axis_index("y")).astype(jnp.int32).reshape(1)
    return pl.pallas_call(
        body, name=f"rs_sum_chips{tag}",
        grid_spec=pltpu.PrefetchScalarGridSpec(
            num_scalar_prefetch=1, grid=(hrows // tr,),
            in_specs=[pl.BlockSpec((1, tr, c), lambda i, pr: (pr[0], i, 0)), pl.BlockSpec((3, tr, c), lambda i, pr: (0, i, 0))],
            out_specs=pl.BlockSpec((tr, c), lambda i, pr: (i, 0))),
        out_shape=jax.ShapeDtypeStruct((hrows, c), F32), compiler_params=_params(("parallel",)),
    )(idx, s, got)


def _join_copy(t_ref, out_ref, send_sem, recv_sem):
    x, y, cc = _coords()
    return pltpu.make_async_remote_copy(src_ref=t_ref, dst_ref=out_ref, send_sem=send_sem, recv_sem=recv_sem,
                                        device_id=(x, y, 1 - cc), device_id_type=MESH)


def _swap_sibling(t, tag):
    def body(t_ref, out_ref, send_sem, recv_sem):
        cp = _join_copy(t_ref, out_ref, send_sem, recv_sem)
        cp.start()
        cp.wait()

    return pl.pallas_call(
        body, name=f"rs_join{tag}", in_specs=[ANY], out_specs=ANY, out_shape=jax.ShapeDtypeStruct(t.shape, t.dtype),
        scratch_shapes=[pltpu.SemaphoreType.DMA, pltpu.SemaphoreType.DMA], compiler_params=_params(),
    )(t)


def _rs_local(g, tag):
    return _add_halves(g, _swap_halves(g, name=f"rs_swap{tag}"), tag)


def _both_halves(t, r):
    first = lax.axis_index("c") == 0
    return jnp.concatenate([jnp.where(first, t, r), jnp.where(first, r, t)], axis=0)


def _rs_finish(s, recv, tag):
    t = _sum_chips(s, recv, tag)
    return _both_halves(t, _swap_sibling(t, tag))


_SMALL_SHARDED = (("meta_tokens", 1), ("gla_w_alpha2", 2), ("gdn_conv_w", 3))
_REPLICATED = ("norm_mix", "norm_ffn", "fox_b_f", "fox_q_gain", "fox_k_gain", "gla_b_alpha", "gla_o_gain",
               "gdn_a_log", "gdn_dt_bias", "gdn_o_gain")
_WEIGHTS = ("meta_tokens", "norm_mix", "norm_ffn", "w_gate_up", "w_down", "fox_w_in", "fox_b_f", "fox_q_gain",
            "fox_k_gain", "fox_w_out", "gla_w_in", "gla_w_alpha2", "gla_b_alpha", "gla_o_gain", "gla_w_out",
            "gdn_w_in", "gdn_conv_w", "gdn_a_log", "gdn_dt_bias", "gdn_o_gain", "gdn_w_out")
_PACK_ROWS = 512
_IN_W = ("fox_w_in", "gla_w_in", "gdn_w_in")
_OUT_W = ("fox_w_out", "gla_w_out", "gdn_w_out")


def _piece_rows(n):
    return -(-n // 32) * 32


def _pack(arrays, width, row_mult, dtype):
    flat = jnp.concatenate([a.astype(dtype).reshape(-1) for a in arrays])
    per = width * row_mult
    n = -(-flat.shape[0] // per) * per
    return jnp.pad(flat, (0, n - flat.shape[0])).reshape(n // width, width)


def _unpack(flat, shapes):
    out, off = [], 0
    for s in shapes:
        n = 1
        for d in s:
            n *= d
        out.append(flat[off:off + n].reshape(s))
        off += n
    return out


def _unpack_cols(flat2, shapes):
    out, off = [], 0
    for s in shapes:
        n = 1
        for d in s:
            n *= d
        out.append(flat2[:, off:off + n].reshape((flat2.shape[0],) + tuple(s)))
        off += n
    return out


def kernel(x, meta_tokens, norm_mix, norm_ffn, w_gate_up, w_down, fox_w_in, fox_b_f, fox_q_gain, fox_k_gain, fox_w_out, gla_w_in, gla_w_alpha2, gla_b_alpha, gla_o_gain, gla_w_out, gdn_w_in, gdn_conv_w, gdn_a_log, gdn_dt_bias, gdn_o_gain, gdn_w_out, loss_target, m_meta_tokens, m_norm_mix, m_norm_ffn, m_w_gate_up, m_w_down, m_fox_w_in, m_fox_b_f, m_fox_q_gain, m_fox_k_gain, m_fox_w_out, m_gla_w_in, m_gla_w_alpha2, m_gla_b_alpha, m_gla_o_gain, m_gla_w_out, m_gdn_w_in, m_gdn_conv_w, m_gdn_a_log, m_gdn_dt_bias, m_gdn_o_gain, m_gdn_w_out, v_meta_tokens, v_norm_mix, v_norm_ffn, v_w_gate_up, v_w_down, v_fox_w_in, v_fox_b_f, v_fox_q_gain, v_fox_k_gain, v_fox_w_out, v_gla_w_in, v_gla_w_alpha2, v_gla_b_alpha, v_gla_o_gain, v_gla_w_out, v_gdn_w_in, v_gdn_conv_w, v_gdn_a_log, v_gdn_dt_bias, v_gdn_o_gain, v_gdn_w_out):
    W = dict(meta_tokens=meta_tokens, norm_mix=norm_mix, norm_ffn=norm_ffn, w_gate_up=w_gate_up, w_down=w_down,
             fox_w_in=fox_w_in, fox_b_f=fox_b_f, fox_q_gain=fox_q_gain, fox_k_gain=fox_k_gain, fox_w_out=fox_w_out,
             gla_w_in=gla_w_in, gla_w_alpha2=gla_w_alpha2, gla_b_alpha=gla_b_alpha, gla_o_gain=gla_o_gain,
             gla_w_out=gla_w_out, gdn_w_in=gdn_w_in, gdn_conv_w=gdn_conv_w, gdn_a_log=gdn_a_log,
             gdn_dt_bias=gdn_dt_bias, gdn_o_gain=gdn_o_gain, gdn_w_out=gdn_w_out)
    M = dict(meta_tokens=m_meta_tokens, norm_mix=m_norm_mix, norm_ffn=m_norm_ffn, w_gate_up=m_w_gate_up, w_down=m_w_down,
             fox_w_in=m_fox_w_in, fox_b_f=m_fox_b_f, fox_q_gain=m_fox_q_gain, fox_k_gain=m_fox_k_gain,
             fox_w_out=m_fox_w_out, gla_w_in=m_gla_w_in, gla_w_alpha2=m_gla_w_alpha2, gla_b_alpha=m_gla_b_alpha,
             gla_o_gain=m_gla_o_gain, gla_w_out=m_gla_w_out, gdn_w_in=m_gdn_w_in, gdn_conv_w=m_gdn_conv_w,
             gdn_a_log=m_gdn_a_log, gdn_dt_bias=m_gdn_dt_bias, gdn_o_gain=m_gdn_o_gain, gdn_w_out=m_gdn_w_out)
    V = dict(meta_tokens=v_meta_tokens, norm_mix=v_norm_mix, norm_ffn=v_norm_ffn, w_gate_up=v_w_gate_up, w_down=v_w_down,
             fox_w_in=v_fox_w_in, fox_b_f=v_fox_b_f, fox_q_gain=v_fox_q_gain, fox_k_gain=v_fox_k_gain,
             fox_w_out=v_fox_w_out, gla_w_in=v_gla_w_in, gla_w_alpha2=v_gla_w_alpha2, gla_b_alpha=v_gla_b_alpha,
             gla_o_gain=v_gla_o_gain, gla_w_out=v_gla_w_out, gdn_w_in=v_gdn_w_in, gdn_conv_w=v_gdn_conv_w,
             gdn_a_log=v_gdn_a_log, gdn_dt_bias=v_gdn_dt_bias, gdn_o_gain=v_gdn_o_gain, gdn_w_out=v_gdn_w_out)
    chip = 2 * lax.axis_index("x") + lax.axis_index("y")

    pieces, offs, r = [], {}, FFN_ROWS
    for n in _IN_W:
        nc = W[n].shape[2]
        for l in range(W[n].shape[0]):
            pieces.append(jnp.pad(W[n][l].T.astype(BF16), ((0, _piece_rows(nc) - nc), (0, 0))))
            offs[n, l] = r
            r += _piece_rows(nc)
    for n in _OUT_W:
        for l in range(W[n].shape[0]):
            pieces.append(W[n][l].astype(BF16))
            offs[n, l] = r
            r += W[n].shape[1]
    rows = -(-r // _PACK_ROWS) * _PACK_ROWS
    packed = jnp.concatenate([jnp.swapaxes(w_gate_up, 1, 2).reshape(-1, D).astype(BF16), w_down.reshape(-1, D).astype(BF16)]
                             + pieces + [jnp.zeros((rows - r, D), BF16)], axis=0)
    first_rows = [(offs["fox_w_in", 0], offs["fox_w_in", 1] - offs["fox_w_in", 0]),
                  (offs["fox_w_out", 0], offs["fox_w_out", 1] - offs["fox_w_out", 0])]
    later_rows = [(0, FFN_ROWS), (offs["fox_w_in", 1], offs["fox_w_out", 0] - offs["fox_w_in", 1]),
                  (offs["fox_w_out", 1], r - offs["fox_w_out", 1])]
    wpk = _ag_weights(lax.dynamic_update_slice(lax.empty((4, rows, D), BF16), packed[None], (chip, 0, 0)), first_rows)

    def in_t(buf, n, l, npad):
        nc = W[n].shape[2]
        return jnp.concatenate([buf[q, offs[n, l]:offs[n, l] + nc] for q in range(4)] + [jnp.zeros((npad - 4 * nc, D), BF16)], 0)

    def out_w(buf, n, l):
        return jnp.concatenate([buf[q, offs[n, l]:offs[n, l] + W[n].shape[1]] for q in range(4)], axis=0)

    fox_in0, fox_out0 = in_t(wpk, "fox_w_in", 0, FOX_INP), out_w(wpk, "fox_w_out", 0)
    full = {}
    small = _pack([W[n] for n, _ in _SMALL_SHARDED], LANES, 8, F32)
    small_all = _gather8(small, reduce=False, name="gather_small").reshape(8, -1)
    for (n, ax), seg in zip(_SMALL_SHARDED, _unpack_cols(small_all, [W[n].shape for n, _ in _SMALL_SHARDED])):
        full[n] = jnp.concatenate([seg[2 * q] for q in range(4)], axis=ax)
    fox_in, full["fox_w_out"] = [fox_in0], [fox_out0]
    w_alpha2, conv_w = full["gla_w_alpha2"][0], full["gdn_conv_w"][0]

    h = jnp.concatenate([jnp.zeros((META0, D), F32), full["meta_tokens"], x[0]], axis=0)
    saved = []
    y = _rms_fwd(h, norm_mix[0], name="norm_mix0")
    for i in range(DEPTH):
        kind, j = i % 3, i // 3
        if kind == 0:
            proj = _mm(y, fox_in[j], tb=True, name=f"fox_in{j}")
            qa, ka, va = _fox_prep(proj, fox_b_f[j], fox_q_gain[j], fox_k_gain[j])
            if i == 0:
                o, og, lse, wpk = _fox_attn_fwd(qa, ka, va, proj, ag=(wpk, later_rows))
                fox_in += [in_t(wpk, "fox_w_in", l, FOX_INP) for l in range(1, fox_w_in.shape[0])]
                full["fox_w_out"] += [out_w(wpk, "fox_w_out", l) for l in range(1, fox_w_out.shape[0])]
                gla_in = [in_t(wpk, "gla_w_in", l, GLA_INP) for l in range(gla_w_in.shape[0])]
                gdn_in = [in_t(wpk, "gdn_w_in", l, GDN_INP) for l in range(gdn_w_in.shape[0])]
                for n in ("gla_w_out", "gdn_w_out"):
                    full[n] = [out_w(wpk, n, l) for l in range(W[n].shape[0])]
            else:
                o, og, lse = _fox_attn_fwd(qa, ka, va, proj)
            w_out, mix = full["fox_w_out"][j], (proj, qa, ka, va, o, lse)
        elif kind == 1:
            proj = _mm(y, gla_in[j], tb=True, name=f"gla_in{j}")
            o, og, states = _gla_fwd(proj, w_alpha2, gla_b_alpha[j], gla_o_gain[j])
            w_out, mix = full["gla_w_out"][j], (proj, o, states)
        else:
            proj = _mm(y, gdn_in[j], tb=True, name=f"gdn_in{j}")
            o, og, states, tinvs = _gdn_fwd(proj, conv_w, gdn_a_log[j], gdn_dt_bias[j], gdn_o_gain[j])
            w_out, mix = full["gdn_w_out"][j], (proj, o, states, tinvs)
        hm, yf = _mm(og, w_out, add=h, norm=norm_ffn[i], name=f"mix_out{i}")
        gate, up, act = _ffn_up(yf, wpk, i)
        hn, y_next = _ffn_down(act, wpk, i, hm, norm_mix[(i + 1) % DEPTH])
        saved.append((h, y, mix, og, w_out, hm, yf, gate, up, act))
        h, y = hn, y_next
    dh, loss_tile = _loss_head(h, loss_target[0])

    G = {n: [None] * W[n].shape[0] for n in _WEIGHTS if n not in ("meta_tokens", "w_gate_up", "w_down") + _IN_W}
    GT = {}

    def grad_layout(ffn_layers, pieces):
        off, end = {}, 0
        for l in ffn_layers:
            off["gu", l] = end
            end += GU_ROWS
        for l in ffn_layers:
            off["down", l] = end
            end += DOWN_ROWS
        for n, l in pieces:
            off[n, l] = end
            end += _piece_rows(W[n].shape[2]) if n in _IN_W else W[n].shape[1]
        return off, end, -(-end // _PACK_ROWS) * _PACK_ROWS

    first_pieces = [("fox_w_in", 0)]
    later_pieces = [(n, l) for n in _IN_W + _OUT_W for l in range(W[n].shape[0]) if (n, l) not in first_pieces]
    layouts = [grad_layout([], first_pieces), grad_layout(list(range(DEPTH)), later_pieces)]
    gbuf = [jnp.zeros((4, lay[2], D), BF16) for lay in layouts]

    def with_pieces(buf, lay, pieces):
        off, end, total = lay
        blocks = []
        for q in range(4):
            parts = []
            for n, l in pieces:
                if n in _IN_W:
                    nc = W[n].shape[2]
                    parts.append(jnp.pad(GT[n, l][q * nc:(q + 1) * nc], ((0, _piece_rows(nc) - nc), (0, 0))))
                else:
                    nr = W[n].shape[1]
                    parts.append(G[n][l][q * nr:(q + 1) * nr])
            blocks.append(jnp.concatenate(parts + [jnp.zeros((total - end, D), BF16)], axis=0))
        return lax.dynamic_update_slice(buf, jnp.stack(blocks), (0, off[pieces[0]], 0))

    s_later = None
    for i in reversed(range(DEPTH)):
        kind, j = i % 3, i // 3
        h_in, y, mix, og, w_out, hm, yf, gate, up, act = saved[i]
        b = 1
        dg, du = _ffn_dact(dh, wpk, i, gate, up)
        gbuf[b] = _ffn_dw_down(act, dh, gbuf[b], i, layouts[b][0]["down", i])
        dhm, dnf = _ffn_dyf(dg, du, wpk, i, hm, norm_ffn[i], dh)
        gbuf[b] = _ffn_dw_gu(dg, du, yf, gbuf[b], i, layouts[b][0]["gu", i] // GU_ROWS)
        G["norm_ffn"][i] = dnf[0]
        dog = _mm(dhm, w_out, tb=True, name=f"d_og{i}")
        dw_out = _mm(og, dhm, ta=True, out_dtype=BF16, name=f"d_w_out{i}")
        if kind == 0:
            proj, qa, ka, va, o, lse = mix
            G["fox_w_out"][j] = dw_out
            if i == 0:
                g_later = with_pieces(gbuf[1], layouts[1], later_pieces)
                doa, q2, dgate, got = _fox_gate_bwd(dog, o, proj, lse, qa, swap=g_later)
                s_later = _add_halves(g_later, got, "_later")
                dqn, dkn, dv, dct, recv_later = _fox_attn_bwd(q2, ka, va, doa, rs=s_later)
                t_later = _sum_chips(s_later, recv_later, "_later")
                dproj, dqg, dkg, dbf, r_later = _fox_prep_bwd(proj, fox_b_f[j], fox_q_gain[j], fox_k_gain[j], dqn, dkn, dv,
                                                              dgate, dct, join=t_later)
            else:
                doa, q2, dgate = _fox_gate_bwd(dog, o, proj, lse, qa)
                dqn, dkn, dv, dct = _fox_attn_bwd(q2, ka, va, doa)
                dproj, dqg, dkg, dbf = _fox_prep_bwd(proj, fox_b_f[j], fox_q_gain[j], fox_k_gain[j], dqn, dkn, dv, dgate, dct)
            G["fox_q_gain"][j] = dqg.reshape(FOX_H, FOX_DH).sum(0)
            G["fox_k_gain"][j] = dkg.reshape(FOX_H, FOX_DH).sum(0)
            G["fox_b_f"][j] = dbf[0, :FOX_H]
            w_in, wname = fox_in[j], "fox_w_in"
        elif kind == 1:
            proj, o, states = mix
            dproj, dwa, dba, dogain = _gla_bwd(proj, w_alpha2, gla_b_alpha[j], gla_o_gain[j], o, states, dog)
            G["gla_w_out"][j] = dw_out
            G["gla_w_alpha2"][j] = dwa[:GLA_RANK]
            G["gla_b_alpha"][j] = dba[0]
            G["gla_o_gain"][j] = dogain[0]
            w_in, wname = gla_in[j], "gla_w_in"
        else:
            proj, o, states, tinvs = mix
            dproj, dcw, dal, ddt, dogain = _gdn_bwd(proj, conv_w, gdn_a_log[j], gdn_dt_bias[j], gdn_o_gain[j], o, states,
                                                    tinvs, dog)
            G["gdn_w_out"][j] = dw_out
            G["gdn_conv_w"][j] = dcw[:4].reshape(4, 1, GDN_CONV)
            G["gdn_a_log"][j] = dal[0, :GDN_H]
            G["gdn_dt_bias"][j] = ddt[0, :GDN_H]
            G["gdn_o_gain"][j] = dogain[0]
            w_in, wname = gdn_in[j], "gdn_w_in"
        GT[wname, j] = _mm(dproj, y, ta=True, out_dtype=BF16, name=f"d_w_in{i}")
        if i == 0:
            s_first = _rs_local(with_pieces(gbuf[0], layouts[0], first_pieces), "_first")
            dh, dnm, recv_first = _mm(dproj, w_in, rms_bwd=(h_in, norm_mix[i], dhm), rs=s_first, name=f"d_y{i}")
        else:
            dh, dnm = _mm(dproj, w_in, rms_bwd=(h_in, norm_mix[i], dhm), name=f"d_y{i}")
        G["norm_mix"][i] = dnm[0]
    grad_x = dh[ROW0:][None]
    G = {n: (v if n in _OUT_W else jnp.stack(v)) for n, v in G.items()}
    G["meta_tokens"] = dh[META0:ROW0]

    reduced = [_rs_finish(s_first, recv_first, "_first"), _both_halves(t_later, r_later)]

    def reduced_piece(n, l):
        b = 0 if (n, l) in first_pieces else 1
        start = layouts[b][0][n, l]
        return reduced[b][start:start + (W[n].shape[2] if n in _IN_W else W[n].shape[1])]

    grads = {}
    for n in _IN_W:
        grads[n] = jnp.stack([reduced_piece(n, l).T for l in range(W[n].shape[0])])
    for n in _OUT_W:
        grads[n] = jnp.stack([reduced_piece(n, l) for l in range(W[n].shape[0])])
    small_names = [n for n, _ in _SMALL_SHARDED] + list(_REPLICATED)
    small_g = _pack([G[n] for n in small_names] + [loss_tile[0, 0:1]], LANES, 8, F32)
    small_sum = _gather8(small_g, reduce=True, name="allreduce_small").reshape(-1)
    small_shapes = [G[n].shape for n in small_names] + [(1,)]
    small_vals = _unpack(small_sum, small_shapes)
    loss = small_vals[-1][0]
    for n, val in zip(small_names, small_vals[:-1]):
        grads[n] = val
    for n, ax in _SMALL_SHARDED:
        sz = W[n].shape[ax]
        grads[n] = lax.dynamic_slice_in_dim(grads[n], chip * sz, sz, axis=ax)

    delta, new_m, new_v = {}, {}, {}
    for n, key, tr_ in (("w_gate_up", "gu", True), ("w_down", "down", False)):
        grads[n], delta[n], new_m[n], new_v[n] = _adamw_packed(
            W[n], reduced[1], reduced[1], M[n], V[n], row0=layouts[1][0][key, 0], row_off=layouts[1][0][key, 1],
            transposed=tr_, name=f"adamw_{n}")
    for n in _IN_W + _OUT_W:
        delta[n], new_m[n], new_v[n] = _adamw(W[n], grads[n], M[n], V[n], name=f"adamw_{n}")
    tiny = [n for n in _WEIGHTS if n not in ("w_gate_up", "w_down") + _IN_W + _OUT_W]
    packs = [_pack([T[n] for n in tiny], LANES, 8, F32) for T in (W, grads, M, V)]
    outs = _adamw(*packs, name="adamw_small")
    shapes = [W[n].shape for n in tiny]
    for dst, o in zip((delta, new_m, new_v), outs):
        for n, val in zip(tiny, _unpack(o.reshape(-1), shapes)):
            dst[n] = val
    return (loss, grad_x, *[grads[n] for n in _WEIGHTS], *[delta[n] for n in _WEIGHTS],
            *[new_m[n] for n in _WEIGHTS], *[new_v[n] for n in _WEIGHTS])
```

```python
import jax
import jax.numpy as jnp
from jax import lax
from jax.experimental import pallas as pl
from jax.experimental.pallas import tpu as pltpu

F32, BF16 = jnp.float32, jnp.bfloat16
D = 1024
N_META = 16
ROW0 = 128
META0 = ROW0 - N_META
EPS = 1e-6
LANES = 128
VMEM_LIMIT = 56 * 1024 * 1024

FOX_H, FOX_DH = 16, 64
FOX_INP = 4224
GLA_H, GLA_DK, GLA_DV, GLA_RANK = 4, 128, 256, 16
GLA_QK, GLA_V = 512, 1024
GLA_INP = 3200
GLA_NORM = 16.0
GDN_H, GDN_DK, GDN_DV = 8, 128, 128
GDN_CONV = 3072
GDN_INP = 4224
CHUNK = 64
D_FF = 2816
DEPTH = 4

ADAM_LR, ADAM_B1, ADAM_B2, ADAM_EPS, ADAM_WD, ADAM_STEP = 0.001, 0.9, 0.999, 1e-08, 0.01, 10

MESH = pl.DeviceIdType.MESH
ANY = pl.BlockSpec(memory_space=pl.ANY)
VM = pl.BlockSpec(memory_space=pltpu.VMEM)


def _params(sem=None, **kw):
    if sem is not None:
        kw["dimension_semantics"] = sem
    return pltpu.CompilerParams(vmem_limit_bytes=VMEM_LIMIT, **kw)


def _tile(n, cap, mult=LANES):
    best = None
    for t in range(mult, min(n, cap) + 1, mult):
        if n % t == 0:
            best = t
    return best if best is not None else n


def nn(a, b, **kw):
    return jnp.dot(a, b, preferred_element_type=F32, **kw)


def nt(a, b, **kw):
    return lax.dot_general(a, b, (((1,), (1,)), ((), ())), preferred_element_type=F32, **kw)


def tn(a, b, **kw):
    return lax.dot_general(a, b, (((0,), (0,)), ((), ())), preferred_element_type=F32, **kw)


def _split3(x):
    hi = x.astype(BF16)
    r = x - hi.astype(F32)
    mid = r.astype(BF16)
    lo = (r - mid.astype(F32)).astype(BF16)
    return hi, mid, lo


def _sel_l(sel, x):
    a, b, c = _split3(x)
    return nn(sel, a) + nn(sel, b) + nn(sel, c)


def _sel_r(x, sel):
    a, b, c = _split3(x)
    return nn(a, sel) + nn(b, sel) + nn(c, sel)


def _sel_r2(x, sel):
    a = x.astype(BF16)
    return nn(a, sel) + nn((x - a.astype(F32)).astype(BF16), sel)


def _iota(shape, dim):
    return lax.broadcasted_iota(jnp.int32, shape, dim)


def _tri(n, upper=False, strict=False):
    i, j = _iota((n, n), 0), _iota((n, n), 1)
    if upper:
        m = (j > i) if strict else (j >= i)
    else:
        m = (j < i) if strict else (j <= i)
    return m


def _sigmoid(x):
    return 1.0 / (1.0 + jnp.exp(-x))


def _log_sigmoid(x):
    return jnp.minimum(x, 0.0) - jnp.log(1.0 + jnp.exp(-jnp.abs(x)))


def _softplus(x):
    return jnp.maximum(x, 0.0) + jnp.log(1.0 + jnp.exp(-jnp.abs(x)))


def _silu(x):
    return x * _sigmoid(x)


def _dsilu(x):
    s = _sigmoid(x)
    return s * (1.0 + x * (1.0 - s))


def _rms(x, g):
    return (x * lax.rsqrt(jnp.mean(x * x, axis=-1, keepdims=True) + EPS) * g).astype(BF16)


def _rms_grad(x, g, dy):
    r = lax.rsqrt(jnp.mean(x * x, axis=-1, keepdims=True) + EPS)
    u = dy * g
    return r * u - x * (r * r * r) * jnp.mean(x * u, axis=-1, keepdims=True), jnp.sum(dy * x * r, axis=0, keepdims=True)


def _mm(a, b, *, ta=False, tb=False, add=None, norm=None, rms_bwd=None, rs=None, out_dtype=F32, name):
    m, k = (a.shape[1], a.shape[0]) if ta else a.shape
    n = b.shape[0] if tb else b.shape[1]
    assert k == (b.shape[1] if tb else b.shape[0])
    rows_whole = norm is not None or rms_bwd is not None
    tm, tn_, tk = _tile(m, 704 if rows_whole else 1408, LANES if ta else 16), _tile(n, 1408), _tile(k, 1408)
    nk = k // tk
    assert not rows_whole or tn_ == n
    assert rs is None or rms_bwd is not None

    def body(*refs):
        refs = list(refs)
        if rs is not None:
            send_sems, recv_sems = refs[-2:]
            refs = refs[:-2]
            s_ref, got_ref = refs.pop(5), refs.pop(-2)
            sends, arrivals = _scatter_copies(s_ref, got_ref, send_sems, recv_sems)

            @pl.when((pl.program_id(0) == 0) & (pl.program_id(2) == 0))
            def _():
                for cp in sends:
                    cp.start()

            @pl.when((pl.program_id(0) == m // tm - 1) & (pl.program_id(2) == nk - 1))
            def _():
                for cp in arrivals:
                    cp.wait_recv()
                for cp in sends:
                    cp.wait_send()

        a_ref, b_ref = refs[:2]
        extra = refs[2:-1]
        acc = refs[-1]
        i, kk = pl.program_id(0), pl.program_id(2)

        @pl.when(kk == 0)
        def _():
            acc[...] = jnp.zeros_like(acc)

        av, bv = a_ref[...].astype(BF16), b_ref[...].astype(BF16)
        dims = (((0,) if ta else (1,), (1,) if tb else (0,)), ((), ()))
        acc[...] += lax.dot_general(av, bv, dims, preferred_element_type=F32)

        @pl.when(kk == nk - 1)
        def _():
            r = acc[...]
            if rms_bwd is not None:
                h_ref, g_ref, dres_ref, o_ref, dg_ref = extra
                dx, dgain = _rms_grad(h_ref[...], g_ref[...], r)
                o_ref[...] = dres_ref[...] + dx

                @pl.when(i == 0)
                def _():
                    dg_ref[...] = jnp.zeros_like(dg_ref)

                dg_ref[...] += dgain
                return
            if add is not None:
                r = r + extra[0][...].astype(F32)
            if norm is not None:
                g_ref, o_ref, y_ref = extra[-3:]
                y_ref[...] = _rms(r, g_ref[...])
            else:
                o_ref = extra[-1]
            o_ref[...] = r.astype(out_dtype)

    a_spec = pl.BlockSpec((tk, tm), lambda i, j, q: (q, i)) if ta else pl.BlockSpec((tm, tk), lambda i, j, q: (i, q))
    b_spec = pl.BlockSpec((tn_, tk), lambda i, j, q: (j, q)) if tb else pl.BlockSpec((tk, tn_), lambda i, j, q: (q, j))
    o_spec = pl.BlockSpec((tm, tn_), lambda i, j, q: (i, j))
    g_spec = pl.BlockSpec((1, n), lambda i, j, q: (0, 0))
    ins, specs = [a, b], [a_spec, b_spec]
    out_specs, out_shape = o_spec, jax.ShapeDtypeStruct((m, n), out_dtype)
    sem = ("parallel", "parallel", "arbitrary")
    scratch = [pltpu.VMEM((tm, tn_), F32)]
    if rms_bwd is not None:
        ins += [rms_bwd[0], rms_bwd[1].reshape(1, n), rms_bwd[2]]
        specs += [o_spec, g_spec, o_spec]
        out_specs, out_shape = [o_spec, g_spec], [jax.ShapeDtypeStruct((m, n), F32), jax.ShapeDtypeStruct((1, n), F32)]
        sem = ("arbitrary", "arbitrary", "arbitrary")
        if rs is not None:
            ins.append(rs)
            specs.append(ANY)
            out_specs.append(ANY)
            out_shape.append(jax.ShapeDtypeStruct((3,) + rs.shape[1:], rs.dtype))
            scratch += [pltpu.SemaphoreType.DMA((3,)), pltpu.SemaphoreType.DMA((3,))]
    else:
        if add is not None:
            ins.append(add)
            specs.append(o_spec)
        if norm is not None:
            ins.append(norm.reshape(1, n))
            specs.append(g_spec)
            out_specs, out_shape = [o_spec, o_spec], [out_shape, jax.ShapeDtypeStruct((m, n), BF16)]
    return pl.pallas_call(
        body, name=name, grid=(m // tm, n // tn_, nk), in_specs=specs, out_specs=out_specs, out_shape=out_shape,
        scratch_shapes=scratch, compiler_params=_params(sem),
    )(*ins)


def _rms_fwd(h, g, *, name):
    lp = h.shape[0]
    tr = _tile(lp, 512)

    def body(h_ref, g_ref, y_ref):
        x = h_ref[...]
        r = lax.rsqrt(jnp.mean(x * x, axis=-1, keepdims=True) + EPS)
        y_ref[...] = (x * r * g_ref[...]).astype(BF16)

    return pl.pallas_call(
        body, name=name, grid=(lp // tr,),
        in_specs=[pl.BlockSpec((tr, D), lambda i: (i, 0)), pl.BlockSpec((1, D), lambda i: (0, 0))],
        out_specs=pl.BlockSpec((tr, D), lambda i: (i, 0)),
        out_shape=jax.ShapeDtypeStruct((lp, D), BF16), compiler_params=_params(("parallel",)),
    )(h, g.reshape(1, D))


GU_ROWS, DOWN_ROWS = 1408, 704
OFF_GU, OFF_DOWN = 0, DEPTH * GU_ROWS
FFN_ROWS = DEPTH * (GU_ROWS + DOWN_ROWS)
FFN_TM = 704


def _gu_spec(fn):
    return pl.BlockSpec((None, GU_ROWS, D), fn)


def _down_spec(fn):
    return pl.BlockSpec((None, DOWN_ROWS, D), fn)


def _down_pair(w0_ref, w1_ref):
    return jnp.concatenate([w0_ref[...], w1_ref[...]], axis=0)


def _ffn_up(yf, wpk, layer):
    lp = yf.shape[0]
    tm = _tile(lp, FFN_TM, 16)

    def body(y_ref, wg_ref, wu_ref, g_ref, u_ref, a_ref):
        y = y_ref[...]
        g, u = nt(y, wg_ref[...]), nt(y, wu_ref[...])
        g_ref[...] = g.astype(BF16)
        u_ref[...] = u.astype(BF16)
        a_ref[...] = (_silu(g) * u).astype(BF16)

    o = pl.BlockSpec((tm, GU_ROWS), lambda i, j: (i, j))
    return pl.pallas_call(
        body, name=f"ffn_up{layer}", grid=(lp // tm, 2),
        in_specs=[pl.BlockSpec((tm, D), lambda i, j: (i, 0)), _gu_spec(lambda i, j: (j, OFF_GU // GU_ROWS + layer, 0)),
                  _gu_spec(lambda i, j: (2 + j, OFF_GU // GU_ROWS + layer, 0))],
        out_specs=[o, o, o], out_shape=[jax.ShapeDtypeStruct((lp, D_FF), BF16)] * 3,
        compiler_params=_params(("parallel", "parallel")),
    )(yf, wpk, wpk)


def _ffn_down(act, wpk, layer, res, norm):
    lp = act.shape[0]
    tm = _tile(lp, FFN_TM, 16)

    def body(a_ref, w0_ref, w1_ref, r_ref, g_ref, o_ref, y_ref, acc):
        kk = pl.program_id(1)

        @pl.when(kk == 0)
        def _():
            acc[...] = r_ref[...]

        acc[...] += nn(a_ref[...], _down_pair(w0_ref, w1_ref))

        @pl.when(kk == 1)
        def _():
            o_ref[...] = acc[...]
            y_ref[...] = _rms(acc[...], g_ref[...])

    o = pl.BlockSpec((tm, D), lambda i, kk: (i, 0))
    blk = OFF_DOWN // DOWN_ROWS + layer
    return pl.pallas_call(
        body, name=f"ffn_down{layer}", grid=(lp // tm, 2),
        in_specs=[pl.BlockSpec((tm, GU_ROWS), lambda i, kk: (i, kk)), _down_spec(lambda i, kk: (2 * kk, blk, 0)),
                  _down_spec(lambda i, kk: (2 * kk + 1, blk, 0)), o, pl.BlockSpec((1, D), lambda i, kk: (0, 0))],
        out_specs=[o, o], out_shape=[jax.ShapeDtypeStruct((lp, D), F32), jax.ShapeDtypeStruct((lp, D), BF16)],
        scratch_shapes=[pltpu.VMEM((tm, D), F32)], compiler_params=_params(("parallel", "arbitrary")),
    )(act, wpk, wpk, res, norm.reshape(1, D))


def _ffn_dact(dh, wpk, layer, gate, up):
    lp = dh.shape[0]
    tm = _tile(lp, FFN_TM, 16)

    def body(d_ref, w0_ref, w1_ref, g_ref, u_ref, dg_ref, du_ref):
        da = nt(d_ref[...].astype(BF16), _down_pair(w0_ref, w1_ref))
        g, u = g_ref[...].astype(F32), u_ref[...].astype(F32)
        sg = _sigmoid(g)
        dg_ref[...] = (da * u * (sg * (1.0 + g * (1.0 - sg)))).astype(BF16)
        du_ref[...] = (da * (g * sg)).astype(BF16)

    o = pl.BlockSpec((tm, GU_ROWS), lambda i, j: (i, j))
    blk = OFF_DOWN // DOWN_ROWS + layer
    return pl.pallas_call(
        body, name=f"d_act{layer}", grid=(lp // tm, 2),
        in_specs=[pl.BlockSpec((tm, D), lambda i, j: (i, 0)), _down_spec(lambda i, j: (2 * j, blk, 0)),
                  _down_spec(lambda i, j: (2 * j + 1, blk, 0)), o, o],
        out_specs=[o, o], out_shape=[jax.ShapeDtypeStruct((lp, D_FF), BF16)] * 2,
        compiler_params=_params(("parallel", "parallel")),
    )(dh, wpk, wpk, gate, up)


def _ffn_dyf(dg, du, wpk, layer, hm, norm, dres):
    lp = dg.shape[0]
    tm = _tile(lp, FFN_TM, 16)

    def body(dg_ref, du_ref, w_ref, h_ref, g_ref, dres_ref, o_ref, dgain_ref, acc):
        i, kk = pl.program_id(0), pl.program_id(1)

        @pl.when(kk == 0)
        def _():
            acc[...] = jnp.zeros_like(acc)

        @pl.when(kk < 2)
        def _():
            acc[...] += nn(dg_ref[...], w_ref[...])

        @pl.when(kk >= 2)
        def _():
            acc[...] += nn(du_ref[...], w_ref[...])

        @pl.when(kk == 3)
        def _():
            dx, dgain = _rms_grad(h_ref[...], g_ref[...], acc[...])
            o_ref[...] = dres_ref[...] + dx

            @pl.when(i == 0)
            def _():
                dgain_ref[...] = jnp.zeros_like(dgain_ref)

            dgain_ref[...] += dgain

    o = pl.BlockSpec((tm, D), lambda i, kk: (i, 0))
    row = pl.BlockSpec((1, D), lambda i, kk: (0, 0))
    return pl.pallas_call(
        body, name=f"d_yf{layer}", grid=(lp // tm, 4),
        in_specs=[pl.BlockSpec((tm, GU_ROWS), lambda i, kk: (i, jnp.minimum(kk, 1))),
                  pl.BlockSpec((tm, GU_ROWS), lambda i, kk: (i, jnp.maximum(kk - 2, 0))),
                  _gu_spec(lambda i, kk: (kk, OFF_GU // GU_ROWS + layer, 0)), o, row, o],
        out_specs=[o, row], out_shape=[jax.ShapeDtypeStruct((lp, D), F32), jax.ShapeDtypeStruct((1, D), F32)],
        scratch_shapes=[pltpu.VMEM((tm, D), F32)], compiler_params=_params(("arbitrary", "arbitrary")),
    )(dg, du, wpk, hm, norm.reshape(1, D), dres)


def _ffn_dw_down(act, dh, gpk, layer, row):
    lp = act.shape[0]
    tk = _tile(lp, 1408, 16)
    nk = lp // tk

    def body(a_ref, d_ref, g_in, g_out, acc, stage, sems):
        jp, kk = pl.program_id(0), pl.program_id(1)

        @pl.when(kk == 0)
        def _():
            acc[...] = jnp.zeros_like(acc)

        acc[...] += tn(a_ref[...], d_ref[...].astype(BF16))

        @pl.when(kk == nk - 1)
        def _():
            stage[...] = acc[...].astype(BF16)
            copies = [pltpu.make_async_copy(stage.at[pl.ds(hf * DOWN_ROWS, DOWN_ROWS), :],
                                            g_out.at[2 * jp + hf, pl.ds(row, DOWN_ROWS), :], sems.at[hf]) for hf in range(2)]
            for cp in copies:
                cp.start()
            for cp in copies:
                cp.wait()

    return pl.pallas_call(
        body, name=f"d_w_down{layer}", grid=(2, nk),
        in_specs=[pl.BlockSpec((tk, GU_ROWS), lambda jp, kk: (kk, jp)), pl.BlockSpec((tk, D), lambda jp, kk: (kk, 0)), ANY],
        out_specs=ANY, out_shape=jax.ShapeDtypeStruct(gpk.shape, gpk.dtype),
        scratch_shapes=[pltpu.VMEM((GU_ROWS, D), F32), pltpu.VMEM((GU_ROWS, D), BF16), pltpu.SemaphoreType.DMA((2,))],
        input_output_aliases={2: 0}, compiler_params=_params(("arbitrary", "arbitrary")),
    )(act, dh, gpk)


def _ffn_dw_gu(dg, du, yf, gpk, layer, blk):
    lp = dg.shape[0]
    tk = _tile(lp, 1408, 16)
    nk = lp // tk

    def body(dg_ref, du_ref, y_ref, g_in, o_ref, acc):
        c, kk = pl.program_id(0), pl.program_id(1)

        @pl.when(kk == 0)
        def _():
            acc[...] = jnp.zeros_like(acc)

        @pl.when(c < 2)
        def _():
            acc[...] += tn(dg_ref[...], y_ref[...])

        @pl.when(c >= 2)
        def _():
            acc[...] += tn(du_ref[...], y_ref[...])

        @pl.when(kk == nk - 1)
        def _():
            o_ref[...] = acc[...].astype(BF16)

    return pl.pallas_call(
        body, name=f"d_w_gate_up{layer}", grid=(4, nk),
        in_specs=[pl.BlockSpec((tk, GU_ROWS), lambda c, kk: (kk, jnp.minimum(c, 1))),
                  pl.BlockSpec((tk, GU_ROWS), lambda c, kk: (kk, jnp.maximum(c - 2, 0))),
                  pl.BlockSpec((tk, D), lambda c, kk: (kk, 0)), ANY],
        out_specs=_gu_spec(lambda c, kk: (c, blk, 0)),
        out_shape=jax.ShapeDtypeStruct(gpk.shape, gpk.dtype),
        scratch_shapes=[pltpu.VMEM((GU_ROWS, D), F32)], input_output_aliases={3: 0},
        compiler_params=_params(("parallel", "arbitrary")),
    )(dg, du, yf, gpk)


def _loss_head(h, target):
    lp = h.shape[0]
    nb = lp // ROW0

    def body(h_ref, t_ref, dh_ref, l_ref):
        i = pl.program_id(0)

        @pl.when(i == 0)
        def _():
            l_ref[...] = jnp.zeros_like(l_ref)
            dh_ref[...] = jnp.zeros_like(dh_ref)

        @pl.when(i > 0)
        def _():
            err = h_ref[...] - t_ref[...]
            dh_ref[...] = err * (1.0 / D)
            l_ref[...] += jnp.sum(err * err) * (0.5 / D)

    return pl.pallas_call(
        body, name="loss_head", grid=(nb,),
        in_specs=[pl.BlockSpec((ROW0, D), lambda i: (i, 0)), pl.BlockSpec((ROW0, D), lambda i: (jnp.maximum(i - 1, 0), 0))],
        out_specs=[pl.BlockSpec((ROW0, D), lambda i: (i, 0)), pl.BlockSpec((8, LANES), lambda i: (0, 0))],
        out_shape=[jax.ShapeDtypeStruct((lp, D), F32), jax.ShapeDtypeStruct((8, LANES), F32)],
        compiler_params=_params(("arbitrary",)),
    )(h, target)


def _adamw(w, g, m, v, *, name):
    if w.ndim == 2:
        w, g, m, v = (t[None] for t in (w, g, m, v))
        return tuple(o[0] for o in _adamw(w, g, m, v, name=name))
    nl, r, c = w.shape
    tr = _tile(r, max(8, (1 << 19) // c), 8)

    def body(w_ref, g_ref, m_ref, v_ref, d_ref, nm_ref, nv_ref):
        d_ref[...], nm_ref[...], nv_ref[...] = _adam_math(w_ref[...], g_ref[...], m_ref[...], v_ref[...])

    spec = pl.BlockSpec((1, tr, c), lambda l, i: (l, i, 0))
    return tuple(pl.pallas_call(
        body, name=name, grid=(nl, r // tr), in_specs=[spec] * 4, out_specs=[spec] * 3,
        out_shape=[jax.ShapeDtypeStruct(w.shape, F32)] * 3, compiler_params=_params(("parallel", "parallel")),
    )(w, g, m, v))


def _adam_math(w, g, m, v):
    nm = ADAM_B1 * m + (1.0 - ADAM_B1) * g
    nv = ADAM_B2 * v + (1.0 - ADAM_B2) * (g * g)
    m_hat = nm / (1.0 - ADAM_B1 ** ADAM_STEP)
    v_hat = nv / (1.0 - ADAM_B2 ** ADAM_STEP)
    return -ADAM_LR * (m_hat / (jnp.sqrt(v_hat) + ADAM_EPS) + ADAM_WD * w), nm, nv


def _adamw_packed(w, gred0, gred, m, v, *, row0, row_off, transposed, name):
    nl, a, b = w.shape
    nr = b if transposed else a
    later = lambda l: row_off // nr + jnp.maximum(l - 1, 0)
    if transposed:
        ta = _tile(a, 256)
        wspec = pl.BlockSpec((1, ta, b), lambda l, r: (l, r, 0))
        g0spec = pl.BlockSpec((b, ta), lambda l, r: (row0 // nr, r))
        gspec = pl.BlockSpec((b, ta), lambda l, r: (later(l), r))
        grid = (nl, a // ta)
    else:
        wspec = pl.BlockSpec((1, a, b), lambda l, r: (l, 0, 0))
        g0spec = pl.BlockSpec((a, b), lambda l, r: (row0 // nr, 0))
        gspec = pl.BlockSpec((a, b), lambda l, r: (later(l), 0))
        grid = (nl, 1)

    def body(w_ref, g0_ref, g_ref, m_ref, v_ref, go_ref, d_ref, nm_ref, nv_ref):
        g = jnp.where(pl.program_id(0) == 0, g0_ref[...], g_ref[...])
        g = g.T if transposed else g
        d, nm, nv = _adam_math(w_ref[0], g, m_ref[0], v_ref[0])
        go_ref[0], d_ref[0], nm_ref[0], nv_ref[0] = g, d, nm, nv

    return pl.pallas_call(
        body, name=name, grid=grid, in_specs=[wspec, g0spec, gspec, wspec, wspec], out_specs=[wspec] * 4,
        out_shape=[jax.ShapeDtypeStruct(w.shape, F32)] * 4, compiler_params=_params(("parallel", "parallel")),
    )(w, gred0, gred, m, v)


FOX_AUG = FOX_H * LANES
L_C = 64
L_K = 67
L_LSE = 70
PAD_KEY = -30000.0
FOX_TQ = 384


def _head_sel(n_heads, width, lanes=LANES):
    r, c = _iota((n_heads * width, lanes), 0), _iota((n_heads * width, lanes), 1)
    down = (r // width == c).astype(BF16)
    r2, c2 = _iota((lanes, n_heads * width), 0), _iota((lanes, n_heads * width), 1)
    up = (c2 // width == r2).astype(BF16)
    return down, up


def _place(lane0):
    r, c = _iota((LANES, FOX_AUG), 0), _iota((LANES, FOX_AUG), 1)
    return [((c // LANES == r) & (c % LANES == lane0 + m)).astype(BF16) for m in range(3)]


def _placed(x, lane0):
    pcs = _split3(x)
    mats = _place(lane0)
    return nn(pcs[0], mats[0]) + nn(pcs[1], mats[1]) + nn(pcs[2], mats[2])


def _ones_at(rows, lanes):
    c = _iota((rows, FOX_AUG), 1) % LANES
    m = c == lanes[0]
    for l in lanes[1:]:
        m = m | (c == l)
    return m.astype(F32)


def _spread(x, extras, out_ref):
    rows = x.shape[0]
    left = _iota((rows, LANES), 1) < FOX_DH
    for p in range(FOX_H // 2):
        slab = x[:, p * LANES:(p + 1) * LANES]
        a = jnp.where(left, slab, extras[:, 2 * p * LANES:(2 * p + 1) * LANES])
        b = jnp.where(left, pltpu.roll(slab, FOX_DH, 1), extras[:, (2 * p + 1) * LANES:(2 * p + 2) * LANES])
        out_ref[:, 2 * p * LANES:(2 * p + 1) * LANES] = a.astype(BF16)
        out_ref[:, (2 * p + 1) * LANES:(2 * p + 2) * LANES] = b.astype(BF16)


def _fox_prep(proj, b_f, q_gain, k_gain):
    lp = proj.shape[0]
    nb = lp // LANES

    def body(p_ref, bf_ref, qg_ref, kg_ref, q_ref, k_ref, v_ref, carry):
        i = pl.program_id(0)

        @pl.when(i == 0)
        def _():
            carry[...] = jnp.zeros_like(carry)

        down, up = _head_sel(FOX_H, FOX_DH)

        def normed(x, gain):
            ms = _sel_r2(x * x, down) * (1.0 / FOX_DH)
            r = _sel_r2(lax.rsqrt(ms + EPS), up)
            return x * r * gain

        lane = _iota((LANES, LANES), 1)
        lf = jnp.where(lane < FOX_H, _log_sigmoid(p_ref[:, 4 * D:4 * D + LANES] + bf_ref[...]), 0.0)
        c = _sel_l(_tri(LANES).astype(BF16), lf) + carry[0:1, :]
        carry[...] = jnp.broadcast_to(c[LANES - 1:LANES, :], carry.shape)
        q_extra = _placed(c, L_C) + _ones_at(LANES, (L_K, L_K + 1, L_K + 2))
        row = i * LANES + _iota((LANES, FOX_AUG), 0)
        lane_a = _iota((LANES, FOX_AUG), 1) % LANES
        k_extra = -_placed(c, L_K) + _ones_at(LANES, (L_C, L_C + 1, L_C + 2, L_LSE, L_LSE + 1, L_LSE + 2))
        pad_val = jnp.where(lane_a == L_K, PAD_KEY, 0.0)
        k_extra = jnp.where((row < META0) & (lane_a >= L_K) & (lane_a < L_K + 3), pad_val, k_extra)
        v_extra = _ones_at(LANES, (L_C, L_C + 1, L_C + 2))
        _spread(normed(p_ref[:, 0:D], qg_ref[...]) * (FOX_DH ** -0.5), q_extra, q_ref)
        _spread(normed(p_ref[:, D:2 * D], kg_ref[...]), k_extra, k_ref)
        _spread(p_ref[:, 2 * D:3 * D], v_extra, v_ref)

    row = pl.BlockSpec((1, D), lambda i: (0, 0))
    aug = pl.BlockSpec((LANES, FOX_AUG), lambda i: (i, 0))
    return pl.pallas_call(
        body, name="fox_prep", grid=(nb,),
        in_specs=[pl.BlockSpec((LANES, FOX_INP), lambda i: (i, 0)), pl.BlockSpec((1, LANES), lambda i: (0, 0)), row, row],
        out_specs=[aug] * 3, out_shape=[jax.ShapeDtypeStruct((lp, FOX_AUG), BF16)] * 3,
        scratch_shapes=[pltpu.VMEM((8, LANES), F32)],
        compiler_params=_params(("arbitrary",)),
    )(proj, jnp.pad(b_f, (0, LANES - FOX_H)).reshape(1, LANES), jnp.tile(q_gain, FOX_H).reshape(1, D),
      jnp.tile(k_gain, FOX_H).reshape(1, D))


def _fox_attn_fwd(qa, ka, va, proj, ag=None):
    lp = qa.shape[0]
    tq = _tile(lp, FOX_TQ)
    nq = lp // tq
    npair = FOX_H // 2

    def body(q_ref, k_ref, v_ref, gate_ref, *rest):
        if ag is None:
            o_ref, og_ref, lse_ref = rest
        else:
            _, o_ref, og_ref, lse_ref, w_out, send_sems, recv_sems, send2, recv2 = rest
            copies, fwd = _AgCopies(w_out, ag[1], send_sems, recv_sems), _AgCopies(w_out, ag[1], send2, recv2)

            @pl.when((pl.program_id(0) == 0) & (pl.program_id(1) == 0))
            def _():
                for r, k in copies.pairs():
                    copies.ici(r, k).start()

            @pl.when((pl.program_id(0) == npair - 1) & (pl.program_id(1) == 0))
            def _():
                for r, k in copies.pairs():
                    copies.ici_arrival(r, k).wait_recv()
                    fwd.forward(r, k).start()

        i = pl.program_id(1)
        causal = _iota((tq, tq), 1) <= _iota((tq, tq), 0)
        qs = [q_ref[:, hh * LANES:(hh + 1) * LANES] for hh in range(2)]

        def block(j, carry, diag):
            off = pl.multiple_of(j * tq, tq)
            out = []
            for hh in range(2):
                m, acc = carry[hh]
                k = k_ref[pl.ds(off, tq), hh * LANES:(hh + 1) * LANES]
                v = v_ref[pl.ds(off, tq), hh * LANES:(hh + 1) * LANES]
                s = nt(qs[hh], k)
                if diag:
                    s = jnp.where(causal, s, -1e30)
                m2 = jnp.maximum(m, jnp.max(s, axis=-1, keepdims=True))
                p = jnp.exp(s - m2)
                p_hi = p.astype(BF16)
                p_lo = (p - p_hi.astype(F32)).astype(BF16)
                out.append((m2, jnp.exp(m - m2) * acc + nn(p_hi, v) + nn(p_lo, v)))
            return tuple(out)

        init = tuple((jnp.full((tq, 1), -1e30, F32), jnp.zeros((tq, LANES), F32)) for _ in range(2))
        carry = lax.fori_loop(0, i // 2, lambda j, c: block(2 * j + 1, block(2 * j, c, False), False), init)
        carry = lax.cond(i % 2 == 1, lambda c: block(i - 1, c, False), lambda c: c, carry)
        carry = block(i, carry, True)
        outs, lses = [], []
        for hh in range(2):
            m, acc = carry[hh]
            l = acc[:, L_C:L_C + 1]
            outs.append(acc / l)
            lses.append(jnp.broadcast_to(m + jnp.log(l), (tq, LANES)))
        left = _iota((tq, LANES), 1) < FOX_DH
        o = jnp.where(left, outs[0], pltpu.roll(outs[1], FOX_DH, 1))
        o_ref[...] = o
        og_ref[...] = (o * _sigmoid(gate_ref[...])).astype(BF16)
        lse_ref[...] = jnp.where(left, lses[0], lses[1])

        if ag is not None:
            @pl.when((pl.program_id(0) == npair - 1) & (pl.program_id(1) == nq - 1))
            def _():
                for r, k in copies.pairs():
                    fwd.forward_arrival(r, k).wait_recv()
                for r, k in copies.pairs():
                    copies.ici(r, k).wait_send()
                    fwd.forward(r, k).wait_send()

    qspec = pl.BlockSpec((tq, 2 * LANES), lambda p, i: (i, p))
    kspec = pl.BlockSpec((lp, 2 * LANES), lambda p, i: (0, p))
    ospec = pl.BlockSpec((tq, LANES), lambda p, i: (i, p))
    ins, in_specs = [qa, ka, va, proj], [qspec, kspec, kspec, pl.BlockSpec((tq, LANES), lambda p, i: (i, 3 * D // LANES + p))]
    out_specs = [ospec] * 3
    out_shape = [jax.ShapeDtypeStruct((lp, D), F32), jax.ShapeDtypeStruct((lp, D), BF16), jax.ShapeDtypeStruct((lp, D), F32)]
    if ag is None:
        return pl.pallas_call(body, name="fox_attn_fwd", grid=(npair, nq), in_specs=in_specs, out_specs=out_specs,
                              out_shape=out_shape, compiler_params=_params(("parallel", "arbitrary")))(*ins)
    n = 3 * len(ag[1])
    return pl.pallas_call(
        body, name="fox_attn_fwd_ag", grid=(npair, nq), in_specs=in_specs + [ANY], out_specs=out_specs + [ANY],
        out_shape=out_shape + [jax.ShapeDtypeStruct(ag[0].shape, ag[0].dtype)],
        scratch_shapes=[pltpu.SemaphoreType.DMA((n,))] * 4, input_output_aliases={4: 3},
        compiler_params=_params(("arbitrary", "arbitrary")),
    )(*ins, ag[0])


def _fox_gate_bwd(dog, o, proj, lse, qa, swap=None):
    lp = o.shape[0]
    tr = LANES
    steps = lp // tr

    def body(d_ref, o_ref, g_ref, lse_ref, q_ref, *rest):
        if swap is None:
            do_ref, q2_ref, dgate_ref = rest
        else:
            src_ref, do_ref, q2_ref, dgate_ref, got_ref, send_sem, recv_sem = rest
            cp = _swap_copy(src_ref, got_ref, send_sem, recv_sem)

            @pl.when(pl.program_id(0) == 0)
            def _():
                cp.start()

            @pl.when(pl.program_id(0) == steps - 1)
            def _():
                cp.wait()

        down, _ = _head_sel(FOX_H, FOX_DH)
        sg = _sigmoid(g_ref[...])
        dv, ov = d_ref[...], o_ref[...]
        do = (dv * sg).astype(BF16).astype(F32)
        dgate_ref[...] = dv * ov * sg * (1.0 - sg)
        delta = _sel_r(do * ov, down)
        _spread(do, -_placed(delta, L_C), do_ref)
        r_, c_ = _iota((D, LANES), 0), _iota((D, LANES), 1)
        lse_c = _sel_r(lse_ref[...], (r_ == c_ * FOX_DH).astype(BF16))
        q2_ref[...] = (q_ref[...].astype(F32) - _placed(lse_c, L_LSE)).astype(BF16)

    spec = pl.BlockSpec((tr, D), lambda i: (i, 0))
    aug = pl.BlockSpec((tr, FOX_AUG), lambda i: (i, 0))
    in_specs = [spec, spec, pl.BlockSpec((tr, D), lambda i: (i, 3)), spec, aug]
    out_specs = [aug, aug, spec]
    out_shape = [jax.ShapeDtypeStruct((lp, FOX_AUG), BF16), jax.ShapeDtypeStruct((lp, FOX_AUG), BF16),
                 jax.ShapeDtypeStruct((lp, D), F32)]
    if swap is None:
        return pl.pallas_call(body, name="fox_gate_bwd", grid=(steps,), in_specs=in_specs, out_specs=out_specs,
                              out_shape=out_shape, compiler_params=_params(("parallel",)))(dog, o, proj, lse, qa)
    nb, r, c = swap.shape
    return pl.pallas_call(
        body, name="fox_gate_bwd_swap", grid=(steps,), in_specs=in_specs + [ANY], out_specs=out_specs + [ANY],
        out_shape=out_shape + [jax.ShapeDtypeStruct((nb, r // 2, c), swap.dtype)],
        scratch_shapes=[pltpu.SemaphoreType.DMA, pltpu.SemaphoreType.DMA], compiler_params=_params(("arbitrary",)),
    )(dog, o, proj, lse, qa, swap)


def _fox_attn_bwd(q2, ka, va, doa, rs=None):
    lp = q2.shape[0]
    t = _tile(lp, FOX_TQ)
    nb = lp // t
    npair = FOX_H // 2

    def body(q_ref, k_ref, v_ref, do_ref, *rest):
        if rs is None:
            dq_ref, dk_ref, dv_ref, dc_ref, dq_acc, dk_acc, dv_acc, dc_acc = rest
        else:
            s_ref, dq_ref, dk_ref, dv_ref, dc_ref, got_ref, dq_acc, dk_acc, dv_acc, dc_acc, send_sems, recv_sems = rest
            sends, arrivals = _scatter_copies(s_ref, got_ref, send_sems, recv_sems)

            @pl.when((pl.program_id(0) == 0) & (pl.program_id(1) == 0))
            def _():
                for cp in sends:
                    cp.start()

            @pl.when((pl.program_id(0) == npair - 1) & (pl.program_id(1) == nb - 1))
            def _():
                for cp in arrivals:
                    cp.wait_recv()
                for cp in sends:
                    cp.wait_send()

        j = pl.program_id(1)

        @pl.when(j == 0)
        def _():
            dq_acc[...] = jnp.zeros_like(dq_acc)

        causal = _iota((t, t), 1) <= _iota((t, t), 0)
        ks = [k_ref[:, hh * LANES:(hh + 1) * LANES] for hh in range(2)]
        vs = [v_ref[:, hh * LANES:(hh + 1) * LANES] for hh in range(2)]
        dk_acc[...] = jnp.zeros_like(dk_acc)
        dv_acc[...] = jnp.zeros_like(dv_acc)
        dc_acc[...] = jnp.zeros_like(dc_acc)

        def block(i, diag):
            off = pl.multiple_of(i * t, t)
            for hh in range(2):
                q = q_ref[pl.ds(off, t), hh * LANES:(hh + 1) * LANES]
                do = do_ref[pl.ds(off, t), hh * LANES:(hh + 1) * LANES]
                s = nt(q, ks[hh])
                if diag:
                    s = jnp.where(causal, s, -1e30)
                p = jnp.exp(s)
                ds = p * nt(do, vs[hh])
                dc_acc[hh] += jnp.sum(ds, axis=0, keepdims=True)
                dsb = ds.astype(BF16)
                dv_acc[hh] += tn(p.astype(BF16), do)
                dk_acc[hh] += tn(dsb, q)
                dq_acc[hh, pl.ds(off, t), :] += nn(dsb, ks[hh])

        block(j, True)
        below = nb - 1 - j

        def step(u, c):
            block(j + 1 + 2 * u, False)
            block(j + 2 + 2 * u, False)
            return c

        lax.fori_loop(0, below // 2, step, 0)

        @pl.when(below % 2 == 1)
        def _():
            block(nb - 1, False)
        left = _iota((t, LANES), 1) < FOX_DH
        dk_ref[...] = jnp.where(left, dk_acc[0], pltpu.roll(dk_acc[1], FOX_DH, 1))
        dv_ref[...] = jnp.where(left, dv_acc[0], pltpu.roll(dv_acc[1], FOX_DH, 1))
        for hh in range(2):
            dc_ref[hh] = jnp.broadcast_to(-dc_acc[hh], (8, t))

        @pl.when(j == nb - 1)
        def _():
            left = _iota((lp, LANES), 1) < FOX_DH
            dq_ref[...] = jnp.where(left, dq_acc[0], pltpu.roll(dq_acc[1], FOX_DH, 1))

    full = pl.BlockSpec((lp, 2 * LANES), lambda p, j: (0, p))
    kblk = pl.BlockSpec((t, 2 * LANES), lambda p, j: (j, p))
    oblk = pl.BlockSpec((t, LANES), lambda p, j: (j, p))
    in_specs = [full, kblk, kblk, full]
    out_specs = [pl.BlockSpec((lp, LANES), lambda p, j: (0, p)), oblk, oblk, pl.BlockSpec((2, 8, t), lambda p, j: (p, 0, j))]
    out_shape = [jax.ShapeDtypeStruct((lp, D), F32)] * 3 + [jax.ShapeDtypeStruct((FOX_H, 8, lp), F32)]
    scratch = [pltpu.VMEM((2, lp, LANES), F32), pltpu.VMEM((2, t, LANES), F32), pltpu.VMEM((2, t, LANES), F32),
               pltpu.VMEM((2, 1, t), F32)]
    if rs is None:
        return pl.pallas_call(body, name="fox_attn_bwd", grid=(npair, nb), in_specs=in_specs, out_specs=out_specs,
                              out_shape=out_shape, scratch_shapes=scratch,
                              compiler_params=_params(("parallel", "arbitrary")))(q2, ka, va, doa)
    return pl.pallas_call(
        body, name="fox_attn_bwd_rs", grid=(npair, nb), in_specs=in_specs + [ANY], out_specs=out_specs + [ANY],
        out_shape=out_shape + [jax.ShapeDtypeStruct((3,) + rs.shape[1:], rs.dtype)],
        scratch_shapes=scratch + [pltpu.SemaphoreType.DMA((3,)), pltpu.SemaphoreType.DMA((3,))],
        compiler_params=_params(("arbitrary", "arbitrary")),
    )(q2, ka, va, doa, rs)


def _fox_prep_bwd(proj, b_f, q_gain, k_gain, dqn, dkn, dv, dgate, dct, join=None):
    lp = proj.shape[0]
    nb = lp // LANES

    def body(p_ref, bf_ref, qg_ref, kg_ref, dq_ref, dk_ref, dv_ref, dg_ref, dc_ref, *rest):
        if join is None:
            dp_ref, dqg_ref, dkg_ref, dbf_ref, carry = rest
        else:
            t_ref, dp_ref, dqg_ref, dkg_ref, dbf_ref, got_ref, carry, send_sem, recv_sem = rest
            cp = _join_copy(t_ref, got_ref, send_sem, recv_sem)

            @pl.when(pl.program_id(0) == 0)
            def _():
                cp.start()

            @pl.when(pl.program_id(0) == nb - 1)
            def _():
                cp.wait()

        i = pl.program_id(0)

        @pl.when(i == 0)
        def _():
            carry[...] = jnp.zeros_like(carry)
            dqg_ref[...] = jnp.zeros_like(dqg_ref)
            dkg_ref[...] = jnp.zeros_like(dkg_ref)
            dbf_ref[...] = jnp.zeros_like(dbf_ref)

        down, up = _head_sel(FOX_H, FOX_DH)

        def norm_bwd(x, gain, dy, scale, dgain_ref):
            ms = _sel_r2(x * x, down) * (1.0 / FOX_DH)
            r = _sel_r2(lax.rsqrt(ms + EPS), up)
            u = dy * gain * scale
            mean_xu = _sel_r2(_sel_r2(x * u, down) * (1.0 / FOX_DH), up)
            dgain_ref[...] += jnp.sum(dy * scale * x * r, axis=0, keepdims=True)
            return r * u - x * (r * r * r) * mean_xu

        dp_ref[:, 0:D] = norm_bwd(p_ref[:, 0:D], qg_ref[...], dq_ref[...], FOX_DH ** -0.5, dqg_ref).astype(BF16)
        dp_ref[:, D:2 * D] = norm_bwd(p_ref[:, D:2 * D], kg_ref[...], dk_ref[...], 1.0, dkg_ref).astype(BF16)
        dp_ref[:, 2 * D:3 * D] = dv_ref[...].astype(BF16)
        dp_ref[:, 3 * D:4 * D] = dg_ref[...].astype(BF16)
        rows = jnp.concatenate([dc_ref[h, 0:1, :] for h in range(FOX_H)] + [jnp.zeros((LANES - FOX_H, LANES), F32)], axis=0)
        dlf = _sel_l(_tri(LANES, upper=True).astype(BF16), rows.T) + carry[0:1, :]
        carry[...] = jnp.broadcast_to(dlf[0:1, :], carry.shape)
        lane = _iota((LANES, LANES), 1)
        z = p_ref[:, 4 * D:4 * D + LANES] + bf_ref[...]
        df = jnp.where(lane < FOX_H, dlf * _sigmoid(-z), 0.0)
        dp_ref[:, 4 * D:4 * D + LANES] = df.astype(BF16)
        dbf_ref[...] += jnp.sum(df, axis=0, keepdims=True)

    rev = lambda i: (nb - 1 - i, 0)
    blk = pl.BlockSpec((LANES, D), rev)
    row = pl.BlockSpec((1, D), lambda i: (0, 0))
    row128 = pl.BlockSpec((1, LANES), lambda i: (0, 0))
    ins = [proj, jnp.pad(b_f, (0, LANES - FOX_H)).reshape(1, LANES), jnp.tile(q_gain, FOX_H).reshape(1, D),
           jnp.tile(k_gain, FOX_H).reshape(1, D), dqn, dkn, dv, dgate, dct]
    in_specs = [pl.BlockSpec((LANES, FOX_INP), rev), row128, row, row, blk, blk, blk, blk,
                pl.BlockSpec((FOX_H, 8, LANES), lambda i: (0, 0, nb - 1 - i))]
    out_specs = [pl.BlockSpec((LANES, FOX_INP), rev), row, row, row128]
    out_shape = [jax.ShapeDtypeStruct((lp, FOX_INP), BF16), jax.ShapeDtypeStruct((1, D), F32),
                 jax.ShapeDtypeStruct((1, D), F32), jax.ShapeDtypeStruct((1, LANES), F32)]
    scratch = [pltpu.VMEM((8, LANES), F32)]
    name = "fox_prep_bwd"
    if join is not None:
        ins, in_specs, out_specs = ins + [join], in_specs + [ANY], out_specs + [ANY]
        out_shape = out_shape + [jax.ShapeDtypeStruct(join.shape, join.dtype)]
        scratch = scratch + [pltpu.SemaphoreType.DMA, pltpu.SemaphoreType.DMA]
        name = "fox_prep_bwd_join"
    return pl.pallas_call(body, name=name, grid=(nb,), in_specs=in_specs, out_specs=out_specs, out_shape=out_shape,
                          scratch_shapes=scratch, compiler_params=_params(("arbitrary",)))(*ins)


def _gla_gates(p_ref, wa_ref, ba_ref):
    a_lr = p_ref[:, 3072:3072 + LANES]
    z = nn(a_lr.astype(BF16), wa_ref[...].astype(BF16)) + ba_ref[...]
    g = _log_sigmoid(z) * (1.0 / GLA_NORM)
    b = _sel_l(_tri(CHUNK).astype(BF16), g)
    return a_lr, z, b


def _gla_chunk_fwd(q, k, v, b, st0):
    hs = range(len(q))
    low = _tri(CHUNK)
    bl = [b[h][CHUNK - 1:CHUNK, :] for h in hs]
    qe = [q[h] * jnp.exp(b[h]) for h in hs]
    ke = [k[h] * jnp.exp(-b[h]) for h in hs]
    kd = [k[h] * jnp.exp(bl[h] - b[h]) for h in hs]
    a = [jnp.where(low, nt(qe[h], ke[h]), 0.0) for h in hs]
    o = [nn(a[h], v[h]) + nt(qe[h], st0[h]) for h in hs]
    st1 = [st0[h] * jnp.exp(bl[h]) + tn(v[h], kd[h]) for h in hs]
    return o, st1, (qe, ke, kd, a, bl)


def _gla_slices(p_ref, b_all, h):
    q = p_ref[:, h * GLA_DK:(h + 1) * GLA_DK] * (GLA_DK ** -0.5)
    k = p_ref[:, GLA_QK + h * GLA_DK:GLA_QK + (h + 1) * GLA_DK]
    v = p_ref[:, 2 * GLA_QK + h * GLA_DV:2 * GLA_QK + (h + 1) * GLA_DV]
    r = p_ref[:, 2 * GLA_QK + GLA_V + h * GLA_DV:2 * GLA_QK + GLA_V + (h + 1) * GLA_DV]
    return q, k, v, r, b_all[:, h * GLA_DK:(h + 1) * GLA_DK]


def _gla_fwd(proj, w_alpha2, b_alpha, o_gain):
    lp = proj.shape[0]
    nc = lp // CHUNK

    def body(p_ref, wa_ref, ba_ref, og_ref, o_ref, y_ref, s_ref, st):
        @pl.when(pl.program_id(0) == 0)
        def _():
            st[...] = jnp.zeros_like(st)

        _, _, b_all = _gla_gates(p_ref, wa_ref, ba_ref)
        hs = range(GLA_H)
        parts = [_gla_slices(p_ref, b_all, h) for h in hs]
        st0 = [st[h] for h in hs]
        for h in hs:
            s_ref[0, h] = st0[h]
        o, st1, _ = _gla_chunk_fwd([p[0] for p in parts], [p[1] for p in parts], [p[2] for p in parts],
                                   [p[4] for p in parts], st0)
        for h in hs:
            st[h] = st1[h]
            o_ref[:, h * GLA_DV:(h + 1) * GLA_DV] = o[h]
            rs = lax.rsqrt(jnp.mean(o[h] * o[h], axis=-1, keepdims=True) + EPS)
            y_ref[:, h * GLA_DV:(h + 1) * GLA_DV] = (o[h] * rs * og_ref[...] * _silu(parts[h][3])).astype(BF16)

    blk = pl.BlockSpec((CHUNK, D), lambda i: (i, 0))
    return pl.pallas_call(
        body, name="gla_fwd", grid=(nc,),
        in_specs=[pl.BlockSpec((CHUNK, GLA_INP), lambda i: (i, 0)), pl.BlockSpec((LANES, GLA_QK), lambda i: (0, 0)),
                  pl.BlockSpec((1, GLA_QK), lambda i: (0, 0)), pl.BlockSpec((1, GLA_DV), lambda i: (0, 0))],
        out_specs=[blk, blk, pl.BlockSpec((1, GLA_H, GLA_DV, GLA_DK), lambda i: (i, 0, 0, 0))],
        out_shape=[jax.ShapeDtypeStruct((lp, D), F32), jax.ShapeDtypeStruct((lp, D), BF16),
                   jax.ShapeDtypeStruct((nc, GLA_H, GLA_DV, GLA_DK), F32)],
        scratch_shapes=[pltpu.VMEM((GLA_H, GLA_DV, GLA_DK), F32)],
        compiler_params=_params(("arbitrary",)),
    )(proj, jnp.pad(w_alpha2, ((0, LANES - GLA_RANK), (0, 0))), b_alpha.reshape(1, GLA_QK), o_gain.reshape(1, GLA_DV))


def _gla_bwd(proj, w_alpha2, b_alpha, o_gain, o, states, dy):
    lp = proj.shape[0]
    nc = lp // CHUNK

    def body(p_ref, wa_ref, ba_ref, og_ref, o_ref, s_ref, dy_ref, dp_ref, dwa_ref, dba_ref, dog_ref, dst):
        @pl.when(pl.program_id(0) == 0)
        def _():
            dst[...] = jnp.zeros_like(dst)
            dwa_ref[...] = jnp.zeros_like(dwa_ref)
            dba_ref[...] = jnp.zeros_like(dba_ref)
            dog_ref[...] = jnp.zeros_like(dog_ref)

        a_lr, z, b_all = _gla_gates(p_ref, wa_ref, ba_ref)
        last_row = _iota((CHUNK, GLA_DK), 0) == CHUNK - 1
        rev = _tri(CHUNK, upper=True).astype(BF16)
        hs = range(GLA_H)
        scale = GLA_DK ** -0.5
        parts = [_gla_slices(p_ref, b_all, h) for h in hs]
        q, k, v, b = [p[0] for p in parts], [p[1] for p in parts], [p[2] for p in parts], [p[4] for p in parts]
        st0 = [s_ref[0, h] for h in hs]
        dst1 = [dst[h] for h in hs]
        do = []
        for h in hs:
            r = parts[h][3]
            ov = o_ref[:, h * GLA_DV:(h + 1) * GLA_DV]
            dyv = dy_ref[:, h * GLA_DV:(h + 1) * GLA_DV]
            rs = lax.rsqrt(jnp.mean(ov * ov, axis=-1, keepdims=True) + EPS)
            on = ov * rs
            dp_ref[:, 2 * GLA_QK + GLA_V + h * GLA_DV:2 * GLA_QK + GLA_V + (h + 1) * GLA_DV] = (
                dyv * on * og_ref[...] * _dsilu(r)).astype(BF16)
            don = dyv * _silu(r)
            dog_ref[...] += jnp.sum(don * on, axis=0, keepdims=True)
            u = don * og_ref[...]
            do.append(rs * u - ov * (rs * rs * rs) * jnp.mean(ov * u, axis=-1, keepdims=True))
        _, _, (qe, ke, kd, a, bl) = _gla_chunk_fwd(q, k, v, b, st0)
        low = _tri(CHUNK)
        da = [jnp.where(low, nt(do[h], v[h]), 0.0) for h in hs]
        dkd = [nn(v[h], dst1[h]) for h in hs]
        dvv = [tn(a[h], do[h]) + nt(kd[h], dst1[h]) for h in hs]
        dqe = [nn(da[h], ke[h]) + nn(do[h], st0[h]) for h in hs]
        dke = [tn(da[h], qe[h]) for h in hs]
        dg_parts = []
        for h in hs:
            ebl = jnp.exp(bl[h])
            dst[h] = dst1[h] * ebl + tn(do[h], qe[h])
            db = dqe[h] * qe[h] - dke[h] * ke[h] - dkd[h] * kd[h]
            db_last = (jnp.sum(dkd[h] * kd[h], axis=0, keepdims=True)
                       + jnp.sum(dst1[h] * st0[h], axis=0, keepdims=True) * ebl)
            db = db + jnp.where(last_row, db_last, 0.0)
            dg_parts.append(_sel_l(rev, db))
            dp_ref[:, h * GLA_DK:(h + 1) * GLA_DK] = (dqe[h] * jnp.exp(b[h]) * scale).astype(BF16)
            dp_ref[:, GLA_QK + h * GLA_DK:GLA_QK + (h + 1) * GLA_DK] = (
                dke[h] * jnp.exp(-b[h]) + dkd[h] * jnp.exp(bl[h] - b[h])).astype(BF16)
            dp_ref[:, 2 * GLA_QK + h * GLA_DV:2 * GLA_QK + (h + 1) * GLA_DV] = dvv[h].astype(BF16)
        dg = jnp.concatenate(dg_parts, axis=1)
        dz = dg * (1.0 / GLA_NORM) * _sigmoid(-z)
        dzb = dz.astype(BF16)
        dp_ref[:, 3072:3072 + LANES] = nt(dzb, wa_ref[...].astype(BF16)).astype(BF16)
        dwa_ref[...] += tn(a_lr.astype(BF16), dzb)
        dba_ref[...] += jnp.sum(dz, axis=0, keepdims=True)

    rv = lambda i: (nc - 1 - i, 0)
    blk = pl.BlockSpec((CHUNK, D), rv)
    fixed = lambda r, c: pl.BlockSpec((r, c), lambda i: (0, 0))
    return pl.pallas_call(
        body, name="gla_bwd", grid=(nc,),
        in_specs=[pl.BlockSpec((CHUNK, GLA_INP), rv), fixed(LANES, GLA_QK), fixed(1, GLA_QK), fixed(1, GLA_DV), blk,
                  pl.BlockSpec((1, GLA_H, GLA_DV, GLA_DK), lambda i: (nc - 1 - i, 0, 0, 0)), blk],
        out_specs=[pl.BlockSpec((CHUNK, GLA_INP), rv), fixed(LANES, GLA_QK), fixed(1, GLA_QK), fixed(1, GLA_DV)],
        out_shape=[jax.ShapeDtypeStruct((lp, GLA_INP), BF16), jax.ShapeDtypeStruct((LANES, GLA_QK), F32),
                   jax.ShapeDtypeStruct((1, GLA_QK), F32), jax.ShapeDtypeStruct((1, GLA_DV), F32)],
        scratch_shapes=[pltpu.VMEM((GLA_H, GLA_DV, GLA_DK), F32)],
        compiler_params=_params(("arbitrary",)),
    )(proj, jnp.pad(w_alpha2, ((0, LANES - GLA_RANK), (0, 0))), b_alpha.reshape(1, GLA_QK), o_gain.reshape(1, GLA_DV),
      o, states, dy)


HI = lax.Precision.HIGH


def _gdn_pre(prev_ref, p_ref, cw_ref, al_ref, dt_ref):
    xc = jnp.concatenate([prev_ref[:, 0:GDN_CONV], p_ref[:, 0:GDN_CONV]], axis=0)
    shifted = [pltpu.roll(xc, 3 - j, 0)[CHUNK:, :] if j < 3 else xc[CHUNK:, :] for j in range(4)]
    conv = sum(shifted[j] * cw_ref[j:j + 1, :] for j in range(4))
    act = _silu(conv)
    slab = p_ref[:, 4096:4096 + LANES]
    lane = _iota((CHUNK, LANES), 1)
    zs = slab + dt_ref[...]
    g = jnp.where(lane < GDN_H, -jnp.exp(al_ref[...]) * _softplus(zs), 0.0)
    bs = _sel_l(_tri(CHUNK).astype(BF16), g)
    beta = _sigmoid(slab)
    return shifted, conv, act, slab, zs, g, bs, beta


def _l2n(x):
    r = lax.rsqrt(jnp.sum(x * x, axis=-1, keepdims=True) + EPS)
    return x * r, r


def _gdn_chunk_fwd(q, k, v, beta, bcol, brow, s0, tinv=None):
    hs = range(len(q))
    ii, jj = _iota((CHUNK, CHUNK), 0), _iota((CHUNK, CHUNK), 1)
    low, eye = ii >= jj, (ii == jj).astype(F32)
    dm = [jnp.where(low, jnp.exp(jnp.where(low, bcol[h] - brow[h], 0.0)), 0.0) for h in hs]
    dstrict = [jnp.where(ii > jj, dm[h], 0.0) for h in hs]
    eb = [jnp.exp(bcol[h]) for h in hs]
    bl = [bcol[h][CHUNK - 1:CHUNK, :] for h in hs]
    kb = [k[h] * beta[h] for h in hs]
    vb = [v[h] * beta[h] for h in hs]
    nmat = [nt(kb[h], k[h]) * dstrict[h] for h in hs]
    if tinv is None:
        x = [eye - nmat[h] for h in hs]
        pw = [nn(nmat[h], nmat[h], precision=HI) for h in hs]
        for it in range(5):
            x = [x[h] + nn(x[h], pw[h], precision=HI) for h in hs]
            if it < 4:
                pw = [nn(pw[h], pw[h], precision=HI) for h in hs]
    else:
        x = tinv
    kbe = [kb[h] * eb[h] for h in hs]
    u = [nn(x[h], vb[h], precision=HI) for h in hs]
    w = [nn(x[h], kbe[h], precision=HI) for h in hs]
    vn = [u[h] - nn(w[h], s0[h]) for h in hs]
    pm = [nt(q[h], k[h]) * dm[h] for h in hs]
    qe = [q[h] * eb[h] for h in hs]
    o = [nn(pm[h], vn[h]) + nn(qe[h], s0[h]) for h in hs]
    kd = [k[h] * jnp.exp(bl[h] - bcol[h]) for h in hs]
    s1 = [s0[h] * jnp.exp(bl[h]) + tn(kd[h], vn[h]) for h in hs]
    return o, s1, dict(dm=dm, dstrict=dstrict, eb=eb, bl=bl, kb=kb, vb=vb, nmat=nmat, tinv=x, kbe=kbe, u=u, w=w, vn=vn,
                       pm=pm, qe=qe, kd=kd)


def _gdn_heads(act, beta_slab, bs, h):
    qa = act[:, h * GDN_DK:(h + 1) * GDN_DK]
    ka = act[:, GDN_H * GDN_DK + h * GDN_DK:GDN_H * GDN_DK + (h + 1) * GDN_DK]
    v = act[:, 2 * GDN_H * GDN_DK + h * GDN_DV:2 * GDN_H * GDN_DK + (h + 1) * GDN_DV]
    return qa, ka, v, beta_slab[:, GDN_H + h:GDN_H + h + 1], bs[:, h:h + 1]


def _gdn_fwd(proj, conv_w, a_log, dt_bias, o_gain):
    lp = proj.shape[0]
    nc = lp // CHUNK

    def body(prev_ref, p_ref, cw_ref, al_ref, dt_ref, og_ref, o_ref, y_ref, s_ref, t_ref, st):
        @pl.when(pl.program_id(0) == 0)
        def _():
            st[...] = jnp.zeros_like(st)

        _, _, act, _, _, _, bs, beta = _gdn_pre(prev_ref, p_ref, cw_ref, al_ref, dt_ref)
        bst = bs.T
        hs = range(GDN_H)
        parts = [_gdn_heads(act, beta, bs, h) for h in hs]
        q = [_l2n(parts[h][0])[0] * (GDN_DK ** -0.5) for h in hs]
        k = [_l2n(parts[h][1])[0] for h in hs]
        s0 = [st[h] for h in hs]
        for h in hs:
            s_ref[0, h] = s0[h]
        o, s1, f = _gdn_chunk_fwd(q, k, [parts[h][2] for h in hs], [parts[h][3] for h in hs], [parts[h][4] for h in hs],
                                  [bst[h:h + 1, :] for h in hs], s0)
        for h in hs:
            t_ref[0, h] = f["tinv"][h]
            st[h] = s1[h]
            o_ref[:, h * GDN_DV:(h + 1) * GDN_DV] = o[h]
            rs = lax.rsqrt(jnp.mean(o[h] * o[h], axis=-1, keepdims=True) + EPS)
            gate = p_ref[:, GDN_CONV + h * GDN_DV:GDN_CONV + (h + 1) * GDN_DV]
            y_ref[:, h * GDN_DV:(h + 1) * GDN_DV] = (o[h] * rs * og_ref[...] * _silu(gate)).astype(BF16)

    blk = pl.BlockSpec((CHUNK, D), lambda i: (i, 0))
    fixed = lambda r, c: pl.BlockSpec((r, c), lambda i: (0, 0))
    return pl.pallas_call(
        body, name="gdn_fwd", grid=(nc,),
        in_specs=[pl.BlockSpec((CHUNK, GDN_INP), lambda i: (jnp.maximum(i - 1, 0), 0)),
                  pl.BlockSpec((CHUNK, GDN_INP), lambda i: (i, 0)), fixed(8, GDN_CONV), fixed(1, LANES), fixed(1, LANES),
                  fixed(1, GDN_DV)],
        out_specs=[blk, blk, pl.BlockSpec((1, GDN_H, GDN_DK, GDN_DV), lambda i: (i, 0, 0, 0)),
                   pl.BlockSpec((1, GDN_H, CHUNK, CHUNK), lambda i: (i, 0, 0, 0))],
        out_shape=[jax.ShapeDtypeStruct((lp, D), F32), jax.ShapeDtypeStruct((lp, D), BF16),
                   jax.ShapeDtypeStruct((nc, GDN_H, GDN_DK, GDN_DV), F32), jax.ShapeDtypeStruct((nc, GDN_H, CHUNK, CHUNK), F32)],
        scratch_shapes=[pltpu.VMEM((GDN_H, GDN_DK, GDN_DV), F32)],
        compiler_params=_params(("arbitrary",)),
    )(proj, proj, jnp.pad(conv_w.reshape(4, GDN_CONV), ((0, 4), (0, 0))), jnp.pad(a_log, (0, LANES - GDN_H)).reshape(1, LANES),
      jnp.pad(dt_bias, (0, LANES - GDN_H)).reshape(1, LANES), o_gain.reshape(1, GDN_DV))


def _gdn_bwd(proj, conv_w, a_log, dt_bias, o_gain, o, states, tinvs, dy):
    lp = proj.shape[0]
    nc = lp // CHUNK

    def body(prev_ref, p_ref, cw_ref, al_ref, dt_ref, og_ref, o_ref, s_ref, t_ref, dy_ref,
             dp_ref, dcw_ref, dal_ref, ddt_ref, dog_ref, dst, dconv_next):
        @pl.when(pl.program_id(0) == 0)
        def _():
            dst[...] = jnp.zeros_like(dst)
            dconv_next[...] = jnp.zeros_like(dconv_next)
            dcw_ref[...] = jnp.zeros_like(dcw_ref)
            dal_ref[...] = jnp.zeros_like(dal_ref)
            ddt_ref[...] = jnp.zeros_like(ddt_ref)
            dog_ref[...] = jnp.zeros_like(dog_ref)

        shifted, conv, act, slab, zs, g, bs, beta = _gdn_pre(prev_ref, p_ref, cw_ref, al_ref, dt_ref)
        bst = bs.T
        lane = _iota((CHUNK, LANES), 1)
        ones = jnp.ones((CHUNK, LANES), F32)
        db_slab = jnp.zeros((CHUNK, LANES), F32)
        dbeta_slab = jnp.zeros((CHUNK, LANES), F32)
        last_row = _iota((CHUNK, 1), 0) == CHUNK - 1
        hs = range(GDN_H)
        scale = GDN_DK ** -0.5
        parts = [_gdn_heads(act, beta, bs, h) for h in hs]
        qa, ka, v = [parts[h][0] for h in hs], [parts[h][1] for h in hs], [parts[h][2] for h in hs]
        bet, bcol = [parts[h][3] for h in hs], [parts[h][4] for h in hs]
        qn_ = [_l2n(qa[h]) for h in hs]
        kn_ = [_l2n(ka[h]) for h in hs]
        q = [qn_[h][0] * scale for h in hs]
        k, rq, rk = [kn_[h][0] for h in hs], [qn_[h][1] for h in hs], [kn_[h][1] for h in hs]
        s0 = [s_ref[0, h] for h in hs]
        ds1 = [dst[h] for h in hs]
        do = []
        for h in hs:
            ov = o_ref[:, h * GDN_DV:(h + 1) * GDN_DV]
            dyv = dy_ref[:, h * GDN_DV:(h + 1) * GDN_DV]
            gate = p_ref[:, GDN_CONV + h * GDN_DV:GDN_CONV + (h + 1) * GDN_DV]
            rs = lax.rsqrt(jnp.mean(ov * ov, axis=-1, keepdims=True) + EPS)
            on = ov * rs
            dp_ref[:, GDN_CONV + h * GDN_DV:GDN_CONV + (h + 1) * GDN_DV] = (dyv * on * og_ref[...] * _dsilu(gate)).astype(BF16)
            don = dyv * _silu(gate)
            dog_ref[...] += jnp.sum(don * on, axis=0, keepdims=True)
            uu = don * og_ref[...]
            do.append(rs * uu - ov * (rs * rs * rs) * jnp.mean(ov * uu, axis=-1, keepdims=True))
        _, _, f = _gdn_chunk_fwd(q, k, v, bet, bcol, [bst[h:h + 1, :] for h in hs], s0, tinv=[t_ref[0, h] for h in hs])
        dm, dstrict, eb, bl, kb, nmat, tinv = f["dm"], f["dstrict"], f["eb"], f["bl"], f["kb"], f["nmat"], f["tinv"]
        kbe, u, w, vn, pm, qe, kd = f["kbe"], f["u"], f["w"], f["vn"], f["pm"], f["qe"], f["kd"]
        ebl = [jnp.exp(bl[h]) for h in hs]
        dvn = [tn(pm[h], do[h]) + nn(kd[h], ds1[h]) for h in hs]
        dpr = [nt(do[h], vn[h]) for h in hs]
        dqe = [nt(do[h], s0[h]) for h in hs]
        dkd = [nt(vn[h], ds1[h]) for h in hs]
        for h in hs:
            dst[h] = ds1[h] * ebl[h] + tn(qe[h], do[h]) - tn(w[h], dvn[h])
        du_ = [tn(tinv[h], dvn[h], precision=HI) for h in hs]
        dw_ = [tn(tinv[h], -nt(dvn[h], s0[h]), precision=HI) for h in hs]
        dn = [-(nt(du_[h], u[h]) + nt(dw_[h], w[h])) for h in hs]
        dqk = [dpr[h] * dm[h] for h in hs]
        dkk = [dn[h] * dstrict[h] for h in hs]
        gsum = [dpr[h] * pm[h] + dn[h] * nmat[h] for h in hs]
        dkb = [nn(dkk[h], k[h]) + dw_[h] * eb[h] for h in hs]
        dk = [tn(dkk[h], kb[h]) + tn(dqk[h], q[h]) + dkd[h] * jnp.exp(bl[h] - bcol[h]) + dkb[h] * bet[h] for h in hs]
        dq = [nn(dqk[h], k[h]) + dqe[h] * eb[h] for h in hs]
        colsum = [tn(gsum[h], ones, precision=HI)[:, 0:1] for h in hs]
        dact_q, dact_k, dact_v = [], [], []
        for h in hs:
            dbeta = jnp.sum(dkb[h] * k[h], axis=-1, keepdims=True) + jnp.sum(du_[h] * v[h], axis=-1, keepdims=True)
            skd = jnp.sum(dkd[h] * kd[h], axis=-1, keepdims=True)
            db = (jnp.sum(gsum[h], axis=-1, keepdims=True) - colsum[h] + jnp.sum(dqe[h] * qe[h], axis=-1, keepdims=True)
                  + jnp.sum(dw_[h] * kbe[h], axis=-1, keepdims=True) - skd)
            db_last = jnp.sum(skd, axis=0, keepdims=True) + jnp.sum(ds1[h] * s0[h]) * ebl[h]
            db = db + jnp.where(last_row, db_last, 0.0)
            db_slab = db_slab + jnp.where(lane == h, db, 0.0)
            dbeta_slab = dbeta_slab + jnp.where(lane == GDN_H + h, dbeta, 0.0)
            dqn = dq[h] * scale
            dact_q.append(rq[h] * dqn - qa[h] * (rq[h] * rq[h] * rq[h]) * jnp.sum(qa[h] * dqn, axis=-1, keepdims=True))
            dact_k.append(rk[h] * dk[h] - ka[h] * (rk[h] * rk[h] * rk[h]) * jnp.sum(ka[h] * dk[h], axis=-1, keepdims=True))
            dact_v.append(du_[h] * bet[h])
        dact = jnp.concatenate(dact_q + dact_k + dact_v, axis=1)
        dconv = dact * _dsilu(conv)
        for j in range(4):
            dcw_ref[j:j + 1, :] += jnp.sum(dconv * shifted[j], axis=0, keepdims=True)
        dcat = jnp.concatenate([dconv, dconv_next[...]], axis=0)
        dx = dconv * cw_ref[3:4, :]
        for j in range(3):
            dx = dx + pltpu.roll(dcat, 2 * CHUNK - (3 - j), 0)[:CHUNK, :] * cw_ref[j:j + 1, :]
        dconv_next[...] = dconv
        dp_ref[:, 0:GDN_CONV] = dx.astype(BF16)
        dg = _sel_l(_tri(CHUNK, upper=True).astype(BF16), db_slab)
        da = dg * (-jnp.exp(al_ref[...])) * _sigmoid(zs)
        da = jnp.where(lane < GDN_H, da, 0.0)
        dal_ref[...] += jnp.sum(dg * g, axis=0, keepdims=True)
        ddt_ref[...] += jnp.sum(da, axis=0, keepdims=True)
        dp_ref[:, 4096:4096 + LANES] = (da + dbeta_slab * beta * (1.0 - beta)).astype(BF16)

    rv = lambda i: (nc - 1 - i, 0)
    blk = pl.BlockSpec((CHUNK, D), rv)
    fixed = lambda r, c: pl.BlockSpec((r, c), lambda i: (0, 0))
    return pl.pallas_call(
        body, name="gdn_bwd", grid=(nc,),
        in_specs=[pl.BlockSpec((CHUNK, GDN_INP), lambda i: (jnp.maximum(nc - 2 - i, 0), 0)),
                  pl.BlockSpec((CHUNK, GDN_INP), rv), fixed(8, GDN_CONV), fixed(1, LANES), fixed(1, LANES), fixed(1, GDN_DV),
                  blk, pl.BlockSpec((1, GDN_H, GDN_DK, GDN_DV), lambda i: (nc - 1 - i, 0, 0, 0)),
                  pl.BlockSpec((1, GDN_H, CHUNK, CHUNK), lambda i: (nc - 1 - i, 0, 0, 0)), blk],
        out_specs=[pl.BlockSpec((CHUNK, GDN_INP), rv), fixed(8, GDN_CONV), fixed(1, LANES), fixed(1, LANES), fixed(1, GDN_DV)],
        out_shape=[jax.ShapeDtypeStruct((lp, GDN_INP), BF16), jax.ShapeDtypeStruct((8, GDN_CONV), F32),
                   jax.ShapeDtypeStruct((1, LANES), F32), jax.ShapeDtypeStruct((1, LANES), F32),
                   jax.ShapeDtypeStruct((1, GDN_DV), F32)],
        scratch_shapes=[pltpu.VMEM((GDN_H, GDN_DK, GDN_DV), F32), pltpu.VMEM((CHUNK, GDN_CONV), F32)],
        compiler_params=_params(("arbitrary",)),
    )(proj, proj, jnp.pad(conv_w.reshape(4, GDN_CONV), ((0, 4), (0, 0))), jnp.pad(a_log, (0, LANES - GDN_H)).reshape(1, LANES),
      jnp.pad(dt_bias, (0, LANES - GDN_H)).reshape(1, LANES), o_gain.reshape(1, GDN_DV), o, states, tinvs, dy)


def _coords():
    return lax.axis_index("x"), lax.axis_index("y"), lax.axis_index("c")


def _other_chips(x, y):
    return [(1 - x, y, 2 * (1 - x) + y), (x, 1 - y, 2 * x + 1 - y), (1 - x, 1 - y, 2 * (1 - x) + 1 - y)]


def _gather8(v, *, reduce, name):
    r, c = v.shape

    def body(v_ref, out_ref, *scratch):
        if reduce:
            buf, send_sems, recv_sems = scratch
        else:
            buf = out_ref
            send_sems, recv_sems = scratch
        x, y, cc = _coords()
        me = 4 * x + 2 * y + cc
        buf[me] = v_ref[...]
        copies = []
        for k in range(1, 8):
            px, py, pc = x ^ (k >> 2), y ^ ((k >> 1) & 1), cc ^ (k & 1)
            copies.append(pltpu.make_async_remote_copy(
                src_ref=v_ref, dst_ref=buf.at[me], send_sem=send_sems.at[k - 1], recv_sem=recv_sems.at[k - 1],
                device_id=(px, py, pc), device_id_type=MESH))
        for cp in copies:
            cp.start()
        for k in range(1, 8):
            peer = (x ^ (k >> 2)) * 4 + (y ^ ((k >> 1) & 1)) * 2 + (cc ^ (k & 1))
            pltpu.make_async_remote_copy(
                src_ref=v_ref, dst_ref=buf.at[peer], send_sem=send_sems.at[k - 1], recv_sem=recv_sems.at[k - 1],
                device_id=(x, y, cc), device_id_type=MESH).wait_recv()
        for cp in copies:
            cp.wait_send()
        if reduce:
            acc = buf[0]
            for d in range(1, 8):
                acc = acc + buf[d]
            out_ref[...] = acc

    scratch = [pltpu.SemaphoreType.DMA((7,)), pltpu.SemaphoreType.DMA((7,))]
    if reduce:
        scratch = [pltpu.VMEM((8, r, c), F32)] + scratch
    return pl.pallas_call(
        body, name=name, in_specs=[VM], out_specs=VM,
        out_shape=jax.ShapeDtypeStruct((r, c) if reduce else (8, r, c), F32),
        scratch_shapes=scratch, compiler_params=_params(),
    )(v)


class _AgCopies:
    def __init__(self, buf, ranges, send_sems, recv_sems):
        self.buf, self.ranges, self.send_sems, self.recv_sems = buf, ranges, send_sems, recv_sems
        self.x, self.y, self.cc = _coords()
        self.p = 2 * self.x + self.y
        self.chips = _other_chips(self.x, self.y)

    def rows(self, chip, r, hf):
        start, n = self.ranges[r]
        return self.buf.at[chip, pl.ds(start + hf * (n // 2), n // 2), :]

    def _copy(self, r, k, chip, hf, to):
        return pltpu.make_async_remote_copy(
            src_ref=self.rows(chip, r, hf), dst_ref=self.rows(chip, r, hf), send_sem=self.send_sems.at[3 * r + k],
            recv_sem=self.recv_sems.at[3 * r + k], device_id=to, device_id_type=MESH)

    def pairs(self):
        return [(r, k) for r in range(len(self.ranges)) for k in range(3)]

    def ici(self, r, k):
        cx, cy, _ = self.chips[k]
        return self._copy(r, k, self.p, self.cc, (cx, cy, self.cc))

    def ici_arrival(self, r, k):
        return self._copy(r, k, self.chips[k][2], self.cc, (self.x, self.y, self.cc))

    def forward(self, r, k):
        return self._copy(r, k, self.chips[k][2], self.cc, (self.x, self.y, 1 - self.cc))

    def forward_arrival(self, r, k):
        return self._copy(r, k, self.chips[k][2], 1 - self.cc, (self.x, self.y, self.cc))


def _ag_weights(w4, ranges):
    n = 3 * len(ranges)

    def body(w_ref, out_ref, send1, recv1, send2, recv2):
        ici, fwd = _AgCopies(out_ref, ranges, send1, recv1), _AgCopies(out_ref, ranges, send2, recv2)
        for r, k in ici.pairs():
            ici.ici(r, k).start()
        for r, k in ici.pairs():
            ici.ici_arrival(r, k).wait_recv()
            fwd.forward(r, k).start()
        for r, k in ici.pairs():
            fwd.forward_arrival(r, k).wait_recv()
        for r, k in ici.pairs():
            ici.ici(r, k).wait_send()
            fwd.forward(r, k).wait_send()

    return pl.pallas_call(
        body, name="ag_weights", in_specs=[ANY], out_specs=ANY, out_shape=jax.ShapeDtypeStruct(w4.shape, w4.dtype),
        scratch_shapes=[pltpu.SemaphoreType.DMA((n,))] * 4, input_output_aliases={0: 0}, compiler_params=_params(),
    )(w4)


def _swap_copy(g_ref, out_ref, send_sem, recv_sem):
    x, y, cc = _coords()
    half = g_ref.shape[1] // 2
    return pltpu.make_async_remote_copy(
        src_ref=g_ref.at[:, pl.ds((1 - cc) * half, half), :], dst_ref=out_ref, send_sem=send_sem, recv_sem=recv_sem,
        device_id=(x, y, 1 - cc), device_id_type=MESH)


def _swap_halves(g, *, name):
    nb, r, c = g.shape
    half = r // 2

    def body(g_ref, out_ref, send_sem, recv_sem):
        cp = _swap_copy(g_ref, out_ref, send_sem, recv_sem)
        cp.start()
        cp.wait()

    return pl.pallas_call(
        body, name=name, in_specs=[ANY], out_specs=ANY, out_shape=jax.ShapeDtypeStruct((nb, half, c), g.dtype),
        scratch_shapes=[pltpu.SemaphoreType.DMA, pltpu.SemaphoreType.DMA], compiler_params=_params(),
    )(g)


def _my_half_index():
    return lax.axis_index("c").astype(jnp.int32).reshape(1)


def _add_halves(g, got, tag):
    nb, r, c = g.shape
    half = r // 2
    tr = _tile(half, 512, 16)
    nt_ = half // tr

    def body(c_ref, a_ref, b_ref, o_ref):
        o_ref[...] = (a_ref[...].astype(F32) + b_ref[...].astype(F32)).astype(BF16)

    return pl.pallas_call(
        body, name=f"rs_add_sibling{tag}",
        grid_spec=pltpu.PrefetchScalarGridSpec(
            num_scalar_prefetch=1, grid=(nb, nt_),
            in_specs=[pl.BlockSpec((1, tr, c), lambda b, i, cr: (b, cr[0] * nt_ + i, 0)),
                      pl.BlockSpec((1, tr, c), lambda b, i, cr: (b, i, 0))],
            out_specs=pl.BlockSpec((1, tr, c), lambda b, i, cr: (b, i, 0))),
        out_shape=jax.ShapeDtypeStruct((nb, half, c), BF16), compiler_params=_params(("parallel", "parallel")),
    )(_my_half_index(), g, got)


def _scatter_copies(s_ref, out_ref, send_sems, recv_sems):
    x, y, cc = _coords()
    sends = [pltpu.make_async_remote_copy(
        src_ref=s_ref.at[blk], dst_ref=out_ref.at[k], send_sem=send_sems.at[k], recv_sem=recv_sems.at[k],
        device_id=(cx, cy, cc), device_id_type=MESH) for k, (cx, cy, blk) in enumerate(_other_chips(x, y))]
    arrivals = [pltpu.make_async_remote_copy(
        src_ref=s_ref.at[2 * x + y], dst_ref=out_ref.at[k], send_sem=send_sems.at[k], recv_sem=recv_sems.at[k],
        device_id=(x, y, cc), device_id_type=MESH) for k in range(3)]
    return sends, arrivals


def _sum_chips(s, got, tag):
    nb, hrows, c = s.shape
    tr = _tile(hrows, 512, 16)

    def body(idx_ref, own_ref, got_ref, o_ref):
        p = idx_ref[0]
        own = own_ref[0].astype(F32)
        parts = [got_ref[k].astype(F32) for k in range(3)]
        acc = jnp.zeros_like(own)
        for q in range(4):
            val = own
            for k, rel in enumerate((2, 1, 3)):
                val = jnp.where((p ^ rel) == q, parts[k], val)
            acc = acc + val
        o_ref[...] = acc

    idx = (2 * lax.axis_index("x") + lax.axis_index("y")).astype(jnp.int32).reshape(1)
    return pl.pallas_call(
        body, name=f"rs_sum_chips{tag}",
        grid_spec=pltpu.PrefetchScalarGridSpec(
            num_scalar_prefetch=1, grid=(hrows // tr,),
            in_specs=[pl.BlockSpec((1, tr, c), lambda i, pr: (pr[0], i, 0)), pl.BlockSpec((3, tr, c), lambda i, pr: (0, i, 0))],
            out_specs=pl.BlockSpec((tr, c), lambda i, pr: (i, 0))),
        out_shape=jax.ShapeDtypeStruct((hrows, c), F32), compiler_params=_params(("parallel",)),
    )(idx, s, got)


def _join_copy(t_ref, out_ref, send_sem, recv_sem):
    x, y, cc = _coords()
    return pltpu.make_async_remote_copy(src_ref=t_ref, dst_ref=out_ref, send_sem=send_sem, recv_sem=recv_sem,
                                        device_id=(x, y, 1 - cc), device_id_type=MESH)


def _swap_sibling(t, tag):
    def body(t_ref, out_ref, send_sem, recv_sem):
        cp = _join_copy(t_ref, out_ref, send_sem, recv_sem)
        cp.start()
        cp.wait()

    return pl.pallas_call(
        body, name=f"rs_join{tag}", in_specs=[ANY], out_specs=ANY, out_shape=jax.ShapeDtypeStruct(t.shape, t.dtype),
        scratch_shapes=[pltpu.SemaphoreType.DMA, pltpu.SemaphoreType.DMA], compiler_params=_params(),
    )(t)


def _rs_local(g, tag):
    return _add_halves(g, _swap_halves(g, name=f"rs_swap{tag}"), tag)


def _both_halves(t, r):
    first = lax.axis_index("c") == 0
    return jnp.concatenate([jnp.where(first, t, r), jnp.where(first, r, t)], axis=0)


def _rs_finish(s, recv, tag):
    t = _sum_chips(s, recv, tag)
    return _both_halves(t, _swap_sibling(t, tag))


_SMALL_SHARDED = (("meta_tokens", 1), ("gla_w_alpha2", 2), ("gdn_conv_w", 3))
_REPLICATED = ("norm_mix", "norm_ffn", "fox_b_f", "fox_q_gain", "fox_k_gain", "gla_b_alpha", "gla_o_gain",
               "gdn_a_log", "gdn_dt_bias", "gdn_o_gain")
_WEIGHTS = ("meta_tokens", "norm_mix", "norm_ffn", "w_gate_up", "w_down", "fox_w_in", "fox_b_f", "fox_q_gain",
            "fox_k_gain", "fox_w_out", "gla_w_in", "gla_w_alpha2", "gla_b_alpha", "gla_o_gain", "gla_w_out",
            "gdn_w_in", "gdn_conv_w", "gdn_a_log", "gdn_dt_bias", "gdn_o_gain", "gdn_w_out")
_PACK_ROWS = 512
_IN_W = ("fox_w_in", "gla_w_in", "gdn_w_in")
_OUT_W = ("fox_w_out", "gla_w_out", "gdn_w_out")


def _piece_rows(n):
    return -(-n // 32) * 32


def _pack(arrays, width, row_mult, dtype):
    flat = jnp.concatenate([a.astype(dtype).reshape(-1) for a in arrays])
    per = width * row_mult
    n = -(-flat.shape[0] // per) * per
    return jnp.pad(flat, (0, n - flat.shape[0])).reshape(n // width, width)


def _unpack(flat, shapes):
    out, off = [], 0
    for s in shapes:
        n = 1
        for d in s:
            n *= d
        out.append(flat[off:off + n].reshape(s))
        off += n
    return out


def _unpack_cols(flat2, shapes):
    out, off = [], 0
    for s in shapes:
        n = 1
        for d in s:
            n *= d
        out.append(flat2[:, off:off + n].reshape((flat2.shape[0],) + tuple(s)))
        off += n
    return out


def kernel(x, meta_tokens, norm_mix, norm_ffn, w_gate_up, w_down, fox_w_in, fox_b_f, fox_q_gain, fox_k_gain, fox_w_out, gla_w_in, gla_w_alpha2, gla_b_alpha, gla_o_gain, gla_w_out, gdn_w_in, gdn_conv_w, gdn_a_log, gdn_dt_bias, gdn_o_gain, gdn_w_out, loss_target, m_meta_tokens, m_norm_mix, m_norm_ffn, m_w_gate_up, m_w_down, m_fox_w_in, m_fox_b_f, m_fox_q_gain, m_fox_k_gain, m_fox_w_out, m_gla_w_in, m_gla_w_alpha2, m_gla_b_alpha, m_gla_o_gain, m_gla_w_out, m_gdn_w_in, m_gdn_conv_w, m_gdn_a_log, m_gdn_dt_bias, m_gdn_o_gain, m_gdn_w_out, v_meta_tokens, v_norm_mix, v_norm_ffn, v_w_gate_up, v_w_down, v_fox_w_in, v_fox_b_f, v_fox_q_gain, v_fox_k_gain, v_fox_w_out, v_gla_w_in, v_gla_w_alpha2, v_gla_b_alpha, v_gla_o_gain, v_gla_w_out, v_gdn_w_in, v_gdn_conv_w, v_gdn_a_log, v_gdn_dt_bias, v_gdn_o_gain, v_gdn_w_out):
    W = dict(meta_tokens=meta_tokens, norm_mix=norm_mix, norm_ffn=norm_ffn, w_gate_up=w_gate_up, w_down=w_down,
             fox_w_in=fox_w_in, fox_b_f=fox_b_f, fox_q_gain=fox_q_gain, fox_k_gain=fox_k_gain, fox_w_out=fox_w_out,
             gla_w_in=gla_w_in, gla_w_alpha2=gla_w_alpha2, gla_b_alpha=gla_b_alpha, gla_o_gain=gla_o_gain,
             gla_w_out=gla_w_out, gdn_w_in=gdn_w_in, gdn_conv_w=gdn_conv_w, gdn_a_log=gdn_a_log,
             gdn_dt_bias=gdn_dt_bias, gdn_o_gain=gdn_o_gain, gdn_w_out=gdn_w_out)
    M = dict(meta_tokens=m_meta_tokens, norm_mix=m_norm_mix, norm_ffn=m_norm_ffn, w_gate_up=m_w_gate_up, w_down=m_w_down,
             fox_w_in=m_fox_w_in, fox_b_f=m_fox_b_f, fox_q_gain=m_fox_q_gain, fox_k_gain=m_fox_k_gain,
             fox_w_out=m_fox_w_out, gla_w_in=m_gla_w_in, gla_w_alpha2=m_gla_w_alpha2, gla_b_alpha=m_gla_b_alpha,
             gla_o_gain=m_gla_o_gain, gla_w_out=m_gla_w_out, gdn_w_in=m_gdn_w_in, gdn_conv_w=m_gdn_conv_w,
             gdn_a_log=m_gdn_a_log, gdn_dt_bias=m_gdn_dt_bias, gdn_o_gain=m_gdn_o_gain, gdn_w_out=m_gdn_w_out)
    V = dict(meta_tokens=v_meta_tokens, norm_mix=v_norm_mix, norm_ffn=v_norm_ffn, w_gate_up=v_w_gate_up, w_down=v_w_down,
             fox_w_in=v_fox_w_in, fox_b_f=v_fox_b_f, fox_q_gain=v_fox_q_gain, fox_k_gain=v_fox_k_gain,
             fox_w_out=v_fox_w_out, gla_w_in=v_gla_w_in, gla_w_alpha2=v_gla_w_alpha2, gla_b_alpha=v_gla_b_alpha,
             gla_o_gain=v_gla_o_gain, gla_w_out=v_gla_w_out, gdn_w_in=v_gdn_w_in, gdn_conv_w=v_gdn_conv_w,
             gdn_a_log=v_gdn_a_log, gdn_dt_bias=v_gdn_dt_bias, gdn_o_gain=v_gdn_o_gain, gdn_w_out=v_gdn_w_out)
    chip = 2 * lax.axis_index("x") + lax.axis_index("y")

    pieces, offs, r = [], {}, FFN_ROWS
    for n in _IN_W:
        nc = W[n].shape[2]
        for l in range(W[n].shape[0]):
            pieces.append(jnp.pad(W[n][l].T.astype(BF16), ((0, _piece_rows(nc) - nc), (0, 0))))
            offs[n, l] = r
            r += _piece_rows(nc)
    for n in _OUT_W:
        for l in range(W[n].shape[0]):
            pieces.append(W[n][l].astype(BF16))
            offs[n, l] = r
            r += W[n].shape[1]
    rows = -(-r // _PACK_ROWS) * _PACK_ROWS
    packed = jnp.concatenate([jnp.swapaxes(w_gate_up, 1, 2).reshape(-1, D).astype(BF16), w_down.reshape(-1, D).astype(BF16)]
                             + pieces + [jnp.zeros((rows - r, D), BF16)], axis=0)
    first_rows = [(offs["fox_w_in", 0], offs["fox_w_in", 1] - offs["fox_w_in", 0])]
    later_rows = [(0, FFN_ROWS), (offs["fox_w_in", 1], r - offs["fox_w_in", 1])]
    wpk = _ag_weights(lax.dynamic_update_slice(lax.empty((4, rows, D), BF16), packed[None], (chip, 0, 0)), first_rows)

    def in_t(buf, n, l, npad):
        nc = W[n].shape[2]
        return jnp.concatenate([buf[q, offs[n, l]:offs[n, l] + nc] for q in range(4)] + [jnp.zeros((npad - 4 * nc, D), BF16)], 0)

    def out_w(buf, n, l):
        return jnp.concatenate([buf[q, offs[n, l]:offs[n, l] + W[n].shape[1]] for q in range(4)], axis=0)

    fox_in0 = in_t(wpk, "fox_w_in", 0, FOX_INP)
    full = {}
    small = _pack([W[n] for n, _ in _SMALL_SHARDED], LANES, 8, F32)
    small_all = _gather8(small, reduce=False, name="gather_small").reshape(8, -1)
    for (n, ax), seg in zip(_SMALL_SHARDED, _unpack_cols(small_all, [W[n].shape for n, _ in _SMALL_SHARDED])):
        full[n] = jnp.concatenate([seg[2 * q] for q in range(4)], axis=ax)
    fox_in = [fox_in0]
    w_alpha2, conv_w = full["gla_w_alpha2"][0], full["gdn_conv_w"][0]

    h = jnp.concatenate([jnp.zeros((META0, D), F32), full["meta_tokens"], x[0]], axis=0)
    saved = []
    y = _rms_fwd(h, norm_mix[0], name="norm_mix0")
    for i in range(DEPTH):
        kind, j = i % 3, i // 3
        if kind == 0:
            proj = _mm(y, fox_in[j], tb=True, name=f"fox_in{j}")
            qa, ka, va = _fox_prep(proj, fox_b_f[j], fox_q_gain[j], fox_k_gain[j])
            if i == 0:
                o, og, lse, wpk = _fox_attn_fwd(qa, ka, va, proj, ag=(wpk, later_rows))
                fox_in += [in_t(wpk, "fox_w_in", l, FOX_INP) for l in range(1, fox_w_in.shape[0])]
                full["fox_w_out"] = [out_w(wpk, "fox_w_out", l) for l in range(fox_w_out.shape[0])]
                gla_in = [in_t(wpk, "gla_w_in", l, GLA_INP) for l in range(gla_w_in.shape[0])]
                gdn_in = [in_t(wpk, "gdn_w_in", l, GDN_INP) for l in range(gdn_w_in.shape[0])]
                for n in ("gla_w_out", "gdn_w_out"):
                    full[n] = [out_w(wpk, n, l) for l in range(W[n].shape[0])]
            else:
                o, og, lse = _fox_attn_fwd(qa, ka, va, proj)
            w_out, mix = full["fox_w_out"][j], (proj, qa, ka, va, o, lse)
        elif kind == 1:
            proj = _mm(y, gla_in[j], tb=True, name=f"gla_in{j}")
            o, og, states = _gla_fwd(proj, w_alpha2, gla_b_alpha[j], gla_o_gain[j])
            w_out, mix = full["gla_w_out"][j], (proj, o, states)
        else:
            proj = _mm(y, gdn_in[j], tb=True, name=f"gdn_in{j}")
            o, og, states, tinvs = _gdn_fwd(proj, conv_w, gdn_a_log[j], gdn_dt_bias[j], gdn_o_gain[j])
            w_out, mix = full["gdn_w_out"][j], (proj, o, states, tinvs)
        hm, yf = _mm(og, w_out, add=h, norm=norm_ffn[i], name=f"mix_out{i}")
        gate, up, act = _ffn_up(yf, wpk, i)
        hn, y_next = _ffn_down(act, wpk, i, hm, norm_mix[(i + 1) % DEPTH])
        saved.append((h, y, mix, og, w_out, hm, yf, gate, up, act))
        h, y = hn, y_next
    dh, loss_tile = _loss_head(h, loss_target[0])

    G = {n: [None] * W[n].shape[0] for n in _WEIGHTS if n not in ("meta_tokens", "w_gate_up", "w_down") + _IN_W}
    GT = {}

    def grad_layout(ffn_layers, pieces):
        off, end = {}, 0
        for l in ffn_layers:
            off["gu", l] = end
            end += GU_ROWS
        for l in ffn_layers:
            off["down", l] = end
            end += DOWN_ROWS
        for n, l in pieces:
            off[n, l] = end
            end += _piece_rows(W[n].shape[2]) if n in _IN_W else W[n].shape[1]
        return off, end, -(-end // _PACK_ROWS) * _PACK_ROWS

    first_pieces = [("fox_w_in", 0)]
    later_pieces = [(n, l) for n in _IN_W + _OUT_W for l in range(W[n].shape[0]) if (n, l) not in first_pieces]
    layouts = [grad_layout([], first_pieces), grad_layout(list(range(DEPTH)), later_pieces)]
    gbuf = [jnp.zeros((4, lay[2], D), BF16) for lay in layouts]

    def with_pieces(buf, lay, pieces):
        off, end, total = lay
        blocks = []
        for q in range(4):
            parts = []
            for n, l in pieces:
                if n in _IN_W:
                    nc = W[n].shape[2]
                    parts.append(jnp.pad(GT[n, l][q * nc:(q + 1) * nc], ((0, _piece_rows(nc) - nc), (0, 0))))
                else:
                    nr = W[n].shape[1]
                    parts.append(G[n][l][q * nr:(q + 1) * nr])
            blocks.append(jnp.concatenate(parts + [jnp.zeros((total - end, D), BF16)], axis=0))
        return lax.dynamic_update_slice(buf, jnp.stack(blocks), (0, off[pieces[0]], 0))

    s_later = None
    for i in reversed(range(DEPTH)):
        kind, j = i % 3, i // 3
        h_in, y, mix, og, w_out, hm, yf, gate, up, act = saved[i]
        b = 1
        dg, du = _ffn_dact(dh, wpk, i, gate, up)
        gbuf[b] = _ffn_dw_down(act, dh, gbuf[b], i, layouts[b][0]["down", i])
        dhm, dnf = _ffn_dyf(dg, du, wpk, i, hm, norm_ffn[i], dh)
        gbuf[b] = _ffn_dw_gu(dg, du, yf, gbuf[b], i, layouts[b][0]["gu", i] // GU_ROWS)
        G["norm_ffn"][i] = dnf[0]
        dog = _mm(dhm, w_out, tb=True, name=f"d_og{i}")
        dw_out = _mm(og, dhm, ta=True, out_dtype=BF16, name=f"d_w_out{i}")
        if kind == 0:
            proj, qa, ka, va, o, lse = mix
            G["fox_w_out"][j] = dw_out
            if i == 0:
                g_later = with_pieces(gbuf[1], layouts[1], later_pieces)
                doa, q2, dgate, got = _fox_gate_bwd(dog, o, proj, lse, qa, swap=g_later)
                s_later = _add_halves(g_later, got, "_later")
                dqn, dkn, dv, dct, recv_later = _fox_attn_bwd(q2, ka, va, doa, rs=s_later)
                t_later = _sum_chips(s_later, recv_later, "_later")
                dproj, dqg, dkg, dbf, r_later = _fox_prep_bwd(proj, fox_b_f[j], fox_q_gain[j], fox_k_gain[j], dqn, dkn, dv,
                                                              dgate, dct, join=t_later)
            else:
                doa, q2, dgate = _fox_gate_bwd(dog, o, proj, lse, qa)
                dqn, dkn, dv, dct = _fox_attn_bwd(q2, ka, va, doa)
                dproj, dqg, dkg, dbf = _fox_prep_bwd(proj, fox_b_f[j], fox_q_gain[j], fox_k_gain[j], dqn, dkn, dv, dgate, dct)
            G["fox_q_gain"][j] = dqg.reshape(FOX_H, FOX_DH).sum(0)
            G["fox_k_gain"][j] = dkg.reshape(FOX_H, FOX_DH).sum(0)
            G["fox_b_f"][j] = dbf[0, :FOX_H]
            w_in, wname = fox_in[j], "fox_w_in"
        elif kind == 1:
            proj, o, states = mix
            dproj, dwa, dba, dogain = _gla_bwd(proj, w_alpha2, gla_b_alpha[j], gla_o_gain[j], o, states, dog)
            G["gla_w_out"][j] = dw_out
            G["gla_w_alpha2"][j] = dwa[:GLA_RANK]
            G["gla_b_alpha"][j] = dba[0]
            G["gla_o_gain"][j] = dogain[0]
            w_in, wname = gla_in[j], "gla_w_in"
        else:
            proj, o, states, tinvs = mix
            dproj, dcw, dal, ddt, dogain = _gdn_bwd(proj, conv_w, gdn_a_log[j], gdn_dt_bias[j], gdn_o_gain[j], o, states,
                                                    tinvs, dog)
            G["gdn_w_out"][j] = dw_out
            G["gdn_conv_w"][j] = dcw[:4].reshape(4, 1, GDN_CONV)
            G["gdn_a_log"][j] = dal[0, :GDN_H]
            G["gdn_dt_bias"][j] = ddt[0, :GDN_H]
            G["gdn_o_gain"][j] = dogain[0]
            w_in, wname = gdn_in[j], "gdn_w_in"
        GT[wname, j] = _mm(dproj, y, ta=True, out_dtype=BF16, name=f"d_w_in{i}")
        if i == 0:
            s_first = _rs_local(with_pieces(gbuf[0], layouts[0], first_pieces), "_first")
            dh, dnm, recv_first = _mm(dproj, w_in, rms_bwd=(h_in, norm_mix[i], dhm), rs=s_first, name=f"d_y{i}")
        else:
            dh, dnm = _mm(dproj, w_in, rms_bwd=(h_in, norm_mix[i], dhm), name=f"d_y{i}")
        G["norm_mix"][i] = dnm[0]
    grad_x = dh[ROW0:][None]
    G = {n: (v if n in _OUT_W else jnp.stack(v)) for n, v in G.items()}
    G["meta_tokens"] = dh[META0:ROW0]

    reduced = [_rs_finish(s_first, recv_first, "_first"), _both_halves(t_later, r_later)]

    def reduced_piece(n, l):
        b = 0 if (n, l) in first_pieces else 1
        start = layouts[b][0][n, l]
        return reduced[b][start:start + (W[n].shape[2] if n in _IN_W else W[n].shape[1])]

    grads = {}
    for n in _IN_W:
        grads[n] = jnp.stack([reduced_piece(n, l).T for l in range(W[n].shape[0])])
    for n in _OUT_W:
        grads[n] = jnp.stack([reduced_piece(n, l) for l in range(W[n].shape[0])])
    small_names = [n for n, _ in _SMALL_SHARDED] + list(_REPLICATED)
    small_g = _pack([G[n] for n in small_names] + [loss_tile[0, 0:1]], LANES, 8, F32)
    small_sum = _gather8(small_g, reduce=True, name="allreduce_small").reshape(-1)
    small_shapes = [G[n].shape for n in small_names] + [(1,)]
    small_vals = _unpack(small_sum, small_shapes)
    loss = small_vals[-1][0]
    for n, val in zip(small_names, small_vals[:-1]):
        grads[n] = val
    for n, ax in _SMALL_SHARDED:
        sz = W[n].shape[ax]
        grads[n] = lax.dynamic_slice_in_dim(grads[n], chip * sz, sz, axis=ax)

    delta, new_m, new_v = {}, {}, {}
    for n, key, tr_ in (("w_gate_up", "gu", True), ("w_down", "down", False)):
        grads[n], delta[n], new_m[n], new_v[n] = _adamw_packed(
            W[n], reduced[1], reduced[1], M[n], V[n], row0=layouts[1][0][key, 0], row_off=layouts[1][0][key, 1],
            transposed=tr_, name=f"adamw_{n}")
    for n in _IN_W + _OUT_W:
        delta[n], new_m[n], new_v[n] = _adamw(W[n], grads[n], M[n], V[n], name=f"adamw_{n}")
    tiny = [n for n in _WEIGHTS if n not in ("w_gate_up", "w_down") + _IN_W + _OUT_W]
    packs = [_pack([T[n] for n in tiny], LANES, 8, F32) for T in (W, grads, M, V)]
    outs = _adamw(*packs, name="adamw_small")
    shapes = [W[n].shape for n in tiny]
    for dst, o in zip((delta, new_m, new_v), outs):
        for n, val in zip(tiny, _unpack(o.reshape(-1), shapes)):
            dst[n] = val
    return (loss, grad_x, *[grads[n] for n in _WEIGHTS], *[delta[n] for n in _WEIGHTS],
            *[new_m[n] for n in _WEIGHTS], *[new_v[n] for n in _WEIGHTS])
```
